```python
import math
import jax, jax.numpy as jnp
from jax import lax
import numpy as np

D_MODEL = 1024
BATCH = 16
SEQ = 4096
DEPTH = 1

A_HEADS = 8
A_KV_HEADS = 2
A_HEAD_DIM = 64
A_WIDTH = A_HEADS * A_HEAD_DIM
A_KV_WIDTH = A_KV_HEADS * A_HEAD_DIM
WINDOW = 128
BLOCK = 128
ROPE_THETA = 500000.0
ROPE_DIM = A_HEAD_DIM // 4
B_HEADS = 4
B_KEY_DIM = 64
B_VAL_DIM = 128
B_KEY_WIDTH = B_HEADS * B_KEY_DIM
B_WIDTH = B_HEADS * B_VAL_DIM
B_GATE_RANK = 16
B_GATE_TEMP = 16.0
B_CHUNK = 64
NORM_EPS = 1e-6
NEG_BIG = -1e30

IN_SPLITS = (A_WIDTH, A_KV_WIDTH, A_KV_WIDTH, A_WIDTH,
             B_KEY_WIDTH, B_KEY_WIDTH, B_WIDTH, B_WIDTH, B_GATE_RANK,
             D_MODEL, D_MODEL)
D_IN = 2 * A_WIDTH + 2 * A_KV_WIDTH + 2 * B_KEY_WIDTH + 2 * B_WIDTH + B_GATE_RANK + 2 * D_MODEL

kernel_name = "hybrid_swa_sink_gla_gated_merge"


def rms_norm(x, g):
    xf = x.astype(jnp.float32)
    y = xf * lax.rsqrt(jnp.mean(xf * xf, axis=-1, keepdims=True) + NORM_EPS)
    return (y * g.astype(jnp.float32)).astype(x.dtype)


def partial_rope(t, positions):
    half = ROPE_DIM // 2
    inv_freq = jnp.exp(-math.log(ROPE_THETA) * jnp.arange(half, dtype=jnp.float32) * (2.0 / ROPE_DIM))
    ang = positions.astype(jnp.float32)[..., None] * inv_freq
    cos = jnp.cos(ang)[:, :, None, :]
    sin = jnp.sin(ang)[:, :, None, :]
    tr = t[..., :ROPE_DIM].astype(jnp.float32)
    t1, t2 = tr[..., :half], tr[..., half:]
    rot = jnp.concatenate([t1 * cos - t2 * sin, t2 * cos + t1 * sin], axis=-1)
    return jnp.concatenate([rot.astype(t.dtype), t[..., ROPE_DIM:]], axis=-1)


def sliding_window_gqa_sinks(q, k, v, sinks):
    b, s = q.shape[0], q.shape[1]
    nb = s // BLOCK
    grp = A_HEADS // A_KV_HEADS
    qb = q.reshape(b, nb, BLOCK, A_KV_HEADS, grp, A_HEAD_DIM).astype(jnp.float32)
    kb = k.reshape(b, nb, BLOCK, A_KV_HEADS, A_HEAD_DIM).astype(jnp.float32)
    vb = v.reshape(b, nb, BLOCK, A_KV_HEADS, A_HEAD_DIM).astype(jnp.float32)

    def with_prev(t):
        prev = jnp.pad(t[:, :-1], ((0, 0), (1, 0), (0, 0), (0, 0), (0, 0)))
        return jnp.concatenate([prev, t], axis=2)

    kw, vw = with_prev(kb), with_prev(vb)
    scores = jnp.einsum('bnqhgd,bnkhd->bnhgqk', qb, kw) * (A_HEAD_DIM ** -0.5)
    qi = jnp.arange(BLOCK)[:, None]
    kj = jnp.arange(2 * BLOCK)[None, :] - BLOCK
    band = (kj <= qi) & (qi - kj < WINDOW)
    blk = jnp.arange(nb)[:, None, None]
    valid = band[None] & ((blk > 0) | (kj[None] >= 0))
    scores = jnp.where(valid[None, :, None, None], scores, NEG_BIG)
    sink = sinks.astype(jnp.float32).reshape(A_KV_HEADS, grp)[None, None, :, :, None, None]
    m = jnp.maximum(jnp.max(scores, axis=-1, keepdims=True), sink)
    p = jnp.exp(scores - m)
    denom = jnp.sum(p, axis=-1, keepdims=True) + jnp.exp(sink - m)
    probs = p / denom
    out = jnp.einsum('bnhgqk,bnkhd->bnqhgd', probs, vw)
    return out.reshape(b, s, A_WIDTH).astype(q.dtype)


def gla_chunked(q, k, v, log_a):
    b, s = q.shape[0], q.shape[1]
    nc = s // B_CHUNK

    def chunks(t):
        return t.astype(jnp.float32).reshape(b, nc, B_CHUNK, B_HEADS, t.shape[-1]).transpose(0, 3, 1, 2, 4)

    qc = chunks(q) * (B_KEY_DIM ** -0.5)
    kc, vc, gc = chunks(k), chunks(v), chunks(log_a)
    cum = jnp.cumsum(gc, axis=3)
    last = cum[:, :, :, -1:, :]
    mid = cum[:, :, :, B_CHUNK // 2 - 1:B_CHUNK // 2, :]
    attn = jnp.einsum('bhnid,bhnjd->bhnij', qc * jnp.exp(cum - mid), kc * jnp.exp(mid - cum))
    causal = jnp.tril(jnp.ones((B_CHUNK, B_CHUNK), dtype=bool))
    attn = jnp.where(causal, attn, 0.0)
    o_intra = jnp.einsum('bhnij,bhnjv->bhniv', attn, vc)
    inc = jnp.einsum('bhnjd,bhnjv->nbhdv', kc * jnp.exp(last - cum), vc)
    decay = jnp.exp(last[:, :, :, 0, :]).transpose(2, 0, 1, 3)

    def step(state, xs):
        dec, add = xs
        return dec[..., None] * state + add, state

    init = jnp.zeros((b, B_HEADS, B_KEY_DIM, B_VAL_DIM), jnp.float32)
    _, s_prev = lax.scan(step, init, (decay, inc))
    o_inter = jnp.einsum('bhnid,nbhdv->bhniv', qc * jnp.exp(cum), s_prev)
    o = (o_intra + o_inter).transpose(0, 2, 3, 1, 4)
    return o.reshape(b, s, B_HEADS, B_VAL_DIM)


def hybrid_layer(x, positions, g_in, w_in, w_alpha_up, b_alpha, attn_sinks, g_gla_norm,
                 w_out_a, w_out_b, w_o):
    b, s = x.shape[0], x.shape[1]
    h = rms_norm(x, g_in)
    proj = jnp.einsum('bsd,de->bse', h, w_in)
    split_points = np.cumsum(IN_SPLITS)[:-1].tolist()
    qa, ka, va, za, qb, kb, vb, zb, a_lr, gate_a, gate_b = jnp.split(proj, split_points, axis=-1)

    qa = partial_rope(qa.reshape(b, s, A_HEADS, A_HEAD_DIM), positions)
    ka = partial_rope(ka.reshape(b, s, A_KV_HEADS, A_HEAD_DIM), positions)
    va = va.reshape(b, s, A_KV_HEADS, A_HEAD_DIM)
    oa = sliding_window_gqa_sinks(qa, ka, va, attn_sinks) * jax.nn.silu(za)
    ya = jnp.einsum('bse,ed->bsd', oa, w_out_a)

    log_a = jax.nn.log_sigmoid(
        (jnp.einsum('bsr,re->bse', a_lr, w_alpha_up) + b_alpha).astype(jnp.float32)) / B_GATE_TEMP
    ob = gla_chunked(qb.reshape(b, s, B_HEADS, B_KEY_DIM),
                     kb.reshape(b, s, B_HEADS, B_KEY_DIM),
                     vb.reshape(b, s, B_HEADS, B_VAL_DIM),
                     log_a.reshape(b, s, B_HEADS, B_KEY_DIM))
    ob = ob * lax.rsqrt(jnp.mean(ob * ob, axis=-1, keepdims=True) + NORM_EPS)
    ob = (ob.reshape(b, s, B_WIDTH) * g_gla_norm.astype(jnp.float32)).astype(x.dtype) * jax.nn.silu(zb)
    yb = jnp.einsum('bse,ed->bsd', ob, w_out_b)

    merged = jax.nn.sigmoid(gate_a) * ya + jax.nn.sigmoid(gate_b) * yb
    return x + jnp.einsum('bsd,de->bse', merged, w_o)


def _fwd_setup_inputs(seed: int = 0) -> dict:
    key = jax.random.key(seed)
    ks = jax.random.split(key, 12)
    nrm = jax.random.normal
    f32 = jnp.float32
    x = nrm(ks[0], (BATCH, SEQ, D_MODEL), f32)
    positions = jnp.broadcast_to(jnp.arange(SEQ, dtype=jnp.int32)[None, :], (BATCH, SEQ))
    g_in = 1.0 + 0.02 * nrm(ks[1], (DEPTH, D_MODEL), f32)
    w_in = nrm(ks[2], (DEPTH, D_MODEL, D_IN), f32) * D_MODEL ** -0.5
    w_alpha_up = nrm(ks[3], (DEPTH, B_GATE_RANK, B_KEY_WIDTH), f32) * B_GATE_RANK ** -0.5
    b_alpha = 0.1 * nrm(ks[4], (DEPTH, B_KEY_WIDTH), f32)
    attn_sinks = 0.5 * nrm(ks[5], (DEPTH, A_HEADS), f32)
    g_gla_norm = 1.0 + 0.02 * nrm(ks[6], (DEPTH, B_WIDTH), f32)
    w_out_a = nrm(ks[7], (DEPTH, A_WIDTH, D_MODEL), f32) * A_WIDTH ** -0.5
    w_out_b = nrm(ks[8], (DEPTH, B_WIDTH, D_MODEL), f32) * B_WIDTH ** -0.5
    w_o = nrm(ks[9], (DEPTH, D_MODEL, D_MODEL), f32) * D_MODEL ** -0.5
    g_final = 1.0 + 0.02 * nrm(ks[10], (D_MODEL,), f32)
    return {"x": x, "positions": positions, "g_in": g_in, "w_in": w_in,
            "w_alpha_up": w_alpha_up, "b_alpha": b_alpha, "attn_sinks": attn_sinks,
            "g_gla_norm": g_gla_norm, "w_out_a": w_out_a, "w_out_b": w_out_b,
            "w_o": w_o, "g_final": g_final}


def _fwd_reference(x, positions, g_in, w_in, w_alpha_up, b_alpha, attn_sinks, g_gla_norm,
              w_out_a, w_out_b, w_o, g_final):
    h = x
    for layer in range(DEPTH):
        h = hybrid_layer(h, positions, g_in[layer], w_in[layer], w_alpha_up[layer], b_alpha[layer],
                         attn_sinks[layer], g_gla_norm[layer], w_out_a[layer], w_out_b[layer],
                         w_o[layer])
    return rms_norm(h, g_final)


import jax as _jax
import jax.numpy as _jnp

TWIN_FORMAT = 'train_step'
FWD_PARAMS = ['x', 'positions', 'g_in', 'w_in', 'w_alpha_up', 'b_alpha', 'attn_sinks', 'g_gla_norm', 'w_out_a', 'w_out_b', 'w_o', 'g_final']
TWIN_WEIGHTS = ['g_in', 'w_in', 'w_alpha_up', 'b_alpha', 'attn_sinks', 'g_gla_norm', 'w_out_a', 'w_out_b', 'w_o', 'g_final']
TWIN_DIFF_INPUT = 'x'
TWIN_INPUTS = ['x', 'positions', 'g_in', 'w_in', 'w_alpha_up', 'b_alpha', 'attn_sinks', 'g_gla_norm', 'w_out_a', 'w_out_b', 'w_o', 'g_final', 'loss_target', 'm_g_in', 'm_w_in', 'm_w_alpha_up', 'm_b_alpha', 'm_attn_sinks', 'm_g_gla_norm', 'm_w_out_a', 'm_w_out_b', 'm_w_o', 'm_g_final', 'v_g_in', 'v_w_in', 'v_w_alpha_up', 'v_b_alpha', 'v_attn_sinks', 'v_g_gla_norm', 'v_w_out_a', 'v_w_out_b', 'v_w_o', 'v_g_final']
TWIN_OUTPUTS = ['loss', 'grad_x', 'grad_g_in', 'grad_w_in', 'grad_w_alpha_up', 'grad_b_alpha', 'grad_attn_sinks', 'grad_g_gla_norm', 'grad_w_out_a', 'grad_w_out_b', 'grad_w_o', 'grad_g_final', 'delta_g_in', 'delta_w_in', 'delta_w_alpha_up', 'delta_b_alpha', 'delta_attn_sinks', 'delta_g_gla_norm', 'delta_w_out_a', 'delta_w_out_b', 'delta_w_o', 'delta_g_final', 'new_m_g_in', 'new_m_w_in', 'new_m_w_alpha_up', 'new_m_b_alpha', 'new_m_attn_sinks', 'new_m_g_gla_norm', 'new_m_w_out_a', 'new_m_w_out_b', 'new_m_w_o', 'new_m_g_final', 'new_v_g_in', 'new_v_w_in', 'new_v_w_alpha_up', 'new_v_b_alpha', 'new_v_attn_sinks', 'new_v_g_gla_norm', 'new_v_w_out_a', 'new_v_w_out_b', 'new_v_w_o', 'new_v_g_final']
TWIN_LEAF_KINDS = {'loss': 'loss', 'grad_x': 'grad_x', 'grad_g_in': 'grad_w', 'grad_w_in': 'grad_w', 'grad_w_alpha_up': 'grad_w', 'grad_b_alpha': 'grad_w', 'grad_attn_sinks': 'grad_w', 'grad_g_gla_norm': 'grad_w', 'grad_w_out_a': 'grad_w', 'grad_w_out_b': 'grad_w', 'grad_w_o': 'grad_w', 'grad_g_final': 'grad_w', 'delta_g_in': 'delta_w', 'delta_w_in': 'delta_w', 'delta_w_alpha_up': 'delta_w', 'delta_b_alpha': 'delta_w', 'delta_attn_sinks': 'delta_w', 'delta_g_gla_norm': 'delta_w', 'delta_w_out_a': 'delta_w', 'delta_w_out_b': 'delta_w', 'delta_w_o': 'delta_w', 'delta_g_final': 'delta_w', 'new_m_g_in': 'new_m', 'new_m_w_in': 'new_m', 'new_m_w_alpha_up': 'new_m', 'new_m_b_alpha': 'new_m', 'new_m_attn_sinks': 'new_m', 'new_m_g_gla_norm': 'new_m', 'new_m_w_out_a': 'new_m', 'new_m_w_out_b': 'new_m', 'new_m_w_o': 'new_m', 'new_m_g_final': 'new_m', 'new_v_g_in': 'new_v', 'new_v_w_in': 'new_v', 'new_v_w_alpha_up': 'new_v', 'new_v_b_alpha': 'new_v', 'new_v_attn_sinks': 'new_v', 'new_v_g_gla_norm': 'new_v', 'new_v_w_out_a': 'new_v', 'new_v_w_out_b': 'new_v', 'new_v_w_o': 'new_v', 'new_v_g_final': 'new_v'}


def _forward(args):
    return _fwd_reference(*[args[k] for k in FWD_PARAMS])


def _output_shape():
    out = _jax.eval_shape(lambda: _forward(_fwd_setup_inputs(0)))
    return out.shape, out.dtype

N_MICROBATCH = 1
ADAM_LR = 0.001
ADAM_B1 = 0.9
ADAM_B2 = 0.999
ADAM_EPS = 1e-08
ADAM_WD = 0.01
ADAM_STEP = 10
PER_EXAMPLE_BATCH_AXIS = {'x': 0, 'positions': 0, 'loss_target': 0}
SHARED_INPUTS = []
_WEIGHT_DTYPES = {'g_in': _jnp.float32, 'w_in': _jnp.float32, 'w_alpha_up': _jnp.float32, 'b_alpha': _jnp.float32, 'attn_sinks': _jnp.float32, 'g_gla_norm': _jnp.float32, 'w_out_a': _jnp.float32, 'w_out_b': _jnp.float32, 'w_o': _jnp.float32, 'g_final': _jnp.float32}
MOMENT_SCALE = {'g_in': 1.676031e-01, 'w_in': 7.468664e-02, 'w_alpha_up': 1.936171e-02, 'b_alpha': 6.879613e-02, 'attn_sinks': 2.706674e-02, 'g_gla_norm': 1.289692e-01, 'w_out_a': 1.564341e-02, 'w_out_b': 7.693724e-02, 'w_o': 7.832812e-02, 'g_final': 6.388165e+01}


def _to_microbatches(a, axis):
    t = _jnp.moveaxis(a, axis, 0)
    t = t.reshape((N_MICROBATCH, t.shape[0] // N_MICROBATCH) + t.shape[1:])
    return _jnp.moveaxis(t, 1, axis + 1)


def setup_inputs(seed: int = 0) -> dict:
    inp = _fwd_setup_inputs(seed)
    key = _jax.random.fold_in(_jax.random.key(seed), 7919)
    shape, _ = _output_shape()
    out = dict(inp)
    out["loss_target"] = _jax.random.normal(_jax.random.fold_in(key, 0), shape, _jnp.float32)
    for i, name in enumerate(TWIN_WEIGHTS):
        w = inp[name].astype(_jnp.float32)
        if MOMENT_SCALE is None:
            s = _jnp.sqrt(_jnp.mean(_jnp.square(w)) + 1e-30)
        else:
            s = MOMENT_SCALE[name]
        km, kv = _jax.random.split(_jax.random.fold_in(key, i + 1))
        out[name] = w
        out["m_" + name] = s * _jax.random.normal(km, w.shape, _jnp.float32)
        out["v_" + name] = (s * s) * _jax.random.uniform(kv, w.shape, _jnp.float32, 0.5, 1.5)
    if N_MICROBATCH > 1:
        for name, axis in PER_EXAMPLE_BATCH_AXIS.items():
            out[name] = _to_microbatches(out[name], axis)
    return {'x': out['x'], 'positions': out['positions'], 'g_in': out['g_in'], 'w_in': out['w_in'], 'w_alpha_up': out['w_alpha_up'], 'b_alpha': out['b_alpha'], 'attn_sinks': out['attn_sinks'], 'g_gla_norm': out['g_gla_norm'], 'w_out_a': out['w_out_a'], 'w_out_b': out['w_out_b'], 'w_o': out['w_o'], 'g_final': out['g_final'], 'loss_target': out['loss_target'], 'm_g_in': out['m_g_in'], 'm_w_in': out['m_w_in'], 'm_w_alpha_up': out['m_w_alpha_up'], 'm_b_alpha': out['m_b_alpha'], 'm_attn_sinks': out['m_attn_sinks'], 'm_g_gla_norm': out['m_g_gla_norm'], 'm_w_out_a': out['m_w_out_a'], 'm_w_out_b': out['m_w_out_b'], 'm_w_o': out['m_w_o'], 'm_g_final': out['m_g_final'], 'v_g_in': out['v_g_in'], 'v_w_in': out['v_w_in'], 'v_w_alpha_up': out['v_w_alpha_up'], 'v_b_alpha': out['v_b_alpha'], 'v_attn_sinks': out['v_attn_sinks'], 'v_g_gla_norm': out['v_g_gla_norm'], 'v_w_out_a': out['v_w_out_a'], 'v_w_out_b': out['v_w_out_b'], 'v_w_o': out['v_w_o'], 'v_g_final': out['v_g_final']}


def _loss(weights, diff, rest, loss_target):
    with _jax.named_scope("forward"):
        args = {**rest, TWIN_DIFF_INPUT: diff, **{k: w.astype(_WEIGHT_DTYPES[k]) for k, w in weights.items()}}
        y = _forward(args)
    with _jax.named_scope("loss_head"):
        err = _jnp.square(y.astype(_jnp.float32) - loss_target)
        return 0.5 * _jnp.sum(_jnp.mean(err, axis=-1)) if err.ndim else 0.5 * err


def _adamw(w, g, m, v):
    m = ADAM_B1 * m + (1.0 - ADAM_B1) * g
    v = ADAM_B2 * v + (1.0 - ADAM_B2) * _jnp.square(g)
    m_hat = m / (1.0 - ADAM_B1 ** ADAM_STEP)
    v_hat = v / (1.0 - ADAM_B2 ** ADAM_STEP)
    delta = -ADAM_LR * (m_hat / (_jnp.sqrt(v_hat) + ADAM_EPS) + ADAM_WD * w)
    return delta, m, v


def reference(x, positions, g_in, w_in, w_alpha_up, b_alpha, attn_sinks, g_gla_norm, w_out_a, w_out_b, w_o, g_final, loss_target, m_g_in, m_w_in, m_w_alpha_up, m_b_alpha, m_attn_sinks, m_g_gla_norm, m_w_out_a, m_w_out_b, m_w_o, m_g_final, v_g_in, v_w_in, v_w_alpha_up, v_b_alpha, v_attn_sinks, v_g_gla_norm, v_w_out_a, v_w_out_b, v_w_o, v_g_final):
    given = dict(x=x, positions=positions, g_in=g_in, w_in=w_in, w_alpha_up=w_alpha_up, b_alpha=b_alpha, attn_sinks=attn_sinks, g_gla_norm=g_gla_norm, w_out_a=w_out_a, w_out_b=w_out_b, w_o=w_o, g_final=g_final, loss_target=loss_target, m_g_in=m_g_in, m_w_in=m_w_in, m_w_alpha_up=m_w_alpha_up, m_b_alpha=m_b_alpha, m_attn_sinks=m_attn_sinks, m_g_gla_norm=m_g_gla_norm, m_w_out_a=m_w_out_a, m_w_out_b=m_w_out_b, m_w_o=m_w_o, m_g_final=m_g_final, v_g_in=v_g_in, v_w_in=v_w_in, v_w_alpha_up=v_w_alpha_up, v_b_alpha=v_b_alpha, v_attn_sinks=v_attn_sinks, v_g_gla_norm=v_g_gla_norm, v_w_out_a=v_w_out_a, v_w_out_b=v_w_out_b, v_w_o=v_w_o, v_g_final=v_g_final)
    weights = {n: given[n] for n in TWIN_WEIGHTS}
    shared = {n: given[n] for n in SHARED_INPUTS}
    per_example = {n: given[n] for n in ['x', 'positions']}
    grad_fn = _jax.value_and_grad(_loss, argnums=(0, 1))

    def one_microbatch(ex, loss_target):
        ex = dict(ex)
        diff = ex.pop(TWIN_DIFF_INPUT)
        return grad_fn(weights, diff, {**shared, **ex}, loss_target)

    if N_MICROBATCH == 1:
        loss, (grad_w, grad_x) = one_microbatch(per_example, given["loss_target"])
    else:
        def body(carry, xs):
            loss_sum, grad_sum = carry
            l_k, (gw_k, gx_k) = one_microbatch(xs[0], xs[1])
            with _jax.named_scope("update"):
                return (loss_sum + l_k, _jax.tree.map(_jnp.add, grad_sum, gw_k)), gx_k

        init = (_jnp.zeros((), _jnp.float32), _jax.tree.map(_jnp.zeros_like, weights))
        (loss, grad_w), grad_x = _jax.lax.scan(body, init, (per_example, given["loss_target"]))
    with _jax.named_scope("update"):
        delta_w, new_m, new_v = {}, {}, {}
        for n in TWIN_WEIGHTS:
            delta_w[n], new_m[n], new_v[n] = _adamw(weights[n], grad_w[n], given["m_" + n], given["v_" + n])
    return (loss, grad_x, *[grad_w[n] for n in TWIN_WEIGHTS], *[delta_w[n] for n in TWIN_WEIGHTS],
            *[new_m[n] for n in TWIN_WEIGHTS], *[new_v[n] for n in TWIN_WEIGHTS])
```

```python
import functools
import math

import numpy as np
import jax
import jax.numpy as jnp
from jax import lax
from jax.experimental import pallas as pl
from jax.experimental.pallas import tpu as pltpu

F32 = jnp.float32
BF16 = jnp.bfloat16
MESH = pl.DeviceIdType.MESH

D_MODEL = 1024
A_HEADS, A_KV_HEADS, A_HEAD_DIM = 8, 2, 64
A_WIDTH, A_KV_WIDTH = 512, 128
WINDOW = 128
ROPE_THETA = 500000.0
ROPE_DIM = 16
B_HEADS, B_KEY_DIM, B_VAL_DIM = 4, 64, 128
B_KEY_WIDTH, B_WIDTH = 256, 512
B_GATE_RANK = 16
B_GATE_TEMP = 16.0
B_CHUNK = 64
NORM_EPS = 1e-6
NEG_BIG = -1e30
D_IN = 4880
N_DEV = 8
ADAM_LR, ADAM_B1, ADAM_B2, ADAM_EPS, ADAM_WD, ADAM_STEP = 0.001, 0.9, 0.999, 1e-08, 0.01, 10

LANES = 128
V7X_VMEM_LIMIT = 56 * 1024 * 1024

RANK_PAD = LANES
SEG = {}
_off = 0
for _name, _w in (("qa", 512), ("ka", 128), ("va", 128), ("za", 512), ("qb", 256), ("kb", 256),
                  ("vb", 512), ("zb", 512), ("alr", RANK_PAD), ("ga", 1024), ("gb", 1024)):
    SEG[_name] = (_off, _off + _w)
    _off += _w
D_IN_PAD = _off
ALR_SRC = 2816

ROWS_W_IN = D_MODEL * (D_IN // N_DEV) // LANES
ROWS_W_OUT = A_WIDTH * (D_MODEL // N_DEV) // LANES
ROWS_W_O = (D_MODEL // N_DEV) * D_MODEL // LANES
ROWS_BIG = ROWS_W_IN + 2 * ROWS_W_OUT + ROWS_W_O
ROWS_ALPHA = 16
ROWS_GATHER = ROWS_BIG + ROWS_ALPHA
ROWS_SMALL = 56


def _dot(a, b):
    return jnp.dot(a, b, preferred_element_type=F32)


def _dot_nt(a, b):
    return lax.dot_general(a, b, (((1,), (1,)), ((), ())), preferred_element_type=F32)


def _dot_tn(a, b):
    return lax.dot_general(a, b, (((0,), (0,)), ((), ())), preferred_element_type=F32)


def _sigmoid(z):
    return 1.0 / (1.0 + jnp.exp(-z))


def _params(*sem):
    return pltpu.CompilerParams(dimension_semantics=sem, vmem_limit_bytes=V7X_VMEM_LIMIT)


def _const_spec(shape):
    nd = len(shape)
    return pl.BlockSpec(shape, lambda *_: (0,) * nd, pipeline_mode=pl.Buffered(1))


def _lane_iota(shape):
    return lax.broadcasted_iota(jnp.int32, shape, 1)


def _row_iota(shape):
    return lax.broadcasted_iota(jnp.int32, shape, 0)


def _split3(v):
    hi = v.astype(BF16)
    r1 = v - hi.astype(F32)
    mid = r1.astype(BF16)
    lo = (r1 - mid.astype(F32)).astype(BF16)
    return hi, mid, lo


def _rope_lane_constants():
    half = ROPE_DIM // 2
    inv_freq = np.exp(-math.log(ROPE_THETA) * np.arange(half, dtype=np.float32) * np.float32(2.0 / ROPE_DIM)).astype(np.float32)
    lane = np.arange(LANES)
    j = lane % A_HEAD_DIM
    invf = np.where(j < ROPE_DIM, inv_freq[j % half], 0.0).astype(np.float32)
    sign = np.where(j < half, -1.0, np.where(j < ROPE_DIM, 1.0, 0.0)).astype(np.float32)
    return jnp.asarray(invf)[None, :], jnp.asarray(sign)[None, :]


def _rope_tables(pos_col):
    T = pos_col.shape[0]
    tm = math.gcd(T, 1024)
    invf, sign = _rope_lane_constants()

    def body(pos_ref, invf_ref, sign_ref, cos_ref, sin_ref):
        ang = pos_ref[...].astype(F32) * invf_ref[...]
        cos_ref[...] = jnp.cos(ang)
        sin_ref[...] = jnp.sin(ang) * sign_ref[...]

    return pl.pallas_call(
        body, name="rope_tables", grid=(T // tm,),
        in_specs=[pl.BlockSpec((tm, 1), lambda i: (i, 0)), _const_spec((1, LANES)), _const_spec((1, LANES))],
        out_specs=[pl.BlockSpec((tm, LANES), lambda i: (i, 0))] * 2,
        out_shape=[jax.ShapeDtypeStruct((T, LANES), F32)] * 2,
        compiler_params=_params("parallel"),
    )(pos_col, invf, sign)


def _rope_slab(t, cos, sin_signed):
    first = (_lane_iota(t.shape) % A_HEAD_DIM) < (ROPE_DIM // 2)
    partner = jnp.where(first, pltpu.roll(t, LANES - ROPE_DIM // 2, 1), pltpu.roll(t, ROPE_DIM // 2, 1))
    return t * cos + partner * sin_signed


def _chunk_tri(tm, upper):
    r, c = _row_iota((tm, tm)), _lane_iota((tm, tm))
    return r, c


def _in_proj(x2, cosf, sinf, g_in, w_pad, wa_pad, b_alpha):
    T = x2.shape[0]
    tm = 256

    def body(x_ref, cos_ref, sin_ref, g_ref, w_ref, wa_ref, ba_ref,
             h_ref, qkv_ref, za_ref, q_ref, k_ref, vb_ref, zb_ref, alr_ref, u_ref, cum_ref, ga_ref, gb_ref):
        x = x_ref[...]
        r = lax.rsqrt(jnp.mean(x * x, axis=-1, keepdims=True) + NORM_EPS)
        h = (x * r * g_ref[...]).astype(BF16)
        h_ref[...] = h

        def seg(name):
            a, b = SEG[name]
            return _dot(h, w_ref[:, a:b])

        cos, sin = cos_ref[...], sin_ref[...]
        qa = seg("qa")
        for s in range(A_WIDTH // LANES):
            qkv_ref[:, s * LANES:(s + 1) * LANES] = _rope_slab(qa[:, s * LANES:(s + 1) * LANES], cos, sin).astype(BF16)
        qkv_ref[:, 512:640] = _rope_slab(seg("ka"), cos, sin).astype(BF16)
        qkv_ref[:, 640:768] = seg("va").astype(BF16)
        za_ref[...] = seg("za")
        q_ref[...] = seg("qb")
        k_ref[...] = seg("kb")
        vb_ref[...] = seg("vb").astype(BF16)
        zb_ref[...] = seg("zb")
        ga_ref[...] = seg("ga")
        gb_ref[...] = seg("gb")
        alr = seg("alr").astype(BF16)
        alr_ref[...] = alr
        u = _dot(alr, wa_ref[...]) + ba_ref[...]
        u_ref[...] = u
        log_a = (jnp.minimum(u, 0.0) - jnp.log(1.0 + jnp.exp(-jnp.abs(u)))) * (1.0 / B_GATE_TEMP)
        row, col = _row_iota((tm, tm)), _lane_iota((tm, tm))
        tri = ((row // B_CHUNK == col // B_CHUNK) & (col <= row)).astype(BF16)
        hi, mid, lo = _split3(log_a)
        cum_ref[...] = _dot(tri, hi) + _dot(tri, mid) + _dot(tri, lo)

    def rows(w):
        return pl.BlockSpec((tm, w), lambda i: (i, 0))

    outs = [("h", D_MODEL, BF16), ("qkv", 768, BF16), ("za", 512, F32), ("q", 256, F32), ("k", 256, F32),
            ("vb", 512, BF16), ("zb", 512, F32), ("alr", RANK_PAD, BF16), ("u", 256, F32), ("cum", 256, F32),
            ("ga", 1024, F32), ("gb", 1024, F32)]
    res = pl.pallas_call(
        body, name="in_proj", grid=(T // tm,),
        in_specs=[rows(D_MODEL), rows(LANES), rows(LANES), _const_spec((1, D_MODEL)),
                  _const_spec((D_MODEL, D_IN_PAD)), _const_spec((RANK_PAD, B_KEY_WIDTH)), _const_spec((1, B_KEY_WIDTH))],
        out_specs=[rows(w) for _, w, _ in outs],
        out_shape=[jax.ShapeDtypeStruct((T, w), dt) for _, w, dt in outs],
        compiler_params=_params("parallel"),
    )(x2, cosf, sinf, g_in, w_pad, wa_pad, b_alpha)
    return dict(zip([n for n, _, _ in outs], res))


def _dup_kv_head(t, g):
    tf = t.astype(F32)
    keep = (_lane_iota(tf.shape) < A_HEAD_DIM) == (g == 0)
    return jnp.where(keep, tf, pltpu.roll(tf, A_HEAD_DIM, 1)).astype(BF16)


def _stack_heads(t):
    lo = _lane_iota(t.shape) < A_HEAD_DIM
    zero = jnp.zeros_like(t)
    return jnp.concatenate([jnp.where(lo, t, zero), jnp.where(lo, zero, t)], axis=0)


def _band_mask(n):
    qi = _row_iota((2 * WINDOW, 2 * WINDOW)) % WINDOW
    kj = _lane_iota((2 * WINDOW, 2 * WINDOW)) - WINDOW
    return (kj <= qi) & (qi - kj < WINDOW) & ((n > 0) | (kj >= 0))


def _attn_fwd(qkv, sinks, B, S):
    T = B * S
    nb = S // WINDOW
    scale = A_HEAD_DIM ** -0.5

    def body(sink_ref, q_ref, kc_ref, vc_ref, kp_ref, vp_ref, o_ref, lse_ref):
        n = pl.program_id(1)
        valid = _band_mask(n)
        k = jnp.concatenate([kp_ref[...], kc_ref[...]], axis=0)
        v = jnp.concatenate([vp_ref[...], vc_ref[...]], axis=0)
        top = _row_iota((2 * WINDOW, 1)) < WINDOW
        lo = _lane_iota((WINDOW, LANES)) < A_HEAD_DIM
        lane = _lane_iota((WINDOW, LANES))
        lse_tile = jnp.zeros((WINDOW, LANES), F32)
        for g in range(A_KV_HEADS):
            kd, vd = _dup_kv_head(k, g), _dup_kv_head(v, g)
            for p in (2 * g, 2 * g + 1):
                qs = _stack_heads(q_ref[:, p * LANES:(p + 1) * LANES])
                s = jnp.where(valid, _dot_nt(qs, kd) * scale, NEG_BIG)
                sink = jnp.where(top, sink_ref[2 * p], sink_ref[2 * p + 1])
                m = jnp.maximum(jnp.max(s, axis=-1, keepdims=True), sink)
                e = jnp.exp(s - m)
                den = jnp.sum(e, axis=-1, keepdims=True) + jnp.exp(sink - m)
                o = _dot((e * (1.0 / den)).astype(BF16), vd)
                o_ref[:, p * LANES:(p + 1) * LANES] = jnp.where(lo, o[:WINDOW], o[WINDOW:])
                lse = m + jnp.log(den)
                lse_tile = jnp.where(lane == 2 * p, lse[:WINDOW], lse_tile)
                lse_tile = jnp.where(lane == 2 * p + 1, lse[WINDOW:], lse_tile)
        lse_ref[...] = lse_tile

    def cur(col, w):
        return pl.BlockSpec((WINDOW, w), lambda b, n: (b * nb + n, col))

    def prev(col):
        return pl.BlockSpec((WINDOW, LANES), lambda b, n: (b * nb + jnp.maximum(n - 1, 0), col))

    return pl.pallas_call(
        body, name="attn_fwd", grid=(B, nb),
        in_specs=[pl.BlockSpec(memory_space=pltpu.SMEM), cur(0, A_WIDTH), cur(4, LANES), cur(5, LANES), prev(4), prev(5)],
        out_specs=[cur(0, A_WIDTH), cur(0, LANES)],
        out_shape=[jax.ShapeDtypeStruct((T, A_WIDTH), F32), jax.ShapeDtypeStruct((T, LANES), F32)],
        compiler_params=_params("parallel", "parallel"),
    )(sinks, qkv, qkv, qkv, qkv, qkv)


def _attn_bwd(qkv, do, out, lse, sinks, B, S):
    T = B * S
    nb = S // WINDOW
    scale = A_HEAD_DIM ** -0.5

    def body(sink_ref, q_ref, kc_ref, vc_ref, kp_ref, vp_ref, do_ref, out_ref, lse_ref,
             dq_ref, dkv_ref, dsink_ref, carry_ref):
        b, n = pl.program_id(0), pl.program_id(1)
        active = n < nb
        nq = jnp.minimum(n, nb - 1)
        valid = _band_mask(nq)

        @pl.when((b == 0) & (n == 0))
        def _():
            dsink_ref[...] = jnp.zeros_like(dsink_ref)

        k = jnp.concatenate([kp_ref[...], kc_ref[...]], axis=0)
        v = jnp.concatenate([vp_ref[...], vc_ref[...]], axis=0)
        top = _row_iota((2 * WINDOW, 1)) < WINDOW
        lane = _lane_iota((WINDOW, LANES))
        lo = lane < A_HEAD_DIM
        lane2 = _lane_iota((2 * WINDOW, LANES))
        lse_tile = lse_ref[...]
        dk_tot = jnp.zeros((2 * WINDOW, LANES), F32)
        dv_tot = jnp.zeros((2 * WINDOW, LANES), F32)
        dsink_row = jnp.zeros((1, LANES), F32)
        for g in range(A_KV_HEADS):
            kd, vd = _dup_kv_head(k, g), _dup_kv_head(v, g)
            dk_acc = jnp.zeros((2 * WINDOW, LANES), F32)
            dv_acc = jnp.zeros((2 * WINDOW, LANES), F32)
            for p in (2 * g, 2 * g + 1):
                sl = slice(p * LANES, (p + 1) * LANES)
                qs = _stack_heads(q_ref[:, sl])
                dos = _stack_heads(do_ref[:, sl])
                s = jnp.where(valid, _dot_nt(qs, kd) * scale, NEG_BIG)
                lse0 = jnp.sum(jnp.where(lane == 2 * p, lse_tile, 0.0), axis=-1, keepdims=True)
                lse1 = jnp.sum(jnp.where(lane == 2 * p + 1, lse_tile, 0.0), axis=-1, keepdims=True)
                lse_col = jnp.concatenate([lse0, lse1], axis=0)
                prob = jnp.exp(s - lse_col)
                prod = do_ref[:, sl].astype(F32) * out_ref[:, sl]
                d0 = jnp.sum(jnp.where(lo, prod, 0.0), axis=-1, keepdims=True)
                d1 = jnp.sum(jnp.where(lo, 0.0, prod), axis=-1, keepdims=True)
                delta = jnp.concatenate([d0, d1], axis=0)
                dp = _dot_nt(dos, vd)
                ds = (prob * (dp - delta) * scale).astype(BF16)
                dq = _dot(ds, kd)
                dq_ref[:, sl] = jnp.where(lo, dq[:WINDOW], dq[WINDOW:]).astype(BF16)
                dk_acc += _dot_tn(ds, qs)
                dv_acc += _dot_tn(prob.astype(BF16), dos)
                sink = jnp.where(top, sink_ref[2 * p], sink_ref[2 * p + 1])
                w = -jnp.exp(sink - lse_col) * delta
                w0 = jnp.sum(w[:WINDOW], axis=0, keepdims=True)
                w1 = jnp.sum(w[WINDOW:], axis=0, keepdims=True)
                lane1 = _lane_iota((1, LANES))
                dsink_row += jnp.where(lane1 == 2 * p, w0, 0.0) + jnp.where(lane1 == 2 * p + 1, w1, 0.0)
            mine = (lane2 < A_HEAD_DIM) == (g == 0)
            dk_tot = jnp.where(mine, dk_acc + pltpu.roll(dk_acc, A_HEAD_DIM, 1), dk_tot)
            dv_tot = jnp.where(mine, dv_acc + pltpu.roll(dv_acc, A_HEAD_DIM, 1), dv_tot)
        gate = jnp.where(active, 1.0, 0.0)
        dsink_ref[0:1, :] += dsink_row * gate
        dkv_ref[:, 0:LANES] = (carry_ref[:, 0:LANES] + dk_tot[:WINDOW] * gate).astype(BF16)
        dkv_ref[:, LANES:] = (carry_ref[:, LANES:] + dv_tot[:WINDOW] * gate).astype(BF16)
        carry_ref[:, 0:LANES] = dk_tot[WINDOW:]
        carry_ref[:, LANES:] = dv_tot[WINDOW:]

    def cur(col, w):
        return pl.BlockSpec((WINDOW, w), lambda b, n: (b * nb + jnp.minimum(n, nb - 1), col))

    def prev(col):
        return pl.BlockSpec((WINDOW, LANES), lambda b, n: (b * nb + jnp.maximum(jnp.minimum(n, nb - 1) - 1, 0), col))

    lag = pl.BlockSpec((WINDOW, 2 * LANES), lambda b, n: (b * nb + jnp.maximum(n - 1, 0), 0))
    return pl.pallas_call(
        body, name="attn_bwd", grid=(B, nb + 1),
        in_specs=[pl.BlockSpec(memory_space=pltpu.SMEM), cur(0, A_WIDTH), cur(4, LANES), cur(5, LANES), prev(4), prev(5),
                  cur(0, A_WIDTH), cur(0, A_WIDTH), cur(0, LANES)],
        out_specs=[cur(0, A_WIDTH), lag, pl.BlockSpec((8, LANES), lambda b, n: (0, 0))],
        out_shape=[jax.ShapeDtypeStruct((T, A_WIDTH), BF16), jax.ShapeDtypeStruct((T, 2 * LANES), BF16),
                   jax.ShapeDtypeStruct((8, LANES), F32)],
        scratch_shapes=[pltpu.VMEM((WINDOW, 2 * LANES), F32)],
        compiler_params=_params("arbitrary", "arbitrary"),
    )(sinks, qkv, qkv, qkv, qkv, qkv, do, out, lse)


GLA_TILE = 256
CHUNKS_PER_TILE = GLA_TILE // B_CHUNK


def _gla_factors(q_ref, k_ref, cum_ref):
    scale = B_KEY_DIM ** -0.5
    cum = cum_ref[...]
    shape = (B_CHUNK, B_KEY_WIDTH)
    last = jnp.concatenate([jnp.broadcast_to(cum_ref[pl.ds(c * B_CHUNK + B_CHUNK - 1, 1), :], shape)
                            for c in range(CHUNKS_PER_TILE)], axis=0)
    mid = jnp.concatenate([jnp.broadcast_to(cum_ref[pl.ds(c * B_CHUNK + B_CHUNK // 2 - 1, 1), :], shape)
                           for c in range(CHUNKS_PER_TILE)], axis=0)
    e_qm, e_km, e_qe, e_kd = jnp.exp(cum - mid), jnp.exp(mid - cum), jnp.exp(cum), jnp.exp(last - cum)
    qs = q_ref[...] * scale
    k = k_ref[...]
    return qs, k, (e_qm, e_km, e_qe, e_kd)


def _head_mask(shape, h):
    return (_lane_iota(shape) // B_KEY_DIM) == h


def _stack_masked(t):
    return jnp.concatenate([jnp.where(_head_mask(t.shape, h), t, 0.0) for h in range(B_HEADS)], axis=0).astype(BF16)


def _select_heads(t):
    shape = (B_CHUNK, B_KEY_WIDTH)
    out = jnp.zeros(shape, F32)
    for h in range(B_HEADS):
        out = jnp.where(_head_mask(shape, h), t[h * B_CHUNK:(h + 1) * B_CHUNK], out)
    return out


def _select_state(t):
    shape = (B_VAL_DIM, B_KEY_WIDTH)
    out = jnp.zeros(shape, F32)
    for h in range(B_HEADS):
        out = jnp.where(_head_mask(shape, h), t[h * B_VAL_DIM:(h + 1) * B_VAL_DIM], out)
    return out


def _rows_by_head(t):
    return jnp.concatenate([t[:, h * B_VAL_DIM:(h + 1) * B_VAL_DIM] for h in range(B_HEADS)], axis=0)


def _intra_mask():
    i, j = _row_iota((GLA_TILE, GLA_TILE)), _lane_iota((GLA_TILE, GLA_TILE))
    return (i // B_CHUNK == j // B_CHUNK) & (j <= i)


def _pair_stack(t, p):
    slab = t[:, p * LANES:(p + 1) * LANES]
    lo = _lane_iota(slab.shape) < B_KEY_DIM
    return jnp.concatenate([jnp.where(lo, slab, 0.0), jnp.where(lo, 0.0, slab)], axis=0).astype(BF16)


def _gla_fwd(q, k, cum, vb, B, S):
    T = B * S
    nt = S // GLA_TILE

    def body(q_ref, k_ref, cum_ref, v_ref, o_ref, st_all_ref, st_ref):
        @pl.when(pl.program_id(1) == 0)
        def _():
            st_ref[...] = jnp.zeros_like(st_ref)

        qs, kk, (e_qm, e_km, e_qe, e_kd) = _gla_factors(q_ref, k_ref, cum_ref)
        qm, km, qe, kd = qs * e_qm, kk * e_km, qs * e_qe, (kk * e_kd).astype(BF16)
        mask = _intra_mask()
        intra = []
        for p in range(B_HEADS // 2):
            a = _dot_nt(_pair_stack(qm, p), km[:, p * LANES:(p + 1) * LANES].astype(BF16))
            for hh in range(2):
                h = 2 * p + hh
                att = jnp.where(mask, a[hh * GLA_TILE:(hh + 1) * GLA_TILE], 0.0).astype(BF16)
                intra.append(_dot(att, v_ref[:, h * B_VAL_DIM:(h + 1) * B_VAL_DIM]))
        inter = []
        for c in range(CHUNKS_PER_TILE):
            rows = slice(c * B_CHUNK, (c + 1) * B_CHUNK)
            st = st_ref[...]
            st_all_ref[c] = st
            inter.append(_dot_nt(_stack_masked(qe[rows]), st.astype(BF16)))
            inc = _select_state(_dot_tn(v_ref[rows, :], kd[rows]))
            decay = jnp.exp(cum_ref[pl.ds(c * B_CHUNK + B_CHUNK - 1, 1), :])
            st_ref[...] = st * decay + inc
        for h in range(B_HEADS):
            oi = jnp.concatenate([inter[c][h * B_CHUNK:(h + 1) * B_CHUNK] for c in range(CHUNKS_PER_TILE)], axis=0)
            o_ref[:, h * B_VAL_DIM:(h + 1) * B_VAL_DIM] = intra[h] + oi

    def rows(w):
        return pl.BlockSpec((GLA_TILE, w), lambda b, t: (b * nt + t, 0))

    return pl.pallas_call(
        body, name="gla_fwd", grid=(B, nt),
        in_specs=[rows(B_KEY_WIDTH), rows(B_KEY_WIDTH), rows(B_KEY_WIDTH), rows(B_WIDTH)],
        out_specs=[rows(B_WIDTH),
                   pl.BlockSpec((CHUNKS_PER_TILE, B_VAL_DIM, B_KEY_WIDTH), lambda b, t: (b * nt + t, 0, 0))],
        out_shape=[jax.ShapeDtypeStruct((T, B_WIDTH), F32),
                   jax.ShapeDtypeStruct((T // B_CHUNK, B_VAL_DIM, B_KEY_WIDTH), F32)],
        scratch_shapes=[pltpu.VMEM((B_VAL_DIM, B_KEY_WIDTH), F32)],
        compiler_params=_params("arbitrary", "arbitrary"),
    )(q, k, cum, vb)


def _gla_bwd(q, k, cum, vb, do, st_all, B, S):
    T = B * S
    nt = S // GLA_TILE
    scale = B_KEY_DIM ** -0.5

    def body(q_ref, k_ref, cum_ref, v_ref, do_ref, st_all_ref, dq_ref, dk_ref, dv_ref, dla_ref, dst_ref):
        @pl.when(pl.program_id(1) == 0)
        def _():
            dst_ref[...] = jnp.zeros_like(dst_ref)

        qs, kk, (e_qm, e_km, e_qe, e_kd) = _gla_factors(q_ref, k_ref, cum_ref)
        qm, km, qe, kd = qs * e_qm, kk * e_km, qs * e_qe, kk * e_kd
        mask = _intra_mask()
        dqm_slabs, dkm_slabs, dv_intra = [], [], []
        for p in range(B_HEADS // 2):
            qm_st = _pair_stack(qm, p)
            km_p = km[:, p * LANES:(p + 1) * LANES].astype(BF16)
            a = _dot_nt(qm_st, km_p)
            da_blocks, dqm_h = [], []
            for hh in range(2):
                h = 2 * p + hh
                vs = slice(h * B_VAL_DIM, (h + 1) * B_VAL_DIM)
                att = jnp.where(mask, a[hh * GLA_TILE:(hh + 1) * GLA_TILE], 0.0).astype(BF16)
                dv_intra.append(_dot_tn(att, do_ref[:, vs]))
                da = jnp.where(mask, _dot_nt(do_ref[:, vs], v_ref[:, vs]), 0.0).astype(BF16)
                da_blocks.append(da)
                dqm_h.append(_dot(da, km_p))
            lo = _lane_iota((GLA_TILE, LANES)) < B_KEY_DIM
            dqm_slabs.append(jnp.where(lo, dqm_h[0], dqm_h[1]))
            dkm_slabs.append(_dot_tn(jnp.concatenate(da_blocks, axis=0), qm_st))
        dqm = jnp.concatenate(dqm_slabs, axis=1)
        dkm = jnp.concatenate(dkm_slabs, axis=1)

        dqe_c, dkd_c, dv_inter, tail_c = ([None] * CHUNKS_PER_TILE for _ in range(4))
        for c in reversed(range(CHUNKS_PER_TILE)):
            rows = slice(c * B_CHUNK, (c + 1) * B_CHUNK)
            dst = dst_ref[...]
            dst_b = dst.astype(BF16)
            dv_inter[c] = _dot_nt(_stack_masked(kd[rows]), dst_b)
            dkd_c[c] = _select_heads(_dot(_rows_by_head(v_ref[rows, :]), dst_b))
            do_c = do_ref[rows, :]
            dqe_c[c] = _select_heads(_dot(_rows_by_head(do_c), st_all_ref[c].astype(BF16)))
            contrib = _select_state(_dot_tn(do_c, qe[rows].astype(BF16)))
            decay = jnp.exp(cum_ref[pl.ds(c * B_CHUNK + B_CHUNK - 1, 1), :])
            tail = (jnp.sum(kk[rows] * dkd_c[c] * e_kd[rows], axis=0, keepdims=True)
                    + decay * jnp.sum(st_all_ref[c] * dst, axis=0, keepdims=True))
            tail_c[c] = jnp.broadcast_to(tail, (B_CHUNK, B_KEY_WIDTH))
            dst_ref[...] = dst * decay + contrib
        dqe = jnp.concatenate(dqe_c, axis=0)
        dkd = jnp.concatenate(dkd_c, axis=0)
        dqs = dqm * e_qm + dqe * e_qe
        dk = dkm * e_km + dkd * e_kd
        dq_ref[...] = (dqs * scale).astype(BF16)
        dk_ref[...] = dk.astype(BF16)
        for h in range(B_HEADS):
            dvi = jnp.concatenate([dv_inter[c][h * B_CHUNK:(h + 1) * B_CHUNK] for c in range(CHUNKS_PER_TILE)], axis=0)
            dv_ref[:, h * B_VAL_DIM:(h + 1) * B_VAL_DIM] = (dv_intra[h] + dvi).astype(BF16)
        dd = qs * dqs - kk * dk
        i, j = _row_iota((GLA_TILE, GLA_TILE)), _lane_iota((GLA_TILE, GLA_TILE))
        upper = ((i // B_CHUNK == j // B_CHUNK) & (j >= i)).astype(BF16)
        hi, mid, lo3 = _split3(dd)
        dla_ref[...] = _dot(upper, hi) + _dot(upper, mid) + _dot(upper, lo3) + jnp.concatenate(tail_c, axis=0)

    def rows(w):
        return pl.BlockSpec((GLA_TILE, w), lambda b, t: (b * nt + nt - 1 - t, 0))

    return pl.pallas_call(
        body, name="gla_bwd", grid=(B, nt),
        in_specs=[rows(B_KEY_WIDTH), rows(B_KEY_WIDTH), rows(B_KEY_WIDTH), rows(B_WIDTH), rows(B_WIDTH),
                  pl.BlockSpec((CHUNKS_PER_TILE, B_VAL_DIM, B_KEY_WIDTH), lambda b, t: (b * nt + nt - 1 - t, 0, 0))],
        out_specs=[rows(B_KEY_WIDTH), rows(B_KEY_WIDTH), rows(B_WIDTH), rows(B_KEY_WIDTH)],
        out_shape=[jax.ShapeDtypeStruct((T, B_KEY_WIDTH), BF16), jax.ShapeDtypeStruct((T, B_KEY_WIDTH), BF16),
                   jax.ShapeDtypeStruct((T, B_WIDTH), BF16), jax.ShapeDtypeStruct((T, B_KEY_WIDTH), F32)],
        scratch_shapes=[pltpu.VMEM((B_VAL_DIM, B_KEY_WIDTH), F32)],
        compiler_params=_params("arbitrary", "arbitrary"),
    )(q, k, cum, vb, do, st_all)


def _merge(x2, tgt2, attn, za, o_gla, zb, ga, gb, w_oa, w_ob, w_o, w_oa_t, w_ob_t, w_o_t, g_gla, g_final):
    T = x2.shape[0]
    tm = 256

    def body(x_ref, tgt_ref, attn_ref, za_ref, og_ref, zb_ref, ga_ref, gb_ref,
             woa_ref, wob_ref, wo_ref, woat_ref, wobt_ref, wot_ref, gg_ref, gf_ref,
             dxres_ref, dattn_ref, dog_ref, dza_ref, dzb_ref, dga_ref, dgb_ref,
             dwo_ref, dwoa_ref, dwob_ref, dgf_ref, dgg_ref, loss_ref):
        @pl.when(pl.program_id(0) == 0)
        def _():
            for r in (dwo_ref, dwoa_ref, dwob_ref, dgf_ref, dgg_ref, loss_ref):
                r[...] = jnp.zeros_like(r)

        za_v = za_ref[...]
        sig_za = _sigmoid(za_v)
        silu_a = za_v * sig_za
        attn_v = attn_ref[...]
        oa = (attn_v * silu_a).astype(BF16)
        ya = _dot(oa, woa_ref[...])
        og = og_ref[...]
        zb_v = zb_ref[...]
        sig_zb = _sigmoid(zb_v)
        silu_b = zb_v * sig_zb
        gg = gg_ref[...]
        on_parts, rinv_parts = [], []
        for h in range(B_HEADS):
            seg = og[:, h * B_VAL_DIM:(h + 1) * B_VAL_DIM]
            rinv = lax.rsqrt(jnp.mean(seg * seg, axis=-1, keepdims=True) + NORM_EPS)
            rinv_parts.append(rinv)
            on_parts.append(seg * rinv)
        on = jnp.concatenate(on_parts, axis=1)
        obn = on * gg
        ob = (obn * silu_b).astype(BF16)
        yb = _dot(ob, wob_ref[...])
        sig_a, sig_b = _sigmoid(ga_ref[...]), _sigmoid(gb_ref[...])
        merged = (sig_a * ya + sig_b * yb).astype(BF16)
        out = x_ref[...] + _dot(merged, wo_ref[...])
        rf = lax.rsqrt(jnp.mean(out * out, axis=-1, keepdims=True) + NORM_EPS)
        nrm = out * rf
        gf = gf_ref[...]
        err = nrm * gf - tgt_ref[...]
        loss_ref[...] += jnp.sum(err * err) * (0.5 / D_MODEL)

        dy = err * (1.0 / D_MODEL)
        dgf_ref[...] += jnp.sum(dy * nrm, axis=0, keepdims=True)
        dn = dy * gf
        dout = rf * (dn - nrm * jnp.mean(dn * nrm, axis=-1, keepdims=True))
        dxres_ref[...] = dout
        dout_b = dout.astype(BF16)
        dmerged = _dot(dout_b, wot_ref[...])
        dwo_ref[...] += _dot_tn(merged, dout_b)
        dya = dmerged * sig_a
        dyb = dmerged * sig_b
        dga_ref[...] = (dmerged * ya * sig_a * (1.0 - sig_a)).astype(BF16)
        dgb_ref[...] = (dmerged * yb * sig_b * (1.0 - sig_b)).astype(BF16)
        dya_b, dyb_b = dya.astype(BF16), dyb.astype(BF16)
        dwoa_ref[...] += _dot_tn(oa, dya_b)
        dwob_ref[...] += _dot_tn(ob, dyb_b)
        doa = _dot(dya_b, woat_ref[...])
        dattn_ref[...] = (doa * silu_a).astype(BF16)
        dza_ref[...] = (doa * attn_v * (sig_za * (1.0 + za_v * (1.0 - sig_za)))).astype(BF16)
        dob = _dot(dyb_b, wobt_ref[...])
        dzb_ref[...] = (dob * obn * (sig_zb * (1.0 + zb_v * (1.0 - sig_zb)))).astype(BF16)
        dobn = dob * silu_b
        dgg_ref[...] += jnp.sum(dobn * on, axis=0, keepdims=True)
        don = dobn * gg
        for h in range(B_HEADS):
            sl = slice(h * B_VAL_DIM, (h + 1) * B_VAL_DIM)
            don_h, on_h = don[:, sl], on[:, sl]
            dog_ref[:, sl] = (rinv_parts[h] * (don_h - on_h * jnp.mean(don_h * on_h, axis=-1, keepdims=True))).astype(BF16)

    def rows(w):
        return pl.BlockSpec((tm, w), lambda i: (i, 0))

    def acc(shape):
        return pl.BlockSpec(shape, lambda i: (0, 0))

    outs = [((T, D_MODEL), F32, rows(D_MODEL)), ((T, A_WIDTH), BF16, rows(A_WIDTH)), ((T, B_WIDTH), BF16, rows(B_WIDTH)),
            ((T, A_WIDTH), BF16, rows(A_WIDTH)), ((T, B_WIDTH), BF16, rows(B_WIDTH)),
            ((T, D_MODEL), BF16, rows(D_MODEL)), ((T, D_MODEL), BF16, rows(D_MODEL)),
            ((D_MODEL, D_MODEL), F32, acc((D_MODEL, D_MODEL))), ((A_WIDTH, D_MODEL), F32, acc((A_WIDTH, D_MODEL))),
            ((B_WIDTH, D_MODEL), F32, acc((B_WIDTH, D_MODEL))), ((1, D_MODEL), F32, acc((1, D_MODEL))),
            ((1, B_WIDTH), F32, acc((1, B_WIDTH))), ((8, LANES), F32, acc((8, LANES)))]
    return pl.pallas_call(
        body, name="merge", grid=(T // tm,),
        in_specs=[rows(D_MODEL), rows(D_MODEL), rows(A_WIDTH), rows(A_WIDTH), rows(B_WIDTH), rows(B_WIDTH),
                  rows(D_MODEL), rows(D_MODEL),
                  _const_spec((A_WIDTH, D_MODEL)), _const_spec((B_WIDTH, D_MODEL)), _const_spec((D_MODEL, D_MODEL)),
                  _const_spec((D_MODEL, A_WIDTH)), _const_spec((D_MODEL, B_WIDTH)), _const_spec((D_MODEL, D_MODEL)),
                  _const_spec((1, B_WIDTH)), _const_spec((1, D_MODEL))],
        out_specs=[o[2] for o in outs],
        out_shape=[jax.ShapeDtypeStruct(o[0], o[1]) for o in outs],
        compiler_params=_params("arbitrary"),
    )(x2, tgt2, attn, za, o_gla, zb, ga, gb, w_oa, w_ob, w_o, w_oa_t, w_ob_t, w_o_t, g_gla, g_final)


def _in_proj_bwd(x2, dxres, cosf, sinf, g_in, w_pad_t, wa_pad_t, parts):
    T = x2.shape[0]
    tm = 256

    def body(x_ref, dxres_ref, cos_ref, sin_ref, g_ref, wt_ref, wat_ref,
             dq_ref, dkv_ref, dza_ref, dqb_ref, dkb_ref, dvb_ref, dzb_ref, dla_ref, u_ref, alr_ref, dga_ref, dgb_ref,
             dx_ref, dproj_ref, dgin_ref, dba_ref, dwa_ref):
        @pl.when(pl.program_id(0) == 0)
        def _():
            for r in (dgin_ref, dba_ref, dwa_ref):
                r[...] = jnp.zeros_like(r)

        cos, nsin = cos_ref[...], -sin_ref[...]
        for s in range(A_WIDTH // LANES):
            sl = slice(s * LANES, (s + 1) * LANES)
            dproj_ref[:, sl] = _rope_slab(dq_ref[:, sl].astype(F32), cos, nsin).astype(BF16)
        dproj_ref[:, 512:640] = _rope_slab(dkv_ref[:, 0:LANES].astype(F32), cos, nsin).astype(BF16)
        dproj_ref[:, 640:768] = dkv_ref[:, LANES:]

        def put(name, val):
            a, b = SEG[name]
            dproj_ref[:, a:b] = val

        put("za", dza_ref[...])
        put("qb", dqb_ref[...])
        put("kb", dkb_ref[...])
        put("vb", dvb_ref[...])
        put("zb", dzb_ref[...])
        put("ga", dga_ref[...])
        put("gb", dgb_ref[...])
        du = dla_ref[...] * (1.0 / B_GATE_TEMP) * _sigmoid(-u_ref[...])
        dba_ref[...] += jnp.sum(du, axis=0, keepdims=True)
        du_b = du.astype(BF16)
        dwa_ref[...] += _dot_tn(alr_ref[...], du_b)
        put("alr", _dot(du_b, wat_ref[...]).astype(BF16))

        dh = _dot(dproj_ref[...], wt_ref[...])
        x = x_ref[...]
        r = lax.rsqrt(jnp.mean(x * x, axis=-1, keepdims=True) + NORM_EPS)
        nrm = x * r
        dgin_ref[...] += jnp.sum(dh * nrm, axis=0, keepdims=True)
        dn = dh * g_ref[...]
        dx_ref[...] = dxres_ref[...] + r * (dn - nrm * jnp.mean(dn * nrm, axis=-1, keepdims=True))

    def rows(w):
        return pl.BlockSpec((tm, w), lambda i: (i, 0))

    def acc(shape):
        return pl.BlockSpec(shape, lambda i: (0, 0))

    names = ["dq", "dkv", "dza", "dqb", "dkb", "dvb", "dzb", "dla", "u", "alr", "dga", "dgb"]
    return pl.pallas_call(
        body, name="in_proj_bwd", grid=(T // tm,),
        in_specs=[rows(D_MODEL), rows(D_MODEL), rows(LANES), rows(LANES), _const_spec((1, D_MODEL)),
                  _const_spec((D_IN_PAD, D_MODEL)), _const_spec((B_KEY_WIDTH, RANK_PAD))]
                 + [rows(parts[n].shape[1]) for n in names],
        out_specs=[rows(D_MODEL), rows(D_IN_PAD), acc((1, D_MODEL)), acc((1, B_KEY_WIDTH)), acc((RANK_PAD, B_KEY_WIDTH))],
        out_shape=[jax.ShapeDtypeStruct((T, D_MODEL), F32), jax.ShapeDtypeStruct((T, D_IN_PAD), BF16),
                   jax.ShapeDtypeStruct((1, D_MODEL), F32), jax.ShapeDtypeStruct((1, B_KEY_WIDTH), F32),
                   jax.ShapeDtypeStruct((RANK_PAD, B_KEY_WIDTH), F32)],
        compiler_params=_params("arbitrary"),
    )(x2, dxres, cosf, sinf, g_in, w_pad_t, wa_pad_t, *[parts[n] for n in names])


def _w_in_grad(h_t, dproj):
    T = h_t.shape[1]
    tn, tk = 384, math.gcd(T, 1024)

    def body(a_ref, b_ref, o_ref):
        @pl.when(pl.program_id(1) == 0)
        def _():
            o_ref[...] = jnp.zeros_like(o_ref)

        o_ref[...] += _dot(a_ref[...], b_ref[...])

    return pl.pallas_call(
        body, name="w_in_grad", grid=(D_IN_PAD // tn, T // tk),
        in_specs=[pl.BlockSpec((D_MODEL, tk), lambda j, kk: (0, kk)), pl.BlockSpec((tk, tn), lambda j, kk: (kk, j))],
        out_specs=pl.BlockSpec((D_MODEL, tn), lambda j, kk: (0, j)),
        out_shape=jax.ShapeDtypeStruct((D_MODEL, D_IN_PAD), F32),
        compiler_params=_params("parallel", "arbitrary"),
    )(h_t, dproj)


def _my_place():
    return lax.axis_index("x"), lax.axis_index("y"), lax.axis_index("c")


def _all_gather_slab(slab):
    R = slab.shape[0]

    def body(x_ref, out_ref, send_sems, recv_sems, local_sem):
        x, y, c = _my_place()
        me, sibling = (x, y, c), (x, y, 1 - c)
        chips = [(1 - x, y), (x, 1 - y), (1 - x, 1 - y)]

        def rows(px, py, pc):
            return out_ref.at[4 * px + 2 * py + pc]

        def copy(k, block, to, src=None):
            return pltpu.make_async_remote_copy(
                src_ref=rows(*block) if src is None else src, dst_ref=rows(*block),
                send_sem=send_sems.at[k], recv_sem=recv_sems.at[k], device_id=to, device_id_type=MESH)

        mine = pltpu.make_async_copy(x_ref, rows(*me), local_sem)
        mine.start()
        first = [copy(0, me, sibling, src=x_ref)]
        first += [copy(1 + j, me, (*chip, c), src=x_ref) for j, chip in enumerate(chips)]
        for cp in first:
            cp.start()
        passed = [copy(4 + j, (*chip, c), sibling) for j, chip in enumerate(chips)]
        for j, chip in enumerate(chips):
            copy(1 + j, (*chip, c), me).wait_recv()
            passed[j].start()
        copy(0, sibling, me).wait_recv()
        for j, chip in enumerate(chips):
            copy(4 + j, (*chip, 1 - c), me).wait_recv()
        for cp in first + passed:
            cp.wait_send()
        mine.wait()

    return pl.pallas_call(
        body, name="gather_weights",
        in_specs=[pl.BlockSpec(memory_space=pl.ANY)], out_specs=pl.BlockSpec(memory_space=pl.ANY),
        out_shape=jax.ShapeDtypeStruct((N_DEV, R, LANES), slab.dtype),
        scratch_shapes=[pltpu.SemaphoreType.DMA((7,)), pltpu.SemaphoreType.DMA((7,)), pltpu.SemaphoreType.DMA],
    )(slab)


def _exchange_sibling(g4):
    R = g4.shape[2]

    def body(g_ref, recv_ref, send_sems, recv_sems):
        x, y, c = _my_place()
        copies = [pltpu.make_async_remote_copy(
            src_ref=g_ref.at[j, 1 - c], dst_ref=recv_ref.at[j], send_sem=send_sems.at[j], recv_sem=recv_sems.at[j],
            device_id=(x, y, 1 - c), device_id_type=MESH) for j in range(4)]
        for cp in copies:
            cp.start()
        for cp in copies:
            cp.wait()

    return pl.pallas_call(
        body, name="grad_exchange_sibling",
        in_specs=[pl.BlockSpec(memory_space=pl.ANY)], out_specs=pl.BlockSpec(memory_space=pl.ANY),
        out_shape=jax.ShapeDtypeStruct((4, R, LANES), g4.dtype),
        scratch_shapes=[pltpu.SemaphoreType.DMA((4,)), pltpu.SemaphoreType.DMA((4,))],
    )(g4)


def _add_sibling(g4, recv, c_idx):
    R = g4.shape[2]
    tr = R // 2

    def body(c_ref, a_ref, b_ref, o_ref):
        o_ref[...] = a_ref[...] + b_ref[...]

    return pl.pallas_call(
        body, name="grad_add_sibling",
        grid_spec=pltpu.PrefetchScalarGridSpec(
            num_scalar_prefetch=1, grid=(4, R // tr),
            in_specs=[pl.BlockSpec((None, None, tr, LANES), lambda j, r, c_ref: (j, c_ref[0], r, 0)),
                      pl.BlockSpec((None, tr, LANES), lambda j, r, c_ref: (j, r, 0))],
            out_specs=pl.BlockSpec((None, tr, LANES), lambda j, r, c_ref: (j, r, 0))),
        out_shape=jax.ShapeDtypeStruct((4, R, LANES), F32),
        compiler_params=_params("parallel", "parallel"),
    )(c_idx, g4, recv)


def _exchange_chips(p4, small):
    R = p4.shape[1]
    Rs = small.shape[0]
    flips = [(dx, dy, dc) for dx in (0, 1) for dy in (0, 1) for dc in (0, 1)][1:]

    def body(p_ref, s_ref, mine_ref, recv_ref, sall_ref, send_sems, recv_sems, ssend_sems, srecv_sems, local_sems):
        x, y, c = _my_place()
        my_chip = 2 * x + y
        my_dev = 4 * x + 2 * y + c
        keep = pltpu.make_async_copy(p_ref.at[my_chip], mine_ref, local_sems.at[0])
        keep_small = pltpu.make_async_copy(s_ref, sall_ref.at[my_dev], local_sems.at[1])
        keep.start()
        keep_small.start()
        big = []
        for kk, (dx, dy) in enumerate([(1, 0), (0, 1), (1, 1)]):
            px, py = x ^ dx, y ^ dy
            big.append(pltpu.make_async_remote_copy(
                src_ref=p_ref.at[2 * px + py], dst_ref=recv_ref.at[kk], send_sem=send_sems.at[kk],
                recv_sem=recv_sems.at[kk], device_id=(px, py, c), device_id_type=MESH))
        tiny = []
        for kk, (dx, dy, dc) in enumerate(flips):
            tiny.append(pltpu.make_async_remote_copy(
                src_ref=s_ref, dst_ref=sall_ref.at[my_dev], send_sem=ssend_sems.at[kk], recv_sem=srecv_sems.at[kk],
                device_id=(x ^ dx, y ^ dy, c ^ dc), device_id_type=MESH))
        for cp in tiny + big:
            cp.start()
        for kk, (dx, dy, dc) in enumerate(flips):
            peer = 4 * (x ^ dx) + 2 * (y ^ dy) + (c ^ dc)
            pltpu.make_async_remote_copy(
                src_ref=s_ref, dst_ref=sall_ref.at[peer], send_sem=ssend_sems.at[kk], recv_sem=srecv_sems.at[kk],
                device_id=(x ^ dx, y ^ dy, c ^ dc), device_id_type=MESH).wait_recv()
        for cp in big:
            cp.wait_recv()
        for cp in tiny + big:
            cp.wait_send()
        keep.wait()
        keep_small.wait()

    return pl.pallas_call(
        body, name="grad_exchange_chips",
        in_specs=[pl.BlockSpec(memory_space=pl.ANY)] * 2, out_specs=[pl.BlockSpec(memory_space=pl.ANY)] * 3,
        out_shape=[jax.ShapeDtypeStruct((R, LANES), F32), jax.ShapeDtypeStruct((3, R, LANES), F32),
                   jax.ShapeDtypeStruct((N_DEV, Rs, LANES), F32)],
        scratch_shapes=[pltpu.SemaphoreType.DMA((3,)), pltpu.SemaphoreType.DMA((3,)),
                        pltpu.SemaphoreType.DMA((7,)), pltpu.SemaphoreType.DMA((7,)), pltpu.SemaphoreType.DMA((2,))],
    )(p4, small)


def _adam_math(w, g, m, v):
    m_new = ADAM_B1 * m + (1.0 - ADAM_B1) * g
    v_new = ADAM_B2 * v + (1.0 - ADAM_B2) * (g * g)
    m_hat = m_new / (1.0 - ADAM_B1 ** ADAM_STEP)
    v_hat = v_new / (1.0 - ADAM_B2 ** ADAM_STEP)
    delta = -ADAM_LR * (m_hat / (jnp.sqrt(v_hat) + ADAM_EPS) + ADAM_WD * w)
    return delta, m_new, v_new


def _adam_big(mine, recv, w, m, v, row0, nrows):
    rw, cw = w.shape
    tr = rw // 8 if rw % 64 == 0 else rw

    def body(g0_ref, g1_ref, g2_ref, g3_ref, w_ref, m_ref, v_ref, g_ref, d_ref, mo_ref, vo_ref):
        g = ((g0_ref[...] + g1_ref[...]) + g2_ref[...]) + g3_ref[...]
        g_ref[...] = g
        d_ref[...], mo_ref[...], vo_ref[...] = _adam_math(w_ref[...], g, m_ref[...], v_ref[...])

    spec = pl.BlockSpec((tr, cw), lambda i: (i, 0))
    parts = [mine[row0:row0 + nrows].reshape(rw, cw)] + [recv[kk, row0:row0 + nrows].reshape(rw, cw) for kk in range(3)]
    return pl.pallas_call(
        body, name=f"adam_{row0}", grid=(rw // tr,),
        in_specs=[spec] * 7, out_specs=[spec] * 4,
        out_shape=[jax.ShapeDtypeStruct((rw, cw), F32)] * 4,
        compiler_params=_params("parallel"),
    )(*parts, w, m, v)


SMALL_ROWS = {"g_in": (0, 8), "g_final": (8, 8), "g_gla": (16, 4), "b_alpha": (20, 2), "sinks": (22, 1), "w_alpha": (24, 32)}


def _adam_small(small_all, w_slab, m_slab, v_slab):
    Rs = w_slab.shape[0]

    def body(s_ref, w_ref, m_ref, v_ref, g_ref, d_ref, mo_ref, vo_ref):
        g = s_ref[0]
        for dev in range(1, N_DEV):
            g = g + s_ref[dev]
        g_ref[...] = g
        d_ref[...], mo_ref[...], vo_ref[...] = _adam_math(w_ref[...], g, m_ref[...], v_ref[...])

    return pl.pallas_call(
        body, name="adam_small",
        out_shape=[jax.ShapeDtypeStruct((Rs, LANES), F32)] * 4,
    )(small_all, w_slab, m_slab, v_slab)


def _local_step(x, positions, loss_target, g_in, w_pad, wa_pad, b_alpha, sinks, g_gla, w_oa, w_ob, w_o, g_final):
    B, S, _ = x.shape
    T = B * S
    x2 = x.reshape(T, D_MODEL)
    tgt2 = loss_target.reshape(T, D_MODEL)
    cosf, sinf = _rope_tables(positions.reshape(T, 1))
    f = _in_proj(x2, cosf, sinf, g_in, w_pad, wa_pad, b_alpha)
    attn, lse = _attn_fwd(f["qkv"], sinks, B, S)
    o_gla, st_all = _gla_fwd(f["q"], f["k"], f["cum"], f["vb"], B, S)
    (dxres, dattn, dog, dza, dzb, dga, dgb, dw_o, dw_oa, dw_ob, dg_final, dg_gla, loss_acc) = _merge(
        x2, tgt2, attn, f["za"], o_gla, f["zb"], f["ga"], f["gb"], w_oa, w_ob, w_o, w_oa.T, w_ob.T, w_o.T, g_gla, g_final)
    dq, dkv, dsink = _attn_bwd(f["qkv"], dattn, attn, lse, sinks, B, S)
    dqb, dkb, dvb, dla = _gla_bwd(f["q"], f["k"], f["cum"], f["vb"], dog, st_all, B, S)
    parts = dict(dq=dq, dkv=dkv, dza=dza, dqb=dqb, dkb=dkb, dvb=dvb, dzb=dzb, dla=dla, u=f["u"], alr=f["alr"],
                 dga=dga, dgb=dgb)
    dx, dproj, dg_in, db_alpha, dw_alpha = _in_proj_bwd(x2, dxres, cosf, sinf, g_in, w_pad.T, wa_pad.T, parts)
    dw_in_pad = _w_in_grad(f["h"].T, dproj)
    return dict(loss=loss_acc[0, 0], grad_x=dx.reshape(B, S, D_MODEL), dw_in_pad=dw_in_pad, dw_alpha=dw_alpha[:B_GATE_RANK],
                db_alpha=db_alpha, dsinks=dsink[0:1, :A_HEADS], dg_gla=dg_gla, dw_oa=dw_oa, dw_ob=dw_ob, dw_o=dw_o,
                dg_final=dg_final, dg_in=dg_in)


def _pad_alr_cols(w):
    z = jnp.zeros((w.shape[0], RANK_PAD - B_GATE_RANK), w.dtype)
    return jnp.concatenate([w[:, :ALR_SRC + B_GATE_RANK], z, w[:, ALR_SRC + B_GATE_RANK:]], axis=1)


def _unpad_alr_cols(w):
    a = SEG["alr"][0]
    return jnp.concatenate([w[:, :a + B_GATE_RANK], w[:, a + RANK_PAD:]], axis=1)


def _small_slab(g_in, g_final, g_gla, b_alpha, sinks, w_alpha):
    rows = [g_in.reshape(8, LANES), g_final.reshape(8, LANES), g_gla.reshape(4, LANES), b_alpha.reshape(2, LANES),
            jnp.pad(sinks.reshape(1, A_HEADS), ((0, 1), (0, LANES - A_HEADS))), w_alpha.reshape(32, LANES)]
    return jnp.concatenate(rows, axis=0)


def kernel(x, positions, g_in, w_in, w_alpha_up, b_alpha, attn_sinks, g_gla_norm, w_out_a, w_out_b, w_o, g_final, loss_target, m_g_in, m_w_in, m_w_alpha_up, m_b_alpha, m_attn_sinks, m_g_gla_norm, m_w_out_a, m_w_out_b, m_w_o, m_g_final, v_g_in, v_w_in, v_w_alpha_up, v_b_alpha, v_attn_sinks, v_g_gla_norm, v_w_out_a, v_w_out_b, v_w_o, v_g_final):
    xi, yi, ci = _my_place()
    dev = 4 * xi + 2 * yi + ci

    alpha_rows = jnp.pad(w_alpha_up[0].reshape(4, LANES), ((0, ROWS_ALPHA - 4), (0, 0)))
    slab = jnp.concatenate([w_in[0].reshape(ROWS_W_IN, LANES), w_out_a[0].reshape(ROWS_W_OUT, LANES),
                            w_out_b[0].reshape(ROWS_W_OUT, LANES), w_o[0].reshape(ROWS_W_O, LANES), alpha_rows],
                           axis=0).astype(BF16)
    gathered = _all_gather_slab(slab)
    r0 = 0
    w_in_full = gathered[:, r0:r0 + ROWS_W_IN].reshape(N_DEV, D_MODEL, D_IN // N_DEV).transpose(1, 0, 2).reshape(D_MODEL, D_IN)
    r0 += ROWS_W_IN
    w_oa_full = gathered[:, r0:r0 + ROWS_W_OUT].reshape(N_DEV, A_WIDTH, LANES).transpose(1, 0, 2).reshape(A_WIDTH, D_MODEL)
    r0 += ROWS_W_OUT
    w_ob_full = gathered[:, r0:r0 + ROWS_W_OUT].reshape(N_DEV, B_WIDTH, LANES).transpose(1, 0, 2).reshape(B_WIDTH, D_MODEL)
    r0 += ROWS_W_OUT
    w_o_full = gathered[:, r0:r0 + ROWS_W_O].reshape(D_MODEL, D_MODEL)
    r0 += ROWS_W_O
    w_alpha_full = gathered[:, r0:r0 + 4].reshape(N_DEV, B_GATE_RANK, B_KEY_WIDTH // N_DEV).transpose(1, 0, 2).reshape(B_GATE_RANK, B_KEY_WIDTH)
    w_pad = _pad_alr_cols(w_in_full)
    wa_pad = jnp.pad(w_alpha_full, ((0, RANK_PAD - B_GATE_RANK), (0, 0)))

    r = _local_step(x, positions, loss_target, g_in, w_pad, wa_pad, b_alpha, attn_sinks[0], g_gla_norm, w_oa_full,
                    w_ob_full, w_o_full, g_final.reshape(1, D_MODEL))
    loss = lax.psum(r["loss"], ("x", "y", "c"))

    dw_in = _unpad_alr_cols(r["dw_in_pad"])
    g_slab = jnp.concatenate([
        dw_in.reshape(D_MODEL, N_DEV, D_IN // N_DEV).transpose(1, 0, 2).reshape(N_DEV, ROWS_W_IN, LANES),
        r["dw_oa"].reshape(A_WIDTH, N_DEV, LANES).transpose(1, 0, 2),
        r["dw_ob"].reshape(B_WIDTH, N_DEV, LANES).transpose(1, 0, 2),
        r["dw_o"].reshape(N_DEV, ROWS_W_O, LANES)], axis=1)
    g4 = g_slab.reshape(4, 2, ROWS_BIG, LANES)
    from_sibling = _exchange_sibling(g4)
    p4 = _add_sibling(g4, from_sibling, ci.reshape(1).astype(jnp.int32))
    small = _small_slab(r["dg_in"], r["dg_final"], r["dg_gla"], r["db_alpha"], r["dsinks"], r["dw_alpha"])
    mine, recv, small_all = _exchange_chips(p4, small)

    cw_in = D_IN // N_DEV
    g_w_in, d_w_in, nm_w_in, nv_w_in = _adam_big(mine, recv, w_in[0], m_w_in[0], v_w_in[0], 0, ROWS_W_IN)
    g_w_oa, d_w_oa, nm_w_oa, nv_w_oa = _adam_big(mine, recv, w_out_a[0], m_w_out_a[0], v_w_out_a[0], ROWS_W_IN, ROWS_W_OUT)
    g_w_ob, d_w_ob, nm_w_ob, nv_w_ob = _adam_big(mine, recv, w_out_b[0], m_w_out_b[0], v_w_out_b[0],
                                                 ROWS_W_IN + ROWS_W_OUT, ROWS_W_OUT)
    g_w_o, d_w_o, nm_w_o, nv_w_o = _adam_big(mine, recv, w_o[0], m_w_o[0], v_w_o[0], ROWS_W_IN + 2 * ROWS_W_OUT, ROWS_W_O)

    def full_alpha(a):
        return lax.dynamic_update_slice(jnp.zeros((B_GATE_RANK, B_KEY_WIDTH), F32), a[0], (0, dev * (B_KEY_WIDTH // N_DEV)))

    def pack(gi, gf, gg, ba, sk, wa):
        return _small_slab(gi, gf, gg, ba, sk, full_alpha(wa))

    w_s = pack(g_in, g_final, g_gla_norm, b_alpha, attn_sinks, w_alpha_up)
    m_s = pack(m_g_in, m_g_final, m_g_gla_norm, m_b_alpha, m_attn_sinks, m_w_alpha_up)
    v_s = pack(v_g_in, v_g_final, v_g_gla_norm, v_b_alpha, v_attn_sinks, v_w_alpha_up)
    small_out = _adam_small(small_all, w_s, m_s, v_s)

    def unpack(s):
        def take(name, shape):
            a, n = SMALL_ROWS[name]
            return s[a:a + n].reshape(shape)
        wa = lax.dynamic_slice(take("w_alpha", (B_GATE_RANK, B_KEY_WIDTH)), (0, dev * (B_KEY_WIDTH // N_DEV)),
                               (B_GATE_RANK, B_KEY_WIDTH // N_DEV))[None]
        return dict(g_in=take("g_in", (1, D_MODEL)), g_final=take("g_final", (D_MODEL,)), g_gla=take("g_gla", (1, B_WIDTH)),
                    b_alpha=take("b_alpha", (1, B_KEY_WIDTH)), sinks=s[22:23, :A_HEADS], w_alpha=wa)

    sg, sd, sm, sv = [unpack(s) for s in small_out]

    def group(s, big_in, big_oa, big_ob, big_o):
        return (s["g_in"], big_in[None], s["w_alpha"], s["b_alpha"], s["sinks"], s["g_gla"], big_oa[None], big_ob[None],
                big_o[None], s["g_final"])

    return (loss, r["grad_x"],
            *group(sg, g_w_in, g_w_oa, g_w_ob, g_w_o),
            *group(sd, d_w_in, d_w_oa, d_w_ob, d_w_o),
            *group(sm, nm_w_in, nm_w_oa, nm_w_ob, nm_w_o),
            *group(sv, nv_w_in, nv_w_oa, nv_w_ob, nv_w_o))
```

```python
import functools
import math

import numpy as np
import jax
import jax.numpy as jnp
from jax import lax
from jax.experimental import pallas as pl
from jax.experimental.pallas import tpu as pltpu

F32 = jnp.float32
BF16 = jnp.bfloat16
MESH = pl.DeviceIdType.MESH

D_MODEL = 1024
A_HEADS, A_KV_HEADS, A_HEAD_DIM = 8, 2, 64
A_WIDTH, A_KV_WIDTH = 512, 128
WINDOW = 128
ROPE_THETA = 500000.0
ROPE_DIM = 16
B_HEADS, B_KEY_DIM, B_VAL_DIM = 4, 64, 128
B_KEY_WIDTH, B_WIDTH = 256, 512
B_GATE_RANK = 16
B_GATE_TEMP = 16.0
B_CHUNK = 64
NORM_EPS = 1e-6
NEG_BIG = -1e30
D_IN = 4880
N_DEV = 8
N_CHIPS = 4
ADAM_LR, ADAM_B1, ADAM_B2, ADAM_EPS, ADAM_WD, ADAM_STEP = 0.001, 0.9, 0.999, 1e-08, 0.01, 10

LANES = 128
V7X_VMEM_LIMIT = 56 * 1024 * 1024

RANK_PAD = LANES
SEG = {}
_off = 0
for _name, _w in (("qa", 512), ("ka", 128), ("va", 128), ("za", 512), ("qb", 256), ("kb", 256),
                  ("vb", 512), ("zb", 512), ("alr", RANK_PAD), ("ga", 1024), ("gb", 1024)):
    SEG[_name] = (_off, _off + _w)
    _off += _w
D_IN_PAD = _off
ALR_SRC = SEG["alr"][0]
QKV_K, QKV_V, QKV_W = SEG["ka"][0], SEG["va"][0], SEG["va"][1]

SHARD_IN = D_IN // N_DEV
SHARD_OUT = D_MODEL // N_DEV
SHARD_ALPHA = B_KEY_WIDTH // N_DEV

SMALL_G_FINAL, SMALL_G_GLA, SMALL_SINKS, SMALL_G_IN, SMALL_B_ALPHA, SMALL_W_ALPHA = 0, 8, 16, 24, 32, 40
SMALL_ROWS = 72


def _dot(a, b):
    return jnp.dot(a, b, preferred_element_type=F32)


def _dot_nt(a, b):
    return lax.dot_general(a, b, (((1,), (1,)), ((), ())), preferred_element_type=F32)


def _dot_tn(a, b):
    return lax.dot_general(a, b, (((0,), (0,)), ((), ())), preferred_element_type=F32)


def _sigmoid(z):
    return 1.0 / (1.0 + jnp.exp(-z))


def _params(*sem):
    return pltpu.CompilerParams(dimension_semantics=sem, vmem_limit_bytes=V7X_VMEM_LIMIT)


def _const_spec(shape):
    nd = len(shape)
    return pl.BlockSpec(shape, lambda *_: (0,) * nd, pipeline_mode=pl.Buffered(1))


def _lane_iota(shape):
    return lax.broadcasted_iota(jnp.int32, shape, 1)


def _row_iota(shape):
    return lax.broadcasted_iota(jnp.int32, shape, 0)


def _split3(v):
    hi = v.astype(BF16)
    r1 = v - hi.astype(F32)
    mid = r1.astype(BF16)
    lo = (r1 - mid.astype(F32)).astype(BF16)
    return hi, mid, lo


def _put_rows(ref, row0, vec):
    for r in range(vec.shape[1] // LANES):
        ref[row0 + r:row0 + r + 1, :] = vec[:, r * LANES:(r + 1) * LANES]


def _take_rows(slab, row0, n):
    return jnp.concatenate([slab[row0 + r:row0 + r + 1, :] for r in range(n)], axis=1)


def _rope_lane_constants():
    half = ROPE_DIM // 2
    inv_freq = np.exp(-math.log(ROPE_THETA) * np.arange(half, dtype=np.float32) * np.float32(2.0 / ROPE_DIM)).astype(np.float32)
    lane = np.arange(LANES)
    j = lane % A_HEAD_DIM
    invf = np.where(j < ROPE_DIM, inv_freq[j % half], 0.0).astype(np.float32)
    sign = np.where(j < half, -1.0, np.where(j < ROPE_DIM, 1.0, 0.0)).astype(np.float32)
    return jnp.asarray(invf)[None, :], jnp.asarray(sign)[None, :]


def _rope_tables(pos_col):
    T = pos_col.shape[0]
    tm = math.gcd(T, 1024)
    invf, sign = _rope_lane_constants()

    def body(pos_ref, invf_ref, sign_ref, cos_ref, sin_ref):
        ang = pos_ref[...].astype(F32) * invf_ref[...]
        cos_ref[...] = jnp.cos(ang)
        sin_ref[...] = jnp.sin(ang) * sign_ref[...]

    return pl.pallas_call(
        body, name="rope_tables", grid=(T // tm,),
        in_specs=[pl.BlockSpec((tm, 1), lambda i: (i, 0)), _const_spec((1, LANES)), _const_spec((1, LANES))],
        out_specs=[pl.BlockSpec((tm, LANES), lambda i: (i, 0))] * 2,
        out_shape=[jax.ShapeDtypeStruct((T, LANES), F32)] * 2,
        compiler_params=_params("parallel"),
    )(pos_col, invf, sign)


def _rope_slab(t, cos, sin_signed):
    first = (_lane_iota(t.shape) % A_HEAD_DIM) < (ROPE_DIM // 2)
    partner = jnp.where(first, pltpu.roll(t, LANES - ROPE_DIM // 2, 1), pltpu.roll(t, ROPE_DIM // 2, 1))
    return t * cos + partner * sin_signed


def _in_proj(x2, cosf, sinf, g_in, w_pad, wa_pad, b_alpha):
    T = x2.shape[0]
    tm = 256

    def body(x_ref, cos_ref, sin_ref, g_ref, w_ref, wa_ref, ba_ref,
             h_ref, qkv_ref, za_ref, q_ref, k_ref, vb_ref, zb_ref, alr_ref, u_ref, cum_ref, ga_ref, gb_ref):
        x = x_ref[...]
        r = lax.rsqrt(jnp.mean(x * x, axis=-1, keepdims=True) + NORM_EPS)
        h = (x * r * g_ref[...]).astype(BF16)
        h_ref[...] = h

        def seg(name):
            a, b = SEG[name]
            return _dot(h, w_ref[:, a:b])

        cos, sin = cos_ref[...], sin_ref[...]
        qa = seg("qa")
        for s in range(A_WIDTH // LANES):
            qkv_ref[:, s * LANES:(s + 1) * LANES] = _rope_slab(qa[:, s * LANES:(s + 1) * LANES], cos, sin).astype(BF16)
        qkv_ref[:, QKV_K:QKV_V] = _rope_slab(seg("ka"), cos, sin).astype(BF16)
        qkv_ref[:, QKV_V:QKV_W] = seg("va").astype(BF16)
        za_ref[...] = seg("za")
        q_ref[...] = seg("qb")
        k_ref[...] = seg("kb")
        vb_ref[...] = seg("vb").astype(BF16)
        zb_ref[...] = seg("zb")
        ga_ref[...] = seg("ga")
        gb_ref[...] = seg("gb")
        alr = seg("alr").astype(BF16)
        alr_ref[...] = alr
        u = _dot(alr, wa_ref[...]) + ba_ref[...]
        u_ref[...] = u
        log_a = (jnp.minimum(u, 0.0) - jnp.log(1.0 + jnp.exp(-jnp.abs(u)))) * (1.0 / B_GATE_TEMP)
        row, col = _row_iota((tm, tm)), _lane_iota((tm, tm))
        tri = ((row // B_CHUNK == col // B_CHUNK) & (col <= row)).astype(BF16)
        hi, mid, lo = _split3(log_a)
        cum_ref[...] = _dot(tri, hi) + _dot(tri, mid) + _dot(tri, lo)

    def rows(w):
        return pl.BlockSpec((tm, w), lambda i: (i, 0))

    outs = [("h", D_MODEL, BF16), ("qkv", QKV_W, BF16), ("za", A_WIDTH, F32), ("q", B_KEY_WIDTH, F32),
            ("k", B_KEY_WIDTH, F32), ("vb", B_WIDTH, BF16), ("zb", B_WIDTH, F32), ("alr", RANK_PAD, BF16),
            ("u", B_KEY_WIDTH, F32), ("cum", B_KEY_WIDTH, F32), ("ga", D_MODEL, F32), ("gb", D_MODEL, F32)]
    res = pl.pallas_call(
        body, name="in_proj", grid=(T // tm,),
        in_specs=[rows(D_MODEL), rows(LANES), rows(LANES), _const_spec((1, D_MODEL)),
                  _const_spec((D_MODEL, D_IN_PAD)), _const_spec((RANK_PAD, B_KEY_WIDTH)), _const_spec((1, B_KEY_WIDTH))],
        out_specs=[rows(w) for _, w, _ in outs],
        out_shape=[jax.ShapeDtypeStruct((T, w), dt) for _, w, dt in outs],
        compiler_params=_params("parallel"),
    )(x2, cosf, sinf, g_in, w_pad, wa_pad, b_alpha)
    return dict(zip([n for n, _, _ in outs], res))


def _dup_kv_head(t, g):
    tf = t.astype(F32)
    keep = (_lane_iota(tf.shape) < A_HEAD_DIM) == (g == 0)
    return jnp.where(keep, tf, pltpu.roll(tf, A_HEAD_DIM, 1)).astype(BF16)


def _stack_heads(t):
    lo = _lane_iota(t.shape) < A_HEAD_DIM
    zero = jnp.zeros_like(t)
    return jnp.concatenate([jnp.where(lo, t, zero), jnp.where(lo, zero, t)], axis=0)


def _band_mask(n):
    qi = _row_iota((2 * WINDOW, 2 * WINDOW)) % WINDOW
    kj = _lane_iota((2 * WINDOW, 2 * WINDOW)) - WINDOW
    return (kj <= qi) & (qi - kj < WINDOW) & ((n > 0) | (kj >= 0))


def _attn_fwd(qkv, sinks, B, S):
    T = B * S
    nb = S // WINDOW
    scale = A_HEAD_DIM ** -0.5

    def body(sink_ref, q_ref, kc_ref, vc_ref, kp_ref, vp_ref, o_ref, lse_ref):
        n = pl.program_id(1)
        valid = _band_mask(n)
        k = jnp.concatenate([kp_ref[...], kc_ref[...]], axis=0)
        v = jnp.concatenate([vp_ref[...], vc_ref[...]], axis=0)
        top = _row_iota((2 * WINDOW, 1)) < WINDOW
        lo = _lane_iota((WINDOW, LANES)) < A_HEAD_DIM
        lane = _lane_iota((WINDOW, LANES))
        lse_tile = jnp.zeros((WINDOW, LANES), F32)
        for g in range(A_KV_HEADS):
            kd, vd = _dup_kv_head(k, g), _dup_kv_head(v, g)
            for p in (2 * g, 2 * g + 1):
                qs = _stack_heads(q_ref[:, p * LANES:(p + 1) * LANES])
                s = jnp.where(valid, _dot_nt(qs, kd) * scale, NEG_BIG)
                sink = jnp.where(top, sink_ref[2 * p], sink_ref[2 * p + 1])
                m = jnp.maximum(jnp.max(s, axis=-1, keepdims=True), sink)
                e = jnp.exp(s - m)
                den = jnp.sum(e, axis=-1, keepdims=True) + jnp.exp(sink - m)
                o = _dot((e * (1.0 / den)).astype(BF16), vd)
                o_ref[:, p * LANES:(p + 1) * LANES] = jnp.where(lo, o[:WINDOW], o[WINDOW:])
                lse = m + jnp.log(den)
                lse_tile = jnp.where(lane == 2 * p, lse[:WINDOW], lse_tile)
                lse_tile = jnp.where(lane == 2 * p + 1, lse[WINDOW:], lse_tile)
        lse_ref[...] = lse_tile

    def cur(col, w):
        return pl.BlockSpec((WINDOW, w), lambda b, n: (b * nb + n, col))

    def prev(col):
        return pl.BlockSpec((WINDOW, LANES), lambda b, n: (b * nb + jnp.maximum(n - 1, 0), col))

    kcol, vcol = QKV_K // LANES, QKV_V // LANES
    return pl.pallas_call(
        body, name="attn_fwd", grid=(B, nb),
        in_specs=[pl.BlockSpec(memory_space=pltpu.SMEM), cur(0, A_WIDTH), cur(kcol, LANES), cur(vcol, LANES),
                  prev(kcol), prev(vcol)],
        out_specs=[cur(0, A_WIDTH), cur(0, LANES)],
        out_shape=[jax.ShapeDtypeStruct((T, A_WIDTH), F32), jax.ShapeDtypeStruct((T, LANES), F32)],
        compiler_params=_params("parallel", "parallel"),
    )(sinks, qkv, qkv, qkv, qkv, qkv)


def _attn_bwd(qkv, do, out, lse, sinks, B, S):
    T = B * S
    nb = S // WINDOW
    scale = A_HEAD_DIM ** -0.5

    def body(sink_ref, q_ref, kc_ref, vc_ref, kp_ref, vp_ref, do_ref, out_ref, lse_ref,
             dq_ref, dkv_ref, dsink_ref, carry_ref):
        b, n = pl.program_id(0), pl.program_id(1)
        active = n < nb
        nq = jnp.minimum(n, nb - 1)
        valid = _band_mask(nq)

        @pl.when((b == 0) & (n == 0))
        def _():
            dsink_ref[...] = jnp.zeros_like(dsink_ref)

        k = jnp.concatenate([kp_ref[...], kc_ref[...]], axis=0)
        v = jnp.concatenate([vp_ref[...], vc_ref[...]], axis=0)
        top = _row_iota((2 * WINDOW, 1)) < WINDOW
        lane = _lane_iota((WINDOW, LANES))
        lo = lane < A_HEAD_DIM
        lane2 = _lane_iota((2 * WINDOW, LANES))
        lse_tile = lse_ref[...]
        dk_tot = jnp.zeros((2 * WINDOW, LANES), F32)
        dv_tot = jnp.zeros((2 * WINDOW, LANES), F32)
        dsink_row = jnp.zeros((1, LANES), F32)
        for g in range(A_KV_HEADS):
            kd, vd = _dup_kv_head(k, g), _dup_kv_head(v, g)
            dk_acc = jnp.zeros((2 * WINDOW, LANES), F32)
            dv_acc = jnp.zeros((2 * WINDOW, LANES), F32)
            for p in (2 * g, 2 * g + 1):
                sl = slice(p * LANES, (p + 1) * LANES)
                qs = _stack_heads(q_ref[:, sl])
                dos = _stack_heads(do_ref[:, sl])
                s = jnp.where(valid, _dot_nt(qs, kd) * scale, NEG_BIG)
                lse0 = jnp.sum(jnp.where(lane == 2 * p, lse_tile, 0.0), axis=-1, keepdims=True)
                lse1 = jnp.sum(jnp.where(lane == 2 * p + 1, lse_tile, 0.0), axis=-1, keepdims=True)
                lse_col = jnp.concatenate([lse0, lse1], axis=0)
                prob = jnp.exp(s - lse_col)
                prod = do_ref[:, sl].astype(F32) * out_ref[:, sl]
                d0 = jnp.sum(jnp.where(lo, prod, 0.0), axis=-1, keepdims=True)
                d1 = jnp.sum(jnp.where(lo, 0.0, prod), axis=-1, keepdims=True)
                delta = jnp.concatenate([d0, d1], axis=0)
                dp = _dot_nt(dos, vd)
                ds = (prob * (dp - delta) * scale).astype(BF16)
                dq = _dot(ds, kd)
                dq_ref[:, sl] = jnp.where(lo, dq[:WINDOW], dq[WINDOW:]).astype(BF16)
                dk_acc += _dot_tn(ds, qs)
                dv_acc += _dot_tn(prob.astype(BF16), dos)
                sink = jnp.where(top, sink_ref[2 * p], sink_ref[2 * p + 1])
                w = -jnp.exp(sink - lse_col) * delta
                w0 = jnp.sum(w[:WINDOW], axis=0, keepdims=True)
                w1 = jnp.sum(w[WINDOW:], axis=0, keepdims=True)
                lane1 = _lane_iota((1, LANES))
                dsink_row += jnp.where(lane1 == 2 * p, w0, 0.0) + jnp.where(lane1 == 2 * p + 1, w1, 0.0)
            mine = (lane2 < A_HEAD_DIM) == (g == 0)
            dk_tot = jnp.where(mine, dk_acc + pltpu.roll(dk_acc, A_HEAD_DIM, 1), dk_tot)
            dv_tot = jnp.where(mine, dv_acc + pltpu.roll(dv_acc, A_HEAD_DIM, 1), dv_tot)
        gate = jnp.where(active, 1.0, 0.0)
        dsink_ref[0:1, :] += dsink_row * gate
        dkv_ref[:, 0:LANES] = (carry_ref[:, 0:LANES] + dk_tot[:WINDOW] * gate).astype(BF16)
        dkv_ref[:, LANES:] = (carry_ref[:, LANES:] + dv_tot[:WINDOW] * gate).astype(BF16)
        carry_ref[:, 0:LANES] = dk_tot[WINDOW:]
        carry_ref[:, LANES:] = dv_tot[WINDOW:]

    def cur(col, w):
        return pl.BlockSpec((WINDOW, w), lambda b, n: (b * nb + jnp.minimum(n, nb - 1), col))

    def prev(col):
        return pl.BlockSpec((WINDOW, LANES), lambda b, n: (b * nb + jnp.maximum(jnp.minimum(n, nb - 1) - 1, 0), col))

    lag = pl.BlockSpec((WINDOW, 2 * LANES), lambda b, n: (b * nb + jnp.maximum(n - 1, 0), 0))
    kcol, vcol = QKV_K // LANES, QKV_V // LANES
    return pl.pallas_call(
        body, name="attn_bwd", grid=(B, nb + 1),
        in_specs=[pl.BlockSpec(memory_space=pltpu.SMEM), cur(0, A_WIDTH), cur(kcol, LANES), cur(vcol, LANES),
                  prev(kcol), prev(vcol), cur(0, A_WIDTH), cur(0, A_WIDTH), cur(0, LANES)],
        out_specs=[cur(0, A_WIDTH), lag, pl.BlockSpec((8, LANES), lambda b, n: (0, 0))],
        out_shape=[jax.ShapeDtypeStruct((T, A_WIDTH), BF16), jax.ShapeDtypeStruct((T, 2 * LANES), BF16),
                   jax.ShapeDtypeStruct((8, LANES), F32)],
        scratch_shapes=[pltpu.VMEM((WINDOW, 2 * LANES), F32)],
        compiler_params=_params("arbitrary", "arbitrary"),
    )(sinks, qkv, qkv, qkv, qkv, qkv, do, out, lse)


GLA_TILE = 256
CHUNKS_PER_TILE = GLA_TILE // B_CHUNK


def _gla_factors(q_ref, k_ref, cum_ref):
    scale = B_KEY_DIM ** -0.5
    cum = cum_ref[...]
    shape = (B_CHUNK, B_KEY_WIDTH)
    last = jnp.concatenate([jnp.broadcast_to(cum_ref[pl.ds(c * B_CHUNK + B_CHUNK - 1, 1), :], shape)
                            for c in range(CHUNKS_PER_TILE)], axis=0)
    mid = jnp.concatenate([jnp.broadcast_to(cum_ref[pl.ds(c * B_CHUNK + B_CHUNK // 2 - 1, 1), :], shape)
                           for c in range(CHUNKS_PER_TILE)], axis=0)
    e_qm, e_km, e_qe, e_kd = jnp.exp(cum - mid), jnp.exp(mid - cum), jnp.exp(cum), jnp.exp(last - cum)
    qs = q_ref[...] * scale
    k = k_ref[...]
    return qs, k, (e_qm, e_km, e_qe, e_kd)


def _head_mask(shape, h):
    return (_lane_iota(shape) // B_KEY_DIM) == h


def _stack_masked(t):
    return jnp.concatenate([jnp.where(_head_mask(t.shape, h), t, 0.0) for h in range(B_HEADS)], axis=0).astype(BF16)


def _select_heads(t):
    shape = (B_CHUNK, B_KEY_WIDTH)
    out = jnp.zeros(shape, F32)
    for h in range(B_HEADS):
        out = jnp.where(_head_mask(shape, h), t[h * B_CHUNK:(h + 1) * B_CHUNK], out)
    return out


def _select_state(t):
    shape = (B_VAL_DIM, B_KEY_WIDTH)
    out = jnp.zeros(shape, F32)
    for h in range(B_HEADS):
        out = jnp.where(_head_mask(shape, h), t[h * B_VAL_DIM:(h + 1) * B_VAL_DIM], out)
    return out


def _rows_by_head(t):
    return jnp.concatenate([t[:, h * B_VAL_DIM:(h + 1) * B_VAL_DIM] for h in range(B_HEADS)], axis=0)


def _intra_mask():
    i, j = _row_iota((GLA_TILE, GLA_TILE)), _lane_iota((GLA_TILE, GLA_TILE))
    return (i // B_CHUNK == j // B_CHUNK) & (j <= i)


def _pair_stack(t, p):
    slab = t[:, p * LANES:(p + 1) * LANES]
    lo = _lane_iota(slab.shape) < B_KEY_DIM
    return jnp.concatenate([jnp.where(lo, slab, 0.0), jnp.where(lo, 0.0, slab)], axis=0).astype(BF16)


def _gla_fwd(q, k, cum, vb, B, S):
    T = B * S
    nt = S // GLA_TILE

    def body(q_ref, k_ref, cum_ref, v_ref, o_ref, st_all_ref, st_ref):
        @pl.when(pl.program_id(1) == 0)
        def _():
            st_ref[...] = jnp.zeros_like(st_ref)

        qs, kk, (e_qm, e_km, e_qe, e_kd) = _gla_factors(q_ref, k_ref, cum_ref)
        qm, km, qe, kd = qs * e_qm, kk * e_km, qs * e_qe, (kk * e_kd).astype(BF16)
        mask = _intra_mask()
        intra = []
        for p in range(B_HEADS // 2):
            a = _dot_nt(_pair_stack(qm, p), km[:, p * LANES:(p + 1) * LANES].astype(BF16))
            for hh in range(2):
                h = 2 * p + hh
                att = jnp.where(mask, a[hh * GLA_TILE:(hh + 1) * GLA_TILE], 0.0).astype(BF16)
                intra.append(_dot(att, v_ref[:, h * B_VAL_DIM:(h + 1) * B_VAL_DIM]))
        inter = []
        for c in range(CHUNKS_PER_TILE):
            rows = slice(c * B_CHUNK, (c + 1) * B_CHUNK)
            st = st_ref[...]
            st_all_ref[c] = st
            inter.append(_dot_nt(_stack_masked(qe[rows]), st.astype(BF16)))
            inc = _select_state(_dot_tn(v_ref[rows, :], kd[rows]))
            decay = jnp.exp(cum_ref[pl.ds(c * B_CHUNK + B_CHUNK - 1, 1), :])
            st_ref[...] = st * decay + inc
        for h in range(B_HEADS):
            oi = jnp.concatenate([inter[c][h * B_CHUNK:(h + 1) * B_CHUNK] for c in range(CHUNKS_PER_TILE)], axis=0)
            o_ref[:, h * B_VAL_DIM:(h + 1) * B_VAL_DIM] = intra[h] + oi

    def rows(w):
        return pl.BlockSpec((GLA_TILE, w), lambda b, t: (b * nt + t, 0))

    return pl.pallas_call(
        body, name="gla_fwd", grid=(B, nt),
        in_specs=[rows(B_KEY_WIDTH), rows(B_KEY_WIDTH), rows(B_KEY_WIDTH), rows(B_WIDTH)],
        out_specs=[rows(B_WIDTH),
                   pl.BlockSpec((CHUNKS_PER_TILE, B_VAL_DIM, B_KEY_WIDTH), lambda b, t: (b * nt + t, 0, 0))],
        out_shape=[jax.ShapeDtypeStruct((T, B_WIDTH), F32),
                   jax.ShapeDtypeStruct((T // B_CHUNK, B_VAL_DIM, B_KEY_WIDTH), F32)],
        scratch_shapes=[pltpu.VMEM((B_VAL_DIM, B_KEY_WIDTH), F32)],
        compiler_params=_params("arbitrary", "arbitrary"),
    )(q, k, cum, vb)


def _gla_bwd(q, k, cum, vb, do, st_all, B, S):
    T = B * S
    nt = S // GLA_TILE
    scale = B_KEY_DIM ** -0.5

    def body(q_ref, k_ref, cum_ref, v_ref, do_ref, st_all_ref, dq_ref, dk_ref, dv_ref, dla_ref, dst_ref):
        @pl.when(pl.program_id(1) == 0)
        def _():
            dst_ref[...] = jnp.zeros_like(dst_ref)

        qs, kk, (e_qm, e_km, e_qe, e_kd) = _gla_factors(q_ref, k_ref, cum_ref)
        qm, km, qe, kd = qs * e_qm, kk * e_km, qs * e_qe, kk * e_kd
        mask = _intra_mask()
        dqm_slabs, dkm_slabs, dv_intra = [], [], []
        for p in range(B_HEADS // 2):
            qm_st = _pair_stack(qm, p)
            km_p = km[:, p * LANES:(p + 1) * LANES].astype(BF16)
            a = _dot_nt(qm_st, km_p)
            da_blocks, dqm_h = [], []
            for hh in range(2):
                h = 2 * p + hh
                vs = slice(h * B_VAL_DIM, (h + 1) * B_VAL_DIM)
                att = jnp.where(mask, a[hh * GLA_TILE:(hh + 1) * GLA_TILE], 0.0).astype(BF16)
                dv_intra.append(_dot_tn(att, do_ref[:, vs]))
                da = jnp.where(mask, _dot_nt(do_ref[:, vs], v_ref[:, vs]), 0.0).astype(BF16)
                da_blocks.append(da)
                dqm_h.append(_dot(da, km_p))
            lo = _lane_iota((GLA_TILE, LANES)) < B_KEY_DIM
            dqm_slabs.append(jnp.where(lo, dqm_h[0], dqm_h[1]))
            dkm_slabs.append(_dot_tn(jnp.concatenate(da_blocks, axis=0), qm_st))
        dqm = jnp.concatenate(dqm_slabs, axis=1)
        dkm = jnp.concatenate(dkm_slabs, axis=1)

        dqe_c, dkd_c, dv_inter, tail_c = ([None] * CHUNKS_PER_TILE for _ in range(4))
        for c in reversed(range(CHUNKS_PER_TILE)):
            rows = slice(c * B_CHUNK, (c + 1) * B_CHUNK)
            dst = dst_ref[...]
            dst_b = dst.astype(BF16)
            dv_inter[c] = _dot_nt(_stack_masked(kd[rows]), dst_b)
            dkd_c[c] = _select_heads(_dot(_rows_by_head(v_ref[rows, :]), dst_b))
            do_c = do_ref[rows, :]
            dqe_c[c] = _select_heads(_dot(_rows_by_head(do_c), st_all_ref[c].astype(BF16)))
            contrib = _select_state(_dot_tn(do_c, qe[rows].astype(BF16)))
            decay = jnp.exp(cum_ref[pl.ds(c * B_CHUNK + B_CHUNK - 1, 1), :])
            tail = (jnp.sum(kk[rows] * dkd_c[c] * e_kd[rows], axis=0, keepdims=True)
                    + decay * jnp.sum(st_all_ref[c] * dst, axis=0, keepdims=True))
            tail_c[c] = jnp.broadcast_to(tail, (B_CHUNK, B_KEY_WIDTH))
            dst_ref[...] = dst * decay + contrib
        dqe = jnp.concatenate(dqe_c, axis=0)
        dkd = jnp.concatenate(dkd_c, axis=0)
        dqs = dqm * e_qm + dqe * e_qe
        dk = dkm * e_km + dkd * e_kd
        dq_ref[...] = (dqs * scale).astype(BF16)
        dk_ref[...] = dk.astype(BF16)
        for h in range(B_HEADS):
            dvi = jnp.concatenate([dv_inter[c][h * B_CHUNK:(h + 1) * B_CHUNK] for c in range(CHUNKS_PER_TILE)], axis=0)
            dv_ref[:, h * B_VAL_DIM:(h + 1) * B_VAL_DIM] = (dv_intra[h] + dvi).astype(BF16)
        dd = qs * dqs - kk * dk
        i, j = _row_iota((GLA_TILE, GLA_TILE)), _lane_iota((GLA_TILE, GLA_TILE))
        upper = ((i // B_CHUNK == j // B_CHUNK) & (j >= i)).astype(BF16)
        hi, mid, lo3 = _split3(dd)
        dla_ref[...] = _dot(upper, hi) + _dot(upper, mid) + _dot(upper, lo3) + jnp.concatenate(tail_c, axis=0)

    def rows(w):
        return pl.BlockSpec((GLA_TILE, w), lambda b, t: (b * nt + nt - 1 - t, 0))

    return pl.pallas_call(
        body, name="gla_bwd", grid=(B, nt),
        in_specs=[rows(B_KEY_WIDTH), rows(B_KEY_WIDTH), rows(B_KEY_WIDTH), rows(B_WIDTH), rows(B_WIDTH),
                  pl.BlockSpec((CHUNKS_PER_TILE, B_VAL_DIM, B_KEY_WIDTH), lambda b, t: (b * nt + nt - 1 - t, 0, 0))],
        out_specs=[rows(B_KEY_WIDTH), rows(B_KEY_WIDTH), rows(B_WIDTH), rows(B_KEY_WIDTH)],
        out_shape=[jax.ShapeDtypeStruct((T, B_KEY_WIDTH), BF16), jax.ShapeDtypeStruct((T, B_KEY_WIDTH), BF16),
                   jax.ShapeDtypeStruct((T, B_WIDTH), BF16), jax.ShapeDtypeStruct((T, B_KEY_WIDTH), F32)],
        scratch_shapes=[pltpu.VMEM((B_VAL_DIM, B_KEY_WIDTH), F32)],
        compiler_params=_params("arbitrary", "arbitrary"),
    )(q, k, cum, vb, do, st_all)


def _merge(x2, tgt2, attn, za, o_gla, zb, ga, gb, w_oa, w_ob, w_o, g_gla, g_final):
    T = x2.shape[0]
    tm = 256
    last = T // tm - 1

    def body(x_ref, tgt_ref, attn_ref, za_ref, og_ref, zb_ref, ga_ref, gb_ref,
             woa_ref, wob_ref, wo_ref, gg_ref, gf_ref,
             dxres_ref, dattn_ref, dog_ref, dza_ref, dzb_ref, dga_ref, dgb_ref,
             dwo_ref, dwoa_ref, dwob_ref, small_ref, loss_ref,
             awo_ref, awoa_ref, awob_ref, agf_ref, agg_ref):
        @pl.when(pl.program_id(0) == 0)
        def _():
            for r in (awo_ref, awoa_ref, awob_ref, agf_ref, agg_ref, loss_ref):
                r[...] = jnp.zeros_like(r)

        za_v = za_ref[...]
        sig_za = _sigmoid(za_v)
        silu_a = za_v * sig_za
        attn_v = attn_ref[...]
        oa = (attn_v * silu_a).astype(BF16)
        ya = _dot(oa, woa_ref[...])
        og = og_ref[...]
        zb_v = zb_ref[...]
        sig_zb = _sigmoid(zb_v)
        silu_b = zb_v * sig_zb
        gg = gg_ref[...]
        on_parts, rinv_parts = [], []
        for h in range(B_HEADS):
            seg = og[:, h * B_VAL_DIM:(h + 1) * B_VAL_DIM]
            rinv = lax.rsqrt(jnp.mean(seg * seg, axis=-1, keepdims=True) + NORM_EPS)
            rinv_parts.append(rinv)
            on_parts.append(seg * rinv)
        on = jnp.concatenate(on_parts, axis=1)
        obn = on * gg
        ob = (obn * silu_b).astype(BF16)
        yb = _dot(ob, wob_ref[...])
        sig_a, sig_b = _sigmoid(ga_ref[...]), _sigmoid(gb_ref[...])
        merged = (sig_a * ya + sig_b * yb).astype(BF16)
        out = x_ref[...] + _dot(merged, wo_ref[...])
        rf = lax.rsqrt(jnp.mean(out * out, axis=-1, keepdims=True) + NORM_EPS)
        nrm = out * rf
        gf = gf_ref[...]
        err = nrm * gf - tgt_ref[...]
        loss_ref[...] += jnp.sum(err * err) * (0.5 / D_MODEL)

        dy = err * (1.0 / D_MODEL)
        agf_ref[...] += jnp.sum(dy * nrm, axis=0, keepdims=True)
        dn = dy * gf
        dout = rf * (dn - nrm * jnp.mean(dn * nrm, axis=-1, keepdims=True))
        dxres_ref[...] = dout
        dout_b = dout.astype(BF16)
        dmerged = _dot_nt(dout_b, wo_ref[...])
        awo_ref[...] += _dot_tn(merged, dout_b)
        dya = dmerged * sig_a
        dyb = dmerged * sig_b
        dga_ref[...] = (dmerged * ya * sig_a * (1.0 - sig_a)).astype(BF16)
        dgb_ref[...] = (dmerged * yb * sig_b * (1.0 - sig_b)).astype(BF16)
        dya_b, dyb_b = dya.astype(BF16), dyb.astype(BF16)
        awoa_ref[...] += _dot_tn(oa, dya_b)
        awob_ref[...] += _dot_tn(ob, dyb_b)
        doa = _dot_nt(dya_b, woa_ref[...])
        dattn_ref[...] = (doa * silu_a).astype(BF16)
        dza_ref[...] = (doa * attn_v * (sig_za * (1.0 + za_v * (1.0 - sig_za)))).astype(BF16)
        dob = _dot_nt(dyb_b, wob_ref[...])
        dzb_ref[...] = (dob * obn * (sig_zb * (1.0 + zb_v * (1.0 - sig_zb)))).astype(BF16)
        dobn = dob * silu_b
        agg_ref[...] += jnp.sum(dobn * on, axis=0, keepdims=True)
        don = dobn * gg
        for h in range(B_HEADS):
            sl = slice(h * B_VAL_DIM, (h + 1) * B_VAL_DIM)
            don_h, on_h = don[:, sl], on[:, sl]
            dog_ref[:, sl] = (rinv_parts[h] * (don_h - on_h * jnp.mean(don_h * on_h, axis=-1, keepdims=True))).astype(BF16)

        @pl.when(pl.program_id(0) == last)
        def _():
            for j in range(N_DEV):
                dwo_ref[j] = awo_ref[j * SHARD_OUT:(j + 1) * SHARD_OUT, :].astype(BF16)
                dwoa_ref[j] = awoa_ref[:, j * SHARD_OUT:(j + 1) * SHARD_OUT].astype(BF16)
                dwob_ref[j] = awob_ref[:, j * SHARD_OUT:(j + 1) * SHARD_OUT].astype(BF16)
            small_ref[...] = jnp.zeros_like(small_ref)
            _put_rows(small_ref, SMALL_G_FINAL, agf_ref[...])
            _put_rows(small_ref, SMALL_G_GLA, agg_ref[...])

    def rows(w):
        return pl.BlockSpec((tm, w), lambda i: (i, 0))

    def whole(shape):
        nd = len(shape)
        return pl.BlockSpec(shape, lambda i: (0,) * nd)

    outs = [((T, D_MODEL), F32, rows(D_MODEL)), ((T, A_WIDTH), BF16, rows(A_WIDTH)), ((T, B_WIDTH), BF16, rows(B_WIDTH)),
            ((T, A_WIDTH), BF16, rows(A_WIDTH)), ((T, B_WIDTH), BF16, rows(B_WIDTH)),
            ((T, D_MODEL), BF16, rows(D_MODEL)), ((T, D_MODEL), BF16, rows(D_MODEL)),
            ((N_DEV, SHARD_OUT, D_MODEL), BF16, whole((N_DEV, SHARD_OUT, D_MODEL))),
            ((N_DEV, A_WIDTH, SHARD_OUT), BF16, whole((N_DEV, A_WIDTH, SHARD_OUT))),
            ((N_DEV, B_WIDTH, SHARD_OUT), BF16, whole((N_DEV, B_WIDTH, SHARD_OUT))),
            ((SMALL_SINKS, LANES), F32, whole((SMALL_SINKS, LANES))), ((8, LANES), F32, whole((8, LANES)))]
    return pl.pallas_call(
        body, name="merge", grid=(T // tm,),
        in_specs=[rows(D_MODEL), rows(D_MODEL), rows(A_WIDTH), rows(A_WIDTH), rows(B_WIDTH), rows(B_WIDTH),
                  rows(D_MODEL), rows(D_MODEL),
                  _const_spec((A_WIDTH, D_MODEL)), _const_spec((B_WIDTH, D_MODEL)), _const_spec((D_MODEL, D_MODEL)),
                  _const_spec((1, B_WIDTH)), _const_spec((1, D_MODEL))],
        out_specs=[o[2] for o in outs],
        out_shape=[jax.ShapeDtypeStruct(o[0], o[1]) for o in outs],
        scratch_shapes=[pltpu.VMEM((D_MODEL, D_MODEL), F32), pltpu.VMEM((A_WIDTH, D_MODEL), F32),
                        pltpu.VMEM((B_WIDTH, D_MODEL), F32), pltpu.VMEM((1, D_MODEL), F32), pltpu.VMEM((1, B_WIDTH), F32)],
        compiler_params=_params("arbitrary"),
    )(x2, tgt2, attn, za, o_gla, zb, ga, gb, w_oa, w_ob, w_o, g_gla, g_final)


def _in_proj_bwd(x2, dxres, cosf, sinf, g_in, w_pad, wa_pad, parts):
    T = x2.shape[0]
    tm = 256
    last = T // tm - 1
    base = SMALL_G_IN

    def body(x_ref, dxres_ref, cos_ref, sin_ref, g_ref, w_ref, wa_ref,
             dq_ref, dkv_ref, dza_ref, dqb_ref, dkb_ref, dvb_ref, dzb_ref, dla_ref, u_ref, alr_ref, dga_ref, dgb_ref,
             dx_ref, dproj_ref, small_ref, agin_ref, aba_ref, awa_ref):
        @pl.when(pl.program_id(0) == 0)
        def _():
            for r in (agin_ref, aba_ref, awa_ref):
                r[...] = jnp.zeros_like(r)

        cos, nsin = cos_ref[...], -sin_ref[...]
        for s in range(A_WIDTH // LANES):
            sl = slice(s * LANES, (s + 1) * LANES)
            dproj_ref[:, sl] = _rope_slab(dq_ref[:, sl].astype(F32), cos, nsin).astype(BF16)
        dproj_ref[:, QKV_K:QKV_V] = _rope_slab(dkv_ref[:, 0:LANES].astype(F32), cos, nsin).astype(BF16)
        dproj_ref[:, QKV_V:QKV_W] = dkv_ref[:, LANES:]

        def put(name, val):
            a, b = SEG[name]
            dproj_ref[:, a:b] = val

        put("za", dza_ref[...])
        put("qb", dqb_ref[...])
        put("kb", dkb_ref[...])
        put("vb", dvb_ref[...])
        put("zb", dzb_ref[...])
        put("ga", dga_ref[...])
        put("gb", dgb_ref[...])
        du = dla_ref[...] * (1.0 / B_GATE_TEMP) * _sigmoid(-u_ref[...])
        aba_ref[...] += jnp.sum(du, axis=0, keepdims=True)
        du_b = du.astype(BF16)
        awa_ref[...] += _dot_tn(alr_ref[...], du_b)
        put("alr", _dot_nt(du_b, wa_ref[...]).astype(BF16))

        dh = _dot_nt(dproj_ref[...], w_ref[...])
        x = x_ref[...]
        r = lax.rsqrt(jnp.mean(x * x, axis=-1, keepdims=True) + NORM_EPS)
        nrm = x * r
        agin_ref[...] += jnp.sum(dh * nrm, axis=0, keepdims=True)
        dn = dh * g_ref[...]
        dx_ref[...] = dxres_ref[...] + r * (dn - nrm * jnp.mean(dn * nrm, axis=-1, keepdims=True))

        @pl.when(pl.program_id(0) == last)
        def _():
            small_ref[...] = jnp.zeros_like(small_ref)
            _put_rows(small_ref, SMALL_G_IN - base, agin_ref[...])
            _put_rows(small_ref, SMALL_B_ALPHA - base, aba_ref[...])
            for half in range(B_KEY_WIDTH // LANES):
                r0 = SMALL_W_ALPHA - base + half * B_GATE_RANK
                small_ref[r0:r0 + B_GATE_RANK, :] = awa_ref[0:B_GATE_RANK, half * LANES:(half + 1) * LANES]

    def rows(w):
        return pl.BlockSpec((tm, w), lambda i: (i, 0))

    names = ["dq", "dkv", "dza", "dqb", "dkb", "dvb", "dzb", "dla", "u", "alr", "dga", "dgb"]
    return pl.pallas_call(
        body, name="in_proj_bwd", grid=(T // tm,),
        in_specs=[rows(D_MODEL), rows(D_MODEL), rows(LANES), rows(LANES), _const_spec((1, D_MODEL)),
                  _const_spec((D_MODEL, D_IN_PAD)), _const_spec((RANK_PAD, B_KEY_WIDTH))]
                 + [rows(parts[n].shape[1]) for n in names],
        out_specs=[rows(D_MODEL), rows(D_IN_PAD), pl.BlockSpec((SMALL_ROWS - base, LANES), lambda i: (0, 0))],
        out_shape=[jax.ShapeDtypeStruct((T, D_MODEL), F32), jax.ShapeDtypeStruct((T, D_IN_PAD), BF16),
                   jax.ShapeDtypeStruct((SMALL_ROWS - base, LANES), F32)],
        scratch_shapes=[pltpu.VMEM((1, D_MODEL), F32), pltpu.VMEM((1, B_KEY_WIDTH), F32),
                        pltpu.VMEM((RANK_PAD, B_KEY_WIDTH), F32)],
        compiler_params=_params("arbitrary"),
    )(x2, dxres, cosf, sinf, g_in, w_pad, wa_pad, *[parts[n] for n in names])


def _w_in_grad(h, dproj):
    T = h.shape[0]
    tn, tk = D_IN_PAD // 3, math.gcd(T, 512)
    nk = T // tk

    def body(a_ref, b_ref, o_ref, acc_ref):
        @pl.when(pl.program_id(1) == 0)
        def _():
            acc_ref[...] = jnp.zeros_like(acc_ref)

        acc_ref[...] += _dot_tn(a_ref[...], b_ref[...])

        @pl.when(pl.program_id(1) == nk - 1)
        def _():
            o_ref[...] = acc_ref[...].astype(BF16)

    return pl.pallas_call(
        body, name="w_in_grad", grid=(D_IN_PAD // tn, nk),
        in_specs=[pl.BlockSpec((tk, D_MODEL), lambda j, kk: (kk, 0)), pl.BlockSpec((tk, tn), lambda j, kk: (kk, j))],
        out_specs=pl.BlockSpec((D_MODEL, tn), lambda j, kk: (0, j)),
        out_shape=jax.ShapeDtypeStruct((D_MODEL, D_IN_PAD), BF16),
        scratch_shapes=[pltpu.VMEM((D_MODEL, tn), F32)],
        compiler_params=_params("parallel", "arbitrary"),
    )(h, dproj)


def _my_place():
    return lax.axis_index("x"), lax.axis_index("y"), lax.axis_index("c")


def _any_specs(n):
    return [pl.BlockSpec(memory_space=pl.ANY)] * n


def _all_gather(shards):
    n = len(shards)

    def body(*refs):
        ins, outs = refs[:n], refs[n:2 * n]
        send_sems, recv_sems, local_sems = refs[2 * n:]
        x, y, c = _my_place()
        me, sibling = (x, y, c), (x, y, 1 - c)
        chips = [(1 - x, y), (x, 1 - y), (1 - x, 1 - y)]

        def block(a, px, py, pc):
            return outs[a].at[4 * px + 2 * py + pc]

        def copy(a, k, blk, to, src=None):
            return pltpu.make_async_remote_copy(
                src_ref=block(a, *blk) if src is None else src, dst_ref=block(a, *blk),
                send_sem=send_sems.at[a, k], recv_sem=recv_sems.at[a, k], device_id=to, device_id_type=MESH)

        mine = [pltpu.make_async_copy(ins[a], block(a, *me), local_sems.at[a]) for a in range(n)]
        for cp in mine:
            cp.start()
        first = []
        for a in range(n):
            first.append(copy(a, 0, me, sibling, src=ins[a]))
            first += [copy(a, 1 + j, me, (*chip, c), src=ins[a]) for j, chip in enumerate(chips)]
        for cp in first:
            cp.start()
        passed = []
        for j, chip in enumerate(chips):
            for a in range(n):
                copy(a, 1 + j, (*chip, c), me).wait_recv()
                fwd = copy(a, 4 + j, (*chip, c), sibling)
                fwd.start()
                passed.append(fwd)
        for a in range(n):
            copy(a, 0, sibling, me).wait_recv()
            for j, chip in enumerate(chips):
                copy(a, 4 + j, (*chip, 1 - c), me).wait_recv()
        for cp in first + passed:
            cp.wait_send()
        for cp in mine:
            cp.wait()

    return pl.pallas_call(
        body, name="gather_weights",
        in_specs=_any_specs(n), out_specs=_any_specs(n),
        out_shape=[jax.ShapeDtypeStruct((N_DEV, *s.shape), s.dtype) for s in shards],
        scratch_shapes=[pltpu.SemaphoreType.DMA((n, 7)), pltpu.SemaphoreType.DMA((n, 7)), pltpu.SemaphoreType.DMA((n,))],
    )(*shards)


def _exchange_sibling(grads):
    n = len(grads)

    def body(*refs):
        ins, outs = refs[:n], refs[n:2 * n]
        send_sems, recv_sems = refs[2 * n:]
        x, y, c = _my_place()
        copies = [pltpu.make_async_remote_copy(
            src_ref=ins[a].at[j, 1 - c], dst_ref=outs[a].at[j], send_sem=send_sems.at[a, j], recv_sem=recv_sems.at[a, j],
            device_id=(x, y, 1 - c), device_id_type=MESH) for a in range(n) for j in range(N_CHIPS)]
        for cp in copies:
            cp.start()
        for cp in copies:
            cp.wait()

    return pl.pallas_call(
        body, name="grad_exchange_sibling",
        in_specs=_any_specs(n), out_specs=_any_specs(n),
        out_shape=[jax.ShapeDtypeStruct((N_CHIPS, *g.shape[2:]), g.dtype) for g in grads],
        scratch_shapes=[pltpu.SemaphoreType.DMA((n, N_CHIPS)), pltpu.SemaphoreType.DMA((n, N_CHIPS))],
    )(*grads)


def _add_sibling(grads, recvd, c_idx):
    n = len(grads)

    def body(c_ref, *refs):
        for a in range(n):
            refs[2 * n + a][...] = (refs[a][...].astype(F32) + refs[n + a][...].astype(F32)).astype(BF16)

    def mine_spec(g):
        return pl.BlockSpec((None, None, *g.shape[2:]), lambda j, c_ref: (j, c_ref[0], 0, 0))

    def recv_spec(g):
        return pl.BlockSpec((None, *g.shape[2:]), lambda j, c_ref: (j, 0, 0))

    return pl.pallas_call(
        body, name="grad_add_sibling",
        grid_spec=pltpu.PrefetchScalarGridSpec(
            num_scalar_prefetch=1, grid=(N_CHIPS,),
            in_specs=[mine_spec(g) for g in grads] + [recv_spec(g) for g in grads],
            out_specs=[recv_spec(g) for g in grads]),
        out_shape=[jax.ShapeDtypeStruct((N_CHIPS, *g.shape[2:]), BF16) for g in grads],
        compiler_params=_params("parallel"),
    )(c_idx, *grads, *recvd)


def _exchange_chips(parts, small):
    n = len(parts)
    flips = [(dx, dy, dc) for dx in (0, 1) for dy in (0, 1) for dc in (0, 1)][1:]

    def body(*refs):
        ins, s_ref, outs, sall_ref = refs[:n], refs[n], refs[n + 1:2 * n + 1], refs[2 * n + 1]
        send_sems, recv_sems, ssend_sems, srecv_sems, local_sem = refs[2 * n + 2:]
        x, y, c = _my_place()
        my_dev = 4 * x + 2 * y + c
        keep_small = pltpu.make_async_copy(s_ref, sall_ref.at[my_dev], local_sem)
        keep_small.start()
        big = []
        for kk, (dx, dy) in enumerate([(1, 0), (0, 1), (1, 1)]):
            px, py = x ^ dx, y ^ dy
            for a in range(n):
                big.append(pltpu.make_async_remote_copy(
                    src_ref=ins[a].at[2 * px + py], dst_ref=outs[a].at[kk], send_sem=send_sems.at[a, kk],
                    recv_sem=recv_sems.at[a, kk], device_id=(px, py, c), device_id_type=MESH))

        def small_copy(kk, slot):
            dx, dy, dc = flips[kk]
            return pltpu.make_async_remote_copy(
                src_ref=s_ref, dst_ref=sall_ref.at[slot], send_sem=ssend_sems.at[kk], recv_sem=srecv_sems.at[kk],
                device_id=(x ^ dx, y ^ dy, c ^ dc), device_id_type=MESH)

        tiny = [small_copy(kk, my_dev) for kk in range(len(flips))]
        for cp in tiny + big:
            cp.start()
        for kk, (dx, dy, dc) in enumerate(flips):
            small_copy(kk, 4 * (x ^ dx) + 2 * (y ^ dy) + (c ^ dc)).wait_recv()
        for cp in big:
            cp.wait_recv()
        for cp in tiny + big:
            cp.wait_send()
        keep_small.wait()

    res = pl.pallas_call(
        body, name="grad_exchange_chips",
        in_specs=_any_specs(n + 1), out_specs=_any_specs(n + 1),
        out_shape=[jax.ShapeDtypeStruct((3, *p.shape[1:]), p.dtype) for p in parts]
                  + [jax.ShapeDtypeStruct((N_DEV, *small.shape), F32)],
        scratch_shapes=[pltpu.SemaphoreType.DMA((n, 3)), pltpu.SemaphoreType.DMA((n, 3)),
                        pltpu.SemaphoreType.DMA((7,)), pltpu.SemaphoreType.DMA((7,)), pltpu.SemaphoreType.DMA],
    )(*parts, small)
    return res[:n], res[n]


def _adam_math(w, g, m, v):
    m_new = ADAM_B1 * m + (1.0 - ADAM_B1) * g
    v_new = ADAM_B2 * v + (1.0 - ADAM_B2) * (g * g)
    m_hat = m_new / (1.0 - ADAM_B1 ** ADAM_STEP)
    v_hat = v_new / (1.0 - ADAM_B2 ** ADAM_STEP)
    delta = -ADAM_LR * (m_hat / (jnp.sqrt(v_hat) + ADAM_EPS) + ADAM_WD * w)
    return delta, m_new, v_new


def _adam_big(name, part, recv, chip_idx, w, m, v):
    rw, cw = w.shape
    tr = rw // 8

    def body(chip_ref, p_ref, r_ref, w_ref, m_ref, v_ref, g_ref, d_ref, mo_ref, vo_ref):
        g = ((p_ref[...].astype(F32) + r_ref[0].astype(F32)) + r_ref[1].astype(F32)) + r_ref[2].astype(F32)
        g_ref[...] = g
        d_ref[...], mo_ref[...], vo_ref[...] = _adam_math(w_ref[...], g, m_ref[...], v_ref[...])

    spec = pl.BlockSpec((tr, cw), lambda i, chip_ref: (i, 0))
    return pl.pallas_call(
        body, name=name,
        grid_spec=pltpu.PrefetchScalarGridSpec(
            num_scalar_prefetch=1, grid=(rw // tr,),
            in_specs=[pl.BlockSpec((None, tr, cw), lambda i, chip_ref: (chip_ref[0], i, 0)),
                      pl.BlockSpec((3, tr, cw), lambda i, chip_ref: (0, i, 0)), spec, spec, spec],
            out_specs=[spec] * 4),
        out_shape=[jax.ShapeDtypeStruct((rw, cw), F32)] * 4,
        compiler_params=_params("parallel"),
    )(chip_idx, part, recv, w, m, v)


def _adam_small(small_all, params):
    flat = [a for triple in params for a in triple]
    n_par = len(params)

    def body(s_ref, *refs):
        ins, outs = refs[:3 * n_par], refs[3 * n_par:]
        g_slab = s_ref[0]
        for dev in range(1, N_DEV):
            g_slab = g_slab + s_ref[dev]
        dev = 4 * lax.axis_index("x") + 2 * lax.axis_index("y") + lax.axis_index("c")
        alpha_full = jnp.concatenate([g_slab[SMALL_W_ALPHA + half * B_GATE_RANK:SMALL_W_ALPHA + (half + 1) * B_GATE_RANK]
                                      for half in range(B_KEY_WIDTH // LANES)], axis=1)
        alpha_mine = pltpu.roll(alpha_full, (B_KEY_WIDTH - dev * SHARD_ALPHA) % B_KEY_WIDTH, 1)[:, 0:SHARD_ALPHA]
        grads = [_take_rows(g_slab, SMALL_G_IN, D_MODEL // LANES), _take_rows(g_slab, SMALL_G_FINAL, D_MODEL // LANES),
                 _take_rows(g_slab, SMALL_G_GLA, B_WIDTH // LANES), _take_rows(g_slab, SMALL_B_ALPHA, B_KEY_WIDTH // LANES),
                 g_slab[SMALL_SINKS:SMALL_SINKS + 1, 0:A_HEADS], alpha_mine]
        for i, g in enumerate(grads):
            w_ref, m_ref, v_ref = ins[3 * i:3 * i + 3]
            delta, m_new, v_new = _adam_math(w_ref[...], g, m_ref[...], v_ref[...])
            outs[4 * i][...] = g
            outs[4 * i + 1][...] = delta
            outs[4 * i + 2][...] = m_new
            outs[4 * i + 3][...] = v_new

    res = pl.pallas_call(
        body, name="adam_small",
        out_shape=[jax.ShapeDtypeStruct(t[0].shape, F32) for t in params for _ in range(4)],
    )(small_all, *flat)
    return [res[4 * i:4 * i + 4] for i in range(n_par)]


def _local_step(x, positions, loss_target, g_in, w_pad, wa_pad, b_alpha, sinks, g_gla, w_oa, w_ob, w_o, g_final):
    B, S, _ = x.shape
    T = B * S
    x2 = x.reshape(T, D_MODEL)
    tgt2 = loss_target.reshape(T, D_MODEL)
    cosf, sinf = _rope_tables(positions.reshape(T, 1))
    f = _in_proj(x2, cosf, sinf, g_in, w_pad, wa_pad, b_alpha)
    attn, lse = _attn_fwd(f["qkv"], sinks, B, S)
    o_gla, st_all = _gla_fwd(f["q"], f["k"], f["cum"], f["vb"], B, S)
    (dxres, dattn, dog, dza, dzb, dga, dgb, dw_o, dw_oa, dw_ob, small_a, loss_acc) = _merge(
        x2, tgt2, attn, f["za"], o_gla, f["zb"], f["ga"], f["gb"], w_oa, w_ob, w_o, g_gla, g_final)
    dq, dkv, dsink = _attn_bwd(f["qkv"], dattn, attn, lse, sinks, B, S)
    dqb, dkb, dvb, dla = _gla_bwd(f["q"], f["k"], f["cum"], f["vb"], dog, st_all, B, S)
    parts = dict(dq=dq, dkv=dkv, dza=dza, dqb=dqb, dkb=dkb, dvb=dvb, dzb=dzb, dla=dla, u=f["u"], alr=f["alr"],
                 dga=dga, dgb=dgb)
    dx, dproj, small_c = _in_proj_bwd(x2, dxres, cosf, sinf, g_in, w_pad, wa_pad, parts)
    dw_in_pad = _w_in_grad(f["h"], dproj)
    small = jnp.concatenate([small_a, dsink, small_c], axis=0)
    return dict(loss=loss_acc[0, 0], grad_x=dx.reshape(B, S, D_MODEL), dw_in_pad=dw_in_pad, dw_oa=dw_oa, dw_ob=dw_ob,
                dw_o=dw_o, small=small)


def _ref_to_pad_col(c):
    return c if c < ALR_SRC + B_GATE_RANK else c + RANK_PAD - B_GATE_RANK


def _pad_from_shards(shards):
    cols = jnp.concatenate([shards[j] for j in range(N_DEV)], axis=1)
    z = jnp.zeros((cols.shape[0], RANK_PAD - B_GATE_RANK), cols.dtype)
    return jnp.concatenate([cols[:, :ALR_SRC + B_GATE_RANK], z, cols[:, ALR_SRC + B_GATE_RANK:]], axis=1)


def _shards_from_pad(w):
    cut = ALR_SRC + B_GATE_RANK
    blocks = []
    for j in range(N_DEV):
        a, b = j * SHARD_IN, (j + 1) * SHARD_IN
        if b <= cut or a >= cut:
            blocks.append(w[:, _ref_to_pad_col(a):_ref_to_pad_col(a) + SHARD_IN])
        else:
            blocks.append(jnp.concatenate([w[:, a:cut], w[:, cut + RANK_PAD - B_GATE_RANK:_ref_to_pad_col(b - 1) + 1]], axis=1))
    return jnp.stack(blocks, axis=0)


def kernel(x, positions, g_in, w_in, w_alpha_up, b_alpha, attn_sinks, g_gla_norm, w_out_a, w_out_b, w_o, g_final, loss_target, m_g_in, m_w_in, m_w_alpha_up, m_b_alpha, m_attn_sinks, m_g_gla_norm, m_w_out_a, m_w_out_b, m_w_o, m_g_final, v_g_in, v_w_in, v_w_alpha_up, v_b_alpha, v_attn_sinks, v_g_gla_norm, v_w_out_a, v_w_out_b, v_w_o, v_g_final):
    xi, yi, ci = _my_place()
    c_idx = ci.reshape(1).astype(jnp.int32)
    chip_idx = (2 * xi + yi).reshape(1).astype(jnp.int32)

    g_win, g_woa, g_wob, g_wo, g_wa = _all_gather(
        [w_in[0].astype(BF16), w_out_a[0].astype(BF16), w_out_b[0].astype(BF16), w_o[0].astype(BF16),
         w_alpha_up[0].astype(BF16)])
    w_pad = _pad_from_shards(g_win)
    w_oa_full = jnp.concatenate([g_woa[j] for j in range(N_DEV)], axis=1)
    w_ob_full = jnp.concatenate([g_wob[j] for j in range(N_DEV)], axis=1)
    w_o_full = g_wo.reshape(D_MODEL, D_MODEL)
    wa_pad = jnp.pad(jnp.concatenate([g_wa[j] for j in range(N_DEV)], axis=1), ((0, RANK_PAD - B_GATE_RANK), (0, 0)))

    r = _local_step(x, positions, loss_target, g_in, w_pad, wa_pad, b_alpha, attn_sinks[0], g_gla_norm, w_oa_full,
                    w_ob_full, w_o_full, g_final.reshape(1, D_MODEL))
    loss = lax.psum(r["loss"], ("x", "y", "c"))

    grads = [_shards_from_pad(r["dw_in_pad"]), r["dw_oa"], r["dw_ob"], r["dw_o"]]
    grads = [g.reshape(N_CHIPS, 2, *g.shape[1:]) for g in grads]
    from_sibling = _exchange_sibling(grads)
    chip_parts = _add_sibling(grads, from_sibling, c_idx)
    recv, small_all = _exchange_chips(chip_parts, r["small"])

    big = [_adam_big("adam_w_in", chip_parts[0], recv[0], chip_idx, w_in[0], m_w_in[0], v_w_in[0]),
           _adam_big("adam_w_out_a", chip_parts[1], recv[1], chip_idx, w_out_a[0], m_w_out_a[0], v_w_out_a[0]),
           _adam_big("adam_w_out_b", chip_parts[2], recv[2], chip_idx, w_out_b[0], m_w_out_b[0], v_w_out_b[0]),
           _adam_big("adam_w_o", chip_parts[3], recv[3], chip_idx, w_o[0], m_w_o[0], v_w_o[0])]
    row = lambda a: a.reshape(1, D_MODEL)
    s_in, s_final, s_gla, s_ba, s_sinks, s_wa = _adam_small(small_all, [
        (g_in, m_g_in, v_g_in), (row(g_final), row(m_g_final), row(v_g_final)),
        (g_gla_norm, m_g_gla_norm, v_g_gla_norm), (b_alpha, m_b_alpha, v_b_alpha),
        (attn_sinks, m_attn_sinks, v_attn_sinks), (w_alpha_up[0], m_w_alpha_up[0], v_w_alpha_up[0])])

    def group(i):
        return (s_in[i], big[0][i][None], s_wa[i][None], s_ba[i], s_sinks[i], s_gla[i], big[1][i][None], big[2][i][None],
                big[3][i][None], s_final[i].reshape(D_MODEL))

    return (loss, r["grad_x"], *group(0), *group(1), *group(2), *group(3))
```

```python
import functools
import math

import numpy as np
import jax
import jax.numpy as jnp
from jax import lax
from jax.experimental import pallas as pl
from jax.experimental.pallas import tpu as pltpu

F32 = jnp.float32
BF16 = jnp.bfloat16
MESH = pl.DeviceIdType.MESH

D_MODEL = 1024
A_HEADS, A_KV_HEADS, A_HEAD_DIM = 8, 2, 64
A_WIDTH, A_KV_WIDTH = 512, 128
WINDOW = 128
ROPE_THETA = 500000.0
ROPE_DIM = 16
B_HEADS, B_KEY_DIM, B_VAL_DIM = 4, 64, 128
B_KEY_WIDTH, B_WIDTH = 256, 512
B_GATE_RANK = 16
B_GATE_TEMP = 16.0
B_CHUNK = 64
NORM_EPS = 1e-6
NEG_BIG = -1e30
D_IN = 4880
N_DEV = 8
N_CHIPS = 4
ADAM_LR, ADAM_B1, ADAM_B2, ADAM_EPS, ADAM_WD, ADAM_STEP = 0.001, 0.9, 0.999, 1e-08, 0.01, 10

LANES = 128
V7X_VMEM_LIMIT = 56 * 1024 * 1024

RANK_PAD = LANES
SEG = {}
_off = 0
for _name, _w in (("qa", 512), ("ka", 128), ("va", 128), ("za", 512), ("qb", 256), ("kb", 256),
                  ("vb", 512), ("zb", 512), ("alr", RANK_PAD), ("ga", 1024), ("gb", 1024)):
    SEG[_name] = (_off, _off + _w)
    _off += _w
D_IN_PAD = _off
ALR_SRC = SEG["alr"][0]
QKV_K, QKV_V, QKV_W = SEG["ka"][0], SEG["va"][0], SEG["va"][1]

SHARD_IN = D_IN // N_DEV
SHARD_PAD = 640
SHARD_OUT = D_MODEL // N_DEV
SHARD_ALPHA = B_KEY_WIDTH // N_DEV

SMALL_G_FINAL, SMALL_G_GLA, SMALL_SINKS, SMALL_G_IN, SMALL_B_ALPHA, SMALL_W_ALPHA = 0, 8, 16, 24, 32, 40
SMALL_ROWS = 72


def _dot(a, b):
    return jnp.dot(a, b, preferred_element_type=F32)


def _dot_nt(a, b):
    return lax.dot_general(a, b, (((1,), (1,)), ((), ())), preferred_element_type=F32)


def _dot_tn(a, b):
    return lax.dot_general(a, b, (((0,), (0,)), ((), ())), preferred_element_type=F32)


def _sigmoid(z):
    return 1.0 / (1.0 + jnp.exp(-z))


def _params(*sem):
    return pltpu.CompilerParams(dimension_semantics=sem, vmem_limit_bytes=V7X_VMEM_LIMIT)


def _const_spec(shape):
    nd = len(shape)
    return pl.BlockSpec(shape, lambda *_: (0,) * nd, pipeline_mode=pl.Buffered(1))


def _lane_iota(shape):
    return lax.broadcasted_iota(jnp.int32, shape, 1)


def _row_iota(shape):
    return lax.broadcasted_iota(jnp.int32, shape, 0)


def _split3(v):
    hi = v.astype(BF16)
    r1 = v - hi.astype(F32)
    mid = r1.astype(BF16)
    lo = (r1 - mid.astype(F32)).astype(BF16)
    return hi, mid, lo


def _put_rows(ref, row0, vec):
    for r in range(vec.shape[1] // LANES):
        ref[row0 + r:row0 + r + 1, :] = vec[:, r * LANES:(r + 1) * LANES]


def _take_rows(slab, row0, n):
    return jnp.concatenate([slab[row0 + r:row0 + r + 1, :] for r in range(n)], axis=1)


def _rope_lane_constants():
    half = ROPE_DIM // 2
    inv_freq = np.exp(-math.log(ROPE_THETA) * np.arange(half, dtype=np.float32) * np.float32(2.0 / ROPE_DIM)).astype(np.float32)
    lane = np.arange(LANES)
    j = lane % A_HEAD_DIM
    invf = np.where(j < ROPE_DIM, inv_freq[j % half], 0.0).astype(np.float32)
    sign = np.where(j < half, -1.0, np.where(j < ROPE_DIM, 1.0, 0.0)).astype(np.float32)
    return jnp.asarray(invf)[None, :], jnp.asarray(sign)[None, :]


def _rope_tables(pos_col):
    T = pos_col.shape[0]
    tm = math.gcd(T, 1024)
    invf, sign = _rope_lane_constants()

    def body(pos_ref, invf_ref, sign_ref, cos_ref, sin_ref):
        ang = pos_ref[...].astype(F32) * invf_ref[...]
        cos_ref[...] = jnp.cos(ang)
        sin_ref[...] = jnp.sin(ang) * sign_ref[...]

    return pl.pallas_call(
        body, name="rope_tables", grid=(T // tm,),
        in_specs=[pl.BlockSpec((tm, 1), lambda i: (i, 0)), _const_spec((1, LANES)), _const_spec((1, LANES))],
        out_specs=[pl.BlockSpec((tm, LANES), lambda i: (i, 0))] * 2,
        out_shape=[jax.ShapeDtypeStruct((T, LANES), F32)] * 2,
        compiler_params=_params("parallel"),
    )(pos_col, invf, sign)


def _rope_slab(t, cos, sin_signed):
    first = (_lane_iota(t.shape) % A_HEAD_DIM) < (ROPE_DIM // 2)
    partner = jnp.where(first, pltpu.roll(t, LANES - ROPE_DIM // 2, 1), pltpu.roll(t, ROPE_DIM // 2, 1))
    return t * cos + partner * sin_signed


def _in_proj(x2, cosf, sinf, g_in, w_pad, wa_pad, b_alpha):
    T = x2.shape[0]
    tm = 256

    def body(x_ref, cos_ref, sin_ref, g_ref, w_ref, wa_ref, ba_ref,
             h_ref, qkv_ref, za_ref, q_ref, k_ref, vb_ref, zb_ref, alr_ref, u_ref, cum_ref, ga_ref, gb_ref):
        x = x_ref[...]
        r = lax.rsqrt(jnp.mean(x * x, axis=-1, keepdims=True) + NORM_EPS)
        h = (x * r * g_ref[...]).astype(BF16)
        h_ref[...] = h

        def seg(name):
            a, b = SEG[name]
            return _dot(h, w_ref[:, a:b])

        cos, sin = cos_ref[...], sin_ref[...]
        qa = seg("qa")
        for s in range(A_WIDTH // LANES):
            qkv_ref[:, s * LANES:(s + 1) * LANES] = _rope_slab(qa[:, s * LANES:(s + 1) * LANES], cos, sin).astype(BF16)
        qkv_ref[:, QKV_K:QKV_V] = _rope_slab(seg("ka"), cos, sin).astype(BF16)
        qkv_ref[:, QKV_V:QKV_W] = seg("va").astype(BF16)
        za_ref[...] = seg("za")
        q_ref[...] = seg("qb")
        k_ref[...] = seg("kb")
        vb_ref[...] = seg("vb").astype(BF16)
        zb_ref[...] = seg("zb")
        ga_ref[...] = seg("ga")
        gb_ref[...] = seg("gb")
        alr = seg("alr").astype(BF16)
        alr_ref[...] = alr
        u = _dot(alr, wa_ref[...]) + ba_ref[...]
        u_ref[...] = u
        log_a = (jnp.minimum(u, 0.0) - jnp.log(1.0 + jnp.exp(-jnp.abs(u)))) * (1.0 / B_GATE_TEMP)
        row, col = _row_iota((tm, tm)), _lane_iota((tm, tm))
        tri = ((row // B_CHUNK == col // B_CHUNK) & (col <= row)).astype(BF16)
        hi, mid, lo = _split3(log_a)
        cum_ref[...] = _dot(tri, hi) + _dot(tri, mid) + _dot(tri, lo)

    def rows(w):
        return pl.BlockSpec((tm, w), lambda i: (i, 0))

    outs = [("h", D_MODEL, BF16), ("qkv", QKV_W, BF16), ("za", A_WIDTH, F32), ("q", B_KEY_WIDTH, F32),
            ("k", B_KEY_WIDTH, F32), ("vb", B_WIDTH, BF16), ("zb", B_WIDTH, F32), ("alr", RANK_PAD, BF16),
            ("u", B_KEY_WIDTH, F32), ("cum", B_KEY_WIDTH, F32), ("ga", D_MODEL, F32), ("gb", D_MODEL, F32)]
    res = pl.pallas_call(
        body, name="in_proj", grid=(T // tm,),
        in_specs=[rows(D_MODEL), rows(LANES), rows(LANES), _const_spec((1, D_MODEL)),
                  _const_spec((D_MODEL, D_IN_PAD)), _const_spec((RANK_PAD, B_KEY_WIDTH)), _const_spec((1, B_KEY_WIDTH))],
        out_specs=[rows(w) for _, w, _ in outs],
        out_shape=[jax.ShapeDtypeStruct((T, w), dt) for _, w, dt in outs],
        compiler_params=_params("parallel"),
    )(x2, cosf, sinf, g_in, w_pad, wa_pad, b_alpha)
    return dict(zip([n for n, _, _ in outs], res))


def _dup_kv_head(t, g):
    tf = t.astype(F32)
    keep = (_lane_iota(tf.shape) < A_HEAD_DIM) == (g == 0)
    return jnp.where(keep, tf, pltpu.roll(tf, A_HEAD_DIM, 1)).astype(BF16)


def _stack_heads(t):
    lo = _lane_iota(t.shape) < A_HEAD_DIM
    zero = jnp.zeros_like(t)
    return jnp.concatenate([jnp.where(lo, t, zero), jnp.where(lo, zero, t)], axis=0)


def _band_mask(n):
    qi = _row_iota((2 * WINDOW, 2 * WINDOW)) % WINDOW
    kj = _lane_iota((2 * WINDOW, 2 * WINDOW)) - WINDOW
    return (kj <= qi) & (qi - kj < WINDOW) & ((n > 0) | (kj >= 0))


def _attn_fwd(qkv, sinks, B, S):
    T = B * S
    nb = S // WINDOW
    scale = A_HEAD_DIM ** -0.5

    def body(sink_ref, q_ref, kc_ref, vc_ref, kp_ref, vp_ref, o_ref, lse_ref):
        n = pl.program_id(1)
        valid = _band_mask(n)
        k = jnp.concatenate([kp_ref[...], kc_ref[...]], axis=0)
        v = jnp.concatenate([vp_ref[...], vc_ref[...]], axis=0)
        top = _row_iota((2 * WINDOW, 1)) < WINDOW
        lo = _lane_iota((WINDOW, LANES)) < A_HEAD_DIM
        lane = _lane_iota((WINDOW, LANES))
        lse_tile = jnp.zeros((WINDOW, LANES), F32)
        for g in range(A_KV_HEADS):
            kd, vd = _dup_kv_head(k, g), _dup_kv_head(v, g)
            for p in (2 * g, 2 * g + 1):
                qs = _stack_heads(q_ref[:, p * LANES:(p + 1) * LANES])
                s = jnp.where(valid, _dot_nt(qs, kd) * scale, NEG_BIG)
                sink = jnp.where(top, sink_ref[2 * p], sink_ref[2 * p + 1])
                m = jnp.maximum(jnp.max(s, axis=-1, keepdims=True), sink)
                e = jnp.exp(s - m)
                den = jnp.sum(e, axis=-1, keepdims=True) + jnp.exp(sink - m)
                o = _dot((e * (1.0 / den)).astype(BF16), vd)
                o_ref[:, p * LANES:(p + 1) * LANES] = jnp.where(lo, o[:WINDOW], o[WINDOW:])
                lse = m + jnp.log(den)
                lse_tile = jnp.where(lane == 2 * p, lse[:WINDOW], lse_tile)
                lse_tile = jnp.where(lane == 2 * p + 1, lse[WINDOW:], lse_tile)
        lse_ref[...] = lse_tile

    def cur(col, w):
        return pl.BlockSpec((WINDOW, w), lambda b, n: (b * nb + n, col))

    def prev(col):
        return pl.BlockSpec((WINDOW, LANES), lambda b, n: (b * nb + jnp.maximum(n - 1, 0), col))

    kcol, vcol = QKV_K // LANES, QKV_V // LANES
    return pl.pallas_call(
        body, name="attn_fwd", grid=(B, nb),
        in_specs=[pl.BlockSpec(memory_space=pltpu.SMEM), cur(0, A_WIDTH), cur(kcol, LANES), cur(vcol, LANES),
                  prev(kcol), prev(vcol)],
        out_specs=[cur(0, A_WIDTH), cur(0, LANES)],
        out_shape=[jax.ShapeDtypeStruct((T, A_WIDTH), F32), jax.ShapeDtypeStruct((T, LANES), F32)],
        compiler_params=_params("parallel", "parallel"),
    )(sinks, qkv, qkv, qkv, qkv, qkv)


def _attn_bwd(qkv, do, out, lse, sinks, B, S):
    T = B * S
    nb = S // WINDOW
    scale = A_HEAD_DIM ** -0.5

    def body(sink_ref, q_ref, kc_ref, vc_ref, kp_ref, vp_ref, do_ref, out_ref, lse_ref,
             dq_ref, dkv_ref, dsink_ref, carry_ref):
        b, n = pl.program_id(0), pl.program_id(1)
        active = n < nb
        nq = jnp.minimum(n, nb - 1)
        valid = _band_mask(nq)

        @pl.when((b == 0) & (n == 0))
        def _():
            dsink_ref[...] = jnp.zeros_like(dsink_ref)

        k = jnp.concatenate([kp_ref[...], kc_ref[...]], axis=0)
        v = jnp.concatenate([vp_ref[...], vc_ref[...]], axis=0)
        top = _row_iota((2 * WINDOW, 1)) < WINDOW
        lane = _lane_iota((WINDOW, LANES))
        lo = lane < A_HEAD_DIM
        lane2 = _lane_iota((2 * WINDOW, LANES))
        lse_tile = lse_ref[...]
        dk_tot = jnp.zeros((2 * WINDOW, LANES), F32)
        dv_tot = jnp.zeros((2 * WINDOW, LANES), F32)
        dsink_row = jnp.zeros((1, LANES), F32)
        for g in range(A_KV_HEADS):
            kd, vd = _dup_kv_head(k, g), _dup_kv_head(v, g)
            dk_acc = jnp.zeros((2 * WINDOW, LANES), F32)
            dv_acc = jnp.zeros((2 * WINDOW, LANES), F32)
            for p in (2 * g, 2 * g + 1):
                sl = slice(p * LANES, (p + 1) * LANES)
                qs = _stack_heads(q_ref[:, sl])
                dos = _stack_heads(do_ref[:, sl])
                s = jnp.where(valid, _dot_nt(qs, kd) * scale, NEG_BIG)
                lse0 = jnp.sum(jnp.where(lane == 2 * p, lse_tile, 0.0), axis=-1, keepdims=True)
                lse1 = jnp.sum(jnp.where(lane == 2 * p + 1, lse_tile, 0.0), axis=-1, keepdims=True)
                lse_col = jnp.concatenate([lse0, lse1], axis=0)
                prob = jnp.exp(s - lse_col)
                prod = do_ref[:, sl].astype(F32) * out_ref[:, sl]
                d0 = jnp.sum(jnp.where(lo, prod, 0.0), axis=-1, keepdims=True)
                d1 = jnp.sum(jnp.where(lo, 0.0, prod), axis=-1, keepdims=True)
                delta = jnp.concatenate([d0, d1], axis=0)
                dp = _dot_nt(dos, vd)
                ds = (prob * (dp - delta) * scale).astype(BF16)
                dq = _dot(ds, kd)
                dq_ref[:, sl] = jnp.where(lo, dq[:WINDOW], dq[WINDOW:]).astype(BF16)
                dk_acc += _dot_tn(ds, qs)
                dv_acc += _dot_tn(prob.astype(BF16), dos)
                sink = jnp.where(top, sink_ref[2 * p], sink_ref[2 * p + 1])
                w = -jnp.exp(sink - lse_col) * delta
                w0 = jnp.sum(w[:WINDOW], axis=0, keepdims=True)
                w1 = jnp.sum(w[WINDOW:], axis=0, keepdims=True)
                lane1 = _lane_iota((1, LANES))
                dsink_row += jnp.where(lane1 == 2 * p, w0, 0.0) + jnp.where(lane1 == 2 * p + 1, w1, 0.0)
            mine = (lane2 < A_HEAD_DIM) == (g == 0)
            dk_tot = jnp.where(mine, dk_acc + pltpu.roll(dk_acc, A_HEAD_DIM, 1), dk_tot)
            dv_tot = jnp.where(mine, dv_acc + pltpu.roll(dv_acc, A_HEAD_DIM, 1), dv_tot)
        gate = jnp.where(active, 1.0, 0.0)
        dsink_ref[0:1, :] += dsink_row * gate
        dkv_ref[:, 0:LANES] = (carry_ref[:, 0:LANES] + dk_tot[:WINDOW] * gate).astype(BF16)
        dkv_ref[:, LANES:] = (carry_ref[:, LANES:] + dv_tot[:WINDOW] * gate).astype(BF16)
        carry_ref[:, 0:LANES] = dk_tot[WINDOW:]
        carry_ref[:, LANES:] = dv_tot[WINDOW:]

    def cur(col, w):
        return pl.BlockSpec((WINDOW, w), lambda b, n: (b * nb + jnp.minimum(n, nb - 1), col))

    def prev(col):
        return pl.BlockSpec((WINDOW, LANES), lambda b, n: (b * nb + jnp.maximum(jnp.minimum(n, nb - 1) - 1, 0), col))

    lag = pl.BlockSpec((WINDOW, 2 * LANES), lambda b, n: (b * nb + jnp.maximum(n - 1, 0), 0))
    kcol, vcol = QKV_K // LANES, QKV_V // LANES
    return pl.pallas_call(
        body, name="attn_bwd", grid=(B, nb + 1),
        in_specs=[pl.BlockSpec(memory_space=pltpu.SMEM), cur(0, A_WIDTH), cur(kcol, LANES), cur(vcol, LANES),
                  prev(kcol), prev(vcol), cur(0, A_WIDTH), cur(0, A_WIDTH), cur(0, LANES)],
        out_specs=[cur(0, A_WIDTH), lag, pl.BlockSpec((8, LANES), lambda b, n: (0, 0))],
        out_shape=[jax.ShapeDtypeStruct((T, A_WIDTH), BF16), jax.ShapeDtypeStruct((T, 2 * LANES), BF16),
                   jax.ShapeDtypeStruct((8, LANES), F32)],
        scratch_shapes=[pltpu.VMEM((WINDOW, 2 * LANES), F32)],
        compiler_params=_params("arbitrary", "arbitrary"),
    )(sinks, qkv, qkv, qkv, qkv, qkv, do, out, lse)


GLA_TILE = 256
CHUNKS_PER_TILE = GLA_TILE // B_CHUNK


def _gla_factors(q_ref, k_ref, cum_ref):
    scale = B_KEY_DIM ** -0.5
    cum = cum_ref[...]
    shape = (B_CHUNK, B_KEY_WIDTH)
    last = jnp.concatenate([jnp.broadcast_to(cum_ref[pl.ds(c * B_CHUNK + B_CHUNK - 1, 1), :], shape)
                            for c in range(CHUNKS_PER_TILE)], axis=0)
    mid = jnp.concatenate([jnp.broadcast_to(cum_ref[pl.ds(c * B_CHUNK + B_CHUNK // 2 - 1, 1), :], shape)
                           for c in range(CHUNKS_PER_TILE)], axis=0)
    e_qm, e_km, e_qe, e_kd = jnp.exp(cum - mid), jnp.exp(mid - cum), jnp.exp(cum), jnp.exp(last - cum)
    qs = q_ref[...] * scale
    k = k_ref[...]
    return qs, k, (e_qm, e_km, e_qe, e_kd)


def _head_mask(shape, h):
    return (_lane_iota(shape) // B_KEY_DIM) == h


def _stack_masked(t):
    return jnp.concatenate([jnp.where(_head_mask(t.shape, h), t, 0.0) for h in range(B_HEADS)], axis=0).astype(BF16)


def _select_heads(t):
    shape = (B_CHUNK, B_KEY_WIDTH)
    out = jnp.zeros(shape, F32)
    for h in range(B_HEADS):
        out = jnp.where(_head_mask(shape, h), t[h * B_CHUNK:(h + 1) * B_CHUNK], out)
    return out


def _select_state(t):
    shape = (B_VAL_DIM, B_KEY_WIDTH)
    out = jnp.zeros(shape, F32)
    for h in range(B_HEADS):
        out = jnp.where(_head_mask(shape, h), t[h * B_VAL_DIM:(h + 1) * B_VAL_DIM], out)
    return out


def _rows_by_head(t):
    return jnp.concatenate([t[:, h * B_VAL_DIM:(h + 1) * B_VAL_DIM] for h in range(B_HEADS)], axis=0)


def _intra_mask():
    i, j = _row_iota((GLA_TILE, GLA_TILE)), _lane_iota((GLA_TILE, GLA_TILE))
    return (i // B_CHUNK == j // B_CHUNK) & (j <= i)


def _pair_stack(t, p):
    slab = t[:, p * LANES:(p + 1) * LANES]
    lo = _lane_iota(slab.shape) < B_KEY_DIM
    return jnp.concatenate([jnp.where(lo, slab, 0.0), jnp.where(lo, 0.0, slab)], axis=0).astype(BF16)


def _gla_fwd(q, k, cum, vb, B, S):
    T = B * S
    nt = S // GLA_TILE

    def body(q_ref, k_ref, cum_ref, v_ref, o_ref, st_all_ref, st_ref):
        @pl.when(pl.program_id(1) == 0)
        def _():
            st_ref[...] = jnp.zeros_like(st_ref)

        qs, kk, (e_qm, e_km, e_qe, e_kd) = _gla_factors(q_ref, k_ref, cum_ref)
        qm, km, qe, kd = qs * e_qm, kk * e_km, qs * e_qe, (kk * e_kd).astype(BF16)
        mask = _intra_mask()
        intra = []
        for p in range(B_HEADS // 2):
            a = _dot_nt(_pair_stack(qm, p), km[:, p * LANES:(p + 1) * LANES].astype(BF16))
            for hh in range(2):
                h = 2 * p + hh
                att = jnp.where(mask, a[hh * GLA_TILE:(hh + 1) * GLA_TILE], 0.0).astype(BF16)
                intra.append(_dot(att, v_ref[:, h * B_VAL_DIM:(h + 1) * B_VAL_DIM]))
        inter = []
        for c in range(CHUNKS_PER_TILE):
            rows = slice(c * B_CHUNK, (c + 1) * B_CHUNK)
            st = st_ref[...]
            st_all_ref[c] = st
            inter.append(_dot_nt(_stack_masked(qe[rows]), st.astype(BF16)))
            inc = _select_state(_dot_tn(v_ref[rows, :], kd[rows]))
            decay = jnp.exp(cum_ref[pl.ds(c * B_CHUNK + B_CHUNK - 1, 1), :])
            st_ref[...] = st * decay + inc
        for h in range(B_HEADS):
            oi = jnp.concatenate([inter[c][h * B_CHUNK:(h + 1) * B_CHUNK] for c in range(CHUNKS_PER_TILE)], axis=0)
            o_ref[:, h * B_VAL_DIM:(h + 1) * B_VAL_DIM] = intra[h] + oi

    def rows(w):
        return pl.BlockSpec((GLA_TILE, w), lambda b, t: (b * nt + t, 0))

    return pl.pallas_call(
        body, name="gla_fwd", grid=(B, nt),
        in_specs=[rows(B_KEY_WIDTH), rows(B_KEY_WIDTH), rows(B_KEY_WIDTH), rows(B_WIDTH)],
        out_specs=[rows(B_WIDTH),
                   pl.BlockSpec((CHUNKS_PER_TILE, B_VAL_DIM, B_KEY_WIDTH), lambda b, t: (b * nt + t, 0, 0))],
        out_shape=[jax.ShapeDtypeStruct((T, B_WIDTH), F32),
                   jax.ShapeDtypeStruct((T // B_CHUNK, B_VAL_DIM, B_KEY_WIDTH), F32)],
        scratch_shapes=[pltpu.VMEM((B_VAL_DIM, B_KEY_WIDTH), F32)],
        compiler_params=_params("arbitrary", "arbitrary"),
    )(q, k, cum, vb)


def _gla_bwd(q, k, cum, vb, do, st_all, B, S, wgrads):
    T = B * S
    nt = S // GLA_TILE
    scale = B_KEY_DIM ** -0.5
    nw = len(wgrads)

    def body(q_ref, k_ref, cum_ref, v_ref, do_ref, st_all_ref, *rest):
        g_refs, (dq_ref, dk_ref, dv_ref, dla_ref) = rest[:nw], rest[nw:nw + 4]
        rv_refs, (dst_ref, send_sems, recv_sems) = rest[nw + 4:2 * nw + 4], rest[2 * nw + 4:]
        x, y, c = _my_place()

        def wcopy(a, r):
            dx, dy, dc = FLIPS[r]
            return pltpu.make_async_remote_copy(
                src_ref=g_refs[a].at[4 * (x ^ dx) + 2 * (y ^ dy) + (c ^ dc)], dst_ref=rv_refs[a].at[r],
                send_sem=send_sems.at[a, r], recv_sem=recv_sems.at[a, r],
                device_id=(x ^ dx, y ^ dy, c ^ dc), device_id_type=MESH)

        @pl.when((pl.program_id(0) == 0) & (pl.program_id(1) == 0))
        def _():
            for a in range(nw):
                for r in range(len(FLIPS)):
                    wcopy(a, r).start()

        @pl.when(pl.program_id(1) == 0)
        def _():
            dst_ref[...] = jnp.zeros_like(dst_ref)

        qs, kk, (e_qm, e_km, e_qe, e_kd) = _gla_factors(q_ref, k_ref, cum_ref)
        qm, km, qe, kd = qs * e_qm, kk * e_km, qs * e_qe, kk * e_kd
        mask = _intra_mask()
        dqm_slabs, dkm_slabs, dv_intra = [], [], []
        for p in range(B_HEADS // 2):
            qm_st = _pair_stack(qm, p)
            km_p = km[:, p * LANES:(p + 1) * LANES].astype(BF16)
            a = _dot_nt(qm_st, km_p)
            da_blocks, dqm_h = [], []
            for hh in range(2):
                h = 2 * p + hh
                vs = slice(h * B_VAL_DIM, (h + 1) * B_VAL_DIM)
                att = jnp.where(mask, a[hh * GLA_TILE:(hh + 1) * GLA_TILE], 0.0).astype(BF16)
                dv_intra.append(_dot_tn(att, do_ref[:, vs]))
                da = jnp.where(mask, _dot_nt(do_ref[:, vs], v_ref[:, vs]), 0.0).astype(BF16)
                da_blocks.append(da)
                dqm_h.append(_dot(da, km_p))
            lo = _lane_iota((GLA_TILE, LANES)) < B_KEY_DIM
            dqm_slabs.append(jnp.where(lo, dqm_h[0], dqm_h[1]))
            dkm_slabs.append(_dot_tn(jnp.concatenate(da_blocks, axis=0), qm_st))
        dqm = jnp.concatenate(dqm_slabs, axis=1)
        dkm = jnp.concatenate(dkm_slabs, axis=1)

        dqe_c, dkd_c, dv_inter, tail_c = ([None] * CHUNKS_PER_TILE for _ in range(4))
        for c in reversed(range(CHUNKS_PER_TILE)):
            rows = slice(c * B_CHUNK, (c + 1) * B_CHUNK)
            dst = dst_ref[...]
            dst_b = dst.astype(BF16)
            dv_inter[c] = _dot_nt(_stack_masked(kd[rows]), dst_b)
            dkd_c[c] = _select_heads(_dot(_rows_by_head(v_ref[rows, :]), dst_b))
            do_c = do_ref[rows, :]
            dqe_c[c] = _select_heads(_dot(_rows_by_head(do_c), st_all_ref[c].astype(BF16)))
            contrib = _select_state(_dot_tn(do_c, qe[rows].astype(BF16)))
            decay = jnp.exp(cum_ref[pl.ds(c * B_CHUNK + B_CHUNK - 1, 1), :])
            tail = (jnp.sum(kk[rows] * dkd_c[c] * e_kd[rows], axis=0, keepdims=True)
                    + decay * jnp.sum(st_all_ref[c] * dst, axis=0, keepdims=True))
            tail_c[c] = jnp.broadcast_to(tail, (B_CHUNK, B_KEY_WIDTH))
            dst_ref[...] = dst * decay + contrib
        dqe = jnp.concatenate(dqe_c, axis=0)
        dkd = jnp.concatenate(dkd_c, axis=0)
        dqs = dqm * e_qm + dqe * e_qe
        dk = dkm * e_km + dkd * e_kd
        dq_ref[...] = (dqs * scale).astype(BF16)
        dk_ref[...] = dk.astype(BF16)
        for h in range(B_HEADS):
            dvi = jnp.concatenate([dv_inter[c][h * B_CHUNK:(h + 1) * B_CHUNK] for c in range(CHUNKS_PER_TILE)], axis=0)
            dv_ref[:, h * B_VAL_DIM:(h + 1) * B_VAL_DIM] = (dv_intra[h] + dvi).astype(BF16)
        dd = qs * dqs - kk * dk
        i, j = _row_iota((GLA_TILE, GLA_TILE)), _lane_iota((GLA_TILE, GLA_TILE))
        upper = ((i // B_CHUNK == j // B_CHUNK) & (j >= i)).astype(BF16)
        hi, mid, lo3 = _split3(dd)
        dla_ref[...] = _dot(upper, hi) + _dot(upper, mid) + _dot(upper, lo3) + jnp.concatenate(tail_c, axis=0)

        @pl.when((pl.program_id(0) == B - 1) & (pl.program_id(1) == nt - 1))
        def _():
            for a in range(nw):
                for r in range(len(FLIPS)):
                    wcopy(a, r).wait()

    def rows(w):
        return pl.BlockSpec((GLA_TILE, w), lambda b, t: (b * nt + nt - 1 - t, 0))

    res = pl.pallas_call(
        body, name="gla_bwd", grid=(B, nt),
        in_specs=[rows(B_KEY_WIDTH), rows(B_KEY_WIDTH), rows(B_KEY_WIDTH), rows(B_WIDTH), rows(B_WIDTH),
                  pl.BlockSpec((CHUNKS_PER_TILE, B_VAL_DIM, B_KEY_WIDTH), lambda b, t: (b * nt + nt - 1 - t, 0, 0))]
                 + _any_specs(nw),
        out_specs=[rows(B_KEY_WIDTH), rows(B_KEY_WIDTH), rows(B_WIDTH), rows(B_KEY_WIDTH)] + _any_specs(nw),
        out_shape=[jax.ShapeDtypeStruct((T, B_KEY_WIDTH), BF16), jax.ShapeDtypeStruct((T, B_KEY_WIDTH), BF16),
                   jax.ShapeDtypeStruct((T, B_WIDTH), BF16), jax.ShapeDtypeStruct((T, B_KEY_WIDTH), F32)]
                  + [jax.ShapeDtypeStruct((len(FLIPS), *g.shape[1:]), g.dtype) for g in wgrads],
        scratch_shapes=[pltpu.VMEM((B_VAL_DIM, B_KEY_WIDTH), F32),
                        pltpu.SemaphoreType.DMA((nw, len(FLIPS))), pltpu.SemaphoreType.DMA((nw, len(FLIPS)))],
        compiler_params=_params("arbitrary", "arbitrary"),
    )(q, k, cum, vb, do, st_all, *wgrads)
    return res[:4], res[4:]


def _merge(x2, tgt2, attn, za, o_gla, zb, ga, gb, w_oa, w_ob, w_o, g_gla, g_final):
    T = x2.shape[0]
    tm = 256
    last = T // tm - 1

    def body(x_ref, tgt_ref, attn_ref, za_ref, og_ref, zb_ref, ga_ref, gb_ref,
             woa_ref, wob_ref, wo_ref, gg_ref, gf_ref,
             dxres_ref, dattn_ref, dog_ref, dza_ref, dzb_ref, dga_ref, dgb_ref,
             dwo_ref, dwoa_ref, dwob_ref, small_ref, loss_ref,
             awo_ref, awoa_ref, awob_ref, agf_ref, agg_ref):
        @pl.when(pl.program_id(0) == 0)
        def _():
            for r in (awo_ref, awoa_ref, awob_ref, agf_ref, agg_ref, loss_ref):
                r[...] = jnp.zeros_like(r)

        za_v = za_ref[...]
        sig_za = _sigmoid(za_v)
        silu_a = za_v * sig_za
        attn_v = attn_ref[...]
        oa = (attn_v * silu_a).astype(BF16)
        ya = _dot(oa, woa_ref[...])
        og = og_ref[...]
        zb_v = zb_ref[...]
        sig_zb = _sigmoid(zb_v)
        silu_b = zb_v * sig_zb
        gg = gg_ref[...]
        on_parts, rinv_parts = [], []
        for h in range(B_HEADS):
            seg = og[:, h * B_VAL_DIM:(h + 1) * B_VAL_DIM]
            rinv = lax.rsqrt(jnp.mean(seg * seg, axis=-1, keepdims=True) + NORM_EPS)
            rinv_parts.append(rinv)
            on_parts.append(seg * rinv)
        on = jnp.concatenate(on_parts, axis=1)
        obn = on * gg
        ob = (obn * silu_b).astype(BF16)
        yb = _dot(ob, wob_ref[...])
        sig_a, sig_b = _sigmoid(ga_ref[...]), _sigmoid(gb_ref[...])
        merged = (sig_a * ya + sig_b * yb).astype(BF16)
        out = x_ref[...] + _dot(merged, wo_ref[...])
        rf = lax.rsqrt(jnp.mean(out * out, axis=-1, keepdims=True) + NORM_EPS)
        nrm = out * rf
        gf = gf_ref[...]
        err = nrm * gf - tgt_ref[...]
        loss_ref[...] += jnp.sum(err * err) * (0.5 / D_MODEL)

        dy = err * (1.0 / D_MODEL)
        agf_ref[...] += jnp.sum(dy * nrm, axis=0, keepdims=True)
        dn = dy * gf
        dout = rf * (dn - nrm * jnp.mean(dn * nrm, axis=-1, keepdims=True))
        dxres_ref[...] = dout
        dout_b = dout.astype(BF16)
        dmerged = _dot_nt(dout_b, wo_ref[...])
        awo_ref[...] += _dot_tn(merged, dout_b)
        dya = dmerged * sig_a
        dyb = dmerged * sig_b
        dga_ref[...] = (dmerged * ya * sig_a * (1.0 - sig_a)).astype(BF16)
        dgb_ref[...] = (dmerged * yb * sig_b * (1.0 - sig_b)).astype(BF16)
        dya_b, dyb_b = dya.astype(BF16), dyb.astype(BF16)
        awoa_ref[...] += _dot_tn(oa, dya_b)
        awob_ref[...] += _dot_tn(ob, dyb_b)
        doa = _dot_nt(dya_b, woa_ref[...])
        dattn_ref[...] = (doa * silu_a).astype(BF16)
        dza_ref[...] = (doa * attn_v * (sig_za * (1.0 + za_v * (1.0 - sig_za)))).astype(BF16)
        dob = _dot_nt(dyb_b, wob_ref[...])
        dzb_ref[...] = (dob * obn * (sig_zb * (1.0 + zb_v * (1.0 - sig_zb)))).astype(BF16)
        dobn = dob * silu_b
        agg_ref[...] += jnp.sum(dobn * on, axis=0, keepdims=True)
        don = dobn * gg
        for h in range(B_HEADS):
            sl = slice(h * B_VAL_DIM, (h + 1) * B_VAL_DIM)
            don_h, on_h = don[:, sl], on[:, sl]
            dog_ref[:, sl] = (rinv_parts[h] * (don_h - on_h * jnp.mean(don_h * on_h, axis=-1, keepdims=True))).astype(BF16)

        @pl.when(pl.program_id(0) == last)
        def _():
            for j in range(N_DEV):
                dwo_ref[j] = awo_ref[j * SHARD_OUT:(j + 1) * SHARD_OUT, :].astype(BF16)
                dwoa_ref[j] = awoa_ref[:, j * SHARD_OUT:(j + 1) * SHARD_OUT].astype(BF16)
                dwob_ref[j] = awob_ref[:, j * SHARD_OUT:(j + 1) * SHARD_OUT].astype(BF16)
            small_ref[...] = jnp.zeros_like(small_ref)
            _put_rows(small_ref, SMALL_G_FINAL, agf_ref[...])
            _put_rows(small_ref, SMALL_G_GLA, agg_ref[...])

    def rows(w):
        return pl.BlockSpec((tm, w), lambda i: (i, 0))

    def whole(shape):
        nd = len(shape)
        return pl.BlockSpec(shape, lambda i: (0,) * nd)

    outs = [((T, D_MODEL), F32, rows(D_MODEL)), ((T, A_WIDTH), BF16, rows(A_WIDTH)), ((T, B_WIDTH), BF16, rows(B_WIDTH)),
            ((T, A_WIDTH), BF16, rows(A_WIDTH)), ((T, B_WIDTH), BF16, rows(B_WIDTH)),
            ((T, D_MODEL), BF16, rows(D_MODEL)), ((T, D_MODEL), BF16, rows(D_MODEL)),
            ((N_DEV, SHARD_OUT, D_MODEL), BF16, whole((N_DEV, SHARD_OUT, D_MODEL))),
            ((N_DEV, A_WIDTH, SHARD_OUT), BF16, whole((N_DEV, A_WIDTH, SHARD_OUT))),
            ((N_DEV, B_WIDTH, SHARD_OUT), BF16, whole((N_DEV, B_WIDTH, SHARD_OUT))),
            ((SMALL_SINKS, LANES), F32, whole((SMALL_SINKS, LANES))), ((8, LANES), F32, whole((8, LANES)))]
    return pl.pallas_call(
        body, name="merge", grid=(T // tm,),
        in_specs=[rows(D_MODEL), rows(D_MODEL), rows(A_WIDTH), rows(A_WIDTH), rows(B_WIDTH), rows(B_WIDTH),
                  rows(D_MODEL), rows(D_MODEL),
                  _const_spec((A_WIDTH, D_MODEL)), _const_spec((B_WIDTH, D_MODEL)), _const_spec((D_MODEL, D_MODEL)),
                  _const_spec((1, B_WIDTH)), _const_spec((1, D_MODEL))],
        out_specs=[o[2] for o in outs],
        out_shape=[jax.ShapeDtypeStruct(o[0], o[1]) for o in outs],
        scratch_shapes=[pltpu.VMEM((D_MODEL, D_MODEL), F32), pltpu.VMEM((A_WIDTH, D_MODEL), F32),
                        pltpu.VMEM((B_WIDTH, D_MODEL), F32), pltpu.VMEM((1, D_MODEL), F32), pltpu.VMEM((1, B_WIDTH), F32)],
        compiler_params=_params("arbitrary"),
    )(x2, tgt2, attn, za, o_gla, zb, ga, gb, w_oa, w_ob, w_o, g_gla, g_final)


def _shard_pad_cols(j):
    cut = ALR_SRC + B_GATE_RANK
    shift = RANK_PAD - B_GATE_RANK
    a, b = j * SHARD_IN, (j + 1) * SHARD_IN
    if b <= cut:
        return [(a, b)]
    if a >= cut:
        return [(a + shift, b + shift)]
    return [(a, cut), (cut + shift, b + shift)]


def _in_proj_bwd(x2, dxres, cosf, sinf, g_in, w_pad, wa_pad, parts):
    T = x2.shape[0]
    tm = 256
    last = T // tm - 1
    base = SMALL_G_IN

    def body(x_ref, dxres_ref, cos_ref, sin_ref, g_ref, w_ref, wa_ref,
             dq_ref, dkv_ref, dza_ref, dqb_ref, dkb_ref, dvb_ref, dzb_ref, dla_ref, u_ref, alr_ref, dga_ref, dgb_ref,
             dx_ref, dsh_ref, small_ref, dproj_ref, agin_ref, aba_ref, awa_ref):
        @pl.when(pl.program_id(0) == 0)
        def _():
            for r in (agin_ref, aba_ref, awa_ref):
                r[...] = jnp.zeros_like(r)

        cos, nsin = cos_ref[...], -sin_ref[...]
        for s in range(A_WIDTH // LANES):
            sl = slice(s * LANES, (s + 1) * LANES)
            dproj_ref[:, sl] = _rope_slab(dq_ref[:, sl].astype(F32), cos, nsin).astype(BF16)
        dproj_ref[:, QKV_K:QKV_V] = _rope_slab(dkv_ref[:, 0:LANES].astype(F32), cos, nsin).astype(BF16)
        dproj_ref[:, QKV_V:QKV_W] = dkv_ref[:, LANES:]

        def put(name, val):
            a, b = SEG[name]
            dproj_ref[:, a:b] = val

        put("za", dza_ref[...])
        put("qb", dqb_ref[...])
        put("kb", dkb_ref[...])
        put("vb", dvb_ref[...])
        put("zb", dzb_ref[...])
        put("ga", dga_ref[...])
        put("gb", dgb_ref[...])
        du = dla_ref[...] * (1.0 / B_GATE_TEMP) * _sigmoid(-u_ref[...])
        aba_ref[...] += jnp.sum(du, axis=0, keepdims=True)
        du_b = du.astype(BF16)
        awa_ref[...] += _dot_tn(alr_ref[...], du_b)
        put("alr", _dot_nt(du_b, wa_ref[...]).astype(BF16))

        for j in range(N_DEV):
            col = (j % 2) * SHARD_PAD
            for a, b in _shard_pad_cols(j):
                dsh_ref[j // 2, :, col:col + b - a] = dproj_ref[:, a:b]
                col += b - a
            dsh_ref[j // 2, :, col:(j % 2 + 1) * SHARD_PAD] = jnp.zeros((tm, SHARD_PAD - SHARD_IN), BF16)

        dh = _dot_nt(dproj_ref[...], w_ref[...])
        x = x_ref[...]
        r = lax.rsqrt(jnp.mean(x * x, axis=-1, keepdims=True) + NORM_EPS)
        nrm = x * r
        agin_ref[...] += jnp.sum(dh * nrm, axis=0, keepdims=True)
        dn = dh * g_ref[...]
        dx_ref[...] = dxres_ref[...] + r * (dn - nrm * jnp.mean(dn * nrm, axis=-1, keepdims=True))

        @pl.when(pl.program_id(0) == last)
        def _():
            small_ref[...] = jnp.zeros_like(small_ref)
            _put_rows(small_ref, SMALL_G_IN - base, agin_ref[...])
            _put_rows(small_ref, SMALL_B_ALPHA - base, aba_ref[...])
            for half in range(B_KEY_WIDTH // LANES):
                r0 = SMALL_W_ALPHA - base + half * B_GATE_RANK
                small_ref[r0:r0 + B_GATE_RANK, :] = awa_ref[0:B_GATE_RANK, half * LANES:(half + 1) * LANES]

    def rows(w):
        return pl.BlockSpec((tm, w), lambda i: (i, 0))

    names = ["dq", "dkv", "dza", "dqb", "dkb", "dvb", "dzb", "dla", "u", "alr", "dga", "dgb"]
    return pl.pallas_call(
        body, name="in_proj_bwd", grid=(T // tm,),
        in_specs=[rows(D_MODEL), rows(D_MODEL), rows(LANES), rows(LANES), _const_spec((1, D_MODEL)),
                  _const_spec((D_MODEL, D_IN_PAD)), _const_spec((RANK_PAD, B_KEY_WIDTH))]
                 + [rows(parts[n].shape[1]) for n in names],
        out_specs=[rows(D_MODEL), pl.BlockSpec((N_CHIPS, tm, 2 * SHARD_PAD), lambda i: (0, i, 0)),
                   pl.BlockSpec((SMALL_ROWS - base, LANES), lambda i: (0, 0))],
        out_shape=[jax.ShapeDtypeStruct((T, D_MODEL), F32), jax.ShapeDtypeStruct((N_CHIPS, T, 2 * SHARD_PAD), BF16),
                   jax.ShapeDtypeStruct((SMALL_ROWS - base, LANES), F32)],
        scratch_shapes=[pltpu.VMEM((tm, D_IN_PAD), BF16), pltpu.VMEM((1, D_MODEL), F32), pltpu.VMEM((1, B_KEY_WIDTH), F32),
                        pltpu.VMEM((RANK_PAD, B_KEY_WIDTH), F32)],
        compiler_params=_params("arbitrary"),
    )(x2, dxres, cosf, sinf, g_in, w_pad, wa_pad, *[parts[n] for n in names])


FLIPS = [(dx, dy, dc) for dx in (0, 1) for dy in (0, 1) for dc in (0, 1)][1:]


def _my_place():
    return lax.axis_index("x"), lax.axis_index("y"), lax.axis_index("c")


def _any_specs(n):
    return [pl.BlockSpec(memory_space=pl.ANY)] * n


def _all_gather(shards):
    n = len(shards)

    def body(*refs):
        ins, outs = refs[:n], refs[n:2 * n]
        send_sems, recv_sems, local_sems = refs[2 * n:]
        x, y, c = _my_place()
        me, sibling = (x, y, c), (x, y, 1 - c)
        chips = [(1 - x, y), (x, 1 - y), (1 - x, 1 - y)]

        def block(a, px, py, pc):
            return outs[a].at[4 * px + 2 * py + pc]

        def copy(a, k, blk, to, src=None):
            return pltpu.make_async_remote_copy(
                src_ref=block(a, *blk) if src is None else src, dst_ref=block(a, *blk),
                send_sem=send_sems.at[a, k], recv_sem=recv_sems.at[a, k], device_id=to, device_id_type=MESH)

        mine = [pltpu.make_async_copy(ins[a], block(a, *me), local_sems.at[a]) for a in range(n)]
        for cp in mine:
            cp.start()
        first = []
        for a in range(n):
            first.append(copy(a, 0, me, sibling, src=ins[a]))
            first += [copy(a, 1 + j, me, (*chip, c), src=ins[a]) for j, chip in enumerate(chips)]
        for cp in first:
            cp.start()
        passed = []
        for j, chip in enumerate(chips):
            for a in range(n):
                copy(a, 1 + j, (*chip, c), me).wait_recv()
                fwd = copy(a, 4 + j, (*chip, c), sibling)
                fwd.start()
                passed.append(fwd)
        for a in range(n):
            copy(a, 0, sibling, me).wait_recv()
            for j, chip in enumerate(chips):
                copy(a, 4 + j, (*chip, 1 - c), me).wait_recv()
        for cp in first + passed:
            cp.wait_send()
        for cp in mine:
            cp.wait()

    return pl.pallas_call(
        body, name="gather_weights",
        in_specs=_any_specs(n), out_specs=_any_specs(n),
        out_shape=[jax.ShapeDtypeStruct((N_DEV, *s.shape), s.dtype) for s in shards],
        scratch_shapes=[pltpu.SemaphoreType.DMA((n, 7)), pltpu.SemaphoreType.DMA((n, 7)), pltpu.SemaphoreType.DMA((n,))],
    )(*shards)


def _w_in_grad_rs(h, dsh, chip_order, small):
    T = h.shape[0]
    tk = math.gcd(T, 2048)
    nk = T // tk
    chip_flips = [(1, 1), (1, 0), (0, 1)]
    n_steps = len(chip_flips) + 1

    def body(order_ref, h_ref, d_ref, s_ref, own_ref, recv_ref, sall_ref,
             acc_ref, stage_ref, send_sems, recv_sems, ssend_sems, srecv_sems, local_sem):
        i, kk = pl.program_id(0), pl.program_id(1)
        x, y, c = _my_place()
        my_dev = 4 * x + 2 * y + c

        def small_copy(r, slot):
            dx, dy, dc = FLIPS[r]
            return pltpu.make_async_remote_copy(
                src_ref=s_ref, dst_ref=sall_ref.at[slot], send_sem=ssend_sems.at[r], recv_sem=srecv_sems.at[r],
                device_id=(x ^ dx, y ^ dy, c ^ dc), device_id_type=MESH)

        keep_small = pltpu.make_async_copy(s_ref, sall_ref.at[my_dev], local_sem)

        def shard_copy(slot, dx, dy, dc):
            r = FLIPS.index((dx, dy, dc))
            return pltpu.make_async_remote_copy(
                src_ref=stage_ref.at[slot, c ^ dc], dst_ref=recv_ref.at[r], send_sem=send_sems.at[r],
                recv_sem=recv_sems.at[r], device_id=(x ^ dx, y ^ dy, c ^ dc), device_id_type=MESH)

        def stage(slot):
            stage_ref[slot, 0] = acc_ref[:, 0:SHARD_PAD].astype(BF16)
            stage_ref[slot, 1] = acc_ref[:, SHARD_PAD:2 * SHARD_PAD].astype(BF16)

        @pl.when((i == 0) & (kk == 0))
        def _():
            keep_small.start()
            for r in range(len(FLIPS)):
                small_copy(r, my_dev).start()

        @pl.when(kk == 0)
        def _():
            acc_ref[...] = jnp.zeros_like(acc_ref)

        acc_ref[...] += _dot_tn(h_ref[...], d_ref[...])

        for t, (dx, dy) in enumerate(chip_flips):
            @pl.when((i == t) & (kk == nk - 1))
            def _(t=t, dx=dx, dy=dy):
                if t >= 2:
                    for dc in (0, 1):
                        shard_copy(t % 2, *chip_flips[t - 2], dc).wait_send()
                stage(t % 2)
                for dc in (0, 1):
                    shard_copy(t % 2, dx, dy, dc).start()

        @pl.when((i == n_steps - 1) & (kk == nk - 1))
        def _():
            for dc in (0, 1):
                shard_copy(1, *chip_flips[1], dc).wait_send()
            stage(1)
            shard_copy(1, 0, 0, 1).start()

            @pl.when(c == 0)
            def _():
                own_ref[...] = acc_ref[:, 0:SHARD_PAD]

            @pl.when(c == 1)
            def _():
                own_ref[...] = acc_ref[:, SHARD_PAD:2 * SHARD_PAD]

            for dc in (0, 1):
                shard_copy(0, *chip_flips[2], dc).wait_send()
            shard_copy(1, 0, 0, 1).wait_send()
            for r, (dx, dy, dc) in enumerate(FLIPS):
                shard_copy(0, dx, dy, dc).wait_recv()
                small_copy(r, 4 * (x ^ dx) + 2 * (y ^ dy) + (c ^ dc)).wait_recv()
                small_copy(r, my_dev).wait_send()
            keep_small.wait()

    return pl.pallas_call(
        body, name="w_in_grad_rs",
        grid_spec=pltpu.PrefetchScalarGridSpec(
            num_scalar_prefetch=1, grid=(n_steps, nk),
            in_specs=[pl.BlockSpec((tk, D_MODEL), lambda i, kk, order: (kk, 0)),
                      pl.BlockSpec((None, tk, 2 * SHARD_PAD), lambda i, kk, order: (order[i], kk, 0)),
                      pl.BlockSpec(memory_space=pl.ANY)],
            out_specs=[pl.BlockSpec((D_MODEL, SHARD_PAD), lambda i, kk, order: (0, 0)),
                       pl.BlockSpec(memory_space=pl.ANY), pl.BlockSpec(memory_space=pl.ANY)],
            scratch_shapes=[pltpu.VMEM((D_MODEL, 2 * SHARD_PAD), F32), pltpu.VMEM((2, 2, D_MODEL, SHARD_PAD), BF16),
                            pltpu.SemaphoreType.DMA((7,)), pltpu.SemaphoreType.DMA((7,)),
                            pltpu.SemaphoreType.DMA((7,)), pltpu.SemaphoreType.DMA((7,)), pltpu.SemaphoreType.DMA]),
        out_shape=[jax.ShapeDtypeStruct((D_MODEL, SHARD_PAD), F32),
                   jax.ShapeDtypeStruct((len(FLIPS), D_MODEL, SHARD_PAD), BF16),
                   jax.ShapeDtypeStruct((N_DEV, *small.shape), F32)],
        compiler_params=_params("arbitrary", "arbitrary"),
    )(chip_order, h, dsh, small)


def _adam_math(w, g, m, v):
    m_new = ADAM_B1 * m + (1.0 - ADAM_B1) * g
    v_new = ADAM_B2 * v + (1.0 - ADAM_B2) * (g * g)
    m_hat = m_new / (1.0 - ADAM_B1 ** ADAM_STEP)
    v_hat = v_new / (1.0 - ADAM_B2 ** ADAM_STEP)
    delta = -ADAM_LR * (m_hat / (jnp.sqrt(v_hat) + ADAM_EPS) + ADAM_WD * w)
    return delta, m_new, v_new


def _adam_big(name, own, own_idx, recv, w, m, v):
    rw, cw = w.shape
    cwp = own.shape[2]
    tr = rw // 8

    def body(idx_ref, o_ref, r_ref, w_ref, m_ref, v_ref, g_ref, d_ref, mo_ref, vo_ref):
        g = o_ref[...].astype(F32)
        for r in range(len(FLIPS)):
            g = g + r_ref[r].astype(F32)
        g = g[:, 0:cw]
        g_ref[...] = g
        d_ref[...], mo_ref[...], vo_ref[...] = _adam_math(w_ref[...], g, m_ref[...], v_ref[...])

    spec = pl.BlockSpec((tr, cw), lambda i, idx_ref: (i, 0))
    return pl.pallas_call(
        body, name=name,
        grid_spec=pltpu.PrefetchScalarGridSpec(
            num_scalar_prefetch=1, grid=(rw // tr,),
            in_specs=[pl.BlockSpec((None, tr, cwp), lambda i, idx_ref: (idx_ref[0], i, 0)),
                      pl.BlockSpec((len(FLIPS), tr, cwp), lambda i, idx_ref: (0, i, 0)), spec, spec, spec],
            out_specs=[spec] * 4),
        out_shape=[jax.ShapeDtypeStruct((rw, cw), F32)] * 4,
        compiler_params=_params("parallel"),
    )(own_idx, own, recv, w, m, v)


def _adam_small(small_all, params):
    flat = [a for triple in params for a in triple]
    n_par = len(params)

    def body(s_ref, *refs):
        ins, outs = refs[:3 * n_par], refs[3 * n_par:]
        g_slab = s_ref[0]
        for dev in range(1, N_DEV):
            g_slab = g_slab + s_ref[dev]
        dev = 4 * lax.axis_index("x") + 2 * lax.axis_index("y") + lax.axis_index("c")
        alpha_full = jnp.concatenate([g_slab[SMALL_W_ALPHA + half * B_GATE_RANK:SMALL_W_ALPHA + (half + 1) * B_GATE_RANK]
                                      for half in range(B_KEY_WIDTH // LANES)], axis=1)
        alpha_mine = pltpu.roll(alpha_full, (B_KEY_WIDTH - dev * SHARD_ALPHA) % B_KEY_WIDTH, 1)[:, 0:SHARD_ALPHA]
        grads = [_take_rows(g_slab, SMALL_G_IN, D_MODEL // LANES), _take_rows(g_slab, SMALL_G_FINAL, D_MODEL // LANES),
                 _take_rows(g_slab, SMALL_G_GLA, B_WIDTH // LANES), _take_rows(g_slab, SMALL_B_ALPHA, B_KEY_WIDTH // LANES),
                 g_slab[SMALL_SINKS:SMALL_SINKS + 1, 0:A_HEADS], alpha_mine]
        for i, g in enumerate(grads):
            w_ref, m_ref, v_ref = ins[3 * i:3 * i + 3]
            delta, m_new, v_new = _adam_math(w_ref[...], g, m_ref[...], v_ref[...])
            outs[4 * i][...] = g
            outs[4 * i + 1][...] = delta
            outs[4 * i + 2][...] = m_new
            outs[4 * i + 3][...] = v_new

    res = pl.pallas_call(
        body, name="adam_small",
        out_shape=[jax.ShapeDtypeStruct(t[0].shape, F32) for t in params for _ in range(4)],
    )(small_all, *flat)
    return [res[4 * i:4 * i + 4] for i in range(n_par)]


def _local_step(x, positions, loss_target, g_in, w_pad, wa_pad, b_alpha, sinks, g_gla, w_oa, w_ob, w_o, g_final,
                chip_order):
    B, S, _ = x.shape
    T = B * S
    x2 = x.reshape(T, D_MODEL)
    tgt2 = loss_target.reshape(T, D_MODEL)
    cosf, sinf = _rope_tables(positions.reshape(T, 1))
    f = _in_proj(x2, cosf, sinf, g_in, w_pad, wa_pad, b_alpha)
    attn, lse = _attn_fwd(f["qkv"], sinks, B, S)
    o_gla, st_all = _gla_fwd(f["q"], f["k"], f["cum"], f["vb"], B, S)
    (dxres, dattn, dog, dza, dzb, dga, dgb, dw_o, dw_oa, dw_ob, small_a, loss_acc) = _merge(
        x2, tgt2, attn, f["za"], o_gla, f["zb"], f["ga"], f["gb"], w_oa, w_ob, w_o, g_gla, g_final)
    dq, dkv, dsink = _attn_bwd(f["qkv"], dattn, attn, lse, sinks, B, S)
    (dqb, dkb, dvb, dla), (rv_o, rv_oa, rv_ob) = _gla_bwd(f["q"], f["k"], f["cum"], f["vb"], dog, st_all, B, S,
                                                        [dw_o, dw_oa, dw_ob])
    parts = dict(dq=dq, dkv=dkv, dza=dza, dqb=dqb, dkb=dkb, dvb=dvb, dzb=dzb, dla=dla, u=f["u"], alr=f["alr"],
                 dga=dga, dgb=dgb)
    dx, dsh, small_c = _in_proj_bwd(x2, dxres, cosf, sinf, g_in, w_pad, wa_pad, parts)
    small = jnp.concatenate([small_a, dsink, small_c], axis=0)
    own_in, rv_in, small_all = _w_in_grad_rs(f["h"], dsh, chip_order, small)
    return dict(loss=loss_acc[0, 0], grad_x=dx.reshape(B, S, D_MODEL), own_in=own_in, rv_in=rv_in,
                own_o=dw_o, rv_o=rv_o, own_oa=dw_oa, rv_oa=rv_oa, own_ob=dw_ob, rv_ob=rv_ob, small_all=small_all)


def _pad_from_shards(shards):
    cols = jnp.concatenate([shards[j] for j in range(N_DEV)], axis=1)
    z = jnp.zeros((cols.shape[0], RANK_PAD - B_GATE_RANK), cols.dtype)
    return jnp.concatenate([cols[:, :ALR_SRC + B_GATE_RANK], z, cols[:, ALR_SRC + B_GATE_RANK:]], axis=1)


def kernel(x, positions, g_in, w_in, w_alpha_up, b_alpha, attn_sinks, g_gla_norm, w_out_a, w_out_b, w_o, g_final, loss_target, m_g_in, m_w_in, m_w_alpha_up, m_b_alpha, m_attn_sinks, m_g_gla_norm, m_w_out_a, m_w_out_b, m_w_o, m_g_final, v_g_in, v_w_in, v_w_alpha_up, v_b_alpha, v_attn_sinks, v_g_gla_norm, v_w_out_a, v_w_out_b, v_w_o, v_g_final):
    xi, yi, ci = _my_place()
    dev_idx = (4 * xi + 2 * yi + ci).reshape(1).astype(jnp.int32)
    chip = 2 * xi + yi
    chip_order = jnp.stack([chip ^ 3, chip ^ 2, chip ^ 1, chip]).astype(jnp.int32)

    g_win, g_woa, g_wob, g_wo, g_wa = _all_gather(
        [w_in[0].astype(BF16), w_out_a[0].astype(BF16), w_out_b[0].astype(BF16), w_o[0].astype(BF16),
         w_alpha_up[0].astype(BF16)])
    w_pad = _pad_from_shards(g_win)
    w_oa_full = jnp.concatenate([g_woa[j] for j in range(N_DEV)], axis=1)
    w_ob_full = jnp.concatenate([g_wob[j] for j in range(N_DEV)], axis=1)
    w_o_full = g_wo.reshape(D_MODEL, D_MODEL)
    wa_pad = jnp.pad(jnp.concatenate([g_wa[j] for j in range(N_DEV)], axis=1), ((0, RANK_PAD - B_GATE_RANK), (0, 0)))

    r = _local_step(x, positions, loss_target, g_in, w_pad, wa_pad, b_alpha, attn_sinks[0], g_gla_norm, w_oa_full,
                    w_ob_full, w_o_full, g_final.reshape(1, D_MODEL), chip_order)
    loss = lax.psum(r["loss"], ("x", "y", "c"))

    first = jnp.zeros((1,), jnp.int32)
    big = [_adam_big("adam_w_in", r["own_in"][None], first, r["rv_in"], w_in[0], m_w_in[0], v_w_in[0]),
           _adam_big("adam_w_out_a", r["own_oa"], dev_idx, r["rv_oa"], w_out_a[0], m_w_out_a[0], v_w_out_a[0]),
           _adam_big("adam_w_out_b", r["own_ob"], dev_idx, r["rv_ob"], w_out_b[0], m_w_out_b[0], v_w_out_b[0]),
           _adam_big("adam_w_o", r["own_o"], dev_idx, r["rv_o"], w_o[0], m_w_o[0], v_w_o[0])]
    row = lambda a: a.reshape(1, D_MODEL)
    s_in, s_final, s_gla, s_ba, s_sinks, s_wa = _adam_small(r["small_all"], [
        (g_in, m_g_in, v_g_in), (row(g_final), row(m_g_final), row(v_g_final)),
        (g_gla_norm, m_g_gla_norm, v_g_gla_norm), (b_alpha, m_b_alpha, v_b_alpha),
        (attn_sinks, m_attn_sinks, v_attn_sinks), (w_alpha_up[0], m_w_alpha_up[0], v_w_alpha_up[0])])

    def group(i):
        return (s_in[i], big[0][i][None], s_wa[i][None], s_ba[i], s_sinks[i], s_gla[i], big[1][i][None], big[2][i][None],
                big[3][i][None], s_final[i].reshape(D_MODEL))

    return (loss, r["grad_x"], *group(0), *group(1), *group(2), *group(3))
```

```python
import functools
import math

import numpy as np
import jax
import jax.numpy as jnp
from jax import lax
from jax.experimental import pallas as pl
from jax.experimental.pallas import tpu as pltpu

F32 = jnp.float32
BF16 = jnp.bfloat16
MESH = pl.DeviceIdType.MESH

D_MODEL = 1024
A_HEADS, A_KV_HEADS, A_HEAD_DIM = 8, 2, 64
A_WIDTH, A_KV_WIDTH = 512, 128
WINDOW = 128
ROPE_THETA = 500000.0
ROPE_DIM = 16
B_HEADS, B_KEY_DIM, B_VAL_DIM = 4, 64, 128
B_KEY_WIDTH, B_WIDTH = 256, 512
B_GATE_RANK = 16
B_GATE_TEMP = 16.0
B_CHUNK = 64
NORM_EPS = 1e-6
NEG_BIG = -1e30
D_IN = 4880
N_DEV = 8
N_CHIPS = 4
ADAM_LR, ADAM_B1, ADAM_B2, ADAM_EPS, ADAM_WD, ADAM_STEP = 0.001, 0.9, 0.999, 1e-08, 0.01, 10

LANES = 128
V7X_VMEM_LIMIT = 56 * 1024 * 1024

RANK_PAD = LANES
SEG = {}
_off = 0
for _name, _w in (("qa", 512), ("ka", 128), ("va", 128), ("za", 512), ("qb", 256), ("kb", 256),
                  ("vb", 512), ("zb", 512), ("alr", RANK_PAD), ("ga", 1024), ("gb", 1024)):
    SEG[_name] = (_off, _off + _w)
    _off += _w
D_IN_PAD = _off
ALR_SRC = SEG["alr"][0]
QKV_K, QKV_V, QKV_W = SEG["ka"][0], SEG["va"][0], SEG["va"][1]

SHARD_IN = D_IN // N_DEV
SHARD_PAD = 640
SHARD_OUT = D_MODEL // N_DEV
SHARD_ALPHA = B_KEY_WIDTH // N_DEV

SMALL_G_FINAL, SMALL_G_GLA, SMALL_LOSS, SMALL_SINKS, SMALL_G_IN, SMALL_B_ALPHA, SMALL_W_ALPHA = 0, 8, 12, 16, 24, 32, 40
SMALL_ROWS = 72


def _dot(a, b):
    return jnp.dot(a, b, preferred_element_type=F32)


def _dot_nt(a, b):
    return lax.dot_general(a, b, (((1,), (1,)), ((), ())), preferred_element_type=F32)


def _dot_tn(a, b):
    return lax.dot_general(a, b, (((0,), (0,)), ((), ())), preferred_element_type=F32)


def _sigmoid(z):
    return 1.0 / (1.0 + jnp.exp(-z))


def _params(*sem):
    return pltpu.CompilerParams(dimension_semantics=sem, vmem_limit_bytes=V7X_VMEM_LIMIT)


def _const_spec(shape):
    nd = len(shape)
    return pl.BlockSpec(shape, lambda *_: (0,) * nd, pipeline_mode=pl.Buffered(1))


def _lane_iota(shape):
    return lax.broadcasted_iota(jnp.int32, shape, 1)


def _row_iota(shape):
    return lax.broadcasted_iota(jnp.int32, shape, 0)


def _split3(v):
    hi = v.astype(BF16)
    r1 = v - hi.astype(F32)
    mid = r1.astype(BF16)
    lo = (r1 - mid.astype(F32)).astype(BF16)
    return hi, mid, lo


def _put_rows(ref, row0, vec):
    for r in range(vec.shape[1] // LANES):
        ref[row0 + r:row0 + r + 1, :] = vec[:, r * LANES:(r + 1) * LANES]


def _take_rows(slab, row0, n):
    return jnp.concatenate([slab[row0 + r:row0 + r + 1, :] for r in range(n)], axis=1)


def _rope_lane_constants():
    half = ROPE_DIM // 2
    inv_freq = np.exp(-math.log(ROPE_THETA) * np.arange(half, dtype=np.float32) * np.float32(2.0 / ROPE_DIM)).astype(np.float32)
    lane = np.arange(LANES)
    j = lane % A_HEAD_DIM
    invf = np.where(j < ROPE_DIM, inv_freq[j % half], 0.0).astype(np.float32)
    sign = np.where(j < half, -1.0, np.where(j < ROPE_DIM, 1.0, 0.0)).astype(np.float32)
    return jnp.asarray(invf)[None, :], jnp.asarray(sign)[None, :]


def _rope_tables(pos_col):
    T = pos_col.shape[0]
    tm = math.gcd(T, 1024)
    invf, sign = _rope_lane_constants()

    def body(pos_ref, invf_ref, sign_ref, cos_ref, sin_ref):
        ang = pos_ref[...].astype(F32) * invf_ref[...]
        cos_ref[...] = jnp.cos(ang)
        sin_ref[...] = jnp.sin(ang) * sign_ref[...]

    return pl.pallas_call(
        body, name="rope_tables", grid=(T // tm,),
        in_specs=[pl.BlockSpec((tm, 1), lambda i: (i, 0)), _const_spec((1, LANES)), _const_spec((1, LANES))],
        out_specs=[pl.BlockSpec((tm, LANES), lambda i: (i, 0))] * 2,
        out_shape=[jax.ShapeDtypeStruct((T, LANES), F32)] * 2,
        compiler_params=_params("parallel"),
    )(pos_col, invf, sign)


def _rope_slab(t, cos, sin_signed):
    first = (_lane_iota(t.shape) % A_HEAD_DIM) < (ROPE_DIM // 2)
    partner = jnp.where(first, pltpu.roll(t, LANES - ROPE_DIM // 2, 1), pltpu.roll(t, ROPE_DIM // 2, 1))
    return t * cos + partner * sin_signed


def _in_proj(x2, cosf, sinf, g_in, wt_pad, wa_pad, b_alpha):
    T = x2.shape[0]
    tm = 256

    def body(x_ref, cos_ref, sin_ref, g_ref, wt_ref, wa_ref, ba_ref,
             h_ref, qkv_ref, za_ref, q_ref, k_ref, vb_ref, zb_ref, alr_ref, u_ref, cum_ref, ga_ref, gb_ref):
        x = x_ref[...]
        r = lax.rsqrt(jnp.mean(x * x, axis=-1, keepdims=True) + NORM_EPS)
        h = (x * r * g_ref[...]).astype(BF16)
        h_ref[...] = h

        def seg(name):
            a, b = SEG[name]
            return _dot_nt(h, wt_ref[a:b, :])

        cos, sin = cos_ref[...], sin_ref[...]
        qa = seg("qa")
        for s in range(A_WIDTH // LANES):
            qkv_ref[:, s * LANES:(s + 1) * LANES] = _rope_slab(qa[:, s * LANES:(s + 1) * LANES], cos, sin).astype(BF16)
        qkv_ref[:, QKV_K:QKV_V] = _rope_slab(seg("ka"), cos, sin).astype(BF16)
        qkv_ref[:, QKV_V:QKV_W] = seg("va").astype(BF16)
        za_ref[...] = seg("za")
        q_ref[...] = seg("qb")
        k_ref[...] = seg("kb")
        vb_ref[...] = seg("vb").astype(BF16)
        zb_ref[...] = seg("zb")
        ga_ref[...] = seg("ga")
        gb_ref[...] = seg("gb")
        alr = seg("alr").astype(BF16)
        alr_ref[...] = alr
        u = _dot(alr, wa_ref[...]) + ba_ref[...]
        u_ref[...] = u
        log_a = (jnp.minimum(u, 0.0) - jnp.log(1.0 + jnp.exp(-jnp.abs(u)))) * (1.0 / B_GATE_TEMP)
        row, col = _row_iota((tm, tm)), _lane_iota((tm, tm))
        tri = ((row // B_CHUNK == col // B_CHUNK) & (col <= row)).astype(BF16)
        hi, mid, lo = _split3(log_a)
        cum_ref[...] = _dot(tri, hi) + _dot(tri, mid) + _dot(tri, lo)

    def rows(w):
        return pl.BlockSpec((tm, w), lambda i: (i, 0))

    outs = [("h", D_MODEL, BF16), ("qkv", QKV_W, BF16), ("za", A_WIDTH, F32), ("q", B_KEY_WIDTH, F32),
            ("k", B_KEY_WIDTH, F32), ("vb", B_WIDTH, BF16), ("zb", B_WIDTH, F32), ("alr", RANK_PAD, BF16),
            ("u", B_KEY_WIDTH, F32), ("cum", B_KEY_WIDTH, F32), ("ga", D_MODEL, F32), ("gb", D_MODEL, F32)]
    res = pl.pallas_call(
        body, name="in_proj", grid=(T // tm,),
        in_specs=[rows(D_MODEL), rows(LANES), rows(LANES), _const_spec((1, D_MODEL)),
                  _const_spec((D_IN_PAD, D_MODEL)), _const_spec((RANK_PAD, B_KEY_WIDTH)), _const_spec((1, B_KEY_WIDTH))],
        out_specs=[rows(w) for _, w, _ in outs],
        out_shape=[jax.ShapeDtypeStruct((T, w), dt) for _, w, dt in outs],
        compiler_params=_params("parallel"),
    )(x2, cosf, sinf, g_in, wt_pad, wa_pad, b_alpha)
    return dict(zip([n for n, _, _ in outs], res))


def _dup_kv_head(t, g):
    tf = t.astype(F32)
    keep = (_lane_iota(tf.shape) < A_HEAD_DIM) == (g == 0)
    return jnp.where(keep, tf, pltpu.roll(tf, A_HEAD_DIM, 1)).astype(BF16)


def _stack_heads(t):
    lo = _lane_iota(t.shape) < A_HEAD_DIM
    zero = jnp.zeros_like(t)
    return jnp.concatenate([jnp.where(lo, t, zero), jnp.where(lo, zero, t)], axis=0)


def _band_mask(n):
    qi = _row_iota((2 * WINDOW, 2 * WINDOW)) % WINDOW
    kj = _lane_iota((2 * WINDOW, 2 * WINDOW)) - WINDOW
    return (kj <= qi) & (qi - kj < WINDOW) & ((n > 0) | (kj >= 0))


def _attn_fwd(qkv, sinks, B, S):
    T = B * S
    nb = S // WINDOW
    scale = A_HEAD_DIM ** -0.5

    def body(sink_ref, q_ref, kc_ref, vc_ref, kp_ref, vp_ref, o_ref, lse_ref):
        n = pl.program_id(1)
        valid = _band_mask(n)
        k = jnp.concatenate([kp_ref[...], kc_ref[...]], axis=0)
        v = jnp.concatenate([vp_ref[...], vc_ref[...]], axis=0)
        top = _row_iota((2 * WINDOW, 1)) < WINDOW
        lo = _lane_iota((WINDOW, LANES)) < A_HEAD_DIM
        lane = _lane_iota((WINDOW, LANES))
        lse_tile = jnp.zeros((WINDOW, LANES), F32)
        for g in range(A_KV_HEADS):
            kd, vd = _dup_kv_head(k, g), _dup_kv_head(v, g)
            for p in (2 * g, 2 * g + 1):
                qs = _stack_heads(q_ref[:, p * LANES:(p + 1) * LANES])
                s = jnp.where(valid, _dot_nt(qs, kd) * scale, NEG_BIG)
                sink = jnp.where(top, sink_ref[2 * p], sink_ref[2 * p + 1])
                m = jnp.maximum(jnp.max(s, axis=-1, keepdims=True), sink)
                e = jnp.exp(s - m)
                den = jnp.sum(e, axis=-1, keepdims=True) + jnp.exp(sink - m)
                o = _dot((e * (1.0 / den)).astype(BF16), vd)
                o_ref[:, p * LANES:(p + 1) * LANES] = jnp.where(lo, o[:WINDOW], o[WINDOW:])
                lse = m + jnp.log(den)
                lse_tile = jnp.where(lane == 2 * p, lse[:WINDOW], lse_tile)
                lse_tile = jnp.where(lane == 2 * p + 1, lse[WINDOW:], lse_tile)
        lse_ref[...] = lse_tile

    def cur(col, w):
        return pl.BlockSpec((WINDOW, w), lambda b, n: (b * nb + n, col))

    def prev(col):
        return pl.BlockSpec((WINDOW, LANES), lambda b, n: (b * nb + jnp.maximum(n - 1, 0), col))

    kcol, vcol = QKV_K // LANES, QKV_V // LANES
    return pl.pallas_call(
        body, name="attn_fwd", grid=(B, nb),
        in_specs=[pl.BlockSpec(memory_space=pltpu.SMEM), cur(0, A_WIDTH), cur(kcol, LANES), cur(vcol, LANES),
                  prev(kcol), prev(vcol)],
        out_specs=[cur(0, A_WIDTH), cur(0, LANES)],
        out_shape=[jax.ShapeDtypeStruct((T, A_WIDTH), F32), jax.ShapeDtypeStruct((T, LANES), F32)],
        compiler_params=_params("parallel", "parallel"),
    )(sinks, qkv, qkv, qkv, qkv, qkv)


def _attn_bwd(qkv, do, out, lse, sinks, B, S):
    T = B * S
    nb = S // WINDOW
    scale = A_HEAD_DIM ** -0.5

    def body(sink_ref, q_ref, kc_ref, vc_ref, kp_ref, vp_ref, do_ref, out_ref, lse_ref,
             dq_ref, dkv_ref, dsink_ref, carry_ref):
        b, n = pl.program_id(0), pl.program_id(1)
        active = n < nb
        nq = jnp.minimum(n, nb - 1)
        valid = _band_mask(nq)

        @pl.when((b == 0) & (n == 0))
        def _():
            dsink_ref[...] = jnp.zeros_like(dsink_ref)

        k = jnp.concatenate([kp_ref[...], kc_ref[...]], axis=0)
        v = jnp.concatenate([vp_ref[...], vc_ref[...]], axis=0)
        top = _row_iota((2 * WINDOW, 1)) < WINDOW
        lane = _lane_iota((WINDOW, LANES))
        lo = lane < A_HEAD_DIM
        lane2 = _lane_iota((2 * WINDOW, LANES))
        lse_tile = lse_ref[...]
        dk_tot = jnp.zeros((2 * WINDOW, LANES), F32)
        dv_tot = jnp.zeros((2 * WINDOW, LANES), F32)
        dsink_row = jnp.zeros((1, LANES), F32)
        for g in range(A_KV_HEADS):
            kd, vd = _dup_kv_head(k, g), _dup_kv_head(v, g)
            dk_acc = jnp.zeros((2 * WINDOW, LANES), F32)
            dv_acc = jnp.zeros((2 * WINDOW, LANES), F32)
            for p in (2 * g, 2 * g + 1):
                sl = slice(p * LANES, (p + 1) * LANES)
                qs = _stack_heads(q_ref[:, sl])
                dos = _stack_heads(do_ref[:, sl])
                s = jnp.where(valid, _dot_nt(qs, kd) * scale, NEG_BIG)
                lse0 = jnp.sum(jnp.where(lane == 2 * p, lse_tile, 0.0), axis=-1, keepdims=True)
                lse1 = jnp.sum(jnp.where(lane == 2 * p + 1, lse_tile, 0.0), axis=-1, keepdims=True)
                lse_col = jnp.concatenate([lse0, lse1], axis=0)
                prob = jnp.exp(s - lse_col)
                prod = do_ref[:, sl].astype(F32) * out_ref[:, sl]
                d0 = jnp.sum(jnp.where(lo, prod, 0.0), axis=-1, keepdims=True)
                d1 = jnp.sum(jnp.where(lo, 0.0, prod), axis=-1, keepdims=True)
                delta = jnp.concatenate([d0, d1], axis=0)
                dp = _dot_nt(dos, vd)
                ds = (prob * (dp - delta) * scale).astype(BF16)
                dq = _dot(ds, kd)
                dq_ref[:, sl] = jnp.where(lo, dq[:WINDOW], dq[WINDOW:]).astype(BF16)
                dk_acc += _dot_tn(ds, qs)
                dv_acc += _dot_tn(prob.astype(BF16), dos)
                sink = jnp.where(top, sink_ref[2 * p], sink_ref[2 * p + 1])
                w = -jnp.exp(sink - lse_col) * delta
                w0 = jnp.sum(w[:WINDOW], axis=0, keepdims=True)
                w1 = jnp.sum(w[WINDOW:], axis=0, keepdims=True)
                lane1 = _lane_iota((1, LANES))
                dsink_row += jnp.where(lane1 == 2 * p, w0, 0.0) + jnp.where(lane1 == 2 * p + 1, w1, 0.0)
            mine = (lane2 < A_HEAD_DIM) == (g == 0)
            dk_tot = jnp.where(mine, dk_acc + pltpu.roll(dk_acc, A_HEAD_DIM, 1), dk_tot)
            dv_tot = jnp.where(mine, dv_acc + pltpu.roll(dv_acc, A_HEAD_DIM, 1), dv_tot)
        gate = jnp.where(active, 1.0, 0.0)
        dsink_ref[0:1, :] += dsink_row * gate
        dkv_ref[:, 0:LANES] = (carry_ref[:, 0:LANES] + dk_tot[:WINDOW] * gate).astype(BF16)
        dkv_ref[:, LANES:] = (carry_ref[:, LANES:] + dv_tot[:WINDOW] * gate).astype(BF16)
        carry_ref[:, 0:LANES] = dk_tot[WINDOW:]
        carry_ref[:, LANES:] = dv_tot[WINDOW:]

    def cur(col, w):
        return pl.BlockSpec((WINDOW, w), lambda b, n: (b * nb + jnp.minimum(n, nb - 1), col))

    def prev(col):
        return pl.BlockSpec((WINDOW, LANES), lambda b, n: (b * nb + jnp.maximum(jnp.minimum(n, nb - 1) - 1, 0), col))

    lag = pl.BlockSpec((WINDOW, 2 * LANES), lambda b, n: (b * nb + jnp.maximum(n - 1, 0), 0))
    kcol, vcol = QKV_K // LANES, QKV_V // LANES
    return pl.pallas_call(
        body, name="attn_bwd", grid=(B, nb + 1),
        in_specs=[pl.BlockSpec(memory_space=pltpu.SMEM), cur(0, A_WIDTH), cur(kcol, LANES), cur(vcol, LANES),
                  prev(kcol), prev(vcol), cur(0, A_WIDTH), cur(0, A_WIDTH), cur(0, LANES)],
        out_specs=[cur(0, A_WIDTH), lag, pl.BlockSpec((8, LANES), lambda b, n: (0, 0))],
        out_shape=[jax.ShapeDtypeStruct((T, A_WIDTH), BF16), jax.ShapeDtypeStruct((T, 2 * LANES), BF16),
                   jax.ShapeDtypeStruct((8, LANES), F32)],
        scratch_shapes=[pltpu.VMEM((WINDOW, 2 * LANES), F32)],
        compiler_params=_params("arbitrary", "arbitrary"),
    )(sinks, qkv, qkv, qkv, qkv, qkv, do, out, lse)


GLA_TILE = 256
CHUNKS_PER_TILE = GLA_TILE // B_CHUNK


def _gla_factors(q_ref, k_ref, cum_ref):
    scale = B_KEY_DIM ** -0.5
    cum = cum_ref[...]
    shape = (B_CHUNK, B_KEY_WIDTH)
    last = jnp.concatenate([jnp.broadcast_to(cum_ref[pl.ds(c * B_CHUNK + B_CHUNK - 1, 1), :], shape)
                            for c in range(CHUNKS_PER_TILE)], axis=0)
    mid = jnp.concatenate([jnp.broadcast_to(cum_ref[pl.ds(c * B_CHUNK + B_CHUNK // 2 - 1, 1), :], shape)
                           for c in range(CHUNKS_PER_TILE)], axis=0)
    e_qm, e_km, e_qe, e_kd = jnp.exp(cum - mid), jnp.exp(mid - cum), jnp.exp(cum), jnp.exp(last - cum)
    qs = q_ref[...] * scale
    k = k_ref[...]
    return qs, k, (e_qm, e_km, e_qe, e_kd)


def _head_mask(shape, h):
    return (_lane_iota(shape) // B_KEY_DIM) == h


def _stack_masked(t):
    return jnp.concatenate([jnp.where(_head_mask(t.shape, h), t, 0.0) for h in range(B_HEADS)], axis=0).astype(BF16)


def _select_heads(t):
    shape = (B_CHUNK, B_KEY_WIDTH)
    out = jnp.zeros(shape, F32)
    for h in range(B_HEADS):
        out = jnp.where(_head_mask(shape, h), t[h * B_CHUNK:(h + 1) * B_CHUNK], out)
    return out


def _select_state(t):
    shape = (B_VAL_DIM, B_KEY_WIDTH)
    out = jnp.zeros(shape, F32)
    for h in range(B_HEADS):
        out = jnp.where(_head_mask(shape, h), t[h * B_VAL_DIM:(h + 1) * B_VAL_DIM], out)
    return out


def _rows_by_head(t):
    return jnp.concatenate([t[:, h * B_VAL_DIM:(h + 1) * B_VAL_DIM] for h in range(B_HEADS)], axis=0)


def _intra_mask():
    i, j = _row_iota((GLA_TILE, GLA_TILE)), _lane_iota((GLA_TILE, GLA_TILE))
    return (i // B_CHUNK == j // B_CHUNK) & (j <= i)


def _pair_stack(t, p):
    slab = t[:, p * LANES:(p + 1) * LANES]
    lo = _lane_iota(slab.shape) < B_KEY_DIM
    return jnp.concatenate([jnp.where(lo, slab, 0.0), jnp.where(lo, 0.0, slab)], axis=0).astype(BF16)


def _gla_fwd(q, k, cum, vb, B, S):
    T = B * S
    nt = S // GLA_TILE

    def body(q_ref, k_ref, cum_ref, v_ref, o_ref, st_all_ref, st_ref):
        @pl.when(pl.program_id(1) == 0)
        def _():
            st_ref[...] = jnp.zeros_like(st_ref)

        qs, kk, (e_qm, e_km, e_qe, e_kd) = _gla_factors(q_ref, k_ref, cum_ref)
        qm, km, qe, kd = qs * e_qm, kk * e_km, qs * e_qe, (kk * e_kd).astype(BF16)
        mask = _intra_mask()
        intra = []
        for p in range(B_HEADS // 2):
            a = _dot_nt(_pair_stack(qm, p), km[:, p * LANES:(p + 1) * LANES].astype(BF16))
            for hh in range(2):
                h = 2 * p + hh
                att = jnp.where(mask, a[hh * GLA_TILE:(hh + 1) * GLA_TILE], 0.0).astype(BF16)
                intra.append(_dot(att, v_ref[:, h * B_VAL_DIM:(h + 1) * B_VAL_DIM]))
        inter = []
        for c in range(CHUNKS_PER_TILE):
            rows = slice(c * B_CHUNK, (c + 1) * B_CHUNK)
            st = st_ref[...]
            st_all_ref[c] = st
            inter.append(_dot_nt(_stack_masked(qe[rows]), st.astype(BF16)))
            inc = _select_state(_dot_tn(v_ref[rows, :], kd[rows]))
            decay = jnp.exp(cum_ref[pl.ds(c * B_CHUNK + B_CHUNK - 1, 1), :])
            st_ref[...] = st * decay + inc
        for h in range(B_HEADS):
            oi = jnp.concatenate([inter[c][h * B_CHUNK:(h + 1) * B_CHUNK] for c in range(CHUNKS_PER_TILE)], axis=0)
            o_ref[:, h * B_VAL_DIM:(h + 1) * B_VAL_DIM] = intra[h] + oi

    def rows(w):
        return pl.BlockSpec((GLA_TILE, w), lambda b, t: (b * nt + t, 0))

    return pl.pallas_call(
        body, name="gla_fwd", grid=(B, nt),
        in_specs=[rows(B_KEY_WIDTH), rows(B_KEY_WIDTH), rows(B_KEY_WIDTH), rows(B_WIDTH)],
        out_specs=[rows(B_WIDTH),
                   pl.BlockSpec((CHUNKS_PER_TILE, B_VAL_DIM, B_KEY_WIDTH), lambda b, t: (b * nt + t, 0, 0))],
        out_shape=[jax.ShapeDtypeStruct((T, B_WIDTH), F32),
                   jax.ShapeDtypeStruct((T // B_CHUNK, B_VAL_DIM, B_KEY_WIDTH), F32)],
        scratch_shapes=[pltpu.VMEM((B_VAL_DIM, B_KEY_WIDTH), F32)],
        compiler_params=_params("arbitrary", "arbitrary"),
    )(q, k, cum, vb)


def _gla_bwd(q, k, cum, vb, do, st_all, B, S, wgrads):
    T = B * S
    nt = S // GLA_TILE
    scale = B_KEY_DIM ** -0.5
    nw = len(wgrads)

    def body(q_ref, k_ref, cum_ref, v_ref, do_ref, st_all_ref, *rest):
        g_refs, (dq_ref, dk_ref, dv_ref, dla_ref) = rest[:nw], rest[nw:nw + 4]
        rv_refs, (dst_ref, send_sems, recv_sems) = rest[nw + 4:2 * nw + 4], rest[2 * nw + 4:]
        x, y, c = _my_place()

        def wcopy(a, r):
            dx, dy, dc = FLIPS[r]
            return pltpu.make_async_remote_copy(
                src_ref=g_refs[a].at[4 * (x ^ dx) + 2 * (y ^ dy) + (c ^ dc)], dst_ref=rv_refs[a].at[r],
                send_sem=send_sems.at[a, r], recv_sem=recv_sems.at[a, r],
                device_id=(x ^ dx, y ^ dy, c ^ dc), device_id_type=MESH)

        @pl.when((pl.program_id(0) == 0) & (pl.program_id(1) == 0))
        def _():
            for a in range(nw):
                for r in range(len(FLIPS)):
                    wcopy(a, r).start()

        @pl.when(pl.program_id(1) == 0)
        def _():
            dst_ref[...] = jnp.zeros_like(dst_ref)

        qs, kk, (e_qm, e_km, e_qe, e_kd) = _gla_factors(q_ref, k_ref, cum_ref)
        qm, km, qe, kd = qs * e_qm, kk * e_km, qs * e_qe, kk * e_kd
        mask = _intra_mask()
        dqm_slabs, dkm_slabs, dv_intra = [], [], []
        for p in range(B_HEADS // 2):
            qm_st = _pair_stack(qm, p)
            km_p = km[:, p * LANES:(p + 1) * LANES].astype(BF16)
            a = _dot_nt(qm_st, km_p)
            da_blocks, dqm_h = [], []
            for hh in range(2):
                h = 2 * p + hh
                vs = slice(h * B_VAL_DIM, (h + 1) * B_VAL_DIM)
                att = jnp.where(mask, a[hh * GLA_TILE:(hh + 1) * GLA_TILE], 0.0).astype(BF16)
                dv_intra.append(_dot_tn(att, do_ref[:, vs]))
                da = jnp.where(mask, _dot_nt(do_ref[:, vs], v_ref[:, vs]), 0.0).astype(BF16)
                da_blocks.append(da)
                dqm_h.append(_dot(da, km_p))
            lo = _lane_iota((GLA_TILE, LANES)) < B_KEY_DIM
            dqm_slabs.append(jnp.where(lo, dqm_h[0], dqm_h[1]))
            dkm_slabs.append(_dot_tn(jnp.concatenate(da_blocks, axis=0), qm_st))
        dqm = jnp.concatenate(dqm_slabs, axis=1)
        dkm = jnp.concatenate(dkm_slabs, axis=1)

        dqe_c, dkd_c, dv_inter, tail_c = ([None] * CHUNKS_PER_TILE for _ in range(4))
        for c in reversed(range(CHUNKS_PER_TILE)):
            rows = slice(c * B_CHUNK, (c + 1) * B_CHUNK)
            dst = dst_ref[...]
            dst_b = dst.astype(BF16)
            dv_inter[c] = _dot_nt(_stack_masked(kd[rows]), dst_b)
            dkd_c[c] = _select_heads(_dot(_rows_by_head(v_ref[rows, :]), dst_b))
            do_c = do_ref[rows, :]
            dqe_c[c] = _select_heads(_dot(_rows_by_head(do_c), st_all_ref[c].astype(BF16)))
            contrib = _select_state(_dot_tn(do_c, qe[rows].astype(BF16)))
            decay = jnp.exp(cum_ref[pl.ds(c * B_CHUNK + B_CHUNK - 1, 1), :])
            tail = (jnp.sum(kk[rows] * dkd_c[c] * e_kd[rows], axis=0, keepdims=True)
                    + decay * jnp.sum(st_all_ref[c] * dst, axis=0, keepdims=True))
            tail_c[c] = jnp.broadcast_to(tail, (B_CHUNK, B_KEY_WIDTH))
            dst_ref[...] = dst * decay + contrib
        dqe = jnp.concatenate(dqe_c, axis=0)
        dkd = jnp.concatenate(dkd_c, axis=0)
        dqs = dqm * e_qm + dqe * e_qe
        dk = dkm * e_km + dkd * e_kd
        dq_ref[...] = (dqs * scale).astype(BF16)
        dk_ref[...] = dk.astype(BF16)
        for h in range(B_HEADS):
            dvi = jnp.concatenate([dv_inter[c][h * B_CHUNK:(h + 1) * B_CHUNK] for c in range(CHUNKS_PER_TILE)], axis=0)
            dv_ref[:, h * B_VAL_DIM:(h + 1) * B_VAL_DIM] = (dv_intra[h] + dvi).astype(BF16)
        dd = qs * dqs - kk * dk
        i, j = _row_iota((GLA_TILE, GLA_TILE)), _lane_iota((GLA_TILE, GLA_TILE))
        upper = ((i // B_CHUNK == j // B_CHUNK) & (j >= i)).astype(BF16)
        hi, mid, lo3 = _split3(dd)
        dla_ref[...] = _dot(upper, hi) + _dot(upper, mid) + _dot(upper, lo3) + jnp.concatenate(tail_c, axis=0)

        @pl.when((pl.program_id(0) == B - 1) & (pl.program_id(1) == nt - 1))
        def _():
            for a in range(nw):
                for r in range(len(FLIPS)):
                    wcopy(a, r).wait()

    def rows(w):
        return pl.BlockSpec((GLA_TILE, w), lambda b, t: (b * nt + nt - 1 - t, 0))

    res = pl.pallas_call(
        body, name="gla_bwd", grid=(B, nt),
        in_specs=[rows(B_KEY_WIDTH), rows(B_KEY_WIDTH), rows(B_KEY_WIDTH), rows(B_WIDTH), rows(B_WIDTH),
                  pl.BlockSpec((CHUNKS_PER_TILE, B_VAL_DIM, B_KEY_WIDTH), lambda b, t: (b * nt + nt - 1 - t, 0, 0))]
                 + _any_specs(nw),
        out_specs=[rows(B_KEY_WIDTH), rows(B_KEY_WIDTH), rows(B_WIDTH), rows(B_KEY_WIDTH)] + _any_specs(nw),
        out_shape=[jax.ShapeDtypeStruct((T, B_KEY_WIDTH), BF16), jax.ShapeDtypeStruct((T, B_KEY_WIDTH), BF16),
                   jax.ShapeDtypeStruct((T, B_WIDTH), BF16), jax.ShapeDtypeStruct((T, B_KEY_WIDTH), F32)]
                  + [jax.ShapeDtypeStruct((len(FLIPS), *g.shape[1:]), g.dtype) for g in wgrads],
        scratch_shapes=[pltpu.VMEM((B_VAL_DIM, B_KEY_WIDTH), F32),
                        pltpu.SemaphoreType.DMA((nw, len(FLIPS))), pltpu.SemaphoreType.DMA((nw, len(FLIPS)))],
        compiler_params=_params("arbitrary", "arbitrary"),
    )(q, k, cum, vb, do, st_all, *wgrads)
    return res[:4], res[4:]


def _merge(x2, tgt2, attn, za, o_gla, zb, ga, gb, w_oa, w_ob, w_o, g_gla, g_final):
    T = x2.shape[0]
    tm = 256
    last = T // tm - 1

    def body(x_ref, tgt_ref, attn_ref, za_ref, og_ref, zb_ref, ga_ref, gb_ref,
             woa_ref, wob_ref, wo_ref, gg_ref, gf_ref,
             dxres_ref, dattn_ref, dog_ref, dza_ref, dzb_ref, dga_ref, dgb_ref,
             dwo_ref, dwoa_ref, dwob_ref, small_ref,
             awo_ref, awoa_ref, awob_ref, agf_ref, agg_ref, loss_ref):
        @pl.when(pl.program_id(0) == 0)
        def _():
            for r in (awo_ref, awoa_ref, awob_ref, agf_ref, agg_ref, loss_ref):
                r[...] = jnp.zeros_like(r)

        za_v = za_ref[...]
        sig_za = _sigmoid(za_v)
        silu_a = za_v * sig_za
        attn_v = attn_ref[...]
        oa = (attn_v * silu_a).astype(BF16)
        ya = _dot(oa, woa_ref[...])
        og = og_ref[...]
        zb_v = zb_ref[...]
        sig_zb = _sigmoid(zb_v)
        silu_b = zb_v * sig_zb
        gg = gg_ref[...]
        on_parts, rinv_parts = [], []
        for h in range(B_HEADS):
            seg = og[:, h * B_VAL_DIM:(h + 1) * B_VAL_DIM]
            rinv = lax.rsqrt(jnp.mean(seg * seg, axis=-1, keepdims=True) + NORM_EPS)
            rinv_parts.append(rinv)
            on_parts.append(seg * rinv)
        on = jnp.concatenate(on_parts, axis=1)
        obn = on * gg
        ob = (obn * silu_b).astype(BF16)
        yb = _dot(ob, wob_ref[...])
        sig_a, sig_b = _sigmoid(ga_ref[...]), _sigmoid(gb_ref[...])
        merged = (sig_a * ya + sig_b * yb).astype(BF16)
        out = x_ref[...] + _dot(merged, wo_ref[...])
        rf = lax.rsqrt(jnp.mean(out * out, axis=-1, keepdims=True) + NORM_EPS)
        nrm = out * rf
        gf = gf_ref[...]
        err = nrm * gf - tgt_ref[...]
        loss_ref[...] += jnp.sum(err * err) * (0.5 / D_MODEL)

        dy = err * (1.0 / D_MODEL)
        agf_ref[...] += jnp.sum(dy * nrm, axis=0, keepdims=True)
        dn = dy * gf
        dout = rf * (dn - nrm * jnp.mean(dn * nrm, axis=-1, keepdims=True))
        dxres_ref[...] = dout
        dout_b = dout.astype(BF16)
        dmerged = _dot_nt(dout_b, wo_ref[...])
        awo_ref[...] += _dot_tn(merged, dout_b)
        dya = dmerged * sig_a
        dyb = dmerged * sig_b
        dga_ref[...] = (dmerged * ya * sig_a * (1.0 - sig_a)).astype(BF16)
        dgb_ref[...] = (dmerged * yb * sig_b * (1.0 - sig_b)).astype(BF16)
        dya_b, dyb_b = dya.astype(BF16), dyb.astype(BF16)
        awoa_ref[...] += _dot_tn(oa, dya_b)
        awob_ref[...] += _dot_tn(ob, dyb_b)
        doa = _dot_nt(dya_b, woa_ref[...])
        dattn_ref[...] = (doa * silu_a).astype(BF16)
        dza_ref[...] = (doa * attn_v * (sig_za * (1.0 + za_v * (1.0 - sig_za)))).astype(BF16)
        dob = _dot_nt(dyb_b, wob_ref[...])
        dzb_ref[...] = (dob * obn * (sig_zb * (1.0 + zb_v * (1.0 - sig_zb)))).astype(BF16)
        dobn = dob * silu_b
        agg_ref[...] += jnp.sum(dobn * on, axis=0, keepdims=True)
        don = dobn * gg
        for h in range(B_HEADS):
            sl = slice(h * B_VAL_DIM, (h + 1) * B_VAL_DIM)
            don_h, on_h = don[:, sl], on[:, sl]
            dog_ref[:, sl] = (rinv_parts[h] * (don_h - on_h * jnp.mean(don_h * on_h, axis=-1, keepdims=True))).astype(BF16)

        @pl.when(pl.program_id(0) == last)
        def _():
            for j in range(N_DEV):
                dwo_ref[j] = awo_ref[j * SHARD_OUT:(j + 1) * SHARD_OUT, :].astype(BF16)
                dwoa_ref[j] = awoa_ref[:, j * SHARD_OUT:(j + 1) * SHARD_OUT].astype(BF16)
                dwob_ref[j] = awob_ref[:, j * SHARD_OUT:(j + 1) * SHARD_OUT].astype(BF16)
            small_ref[...] = jnp.zeros_like(small_ref)
            _put_rows(small_ref, SMALL_G_FINAL, agf_ref[...])
            _put_rows(small_ref, SMALL_G_GLA, agg_ref[...])
            small_ref[SMALL_LOSS:SMALL_LOSS + 1, :] = loss_ref[...]

    def rows(w):
        return pl.BlockSpec((tm, w), lambda i: (i, 0))

    def whole(shape):
        nd = len(shape)
        return pl.BlockSpec(shape, lambda i: (0,) * nd)

    outs = [((T, D_MODEL), F32, rows(D_MODEL)), ((T, A_WIDTH), BF16, rows(A_WIDTH)), ((T, B_WIDTH), BF16, rows(B_WIDTH)),
            ((T, A_WIDTH), BF16, rows(A_WIDTH)), ((T, B_WIDTH), BF16, rows(B_WIDTH)),
            ((T, D_MODEL), BF16, rows(D_MODEL)), ((T, D_MODEL), BF16, rows(D_MODEL)),
            ((N_DEV, SHARD_OUT, D_MODEL), BF16, whole((N_DEV, SHARD_OUT, D_MODEL))),
            ((N_DEV, A_WIDTH, SHARD_OUT), BF16, whole((N_DEV, A_WIDTH, SHARD_OUT))),
            ((N_DEV, B_WIDTH, SHARD_OUT), BF16, whole((N_DEV, B_WIDTH, SHARD_OUT))),
            ((SMALL_SINKS, LANES), F32, whole((SMALL_SINKS, LANES)))]
    return pl.pallas_call(
        body, name="merge", grid=(T // tm,),
        in_specs=[rows(D_MODEL), rows(D_MODEL), rows(A_WIDTH), rows(A_WIDTH), rows(B_WIDTH), rows(B_WIDTH),
                  rows(D_MODEL), rows(D_MODEL),
                  _const_spec((A_WIDTH, D_MODEL)), _const_spec((B_WIDTH, D_MODEL)), _const_spec((D_MODEL, D_MODEL)),
                  _const_spec((1, B_WIDTH)), _const_spec((1, D_MODEL))],
        out_specs=[o[2] for o in outs],
        out_shape=[jax.ShapeDtypeStruct(o[0], o[1]) for o in outs],
        scratch_shapes=[pltpu.VMEM((D_MODEL, D_MODEL), F32), pltpu.VMEM((A_WIDTH, D_MODEL), F32),
                        pltpu.VMEM((B_WIDTH, D_MODEL), F32), pltpu.VMEM((1, D_MODEL), F32), pltpu.VMEM((1, B_WIDTH), F32),
                        pltpu.VMEM((1, LANES), F32)],
        compiler_params=_params("arbitrary"),
    )(x2, tgt2, attn, za, o_gla, zb, ga, gb, w_oa, w_ob, w_o, g_gla, g_final)


def _shard_pad_cols(j):
    cut = ALR_SRC + B_GATE_RANK
    shift = RANK_PAD - B_GATE_RANK
    a, b = j * SHARD_IN, (j + 1) * SHARD_IN
    if b <= cut:
        return [(a, b)]
    if a >= cut:
        return [(a + shift, b + shift)]
    return [(a, cut), (cut + shift, b + shift)]


def _in_proj_bwd(x2, dxres, cosf, sinf, g_in, wt_pad, wa_pad, parts):
    T = x2.shape[0]
    tm = 256
    last = T // tm - 1
    base = SMALL_G_IN

    def body(x_ref, dxres_ref, cos_ref, sin_ref, g_ref, wt_ref, wa_ref,
             dq_ref, dkv_ref, dza_ref, dqb_ref, dkb_ref, dvb_ref, dzb_ref, dla_ref, u_ref, alr_ref, dga_ref, dgb_ref,
             dx_ref, dsh_ref, small_ref, dproj_ref, agin_ref, aba_ref, awa_ref):
        @pl.when(pl.program_id(0) == 0)
        def _():
            for r in (agin_ref, aba_ref, awa_ref):
                r[...] = jnp.zeros_like(r)

        cos, nsin = cos_ref[...], -sin_ref[...]
        for s in range(A_WIDTH // LANES):
            sl = slice(s * LANES, (s + 1) * LANES)
            dproj_ref[:, sl] = _rope_slab(dq_ref[:, sl].astype(F32), cos, nsin).astype(BF16)
        dproj_ref[:, QKV_K:QKV_V] = _rope_slab(dkv_ref[:, 0:LANES].astype(F32), cos, nsin).astype(BF16)
        dproj_ref[:, QKV_V:QKV_W] = dkv_ref[:, LANES:]

        def put(name, val):
            a, b = SEG[name]
            dproj_ref[:, a:b] = val

        put("za", dza_ref[...])
        put("qb", dqb_ref[...])
        put("kb", dkb_ref[...])
        put("vb", dvb_ref[...])
        put("zb", dzb_ref[...])
        put("ga", dga_ref[...])
        put("gb", dgb_ref[...])
        du = dla_ref[...] * (1.0 / B_GATE_TEMP) * _sigmoid(-u_ref[...])
        aba_ref[...] += jnp.sum(du, axis=0, keepdims=True)
        du_b = du.astype(BF16)
        awa_ref[...] += _dot_tn(alr_ref[...], du_b)
        put("alr", _dot_nt(du_b, wa_ref[...]).astype(BF16))

        for j in range(N_DEV):
            col = (j % 2) * SHARD_PAD
            for a, b in _shard_pad_cols(j):
                dsh_ref[j // 2, :, col:col + b - a] = dproj_ref[:, a:b]
                col += b - a
            dsh_ref[j // 2, :, col:(j % 2 + 1) * SHARD_PAD] = jnp.zeros((tm, SHARD_PAD - SHARD_IN), BF16)

        dh = _dot(dproj_ref[...], wt_ref[...])
        x = x_ref[...]
        r = lax.rsqrt(jnp.mean(x * x, axis=-1, keepdims=True) + NORM_EPS)
        nrm = x * r
        agin_ref[...] += jnp.sum(dh * nrm, axis=0, keepdims=True)
        dn = dh * g_ref[...]
        dx_ref[...] = dxres_ref[...] + r * (dn - nrm * jnp.mean(dn * nrm, axis=-1, keepdims=True))

        @pl.when(pl.program_id(0) == last)
        def _():
            small_ref[...] = jnp.zeros_like(small_ref)
            _put_rows(small_ref, SMALL_G_IN - base, agin_ref[...])
            _put_rows(small_ref, SMALL_B_ALPHA - base, aba_ref[...])
            for half in range(B_KEY_WIDTH // LANES):
                r0 = SMALL_W_ALPHA - base + half * B_GATE_RANK
                small_ref[r0:r0 + B_GATE_RANK, :] = awa_ref[0:B_GATE_RANK, half * LANES:(half + 1) * LANES]

    def rows(w):
        return pl.BlockSpec((tm, w), lambda i: (i, 0))

    names = ["dq", "dkv", "dza", "dqb", "dkb", "dvb", "dzb", "dla", "u", "alr", "dga", "dgb"]
    return pl.pallas_call(
        body, name="in_proj_bwd", grid=(T // tm,),
        in_specs=[rows(D_MODEL), rows(D_MODEL), rows(LANES), rows(LANES), _const_spec((1, D_MODEL)),
                  _const_spec((D_IN_PAD, D_MODEL)), _const_spec((RANK_PAD, B_KEY_WIDTH))]
                 + [rows(parts[n].shape[1]) for n in names],
        out_specs=[rows(D_MODEL), pl.BlockSpec((N_CHIPS, tm, 2 * SHARD_PAD), lambda i: (0, i, 0)),
                   pl.BlockSpec((SMALL_ROWS - base, LANES), lambda i: (0, 0))],
        out_shape=[jax.ShapeDtypeStruct((T, D_MODEL), F32), jax.ShapeDtypeStruct((N_CHIPS, T, 2 * SHARD_PAD), BF16),
                   jax.ShapeDtypeStruct((SMALL_ROWS - base, LANES), F32)],
        scratch_shapes=[pltpu.VMEM((tm, D_IN_PAD), BF16), pltpu.VMEM((1, D_MODEL), F32), pltpu.VMEM((1, B_KEY_WIDTH), F32),
                        pltpu.VMEM((RANK_PAD, B_KEY_WIDTH), F32)],
        compiler_params=_params("arbitrary"),
    )(x2, dxres, cosf, sinf, g_in, wt_pad, wa_pad, *[parts[n] for n in names])


FLIPS = [(dx, dy, dc) for dx in (0, 1) for dy in (0, 1) for dc in (0, 1)][1:]


def _my_place():
    return lax.axis_index("x"), lax.axis_index("y"), lax.axis_index("c")


def _any_specs(n):
    return [pl.BlockSpec(memory_space=pl.ANY)] * n


def _all_gather(shards):
    n = len(shards)

    def body(*refs):
        ins, outs = refs[:n], refs[n:2 * n]
        send_sems, recv_sems, local_sems = refs[2 * n:]
        x, y, c = _my_place()
        me, sibling = (x, y, c), (x, y, 1 - c)
        chips = [(1 - x, y), (x, 1 - y), (1 - x, 1 - y)]

        def block(a, px, py, pc):
            return outs[a].at[4 * px + 2 * py + pc]

        def copy(a, k, blk, to, src=None):
            return pltpu.make_async_remote_copy(
                src_ref=block(a, *blk) if src is None else src, dst_ref=block(a, *blk),
                send_sem=send_sems.at[a, k], recv_sem=recv_sems.at[a, k], device_id=to, device_id_type=MESH)

        mine = [pltpu.make_async_copy(ins[a], block(a, *me), local_sems.at[a]) for a in range(n)]
        for cp in mine:
            cp.start()
        first = []
        for a in range(n):
            first.append(copy(a, 0, me, sibling, src=ins[a]))
            first += [copy(a, 1 + j, me, (*chip, c), src=ins[a]) for j, chip in enumerate(chips)]
        for cp in first:
            cp.start()
        passed = []
        for j, chip in enumerate(chips):
            for a in range(n):
                copy(a, 1 + j, (*chip, c), me).wait_recv()
                fwd = copy(a, 4 + j, (*chip, c), sibling)
                fwd.start()
                passed.append(fwd)
        for a in range(n):
            copy(a, 0, sibling, me).wait_recv()
            for j, chip in enumerate(chips):
                copy(a, 4 + j, (*chip, 1 - c), me).wait_recv()
        for cp in first + passed:
            cp.wait_send()
        for cp in mine:
            cp.wait()

    return pl.pallas_call(
        body, name="gather_weights",
        in_specs=_any_specs(n), out_specs=_any_specs(n),
        out_shape=[jax.ShapeDtypeStruct((N_DEV, *s.shape), s.dtype) for s in shards],
        scratch_shapes=[pltpu.SemaphoreType.DMA((n, 7)), pltpu.SemaphoreType.DMA((n, 7)), pltpu.SemaphoreType.DMA((n,))],
    )(*shards)


def _w_in_grad_rs(h, dsh, chip_order, small):
    T = h.shape[0]
    tk = math.gcd(T, 2048)
    nk = T // tk
    chip_flips = [(1, 1), (1, 0), (0, 1)]
    n_steps = len(chip_flips) + 1

    def body(order_ref, h_ref, d_ref, s_ref, own_ref, recv_ref, sall_ref,
             acc_ref, stage_ref, send_sems, recv_sems, ssend_sems, srecv_sems, local_sem):
        i, kk = pl.program_id(0), pl.program_id(1)
        x, y, c = _my_place()
        my_dev = 4 * x + 2 * y + c

        def small_copy(r, slot):
            dx, dy, dc = FLIPS[r]
            return pltpu.make_async_remote_copy(
                src_ref=s_ref, dst_ref=sall_ref.at[slot], send_sem=ssend_sems.at[r], recv_sem=srecv_sems.at[r],
                device_id=(x ^ dx, y ^ dy, c ^ dc), device_id_type=MESH)

        keep_small = pltpu.make_async_copy(s_ref, sall_ref.at[my_dev], local_sem)

        def shard_copy(slot, dx, dy, dc):
            r = FLIPS.index((dx, dy, dc))
            return pltpu.make_async_remote_copy(
                src_ref=stage_ref.at[slot, c ^ dc], dst_ref=recv_ref.at[r], send_sem=send_sems.at[r],
                recv_sem=recv_sems.at[r], device_id=(x ^ dx, y ^ dy, c ^ dc), device_id_type=MESH)

        def stage(slot):
            stage_ref[slot, 0] = acc_ref[0:SHARD_PAD, :].astype(BF16)
            stage_ref[slot, 1] = acc_ref[SHARD_PAD:2 * SHARD_PAD, :].astype(BF16)

        @pl.when((i == 0) & (kk == 0))
        def _():
            keep_small.start()
            for r in range(len(FLIPS)):
                small_copy(r, my_dev).start()

        @pl.when(kk == 0)
        def _():
            acc_ref[...] = jnp.zeros_like(acc_ref)

        acc_ref[...] += _dot_tn(d_ref[...], h_ref[...])

        for t, (dx, dy) in enumerate(chip_flips):
            @pl.when((i == t) & (kk == nk - 1))
            def _(t=t, dx=dx, dy=dy):
                if t >= 2:
                    for dc in (0, 1):
                        shard_copy(t % 2, *chip_flips[t - 2], dc).wait_send()
                stage(t % 2)
                for dc in (0, 1):
                    shard_copy(t % 2, dx, dy, dc).start()

        @pl.when((i == n_steps - 1) & (kk == nk - 1))
        def _():
            for dc in (0, 1):
                shard_copy(1, *chip_flips[1], dc).wait_send()
            stage(1)
            shard_copy(1, 0, 0, 1).start()

            @pl.when(c == 0)
            def _():
                own_ref[...] = acc_ref[0:SHARD_PAD, :]

            @pl.when(c == 1)
            def _():
                own_ref[...] = acc_ref[SHARD_PAD:2 * SHARD_PAD, :]

            for dc in (0, 1):
                shard_copy(0, *chip_flips[2], dc).wait_send()
            shard_copy(1, 0, 0, 1).wait_send()
            for r, (dx, dy, dc) in enumerate(FLIPS):
                shard_copy(0, dx, dy, dc).wait_recv()
                small_copy(r, 4 * (x ^ dx) + 2 * (y ^ dy) + (c ^ dc)).wait_recv()
                small_copy(r, my_dev).wait_send()
            keep_small.wait()

    return pl.pallas_call(
        body, name="w_in_grad_rs",
        grid_spec=pltpu.PrefetchScalarGridSpec(
            num_scalar_prefetch=1, grid=(n_steps, nk),
            in_specs=[pl.BlockSpec((tk, D_MODEL), lambda i, kk, order: (kk, 0)),
                      pl.BlockSpec((None, tk, 2 * SHARD_PAD), lambda i, kk, order: (order[i], kk, 0)),
                      pl.BlockSpec(memory_space=pl.ANY)],
            out_specs=[pl.BlockSpec((SHARD_PAD, D_MODEL), lambda i, kk, order: (0, 0)),
                       pl.BlockSpec(memory_space=pl.ANY), pl.BlockSpec(memory_space=pl.ANY)],
            scratch_shapes=[pltpu.VMEM((2 * SHARD_PAD, D_MODEL), F32), pltpu.VMEM((2, 2, SHARD_PAD, D_MODEL), BF16),
                            pltpu.SemaphoreType.DMA((7,)), pltpu.SemaphoreType.DMA((7,)),
                            pltpu.SemaphoreType.DMA((7,)), pltpu.SemaphoreType.DMA((7,)), pltpu.SemaphoreType.DMA]),
        out_shape=[jax.ShapeDtypeStruct((SHARD_PAD, D_MODEL), F32),
                   jax.ShapeDtypeStruct((len(FLIPS), SHARD_PAD, D_MODEL), BF16),
                   jax.ShapeDtypeStruct((N_DEV, *small.shape), F32)],
        compiler_params=_params("arbitrary", "arbitrary"),
    )(chip_order, h, dsh, small)


def _adam_math(w, g, m, v):
    m_new = ADAM_B1 * m + (1.0 - ADAM_B1) * g
    v_new = ADAM_B2 * v + (1.0 - ADAM_B2) * (g * g)
    m_hat = m_new / (1.0 - ADAM_B1 ** ADAM_STEP)
    v_hat = v_new / (1.0 - ADAM_B2 ** ADAM_STEP)
    delta = -ADAM_LR * (m_hat / (jnp.sqrt(v_hat) + ADAM_EPS) + ADAM_WD * w)
    return delta, m_new, v_new


def _adam_big(name, own, own_idx, recv, w, m, v):
    rw, cw = w.shape
    rp = own.shape[1]
    steps = 8
    by_cols = rp != rw
    blk_w = (rw, cw // steps) if by_cols else (rw // steps, cw)
    blk_g = (rp, cw // steps) if by_cols else (rw // steps, cw)
    at = (lambda i: (0, i)) if by_cols else (lambda i: (i, 0))

    def body(idx_ref, o_ref, r_ref, w_ref, m_ref, v_ref, g_ref, d_ref, mo_ref, vo_ref):
        g = o_ref[...].astype(F32)
        for r in range(len(FLIPS)):
            g = g + r_ref[r].astype(F32)
        g = g[0:blk_w[0], :]
        g_ref[...] = g
        d_ref[...], mo_ref[...], vo_ref[...] = _adam_math(w_ref[...], g, m_ref[...], v_ref[...])

    spec = pl.BlockSpec(blk_w, lambda i, idx_ref: at(i))
    return pl.pallas_call(
        body, name=name,
        grid_spec=pltpu.PrefetchScalarGridSpec(
            num_scalar_prefetch=1, grid=(steps,),
            in_specs=[pl.BlockSpec((None, *blk_g), lambda i, idx_ref: (idx_ref[0], *at(i))),
                      pl.BlockSpec((len(FLIPS), *blk_g), lambda i, idx_ref: (0, *at(i))), spec, spec, spec],
            out_specs=[spec] * 4),
        out_shape=[jax.ShapeDtypeStruct((rw, cw), F32)] * 4,
        compiler_params=_params("parallel"),
    )(own_idx, own, recv, w, m, v)


def _adam_small(small_all, params):
    flat = [a for triple in params for a in triple]
    n_par = len(params)

    def body(s_ref, *refs):
        ins, outs, loss_ref = refs[:3 * n_par], refs[3 * n_par:-1], refs[-1]
        g_slab = s_ref[0]
        for dev in range(1, N_DEV):
            g_slab = g_slab + s_ref[dev]
        loss_ref[...] = g_slab[SMALL_LOSS:SMALL_LOSS + 1, :]
        dev = 4 * lax.axis_index("x") + 2 * lax.axis_index("y") + lax.axis_index("c")
        alpha_full = jnp.concatenate([g_slab[SMALL_W_ALPHA + half * B_GATE_RANK:SMALL_W_ALPHA + (half + 1) * B_GATE_RANK]
                                      for half in range(B_KEY_WIDTH // LANES)], axis=1)
        alpha_mine = pltpu.roll(alpha_full, (B_KEY_WIDTH - dev * SHARD_ALPHA) % B_KEY_WIDTH, 1)[:, 0:SHARD_ALPHA]
        grads = [_take_rows(g_slab, SMALL_G_IN, D_MODEL // LANES), _take_rows(g_slab, SMALL_G_FINAL, D_MODEL // LANES),
                 _take_rows(g_slab, SMALL_G_GLA, B_WIDTH // LANES), _take_rows(g_slab, SMALL_B_ALPHA, B_KEY_WIDTH // LANES),
                 g_slab[SMALL_SINKS:SMALL_SINKS + 1, 0:A_HEADS], alpha_mine]
        for i, g in enumerate(grads):
            w_ref, m_ref, v_ref = ins[3 * i:3 * i + 3]
            delta, m_new, v_new = _adam_math(w_ref[...], g, m_ref[...], v_ref[...])
            outs[4 * i][...] = g
            outs[4 * i + 1][...] = delta
            outs[4 * i + 2][...] = m_new
            outs[4 * i + 3][...] = v_new

    res = pl.pallas_call(
        body, name="adam_small",
        out_shape=[jax.ShapeDtypeStruct(t[0].shape, F32) for t in params for _ in range(4)]
                  + [jax.ShapeDtypeStruct((1, LANES), F32)],
    )(small_all, *flat)
    return [res[4 * i:4 * i + 4] for i in range(n_par)], res[-1]


def _local_step(x, positions, loss_target, g_in, wt_pad, wa_pad, b_alpha, sinks, g_gla, w_oa, w_ob, w_o, g_final,
                chip_order):
    B, S, _ = x.shape
    T = B * S
    x2 = x.reshape(T, D_MODEL)
    tgt2 = loss_target.reshape(T, D_MODEL)
    cosf, sinf = _rope_tables(positions.reshape(T, 1))
    f = _in_proj(x2, cosf, sinf, g_in, wt_pad, wa_pad, b_alpha)
    attn, lse = _attn_fwd(f["qkv"], sinks, B, S)
    o_gla, st_all = _gla_fwd(f["q"], f["k"], f["cum"], f["vb"], B, S)
    (dxres, dattn, dog, dza, dzb, dga, dgb, dw_o, dw_oa, dw_ob, small_a) = _merge(
        x2, tgt2, attn, f["za"], o_gla, f["zb"], f["ga"], f["gb"], w_oa, w_ob, w_o, g_gla, g_final)
    dq, dkv, dsink = _attn_bwd(f["qkv"], dattn, attn, lse, sinks, B, S)
    (dqb, dkb, dvb, dla), (rv_o, rv_oa, rv_ob) = _gla_bwd(f["q"], f["k"], f["cum"], f["vb"], dog, st_all, B, S,
                                                        [dw_o, dw_oa, dw_ob])
    parts = dict(dq=dq, dkv=dkv, dza=dza, dqb=dqb, dkb=dkb, dvb=dvb, dzb=dzb, dla=dla, u=f["u"], alr=f["alr"],
                 dga=dga, dgb=dgb)
    dx, dsh, small_c = _in_proj_bwd(x2, dxres, cosf, sinf, g_in, wt_pad, wa_pad, parts)
    small = jnp.concatenate([small_a, dsink, small_c], axis=0)
    own_in, rv_in, small_all = _w_in_grad_rs(f["h"], dsh, chip_order, small)
    return dict(grad_x=dx.reshape(B, S, D_MODEL), own_in=own_in, rv_in=rv_in,
                own_o=dw_o, rv_o=rv_o, own_oa=dw_oa, rv_oa=rv_oa, own_ob=dw_ob, rv_ob=rv_ob, small_all=small_all)


def _pad_rows_from_shards(shards):
    rows = jnp.concatenate([shards[j] for j in range(N_DEV)], axis=0)
    z = jnp.zeros((RANK_PAD - B_GATE_RANK, rows.shape[1]), rows.dtype)
    return jnp.concatenate([rows[:ALR_SRC + B_GATE_RANK], z, rows[ALR_SRC + B_GATE_RANK:]], axis=0)


def kernel(x, positions, g_in, w_in, w_alpha_up, b_alpha, attn_sinks, g_gla_norm, w_out_a, w_out_b, w_o, g_final, loss_target, m_g_in, m_w_in, m_w_alpha_up, m_b_alpha, m_attn_sinks, m_g_gla_norm, m_w_out_a, m_w_out_b, m_w_o, m_g_final, v_g_in, v_w_in, v_w_alpha_up, v_b_alpha, v_attn_sinks, v_g_gla_norm, v_w_out_a, v_w_out_b, v_w_o, v_g_final):
    xi, yi, ci = _my_place()
    dev_idx = (4 * xi + 2 * yi + ci).reshape(1).astype(jnp.int32)
    chip = 2 * xi + yi
    chip_order = jnp.stack([chip ^ 3, chip ^ 2, chip ^ 1, chip]).astype(jnp.int32)

    g_win, g_woa, g_wob, g_wo, g_wa = _all_gather(
        [w_in[0].T.astype(BF16), w_out_a[0].astype(BF16), w_out_b[0].astype(BF16), w_o[0].astype(BF16),
         w_alpha_up[0].astype(BF16)])
    wt_pad = _pad_rows_from_shards(g_win)
    w_oa_full = jnp.concatenate([g_woa[j] for j in range(N_DEV)], axis=1)
    w_ob_full = jnp.concatenate([g_wob[j] for j in range(N_DEV)], axis=1)
    w_o_full = g_wo.reshape(D_MODEL, D_MODEL)
    wa_pad = jnp.pad(jnp.concatenate([g_wa[j] for j in range(N_DEV)], axis=1), ((0, RANK_PAD - B_GATE_RANK), (0, 0)))

    r = _local_step(x, positions, loss_target, g_in, wt_pad, wa_pad, b_alpha, attn_sinks[0], g_gla_norm, w_oa_full,
                    w_ob_full, w_o_full, g_final.reshape(1, D_MODEL), chip_order)

    first = jnp.zeros((1,), jnp.int32)
    big = [_adam_big("adam_w_in", r["own_in"][None], first, r["rv_in"], w_in[0].T, m_w_in[0].T, v_w_in[0].T),
           _adam_big("adam_w_out_a", r["own_oa"], dev_idx, r["rv_oa"], w_out_a[0], m_w_out_a[0], v_w_out_a[0]),
           _adam_big("adam_w_out_b", r["own_ob"], dev_idx, r["rv_ob"], w_out_b[0], m_w_out_b[0], v_w_out_b[0]),
           _adam_big("adam_w_o", r["own_o"], dev_idx, r["rv_o"], w_o[0], m_w_o[0], v_w_o[0])]
    row = lambda a: a.reshape(1, D_MODEL)
    big[0] = [a.T for a in big[0]]
    (s_in, s_final, s_gla, s_ba, s_sinks, s_wa), loss_row = _adam_small(r["small_all"], [
        (g_in, m_g_in, v_g_in), (row(g_final), row(m_g_final), row(v_g_final)),
        (g_gla_norm, m_g_gla_norm, v_g_gla_norm), (b_alpha, m_b_alpha, v_b_alpha),
        (attn_sinks, m_attn_sinks, v_attn_sinks), (w_alpha_up[0], m_w_alpha_up[0], v_w_alpha_up[0])])

    def group(i):
        return (s_in[i], big[0][i][None], s_wa[i][None], s_ba[i], s_sinks[i], s_gla[i], big[1][i][None], big[2][i][None],
                big[3][i][None], s_final[i].reshape(D_MODEL))

    return (loss_row[0, 0], r["grad_x"], *group(0), *group(1), *group(2), *group(3))
```

```python
import functools
import math

import numpy as np
import jax
import jax.numpy as jnp
from jax import lax
from jax.experimental import pallas as pl
from jax.experimental.pallas import tpu as pltpu

F32 = jnp.float32
BF16 = jnp.bfloat16
MESH = pl.DeviceIdType.MESH

D_MODEL = 1024
A_HEADS, A_KV_HEADS, A_HEAD_DIM = 8, 2, 64
A_WIDTH, A_KV_WIDTH = 512, 128
WINDOW = 128
ROPE_THETA = 500000.0
ROPE_DIM = 16
B_HEADS, B_KEY_DIM, B_VAL_DIM = 4, 64, 128
B_KEY_WIDTH, B_WIDTH = 256, 512
B_GATE_RANK = 16
B_GATE_TEMP = 16.0
B_CHUNK = 64
NORM_EPS = 1e-6
NEG_BIG = -1e30
D_IN = 4880
N_DEV = 8
N_CHIPS = 4
ADAM_LR, ADAM_B1, ADAM_B2, ADAM_EPS, ADAM_WD, ADAM_STEP = 0.001, 0.9, 0.999, 1e-08, 0.01, 10

LANES = 128
V7X_VMEM_LIMIT = 56 * 1024 * 1024

RANK_PAD = LANES
SEG = {}
_off = 0
for _name, _w in (("qa", 512), ("ka", 128), ("va", 128), ("za", 512), ("qb", 256), ("kb", 256),
                  ("vb", 512), ("zb", 512), ("alr", RANK_PAD), ("ga", 1024), ("gb", 1024)):
    SEG[_name] = (_off, _off + _w)
    _off += _w
D_IN_PAD = _off
ALR_SRC = SEG["alr"][0]
QKV_K, QKV_V, QKV_W = SEG["ka"][0], SEG["va"][0], SEG["va"][1]

SHARD_IN = D_IN // N_DEV
SHARD_PAD = 640
SHARD_OUT = D_MODEL // N_DEV
SHARD_ALPHA = B_KEY_WIDTH // N_DEV

SMALL_G_FINAL, SMALL_G_GLA, SMALL_LOSS, SMALL_SINKS, SMALL_G_IN, SMALL_B_ALPHA, SMALL_W_ALPHA = 0, 8, 12, 16, 24, 32, 40
SMALL_ROWS = 72


def _dot(a, b):
    return jnp.dot(a, b, preferred_element_type=F32)


def _dot_nt(a, b):
    return lax.dot_general(a, b, (((1,), (1,)), ((), ())), preferred_element_type=F32)


def _dot_tn(a, b):
    return lax.dot_general(a, b, (((0,), (0,)), ((), ())), preferred_element_type=F32)


def _sigmoid(z):
    return 1.0 / (1.0 + jnp.exp(-z))


def _params(*sem):
    return pltpu.CompilerParams(dimension_semantics=sem, vmem_limit_bytes=V7X_VMEM_LIMIT)


def _const_spec(shape):
    nd = len(shape)
    return pl.BlockSpec(shape, lambda *_: (0,) * nd, pipeline_mode=pl.Buffered(1))


def _lane_iota(shape):
    return lax.broadcasted_iota(jnp.int32, shape, 1)


def _row_iota(shape):
    return lax.broadcasted_iota(jnp.int32, shape, 0)


def _split3(v):
    hi = v.astype(BF16)
    r1 = v - hi.astype(F32)
    mid = r1.astype(BF16)
    lo = (r1 - mid.astype(F32)).astype(BF16)
    return hi, mid, lo


def _put_rows(ref, row0, vec):
    for r in range(vec.shape[1] // LANES):
        ref[row0 + r:row0 + r + 1, :] = vec[:, r * LANES:(r + 1) * LANES]


def _take_rows(slab, row0, n):
    return jnp.concatenate([slab[row0 + r:row0 + r + 1, :] for r in range(n)], axis=1)


def _rope_lane_constants():
    half = ROPE_DIM // 2
    inv_freq = np.exp(-math.log(ROPE_THETA) * np.arange(half, dtype=np.float32) * np.float32(2.0 / ROPE_DIM)).astype(np.float32)
    lane = np.arange(LANES)
    j = lane % A_HEAD_DIM
    invf = np.where(j < ROPE_DIM, inv_freq[j % half], 0.0).astype(np.float32)
    sign = np.where(j < half, -1.0, np.where(j < ROPE_DIM, 1.0, 0.0)).astype(np.float32)
    return jnp.asarray(invf)[None, :], jnp.asarray(sign)[None, :]


def _rope_tables(pos_col):
    T = pos_col.shape[0]
    tm = math.gcd(T, 1024)
    invf, sign = _rope_lane_constants()

    def body(pos_ref, invf_ref, sign_ref, cos_ref, sin_ref):
        ang = pos_ref[...].astype(F32) * invf_ref[...]
        cos_ref[...] = jnp.cos(ang)
        sin_ref[...] = jnp.sin(ang) * sign_ref[...]

    return pl.pallas_call(
        body, name="rope_tables", grid=(T // tm,),
        in_specs=[pl.BlockSpec((tm, 1), lambda i: (i, 0)), _const_spec((1, LANES)), _const_spec((1, LANES))],
        out_specs=[pl.BlockSpec((tm, LANES), lambda i: (i, 0))] * 2,
        out_shape=[jax.ShapeDtypeStruct((T, LANES), F32)] * 2,
        compiler_params=_params("parallel"),
    )(pos_col, invf, sign)


def _rope_slab(t, cos, sin_signed):
    first = (_lane_iota(t.shape) % A_HEAD_DIM) < (ROPE_DIM // 2)
    partner = jnp.where(first, pltpu.roll(t, LANES - ROPE_DIM // 2, 1), pltpu.roll(t, ROPE_DIM // 2, 1))
    return t * cos + partner * sin_signed


def _shard_pad_cols(j):
    cut = ALR_SRC + B_GATE_RANK
    shift = RANK_PAD - B_GATE_RANK
    a, b = j * SHARD_IN, (j + 1) * SHARD_IN
    if b <= cut:
        return [(a, b)]
    if a >= cut:
        return [(a + shift, b + shift)]
    return [(a, cut), (cut + shift, b + shift)]


def _in_proj(x2, cosf, sinf, g_in, wt_sh, wa_pad, b_alpha):
    T = x2.shape[0]
    tm = 256

    def body(x_ref, cos_ref, sin_ref, g_ref, wsh_ref, wa_ref, ba_ref,
             h_ref, qkv_ref, za_ref, q_ref, k_ref, vb_ref, zb_ref, alr_ref, u_ref, cum_ref, ga_ref, gb_ref, wt_ref):
        @pl.when(pl.program_id(0) == 0)
        def _():
            for j in range(N_DEV):
                src = j * SHARD_PAD
                for a, b in _shard_pad_cols(j):
                    wt_ref[a:b, :] = wsh_ref[src:src + b - a, :]
                    src += b - a
            a, b = SEG["alr"]
            wt_ref[a + B_GATE_RANK:b, :] = jnp.zeros((RANK_PAD - B_GATE_RANK, D_MODEL), BF16)

        x = x_ref[...]
        r = lax.rsqrt(jnp.mean(x * x, axis=-1, keepdims=True) + NORM_EPS)
        h = (x * r * g_ref[...]).astype(BF16)
        h_ref[...] = h

        def seg(name):
            a, b = SEG[name]
            return _dot_nt(h, wt_ref[a:b, :])

        cos, sin = cos_ref[...], sin_ref[...]
        qa = seg("qa")
        for s in range(A_WIDTH // LANES):
            qkv_ref[:, s * LANES:(s + 1) * LANES] = _rope_slab(qa[:, s * LANES:(s + 1) * LANES], cos, sin).astype(BF16)
        qkv_ref[:, QKV_K:QKV_V] = _rope_slab(seg("ka"), cos, sin).astype(BF16)
        qkv_ref[:, QKV_V:QKV_W] = seg("va").astype(BF16)
        za_ref[...] = seg("za")
        q_ref[...] = seg("qb")
        k_ref[...] = seg("kb")
        vb_ref[...] = seg("vb").astype(BF16)
        zb_ref[...] = seg("zb")
        ga_ref[...] = seg("ga")
        gb_ref[...] = seg("gb")
        alr = seg("alr").astype(BF16)
        alr_ref[...] = alr
        u = _dot(alr, wa_ref[...]) + ba_ref[...]
        u_ref[...] = u
        log_a = (jnp.minimum(u, 0.0) - jnp.log(1.0 + jnp.exp(-jnp.abs(u)))) * (1.0 / B_GATE_TEMP)
        row, col = _row_iota((tm, tm)), _lane_iota((tm, tm))
        tri = ((row // B_CHUNK == col // B_CHUNK) & (col <= row)).astype(BF16)
        hi, mid, lo = _split3(log_a)
        cum_ref[...] = _dot(tri, hi) + _dot(tri, mid) + _dot(tri, lo)

    def rows(w):
        return pl.BlockSpec((tm, w), lambda i: (i, 0))

    outs = [("h", D_MODEL, BF16), ("qkv", QKV_W, BF16), ("za", A_WIDTH, F32), ("q", B_KEY_WIDTH, F32),
            ("k", B_KEY_WIDTH, F32), ("vb", B_WIDTH, BF16), ("zb", B_WIDTH, F32), ("alr", RANK_PAD, BF16),
            ("u", B_KEY_WIDTH, F32), ("cum", B_KEY_WIDTH, F32), ("ga", D_MODEL, F32), ("gb", D_MODEL, F32)]
    res = pl.pallas_call(
        body, name="in_proj", grid=(T // tm,),
        in_specs=[rows(D_MODEL), rows(LANES), rows(LANES), _const_spec((1, D_MODEL)),
                  _const_spec((N_DEV * SHARD_PAD, D_MODEL)), _const_spec((RANK_PAD, B_KEY_WIDTH)),
                  _const_spec((1, B_KEY_WIDTH))],
        out_specs=[rows(w) for _, w, _ in outs] + [pl.BlockSpec((D_IN_PAD, D_MODEL), lambda i: (0, 0))],
        out_shape=[jax.ShapeDtypeStruct((T, w), dt) for _, w, dt in outs]
                  + [jax.ShapeDtypeStruct((D_IN_PAD, D_MODEL), BF16)],
        compiler_params=_params("arbitrary"),
    )(x2, cosf, sinf, g_in, wt_sh, wa_pad, b_alpha)
    return dict(zip([n for n, _, _ in outs] + ["wt_pad"], res))


def _dup_kv_head(t, g):
    tf = t.astype(F32)
    keep = (_lane_iota(tf.shape) < A_HEAD_DIM) == (g == 0)
    return jnp.where(keep, tf, pltpu.roll(tf, A_HEAD_DIM, 1)).astype(BF16)


def _stack_heads(t):
    lo = _lane_iota(t.shape) < A_HEAD_DIM
    zero = jnp.zeros_like(t)
    return jnp.concatenate([jnp.where(lo, t, zero), jnp.where(lo, zero, t)], axis=0)


def _band_mask(n):
    qi = _row_iota((2 * WINDOW, 2 * WINDOW)) % WINDOW
    kj = _lane_iota((2 * WINDOW, 2 * WINDOW)) - WINDOW
    return (kj <= qi) & (qi - kj < WINDOW) & ((n > 0) | (kj >= 0))


def _attn_fwd(qkv, sinks, B, S):
    T = B * S
    nb = S // WINDOW
    scale = A_HEAD_DIM ** -0.5

    def body(sink_ref, q_ref, kc_ref, vc_ref, kp_ref, vp_ref, o_ref, lse_ref):
        n = pl.program_id(1)
        valid = _band_mask(n)
        k = jnp.concatenate([kp_ref[...], kc_ref[...]], axis=0)
        v = jnp.concatenate([vp_ref[...], vc_ref[...]], axis=0)
        top = _row_iota((2 * WINDOW, 1)) < WINDOW
        lo = _lane_iota((WINDOW, LANES)) < A_HEAD_DIM
        lane = _lane_iota((WINDOW, LANES))
        lse_tile = jnp.zeros((WINDOW, LANES), F32)
        for g in range(A_KV_HEADS):
            kd, vd = _dup_kv_head(k, g), _dup_kv_head(v, g)
            for p in (2 * g, 2 * g + 1):
                qs = _stack_heads(q_ref[:, p * LANES:(p + 1) * LANES])
                s = jnp.where(valid, _dot_nt(qs, kd) * scale, NEG_BIG)
                sink = jnp.where(top, sink_ref[2 * p], sink_ref[2 * p + 1])
                m = jnp.maximum(jnp.max(s, axis=-1, keepdims=True), sink)
                e = jnp.exp(s - m)
                den = jnp.sum(e, axis=-1, keepdims=True) + jnp.exp(sink - m)
                o = _dot((e * (1.0 / den)).astype(BF16), vd)
                o_ref[:, p * LANES:(p + 1) * LANES] = jnp.where(lo, o[:WINDOW], o[WINDOW:])
                lse = m + jnp.log(den)
                lse_tile = jnp.where(lane == 2 * p, lse[:WINDOW], lse_tile)
                lse_tile = jnp.where(lane == 2 * p + 1, lse[WINDOW:], lse_tile)
        lse_ref[...] = lse_tile

    def cur(col, w):
        return pl.BlockSpec((WINDOW, w), lambda b, n: (b * nb + n, col))

    def prev(col):
        return pl.BlockSpec((WINDOW, LANES), lambda b, n: (b * nb + jnp.maximum(n - 1, 0), col))

    kcol, vcol = QKV_K // LANES, QKV_V // LANES
    return pl.pallas_call(
        body, name="attn_fwd", grid=(B, nb),
        in_specs=[pl.BlockSpec(memory_space=pltpu.SMEM), cur(0, A_WIDTH), cur(kcol, LANES), cur(vcol, LANES),
                  prev(kcol), prev(vcol)],
        out_specs=[cur(0, A_WIDTH), cur(0, LANES)],
        out_shape=[jax.ShapeDtypeStruct((T, A_WIDTH), F32), jax.ShapeDtypeStruct((T, LANES), F32)],
        compiler_params=_params("parallel", "parallel"),
    )(sinks, qkv, qkv, qkv, qkv, qkv)


def _attn_bwd(qkv, do, out, lse, sinks, B, S):
    T = B * S
    nb = S // WINDOW
    scale = A_HEAD_DIM ** -0.5

    def body(sink_ref, q_ref, kc_ref, vc_ref, kp_ref, vp_ref, do_ref, out_ref, lse_ref,
             dq_ref, dkv_ref, dsink_ref, carry_ref):
        b, n = pl.program_id(0), pl.program_id(1)
        active = n < nb
        nq = jnp.minimum(n, nb - 1)
        valid = _band_mask(nq)

        @pl.when((b == 0) & (n == 0))
        def _():
            dsink_ref[...] = jnp.zeros_like(dsink_ref)

        k = jnp.concatenate([kp_ref[...], kc_ref[...]], axis=0)
        v = jnp.concatenate([vp_ref[...], vc_ref[...]], axis=0)
        top = _row_iota((2 * WINDOW, 1)) < WINDOW
        lane = _lane_iota((WINDOW, LANES))
        lo = lane < A_HEAD_DIM
        lane2 = _lane_iota((2 * WINDOW, LANES))
        lse_tile = lse_ref[...]
        dk_tot = jnp.zeros((2 * WINDOW, LANES), F32)
        dv_tot = jnp.zeros((2 * WINDOW, LANES), F32)
        dsink_row = jnp.zeros((1, LANES), F32)
        for g in range(A_KV_HEADS):
            kd, vd = _dup_kv_head(k, g), _dup_kv_head(v, g)
            dk_acc = jnp.zeros((2 * WINDOW, LANES), F32)
            dv_acc = jnp.zeros((2 * WINDOW, LANES), F32)
            for p in (2 * g, 2 * g + 1):
                sl = slice(p * LANES, (p + 1) * LANES)
                qs = _stack_heads(q_ref[:, sl])
                dos = _stack_heads(do_ref[:, sl])
                s = jnp.where(valid, _dot_nt(qs, kd) * scale, NEG_BIG)
                lse0 = jnp.sum(jnp.where(lane == 2 * p, lse_tile, 0.0), axis=-1, keepdims=True)
                lse1 = jnp.sum(jnp.where(lane == 2 * p + 1, lse_tile, 0.0), axis=-1, keepdims=True)
                lse_col = jnp.concatenate([lse0, lse1], axis=0)
                prob = jnp.exp(s - lse_col)
                prod = do_ref[:, sl].astype(F32) * out_ref[:, sl]
                d0 = jnp.sum(jnp.where(lo, prod, 0.0), axis=-1, keepdims=True)
                d1 = jnp.sum(jnp.where(lo, 0.0, prod), axis=-1, keepdims=True)
                delta = jnp.concatenate([d0, d1], axis=0)
                dp = _dot_nt(dos, vd)
                ds = (prob * (dp - delta) * scale).astype(BF16)
                dq = _dot(ds, kd)
                dq_ref[:, sl] = jnp.where(lo, dq[:WINDOW], dq[WINDOW:]).astype(BF16)
                dk_acc += _dot_tn(ds, qs)
                dv_acc += _dot_tn(prob.astype(BF16), dos)
                sink = jnp.where(top, sink_ref[2 * p], sink_ref[2 * p + 1])
                w = -jnp.exp(sink - lse_col) * delta
                w0 = jnp.sum(w[:WINDOW], axis=0, keepdims=True)
                w1 = jnp.sum(w[WINDOW:], axis=0, keepdims=True)
                lane1 = _lane_iota((1, LANES))
                dsink_row += jnp.where(lane1 == 2 * p, w0, 0.0) + jnp.where(lane1 == 2 * p + 1, w1, 0.0)
            mine = (lane2 < A_HEAD_DIM) == (g == 0)
            dk_tot = jnp.where(mine, dk_acc + pltpu.roll(dk_acc, A_HEAD_DIM, 1), dk_tot)
            dv_tot = jnp.where(mine, dv_acc + pltpu.roll(dv_acc, A_HEAD_DIM, 1), dv_tot)
        gate = jnp.where(active, 1.0, 0.0)
        dsink_ref[0:1, :] += dsink_row * gate
        dkv_ref[:, 0:LANES] = (carry_ref[:, 0:LANES] + dk_tot[:WINDOW] * gate).astype(BF16)
        dkv_ref[:, LANES:] = (carry_ref[:, LANES:] + dv_tot[:WINDOW] * gate).astype(BF16)
        carry_ref[:, 0:LANES] = dk_tot[WINDOW:]
        carry_ref[:, LANES:] = dv_tot[WINDOW:]

    def cur(col, w):
        return pl.BlockSpec((WINDOW, w), lambda b, n: (b * nb + jnp.minimum(n, nb - 1), col))

    def prev(col):
        return pl.BlockSpec((WINDOW, LANES), lambda b, n: (b * nb + jnp.maximum(jnp.minimum(n, nb - 1) - 1, 0), col))

    lag = pl.BlockSpec((WINDOW, 2 * LANES), lambda b, n: (b * nb + jnp.maximum(n - 1, 0), 0))
    kcol, vcol = QKV_K // LANES, QKV_V // LANES
    return pl.pallas_call(
        body, name="attn_bwd", grid=(B, nb + 1),
        in_specs=[pl.BlockSpec(memory_space=pltpu.SMEM), cur(0, A_WIDTH), cur(kcol, LANES), cur(vcol, LANES),
                  prev(kcol), prev(vcol), cur(0, A_WIDTH), cur(0, A_WIDTH), cur(0, LANES)],
        out_specs=[cur(0, A_WIDTH), lag, pl.BlockSpec((8, LANES), lambda b, n: (0, 0))],
        out_shape=[jax.ShapeDtypeStruct((T, A_WIDTH), BF16), jax.ShapeDtypeStruct((T, 2 * LANES), BF16),
                   jax.ShapeDtypeStruct((8, LANES), F32)],
        scratch_shapes=[pltpu.VMEM((WINDOW, 2 * LANES), F32)],
        compiler_params=_params("arbitrary", "arbitrary"),
    )(sinks, qkv, qkv, qkv, qkv, qkv, do, out, lse)


GLA_TILE = 256
CHUNKS_PER_TILE = GLA_TILE // B_CHUNK


def _gla_factors(q_ref, k_ref, cum_ref):
    scale = B_KEY_DIM ** -0.5
    cum = cum_ref[...]
    shape = (B_CHUNK, B_KEY_WIDTH)
    last = jnp.concatenate([jnp.broadcast_to(cum_ref[pl.ds(c * B_CHUNK + B_CHUNK - 1, 1), :], shape)
                            for c in range(CHUNKS_PER_TILE)], axis=0)
    mid = jnp.concatenate([jnp.broadcast_to(cum_ref[pl.ds(c * B_CHUNK + B_CHUNK // 2 - 1, 1), :], shape)
                           for c in range(CHUNKS_PER_TILE)], axis=0)
    e_qm, e_km, e_qe, e_kd = jnp.exp(cum - mid), jnp.exp(mid - cum), jnp.exp(cum), jnp.exp(last - cum)
    qs = q_ref[...] * scale
    k = k_ref[...]
    return qs, k, (e_qm, e_km, e_qe, e_kd)


def _head_mask(shape, h):
    return (_lane_iota(shape) // B_KEY_DIM) == h


def _stack_masked(t):
    return jnp.concatenate([jnp.where(_head_mask(t.shape, h), t, 0.0) for h in range(B_HEADS)], axis=0).astype(BF16)


def _select_heads(t):
    shape = (B_CHUNK, B_KEY_WIDTH)
    out = jnp.zeros(shape, F32)
    for h in range(B_HEADS):
        out = jnp.where(_head_mask(shape, h), t[h * B_CHUNK:(h + 1) * B_CHUNK], out)
    return out


def _select_state(t):
    shape = (B_VAL_DIM, B_KEY_WIDTH)
    out = jnp.zeros(shape, F32)
    for h in range(B_HEADS):
        out = jnp.where(_head_mask(shape, h), t[h * B_VAL_DIM:(h + 1) * B_VAL_DIM], out)
    return out


def _rows_by_head(t):
    return jnp.concatenate([t[:, h * B_VAL_DIM:(h + 1) * B_VAL_DIM] for h in range(B_HEADS)], axis=0)


def _intra_mask():
    i, j = _row_iota((GLA_TILE, GLA_TILE)), _lane_iota((GLA_TILE, GLA_TILE))
    return (i // B_CHUNK == j // B_CHUNK) & (j <= i)


def _pair_stack(t, p):
    slab = t[:, p * LANES:(p + 1) * LANES]
    lo = _lane_iota(slab.shape) < B_KEY_DIM
    return jnp.concatenate([jnp.where(lo, slab, 0.0), jnp.where(lo, 0.0, slab)], axis=0).astype(BF16)


def _gla_fwd(q, k, cum, vb, B, S):
    T = B * S
    nt = S // GLA_TILE

    def body(q_ref, k_ref, cum_ref, v_ref, o_ref, st_all_ref, st_ref):
        @pl.when(pl.program_id(1) == 0)
        def _():
            st_ref[...] = jnp.zeros_like(st_ref)

        qs, kk, (e_qm, e_km, e_qe, e_kd) = _gla_factors(q_ref, k_ref, cum_ref)
        qm, km, qe, kd = qs * e_qm, kk * e_km, qs * e_qe, (kk * e_kd).astype(BF16)
        mask = _intra_mask()
        intra = []
        for p in range(B_HEADS // 2):
            a = _dot_nt(_pair_stack(qm, p), km[:, p * LANES:(p + 1) * LANES].astype(BF16))
            for hh in range(2):
                h = 2 * p + hh
                att = jnp.where(mask, a[hh * GLA_TILE:(hh + 1) * GLA_TILE], 0.0).astype(BF16)
                intra.append(_dot(att, v_ref[:, h * B_VAL_DIM:(h + 1) * B_VAL_DIM]))
        inter = []
        for c in range(CHUNKS_PER_TILE):
            rows = slice(c * B_CHUNK, (c + 1) * B_CHUNK)
            st = st_ref[...]
            st_all_ref[c] = st
            inter.append(_dot_nt(_stack_masked(qe[rows]), st.astype(BF16)))
            inc = _select_state(_dot_tn(v_ref[rows, :], kd[rows]))
            decay = jnp.exp(cum_ref[pl.ds(c * B_CHUNK + B_CHUNK - 1, 1), :])
            st_ref[...] = st * decay + inc
        for h in range(B_HEADS):
            oi = jnp.concatenate([inter[c][h * B_CHUNK:(h + 1) * B_CHUNK] for c in range(CHUNKS_PER_TILE)], axis=0)
            o_ref[:, h * B_VAL_DIM:(h + 1) * B_VAL_DIM] = intra[h] + oi

    def rows(w):
        return pl.BlockSpec((GLA_TILE, w), lambda b, t: (b * nt + t, 0))

    return pl.pallas_call(
        body, name="gla_fwd", grid=(B, nt),
        in_specs=[rows(B_KEY_WIDTH), rows(B_KEY_WIDTH), rows(B_KEY_WIDTH), rows(B_WIDTH)],
        out_specs=[rows(B_WIDTH),
                   pl.BlockSpec((CHUNKS_PER_TILE, B_VAL_DIM, B_KEY_WIDTH), lambda b, t: (b * nt + t, 0, 0))],
        out_shape=[jax.ShapeDtypeStruct((T, B_WIDTH), F32),
                   jax.ShapeDtypeStruct((T // B_CHUNK, B_VAL_DIM, B_KEY_WIDTH), F32)],
        scratch_shapes=[pltpu.VMEM((B_VAL_DIM, B_KEY_WIDTH), F32)],
        compiler_params=_params("arbitrary", "arbitrary"),
    )(q, k, cum, vb)


def _gla_bwd(q, k, cum, vb, do, st_all, B, S, wgrads):
    T = B * S
    nt = S // GLA_TILE
    scale = B_KEY_DIM ** -0.5
    nw = len(wgrads)

    def body(q_ref, k_ref, cum_ref, v_ref, do_ref, st_all_ref, *rest):
        g_refs, (dq_ref, dk_ref, dv_ref, dla_ref) = rest[:nw], rest[nw:nw + 4]
        rv_refs, (dst_ref, send_sems, recv_sems) = rest[nw + 4:2 * nw + 4], rest[2 * nw + 4:]
        x, y, c = _my_place()

        def wcopy(a, r):
            dx, dy, dc = FLIPS[r]
            return pltpu.make_async_remote_copy(
                src_ref=g_refs[a].at[4 * (x ^ dx) + 2 * (y ^ dy) + (c ^ dc)], dst_ref=rv_refs[a].at[r],
                send_sem=send_sems.at[a, r], recv_sem=recv_sems.at[a, r],
                device_id=(x ^ dx, y ^ dy, c ^ dc), device_id_type=MESH)

        @pl.when((pl.program_id(0) == 0) & (pl.program_id(1) == 0))
        def _():
            for a in range(nw):
                for r in range(len(FLIPS)):
                    wcopy(a, r).start()

        @pl.when(pl.program_id(1) == 0)
        def _():
            dst_ref[...] = jnp.zeros_like(dst_ref)

        qs, kk, (e_qm, e_km, e_qe, e_kd) = _gla_factors(q_ref, k_ref, cum_ref)
        qm, km, qe, kd = qs * e_qm, kk * e_km, qs * e_qe, kk * e_kd
        mask = _intra_mask()
        dqm_slabs, dkm_slabs, dv_intra = [], [], []
        for p in range(B_HEADS // 2):
            qm_st = _pair_stack(qm, p)
            km_p = km[:, p * LANES:(p + 1) * LANES].astype(BF16)
            a = _dot_nt(qm_st, km_p)
            da_blocks, dqm_h = [], []
            for hh in range(2):
                h = 2 * p + hh
                vs = slice(h * B_VAL_DIM, (h + 1) * B_VAL_DIM)
                att = jnp.where(mask, a[hh * GLA_TILE:(hh + 1) * GLA_TILE], 0.0).astype(BF16)
                dv_intra.append(_dot_tn(att, do_ref[:, vs]))
                da = jnp.where(mask, _dot_nt(do_ref[:, vs], v_ref[:, vs]), 0.0).astype(BF16)
                da_blocks.append(da)
                dqm_h.append(_dot(da, km_p))
            lo = _lane_iota((GLA_TILE, LANES)) < B_KEY_DIM
            dqm_slabs.append(jnp.where(lo, dqm_h[0], dqm_h[1]))
            dkm_slabs.append(_dot_tn(jnp.concatenate(da_blocks, axis=0), qm_st))
        dqm = jnp.concatenate(dqm_slabs, axis=1)
        dkm = jnp.concatenate(dkm_slabs, axis=1)

        dqe_c, dkd_c, dv_inter, tail_c = ([None] * CHUNKS_PER_TILE for _ in range(4))
        for c in reversed(range(CHUNKS_PER_TILE)):
            rows = slice(c * B_CHUNK, (c + 1) * B_CHUNK)
            dst = dst_ref[...]
            dst_b = dst.astype(BF16)
            dv_inter[c] = _dot_nt(_stack_masked(kd[rows]), dst_b)
            dkd_c[c] = _select_heads(_dot(_rows_by_head(v_ref[rows, :]), dst_b))
            do_c = do_ref[rows, :]
            dqe_c[c] = _select_heads(_dot(_rows_by_head(do_c), st_all_ref[c].astype(BF16)))
            contrib = _select_state(_dot_tn(do_c, qe[rows].astype(BF16)))
            decay = jnp.exp(cum_ref[pl.ds(c * B_CHUNK + B_CHUNK - 1, 1), :])
            tail = (jnp.sum(kk[rows] * dkd_c[c] * e_kd[rows], axis=0, keepdims=True)
                    + decay * jnp.sum(st_all_ref[c] * dst, axis=0, keepdims=True))
            tail_c[c] = jnp.broadcast_to(tail, (B_CHUNK, B_KEY_WIDTH))
            dst_ref[...] = dst * decay + contrib
        dqe = jnp.concatenate(dqe_c, axis=0)
        dkd = jnp.concatenate(dkd_c, axis=0)
        dqs = dqm * e_qm + dqe * e_qe
        dk = dkm * e_km + dkd * e_kd
        dq_ref[...] = (dqs * scale).astype(BF16)
        dk_ref[...] = dk.astype(BF16)
        for h in range(B_HEADS):
            dvi = jnp.concatenate([dv_inter[c][h * B_CHUNK:(h + 1) * B_CHUNK] for c in range(CHUNKS_PER_TILE)], axis=0)
            dv_ref[:, h * B_VAL_DIM:(h + 1) * B_VAL_DIM] = (dv_intra[h] + dvi).astype(BF16)
        dd = qs * dqs - kk * dk
        i, j = _row_iota((GLA_TILE, GLA_TILE)), _lane_iota((GLA_TILE, GLA_TILE))
        upper = ((i // B_CHUNK == j // B_CHUNK) & (j >= i)).astype(BF16)
        hi, mid, lo3 = _split3(dd)
        dla_ref[...] = _dot(upper, hi) + _dot(upper, mid) + _dot(upper, lo3) + jnp.concatenate(tail_c, axis=0)

        @pl.when((pl.program_id(0) == B - 1) & (pl.program_id(1) == nt - 1))
        def _():
            for a in range(nw):
                for r in range(len(FLIPS)):
                    wcopy(a, r).wait()

    def rows(w):
        return pl.BlockSpec((GLA_TILE, w), lambda b, t: (b * nt + nt - 1 - t, 0))

    res = pl.pallas_call(
        body, name="gla_bwd", grid=(B, nt),
        in_specs=[rows(B_KEY_WIDTH), rows(B_KEY_WIDTH), rows(B_KEY_WIDTH), rows(B_WIDTH), rows(B_WIDTH),
                  pl.BlockSpec((CHUNKS_PER_TILE, B_VAL_DIM, B_KEY_WIDTH), lambda b, t: (b * nt + nt - 1 - t, 0, 0))]
                 + _any_specs(nw),
        out_specs=[rows(B_KEY_WIDTH), rows(B_KEY_WIDTH), rows(B_WIDTH), rows(B_KEY_WIDTH)] + _any_specs(nw),
        out_shape=[jax.ShapeDtypeStruct((T, B_KEY_WIDTH), BF16), jax.ShapeDtypeStruct((T, B_KEY_WIDTH), BF16),
                   jax.ShapeDtypeStruct((T, B_WIDTH), BF16), jax.ShapeDtypeStruct((T, B_KEY_WIDTH), F32)]
                  + [jax.ShapeDtypeStruct((len(FLIPS), *g.shape[1:]), g.dtype) for g in wgrads],
        scratch_shapes=[pltpu.VMEM((B_VAL_DIM, B_KEY_WIDTH), F32),
                        pltpu.SemaphoreType.DMA((nw, len(FLIPS))), pltpu.SemaphoreType.DMA((nw, len(FLIPS)))],
        compiler_params=_params("arbitrary", "arbitrary"),
    )(q, k, cum, vb, do, st_all, *wgrads)
    return res[:4], res[4:]


def _merge(x2, tgt2, attn, za, o_gla, zb, ga, gb, w_oa, w_ob, w_o, g_gla, g_final):
    T = x2.shape[0]
    tm = 256
    last = T // tm - 1

    def body(x_ref, tgt_ref, attn_ref, za_ref, og_ref, zb_ref, ga_ref, gb_ref,
             woa_ref, wob_ref, wo_ref, gg_ref, gf_ref,
             dxres_ref, dattn_ref, dog_ref, dza_ref, dzb_ref, dga_ref, dgb_ref,
             dwo_ref, dwoa_ref, dwob_ref, small_ref,
             awo_ref, awoa_ref, awob_ref, agf_ref, agg_ref, loss_ref):
        @pl.when(pl.program_id(0) == 0)
        def _():
            for r in (awo_ref, awoa_ref, awob_ref, agf_ref, agg_ref, loss_ref):
                r[...] = jnp.zeros_like(r)

        za_v = za_ref[...]
        sig_za = _sigmoid(za_v)
        silu_a = za_v * sig_za
        attn_v = attn_ref[...]
        oa = (attn_v * silu_a).astype(BF16)
        ya = _dot(oa, woa_ref[...])
        og = og_ref[...]
        zb_v = zb_ref[...]
        sig_zb = _sigmoid(zb_v)
        silu_b = zb_v * sig_zb
        gg = gg_ref[...]
        on_parts, rinv_parts = [], []
        for h in range(B_HEADS):
            seg = og[:, h * B_VAL_DIM:(h + 1) * B_VAL_DIM]
            rinv = lax.rsqrt(jnp.mean(seg * seg, axis=-1, keepdims=True) + NORM_EPS)
            rinv_parts.append(rinv)
            on_parts.append(seg * rinv)
        on = jnp.concatenate(on_parts, axis=1)
        obn = on * gg
        ob = (obn * silu_b).astype(BF16)
        yb = _dot(ob, wob_ref[...])
        sig_a, sig_b = _sigmoid(ga_ref[...]), _sigmoid(gb_ref[...])
        merged = (sig_a * ya + sig_b * yb).astype(BF16)
        out = x_ref[...] + _dot(merged, wo_ref[...])
        rf = lax.rsqrt(jnp.mean(out * out, axis=-1, keepdims=True) + NORM_EPS)
        nrm = out * rf
        gf = gf_ref[...]
        err = nrm * gf - tgt_ref[...]
        loss_ref[...] += jnp.sum(err * err) * (0.5 / D_MODEL)

        dy = err * (1.0 / D_MODEL)
        agf_ref[...] += jnp.sum(dy * nrm, axis=0, keepdims=True)
        dn = dy * gf
        dout = rf * (dn - nrm * jnp.mean(dn * nrm, axis=-1, keepdims=True))
        dxres_ref[...] = dout
        dout_b = dout.astype(BF16)
        dmerged = _dot_nt(dout_b, wo_ref[...])
        awo_ref[...] += _dot_tn(merged, dout_b)
        dya = dmerged * sig_a
        dyb = dmerged * sig_b
        dga_ref[...] = (dmerged * ya * sig_a * (1.0 - sig_a)).astype(BF16)
        dgb_ref[...] = (dmerged * yb * sig_b * (1.0 - sig_b)).astype(BF16)
        dya_b, dyb_b = dya.astype(BF16), dyb.astype(BF16)
        awoa_ref[...] += _dot_tn(oa, dya_b)
        awob_ref[...] += _dot_tn(ob, dyb_b)
        doa = _dot_nt(dya_b, woa_ref[...])
        dattn_ref[...] = (doa * silu_a).astype(BF16)
        dza_ref[...] = (doa * attn_v * (sig_za * (1.0 + za_v * (1.0 - sig_za)))).astype(BF16)
        dob = _dot_nt(dyb_b, wob_ref[...])
        dzb_ref[...] = (dob * obn * (sig_zb * (1.0 + zb_v * (1.0 - sig_zb)))).astype(BF16)
        dobn = dob * silu_b
        agg_ref[...] += jnp.sum(dobn * on, axis=0, keepdims=True)
        don = dobn * gg
        for h in range(B_HEADS):
            sl = slice(h * B_VAL_DIM, (h + 1) * B_VAL_DIM)
            don_h, on_h = don[:, sl], on[:, sl]
            dog_ref[:, sl] = (rinv_parts[h] * (don_h - on_h * jnp.mean(don_h * on_h, axis=-1, keepdims=True))).astype(BF16)

        @pl.when(pl.program_id(0) == last)
        def _():
            for j in range(N_DEV):
                dwo_ref[j] = awo_ref[j * SHARD_OUT:(j + 1) * SHARD_OUT, :].astype(BF16)
                dwoa_ref[j] = awoa_ref[:, j * SHARD_OUT:(j + 1) * SHARD_OUT].astype(BF16)
                dwob_ref[j] = awob_ref[:, j * SHARD_OUT:(j + 1) * SHARD_OUT].astype(BF16)
            small_ref[...] = jnp.zeros_like(small_ref)
            _put_rows(small_ref, SMALL_G_FINAL, agf_ref[...])
            _put_rows(small_ref, SMALL_G_GLA, agg_ref[...])
            small_ref[SMALL_LOSS:SMALL_LOSS + 1, :] = loss_ref[...]

    def rows(w):
        return pl.BlockSpec((tm, w), lambda i: (i, 0))

    def whole(shape):
        nd = len(shape)
        return pl.BlockSpec(shape, lambda i: (0,) * nd)

    outs = [((T, D_MODEL), F32, rows(D_MODEL)), ((T, A_WIDTH), BF16, rows(A_WIDTH)), ((T, B_WIDTH), BF16, rows(B_WIDTH)),
            ((T, A_WIDTH), BF16, rows(A_WIDTH)), ((T, B_WIDTH), BF16, rows(B_WIDTH)),
            ((T, D_MODEL), BF16, rows(D_MODEL)), ((T, D_MODEL), BF16, rows(D_MODEL)),
            ((N_DEV, SHARD_OUT, D_MODEL), BF16, whole((N_DEV, SHARD_OUT, D_MODEL))),
            ((N_DEV, A_WIDTH, SHARD_OUT), BF16, whole((N_DEV, A_WIDTH, SHARD_OUT))),
            ((N_DEV, B_WIDTH, SHARD_OUT), BF16, whole((N_DEV, B_WIDTH, SHARD_OUT))),
            ((SMALL_SINKS, LANES), F32, whole((SMALL_SINKS, LANES)))]
    return pl.pallas_call(
        body, name="merge", grid=(T // tm,),
        in_specs=[rows(D_MODEL), rows(D_MODEL), rows(A_WIDTH), rows(A_WIDTH), rows(B_WIDTH), rows(B_WIDTH),
                  rows(D_MODEL), rows(D_MODEL),
                  _const_spec((A_WIDTH, D_MODEL)), _const_spec((B_WIDTH, D_MODEL)), _const_spec((D_MODEL, D_MODEL)),
                  _const_spec((1, B_WIDTH)), _const_spec((1, D_MODEL))],
        out_specs=[o[2] for o in outs],
        out_shape=[jax.ShapeDtypeStruct(o[0], o[1]) for o in outs],
        scratch_shapes=[pltpu.VMEM((D_MODEL, D_MODEL), F32), pltpu.VMEM((A_WIDTH, D_MODEL), F32),
                        pltpu.VMEM((B_WIDTH, D_MODEL), F32), pltpu.VMEM((1, D_MODEL), F32), pltpu.VMEM((1, B_WIDTH), F32),
                        pltpu.VMEM((1, LANES), F32)],
        compiler_params=_params("arbitrary"),
    )(x2, tgt2, attn, za, o_gla, zb, ga, gb, w_oa, w_ob, w_o, g_gla, g_final)


def _in_proj_bwd(x2, dxres, cosf, sinf, g_in, wt_pad, wa_pad, parts):
    T = x2.shape[0]
    tm = 256
    last = T // tm - 1
    base = SMALL_G_IN

    def body(x_ref, dxres_ref, cos_ref, sin_ref, g_ref, wt_ref, wa_ref,
             dq_ref, dkv_ref, dza_ref, dqb_ref, dkb_ref, dvb_ref, dzb_ref, dla_ref, u_ref, alr_ref, dga_ref, dgb_ref,
             dx_ref, dsh_ref, small_ref, dproj_ref, agin_ref, aba_ref, awa_ref):
        @pl.when(pl.program_id(0) == 0)
        def _():
            for r in (agin_ref, aba_ref, awa_ref):
                r[...] = jnp.zeros_like(r)

        cos, nsin = cos_ref[...], -sin_ref[...]
        for s in range(A_WIDTH // LANES):
            sl = slice(s * LANES, (s + 1) * LANES)
            dproj_ref[:, sl] = _rope_slab(dq_ref[:, sl].astype(F32), cos, nsin).astype(BF16)
        dproj_ref[:, QKV_K:QKV_V] = _rope_slab(dkv_ref[:, 0:LANES].astype(F32), cos, nsin).astype(BF16)
        dproj_ref[:, QKV_V:QKV_W] = dkv_ref[:, LANES:]

        def put(name, val):
            a, b = SEG[name]
            dproj_ref[:, a:b] = val

        put("za", dza_ref[...])
        put("qb", dqb_ref[...])
        put("kb", dkb_ref[...])
        put("vb", dvb_ref[...])
        put("zb", dzb_ref[...])
        put("ga", dga_ref[...])
        put("gb", dgb_ref[...])
        du = dla_ref[...] * (1.0 / B_GATE_TEMP) * _sigmoid(-u_ref[...])
        aba_ref[...] += jnp.sum(du, axis=0, keepdims=True)
        du_b = du.astype(BF16)
        awa_ref[...] += _dot_tn(alr_ref[...], du_b)
        put("alr", _dot_nt(du_b, wa_ref[...]).astype(BF16))

        for j in range(N_DEV):
            col = (j % 2) * SHARD_PAD
            for a, b in _shard_pad_cols(j):
                dsh_ref[j // 2, :, col:col + b - a] = dproj_ref[:, a:b]
                col += b - a
            dsh_ref[j // 2, :, col:(j % 2 + 1) * SHARD_PAD] = jnp.zeros((tm, SHARD_PAD - SHARD_IN), BF16)

        dh = _dot(dproj_ref[...], wt_ref[...])
        x = x_ref[...]
        r = lax.rsqrt(jnp.mean(x * x, axis=-1, keepdims=True) + NORM_EPS)
        nrm = x * r
        agin_ref[...] += jnp.sum(dh * nrm, axis=0, keepdims=True)
        dn = dh * g_ref[...]
        dx_ref[...] = dxres_ref[...] + r * (dn - nrm * jnp.mean(dn * nrm, axis=-1, keepdims=True))

        @pl.when(pl.program_id(0) == last)
        def _():
            small_ref[...] = jnp.zeros_like(small_ref)
            _put_rows(small_ref, SMALL_G_IN - base, agin_ref[...])
            _put_rows(small_ref, SMALL_B_ALPHA - base, aba_ref[...])
            for half in range(B_KEY_WIDTH // LANES):
                r0 = SMALL_W_ALPHA - base + half * B_GATE_RANK
                small_ref[r0:r0 + B_GATE_RANK, :] = awa_ref[0:B_GATE_RANK, half * LANES:(half + 1) * LANES]

    def rows(w):
        return pl.BlockSpec((tm, w), lambda i: (i, 0))

    names = ["dq", "dkv", "dza", "dqb", "dkb", "dvb", "dzb", "dla", "u", "alr", "dga", "dgb"]
    return pl.pallas_call(
        body, name="in_proj_bwd", grid=(T // tm,),
        in_specs=[rows(D_MODEL), rows(D_MODEL), rows(LANES), rows(LANES), _const_spec((1, D_MODEL)),
                  _const_spec((D_IN_PAD, D_MODEL)), _const_spec((RANK_PAD, B_KEY_WIDTH))]
                 + [rows(parts[n].shape[1]) for n in names],
        out_specs=[rows(D_MODEL), pl.BlockSpec((N_CHIPS, tm, 2 * SHARD_PAD), lambda i: (0, i, 0)),
                   pl.BlockSpec((SMALL_ROWS - base, LANES), lambda i: (0, 0))],
        out_shape=[jax.ShapeDtypeStruct((T, D_MODEL), F32), jax.ShapeDtypeStruct((N_CHIPS, T, 2 * SHARD_PAD), BF16),
                   jax.ShapeDtypeStruct((SMALL_ROWS - base, LANES), F32)],
        scratch_shapes=[pltpu.VMEM((tm, D_IN_PAD), BF16), pltpu.VMEM((1, D_MODEL), F32), pltpu.VMEM((1, B_KEY_WIDTH), F32),
                        pltpu.VMEM((RANK_PAD, B_KEY_WIDTH), F32)],
        compiler_params=_params("arbitrary"),
    )(x2, dxres, cosf, sinf, g_in, wt_pad, wa_pad, *[parts[n] for n in names])


FLIPS = [(dx, dy, dc) for dx in (0, 1) for dy in (0, 1) for dc in (0, 1)][1:]


def _my_place():
    return lax.axis_index("x"), lax.axis_index("y"), lax.axis_index("c")


def _any_specs(n):
    return [pl.BlockSpec(memory_space=pl.ANY)] * n


def _all_gather(shards):
    n = len(shards)

    def body(*refs):
        ins, outs = refs[:n], refs[n:2 * n]
        send_sems, recv_sems, local_sems = refs[2 * n:]
        x, y, c = _my_place()
        me, sibling = (x, y, c), (x, y, 1 - c)
        chips = [(1 - x, y), (x, 1 - y), (1 - x, 1 - y)]

        def block(a, px, py, pc):
            return outs[a].at[4 * px + 2 * py + pc]

        def copy(a, k, blk, to, src=None):
            return pltpu.make_async_remote_copy(
                src_ref=block(a, *blk) if src is None else src, dst_ref=block(a, *blk),
                send_sem=send_sems.at[a, k], recv_sem=recv_sems.at[a, k], device_id=to, device_id_type=MESH)

        mine = [pltpu.make_async_copy(ins[a], block(a, *me), local_sems.at[a]) for a in range(n)]
        for cp in mine:
            cp.start()
        first = []
        for a in range(n):
            first.append(copy(a, 0, me, sibling, src=ins[a]))
            first += [copy(a, 1 + j, me, (*chip, c), src=ins[a]) for j, chip in enumerate(chips)]
        for cp in first:
            cp.start()
        passed = []
        for j, chip in enumerate(chips):
            for a in range(n):
                copy(a, 1 + j, (*chip, c), me).wait_recv()
                fwd = copy(a, 4 + j, (*chip, c), sibling)
                fwd.start()
                passed.append(fwd)
        for a in range(n):
            copy(a, 0, sibling, me).wait_recv()
            for j, chip in enumerate(chips):
                copy(a, 4 + j, (*chip, 1 - c), me).wait_recv()
        for cp in first + passed:
            cp.wait_send()
        for cp in mine:
            cp.wait()

    return pl.pallas_call(
        body, name="gather_weights",
        in_specs=_any_specs(n), out_specs=_any_specs(n),
        out_shape=[jax.ShapeDtypeStruct((N_DEV, *s.shape), s.dtype) for s in shards],
        scratch_shapes=[pltpu.SemaphoreType.DMA((n, 7)), pltpu.SemaphoreType.DMA((n, 7)), pltpu.SemaphoreType.DMA((n,))],
    )(*shards)


def _w_in_grad_rs(h, dsh, chip_order, small):
    T = h.shape[0]
    tk = math.gcd(T, 2048)
    nk = T // tk
    chip_flips = [(1, 1), (1, 0), (0, 1)]
    n_steps = len(chip_flips) + 1

    def body(order_ref, h_ref, d_ref, s_ref, own_ref, recv_ref, sall_ref,
             acc_ref, stage_ref, send_sems, recv_sems, ssend_sems, srecv_sems, local_sem):
        i, kk = pl.program_id(0), pl.program_id(1)
        x, y, c = _my_place()
        my_dev = 4 * x + 2 * y + c

        def small_copy(r, slot):
            dx, dy, dc = FLIPS[r]
            return pltpu.make_async_remote_copy(
                src_ref=s_ref, dst_ref=sall_ref.at[slot], send_sem=ssend_sems.at[r], recv_sem=srecv_sems.at[r],
                device_id=(x ^ dx, y ^ dy, c ^ dc), device_id_type=MESH)

        keep_small = pltpu.make_async_copy(s_ref, sall_ref.at[my_dev], local_sem)

        def shard_copy(slot, dx, dy, dc):
            r = FLIPS.index((dx, dy, dc))
            return pltpu.make_async_remote_copy(
                src_ref=stage_ref.at[slot, c ^ dc], dst_ref=recv_ref.at[r], send_sem=send_sems.at[r],
                recv_sem=recv_sems.at[r], device_id=(x ^ dx, y ^ dy, c ^ dc), device_id_type=MESH)

        def stage(slot):
            stage_ref[slot, 0] = acc_ref[0:SHARD_PAD, :].astype(BF16)
            stage_ref[slot, 1] = acc_ref[SHARD_PAD:2 * SHARD_PAD, :].astype(BF16)

        @pl.when((i == 0) & (kk == 0))
        def _():
            keep_small.start()
            for r in range(len(FLIPS)):
                small_copy(r, my_dev).start()

        @pl.when(kk == 0)
        def _():
            acc_ref[...] = jnp.zeros_like(acc_ref)

        acc_ref[...] += _dot_tn(d_ref[...], h_ref[...])

        for t, (dx, dy) in enumerate(chip_flips):
            @pl.when((i == t) & (kk == nk - 1))
            def _(t=t, dx=dx, dy=dy):
                if t >= 2:
                    for dc in (0, 1):
                        shard_copy(t % 2, *chip_flips[t - 2], dc).wait_send()
                stage(t % 2)
                for dc in (0, 1):
                    shard_copy(t % 2, dx, dy, dc).start()

        @pl.when((i == n_steps - 1) & (kk == nk - 1))
        def _():
            for dc in (0, 1):
                shard_copy(1, *chip_flips[1], dc).wait_send()
            stage(1)
            shard_copy(1, 0, 0, 1).start()

            @pl.when(c == 0)
            def _():
                own_ref[...] = acc_ref[0:SHARD_PAD, :]

            @pl.when(c == 1)
            def _():
                own_ref[...] = acc_ref[SHARD_PAD:2 * SHARD_PAD, :]

            for dc in (0, 1):
                shard_copy(0, *chip_flips[2], dc).wait_send()
            shard_copy(1, 0, 0, 1).wait_send()
            for r, (dx, dy, dc) in enumerate(FLIPS):
                shard_copy(0, dx, dy, dc).wait_recv()
                small_copy(r, 4 * (x ^ dx) + 2 * (y ^ dy) + (c ^ dc)).wait_recv()
                small_copy(r, my_dev).wait_send()
            keep_small.wait()

    return pl.pallas_call(
        body, name="w_in_grad_rs",
        grid_spec=pltpu.PrefetchScalarGridSpec(
            num_scalar_prefetch=1, grid=(n_steps, nk),
            in_specs=[pl.BlockSpec((tk, D_MODEL), lambda i, kk, order: (kk, 0)),
                      pl.BlockSpec((None, tk, 2 * SHARD_PAD), lambda i, kk, order: (order[i], kk, 0)),
                      pl.BlockSpec(memory_space=pl.ANY)],
            out_specs=[pl.BlockSpec((SHARD_PAD, D_MODEL), lambda i, kk, order: (0, 0)),
                       pl.BlockSpec(memory_space=pl.ANY), pl.BlockSpec(memory_space=pl.ANY)],
            scratch_shapes=[pltpu.VMEM((2 * SHARD_PAD, D_MODEL), F32), pltpu.VMEM((2, 2, SHARD_PAD, D_MODEL), BF16),
                            pltpu.SemaphoreType.DMA((7,)), pltpu.SemaphoreType.DMA((7,)),
                            pltpu.SemaphoreType.DMA((7,)), pltpu.SemaphoreType.DMA((7,)), pltpu.SemaphoreType.DMA]),
        out_shape=[jax.ShapeDtypeStruct((SHARD_PAD, D_MODEL), F32),
                   jax.ShapeDtypeStruct((len(FLIPS), SHARD_PAD, D_MODEL), BF16),
                   jax.ShapeDtypeStruct((N_DEV, *small.shape), F32)],
        compiler_params=_params("arbitrary", "arbitrary"),
    )(chip_order, h, dsh, small)


def _adam_math(w, g, m, v):
    m_new = ADAM_B1 * m + (1.0 - ADAM_B1) * g
    v_new = ADAM_B2 * v + (1.0 - ADAM_B2) * (g * g)
    m_hat = m_new / (1.0 - ADAM_B1 ** ADAM_STEP)
    v_hat = v_new / (1.0 - ADAM_B2 ** ADAM_STEP)
    delta = -ADAM_LR * (m_hat / (jnp.sqrt(v_hat) + ADAM_EPS) + ADAM_WD * w)
    return delta, m_new, v_new


def _adam_big(name, own, own_idx, recv, w, m, v):
    rw, cw = w.shape
    rp = own.shape[1]
    steps = 8
    by_cols = rp != rw
    blk_w = (rw, cw // steps) if by_cols else (rw // steps, cw)
    blk_g = (rp, cw // steps) if by_cols else (rw // steps, cw)
    at = (lambda i: (0, i)) if by_cols else (lambda i: (i, 0))

    def body(idx_ref, o_ref, r_ref, w_ref, m_ref, v_ref, g_ref, d_ref, mo_ref, vo_ref):
        g = o_ref[...].astype(F32)
        for r in range(len(FLIPS)):
            g = g + r_ref[r].astype(F32)
        g = g[0:blk_w[0], :]
        g_ref[...] = g
        d_ref[...], mo_ref[...], vo_ref[...] = _adam_math(w_ref[...], g, m_ref[...], v_ref[...])

    spec = pl.BlockSpec(blk_w, lambda i, idx_ref: at(i))
    return pl.pallas_call(
        body, name=name,
        grid_spec=pltpu.PrefetchScalarGridSpec(
            num_scalar_prefetch=1, grid=(steps,),
            in_specs=[pl.BlockSpec((None, *blk_g), lambda i, idx_ref: (idx_ref[0], *at(i))),
                      pl.BlockSpec((len(FLIPS), *blk_g), lambda i, idx_ref: (0, *at(i))), spec, spec, spec],
            out_specs=[spec] * 4),
        out_shape=[jax.ShapeDtypeStruct((rw, cw), F32)] * 4,
        compiler_params=_params("parallel"),
    )(own_idx, own, recv, w, m, v)


def _adam_small(small_all, params):
    flat = [a for triple in params for a in triple]
    n_par = len(params)

    def body(s_ref, *refs):
        ins, outs, loss_ref = refs[:3 * n_par], refs[3 * n_par:-1], refs[-1]
        g_slab = s_ref[0]
        for dev in range(1, N_DEV):
            g_slab = g_slab + s_ref[dev]
        loss_ref[...] = g_slab[SMALL_LOSS:SMALL_LOSS + 1, :]
        dev = 4 * lax.axis_index("x") + 2 * lax.axis_index("y") + lax.axis_index("c")
        alpha_full = jnp.concatenate([g_slab[SMALL_W_ALPHA + half * B_GATE_RANK:SMALL_W_ALPHA + (half + 1) * B_GATE_RANK]
                                      for half in range(B_KEY_WIDTH // LANES)], axis=1)
        alpha_mine = pltpu.roll(alpha_full, (B_KEY_WIDTH - dev * SHARD_ALPHA) % B_KEY_WIDTH, 1)[:, 0:SHARD_ALPHA]
        grads = [_take_rows(g_slab, SMALL_G_IN, D_MODEL // LANES), _take_rows(g_slab, SMALL_G_FINAL, D_MODEL // LANES),
                 _take_rows(g_slab, SMALL_G_GLA, B_WIDTH // LANES), _take_rows(g_slab, SMALL_B_ALPHA, B_KEY_WIDTH // LANES),
                 g_slab[SMALL_SINKS:SMALL_SINKS + 1, 0:A_HEADS], alpha_mine]
        for i, g in enumerate(grads):
            w_ref, m_ref, v_ref = ins[3 * i:3 * i + 3]
            delta, m_new, v_new = _adam_math(w_ref[...], g, m_ref[...], v_ref[...])
            outs[4 * i][...] = g
            outs[4 * i + 1][...] = delta
            outs[4 * i + 2][...] = m_new
            outs[4 * i + 3][...] = v_new

    res = pl.pallas_call(
        body, name="adam_small",
        out_shape=[jax.ShapeDtypeStruct(t[0].shape, F32) for t in params for _ in range(4)]
                  + [jax.ShapeDtypeStruct((1, LANES), F32)],
    )(small_all, *flat)
    return [res[4 * i:4 * i + 4] for i in range(n_par)], res[-1]


def _local_step(x, positions, loss_target, g_in, wt_sh, wa_pad, b_alpha, sinks, g_gla, w_oa, w_ob, w_o, g_final,
                chip_order):
    B, S, _ = x.shape
    T = B * S
    x2 = x.reshape(T, D_MODEL)
    tgt2 = loss_target.reshape(T, D_MODEL)
    cosf, sinf = _rope_tables(positions.reshape(T, 1))
    f = _in_proj(x2, cosf, sinf, g_in, wt_sh, wa_pad, b_alpha)
    attn, lse = _attn_fwd(f["qkv"], sinks, B, S)
    o_gla, st_all = _gla_fwd(f["q"], f["k"], f["cum"], f["vb"], B, S)
    (dxres, dattn, dog, dza, dzb, dga, dgb, dw_o, dw_oa, dw_ob, small_a) = _merge(
        x2, tgt2, attn, f["za"], o_gla, f["zb"], f["ga"], f["gb"], w_oa, w_ob, w_o, g_gla, g_final)
    dq, dkv, dsink = _attn_bwd(f["qkv"], dattn, attn, lse, sinks, B, S)
    (dqb, dkb, dvb, dla), (rv_o, rv_oa, rv_ob) = _gla_bwd(f["q"], f["k"], f["cum"], f["vb"], dog, st_all, B, S,
                                                        [dw_o, dw_oa, dw_ob])
    parts = dict(dq=dq, dkv=dkv, dza=dza, dqb=dqb, dkb=dkb, dvb=dvb, dzb=dzb, dla=dla, u=f["u"], alr=f["alr"],
                 dga=dga, dgb=dgb)
    dx, dsh, small_c = _in_proj_bwd(x2, dxres, cosf, sinf, g_in, f["wt_pad"], wa_pad, parts)
    small = jnp.concatenate([small_a, dsink, small_c], axis=0)
    own_in, rv_in, small_all = _w_in_grad_rs(f["h"], dsh, chip_order, small)
    return dict(grad_x=dx.reshape(B, S, D_MODEL), own_in=own_in, rv_in=rv_in,
                own_o=dw_o, rv_o=rv_o, own_oa=dw_oa, rv_oa=rv_oa, own_ob=dw_ob, rv_ob=rv_ob, small_all=small_all)


def kernel(x, positions, g_in, w_in, w_alpha_up, b_alpha, attn_sinks, g_gla_norm, w_out_a, w_out_b, w_o, g_final, loss_target, m_g_in, m_w_in, m_w_alpha_up, m_b_alpha, m_attn_sinks, m_g_gla_norm, m_w_out_a, m_w_out_b, m_w_o, m_g_final, v_g_in, v_w_in, v_w_alpha_up, v_b_alpha, v_attn_sinks, v_g_gla_norm, v_w_out_a, v_w_out_b, v_w_o, v_g_final):
    xi, yi, ci = _my_place()
    dev_idx = (4 * xi + 2 * yi + ci).reshape(1).astype(jnp.int32)
    chip = 2 * xi + yi
    chip_order = jnp.stack([chip ^ 3, chip ^ 2, chip ^ 1, chip]).astype(jnp.int32)

    g_win, g_woa, g_wob, g_wo, g_wa = _all_gather(
        [jnp.pad(w_in[0].T.astype(BF16), ((0, SHARD_PAD - SHARD_IN), (0, 0))), w_out_a[0].astype(BF16),
         w_out_b[0].astype(BF16), w_o[0].astype(BF16), w_alpha_up[0].astype(BF16)])
    wt_sh = g_win.reshape(N_DEV * SHARD_PAD, D_MODEL)
    w_oa_full = jnp.concatenate([g_woa[j] for j in range(N_DEV)], axis=1)
    w_ob_full = jnp.concatenate([g_wob[j] for j in range(N_DEV)], axis=1)
    w_o_full = g_wo.reshape(D_MODEL, D_MODEL)
    wa_pad = jnp.pad(jnp.concatenate([g_wa[j] for j in range(N_DEV)], axis=1), ((0, RANK_PAD - B_GATE_RANK), (0, 0)))

    r = _local_step(x, positions, loss_target, g_in, wt_sh, wa_pad, b_alpha, attn_sinks[0], g_gla_norm, w_oa_full,
                    w_ob_full, w_o_full, g_final.reshape(1, D_MODEL), chip_order)

    first = jnp.zeros((1,), jnp.int32)
    big = [_adam_big("adam_w_in", r["own_in"][None], first, r["rv_in"], w_in[0].T, m_w_in[0].T, v_w_in[0].T),
           _adam_big("adam_w_out_a", r["own_oa"], dev_idx, r["rv_oa"], w_out_a[0], m_w_out_a[0], v_w_out_a[0]),
           _adam_big("adam_w_out_b", r["own_ob"], dev_idx, r["rv_ob"], w_out_b[0], m_w_out_b[0], v_w_out_b[0]),
           _adam_big("adam_w_o", r["own_o"], dev_idx, r["rv_o"], w_o[0], m_w_o[0], v_w_o[0])]
    row = lambda a: a.reshape(1, D_MODEL)
    big[0] = [a.T for a in big[0]]
    (s_in, s_final, s_gla, s_ba, s_sinks, s_wa), loss_row = _adam_small(r["small_all"], [
        (g_in, m_g_in, v_g_in), (row(g_final), row(m_g_final), row(v_g_final)),
        (g_gla_norm, m_g_gla_norm, v_g_gla_norm), (b_alpha, m_b_alpha, v_b_alpha),
        (attn_sinks, m_attn_sinks, v_attn_sinks), (w_alpha_up[0], m_w_alpha_up[0], v_w_alpha_up[0])])

    def group(i):
        return (s_in[i], big[0][i][None], s_wa[i][None], s_ba[i], s_sinks[i], s_gla[i], big[1][i][None], big[2][i][None],
                big[3][i][None], s_final[i].reshape(D_MODEL))

    return (loss_row[0, 0], r["grad_x"], *group(0), *group(1), *group(2), *group(3))
```

```python
import functools
import math

import numpy as np
import jax
import jax.numpy as jnp
from jax import lax
from jax.experimental import pallas as pl
from jax.experimental.pallas import tpu as pltpu

F32 = jnp.float32
BF16 = jnp.bfloat16
MESH = pl.DeviceIdType.MESH

D_MODEL = 1024
A_HEADS, A_KV_HEADS, A_HEAD_DIM = 8, 2, 64
A_WIDTH, A_KV_WIDTH = 512, 128
WINDOW = 128
ROPE_THETA = 500000.0
ROPE_DIM = 16
B_HEADS, B_KEY_DIM, B_VAL_DIM = 4, 64, 128
B_KEY_WIDTH, B_WIDTH = 256, 512
B_GATE_RANK = 16
B_GATE_TEMP = 16.0
B_CHUNK = 64
NORM_EPS = 1e-6
NEG_BIG = -1e30
D_IN = 4880
N_DEV = 8
N_CHIPS = 4
ADAM_LR, ADAM_B1, ADAM_B2, ADAM_EPS, ADAM_WD, ADAM_STEP = 0.001, 0.9, 0.999, 1e-08, 0.01, 10

LANES = 128
V7X_VMEM_LIMIT = 56 * 1024 * 1024

RANK_PAD = LANES
SEG = {}
_off = 0
for _name, _w in (("qa", 512), ("ka", 128), ("va", 128), ("za", 512), ("qb", 256), ("kb", 256),
                  ("vb", 512), ("zb", 512), ("alr", RANK_PAD), ("ga", 1024), ("gb", 1024)):
    SEG[_name] = (_off, _off + _w)
    _off += _w
D_IN_PAD = _off
ALR_SRC = SEG["alr"][0]
QKV_K, QKV_V, QKV_W = SEG["ka"][0], SEG["va"][0], SEG["va"][1]

SHARD_IN = D_IN // N_DEV
SHARD_PAD = 640
SHARD_OUT = D_MODEL // N_DEV
SHARD_ALPHA = B_KEY_WIDTH // N_DEV

SMALL_G_FINAL, SMALL_G_GLA, SMALL_LOSS, SMALL_SINKS, SMALL_G_IN, SMALL_B_ALPHA, SMALL_W_ALPHA = 0, 8, 12, 16, 24, 32, 40
SMALL_ROWS = 72


def _dot(a, b):
    return jnp.dot(a, b, preferred_element_type=F32)


def _dot_nt(a, b):
    return lax.dot_general(a, b, (((1,), (1,)), ((), ())), preferred_element_type=F32)


def _dot_tn(a, b):
    return lax.dot_general(a, b, (((0,), (0,)), ((), ())), preferred_element_type=F32)


def _sigmoid(z):
    return 1.0 / (1.0 + jnp.exp(-z))


def _params(*sem):
    return pltpu.CompilerParams(dimension_semantics=sem, vmem_limit_bytes=V7X_VMEM_LIMIT)


def _const_spec(shape):
    nd = len(shape)
    return pl.BlockSpec(shape, lambda *_: (0,) * nd, pipeline_mode=pl.Buffered(1))


def _lane_iota(shape):
    return lax.broadcasted_iota(jnp.int32, shape, 1)


def _row_iota(shape):
    return lax.broadcasted_iota(jnp.int32, shape, 0)


def _split3(v):
    hi = v.astype(BF16)
    r1 = v - hi.astype(F32)
    mid = r1.astype(BF16)
    lo = (r1 - mid.astype(F32)).astype(BF16)
    return hi, mid, lo


def _put_rows(ref, row0, vec):
    for r in range(vec.shape[1] // LANES):
        ref[row0 + r:row0 + r + 1, :] = vec[:, r * LANES:(r + 1) * LANES]


def _take_rows(slab, row0, n):
    return jnp.concatenate([slab[row0 + r:row0 + r + 1, :] for r in range(n)], axis=1)


def _rope_lane_constants():
    half = ROPE_DIM // 2
    inv_freq = np.exp(-math.log(ROPE_THETA) * np.arange(half, dtype=np.float32) * np.float32(2.0 / ROPE_DIM)).astype(np.float32)
    lane = np.arange(LANES)
    j = lane % A_HEAD_DIM
    invf = np.where(j < ROPE_DIM, inv_freq[j % half], 0.0).astype(np.float32)
    sign = np.where(j < half, -1.0, np.where(j < ROPE_DIM, 1.0, 0.0)).astype(np.float32)
    return jnp.asarray(invf)[None, :], jnp.asarray(sign)[None, :]


def _rope_slab(t, cos, sin_signed):
    first = (_lane_iota(t.shape) % A_HEAD_DIM) < (ROPE_DIM // 2)
    partner = jnp.where(first, pltpu.roll(t, LANES - ROPE_DIM // 2, 1), pltpu.roll(t, ROPE_DIM // 2, 1))
    return t * cos + partner * sin_signed


def _shard_pad_cols(j):
    cut = ALR_SRC + B_GATE_RANK
    shift = RANK_PAD - B_GATE_RANK
    a, b = j * SHARD_IN, (j + 1) * SHARD_IN
    if b <= cut:
        return [(a, b)]
    if a >= cut:
        return [(a + shift, b + shift)]
    return [(a, cut), (cut + shift, b + shift)]


def _in_proj(x2, cosf, sinf, g_in, wt_sh, wa_pad, b_alpha, later_shards):
    T = x2.shape[0]
    tm = 256
    last = T // tm - 1
    nl = len(later_shards)

    def body(x_ref, cos_ref, sin_ref, g_ref, wsh_ref, wa_ref, ba_ref, *rest):
        sh_refs, rest = rest[:nl], rest[nl:]
        (h_ref, qkv_ref, za_ref, q_ref, k_ref, vb_ref, zb_ref, alr_ref, u_ref, cum_ref, ga_ref, gb_ref, wt_ref) = rest[:13]
        all_refs, (send_sems, recv_sems, local_sems) = rest[13:13 + nl], rest[13 + nl:]
        px, py, pc = _my_place()
        my_dev = 4 * px + 2 * py + pc

        def wcopy(a, r, slot):
            dx, dy, dc = FLIPS[r]
            return pltpu.make_async_remote_copy(
                src_ref=sh_refs[a], dst_ref=all_refs[a].at[slot], send_sem=send_sems.at[a, r],
                recv_sem=recv_sems.at[a, r], device_id=(px ^ dx, py ^ dy, pc ^ dc), device_id_type=MESH)

        keep = [pltpu.make_async_copy(sh_refs[a], all_refs[a].at[my_dev], local_sems.at[a]) for a in range(nl)]

        @pl.when(pl.program_id(0) == 0)
        def _():
            for a in range(nl):
                keep[a].start()
                for r in range(len(FLIPS)):
                    wcopy(a, r, my_dev).start()

        @pl.when(pl.program_id(0) == 0)
        def _():
            for j in range(N_DEV):
                src = j * SHARD_PAD
                for a, b in _shard_pad_cols(j):
                    wt_ref[a:b, :] = wsh_ref[src:src + b - a, :]
                    src += b - a
            a, b = SEG["alr"]
            wt_ref[a + B_GATE_RANK:b, :] = jnp.zeros((RANK_PAD - B_GATE_RANK, D_MODEL), BF16)

        x = x_ref[...]
        r = lax.rsqrt(jnp.mean(x * x, axis=-1, keepdims=True) + NORM_EPS)
        h = (x * r * g_ref[...]).astype(BF16)
        h_ref[...] = h

        def seg(name):
            a, b = SEG[name]
            return _dot_nt(h, wt_ref[a:b, :])

        cos, sin = cos_ref[...], sin_ref[...]
        qa = seg("qa")
        for s in range(A_WIDTH // LANES):
            qkv_ref[:, s * LANES:(s + 1) * LANES] = _rope_slab(qa[:, s * LANES:(s + 1) * LANES], cos, sin).astype(BF16)
        qkv_ref[:, QKV_K:QKV_V] = _rope_slab(seg("ka"), cos, sin).astype(BF16)
        qkv_ref[:, QKV_V:QKV_W] = seg("va").astype(BF16)
        za_ref[...] = seg("za")
        q_ref[...] = seg("qb")
        k_ref[...] = seg("kb")
        vb_ref[...] = seg("vb").astype(BF16)
        zb_ref[...] = seg("zb")
        ga_ref[...] = seg("ga")
        gb_ref[...] = seg("gb")
        alr = seg("alr").astype(BF16)
        alr_ref[...] = alr
        u = _dot(alr, wa_ref[...]) + ba_ref[...]
        u_ref[...] = u
        log_a = (jnp.minimum(u, 0.0) - jnp.log(1.0 + jnp.exp(-jnp.abs(u)))) * (1.0 / B_GATE_TEMP)
        row, col = _row_iota((tm, tm)), _lane_iota((tm, tm))
        tri = ((row // B_CHUNK == col // B_CHUNK) & (col <= row)).astype(BF16)
        hi, mid, lo = _split3(log_a)
        cum_ref[...] = _dot(tri, hi) + _dot(tri, mid) + _dot(tri, lo)

        @pl.when(pl.program_id(0) == last)
        def _():
            for a in range(nl):
                for r, (dx, dy, dc) in enumerate(FLIPS):
                    wcopy(a, r, 4 * (px ^ dx) + 2 * (py ^ dy) + (pc ^ dc)).wait_recv()
                    wcopy(a, r, my_dev).wait_send()
                keep[a].wait()

    def rows(w):
        return pl.BlockSpec((tm, w), lambda i: (i, 0))

    outs = [("h", D_MODEL, BF16), ("qkv", QKV_W, BF16), ("za", A_WIDTH, F32), ("q", B_KEY_WIDTH, F32),
            ("k", B_KEY_WIDTH, F32), ("vb", B_WIDTH, BF16), ("zb", B_WIDTH, F32), ("alr", RANK_PAD, BF16),
            ("u", B_KEY_WIDTH, F32), ("cum", B_KEY_WIDTH, F32), ("ga", D_MODEL, F32), ("gb", D_MODEL, F32)]
    res = pl.pallas_call(
        body, name="in_proj", grid=(T // tm,),
        in_specs=[rows(D_MODEL), rows(LANES), rows(LANES), _const_spec((1, D_MODEL)),
                  _const_spec((N_DEV * SHARD_PAD, D_MODEL)), _const_spec((RANK_PAD, B_KEY_WIDTH)),
                  _const_spec((1, B_KEY_WIDTH))] + _any_specs(nl),
        out_specs=[rows(w) for _, w, _ in outs] + [pl.BlockSpec((D_IN_PAD, D_MODEL), lambda i: (0, 0))] + _any_specs(nl),
        out_shape=[jax.ShapeDtypeStruct((T, w), dt) for _, w, dt in outs]
                  + [jax.ShapeDtypeStruct((D_IN_PAD, D_MODEL), BF16)]
                  + [jax.ShapeDtypeStruct((N_DEV, *sh.shape), sh.dtype) for sh in later_shards],
        scratch_shapes=[pltpu.SemaphoreType.DMA((nl, len(FLIPS))), pltpu.SemaphoreType.DMA((nl, len(FLIPS))),
                        pltpu.SemaphoreType.DMA((nl,))],
        compiler_params=_params("arbitrary"),
    )(x2, cosf, sinf, g_in, wt_sh, wa_pad, b_alpha, *later_shards)
    n_out = len(outs) + 1
    return dict(zip([n for n, _, _ in outs] + ["wt_pad"], res[:n_out])), res[n_out:]


def _dup_kv_head(t, g):
    tf = t.astype(F32)
    keep = (_lane_iota(tf.shape) < A_HEAD_DIM) == (g == 0)
    return jnp.where(keep, tf, pltpu.roll(tf, A_HEAD_DIM, 1)).astype(BF16)


def _stack_heads(t):
    lo = _lane_iota(t.shape) < A_HEAD_DIM
    zero = jnp.zeros_like(t)
    return jnp.concatenate([jnp.where(lo, t, zero), jnp.where(lo, zero, t)], axis=0)


def _band_mask(n):
    qi = _row_iota((2 * WINDOW, 2 * WINDOW)) % WINDOW
    kj = _lane_iota((2 * WINDOW, 2 * WINDOW)) - WINDOW
    return (kj <= qi) & (qi - kj < WINDOW) & ((n > 0) | (kj >= 0))


def _attn_fwd(qkv, sinks, B, S):
    T = B * S
    nb = S // WINDOW
    scale = A_HEAD_DIM ** -0.5

    def body(sink_ref, q_ref, kc_ref, vc_ref, kp_ref, vp_ref, o_ref, lse_ref):
        n = pl.program_id(1)
        valid = _band_mask(n)
        k = jnp.concatenate([kp_ref[...], kc_ref[...]], axis=0)
        v = jnp.concatenate([vp_ref[...], vc_ref[...]], axis=0)
        top = _row_iota((2 * WINDOW, 1)) < WINDOW
        lo = _lane_iota((WINDOW, LANES)) < A_HEAD_DIM
        lane = _lane_iota((WINDOW, LANES))
        lse_tile = jnp.zeros((WINDOW, LANES), F32)
        for g in range(A_KV_HEADS):
            kd, vd = _dup_kv_head(k, g), _dup_kv_head(v, g)
            for p in (2 * g, 2 * g + 1):
                qs = _stack_heads(q_ref[:, p * LANES:(p + 1) * LANES])
                s = jnp.where(valid, _dot_nt(qs, kd) * scale, NEG_BIG)
                sink = jnp.where(top, sink_ref[2 * p], sink_ref[2 * p + 1])
                m = jnp.maximum(jnp.max(s, axis=-1, keepdims=True), sink)
                e = jnp.exp(s - m)
                den = jnp.sum(e, axis=-1, keepdims=True) + jnp.exp(sink - m)
                o = _dot((e * (1.0 / den)).astype(BF16), vd)
                o_ref[:, p * LANES:(p + 1) * LANES] = jnp.where(lo, o[:WINDOW], o[WINDOW:])
                lse = m + jnp.log(den)
                lse_tile = jnp.where(lane == 2 * p, lse[:WINDOW], lse_tile)
                lse_tile = jnp.where(lane == 2 * p + 1, lse[WINDOW:], lse_tile)
        lse_ref[...] = lse_tile

    def cur(col, w):
        return pl.BlockSpec((WINDOW, w), lambda b, n: (b * nb + n, col))

    def prev(col):
        return pl.BlockSpec((WINDOW, LANES), lambda b, n: (b * nb + jnp.maximum(n - 1, 0), col))

    kcol, vcol = QKV_K // LANES, QKV_V // LANES
    return pl.pallas_call(
        body, name="attn_fwd", grid=(B, nb),
        in_specs=[pl.BlockSpec(memory_space=pltpu.SMEM), cur(0, A_WIDTH), cur(kcol, LANES), cur(vcol, LANES),
                  prev(kcol), prev(vcol)],
        out_specs=[cur(0, A_WIDTH), cur(0, LANES)],
        out_shape=[jax.ShapeDtypeStruct((T, A_WIDTH), F32), jax.ShapeDtypeStruct((T, LANES), F32)],
        compiler_params=_params("parallel", "parallel"),
    )(sinks, qkv, qkv, qkv, qkv, qkv)


def _attn_bwd(qkv, do, out, lse, sinks, B, S):
    T = B * S
    nb = S // WINDOW
    scale = A_HEAD_DIM ** -0.5

    def body(sink_ref, q_ref, kc_ref, vc_ref, kp_ref, vp_ref, do_ref, out_ref, lse_ref,
             dq_ref, dkv_ref, dsink_ref, carry_ref):
        b, n = pl.program_id(0), pl.program_id(1)
        active = n < nb
        nq = jnp.minimum(n, nb - 1)
        valid = _band_mask(nq)

        @pl.when((b == 0) & (n == 0))
        def _():
            dsink_ref[...] = jnp.zeros_like(dsink_ref)

        k = jnp.concatenate([kp_ref[...], kc_ref[...]], axis=0)
        v = jnp.concatenate([vp_ref[...], vc_ref[...]], axis=0)
        top = _row_iota((2 * WINDOW, 1)) < WINDOW
        lane = _lane_iota((WINDOW, LANES))
        lo = lane < A_HEAD_DIM
        lane2 = _lane_iota((2 * WINDOW, LANES))
        lse_tile = lse_ref[...]
        dk_tot = jnp.zeros((2 * WINDOW, LANES), F32)
        dv_tot = jnp.zeros((2 * WINDOW, LANES), F32)
        dsink_row = jnp.zeros((1, LANES), F32)
        for g in range(A_KV_HEADS):
            kd, vd = _dup_kv_head(k, g), _dup_kv_head(v, g)
            dk_acc = jnp.zeros((2 * WINDOW, LANES), F32)
            dv_acc = jnp.zeros((2 * WINDOW, LANES), F32)
            for p in (2 * g, 2 * g + 1):
                sl = slice(p * LANES, (p + 1) * LANES)
                qs = _stack_heads(q_ref[:, sl])
                dos = _stack_heads(do_ref[:, sl])
                s = jnp.where(valid, _dot_nt(qs, kd) * scale, NEG_BIG)
                lse0 = jnp.sum(jnp.where(lane == 2 * p, lse_tile, 0.0), axis=-1, keepdims=True)
                lse1 = jnp.sum(jnp.where(lane == 2 * p + 1, lse_tile, 0.0), axis=-1, keepdims=True)
                lse_col = jnp.concatenate([lse0, lse1], axis=0)
                prob = jnp.exp(s - lse_col)
                prod = do_ref[:, sl].astype(F32) * out_ref[:, sl]
                d0 = jnp.sum(jnp.where(lo, prod, 0.0), axis=-1, keepdims=True)
                d1 = jnp.sum(jnp.where(lo, 0.0, prod), axis=-1, keepdims=True)
                delta = jnp.concatenate([d0, d1], axis=0)
                dp = _dot_nt(dos, vd)
                ds = (prob * (dp - delta) * scale).astype(BF16)
                dq = _dot(ds, kd)
                dq_ref[:, sl] = jnp.where(lo, dq[:WINDOW], dq[WINDOW:]).astype(BF16)
                dk_acc += _dot_tn(ds, qs)
                dv_acc += _dot_tn(prob.astype(BF16), dos)
                sink = jnp.where(top, sink_ref[2 * p], sink_ref[2 * p + 1])
                w = -jnp.exp(sink - lse_col) * delta
                w0 = jnp.sum(w[:WINDOW], axis=0, keepdims=True)
                w1 = jnp.sum(w[WINDOW:], axis=0, keepdims=True)
                lane1 = _lane_iota((1, LANES))
                dsink_row += jnp.where(lane1 == 2 * p, w0, 0.0) + jnp.where(lane1 == 2 * p + 1, w1, 0.0)
            mine = (lane2 < A_HEAD_DIM) == (g == 0)
            dk_tot = jnp.where(mine, dk_acc + pltpu.roll(dk_acc, A_HEAD_DIM, 1), dk_tot)
            dv_tot = jnp.where(mine, dv_acc + pltpu.roll(dv_acc, A_HEAD_DIM, 1), dv_tot)
        gate = jnp.where(active, 1.0, 0.0)
        dsink_ref[0:1, :] += dsink_row * gate
        dkv_ref[:, 0:LANES] = (carry_ref[:, 0:LANES] + dk_tot[:WINDOW] * gate).astype(BF16)
        dkv_ref[:, LANES:] = (carry_ref[:, LANES:] + dv_tot[:WINDOW] * gate).astype(BF16)
        carry_ref[:, 0:LANES] = dk_tot[WINDOW:]
        carry_ref[:, LANES:] = dv_tot[WINDOW:]

    def cur(col, w):
        return pl.BlockSpec((WINDOW, w), lambda b, n: (b * nb + jnp.minimum(n, nb - 1), col))

    def prev(col):
        return pl.BlockSpec((WINDOW, LANES), lambda b, n: (b * nb + jnp.maximum(jnp.minimum(n, nb - 1) - 1, 0), col))

    lag = pl.BlockSpec((WINDOW, 2 * LANES), lambda b, n: (b * nb + jnp.maximum(n - 1, 0), 0))
    kcol, vcol = QKV_K // LANES, QKV_V // LANES
    return pl.pallas_call(
        body, name="attn_bwd", grid=(B, nb + 1),
        in_specs=[pl.BlockSpec(memory_space=pltpu.SMEM), cur(0, A_WIDTH), cur(kcol, LANES), cur(vcol, LANES),
                  prev(kcol), prev(vcol), cur(0, A_WIDTH), cur(0, A_WIDTH), cur(0, LANES)],
        out_specs=[cur(0, A_WIDTH), lag, pl.BlockSpec((8, LANES), lambda b, n: (0, 0))],
        out_shape=[jax.ShapeDtypeStruct((T, A_WIDTH), BF16), jax.ShapeDtypeStruct((T, 2 * LANES), BF16),
                   jax.ShapeDtypeStruct((8, LANES), F32)],
        scratch_shapes=[pltpu.VMEM((WINDOW, 2 * LANES), F32)],
        compiler_params=_params("arbitrary", "arbitrary"),
    )(sinks, qkv, qkv, qkv, qkv, qkv, do, out, lse)


GLA_TILE = 256
CHUNKS_PER_TILE = GLA_TILE // B_CHUNK


def _gla_factors(q_ref, k_ref, cum_ref):
    scale = B_KEY_DIM ** -0.5
    cum = cum_ref[...]
    shape = (B_CHUNK, B_KEY_WIDTH)
    last = jnp.concatenate([jnp.broadcast_to(cum_ref[pl.ds(c * B_CHUNK + B_CHUNK - 1, 1), :], shape)
                            for c in range(CHUNKS_PER_TILE)], axis=0)
    mid = jnp.concatenate([jnp.broadcast_to(cum_ref[pl.ds(c * B_CHUNK + B_CHUNK // 2 - 1, 1), :], shape)
                           for c in range(CHUNKS_PER_TILE)], axis=0)
    e_qm, e_km, e_qe, e_kd = jnp.exp(cum - mid), jnp.exp(mid - cum), jnp.exp(cum), jnp.exp(last - cum)
    qs = q_ref[...] * scale
    k = k_ref[...]
    return qs, k, (e_qm, e_km, e_qe, e_kd)


def _head_mask(shape, h):
    return (_lane_iota(shape) // B_KEY_DIM) == h


def _stack_masked(t):
    return jnp.concatenate([jnp.where(_head_mask(t.shape, h), t, 0.0) for h in range(B_HEADS)], axis=0).astype(BF16)


def _select_heads(t):
    shape = (B_CHUNK, B_KEY_WIDTH)
    out = jnp.zeros(shape, F32)
    for h in range(B_HEADS):
        out = jnp.where(_head_mask(shape, h), t[h * B_CHUNK:(h + 1) * B_CHUNK], out)
    return out


def _select_state(t):
    shape = (B_VAL_DIM, B_KEY_WIDTH)
    out = jnp.zeros(shape, F32)
    for h in range(B_HEADS):
        out = jnp.where(_head_mask(shape, h), t[h * B_VAL_DIM:(h + 1) * B_VAL_DIM], out)
    return out


def _rows_by_head(t):
    return jnp.concatenate([t[:, h * B_VAL_DIM:(h + 1) * B_VAL_DIM] for h in range(B_HEADS)], axis=0)


def _intra_mask():
    i, j = _row_iota((GLA_TILE, GLA_TILE)), _lane_iota((GLA_TILE, GLA_TILE))
    return (i // B_CHUNK == j // B_CHUNK) & (j <= i)


def _pair_stack(t, p):
    slab = t[:, p * LANES:(p + 1) * LANES]
    lo = _lane_iota(slab.shape) < B_KEY_DIM
    return jnp.concatenate([jnp.where(lo, slab, 0.0), jnp.where(lo, 0.0, slab)], axis=0).astype(BF16)


def _gla_fwd(q, k, cum, vb, B, S):
    T = B * S
    nt = S // GLA_TILE

    def body(q_ref, k_ref, cum_ref, v_ref, o_ref, st_all_ref, st_ref):
        @pl.when(pl.program_id(1) == 0)
        def _():
            st_ref[...] = jnp.zeros_like(st_ref)

        qs, kk, (e_qm, e_km, e_qe, e_kd) = _gla_factors(q_ref, k_ref, cum_ref)
        qm, km, qe, kd = qs * e_qm, kk * e_km, qs * e_qe, (kk * e_kd).astype(BF16)
        mask = _intra_mask()
        intra = []
        for p in range(B_HEADS // 2):
            a = _dot_nt(_pair_stack(qm, p), km[:, p * LANES:(p + 1) * LANES].astype(BF16))
            for hh in range(2):
                h = 2 * p + hh
                att = jnp.where(mask, a[hh * GLA_TILE:(hh + 1) * GLA_TILE], 0.0).astype(BF16)
                intra.append(_dot(att, v_ref[:, h * B_VAL_DIM:(h + 1) * B_VAL_DIM]))
        inter = []
        for c in range(CHUNKS_PER_TILE):
            rows = slice(c * B_CHUNK, (c + 1) * B_CHUNK)
            st = st_ref[...]
            st_all_ref[c] = st
            inter.append(_dot_nt(_stack_masked(qe[rows]), st.astype(BF16)))
            inc = _select_state(_dot_tn(v_ref[rows, :], kd[rows]))
            decay = jnp.exp(cum_ref[pl.ds(c * B_CHUNK + B_CHUNK - 1, 1), :])
            st_ref[...] = st * decay + inc
        for h in range(B_HEADS):
            oi = jnp.concatenate([inter[c][h * B_CHUNK:(h + 1) * B_CHUNK] for c in range(CHUNKS_PER_TILE)], axis=0)
            o_ref[:, h * B_VAL_DIM:(h + 1) * B_VAL_DIM] = intra[h] + oi

    def rows(w):
        return pl.BlockSpec((GLA_TILE, w), lambda b, t: (b * nt + t, 0))

    return pl.pallas_call(
        body, name="gla_fwd", grid=(B, nt),
        in_specs=[rows(B_KEY_WIDTH), rows(B_KEY_WIDTH), rows(B_KEY_WIDTH), rows(B_WIDTH)],
        out_specs=[rows(B_WIDTH),
                   pl.BlockSpec((CHUNKS_PER_TILE, B_VAL_DIM, B_KEY_WIDTH), lambda b, t: (b * nt + t, 0, 0))],
        out_shape=[jax.ShapeDtypeStruct((T, B_WIDTH), F32),
                   jax.ShapeDtypeStruct((T // B_CHUNK, B_VAL_DIM, B_KEY_WIDTH), F32)],
        scratch_shapes=[pltpu.VMEM((B_VAL_DIM, B_KEY_WIDTH), F32)],
        compiler_params=_params("arbitrary", "arbitrary"),
    )(q, k, cum, vb)


def _gla_bwd(q, k, cum, vb, do, st_all, B, S, wgrads):
    T = B * S
    nt = S // GLA_TILE
    scale = B_KEY_DIM ** -0.5
    nw = len(wgrads)

    def body(q_ref, k_ref, cum_ref, v_ref, do_ref, st_all_ref, *rest):
        g_refs, (dq_ref, dk_ref, dv_ref, dla_ref) = rest[:nw], rest[nw:nw + 4]
        rv_refs, (dst_ref, send_sems, recv_sems) = rest[nw + 4:2 * nw + 4], rest[2 * nw + 4:]
        x, y, c = _my_place()

        def wcopy(a, r):
            dx, dy, dc = FLIPS[r]
            return pltpu.make_async_remote_copy(
                src_ref=g_refs[a].at[4 * (x ^ dx) + 2 * (y ^ dy) + (c ^ dc)], dst_ref=rv_refs[a].at[r],
                send_sem=send_sems.at[a, r], recv_sem=recv_sems.at[a, r],
                device_id=(x ^ dx, y ^ dy, c ^ dc), device_id_type=MESH)

        @pl.when((pl.program_id(0) == 0) & (pl.program_id(1) == 0))
        def _():
            for a in range(nw):
                for r in range(len(FLIPS)):
                    wcopy(a, r).start()

        @pl.when(pl.program_id(1) == 0)
        def _():
            dst_ref[...] = jnp.zeros_like(dst_ref)

        qs, kk, (e_qm, e_km, e_qe, e_kd) = _gla_factors(q_ref, k_ref, cum_ref)
        qm, km, qe, kd = qs * e_qm, kk * e_km, qs * e_qe, kk * e_kd
        mask = _intra_mask()
        dqm_slabs, dkm_slabs, dv_intra = [], [], []
        for p in range(B_HEADS // 2):
            qm_st = _pair_stack(qm, p)
            km_p = km[:, p * LANES:(p + 1) * LANES].astype(BF16)
            a = _dot_nt(qm_st, km_p)
            da_blocks, dqm_h = [], []
            for hh in range(2):
                h = 2 * p + hh
                vs = slice(h * B_VAL_DIM, (h + 1) * B_VAL_DIM)
                att = jnp.where(mask, a[hh * GLA_TILE:(hh + 1) * GLA_TILE], 0.0).astype(BF16)
                dv_intra.append(_dot_tn(att, do_ref[:, vs]))
                da = jnp.where(mask, _dot_nt(do_ref[:, vs], v_ref[:, vs]), 0.0).astype(BF16)
                da_blocks.append(da)
                dqm_h.append(_dot(da, km_p))
            lo = _lane_iota((GLA_TILE, LANES)) < B_KEY_DIM
            dqm_slabs.append(jnp.where(lo, dqm_h[0], dqm_h[1]))
            dkm_slabs.append(_dot_tn(jnp.concatenate(da_blocks, axis=0), qm_st))
        dqm = jnp.concatenate(dqm_slabs, axis=1)
        dkm = jnp.concatenate(dkm_slabs, axis=1)

        dqe_c, dkd_c, dv_inter, tail_c = ([None] * CHUNKS_PER_TILE for _ in range(4))
        for c in reversed(range(CHUNKS_PER_TILE)):
            rows = slice(c * B_CHUNK, (c + 1) * B_CHUNK)
            dst = dst_ref[...]
            dst_b = dst.astype(BF16)
            dv_inter[c] = _dot_nt(_stack_masked(kd[rows]), dst_b)
            dkd_c[c] = _select_heads(_dot(_rows_by_head(v_ref[rows, :]), dst_b))
            do_c = do_ref[rows, :]
            dqe_c[c] = _select_heads(_dot(_rows_by_head(do_c), st_all_ref[c].astype(BF16)))
            contrib = _select_state(_dot_tn(do_c, qe[rows].astype(BF16)))
            decay = jnp.exp(cum_ref[pl.ds(c * B_CHUNK + B_CHUNK - 1, 1), :])
            tail = (jnp.sum(kk[rows] * dkd_c[c] * e_kd[rows], axis=0, keepdims=True)
                    + decay * jnp.sum(st_all_ref[c] * dst, axis=0, keepdims=True))
            tail_c[c] = jnp.broadcast_to(tail, (B_CHUNK, B_KEY_WIDTH))
            dst_ref[...] = dst * decay + contrib
        dqe = jnp.concatenate(dqe_c, axis=0)
        dkd = jnp.concatenate(dkd_c, axis=0)
        dqs = dqm * e_qm + dqe * e_qe
        dk = dkm * e_km + dkd * e_kd
        dq_ref[...] = (dqs * scale).astype(BF16)
        dk_ref[...] = dk.astype(BF16)
        for h in range(B_HEADS):
            dvi = jnp.concatenate([dv_inter[c][h * B_CHUNK:(h + 1) * B_CHUNK] for c in range(CHUNKS_PER_TILE)], axis=0)
            dv_ref[:, h * B_VAL_DIM:(h + 1) * B_VAL_DIM] = (dv_intra[h] + dvi).astype(BF16)
        dd = qs * dqs - kk * dk
        i, j = _row_iota((GLA_TILE, GLA_TILE)), _lane_iota((GLA_TILE, GLA_TILE))
        upper = ((i // B_CHUNK == j // B_CHUNK) & (j >= i)).astype(BF16)
        hi, mid, lo3 = _split3(dd)
        dla_ref[...] = _dot(upper, hi) + _dot(upper, mid) + _dot(upper, lo3) + jnp.concatenate(tail_c, axis=0)

        @pl.when((pl.program_id(0) == B - 1) & (pl.program_id(1) == nt - 1))
        def _():
            for a in range(nw):
                for r in range(len(FLIPS)):
                    wcopy(a, r).wait()

    def rows(w):
        return pl.BlockSpec((GLA_TILE, w), lambda b, t: (b * nt + nt - 1 - t, 0))

    res = pl.pallas_call(
        body, name="gla_bwd", grid=(B, nt),
        in_specs=[rows(B_KEY_WIDTH), rows(B_KEY_WIDTH), rows(B_KEY_WIDTH), rows(B_WIDTH), rows(B_WIDTH),
                  pl.BlockSpec((CHUNKS_PER_TILE, B_VAL_DIM, B_KEY_WIDTH), lambda b, t: (b * nt + nt - 1 - t, 0, 0))]
                 + _any_specs(nw),
        out_specs=[rows(B_KEY_WIDTH), rows(B_KEY_WIDTH), rows(B_WIDTH), rows(B_KEY_WIDTH)] + _any_specs(nw),
        out_shape=[jax.ShapeDtypeStruct((T, B_KEY_WIDTH), BF16), jax.ShapeDtypeStruct((T, B_KEY_WIDTH), BF16),
                   jax.ShapeDtypeStruct((T, B_WIDTH), BF16), jax.ShapeDtypeStruct((T, B_KEY_WIDTH), F32)]
                  + [jax.ShapeDtypeStruct((len(FLIPS), *g.shape[1:]), g.dtype) for g in wgrads],
        scratch_shapes=[pltpu.VMEM((B_VAL_DIM, B_KEY_WIDTH), F32),
                        pltpu.SemaphoreType.DMA((nw, len(FLIPS))), pltpu.SemaphoreType.DMA((nw, len(FLIPS)))],
        compiler_params=_params("arbitrary", "arbitrary"),
    )(q, k, cum, vb, do, st_all, *wgrads)
    return res[:4], res[4:]


def _merge(x2, tgt2, attn, za, o_gla, zb, ga, gb, w_oa, w_ob, w_o, g_gla, g_final):
    T = x2.shape[0]
    tm = 256
    last = T // tm - 1

    def body(x_ref, tgt_ref, attn_ref, za_ref, og_ref, zb_ref, ga_ref, gb_ref,
             woa_ref, wob_ref, wo_ref, gg_ref, gf_ref,
             dxres_ref, dattn_ref, dog_ref, dza_ref, dzb_ref, dga_ref, dgb_ref,
             dwo_ref, dwoa_ref, dwob_ref, small_ref,
             awo_ref, awoa_ref, awob_ref, agf_ref, agg_ref, loss_ref):
        @pl.when(pl.program_id(0) == 0)
        def _():
            for r in (awo_ref, awoa_ref, awob_ref, agf_ref, agg_ref, loss_ref):
                r[...] = jnp.zeros_like(r)

        za_v = za_ref[...]
        sig_za = _sigmoid(za_v)
        silu_a = za_v * sig_za
        attn_v = attn_ref[...]
        oa = (attn_v * silu_a).astype(BF16)
        ya = _dot(oa, woa_ref[...])
        og = og_ref[...]
        zb_v = zb_ref[...]
        sig_zb = _sigmoid(zb_v)
        silu_b = zb_v * sig_zb
        gg = gg_ref[...]
        on_parts, rinv_parts = [], []
        for h in range(B_HEADS):
            seg = og[:, h * B_VAL_DIM:(h + 1) * B_VAL_DIM]
            rinv = lax.rsqrt(jnp.mean(seg * seg, axis=-1, keepdims=True) + NORM_EPS)
            rinv_parts.append(rinv)
            on_parts.append(seg * rinv)
        on = jnp.concatenate(on_parts, axis=1)
        obn = on * gg
        ob = (obn * silu_b).astype(BF16)
        yb = _dot(ob, wob_ref[...])
        sig_a, sig_b = _sigmoid(ga_ref[...]), _sigmoid(gb_ref[...])
        merged = (sig_a * ya + sig_b * yb).astype(BF16)
        out = x_ref[...] + _dot(merged, wo_ref[...])
        rf = lax.rsqrt(jnp.mean(out * out, axis=-1, keepdims=True) + NORM_EPS)
        nrm = out * rf
        gf = gf_ref[...]
        err = nrm * gf - tgt_ref[...]
        loss_ref[...] += jnp.sum(err * err) * (0.5 / D_MODEL)

        dy = err * (1.0 / D_MODEL)
        agf_ref[...] += jnp.sum(dy * nrm, axis=0, keepdims=True)
        dn = dy * gf
        dout = rf * (dn - nrm * jnp.mean(dn * nrm, axis=-1, keepdims=True))
        dxres_ref[...] = dout
        dout_b = dout.astype(BF16)
        dmerged = _dot_nt(dout_b, wo_ref[...])
        awo_ref[...] += _dot_tn(merged, dout_b)
        dya = dmerged * sig_a
        dyb = dmerged * sig_b
        dga_ref[...] = (dmerged * ya * sig_a * (1.0 - sig_a)).astype(BF16)
        dgb_ref[...] = (dmerged * yb * sig_b * (1.0 - sig_b)).astype(BF16)
        dya_b, dyb_b = dya.astype(BF16), dyb.astype(BF16)
        awoa_ref[...] += _dot_tn(oa, dya_b)
        awob_ref[...] += _dot_tn(ob, dyb_b)
        doa = _dot_nt(dya_b, woa_ref[...])
        dattn_ref[...] = (doa * silu_a).astype(BF16)
        dza_ref[...] = (doa * attn_v * (sig_za * (1.0 + za_v * (1.0 - sig_za)))).astype(BF16)
        dob = _dot_nt(dyb_b, wob_ref[...])
        dzb_ref[...] = (dob * obn * (sig_zb * (1.0 + zb_v * (1.0 - sig_zb)))).astype(BF16)
        dobn = dob * silu_b
        agg_ref[...] += jnp.sum(dobn * on, axis=0, keepdims=True)
        don = dobn * gg
        for h in range(B_HEADS):
            sl = slice(h * B_VAL_DIM, (h + 1) * B_VAL_DIM)
            don_h, on_h = don[:, sl], on[:, sl]
            dog_ref[:, sl] = (rinv_parts[h] * (don_h - on_h * jnp.mean(don_h * on_h, axis=-1, keepdims=True))).astype(BF16)

        @pl.when(pl.program_id(0) == last)
        def _():
            for j in range(N_DEV):
                dwo_ref[j] = awo_ref[j * SHARD_OUT:(j + 1) * SHARD_OUT, :].astype(BF16)
                dwoa_ref[j] = awoa_ref[:, j * SHARD_OUT:(j + 1) * SHARD_OUT].astype(BF16)
                dwob_ref[j] = awob_ref[:, j * SHARD_OUT:(j + 1) * SHARD_OUT].astype(BF16)
            small_ref[...] = jnp.zeros_like(small_ref)
            _put_rows(small_ref, SMALL_G_FINAL, agf_ref[...])
            _put_rows(small_ref, SMALL_G_GLA, agg_ref[...])
            small_ref[SMALL_LOSS:SMALL_LOSS + 1, :] = loss_ref[...]

    def rows(w):
        return pl.BlockSpec((tm, w), lambda i: (i, 0))

    def whole(shape):
        nd = len(shape)
        return pl.BlockSpec(shape, lambda i: (0,) * nd)

    outs = [((T, D_MODEL), F32, rows(D_MODEL)), ((T, A_WIDTH), BF16, rows(A_WIDTH)), ((T, B_WIDTH), BF16, rows(B_WIDTH)),
            ((T, A_WIDTH), BF16, rows(A_WIDTH)), ((T, B_WIDTH), BF16, rows(B_WIDTH)),
            ((T, D_MODEL), BF16, rows(D_MODEL)), ((T, D_MODEL), BF16, rows(D_MODEL)),
            ((N_DEV, SHARD_OUT, D_MODEL), BF16, whole((N_DEV, SHARD_OUT, D_MODEL))),
            ((N_DEV, A_WIDTH, SHARD_OUT), BF16, whole((N_DEV, A_WIDTH, SHARD_OUT))),
            ((N_DEV, B_WIDTH, SHARD_OUT), BF16, whole((N_DEV, B_WIDTH, SHARD_OUT))),
            ((SMALL_SINKS, LANES), F32, whole((SMALL_SINKS, LANES)))]
    return pl.pallas_call(
        body, name="merge", grid=(T // tm,),
        in_specs=[rows(D_MODEL), rows(D_MODEL), rows(A_WIDTH), rows(A_WIDTH), rows(B_WIDTH), rows(B_WIDTH),
                  rows(D_MODEL), rows(D_MODEL),
                  _const_spec((A_WIDTH, D_MODEL)), _const_spec((B_WIDTH, D_MODEL)), _const_spec((D_MODEL, D_MODEL)),
                  _const_spec((1, B_WIDTH)), _const_spec((1, D_MODEL))],
        out_specs=[o[2] for o in outs],
        out_shape=[jax.ShapeDtypeStruct(o[0], o[1]) for o in outs],
        scratch_shapes=[pltpu.VMEM((D_MODEL, D_MODEL), F32), pltpu.VMEM((A_WIDTH, D_MODEL), F32),
                        pltpu.VMEM((B_WIDTH, D_MODEL), F32), pltpu.VMEM((1, D_MODEL), F32), pltpu.VMEM((1, B_WIDTH), F32),
                        pltpu.VMEM((1, LANES), F32)],
        compiler_params=_params("arbitrary"),
    )(x2, tgt2, attn, za, o_gla, zb, ga, gb, w_oa, w_ob, w_o, g_gla, g_final)


def _in_proj_bwd(x2, dxres, cosf, sinf, g_in, wt_pad, wa_pad, parts):
    T = x2.shape[0]
    tm = 256
    last = T // tm - 1
    base = SMALL_G_IN

    def body(x_ref, dxres_ref, cos_ref, sin_ref, g_ref, wt_ref, wa_ref,
             dq_ref, dkv_ref, dza_ref, dqb_ref, dkb_ref, dvb_ref, dzb_ref, dla_ref, u_ref, alr_ref, dga_ref, dgb_ref,
             dx_ref, dsh_ref, small_ref, dproj_ref, agin_ref, aba_ref, awa_ref):
        @pl.when(pl.program_id(0) == 0)
        def _():
            for r in (agin_ref, aba_ref, awa_ref):
                r[...] = jnp.zeros_like(r)

        cos, nsin = cos_ref[...], -sin_ref[...]
        for s in range(A_WIDTH // LANES):
            sl = slice(s * LANES, (s + 1) * LANES)
            dproj_ref[:, sl] = _rope_slab(dq_ref[:, sl].astype(F32), cos, nsin).astype(BF16)
        dproj_ref[:, QKV_K:QKV_V] = _rope_slab(dkv_ref[:, 0:LANES].astype(F32), cos, nsin).astype(BF16)
        dproj_ref[:, QKV_V:QKV_W] = dkv_ref[:, LANES:]

        def put(name, val):
            a, b = SEG[name]
            dproj_ref[:, a:b] = val

        put("za", dza_ref[...])
        put("qb", dqb_ref[...])
        put("kb", dkb_ref[...])
        put("vb", dvb_ref[...])
        put("zb", dzb_ref[...])
        put("ga", dga_ref[...])
        put("gb", dgb_ref[...])
        du = dla_ref[...] * (1.0 / B_GATE_TEMP) * _sigmoid(-u_ref[...])
        aba_ref[...] += jnp.sum(du, axis=0, keepdims=True)
        du_b = du.astype(BF16)
        awa_ref[...] += _dot_tn(alr_ref[...], du_b)
        put("alr", _dot_nt(du_b, wa_ref[...]).astype(BF16))

        for j in range(N_DEV):
            col = (j % 2) * SHARD_PAD
            for a, b in _shard_pad_cols(j):
                dsh_ref[j // 2, :, col:col + b - a] = dproj_ref[:, a:b]
                col += b - a
            dsh_ref[j // 2, :, col:(j % 2 + 1) * SHARD_PAD] = jnp.zeros((tm, SHARD_PAD - SHARD_IN), BF16)

        dh = _dot(dproj_ref[...], wt_ref[...])
        x = x_ref[...]
        r = lax.rsqrt(jnp.mean(x * x, axis=-1, keepdims=True) + NORM_EPS)
        nrm = x * r
        agin_ref[...] += jnp.sum(dh * nrm, axis=0, keepdims=True)
        dn = dh * g_ref[...]
        dx_ref[...] = dxres_ref[...] + r * (dn - nrm * jnp.mean(dn * nrm, axis=-1, keepdims=True))

        @pl.when(pl.program_id(0) == last)
        def _():
            small_ref[...] = jnp.zeros_like(small_ref)
            _put_rows(small_ref, SMALL_G_IN - base, agin_ref[...])
            _put_rows(small_ref, SMALL_B_ALPHA - base, aba_ref[...])
            for half in range(B_KEY_WIDTH // LANES):
                r0 = SMALL_W_ALPHA - base + half * B_GATE_RANK
                small_ref[r0:r0 + B_GATE_RANK, :] = awa_ref[0:B_GATE_RANK, half * LANES:(half + 1) * LANES]

    def rows(w):
        return pl.BlockSpec((tm, w), lambda i: (i, 0))

    names = ["dq", "dkv", "dza", "dqb", "dkb", "dvb", "dzb", "dla", "u", "alr", "dga", "dgb"]
    return pl.pallas_call(
        body, name="in_proj_bwd", grid=(T // tm,),
        in_specs=[rows(D_MODEL), rows(D_MODEL), rows(LANES), rows(LANES), _const_spec((1, D_MODEL)),
                  _const_spec((D_IN_PAD, D_MODEL)), _const_spec((RANK_PAD, B_KEY_WIDTH))]
                 + [rows(parts[n].shape[1]) for n in names],
        out_specs=[rows(D_MODEL), pl.BlockSpec((N_CHIPS, tm, 2 * SHARD_PAD), lambda i: (0, i, 0)),
                   pl.BlockSpec((SMALL_ROWS - base, LANES), lambda i: (0, 0))],
        out_shape=[jax.ShapeDtypeStruct((T, D_MODEL), F32), jax.ShapeDtypeStruct((N_CHIPS, T, 2 * SHARD_PAD), BF16),
                   jax.ShapeDtypeStruct((SMALL_ROWS - base, LANES), F32)],
        scratch_shapes=[pltpu.VMEM((tm, D_IN_PAD), BF16), pltpu.VMEM((1, D_MODEL), F32), pltpu.VMEM((1, B_KEY_WIDTH), F32),
                        pltpu.VMEM((RANK_PAD, B_KEY_WIDTH), F32)],
        compiler_params=_params("arbitrary"),
    )(x2, dxres, cosf, sinf, g_in, wt_pad, wa_pad, *[parts[n] for n in names])


FLIPS = [(dx, dy, dc) for dx in (0, 1) for dy in (0, 1) for dc in (0, 1)][1:]


def _my_place():
    return lax.axis_index("x"), lax.axis_index("y"), lax.axis_index("c")


def _any_specs(n):
    return [pl.BlockSpec(memory_space=pl.ANY)] * n


def _gather_first(shards, pos_col):
    n = len(shards)
    T = pos_col.shape[0]
    rows_per_pass = math.gcd(T, 512)
    invf, sign = _rope_lane_constants()

    def body(*refs):
        ins, (pos_ref, invf_ref, sign_ref) = refs[:n], refs[n:n + 3]
        outs, (cos_ref, sin_ref) = refs[n + 3:2 * n + 3], refs[2 * n + 3:2 * n + 5]
        send_sems, recv_sems, local_sems = refs[2 * n + 5:]
        x, y, c = _my_place()
        me, sibling = (x, y, c), (x, y, 1 - c)
        chips = [(1 - x, y), (x, 1 - y), (1 - x, 1 - y)]

        def block(a, px, py, pc):
            return outs[a].at[4 * px + 2 * py + pc]

        def copy(a, k, blk, to, src=None):
            return pltpu.make_async_remote_copy(
                src_ref=block(a, *blk) if src is None else src, dst_ref=block(a, *blk),
                send_sem=send_sems.at[a, k], recv_sem=recv_sems.at[a, k], device_id=to, device_id_type=MESH)

        mine = [pltpu.make_async_copy(ins[a], block(a, *me), local_sems.at[a]) for a in range(n)]
        for cp in mine:
            cp.start()
        first = []
        for a in range(n):
            first.append(copy(a, 0, me, sibling, src=ins[a]))
            first += [copy(a, 1 + j, me, (*chip, c), src=ins[a]) for j, chip in enumerate(chips)]
        for cp in first:
            cp.start()

        def tables(i, carry):
            rows = pl.ds(pl.multiple_of(i * rows_per_pass, rows_per_pass), rows_per_pass)
            ang = pos_ref[rows, :].astype(F32) * invf_ref[...]
            cos_ref[rows, :] = jnp.cos(ang)
            sin_ref[rows, :] = jnp.sin(ang) * sign_ref[...]
            return carry

        lax.fori_loop(0, T // rows_per_pass, tables, 0)

        passed = []
        for j, chip in enumerate(chips):
            for a in range(n):
                copy(a, 1 + j, (*chip, c), me).wait_recv()
                fwd = copy(a, 4 + j, (*chip, c), sibling)
                fwd.start()
                passed.append(fwd)
        for a in range(n):
            copy(a, 0, sibling, me).wait_recv()
            for j, chip in enumerate(chips):
                copy(a, 4 + j, (*chip, 1 - c), me).wait_recv()
        for cp in first + passed:
            cp.wait_send()
        for cp in mine:
            cp.wait()

    vmem = pl.BlockSpec(memory_space=pltpu.VMEM)
    res = pl.pallas_call(
        body, name="gather_weights",
        in_specs=_any_specs(n) + [vmem] * 3, out_specs=_any_specs(n) + [vmem] * 2,
        out_shape=[jax.ShapeDtypeStruct((N_DEV, *s.shape), s.dtype) for s in shards]
                  + [jax.ShapeDtypeStruct((T, LANES), F32)] * 2,
        scratch_shapes=[pltpu.SemaphoreType.DMA((n, 7)), pltpu.SemaphoreType.DMA((n, 7)), pltpu.SemaphoreType.DMA((n,))],
        compiler_params=pltpu.CompilerParams(vmem_limit_bytes=V7X_VMEM_LIMIT),
    )(*shards, pos_col, invf, sign)
    return res[:n], res[n], res[n + 1]


def _w_in_grad_rs(h, dsh, chip_order, small):
    T = h.shape[0]
    tk = math.gcd(T, 2048)
    nk = T // tk
    chip_flips = [(1, 1), (1, 0), (0, 1)]
    n_steps = len(chip_flips) + 1

    def body(order_ref, h_ref, d_ref, s_ref, own_ref, recv_ref, sall_ref,
             acc_ref, stage_ref, send_sems, recv_sems, ssend_sems, srecv_sems, local_sem):
        i, kk = pl.program_id(0), pl.program_id(1)
        x, y, c = _my_place()
        my_dev = 4 * x + 2 * y + c

        def small_copy(r, slot):
            dx, dy, dc = FLIPS[r]
            return pltpu.make_async_remote_copy(
                src_ref=s_ref, dst_ref=sall_ref.at[slot], send_sem=ssend_sems.at[r], recv_sem=srecv_sems.at[r],
                device_id=(x ^ dx, y ^ dy, c ^ dc), device_id_type=MESH)

        keep_small = pltpu.make_async_copy(s_ref, sall_ref.at[my_dev], local_sem)

        def shard_copy(slot, dx, dy, dc):
            r = FLIPS.index((dx, dy, dc))
            return pltpu.make_async_remote_copy(
                src_ref=stage_ref.at[slot, c ^ dc], dst_ref=recv_ref.at[r], send_sem=send_sems.at[r],
                recv_sem=recv_sems.at[r], device_id=(x ^ dx, y ^ dy, c ^ dc), device_id_type=MESH)

        def stage(slot):
            stage_ref[slot, 0] = acc_ref[0:SHARD_PAD, :].astype(BF16)
            stage_ref[slot, 1] = acc_ref[SHARD_PAD:2 * SHARD_PAD, :].astype(BF16)

        @pl.when((i == 0) & (kk == 0))
        def _():
            keep_small.start()
            for r in range(len(FLIPS)):
                small_copy(r, my_dev).start()

        @pl.when(kk == 0)
        def _():
            acc_ref[...] = jnp.zeros_like(acc_ref)

        acc_ref[...] += _dot_tn(d_ref[...], h_ref[...])

        for t, (dx, dy) in enumerate(chip_flips):
            @pl.when((i == t) & (kk == nk - 1))
            def _(t=t, dx=dx, dy=dy):
                if t >= 2:
                    for dc in (0, 1):
                        shard_copy(t % 2, *chip_flips[t - 2], dc).wait_send()
                stage(t % 2)
                for dc in (0, 1):
                    shard_copy(t % 2, dx, dy, dc).start()

        @pl.when((i == n_steps - 1) & (kk == nk - 1))
        def _():
            for dc in (0, 1):
                shard_copy(1, *chip_flips[1], dc).wait_send()
            stage(1)
            shard_copy(1, 0, 0, 1).start()

            @pl.when(c == 0)
            def _():
                own_ref[...] = acc_ref[0:SHARD_PAD, :]

            @pl.when(c == 1)
            def _():
                own_ref[...] = acc_ref[SHARD_PAD:2 * SHARD_PAD, :]

            for dc in (0, 1):
                shard_copy(0, *chip_flips[2], dc).wait_send()
            shard_copy(1, 0, 0, 1).wait_send()
            for r, (dx, dy, dc) in enumerate(FLIPS):
                shard_copy(0, dx, dy, dc).wait_recv()
                small_copy(r, 4 * (x ^ dx) + 2 * (y ^ dy) + (c ^ dc)).wait_recv()
                small_copy(r, my_dev).wait_send()
            keep_small.wait()

    return pl.pallas_call(
        body, name="w_in_grad_rs",
        grid_spec=pltpu.PrefetchScalarGridSpec(
            num_scalar_prefetch=1, grid=(n_steps, nk),
            in_specs=[pl.BlockSpec((tk, D_MODEL), lambda i, kk, order: (kk, 0)),
                      pl.BlockSpec((None, tk, 2 * SHARD_PAD), lambda i, kk, order: (order[i], kk, 0)),
                      pl.BlockSpec(memory_space=pl.ANY)],
            out_specs=[pl.BlockSpec((SHARD_PAD, D_MODEL), lambda i, kk, order: (0, 0)),
                       pl.BlockSpec(memory_space=pl.ANY), pl.BlockSpec(memory_space=pl.ANY)],
            scratch_shapes=[pltpu.VMEM((2 * SHARD_PAD, D_MODEL), F32), pltpu.VMEM((2, 2, SHARD_PAD, D_MODEL), BF16),
                            pltpu.SemaphoreType.DMA((7,)), pltpu.SemaphoreType.DMA((7,)),
                            pltpu.SemaphoreType.DMA((7,)), pltpu.SemaphoreType.DMA((7,)), pltpu.SemaphoreType.DMA]),
        out_shape=[jax.ShapeDtypeStruct((SHARD_PAD, D_MODEL), F32),
                   jax.ShapeDtypeStruct((len(FLIPS), SHARD_PAD, D_MODEL), BF16),
                   jax.ShapeDtypeStruct((N_DEV, *small.shape), F32)],
        compiler_params=_params("arbitrary", "arbitrary"),
    )(chip_order, h, dsh, small)


def _adam_math(w, g, m, v):
    m_new = ADAM_B1 * m + (1.0 - ADAM_B1) * g
    v_new = ADAM_B2 * v + (1.0 - ADAM_B2) * (g * g)
    m_hat = m_new / (1.0 - ADAM_B1 ** ADAM_STEP)
    v_hat = v_new / (1.0 - ADAM_B2 ** ADAM_STEP)
    delta = -ADAM_LR * (m_hat / (jnp.sqrt(v_hat) + ADAM_EPS) + ADAM_WD * w)
    return delta, m_new, v_new


def _adam_big(name, own, own_idx, recv, w, m, v):
    rw, cw = w.shape
    rp = own.shape[1]
    steps = 8
    by_cols = rp != rw
    blk_w = (rw, cw // steps) if by_cols else (rw // steps, cw)
    blk_g = (rp, cw // steps) if by_cols else (rw // steps, cw)
    at = (lambda i: (0, i)) if by_cols else (lambda i: (i, 0))

    def body(idx_ref, o_ref, r_ref, w_ref, m_ref, v_ref, g_ref, d_ref, mo_ref, vo_ref):
        g = o_ref[...].astype(F32)
        for r in range(len(FLIPS)):
            g = g + r_ref[r].astype(F32)
        g = g[0:blk_w[0], :]
        g_ref[...] = g
        d_ref[...], mo_ref[...], vo_ref[...] = _adam_math(w_ref[...], g, m_ref[...], v_ref[...])

    spec = pl.BlockSpec(blk_w, lambda i, idx_ref: at(i))
    return pl.pallas_call(
        body, name=name,
        grid_spec=pltpu.PrefetchScalarGridSpec(
            num_scalar_prefetch=1, grid=(steps,),
            in_specs=[pl.BlockSpec((None, *blk_g), lambda i, idx_ref: (idx_ref[0], *at(i))),
                      pl.BlockSpec((len(FLIPS), *blk_g), lambda i, idx_ref: (0, *at(i))), spec, spec, spec],
            out_specs=[spec] * 4),
        out_shape=[jax.ShapeDtypeStruct((rw, cw), F32)] * 4,
        compiler_params=_params("parallel"),
    )(own_idx, own, recv, w, m, v)


def _adam_small(small_all, params):
    flat = [a for triple in params for a in triple]
    n_par = len(params)

    def body(s_ref, *refs):
        ins, outs, loss_ref = refs[:3 * n_par], refs[3 * n_par:-1], refs[-1]
        g_slab = s_ref[0]
        for dev in range(1, N_DEV):
            g_slab = g_slab + s_ref[dev]
        loss_ref[...] = g_slab[SMALL_LOSS:SMALL_LOSS + 1, :]
        dev = 4 * lax.axis_index("x") + 2 * lax.axis_index("y") + lax.axis_index("c")
        alpha_full = jnp.concatenate([g_slab[SMALL_W_ALPHA + half * B_GATE_RANK:SMALL_W_ALPHA + (half + 1) * B_GATE_RANK]
                                      for half in range(B_KEY_WIDTH // LANES)], axis=1)
        alpha_mine = pltpu.roll(alpha_full, (B_KEY_WIDTH - dev * SHARD_ALPHA) % B_KEY_WIDTH, 1)[:, 0:SHARD_ALPHA]
        grads = [_take_rows(g_slab, SMALL_G_IN, D_MODEL // LANES), _take_rows(g_slab, SMALL_G_FINAL, D_MODEL // LANES),
                 _take_rows(g_slab, SMALL_G_GLA, B_WIDTH // LANES), _take_rows(g_slab, SMALL_B_ALPHA, B_KEY_WIDTH // LANES),
                 g_slab[SMALL_SINKS:SMALL_SINKS + 1, 0:A_HEADS], alpha_mine]
        for i, g in enumerate(grads):
            w_ref, m_ref, v_ref = ins[3 * i:3 * i + 3]
            delta, m_new, v_new = _adam_math(w_ref[...], g, m_ref[...], v_ref[...])
            outs[4 * i][...] = g
            outs[4 * i + 1][...] = delta
            outs[4 * i + 2][...] = m_new
            outs[4 * i + 3][...] = v_new

    res = pl.pallas_call(
        body, name="adam_small",
        out_shape=[jax.ShapeDtypeStruct(t[0].shape, F32) for t in params for _ in range(4)]
                  + [jax.ShapeDtypeStruct((1, LANES), F32)],
    )(small_all, *flat)
    return [res[4 * i:4 * i + 4] for i in range(n_par)], res[-1]


def _local_step(x, cosf, sinf, loss_target, g_in, wt_sh, wa_pad, b_alpha, sinks, g_gla, out_shards, g_final, chip_order):
    B, S, _ = x.shape
    T = B * S
    x2 = x.reshape(T, D_MODEL)
    tgt2 = loss_target.reshape(T, D_MODEL)
    f, (g_woa, g_wob, g_wo) = _in_proj(x2, cosf, sinf, g_in, wt_sh, wa_pad, b_alpha, out_shards)
    w_oa = jnp.concatenate([g_woa[j] for j in range(N_DEV)], axis=1)
    w_ob = jnp.concatenate([g_wob[j] for j in range(N_DEV)], axis=1)
    w_o = g_wo.reshape(D_MODEL, D_MODEL)
    attn, lse = _attn_fwd(f["qkv"], sinks, B, S)
    o_gla, st_all = _gla_fwd(f["q"], f["k"], f["cum"], f["vb"], B, S)
    (dxres, dattn, dog, dza, dzb, dga, dgb, dw_o, dw_oa, dw_ob, small_a) = _merge(
        x2, tgt2, attn, f["za"], o_gla, f["zb"], f["ga"], f["gb"], w_oa, w_ob, w_o, g_gla, g_final)
    dq, dkv, dsink = _attn_bwd(f["qkv"], dattn, attn, lse, sinks, B, S)
    (dqb, dkb, dvb, dla), (rv_o, rv_oa, rv_ob) = _gla_bwd(f["q"], f["k"], f["cum"], f["vb"], dog, st_all, B, S,
                                                        [dw_o, dw_oa, dw_ob])
    parts = dict(dq=dq, dkv=dkv, dza=dza, dqb=dqb, dkb=dkb, dvb=dvb, dzb=dzb, dla=dla, u=f["u"], alr=f["alr"],
                 dga=dga, dgb=dgb)
    dx, dsh, small_c = _in_proj_bwd(x2, dxres, cosf, sinf, g_in, f["wt_pad"], wa_pad, parts)
    small = jnp.concatenate([small_a, dsink, small_c], axis=0)
    own_in, rv_in, small_all = _w_in_grad_rs(f["h"], dsh, chip_order, small)
    return dict(grad_x=dx.reshape(B, S, D_MODEL), own_in=own_in, rv_in=rv_in,
                own_o=dw_o, rv_o=rv_o, own_oa=dw_oa, rv_oa=rv_oa, own_ob=dw_ob, rv_ob=rv_ob, small_all=small_all)


def kernel(x, positions, g_in, w_in, w_alpha_up, b_alpha, attn_sinks, g_gla_norm, w_out_a, w_out_b, w_o, g_final, loss_target, m_g_in, m_w_in, m_w_alpha_up, m_b_alpha, m_attn_sinks, m_g_gla_norm, m_w_out_a, m_w_out_b, m_w_o, m_g_final, v_g_in, v_w_in, v_w_alpha_up, v_b_alpha, v_attn_sinks, v_g_gla_norm, v_w_out_a, v_w_out_b, v_w_o, v_g_final):
    xi, yi, ci = _my_place()
    dev_idx = (4 * xi + 2 * yi + ci).reshape(1).astype(jnp.int32)
    chip = 2 * xi + yi
    chip_order = jnp.stack([chip ^ 3, chip ^ 2, chip ^ 1, chip]).astype(jnp.int32)

    (g_win, g_wa), cosf, sinf = _gather_first(
        [jnp.pad(w_in[0].T.astype(BF16), ((0, SHARD_PAD - SHARD_IN), (0, 0))), w_alpha_up[0].astype(BF16)],
        positions.reshape(-1, 1))
    wt_sh = g_win.reshape(N_DEV * SHARD_PAD, D_MODEL)
    wa_pad = jnp.pad(jnp.concatenate([g_wa[j] for j in range(N_DEV)], axis=1), ((0, RANK_PAD - B_GATE_RANK), (0, 0)))

    r = _local_step(x, cosf, sinf, loss_target, g_in, wt_sh, wa_pad, b_alpha, attn_sinks[0], g_gla_norm,
                    [w_out_a[0].astype(BF16), w_out_b[0].astype(BF16), w_o[0].astype(BF16)],
                    g_final.reshape(1, D_MODEL), chip_order)

    first = jnp.zeros((1,), jnp.int32)
    big = [_adam_big("adam_w_in", r["own_in"][None], first, r["rv_in"], w_in[0].T, m_w_in[0].T, v_w_in[0].T),
           _adam_big("adam_w_out_a", r["own_oa"], dev_idx, r["rv_oa"], w_out_a[0], m_w_out_a[0], v_w_out_a[0]),
           _adam_big("adam_w_out_b", r["own_ob"], dev_idx, r["rv_ob"], w_out_b[0], m_w_out_b[0], v_w_out_b[0]),
           _adam_big("adam_w_o", r["own_o"], dev_idx, r["rv_o"], w_o[0], m_w_o[0], v_w_o[0])]
    row = lambda a: a.reshape(1, D_MODEL)
    big[0] = [a.T for a in big[0]]
    (s_in, s_final, s_gla, s_ba, s_sinks, s_wa), loss_row = _adam_small(r["small_all"], [
        (g_in, m_g_in, v_g_in), (row(g_final), row(m_g_final), row(v_g_final)),
        (g_gla_norm, m_g_gla_norm, v_g_gla_norm), (b_alpha, m_b_alpha, v_b_alpha),
        (attn_sinks, m_attn_sinks, v_attn_sinks), (w_alpha_up[0], m_w_alpha_up[0], v_w_alpha_up[0])])

    def group(i):
        return (s_in[i], big[0][i][None], s_wa[i][None], s_ba[i], s_sinks[i], s_gla[i], big[1][i][None], big[2][i][None],
                big[3][i][None], s_final[i].reshape(D_MODEL))

    return (loss_row[0, 0], r["grad_x"], *group(0), *group(1), *group(2), *group(3))
```

```python
import functools
import math

import numpy as np
import jax
import jax.numpy as jnp
from jax import lax
from jax.experimental import pallas as pl
from jax.experimental.pallas import tpu as pltpu

F32 = jnp.float32
BF16 = jnp.bfloat16
MESH = pl.DeviceIdType.MESH

D_MODEL = 1024
A_HEADS, A_KV_HEADS, A_HEAD_DIM = 8, 2, 64
A_WIDTH, A_KV_WIDTH = 512, 128
WINDOW = 128
ROPE_THETA = 500000.0
ROPE_DIM = 16
B_HEADS, B_KEY_DIM, B_VAL_DIM = 4, 64, 128
B_KEY_WIDTH, B_WIDTH = 256, 512
B_GATE_RANK = 16
B_GATE_TEMP = 16.0
B_CHUNK = 64
NORM_EPS = 1e-6
NEG_BIG = -1e30
D_IN = 4880
N_DEV = 8
N_CHIPS = 4
ADAM_LR, ADAM_B1, ADAM_B2, ADAM_EPS, ADAM_WD, ADAM_STEP = 0.001, 0.9, 0.999, 1e-08, 0.01, 10

LANES = 128
V7X_VMEM_LIMIT = 56 * 1024 * 1024

RANK_PAD = LANES
SEG = {}
_off = 0
for _name, _w in (("qa", 512), ("ka", 128), ("va", 128), ("za", 512), ("qb", 256), ("kb", 256),
                  ("vb", 512), ("zb", 512), ("alr", RANK_PAD), ("ga", 1024), ("gb", 1024)):
    SEG[_name] = (_off, _off + _w)
    _off += _w
D_IN_PAD = _off
ALR_SRC = SEG["alr"][0]
QKV_K, QKV_V, QKV_W = SEG["ka"][0], SEG["va"][0], SEG["va"][1]

SHARD_IN = D_IN // N_DEV
SHARD_PAD = 640
SHARD_OUT = D_MODEL // N_DEV
SHARD_ALPHA = B_KEY_WIDTH // N_DEV

SMALL_G_FINAL, SMALL_G_GLA, SMALL_LOSS, SMALL_SINKS, SMALL_G_IN, SMALL_B_ALPHA, SMALL_W_ALPHA = 0, 8, 12, 16, 24, 32, 40
SMALL_ROWS = 72


def _dot(a, b):
    return jnp.dot(a, b, preferred_element_type=F32)


def _dot_nt(a, b):
    return lax.dot_general(a, b, (((1,), (1,)), ((), ())), preferred_element_type=F32)


def _dot_tn(a, b):
    return lax.dot_general(a, b, (((0,), (0,)), ((), ())), preferred_element_type=F32)


def _sigmoid(z):
    return 1.0 / (1.0 + jnp.exp(-z))


def _params(*sem):
    return pltpu.CompilerParams(dimension_semantics=sem, vmem_limit_bytes=V7X_VMEM_LIMIT)


def _const_spec(shape):
    nd = len(shape)
    return pl.BlockSpec(shape, lambda *_: (0,) * nd, pipeline_mode=pl.Buffered(1))


def _lane_iota(shape):
    return lax.broadcasted_iota(jnp.int32, shape, 1)


def _row_iota(shape):
    return lax.broadcasted_iota(jnp.int32, shape, 0)


def _split3(v):
    hi = v.astype(BF16)
    r1 = v - hi.astype(F32)
    mid = r1.astype(BF16)
    lo = (r1 - mid.astype(F32)).astype(BF16)
    return hi, mid, lo


def _put_rows(ref, row0, vec):
    for r in range(vec.shape[1] // LANES):
        ref[row0 + r:row0 + r + 1, :] = vec[:, r * LANES:(r + 1) * LANES]


def _take_rows(slab, row0, n):
    return jnp.concatenate([slab[row0 + r:row0 + r + 1, :] for r in range(n)], axis=1)


def _rope_lane_constants():
    half = ROPE_DIM // 2
    inv_freq = np.exp(-math.log(ROPE_THETA) * np.arange(half, dtype=np.float32) * np.float32(2.0 / ROPE_DIM)).astype(np.float32)
    lane = np.arange(LANES)
    j = lane % A_HEAD_DIM
    invf = np.where(j < ROPE_DIM, inv_freq[j % half], 0.0).astype(np.float32)
    sign = np.where(j < half, -1.0, np.where(j < ROPE_DIM, 1.0, 0.0)).astype(np.float32)
    return jnp.asarray(invf)[None, :], jnp.asarray(sign)[None, :]


def _rope_slab(t, cos, sin_signed):
    first = (_lane_iota(t.shape) % A_HEAD_DIM) < (ROPE_DIM // 2)
    partner = jnp.where(first, pltpu.roll(t, LANES - ROPE_DIM // 2, 1), pltpu.roll(t, ROPE_DIM // 2, 1))
    return t * cos + partner * sin_signed


def _shard_pad_cols(j):
    cut = ALR_SRC + B_GATE_RANK
    shift = RANK_PAD - B_GATE_RANK
    a, b = j * SHARD_IN, (j + 1) * SHARD_IN
    if b <= cut:
        return [(a, b)]
    if a >= cut:
        return [(a + shift, b + shift)]
    return [(a, cut), (cut + shift, b + shift)]


def _in_proj(x2, cosf, sinf, g_in, wt_sh, wa_pad, b_alpha, later_shards):
    T = x2.shape[0]
    tm = 256
    last = T // tm - 1
    nl = len(later_shards)

    def body(x_ref, cos_ref, sin_ref, g_ref, wsh_ref, wa_ref, ba_ref, *rest):
        sh_refs, rest = rest[:nl], rest[nl:]
        (h_ref, qkv_ref, za_ref, q_ref, k_ref, vb_ref, zb_ref, alr_ref, u_ref, cum_ref, ga_ref, gb_ref, wt_ref) = rest[:13]
        all_refs, (send_sems, recv_sems, local_sems) = rest[13:13 + nl], rest[13 + nl:]
        px, py, pc = _my_place()
        my_dev = 4 * px + 2 * py + pc

        def wcopy(a, r, slot):
            dx, dy, dc = FLIPS[r]
            return pltpu.make_async_remote_copy(
                src_ref=sh_refs[a], dst_ref=all_refs[a].at[slot], send_sem=send_sems.at[a, r],
                recv_sem=recv_sems.at[a, r], device_id=(px ^ dx, py ^ dy, pc ^ dc), device_id_type=MESH)

        keep = [pltpu.make_async_copy(sh_refs[a], all_refs[a].at[my_dev], local_sems.at[a]) for a in range(nl)]

        @pl.when(pl.program_id(0) == 0)
        def _():
            for a in range(nl):
                keep[a].start()
                for r in range(len(FLIPS)):
                    wcopy(a, r, my_dev).start()

        @pl.when(pl.program_id(0) == 0)
        def _():
            for j in range(N_DEV):
                src = j * SHARD_PAD
                for a, b in _shard_pad_cols(j):
                    wt_ref[a:b, :] = wsh_ref[src:src + b - a, :]
                    src += b - a
            a, b = SEG["alr"]
            wt_ref[a + B_GATE_RANK:b, :] = jnp.zeros((RANK_PAD - B_GATE_RANK, D_MODEL), BF16)

        x = x_ref[...]
        r = lax.rsqrt(jnp.mean(x * x, axis=-1, keepdims=True) + NORM_EPS)
        h = (x * r * g_ref[...]).astype(BF16)
        h_ref[...] = h

        def seg(name):
            a, b = SEG[name]
            return _dot_nt(h, wt_ref[a:b, :])

        cos, sin = cos_ref[...], sin_ref[...]
        qa = seg("qa")
        for s in range(A_WIDTH // LANES):
            qkv_ref[:, s * LANES:(s + 1) * LANES] = _rope_slab(qa[:, s * LANES:(s + 1) * LANES], cos, sin).astype(BF16)
        qkv_ref[:, QKV_K:QKV_V] = _rope_slab(seg("ka"), cos, sin).astype(BF16)
        qkv_ref[:, QKV_V:QKV_W] = seg("va").astype(BF16)
        za_ref[...] = seg("za")
        q_ref[...] = seg("qb")
        k_ref[...] = seg("kb")
        vb_ref[...] = seg("vb").astype(BF16)
        zb_ref[...] = seg("zb")
        ga_ref[...] = seg("ga")
        gb_ref[...] = seg("gb")
        alr = seg("alr").astype(BF16)
        alr_ref[...] = alr
        u = _dot(alr, wa_ref[...]) + ba_ref[...]
        u_ref[...] = u
        log_a = (jnp.minimum(u, 0.0) - jnp.log(1.0 + jnp.exp(-jnp.abs(u)))) * (1.0 / B_GATE_TEMP)
        row, col = _row_iota((tm, tm)), _lane_iota((tm, tm))
        tri = ((row // B_CHUNK == col // B_CHUNK) & (col <= row)).astype(BF16)
        hi, mid, lo = _split3(log_a)
        cum_ref[...] = _dot(tri, hi) + _dot(tri, mid) + _dot(tri, lo)

        @pl.when(pl.program_id(0) == last)
        def _():
            for a in range(nl):
                for r, (dx, dy, dc) in enumerate(FLIPS):
                    wcopy(a, r, 4 * (px ^ dx) + 2 * (py ^ dy) + (pc ^ dc)).wait_recv()
                    wcopy(a, r, my_dev).wait_send()
                keep[a].wait()

    def rows(w):
        return pl.BlockSpec((tm, w), lambda i: (i, 0))

    outs = [("h", D_MODEL, BF16), ("qkv", QKV_W, BF16), ("za", A_WIDTH, F32), ("q", B_KEY_WIDTH, F32),
            ("k", B_KEY_WIDTH, F32), ("vb", B_WIDTH, BF16), ("zb", B_WIDTH, F32), ("alr", RANK_PAD, BF16),
            ("u", B_KEY_WIDTH, F32), ("cum", B_KEY_WIDTH, F32), ("ga", D_MODEL, F32), ("gb", D_MODEL, F32)]
    res = pl.pallas_call(
        body, name="in_proj", grid=(T // tm,),
        in_specs=[rows(D_MODEL), rows(LANES), rows(LANES), _const_spec((1, D_MODEL)),
                  _const_spec((N_DEV * SHARD_PAD, D_MODEL)), _const_spec((RANK_PAD, B_KEY_WIDTH)),
                  _const_spec((1, B_KEY_WIDTH))] + _any_specs(nl),
        out_specs=[rows(w) for _, w, _ in outs] + [pl.BlockSpec((D_IN_PAD, D_MODEL), lambda i: (0, 0))] + _any_specs(nl),
        out_shape=[jax.ShapeDtypeStruct((T, w), dt) for _, w, dt in outs]
                  + [jax.ShapeDtypeStruct((D_IN_PAD, D_MODEL), BF16)]
                  + [jax.ShapeDtypeStruct((N_DEV, *sh.shape), sh.dtype) for sh in later_shards],
        scratch_shapes=[pltpu.SemaphoreType.DMA((nl, len(FLIPS))), pltpu.SemaphoreType.DMA((nl, len(FLIPS))),
                        pltpu.SemaphoreType.DMA((nl,))],
        compiler_params=_params("arbitrary"),
    )(x2, cosf, sinf, g_in, wt_sh, wa_pad, b_alpha, *later_shards)
    n_out = len(outs) + 1
    return dict(zip([n for n, _, _ in outs] + ["wt_pad"], res[:n_out])), res[n_out:]


def _dup_kv_head(t, g):
    tf = t.astype(F32)
    keep = (_lane_iota(tf.shape) < A_HEAD_DIM) == (g == 0)
    return jnp.where(keep, tf, pltpu.roll(tf, A_HEAD_DIM, 1)).astype(BF16)


def _stack_heads(t):
    lo = _lane_iota(t.shape) < A_HEAD_DIM
    zero = jnp.zeros_like(t)
    return jnp.concatenate([jnp.where(lo, t, zero), jnp.where(lo, zero, t)], axis=0)


ATT_ROWS = A_HEADS * WINDOW
ATT_CHUNK = 64
GROUP_ROWS = ATT_ROWS // A_KV_HEADS


def _chunk_masks(n):
    masks = []
    for half in range(WINDOW // ATT_CHUNK):
        qi = _row_iota((ATT_CHUNK, 2 * WINDOW)) + half * ATT_CHUNK
        kj = _lane_iota((ATT_CHUNK, 2 * WINDOW)) - WINDOW
        masks.append((kj <= qi) & (qi - kj < WINDOW) & ((n > 0) | (kj >= 0)))
    return masks


def _stacked_queries(ref):
    return jnp.concatenate([_stack_heads(ref[:, p * LANES:(p + 1) * LANES]) for p in range(A_HEADS // 2)], axis=0)


def _by_group(fn, lhs, rhs_per_group):
    return jnp.concatenate([fn(lhs[g * GROUP_ROWS:(g + 1) * GROUP_ROWS], rhs_per_group[g])
                            for g in range(A_KV_HEADS)], axis=0)


def _attn_fwd(qkv, sinks, B, S):
    T = B * S
    nb = S // WINDOW
    scale = A_HEAD_DIM ** -0.5

    def body(sink_ref, q_ref, kc_ref, vc_ref, kp_ref, vp_ref, o_ref, lse_ref):
        valid = jnp.concatenate(_chunk_masks(pl.program_id(1)), axis=0)
        valid = jnp.concatenate([valid, valid], axis=0)
        k = jnp.concatenate([kp_ref[...], kc_ref[...]], axis=0)
        v = jnp.concatenate([vp_ref[...], vc_ref[...]], axis=0)
        top = _row_iota((2 * WINDOW, 1)) < WINDOW
        lo = _lane_iota((WINDOW, LANES)) < A_HEAD_DIM
        lane = _lane_iota((WINDOW, LANES))
        lse_tile = jnp.zeros((WINDOW, LANES), F32)
        for g in range(A_KV_HEADS):
            kd, vd = _dup_kv_head(k, g), _dup_kv_head(v, g)
            for p in (2 * g, 2 * g + 1):
                qs = _stack_heads(q_ref[:, p * LANES:(p + 1) * LANES])
                s = jnp.where(valid, _dot_nt(qs, kd) * scale, NEG_BIG)
                sink = jnp.where(top, sink_ref[2 * p], sink_ref[2 * p + 1])
                m = jnp.maximum(jnp.max(s, axis=-1, keepdims=True), sink)
                e = jnp.exp(s - m)
                den = jnp.sum(e, axis=-1, keepdims=True) + jnp.exp(sink - m)
                o = _dot((e * (1.0 / den)).astype(BF16), vd)
                o_ref[:, p * LANES:(p + 1) * LANES] = jnp.where(lo, o[:WINDOW], o[WINDOW:])
                lse = m + jnp.log(den)
                lse_tile = jnp.where(lane == 2 * p, lse[:WINDOW], lse_tile)
                lse_tile = jnp.where(lane == 2 * p + 1, lse[WINDOW:], lse_tile)
        lse_ref[...] = lse_tile

    def cur(col, w):
        return pl.BlockSpec((WINDOW, w), lambda b, n: (b * nb + n, col))

    def prev(col):
        return pl.BlockSpec((WINDOW, LANES), lambda b, n: (b * nb + jnp.maximum(n - 1, 0), col))

    kcol, vcol = QKV_K // LANES, QKV_V // LANES
    return pl.pallas_call(
        body, name="attn_fwd", grid=(B, nb),
        in_specs=[pl.BlockSpec(memory_space=pltpu.SMEM), cur(0, A_WIDTH), cur(kcol, LANES), cur(vcol, LANES),
                  prev(kcol), prev(vcol)],
        out_specs=[cur(0, A_WIDTH), cur(0, LANES)],
        out_shape=[jax.ShapeDtypeStruct((T, A_WIDTH), F32), jax.ShapeDtypeStruct((T, LANES), F32)],
        compiler_params=_params("parallel", "parallel"),
    )(sinks, qkv, qkv, qkv, qkv, qkv)


def _attn_bwd(qkv, do, out, lse, sink_col, B, S):
    T = B * S
    nb = S // WINDOW
    scale = A_HEAD_DIM ** -0.5

    def body(sink_ref, q_ref, kc_ref, vc_ref, kp_ref, vp_ref, do_ref, out_ref, lse_ref,
             dq_ref, dkv_ref, dsink_ref, carry_ref, s_ref, dp_ref, ds_ref, p_ref):
        b, n = pl.program_id(0), pl.program_id(1)
        active = n < nb
        masks = _chunk_masks(jnp.minimum(n, nb - 1))

        @pl.when((b == 0) & (n == 0))
        def _():
            dsink_ref[...] = jnp.zeros_like(dsink_ref)

        k = jnp.concatenate([kp_ref[...], kc_ref[...]], axis=0)
        v = jnp.concatenate([vp_ref[...], vc_ref[...]], axis=0)
        kd = [_dup_kv_head(k, g) for g in range(A_KV_HEADS)]
        vd = [_dup_kv_head(v, g) for g in range(A_KV_HEADS)]
        qs = _stacked_queries(q_ref)
        dos = _stacked_queries(do_ref)
        s_ref[...] = _by_group(_dot_nt, qs, kd)
        dp_ref[...] = _by_group(_dot_nt, dos, vd)
        lane = _lane_iota((ATT_CHUNK, LANES))
        lo = lane < A_HEAD_DIM
        lane1 = _lane_iota((1, LANES))
        dsink_row = jnp.zeros((1, LANES), F32)
        for c in range(ATT_ROWS // ATT_CHUNK):
            rows = slice(c * ATT_CHUNK, (c + 1) * ATT_CHUNK)
            head, half = divmod(c, WINDOW // ATT_CHUNK)
            qrows = slice(half * ATT_CHUNK, (half + 1) * ATT_CHUNK)
            slab = slice((head // 2) * LANES, (head // 2 + 1) * LANES)
            lse_col = jnp.sum(jnp.where(lane == head, lse_ref[qrows, :], 0.0), axis=-1, keepdims=True)
            prod = do_ref[qrows, slab].astype(F32) * out_ref[qrows, slab]
            mine = lo if head % 2 == 0 else jnp.logical_not(lo)
            delta = jnp.sum(jnp.where(mine, prod, 0.0), axis=-1, keepdims=True)
            s = jnp.where(masks[half], s_ref[rows, :] * scale, NEG_BIG)
            prob = jnp.exp(s - lse_col)
            p_ref[rows, :] = prob.astype(BF16)
            ds_ref[rows, :] = (prob * (dp_ref[rows, :] - delta) * scale).astype(BF16)
            w = -jnp.exp(sink_ref[rows, :] - lse_col) * delta
            dsink_row += jnp.where(lane1 == head, jnp.sum(w, axis=0, keepdims=True), 0.0)
        dq = _by_group(_dot, ds_ref[...], kd)
        lo_q = _lane_iota((WINDOW, LANES)) < A_HEAD_DIM
        for p in range(A_HEADS // 2):
            dq_ref[:, p * LANES:(p + 1) * LANES] = jnp.where(
                lo_q, dq[2 * p * WINDOW:(2 * p + 1) * WINDOW], dq[(2 * p + 1) * WINDOW:(2 * p + 2) * WINDOW]).astype(BF16)
        lane2 = _lane_iota((2 * WINDOW, LANES))
        dk_tot = jnp.zeros((2 * WINDOW, LANES), F32)
        dv_tot = jnp.zeros((2 * WINDOW, LANES), F32)
        for g in range(A_KV_HEADS):
            grows = slice(g * GROUP_ROWS, (g + 1) * GROUP_ROWS)
            dk_acc = _dot_tn(ds_ref[grows, :], qs[grows])
            dv_acc = _dot_tn(p_ref[grows, :], dos[grows])
            mine = (lane2 < A_HEAD_DIM) == (g == 0)
            dk_tot = jnp.where(mine, dk_acc + pltpu.roll(dk_acc, A_HEAD_DIM, 1), dk_tot)
            dv_tot = jnp.where(mine, dv_acc + pltpu.roll(dv_acc, A_HEAD_DIM, 1), dv_tot)
        gate = jnp.where(active, 1.0, 0.0)
        dsink_ref[0:1, :] += dsink_row * gate
        dkv_ref[:, 0:LANES] = (carry_ref[:, 0:LANES] + dk_tot[:WINDOW] * gate).astype(BF16)
        dkv_ref[:, LANES:] = (carry_ref[:, LANES:] + dv_tot[:WINDOW] * gate).astype(BF16)
        carry_ref[:, 0:LANES] = dk_tot[WINDOW:]
        carry_ref[:, LANES:] = dv_tot[WINDOW:]

    def cur(col, w):
        return pl.BlockSpec((WINDOW, w), lambda b, n: (b * nb + jnp.minimum(n, nb - 1), col))

    def prev(col):
        return pl.BlockSpec((WINDOW, LANES), lambda b, n: (b * nb + jnp.maximum(jnp.minimum(n, nb - 1) - 1, 0), col))

    lag = pl.BlockSpec((WINDOW, 2 * LANES), lambda b, n: (b * nb + jnp.maximum(n - 1, 0), 0))
    kcol, vcol = QKV_K // LANES, QKV_V // LANES
    scores = (ATT_ROWS, 2 * WINDOW)
    return pl.pallas_call(
        body, name="attn_bwd", grid=(B, nb + 1),
        in_specs=[_const_spec((ATT_ROWS, 1)), cur(0, A_WIDTH), cur(kcol, LANES), cur(vcol, LANES),
                  prev(kcol), prev(vcol), cur(0, A_WIDTH), cur(0, A_WIDTH), cur(0, LANES)],
        out_specs=[cur(0, A_WIDTH), lag, pl.BlockSpec((8, LANES), lambda b, n: (0, 0))],
        out_shape=[jax.ShapeDtypeStruct((T, A_WIDTH), BF16), jax.ShapeDtypeStruct((T, 2 * LANES), BF16),
                   jax.ShapeDtypeStruct((8, LANES), F32)],
        scratch_shapes=[pltpu.VMEM((WINDOW, 2 * LANES), F32), pltpu.VMEM(scores, F32), pltpu.VMEM(scores, F32),
                        pltpu.VMEM(scores, BF16), pltpu.VMEM(scores, BF16)],
        compiler_params=_params("arbitrary", "arbitrary"),
    )(sink_col, qkv, qkv, qkv, qkv, qkv, do, out, lse)


GLA_TILE = 256
CHUNKS_PER_TILE = GLA_TILE // B_CHUNK


def _gla_factors(q_ref, k_ref, cum_ref):
    scale = B_KEY_DIM ** -0.5
    cum = cum_ref[...]
    shape = (B_CHUNK, B_KEY_WIDTH)
    last = jnp.concatenate([jnp.broadcast_to(cum_ref[pl.ds(c * B_CHUNK + B_CHUNK - 1, 1), :], shape)
                            for c in range(CHUNKS_PER_TILE)], axis=0)
    mid = jnp.concatenate([jnp.broadcast_to(cum_ref[pl.ds(c * B_CHUNK + B_CHUNK // 2 - 1, 1), :], shape)
                           for c in range(CHUNKS_PER_TILE)], axis=0)
    e_qm, e_km, e_qe, e_kd = jnp.exp(cum - mid), jnp.exp(mid - cum), jnp.exp(cum), jnp.exp(last - cum)
    qs = q_ref[...] * scale
    k = k_ref[...]
    return qs, k, (e_qm, e_km, e_qe, e_kd)


def _head_mask(shape, h):
    return (_lane_iota(shape) // B_KEY_DIM) == h


def _stack_masked(t):
    return jnp.concatenate([jnp.where(_head_mask(t.shape, h), t, 0.0) for h in range(B_HEADS)], axis=0).astype(BF16)


def _select_heads(t):
    shape = (B_CHUNK, B_KEY_WIDTH)
    out = jnp.zeros(shape, F32)
    for h in range(B_HEADS):
        out = jnp.where(_head_mask(shape, h), t[h * B_CHUNK:(h + 1) * B_CHUNK], out)
    return out


def _select_state(t):
    shape = (B_VAL_DIM, B_KEY_WIDTH)
    out = jnp.zeros(shape, F32)
    for h in range(B_HEADS):
        out = jnp.where(_head_mask(shape, h), t[h * B_VAL_DIM:(h + 1) * B_VAL_DIM], out)
    return out


def _rows_by_head(t):
    return jnp.concatenate([t[:, h * B_VAL_DIM:(h + 1) * B_VAL_DIM] for h in range(B_HEADS)], axis=0)


def _intra_mask():
    i, j = _row_iota((GLA_TILE, GLA_TILE)), _lane_iota((GLA_TILE, GLA_TILE))
    return (i // B_CHUNK == j // B_CHUNK) & (j <= i)


def _pair_stack(t, p):
    slab = t[:, p * LANES:(p + 1) * LANES]
    lo = _lane_iota(slab.shape) < B_KEY_DIM
    return jnp.concatenate([jnp.where(lo, slab, 0.0), jnp.where(lo, 0.0, slab)], axis=0).astype(BF16)


def _gla_fwd(q, k, cum, vb, B, S):
    T = B * S
    nt = S // GLA_TILE

    def body(q_ref, k_ref, cum_ref, v_ref, o_ref, st_all_ref, st_ref):
        @pl.when(pl.program_id(1) == 0)
        def _():
            st_ref[...] = jnp.zeros_like(st_ref)

        qs, kk, (e_qm, e_km, e_qe, e_kd) = _gla_factors(q_ref, k_ref, cum_ref)
        qm, km, qe, kd = qs * e_qm, kk * e_km, qs * e_qe, (kk * e_kd).astype(BF16)
        mask = _intra_mask()
        intra = []
        for p in range(B_HEADS // 2):
            a = _dot_nt(_pair_stack(qm, p), km[:, p * LANES:(p + 1) * LANES].astype(BF16))
            for hh in range(2):
                h = 2 * p + hh
                att = jnp.where(mask, a[hh * GLA_TILE:(hh + 1) * GLA_TILE], 0.0).astype(BF16)
                intra.append(_dot(att, v_ref[:, h * B_VAL_DIM:(h + 1) * B_VAL_DIM]))
        inter = []
        for c in range(CHUNKS_PER_TILE):
            rows = slice(c * B_CHUNK, (c + 1) * B_CHUNK)
            st = st_ref[...]
            st_all_ref[c] = st
            inter.append(_dot_nt(_stack_masked(qe[rows]), st.astype(BF16)))
            inc = _select_state(_dot_tn(v_ref[rows, :], kd[rows]))
            decay = jnp.exp(cum_ref[pl.ds(c * B_CHUNK + B_CHUNK - 1, 1), :])
            st_ref[...] = st * decay + inc
        for h in range(B_HEADS):
            oi = jnp.concatenate([inter[c][h * B_CHUNK:(h + 1) * B_CHUNK] for c in range(CHUNKS_PER_TILE)], axis=0)
            o_ref[:, h * B_VAL_DIM:(h + 1) * B_VAL_DIM] = intra[h] + oi

    def rows(w):
        return pl.BlockSpec((GLA_TILE, w), lambda b, t: (b * nt + t, 0))

    return pl.pallas_call(
        body, name="gla_fwd", grid=(B, nt),
        in_specs=[rows(B_KEY_WIDTH), rows(B_KEY_WIDTH), rows(B_KEY_WIDTH), rows(B_WIDTH)],
        out_specs=[rows(B_WIDTH),
                   pl.BlockSpec((CHUNKS_PER_TILE, B_VAL_DIM, B_KEY_WIDTH), lambda b, t: (b * nt + t, 0, 0))],
        out_shape=[jax.ShapeDtypeStruct((T, B_WIDTH), F32),
                   jax.ShapeDtypeStruct((T // B_CHUNK, B_VAL_DIM, B_KEY_WIDTH), F32)],
        scratch_shapes=[pltpu.VMEM((B_VAL_DIM, B_KEY_WIDTH), F32)],
        compiler_params=_params("arbitrary", "arbitrary"),
    )(q, k, cum, vb)


def _gla_bwd(q, k, cum, vb, do, st_all, B, S, wgrads):
    T = B * S
    nt = S // GLA_TILE
    scale = B_KEY_DIM ** -0.5
    nw = len(wgrads)

    def body(q_ref, k_ref, cum_ref, v_ref, do_ref, st_all_ref, *rest):
        g_refs, (dq_ref, dk_ref, dv_ref, dla_ref) = rest[:nw], rest[nw:nw + 4]
        rv_refs, (dst_ref, send_sems, recv_sems) = rest[nw + 4:2 * nw + 4], rest[2 * nw + 4:]
        x, y, c = _my_place()

        def wcopy(a, r):
            dx, dy, dc = FLIPS[r]
            return pltpu.make_async_remote_copy(
                src_ref=g_refs[a].at[4 * (x ^ dx) + 2 * (y ^ dy) + (c ^ dc)], dst_ref=rv_refs[a].at[r],
                send_sem=send_sems.at[a, r], recv_sem=recv_sems.at[a, r],
                device_id=(x ^ dx, y ^ dy, c ^ dc), device_id_type=MESH)

        @pl.when((pl.program_id(0) == 0) & (pl.program_id(1) == 0))
        def _():
            for a in range(nw):
                for r in range(len(FLIPS)):
                    wcopy(a, r).start()

        @pl.when(pl.program_id(1) == 0)
        def _():
            dst_ref[...] = jnp.zeros_like(dst_ref)

        qs, kk, (e_qm, e_km, e_qe, e_kd) = _gla_factors(q_ref, k_ref, cum_ref)
        qm, km, qe, kd = qs * e_qm, kk * e_km, qs * e_qe, kk * e_kd
        mask = _intra_mask()
        dqm_slabs, dkm_slabs, dv_intra = [], [], []
        for p in range(B_HEADS // 2):
            qm_st = _pair_stack(qm, p)
            km_p = km[:, p * LANES:(p + 1) * LANES].astype(BF16)
            a = _dot_nt(qm_st, km_p)
            da_blocks, dqm_h = [], []
            for hh in range(2):
                h = 2 * p + hh
                vs = slice(h * B_VAL_DIM, (h + 1) * B_VAL_DIM)
                att = jnp.where(mask, a[hh * GLA_TILE:(hh + 1) * GLA_TILE], 0.0).astype(BF16)
                dv_intra.append(_dot_tn(att, do_ref[:, vs]))
                da = jnp.where(mask, _dot_nt(do_ref[:, vs], v_ref[:, vs]), 0.0).astype(BF16)
                da_blocks.append(da)
                dqm_h.append(_dot(da, km_p))
            lo = _lane_iota((GLA_TILE, LANES)) < B_KEY_DIM
            dqm_slabs.append(jnp.where(lo, dqm_h[0], dqm_h[1]))
            dkm_slabs.append(_dot_tn(jnp.concatenate(da_blocks, axis=0), qm_st))
        dqm = jnp.concatenate(dqm_slabs, axis=1)
        dkm = jnp.concatenate(dkm_slabs, axis=1)

        dqe_c, dkd_c, dv_inter, tail_c = ([None] * CHUNKS_PER_TILE for _ in range(4))
        for c in reversed(range(CHUNKS_PER_TILE)):
            rows = slice(c * B_CHUNK, (c + 1) * B_CHUNK)
            dst = dst_ref[...]
            dst_b = dst.astype(BF16)
            dv_inter[c] = _dot_nt(_stack_masked(kd[rows]), dst_b)
            dkd_c[c] = _select_heads(_dot(_rows_by_head(v_ref[rows, :]), dst_b))
            do_c = do_ref[rows, :]
            dqe_c[c] = _select_heads(_dot(_rows_by_head(do_c), st_all_ref[c].astype(BF16)))
            contrib = _select_state(_dot_tn(do_c, qe[rows].astype(BF16)))
            decay = jnp.exp(cum_ref[pl.ds(c * B_CHUNK + B_CHUNK - 1, 1), :])
            tail = (jnp.sum(kk[rows] * dkd_c[c] * e_kd[rows], axis=0, keepdims=True)
                    + decay * jnp.sum(st_all_ref[c] * dst, axis=0, keepdims=True))
            tail_c[c] = jnp.broadcast_to(tail, (B_CHUNK, B_KEY_WIDTH))
            dst_ref[...] = dst * decay + contrib
        dqe = jnp.concatenate(dqe_c, axis=0)
        dkd = jnp.concatenate(dkd_c, axis=0)
        dqs = dqm * e_qm + dqe * e_qe
        dk = dkm * e_km + dkd * e_kd
        dq_ref[...] = (dqs * scale).astype(BF16)
        dk_ref[...] = dk.astype(BF16)
        for h in range(B_HEADS):
            dvi = jnp.concatenate([dv_inter[c][h * B_CHUNK:(h + 1) * B_CHUNK] for c in range(CHUNKS_PER_TILE)], axis=0)
            dv_ref[:, h * B_VAL_DIM:(h + 1) * B_VAL_DIM] = (dv_intra[h] + dvi).astype(BF16)
        dd = qs * dqs - kk * dk
        i, j = _row_iota((GLA_TILE, GLA_TILE)), _lane_iota((GLA_TILE, GLA_TILE))
        upper = ((i // B_CHUNK == j // B_CHUNK) & (j >= i)).astype(BF16)
        hi, mid, lo3 = _split3(dd)
        dla_ref[...] = _dot(upper, hi) + _dot(upper, mid) + _dot(upper, lo3) + jnp.concatenate(tail_c, axis=0)

        @pl.when((pl.program_id(0) == B - 1) & (pl.program_id(1) == nt - 1))
        def _():
            for a in range(nw):
                for r in range(len(FLIPS)):
                    wcopy(a, r).wait()

    def rows(w):
        return pl.BlockSpec((GLA_TILE, w), lambda b, t: (b * nt + nt - 1 - t, 0))

    res = pl.pallas_call(
        body, name="gla_bwd", grid=(B, nt),
        in_specs=[rows(B_KEY_WIDTH), rows(B_KEY_WIDTH), rows(B_KEY_WIDTH), rows(B_WIDTH), rows(B_WIDTH),
                  pl.BlockSpec((CHUNKS_PER_TILE, B_VAL_DIM, B_KEY_WIDTH), lambda b, t: (b * nt + nt - 1 - t, 0, 0))]
                 + _any_specs(nw),
        out_specs=[rows(B_KEY_WIDTH), rows(B_KEY_WIDTH), rows(B_WIDTH), rows(B_KEY_WIDTH)] + _any_specs(nw),
        out_shape=[jax.ShapeDtypeStruct((T, B_KEY_WIDTH), BF16), jax.ShapeDtypeStruct((T, B_KEY_WIDTH), BF16),
                   jax.ShapeDtypeStruct((T, B_WIDTH), BF16), jax.ShapeDtypeStruct((T, B_KEY_WIDTH), F32)]
                  + [jax.ShapeDtypeStruct((len(FLIPS), *g.shape[1:]), g.dtype) for g in wgrads],
        scratch_shapes=[pltpu.VMEM((B_VAL_DIM, B_KEY_WIDTH), F32),
                        pltpu.SemaphoreType.DMA((nw, len(FLIPS))), pltpu.SemaphoreType.DMA((nw, len(FLIPS)))],
        compiler_params=_params("arbitrary", "arbitrary"),
    )(q, k, cum, vb, do, st_all, *wgrads)
    return res[:4], res[4:]


def _merge(x2, tgt2, attn, za, o_gla, zb, ga, gb, w_oa, w_ob, w_o, g_gla, g_final):
    T = x2.shape[0]
    tm = 256
    last = T // tm - 1

    def body(x_ref, tgt_ref, attn_ref, za_ref, og_ref, zb_ref, ga_ref, gb_ref,
             woa_ref, wob_ref, wo_ref, gg_ref, gf_ref,
             dxres_ref, dattn_ref, dog_ref, dza_ref, dzb_ref, dga_ref, dgb_ref,
             dwo_ref, dwoa_ref, dwob_ref, small_ref,
             awo_ref, awoa_ref, awob_ref, agf_ref, agg_ref, loss_ref):
        @pl.when(pl.program_id(0) == 0)
        def _():
            for r in (awo_ref, awoa_ref, awob_ref, agf_ref, agg_ref, loss_ref):
                r[...] = jnp.zeros_like(r)

        za_v = za_ref[...]
        sig_za = _sigmoid(za_v)
        silu_a = za_v * sig_za
        attn_v = attn_ref[...]
        oa = (attn_v * silu_a).astype(BF16)
        ya = _dot(oa, woa_ref[...])
        og = og_ref[...]
        zb_v = zb_ref[...]
        sig_zb = _sigmoid(zb_v)
        silu_b = zb_v * sig_zb
        gg = gg_ref[...]
        on_parts, rinv_parts = [], []
        for h in range(B_HEADS):
            seg = og[:, h * B_VAL_DIM:(h + 1) * B_VAL_DIM]
            rinv = lax.rsqrt(jnp.mean(seg * seg, axis=-1, keepdims=True) + NORM_EPS)
            rinv_parts.append(rinv)
            on_parts.append(seg * rinv)
        on = jnp.concatenate(on_parts, axis=1)
        obn = on * gg
        ob = (obn * silu_b).astype(BF16)
        yb = _dot(ob, wob_ref[...])
        sig_a, sig_b = _sigmoid(ga_ref[...]), _sigmoid(gb_ref[...])
        merged = (sig_a * ya + sig_b * yb).astype(BF16)
        out = x_ref[...] + _dot(merged, wo_ref[...])
        rf = lax.rsqrt(jnp.mean(out * out, axis=-1, keepdims=True) + NORM_EPS)
        nrm = out * rf
        gf = gf_ref[...]
        err = nrm * gf - tgt_ref[...]
        loss_ref[...] += jnp.sum(err * err) * (0.5 / D_MODEL)

        dy = err * (1.0 / D_MODEL)
        agf_ref[...] += jnp.sum(dy * nrm, axis=0, keepdims=True)
        dn = dy * gf
        dout = rf * (dn - nrm * jnp.mean(dn * nrm, axis=-1, keepdims=True))
        dxres_ref[...] = dout
        dout_b = dout.astype(BF16)
        dmerged = _dot_nt(dout_b, wo_ref[...])
        awo_ref[...] += _dot_tn(merged, dout_b)
        dya = dmerged * sig_a
        dyb = dmerged * sig_b
        dga_ref[...] = (dmerged * ya * sig_a * (1.0 - sig_a)).astype(BF16)
        dgb_ref[...] = (dmerged * yb * sig_b * (1.0 - sig_b)).astype(BF16)
        dya_b, dyb_b = dya.astype(BF16), dyb.astype(BF16)
        awoa_ref[...] += _dot_tn(oa, dya_b)
        awob_ref[...] += _dot_tn(ob, dyb_b)
        doa = _dot_nt(dya_b, woa_ref[...])
        dattn_ref[...] = (doa * silu_a).astype(BF16)
        dza_ref[...] = (doa * attn_v * (sig_za * (1.0 + za_v * (1.0 - sig_za)))).astype(BF16)
        dob = _dot_nt(dyb_b, wob_ref[...])
        dzb_ref[...] = (dob * obn * (sig_zb * (1.0 + zb_v * (1.0 - sig_zb)))).astype(BF16)
        dobn = dob * silu_b
        agg_ref[...] += jnp.sum(dobn * on, axis=0, keepdims=True)
        don = dobn * gg
        for h in range(B_HEADS):
            sl = slice(h * B_VAL_DIM, (h + 1) * B_VAL_DIM)
            don_h, on_h = don[:, sl], on[:, sl]
            dog_ref[:, sl] = (rinv_parts[h] * (don_h - on_h * jnp.mean(don_h * on_h, axis=-1, keepdims=True))).astype(BF16)

        @pl.when(pl.program_id(0) == last)
        def _():
            for j in range(N_DEV):
                dwo_ref[j] = awo_ref[j * SHARD_OUT:(j + 1) * SHARD_OUT, :].astype(BF16)
                dwoa_ref[j] = awoa_ref[:, j * SHARD_OUT:(j + 1) * SHARD_OUT].astype(BF16)
                dwob_ref[j] = awob_ref[:, j * SHARD_OUT:(j + 1) * SHARD_OUT].astype(BF16)
            small_ref[...] = jnp.zeros_like(small_ref)
            _put_rows(small_ref, SMALL_G_FINAL, agf_ref[...])
            _put_rows(small_ref, SMALL_G_GLA, agg_ref[...])
            small_ref[SMALL_LOSS:SMALL_LOSS + 1, :] = loss_ref[...]

    def rows(w):
        return pl.BlockSpec((tm, w), lambda i: (i, 0))

    def whole(shape):
        nd = len(shape)
        return pl.BlockSpec(shape, lambda i: (0,) * nd)

    outs = [((T, D_MODEL), F32, rows(D_MODEL)), ((T, A_WIDTH), BF16, rows(A_WIDTH)), ((T, B_WIDTH), BF16, rows(B_WIDTH)),
            ((T, A_WIDTH), BF16, rows(A_WIDTH)), ((T, B_WIDTH), BF16, rows(B_WIDTH)),
            ((T, D_MODEL), BF16, rows(D_MODEL)), ((T, D_MODEL), BF16, rows(D_MODEL)),
            ((N_DEV, SHARD_OUT, D_MODEL), BF16, whole((N_DEV, SHARD_OUT, D_MODEL))),
            ((N_DEV, A_WIDTH, SHARD_OUT), BF16, whole((N_DEV, A_WIDTH, SHARD_OUT))),
            ((N_DEV, B_WIDTH, SHARD_OUT), BF16, whole((N_DEV, B_WIDTH, SHARD_OUT))),
            ((SMALL_SINKS, LANES), F32, whole((SMALL_SINKS, LANES)))]
    return pl.pallas_call(
        body, name="merge", grid=(T // tm,),
        in_specs=[rows(D_MODEL), rows(D_MODEL), rows(A_WIDTH), rows(A_WIDTH), rows(B_WIDTH), rows(B_WIDTH),
                  rows(D_MODEL), rows(D_MODEL),
                  _const_spec((A_WIDTH, D_MODEL)), _const_spec((B_WIDTH, D_MODEL)), _const_spec((D_MODEL, D_MODEL)),
                  _const_spec((1, B_WIDTH)), _const_spec((1, D_MODEL))],
        out_specs=[o[2] for o in outs],
        out_shape=[jax.ShapeDtypeStruct(o[0], o[1]) for o in outs],
        scratch_shapes=[pltpu.VMEM((D_MODEL, D_MODEL), F32), pltpu.VMEM((A_WIDTH, D_MODEL), F32),
                        pltpu.VMEM((B_WIDTH, D_MODEL), F32), pltpu.VMEM((1, D_MODEL), F32), pltpu.VMEM((1, B_WIDTH), F32),
                        pltpu.VMEM((1, LANES), F32)],
        compiler_params=_params("arbitrary"),
    )(x2, tgt2, attn, za, o_gla, zb, ga, gb, w_oa, w_ob, w_o, g_gla, g_final)


def _in_proj_bwd(x2, dxres, cosf, sinf, g_in, wt_pad, wa_pad, parts):
    T = x2.shape[0]
    tm = 256
    last = T // tm - 1
    base = SMALL_G_IN

    def body(x_ref, dxres_ref, cos_ref, sin_ref, g_ref, wt_ref, wa_ref,
             dq_ref, dkv_ref, dza_ref, dqb_ref, dkb_ref, dvb_ref, dzb_ref, dla_ref, u_ref, alr_ref, dga_ref, dgb_ref,
             dx_ref, dsh_ref, small_ref, dproj_ref, agin_ref, aba_ref, awa_ref):
        @pl.when(pl.program_id(0) == 0)
        def _():
            for r in (agin_ref, aba_ref, awa_ref):
                r[...] = jnp.zeros_like(r)

        cos, nsin = cos_ref[...], -sin_ref[...]
        for s in range(A_WIDTH // LANES):
            sl = slice(s * LANES, (s + 1) * LANES)
            dproj_ref[:, sl] = _rope_slab(dq_ref[:, sl].astype(F32), cos, nsin).astype(BF16)
        dproj_ref[:, QKV_K:QKV_V] = _rope_slab(dkv_ref[:, 0:LANES].astype(F32), cos, nsin).astype(BF16)
        dproj_ref[:, QKV_V:QKV_W] = dkv_ref[:, LANES:]

        def put(name, val):
            a, b = SEG[name]
            dproj_ref[:, a:b] = val

        put("za", dza_ref[...])
        put("qb", dqb_ref[...])
        put("kb", dkb_ref[...])
        put("vb", dvb_ref[...])
        put("zb", dzb_ref[...])
        put("ga", dga_ref[...])
        put("gb", dgb_ref[...])
        du = dla_ref[...] * (1.0 / B_GATE_TEMP) * _sigmoid(-u_ref[...])
        aba_ref[...] += jnp.sum(du, axis=0, keepdims=True)
        du_b = du.astype(BF16)
        awa_ref[...] += _dot_tn(alr_ref[...], du_b)
        put("alr", _dot_nt(du_b, wa_ref[...]).astype(BF16))

        for j in range(N_DEV):
            col = (j % 2) * SHARD_PAD
            for a, b in _shard_pad_cols(j):
                dsh_ref[j // 2, :, col:col + b - a] = dproj_ref[:, a:b]
                col += b - a
            dsh_ref[j // 2, :, col:(j % 2 + 1) * SHARD_PAD] = jnp.zeros((tm, SHARD_PAD - SHARD_IN), BF16)

        dh = _dot(dproj_ref[...], wt_ref[...])
        x = x_ref[...]
        r = lax.rsqrt(jnp.mean(x * x, axis=-1, keepdims=True) + NORM_EPS)
        nrm = x * r
        agin_ref[...] += jnp.sum(dh * nrm, axis=0, keepdims=True)
        dn = dh * g_ref[...]
        dx_ref[...] = dxres_ref[...] + r * (dn - nrm * jnp.mean(dn * nrm, axis=-1, keepdims=True))

        @pl.when(pl.program_id(0) == last)
        def _():
            small_ref[...] = jnp.zeros_like(small_ref)
            _put_rows(small_ref, SMALL_G_IN - base, agin_ref[...])
            _put_rows(small_ref, SMALL_B_ALPHA - base, aba_ref[...])
            for half in range(B_KEY_WIDTH // LANES):
                r0 = SMALL_W_ALPHA - base + half * B_GATE_RANK
                small_ref[r0:r0 + B_GATE_RANK, :] = awa_ref[0:B_GATE_RANK, half * LANES:(half + 1) * LANES]

    def rows(w):
        return pl.BlockSpec((tm, w), lambda i: (i, 0))

    names = ["dq", "dkv", "dza", "dqb", "dkb", "dvb", "dzb", "dla", "u", "alr", "dga", "dgb"]
    return pl.pallas_call(
        body, name="in_proj_bwd", grid=(T // tm,),
        in_specs=[rows(D_MODEL), rows(D_MODEL), rows(LANES), rows(LANES), _const_spec((1, D_MODEL)),
                  _const_spec((D_IN_PAD, D_MODEL)), _const_spec((RANK_PAD, B_KEY_WIDTH))]
                 + [rows(parts[n].shape[1]) for n in names],
        out_specs=[rows(D_MODEL), pl.BlockSpec((N_CHIPS, tm, 2 * SHARD_PAD), lambda i: (0, i, 0)),
                   pl.BlockSpec((SMALL_ROWS - base, LANES), lambda i: (0, 0))],
        out_shape=[jax.ShapeDtypeStruct((T, D_MODEL), F32), jax.ShapeDtypeStruct((N_CHIPS, T, 2 * SHARD_PAD), BF16),
                   jax.ShapeDtypeStruct((SMALL_ROWS - base, LANES), F32)],
        scratch_shapes=[pltpu.VMEM((tm, D_IN_PAD), BF16), pltpu.VMEM((1, D_MODEL), F32), pltpu.VMEM((1, B_KEY_WIDTH), F32),
                        pltpu.VMEM((RANK_PAD, B_KEY_WIDTH), F32)],
        compiler_params=_params("arbitrary"),
    )(x2, dxres, cosf, sinf, g_in, wt_pad, wa_pad, *[parts[n] for n in names])


FLIPS = [(dx, dy, dc) for dx in (0, 1) for dy in (0, 1) for dc in (0, 1)][1:]


def _my_place():
    return lax.axis_index("x"), lax.axis_index("y"), lax.axis_index("c")


def _any_specs(n):
    return [pl.BlockSpec(memory_space=pl.ANY)] * n


def _gather_first(shards, pos_col):
    n = len(shards)
    T = pos_col.shape[0]
    rows_per_pass = math.gcd(T, 512)
    invf, sign = _rope_lane_constants()

    def body(*refs):
        ins, (pos_ref, invf_ref, sign_ref) = refs[:n], refs[n:n + 3]
        outs, (cos_ref, sin_ref) = refs[n + 3:2 * n + 3], refs[2 * n + 3:2 * n + 5]
        send_sems, recv_sems, local_sems = refs[2 * n + 5:]
        x, y, c = _my_place()
        me, sibling = (x, y, c), (x, y, 1 - c)
        chips = [(1 - x, y), (x, 1 - y), (1 - x, 1 - y)]

        def block(a, px, py, pc):
            return outs[a].at[4 * px + 2 * py + pc]

        def copy(a, k, blk, to, src=None):
            return pltpu.make_async_remote_copy(
                src_ref=block(a, *blk) if src is None else src, dst_ref=block(a, *blk),
                send_sem=send_sems.at[a, k], recv_sem=recv_sems.at[a, k], device_id=to, device_id_type=MESH)

        mine = [pltpu.make_async_copy(ins[a], block(a, *me), local_sems.at[a]) for a in range(n)]
        for cp in mine:
            cp.start()
        first = []
        for a in range(n):
            first.append(copy(a, 0, me, sibling, src=ins[a]))
            first += [copy(a, 1 + j, me, (*chip, c), src=ins[a]) for j, chip in enumerate(chips)]
        for cp in first:
            cp.start()

        def tables(i, carry):
            rows = pl.ds(pl.multiple_of(i * rows_per_pass, rows_per_pass), rows_per_pass)
            ang = pos_ref[rows, :].astype(F32) * invf_ref[...]
            cos_ref[rows, :] = jnp.cos(ang)
            sin_ref[rows, :] = jnp.sin(ang) * sign_ref[...]
            return carry

        lax.fori_loop(0, T // rows_per_pass, tables, 0)

        passed = []
        for j, chip in enumerate(chips):
            for a in range(n):
                copy(a, 1 + j, (*chip, c), me).wait_recv()
                fwd = copy(a, 4 + j, (*chip, c), sibling)
                fwd.start()
                passed.append(fwd)
        for a in range(n):
            copy(a, 0, sibling, me).wait_recv()
            for j, chip in enumerate(chips):
                copy(a, 4 + j, (*chip, 1 - c), me).wait_recv()
        for cp in first + passed:
            cp.wait_send()
        for cp in mine:
            cp.wait()

    vmem = pl.BlockSpec(memory_space=pltpu.VMEM)
    res = pl.pallas_call(
        body, name="gather_weights",
        in_specs=_any_specs(n) + [vmem] * 3, out_specs=_any_specs(n) + [vmem] * 2,
        out_shape=[jax.ShapeDtypeStruct((N_DEV, *s.shape), s.dtype) for s in shards]
                  + [jax.ShapeDtypeStruct((T, LANES), F32)] * 2,
        scratch_shapes=[pltpu.SemaphoreType.DMA((n, 7)), pltpu.SemaphoreType.DMA((n, 7)), pltpu.SemaphoreType.DMA((n,))],
        compiler_params=pltpu.CompilerParams(vmem_limit_bytes=V7X_VMEM_LIMIT),
    )(*shards, pos_col, invf, sign)
    return res[:n], res[n], res[n + 1]


def _w_in_grad_rs(h, dsh, chip_order, small):
    T = h.shape[0]
    tk = math.gcd(T, 2048)
    nk = T // tk
    chip_flips = [(1, 1), (1, 0), (0, 1)]
    n_steps = len(chip_flips) + 1

    def body(order_ref, h_ref, d_ref, s_ref, own_ref, recv_ref, sall_ref,
             acc_ref, stage_ref, send_sems, recv_sems, ssend_sems, srecv_sems, local_sem):
        i, kk = pl.program_id(0), pl.program_id(1)
        x, y, c = _my_place()
        my_dev = 4 * x + 2 * y + c

        def small_copy(r, slot):
            dx, dy, dc = FLIPS[r]
            return pltpu.make_async_remote_copy(
                src_ref=s_ref, dst_ref=sall_ref.at[slot], send_sem=ssend_sems.at[r], recv_sem=srecv_sems.at[r],
                device_id=(x ^ dx, y ^ dy, c ^ dc), device_id_type=MESH)

        keep_small = pltpu.make_async_copy(s_ref, sall_ref.at[my_dev], local_sem)

        def shard_copy(slot, dx, dy, dc):
            r = FLIPS.index((dx, dy, dc))
            return pltpu.make_async_remote_copy(
                src_ref=stage_ref.at[slot, c ^ dc], dst_ref=recv_ref.at[r], send_sem=send_sems.at[r],
                recv_sem=recv_sems.at[r], device_id=(x ^ dx, y ^ dy, c ^ dc), device_id_type=MESH)

        def stage(slot):
            stage_ref[slot, 0] = acc_ref[0:SHARD_PAD, :].astype(BF16)
            stage_ref[slot, 1] = acc_ref[SHARD_PAD:2 * SHARD_PAD, :].astype(BF16)

        @pl.when((i == 0) & (kk == 0))
        def _():
            keep_small.start()
            for r in range(len(FLIPS)):
                small_copy(r, my_dev).start()

        @pl.when(kk == 0)
        def _():
            acc_ref[...] = jnp.zeros_like(acc_ref)

        acc_ref[...] += _dot_tn(d_ref[...], h_ref[...])

        for t, (dx, dy) in enumerate(chip_flips):
            @pl.when((i == t) & (kk == nk - 1))
            def _(t=t, dx=dx, dy=dy):
                if t >= 2:
                    for dc in (0, 1):
                        shard_copy(t % 2, *chip_flips[t - 2], dc).wait_send()
                stage(t % 2)
                for dc in (0, 1):
                    shard_copy(t % 2, dx, dy, dc).start()

        @pl.when((i == n_steps - 1) & (kk == nk - 1))
        def _():
            for dc in (0, 1):
                shard_copy(1, *chip_flips[1], dc).wait_send()
            stage(1)
            shard_copy(1, 0, 0, 1).start()

            @pl.when(c == 0)
            def _():
                own_ref[...] = acc_ref[0:SHARD_PAD, :]

            @pl.when(c == 1)
            def _():
                own_ref[...] = acc_ref[SHARD_PAD:2 * SHARD_PAD, :]

            for dc in (0, 1):
                shard_copy(0, *chip_flips[2], dc).wait_send()
            shard_copy(1, 0, 0, 1).wait_send()
            for r, (dx, dy, dc) in enumerate(FLIPS):
                shard_copy(0, dx, dy, dc).wait_recv()
                small_copy(r, 4 * (x ^ dx) + 2 * (y ^ dy) + (c ^ dc)).wait_recv()
                small_copy(r, my_dev).wait_send()
            keep_small.wait()

    return pl.pallas_call(
        body, name="w_in_grad_rs",
        grid_spec=pltpu.PrefetchScalarGridSpec(
            num_scalar_prefetch=1, grid=(n_steps, nk),
            in_specs=[pl.BlockSpec((tk, D_MODEL), lambda i, kk, order: (kk, 0)),
                      pl.BlockSpec((None, tk, 2 * SHARD_PAD), lambda i, kk, order: (order[i], kk, 0)),
                      pl.BlockSpec(memory_space=pl.ANY)],
            out_specs=[pl.BlockSpec((SHARD_PAD, D_MODEL), lambda i, kk, order: (0, 0)),
                       pl.BlockSpec(memory_space=pl.ANY), pl.BlockSpec(memory_space=pl.ANY)],
            scratch_shapes=[pltpu.VMEM((2 * SHARD_PAD, D_MODEL), F32), pltpu.VMEM((2, 2, SHARD_PAD, D_MODEL), BF16),
                            pltpu.SemaphoreType.DMA((7,)), pltpu.SemaphoreType.DMA((7,)),
                            pltpu.SemaphoreType.DMA((7,)), pltpu.SemaphoreType.DMA((7,)), pltpu.SemaphoreType.DMA]),
        out_shape=[jax.ShapeDtypeStruct((SHARD_PAD, D_MODEL), F32),
                   jax.ShapeDtypeStruct((len(FLIPS), SHARD_PAD, D_MODEL), BF16),
                   jax.ShapeDtypeStruct((N_DEV, *small.shape), F32)],
        compiler_params=_params("arbitrary", "arbitrary"),
    )(chip_order, h, dsh, small)


def _adam_math(w, g, m, v):
    m_new = ADAM_B1 * m + (1.0 - ADAM_B1) * g
    v_new = ADAM_B2 * v + (1.0 - ADAM_B2) * (g * g)
    m_hat = m_new / (1.0 - ADAM_B1 ** ADAM_STEP)
    v_hat = v_new / (1.0 - ADAM_B2 ** ADAM_STEP)
    delta = -ADAM_LR * (m_hat / (jnp.sqrt(v_hat) + ADAM_EPS) + ADAM_WD * w)
    return delta, m_new, v_new


def _adam_big(name, own, own_idx, recv, w, m, v):
    rw, cw = w.shape
    rp = own.shape[1]
    steps = 8
    by_cols = rp != rw
    blk_w = (rw, cw // steps) if by_cols else (rw // steps, cw)
    blk_g = (rp, cw // steps) if by_cols else (rw // steps, cw)
    at = (lambda i: (0, i)) if by_cols else (lambda i: (i, 0))

    def body(idx_ref, o_ref, r_ref, w_ref, m_ref, v_ref, g_ref, d_ref, mo_ref, vo_ref):
        g = o_ref[...].astype(F32)
        for r in range(len(FLIPS)):
            g = g + r_ref[r].astype(F32)
        g = g[0:blk_w[0], :]
        g_ref[...] = g
        d_ref[...], mo_ref[...], vo_ref[...] = _adam_math(w_ref[...], g, m_ref[...], v_ref[...])

    spec = pl.BlockSpec(blk_w, lambda i, idx_ref: at(i))
    return pl.pallas_call(
        body, name=name,
        grid_spec=pltpu.PrefetchScalarGridSpec(
            num_scalar_prefetch=1, grid=(steps,),
            in_specs=[pl.BlockSpec((None, *blk_g), lambda i, idx_ref: (idx_ref[0], *at(i))),
                      pl.BlockSpec((len(FLIPS), *blk_g), lambda i, idx_ref: (0, *at(i))), spec, spec, spec],
            out_specs=[spec] * 4),
        out_shape=[jax.ShapeDtypeStruct((rw, cw), F32)] * 4,
        compiler_params=_params("parallel"),
    )(own_idx, own, recv, w, m, v)


def _adam_small(small_all, params):
    flat = [a for triple in params for a in triple]
    n_par = len(params)

    def body(s_ref, *refs):
        ins, outs, loss_ref = refs[:3 * n_par], refs[3 * n_par:-1], refs[-1]
        g_slab = s_ref[0]
        for dev in range(1, N_DEV):
            g_slab = g_slab + s_ref[dev]
        loss_ref[...] = g_slab[SMALL_LOSS:SMALL_LOSS + 1, :]
        dev = 4 * lax.axis_index("x") + 2 * lax.axis_index("y") + lax.axis_index("c")
        alpha_full = jnp.concatenate([g_slab[SMALL_W_ALPHA + half * B_GATE_RANK:SMALL_W_ALPHA + (half + 1) * B_GATE_RANK]
                                      for half in range(B_KEY_WIDTH // LANES)], axis=1)
        alpha_mine = pltpu.roll(alpha_full, (B_KEY_WIDTH - dev * SHARD_ALPHA) % B_KEY_WIDTH, 1)[:, 0:SHARD_ALPHA]
        grads = [_take_rows(g_slab, SMALL_G_IN, D_MODEL // LANES), _take_rows(g_slab, SMALL_G_FINAL, D_MODEL // LANES),
                 _take_rows(g_slab, SMALL_G_GLA, B_WIDTH // LANES), _take_rows(g_slab, SMALL_B_ALPHA, B_KEY_WIDTH // LANES),
                 g_slab[SMALL_SINKS:SMALL_SINKS + 1, 0:A_HEADS], alpha_mine]
        for i, g in enumerate(grads):
            w_ref, m_ref, v_ref = ins[3 * i:3 * i + 3]
            delta, m_new, v_new = _adam_math(w_ref[...], g, m_ref[...], v_ref[...])
            outs[4 * i][...] = g
            outs[4 * i + 1][...] = delta
            outs[4 * i + 2][...] = m_new
            outs[4 * i + 3][...] = v_new

    res = pl.pallas_call(
        body, name="adam_small",
        out_shape=[jax.ShapeDtypeStruct(t[0].shape, F32) for t in params for _ in range(4)]
                  + [jax.ShapeDtypeStruct((1, LANES), F32)],
    )(small_all, *flat)
    return [res[4 * i:4 * i + 4] for i in range(n_par)], res[-1]


def _local_step(x, cosf, sinf, loss_target, g_in, wt_sh, wa_pad, b_alpha, sinks, g_gla, out_shards, g_final, chip_order):
    B, S, _ = x.shape
    T = B * S
    x2 = x.reshape(T, D_MODEL)
    tgt2 = loss_target.reshape(T, D_MODEL)
    f, (g_woa, g_wob, g_wo) = _in_proj(x2, cosf, sinf, g_in, wt_sh, wa_pad, b_alpha, out_shards)
    w_oa = jnp.concatenate([g_woa[j] for j in range(N_DEV)], axis=1)
    w_ob = jnp.concatenate([g_wob[j] for j in range(N_DEV)], axis=1)
    w_o = g_wo.reshape(D_MODEL, D_MODEL)
    sink_col = jnp.repeat(sinks, WINDOW).reshape(ATT_ROWS, 1)
    attn, lse = _attn_fwd(f["qkv"], sinks, B, S)
    o_gla, st_all = _gla_fwd(f["q"], f["k"], f["cum"], f["vb"], B, S)
    (dxres, dattn, dog, dza, dzb, dga, dgb, dw_o, dw_oa, dw_ob, small_a) = _merge(
        x2, tgt2, attn, f["za"], o_gla, f["zb"], f["ga"], f["gb"], w_oa, w_ob, w_o, g_gla, g_final)
    dq, dkv, dsink = _attn_bwd(f["qkv"], dattn, attn, lse, sink_col, B, S)
    (dqb, dkb, dvb, dla), (rv_o, rv_oa, rv_ob) = _gla_bwd(f["q"], f["k"], f["cum"], f["vb"], dog, st_all, B, S,
                                                        [dw_o, dw_oa, dw_ob])
    parts = dict(dq=dq, dkv=dkv, dza=dza, dqb=dqb, dkb=dkb, dvb=dvb, dzb=dzb, dla=dla, u=f["u"], alr=f["alr"],
                 dga=dga, dgb=dgb)
    dx, dsh, small_c = _in_proj_bwd(x2, dxres, cosf, sinf, g_in, f["wt_pad"], wa_pad, parts)
    small = jnp.concatenate([small_a, dsink, small_c], axis=0)
    own_in, rv_in, small_all = _w_in_grad_rs(f["h"], dsh, chip_order, small)
    return dict(grad_x=dx.reshape(B, S, D_MODEL), own_in=own_in, rv_in=rv_in,
                own_o=dw_o, rv_o=rv_o, own_oa=dw_oa, rv_oa=rv_oa, own_ob=dw_ob, rv_ob=rv_ob, small_all=small_all)


def kernel(x, positions, g_in, w_in, w_alpha_up, b_alpha, attn_sinks, g_gla_norm, w_out_a, w_out_b, w_o, g_final, loss_target, m_g_in, m_w_in, m_w_alpha_up, m_b_alpha, m_attn_sinks, m_g_gla_norm, m_w_out_a, m_w_out_b, m_w_o, m_g_final, v_g_in, v_w_in, v_w_alpha_up, v_b_alpha, v_attn_sinks, v_g_gla_norm, v_w_out_a, v_w_out_b, v_w_o, v_g_final):
    xi, yi, ci = _my_place()
    dev_idx = (4 * xi + 2 * yi + ci).reshape(1).astype(jnp.int32)
    chip = 2 * xi + yi
    chip_order = jnp.stack([chip ^ 3, chip ^ 2, chip ^ 1, chip]).astype(jnp.int32)

    (g_win, g_wa), cosf, sinf = _gather_first(
        [jnp.pad(w_in[0].T.astype(BF16), ((0, SHARD_PAD - SHARD_IN), (0, 0))), w_alpha_up[0].astype(BF16)],
        positions.reshape(-1, 1))
    wt_sh = g_win.reshape(N_DEV * SHARD_PAD, D_MODEL)
    wa_pad = jnp.pad(jnp.concatenate([g_wa[j] for j in range(N_DEV)], axis=1), ((0, RANK_PAD - B_GATE_RANK), (0, 0)))

    r = _local_step(x, cosf, sinf, loss_target, g_in, wt_sh, wa_pad, b_alpha, attn_sinks[0], g_gla_norm,
                    [w_out_a[0].astype(BF16), w_out_b[0].astype(BF16), w_o[0].astype(BF16)],
                    g_final.reshape(1, D_MODEL), chip_order)

    first = jnp.zeros((1,), jnp.int32)
    big = [_adam_big("adam_w_in", r["own_in"][None], first, r["rv_in"], w_in[0].T, m_w_in[0].T, v_w_in[0].T),
           _adam_big("adam_w_out_a", r["own_oa"], dev_idx, r["rv_oa"], w_out_a[0], m_w_out_a[0], v_w_out_a[0]),
           _adam_big("adam_w_out_b", r["own_ob"], dev_idx, r["rv_ob"], w_out_b[0], m_w_out_b[0], v_w_out_b[0]),
           _adam_big("adam_w_o", r["own_o"], dev_idx, r["rv_o"], w_o[0], m_w_o[0], v_w_o[0])]
    row = lambda a: a.reshape(1, D_MODEL)
    big[0] = [a.T for a in big[0]]
    (s_in, s_final, s_gla, s_ba, s_sinks, s_wa), loss_row = _adam_small(r["small_all"], [
        (g_in, m_g_in, v_g_in), (row(g_final), row(m_g_final), row(v_g_final)),
        (g_gla_norm, m_g_gla_norm, v_g_gla_norm), (b_alpha, m_b_alpha, v_b_alpha),
        (attn_sinks, m_attn_sinks, v_attn_sinks), (w_alpha_up[0], m_w_alpha_up[0], v_w_alpha_up[0])])

    def group(i):
        return (s_in[i], big[0][i][None], s_wa[i][None], s_ba[i], s_sinks[i], s_gla[i], big[1][i][None], big[2][i][None],
                big[3][i][None], s_final[i].reshape(D_MODEL))

    return (loss_row[0, 0], r["grad_x"], *group(0), *group(1), *group(2), *group(3))
```

```python
import functools
import math

import numpy as np
import jax
import jax.numpy as jnp
from jax import lax
from jax.experimental import pallas as pl
from jax.experimental.pallas import tpu as pltpu

F32 = jnp.float32
BF16 = jnp.bfloat16
MESH = pl.DeviceIdType.MESH

D_MODEL = 1024
A_HEADS, A_KV_HEADS, A_HEAD_DIM = 8, 2, 64
A_WIDTH, A_KV_WIDTH = 512, 128
WINDOW = 128
ROPE_THETA = 500000.0
ROPE_DIM = 16
B_HEADS, B_KEY_DIM, B_VAL_DIM = 4, 64, 128
B_KEY_WIDTH, B_WIDTH = 256, 512
B_GATE_RANK = 16
B_GATE_TEMP = 16.0
B_CHUNK = 64
NORM_EPS = 1e-6
NEG_BIG = -1e30
D_IN = 4880
N_DEV = 8
N_CHIPS = 4
ADAM_LR, ADAM_B1, ADAM_B2, ADAM_EPS, ADAM_WD, ADAM_STEP = 0.001, 0.9, 0.999, 1e-08, 0.01, 10

LANES = 128
V7X_VMEM_LIMIT = 56 * 1024 * 1024

RANK_PAD = LANES
SEG = {}
_off = 0
for _name, _w in (("qa", 512), ("ka", 128), ("va", 128), ("za", 512), ("qb", 256), ("kb", 256),
                  ("vb", 512), ("zb", 512), ("alr", RANK_PAD), ("ga", 1024), ("gb", 1024)):
    SEG[_name] = (_off, _off + _w)
    _off += _w
D_IN_PAD = _off
ALR_SRC = SEG["alr"][0]
QKV_K, QKV_V, QKV_W = SEG["ka"][0], SEG["va"][0], SEG["va"][1]

SHARD_IN = D_IN // N_DEV
SHARD_PAD = 640
SHARD_OUT = D_MODEL // N_DEV
SHARD_ALPHA = B_KEY_WIDTH // N_DEV

SMALL_G_FINAL, SMALL_G_GLA, SMALL_LOSS, SMALL_SINKS, SMALL_G_IN, SMALL_B_ALPHA, SMALL_W_ALPHA = 0, 8, 12, 16, 24, 32, 40
SMALL_ROWS = 72


def _dot(a, b):
    return jnp.dot(a, b, preferred_element_type=F32)


def _dot_nt(a, b):
    return lax.dot_general(a, b, (((1,), (1,)), ((), ())), preferred_element_type=F32)


def _dot_tn(a, b):
    return lax.dot_general(a, b, (((0,), (0,)), ((), ())), preferred_element_type=F32)


def _sigmoid(z):
    return 1.0 / (1.0 + jnp.exp(-z))


def _params(*sem):
    return pltpu.CompilerParams(dimension_semantics=sem, vmem_limit_bytes=V7X_VMEM_LIMIT)


def _const_spec(shape):
    nd = len(shape)
    return pl.BlockSpec(shape, lambda *_: (0,) * nd, pipeline_mode=pl.Buffered(1))


def _lane_iota(shape):
    return lax.broadcasted_iota(jnp.int32, shape, 1)


def _row_iota(shape):
    return lax.broadcasted_iota(jnp.int32, shape, 0)


def _split3(v):
    hi = v.astype(BF16)
    r1 = v - hi.astype(F32)
    mid = r1.astype(BF16)
    lo = (r1 - mid.astype(F32)).astype(BF16)
    return hi, mid, lo


def _put_rows(ref, row0, vec):
    for r in range(vec.shape[1] // LANES):
        ref[row0 + r:row0 + r + 1, :] = vec[:, r * LANES:(r + 1) * LANES]


def _take_rows(slab, row0, n):
    return jnp.concatenate([slab[row0 + r:row0 + r + 1, :] for r in range(n)], axis=1)


def _rope_lane_constants():
    half = ROPE_DIM // 2
    inv_freq = np.exp(-math.log(ROPE_THETA) * np.arange(half, dtype=np.float32) * np.float32(2.0 / ROPE_DIM)).astype(np.float32)
    lane = np.arange(LANES)
    j = lane % A_HEAD_DIM
    invf = np.where(j < ROPE_DIM, inv_freq[j % half], 0.0).astype(np.float32)
    sign = np.where(j < half, -1.0, np.where(j < ROPE_DIM, 1.0, 0.0)).astype(np.float32)
    return jnp.asarray(invf)[None, :], jnp.asarray(sign)[None, :]


def _rope_slab(t, cos, sin_signed):
    first = (_lane_iota(t.shape) % A_HEAD_DIM) < (ROPE_DIM // 2)
    partner = jnp.where(first, pltpu.roll(t, LANES - ROPE_DIM // 2, 1), pltpu.roll(t, ROPE_DIM // 2, 1))
    return t * cos + partner * sin_signed


def _shard_pad_cols(j):
    cut = ALR_SRC + B_GATE_RANK
    shift = RANK_PAD - B_GATE_RANK
    a, b = j * SHARD_IN, (j + 1) * SHARD_IN
    if b <= cut:
        return [(a, b)]
    if a >= cut:
        return [(a + shift, b + shift)]
    return [(a, cut), (cut + shift, b + shift)]


def _in_proj(x2, cosf, sinf, g_in, wt_sh, wa_pad, b_alpha, later_shards):
    T = x2.shape[0]
    tm = math.gcd(T, 512)
    last = T // tm - 1
    nl = len(later_shards)

    def body(x_ref, cos_ref, sin_ref, g_ref, wsh_ref, wa_ref, ba_ref, *rest):
        sh_refs, rest = rest[:nl], rest[nl:]
        (h_ref, qkv_ref, za_ref, q_ref, k_ref, vb_ref, zb_ref, alr_ref, u_ref, cum_ref, ga_ref, gb_ref, wt_out) = rest[:13]
        all_refs, (wt_ref, send_sems, recv_sems, local_sems, wt_sem) = rest[13:13 + nl], rest[13 + nl:]
        wt_copy = pltpu.make_async_copy(wt_ref, wt_out, wt_sem)
        px, py, pc = _my_place()
        my_dev = 4 * px + 2 * py + pc

        def wcopy(a, r, slot):
            dx, dy, dc = FLIPS[r]
            return pltpu.make_async_remote_copy(
                src_ref=sh_refs[a], dst_ref=all_refs[a].at[slot], send_sem=send_sems.at[a, r],
                recv_sem=recv_sems.at[a, r], device_id=(px ^ dx, py ^ dy, pc ^ dc), device_id_type=MESH)

        keep = [pltpu.make_async_copy(sh_refs[a], all_refs[a].at[my_dev], local_sems.at[a]) for a in range(nl)]

        @pl.when(pl.program_id(0) == 0)
        def _():
            for a in range(nl):
                keep[a].start()
                for r in range(len(FLIPS)):
                    wcopy(a, r, my_dev).start()

        @pl.when(pl.program_id(0) == 0)
        def _():
            for j in range(N_DEV):
                src = j * SHARD_PAD
                for a, b in _shard_pad_cols(j):
                    wt_ref[a:b, :] = wsh_ref[src:src + b - a, :]
                    src += b - a
            a, b = SEG["alr"]
            wt_ref[a + B_GATE_RANK:b, :] = jnp.zeros((RANK_PAD - B_GATE_RANK, D_MODEL), BF16)
            wt_copy.start()

        x = x_ref[...]
        r = lax.rsqrt(jnp.mean(x * x, axis=-1, keepdims=True) + NORM_EPS)
        h = (x * r * g_ref[...]).astype(BF16)
        h_ref[...] = h

        def seg(name):
            a, b = SEG[name]
            return _dot_nt(h, wt_ref[a:b, :])

        cos, sin = cos_ref[...], sin_ref[...]
        qa = seg("qa")
        for s in range(A_WIDTH // LANES):
            qkv_ref[:, s * LANES:(s + 1) * LANES] = _rope_slab(qa[:, s * LANES:(s + 1) * LANES], cos, sin).astype(BF16)
        qkv_ref[:, QKV_K:QKV_V] = _rope_slab(seg("ka"), cos, sin).astype(BF16)
        qkv_ref[:, QKV_V:QKV_W] = seg("va").astype(BF16)
        za_ref[...] = seg("za")
        q_ref[...] = seg("qb")
        k_ref[...] = seg("kb")
        vb_ref[...] = seg("vb").astype(BF16)
        zb_ref[...] = seg("zb")
        ga_ref[...] = seg("ga")
        gb_ref[...] = seg("gb")
        alr = seg("alr").astype(BF16)
        alr_ref[...] = alr
        u = _dot(alr, wa_ref[...]) + ba_ref[...]
        u_ref[...] = u
        log_a = (jnp.minimum(u, 0.0) - jnp.log(1.0 + jnp.exp(-jnp.abs(u)))) * (1.0 / B_GATE_TEMP)
        row, col = _row_iota((tm, tm)), _lane_iota((tm, tm))
        tri = ((row // B_CHUNK == col // B_CHUNK) & (col <= row)).astype(BF16)
        hi, mid, lo = _split3(log_a)
        cum_ref[...] = _dot(tri, hi) + _dot(tri, mid) + _dot(tri, lo)

        @pl.when(pl.program_id(0) == last)
        def _():
            for a in range(nl):
                for r, (dx, dy, dc) in enumerate(FLIPS):
                    wcopy(a, r, 4 * (px ^ dx) + 2 * (py ^ dy) + (pc ^ dc)).wait_recv()
                    wcopy(a, r, my_dev).wait_send()
                keep[a].wait()
            wt_copy.wait()

    def rows(w):
        return pl.BlockSpec((tm, w), lambda i: (i, 0))

    outs = [("h", D_MODEL, BF16), ("qkv", QKV_W, BF16), ("za", A_WIDTH, F32), ("q", B_KEY_WIDTH, F32),
            ("k", B_KEY_WIDTH, F32), ("vb", B_WIDTH, BF16), ("zb", B_WIDTH, F32), ("alr", RANK_PAD, BF16),
            ("u", B_KEY_WIDTH, F32), ("cum", B_KEY_WIDTH, F32), ("ga", D_MODEL, F32), ("gb", D_MODEL, F32)]
    res = pl.pallas_call(
        body, name="in_proj", grid=(T // tm,),
        in_specs=[rows(D_MODEL), rows(LANES), rows(LANES), _const_spec((1, D_MODEL)),
                  _const_spec((N_DEV * SHARD_PAD, D_MODEL)), _const_spec((RANK_PAD, B_KEY_WIDTH)),
                  _const_spec((1, B_KEY_WIDTH))] + _any_specs(nl),
        out_specs=[rows(w) for _, w, _ in outs] + _any_specs(1 + nl),
        out_shape=[jax.ShapeDtypeStruct((T, w), dt) for _, w, dt in outs]
                  + [jax.ShapeDtypeStruct((D_IN_PAD, D_MODEL), BF16)]
                  + [jax.ShapeDtypeStruct((N_DEV, *sh.shape), sh.dtype) for sh in later_shards],
        scratch_shapes=[pltpu.VMEM((D_IN_PAD, D_MODEL), BF16),
                        pltpu.SemaphoreType.DMA((nl, len(FLIPS))), pltpu.SemaphoreType.DMA((nl, len(FLIPS))),
                        pltpu.SemaphoreType.DMA((nl,)), pltpu.SemaphoreType.DMA],
        compiler_params=_params("arbitrary"),
    )(x2, cosf, sinf, g_in, wt_sh, wa_pad, b_alpha, *later_shards)
    n_out = len(outs) + 1
    return dict(zip([n for n, _, _ in outs] + ["wt_pad"], res[:n_out])), res[n_out:]


def _dup_kv_head(t, g):
    tf = t.astype(F32)
    keep = (_lane_iota(tf.shape) < A_HEAD_DIM) == (g == 0)
    return jnp.where(keep, tf, pltpu.roll(tf, A_HEAD_DIM, 1)).astype(BF16)


def _stack_heads(t):
    lo = _lane_iota(t.shape) < A_HEAD_DIM
    zero = jnp.zeros_like(t)
    return jnp.concatenate([jnp.where(lo, t, zero), jnp.where(lo, zero, t)], axis=0)


ATT_ROWS = A_HEADS * WINDOW
GROUP_ROWS = ATT_ROWS // A_KV_HEADS
HEADS_PER_GROUP = A_HEADS // A_KV_HEADS


def _band_mask_t(n):
    kj = _row_iota((2 * WINDOW, GROUP_ROWS)) - WINDOW
    qi = _lane_iota((2 * WINDOW, GROUP_ROWS)) % WINDOW
    return (kj <= qi) & (qi - kj < WINDOW) & ((n > 0) | (kj >= 0))


def _stacked_queries(ref, g):
    pairs = range(g * HEADS_PER_GROUP // 2, (g + 1) * HEADS_PER_GROUP // 2)
    return jnp.concatenate([_stack_heads(ref[:, p * LANES:(p + 1) * LANES]) for p in pairs], axis=0)


def _unstack_heads(t, g, ref, dtype):
    lo = _lane_iota((WINDOW, LANES)) < A_HEAD_DIM
    for hh in range(HEADS_PER_GROUP // 2):
        p = g * HEADS_PER_GROUP // 2 + hh
        ref[:, p * LANES:(p + 1) * LANES] = jnp.where(lo, t[2 * hh * WINDOW:(2 * hh + 1) * WINDOW],
                                                       t[(2 * hh + 1) * WINDOW:(2 * hh + 2) * WINDOW]).astype(dtype)


def _attn_fwd(qkv, sink_row, B, S):
    T = B * S
    nb = S // WINDOW
    scale = A_HEAD_DIM ** -0.5

    def body(sink_ref, q_ref, kc_ref, vc_ref, kp_ref, vp_ref, o_ref, lse_ref):
        valid = _band_mask_t(pl.program_id(1))
        k = jnp.concatenate([kp_ref[...], kc_ref[...]], axis=0)
        v = jnp.concatenate([vp_ref[...], vc_ref[...]], axis=0)
        lse_rows = []
        for g in range(A_KV_HEADS):
            kd, vd = _dup_kv_head(k, g), _dup_kv_head(v, g)
            s = jnp.where(valid, _dot_nt(kd, _stacked_queries(q_ref, g)) * scale, NEG_BIG)
            sink = sink_ref[:, g * GROUP_ROWS:(g + 1) * GROUP_ROWS]
            m = jnp.maximum(jnp.max(s, axis=0, keepdims=True), sink)
            e = jnp.exp(s - m)
            den = jnp.sum(e, axis=0, keepdims=True) + jnp.exp(sink - m)
            o = _dot_tn((e * (1.0 / den)).astype(BF16), vd)
            _unstack_heads(o, g, o_ref, F32)
            lse = m + jnp.log(den)
            lse_rows += [lse[:, j * WINDOW:(j + 1) * WINDOW] for j in range(HEADS_PER_GROUP)]
        by_head = jnp.concatenate(lse_rows + [jnp.zeros((WINDOW - A_HEADS, WINDOW), F32)], axis=0)
        lse_ref[...] = by_head.T

    def cur(col, w):
        return pl.BlockSpec((WINDOW, w), lambda b, n: (b * nb + n, col))

    def prev(col):
        return pl.BlockSpec((WINDOW, LANES), lambda b, n: (b * nb + jnp.maximum(n - 1, 0), col))

    kcol, vcol = QKV_K // LANES, QKV_V // LANES
    return pl.pallas_call(
        body, name="attn_fwd", grid=(B, nb),
        in_specs=[_const_spec((1, ATT_ROWS)), cur(0, A_WIDTH), cur(kcol, LANES), cur(vcol, LANES), prev(kcol), prev(vcol)],
        out_specs=[cur(0, A_WIDTH), cur(0, LANES)],
        out_shape=[jax.ShapeDtypeStruct((T, A_WIDTH), F32), jax.ShapeDtypeStruct((T, LANES), F32)],
        compiler_params=_params("parallel", "parallel"),
    )(sink_row, qkv, qkv, qkv, qkv, qkv)


ATT_CHUNK = 64


def _chunk_masks(n):
    masks = []
    for half in range(WINDOW // ATT_CHUNK):
        qi = _row_iota((ATT_CHUNK, 2 * WINDOW)) + half * ATT_CHUNK
        kj = _lane_iota((ATT_CHUNK, 2 * WINDOW)) - WINDOW
        masks.append((kj <= qi) & (qi - kj < WINDOW) & ((n > 0) | (kj >= 0)))
    return masks


def _all_stacked_queries(ref):
    return jnp.concatenate([_stacked_queries(ref, g) for g in range(A_KV_HEADS)], axis=0)


def _by_group(fn, lhs, rhs_per_group):
    return jnp.concatenate([fn(lhs[g * GROUP_ROWS:(g + 1) * GROUP_ROWS], rhs_per_group[g])
                            for g in range(A_KV_HEADS)], axis=0)


def _attn_bwd(qkv, do, out, lse, sink_col, B, S):
    T = B * S
    nb = S // WINDOW
    scale = A_HEAD_DIM ** -0.5

    def body(sink_ref, q_ref, kc_ref, vc_ref, kp_ref, vp_ref, do_ref, out_ref, lse_ref,
             dq_ref, dkv_ref, dsink_ref, carry_ref, s_ref, dp_ref, ds_ref, p_ref):
        b, n = pl.program_id(0), pl.program_id(1)
        active = n < nb
        masks = _chunk_masks(jnp.minimum(n, nb - 1))

        @pl.when((b == 0) & (n == 0))
        def _():
            dsink_ref[...] = jnp.zeros_like(dsink_ref)

        k = jnp.concatenate([kp_ref[...], kc_ref[...]], axis=0)
        v = jnp.concatenate([vp_ref[...], vc_ref[...]], axis=0)
        kd = [_dup_kv_head(k, g) for g in range(A_KV_HEADS)]
        vd = [_dup_kv_head(v, g) for g in range(A_KV_HEADS)]
        qs = _all_stacked_queries(q_ref)
        dos = _all_stacked_queries(do_ref)
        s_ref[...] = _by_group(_dot_nt, qs, kd)
        dp_ref[...] = _by_group(_dot_nt, dos, vd)
        lane = _lane_iota((ATT_CHUNK, LANES))
        lo = lane < A_HEAD_DIM
        lane1 = _lane_iota((1, LANES))
        dsink_row = jnp.zeros((1, LANES), F32)
        for c in range(ATT_ROWS // ATT_CHUNK):
            rows = slice(c * ATT_CHUNK, (c + 1) * ATT_CHUNK)
            head, half = divmod(c, WINDOW // ATT_CHUNK)
            qrows = slice(half * ATT_CHUNK, (half + 1) * ATT_CHUNK)
            slab = slice((head // 2) * LANES, (head // 2 + 1) * LANES)
            lse_col = jnp.sum(jnp.where(lane == head, lse_ref[qrows, :], 0.0), axis=-1, keepdims=True)
            prod = do_ref[qrows, slab].astype(F32) * out_ref[qrows, slab]
            mine = lo if head % 2 == 0 else jnp.logical_not(lo)
            delta = jnp.sum(jnp.where(mine, prod, 0.0), axis=-1, keepdims=True)
            s = jnp.where(masks[half], s_ref[rows, :] * scale, NEG_BIG)
            prob = jnp.exp(s - lse_col)
            p_ref[rows, :] = prob.astype(BF16)
            ds_ref[rows, :] = (prob * (dp_ref[rows, :] - delta) * scale).astype(BF16)
            w = -jnp.exp(sink_ref[rows, :] - lse_col) * delta
            dsink_row += jnp.where(lane1 == head, jnp.sum(w, axis=0, keepdims=True), 0.0)
        dq = _by_group(_dot, ds_ref[...], kd)
        lo_q = _lane_iota((WINDOW, LANES)) < A_HEAD_DIM
        for p in range(A_HEADS // 2):
            dq_ref[:, p * LANES:(p + 1) * LANES] = jnp.where(
                lo_q, dq[2 * p * WINDOW:(2 * p + 1) * WINDOW], dq[(2 * p + 1) * WINDOW:(2 * p + 2) * WINDOW]).astype(BF16)
        lane2 = _lane_iota((2 * WINDOW, LANES))
        dk_tot = jnp.zeros((2 * WINDOW, LANES), F32)
        dv_tot = jnp.zeros((2 * WINDOW, LANES), F32)
        for g in range(A_KV_HEADS):
            grows = slice(g * GROUP_ROWS, (g + 1) * GROUP_ROWS)
            dk_acc = _dot_tn(ds_ref[grows, :], qs[grows])
            dv_acc = _dot_tn(p_ref[grows, :], dos[grows])
            mine = (lane2 < A_HEAD_DIM) == (g == 0)
            dk_tot = jnp.where(mine, dk_acc + pltpu.roll(dk_acc, A_HEAD_DIM, 1), dk_tot)
            dv_tot = jnp.where(mine, dv_acc + pltpu.roll(dv_acc, A_HEAD_DIM, 1), dv_tot)
        gate = jnp.where(active, 1.0, 0.0)
        dsink_ref[0:1, :] += dsink_row * gate
        dkv_ref[:, 0:LANES] = (carry_ref[:, 0:LANES] + dk_tot[:WINDOW] * gate).astype(BF16)
        dkv_ref[:, LANES:] = (carry_ref[:, LANES:] + dv_tot[:WINDOW] * gate).astype(BF16)
        carry_ref[:, 0:LANES] = dk_tot[WINDOW:]
        carry_ref[:, LANES:] = dv_tot[WINDOW:]

    def cur(col, w):
        return pl.BlockSpec((WINDOW, w), lambda b, n: (b * nb + jnp.minimum(n, nb - 1), col))

    def prev(col):
        return pl.BlockSpec((WINDOW, LANES), lambda b, n: (b * nb + jnp.maximum(jnp.minimum(n, nb - 1) - 1, 0), col))

    lag = pl.BlockSpec((WINDOW, 2 * LANES), lambda b, n: (b * nb + jnp.maximum(n - 1, 0), 0))
    kcol, vcol = QKV_K // LANES, QKV_V // LANES
    scores = (ATT_ROWS, 2 * WINDOW)
    return pl.pallas_call(
        body, name="attn_bwd", grid=(B, nb + 1),
        in_specs=[_const_spec((ATT_ROWS, 1)), cur(0, A_WIDTH), cur(kcol, LANES), cur(vcol, LANES),
                  prev(kcol), prev(vcol), cur(0, A_WIDTH), cur(0, A_WIDTH), cur(0, LANES)],
        out_specs=[cur(0, A_WIDTH), lag, pl.BlockSpec((8, LANES), lambda b, n: (0, 0))],
        out_shape=[jax.ShapeDtypeStruct((T, A_WIDTH), BF16), jax.ShapeDtypeStruct((T, 2 * LANES), BF16),
                   jax.ShapeDtypeStruct((8, LANES), F32)],
        scratch_shapes=[pltpu.VMEM((WINDOW, 2 * LANES), F32), pltpu.VMEM(scores, F32), pltpu.VMEM(scores, F32),
                        pltpu.VMEM(scores, BF16), pltpu.VMEM(scores, BF16)],
        compiler_params=_params("arbitrary", "arbitrary"),
    )(sink_col, qkv, qkv, qkv, qkv, qkv, do, out, lse)


GLA_TILE = 256
CHUNKS_PER_TILE = GLA_TILE // B_CHUNK


def _gla_factors(q_ref, k_ref, cum_ref):
    scale = B_KEY_DIM ** -0.5
    cum = cum_ref[...]
    shape = (B_CHUNK, B_KEY_WIDTH)
    last = jnp.concatenate([jnp.broadcast_to(cum_ref[pl.ds(c * B_CHUNK + B_CHUNK - 1, 1), :], shape)
                            for c in range(CHUNKS_PER_TILE)], axis=0)
    mid = jnp.concatenate([jnp.broadcast_to(cum_ref[pl.ds(c * B_CHUNK + B_CHUNK // 2 - 1, 1), :], shape)
                           for c in range(CHUNKS_PER_TILE)], axis=0)
    e_qm, e_km, e_qe, e_kd = jnp.exp(cum - mid), jnp.exp(mid - cum), jnp.exp(cum), jnp.exp(last - cum)
    qs = q_ref[...] * scale
    k = k_ref[...]
    return qs, k, (e_qm, e_km, e_qe, e_kd)


def _head_mask(shape, h):
    return (_lane_iota(shape) // B_KEY_DIM) == h


def _stack_masked(t):
    return jnp.concatenate([jnp.where(_head_mask(t.shape, h), t, 0.0) for h in range(B_HEADS)], axis=0).astype(BF16)


def _select_heads(t):
    shape = (B_CHUNK, B_KEY_WIDTH)
    out = jnp.zeros(shape, F32)
    for h in range(B_HEADS):
        out = jnp.where(_head_mask(shape, h), t[h * B_CHUNK:(h + 1) * B_CHUNK], out)
    return out


def _select_state(t):
    shape = (B_VAL_DIM, B_KEY_WIDTH)
    out = jnp.zeros(shape, F32)
    for h in range(B_HEADS):
        out = jnp.where(_head_mask(shape, h), t[h * B_VAL_DIM:(h + 1) * B_VAL_DIM], out)
    return out


def _rows_by_head(t):
    return jnp.concatenate([t[:, h * B_VAL_DIM:(h + 1) * B_VAL_DIM] for h in range(B_HEADS)], axis=0)


def _intra_mask():
    i, j = _row_iota((GLA_TILE, GLA_TILE)), _lane_iota((GLA_TILE, GLA_TILE))
    return (i // B_CHUNK == j // B_CHUNK) & (j <= i)


def _pair_stack(t, p):
    slab = t[:, p * LANES:(p + 1) * LANES]
    lo = _lane_iota(slab.shape) < B_KEY_DIM
    return jnp.concatenate([jnp.where(lo, slab, 0.0), jnp.where(lo, 0.0, slab)], axis=0).astype(BF16)


def _gla_fwd(q, k, cum, vb, B, S):
    T = B * S
    nt = S // GLA_TILE

    def body(q_ref, k_ref, cum_ref, v_ref, o_ref, st_all_ref, st_ref):
        @pl.when(pl.program_id(1) == 0)
        def _():
            st_ref[...] = jnp.zeros_like(st_ref)

        qs, kk, (e_qm, e_km, e_qe, e_kd) = _gla_factors(q_ref, k_ref, cum_ref)
        qm, km, qe, kd = qs * e_qm, kk * e_km, qs * e_qe, (kk * e_kd).astype(BF16)
        mask = _intra_mask()
        intra = []
        for p in range(B_HEADS // 2):
            a = _dot_nt(_pair_stack(qm, p), km[:, p * LANES:(p + 1) * LANES].astype(BF16))
            for hh in range(2):
                h = 2 * p + hh
                att = jnp.where(mask, a[hh * GLA_TILE:(hh + 1) * GLA_TILE], 0.0).astype(BF16)
                intra.append(_dot(att, v_ref[:, h * B_VAL_DIM:(h + 1) * B_VAL_DIM]))
        inter = []
        for c in range(CHUNKS_PER_TILE):
            rows = slice(c * B_CHUNK, (c + 1) * B_CHUNK)
            st = st_ref[...]
            st_all_ref[c] = st
            inter.append(_dot_nt(_stack_masked(qe[rows]), st.astype(BF16)))
            inc = _select_state(_dot_tn(v_ref[rows, :], kd[rows]))
            decay = jnp.exp(cum_ref[pl.ds(c * B_CHUNK + B_CHUNK - 1, 1), :])
            st_ref[...] = st * decay + inc
        for h in range(B_HEADS):
            oi = jnp.concatenate([inter[c][h * B_CHUNK:(h + 1) * B_CHUNK] for c in range(CHUNKS_PER_TILE)], axis=0)
            o_ref[:, h * B_VAL_DIM:(h + 1) * B_VAL_DIM] = intra[h] + oi

    def rows(w):
        return pl.BlockSpec((GLA_TILE, w), lambda b, t: (b * nt + t, 0))

    return pl.pallas_call(
        body, name="gla_fwd", grid=(B, nt),
        in_specs=[rows(B_KEY_WIDTH), rows(B_KEY_WIDTH), rows(B_KEY_WIDTH), rows(B_WIDTH)],
        out_specs=[rows(B_WIDTH),
                   pl.BlockSpec((CHUNKS_PER_TILE, B_VAL_DIM, B_KEY_WIDTH), lambda b, t: (b * nt + t, 0, 0))],
        out_shape=[jax.ShapeDtypeStruct((T, B_WIDTH), F32),
                   jax.ShapeDtypeStruct((T // B_CHUNK, B_VAL_DIM, B_KEY_WIDTH), F32)],
        scratch_shapes=[pltpu.VMEM((B_VAL_DIM, B_KEY_WIDTH), F32)],
        compiler_params=_params("arbitrary", "arbitrary"),
    )(q, k, cum, vb)


def _gla_bwd(q, k, cum, vb, do, st_all, B, S, wgrads):
    T = B * S
    nt = S // GLA_TILE
    scale = B_KEY_DIM ** -0.5
    nw = len(wgrads)

    def body(q_ref, k_ref, cum_ref, v_ref, do_ref, st_all_ref, *rest):
        g_refs, (dq_ref, dk_ref, dv_ref, dla_ref) = rest[:nw], rest[nw:nw + 4]
        rv_refs, (dst_ref, send_sems, recv_sems) = rest[nw + 4:2 * nw + 4], rest[2 * nw + 4:]
        x, y, c = _my_place()

        def wcopy(a, r):
            dx, dy, dc = FLIPS[r]
            return pltpu.make_async_remote_copy(
                src_ref=g_refs[a].at[4 * (x ^ dx) + 2 * (y ^ dy) + (c ^ dc)], dst_ref=rv_refs[a].at[r],
                send_sem=send_sems.at[a, r], recv_sem=recv_sems.at[a, r],
                device_id=(x ^ dx, y ^ dy, c ^ dc), device_id_type=MESH)

        @pl.when((pl.program_id(0) == 0) & (pl.program_id(1) == 0))
        def _():
            for a in range(nw):
                for r in range(len(FLIPS)):
                    wcopy(a, r).start()

        @pl.when(pl.program_id(1) == 0)
        def _():
            dst_ref[...] = jnp.zeros_like(dst_ref)

        qs, kk, (e_qm, e_km, e_qe, e_kd) = _gla_factors(q_ref, k_ref, cum_ref)
        qm, km, qe, kd = qs * e_qm, kk * e_km, qs * e_qe, kk * e_kd
        mask = _intra_mask()
        dqm_slabs, dkm_slabs, dv_intra = [], [], []
        for p in range(B_HEADS // 2):
            qm_st = _pair_stack(qm, p)
            km_p = km[:, p * LANES:(p + 1) * LANES].astype(BF16)
            a = _dot_nt(qm_st, km_p)
            da_blocks, dqm_h = [], []
            for hh in range(2):
                h = 2 * p + hh
                vs = slice(h * B_VAL_DIM, (h + 1) * B_VAL_DIM)
                att = jnp.where(mask, a[hh * GLA_TILE:(hh + 1) * GLA_TILE], 0.0).astype(BF16)
                dv_intra.append(_dot_tn(att, do_ref[:, vs]))
                da = jnp.where(mask, _dot_nt(do_ref[:, vs], v_ref[:, vs]), 0.0).astype(BF16)
                da_blocks.append(da)
                dqm_h.append(_dot(da, km_p))
            lo = _lane_iota((GLA_TILE, LANES)) < B_KEY_DIM
            dqm_slabs.append(jnp.where(lo, dqm_h[0], dqm_h[1]))
            dkm_slabs.append(_dot_tn(jnp.concatenate(da_blocks, axis=0), qm_st))
        dqm = jnp.concatenate(dqm_slabs, axis=1)
        dkm = jnp.concatenate(dkm_slabs, axis=1)

        dqe_c, dkd_c, dv_inter, tail_c = ([None] * CHUNKS_PER_TILE for _ in range(4))
        for c in reversed(range(CHUNKS_PER_TILE)):
            rows = slice(c * B_CHUNK, (c + 1) * B_CHUNK)
            dst = dst_ref[...]
            dst_b = dst.astype(BF16)
            dv_inter[c] = _dot_nt(_stack_masked(kd[rows]), dst_b)
            dkd_c[c] = _select_heads(_dot(_rows_by_head(v_ref[rows, :]), dst_b))
            do_c = do_ref[rows, :]
            dqe_c[c] = _select_heads(_dot(_rows_by_head(do_c), st_all_ref[c].astype(BF16)))
            contrib = _select_state(_dot_tn(do_c, qe[rows].astype(BF16)))
            decay = jnp.exp(cum_ref[pl.ds(c * B_CHUNK + B_CHUNK - 1, 1), :])
            tail = (jnp.sum(kk[rows] * dkd_c[c] * e_kd[rows], axis=0, keepdims=True)
                    + decay * jnp.sum(st_all_ref[c] * dst, axis=0, keepdims=True))
            tail_c[c] = jnp.broadcast_to(tail, (B_CHUNK, B_KEY_WIDTH))
            dst_ref[...] = dst * decay + contrib
        dqe = jnp.concatenate(dqe_c, axis=0)
        dkd = jnp.concatenate(dkd_c, axis=0)
        dqs = dqm * e_qm + dqe * e_qe
        dk = dkm * e_km + dkd * e_kd
        dq_ref[...] = (dqs * scale).astype(BF16)
        dk_ref[...] = dk.astype(BF16)
        for h in range(B_HEADS):
            dvi = jnp.concatenate([dv_inter[c][h * B_CHUNK:(h + 1) * B_CHUNK] for c in range(CHUNKS_PER_TILE)], axis=0)
            dv_ref[:, h * B_VAL_DIM:(h + 1) * B_VAL_DIM] = (dv_intra[h] + dvi).astype(BF16)
        dd = qs * dqs - kk * dk
        i, j = _row_iota((GLA_TILE, GLA_TILE)), _lane_iota((GLA_TILE, GLA_TILE))
        upper = ((i // B_CHUNK == j // B_CHUNK) & (j >= i)).astype(BF16)
        hi, mid, lo3 = _split3(dd)
        dla_ref[...] = _dot(upper, hi) + _dot(upper, mid) + _dot(upper, lo3) + jnp.concatenate(tail_c, axis=0)

        @pl.when((pl.program_id(0) == B - 1) & (pl.program_id(1) == nt - 1))
        def _():
            for a in range(nw):
                for r in range(len(FLIPS)):
                    wcopy(a, r).wait()

    def rows(w):
        return pl.BlockSpec((GLA_TILE, w), lambda b, t: (b * nt + nt - 1 - t, 0))

    res = pl.pallas_call(
        body, name="gla_bwd", grid=(B, nt),
        in_specs=[rows(B_KEY_WIDTH), rows(B_KEY_WIDTH), rows(B_KEY_WIDTH), rows(B_WIDTH), rows(B_WIDTH),
                  pl.BlockSpec((CHUNKS_PER_TILE, B_VAL_DIM, B_KEY_WIDTH), lambda b, t: (b * nt + nt - 1 - t, 0, 0))]
                 + _any_specs(nw),
        out_specs=[rows(B_KEY_WIDTH), rows(B_KEY_WIDTH), rows(B_WIDTH), rows(B_KEY_WIDTH)] + _any_specs(nw),
        out_shape=[jax.ShapeDtypeStruct((T, B_KEY_WIDTH), BF16), jax.ShapeDtypeStruct((T, B_KEY_WIDTH), BF16),
                   jax.ShapeDtypeStruct((T, B_WIDTH), BF16), jax.ShapeDtypeStruct((T, B_KEY_WIDTH), F32)]
                  + [jax.ShapeDtypeStruct((len(FLIPS), *g.shape[1:]), g.dtype) for g in wgrads],
        scratch_shapes=[pltpu.VMEM((B_VAL_DIM, B_KEY_WIDTH), F32),
                        pltpu.SemaphoreType.DMA((nw, len(FLIPS))), pltpu.SemaphoreType.DMA((nw, len(FLIPS)))],
        compiler_params=_params("arbitrary", "arbitrary"),
    )(q, k, cum, vb, do, st_all, *wgrads)
    return res[:4], res[4:]


def _merge(x2, tgt2, attn, za, o_gla, zb, ga, gb, w_oa_sh, w_ob_sh, w_o, g_gla, g_final):
    T = x2.shape[0]
    tm = 256
    last = T // tm - 1

    def body(x_ref, tgt_ref, attn_ref, za_ref, og_ref, zb_ref, ga_ref, gb_ref,
             woa_sh_ref, wob_sh_ref, wo_ref, gg_ref, gf_ref,
             dxres_ref, dattn_ref, dog_ref, dza_ref, dzb_ref, dga_ref, dgb_ref,
             dwo_ref, dwoa_ref, dwob_ref, small_ref,
             awo_ref, awoa_ref, awob_ref, agf_ref, agg_ref, loss_ref, woa_ref, wob_ref):
        @pl.when(pl.program_id(0) == 0)
        def _():
            for r in (awo_ref, awoa_ref, awob_ref, agf_ref, agg_ref, loss_ref):
                r[...] = jnp.zeros_like(r)
            for j in range(N_DEV):
                woa_ref[:, j * SHARD_OUT:(j + 1) * SHARD_OUT] = woa_sh_ref[j]
                wob_ref[:, j * SHARD_OUT:(j + 1) * SHARD_OUT] = wob_sh_ref[j]

        za_v = za_ref[...]
        sig_za = _sigmoid(za_v)
        silu_a = za_v * sig_za
        attn_v = attn_ref[...]
        oa = (attn_v * silu_a).astype(BF16)
        ya = _dot(oa, woa_ref[...])
        og = og_ref[...]
        zb_v = zb_ref[...]
        sig_zb = _sigmoid(zb_v)
        silu_b = zb_v * sig_zb
        gg = gg_ref[...]
        on_parts, rinv_parts = [], []
        for h in range(B_HEADS):
            seg = og[:, h * B_VAL_DIM:(h + 1) * B_VAL_DIM]
            rinv = lax.rsqrt(jnp.mean(seg * seg, axis=-1, keepdims=True) + NORM_EPS)
            rinv_parts.append(rinv)
            on_parts.append(seg * rinv)
        on = jnp.concatenate(on_parts, axis=1)
        obn = on * gg
        ob = (obn * silu_b).astype(BF16)
        yb = _dot(ob, wob_ref[...])
        sig_a, sig_b = _sigmoid(ga_ref[...]), _sigmoid(gb_ref[...])
        merged = (sig_a * ya + sig_b * yb).astype(BF16)
        out = x_ref[...] + _dot(merged, wo_ref[...])
        rf = lax.rsqrt(jnp.mean(out * out, axis=-1, keepdims=True) + NORM_EPS)
        nrm = out * rf
        gf = gf_ref[...]
        err = nrm * gf - tgt_ref[...]
        loss_ref[...] += jnp.sum(err * err) * (0.5 / D_MODEL)

        dy = err * (1.0 / D_MODEL)
        agf_ref[...] += jnp.sum(dy * nrm, axis=0, keepdims=True)
        dn = dy * gf
        dout = rf * (dn - nrm * jnp.mean(dn * nrm, axis=-1, keepdims=True))
        dxres_ref[...] = dout
        dout_b = dout.astype(BF16)
        dmerged = _dot_nt(dout_b, wo_ref[...])
        awo_ref[...] += _dot_tn(merged, dout_b)
        dya = dmerged * sig_a
        dyb = dmerged * sig_b
        dga_ref[...] = (dmerged * ya * sig_a * (1.0 - sig_a)).astype(BF16)
        dgb_ref[...] = (dmerged * yb * sig_b * (1.0 - sig_b)).astype(BF16)
        dya_b, dyb_b = dya.astype(BF16), dyb.astype(BF16)
        awoa_ref[...] += _dot_tn(oa, dya_b)
        awob_ref[...] += _dot_tn(ob, dyb_b)
        doa = _dot_nt(dya_b, woa_ref[...])
        dattn_ref[...] = (doa * silu_a).astype(BF16)
        dza_ref[...] = (doa * attn_v * (sig_za * (1.0 + za_v * (1.0 - sig_za)))).astype(BF16)
        dob = _dot_nt(dyb_b, wob_ref[...])
        dzb_ref[...] = (dob * obn * (sig_zb * (1.0 + zb_v * (1.0 - sig_zb)))).astype(BF16)
        dobn = dob * silu_b
        agg_ref[...] += jnp.sum(dobn * on, axis=0, keepdims=True)
        don = dobn * gg
        for h in range(B_HEADS):
            sl = slice(h * B_VAL_DIM, (h + 1) * B_VAL_DIM)
            don_h, on_h = don[:, sl], on[:, sl]
            dog_ref[:, sl] = (rinv_parts[h] * (don_h - on_h * jnp.mean(don_h * on_h, axis=-1, keepdims=True))).astype(BF16)

        @pl.when(pl.program_id(0) == last)
        def _():
            for j in range(N_DEV):
                dwo_ref[j] = awo_ref[j * SHARD_OUT:(j + 1) * SHARD_OUT, :].astype(BF16)
                dwoa_ref[j] = awoa_ref[:, j * SHARD_OUT:(j + 1) * SHARD_OUT].astype(BF16)
                dwob_ref[j] = awob_ref[:, j * SHARD_OUT:(j + 1) * SHARD_OUT].astype(BF16)
            small_ref[...] = jnp.zeros_like(small_ref)
            _put_rows(small_ref, SMALL_G_FINAL, agf_ref[...])
            _put_rows(small_ref, SMALL_G_GLA, agg_ref[...])
            small_ref[SMALL_LOSS:SMALL_LOSS + 1, :] = loss_ref[...]

    def rows(w):
        return pl.BlockSpec((tm, w), lambda i: (i, 0))

    def whole(shape):
        nd = len(shape)
        return pl.BlockSpec(shape, lambda i: (0,) * nd)

    outs = [((T, D_MODEL), F32, rows(D_MODEL)), ((T, A_WIDTH), BF16, rows(A_WIDTH)), ((T, B_WIDTH), BF16, rows(B_WIDTH)),
            ((T, A_WIDTH), BF16, rows(A_WIDTH)), ((T, B_WIDTH), BF16, rows(B_WIDTH)),
            ((T, D_MODEL), BF16, rows(D_MODEL)), ((T, D_MODEL), BF16, rows(D_MODEL)),
            ((N_DEV, SHARD_OUT, D_MODEL), BF16, whole((N_DEV, SHARD_OUT, D_MODEL))),
            ((N_DEV, A_WIDTH, SHARD_OUT), BF16, whole((N_DEV, A_WIDTH, SHARD_OUT))),
            ((N_DEV, B_WIDTH, SHARD_OUT), BF16, whole((N_DEV, B_WIDTH, SHARD_OUT))),
            ((SMALL_SINKS, LANES), F32, whole((SMALL_SINKS, LANES)))]
    return pl.pallas_call(
        body, name="merge", grid=(T // tm,),
        in_specs=[rows(D_MODEL), rows(D_MODEL), rows(A_WIDTH), rows(A_WIDTH), rows(B_WIDTH), rows(B_WIDTH),
                  rows(D_MODEL), rows(D_MODEL),
                  _const_spec((N_DEV, A_WIDTH, SHARD_OUT)), _const_spec((N_DEV, B_WIDTH, SHARD_OUT)),
                  _const_spec((D_MODEL, D_MODEL)), _const_spec((1, B_WIDTH)), _const_spec((1, D_MODEL))],
        out_specs=[o[2] for o in outs],
        out_shape=[jax.ShapeDtypeStruct(o[0], o[1]) for o in outs],
        scratch_shapes=[pltpu.VMEM((D_MODEL, D_MODEL), F32), pltpu.VMEM((A_WIDTH, D_MODEL), F32),
                        pltpu.VMEM((B_WIDTH, D_MODEL), F32), pltpu.VMEM((1, D_MODEL), F32), pltpu.VMEM((1, B_WIDTH), F32),
                        pltpu.VMEM((1, LANES), F32), pltpu.VMEM((A_WIDTH, D_MODEL), BF16),
                        pltpu.VMEM((B_WIDTH, D_MODEL), BF16)],
        compiler_params=_params("arbitrary"),
    )(x2, tgt2, attn, za, o_gla, zb, ga, gb, w_oa_sh, w_ob_sh, w_o, g_gla, g_final)


def _in_proj_bwd(x2, dxres, cosf, sinf, g_in, wt_pad, wa_pad, parts):
    T = x2.shape[0]
    tm = 256
    last = T // tm - 1
    base = SMALL_G_IN

    def body(x_ref, dxres_ref, cos_ref, sin_ref, g_ref, wt_ref, wa_ref,
             dq_ref, dkv_ref, dza_ref, dqb_ref, dkb_ref, dvb_ref, dzb_ref, dla_ref, u_ref, alr_ref, dga_ref, dgb_ref,
             dx_ref, dsh_ref, small_ref, dproj_ref, agin_ref, aba_ref, awa_ref):
        @pl.when(pl.program_id(0) == 0)
        def _():
            for r in (agin_ref, aba_ref, awa_ref):
                r[...] = jnp.zeros_like(r)

        cos, nsin = cos_ref[...], -sin_ref[...]
        for s in range(A_WIDTH // LANES):
            sl = slice(s * LANES, (s + 1) * LANES)
            dproj_ref[:, sl] = _rope_slab(dq_ref[:, sl].astype(F32), cos, nsin).astype(BF16)
        dproj_ref[:, QKV_K:QKV_V] = _rope_slab(dkv_ref[:, 0:LANES].astype(F32), cos, nsin).astype(BF16)
        dproj_ref[:, QKV_V:QKV_W] = dkv_ref[:, LANES:]

        def put(name, val):
            a, b = SEG[name]
            dproj_ref[:, a:b] = val

        put("za", dza_ref[...])
        put("qb", dqb_ref[...])
        put("kb", dkb_ref[...])
        put("vb", dvb_ref[...])
        put("zb", dzb_ref[...])
        put("ga", dga_ref[...])
        put("gb", dgb_ref[...])
        du = dla_ref[...] * (1.0 / B_GATE_TEMP) * _sigmoid(-u_ref[...])
        aba_ref[...] += jnp.sum(du, axis=0, keepdims=True)
        du_b = du.astype(BF16)
        awa_ref[...] += _dot_tn(alr_ref[...], du_b)
        put("alr", _dot_nt(du_b, wa_ref[...]).astype(BF16))

        for j in range(N_DEV):
            col = (j % 2) * SHARD_PAD
            for a, b in _shard_pad_cols(j):
                dsh_ref[j // 2, :, col:col + b - a] = dproj_ref[:, a:b]
                col += b - a
            dsh_ref[j // 2, :, col:(j % 2 + 1) * SHARD_PAD] = jnp.zeros((tm, SHARD_PAD - SHARD_IN), BF16)

        dh = _dot(dproj_ref[...], wt_ref[...])
        x = x_ref[...]
        r = lax.rsqrt(jnp.mean(x * x, axis=-1, keepdims=True) + NORM_EPS)
        nrm = x * r
        agin_ref[...] += jnp.sum(dh * nrm, axis=0, keepdims=True)
        dn = dh * g_ref[...]
        dx_ref[...] = dxres_ref[...] + r * (dn - nrm * jnp.mean(dn * nrm, axis=-1, keepdims=True))

        @pl.when(pl.program_id(0) == last)
        def _():
            small_ref[...] = jnp.zeros_like(small_ref)
            _put_rows(small_ref, SMALL_G_IN - base, agin_ref[...])
            _put_rows(small_ref, SMALL_B_ALPHA - base, aba_ref[...])
            for half in range(B_KEY_WIDTH // LANES):
                r0 = SMALL_W_ALPHA - base + half * B_GATE_RANK
                small_ref[r0:r0 + B_GATE_RANK, :] = awa_ref[0:B_GATE_RANK, half * LANES:(half + 1) * LANES]

    def rows(w):
        return pl.BlockSpec((tm, w), lambda i: (i, 0))

    names = ["dq", "dkv", "dza", "dqb", "dkb", "dvb", "dzb", "dla", "u", "alr", "dga", "dgb"]
    return pl.pallas_call(
        body, name="in_proj_bwd", grid=(T // tm,),
        in_specs=[rows(D_MODEL), rows(D_MODEL), rows(LANES), rows(LANES), _const_spec((1, D_MODEL)),
                  _const_spec((D_IN_PAD, D_MODEL)), _const_spec((RANK_PAD, B_KEY_WIDTH))]
                 + [rows(parts[n].shape[1]) for n in names],
        out_specs=[rows(D_MODEL), pl.BlockSpec((N_CHIPS, tm, 2 * SHARD_PAD), lambda i: (0, i, 0)),
                   pl.BlockSpec((SMALL_ROWS - base, LANES), lambda i: (0, 0))],
        out_shape=[jax.ShapeDtypeStruct((T, D_MODEL), F32), jax.ShapeDtypeStruct((N_CHIPS, T, 2 * SHARD_PAD), BF16),
                   jax.ShapeDtypeStruct((SMALL_ROWS - base, LANES), F32)],
        scratch_shapes=[pltpu.VMEM((tm, D_IN_PAD), BF16), pltpu.VMEM((1, D_MODEL), F32), pltpu.VMEM((1, B_KEY_WIDTH), F32),
                        pltpu.VMEM((RANK_PAD, B_KEY_WIDTH), F32)],
        compiler_params=_params("arbitrary"),
    )(x2, dxres, cosf, sinf, g_in, wt_pad, wa_pad, *[parts[n] for n in names])


FLIPS = [(dx, dy, dc) for dx in (0, 1) for dy in (0, 1) for dc in (0, 1)][1:]


def _my_place():
    return lax.axis_index("x"), lax.axis_index("y"), lax.axis_index("c")


def _any_specs(n):
    return [pl.BlockSpec(memory_space=pl.ANY)] * n


def _gather_first(shards, pos_col):
    n = len(shards)
    T = pos_col.shape[0]
    rows_per_pass = math.gcd(T, 512)
    invf, sign = _rope_lane_constants()

    def body(*refs):
        ins, (pos_ref, invf_ref, sign_ref) = refs[:n], refs[n:n + 3]
        outs, (cos_ref, sin_ref) = refs[n + 3:2 * n + 3], refs[2 * n + 3:2 * n + 5]
        send_sems, recv_sems, local_sems = refs[2 * n + 5:]
        x, y, c = _my_place()
        me, sibling = (x, y, c), (x, y, 1 - c)
        chips = [(1 - x, y), (x, 1 - y), (1 - x, 1 - y)]

        def block(a, px, py, pc):
            return outs[a].at[4 * px + 2 * py + pc]

        def copy(a, k, blk, to, src=None):
            return pltpu.make_async_remote_copy(
                src_ref=block(a, *blk) if src is None else src, dst_ref=block(a, *blk),
                send_sem=send_sems.at[a, k], recv_sem=recv_sems.at[a, k], device_id=to, device_id_type=MESH)

        mine = [pltpu.make_async_copy(ins[a], block(a, *me), local_sems.at[a]) for a in range(n)]
        for cp in mine:
            cp.start()
        first = []
        for a in range(n):
            first.append(copy(a, 0, me, sibling, src=ins[a]))
            first += [copy(a, 1 + j, me, (*chip, c), src=ins[a]) for j, chip in enumerate(chips)]
        for cp in first:
            cp.start()

        def tables(i, carry):
            rows = pl.ds(pl.multiple_of(i * rows_per_pass, rows_per_pass), rows_per_pass)
            ang = pos_ref[rows, :].astype(F32) * invf_ref[...]
            cos_ref[rows, :] = jnp.cos(ang)
            sin_ref[rows, :] = jnp.sin(ang) * sign_ref[...]
            return carry

        lax.fori_loop(0, T // rows_per_pass, tables, 0)

        passed = []
        for j, chip in enumerate(chips):
            for a in range(n):
                copy(a, 1 + j, (*chip, c), me).wait_recv()
                fwd = copy(a, 4 + j, (*chip, c), sibling)
                fwd.start()
                passed.append(fwd)
        for a in range(n):
            copy(a, 0, sibling, me).wait_recv()
            for j, chip in enumerate(chips):
                copy(a, 4 + j, (*chip, 1 - c), me).wait_recv()
        for cp in first + passed:
            cp.wait_send()
        for cp in mine:
            cp.wait()

    vmem = pl.BlockSpec(memory_space=pltpu.VMEM)
    res = pl.pallas_call(
        body, name="gather_weights",
        in_specs=_any_specs(n) + [vmem] * 3, out_specs=_any_specs(n) + [vmem] * 2,
        out_shape=[jax.ShapeDtypeStruct((N_DEV, *s.shape), s.dtype) for s in shards]
                  + [jax.ShapeDtypeStruct((T, LANES), F32)] * 2,
        scratch_shapes=[pltpu.SemaphoreType.DMA((n, 7)), pltpu.SemaphoreType.DMA((n, 7)), pltpu.SemaphoreType.DMA((n,))],
        compiler_params=pltpu.CompilerParams(vmem_limit_bytes=V7X_VMEM_LIMIT),
    )(*shards, pos_col, invf, sign)
    return res[:n], res[n], res[n + 1]


def _w_in_grad_rs(h, dsh, chip_order, small):
    T = h.shape[0]
    tk = math.gcd(T, 2048)
    nk = T // tk
    chip_flips = [(1, 1), (1, 0), (0, 1)]
    n_steps = len(chip_flips) + 1

    def body(order_ref, h_ref, d_ref, s_ref, own_ref, recv_ref, sall_ref,
             acc_ref, stage_ref, send_sems, recv_sems, ssend_sems, srecv_sems, local_sem):
        i, kk = pl.program_id(0), pl.program_id(1)
        x, y, c = _my_place()
        my_dev = 4 * x + 2 * y + c

        def small_copy(r, slot):
            dx, dy, dc = FLIPS[r]
            return pltpu.make_async_remote_copy(
                src_ref=s_ref, dst_ref=sall_ref.at[slot], send_sem=ssend_sems.at[r], recv_sem=srecv_sems.at[r],
                device_id=(x ^ dx, y ^ dy, c ^ dc), device_id_type=MESH)

        keep_small = pltpu.make_async_copy(s_ref, sall_ref.at[my_dev], local_sem)

        def shard_copy(slot, dx, dy, dc):
            r = FLIPS.index((dx, dy, dc))
            return pltpu.make_async_remote_copy(
                src_ref=stage_ref.at[slot, c ^ dc], dst_ref=recv_ref.at[r], send_sem=send_sems.at[r],
                recv_sem=recv_sems.at[r], device_id=(x ^ dx, y ^ dy, c ^ dc), device_id_type=MESH)

        def stage(slot):
            stage_ref[slot, 0] = acc_ref[0:SHARD_PAD, :].astype(BF16)
            stage_ref[slot, 1] = acc_ref[SHARD_PAD:2 * SHARD_PAD, :].astype(BF16)

        @pl.when((i == 0) & (kk == 0))
        def _():
            keep_small.start()
            for r in range(len(FLIPS)):
                small_copy(r, my_dev).start()

        @pl.when(kk == 0)
        def _():
            acc_ref[...] = jnp.zeros_like(acc_ref)

        acc_ref[...] += _dot_tn(d_ref[...], h_ref[...])

        for t, (dx, dy) in enumerate(chip_flips):
            @pl.when((i == t) & (kk == nk - 1))
            def _(t=t, dx=dx, dy=dy):
                if t >= 2:
                    for dc in (0, 1):
                        shard_copy(t % 2, *chip_flips[t - 2], dc).wait_send()
                stage(t % 2)
                for dc in (0, 1):
                    shard_copy(t % 2, dx, dy, dc).start()

        @pl.when((i == n_steps - 1) & (kk == nk - 1))
        def _():
            for dc in (0, 1):
                shard_copy(1, *chip_flips[1], dc).wait_send()
            stage(1)
            shard_copy(1, 0, 0, 1).start()

            @pl.when(c == 0)
            def _():
                own_ref[...] = acc_ref[0:SHARD_PAD, :]

            @pl.when(c == 1)
            def _():
                own_ref[...] = acc_ref[SHARD_PAD:2 * SHARD_PAD, :]

            for dc in (0, 1):
                shard_copy(0, *chip_flips[2], dc).wait_send()
            shard_copy(1, 0, 0, 1).wait_send()
            for r, (dx, dy, dc) in enumerate(FLIPS):
                shard_copy(0, dx, dy, dc).wait_recv()
                small_copy(r, 4 * (x ^ dx) + 2 * (y ^ dy) + (c ^ dc)).wait_recv()
                small_copy(r, my_dev).wait_send()
            keep_small.wait()

    return pl.pallas_call(
        body, name="w_in_grad_rs",
        grid_spec=pltpu.PrefetchScalarGridSpec(
            num_scalar_prefetch=1, grid=(n_steps, nk),
            in_specs=[pl.BlockSpec((tk, D_MODEL), lambda i, kk, order: (kk, 0)),
                      pl.BlockSpec((None, tk, 2 * SHARD_PAD), lambda i, kk, order: (order[i], kk, 0)),
                      pl.BlockSpec(memory_space=pl.ANY)],
            out_specs=[pl.BlockSpec((SHARD_PAD, D_MODEL), lambda i, kk, order: (0, 0)),
                       pl.BlockSpec(memory_space=pl.ANY), pl.BlockSpec(memory_space=pl.ANY)],
            scratch_shapes=[pltpu.VMEM((2 * SHARD_PAD, D_MODEL), F32), pltpu.VMEM((2, 2, SHARD_PAD, D_MODEL), BF16),
                            pltpu.SemaphoreType.DMA((7,)), pltpu.SemaphoreType.DMA((7,)),
                            pltpu.SemaphoreType.DMA((7,)), pltpu.SemaphoreType.DMA((7,)), pltpu.SemaphoreType.DMA]),
        out_shape=[jax.ShapeDtypeStruct((SHARD_PAD, D_MODEL), F32),
                   jax.ShapeDtypeStruct((len(FLIPS), SHARD_PAD, D_MODEL), BF16),
                   jax.ShapeDtypeStruct((N_DEV, *small.shape), F32)],
        compiler_params=_params("arbitrary", "arbitrary"),
    )(chip_order, h, dsh, small)


def _adam_math(w, g, m, v):
    m_new = ADAM_B1 * m + (1.0 - ADAM_B1) * g
    v_new = ADAM_B2 * v + (1.0 - ADAM_B2) * (g * g)
    m_hat = m_new / (1.0 - ADAM_B1 ** ADAM_STEP)
    v_hat = v_new / (1.0 - ADAM_B2 ** ADAM_STEP)
    delta = -ADAM_LR * (m_hat / (jnp.sqrt(v_hat) + ADAM_EPS) + ADAM_WD * w)
    return delta, m_new, v_new


def _adam_big(name, own, own_idx, recv, w, m, v):
    rw, cw = w.shape
    rp = own.shape[1]
    steps = 8
    by_cols = rp != rw
    blk_w = (rw, cw // steps) if by_cols else (rw // steps, cw)
    blk_g = (rp, cw // steps) if by_cols else (rw // steps, cw)
    at = (lambda i: (0, i)) if by_cols else (lambda i: (i, 0))

    def body(idx_ref, o_ref, r_ref, w_ref, m_ref, v_ref, g_ref, d_ref, mo_ref, vo_ref):
        g = o_ref[...].astype(F32)
        for r in range(len(FLIPS)):
            g = g + r_ref[r].astype(F32)
        g = g[0:blk_w[0], :]
        g_ref[...] = g
        d_ref[...], mo_ref[...], vo_ref[...] = _adam_math(w_ref[...], g, m_ref[...], v_ref[...])

    spec = pl.BlockSpec(blk_w, lambda i, idx_ref: at(i))
    return pl.pallas_call(
        body, name=name,
        grid_spec=pltpu.PrefetchScalarGridSpec(
            num_scalar_prefetch=1, grid=(steps,),
            in_specs=[pl.BlockSpec((None, *blk_g), lambda i, idx_ref: (idx_ref[0], *at(i))),
                      pl.BlockSpec((len(FLIPS), *blk_g), lambda i, idx_ref: (0, *at(i))), spec, spec, spec],
            out_specs=[spec] * 4),
        out_shape=[jax.ShapeDtypeStruct((rw, cw), F32)] * 4,
        compiler_params=_params("parallel"),
    )(own_idx, own, recv, w, m, v)


def _adam_small(small_all, params):
    flat = [a for triple in params for a in triple]
    n_par = len(params)

    def body(s_ref, *refs):
        ins, outs, loss_ref = refs[:3 * n_par], refs[3 * n_par:-1], refs[-1]
        g_slab = s_ref[0]
        for dev in range(1, N_DEV):
            g_slab = g_slab + s_ref[dev]
        loss_ref[...] = g_slab[SMALL_LOSS:SMALL_LOSS + 1, :]
        dev = 4 * lax.axis_index("x") + 2 * lax.axis_index("y") + lax.axis_index("c")
        alpha_full = jnp.concatenate([g_slab[SMALL_W_ALPHA + half * B_GATE_RANK:SMALL_W_ALPHA + (half + 1) * B_GATE_RANK]
                                      for half in range(B_KEY_WIDTH // LANES)], axis=1)
        alpha_mine = pltpu.roll(alpha_full, (B_KEY_WIDTH - dev * SHARD_ALPHA) % B_KEY_WIDTH, 1)[:, 0:SHARD_ALPHA]
        grads = [_take_rows(g_slab, SMALL_G_IN, D_MODEL // LANES), _take_rows(g_slab, SMALL_G_FINAL, D_MODEL // LANES),
                 _take_rows(g_slab, SMALL_G_GLA, B_WIDTH // LANES), _take_rows(g_slab, SMALL_B_ALPHA, B_KEY_WIDTH // LANES),
                 g_slab[SMALL_SINKS:SMALL_SINKS + 1, 0:A_HEADS], alpha_mine]
        for i, g in enumerate(grads):
            w_ref, m_ref, v_ref = ins[3 * i:3 * i + 3]
            delta, m_new, v_new = _adam_math(w_ref[...], g, m_ref[...], v_ref[...])
            outs[4 * i][...] = g
            outs[4 * i + 1][...] = delta
            outs[4 * i + 2][...] = m_new
            outs[4 * i + 3][...] = v_new

    res = pl.pallas_call(
        body, name="adam_small",
        out_shape=[jax.ShapeDtypeStruct(t[0].shape, F32) for t in params for _ in range(4)]
                  + [jax.ShapeDtypeStruct((1, LANES), F32)],
    )(small_all, *flat)
    return [res[4 * i:4 * i + 4] for i in range(n_par)], res[-1]


def _local_step(x, cosf, sinf, loss_target, g_in, wt_sh, wa_pad, b_alpha, sinks, g_gla, out_shards, g_final, chip_order):
    B, S, _ = x.shape
    T = B * S
    x2 = x.reshape(T, D_MODEL)
    tgt2 = loss_target.reshape(T, D_MODEL)
    f, (g_woa, g_wob, g_wo) = _in_proj(x2, cosf, sinf, g_in, wt_sh, wa_pad, b_alpha, out_shards)
    w_o = g_wo.reshape(D_MODEL, D_MODEL)
    sink_row = jnp.repeat(sinks, WINDOW).reshape(1, ATT_ROWS)
    sink_col = sink_row.reshape(ATT_ROWS, 1)
    attn, lse = _attn_fwd(f["qkv"], sink_row, B, S)
    o_gla, st_all = _gla_fwd(f["q"], f["k"], f["cum"], f["vb"], B, S)
    (dxres, dattn, dog, dza, dzb, dga, dgb, dw_o, dw_oa, dw_ob, small_a) = _merge(
        x2, tgt2, attn, f["za"], o_gla, f["zb"], f["ga"], f["gb"], g_woa, g_wob, w_o, g_gla, g_final)
    dq, dkv, dsink = _attn_bwd(f["qkv"], dattn, attn, lse, sink_col, B, S)
    (dqb, dkb, dvb, dla), (rv_o, rv_oa, rv_ob) = _gla_bwd(f["q"], f["k"], f["cum"], f["vb"], dog, st_all, B, S,
                                                        [dw_o, dw_oa, dw_ob])
    parts = dict(dq=dq, dkv=dkv, dza=dza, dqb=dqb, dkb=dkb, dvb=dvb, dzb=dzb, dla=dla, u=f["u"], alr=f["alr"],
                 dga=dga, dgb=dgb)
    dx, dsh, small_c = _in_proj_bwd(x2, dxres, cosf, sinf, g_in, f["wt_pad"], wa_pad, parts)
    small = jnp.concatenate([small_a, dsink, small_c], axis=0)
    own_in, rv_in, small_all = _w_in_grad_rs(f["h"], dsh, chip_order, small)
    return dict(grad_x=dx.reshape(B, S, D_MODEL), own_in=own_in, rv_in=rv_in,
                own_o=dw_o, rv_o=rv_o, own_oa=dw_oa, rv_oa=rv_oa, own_ob=dw_ob, rv_ob=rv_ob, small_all=small_all)


def kernel(x, positions, g_in, w_in, w_alpha_up, b_alpha, attn_sinks, g_gla_norm, w_out_a, w_out_b, w_o, g_final, loss_target, m_g_in, m_w_in, m_w_alpha_up, m_b_alpha, m_attn_sinks, m_g_gla_norm, m_w_out_a, m_w_out_b, m_w_o, m_g_final, v_g_in, v_w_in, v_w_alpha_up, v_b_alpha, v_attn_sinks, v_g_gla_norm, v_w_out_a, v_w_out_b, v_w_o, v_g_final):
    xi, yi, ci = _my_place()
    dev_idx = (4 * xi + 2 * yi + ci).reshape(1).astype(jnp.int32)
    chip = 2 * xi + yi
    chip_order = jnp.stack([chip ^ 3, chip ^ 2, chip ^ 1, chip]).astype(jnp.int32)

    (g_win, g_wa), cosf, sinf = _gather_first(
        [jnp.pad(w_in[0].T.astype(BF16), ((0, SHARD_PAD - SHARD_IN), (0, 0))), w_alpha_up[0].astype(BF16)],
        positions.reshape(-1, 1))
    wt_sh = g_win.reshape(N_DEV * SHARD_PAD, D_MODEL)
    wa_pad = jnp.pad(jnp.concatenate([g_wa[j] for j in range(N_DEV)], axis=1), ((0, RANK_PAD - B_GATE_RANK), (0, 0)))

    r = _local_step(x, cosf, sinf, loss_target, g_in, wt_sh, wa_pad, b_alpha, attn_sinks[0], g_gla_norm,
                    [w_out_a[0].astype(BF16), w_out_b[0].astype(BF16), w_o[0].astype(BF16)],
                    g_final.reshape(1, D_MODEL), chip_order)

    first = jnp.zeros((1,), jnp.int32)
    big = [_adam_big("adam_w_in", r["own_in"][None], first, r["rv_in"], w_in[0].T, m_w_in[0].T, v_w_in[0].T),
           _adam_big("adam_w_out_a", r["own_oa"], dev_idx, r["rv_oa"], w_out_a[0], m_w_out_a[0], v_w_out_a[0]),
           _adam_big("adam_w_out_b", r["own_ob"], dev_idx, r["rv_ob"], w_out_b[0], m_w_out_b[0], v_w_out_b[0]),
           _adam_big("adam_w_o", r["own_o"], dev_idx, r["rv_o"], w_o[0], m_w_o[0], v_w_o[0])]
    row = lambda a: a.reshape(1, D_MODEL)
    big[0] = [a.T for a in big[0]]
    (s_in, s_final, s_gla, s_ba, s_sinks, s_wa), loss_row = _adam_small(r["small_all"], [
        (g_in, m_g_in, v_g_in), (row(g_final), row(m_g_final), row(v_g_final)),
        (g_gla_norm, m_g_gla_norm, v_g_gla_norm), (b_alpha, m_b_alpha, v_b_alpha),
        (attn_sinks, m_attn_sinks, v_attn_sinks), (w_alpha_up[0], m_w_alpha_up[0], v_w_alpha_up[0])])

    def group(i):
        return (s_in[i], big[0][i][None], s_wa[i][None], s_ba[i], s_sinks[i], s_gla[i], big[1][i][None], big[2][i][None],
                big[3][i][None], s_final[i].reshape(D_MODEL))

    return (loss_row[0, 0], r["grad_x"], *group(0), *group(1), *group(2), *group(3))
```

```python
import functools
import math

import numpy as np
import jax
import jax.numpy as jnp
from jax import lax
from jax.experimental import pallas as pl
from jax.experimental.pallas import tpu as pltpu

F32 = jnp.float32
BF16 = jnp.bfloat16
MESH = pl.DeviceIdType.MESH

D_MODEL = 1024
A_HEADS, A_KV_HEADS, A_HEAD_DIM = 8, 2, 64
A_WIDTH, A_KV_WIDTH = 512, 128
WINDOW = 128
ROPE_THETA = 500000.0
ROPE_DIM = 16
B_HEADS, B_KEY_DIM, B_VAL_DIM = 4, 64, 128
B_KEY_WIDTH, B_WIDTH = 256, 512
B_GATE_RANK = 16
B_GATE_TEMP = 16.0
B_CHUNK = 64
NORM_EPS = 1e-6
NEG_BIG = -1e30
D_IN = 4880
N_DEV = 8
N_CHIPS = 4
ADAM_LR, ADAM_B1, ADAM_B2, ADAM_EPS, ADAM_WD, ADAM_STEP = 0.001, 0.9, 0.999, 1e-08, 0.01, 10

LANES = 128
V7X_VMEM_LIMIT = 56 * 1024 * 1024

RANK_PAD = LANES
SEG = {}
_off = 0
for _name, _w in (("qa", 512), ("ka", 128), ("va", 128), ("za", 512), ("qb", 256), ("kb", 256),
                  ("vb", 512), ("zb", 512), ("alr", RANK_PAD), ("ga", 1024), ("gb", 1024)):
    SEG[_name] = (_off, _off + _w)
    _off += _w
D_IN_PAD = _off
ALR_SRC = SEG["alr"][0]
QKV_K, QKV_V, QKV_W = SEG["ka"][0], SEG["va"][0], SEG["va"][1]

SHARD_IN = D_IN // N_DEV
SHARD_PAD = 640
SHARD_OUT = D_MODEL // N_DEV
SHARD_ALPHA = B_KEY_WIDTH // N_DEV

SMALL_G_FINAL, SMALL_G_GLA, SMALL_LOSS, SMALL_SINKS, SMALL_G_IN, SMALL_B_ALPHA, SMALL_W_ALPHA = 0, 8, 12, 16, 24, 32, 40
SMALL_ROWS = 72


def _dot(a, b):
    return jnp.dot(a, b, preferred_element_type=F32)


def _dot_nt(a, b):
    return lax.dot_general(a, b, (((1,), (1,)), ((), ())), preferred_element_type=F32)


def _dot_tn(a, b):
    return lax.dot_general(a, b, (((0,), (0,)), ((), ())), preferred_element_type=F32)


def _sigmoid(z):
    return 1.0 / (1.0 + jnp.exp(-z))


def _params(*sem):
    return pltpu.CompilerParams(dimension_semantics=sem, vmem_limit_bytes=V7X_VMEM_LIMIT)


def _const_spec(shape):
    nd = len(shape)
    return pl.BlockSpec(shape, lambda *_: (0,) * nd, pipeline_mode=pl.Buffered(1))


def _lane_iota(shape):
    return lax.broadcasted_iota(jnp.int32, shape, 1)


def _row_iota(shape):
    return lax.broadcasted_iota(jnp.int32, shape, 0)


def _split3(v):
    hi = v.astype(BF16)
    r1 = v - hi.astype(F32)
    mid = r1.astype(BF16)
    lo = (r1 - mid.astype(F32)).astype(BF16)
    return hi, mid, lo


def _put_rows(ref, row0, vec):
    for r in range(vec.shape[1] // LANES):
        ref[row0 + r:row0 + r + 1, :] = vec[:, r * LANES:(r + 1) * LANES]


def _take_rows(slab, row0, n):
    return jnp.concatenate([slab[row0 + r:row0 + r + 1, :] for r in range(n)], axis=1)


def _rope_lane_constants():
    half = ROPE_DIM // 2
    inv_freq = np.exp(-math.log(ROPE_THETA) * np.arange(half, dtype=np.float32) * np.float32(2.0 / ROPE_DIM)).astype(np.float32)
    lane = np.arange(LANES)
    j = lane % A_HEAD_DIM
    invf = np.where(j < ROPE_DIM, inv_freq[j % half], 0.0).astype(np.float32)
    sign = np.where(j < half, -1.0, np.where(j < ROPE_DIM, 1.0, 0.0)).astype(np.float32)
    return jnp.asarray(invf)[None, :], jnp.asarray(sign)[None, :]


def _rope_slab(t, cos, sin_signed):
    first = (_lane_iota(t.shape) % A_HEAD_DIM) < (ROPE_DIM // 2)
    partner = jnp.where(first, pltpu.roll(t, LANES - ROPE_DIM // 2, 1), pltpu.roll(t, ROPE_DIM // 2, 1))
    return t * cos + partner * sin_signed


def _shard_pad_cols(j):
    cut = ALR_SRC + B_GATE_RANK
    shift = RANK_PAD - B_GATE_RANK
    a, b = j * SHARD_IN, (j + 1) * SHARD_IN
    if b <= cut:
        return [(a, b)]
    if a >= cut:
        return [(a + shift, b + shift)]
    return [(a, cut), (cut + shift, b + shift)]


def _in_proj(x2, cosf, sinf, g_in, wt_sh, wa_pad, b_alpha, later_shards):
    T = x2.shape[0]
    tm = math.gcd(T, 512)
    last = T // tm - 1
    nl = len(later_shards)

    def body(x_ref, cos_ref, sin_ref, g_ref, wsh_ref, wa_ref, ba_ref, *rest):
        sh_refs, rest = rest[:nl], rest[nl:]
        (h_ref, qkv_ref, za_ref, q_ref, k_ref, vb_ref, zb_ref, alr_ref, u_ref, cum_ref, ga_ref, gb_ref, wt_out) = rest[:13]
        all_refs, (wt_ref, send_sems, recv_sems, local_sems, wt_sem) = rest[13:13 + nl], rest[13 + nl:]
        wt_copy = pltpu.make_async_copy(wt_ref, wt_out, wt_sem)
        px, py, pc = _my_place()
        my_dev = 4 * px + 2 * py + pc

        def wcopy(a, r, slot):
            dx, dy, dc = FLIPS[r]
            return pltpu.make_async_remote_copy(
                src_ref=sh_refs[a], dst_ref=all_refs[a].at[slot], send_sem=send_sems.at[a, r],
                recv_sem=recv_sems.at[a, r], device_id=(px ^ dx, py ^ dy, pc ^ dc), device_id_type=MESH)

        keep = [pltpu.make_async_copy(sh_refs[a], all_refs[a].at[my_dev], local_sems.at[a]) for a in range(nl)]

        @pl.when(pl.program_id(0) == 0)
        def _():
            for a in range(nl):
                keep[a].start()
                for r in range(len(FLIPS)):
                    wcopy(a, r, my_dev).start()

        @pl.when(pl.program_id(0) == 0)
        def _():
            for j in range(N_DEV):
                src = j * SHARD_PAD
                for a, b in _shard_pad_cols(j):
                    wt_ref[a:b, :] = wsh_ref[src:src + b - a, :]
                    src += b - a
            a, b = SEG["alr"]
            wt_ref[a + B_GATE_RANK:b, :] = jnp.zeros((RANK_PAD - B_GATE_RANK, D_MODEL), BF16)
            wt_copy.start()

        x = x_ref[...]
        r = lax.rsqrt(jnp.mean(x * x, axis=-1, keepdims=True) + NORM_EPS)
        h = (x * r * g_ref[...]).astype(BF16)
        h_ref[...] = h

        def seg(name):
            a, b = SEG[name]
            return _dot_nt(h, wt_ref[a:b, :])

        cos, sin = cos_ref[...], sin_ref[...]
        qa = seg("qa")
        for s in range(A_WIDTH // LANES):
            qkv_ref[:, s * LANES:(s + 1) * LANES] = _rope_slab(qa[:, s * LANES:(s + 1) * LANES], cos, sin).astype(BF16)
        qkv_ref[:, QKV_K:QKV_V] = _rope_slab(seg("ka"), cos, sin).astype(BF16)
        qkv_ref[:, QKV_V:QKV_W] = seg("va").astype(BF16)
        za_ref[...] = seg("za")
        q_ref[...] = seg("qb")
        k_ref[...] = seg("kb")
        vb_ref[...] = seg("vb").astype(BF16)
        zb_ref[...] = seg("zb")
        ga_ref[...] = seg("ga")
        gb_ref[...] = seg("gb")
        alr = seg("alr").astype(BF16)
        alr_ref[...] = alr
        u = _dot(alr, wa_ref[...]) + ba_ref[...]
        u_ref[...] = u
        log_a = (jnp.minimum(u, 0.0) - jnp.log(1.0 + jnp.exp(-jnp.abs(u)))) * (1.0 / B_GATE_TEMP)
        row, col = _row_iota((tm, tm)), _lane_iota((tm, tm))
        tri = ((row // B_CHUNK == col // B_CHUNK) & (col <= row)).astype(BF16)
        hi, mid, lo = _split3(log_a)
        cum_ref[...] = _dot(tri, hi) + _dot(tri, mid) + _dot(tri, lo)

        @pl.when(pl.program_id(0) == last)
        def _():
            for a in range(nl):
                for r, (dx, dy, dc) in enumerate(FLIPS):
                    wcopy(a, r, 4 * (px ^ dx) + 2 * (py ^ dy) + (pc ^ dc)).wait_recv()
                    wcopy(a, r, my_dev).wait_send()
                keep[a].wait()
            wt_copy.wait()

    def rows(w):
        return pl.BlockSpec((tm, w), lambda i: (i, 0))

    outs = [("h", D_MODEL, BF16), ("qkv", QKV_W, BF16), ("za", A_WIDTH, F32), ("q", B_KEY_WIDTH, F32),
            ("k", B_KEY_WIDTH, F32), ("vb", B_WIDTH, BF16), ("zb", B_WIDTH, F32), ("alr", RANK_PAD, BF16),
            ("u", B_KEY_WIDTH, F32), ("cum", B_KEY_WIDTH, F32), ("ga", D_MODEL, F32), ("gb", D_MODEL, F32)]
    res = pl.pallas_call(
        body, name="in_proj", grid=(T // tm,),
        in_specs=[rows(D_MODEL), rows(LANES), rows(LANES), _const_spec((1, D_MODEL)),
                  _const_spec((N_DEV * SHARD_PAD, D_MODEL)), _const_spec((RANK_PAD, B_KEY_WIDTH)),
                  _const_spec((1, B_KEY_WIDTH))] + _any_specs(nl),
        out_specs=[rows(w) for _, w, _ in outs] + _any_specs(1 + nl),
        out_shape=[jax.ShapeDtypeStruct((T, w), dt) for _, w, dt in outs]
                  + [jax.ShapeDtypeStruct((D_IN_PAD, D_MODEL), BF16)]
                  + [jax.ShapeDtypeStruct((N_DEV, *sh.shape), sh.dtype) for sh in later_shards],
        scratch_shapes=[pltpu.VMEM((D_IN_PAD, D_MODEL), BF16),
                        pltpu.SemaphoreType.DMA((nl, len(FLIPS))), pltpu.SemaphoreType.DMA((nl, len(FLIPS))),
                        pltpu.SemaphoreType.DMA((nl,)), pltpu.SemaphoreType.DMA],
        compiler_params=_params("arbitrary"),
    )(x2, cosf, sinf, g_in, wt_sh, wa_pad, b_alpha, *later_shards)
    n_out = len(outs) + 1
    return dict(zip([n for n, _, _ in outs] + ["wt_pad"], res[:n_out])), res[n_out:]


def _dup_kv_head(t, g):
    tf = t.astype(F32)
    keep = (_lane_iota(tf.shape) < A_HEAD_DIM) == (g == 0)
    return jnp.where(keep, tf, pltpu.roll(tf, A_HEAD_DIM, 1)).astype(BF16)


def _stack_heads(t):
    lo = _lane_iota(t.shape) < A_HEAD_DIM
    zero = jnp.zeros_like(t)
    return jnp.concatenate([jnp.where(lo, t, zero), jnp.where(lo, zero, t)], axis=0)


ATT_ROWS = A_HEADS * WINDOW
GROUP_ROWS = ATT_ROWS // A_KV_HEADS
HEADS_PER_GROUP = A_HEADS // A_KV_HEADS


def _band_mask_t(n):
    kj = _row_iota((2 * WINDOW, GROUP_ROWS)) - WINDOW
    qi = _lane_iota((2 * WINDOW, GROUP_ROWS)) % WINDOW
    return (kj <= qi) & (qi - kj < WINDOW) & ((n > 0) | (kj >= 0))


def _stacked_queries(ref, g):
    pairs = range(g * HEADS_PER_GROUP // 2, (g + 1) * HEADS_PER_GROUP // 2)
    return jnp.concatenate([_stack_heads(ref[:, p * LANES:(p + 1) * LANES]) for p in pairs], axis=0)


def _unstack_heads(t, g, ref, dtype):
    lo = _lane_iota((WINDOW, LANES)) < A_HEAD_DIM
    for hh in range(HEADS_PER_GROUP // 2):
        p = g * HEADS_PER_GROUP // 2 + hh
        ref[:, p * LANES:(p + 1) * LANES] = jnp.where(lo, t[2 * hh * WINDOW:(2 * hh + 1) * WINDOW],
                                                       t[(2 * hh + 1) * WINDOW:(2 * hh + 2) * WINDOW]).astype(dtype)


def _attn_fwd(qkv, sink_row, B, S):
    T = B * S
    nb = S // WINDOW
    scale = A_HEAD_DIM ** -0.5

    def body(sink_ref, q_ref, kc_ref, vc_ref, kp_ref, vp_ref, o_ref, lse_ref):
        valid = _band_mask_t(pl.program_id(1))
        k = jnp.concatenate([kp_ref[...], kc_ref[...]], axis=0)
        v = jnp.concatenate([vp_ref[...], vc_ref[...]], axis=0)
        lse_rows = []
        for g in range(A_KV_HEADS):
            kd, vd = _dup_kv_head(k, g), _dup_kv_head(v, g)
            s = jnp.where(valid, _dot_nt(kd, _stacked_queries(q_ref, g)) * scale, NEG_BIG)
            sink = sink_ref[:, g * GROUP_ROWS:(g + 1) * GROUP_ROWS]
            m = jnp.maximum(jnp.max(s, axis=0, keepdims=True), sink)
            e = jnp.exp(s - m)
            den = jnp.sum(e, axis=0, keepdims=True) + jnp.exp(sink - m)
            o = _dot_tn((e * (1.0 / den)).astype(BF16), vd)
            _unstack_heads(o, g, o_ref, F32)
            lse = m + jnp.log(den)
            lse_rows += [lse[:, j * WINDOW:(j + 1) * WINDOW] for j in range(HEADS_PER_GROUP)]
        by_head = jnp.concatenate(lse_rows + [jnp.zeros((WINDOW - A_HEADS, WINDOW), F32)], axis=0)
        lse_ref[...] = by_head.T

    def cur(col, w):
        return pl.BlockSpec((WINDOW, w), lambda b, n: (b * nb + n, col))

    def prev(col):
        return pl.BlockSpec((WINDOW, LANES), lambda b, n: (b * nb + jnp.maximum(n - 1, 0), col))

    kcol, vcol = QKV_K // LANES, QKV_V // LANES
    return pl.pallas_call(
        body, name="attn_fwd", grid=(B, nb),
        in_specs=[_const_spec((1, ATT_ROWS)), cur(0, A_WIDTH), cur(kcol, LANES), cur(vcol, LANES), prev(kcol), prev(vcol)],
        out_specs=[cur(0, A_WIDTH), cur(0, LANES)],
        out_shape=[jax.ShapeDtypeStruct((T, A_WIDTH), F32), jax.ShapeDtypeStruct((T, LANES), F32)],
        compiler_params=_params("parallel", "parallel"),
    )(sink_row, qkv, qkv, qkv, qkv, qkv)


ATT_CHUNK = 64


def _chunk_masks(n):
    masks = []
    for half in range(WINDOW // ATT_CHUNK):
        qi = _row_iota((ATT_CHUNK, 2 * WINDOW)) + half * ATT_CHUNK
        kj = _lane_iota((ATT_CHUNK, 2 * WINDOW)) - WINDOW
        masks.append((kj <= qi) & (qi - kj < WINDOW) & ((n > 0) | (kj >= 0)))
    return masks


def _all_stacked_queries(ref):
    return jnp.concatenate([_stacked_queries(ref, g) for g in range(A_KV_HEADS)], axis=0)


def _by_group(fn, lhs, rhs_per_group):
    return jnp.concatenate([fn(lhs[g * GROUP_ROWS:(g + 1) * GROUP_ROWS], rhs_per_group[g])
                            for g in range(A_KV_HEADS)], axis=0)


def _attn_bwd(qkv, do, out, lse, sink_col, B, S):
    T = B * S
    nb = S // WINDOW
    scale = A_HEAD_DIM ** -0.5

    def body(sink_ref, q_ref, kc_ref, vc_ref, kp_ref, vp_ref, do_ref, out_ref, lse_ref,
             dq_ref, dkv_ref, dsink_ref, carry_ref, s_ref, dp_ref, ds_ref, p_ref):
        b, n = pl.program_id(0), pl.program_id(1)
        active = n < nb
        masks = _chunk_masks(jnp.minimum(n, nb - 1))

        @pl.when((b == 0) & (n == 0))
        def _():
            dsink_ref[...] = jnp.zeros_like(dsink_ref)

        k = jnp.concatenate([kp_ref[...], kc_ref[...]], axis=0)
        v = jnp.concatenate([vp_ref[...], vc_ref[...]], axis=0)
        kd = [_dup_kv_head(k, g) for g in range(A_KV_HEADS)]
        vd = [_dup_kv_head(v, g) for g in range(A_KV_HEADS)]
        qs = _all_stacked_queries(q_ref)
        dos = _all_stacked_queries(do_ref)
        s_ref[...] = _by_group(_dot_nt, qs, kd)
        dp_ref[...] = _by_group(_dot_nt, dos, vd)
        lane = _lane_iota((ATT_CHUNK, LANES))
        lo = lane < A_HEAD_DIM
        lane1 = _lane_iota((1, LANES))
        dsink_row = jnp.zeros((1, LANES), F32)
        for c in range(ATT_ROWS // ATT_CHUNK):
            rows = slice(c * ATT_CHUNK, (c + 1) * ATT_CHUNK)
            head, half = divmod(c, WINDOW // ATT_CHUNK)
            qrows = slice(half * ATT_CHUNK, (half + 1) * ATT_CHUNK)
            slab = slice((head // 2) * LANES, (head // 2 + 1) * LANES)
            lse_col = jnp.sum(jnp.where(lane == head, lse_ref[qrows, :], 0.0), axis=-1, keepdims=True)
            prod = do_ref[qrows, slab].astype(F32) * out_ref[qrows, slab]
            mine = lo if head % 2 == 0 else jnp.logical_not(lo)
            delta = jnp.sum(jnp.where(mine, prod, 0.0), axis=-1, keepdims=True)
            s = jnp.where(masks[half], s_ref[rows, :] * scale, NEG_BIG)
            prob = jnp.exp(s - lse_col)
            p_ref[rows, :] = prob.astype(BF16)
            ds_ref[rows, :] = (prob * (dp_ref[rows, :] - delta) * scale).astype(BF16)
            w = -jnp.exp(sink_ref[rows, :] - lse_col) * delta
            dsink_row += jnp.where(lane1 == head, jnp.sum(w, axis=0, keepdims=True), 0.0)
        dq = _by_group(_dot, ds_ref[...], kd)
        lo_q = _lane_iota((WINDOW, LANES)) < A_HEAD_DIM
        for p in range(A_HEADS // 2):
            dq_ref[:, p * LANES:(p + 1) * LANES] = jnp.where(
                lo_q, dq[2 * p * WINDOW:(2 * p + 1) * WINDOW], dq[(2 * p + 1) * WINDOW:(2 * p + 2) * WINDOW]).astype(BF16)
        lane2 = _lane_iota((2 * WINDOW, LANES))
        dk_tot = jnp.zeros((2 * WINDOW, LANES), F32)
        dv_tot = jnp.zeros((2 * WINDOW, LANES), F32)
        for g in range(A_KV_HEADS):
            grows = slice(g * GROUP_ROWS, (g + 1) * GROUP_ROWS)
            dk_acc = _dot_tn(ds_ref[grows, :], qs[grows])
            dv_acc = _dot_tn(p_ref[grows, :], dos[grows])
            mine = (lane2 < A_HEAD_DIM) == (g == 0)
            dk_tot = jnp.where(mine, dk_acc + pltpu.roll(dk_acc, A_HEAD_DIM, 1), dk_tot)
            dv_tot = jnp.where(mine, dv_acc + pltpu.roll(dv_acc, A_HEAD_DIM, 1), dv_tot)
        gate = jnp.where(active, 1.0, 0.0)
        dsink_ref[0:1, :] += dsink_row * gate
        dkv_ref[:, 0:LANES] = (carry_ref[:, 0:LANES] + dk_tot[:WINDOW] * gate).astype(BF16)
        dkv_ref[:, LANES:] = (carry_ref[:, LANES:] + dv_tot[:WINDOW] * gate).astype(BF16)
        carry_ref[:, 0:LANES] = dk_tot[WINDOW:]
        carry_ref[:, LANES:] = dv_tot[WINDOW:]

    def cur(col, w):
        return pl.BlockSpec((WINDOW, w), lambda b, n: (b * nb + jnp.minimum(n, nb - 1), col))

    def prev(col):
        return pl.BlockSpec((WINDOW, LANES), lambda b, n: (b * nb + jnp.maximum(jnp.minimum(n, nb - 1) - 1, 0), col))

    lag = pl.BlockSpec((WINDOW, 2 * LANES), lambda b, n: (b * nb + jnp.maximum(n - 1, 0), 0))
    kcol, vcol = QKV_K // LANES, QKV_V // LANES
    scores = (ATT_ROWS, 2 * WINDOW)
    return pl.pallas_call(
        body, name="attn_bwd", grid=(B, nb + 1),
        in_specs=[_const_spec((ATT_ROWS, 1)), cur(0, A_WIDTH), cur(kcol, LANES), cur(vcol, LANES),
                  prev(kcol), prev(vcol), cur(0, A_WIDTH), cur(0, A_WIDTH), cur(0, LANES)],
        out_specs=[cur(0, A_WIDTH), lag, pl.BlockSpec((8, LANES), lambda b, n: (0, 0))],
        out_shape=[jax.ShapeDtypeStruct((T, A_WIDTH), BF16), jax.ShapeDtypeStruct((T, 2 * LANES), BF16),
                   jax.ShapeDtypeStruct((8, LANES), F32)],
        scratch_shapes=[pltpu.VMEM((WINDOW, 2 * LANES), F32), pltpu.VMEM(scores, F32), pltpu.VMEM(scores, F32),
                        pltpu.VMEM(scores, BF16), pltpu.VMEM(scores, BF16)],
        compiler_params=_params("arbitrary", "arbitrary"),
    )(sink_col, qkv, qkv, qkv, qkv, qkv, do, out, lse)


GLA_TILE = 256
CHUNKS_PER_TILE = GLA_TILE // B_CHUNK


def _gla_factors(q_ref, k_ref, cum_ref):
    scale = B_KEY_DIM ** -0.5
    cum = cum_ref[...]
    shape = (B_CHUNK, B_KEY_WIDTH)
    last = jnp.concatenate([jnp.broadcast_to(cum_ref[pl.ds(c * B_CHUNK + B_CHUNK - 1, 1), :], shape)
                            for c in range(CHUNKS_PER_TILE)], axis=0)
    mid = jnp.concatenate([jnp.broadcast_to(cum_ref[pl.ds(c * B_CHUNK + B_CHUNK // 2 - 1, 1), :], shape)
                           for c in range(CHUNKS_PER_TILE)], axis=0)
    e_qm, e_km, e_qe, e_kd = jnp.exp(cum - mid), jnp.exp(mid - cum), jnp.exp(cum), jnp.exp(last - cum)
    qs = q_ref[...] * scale
    k = k_ref[...]
    return qs, k, (e_qm, e_km, e_qe, e_kd)


def _head_mask(shape, h):
    return (_lane_iota(shape) // B_KEY_DIM) == h


def _stack_masked(t):
    return jnp.concatenate([jnp.where(_head_mask(t.shape, h), t, 0.0) for h in range(B_HEADS)], axis=0).astype(BF16)


def _select_heads(t):
    shape = (B_CHUNK, B_KEY_WIDTH)
    out = jnp.zeros(shape, F32)
    for h in range(B_HEADS):
        out = jnp.where(_head_mask(shape, h), t[h * B_CHUNK:(h + 1) * B_CHUNK], out)
    return out


def _select_state(t):
    shape = (B_VAL_DIM, B_KEY_WIDTH)
    out = jnp.zeros(shape, F32)
    for h in range(B_HEADS):
        out = jnp.where(_head_mask(shape, h), t[h * B_VAL_DIM:(h + 1) * B_VAL_DIM], out)
    return out


def _rows_by_head(t):
    return jnp.concatenate([t[:, h * B_VAL_DIM:(h + 1) * B_VAL_DIM] for h in range(B_HEADS)], axis=0)


def _intra_mask():
    i, j = _row_iota((GLA_TILE, GLA_TILE)), _lane_iota((GLA_TILE, GLA_TILE))
    return (i // B_CHUNK == j // B_CHUNK) & (j <= i)


def _pair_stack(t, p):
    slab = t[:, p * LANES:(p + 1) * LANES]
    lo = _lane_iota(slab.shape) < B_KEY_DIM
    return jnp.concatenate([jnp.where(lo, slab, 0.0), jnp.where(lo, 0.0, slab)], axis=0).astype(BF16)


def _gla_fwd(q, k, cum, vb, B, S):
    T = B * S
    nt = S // GLA_TILE

    def body(q_ref, k_ref, cum_ref, v_ref, o_ref, st_all_ref, st_ref):
        @pl.when(pl.program_id(1) == 0)
        def _():
            st_ref[...] = jnp.zeros_like(st_ref)

        qs, kk, (e_qm, e_km, e_qe, e_kd) = _gla_factors(q_ref, k_ref, cum_ref)
        qm, km, qe, kd = qs * e_qm, kk * e_km, qs * e_qe, (kk * e_kd).astype(BF16)
        mask = _intra_mask()
        intra = []
        for p in range(B_HEADS // 2):
            a = _dot_nt(_pair_stack(qm, p), km[:, p * LANES:(p + 1) * LANES].astype(BF16))
            for hh in range(2):
                h = 2 * p + hh
                att = jnp.where(mask, a[hh * GLA_TILE:(hh + 1) * GLA_TILE], 0.0).astype(BF16)
                intra.append(_dot(att, v_ref[:, h * B_VAL_DIM:(h + 1) * B_VAL_DIM]))
        inter = []
        for c in range(CHUNKS_PER_TILE):
            rows = slice(c * B_CHUNK, (c + 1) * B_CHUNK)
            st = st_ref[...]
            st_all_ref[c] = st
            inter.append(_dot_nt(_stack_masked(qe[rows]), st.astype(BF16)))
            inc = _select_state(_dot_tn(v_ref[rows, :], kd[rows]))
            decay = jnp.exp(cum_ref[pl.ds(c * B_CHUNK + B_CHUNK - 1, 1), :])
            st_ref[...] = st * decay + inc
        for h in range(B_HEADS):
            oi = jnp.concatenate([inter[c][h * B_CHUNK:(h + 1) * B_CHUNK] for c in range(CHUNKS_PER_TILE)], axis=0)
            o_ref[:, h * B_VAL_DIM:(h + 1) * B_VAL_DIM] = intra[h] + oi

    def rows(w):
        return pl.BlockSpec((GLA_TILE, w), lambda b, t: (b * nt + t, 0))

    return pl.pallas_call(
        body, name="gla_fwd", grid=(B, nt),
        in_specs=[rows(B_KEY_WIDTH), rows(B_KEY_WIDTH), rows(B_KEY_WIDTH), rows(B_WIDTH)],
        out_specs=[rows(B_WIDTH),
                   pl.BlockSpec((CHUNKS_PER_TILE, B_VAL_DIM, B_KEY_WIDTH), lambda b, t: (b * nt + t, 0, 0))],
        out_shape=[jax.ShapeDtypeStruct((T, B_WIDTH), F32),
                   jax.ShapeDtypeStruct((T // B_CHUNK, B_VAL_DIM, B_KEY_WIDTH), F32)],
        scratch_shapes=[pltpu.VMEM((B_VAL_DIM, B_KEY_WIDTH), F32)],
        compiler_params=_params("arbitrary", "arbitrary"),
    )(q, k, cum, vb)


def _gla_bwd(q, k, cum, vb, do, st_all, B, S, wgrads):
    T = B * S
    nt = S // GLA_TILE
    scale = B_KEY_DIM ** -0.5
    nw = len(wgrads)

    def body(q_ref, k_ref, cum_ref, v_ref, do_ref, st_all_ref, *rest):
        g_refs, (dq_ref, dk_ref, dv_ref, dla_ref) = rest[:nw], rest[nw:nw + 4]
        rv_refs, (dst_ref, send_sems, recv_sems) = rest[nw + 4:2 * nw + 4], rest[2 * nw + 4:]
        x, y, c = _my_place()

        def wcopy(a, r):
            dx, dy, dc = FLIPS[r]
            return pltpu.make_async_remote_copy(
                src_ref=g_refs[a].at[4 * (x ^ dx) + 2 * (y ^ dy) + (c ^ dc)], dst_ref=rv_refs[a].at[r],
                send_sem=send_sems.at[a, r], recv_sem=recv_sems.at[a, r],
                device_id=(x ^ dx, y ^ dy, c ^ dc), device_id_type=MESH)

        @pl.when((pl.program_id(0) == 0) & (pl.program_id(1) == 0))
        def _():
            for a in range(nw):
                for r in range(len(FLIPS)):
                    wcopy(a, r).start()

        @pl.when(pl.program_id(1) == 0)
        def _():
            dst_ref[...] = jnp.zeros_like(dst_ref)

        qs, kk, (e_qm, e_km, e_qe, e_kd) = _gla_factors(q_ref, k_ref, cum_ref)
        qm, km, qe, kd = qs * e_qm, kk * e_km, qs * e_qe, kk * e_kd
        mask = _intra_mask()
        dqm_slabs, dkm_slabs, dv_intra = [], [], []
        for p in range(B_HEADS // 2):
            qm_st = _pair_stack(qm, p)
            km_p = km[:, p * LANES:(p + 1) * LANES].astype(BF16)
            a = _dot_nt(qm_st, km_p)
            da_blocks, dqm_h = [], []
            for hh in range(2):
                h = 2 * p + hh
                vs = slice(h * B_VAL_DIM, (h + 1) * B_VAL_DIM)
                att = jnp.where(mask, a[hh * GLA_TILE:(hh + 1) * GLA_TILE], 0.0).astype(BF16)
                dv_intra.append(_dot_tn(att, do_ref[:, vs]))
                da = jnp.where(mask, _dot_nt(do_ref[:, vs], v_ref[:, vs]), 0.0).astype(BF16)
                da_blocks.append(da)
                dqm_h.append(_dot(da, km_p))
            lo = _lane_iota((GLA_TILE, LANES)) < B_KEY_DIM
            dqm_slabs.append(jnp.where(lo, dqm_h[0], dqm_h[1]))
            dkm_slabs.append(_dot_tn(jnp.concatenate(da_blocks, axis=0), qm_st))
        dqm = jnp.concatenate(dqm_slabs, axis=1)
        dkm = jnp.concatenate(dkm_slabs, axis=1)

        dqe_c, dkd_c, dv_inter, tail_c = ([None] * CHUNKS_PER_TILE for _ in range(4))
        for c in reversed(range(CHUNKS_PER_TILE)):
            rows = slice(c * B_CHUNK, (c + 1) * B_CHUNK)
            dst = dst_ref[...]
            dst_b = dst.astype(BF16)
            dv_inter[c] = _dot_nt(_stack_masked(kd[rows]), dst_b)
            dkd_c[c] = _select_heads(_dot(_rows_by_head(v_ref[rows, :]), dst_b))
            do_c = do_ref[rows, :]
            dqe_c[c] = _select_heads(_dot(_rows_by_head(do_c), st_all_ref[c].astype(BF16)))
            contrib = _select_state(_dot_tn(do_c, qe[rows].astype(BF16)))
            decay = jnp.exp(cum_ref[pl.ds(c * B_CHUNK + B_CHUNK - 1, 1), :])
            tail = (jnp.sum(kk[rows] * dkd_c[c] * e_kd[rows], axis=0, keepdims=True)
                    + decay * jnp.sum(st_all_ref[c] * dst, axis=0, keepdims=True))
            tail_c[c] = jnp.broadcast_to(tail, (B_CHUNK, B_KEY_WIDTH))
            dst_ref[...] = dst * decay + contrib
        dqe = jnp.concatenate(dqe_c, axis=0)
        dkd = jnp.concatenate(dkd_c, axis=0)
        dqs = dqm * e_qm + dqe * e_qe
        dk = dkm * e_km + dkd * e_kd
        dq_ref[...] = (dqs * scale).astype(BF16)
        dk_ref[...] = dk.astype(BF16)
        for h in range(B_HEADS):
            dvi = jnp.concatenate([dv_inter[c][h * B_CHUNK:(h + 1) * B_CHUNK] for c in range(CHUNKS_PER_TILE)], axis=0)
            dv_ref[:, h * B_VAL_DIM:(h + 1) * B_VAL_DIM] = (dv_intra[h] + dvi).astype(BF16)
        dd = qs * dqs - kk * dk
        i, j = _row_iota((GLA_TILE, GLA_TILE)), _lane_iota((GLA_TILE, GLA_TILE))
        upper = ((i // B_CHUNK == j // B_CHUNK) & (j >= i)).astype(BF16)
        hi, mid, lo3 = _split3(dd)
        dla_ref[...] = _dot(upper, hi) + _dot(upper, mid) + _dot(upper, lo3) + jnp.concatenate(tail_c, axis=0)

        @pl.when((pl.program_id(0) == B - 1) & (pl.program_id(1) == nt - 1))
        def _():
            for a in range(nw):
                for r in range(len(FLIPS)):
                    wcopy(a, r).wait()

    def rows(w):
        return pl.BlockSpec((GLA_TILE, w), lambda b, t: (b * nt + nt - 1 - t, 0))

    res = pl.pallas_call(
        body, name="gla_bwd", grid=(B, nt),
        in_specs=[rows(B_KEY_WIDTH), rows(B_KEY_WIDTH), rows(B_KEY_WIDTH), rows(B_WIDTH), rows(B_WIDTH),
                  pl.BlockSpec((CHUNKS_PER_TILE, B_VAL_DIM, B_KEY_WIDTH), lambda b, t: (b * nt + nt - 1 - t, 0, 0))]
                 + _any_specs(nw),
        out_specs=[rows(B_KEY_WIDTH), rows(B_KEY_WIDTH), rows(B_WIDTH), rows(B_KEY_WIDTH)] + _any_specs(nw),
        out_shape=[jax.ShapeDtypeStruct((T, B_KEY_WIDTH), BF16), jax.ShapeDtypeStruct((T, B_KEY_WIDTH), BF16),
                   jax.ShapeDtypeStruct((T, B_WIDTH), BF16), jax.ShapeDtypeStruct((T, B_KEY_WIDTH), F32)]
                  + [jax.ShapeDtypeStruct((len(FLIPS), *g.shape[1:]), g.dtype) for g in wgrads],
        scratch_shapes=[pltpu.VMEM((B_VAL_DIM, B_KEY_WIDTH), F32),
                        pltpu.SemaphoreType.DMA((nw, len(FLIPS))), pltpu.SemaphoreType.DMA((nw, len(FLIPS)))],
        compiler_params=_params("arbitrary", "arbitrary"),
    )(q, k, cum, vb, do, st_all, *wgrads)
    return res[:4], res[4:]


def _merge(x2, tgt2, attn, za, o_gla, zb, ga, gb, w_oa_sh, w_ob_sh, w_o, g_gla, g_final):
    T = x2.shape[0]
    tm = 256
    last = T // tm - 1

    def body(x_ref, tgt_ref, attn_ref, za_ref, og_ref, zb_ref, ga_ref, gb_ref,
             woa_sh_ref, wob_sh_ref, wo_ref, gg_ref, gf_ref,
             dxres_ref, dattn_ref, dog_ref, dza_ref, dzb_ref, dga_ref, dgb_ref,
             dwo_ref, dwoa_ref, dwob_ref, small_ref,
             awo_ref, awoa_ref, awob_ref, agf_ref, agg_ref, loss_ref, woa_ref, wob_ref):
        @pl.when(pl.program_id(0) == 0)
        def _():
            for r in (awo_ref, awoa_ref, awob_ref, agf_ref, agg_ref, loss_ref):
                r[...] = jnp.zeros_like(r)
            for j in range(N_DEV):
                woa_ref[:, j * SHARD_OUT:(j + 1) * SHARD_OUT] = woa_sh_ref[j]
                wob_ref[:, j * SHARD_OUT:(j + 1) * SHARD_OUT] = wob_sh_ref[j]

        za_v = za_ref[...]
        sig_za = _sigmoid(za_v)
        silu_a = za_v * sig_za
        attn_v = attn_ref[...]
        oa = (attn_v * silu_a).astype(BF16)
        ya = _dot(oa, woa_ref[...])
        og = og_ref[...]
        zb_v = zb_ref[...]
        sig_zb = _sigmoid(zb_v)
        silu_b = zb_v * sig_zb
        gg = gg_ref[...]
        on_parts, rinv_parts = [], []
        for h in range(B_HEADS):
            seg = og[:, h * B_VAL_DIM:(h + 1) * B_VAL_DIM]
            rinv = lax.rsqrt(jnp.mean(seg * seg, axis=-1, keepdims=True) + NORM_EPS)
            rinv_parts.append(rinv)
            on_parts.append(seg * rinv)
        on = jnp.concatenate(on_parts, axis=1)
        obn = on * gg
        ob = (obn * silu_b).astype(BF16)
        yb = _dot(ob, wob_ref[...])
        sig_a, sig_b = _sigmoid(ga_ref[...]), _sigmoid(gb_ref[...])
        merged = (sig_a * ya + sig_b * yb).astype(BF16)
        out = x_ref[...] + _dot(merged, wo_ref[...])
        rf = lax.rsqrt(jnp.mean(out * out, axis=-1, keepdims=True) + NORM_EPS)
        nrm = out * rf
        gf = gf_ref[...]
        err = nrm * gf - tgt_ref[...]
        loss_ref[...] += jnp.sum(err * err) * (0.5 / D_MODEL)

        dy = err * (1.0 / D_MODEL)
        agf_ref[...] += jnp.sum(dy * nrm, axis=0, keepdims=True)
        dn = dy * gf
        dout = rf * (dn - nrm * jnp.mean(dn * nrm, axis=-1, keepdims=True))
        dxres_ref[...] = dout
        dout_b = dout.astype(BF16)
        dmerged = _dot_nt(dout_b, wo_ref[...])
        awo_ref[...] += _dot_tn(merged, dout_b)
        dya = dmerged * sig_a
        dyb = dmerged * sig_b
        dga_ref[...] = (dmerged * ya * sig_a * (1.0 - sig_a)).astype(BF16)
        dgb_ref[...] = (dmerged * yb * sig_b * (1.0 - sig_b)).astype(BF16)
        dya_b, dyb_b = dya.astype(BF16), dyb.astype(BF16)
        awoa_ref[...] += _dot_tn(oa, dya_b)
        awob_ref[...] += _dot_tn(ob, dyb_b)
        doa = _dot_nt(dya_b, woa_ref[...])
        dattn_ref[...] = (doa * silu_a).astype(BF16)
        dza_ref[...] = (doa * attn_v * (sig_za * (1.0 + za_v * (1.0 - sig_za)))).astype(BF16)
        dob = _dot_nt(dyb_b, wob_ref[...])
        dzb_ref[...] = (dob * obn * (sig_zb * (1.0 + zb_v * (1.0 - sig_zb)))).astype(BF16)
        dobn = dob * silu_b
        agg_ref[...] += jnp.sum(dobn * on, axis=0, keepdims=True)
        don = dobn * gg
        for h in range(B_HEADS):
            sl = slice(h * B_VAL_DIM, (h + 1) * B_VAL_DIM)
            don_h, on_h = don[:, sl], on[:, sl]
            dog_ref[:, sl] = (rinv_parts[h] * (don_h - on_h * jnp.mean(don_h * on_h, axis=-1, keepdims=True))).astype(BF16)

        @pl.when(pl.program_id(0) == last)
        def _():
            for j in range(N_DEV):
                dwo_ref[j] = awo_ref[j * SHARD_OUT:(j + 1) * SHARD_OUT, :].astype(BF16)
                dwoa_ref[j] = awoa_ref[:, j * SHARD_OUT:(j + 1) * SHARD_OUT].astype(BF16)
                dwob_ref[j] = awob_ref[:, j * SHARD_OUT:(j + 1) * SHARD_OUT].astype(BF16)
            small_ref[...] = jnp.zeros_like(small_ref)
            _put_rows(small_ref, SMALL_G_FINAL, agf_ref[...])
            _put_rows(small_ref, SMALL_G_GLA, agg_ref[...])
            small_ref[SMALL_LOSS:SMALL_LOSS + 1, :] = loss_ref[...]

    def rows(w):
        return pl.BlockSpec((tm, w), lambda i: (i, 0))

    def whole(shape):
        nd = len(shape)
        return pl.BlockSpec(shape, lambda i: (0,) * nd)

    outs = [((T, D_MODEL), F32, rows(D_MODEL)), ((T, A_WIDTH), BF16, rows(A_WIDTH)), ((T, B_WIDTH), BF16, rows(B_WIDTH)),
            ((T, A_WIDTH), BF16, rows(A_WIDTH)), ((T, B_WIDTH), BF16, rows(B_WIDTH)),
            ((T, D_MODEL), BF16, rows(D_MODEL)), ((T, D_MODEL), BF16, rows(D_MODEL)),
            ((N_DEV, SHARD_OUT, D_MODEL), BF16, whole((N_DEV, SHARD_OUT, D_MODEL))),
            ((N_DEV, A_WIDTH, SHARD_OUT), BF16, whole((N_DEV, A_WIDTH, SHARD_OUT))),
            ((N_DEV, B_WIDTH, SHARD_OUT), BF16, whole((N_DEV, B_WIDTH, SHARD_OUT))),
            ((SMALL_SINKS, LANES), F32, whole((SMALL_SINKS, LANES)))]
    return pl.pallas_call(
        body, name="merge", grid=(T // tm,),
        in_specs=[rows(D_MODEL), rows(D_MODEL), rows(A_WIDTH), rows(A_WIDTH), rows(B_WIDTH), rows(B_WIDTH),
                  rows(D_MODEL), rows(D_MODEL),
                  _const_spec((N_DEV, A_WIDTH, SHARD_OUT)), _const_spec((N_DEV, B_WIDTH, SHARD_OUT)),
                  _const_spec((D_MODEL, D_MODEL)), _const_spec((1, B_WIDTH)), _const_spec((1, D_MODEL))],
        out_specs=[o[2] for o in outs],
        out_shape=[jax.ShapeDtypeStruct(o[0], o[1]) for o in outs],
        scratch_shapes=[pltpu.VMEM((D_MODEL, D_MODEL), F32), pltpu.VMEM((A_WIDTH, D_MODEL), F32),
                        pltpu.VMEM((B_WIDTH, D_MODEL), F32), pltpu.VMEM((1, D_MODEL), F32), pltpu.VMEM((1, B_WIDTH), F32),
                        pltpu.VMEM((1, LANES), F32), pltpu.VMEM((A_WIDTH, D_MODEL), BF16),
                        pltpu.VMEM((B_WIDTH, D_MODEL), BF16)],
        compiler_params=_params("arbitrary"),
    )(x2, tgt2, attn, za, o_gla, zb, ga, gb, w_oa_sh, w_ob_sh, w_o, g_gla, g_final)


def _in_proj_bwd(x2, dxres, cosf, sinf, g_in, wt_pad, wa_pad, parts):
    T = x2.shape[0]
    tm = 256
    last = T // tm - 1
    base = SMALL_G_IN

    def body(x_ref, dxres_ref, cos_ref, sin_ref, g_ref, wt_ref, wa_ref,
             dq_ref, dkv_ref, dza_ref, dqb_ref, dkb_ref, dvb_ref, dzb_ref, dla_ref, u_ref, alr_ref, dga_ref, dgb_ref,
             dx_ref, dsh_ref, small_ref, dproj_ref, agin_ref, aba_ref, awa_ref):
        @pl.when(pl.program_id(0) == 0)
        def _():
            for r in (agin_ref, aba_ref, awa_ref):
                r[...] = jnp.zeros_like(r)

        cos, nsin = cos_ref[...], -sin_ref[...]
        for s in range(A_WIDTH // LANES):
            sl = slice(s * LANES, (s + 1) * LANES)
            dproj_ref[:, sl] = _rope_slab(dq_ref[:, sl].astype(F32), cos, nsin).astype(BF16)
        dproj_ref[:, QKV_K:QKV_V] = _rope_slab(dkv_ref[:, 0:LANES].astype(F32), cos, nsin).astype(BF16)
        dproj_ref[:, QKV_V:QKV_W] = dkv_ref[:, LANES:]

        def put(name, val):
            a, b = SEG[name]
            dproj_ref[:, a:b] = val

        put("za", dza_ref[...])
        put("qb", dqb_ref[...])
        put("kb", dkb_ref[...])
        put("vb", dvb_ref[...])
        put("zb", dzb_ref[...])
        put("ga", dga_ref[...])
        put("gb", dgb_ref[...])
        du = dla_ref[...] * (1.0 / B_GATE_TEMP) * _sigmoid(-u_ref[...])
        aba_ref[...] += jnp.sum(du, axis=0, keepdims=True)
        du_b = du.astype(BF16)
        awa_ref[...] += _dot_tn(alr_ref[...], du_b)
        put("alr", _dot_nt(du_b, wa_ref[...]).astype(BF16))

        for j in range(N_DEV):
            col = (j % 2) * SHARD_PAD
            for a, b in _shard_pad_cols(j):
                dsh_ref[j // 2, :, col:col + b - a] = dproj_ref[:, a:b]
                col += b - a
            dsh_ref[j // 2, :, col:(j % 2 + 1) * SHARD_PAD] = jnp.zeros((tm, SHARD_PAD - SHARD_IN), BF16)

        dh = _dot(dproj_ref[...], wt_ref[...])
        x = x_ref[...]
        r = lax.rsqrt(jnp.mean(x * x, axis=-1, keepdims=True) + NORM_EPS)
        nrm = x * r
        agin_ref[...] += jnp.sum(dh * nrm, axis=0, keepdims=True)
        dn = dh * g_ref[...]
        dx_ref[...] = dxres_ref[...] + r * (dn - nrm * jnp.mean(dn * nrm, axis=-1, keepdims=True))

        @pl.when(pl.program_id(0) == last)
        def _():
            small_ref[...] = jnp.zeros_like(small_ref)
            _put_rows(small_ref, SMALL_G_IN - base, agin_ref[...])
            _put_rows(small_ref, SMALL_B_ALPHA - base, aba_ref[...])
            for half in range(B_KEY_WIDTH // LANES):
                r0 = SMALL_W_ALPHA - base + half * B_GATE_RANK
                small_ref[r0:r0 + B_GATE_RANK, :] = awa_ref[0:B_GATE_RANK, half * LANES:(half + 1) * LANES]

    def rows(w):
        return pl.BlockSpec((tm, w), lambda i: (i, 0))

    names = ["dq", "dkv", "dza", "dqb", "dkb", "dvb", "dzb", "dla", "u", "alr", "dga", "dgb"]
    return pl.pallas_call(
        body, name="in_proj_bwd", grid=(T // tm,),
        in_specs=[rows(D_MODEL), rows(D_MODEL), rows(LANES), rows(LANES), _const_spec((1, D_MODEL)),
                  _const_spec((D_IN_PAD, D_MODEL)), _const_spec((RANK_PAD, B_KEY_WIDTH))]
                 + [rows(parts[n].shape[1]) for n in names],
        out_specs=[rows(D_MODEL), pl.BlockSpec((N_CHIPS, tm, 2 * SHARD_PAD), lambda i: (0, i, 0)),
                   pl.BlockSpec((SMALL_ROWS - base, LANES), lambda i: (0, 0))],
        out_shape=[jax.ShapeDtypeStruct((T, D_MODEL), F32), jax.ShapeDtypeStruct((N_CHIPS, T, 2 * SHARD_PAD), BF16),
                   jax.ShapeDtypeStruct((SMALL_ROWS - base, LANES), F32)],
        scratch_shapes=[pltpu.VMEM((tm, D_IN_PAD), BF16), pltpu.VMEM((1, D_MODEL), F32), pltpu.VMEM((1, B_KEY_WIDTH), F32),
                        pltpu.VMEM((RANK_PAD, B_KEY_WIDTH), F32)],
        compiler_params=_params("arbitrary"),
    )(x2, dxres, cosf, sinf, g_in, wt_pad, wa_pad, *[parts[n] for n in names])


FLIPS = [(dx, dy, dc) for dx in (0, 1) for dy in (0, 1) for dc in (0, 1)][1:]


def _my_place():
    return lax.axis_index("x"), lax.axis_index("y"), lax.axis_index("c")


def _any_specs(n):
    return [pl.BlockSpec(memory_space=pl.ANY)] * n


def _gather_first(shards, pos_col):
    n = len(shards)
    T = pos_col.shape[0]
    rows_per_pass = math.gcd(T, 512)
    invf, sign = _rope_lane_constants()

    def body(*refs):
        ins, (pos_ref, invf_ref, sign_ref) = refs[:n], refs[n:n + 3]
        outs, (cos_ref, sin_ref) = refs[n + 3:2 * n + 3], refs[2 * n + 3:2 * n + 5]
        send_sems, recv_sems, local_sems = refs[2 * n + 5:]
        x, y, c = _my_place()
        me, sibling = (x, y, c), (x, y, 1 - c)
        chips = [(1 - x, y), (x, 1 - y), (1 - x, 1 - y)]

        def block(a, px, py, pc):
            return outs[a].at[4 * px + 2 * py + pc]

        def copy(a, k, blk, to, src=None):
            return pltpu.make_async_remote_copy(
                src_ref=block(a, *blk) if src is None else src, dst_ref=block(a, *blk),
                send_sem=send_sems.at[a, k], recv_sem=recv_sems.at[a, k], device_id=to, device_id_type=MESH)

        mine = [pltpu.make_async_copy(ins[a], block(a, *me), local_sems.at[a]) for a in range(n)]
        for cp in mine:
            cp.start()
        first = []
        for a in range(n):
            first.append(copy(a, 0, me, sibling, src=ins[a]))
            first += [copy(a, 1 + j, me, (*chip, c), src=ins[a]) for j, chip in enumerate(chips)]
        for cp in first:
            cp.start()

        def tables(i, carry):
            rows = pl.ds(pl.multiple_of(i * rows_per_pass, rows_per_pass), rows_per_pass)
            ang = pos_ref[rows, :].astype(F32) * invf_ref[...]
            cos_ref[rows, :] = jnp.cos(ang)
            sin_ref[rows, :] = jnp.sin(ang) * sign_ref[...]
            return carry

        lax.fori_loop(0, T // rows_per_pass, tables, 0)

        passed = []
        for j, chip in enumerate(chips):
            for a in range(n):
                copy(a, 1 + j, (*chip, c), me).wait_recv()
                fwd = copy(a, 4 + j, (*chip, c), sibling)
                fwd.start()
                passed.append(fwd)
        for a in range(n):
            copy(a, 0, sibling, me).wait_recv()
            for j, chip in enumerate(chips):
                copy(a, 4 + j, (*chip, 1 - c), me).wait_recv()
        for cp in first + passed:
            cp.wait_send()
        for cp in mine:
            cp.wait()

    vmem = pl.BlockSpec(memory_space=pltpu.VMEM)
    res = pl.pallas_call(
        body, name="gather_weights",
        in_specs=_any_specs(n) + [vmem] * 3, out_specs=_any_specs(n) + [vmem] * 2,
        out_shape=[jax.ShapeDtypeStruct((N_DEV, *s.shape), s.dtype) for s in shards]
                  + [jax.ShapeDtypeStruct((T, LANES), F32)] * 2,
        scratch_shapes=[pltpu.SemaphoreType.DMA((n, 7)), pltpu.SemaphoreType.DMA((n, 7)), pltpu.SemaphoreType.DMA((n,))],
        compiler_params=pltpu.CompilerParams(vmem_limit_bytes=V7X_VMEM_LIMIT),
    )(*shards, pos_col, invf, sign)
    return res[:n], res[n], res[n + 1]


def _w_in_grad_rs(h, dsh, chip_order, small):
    T = h.shape[0]
    tk = math.gcd(T, 2048)
    nk = T // tk
    chip_flips = [(1, 1), (1, 0), (0, 1)]
    n_steps = len(chip_flips) + 1
    SIB = len(chip_flips)
    k_finish = min(1, nk - 1)

    def body(order_ref, h_ref, d_ref, s_ref, own_ref, recv_ref, sall_ref,
             acc_ref, keep_ref, pre_ref, to_sib_ref, to_chip_ref,
             sib_send, sib_recv, chip_send, chip_recv, ssend_sems, srecv_sems, local_sem):
        i, kk = pl.program_id(0), pl.program_id(1)
        x, y, c = _my_place()
        my_dev = 4 * x + 2 * y + c

        def small_copy(r, slot):
            dx, dy, dc = FLIPS[r]
            return pltpu.make_async_remote_copy(
                src_ref=s_ref, dst_ref=sall_ref.at[slot], send_sem=ssend_sems.at[r], recv_sem=srecv_sems.at[r],
                device_id=(x ^ dx, y ^ dy, c ^ dc), device_id_type=MESH)

        keep_small = pltpu.make_async_copy(s_ref, sall_ref.at[my_dev], local_sem)

        def sib_copy(t):
            dst = recv_ref.at[SIB] if t == SIB else pre_ref.at[t]
            return pltpu.make_async_remote_copy(
                src_ref=to_sib_ref.at[t], dst_ref=dst, send_sem=sib_send.at[t], recv_sem=sib_recv.at[t],
                device_id=(x, y, 1 - c), device_id_type=MESH)

        def chip_copy(t):
            dx, dy = chip_flips[t]
            return pltpu.make_async_remote_copy(
                src_ref=to_chip_ref.at[t], dst_ref=recv_ref.at[t], send_sem=chip_send.at[t], recv_sem=chip_recv.at[t],
                device_id=(x ^ dx, y ^ dy, c), device_id_type=MESH)

        def halves():
            first, second = acc_ref[0:SHARD_PAD, :], acc_ref[SHARD_PAD:2 * SHARD_PAD, :]
            return jnp.where(c == 0, first, second), jnp.where(c == 0, second, first)

        @pl.when((i == 0) & (kk == 0))
        def _():
            keep_small.start()
            for r in range(len(FLIPS)):
                small_copy(r, my_dev).start()

        @pl.when(kk == 0)
        def _():
            acc_ref[...] = jnp.zeros_like(acc_ref)

        acc_ref[...] += _dot_tn(d_ref[...], h_ref[...])

        for t in range(len(chip_flips)):
            @pl.when((i == t + 1) & (kk == k_finish))
            def _(t=t):
                sib_copy(t).wait_recv()
                to_chip_ref[t] = (keep_ref[...] + pre_ref[t].astype(F32)).astype(BF16)
                chip_copy(t).start()

        for t in range(len(chip_flips)):
            @pl.when((i == t) & (kk == nk - 1))
            def _(t=t):
                mine, theirs = halves()
                to_sib_ref[t] = theirs.astype(BF16)
                sib_copy(t).start()
                keep_ref[...] = mine

        @pl.when((i == n_steps - 1) & (kk == nk - 1))
        def _():
            mine, theirs = halves()
            own_ref[...] = mine
            to_sib_ref[SIB] = theirs.astype(BF16)
            sib_copy(SIB).start()
            for t in range(len(chip_flips)):
                sib_copy(t).wait_send()
                chip_copy(t).wait_send()
                chip_copy(t).wait_recv()
            sib_copy(SIB).wait_send()
            sib_copy(SIB).wait_recv()
            for r, (dx, dy, dc) in enumerate(FLIPS):
                small_copy(r, 4 * (x ^ dx) + 2 * (y ^ dy) + (c ^ dc)).wait_recv()
                small_copy(r, my_dev).wait_send()
            keep_small.wait()

    shard = (SHARD_PAD, D_MODEL)
    return pl.pallas_call(
        body, name="w_in_grad_rs",
        grid_spec=pltpu.PrefetchScalarGridSpec(
            num_scalar_prefetch=1, grid=(n_steps, nk),
            in_specs=[pl.BlockSpec((tk, D_MODEL), lambda i, kk, order: (kk, 0)),
                      pl.BlockSpec((None, tk, 2 * SHARD_PAD), lambda i, kk, order: (order[i], kk, 0)),
                      pl.BlockSpec(memory_space=pl.ANY)],
            out_specs=[pl.BlockSpec(shard, lambda i, kk, order: (0, 0)),
                       pl.BlockSpec(memory_space=pl.ANY), pl.BlockSpec(memory_space=pl.ANY)],
            scratch_shapes=[pltpu.VMEM((2 * SHARD_PAD, D_MODEL), F32), pltpu.VMEM(shard, F32),
                            pltpu.VMEM((SIB, *shard), BF16), pltpu.VMEM((SIB + 1, *shard), BF16),
                            pltpu.VMEM((SIB, *shard), BF16),
                            pltpu.SemaphoreType.DMA((SIB + 1,)), pltpu.SemaphoreType.DMA((SIB + 1,)),
                            pltpu.SemaphoreType.DMA((SIB,)), pltpu.SemaphoreType.DMA((SIB,)),
                            pltpu.SemaphoreType.DMA((7,)), pltpu.SemaphoreType.DMA((7,)), pltpu.SemaphoreType.DMA]),
        out_shape=[jax.ShapeDtypeStruct(shard, F32),
                   jax.ShapeDtypeStruct((SIB + 1, *shard), BF16),
                   jax.ShapeDtypeStruct((N_DEV, *small.shape), F32)],
        compiler_params=_params("arbitrary", "arbitrary"),
    )(chip_order, h, dsh, small)


def _adam_math(w, g, m, v):
    m_new = ADAM_B1 * m + (1.0 - ADAM_B1) * g
    v_new = ADAM_B2 * v + (1.0 - ADAM_B2) * (g * g)
    m_hat = m_new / (1.0 - ADAM_B1 ** ADAM_STEP)
    v_hat = v_new / (1.0 - ADAM_B2 ** ADAM_STEP)
    delta = -ADAM_LR * (m_hat / (jnp.sqrt(v_hat) + ADAM_EPS) + ADAM_WD * w)
    return delta, m_new, v_new


def _adam_big(name, own, own_idx, recv, w, m, v):
    rw, cw = w.shape
    rp = own.shape[1]
    steps = 8
    by_cols = rp != rw
    blk_w = (rw, cw // steps) if by_cols else (rw // steps, cw)
    blk_g = (rp, cw // steps) if by_cols else (rw // steps, cw)
    at = (lambda i: (0, i)) if by_cols else (lambda i: (i, 0))

    def body(idx_ref, o_ref, r_ref, w_ref, m_ref, v_ref, g_ref, d_ref, mo_ref, vo_ref):
        g = o_ref[...].astype(F32)
        for r in range(recv.shape[0]):
            g = g + r_ref[r].astype(F32)
        g = g[0:blk_w[0], :]
        g_ref[...] = g
        d_ref[...], mo_ref[...], vo_ref[...] = _adam_math(w_ref[...], g, m_ref[...], v_ref[...])

    spec = pl.BlockSpec(blk_w, lambda i, idx_ref: at(i))
    return pl.pallas_call(
        body, name=name,
        grid_spec=pltpu.PrefetchScalarGridSpec(
            num_scalar_prefetch=1, grid=(steps,),
            in_specs=[pl.BlockSpec((None, *blk_g), lambda i, idx_ref: (idx_ref[0], *at(i))),
                      pl.BlockSpec((recv.shape[0], *blk_g), lambda i, idx_ref: (0, *at(i))), spec, spec, spec],
            out_specs=[spec] * 4),
        out_shape=[jax.ShapeDtypeStruct((rw, cw), F32)] * 4,
        compiler_params=_params("parallel"),
    )(own_idx, own, recv, w, m, v)


def _adam_small(small_all, params):
    flat = [a for triple in params for a in triple]
    n_par = len(params)

    def body(s_ref, *refs):
        ins, outs, loss_ref = refs[:3 * n_par], refs[3 * n_par:-1], refs[-1]
        g_slab = s_ref[0]
        for dev in range(1, N_DEV):
            g_slab = g_slab + s_ref[dev]
        loss_ref[...] = g_slab[SMALL_LOSS:SMALL_LOSS + 1, :]
        dev = 4 * lax.axis_index("x") + 2 * lax.axis_index("y") + lax.axis_index("c")
        alpha_full = jnp.concatenate([g_slab[SMALL_W_ALPHA + half * B_GATE_RANK:SMALL_W_ALPHA + (half + 1) * B_GATE_RANK]
                                      for half in range(B_KEY_WIDTH // LANES)], axis=1)
        alpha_mine = pltpu.roll(alpha_full, (B_KEY_WIDTH - dev * SHARD_ALPHA) % B_KEY_WIDTH, 1)[:, 0:SHARD_ALPHA]
        grads = [_take_rows(g_slab, SMALL_G_IN, D_MODEL // LANES), _take_rows(g_slab, SMALL_G_FINAL, D_MODEL // LANES),
                 _take_rows(g_slab, SMALL_G_GLA, B_WIDTH // LANES), _take_rows(g_slab, SMALL_B_ALPHA, B_KEY_WIDTH // LANES),
                 g_slab[SMALL_SINKS:SMALL_SINKS + 1, 0:A_HEADS], alpha_mine]
        for i, g in enumerate(grads):
            w_ref, m_ref, v_ref = ins[3 * i:3 * i + 3]
            delta, m_new, v_new = _adam_math(w_ref[...], g, m_ref[...], v_ref[...])
            outs[4 * i][...] = g
            outs[4 * i + 1][...] = delta
            outs[4 * i + 2][...] = m_new
            outs[4 * i + 3][...] = v_new

    res = pl.pallas_call(
        body, name="adam_small",
        out_shape=[jax.ShapeDtypeStruct(t[0].shape, F32) for t in params for _ in range(4)]
                  + [jax.ShapeDtypeStruct((1, LANES), F32)],
    )(small_all, *flat)
    return [res[4 * i:4 * i + 4] for i in range(n_par)], res[-1]


def _local_step(x, cosf, sinf, loss_target, g_in, wt_sh, wa_pad, b_alpha, sinks, g_gla, out_shards, g_final, chip_order):
    B, S, _ = x.shape
    T = B * S
    x2 = x.reshape(T, D_MODEL)
    tgt2 = loss_target.reshape(T, D_MODEL)
    f, (g_woa, g_wob, g_wo) = _in_proj(x2, cosf, sinf, g_in, wt_sh, wa_pad, b_alpha, out_shards)
    w_o = g_wo.reshape(D_MODEL, D_MODEL)
    sink_row = jnp.repeat(sinks, WINDOW).reshape(1, ATT_ROWS)
    sink_col = sink_row.reshape(ATT_ROWS, 1)
    attn, lse = _attn_fwd(f["qkv"], sink_row, B, S)
    o_gla, st_all = _gla_fwd(f["q"], f["k"], f["cum"], f["vb"], B, S)
    (dxres, dattn, dog, dza, dzb, dga, dgb, dw_o, dw_oa, dw_ob, small_a) = _merge(
        x2, tgt2, attn, f["za"], o_gla, f["zb"], f["ga"], f["gb"], g_woa, g_wob, w_o, g_gla, g_final)
    dq, dkv, dsink = _attn_bwd(f["qkv"], dattn, attn, lse, sink_col, B, S)
    (dqb, dkb, dvb, dla), (rv_o, rv_oa, rv_ob) = _gla_bwd(f["q"], f["k"], f["cum"], f["vb"], dog, st_all, B, S,
                                                        [dw_o, dw_oa, dw_ob])
    parts = dict(dq=dq, dkv=dkv, dza=dza, dqb=dqb, dkb=dkb, dvb=dvb, dzb=dzb, dla=dla, u=f["u"], alr=f["alr"],
                 dga=dga, dgb=dgb)
    dx, dsh, small_c = _in_proj_bwd(x2, dxres, cosf, sinf, g_in, f["wt_pad"], wa_pad, parts)
    small = jnp.concatenate([small_a, dsink, small_c], axis=0)
    own_in, rv_in, small_all = _w_in_grad_rs(f["h"], dsh, chip_order, small)
    return dict(grad_x=dx.reshape(B, S, D_MODEL), own_in=own_in, rv_in=rv_in,
                own_o=dw_o, rv_o=rv_o, own_oa=dw_oa, rv_oa=rv_oa, own_ob=dw_ob, rv_ob=rv_ob, small_all=small_all)


def kernel(x, positions, g_in, w_in, w_alpha_up, b_alpha, attn_sinks, g_gla_norm, w_out_a, w_out_b, w_o, g_final, loss_target, m_g_in, m_w_in, m_w_alpha_up, m_b_alpha, m_attn_sinks, m_g_gla_norm, m_w_out_a, m_w_out_b, m_w_o, m_g_final, v_g_in, v_w_in, v_w_alpha_up, v_b_alpha, v_attn_sinks, v_g_gla_norm, v_w_out_a, v_w_out_b, v_w_o, v_g_final):
    xi, yi, ci = _my_place()
    dev_idx = (4 * xi + 2 * yi + ci).reshape(1).astype(jnp.int32)
    chip = 2 * xi + yi
    chip_order = jnp.stack([chip ^ 3, chip ^ 2, chip ^ 1, chip]).astype(jnp.int32)

    (g_win, g_wa), cosf, sinf = _gather_first(
        [jnp.pad(w_in[0].T.astype(BF16), ((0, SHARD_PAD - SHARD_IN), (0, 0))), w_alpha_up[0].astype(BF16)],
        positions.reshape(-1, 1))
    wt_sh = g_win.reshape(N_DEV * SHARD_PAD, D_MODEL)
    wa_pad = jnp.pad(jnp.concatenate([g_wa[j] for j in range(N_DEV)], axis=1), ((0, RANK_PAD - B_GATE_RANK), (0, 0)))

    r = _local_step(x, cosf, sinf, loss_target, g_in, wt_sh, wa_pad, b_alpha, attn_sinks[0], g_gla_norm,
                    [w_out_a[0].astype(BF16), w_out_b[0].astype(BF16), w_o[0].astype(BF16)],
                    g_final.reshape(1, D_MODEL), chip_order)

    first = jnp.zeros((1,), jnp.int32)
    big = [_adam_big("adam_w_in", r["own_in"][None], first, r["rv_in"], w_in[0].T, m_w_in[0].T, v_w_in[0].T),
           _adam_big("adam_w_out_a", r["own_oa"], dev_idx, r["rv_oa"], w_out_a[0], m_w_out_a[0], v_w_out_a[0]),
           _adam_big("adam_w_out_b", r["own_ob"], dev_idx, r["rv_ob"], w_out_b[0], m_w_out_b[0], v_w_out_b[0]),
           _adam_big("adam_w_o", r["own_o"], dev_idx, r["rv_o"], w_o[0], m_w_o[0], v_w_o[0])]
    row = lambda a: a.reshape(1, D_MODEL)
    big[0] = [a.T for a in big[0]]
    (s_in, s_final, s_gla, s_ba, s_sinks, s_wa), loss_row = _adam_small(r["small_all"], [
        (g_in, m_g_in, v_g_in), (row(g_final), row(m_g_final), row(v_g_final)),
        (g_gla_norm, m_g_gla_norm, v_g_gla_norm), (b_alpha, m_b_alpha, v_b_alpha),
        (attn_sinks, m_attn_sinks, v_attn_sinks), (w_alpha_up[0], m_w_alpha_up[0], v_w_alpha_up[0])])

    def group(i):
        return (s_in[i], big[0][i][None], s_wa[i][None], s_ba[i], s_sinks[i], s_gla[i], big[1][i][None], big[2][i][None],
                big[3][i][None], s_final[i].reshape(D_MODEL))

    return (loss_row[0, 0], r["grad_x"], *group(0), *group(1), *group(2), *group(3))
```

```python
import functools
import math

import numpy as np
import jax
import jax.numpy as jnp
from jax import lax
from jax.experimental import pallas as pl
from jax.experimental.pallas import tpu as pltpu

F32 = jnp.float32
BF16 = jnp.bfloat16
MESH = pl.DeviceIdType.MESH

D_MODEL = 1024
A_HEADS, A_KV_HEADS, A_HEAD_DIM = 8, 2, 64
A_WIDTH, A_KV_WIDTH = 512, 128
WINDOW = 128
ROPE_THETA = 500000.0
ROPE_DIM = 16
B_HEADS, B_KEY_DIM, B_VAL_DIM = 4, 64, 128
B_KEY_WIDTH, B_WIDTH = 256, 512
B_GATE_RANK = 16
B_GATE_TEMP = 16.0
B_CHUNK = 64
NORM_EPS = 1e-6
NEG_BIG = -1e30
D_IN = 4880
N_DEV = 8
N_CHIPS = 4
ADAM_LR, ADAM_B1, ADAM_B2, ADAM_EPS, ADAM_WD, ADAM_STEP = 0.001, 0.9, 0.999, 1e-08, 0.01, 10

LANES = 128
V7X_VMEM_LIMIT = 56 * 1024 * 1024

RANK_PAD = LANES
SEG = {}
_off = 0
for _name, _w in (("qa", 512), ("ka", 128), ("va", 128), ("za", 512), ("qb", 256), ("kb", 256),
                  ("vb", 512), ("zb", 512), ("alr", RANK_PAD), ("ga", 1024), ("gb", 1024)):
    SEG[_name] = (_off, _off + _w)
    _off += _w
D_IN_PAD = _off
ALR_SRC = SEG["alr"][0]
QKV_K, QKV_V, QKV_W = SEG["ka"][0], SEG["va"][0], SEG["va"][1]
ATT_SCALE = A_HEAD_DIM ** -0.5

SHARD_IN = D_IN // N_DEV
SHARD_PAD = 640
SHARD_OUT = D_MODEL // N_DEV
SHARD_ALPHA = B_KEY_WIDTH // N_DEV

SMALL_G_FINAL, SMALL_G_GLA, SMALL_LOSS, SMALL_SINKS, SMALL_G_IN, SMALL_B_ALPHA, SMALL_W_ALPHA = 0, 8, 12, 16, 24, 32, 40
SMALL_ROWS = 72


def _dot(a, b):
    return jnp.dot(a, b, preferred_element_type=F32)


def _dot_nt(a, b):
    return lax.dot_general(a, b, (((1,), (1,)), ((), ())), preferred_element_type=F32)


def _dot_tn(a, b):
    return lax.dot_general(a, b, (((0,), (0,)), ((), ())), preferred_element_type=F32)


def _sigmoid(z):
    return 1.0 / (1.0 + jnp.exp(-z))


def _sigmoid_tanh(z):
    return 0.5 * jnp.tanh(0.5 * z) + 0.5


def _params(*sem):
    return pltpu.CompilerParams(dimension_semantics=sem, vmem_limit_bytes=V7X_VMEM_LIMIT)


def _const_spec(shape):
    nd = len(shape)
    return pl.BlockSpec(shape, lambda *_: (0,) * nd, pipeline_mode=pl.Buffered(1))


def _lane_iota(shape):
    return lax.broadcasted_iota(jnp.int32, shape, 1)


def _row_iota(shape):
    return lax.broadcasted_iota(jnp.int32, shape, 0)


def _split3(v):
    hi = v.astype(BF16)
    r1 = v - hi.astype(F32)
    mid = r1.astype(BF16)
    lo = (r1 - mid.astype(F32)).astype(BF16)
    return hi, mid, lo


def _put_rows(ref, row0, vec):
    for r in range(vec.shape[1] // LANES):
        ref[row0 + r:row0 + r + 1, :] = vec[:, r * LANES:(r + 1) * LANES]


def _take_rows(slab, row0, n):
    return jnp.concatenate([slab[row0 + r:row0 + r + 1, :] for r in range(n)], axis=1)


def _rope_lane_constants():
    half = ROPE_DIM // 2
    inv_freq = np.exp(-math.log(ROPE_THETA) * np.arange(half, dtype=np.float32) * np.float32(2.0 / ROPE_DIM)).astype(np.float32)
    lane = np.arange(LANES)
    j = lane % A_HEAD_DIM
    invf = np.where(j < ROPE_DIM, inv_freq[j % half], 0.0).astype(np.float32)
    sign = np.where(j < half, -1.0, np.where(j < ROPE_DIM, 1.0, 0.0)).astype(np.float32)
    return jnp.asarray(invf)[None, :], jnp.asarray(sign)[None, :]


def _rope_slab(t, cos, sin_signed):
    first = (_lane_iota(t.shape) % A_HEAD_DIM) < (ROPE_DIM // 2)
    partner = jnp.where(first, pltpu.roll(t, LANES - ROPE_DIM // 2, 1), pltpu.roll(t, ROPE_DIM // 2, 1))
    return t * cos + partner * sin_signed


def _shard_pad_cols(j):
    cut = ALR_SRC + B_GATE_RANK
    shift = RANK_PAD - B_GATE_RANK
    a, b = j * SHARD_IN, (j + 1) * SHARD_IN
    if b <= cut:
        return [(a, b)]
    if a >= cut:
        return [(a + shift, b + shift)]
    return [(a, cut), (cut + shift, b + shift)]


def _in_proj(x2, cosf, sinf, g_in, wt_sh, wa_pad, b_alpha, later_shards):
    T = x2.shape[0]
    tm = math.gcd(T, 512)
    last = T // tm - 1
    nl = len(later_shards)

    def body(x_ref, cos_ref, sin_ref, g_ref, wsh_ref, wa_ref, ba_ref, *rest):
        sh_refs, rest = rest[:nl], rest[nl:]
        (h_ref, qkv_ref, za_ref, q_ref, k_ref, vb_ref, zb_ref, alr_ref, u_ref, cum_ref, ga_ref, gb_ref, wt_out) = rest[:13]
        all_refs, (wt_ref, send_sems, recv_sems, local_sems, wt_sem) = rest[13:13 + nl], rest[13 + nl:]
        wt_copy = pltpu.make_async_copy(wt_ref, wt_out, wt_sem)
        px, py, pc = _my_place()
        my_dev = 4 * px + 2 * py + pc

        def wcopy(a, r, slot):
            dx, dy, dc = FLIPS[r]
            return pltpu.make_async_remote_copy(
                src_ref=sh_refs[a], dst_ref=all_refs[a].at[slot], send_sem=send_sems.at[a, r],
                recv_sem=recv_sems.at[a, r], device_id=(px ^ dx, py ^ dy, pc ^ dc), device_id_type=MESH)

        keep = [pltpu.make_async_copy(sh_refs[a], all_refs[a].at[my_dev], local_sems.at[a]) for a in range(nl)]

        @pl.when(pl.program_id(0) == 0)
        def _():
            for a in range(nl):
                keep[a].start()
                for r in range(len(FLIPS)):
                    wcopy(a, r, my_dev).start()

        @pl.when(pl.program_id(0) == 0)
        def _():
            for j in range(N_DEV):
                src = j * SHARD_PAD
                for a, b in _shard_pad_cols(j):
                    wt_ref[a:b, :] = wsh_ref[src:src + b - a, :]
                    src += b - a
            a, b = SEG["alr"]
            wt_ref[a + B_GATE_RANK:b, :] = jnp.zeros((RANK_PAD - B_GATE_RANK, D_MODEL), BF16)
            wt_copy.start()

        x = x_ref[...]
        r = lax.rsqrt(jnp.mean(x * x, axis=-1, keepdims=True) + NORM_EPS)
        h = (x * r * g_ref[...]).astype(BF16)
        h_ref[...] = h

        def seg(name):
            a, b = SEG[name]
            return _dot_nt(h, wt_ref[a:b, :])

        cos, sin = cos_ref[...], sin_ref[...]
        qa = seg("qa") * ATT_SCALE
        for s in range(A_WIDTH // LANES):
            qkv_ref[:, s * LANES:(s + 1) * LANES] = _rope_slab(qa[:, s * LANES:(s + 1) * LANES], cos, sin).astype(BF16)
        qkv_ref[:, QKV_K:QKV_V] = _rope_slab(seg("ka"), cos, sin).astype(BF16)
        qkv_ref[:, QKV_V:QKV_W] = seg("va").astype(BF16)
        za_ref[...] = seg("za")
        q_ref[...] = seg("qb")
        k_ref[...] = seg("kb")
        vb_ref[...] = seg("vb").astype(BF16)
        zb_ref[...] = seg("zb")
        ga_ref[...] = seg("ga")
        gb_ref[...] = seg("gb")
        alr = seg("alr").astype(BF16)
        alr_ref[...] = alr
        u = _dot(alr, wa_ref[...]) + ba_ref[...]
        u_ref[...] = u
        log_a = (jnp.minimum(u, 0.0) - jnp.log(1.0 + jnp.exp(-jnp.abs(u)))) * (1.0 / B_GATE_TEMP)
        row, col = _row_iota((tm, tm)), _lane_iota((tm, tm))
        tri = ((row // B_CHUNK == col // B_CHUNK) & (col <= row)).astype(BF16)
        hi, mid, lo = _split3(log_a)
        cum_ref[...] = _dot(tri, hi) + _dot(tri, mid) + _dot(tri, lo)

        @pl.when(pl.program_id(0) == last)
        def _():
            for a in range(nl):
                for r, (dx, dy, dc) in enumerate(FLIPS):
                    wcopy(a, r, 4 * (px ^ dx) + 2 * (py ^ dy) + (pc ^ dc)).wait_recv()
                    wcopy(a, r, my_dev).wait_send()
                keep[a].wait()
            wt_copy.wait()

    def rows(w):
        return pl.BlockSpec((tm, w), lambda i: (i, 0))

    outs = [("h", D_MODEL, BF16), ("qkv", QKV_W, BF16), ("za", A_WIDTH, F32), ("q", B_KEY_WIDTH, F32),
            ("k", B_KEY_WIDTH, F32), ("vb", B_WIDTH, BF16), ("zb", B_WIDTH, F32), ("alr", RANK_PAD, BF16),
            ("u", B_KEY_WIDTH, F32), ("cum", B_KEY_WIDTH, F32), ("ga", D_MODEL, F32), ("gb", D_MODEL, F32)]
    res = pl.pallas_call(
        body, name="in_proj", grid=(T // tm,),
        in_specs=[rows(D_MODEL), rows(LANES), rows(LANES), _const_spec((1, D_MODEL)),
                  _const_spec((N_DEV * SHARD_PAD, D_MODEL)), _const_spec((RANK_PAD, B_KEY_WIDTH)),
                  _const_spec((1, B_KEY_WIDTH))] + _any_specs(nl),
        out_specs=[rows(w) for _, w, _ in outs] + _any_specs(1 + nl),
        out_shape=[jax.ShapeDtypeStruct((T, w), dt) for _, w, dt in outs]
                  + [jax.ShapeDtypeStruct((D_IN_PAD, D_MODEL), BF16)]
                  + [jax.ShapeDtypeStruct((N_DEV, *sh.shape), sh.dtype) for sh in later_shards],
        scratch_shapes=[pltpu.VMEM((D_IN_PAD, D_MODEL), BF16),
                        pltpu.SemaphoreType.DMA((nl, len(FLIPS))), pltpu.SemaphoreType.DMA((nl, len(FLIPS))),
                        pltpu.SemaphoreType.DMA((nl,)), pltpu.SemaphoreType.DMA],
        compiler_params=_params("arbitrary"),
    )(x2, cosf, sinf, g_in, wt_sh, wa_pad, b_alpha, *later_shards)
    n_out = len(outs) + 1
    return dict(zip([n for n, _, _ in outs] + ["wt_pad"], res[:n_out])), res[n_out:]


def _dup_kv_head(t, g):
    tf = t.astype(F32)
    keep = (_lane_iota(tf.shape) < A_HEAD_DIM) == (g == 0)
    return jnp.where(keep, tf, pltpu.roll(tf, A_HEAD_DIM, 1)).astype(BF16)


def _stack_heads(t):
    lo = _lane_iota(t.shape) < A_HEAD_DIM
    zero = jnp.zeros_like(t)
    return jnp.concatenate([jnp.where(lo, t, zero), jnp.where(lo, zero, t)], axis=0)


ATT_ROWS = A_HEADS * WINDOW
GROUP_ROWS = ATT_ROWS // A_KV_HEADS
HEADS_PER_GROUP = A_HEADS // A_KV_HEADS


def _band_mask_t(n):
    kj = _row_iota((2 * WINDOW, GROUP_ROWS)) - WINDOW
    qi = _lane_iota((2 * WINDOW, GROUP_ROWS)) % WINDOW
    return (kj <= qi) & (qi - kj < WINDOW) & ((n > 0) | (kj >= 0))


def _stacked_queries(ref, g):
    pairs = range(g * HEADS_PER_GROUP // 2, (g + 1) * HEADS_PER_GROUP // 2)
    return jnp.concatenate([_stack_heads(ref[:, p * LANES:(p + 1) * LANES]) for p in pairs], axis=0)


def _unstack_heads(t, g, ref, dtype):
    lo = _lane_iota((WINDOW, LANES)) < A_HEAD_DIM
    for hh in range(HEADS_PER_GROUP // 2):
        p = g * HEADS_PER_GROUP // 2 + hh
        ref[:, p * LANES:(p + 1) * LANES] = jnp.where(lo, t[2 * hh * WINDOW:(2 * hh + 1) * WINDOW],
                                                       t[(2 * hh + 1) * WINDOW:(2 * hh + 2) * WINDOW]).astype(dtype)


def _attn_fwd(qkv, sink_row, B, S):
    T = B * S
    nb = S // WINDOW
    scale = A_HEAD_DIM ** -0.5

    def body(sink_ref, q_ref, kc_ref, vc_ref, kp_ref, vp_ref, o_ref, lse_ref):
        valid = _band_mask_t(pl.program_id(1))
        k = jnp.concatenate([kp_ref[...], kc_ref[...]], axis=0)
        v = jnp.concatenate([vp_ref[...], vc_ref[...]], axis=0)
        lse_rows = []
        for g in range(A_KV_HEADS):
            kd, vd = _dup_kv_head(k, g), _dup_kv_head(v, g)
            s = jnp.where(valid, _dot_nt(kd, _stacked_queries(q_ref, g)), NEG_BIG)
            sink = sink_ref[:, g * GROUP_ROWS:(g + 1) * GROUP_ROWS]
            m = jnp.maximum(jnp.max(s, axis=0, keepdims=True), sink)
            e = jnp.exp(s - m)
            den = jnp.sum(e, axis=0, keepdims=True) + jnp.exp(sink - m)
            o = _dot_tn((e * (1.0 / den)).astype(BF16), vd)
            _unstack_heads(o, g, o_ref, F32)
            lse = m + jnp.log(den)
            lse_rows += [lse[:, j * WINDOW:(j + 1) * WINDOW] for j in range(HEADS_PER_GROUP)]
        by_head = jnp.concatenate(lse_rows + [jnp.zeros((WINDOW - A_HEADS, WINDOW), F32)], axis=0)
        lse_ref[...] = by_head.T

    def cur(col, w):
        return pl.BlockSpec((WINDOW, w), lambda b, n: (b * nb + n, col))

    def prev(col):
        return pl.BlockSpec((WINDOW, LANES), lambda b, n: (b * nb + jnp.maximum(n - 1, 0), col))

    kcol, vcol = QKV_K // LANES, QKV_V // LANES
    return pl.pallas_call(
        body, name="attn_fwd", grid=(B, nb),
        in_specs=[_const_spec((1, ATT_ROWS)), cur(0, A_WIDTH), cur(kcol, LANES), cur(vcol, LANES), prev(kcol), prev(vcol)],
        out_specs=[cur(0, A_WIDTH), cur(0, LANES)],
        out_shape=[jax.ShapeDtypeStruct((T, A_WIDTH), F32), jax.ShapeDtypeStruct((T, LANES), F32)],
        compiler_params=_params("parallel", "parallel"),
    )(sink_row, qkv, qkv, qkv, qkv, qkv)


ATT_CHUNK = 64


def _chunk_masks(n):
    masks = []
    for half in range(WINDOW // ATT_CHUNK):
        qi = _row_iota((ATT_CHUNK, 2 * WINDOW)) + half * ATT_CHUNK
        kj = _lane_iota((ATT_CHUNK, 2 * WINDOW)) - WINDOW
        masks.append((kj <= qi) & (qi - kj < WINDOW) & ((n > 0) | (kj >= 0)))
    return masks


def _all_stacked_queries(ref):
    return jnp.concatenate([_stacked_queries(ref, g) for g in range(A_KV_HEADS)], axis=0)


def _by_group(fn, lhs, rhs_per_group):
    return jnp.concatenate([fn(lhs[g * GROUP_ROWS:(g + 1) * GROUP_ROWS], rhs_per_group[g])
                            for g in range(A_KV_HEADS)], axis=0)


def _attn_bwd(qkv, do, out, lse, sink_col, B, S):
    T = B * S
    nb = S // WINDOW
    scale = A_HEAD_DIM ** -0.5

    def body(sink_ref, q_ref, kc_ref, vc_ref, kp_ref, vp_ref, do_ref, out_ref, lse_ref,
             dq_ref, dkv_ref, dsink_ref, carry_ref, s_ref, dp_ref, ds_ref, p_ref):
        b, n = pl.program_id(0), pl.program_id(1)
        active = n < nb
        masks = _chunk_masks(jnp.minimum(n, nb - 1))

        @pl.when((b == 0) & (n == 0))
        def _():
            dsink_ref[...] = jnp.zeros_like(dsink_ref)

        k = jnp.concatenate([kp_ref[...], kc_ref[...]], axis=0)
        v = jnp.concatenate([vp_ref[...], vc_ref[...]], axis=0)
        kd = [_dup_kv_head(k, g) for g in range(A_KV_HEADS)]
        vd = [_dup_kv_head(v, g) for g in range(A_KV_HEADS)]
        qs = _all_stacked_queries(q_ref)
        dos = _all_stacked_queries(do_ref)
        s_ref[...] = _by_group(_dot_nt, qs, kd)
        dp_ref[...] = _by_group(_dot_nt, dos, vd)
        lane = _lane_iota((ATT_CHUNK, LANES))
        lo = lane < A_HEAD_DIM
        lane1 = _lane_iota((1, LANES))
        dsink_row = jnp.zeros((1, LANES), F32)
        for c in range(ATT_ROWS // ATT_CHUNK):
            rows = slice(c * ATT_CHUNK, (c + 1) * ATT_CHUNK)
            head, half = divmod(c, WINDOW // ATT_CHUNK)
            qrows = slice(half * ATT_CHUNK, (half + 1) * ATT_CHUNK)
            slab = slice((head // 2) * LANES, (head // 2 + 1) * LANES)
            lse_col = jnp.sum(jnp.where(lane == head, lse_ref[qrows, :], 0.0), axis=-1, keepdims=True)
            prod = do_ref[qrows, slab].astype(F32) * out_ref[qrows, slab]
            mine = lo if head % 2 == 0 else jnp.logical_not(lo)
            delta = jnp.sum(jnp.where(mine, prod, 0.0), axis=-1, keepdims=True)
            s = jnp.where(masks[half], s_ref[rows, :], NEG_BIG)
            prob = jnp.exp(s - lse_col)
            p_ref[rows, :] = prob.astype(BF16)
            ds_ref[rows, :] = (prob * (dp_ref[rows, :] - delta)).astype(BF16)
            w = -jnp.exp(sink_ref[rows, :] - lse_col) * delta
            dsink_row += jnp.where(lane1 == head, jnp.sum(w, axis=0, keepdims=True), 0.0)
        dq = _by_group(_dot, ds_ref[...], kd) * ATT_SCALE
        lo_q = _lane_iota((WINDOW, LANES)) < A_HEAD_DIM
        for p in range(A_HEADS // 2):
            dq_ref[:, p * LANES:(p + 1) * LANES] = jnp.where(
                lo_q, dq[2 * p * WINDOW:(2 * p + 1) * WINDOW], dq[(2 * p + 1) * WINDOW:(2 * p + 2) * WINDOW]).astype(BF16)
        lane2 = _lane_iota((2 * WINDOW, LANES))
        dk_tot = jnp.zeros((2 * WINDOW, LANES), F32)
        dv_tot = jnp.zeros((2 * WINDOW, LANES), F32)
        for g in range(A_KV_HEADS):
            grows = slice(g * GROUP_ROWS, (g + 1) * GROUP_ROWS)
            dk_acc = _dot_tn(ds_ref[grows, :], qs[grows])
            dv_acc = _dot_tn(p_ref[grows, :], dos[grows])
            mine = (lane2 < A_HEAD_DIM) == (g == 0)
            dk_tot = jnp.where(mine, dk_acc + pltpu.roll(dk_acc, A_HEAD_DIM, 1), dk_tot)
            dv_tot = jnp.where(mine, dv_acc + pltpu.roll(dv_acc, A_HEAD_DIM, 1), dv_tot)
        gate = jnp.where(active, 1.0, 0.0)
        dsink_ref[0:1, :] += dsink_row * gate
        dkv_ref[:, 0:LANES] = (carry_ref[:, 0:LANES] + dk_tot[:WINDOW] * gate).astype(BF16)
        dkv_ref[:, LANES:] = (carry_ref[:, LANES:] + dv_tot[:WINDOW] * gate).astype(BF16)
        carry_ref[:, 0:LANES] = dk_tot[WINDOW:]
        carry_ref[:, LANES:] = dv_tot[WINDOW:]

    def cur(col, w):
        return pl.BlockSpec((WINDOW, w), lambda b, n: (b * nb + jnp.minimum(n, nb - 1), col))

    def prev(col):
        return pl.BlockSpec((WINDOW, LANES), lambda b, n: (b * nb + jnp.maximum(jnp.minimum(n, nb - 1) - 1, 0), col))

    lag = pl.BlockSpec((WINDOW, 2 * LANES), lambda b, n: (b * nb + jnp.maximum(n - 1, 0), 0))
    kcol, vcol = QKV_K // LANES, QKV_V // LANES
    scores = (ATT_ROWS, 2 * WINDOW)
    return pl.pallas_call(
        body, name="attn_bwd", grid=(B, nb + 1),
        in_specs=[_const_spec((ATT_ROWS, 1)), cur(0, A_WIDTH), cur(kcol, LANES), cur(vcol, LANES),
                  prev(kcol), prev(vcol), cur(0, A_WIDTH), cur(0, A_WIDTH), cur(0, LANES)],
        out_specs=[cur(0, A_WIDTH), lag, pl.BlockSpec((8, LANES), lambda b, n: (0, 0))],
        out_shape=[jax.ShapeDtypeStruct((T, A_WIDTH), BF16), jax.ShapeDtypeStruct((T, 2 * LANES), BF16),
                   jax.ShapeDtypeStruct((8, LANES), F32)],
        scratch_shapes=[pltpu.VMEM((WINDOW, 2 * LANES), F32), pltpu.VMEM(scores, F32), pltpu.VMEM(scores, F32),
                        pltpu.VMEM(scores, BF16), pltpu.VMEM(scores, BF16)],
        compiler_params=_params("arbitrary", "arbitrary"),
    )(sink_col, qkv, qkv, qkv, qkv, qkv, do, out, lse)


GLA_TILE = 256
CHUNKS_PER_TILE = GLA_TILE // B_CHUNK


def _gla_factors(q_ref, k_ref, cum_ref):
    scale = B_KEY_DIM ** -0.5
    cum = cum_ref[...]
    shape = (B_CHUNK, B_KEY_WIDTH)
    last = jnp.concatenate([jnp.broadcast_to(cum_ref[pl.ds(c * B_CHUNK + B_CHUNK - 1, 1), :], shape)
                            for c in range(CHUNKS_PER_TILE)], axis=0)
    mid = jnp.concatenate([jnp.broadcast_to(cum_ref[pl.ds(c * B_CHUNK + B_CHUNK // 2 - 1, 1), :], shape)
                           for c in range(CHUNKS_PER_TILE)], axis=0)
    e_qm, e_km, e_qe, e_kd = jnp.exp(cum - mid), jnp.exp(mid - cum), jnp.exp(cum), jnp.exp(last - cum)
    qs = q_ref[...] * scale
    k = k_ref[...]
    return qs, k, (e_qm, e_km, e_qe, e_kd)


def _head_mask(shape, h):
    return (_lane_iota(shape) // B_KEY_DIM) == h


def _stack_masked(t):
    return jnp.concatenate([jnp.where(_head_mask(t.shape, h), t, 0.0) for h in range(B_HEADS)], axis=0).astype(BF16)


def _select_heads(t):
    shape = (B_CHUNK, B_KEY_WIDTH)
    out = jnp.zeros(shape, F32)
    for h in range(B_HEADS):
        out = jnp.where(_head_mask(shape, h), t[h * B_CHUNK:(h + 1) * B_CHUNK], out)
    return out


def _select_state(t):
    shape = (B_VAL_DIM, B_KEY_WIDTH)
    out = jnp.zeros(shape, F32)
    for h in range(B_HEADS):
        out = jnp.where(_head_mask(shape, h), t[h * B_VAL_DIM:(h + 1) * B_VAL_DIM], out)
    return out


def _rows_by_head(t):
    return jnp.concatenate([t[:, h * B_VAL_DIM:(h + 1) * B_VAL_DIM] for h in range(B_HEADS)], axis=0)


def _intra_mask():
    i, j = _row_iota((GLA_TILE, GLA_TILE)), _lane_iota((GLA_TILE, GLA_TILE))
    return (i // B_CHUNK == j // B_CHUNK) & (j <= i)


def _pair_stack(t, p):
    slab = t[:, p * LANES:(p + 1) * LANES]
    lo = _lane_iota(slab.shape) < B_KEY_DIM
    return jnp.concatenate([jnp.where(lo, slab, 0.0), jnp.where(lo, 0.0, slab)], axis=0).astype(BF16)


def _gla_fwd(q, k, cum, vb, B, S):
    T = B * S
    nt = S // GLA_TILE

    def one_sequence(q_ref, k_ref, cum_ref, v_ref, o_ref, st_all_ref, st_ref):
        qs, kk, (e_qm, e_km, e_qe, e_kd) = _gla_factors(q_ref, k_ref, cum_ref)
        qm, km, qe, kd = qs * e_qm, kk * e_km, qs * e_qe, (kk * e_kd).astype(BF16)
        mask = _intra_mask()
        intra = []
        for p in range(B_HEADS // 2):
            a = _dot_nt(_pair_stack(qm, p), km[:, p * LANES:(p + 1) * LANES].astype(BF16))
            for hh in range(2):
                h = 2 * p + hh
                att = jnp.where(mask, a[hh * GLA_TILE:(hh + 1) * GLA_TILE], 0.0).astype(BF16)
                intra.append(_dot(att, v_ref[:, h * B_VAL_DIM:(h + 1) * B_VAL_DIM]))
        inter = []
        for c in range(CHUNKS_PER_TILE):
            rows = slice(c * B_CHUNK, (c + 1) * B_CHUNK)
            st = st_ref[...]
            st_all_ref[c] = st
            inter.append(_dot_nt(_stack_masked(qe[rows]), st.astype(BF16)))
            inc = _select_state(_dot_tn(v_ref[rows, :], kd[rows]))
            decay = jnp.exp(cum_ref[pl.ds(c * B_CHUNK + B_CHUNK - 1, 1), :])
            st_ref[...] = st * decay + inc
        for h in range(B_HEADS):
            oi = jnp.concatenate([inter[c][h * B_CHUNK:(h + 1) * B_CHUNK] for c in range(CHUNKS_PER_TILE)], axis=0)
            o_ref[:, h * B_VAL_DIM:(h + 1) * B_VAL_DIM] = intra[h] + oi

    def body(q_ref, k_ref, cum_ref, v_ref, o_ref, st_all_ref, st_ref):
        @pl.when(pl.program_id(0) == 0)
        def _():
            st_ref[...] = jnp.zeros_like(st_ref)

        for b in range(B):
            one_sequence(*[r.at[b] for r in (q_ref, k_ref, cum_ref, v_ref, o_ref, st_all_ref, st_ref)])

    def rows(w):
        return pl.BlockSpec((B, GLA_TILE, w), lambda t: (0, t, 0))

    seq = lambda a: a.reshape(B, S, a.shape[-1])
    o, st_all = pl.pallas_call(
        body, name="gla_fwd", grid=(nt,),
        in_specs=[rows(B_KEY_WIDTH), rows(B_KEY_WIDTH), rows(B_KEY_WIDTH), rows(B_WIDTH)],
        out_specs=[rows(B_WIDTH),
                   pl.BlockSpec((B, CHUNKS_PER_TILE, B_VAL_DIM, B_KEY_WIDTH), lambda t: (0, t, 0, 0))],
        out_shape=[jax.ShapeDtypeStruct((B, S, B_WIDTH), F32),
                   jax.ShapeDtypeStruct((B, S // B_CHUNK, B_VAL_DIM, B_KEY_WIDTH), F32)],
        scratch_shapes=[pltpu.VMEM((B, B_VAL_DIM, B_KEY_WIDTH), F32)],
        compiler_params=_params("arbitrary"),
    )(seq(q), seq(k), seq(cum), seq(vb))
    return o.reshape(T, B_WIDTH), st_all.reshape(T // B_CHUNK, B_VAL_DIM, B_KEY_WIDTH)


def _gla_bwd(q, k, cum, vb, do, st_all, B, S, wgrads):
    T = B * S
    nt = S // GLA_TILE
    scale = B_KEY_DIM ** -0.5
    nw = len(wgrads)

    def one_sequence(q_ref, k_ref, cum_ref, v_ref, do_ref, st_all_ref, dq_ref, dk_ref, dv_ref, dla_ref, dst_ref):
        qs, kk, (e_qm, e_km, e_qe, e_kd) = _gla_factors(q_ref, k_ref, cum_ref)
        qm, km, qe, kd = qs * e_qm, kk * e_km, qs * e_qe, kk * e_kd
        mask = _intra_mask()
        dqm_slabs, dkm_slabs, dv_intra = [], [], []
        for p in range(B_HEADS // 2):
            qm_st = _pair_stack(qm, p)
            km_p = km[:, p * LANES:(p + 1) * LANES].astype(BF16)
            a = _dot_nt(qm_st, km_p)
            da_blocks, dqm_h = [], []
            for hh in range(2):
                h = 2 * p + hh
                vs = slice(h * B_VAL_DIM, (h + 1) * B_VAL_DIM)
                att = jnp.where(mask, a[hh * GLA_TILE:(hh + 1) * GLA_TILE], 0.0).astype(BF16)
                dv_intra.append(_dot_tn(att, do_ref[:, vs]))
                da = jnp.where(mask, _dot_nt(do_ref[:, vs], v_ref[:, vs]), 0.0).astype(BF16)
                da_blocks.append(da)
                dqm_h.append(_dot(da, km_p))
            lo = _lane_iota((GLA_TILE, LANES)) < B_KEY_DIM
            dqm_slabs.append(jnp.where(lo, dqm_h[0], dqm_h[1]))
            dkm_slabs.append(_dot_tn(jnp.concatenate(da_blocks, axis=0), qm_st))
        dqm = jnp.concatenate(dqm_slabs, axis=1)
        dkm = jnp.concatenate(dkm_slabs, axis=1)

        dqe_c, dkd_c, dv_inter, tail_c = ([None] * CHUNKS_PER_TILE for _ in range(4))
        for c in reversed(range(CHUNKS_PER_TILE)):
            rows = slice(c * B_CHUNK, (c + 1) * B_CHUNK)
            dst = dst_ref[...]
            dst_b = dst.astype(BF16)
            dv_inter[c] = _dot_nt(_stack_masked(kd[rows]), dst_b)
            dkd_c[c] = _select_heads(_dot(_rows_by_head(v_ref[rows, :]), dst_b))
            do_c = do_ref[rows, :]
            dqe_c[c] = _select_heads(_dot(_rows_by_head(do_c), st_all_ref[c].astype(BF16)))
            contrib = _select_state(_dot_tn(do_c, qe[rows].astype(BF16)))
            decay = jnp.exp(cum_ref[pl.ds(c * B_CHUNK + B_CHUNK - 1, 1), :])
            tail = (jnp.sum(kk[rows] * dkd_c[c] * e_kd[rows], axis=0, keepdims=True)
                    + decay * jnp.sum(st_all_ref[c] * dst, axis=0, keepdims=True))
            tail_c[c] = jnp.broadcast_to(tail, (B_CHUNK, B_KEY_WIDTH))
            dst_ref[...] = dst * decay + contrib
        dqe = jnp.concatenate(dqe_c, axis=0)
        dkd = jnp.concatenate(dkd_c, axis=0)
        dqs = dqm * e_qm + dqe * e_qe
        dk = dkm * e_km + dkd * e_kd
        dq_ref[...] = (dqs * scale).astype(BF16)
        dk_ref[...] = dk.astype(BF16)
        for h in range(B_HEADS):
            dvi = jnp.concatenate([dv_inter[c][h * B_CHUNK:(h + 1) * B_CHUNK] for c in range(CHUNKS_PER_TILE)], axis=0)
            dv_ref[:, h * B_VAL_DIM:(h + 1) * B_VAL_DIM] = (dv_intra[h] + dvi).astype(BF16)
        dd = qs * dqs - kk * dk
        i, j = _row_iota((GLA_TILE, GLA_TILE)), _lane_iota((GLA_TILE, GLA_TILE))
        upper = ((i // B_CHUNK == j // B_CHUNK) & (j >= i)).astype(BF16)
        hi, mid, lo3 = _split3(dd)
        dla_ref[...] = _dot(upper, hi) + _dot(upper, mid) + _dot(upper, lo3) + jnp.concatenate(tail_c, axis=0)

    def body(q_ref, k_ref, cum_ref, v_ref, do_ref, st_all_ref, *rest):
        g_refs, (dq_ref, dk_ref, dv_ref, dla_ref) = rest[:nw], rest[nw:nw + 4]
        rv_refs, (dst_ref, send_sems, recv_sems) = rest[nw + 4:2 * nw + 4], rest[2 * nw + 4:]
        x, y, c = _my_place()

        def wcopy(a, r):
            dx, dy, dc = FLIPS[r]
            return pltpu.make_async_remote_copy(
                src_ref=g_refs[a].at[4 * (x ^ dx) + 2 * (y ^ dy) + (c ^ dc)], dst_ref=rv_refs[a].at[r],
                send_sem=send_sems.at[a, r], recv_sem=recv_sems.at[a, r],
                device_id=(x ^ dx, y ^ dy, c ^ dc), device_id_type=MESH)

        @pl.when(pl.program_id(0) == 0)
        def _():
            dst_ref[...] = jnp.zeros_like(dst_ref)
            for a in range(nw):
                for r in range(len(FLIPS)):
                    wcopy(a, r).start()

        for b in range(B):
            one_sequence(*[r.at[b] for r in (q_ref, k_ref, cum_ref, v_ref, do_ref, st_all_ref,
                                             dq_ref, dk_ref, dv_ref, dla_ref, dst_ref)])

        @pl.when(pl.program_id(0) == nt - 1)
        def _():
            for a in range(nw):
                for r in range(len(FLIPS)):
                    wcopy(a, r).wait()

    def rows(w):
        return pl.BlockSpec((B, GLA_TILE, w), lambda t: (0, nt - 1 - t, 0))

    seq = lambda a: a.reshape(B, S, a.shape[-1])
    res = pl.pallas_call(
        body, name="gla_bwd", grid=(nt,),
        in_specs=[rows(B_KEY_WIDTH), rows(B_KEY_WIDTH), rows(B_KEY_WIDTH), rows(B_WIDTH), rows(B_WIDTH),
                  pl.BlockSpec((B, CHUNKS_PER_TILE, B_VAL_DIM, B_KEY_WIDTH), lambda t: (0, nt - 1 - t, 0, 0))]
                 + _any_specs(nw),
        out_specs=[rows(B_KEY_WIDTH), rows(B_KEY_WIDTH), rows(B_WIDTH), rows(B_KEY_WIDTH)] + _any_specs(nw),
        out_shape=[jax.ShapeDtypeStruct((B, S, B_KEY_WIDTH), BF16), jax.ShapeDtypeStruct((B, S, B_KEY_WIDTH), BF16),
                   jax.ShapeDtypeStruct((B, S, B_WIDTH), BF16), jax.ShapeDtypeStruct((B, S, B_KEY_WIDTH), F32)]
                  + [jax.ShapeDtypeStruct((len(FLIPS), *g.shape[1:]), g.dtype) for g in wgrads],
        scratch_shapes=[pltpu.VMEM((B, B_VAL_DIM, B_KEY_WIDTH), F32),
                        pltpu.SemaphoreType.DMA((nw, len(FLIPS))), pltpu.SemaphoreType.DMA((nw, len(FLIPS)))],
        compiler_params=_params("arbitrary"),
    )(seq(q), seq(k), seq(cum), seq(vb), seq(do), st_all.reshape(B, S // B_CHUNK, B_VAL_DIM, B_KEY_WIDTH), *wgrads)
    return [a.reshape(T, a.shape[-1]) for a in res[:4]], res[4:]


def _merge(x2, tgt2, attn, za, o_gla, zb, ga, gb, w_oa_sh, w_ob_sh, w_o, g_gla, g_final):
    T = x2.shape[0]
    tm = 256
    last = T // tm - 1

    def body(x_ref, tgt_ref, attn_ref, za_ref, og_ref, zb_ref, ga_ref, gb_ref,
             woa_sh_ref, wob_sh_ref, wo_ref, gg_ref, gf_ref,
             dxres_ref, dattn_ref, dog_ref, dza_ref, dzb_ref, dga_ref, dgb_ref,
             dwo_ref, dwoa_ref, dwob_ref, small_ref,
             awo_ref, awoa_ref, awob_ref, agf_ref, agg_ref, loss_ref, woa_ref, wob_ref):
        @pl.when(pl.program_id(0) == 0)
        def _():
            for r in (awo_ref, awoa_ref, awob_ref, agf_ref, agg_ref, loss_ref):
                r[...] = jnp.zeros_like(r)
            for j in range(N_DEV):
                woa_ref[:, j * SHARD_OUT:(j + 1) * SHARD_OUT] = woa_sh_ref[j]
                wob_ref[:, j * SHARD_OUT:(j + 1) * SHARD_OUT] = wob_sh_ref[j]

        za_v = za_ref[...]
        sig_za = _sigmoid_tanh(za_v)
        silu_a = za_v * sig_za
        attn_v = attn_ref[...]
        oa = (attn_v * silu_a).astype(BF16)
        ya = _dot(oa, woa_ref[...])
        og = og_ref[...]
        zb_v = zb_ref[...]
        sig_zb = _sigmoid_tanh(zb_v)
        silu_b = zb_v * sig_zb
        gg = gg_ref[...]
        on_parts, rinv_parts = [], []
        for h in range(B_HEADS):
            seg = og[:, h * B_VAL_DIM:(h + 1) * B_VAL_DIM]
            rinv = lax.rsqrt(jnp.mean(seg * seg, axis=-1, keepdims=True) + NORM_EPS)
            rinv_parts.append(rinv)
            on_parts.append(seg * rinv)
        on = jnp.concatenate(on_parts, axis=1)
        obn = on * gg
        ob = (obn * silu_b).astype(BF16)
        yb = _dot(ob, wob_ref[...])
        sig_a, sig_b = _sigmoid_tanh(ga_ref[...]), _sigmoid_tanh(gb_ref[...])
        merged = (sig_a * ya + sig_b * yb).astype(BF16)
        out = x_ref[...] + _dot(merged, wo_ref[...])
        rf = lax.rsqrt(jnp.mean(out * out, axis=-1, keepdims=True) + NORM_EPS)
        nrm = out * rf
        gf = gf_ref[...]
        err = nrm * gf - tgt_ref[...]
        loss_ref[...] += jnp.sum(err * err) * (0.5 / D_MODEL)

        dy = err * (1.0 / D_MODEL)
        agf_ref[...] += jnp.sum(dy * nrm, axis=0, keepdims=True)
        dn = dy * gf
        dout = rf * (dn - nrm * jnp.mean(dn * nrm, axis=-1, keepdims=True))
        dxres_ref[...] = dout
        dout_b = dout.astype(BF16)
        dmerged = _dot_nt(dout_b, wo_ref[...])
        awo_ref[...] += _dot_tn(merged, dout_b)
        dya = dmerged * sig_a
        dyb = dmerged * sig_b
        dga_ref[...] = (dmerged * ya * sig_a * (1.0 - sig_a)).astype(BF16)
        dgb_ref[...] = (dmerged * yb * sig_b * (1.0 - sig_b)).astype(BF16)
        dya_b, dyb_b = dya.astype(BF16), dyb.astype(BF16)
        awoa_ref[...] += _dot_tn(oa, dya_b)
        awob_ref[...] += _dot_tn(ob, dyb_b)
        doa = _dot_nt(dya_b, woa_ref[...])
        dattn_ref[...] = (doa * silu_a).astype(BF16)
        dza_ref[...] = (doa * attn_v * (sig_za * (1.0 + za_v * (1.0 - sig_za)))).astype(BF16)
        dob = _dot_nt(dyb_b, wob_ref[...])
        dzb_ref[...] = (dob * obn * (sig_zb * (1.0 + zb_v * (1.0 - sig_zb)))).astype(BF16)
        dobn = dob * silu_b
        agg_ref[...] += jnp.sum(dobn * on, axis=0, keepdims=True)
        don = dobn * gg
        for h in range(B_HEADS):
            sl = slice(h * B_VAL_DIM, (h + 1) * B_VAL_DIM)
            don_h, on_h = don[:, sl], on[:, sl]
            dog_ref[:, sl] = (rinv_parts[h] * (don_h - on_h * jnp.mean(don_h * on_h, axis=-1, keepdims=True))).astype(BF16)

        @pl.when(pl.program_id(0) == last)
        def _():
            for j in range(N_DEV):
                dwo_ref[j] = awo_ref[j * SHARD_OUT:(j + 1) * SHARD_OUT, :].astype(BF16)
                dwoa_ref[j] = awoa_ref[:, j * SHARD_OUT:(j + 1) * SHARD_OUT].astype(BF16)
                dwob_ref[j] = awob_ref[:, j * SHARD_OUT:(j + 1) * SHARD_OUT].astype(BF16)
            small_ref[...] = jnp.zeros_like(small_ref)
            _put_rows(small_ref, SMALL_G_FINAL, agf_ref[...])
            _put_rows(small_ref, SMALL_G_GLA, agg_ref[...])
            small_ref[SMALL_LOSS:SMALL_LOSS + 1, :] = loss_ref[...]

    def rows(w):
        return pl.BlockSpec((tm, w), lambda i: (i, 0))

    def whole(shape):
        nd = len(shape)
        return pl.BlockSpec(shape, lambda i: (0,) * nd)

    outs = [((T, D_MODEL), F32, rows(D_MODEL)), ((T, A_WIDTH), BF16, rows(A_WIDTH)), ((T, B_WIDTH), BF16, rows(B_WIDTH)),
            ((T, A_WIDTH), BF16, rows(A_WIDTH)), ((T, B_WIDTH), BF16, rows(B_WIDTH)),
            ((T, D_MODEL), BF16, rows(D_MODEL)), ((T, D_MODEL), BF16, rows(D_MODEL)),
            ((N_DEV, SHARD_OUT, D_MODEL), BF16, whole((N_DEV, SHARD_OUT, D_MODEL))),
            ((N_DEV, A_WIDTH, SHARD_OUT), BF16, whole((N_DEV, A_WIDTH, SHARD_OUT))),
            ((N_DEV, B_WIDTH, SHARD_OUT), BF16, whole((N_DEV, B_WIDTH, SHARD_OUT))),
            ((SMALL_SINKS, LANES), F32, whole((SMALL_SINKS, LANES)))]
    return pl.pallas_call(
        body, name="merge", grid=(T // tm,),
        in_specs=[rows(D_MODEL), rows(D_MODEL), rows(A_WIDTH), rows(A_WIDTH), rows(B_WIDTH), rows(B_WIDTH),
                  rows(D_MODEL), rows(D_MODEL),
                  _const_spec((N_DEV, A_WIDTH, SHARD_OUT)), _const_spec((N_DEV, B_WIDTH, SHARD_OUT)),
                  _const_spec((D_MODEL, D_MODEL)), _const_spec((1, B_WIDTH)), _const_spec((1, D_MODEL))],
        out_specs=[o[2] for o in outs],
        out_shape=[jax.ShapeDtypeStruct(o[0], o[1]) for o in outs],
        scratch_shapes=[pltpu.VMEM((D_MODEL, D_MODEL), F32), pltpu.VMEM((A_WIDTH, D_MODEL), F32),
                        pltpu.VMEM((B_WIDTH, D_MODEL), F32), pltpu.VMEM((1, D_MODEL), F32), pltpu.VMEM((1, B_WIDTH), F32),
                        pltpu.VMEM((1, LANES), F32), pltpu.VMEM((A_WIDTH, D_MODEL), BF16),
                        pltpu.VMEM((B_WIDTH, D_MODEL), BF16)],
        compiler_params=_params("arbitrary"),
    )(x2, tgt2, attn, za, o_gla, zb, ga, gb, w_oa_sh, w_ob_sh, w_o, g_gla, g_final)


def _in_proj_bwd(x2, dxres, cosf, sinf, g_in, wt_pad, wa_pad, parts):
    T = x2.shape[0]
    tm = 256
    last = T // tm - 1
    base = SMALL_G_IN

    def body(x_ref, dxres_ref, cos_ref, sin_ref, g_ref, wt_ref, wa_ref,
             dq_ref, dkv_ref, dza_ref, dqb_ref, dkb_ref, dvb_ref, dzb_ref, dla_ref, u_ref, alr_ref, dga_ref, dgb_ref,
             dx_ref, dsh_ref, small_ref, dproj_ref, agin_ref, aba_ref, awa_ref):
        @pl.when(pl.program_id(0) == 0)
        def _():
            for r in (agin_ref, aba_ref, awa_ref):
                r[...] = jnp.zeros_like(r)

        cos, nsin = cos_ref[...], -sin_ref[...]
        for s in range(A_WIDTH // LANES):
            sl = slice(s * LANES, (s + 1) * LANES)
            dproj_ref[:, sl] = _rope_slab(dq_ref[:, sl].astype(F32), cos, nsin).astype(BF16)
        dproj_ref[:, QKV_K:QKV_V] = _rope_slab(dkv_ref[:, 0:LANES].astype(F32), cos, nsin).astype(BF16)
        dproj_ref[:, QKV_V:QKV_W] = dkv_ref[:, LANES:]

        def put(name, val):
            a, b = SEG[name]
            dproj_ref[:, a:b] = val

        put("za", dza_ref[...])
        put("qb", dqb_ref[...])
        put("kb", dkb_ref[...])
        put("vb", dvb_ref[...])
        put("zb", dzb_ref[...])
        put("ga", dga_ref[...])
        put("gb", dgb_ref[...])
        du = dla_ref[...] * (1.0 / B_GATE_TEMP) * _sigmoid(-u_ref[...])
        aba_ref[...] += jnp.sum(du, axis=0, keepdims=True)
        du_b = du.astype(BF16)
        awa_ref[...] += _dot_tn(alr_ref[...], du_b)
        put("alr", _dot_nt(du_b, wa_ref[...]).astype(BF16))

        for j in range(N_DEV):
            col = (j % 2) * SHARD_PAD
            for a, b in _shard_pad_cols(j):
                dsh_ref[j // 2, :, col:col + b - a] = dproj_ref[:, a:b]
                col += b - a
            dsh_ref[j // 2, :, col:(j % 2 + 1) * SHARD_PAD] = jnp.zeros((tm, SHARD_PAD - SHARD_IN), BF16)

        dh = _dot(dproj_ref[...], wt_ref[...])
        x = x_ref[...]
        r = lax.rsqrt(jnp.mean(x * x, axis=-1, keepdims=True) + NORM_EPS)
        nrm = x * r
        agin_ref[...] += jnp.sum(dh * nrm, axis=0, keepdims=True)
        dn = dh * g_ref[...]
        dx_ref[...] = dxres_ref[...] + r * (dn - nrm * jnp.mean(dn * nrm, axis=-1, keepdims=True))

        @pl.when(pl.program_id(0) == last)
        def _():
            small_ref[...] = jnp.zeros_like(small_ref)
            _put_rows(small_ref, SMALL_G_IN - base, agin_ref[...])
            _put_rows(small_ref, SMALL_B_ALPHA - base, aba_ref[...])
            for half in range(B_KEY_WIDTH // LANES):
                r0 = SMALL_W_ALPHA - base + half * B_GATE_RANK
                small_ref[r0:r0 + B_GATE_RANK, :] = awa_ref[0:B_GATE_RANK, half * LANES:(half + 1) * LANES]

    def rows(w):
        return pl.BlockSpec((tm, w), lambda i: (i, 0))

    names = ["dq", "dkv", "dza", "dqb", "dkb", "dvb", "dzb", "dla", "u", "alr", "dga", "dgb"]
    return pl.pallas_call(
        body, name="in_proj_bwd", grid=(T // tm,),
        in_specs=[rows(D_MODEL), rows(D_MODEL), rows(LANES), rows(LANES), _const_spec((1, D_MODEL)),
                  _const_spec((D_IN_PAD, D_MODEL)), _const_spec((RANK_PAD, B_KEY_WIDTH))]
                 + [rows(parts[n].shape[1]) for n in names],
        out_specs=[rows(D_MODEL), pl.BlockSpec((N_CHIPS, tm, 2 * SHARD_PAD), lambda i: (0, i, 0)),
                   pl.BlockSpec((SMALL_ROWS - base, LANES), lambda i: (0, 0))],
        out_shape=[jax.ShapeDtypeStruct((T, D_MODEL), F32), jax.ShapeDtypeStruct((N_CHIPS, T, 2 * SHARD_PAD), BF16),
                   jax.ShapeDtypeStruct((SMALL_ROWS - base, LANES), F32)],
        scratch_shapes=[pltpu.VMEM((tm, D_IN_PAD), BF16), pltpu.VMEM((1, D_MODEL), F32), pltpu.VMEM((1, B_KEY_WIDTH), F32),
                        pltpu.VMEM((RANK_PAD, B_KEY_WIDTH), F32)],
        compiler_params=_params("arbitrary"),
    )(x2, dxres, cosf, sinf, g_in, wt_pad, wa_pad, *[parts[n] for n in names])


FLIPS = [(dx, dy, dc) for dx in (0, 1) for dy in (0, 1) for dc in (0, 1)][1:]


def _my_place():
    return lax.axis_index("x"), lax.axis_index("y"), lax.axis_index("c")


def _any_specs(n):
    return [pl.BlockSpec(memory_space=pl.ANY)] * n


def _gather_first(shards, pos_col):
    n = len(shards)
    T = pos_col.shape[0]
    rows_per_pass = math.gcd(T, 512)
    invf, sign = _rope_lane_constants()

    def body(*refs):
        ins, (pos_ref, invf_ref, sign_ref) = refs[:n], refs[n:n + 3]
        outs, (cos_ref, sin_ref) = refs[n + 3:2 * n + 3], refs[2 * n + 3:2 * n + 5]
        send_sems, recv_sems, local_sems = refs[2 * n + 5:]
        x, y, c = _my_place()
        me, sibling = (x, y, c), (x, y, 1 - c)
        chips = [(1 - x, y), (x, 1 - y), (1 - x, 1 - y)]

        def block(a, px, py, pc):
            return outs[a].at[4 * px + 2 * py + pc]

        def copy(a, k, blk, to, src=None):
            return pltpu.make_async_remote_copy(
                src_ref=block(a, *blk) if src is None else src, dst_ref=block(a, *blk),
                send_sem=send_sems.at[a, k], recv_sem=recv_sems.at[a, k], device_id=to, device_id_type=MESH)

        mine = [pltpu.make_async_copy(ins[a], block(a, *me), local_sems.at[a]) for a in range(n)]
        for cp in mine:
            cp.start()
        first = []
        for a in range(n):
            first.append(copy(a, 0, me, sibling, src=ins[a]))
            first += [copy(a, 1 + j, me, (*chip, c), src=ins[a]) for j, chip in enumerate(chips)]
        for cp in first:
            cp.start()

        def tables(i, carry):
            rows = pl.ds(pl.multiple_of(i * rows_per_pass, rows_per_pass), rows_per_pass)
            ang = pos_ref[rows, :].astype(F32) * invf_ref[...]
            cos_ref[rows, :] = jnp.cos(ang)
            sin_ref[rows, :] = jnp.sin(ang) * sign_ref[...]
            return carry

        lax.fori_loop(0, T // rows_per_pass, tables, 0)

        passed = []
        for j, chip in enumerate(chips):
            for a in range(n):
                copy(a, 1 + j, (*chip, c), me).wait_recv()
                fwd = copy(a, 4 + j, (*chip, c), sibling)
                fwd.start()
                passed.append(fwd)
        for a in range(n):
            copy(a, 0, sibling, me).wait_recv()
            for j, chip in enumerate(chips):
                copy(a, 4 + j, (*chip, 1 - c), me).wait_recv()
        for cp in first + passed:
            cp.wait_send()
        for cp in mine:
            cp.wait()

    vmem = pl.BlockSpec(memory_space=pltpu.VMEM)
    res = pl.pallas_call(
        body, name="gather_weights",
        in_specs=_any_specs(n) + [vmem] * 3, out_specs=_any_specs(n) + [vmem] * 2,
        out_shape=[jax.ShapeDtypeStruct((N_DEV, *s.shape), s.dtype) for s in shards]
                  + [jax.ShapeDtypeStruct((T, LANES), F32)] * 2,
        scratch_shapes=[pltpu.SemaphoreType.DMA((n, 7)), pltpu.SemaphoreType.DMA((n, 7)), pltpu.SemaphoreType.DMA((n,))],
        compiler_params=pltpu.CompilerParams(vmem_limit_bytes=V7X_VMEM_LIMIT),
    )(*shards, pos_col, invf, sign)
    return res[:n], res[n], res[n + 1]


def _w_in_grad_rs(h, dsh, chip_order, small):
    T = h.shape[0]
    tk = math.gcd(T, 2048)
    nk = T // tk
    chip_flips = [(1, 1), (1, 0), (0, 1)]
    n_steps = len(chip_flips) + 1
    SIB = len(chip_flips)
    k_finish = min(1, nk - 1)

    def body(order_ref, h_ref, d_ref, s_ref, own_ref, recv_ref, sall_ref,
             acc_ref, keep_ref, pre_ref, to_sib_ref, to_chip_ref,
             sib_send, sib_recv, chip_send, chip_recv, ssend_sems, srecv_sems, local_sem):
        i, kk = pl.program_id(0), pl.program_id(1)
        x, y, c = _my_place()
        my_dev = 4 * x + 2 * y + c

        def small_copy(r, slot):
            dx, dy, dc = FLIPS[r]
            return pltpu.make_async_remote_copy(
                src_ref=s_ref, dst_ref=sall_ref.at[slot], send_sem=ssend_sems.at[r], recv_sem=srecv_sems.at[r],
                device_id=(x ^ dx, y ^ dy, c ^ dc), device_id_type=MESH)

        keep_small = pltpu.make_async_copy(s_ref, sall_ref.at[my_dev], local_sem)

        def sib_copy(t):
            dst = recv_ref.at[SIB] if t == SIB else pre_ref.at[t]
            return pltpu.make_async_remote_copy(
                src_ref=to_sib_ref.at[t], dst_ref=dst, send_sem=sib_send.at[t], recv_sem=sib_recv.at[t],
                device_id=(x, y, 1 - c), device_id_type=MESH)

        def chip_copy(t):
            dx, dy = chip_flips[t]
            return pltpu.make_async_remote_copy(
                src_ref=to_chip_ref.at[t], dst_ref=recv_ref.at[t], send_sem=chip_send.at[t], recv_sem=chip_recv.at[t],
                device_id=(x ^ dx, y ^ dy, c), device_id_type=MESH)

        def halves():
            first, second = acc_ref[0:SHARD_PAD, :], acc_ref[SHARD_PAD:2 * SHARD_PAD, :]
            return jnp.where(c == 0, first, second), jnp.where(c == 0, second, first)

        @pl.when((i == 0) & (kk == 0))
        def _():
            keep_small.start()
            for r in range(len(FLIPS)):
                small_copy(r, my_dev).start()

        @pl.when(kk == 0)
        def _():
            acc_ref[...] = jnp.zeros_like(acc_ref)

        acc_ref[...] += _dot_tn(d_ref[...], h_ref[...])

        for t in range(len(chip_flips)):
            @pl.when((i == t + 1) & (kk == k_finish))
            def _(t=t):
                sib_copy(t).wait_recv()
                to_chip_ref[t] = (keep_ref[...] + pre_ref[t].astype(F32)).astype(BF16)
                chip_copy(t).start()

        for t in range(len(chip_flips)):
            @pl.when((i == t) & (kk == nk - 1))
            def _(t=t):
                mine, theirs = halves()
                to_sib_ref[t] = theirs.astype(BF16)
                sib_copy(t).start()
                keep_ref[...] = mine

        @pl.when((i == n_steps - 1) & (kk == nk - 1))
        def _():
            mine, theirs = halves()
            own_ref[...] = mine
            to_sib_ref[SIB] = theirs.astype(BF16)
            sib_copy(SIB).start()
            for t in range(len(chip_flips)):
                sib_copy(t).wait_send()
                chip_copy(t).wait_send()
                chip_copy(t).wait_recv()
            sib_copy(SIB).wait_send()
            sib_copy(SIB).wait_recv()
            for r, (dx, dy, dc) in enumerate(FLIPS):
                small_copy(r, 4 * (x ^ dx) + 2 * (y ^ dy) + (c ^ dc)).wait_recv()
                small_copy(r, my_dev).wait_send()
            keep_small.wait()

    shard = (SHARD_PAD, D_MODEL)
    return pl.pallas_call(
        body, name="w_in_grad_rs",
        grid_spec=pltpu.PrefetchScalarGridSpec(
            num_scalar_prefetch=1, grid=(n_steps, nk),
            in_specs=[pl.BlockSpec((tk, D_MODEL), lambda i, kk, order: (kk, 0)),
                      pl.BlockSpec((None, tk, 2 * SHARD_PAD), lambda i, kk, order: (order[i], kk, 0)),
                      pl.BlockSpec(memory_space=pl.ANY)],
            out_specs=[pl.BlockSpec(shard, lambda i, kk, order: (0, 0)),
                       pl.BlockSpec(memory_space=pl.ANY), pl.BlockSpec(memory_space=pl.ANY)],
            scratch_shapes=[pltpu.VMEM((2 * SHARD_PAD, D_MODEL), F32), pltpu.VMEM(shard, F32),
                            pltpu.VMEM((SIB, *shard), BF16), pltpu.VMEM((SIB + 1, *shard), BF16),
                            pltpu.VMEM((SIB, *shard), BF16),
                            pltpu.SemaphoreType.DMA((SIB + 1,)), pltpu.SemaphoreType.DMA((SIB + 1,)),
                            pltpu.SemaphoreType.DMA((SIB,)), pltpu.SemaphoreType.DMA((SIB,)),
                            pltpu.SemaphoreType.DMA((7,)), pltpu.SemaphoreType.DMA((7,)), pltpu.SemaphoreType.DMA]),
        out_shape=[jax.ShapeDtypeStruct(shard, F32),
                   jax.ShapeDtypeStruct((SIB + 1, *shard), BF16),
                   jax.ShapeDtypeStruct((N_DEV, *small.shape), F32)],
        compiler_params=_params("arbitrary", "arbitrary"),
    )(chip_order, h, dsh, small)


def _adam_math(w, g, m, v):
    m_new = ADAM_B1 * m + (1.0 - ADAM_B1) * g
    v_new = ADAM_B2 * v + (1.0 - ADAM_B2) * (g * g)
    m_hat = m_new / (1.0 - ADAM_B1 ** ADAM_STEP)
    v_hat = v_new / (1.0 - ADAM_B2 ** ADAM_STEP)
    delta = -ADAM_LR * (m_hat / (jnp.sqrt(v_hat) + ADAM_EPS) + ADAM_WD * w)
    return delta, m_new, v_new


def _adam_big(name, own, own_idx, recv, w, m, v):
    rw, cw = w.shape
    rp = own.shape[1]
    steps = 8
    by_cols = rp != rw
    blk_w = (rw, cw // steps) if by_cols else (rw // steps, cw)
    blk_g = (rp, cw // steps) if by_cols else (rw // steps, cw)
    at = (lambda i: (0, i)) if by_cols else (lambda i: (i, 0))

    def body(idx_ref, o_ref, r_ref, w_ref, m_ref, v_ref, g_ref, d_ref, mo_ref, vo_ref):
        g = o_ref[...].astype(F32)
        for r in range(recv.shape[0]):
            g = g + r_ref[r].astype(F32)
        g = g[0:blk_w[0], :]
        g_ref[...] = g
        d_ref[...], mo_ref[...], vo_ref[...] = _adam_math(w_ref[...], g, m_ref[...], v_ref[...])

    spec = pl.BlockSpec(blk_w, lambda i, idx_ref: at(i))
    return pl.pallas_call(
        body, name=name,
        grid_spec=pltpu.PrefetchScalarGridSpec(
            num_scalar_prefetch=1, grid=(steps,),
            in_specs=[pl.BlockSpec((None, *blk_g), lambda i, idx_ref: (idx_ref[0], *at(i))),
                      pl.BlockSpec((recv.shape[0], *blk_g), lambda i, idx_ref: (0, *at(i))), spec, spec, spec],
            out_specs=[spec] * 4),
        out_shape=[jax.ShapeDtypeStruct((rw, cw), F32)] * 4,
        compiler_params=_params("parallel"),
    )(own_idx, own, recv, w, m, v)


def _adam_small(small_all, params):
    flat = [a for triple in params for a in triple]
    n_par = len(params)

    def body(s_ref, *refs):
        ins, outs, loss_ref = refs[:3 * n_par], refs[3 * n_par:-1], refs[-1]
        g_slab = s_ref[0]
        for dev in range(1, N_DEV):
            g_slab = g_slab + s_ref[dev]
        loss_ref[...] = g_slab[SMALL_LOSS:SMALL_LOSS + 1, :]
        dev = 4 * lax.axis_index("x") + 2 * lax.axis_index("y") + lax.axis_index("c")
        alpha_full = jnp.concatenate([g_slab[SMALL_W_ALPHA + half * B_GATE_RANK:SMALL_W_ALPHA + (half + 1) * B_GATE_RANK]
                                      for half in range(B_KEY_WIDTH // LANES)], axis=1)
        alpha_mine = pltpu.roll(alpha_full, (B_KEY_WIDTH - dev * SHARD_ALPHA) % B_KEY_WIDTH, 1)[:, 0:SHARD_ALPHA]
        grads = [_take_rows(g_slab, SMALL_G_IN, D_MODEL // LANES), _take_rows(g_slab, SMALL_G_FINAL, D_MODEL // LANES),
                 _take_rows(g_slab, SMALL_G_GLA, B_WIDTH // LANES), _take_rows(g_slab, SMALL_B_ALPHA, B_KEY_WIDTH // LANES),
                 g_slab[SMALL_SINKS:SMALL_SINKS + 1, 0:A_HEADS], alpha_mine]
        for i, g in enumerate(grads):
            w_ref, m_ref, v_ref = ins[3 * i:3 * i + 3]
            delta, m_new, v_new = _adam_math(w_ref[...], g, m_ref[...], v_ref[...])
            outs[4 * i][...] = g
            outs[4 * i + 1][...] = delta
            outs[4 * i + 2][...] = m_new
            outs[4 * i + 3][...] = v_new

    res = pl.pallas_call(
        body, name="adam_small",
        out_shape=[jax.ShapeDtypeStruct(t[0].shape, F32) for t in params for _ in range(4)]
                  + [jax.ShapeDtypeStruct((1, LANES), F32)],
    )(small_all, *flat)
    return [res[4 * i:4 * i + 4] for i in range(n_par)], res[-1]


def _local_step(x, cosf, sinf, loss_target, g_in, wt_sh, wa_pad, b_alpha, sinks, g_gla, out_shards, g_final, chip_order):
    B, S, _ = x.shape
    T = B * S
    x2 = x.reshape(T, D_MODEL)
    tgt2 = loss_target.reshape(T, D_MODEL)
    f, (g_woa, g_wob, g_wo) = _in_proj(x2, cosf, sinf, g_in, wt_sh, wa_pad, b_alpha, out_shards)
    w_o = g_wo.reshape(D_MODEL, D_MODEL)
    sink_row = jnp.repeat(sinks, WINDOW).reshape(1, ATT_ROWS)
    sink_col = sink_row.reshape(ATT_ROWS, 1)
    attn, lse = _attn_fwd(f["qkv"], sink_row, B, S)
    o_gla, st_all = _gla_fwd(f["q"], f["k"], f["cum"], f["vb"], B, S)
    (dxres, dattn, dog, dza, dzb, dga, dgb, dw_o, dw_oa, dw_ob, small_a) = _merge(
        x2, tgt2, attn, f["za"], o_gla, f["zb"], f["ga"], f["gb"], g_woa, g_wob, w_o, g_gla, g_final)
    dq, dkv, dsink = _attn_bwd(f["qkv"], dattn, attn, lse, sink_col, B, S)
    (dqb, dkb, dvb, dla), (rv_o, rv_oa, rv_ob) = _gla_bwd(f["q"], f["k"], f["cum"], f["vb"], dog, st_all, B, S,
                                                        [dw_o, dw_oa, dw_ob])
    parts = dict(dq=dq, dkv=dkv, dza=dza, dqb=dqb, dkb=dkb, dvb=dvb, dzb=dzb, dla=dla, u=f["u"], alr=f["alr"],
                 dga=dga, dgb=dgb)
    dx, dsh, small_c = _in_proj_bwd(x2, dxres, cosf, sinf, g_in, f["wt_pad"], wa_pad, parts)
    small = jnp.concatenate([small_a, dsink, small_c], axis=0)
    own_in, rv_in, small_all = _w_in_grad_rs(f["h"], dsh, chip_order, small)
    return dict(grad_x=dx.reshape(B, S, D_MODEL), own_in=own_in, rv_in=rv_in,
                own_o=dw_o, rv_o=rv_o, own_oa=dw_oa, rv_oa=rv_oa, own_ob=dw_ob, rv_ob=rv_ob, small_all=small_all)


def kernel(x, positions, g_in, w_in, w_alpha_up, b_alpha, attn_sinks, g_gla_norm, w_out_a, w_out_b, w_o, g_final, loss_target, m_g_in, m_w_in, m_w_alpha_up, m_b_alpha, m_attn_sinks, m_g_gla_norm, m_w_out_a, m_w_out_b, m_w_o, m_g_final, v_g_in, v_w_in, v_w_alpha_up, v_b_alpha, v_attn_sinks, v_g_gla_norm, v_w_out_a, v_w_out_b, v_w_o, v_g_final):
    xi, yi, ci = _my_place()
    dev_idx = (4 * xi + 2 * yi + ci).reshape(1).astype(jnp.int32)
    chip = 2 * xi + yi
    chip_order = jnp.stack([chip ^ 3, chip ^ 2, chip ^ 1, chip]).astype(jnp.int32)

    (g_win, g_wa), cosf, sinf = _gather_first(
        [jnp.pad(w_in[0].T.astype(BF16), ((0, SHARD_PAD - SHARD_IN), (0, 0))), w_alpha_up[0].astype(BF16)],
        positions.reshape(-1, 1))
    wt_sh = g_win.reshape(N_DEV * SHARD_PAD, D_MODEL)
    wa_pad = jnp.pad(jnp.concatenate([g_wa[j] for j in range(N_DEV)], axis=1), ((0, RANK_PAD - B_GATE_RANK), (0, 0)))

    r = _local_step(x, cosf, sinf, loss_target, g_in, wt_sh, wa_pad, b_alpha, attn_sinks[0], g_gla_norm,
                    [w_out_a[0].astype(BF16), w_out_b[0].astype(BF16), w_o[0].astype(BF16)],
                    g_final.reshape(1, D_MODEL), chip_order)

    first = jnp.zeros((1,), jnp.int32)
    big = [_adam_big("adam_w_in", r["own_in"][None], first, r["rv_in"], w_in[0].T, m_w_in[0].T, v_w_in[0].T),
           _adam_big("adam_w_out_a", r["own_oa"], dev_idx, r["rv_oa"], w_out_a[0], m_w_out_a[0], v_w_out_a[0]),
           _adam_big("adam_w_out_b", r["own_ob"], dev_idx, r["rv_ob"], w_out_b[0], m_w_out_b[0], v_w_out_b[0]),
           _adam_big("adam_w_o", r["own_o"], dev_idx, r["rv_o"], w_o[0], m_w_o[0], v_w_o[0])]
    row = lambda a: a.reshape(1, D_MODEL)
    big[0] = [a.T for a in big[0]]
    (s_in, s_final, s_gla, s_ba, s_sinks, s_wa), loss_row = _adam_small(r["small_all"], [
        (g_in, m_g_in, v_g_in), (row(g_final), row(m_g_final), row(v_g_final)),
        (g_gla_norm, m_g_gla_norm, v_g_gla_norm), (b_alpha, m_b_alpha, v_b_alpha),
        (attn_sinks, m_attn_sinks, v_attn_sinks), (w_alpha_up[0], m_w_alpha_up[0], v_w_alpha_up[0])])

    def group(i):
        return (s_in[i], big[0][i][None], s_wa[i][None], s_ba[i], s_sinks[i], s_gla[i], big[1][i][None], big[2][i][None],
                big[3][i][None], s_final[i].reshape(D_MODEL))

    return (loss_row[0, 0], r["grad_x"], *group(0), *group(1), *group(2), *group(3))
```

```python
import functools
import math

import numpy as np
import jax
import jax.numpy as jnp
from jax import lax
from jax.experimental import pallas as pl
from jax.experimental.pallas import tpu as pltpu

F32 = jnp.float32
BF16 = jnp.bfloat16
MESH = pl.DeviceIdType.MESH

D_MODEL = 1024
A_HEADS, A_KV_HEADS, A_HEAD_DIM = 8, 2, 64
A_WIDTH, A_KV_WIDTH = 512, 128
WINDOW = 128
ROPE_THETA = 500000.0
ROPE_DIM = 16
B_HEADS, B_KEY_DIM, B_VAL_DIM = 4, 64, 128
B_KEY_WIDTH, B_WIDTH = 256, 512
B_GATE_RANK = 16
B_GATE_TEMP = 16.0
B_CHUNK = 64
NORM_EPS = 1e-6
NEG_BIG = -1e30
D_IN = 4880
N_DEV = 8
N_CHIPS = 4
ADAM_LR, ADAM_B1, ADAM_B2, ADAM_EPS, ADAM_WD, ADAM_STEP = 0.001, 0.9, 0.999, 1e-08, 0.01, 10

LANES = 128
V7X_VMEM_LIMIT = 56 * 1024 * 1024

RANK_PAD = LANES
SEG = {}
_off = 0
for _name, _w in (("qa", 512), ("ka", 128), ("va", 128), ("za", 512), ("qb", 256), ("kb", 256),
                  ("vb", 512), ("zb", 512), ("alr", RANK_PAD), ("ga", 1024), ("gb", 1024)):
    SEG[_name] = (_off, _off + _w)
    _off += _w
D_IN_PAD = _off
ALR_SRC = SEG["alr"][0]
QKV_K, QKV_V, QKV_W = SEG["ka"][0], SEG["va"][0], SEG["va"][1]
ATT_SCALE = A_HEAD_DIM ** -0.5

SHARD_IN = D_IN // N_DEV
SHARD_PAD = 640
SHARD_OUT = D_MODEL // N_DEV
SHARD_ALPHA = B_KEY_WIDTH // N_DEV

SMALL_G_FINAL, SMALL_G_GLA, SMALL_LOSS, SMALL_SINKS, SMALL_G_IN, SMALL_B_ALPHA, SMALL_W_ALPHA = 0, 8, 12, 16, 24, 32, 40
SMALL_ROWS = 72


def _dot(a, b):
    return jnp.dot(a, b, preferred_element_type=F32)


def _dot_nt(a, b):
    return lax.dot_general(a, b, (((1,), (1,)), ((), ())), preferred_element_type=F32)


def _dot_tn(a, b):
    return lax.dot_general(a, b, (((0,), (0,)), ((), ())), preferred_element_type=F32)


def _sigmoid(z):
    return 1.0 / (1.0 + jnp.exp(-z))


def _sigmoid_tanh(z):
    return 0.5 * jnp.tanh(0.5 * z) + 0.5


def _params(*sem):
    return pltpu.CompilerParams(dimension_semantics=sem, vmem_limit_bytes=V7X_VMEM_LIMIT)


def _const_spec(shape):
    nd = len(shape)
    return pl.BlockSpec(shape, lambda *_: (0,) * nd, pipeline_mode=pl.Buffered(1))


def _lane_iota(shape):
    return lax.broadcasted_iota(jnp.int32, shape, 1)


def _row_iota(shape):
    return lax.broadcasted_iota(jnp.int32, shape, 0)


def _split3(v):
    hi = v.astype(BF16)
    r1 = v - hi.astype(F32)
    mid = r1.astype(BF16)
    lo = (r1 - mid.astype(F32)).astype(BF16)
    return hi, mid, lo


def _put_rows(ref, row0, vec):
    for r in range(vec.shape[1] // LANES):
        ref[row0 + r:row0 + r + 1, :] = vec[:, r * LANES:(r + 1) * LANES]


def _take_rows(slab, row0, n):
    return jnp.concatenate([slab[row0 + r:row0 + r + 1, :] for r in range(n)], axis=1)


def _rope_lane_constants():
    half = ROPE_DIM // 2
    inv_freq = np.exp(-math.log(ROPE_THETA) * np.arange(half, dtype=np.float32) * np.float32(2.0 / ROPE_DIM)).astype(np.float32)
    lane = np.arange(LANES)
    j = lane % A_HEAD_DIM
    invf = np.where(j < ROPE_DIM, inv_freq[j % half], 0.0).astype(np.float32)
    sign = np.where(j < half, -1.0, np.where(j < ROPE_DIM, 1.0, 0.0)).astype(np.float32)
    return jnp.asarray(invf)[None, :], jnp.asarray(sign)[None, :]


def _rope_slab(t, cos, sin_signed):
    first = (_lane_iota(t.shape) % A_HEAD_DIM) < (ROPE_DIM // 2)
    partner = jnp.where(first, pltpu.roll(t, LANES - ROPE_DIM // 2, 1), pltpu.roll(t, ROPE_DIM // 2, 1))
    return t * cos + partner * sin_signed


def _shard_pad_cols(j):
    cut = ALR_SRC + B_GATE_RANK
    shift = RANK_PAD - B_GATE_RANK
    a, b = j * SHARD_IN, (j + 1) * SHARD_IN
    if b <= cut:
        return [(a, b)]
    if a >= cut:
        return [(a + shift, b + shift)]
    return [(a, cut), (cut + shift, b + shift)]


def _in_proj(x2, cosf, sinf, g_in, wt_sh, wa_pad, b_alpha, later_shards):
    T = x2.shape[0]
    tm = math.gcd(T, 512)
    last = T // tm - 1
    nl = len(later_shards)

    def body(x_ref, cos_ref, sin_ref, g_ref, wsh_ref, wa_ref, ba_ref, *rest):
        sh_refs, rest = rest[:nl], rest[nl:]
        (h_ref, qkv_ref, za_ref, q_ref, k_ref, vb_ref, zb_ref, alr_ref, u_ref, cum_ref, ga_ref, gb_ref, wt_out) = rest[:13]
        all_refs, (wt_ref, send_sems, recv_sems, local_sems, wt_sem) = rest[13:13 + nl], rest[13 + nl:]
        wt_copy = pltpu.make_async_copy(wt_ref, wt_out, wt_sem)
        px, py, pc = _my_place()
        my_dev = 4 * px + 2 * py + pc

        def wcopy(a, r, slot):
            dx, dy, dc = FLIPS[r]
            return pltpu.make_async_remote_copy(
                src_ref=sh_refs[a], dst_ref=all_refs[a].at[slot], send_sem=send_sems.at[a, r],
                recv_sem=recv_sems.at[a, r], device_id=(px ^ dx, py ^ dy, pc ^ dc), device_id_type=MESH)

        keep = [pltpu.make_async_copy(sh_refs[a], all_refs[a].at[my_dev], local_sems.at[a]) for a in range(nl)]

        @pl.when(pl.program_id(0) == 0)
        def _():
            for a in range(nl):
                keep[a].start()
                for r in range(len(FLIPS)):
                    wcopy(a, r, my_dev).start()

        @pl.when(pl.program_id(0) == 0)
        def _():
            for j in range(N_DEV):
                src = j * SHARD_PAD
                for a, b in _shard_pad_cols(j):
                    wt_ref[a:b, :] = wsh_ref[src:src + b - a, :]
                    src += b - a
            a, b = SEG["alr"]
            wt_ref[a + B_GATE_RANK:b, :] = jnp.zeros((RANK_PAD - B_GATE_RANK, D_MODEL), BF16)
            wt_copy.start()

        x = x_ref[...]
        r = lax.rsqrt(jnp.mean(x * x, axis=-1, keepdims=True) + NORM_EPS)
        h = (x * r * g_ref[...]).astype(BF16)
        h_ref[...] = h

        def seg(name):
            a, b = SEG[name]
            return _dot_nt(h, wt_ref[a:b, :])

        cos, sin = cos_ref[...], sin_ref[...]
        qa = seg("qa") * ATT_SCALE
        for s in range(A_WIDTH // LANES):
            qkv_ref[:, s * LANES:(s + 1) * LANES] = _rope_slab(qa[:, s * LANES:(s + 1) * LANES], cos, sin).astype(BF16)
        qkv_ref[:, QKV_K:QKV_V] = _rope_slab(seg("ka"), cos, sin).astype(BF16)
        qkv_ref[:, QKV_V:QKV_W] = seg("va").astype(BF16)
        za_ref[...] = seg("za")
        q_ref[...] = seg("qb")
        k_ref[...] = seg("kb")
        vb_ref[...] = seg("vb").astype(BF16)
        zb_ref[...] = seg("zb")
        ga_ref[...] = seg("ga")
        gb_ref[...] = seg("gb")
        alr = seg("alr").astype(BF16)
        alr_ref[...] = alr
        u = _dot(alr, wa_ref[...]) + ba_ref[...]
        u_ref[...] = u
        log_a = (jnp.minimum(u, 0.0) - jnp.log(1.0 + jnp.exp(-jnp.abs(u)))) * (1.0 / B_GATE_TEMP)
        row, col = _row_iota((tm, tm)), _lane_iota((tm, tm))
        tri = ((row // B_CHUNK == col // B_CHUNK) & (col <= row)).astype(BF16)
        hi, mid, lo = _split3(log_a)
        cum_ref[...] = _dot(tri, hi) + _dot(tri, mid) + _dot(tri, lo)

        @pl.when(pl.program_id(0) == last)
        def _():
            for a in range(nl):
                for r, (dx, dy, dc) in enumerate(FLIPS):
                    wcopy(a, r, 4 * (px ^ dx) + 2 * (py ^ dy) + (pc ^ dc)).wait_recv()
                    wcopy(a, r, my_dev).wait_send()
                keep[a].wait()
            wt_copy.wait()

    def rows(w):
        return pl.BlockSpec((tm, w), lambda i: (i, 0))

    outs = [("h", D_MODEL, BF16), ("qkv", QKV_W, BF16), ("za", A_WIDTH, F32), ("q", B_KEY_WIDTH, F32),
            ("k", B_KEY_WIDTH, F32), ("vb", B_WIDTH, BF16), ("zb", B_WIDTH, F32), ("alr", RANK_PAD, BF16),
            ("u", B_KEY_WIDTH, F32), ("cum", B_KEY_WIDTH, F32), ("ga", D_MODEL, F32), ("gb", D_MODEL, F32)]
    res = pl.pallas_call(
        body, name="in_proj", grid=(T // tm,),
        in_specs=[rows(D_MODEL), rows(LANES), rows(LANES), _const_spec((1, D_MODEL)),
                  _const_spec((N_DEV * SHARD_PAD, D_MODEL)), _const_spec((RANK_PAD, B_KEY_WIDTH)),
                  _const_spec((1, B_KEY_WIDTH))] + _any_specs(nl),
        out_specs=[rows(w) for _, w, _ in outs] + _any_specs(1 + nl),
        out_shape=[jax.ShapeDtypeStruct((T, w), dt) for _, w, dt in outs]
                  + [jax.ShapeDtypeStruct((D_IN_PAD, D_MODEL), BF16)]
                  + [jax.ShapeDtypeStruct((N_DEV, *sh.shape), sh.dtype) for sh in later_shards],
        scratch_shapes=[pltpu.VMEM((D_IN_PAD, D_MODEL), BF16),
                        pltpu.SemaphoreType.DMA((nl, len(FLIPS))), pltpu.SemaphoreType.DMA((nl, len(FLIPS))),
                        pltpu.SemaphoreType.DMA((nl,)), pltpu.SemaphoreType.DMA],
        compiler_params=_params("arbitrary"),
    )(x2, cosf, sinf, g_in, wt_sh, wa_pad, b_alpha, *later_shards)
    n_out = len(outs) + 1
    return dict(zip([n for n, _, _ in outs] + ["wt_pad"], res[:n_out])), res[n_out:]


def _dup_kv_head(t, g):
    tf = t.astype(F32)
    keep = (_lane_iota(tf.shape) < A_HEAD_DIM) == (g == 0)
    return jnp.where(keep, tf, pltpu.roll(tf, A_HEAD_DIM, 1)).astype(BF16)


def _stack_heads(t):
    lo = _lane_iota(t.shape) < A_HEAD_DIM
    zero = jnp.zeros_like(t)
    return jnp.concatenate([jnp.where(lo, t, zero), jnp.where(lo, zero, t)], axis=0)


ATT_ROWS = A_HEADS * WINDOW
GROUP_ROWS = ATT_ROWS // A_KV_HEADS
HEADS_PER_GROUP = A_HEADS // A_KV_HEADS


def _band_mask_t(n):
    kj = _row_iota((2 * WINDOW, GROUP_ROWS)) - WINDOW
    qi = _lane_iota((2 * WINDOW, GROUP_ROWS)) % WINDOW
    return (kj <= qi) & (qi - kj < WINDOW) & ((n > 0) | (kj >= 0))


def _stacked_queries(ref, g):
    pairs = range(g * HEADS_PER_GROUP // 2, (g + 1) * HEADS_PER_GROUP // 2)
    return jnp.concatenate([_stack_heads(ref[:, p * LANES:(p + 1) * LANES]) for p in pairs], axis=0)


def _unstack_heads(t, g, ref, dtype):
    lo = _lane_iota((WINDOW, LANES)) < A_HEAD_DIM
    for hh in range(HEADS_PER_GROUP // 2):
        p = g * HEADS_PER_GROUP // 2 + hh
        ref[:, p * LANES:(p + 1) * LANES] = jnp.where(lo, t[2 * hh * WINDOW:(2 * hh + 1) * WINDOW],
                                                       t[(2 * hh + 1) * WINDOW:(2 * hh + 2) * WINDOW]).astype(dtype)


def _attn_fwd(qkv, sink_row, B, S):
    T = B * S
    nb = S // WINDOW

    def body(sink_ref, q_ref, kc_ref, vc_ref, kp_ref, vp_ref, o_ref, lse_ref):
        valid = _band_mask_t(pl.program_id(1))
        k = jnp.concatenate([kp_ref[...], kc_ref[...]], axis=0)
        v = jnp.concatenate([vp_ref[...], vc_ref[...]], axis=0)
        lse_rows = []
        for g in range(A_KV_HEADS):
            kd, vd = _dup_kv_head(k, g), _dup_kv_head(v, g)
            s = jnp.where(valid, _dot_nt(kd, _stacked_queries(q_ref, g)), NEG_BIG)
            sink = sink_ref[:, g * GROUP_ROWS:(g + 1) * GROUP_ROWS]
            m = jnp.maximum(jnp.max(s, axis=0, keepdims=True), sink)
            e = jnp.exp(s - m)
            den = jnp.sum(e, axis=0, keepdims=True) + jnp.exp(sink - m)
            o = _dot_tn((e * (1.0 / den)).astype(BF16), vd)
            _unstack_heads(o, g, o_ref, F32)
            lse = m + jnp.log(den)
            lse_rows += [lse[:, j * WINDOW:(j + 1) * WINDOW] for j in range(HEADS_PER_GROUP)]
        by_head = jnp.concatenate(lse_rows + [jnp.zeros((WINDOW - A_HEADS, WINDOW), F32)], axis=0)
        lse_ref[...] = by_head.T

    def cur(col, w):
        return pl.BlockSpec((WINDOW, w), lambda b, n: (b * nb + n, col))

    def prev(col):
        return pl.BlockSpec((WINDOW, LANES), lambda b, n: (b * nb + jnp.maximum(n - 1, 0), col))

    kcol, vcol = QKV_K // LANES, QKV_V // LANES
    return pl.pallas_call(
        body, name="attn_fwd", grid=(B, nb),
        in_specs=[_const_spec((1, ATT_ROWS)), cur(0, A_WIDTH), cur(kcol, LANES), cur(vcol, LANES), prev(kcol), prev(vcol)],
        out_specs=[cur(0, A_WIDTH), cur(0, LANES)],
        out_shape=[jax.ShapeDtypeStruct((T, A_WIDTH), F32), jax.ShapeDtypeStruct((T, LANES), F32)],
        compiler_params=_params("parallel", "parallel"),
    )(sink_row, qkv, qkv, qkv, qkv, qkv)


ATT_CHUNK = 64


def _chunk_masks(n):
    masks = []
    for half in range(WINDOW // ATT_CHUNK):
        qi = _row_iota((ATT_CHUNK, 2 * WINDOW)) + half * ATT_CHUNK
        kj = _lane_iota((ATT_CHUNK, 2 * WINDOW)) - WINDOW
        masks.append((kj <= qi) & (qi - kj < WINDOW) & ((n > 0) | (kj >= 0)))
    return masks


def _all_stacked_queries(ref):
    return jnp.concatenate([_stacked_queries(ref, g) for g in range(A_KV_HEADS)], axis=0)


def _by_group(fn, lhs, rhs_per_group):
    return jnp.concatenate([fn(lhs[g * GROUP_ROWS:(g + 1) * GROUP_ROWS], rhs_per_group[g])
                            for g in range(A_KV_HEADS)], axis=0)


def _attn_bwd(qkv, do, out, lse, sink_col, B, S):
    T = B * S
    nb = S // WINDOW

    def body(sink_ref, q_ref, kc_ref, vc_ref, kp_ref, vp_ref, do_ref, out_ref, lse_ref,
             dq_ref, dkv_ref, dsink_ref, carry_ref, s_ref, dp_ref, ds_ref, p_ref):
        b, n = pl.program_id(0), pl.program_id(1)
        active = n < nb
        masks = _chunk_masks(jnp.minimum(n, nb - 1))

        @pl.when((b == 0) & (n == 0))
        def _():
            dsink_ref[...] = jnp.zeros_like(dsink_ref)

        k = jnp.concatenate([kp_ref[...], kc_ref[...]], axis=0)
        v = jnp.concatenate([vp_ref[...], vc_ref[...]], axis=0)
        kd = [_dup_kv_head(k, g) for g in range(A_KV_HEADS)]
        vd = [_dup_kv_head(v, g) for g in range(A_KV_HEADS)]
        qs = _all_stacked_queries(q_ref)
        dos = _all_stacked_queries(do_ref)
        s_ref[...] = _by_group(_dot_nt, qs, kd)
        dp_ref[...] = _by_group(_dot_nt, dos, vd)
        lane = _lane_iota((ATT_CHUNK, LANES))
        lo = lane < A_HEAD_DIM
        lane1 = _lane_iota((1, LANES))
        dsink_row = jnp.zeros((1, LANES), F32)
        for c in range(ATT_ROWS // ATT_CHUNK):
            rows = slice(c * ATT_CHUNK, (c + 1) * ATT_CHUNK)
            head, half = divmod(c, WINDOW // ATT_CHUNK)
            qrows = slice(half * ATT_CHUNK, (half + 1) * ATT_CHUNK)
            slab = slice((head // 2) * LANES, (head // 2 + 1) * LANES)
            lse_col = jnp.sum(jnp.where(lane == head, lse_ref[qrows, :], 0.0), axis=-1, keepdims=True)
            prod = do_ref[qrows, slab].astype(F32) * out_ref[qrows, slab]
            mine = lo if head % 2 == 0 else jnp.logical_not(lo)
            delta = jnp.sum(jnp.where(mine, prod, 0.0), axis=-1, keepdims=True)
            s = jnp.where(masks[half], s_ref[rows, :], NEG_BIG)
            prob = jnp.exp(s - lse_col)
            p_ref[rows, :] = prob.astype(BF16)
            ds_ref[rows, :] = (prob * (dp_ref[rows, :] - delta)).astype(BF16)
            w = -jnp.exp(sink_ref[rows, :] - lse_col) * delta
            dsink_row += jnp.where(lane1 == head, jnp.sum(w, axis=0, keepdims=True), 0.0)
        dq = _by_group(_dot, ds_ref[...], kd) * ATT_SCALE
        lo_q = _lane_iota((WINDOW, LANES)) < A_HEAD_DIM
        for p in range(A_HEADS // 2):
            dq_ref[:, p * LANES:(p + 1) * LANES] = jnp.where(
                lo_q, dq[2 * p * WINDOW:(2 * p + 1) * WINDOW], dq[(2 * p + 1) * WINDOW:(2 * p + 2) * WINDOW]).astype(BF16)
        lane2 = _lane_iota((2 * WINDOW, LANES))
        dk_tot = jnp.zeros((2 * WINDOW, LANES), F32)
        dv_tot = jnp.zeros((2 * WINDOW, LANES), F32)
        for g in range(A_KV_HEADS):
            grows = slice(g * GROUP_ROWS, (g + 1) * GROUP_ROWS)
            dk_acc = _dot_tn(ds_ref[grows, :], qs[grows])
            dv_acc = _dot_tn(p_ref[grows, :], dos[grows])
            mine = (lane2 < A_HEAD_DIM) == (g == 0)
            dk_tot = jnp.where(mine, dk_acc + pltpu.roll(dk_acc, A_HEAD_DIM, 1), dk_tot)
            dv_tot = jnp.where(mine, dv_acc + pltpu.roll(dv_acc, A_HEAD_DIM, 1), dv_tot)
        gate = jnp.where(active, 1.0, 0.0)
        dsink_ref[0:1, :] += dsink_row * gate
        dkv_ref[:, 0:LANES] = (carry_ref[:, 0:LANES] + dk_tot[:WINDOW] * gate).astype(BF16)
        dkv_ref[:, LANES:] = (carry_ref[:, LANES:] + dv_tot[:WINDOW] * gate).astype(BF16)
        carry_ref[:, 0:LANES] = dk_tot[WINDOW:]
        carry_ref[:, LANES:] = dv_tot[WINDOW:]

    def cur(col, w):
        return pl.BlockSpec((WINDOW, w), lambda b, n: (b * nb + jnp.minimum(n, nb - 1), col))

    def prev(col):
        return pl.BlockSpec((WINDOW, LANES), lambda b, n: (b * nb + jnp.maximum(jnp.minimum(n, nb - 1) - 1, 0), col))

    lag = pl.BlockSpec((WINDOW, 2 * LANES), lambda b, n: (b * nb + jnp.maximum(n - 1, 0), 0))
    kcol, vcol = QKV_K // LANES, QKV_V // LANES
    scores = (ATT_ROWS, 2 * WINDOW)
    return pl.pallas_call(
        body, name="attn_bwd", grid=(B, nb + 1),
        in_specs=[_const_spec((ATT_ROWS, 1)), cur(0, A_WIDTH), cur(kcol, LANES), cur(vcol, LANES),
                  prev(kcol), prev(vcol), cur(0, A_WIDTH), cur(0, A_WIDTH), cur(0, LANES)],
        out_specs=[cur(0, A_WIDTH), lag, pl.BlockSpec((8, LANES), lambda b, n: (0, 0))],
        out_shape=[jax.ShapeDtypeStruct((T, A_WIDTH), BF16), jax.ShapeDtypeStruct((T, 2 * LANES), BF16),
                   jax.ShapeDtypeStruct((8, LANES), F32)],
        scratch_shapes=[pltpu.VMEM((WINDOW, 2 * LANES), F32), pltpu.VMEM(scores, F32), pltpu.VMEM(scores, F32),
                        pltpu.VMEM(scores, BF16), pltpu.VMEM(scores, BF16)],
        compiler_params=_params("arbitrary", "arbitrary"),
    )(sink_col, qkv, qkv, qkv, qkv, qkv, do, out, lse)


GLA_TILE = 256
CHUNKS_PER_TILE = GLA_TILE // B_CHUNK


def _gla_factors(q_ref, k_ref, cum_ref):
    scale = B_KEY_DIM ** -0.5
    cum = cum_ref[...]
    shape = (B_CHUNK, B_KEY_WIDTH)
    last = jnp.concatenate([jnp.broadcast_to(cum_ref[pl.ds(c * B_CHUNK + B_CHUNK - 1, 1), :], shape)
                            for c in range(CHUNKS_PER_TILE)], axis=0)
    mid = jnp.concatenate([jnp.broadcast_to(cum_ref[pl.ds(c * B_CHUNK + B_CHUNK // 2 - 1, 1), :], shape)
                           for c in range(CHUNKS_PER_TILE)], axis=0)
    e_qm, e_km, e_qe, e_kd = jnp.exp(cum - mid), jnp.exp(mid - cum), jnp.exp(cum), jnp.exp(last - cum)
    qs = q_ref[...] * scale
    k = k_ref[...]
    return qs, k, (e_qm, e_km, e_qe, e_kd)


def _head_mask(shape, h):
    return (_lane_iota(shape) // B_KEY_DIM) == h


def _stack_masked(t):
    return jnp.concatenate([jnp.where(_head_mask(t.shape, h), t, 0.0) for h in range(B_HEADS)], axis=0).astype(BF16)


def _select_heads(t):
    shape = (B_CHUNK, B_KEY_WIDTH)
    out = jnp.zeros(shape, F32)
    for h in range(B_HEADS):
        out = jnp.where(_head_mask(shape, h), t[h * B_CHUNK:(h + 1) * B_CHUNK], out)
    return out


def _select_state(t):
    shape = (B_VAL_DIM, B_KEY_WIDTH)
    out = jnp.zeros(shape, F32)
    for h in range(B_HEADS):
        out = jnp.where(_head_mask(shape, h), t[h * B_VAL_DIM:(h + 1) * B_VAL_DIM], out)
    return out


def _rows_by_head(t):
    return jnp.concatenate([t[:, h * B_VAL_DIM:(h + 1) * B_VAL_DIM] for h in range(B_HEADS)], axis=0)


def _intra_mask():
    i, j = _row_iota((GLA_TILE, GLA_TILE)), _lane_iota((GLA_TILE, GLA_TILE))
    return (i // B_CHUNK == j // B_CHUNK) & (j <= i)


def _pair_stack(t, p):
    slab = t[:, p * LANES:(p + 1) * LANES]
    lo = _lane_iota(slab.shape) < B_KEY_DIM
    return jnp.concatenate([jnp.where(lo, slab, 0.0), jnp.where(lo, 0.0, slab)], axis=0).astype(BF16)


def _gla_fwd(q, k, cum, vb, B, S):
    T = B * S
    nt = S // GLA_TILE

    def one_sequence(q_ref, k_ref, cum_ref, v_ref, o_ref, st_all_ref, st_ref):
        qs, kk, (e_qm, e_km, e_qe, e_kd) = _gla_factors(q_ref, k_ref, cum_ref)
        qm, km, qe, kd = qs * e_qm, kk * e_km, qs * e_qe, (kk * e_kd).astype(BF16)
        mask = _intra_mask()
        intra = []
        for p in range(B_HEADS // 2):
            a = _dot_nt(_pair_stack(qm, p), km[:, p * LANES:(p + 1) * LANES].astype(BF16))
            for hh in range(2):
                h = 2 * p + hh
                att = jnp.where(mask, a[hh * GLA_TILE:(hh + 1) * GLA_TILE], 0.0).astype(BF16)
                intra.append(_dot(att, v_ref[:, h * B_VAL_DIM:(h + 1) * B_VAL_DIM]))
        inter = []
        for c in range(CHUNKS_PER_TILE):
            rows = slice(c * B_CHUNK, (c + 1) * B_CHUNK)
            st = st_ref[...]
            st_all_ref[c] = st
            inter.append(_dot_nt(_stack_masked(qe[rows]), st.astype(BF16)))
            inc = _select_state(_dot_tn(v_ref[rows, :], kd[rows]))
            decay = jnp.exp(cum_ref[pl.ds(c * B_CHUNK + B_CHUNK - 1, 1), :])
            st_ref[...] = st * decay + inc
        for h in range(B_HEADS):
            oi = jnp.concatenate([inter[c][h * B_CHUNK:(h + 1) * B_CHUNK] for c in range(CHUNKS_PER_TILE)], axis=0)
            o_ref[:, h * B_VAL_DIM:(h + 1) * B_VAL_DIM] = intra[h] + oi

    def body(q_ref, k_ref, cum_ref, v_ref, o_ref, st_all_ref, st_ref):
        @pl.when(pl.program_id(0) == 0)
        def _():
            st_ref[...] = jnp.zeros_like(st_ref)

        for b in range(B):
            one_sequence(*[r.at[b] for r in (q_ref, k_ref, cum_ref, v_ref, o_ref, st_all_ref, st_ref)])

    def rows(w):
        return pl.BlockSpec((B, GLA_TILE, w), lambda t: (0, t, 0))

    seq = lambda a: a.reshape(B, S, a.shape[-1])
    o, st_all = pl.pallas_call(
        body, name="gla_fwd", grid=(nt,),
        in_specs=[rows(B_KEY_WIDTH), rows(B_KEY_WIDTH), rows(B_KEY_WIDTH), rows(B_WIDTH)],
        out_specs=[rows(B_WIDTH),
                   pl.BlockSpec((B, CHUNKS_PER_TILE, B_VAL_DIM, B_KEY_WIDTH), lambda t: (0, t, 0, 0))],
        out_shape=[jax.ShapeDtypeStruct((B, S, B_WIDTH), F32),
                   jax.ShapeDtypeStruct((B, S // B_CHUNK, B_VAL_DIM, B_KEY_WIDTH), F32)],
        scratch_shapes=[pltpu.VMEM((B, B_VAL_DIM, B_KEY_WIDTH), F32)],
        compiler_params=_params("arbitrary"),
    )(seq(q), seq(k), seq(cum), seq(vb))
    return o.reshape(T, B_WIDTH), st_all.reshape(T // B_CHUNK, B_VAL_DIM, B_KEY_WIDTH)


def _gla_bwd(q, k, cum, vb, do, st_all, B, S, wgrads):
    T = B * S
    nt = S // GLA_TILE
    scale = B_KEY_DIM ** -0.5
    nw = len(wgrads)

    def one_sequence(q_ref, k_ref, cum_ref, v_ref, do_ref, st_all_ref, dq_ref, dk_ref, dv_ref, dla_ref, dst_ref):
        qs, kk, (e_qm, e_km, e_qe, e_kd) = _gla_factors(q_ref, k_ref, cum_ref)
        qm, km, qe, kd = qs * e_qm, kk * e_km, qs * e_qe, kk * e_kd
        mask = _intra_mask()
        dqm_slabs, dkm_slabs, dv_intra = [], [], []
        for p in range(B_HEADS // 2):
            qm_st = _pair_stack(qm, p)
            km_p = km[:, p * LANES:(p + 1) * LANES].astype(BF16)
            a = _dot_nt(qm_st, km_p)
            da_blocks, dqm_h = [], []
            for hh in range(2):
                h = 2 * p + hh
                vs = slice(h * B_VAL_DIM, (h + 1) * B_VAL_DIM)
                att = jnp.where(mask, a[hh * GLA_TILE:(hh + 1) * GLA_TILE], 0.0).astype(BF16)
                dv_intra.append(_dot_tn(att, do_ref[:, vs]))
                da = jnp.where(mask, _dot_nt(do_ref[:, vs], v_ref[:, vs]), 0.0).astype(BF16)
                da_blocks.append(da)
                dqm_h.append(_dot(da, km_p))
            lo = _lane_iota((GLA_TILE, LANES)) < B_KEY_DIM
            dqm_slabs.append(jnp.where(lo, dqm_h[0], dqm_h[1]))
            dkm_slabs.append(_dot_tn(jnp.concatenate(da_blocks, axis=0), qm_st))
        dqm = jnp.concatenate(dqm_slabs, axis=1)
        dkm = jnp.concatenate(dkm_slabs, axis=1)

        dqe_c, dkd_c, dv_inter, tail_c = ([None] * CHUNKS_PER_TILE for _ in range(4))
        for c in reversed(range(CHUNKS_PER_TILE)):
            rows = slice(c * B_CHUNK, (c + 1) * B_CHUNK)
            dst = dst_ref[...]
            dst_b = dst.astype(BF16)
            dv_inter[c] = _dot_nt(_stack_masked(kd[rows]), dst_b)
            dkd_c[c] = _select_heads(_dot(_rows_by_head(v_ref[rows, :]), dst_b))
            do_c = do_ref[rows, :]
            dqe_c[c] = _select_heads(_dot(_rows_by_head(do_c), st_all_ref[c].astype(BF16)))
            contrib = _select_state(_dot_tn(do_c, qe[rows].astype(BF16)))
            decay = jnp.exp(cum_ref[pl.ds(c * B_CHUNK + B_CHUNK - 1, 1), :])
            tail = (jnp.sum(kk[rows] * dkd_c[c] * e_kd[rows], axis=0, keepdims=True)
                    + decay * jnp.sum(st_all_ref[c] * dst, axis=0, keepdims=True))
            tail_c[c] = jnp.broadcast_to(tail, (B_CHUNK, B_KEY_WIDTH))
            dst_ref[...] = dst * decay + contrib
        dqe = jnp.concatenate(dqe_c, axis=0)
        dkd = jnp.concatenate(dkd_c, axis=0)
        dqs = dqm * e_qm + dqe * e_qe
        dk = dkm * e_km + dkd * e_kd
        dq_ref[...] = (dqs * scale).astype(BF16)
        dk_ref[...] = dk.astype(BF16)
        for h in range(B_HEADS):
            dvi = jnp.concatenate([dv_inter[c][h * B_CHUNK:(h + 1) * B_CHUNK] for c in range(CHUNKS_PER_TILE)], axis=0)
            dv_ref[:, h * B_VAL_DIM:(h + 1) * B_VAL_DIM] = (dv_intra[h] + dvi).astype(BF16)
        dd = qs * dqs - kk * dk
        i, j = _row_iota((GLA_TILE, GLA_TILE)), _lane_iota((GLA_TILE, GLA_TILE))
        upper = ((i // B_CHUNK == j // B_CHUNK) & (j >= i)).astype(BF16)
        hi, mid, lo3 = _split3(dd)
        dla_ref[...] = _dot(upper, hi) + _dot(upper, mid) + _dot(upper, lo3) + jnp.concatenate(tail_c, axis=0)

    def body(q_ref, k_ref, cum_ref, v_ref, do_ref, st_all_ref, *rest):
        g_refs, (dq_ref, dk_ref, dv_ref, dla_ref) = rest[:nw], rest[nw:nw + 4]
        rv_refs, (dst_ref, send_sems, recv_sems) = rest[nw + 4:2 * nw + 4], rest[2 * nw + 4:]
        x, y, c = _my_place()

        def wcopy(a, r):
            dx, dy, dc = FLIPS[r]
            return pltpu.make_async_remote_copy(
                src_ref=g_refs[a].at[4 * (x ^ dx) + 2 * (y ^ dy) + (c ^ dc)], dst_ref=rv_refs[a].at[r],
                send_sem=send_sems.at[a, r], recv_sem=recv_sems.at[a, r],
                device_id=(x ^ dx, y ^ dy, c ^ dc), device_id_type=MESH)

        @pl.when(pl.program_id(0) == 0)
        def _():
            dst_ref[...] = jnp.zeros_like(dst_ref)
            for a in range(nw):
                for r in range(len(FLIPS)):
                    wcopy(a, r).start()

        for b in range(B):
            one_sequence(*[r.at[b] for r in (q_ref, k_ref, cum_ref, v_ref, do_ref, st_all_ref,
                                             dq_ref, dk_ref, dv_ref, dla_ref, dst_ref)])

        @pl.when(pl.program_id(0) == nt - 1)
        def _():
            for a in range(nw):
                for r in range(len(FLIPS)):
                    wcopy(a, r).wait()

    def rows(w):
        return pl.BlockSpec((B, GLA_TILE, w), lambda t: (0, nt - 1 - t, 0))

    seq = lambda a: a.reshape(B, S, a.shape[-1])
    res = pl.pallas_call(
        body, name="gla_bwd", grid=(nt,),
        in_specs=[rows(B_KEY_WIDTH), rows(B_KEY_WIDTH), rows(B_KEY_WIDTH), rows(B_WIDTH), rows(B_WIDTH),
                  pl.BlockSpec((B, CHUNKS_PER_TILE, B_VAL_DIM, B_KEY_WIDTH), lambda t: (0, nt - 1 - t, 0, 0))]
                 + _any_specs(nw),
        out_specs=[rows(B_KEY_WIDTH), rows(B_KEY_WIDTH), rows(B_WIDTH), rows(B_KEY_WIDTH)] + _any_specs(nw),
        out_shape=[jax.ShapeDtypeStruct((B, S, B_KEY_WIDTH), BF16), jax.ShapeDtypeStruct((B, S, B_KEY_WIDTH), BF16),
                   jax.ShapeDtypeStruct((B, S, B_WIDTH), BF16), jax.ShapeDtypeStruct((B, S, B_KEY_WIDTH), F32)]
                  + [jax.ShapeDtypeStruct((len(FLIPS), *g.shape[1:]), g.dtype) for g in wgrads],
        scratch_shapes=[pltpu.VMEM((B, B_VAL_DIM, B_KEY_WIDTH), F32),
                        pltpu.SemaphoreType.DMA((nw, len(FLIPS))), pltpu.SemaphoreType.DMA((nw, len(FLIPS)))],
        compiler_params=_params("arbitrary"),
    )(seq(q), seq(k), seq(cum), seq(vb), seq(do), st_all.reshape(B, S // B_CHUNK, B_VAL_DIM, B_KEY_WIDTH), *wgrads)
    return [a.reshape(T, a.shape[-1]) for a in res[:4]], res[4:]


def _merge(x2, tgt2, attn, za, o_gla, zb, ga, gb, w_oa_sh, w_ob_sh, w_o, g_gla, g_final):
    T = x2.shape[0]
    tm = 256
    last = T // tm - 1

    def body(x_ref, tgt_ref, attn_ref, za_ref, og_ref, zb_ref, ga_ref, gb_ref,
             woa_sh_ref, wob_sh_ref, wo_ref, gg_ref, gf_ref,
             dxres_ref, dattn_ref, dog_ref, dza_ref, dzb_ref, dga_ref, dgb_ref,
             dwo_ref, dwoa_ref, dwob_ref, small_ref,
             awo_ref, awoa_ref, awob_ref, agf_ref, agg_ref, loss_ref, woa_ref, wob_ref):
        @pl.when(pl.program_id(0) == 0)
        def _():
            for r in (awo_ref, awoa_ref, awob_ref, agf_ref, agg_ref, loss_ref):
                r[...] = jnp.zeros_like(r)
            for j in range(N_DEV):
                woa_ref[:, j * SHARD_OUT:(j + 1) * SHARD_OUT] = woa_sh_ref[j]
                wob_ref[:, j * SHARD_OUT:(j + 1) * SHARD_OUT] = wob_sh_ref[j]

        za_v = za_ref[...]
        sig_za = _sigmoid_tanh(za_v)
        silu_a = za_v * sig_za
        attn_v = attn_ref[...]
        oa = (attn_v * silu_a).astype(BF16)
        ya = _dot(oa, woa_ref[...])
        og = og_ref[...]
        zb_v = zb_ref[...]
        sig_zb = _sigmoid_tanh(zb_v)
        silu_b = zb_v * sig_zb
        gg = gg_ref[...]
        on_parts, rinv_parts = [], []
        for h in range(B_HEADS):
            seg = og[:, h * B_VAL_DIM:(h + 1) * B_VAL_DIM]
            rinv = lax.rsqrt(jnp.mean(seg * seg, axis=-1, keepdims=True) + NORM_EPS)
            rinv_parts.append(rinv)
            on_parts.append(seg * rinv)
        on = jnp.concatenate(on_parts, axis=1)
        obn = on * gg
        ob = (obn * silu_b).astype(BF16)
        yb = _dot(ob, wob_ref[...])
        sig_a, sig_b = _sigmoid_tanh(ga_ref[...]), _sigmoid_tanh(gb_ref[...])
        merged = (sig_a * ya + sig_b * yb).astype(BF16)
        out = x_ref[...] + _dot(merged, wo_ref[...])
        rf = lax.rsqrt(jnp.mean(out * out, axis=-1, keepdims=True) + NORM_EPS)
        nrm = out * rf
        gf = gf_ref[...]
        err = nrm * gf - tgt_ref[...]
        loss_ref[...] += jnp.sum(err * err) * (0.5 / D_MODEL)

        dy = err * (1.0 / D_MODEL)
        agf_ref[...] += jnp.sum(dy * nrm, axis=0, keepdims=True)
        dn = dy * gf
        dout = rf * (dn - nrm * jnp.mean(dn * nrm, axis=-1, keepdims=True))
        dxres_ref[...] = dout
        dout_b = dout.astype(BF16)
        dmerged = _dot_nt(dout_b, wo_ref[...])
        awo_ref[...] += _dot_tn(merged, dout_b)
        dya = dmerged * sig_a
        dyb = dmerged * sig_b
        dga_ref[...] = (dmerged * ya * sig_a * (1.0 - sig_a)).astype(BF16)
        dgb_ref[...] = (dmerged * yb * sig_b * (1.0 - sig_b)).astype(BF16)
        dya_b, dyb_b = dya.astype(BF16), dyb.astype(BF16)
        awoa_ref[...] += _dot_tn(oa, dya_b)
        awob_ref[...] += _dot_tn(ob, dyb_b)
        doa = _dot_nt(dya_b, woa_ref[...])
        dattn_ref[...] = (doa * silu_a).astype(BF16)
        dza_ref[...] = (doa * attn_v * (sig_za * (1.0 + za_v * (1.0 - sig_za)))).astype(BF16)
        dob = _dot_nt(dyb_b, wob_ref[...])
        dzb_ref[...] = (dob * obn * (sig_zb * (1.0 + zb_v * (1.0 - sig_zb)))).astype(BF16)
        dobn = dob * silu_b
        agg_ref[...] += jnp.sum(dobn * on, axis=0, keepdims=True)
        don = dobn * gg
        for h in range(B_HEADS):
            sl = slice(h * B_VAL_DIM, (h + 1) * B_VAL_DIM)
            don_h, on_h = don[:, sl], on[:, sl]
            dog_ref[:, sl] = (rinv_parts[h] * (don_h - on_h * jnp.mean(don_h * on_h, axis=-1, keepdims=True))).astype(BF16)

        @pl.when(pl.program_id(0) == last)
        def _():
            for j in range(N_DEV):
                dwo_ref[j] = awo_ref[j * SHARD_OUT:(j + 1) * SHARD_OUT, :].astype(BF16)
                dwoa_ref[j] = awoa_ref[:, j * SHARD_OUT:(j + 1) * SHARD_OUT].astype(BF16)
                dwob_ref[j] = awob_ref[:, j * SHARD_OUT:(j + 1) * SHARD_OUT].astype(BF16)
            small_ref[...] = jnp.zeros_like(small_ref)
            _put_rows(small_ref, SMALL_G_FINAL, agf_ref[...])
            _put_rows(small_ref, SMALL_G_GLA, agg_ref[...])
            small_ref[SMALL_LOSS:SMALL_LOSS + 1, :] = loss_ref[...]

    def rows(w):
        return pl.BlockSpec((tm, w), lambda i: (i, 0))

    def whole(shape):
        nd = len(shape)
        return pl.BlockSpec(shape, lambda i: (0,) * nd)

    outs = [((T, D_MODEL), F32, rows(D_MODEL)), ((T, A_WIDTH), BF16, rows(A_WIDTH)), ((T, B_WIDTH), BF16, rows(B_WIDTH)),
            ((T, A_WIDTH), BF16, rows(A_WIDTH)), ((T, B_WIDTH), BF16, rows(B_WIDTH)),
            ((T, D_MODEL), BF16, rows(D_MODEL)), ((T, D_MODEL), BF16, rows(D_MODEL)),
            ((N_DEV, SHARD_OUT, D_MODEL), BF16, whole((N_DEV, SHARD_OUT, D_MODEL))),
            ((N_DEV, A_WIDTH, SHARD_OUT), BF16, whole((N_DEV, A_WIDTH, SHARD_OUT))),
            ((N_DEV, B_WIDTH, SHARD_OUT), BF16, whole((N_DEV, B_WIDTH, SHARD_OUT))),
            ((SMALL_SINKS, LANES), F32, whole((SMALL_SINKS, LANES)))]
    return pl.pallas_call(
        body, name="merge", grid=(T // tm,),
        in_specs=[rows(D_MODEL), rows(D_MODEL), rows(A_WIDTH), rows(A_WIDTH), rows(B_WIDTH), rows(B_WIDTH),
                  rows(D_MODEL), rows(D_MODEL),
                  _const_spec((N_DEV, A_WIDTH, SHARD_OUT)), _const_spec((N_DEV, B_WIDTH, SHARD_OUT)),
                  _const_spec((D_MODEL, D_MODEL)), _const_spec((1, B_WIDTH)), _const_spec((1, D_MODEL))],
        out_specs=[o[2] for o in outs],
        out_shape=[jax.ShapeDtypeStruct(o[0], o[1]) for o in outs],
        scratch_shapes=[pltpu.VMEM((D_MODEL, D_MODEL), F32), pltpu.VMEM((A_WIDTH, D_MODEL), F32),
                        pltpu.VMEM((B_WIDTH, D_MODEL), F32), pltpu.VMEM((1, D_MODEL), F32), pltpu.VMEM((1, B_WIDTH), F32),
                        pltpu.VMEM((1, LANES), F32), pltpu.VMEM((A_WIDTH, D_MODEL), BF16),
                        pltpu.VMEM((B_WIDTH, D_MODEL), BF16)],
        compiler_params=_params("arbitrary"),
    )(x2, tgt2, attn, za, o_gla, zb, ga, gb, w_oa_sh, w_ob_sh, w_o, g_gla, g_final)


def _in_proj_bwd(x2, dxres, cosf, sinf, g_in, wt_pad, wa_pad, parts):
    T = x2.shape[0]
    tm = 256
    last = T // tm - 1
    base = SMALL_G_IN

    def body(x_ref, dxres_ref, cos_ref, sin_ref, g_ref, wt_ref, wa_ref,
             dq_ref, dkv_ref, dza_ref, dqb_ref, dkb_ref, dvb_ref, dzb_ref, dla_ref, u_ref, alr_ref, dga_ref, dgb_ref,
             dx_ref, dsh_ref, small_ref, dproj_ref, agin_ref, aba_ref, awa_ref):
        @pl.when(pl.program_id(0) == 0)
        def _():
            for r in (agin_ref, aba_ref, awa_ref):
                r[...] = jnp.zeros_like(r)

        cos, nsin = cos_ref[...], -sin_ref[...]
        for s in range(A_WIDTH // LANES):
            sl = slice(s * LANES, (s + 1) * LANES)
            dproj_ref[:, sl] = _rope_slab(dq_ref[:, sl].astype(F32), cos, nsin).astype(BF16)
        dproj_ref[:, QKV_K:QKV_V] = _rope_slab(dkv_ref[:, 0:LANES].astype(F32), cos, nsin).astype(BF16)
        dproj_ref[:, QKV_V:QKV_W] = dkv_ref[:, LANES:]

        def put(name, val):
            a, b = SEG[name]
            dproj_ref[:, a:b] = val

        put("za", dza_ref[...])
        put("qb", dqb_ref[...])
        put("kb", dkb_ref[...])
        put("vb", dvb_ref[...])
        put("zb", dzb_ref[...])
        put("ga", dga_ref[...])
        put("gb", dgb_ref[...])
        du = dla_ref[...] * (1.0 / B_GATE_TEMP) * _sigmoid(-u_ref[...])
        aba_ref[...] += jnp.sum(du, axis=0, keepdims=True)
        du_b = du.astype(BF16)
        awa_ref[...] += _dot_tn(alr_ref[...], du_b)
        put("alr", _dot_nt(du_b, wa_ref[...]).astype(BF16))

        for j in range(N_DEV):
            col = (j % 2) * SHARD_PAD
            for a, b in _shard_pad_cols(j):
                dsh_ref[j // 2, :, col:col + b - a] = dproj_ref[:, a:b]
                col += b - a
            dsh_ref[j // 2, :, col:(j % 2 + 1) * SHARD_PAD] = jnp.zeros((tm, SHARD_PAD - SHARD_IN), BF16)

        dh = _dot(dproj_ref[...], wt_ref[...])
        x = x_ref[...]
        r = lax.rsqrt(jnp.mean(x * x, axis=-1, keepdims=True) + NORM_EPS)
        nrm = x * r
        agin_ref[...] += jnp.sum(dh * nrm, axis=0, keepdims=True)
        dn = dh * g_ref[...]
        dx_ref[...] = dxres_ref[...] + r * (dn - nrm * jnp.mean(dn * nrm, axis=-1, keepdims=True))

        @pl.when(pl.program_id(0) == last)
        def _():
            small_ref[...] = jnp.zeros_like(small_ref)
            _put_rows(small_ref, SMALL_G_IN - base, agin_ref[...])
            _put_rows(small_ref, SMALL_B_ALPHA - base, aba_ref[...])
            for half in range(B_KEY_WIDTH // LANES):
                r0 = SMALL_W_ALPHA - base + half * B_GATE_RANK
                small_ref[r0:r0 + B_GATE_RANK, :] = awa_ref[0:B_GATE_RANK, half * LANES:(half + 1) * LANES]

    def rows(w):
        return pl.BlockSpec((tm, w), lambda i: (i, 0))

    names = ["dq", "dkv", "dza", "dqb", "dkb", "dvb", "dzb", "dla", "u", "alr", "dga", "dgb"]
    return pl.pallas_call(
        body, name="in_proj_bwd", grid=(T // tm,),
        in_specs=[rows(D_MODEL), rows(D_MODEL), rows(LANES), rows(LANES), _const_spec((1, D_MODEL)),
                  _const_spec((D_IN_PAD, D_MODEL)), _const_spec((RANK_PAD, B_KEY_WIDTH))]
                 + [rows(parts[n].shape[1]) for n in names],
        out_specs=[rows(D_MODEL), pl.BlockSpec((N_CHIPS, tm, 2 * SHARD_PAD), lambda i: (0, i, 0)),
                   pl.BlockSpec((SMALL_ROWS - base, LANES), lambda i: (0, 0))],
        out_shape=[jax.ShapeDtypeStruct((T, D_MODEL), F32), jax.ShapeDtypeStruct((N_CHIPS, T, 2 * SHARD_PAD), BF16),
                   jax.ShapeDtypeStruct((SMALL_ROWS - base, LANES), F32)],
        scratch_shapes=[pltpu.VMEM((tm, D_IN_PAD), BF16), pltpu.VMEM((1, D_MODEL), F32), pltpu.VMEM((1, B_KEY_WIDTH), F32),
                        pltpu.VMEM((RANK_PAD, B_KEY_WIDTH), F32)],
        compiler_params=_params("arbitrary"),
    )(x2, dxres, cosf, sinf, g_in, wt_pad, wa_pad, *[parts[n] for n in names])


FLIPS = [(dx, dy, dc) for dx in (0, 1) for dy in (0, 1) for dc in (0, 1)][1:]


def _my_place():
    return lax.axis_index("x"), lax.axis_index("y"), lax.axis_index("c")


def _any_specs(n):
    return [pl.BlockSpec(memory_space=pl.ANY)] * n


def _gather_first(shards, pos_col):
    n = len(shards)
    T = pos_col.shape[0]
    rows_per_pass = math.gcd(T, 512)
    invf, sign = _rope_lane_constants()

    def body(*refs):
        ins, (pos_ref, invf_ref, sign_ref) = refs[:n], refs[n:n + 3]
        outs, (cos_ref, sin_ref) = refs[n + 3:2 * n + 3], refs[2 * n + 3:2 * n + 5]
        send_sems, recv_sems, local_sems = refs[2 * n + 5:]
        x, y, c = _my_place()
        me, sibling = (x, y, c), (x, y, 1 - c)
        chips = [(1 - x, y), (x, 1 - y), (1 - x, 1 - y)]

        def block(a, px, py, pc):
            return outs[a].at[4 * px + 2 * py + pc]

        def copy(a, k, blk, to, src=None):
            return pltpu.make_async_remote_copy(
                src_ref=block(a, *blk) if src is None else src, dst_ref=block(a, *blk),
                send_sem=send_sems.at[a, k], recv_sem=recv_sems.at[a, k], device_id=to, device_id_type=MESH)

        mine = [pltpu.make_async_copy(ins[a], block(a, *me), local_sems.at[a]) for a in range(n)]
        for cp in mine:
            cp.start()
        first = []
        for a in range(n):
            first.append(copy(a, 0, me, sibling, src=ins[a]))
            first += [copy(a, 1 + j, me, (*chip, c), src=ins[a]) for j, chip in enumerate(chips)]
        for cp in first:
            cp.start()

        def tables(i, carry):
            rows = pl.ds(pl.multiple_of(i * rows_per_pass, rows_per_pass), rows_per_pass)
            ang = pos_ref[rows, :].astype(F32) * invf_ref[...]
            cos_ref[rows, :] = jnp.cos(ang)
            sin_ref[rows, :] = jnp.sin(ang) * sign_ref[...]
            return carry

        lax.fori_loop(0, T // rows_per_pass, tables, 0)

        passed = []
        for j, chip in enumerate(chips):
            for a in range(n):
                copy(a, 1 + j, (*chip, c), me).wait_recv()
                fwd = copy(a, 4 + j, (*chip, c), sibling)
                fwd.start()
                passed.append(fwd)
        for a in range(n):
            copy(a, 0, sibling, me).wait_recv()
            for j, chip in enumerate(chips):
                copy(a, 4 + j, (*chip, 1 - c), me).wait_recv()
        for cp in first + passed:
            cp.wait_send()
        for cp in mine:
            cp.wait()

    vmem = pl.BlockSpec(memory_space=pltpu.VMEM)
    res = pl.pallas_call(
        body, name="gather_weights",
        in_specs=_any_specs(n) + [vmem] * 3, out_specs=_any_specs(n) + [vmem] * 2,
        out_shape=[jax.ShapeDtypeStruct((N_DEV, *s.shape), s.dtype) for s in shards]
                  + [jax.ShapeDtypeStruct((T, LANES), F32)] * 2,
        scratch_shapes=[pltpu.SemaphoreType.DMA((n, 7)), pltpu.SemaphoreType.DMA((n, 7)), pltpu.SemaphoreType.DMA((n,))],
        compiler_params=pltpu.CompilerParams(vmem_limit_bytes=V7X_VMEM_LIMIT),
    )(*shards, pos_col, invf, sign)
    return res[:n], res[n], res[n + 1]


def _w_in_grad_rs(h, dsh, chip_order, small):
    T = h.shape[0]
    tk = math.gcd(T, 2048)
    nk = T // tk
    chip_flips = [(1, 1), (1, 0), (0, 1)]
    n_steps = len(chip_flips) + 1
    SIB = len(chip_flips)
    k_finish = min(1, nk - 1)

    def body(order_ref, h_ref, d_ref, s_ref, own_ref, recv_ref, sall_ref,
             acc_ref, keep_ref, pre_ref, to_sib_ref, to_chip_ref,
             sib_send, sib_recv, chip_send, chip_recv, ssend_sems, srecv_sems, local_sem):
        i, kk = pl.program_id(0), pl.program_id(1)
        x, y, c = _my_place()
        my_dev = 4 * x + 2 * y + c

        def small_copy(r, slot):
            dx, dy, dc = FLIPS[r]
            return pltpu.make_async_remote_copy(
                src_ref=s_ref, dst_ref=sall_ref.at[slot], send_sem=ssend_sems.at[r], recv_sem=srecv_sems.at[r],
                device_id=(x ^ dx, y ^ dy, c ^ dc), device_id_type=MESH)

        keep_small = pltpu.make_async_copy(s_ref, sall_ref.at[my_dev], local_sem)

        def sib_copy(t):
            dst = recv_ref.at[SIB] if t == SIB else pre_ref.at[t]
            return pltpu.make_async_remote_copy(
                src_ref=to_sib_ref.at[t], dst_ref=dst, send_sem=sib_send.at[t], recv_sem=sib_recv.at[t],
                device_id=(x, y, 1 - c), device_id_type=MESH)

        def chip_copy(t):
            dx, dy = chip_flips[t]
            return pltpu.make_async_remote_copy(
                src_ref=to_chip_ref.at[t], dst_ref=recv_ref.at[t], send_sem=chip_send.at[t], recv_sem=chip_recv.at[t],
                device_id=(x ^ dx, y ^ dy, c), device_id_type=MESH)

        def halves():
            first, second = acc_ref[0:SHARD_PAD, :], acc_ref[SHARD_PAD:2 * SHARD_PAD, :]
            return jnp.where(c == 0, first, second), jnp.where(c == 0, second, first)

        @pl.when((i == 0) & (kk == 0))
        def _():
            keep_small.start()
            for r in range(len(FLIPS)):
                small_copy(r, my_dev).start()

        @pl.when(kk == 0)
        def _():
            acc_ref[...] = jnp.zeros_like(acc_ref)

        acc_ref[...] += _dot_tn(d_ref[...], h_ref[...])

        for t in range(len(chip_flips)):
            @pl.when((i == t + 1) & (kk == k_finish))
            def _(t=t):
                sib_copy(t).wait_recv()
                to_chip_ref[t] = (keep_ref[...] + pre_ref[t].astype(F32)).astype(BF16)
                chip_copy(t).start()

        for t in range(len(chip_flips)):
            @pl.when((i == t) & (kk == nk - 1))
            def _(t=t):
                mine, theirs = halves()
                to_sib_ref[t] = theirs.astype(BF16)
                sib_copy(t).start()
                keep_ref[...] = mine

        @pl.when((i == n_steps - 1) & (kk == nk - 1))
        def _():
            mine, theirs = halves()
            own_ref[...] = mine
            to_sib_ref[SIB] = theirs.astype(BF16)
            sib_copy(SIB).start()
            for t in range(len(chip_flips)):
                sib_copy(t).wait_send()
                chip_copy(t).wait_send()
                chip_copy(t).wait_recv()
            sib_copy(SIB).wait_send()
            sib_copy(SIB).wait_recv()
            for r, (dx, dy, dc) in enumerate(FLIPS):
                small_copy(r, 4 * (x ^ dx) + 2 * (y ^ dy) + (c ^ dc)).wait_recv()
                small_copy(r, my_dev).wait_send()
            keep_small.wait()

    shard = (SHARD_PAD, D_MODEL)
    return pl.pallas_call(
        body, name="w_in_grad_rs",
        grid_spec=pltpu.PrefetchScalarGridSpec(
            num_scalar_prefetch=1, grid=(n_steps, nk),
            in_specs=[pl.BlockSpec((tk, D_MODEL), lambda i, kk, order: (kk, 0)),
                      pl.BlockSpec((None, tk, 2 * SHARD_PAD), lambda i, kk, order: (order[i], kk, 0)),
                      pl.BlockSpec(memory_space=pl.ANY)],
            out_specs=[pl.BlockSpec(shard, lambda i, kk, order: (0, 0)),
                       pl.BlockSpec(memory_space=pl.ANY), pl.BlockSpec(memory_space=pl.ANY)],
            scratch_shapes=[pltpu.VMEM((2 * SHARD_PAD, D_MODEL), F32), pltpu.VMEM(shard, F32),
                            pltpu.VMEM((SIB, *shard), BF16), pltpu.VMEM((SIB + 1, *shard), BF16),
                            pltpu.VMEM((SIB, *shard), BF16),
                            pltpu.SemaphoreType.DMA((SIB + 1,)), pltpu.SemaphoreType.DMA((SIB + 1,)),
                            pltpu.SemaphoreType.DMA((SIB,)), pltpu.SemaphoreType.DMA((SIB,)),
                            pltpu.SemaphoreType.DMA((7,)), pltpu.SemaphoreType.DMA((7,)), pltpu.SemaphoreType.DMA]),
        out_shape=[jax.ShapeDtypeStruct(shard, F32),
                   jax.ShapeDtypeStruct((SIB + 1, *shard), BF16),
                   jax.ShapeDtypeStruct((N_DEV, *small.shape), F32)],
        compiler_params=_params("arbitrary", "arbitrary"),
    )(chip_order, h, dsh, small)


def _adam_math(w, g, m, v):
    m_new = ADAM_B1 * m + (1.0 - ADAM_B1) * g
    v_new = ADAM_B2 * v + (1.0 - ADAM_B2) * (g * g)
    m_hat = m_new / (1.0 - ADAM_B1 ** ADAM_STEP)
    v_hat = v_new / (1.0 - ADAM_B2 ** ADAM_STEP)
    delta = -ADAM_LR * (m_hat / (jnp.sqrt(v_hat) + ADAM_EPS) + ADAM_WD * w)
    return delta, m_new, v_new


def _adam_big(name, own, own_idx, recv, w, m, v):
    rw, cw = w.shape
    rp = own.shape[1]
    steps = 8
    by_cols = rp != rw
    blk_w = (rw, cw // steps) if by_cols else (rw // steps, cw)
    blk_g = (rp, cw // steps) if by_cols else (rw // steps, cw)
    at = (lambda i: (0, i)) if by_cols else (lambda i: (i, 0))

    def body(idx_ref, o_ref, r_ref, w_ref, m_ref, v_ref, g_ref, d_ref, mo_ref, vo_ref):
        g = o_ref[...].astype(F32)
        for r in range(recv.shape[0]):
            g = g + r_ref[r].astype(F32)
        g = g[0:blk_w[0], :]
        g_ref[...] = g
        d_ref[...], mo_ref[...], vo_ref[...] = _adam_math(w_ref[...], g, m_ref[...], v_ref[...])

    spec = pl.BlockSpec(blk_w, lambda i, idx_ref: at(i))
    return pl.pallas_call(
        body, name=name,
        grid_spec=pltpu.PrefetchScalarGridSpec(
            num_scalar_prefetch=1, grid=(steps,),
            in_specs=[pl.BlockSpec((None, *blk_g), lambda i, idx_ref: (idx_ref[0], *at(i))),
                      pl.BlockSpec((recv.shape[0], *blk_g), lambda i, idx_ref: (0, *at(i))), spec, spec, spec],
            out_specs=[spec] * 4),
        out_shape=[jax.ShapeDtypeStruct((rw, cw), F32)] * 4,
        compiler_params=_params("parallel"),
    )(own_idx, own, recv, w, m, v)


def _adam_w_in(own, recv, w, m, v):
    groups = D_MODEL // LANES

    def body(o_ref, r_ref, w_ref, m_ref, v_ref, g_ref, d_ref, mo_ref, vo_ref):
        g = o_ref[...]
        for r in range(recv.shape[0]):
            g = g + r_ref[r].astype(F32)
        for k in range(groups):
            g_ref[pl.ds(k, SHARD_IN, stride=groups), :] = g[0:SHARD_IN, k * LANES:(k + 1) * LANES]
        d_ref[...], mo_ref[...], vo_ref[...] = _adam_math(w_ref[...], g_ref[...], m_ref[...], v_ref[...])

    return pl.pallas_call(
        body, name="adam_w_in",
        out_shape=[jax.ShapeDtypeStruct(w.shape, F32)] * 4,
        compiler_params=pltpu.CompilerParams(vmem_limit_bytes=V7X_VMEM_LIMIT),
    )(own, recv, w, m, v)


def _adam_small(small_all, params):
    flat = [a for triple in params for a in triple]
    n_par = len(params)

    def body(s_ref, *refs):
        ins, outs, loss_ref = refs[:3 * n_par], refs[3 * n_par:-1], refs[-1]
        g_slab = s_ref[0]
        for dev in range(1, N_DEV):
            g_slab = g_slab + s_ref[dev]
        loss_ref[...] = g_slab[SMALL_LOSS:SMALL_LOSS + 1, :]
        dev = 4 * lax.axis_index("x") + 2 * lax.axis_index("y") + lax.axis_index("c")
        alpha_full = jnp.concatenate([g_slab[SMALL_W_ALPHA + half * B_GATE_RANK:SMALL_W_ALPHA + (half + 1) * B_GATE_RANK]
                                      for half in range(B_KEY_WIDTH // LANES)], axis=1)
        alpha_mine = pltpu.roll(alpha_full, (B_KEY_WIDTH - dev * SHARD_ALPHA) % B_KEY_WIDTH, 1)[:, 0:SHARD_ALPHA]
        grads = [_take_rows(g_slab, SMALL_G_IN, D_MODEL // LANES), _take_rows(g_slab, SMALL_G_FINAL, D_MODEL // LANES),
                 _take_rows(g_slab, SMALL_G_GLA, B_WIDTH // LANES), _take_rows(g_slab, SMALL_B_ALPHA, B_KEY_WIDTH // LANES),
                 g_slab[SMALL_SINKS:SMALL_SINKS + 1, 0:A_HEADS], alpha_mine]
        for i, g in enumerate(grads):
            w_ref, m_ref, v_ref = ins[3 * i:3 * i + 3]
            delta, m_new, v_new = _adam_math(w_ref[...], g, m_ref[...], v_ref[...])
            outs[4 * i][...] = g
            outs[4 * i + 1][...] = delta
            outs[4 * i + 2][...] = m_new
            outs[4 * i + 3][...] = v_new

    res = pl.pallas_call(
        body, name="adam_small",
        out_shape=[jax.ShapeDtypeStruct(t[0].shape, F32) for t in params for _ in range(4)]
                  + [jax.ShapeDtypeStruct((1, LANES), F32)],
    )(small_all, *flat)
    return [res[4 * i:4 * i + 4] for i in range(n_par)], res[-1]


def _local_step(x, cosf, sinf, loss_target, g_in, wt_sh, wa_pad, b_alpha, sinks, g_gla, out_shards, g_final, chip_order):
    B, S, _ = x.shape
    T = B * S
    x2 = x.reshape(T, D_MODEL)
    tgt2 = loss_target.reshape(T, D_MODEL)
    f, (g_woa, g_wob, g_wo) = _in_proj(x2, cosf, sinf, g_in, wt_sh, wa_pad, b_alpha, out_shards)
    w_o = g_wo.reshape(D_MODEL, D_MODEL)
    sink_row = jnp.repeat(sinks, WINDOW).reshape(1, ATT_ROWS)
    sink_col = sink_row.reshape(ATT_ROWS, 1)
    attn, lse = _attn_fwd(f["qkv"], sink_row, B, S)
    o_gla, st_all = _gla_fwd(f["q"], f["k"], f["cum"], f["vb"], B, S)
    (dxres, dattn, dog, dza, dzb, dga, dgb, dw_o, dw_oa, dw_ob, small_a) = _merge(
        x2, tgt2, attn, f["za"], o_gla, f["zb"], f["ga"], f["gb"], g_woa, g_wob, w_o, g_gla, g_final)
    dq, dkv, dsink = _attn_bwd(f["qkv"], dattn, attn, lse, sink_col, B, S)
    (dqb, dkb, dvb, dla), (rv_o, rv_oa, rv_ob) = _gla_bwd(f["q"], f["k"], f["cum"], f["vb"], dog, st_all, B, S,
                                                        [dw_o, dw_oa, dw_ob])
    parts = dict(dq=dq, dkv=dkv, dza=dza, dqb=dqb, dkb=dkb, dvb=dvb, dzb=dzb, dla=dla, u=f["u"], alr=f["alr"],
                 dga=dga, dgb=dgb)
    dx, dsh, small_c = _in_proj_bwd(x2, dxres, cosf, sinf, g_in, f["wt_pad"], wa_pad, parts)
    small = jnp.concatenate([small_a, dsink, small_c], axis=0)
    own_in, rv_in, small_all = _w_in_grad_rs(f["h"], dsh, chip_order, small)
    return dict(grad_x=dx.reshape(B, S, D_MODEL), own_in=own_in, rv_in=rv_in,
                own_o=dw_o, rv_o=rv_o, own_oa=dw_oa, rv_oa=rv_oa, own_ob=dw_ob, rv_ob=rv_ob, small_all=small_all)


def kernel(x, positions, g_in, w_in, w_alpha_up, b_alpha, attn_sinks, g_gla_norm, w_out_a, w_out_b, w_o, g_final, loss_target, m_g_in, m_w_in, m_w_alpha_up, m_b_alpha, m_attn_sinks, m_g_gla_norm, m_w_out_a, m_w_out_b, m_w_o, m_g_final, v_g_in, v_w_in, v_w_alpha_up, v_b_alpha, v_attn_sinks, v_g_gla_norm, v_w_out_a, v_w_out_b, v_w_o, v_g_final):
    xi, yi, ci = _my_place()
    dev_idx = (4 * xi + 2 * yi + ci).reshape(1).astype(jnp.int32)
    chip = 2 * xi + yi
    chip_order = jnp.stack([chip ^ 3, chip ^ 2, chip ^ 1, chip]).astype(jnp.int32)

    (g_win, g_wa), cosf, sinf = _gather_first(
        [jnp.pad(w_in[0].T.astype(BF16), ((0, SHARD_PAD - SHARD_IN), (0, 0))), w_alpha_up[0].astype(BF16)],
        positions.reshape(-1, 1))
    wt_sh = g_win.reshape(N_DEV * SHARD_PAD, D_MODEL)
    wa_pad = jnp.pad(jnp.concatenate([g_wa[j] for j in range(N_DEV)], axis=1), ((0, RANK_PAD - B_GATE_RANK), (0, 0)))

    r = _local_step(x, cosf, sinf, loss_target, g_in, wt_sh, wa_pad, b_alpha, attn_sinks[0], g_gla_norm,
                    [w_out_a[0].astype(BF16), w_out_b[0].astype(BF16), w_o[0].astype(BF16)],
                    g_final.reshape(1, D_MODEL), chip_order)

    as_stored = lambda a: a[0].T.reshape(SHARD_IN * D_MODEL // LANES, LANES)
    as_given = lambda a: a.reshape(SHARD_IN, D_MODEL).T
    big = [[as_given(a) for a in _adam_w_in(r["own_in"], r["rv_in"], as_stored(w_in), as_stored(m_w_in), as_stored(v_w_in))],
           _adam_big("adam_w_out_a", r["own_oa"], dev_idx, r["rv_oa"], w_out_a[0], m_w_out_a[0], v_w_out_a[0]),
           _adam_big("adam_w_out_b", r["own_ob"], dev_idx, r["rv_ob"], w_out_b[0], m_w_out_b[0], v_w_out_b[0]),
           _adam_big("adam_w_o", r["own_o"], dev_idx, r["rv_o"], w_o[0], m_w_o[0], v_w_o[0])]
    row = lambda a: a.reshape(1, D_MODEL)
    (s_in, s_final, s_gla, s_ba, s_sinks, s_wa), loss_row = _adam_small(r["small_all"], [
        (g_in, m_g_in, v_g_in), (row(g_final), row(m_g_final), row(v_g_final)),
        (g_gla_norm, m_g_gla_norm, v_g_gla_norm), (b_alpha, m_b_alpha, v_b_alpha),
        (attn_sinks, m_attn_sinks, v_attn_sinks), (w_alpha_up[0], m_w_alpha_up[0], v_w_alpha_up[0])])

    def group(i):
        return (s_in[i], big[0][i][None], s_wa[i][None], s_ba[i], s_sinks[i], s_gla[i], big[1][i][None], big[2][i][None],
                big[3][i][None], s_final[i].reshape(D_MODEL))

    return (loss_row[0, 0], r["grad_x"], *group(0), *group(1), *group(2), *group(3))
```

```python
import functools
import math

import numpy as np
import jax
import jax.numpy as jnp
from jax import lax
from jax.experimental import pallas as pl
from jax.experimental.pallas import tpu as pltpu

F32 = jnp.float32
BF16 = jnp.bfloat16
MESH = pl.DeviceIdType.MESH

D_MODEL = 1024
A_HEADS, A_KV_HEADS, A_HEAD_DIM = 8, 2, 64
A_WIDTH, A_KV_WIDTH = 512, 128
WINDOW = 128
ROPE_THETA = 500000.0
ROPE_DIM = 16
B_HEADS, B_KEY_DIM, B_VAL_DIM = 4, 64, 128
B_KEY_WIDTH, B_WIDTH = 256, 512
B_GATE_RANK = 16
B_GATE_TEMP = 16.0
B_CHUNK = 64
NORM_EPS = 1e-6
NEG_BIG = -1e30
D_IN = 4880
N_DEV = 8
N_CHIPS = 4
ADAM_LR, ADAM_B1, ADAM_B2, ADAM_EPS, ADAM_WD, ADAM_STEP = 0.001, 0.9, 0.999, 1e-08, 0.01, 10

LANES = 128
V7X_VMEM_LIMIT = 56 * 1024 * 1024

RANK_PAD = LANES
SEG = {}
_off = 0
for _name, _w in (("qa", 512), ("ka", 128), ("va", 128), ("za", 512), ("qb", 256), ("kb", 256),
                  ("vb", 512), ("zb", 512), ("alr", RANK_PAD), ("ga", 1024), ("gb", 1024)):
    SEG[_name] = (_off, _off + _w)
    _off += _w
D_IN_PAD = _off
ALR_SRC = SEG["alr"][0]
QKV_K, QKV_V, QKV_W = SEG["ka"][0], SEG["va"][0], SEG["va"][1]
ATT_SCALE = A_HEAD_DIM ** -0.5

SHARD_IN = D_IN // N_DEV
SHARD_PAD = 640
SHARD_OUT = D_MODEL // N_DEV
SHARD_ALPHA = B_KEY_WIDTH // N_DEV

SMALL_G_FINAL, SMALL_G_GLA, SMALL_LOSS, SMALL_SINKS, SMALL_G_IN, SMALL_B_ALPHA, SMALL_W_ALPHA = 0, 8, 12, 16, 24, 32, 40
SMALL_ROWS = 72


def _dot(a, b):
    return jnp.dot(a, b, preferred_element_type=F32)


def _dot_nt(a, b):
    return lax.dot_general(a, b, (((1,), (1,)), ((), ())), preferred_element_type=F32)


def _dot_tn(a, b):
    return lax.dot_general(a, b, (((0,), (0,)), ((), ())), preferred_element_type=F32)


def _sigmoid(z):
    return 1.0 / (1.0 + jnp.exp(-z))


def _sigmoid_tanh(z):
    return 0.5 * jnp.tanh(0.5 * z) + 0.5


def _params(*sem):
    return pltpu.CompilerParams(dimension_semantics=sem, vmem_limit_bytes=V7X_VMEM_LIMIT)


def _const_spec(shape):
    nd = len(shape)
    return pl.BlockSpec(shape, lambda *_: (0,) * nd, pipeline_mode=pl.Buffered(1))


def _lane_iota(shape):
    return lax.broadcasted_iota(jnp.int32, shape, 1)


def _row_iota(shape):
    return lax.broadcasted_iota(jnp.int32, shape, 0)


def _split3(v):
    hi = v.astype(BF16)
    r1 = v - hi.astype(F32)
    mid = r1.astype(BF16)
    lo = (r1 - mid.astype(F32)).astype(BF16)
    return hi, mid, lo


def _put_rows(ref, row0, vec):
    for r in range(vec.shape[1] // LANES):
        ref[row0 + r:row0 + r + 1, :] = vec[:, r * LANES:(r + 1) * LANES]


def _take_rows(slab, row0, n):
    return jnp.concatenate([slab[row0 + r:row0 + r + 1, :] for r in range(n)], axis=1)


def _rope_lane_constants():
    half = ROPE_DIM // 2
    inv_freq = np.exp(-math.log(ROPE_THETA) * np.arange(half, dtype=np.float32) * np.float32(2.0 / ROPE_DIM)).astype(np.float32)
    lane = np.arange(LANES)
    j = lane % A_HEAD_DIM
    invf = np.where(j < ROPE_DIM, inv_freq[j % half], 0.0).astype(np.float32)
    sign = np.where(j < half, -1.0, np.where(j < ROPE_DIM, 1.0, 0.0)).astype(np.float32)
    return jnp.asarray(invf)[None, :], jnp.asarray(sign)[None, :]


def _rope_slab(t, cos, sin_signed):
    first = (_lane_iota(t.shape) % A_HEAD_DIM) < (ROPE_DIM // 2)
    partner = jnp.where(first, pltpu.roll(t, LANES - ROPE_DIM // 2, 1), pltpu.roll(t, ROPE_DIM // 2, 1))
    return t * cos + partner * sin_signed


def _shard_pad_cols(j):
    cut = ALR_SRC + B_GATE_RANK
    shift = RANK_PAD - B_GATE_RANK
    a, b = j * SHARD_IN, (j + 1) * SHARD_IN
    if b <= cut:
        return [(a, b)]
    if a >= cut:
        return [(a + shift, b + shift)]
    return [(a, cut), (cut + shift, b + shift)]


def _in_proj(x2, cosf, sinf, g_in, wt_sh, wa_pad, b_alpha, later_shards):
    T = x2.shape[0]
    tm = math.gcd(T, 512)
    last = T // tm - 1
    nl = len(later_shards)

    def body(x_ref, cos_ref, sin_ref, g_ref, wsh_ref, wa_ref, ba_ref, *rest):
        sh_refs, rest = rest[:nl], rest[nl:]
        (h_ref, qkv_ref, za_ref, q_ref, k_ref, vb_ref, zb_ref, alr_ref, u_ref, cum_ref, ga_ref, gb_ref, wt_out) = rest[:13]
        all_refs, (wt_ref, send_sems, recv_sems, local_sems, wt_sem) = rest[13:13 + nl], rest[13 + nl:]
        wt_copy = pltpu.make_async_copy(wt_ref, wt_out, wt_sem)
        px, py, pc = _my_place()
        my_dev = 4 * px + 2 * py + pc

        def wcopy(a, r, slot):
            dx, dy, dc = FLIPS[r]
            return pltpu.make_async_remote_copy(
                src_ref=sh_refs[a], dst_ref=all_refs[a].at[slot], send_sem=send_sems.at[a, r],
                recv_sem=recv_sems.at[a, r], device_id=(px ^ dx, py ^ dy, pc ^ dc), device_id_type=MESH)

        keep = [pltpu.make_async_copy(sh_refs[a], all_refs[a].at[my_dev], local_sems.at[a]) for a in range(nl)]

        @pl.when(pl.program_id(0) == 0)
        def _():
            for a in range(nl):
                keep[a].start()
                for r in range(len(FLIPS)):
                    wcopy(a, r, my_dev).start()

        @pl.when(pl.program_id(0) == 0)
        def _():
            for j in range(N_DEV):
                src = j * SHARD_PAD
                for a, b in _shard_pad_cols(j):
                    wt_ref[a:b, :] = wsh_ref[src:src + b - a, :]
                    src += b - a
            a, b = SEG["alr"]
            wt_ref[a + B_GATE_RANK:b, :] = jnp.zeros((RANK_PAD - B_GATE_RANK, D_MODEL), BF16)
            wt_copy.start()

        x = x_ref[...]
        r = lax.rsqrt(jnp.mean(x * x, axis=-1, keepdims=True) + NORM_EPS)
        h = (x * r * g_ref[...]).astype(BF16)
        h_ref[...] = h

        def seg(name):
            a, b = SEG[name]
            return _dot_nt(h, wt_ref[a:b, :])

        cos, sin = cos_ref[...], sin_ref[...]
        qa = seg("qa") * ATT_SCALE
        for s in range(A_WIDTH // LANES):
            qkv_ref[:, s * LANES:(s + 1) * LANES] = _rope_slab(qa[:, s * LANES:(s + 1) * LANES], cos, sin).astype(BF16)
        qkv_ref[:, QKV_K:QKV_V] = _rope_slab(seg("ka"), cos, sin).astype(BF16)
        qkv_ref[:, QKV_V:QKV_W] = seg("va").astype(BF16)
        za_ref[...] = seg("za")
        q_ref[...] = seg("qb")
        k_ref[...] = seg("kb")
        vb_ref[...] = seg("vb").astype(BF16)
        zb_ref[...] = seg("zb")
        ga_ref[...] = seg("ga")
        gb_ref[...] = seg("gb")
        alr = seg("alr").astype(BF16)
        alr_ref[...] = alr
        u = _dot(alr, wa_ref[...]) + ba_ref[...]
        u_ref[...] = u
        log_a = (jnp.minimum(u, 0.0) - jnp.log(1.0 + jnp.exp(-jnp.abs(u)))) * (1.0 / B_GATE_TEMP)
        row, col = _row_iota((tm, tm)), _lane_iota((tm, tm))
        tri = ((row // B_CHUNK == col // B_CHUNK) & (col <= row)).astype(BF16)
        hi, mid, lo = _split3(log_a)
        cum_ref[...] = _dot(tri, hi) + _dot(tri, mid) + _dot(tri, lo)

        @pl.when(pl.program_id(0) == last)
        def _():
            for a in range(nl):
                for r, (dx, dy, dc) in enumerate(FLIPS):
                    wcopy(a, r, 4 * (px ^ dx) + 2 * (py ^ dy) + (pc ^ dc)).wait_recv()
                    wcopy(a, r, my_dev).wait_send()
                keep[a].wait()
            wt_copy.wait()

    def rows(w):
        return pl.BlockSpec((tm, w), lambda i: (i, 0))

    outs = [("h", D_MODEL, BF16), ("qkv", QKV_W, BF16), ("za", A_WIDTH, F32), ("q", B_KEY_WIDTH, F32),
            ("k", B_KEY_WIDTH, F32), ("vb", B_WIDTH, BF16), ("zb", B_WIDTH, F32), ("alr", RANK_PAD, BF16),
            ("u", B_KEY_WIDTH, F32), ("cum", B_KEY_WIDTH, F32), ("ga", D_MODEL, F32), ("gb", D_MODEL, F32)]
    res = pl.pallas_call(
        body, name="in_proj", grid=(T // tm,),
        in_specs=[rows(D_MODEL), rows(LANES), rows(LANES), _const_spec((1, D_MODEL)),
                  _const_spec((N_DEV * SHARD_PAD, D_MODEL)), _const_spec((RANK_PAD, B_KEY_WIDTH)),
                  _const_spec((1, B_KEY_WIDTH))] + _any_specs(nl),
        out_specs=[rows(w) for _, w, _ in outs] + _any_specs(1 + nl),
        out_shape=[jax.ShapeDtypeStruct((T, w), dt) for _, w, dt in outs]
                  + [jax.ShapeDtypeStruct((D_IN_PAD, D_MODEL), BF16)]
                  + [jax.ShapeDtypeStruct((N_DEV, *sh.shape), sh.dtype) for sh in later_shards],
        scratch_shapes=[pltpu.VMEM((D_IN_PAD, D_MODEL), BF16),
                        pltpu.SemaphoreType.DMA((nl, len(FLIPS))), pltpu.SemaphoreType.DMA((nl, len(FLIPS))),
                        pltpu.SemaphoreType.DMA((nl,)), pltpu.SemaphoreType.DMA],
        compiler_params=_params("arbitrary"),
    )(x2, cosf, sinf, g_in, wt_sh, wa_pad, b_alpha, *later_shards)
    n_out = len(outs) + 1
    return dict(zip([n for n, _, _ in outs] + ["wt_pad"], res[:n_out])), res[n_out:]


def _dup_kv_head(t, g):
    tf = t.astype(F32)
    keep = (_lane_iota(tf.shape) < A_HEAD_DIM) == (g == 0)
    return jnp.where(keep, tf, pltpu.roll(tf, A_HEAD_DIM, 1)).astype(BF16)


def _stack_heads(t):
    lo = _lane_iota(t.shape) < A_HEAD_DIM
    zero = jnp.zeros_like(t)
    return jnp.concatenate([jnp.where(lo, t, zero), jnp.where(lo, zero, t)], axis=0)


ATT_ROWS = A_HEADS * WINDOW
GROUP_ROWS = ATT_ROWS // A_KV_HEADS
HEADS_PER_GROUP = A_HEADS // A_KV_HEADS


def _band_mask_t(n):
    kj = _row_iota((2 * WINDOW, GROUP_ROWS)) - WINDOW
    qi = _lane_iota((2 * WINDOW, GROUP_ROWS)) % WINDOW
    return (kj <= qi) & (qi - kj < WINDOW) & ((n > 0) | (kj >= 0))


def _stacked_queries(ref, g):
    pairs = range(g * HEADS_PER_GROUP // 2, (g + 1) * HEADS_PER_GROUP // 2)
    return jnp.concatenate([_stack_heads(ref[:, p * LANES:(p + 1) * LANES]) for p in pairs], axis=0)


def _unstack_heads(t, g, ref, dtype):
    lo = _lane_iota((WINDOW, LANES)) < A_HEAD_DIM
    for hh in range(HEADS_PER_GROUP // 2):
        p = g * HEADS_PER_GROUP // 2 + hh
        ref[:, p * LANES:(p + 1) * LANES] = jnp.where(lo, t[2 * hh * WINDOW:(2 * hh + 1) * WINDOW],
                                                       t[(2 * hh + 1) * WINDOW:(2 * hh + 2) * WINDOW]).astype(dtype)


FWD_BLOCKS = 8


def _attn_fwd(qkv, sink_row, B, S):
    T = B * S
    nb = S // WINDOW
    blocks = math.gcd(nb, FWD_BLOCKS)
    steps = nb // blocks

    def one_block(has_prev, sink_ref, q, k, v, o_ref, lse_ref):
        valid = _band_mask_t(has_prev)
        lse_rows = []
        for g in range(A_KV_HEADS):
            kd, vd = _dup_kv_head(k, g), _dup_kv_head(v, g)
            s = jnp.where(valid, _dot_nt(kd, _stacked_queries(q, g)), NEG_BIG)
            sink = sink_ref[:, g * GROUP_ROWS:(g + 1) * GROUP_ROWS]
            m = jnp.maximum(jnp.max(s, axis=0, keepdims=True), sink)
            e = jnp.exp(s - m)
            den = jnp.sum(e, axis=0, keepdims=True) + jnp.exp(sink - m)
            o = _dot_tn((e * (1.0 / den)).astype(BF16), vd)
            _unstack_heads(o, g, o_ref, F32)
            lse = m + jnp.log(den)
            lse_rows += [lse[:, j * WINDOW:(j + 1) * WINDOW] for j in range(HEADS_PER_GROUP)]
        by_head = jnp.concatenate(lse_rows + [jnp.zeros((WINDOW - A_HEADS, WINDOW), F32)], axis=0)
        lse_ref[...] = by_head.T

    def body(sink_ref, q_ref, kc_ref, vc_ref, kp_ref, vp_ref, o_ref, lse_ref):
        k_all = jnp.concatenate([kp_ref[...], kc_ref[...]], axis=0)
        v_all = jnp.concatenate([vp_ref[...], vc_ref[...]], axis=0)
        for j in range(blocks):
            rows = pl.ds(j * WINDOW, WINDOW)
            keys = slice(j * WINDOW, (j + 2) * WINDOW)
            has_prev = pl.program_id(1) if j == 0 else 1
            one_block(has_prev, sink_ref, q_ref[rows, :], k_all[keys], v_all[keys], o_ref.at[rows], lse_ref.at[rows])

    def cur(col, w):
        return pl.BlockSpec((blocks * WINDOW, w), lambda b, n: (b * steps + n, col))

    def prev(col):
        return pl.BlockSpec((WINDOW, LANES), lambda b, n: (b * nb + jnp.maximum(blocks * n - 1, 0), col))

    kcol, vcol = QKV_K // LANES, QKV_V // LANES
    return pl.pallas_call(
        body, name="attn_fwd", grid=(B, steps),
        in_specs=[_const_spec((1, ATT_ROWS)), cur(0, A_WIDTH), cur(kcol, LANES), cur(vcol, LANES), prev(kcol), prev(vcol)],
        out_specs=[cur(0, A_WIDTH), cur(0, LANES)],
        out_shape=[jax.ShapeDtypeStruct((T, A_WIDTH), F32), jax.ShapeDtypeStruct((T, LANES), F32)],
        compiler_params=_params("parallel", "parallel"),
    )(sink_row, qkv, qkv, qkv, qkv, qkv)


ATT_CHUNK = 64


def _chunk_masks(n):
    masks = []
    for half in range(WINDOW // ATT_CHUNK):
        qi = _row_iota((ATT_CHUNK, 2 * WINDOW)) + half * ATT_CHUNK
        kj = _lane_iota((ATT_CHUNK, 2 * WINDOW)) - WINDOW
        masks.append((kj <= qi) & (qi - kj < WINDOW) & ((n > 0) | (kj >= 0)))
    return masks


def _all_stacked_queries(ref):
    return jnp.concatenate([_stacked_queries(ref, g) for g in range(A_KV_HEADS)], axis=0)


def _by_group(fn, lhs, rhs_per_group):
    return jnp.concatenate([fn(lhs[g * GROUP_ROWS:(g + 1) * GROUP_ROWS], rhs_per_group[g])
                            for g in range(A_KV_HEADS)], axis=0)


def _attn_bwd(qkv, do, out, lse, sink_col, B, S):
    T = B * S
    nb = S // WINDOW

    def body(sink_ref, q_ref, kc_ref, vc_ref, kp_ref, vp_ref, do_ref, out_ref, lse_ref,
             dq_ref, dkv_ref, dsink_ref, carry_ref, s_ref, dp_ref, ds_ref, p_ref):
        b, n = pl.program_id(0), pl.program_id(1)
        active = n < nb
        masks = _chunk_masks(jnp.minimum(n, nb - 1))

        @pl.when((b == 0) & (n == 0))
        def _():
            dsink_ref[...] = jnp.zeros_like(dsink_ref)

        k = jnp.concatenate([kp_ref[...], kc_ref[...]], axis=0)
        v = jnp.concatenate([vp_ref[...], vc_ref[...]], axis=0)
        kd = [_dup_kv_head(k, g) for g in range(A_KV_HEADS)]
        vd = [_dup_kv_head(v, g) for g in range(A_KV_HEADS)]
        qs = _all_stacked_queries(q_ref)
        dos = _all_stacked_queries(do_ref)
        s_ref[...] = _by_group(_dot_nt, qs, kd)
        dp_ref[...] = _by_group(_dot_nt, dos, vd)
        lane = _lane_iota((ATT_CHUNK, LANES))
        lo = lane < A_HEAD_DIM
        lane1 = _lane_iota((1, LANES))
        dsink_row = jnp.zeros((1, LANES), F32)
        for c in range(ATT_ROWS // ATT_CHUNK):
            rows = slice(c * ATT_CHUNK, (c + 1) * ATT_CHUNK)
            head, half = divmod(c, WINDOW // ATT_CHUNK)
            qrows = slice(half * ATT_CHUNK, (half + 1) * ATT_CHUNK)
            slab = slice((head // 2) * LANES, (head // 2 + 1) * LANES)
            lse_col = jnp.sum(jnp.where(lane == head, lse_ref[qrows, :], 0.0), axis=-1, keepdims=True)
            prod = do_ref[qrows, slab].astype(F32) * out_ref[qrows, slab]
            mine = lo if head % 2 == 0 else jnp.logical_not(lo)
            delta = jnp.sum(jnp.where(mine, prod, 0.0), axis=-1, keepdims=True)
            s = jnp.where(masks[half], s_ref[rows, :], NEG_BIG)
            prob = jnp.exp(s - lse_col)
            p_ref[rows, :] = prob.astype(BF16)
            ds_ref[rows, :] = (prob * (dp_ref[rows, :] - delta)).astype(BF16)
            w = -jnp.exp(sink_ref[rows, :] - lse_col) * delta
            dsink_row += jnp.where(lane1 == head, jnp.sum(w, axis=0, keepdims=True), 0.0)
        dq = _by_group(_dot, ds_ref[...], kd) * ATT_SCALE
        lo_q = _lane_iota((WINDOW, LANES)) < A_HEAD_DIM
        for p in range(A_HEADS // 2):
            dq_ref[:, p * LANES:(p + 1) * LANES] = jnp.where(
                lo_q, dq[2 * p * WINDOW:(2 * p + 1) * WINDOW], dq[(2 * p + 1) * WINDOW:(2 * p + 2) * WINDOW]).astype(BF16)
        lane2 = _lane_iota((2 * WINDOW, LANES))
        dk_tot = jnp.zeros((2 * WINDOW, LANES), F32)
        dv_tot = jnp.zeros((2 * WINDOW, LANES), F32)
        for g in range(A_KV_HEADS):
            grows = slice(g * GROUP_ROWS, (g + 1) * GROUP_ROWS)
            dk_acc = _dot_tn(ds_ref[grows, :], qs[grows])
            dv_acc = _dot_tn(p_ref[grows, :], dos[grows])
            mine = (lane2 < A_HEAD_DIM) == (g == 0)
            dk_tot = jnp.where(mine, dk_acc + pltpu.roll(dk_acc, A_HEAD_DIM, 1), dk_tot)
            dv_tot = jnp.where(mine, dv_acc + pltpu.roll(dv_acc, A_HEAD_DIM, 1), dv_tot)
        gate = jnp.where(active, 1.0, 0.0)
        dsink_ref[0:1, :] += dsink_row * gate
        dkv_ref[:, 0:LANES] = (carry_ref[:, 0:LANES] + dk_tot[:WINDOW] * gate).astype(BF16)
        dkv_ref[:, LANES:] = (carry_ref[:, LANES:] + dv_tot[:WINDOW] * gate).astype(BF16)
        carry_ref[:, 0:LANES] = dk_tot[WINDOW:]
        carry_ref[:, LANES:] = dv_tot[WINDOW:]

    def cur(col, w):
        return pl.BlockSpec((WINDOW, w), lambda b, n: (b * nb + jnp.minimum(n, nb - 1), col))

    def prev(col):
        return pl.BlockSpec((WINDOW, LANES), lambda b, n: (b * nb + jnp.maximum(jnp.minimum(n, nb - 1) - 1, 0), col))

    lag = pl.BlockSpec((WINDOW, 2 * LANES), lambda b, n: (b * nb + jnp.maximum(n - 1, 0), 0))
    kcol, vcol = QKV_K // LANES, QKV_V // LANES
    scores = (ATT_ROWS, 2 * WINDOW)
    return pl.pallas_call(
        body, name="attn_bwd", grid=(B, nb + 1),
        in_specs=[_const_spec((ATT_ROWS, 1)), cur(0, A_WIDTH), cur(kcol, LANES), cur(vcol, LANES),
                  prev(kcol), prev(vcol), cur(0, A_WIDTH), cur(0, A_WIDTH), cur(0, LANES)],
        out_specs=[cur(0, A_WIDTH), lag, pl.BlockSpec((8, LANES), lambda b, n: (0, 0))],
        out_shape=[jax.ShapeDtypeStruct((T, A_WIDTH), BF16), jax.ShapeDtypeStruct((T, 2 * LANES), BF16),
                   jax.ShapeDtypeStruct((8, LANES), F32)],
        scratch_shapes=[pltpu.VMEM((WINDOW, 2 * LANES), F32), pltpu.VMEM(scores, F32), pltpu.VMEM(scores, F32),
                        pltpu.VMEM(scores, BF16), pltpu.VMEM(scores, BF16)],
        compiler_params=_params("arbitrary", "arbitrary"),
    )(sink_col, qkv, qkv, qkv, qkv, qkv, do, out, lse)


GLA_TILE = 256
CHUNKS_PER_TILE = GLA_TILE // B_CHUNK


def _gla_factors(q_ref, k_ref, cum_ref):
    scale = B_KEY_DIM ** -0.5
    cum = cum_ref[...]
    shape = (B_CHUNK, B_KEY_WIDTH)
    last = jnp.concatenate([jnp.broadcast_to(cum_ref[pl.ds(c * B_CHUNK + B_CHUNK - 1, 1), :], shape)
                            for c in range(CHUNKS_PER_TILE)], axis=0)
    mid = jnp.concatenate([jnp.broadcast_to(cum_ref[pl.ds(c * B_CHUNK + B_CHUNK // 2 - 1, 1), :], shape)
                           for c in range(CHUNKS_PER_TILE)], axis=0)
    e_qm, e_km, e_qe, e_kd = jnp.exp(cum - mid), jnp.exp(mid - cum), jnp.exp(cum), jnp.exp(last - cum)
    qs = q_ref[...] * scale
    k = k_ref[...]
    return qs, k, (e_qm, e_km, e_qe, e_kd)


def _head_mask(shape, h):
    return (_lane_iota(shape) // B_KEY_DIM) == h


def _stack_masked(t):
    return jnp.concatenate([jnp.where(_head_mask(t.shape, h), t, 0.0) for h in range(B_HEADS)], axis=0).astype(BF16)


def _select_heads(t):
    shape = (B_CHUNK, B_KEY_WIDTH)
    out = jnp.zeros(shape, F32)
    for h in range(B_HEADS):
        out = jnp.where(_head_mask(shape, h), t[h * B_CHUNK:(h + 1) * B_CHUNK], out)
    return out


def _select_state(t):
    shape = (B_VAL_DIM, B_KEY_WIDTH)
    out = jnp.zeros(shape, F32)
    for h in range(B_HEADS):
        out = jnp.where(_head_mask(shape, h), t[h * B_VAL_DIM:(h + 1) * B_VAL_DIM], out)
    return out


def _rows_by_head(t):
    return jnp.concatenate([t[:, h * B_VAL_DIM:(h + 1) * B_VAL_DIM] for h in range(B_HEADS)], axis=0)


def _intra_mask():
    i, j = _row_iota((GLA_TILE, GLA_TILE)), _lane_iota((GLA_TILE, GLA_TILE))
    return (i // B_CHUNK == j // B_CHUNK) & (j <= i)


def _pair_stack(t, p):
    slab = t[:, p * LANES:(p + 1) * LANES]
    lo = _lane_iota(slab.shape) < B_KEY_DIM
    return jnp.concatenate([jnp.where(lo, slab, 0.0), jnp.where(lo, 0.0, slab)], axis=0).astype(BF16)


def _gla_fwd(q, k, cum, vb, B, S):
    T = B * S
    nt = S // GLA_TILE

    def one_sequence(q_ref, k_ref, cum_ref, v_ref, o_ref, st_all_ref, st_ref):
        qs, kk, (e_qm, e_km, e_qe, e_kd) = _gla_factors(q_ref, k_ref, cum_ref)
        qm, km, qe, kd = qs * e_qm, kk * e_km, qs * e_qe, (kk * e_kd).astype(BF16)
        mask = _intra_mask()
        intra = []
        for p in range(B_HEADS // 2):
            a = _dot_nt(_pair_stack(qm, p), km[:, p * LANES:(p + 1) * LANES].astype(BF16))
            for hh in range(2):
                h = 2 * p + hh
                att = jnp.where(mask, a[hh * GLA_TILE:(hh + 1) * GLA_TILE], 0.0).astype(BF16)
                intra.append(_dot(att, v_ref[:, h * B_VAL_DIM:(h + 1) * B_VAL_DIM]))
        inter = []
        for c in range(CHUNKS_PER_TILE):
            rows = slice(c * B_CHUNK, (c + 1) * B_CHUNK)
            st = st_ref[...]
            st_all_ref[c] = st
            inter.append(_dot_nt(_stack_masked(qe[rows]), st.astype(BF16)))
            inc = _select_state(_dot_tn(v_ref[rows, :], kd[rows]))
            decay = jnp.exp(cum_ref[pl.ds(c * B_CHUNK + B_CHUNK - 1, 1), :])
            st_ref[...] = st * decay + inc
        for h in range(B_HEADS):
            oi = jnp.concatenate([inter[c][h * B_CHUNK:(h + 1) * B_CHUNK] for c in range(CHUNKS_PER_TILE)], axis=0)
            o_ref[:, h * B_VAL_DIM:(h + 1) * B_VAL_DIM] = intra[h] + oi

    def body(q_ref, k_ref, cum_ref, v_ref, o_ref, st_all_ref, st_ref):
        @pl.when(pl.program_id(0) == 0)
        def _():
            st_ref[...] = jnp.zeros_like(st_ref)

        for b in range(B):
            one_sequence(*[r.at[b] for r in (q_ref, k_ref, cum_ref, v_ref, o_ref, st_all_ref, st_ref)])

    def rows(w):
        return pl.BlockSpec((B, GLA_TILE, w), lambda t: (0, t, 0))

    seq = lambda a: a.reshape(B, S, a.shape[-1])
    o, st_all = pl.pallas_call(
        body, name="gla_fwd", grid=(nt,),
        in_specs=[rows(B_KEY_WIDTH), rows(B_KEY_WIDTH), rows(B_KEY_WIDTH), rows(B_WIDTH)],
        out_specs=[rows(B_WIDTH),
                   pl.BlockSpec((B, CHUNKS_PER_TILE, B_VAL_DIM, B_KEY_WIDTH), lambda t: (0, t, 0, 0))],
        out_shape=[jax.ShapeDtypeStruct((B, S, B_WIDTH), F32),
                   jax.ShapeDtypeStruct((B, S // B_CHUNK, B_VAL_DIM, B_KEY_WIDTH), F32)],
        scratch_shapes=[pltpu.VMEM((B, B_VAL_DIM, B_KEY_WIDTH), F32)],
        compiler_params=_params("arbitrary"),
    )(seq(q), seq(k), seq(cum), seq(vb))
    return o.reshape(T, B_WIDTH), st_all.reshape(T // B_CHUNK, B_VAL_DIM, B_KEY_WIDTH)


def _gla_bwd(q, k, cum, vb, do, st_all, B, S, wgrads):
    T = B * S
    nt = S // GLA_TILE
    scale = B_KEY_DIM ** -0.5
    nw = len(wgrads)

    def one_sequence(q_ref, k_ref, cum_ref, v_ref, do_ref, st_all_ref, dq_ref, dk_ref, dv_ref, dla_ref, dst_ref):
        qs, kk, (e_qm, e_km, e_qe, e_kd) = _gla_factors(q_ref, k_ref, cum_ref)
        qm, km, qe, kd = qs * e_qm, kk * e_km, qs * e_qe, kk * e_kd
        mask = _intra_mask()
        dqm_slabs, dkm_slabs, dv_intra = [], [], []
        for p in range(B_HEADS // 2):
            qm_st = _pair_stack(qm, p)
            km_p = km[:, p * LANES:(p + 1) * LANES].astype(BF16)
            a = _dot_nt(qm_st, km_p)
            da_blocks, dqm_h = [], []
            for hh in range(2):
                h = 2 * p + hh
                vs = slice(h * B_VAL_DIM, (h + 1) * B_VAL_DIM)
                att = jnp.where(mask, a[hh * GLA_TILE:(hh + 1) * GLA_TILE], 0.0).astype(BF16)
                dv_intra.append(_dot_tn(att, do_ref[:, vs]))
                da = jnp.where(mask, _dot_nt(do_ref[:, vs], v_ref[:, vs]), 0.0).astype(BF16)
                da_blocks.append(da)
                dqm_h.append(_dot(da, km_p))
            lo = _lane_iota((GLA_TILE, LANES)) < B_KEY_DIM
            dqm_slabs.append(jnp.where(lo, dqm_h[0], dqm_h[1]))
            dkm_slabs.append(_dot_tn(jnp.concatenate(da_blocks, axis=0), qm_st))
        dqm = jnp.concatenate(dqm_slabs, axis=1)
        dkm = jnp.concatenate(dkm_slabs, axis=1)

        dqe_c, dkd_c, dv_inter, tail_c = ([None] * CHUNKS_PER_TILE for _ in range(4))
        for c in reversed(range(CHUNKS_PER_TILE)):
            rows = slice(c * B_CHUNK, (c + 1) * B_CHUNK)
            dst = dst_ref[...]
            dst_b = dst.astype(BF16)
            dv_inter[c] = _dot_nt(_stack_masked(kd[rows]), dst_b)
            dkd_c[c] = _select_heads(_dot(_rows_by_head(v_ref[rows, :]), dst_b))
            do_c = do_ref[rows, :]
            dqe_c[c] = _select_heads(_dot(_rows_by_head(do_c), st_all_ref[c].astype(BF16)))
            contrib = _select_state(_dot_tn(do_c, qe[rows].astype(BF16)))
            decay = jnp.exp(cum_ref[pl.ds(c * B_CHUNK + B_CHUNK - 1, 1), :])
            tail = (jnp.sum(kk[rows] * dkd_c[c] * e_kd[rows], axis=0, keepdims=True)
                    + decay * jnp.sum(st_all_ref[c] * dst, axis=0, keepdims=True))
            tail_c[c] = jnp.broadcast_to(tail, (B_CHUNK, B_KEY_WIDTH))
            dst_ref[...] = dst * decay + contrib
        dqe = jnp.concatenate(dqe_c, axis=0)
        dkd = jnp.concatenate(dkd_c, axis=0)
        dqs = dqm * e_qm + dqe * e_qe
        dk = dkm * e_km + dkd * e_kd
        dq_ref[...] = (dqs * scale).astype(BF16)
        dk_ref[...] = dk.astype(BF16)
        for h in range(B_HEADS):
            dvi = jnp.concatenate([dv_inter[c][h * B_CHUNK:(h + 1) * B_CHUNK] for c in range(CHUNKS_PER_TILE)], axis=0)
            dv_ref[:, h * B_VAL_DIM:(h + 1) * B_VAL_DIM] = (dv_intra[h] + dvi).astype(BF16)
        dd = qs * dqs - kk * dk
        i, j = _row_iota((GLA_TILE, GLA_TILE)), _lane_iota((GLA_TILE, GLA_TILE))
        upper = ((i // B_CHUNK == j // B_CHUNK) & (j >= i)).astype(BF16)
        hi, mid, lo3 = _split3(dd)
        dla_ref[...] = _dot(upper, hi) + _dot(upper, mid) + _dot(upper, lo3) + jnp.concatenate(tail_c, axis=0)

    def body(q_ref, k_ref, cum_ref, v_ref, do_ref, st_all_ref, *rest):
        g_refs, (dq_ref, dk_ref, dv_ref, dla_ref) = rest[:nw], rest[nw:nw + 4]
        rv_refs, (dst_ref, send_sems, recv_sems) = rest[nw + 4:2 * nw + 4], rest[2 * nw + 4:]
        x, y, c = _my_place()

        def wcopy(a, r):
            dx, dy, dc = FLIPS[r]
            return pltpu.make_async_remote_copy(
                src_ref=g_refs[a].at[4 * (x ^ dx) + 2 * (y ^ dy) + (c ^ dc)], dst_ref=rv_refs[a].at[r],
                send_sem=send_sems.at[a, r], recv_sem=recv_sems.at[a, r],
                device_id=(x ^ dx, y ^ dy, c ^ dc), device_id_type=MESH)

        @pl.when(pl.program_id(0) == 0)
        def _():
            dst_ref[...] = jnp.zeros_like(dst_ref)
            for a in range(nw):
                for r in range(len(FLIPS)):
                    wcopy(a, r).start()

        for b in range(B):
            one_sequence(*[r.at[b] for r in (q_ref, k_ref, cum_ref, v_ref, do_ref, st_all_ref,
                                             dq_ref, dk_ref, dv_ref, dla_ref, dst_ref)])

        @pl.when(pl.program_id(0) == nt - 1)
        def _():
            for a in range(nw):
                for r in range(len(FLIPS)):
                    wcopy(a, r).wait()

    def rows(w):
        return pl.BlockSpec((B, GLA_TILE, w), lambda t: (0, nt - 1 - t, 0))

    seq = lambda a: a.reshape(B, S, a.shape[-1])
    res = pl.pallas_call(
        body, name="gla_bwd", grid=(nt,),
        in_specs=[rows(B_KEY_WIDTH), rows(B_KEY_WIDTH), rows(B_KEY_WIDTH), rows(B_WIDTH), rows(B_WIDTH),
                  pl.BlockSpec((B, CHUNKS_PER_TILE, B_VAL_DIM, B_KEY_WIDTH), lambda t: (0, nt - 1 - t, 0, 0))]
                 + _any_specs(nw),
        out_specs=[rows(B_KEY_WIDTH), rows(B_KEY_WIDTH), rows(B_WIDTH), rows(B_KEY_WIDTH)] + _any_specs(nw),
        out_shape=[jax.ShapeDtypeStruct((B, S, B_KEY_WIDTH), BF16), jax.ShapeDtypeStruct((B, S, B_KEY_WIDTH), BF16),
                   jax.ShapeDtypeStruct((B, S, B_WIDTH), BF16), jax.ShapeDtypeStruct((B, S, B_KEY_WIDTH), F32)]
                  + [jax.ShapeDtypeStruct((len(FLIPS), *g.shape[1:]), g.dtype) for g in wgrads],
        scratch_shapes=[pltpu.VMEM((B, B_VAL_DIM, B_KEY_WIDTH), F32),
                        pltpu.SemaphoreType.DMA((nw, len(FLIPS))), pltpu.SemaphoreType.DMA((nw, len(FLIPS)))],
        compiler_params=_params("arbitrary"),
    )(seq(q), seq(k), seq(cum), seq(vb), seq(do), st_all.reshape(B, S // B_CHUNK, B_VAL_DIM, B_KEY_WIDTH), *wgrads)
    return [a.reshape(T, a.shape[-1]) for a in res[:4]], res[4:]


def _merge(x2, tgt2, attn, za, o_gla, zb, ga, gb, w_oa_sh, w_ob_sh, w_o, g_gla, g_final):
    T = x2.shape[0]
    tm = 256
    last = T // tm - 1

    def body(x_ref, tgt_ref, attn_ref, za_ref, og_ref, zb_ref, ga_ref, gb_ref,
             woa_sh_ref, wob_sh_ref, wo_ref, gg_ref, gf_ref,
             dxres_ref, dattn_ref, dog_ref, dza_ref, dzb_ref, dga_ref, dgb_ref,
             dwo_ref, dwoa_ref, dwob_ref, small_ref,
             awo_ref, awoa_ref, awob_ref, agf_ref, agg_ref, loss_ref, woa_ref, wob_ref):
        @pl.when(pl.program_id(0) == 0)
        def _():
            for r in (awo_ref, awoa_ref, awob_ref, agf_ref, agg_ref, loss_ref):
                r[...] = jnp.zeros_like(r)
            for j in range(N_DEV):
                woa_ref[:, j * SHARD_OUT:(j + 1) * SHARD_OUT] = woa_sh_ref[j]
                wob_ref[:, j * SHARD_OUT:(j + 1) * SHARD_OUT] = wob_sh_ref[j]

        za_v = za_ref[...]
        sig_za = _sigmoid_tanh(za_v)
        silu_a = za_v * sig_za
        attn_v = attn_ref[...]
        oa = (attn_v * silu_a).astype(BF16)
        ya = _dot(oa, woa_ref[...])
        og = og_ref[...]
        zb_v = zb_ref[...]
        sig_zb = _sigmoid_tanh(zb_v)
        silu_b = zb_v * sig_zb
        gg = gg_ref[...]
        on_parts, rinv_parts = [], []
        for h in range(B_HEADS):
            seg = og[:, h * B_VAL_DIM:(h + 1) * B_VAL_DIM]
            rinv = lax.rsqrt(jnp.mean(seg * seg, axis=-1, keepdims=True) + NORM_EPS)
            rinv_parts.append(rinv)
            on_parts.append(seg * rinv)
        on = jnp.concatenate(on_parts, axis=1)
        obn = on * gg
        ob = (obn * silu_b).astype(BF16)
        yb = _dot(ob, wob_ref[...])
        sig_a, sig_b = _sigmoid_tanh(ga_ref[...]), _sigmoid_tanh(gb_ref[...])
        merged = (sig_a * ya + sig_b * yb).astype(BF16)
        out = x_ref[...] + _dot(merged, wo_ref[...])
        rf = lax.rsqrt(jnp.mean(out * out, axis=-1, keepdims=True) + NORM_EPS)
        nrm = out * rf
        gf = gf_ref[...]
        err = nrm * gf - tgt_ref[...]
        loss_ref[...] += jnp.sum(err * err) * (0.5 / D_MODEL)

        dy = err * (1.0 / D_MODEL)
        agf_ref[...] += jnp.sum(dy * nrm, axis=0, keepdims=True)
        dn = dy * gf
        dout = rf * (dn - nrm * jnp.mean(dn * nrm, axis=-1, keepdims=True))
        dxres_ref[...] = dout
        dout_b = dout.astype(BF16)
        dmerged = _dot_nt(dout_b, wo_ref[...])
        awo_ref[...] += _dot_tn(merged, dout_b)
        dya = dmerged * sig_a
        dyb = dmerged * sig_b
        dga_ref[...] = (dmerged * ya * sig_a * (1.0 - sig_a)).astype(BF16)
        dgb_ref[...] = (dmerged * yb * sig_b * (1.0 - sig_b)).astype(BF16)
        dya_b, dyb_b = dya.astype(BF16), dyb.astype(BF16)
        awoa_ref[...] += _dot_tn(oa, dya_b)
        awob_ref[...] += _dot_tn(ob, dyb_b)
        doa = _dot_nt(dya_b, woa_ref[...])
        dattn_ref[...] = (doa * silu_a).astype(BF16)
        dza_ref[...] = (doa * attn_v * (sig_za * (1.0 + za_v * (1.0 - sig_za)))).astype(BF16)
        dob = _dot_nt(dyb_b, wob_ref[...])
        dzb_ref[...] = (dob * obn * (sig_zb * (1.0 + zb_v * (1.0 - sig_zb)))).astype(BF16)
        dobn = dob * silu_b
        agg_ref[...] += jnp.sum(dobn * on, axis=0, keepdims=True)
        don = dobn * gg
        for h in range(B_HEADS):
            sl = slice(h * B_VAL_DIM, (h + 1) * B_VAL_DIM)
            don_h, on_h = don[:, sl], on[:, sl]
            dog_ref[:, sl] = (rinv_parts[h] * (don_h - on_h * jnp.mean(don_h * on_h, axis=-1, keepdims=True))).astype(BF16)

        @pl.when(pl.program_id(0) == last)
        def _():
            for j in range(N_DEV):
                dwo_ref[j] = awo_ref[j * SHARD_OUT:(j + 1) * SHARD_OUT, :].astype(BF16)
                dwoa_ref[j] = awoa_ref[:, j * SHARD_OUT:(j + 1) * SHARD_OUT].astype(BF16)
                dwob_ref[j] = awob_ref[:, j * SHARD_OUT:(j + 1) * SHARD_OUT].astype(BF16)
            small_ref[...] = jnp.zeros_like(small_ref)
            _put_rows(small_ref, SMALL_G_FINAL, agf_ref[...])
            _put_rows(small_ref, SMALL_G_GLA, agg_ref[...])
            small_ref[SMALL_LOSS:SMALL_LOSS + 1, :] = loss_ref[...]

    def rows(w):
        return pl.BlockSpec((tm, w), lambda i: (i, 0))

    def whole(shape):
        nd = len(shape)
        return pl.BlockSpec(shape, lambda i: (0,) * nd)

    outs = [((T, D_MODEL), F32, rows(D_MODEL)), ((T, A_WIDTH), BF16, rows(A_WIDTH)), ((T, B_WIDTH), BF16, rows(B_WIDTH)),
            ((T, A_WIDTH), BF16, rows(A_WIDTH)), ((T, B_WIDTH), BF16, rows(B_WIDTH)),
            ((T, D_MODEL), BF16, rows(D_MODEL)), ((T, D_MODEL), BF16, rows(D_MODEL)),
            ((N_DEV, SHARD_OUT, D_MODEL), BF16, whole((N_DEV, SHARD_OUT, D_MODEL))),
            ((N_DEV, A_WIDTH, SHARD_OUT), BF16, whole((N_DEV, A_WIDTH, SHARD_OUT))),
            ((N_DEV, B_WIDTH, SHARD_OUT), BF16, whole((N_DEV, B_WIDTH, SHARD_OUT))),
            ((SMALL_SINKS, LANES), F32, whole((SMALL_SINKS, LANES)))]
    return pl.pallas_call(
        body, name="merge", grid=(T // tm,),
        in_specs=[rows(D_MODEL), rows(D_MODEL), rows(A_WIDTH), rows(A_WIDTH), rows(B_WIDTH), rows(B_WIDTH),
                  rows(D_MODEL), rows(D_MODEL),
                  _const_spec((N_DEV, A_WIDTH, SHARD_OUT)), _const_spec((N_DEV, B_WIDTH, SHARD_OUT)),
                  _const_spec((D_MODEL, D_MODEL)), _const_spec((1, B_WIDTH)), _const_spec((1, D_MODEL))],
        out_specs=[o[2] for o in outs],
        out_shape=[jax.ShapeDtypeStruct(o[0], o[1]) for o in outs],
        scratch_shapes=[pltpu.VMEM((D_MODEL, D_MODEL), F32), pltpu.VMEM((A_WIDTH, D_MODEL), F32),
                        pltpu.VMEM((B_WIDTH, D_MODEL), F32), pltpu.VMEM((1, D_MODEL), F32), pltpu.VMEM((1, B_WIDTH), F32),
                        pltpu.VMEM((1, LANES), F32), pltpu.VMEM((A_WIDTH, D_MODEL), BF16),
                        pltpu.VMEM((B_WIDTH, D_MODEL), BF16)],
        compiler_params=_params("arbitrary"),
    )(x2, tgt2, attn, za, o_gla, zb, ga, gb, w_oa_sh, w_ob_sh, w_o, g_gla, g_final)


def _in_proj_bwd(x2, dxres, cosf, sinf, g_in, wt_pad, wa_pad, parts):
    T = x2.shape[0]
    tm = 256
    last = T // tm - 1
    base = SMALL_G_IN

    def body(x_ref, dxres_ref, cos_ref, sin_ref, g_ref, wt_ref, wa_ref,
             dq_ref, dkv_ref, dza_ref, dqb_ref, dkb_ref, dvb_ref, dzb_ref, dla_ref, u_ref, alr_ref, dga_ref, dgb_ref,
             dx_ref, dsh_ref, small_ref, dproj_ref, agin_ref, aba_ref, awa_ref):
        @pl.when(pl.program_id(0) == 0)
        def _():
            for r in (agin_ref, aba_ref, awa_ref):
                r[...] = jnp.zeros_like(r)

        cos, nsin = cos_ref[...], -sin_ref[...]
        for s in range(A_WIDTH // LANES):
            sl = slice(s * LANES, (s + 1) * LANES)
            dproj_ref[:, sl] = _rope_slab(dq_ref[:, sl].astype(F32), cos, nsin).astype(BF16)
        dproj_ref[:, QKV_K:QKV_V] = _rope_slab(dkv_ref[:, 0:LANES].astype(F32), cos, nsin).astype(BF16)
        dproj_ref[:, QKV_V:QKV_W] = dkv_ref[:, LANES:]

        def put(name, val):
            a, b = SEG[name]
            dproj_ref[:, a:b] = val

        put("za", dza_ref[...])
        put("qb", dqb_ref[...])
        put("kb", dkb_ref[...])
        put("vb", dvb_ref[...])
        put("zb", dzb_ref[...])
        put("ga", dga_ref[...])
        put("gb", dgb_ref[...])
        du = dla_ref[...] * (1.0 / B_GATE_TEMP) * _sigmoid(-u_ref[...])
        aba_ref[...] += jnp.sum(du, axis=0, keepdims=True)
        du_b = du.astype(BF16)
        awa_ref[...] += _dot_tn(alr_ref[...], du_b)
        put("alr", _dot_nt(du_b, wa_ref[...]).astype(BF16))

        for j in range(N_DEV):
            col = (j % 2) * SHARD_PAD
            for a, b in _shard_pad_cols(j):
                dsh_ref[j // 2, :, col:col + b - a] = dproj_ref[:, a:b]
                col += b - a
            dsh_ref[j // 2, :, col:(j % 2 + 1) * SHARD_PAD] = jnp.zeros((tm, SHARD_PAD - SHARD_IN), BF16)

        dh = _dot(dproj_ref[...], wt_ref[...])
        x = x_ref[...]
        r = lax.rsqrt(jnp.mean(x * x, axis=-1, keepdims=True) + NORM_EPS)
        nrm = x * r
        agin_ref[...] += jnp.sum(dh * nrm, axis=0, keepdims=True)
        dn = dh * g_ref[...]
        dx_ref[...] = dxres_ref[...] + r * (dn - nrm * jnp.mean(dn * nrm, axis=-1, keepdims=True))

        @pl.when(pl.program_id(0) == last)
        def _():
            small_ref[...] = jnp.zeros_like(small_ref)
            _put_rows(small_ref, SMALL_G_IN - base, agin_ref[...])
            _put_rows(small_ref, SMALL_B_ALPHA - base, aba_ref[...])
            for half in range(B_KEY_WIDTH // LANES):
                r0 = SMALL_W_ALPHA - base + half * B_GATE_RANK
                small_ref[r0:r0 + B_GATE_RANK, :] = awa_ref[0:B_GATE_RANK, half * LANES:(half + 1) * LANES]

    def rows(w):
        return pl.BlockSpec((tm, w), lambda i: (i, 0))

    names = ["dq", "dkv", "dza", "dqb", "dkb", "dvb", "dzb", "dla", "u", "alr", "dga", "dgb"]
    return pl.pallas_call(
        body, name="in_proj_bwd", grid=(T // tm,),
        in_specs=[rows(D_MODEL), rows(D_MODEL), rows(LANES), rows(LANES), _const_spec((1, D_MODEL)),
                  _const_spec((D_IN_PAD, D_MODEL)), _const_spec((RANK_PAD, B_KEY_WIDTH))]
                 + [rows(parts[n].shape[1]) for n in names],
        out_specs=[rows(D_MODEL), pl.BlockSpec((N_CHIPS, tm, 2 * SHARD_PAD), lambda i: (0, i, 0)),
                   pl.BlockSpec((SMALL_ROWS - base, LANES), lambda i: (0, 0))],
        out_shape=[jax.ShapeDtypeStruct((T, D_MODEL), F32), jax.ShapeDtypeStruct((N_CHIPS, T, 2 * SHARD_PAD), BF16),
                   jax.ShapeDtypeStruct((SMALL_ROWS - base, LANES), F32)],
        scratch_shapes=[pltpu.VMEM((tm, D_IN_PAD), BF16), pltpu.VMEM((1, D_MODEL), F32), pltpu.VMEM((1, B_KEY_WIDTH), F32),
                        pltpu.VMEM((RANK_PAD, B_KEY_WIDTH), F32)],
        compiler_params=_params("arbitrary"),
    )(x2, dxres, cosf, sinf, g_in, wt_pad, wa_pad, *[parts[n] for n in names])


FLIPS = [(dx, dy, dc) for dx in (0, 1) for dy in (0, 1) for dc in (0, 1)][1:]


def _my_place():
    return lax.axis_index("x"), lax.axis_index("y"), lax.axis_index("c")


def _any_specs(n):
    return [pl.BlockSpec(memory_space=pl.ANY)] * n


def _gather_first(shards, pos_col):
    n = len(shards)
    T = pos_col.shape[0]
    rows_per_pass = math.gcd(T, 512)
    invf, sign = _rope_lane_constants()

    def body(*refs):
        ins, (pos_ref, invf_ref, sign_ref) = refs[:n], refs[n:n + 3]
        outs, (cos_ref, sin_ref) = refs[n + 3:2 * n + 3], refs[2 * n + 3:2 * n + 5]
        send_sems, recv_sems, local_sems = refs[2 * n + 5:]
        x, y, c = _my_place()
        me, sibling = (x, y, c), (x, y, 1 - c)
        chips = [(1 - x, y), (x, 1 - y), (1 - x, 1 - y)]

        def block(a, px, py, pc):
            return outs[a].at[4 * px + 2 * py + pc]

        def copy(a, k, blk, to, src=None):
            return pltpu.make_async_remote_copy(
                src_ref=block(a, *blk) if src is None else src, dst_ref=block(a, *blk),
                send_sem=send_sems.at[a, k], recv_sem=recv_sems.at[a, k], device_id=to, device_id_type=MESH)

        mine = [pltpu.make_async_copy(ins[a], block(a, *me), local_sems.at[a]) for a in range(n)]
        for cp in mine:
            cp.start()
        first = []
        for a in range(n):
            first.append(copy(a, 0, me, sibling, src=ins[a]))
            first += [copy(a, 1 + j, me, (*chip, c), src=ins[a]) for j, chip in enumerate(chips)]
        for cp in first:
            cp.start()

        def tables(i, carry):
            rows = pl.ds(pl.multiple_of(i * rows_per_pass, rows_per_pass), rows_per_pass)
            ang = pos_ref[rows, :].astype(F32) * invf_ref[...]
            cos_ref[rows, :] = jnp.cos(ang)
            sin_ref[rows, :] = jnp.sin(ang) * sign_ref[...]
            return carry

        lax.fori_loop(0, T // rows_per_pass, tables, 0)

        passed = []
        for j, chip in enumerate(chips):
            for a in range(n):
                copy(a, 1 + j, (*chip, c), me).wait_recv()
                fwd = copy(a, 4 + j, (*chip, c), sibling)
                fwd.start()
                passed.append(fwd)
        for a in range(n):
            copy(a, 0, sibling, me).wait_recv()
            for j, chip in enumerate(chips):
                copy(a, 4 + j, (*chip, 1 - c), me).wait_recv()
        for cp in first + passed:
            cp.wait_send()
        for cp in mine:
            cp.wait()

    vmem = pl.BlockSpec(memory_space=pltpu.VMEM)
    res = pl.pallas_call(
        body, name="gather_weights",
        in_specs=_any_specs(n) + [vmem] * 3, out_specs=_any_specs(n) + [vmem] * 2,
        out_shape=[jax.ShapeDtypeStruct((N_DEV, *s.shape), s.dtype) for s in shards]
                  + [jax.ShapeDtypeStruct((T, LANES), F32)] * 2,
        scratch_shapes=[pltpu.SemaphoreType.DMA((n, 7)), pltpu.SemaphoreType.DMA((n, 7)), pltpu.SemaphoreType.DMA((n,))],
        compiler_params=pltpu.CompilerParams(vmem_limit_bytes=V7X_VMEM_LIMIT),
    )(*shards, pos_col, invf, sign)
    return res[:n], res[n], res[n + 1]


def _w_in_grad_rs(h, dsh, chip_order, small):
    T = h.shape[0]
    tk = math.gcd(T, 2048)
    nk = T // tk
    chip_flips = [(1, 1), (1, 0), (0, 1)]
    n_steps = len(chip_flips) + 1
    SIB = len(chip_flips)
    k_finish = min(1, nk - 1)

    def body(order_ref, h_ref, d_ref, s_ref, own_ref, recv_ref, sall_ref,
             acc_ref, keep_ref, pre_ref, to_sib_ref, to_chip_ref,
             sib_send, sib_recv, chip_send, chip_recv, ssend_sems, srecv_sems, local_sem):
        i, kk = pl.program_id(0), pl.program_id(1)
        x, y, c = _my_place()
        my_dev = 4 * x + 2 * y + c

        def small_copy(r, slot):
            dx, dy, dc = FLIPS[r]
            return pltpu.make_async_remote_copy(
                src_ref=s_ref, dst_ref=sall_ref.at[slot], send_sem=ssend_sems.at[r], recv_sem=srecv_sems.at[r],
                device_id=(x ^ dx, y ^ dy, c ^ dc), device_id_type=MESH)

        keep_small = pltpu.make_async_copy(s_ref, sall_ref.at[my_dev], local_sem)

        def sib_copy(t):
            dst = recv_ref.at[SIB] if t == SIB else pre_ref.at[t]
            return pltpu.make_async_remote_copy(
                src_ref=to_sib_ref.at[t], dst_ref=dst, send_sem=sib_send.at[t], recv_sem=sib_recv.at[t],
                device_id=(x, y, 1 - c), device_id_type=MESH)

        def chip_copy(t):
            dx, dy = chip_flips[t]
            return pltpu.make_async_remote_copy(
                src_ref=to_chip_ref.at[t], dst_ref=recv_ref.at[t], send_sem=chip_send.at[t], recv_sem=chip_recv.at[t],
                device_id=(x ^ dx, y ^ dy, c), device_id_type=MESH)

        def halves():
            first, second = acc_ref[0:SHARD_PAD, :], acc_ref[SHARD_PAD:2 * SHARD_PAD, :]
            return jnp.where(c == 0, first, second), jnp.where(c == 0, second, first)

        @pl.when((i == 0) & (kk == 0))
        def _():
            keep_small.start()
            for r in range(len(FLIPS)):
                small_copy(r, my_dev).start()

        @pl.when(kk == 0)
        def _():
            acc_ref[...] = jnp.zeros_like(acc_ref)

        acc_ref[...] += _dot_tn(d_ref[...], h_ref[...])

        for t in range(len(chip_flips)):
            @pl.when((i == t + 1) & (kk == k_finish))
            def _(t=t):
                sib_copy(t).wait_recv()
                to_chip_ref[t] = (keep_ref[...] + pre_ref[t].astype(F32)).astype(BF16)
                chip_copy(t).start()

        for t in range(len(chip_flips)):
            @pl.when((i == t) & (kk == nk - 1))
            def _(t=t):
                mine, theirs = halves()
                to_sib_ref[t] = theirs.astype(BF16)
                sib_copy(t).start()
                keep_ref[...] = mine

        @pl.when((i == n_steps - 1) & (kk == nk - 1))
        def _():
            mine, theirs = halves()
            own_ref[...] = mine
            to_sib_ref[SIB] = theirs.astype(BF16)
            sib_copy(SIB).start()
            for t in range(len(chip_flips)):
                sib_copy(t).wait_send()
                chip_copy(t).wait_send()
                chip_copy(t).wait_recv()
            sib_copy(SIB).wait_send()
            sib_copy(SIB).wait_recv()
            for r, (dx, dy, dc) in enumerate(FLIPS):
                small_copy(r, 4 * (x ^ dx) + 2 * (y ^ dy) + (c ^ dc)).wait_recv()
                small_copy(r, my_dev).wait_send()
            keep_small.wait()

    shard = (SHARD_PAD, D_MODEL)
    return pl.pallas_call(
        body, name="w_in_grad_rs",
        grid_spec=pltpu.PrefetchScalarGridSpec(
            num_scalar_prefetch=1, grid=(n_steps, nk),
            in_specs=[pl.BlockSpec((tk, D_MODEL), lambda i, kk, order: (kk, 0)),
                      pl.BlockSpec((None, tk, 2 * SHARD_PAD), lambda i, kk, order: (order[i], kk, 0)),
                      pl.BlockSpec(memory_space=pl.ANY)],
            out_specs=[pl.BlockSpec(shard, lambda i, kk, order: (0, 0)),
                       pl.BlockSpec(memory_space=pl.ANY), pl.BlockSpec(memory_space=pl.ANY)],
            scratch_shapes=[pltpu.VMEM((2 * SHARD_PAD, D_MODEL), F32), pltpu.VMEM(shard, F32),
                            pltpu.VMEM((SIB, *shard), BF16), pltpu.VMEM((SIB + 1, *shard), BF16),
                            pltpu.VMEM((SIB, *shard), BF16),
                            pltpu.SemaphoreType.DMA((SIB + 1,)), pltpu.SemaphoreType.DMA((SIB + 1,)),
                            pltpu.SemaphoreType.DMA((SIB,)), pltpu.SemaphoreType.DMA((SIB,)),
                            pltpu.SemaphoreType.DMA((7,)), pltpu.SemaphoreType.DMA((7,)), pltpu.SemaphoreType.DMA]),
        out_shape=[jax.ShapeDtypeStruct(shard, F32),
                   jax.ShapeDtypeStruct((SIB + 1, *shard), BF16),
                   jax.ShapeDtypeStruct((N_DEV, *small.shape), F32)],
        compiler_params=_params("arbitrary", "arbitrary"),
    )(chip_order, h, dsh, small)


def _adam_math(w, g, m, v):
    m_new = ADAM_B1 * m + (1.0 - ADAM_B1) * g
    v_new = ADAM_B2 * v + (1.0 - ADAM_B2) * (g * g)
    m_hat = m_new / (1.0 - ADAM_B1 ** ADAM_STEP)
    v_hat = v_new / (1.0 - ADAM_B2 ** ADAM_STEP)
    delta = -ADAM_LR * (m_hat / (jnp.sqrt(v_hat) + ADAM_EPS) + ADAM_WD * w)
    return delta, m_new, v_new


def _adam_big(name, own, own_idx, recv, w, m, v):
    rw, cw = w.shape
    rp = own.shape[1]
    steps = 8
    by_cols = rp != rw
    blk_w = (rw, cw // steps) if by_cols else (rw // steps, cw)
    blk_g = (rp, cw // steps) if by_cols else (rw // steps, cw)
    at = (lambda i: (0, i)) if by_cols else (lambda i: (i, 0))

    def body(idx_ref, o_ref, r_ref, w_ref, m_ref, v_ref, g_ref, d_ref, mo_ref, vo_ref):
        g = o_ref[...].astype(F32)
        for r in range(recv.shape[0]):
            g = g + r_ref[r].astype(F32)
        g = g[0:blk_w[0], :]
        g_ref[...] = g
        d_ref[...], mo_ref[...], vo_ref[...] = _adam_math(w_ref[...], g, m_ref[...], v_ref[...])

    spec = pl.BlockSpec(blk_w, lambda i, idx_ref: at(i))
    return pl.pallas_call(
        body, name=name,
        grid_spec=pltpu.PrefetchScalarGridSpec(
            num_scalar_prefetch=1, grid=(steps,),
            in_specs=[pl.BlockSpec((None, *blk_g), lambda i, idx_ref: (idx_ref[0], *at(i))),
                      pl.BlockSpec((recv.shape[0], *blk_g), lambda i, idx_ref: (0, *at(i))), spec, spec, spec],
            out_specs=[spec] * 4),
        out_shape=[jax.ShapeDtypeStruct((rw, cw), F32)] * 4,
        compiler_params=_params("parallel"),
    )(own_idx, own, recv, w, m, v)


def _adam_small(small_all, params):
    flat = [a for triple in params for a in triple]
    n_par = len(params)

    def body(s_ref, *refs):
        ins, outs, loss_ref = refs[:3 * n_par], refs[3 * n_par:-1], refs[-1]
        g_slab = s_ref[0]
        for dev in range(1, N_DEV):
            g_slab = g_slab + s_ref[dev]
        loss_ref[...] = g_slab[SMALL_LOSS:SMALL_LOSS + 1, :]
        dev = 4 * lax.axis_index("x") + 2 * lax.axis_index("y") + lax.axis_index("c")
        alpha_full = jnp.concatenate([g_slab[SMALL_W_ALPHA + half * B_GATE_RANK:SMALL_W_ALPHA + (half + 1) * B_GATE_RANK]
                                      for half in range(B_KEY_WIDTH // LANES)], axis=1)
        alpha_mine = pltpu.roll(alpha_full, (B_KEY_WIDTH - dev * SHARD_ALPHA) % B_KEY_WIDTH, 1)[:, 0:SHARD_ALPHA]
        grads = [_take_rows(g_slab, SMALL_G_IN, D_MODEL // LANES), _take_rows(g_slab, SMALL_G_FINAL, D_MODEL // LANES),
                 _take_rows(g_slab, SMALL_G_GLA, B_WIDTH // LANES), _take_rows(g_slab, SMALL_B_ALPHA, B_KEY_WIDTH // LANES),
                 g_slab[SMALL_SINKS:SMALL_SINKS + 1, 0:A_HEADS], alpha_mine]
        for i, g in enumerate(grads):
            w_ref, m_ref, v_ref = ins[3 * i:3 * i + 3]
            delta, m_new, v_new = _adam_math(w_ref[...], g, m_ref[...], v_ref[...])
            outs[4 * i][...] = g
            outs[4 * i + 1][...] = delta
            outs[4 * i + 2][...] = m_new
            outs[4 * i + 3][...] = v_new

    res = pl.pallas_call(
        body, name="adam_small",
        out_shape=[jax.ShapeDtypeStruct(t[0].shape, F32) for t in params for _ in range(4)]
                  + [jax.ShapeDtypeStruct((1, LANES), F32)],
    )(small_all, *flat)
    return [res[4 * i:4 * i + 4] for i in range(n_par)], res[-1]


def _local_step(x, cosf, sinf, loss_target, g_in, wt_sh, wa_pad, b_alpha, sinks, g_gla, out_shards, g_final, chip_order):
    B, S, _ = x.shape
    T = B * S
    x2 = x.reshape(T, D_MODEL)
    tgt2 = loss_target.reshape(T, D_MODEL)
    f, (g_woa, g_wob, g_wo) = _in_proj(x2, cosf, sinf, g_in, wt_sh, wa_pad, b_alpha, out_shards)
    w_o = g_wo.reshape(D_MODEL, D_MODEL)
    sink_row = jnp.repeat(sinks, WINDOW).reshape(1, ATT_ROWS)
    sink_col = sink_row.reshape(ATT_ROWS, 1)
    attn, lse = _attn_fwd(f["qkv"], sink_row, B, S)
    o_gla, st_all = _gla_fwd(f["q"], f["k"], f["cum"], f["vb"], B, S)
    (dxres, dattn, dog, dza, dzb, dga, dgb, dw_o, dw_oa, dw_ob, small_a) = _merge(
        x2, tgt2, attn, f["za"], o_gla, f["zb"], f["ga"], f["gb"], g_woa, g_wob, w_o, g_gla, g_final)
    dq, dkv, dsink = _attn_bwd(f["qkv"], dattn, attn, lse, sink_col, B, S)
    (dqb, dkb, dvb, dla), (rv_o, rv_oa, rv_ob) = _gla_bwd(f["q"], f["k"], f["cum"], f["vb"], dog, st_all, B, S,
                                                        [dw_o, dw_oa, dw_ob])
    parts = dict(dq=dq, dkv=dkv, dza=dza, dqb=dqb, dkb=dkb, dvb=dvb, dzb=dzb, dla=dla, u=f["u"], alr=f["alr"],
                 dga=dga, dgb=dgb)
    dx, dsh, small_c = _in_proj_bwd(x2, dxres, cosf, sinf, g_in, f["wt_pad"], wa_pad, parts)
    small = jnp.concatenate([small_a, dsink, small_c], axis=0)
    own_in, rv_in, small_all = _w_in_grad_rs(f["h"], dsh, chip_order, small)
    return dict(grad_x=dx.reshape(B, S, D_MODEL), own_in=own_in, rv_in=rv_in,
                own_o=dw_o, rv_o=rv_o, own_oa=dw_oa, rv_oa=rv_oa, own_ob=dw_ob, rv_ob=rv_ob, small_all=small_all)


def kernel(x, positions, g_in, w_in, w_alpha_up, b_alpha, attn_sinks, g_gla_norm, w_out_a, w_out_b, w_o, g_final, loss_target, m_g_in, m_w_in, m_w_alpha_up, m_b_alpha, m_attn_sinks, m_g_gla_norm, m_w_out_a, m_w_out_b, m_w_o, m_g_final, v_g_in, v_w_in, v_w_alpha_up, v_b_alpha, v_attn_sinks, v_g_gla_norm, v_w_out_a, v_w_out_b, v_w_o, v_g_final):
    xi, yi, ci = _my_place()
    dev_idx = (4 * xi + 2 * yi + ci).reshape(1).astype(jnp.int32)
    chip = 2 * xi + yi
    chip_order = jnp.stack([chip ^ 3, chip ^ 2, chip ^ 1, chip]).astype(jnp.int32)

    (g_win, g_wa), cosf, sinf = _gather_first(
        [jnp.pad(w_in[0].T.astype(BF16), ((0, SHARD_PAD - SHARD_IN), (0, 0))), w_alpha_up[0].astype(BF16)],
        positions.reshape(-1, 1))
    wt_sh = g_win.reshape(N_DEV * SHARD_PAD, D_MODEL)
    wa_pad = jnp.pad(jnp.concatenate([g_wa[j] for j in range(N_DEV)], axis=1), ((0, RANK_PAD - B_GATE_RANK), (0, 0)))

    r = _local_step(x, cosf, sinf, loss_target, g_in, wt_sh, wa_pad, b_alpha, attn_sinks[0], g_gla_norm,
                    [w_out_a[0].astype(BF16), w_out_b[0].astype(BF16), w_o[0].astype(BF16)],
                    g_final.reshape(1, D_MODEL), chip_order)

    first = jnp.zeros((1,), jnp.int32)
    big = [[a.T for a in _adam_big("adam_w_in", r["own_in"][None], first, r["rv_in"], w_in[0].T, m_w_in[0].T, v_w_in[0].T)],
           _adam_big("adam_w_out_a", r["own_oa"], dev_idx, r["rv_oa"], w_out_a[0], m_w_out_a[0], v_w_out_a[0]),
           _adam_big("adam_w_out_b", r["own_ob"], dev_idx, r["rv_ob"], w_out_b[0], m_w_out_b[0], v_w_out_b[0]),
           _adam_big("adam_w_o", r["own_o"], dev_idx, r["rv_o"], w_o[0], m_w_o[0], v_w_o[0])]
    row = lambda a: a.reshape(1, D_MODEL)
    (s_in, s_final, s_gla, s_ba, s_sinks, s_wa), loss_row = _adam_small(r["small_all"], [
        (g_in, m_g_in, v_g_in), (row(g_final), row(m_g_final), row(v_g_final)),
        (g_gla_norm, m_g_gla_norm, v_g_gla_norm), (b_alpha, m_b_alpha, v_b_alpha),
        (attn_sinks, m_attn_sinks, v_attn_sinks), (w_alpha_up[0], m_w_alpha_up[0], v_w_alpha_up[0])])

    def group(i):
        return (s_in[i], big[0][i][None], s_wa[i][None], s_ba[i], s_sinks[i], s_gla[i], big[1][i][None], big[2][i][None],
                big[3][i][None], s_final[i].reshape(D_MODEL))

    return (loss_row[0, 0], r["grad_x"], *group(0), *group(1), *group(2), *group(3))
```

```python
import functools
import math

import numpy as np
import jax
import jax.numpy as jnp
from jax import lax
from jax.experimental import pallas as pl
from jax.experimental.pallas import tpu as pltpu

F32 = jnp.float32
BF16 = jnp.bfloat16
MESH = pl.DeviceIdType.MESH

D_MODEL = 1024
A_HEADS, A_KV_HEADS, A_HEAD_DIM = 8, 2, 64
A_WIDTH, A_KV_WIDTH = 512, 128
WINDOW = 128
ROPE_THETA = 500000.0
ROPE_DIM = 16
B_HEADS, B_KEY_DIM, B_VAL_DIM = 4, 64, 128
B_KEY_WIDTH, B_WIDTH = 256, 512
B_GATE_RANK = 16
B_GATE_TEMP = 16.0
B_CHUNK = 64
NORM_EPS = 1e-6
NEG_BIG = -1e30
D_IN = 4880
N_DEV = 8
N_CHIPS = 4
ADAM_LR, ADAM_B1, ADAM_B2, ADAM_EPS, ADAM_WD, ADAM_STEP = 0.001, 0.9, 0.999, 1e-08, 0.01, 10

LANES = 128
V7X_VMEM_LIMIT = 56 * 1024 * 1024
V7X_VMEM_LIMIT_MAX = 62 * 1024 * 1024

RANK_PAD = LANES
SEG = {}
_off = 0
for _name, _w in (("qa", 512), ("ka", 128), ("va", 128), ("za", 512), ("qb", 256), ("kb", 256),
                  ("vb", 512), ("zb", 512), ("alr", RANK_PAD), ("ga", 1024), ("gb", 1024)):
    SEG[_name] = (_off, _off + _w)
    _off += _w
D_IN_PAD = _off
ALR_SRC = SEG["alr"][0]
QKV_K, QKV_V, QKV_W = SEG["ka"][0], SEG["va"][0], SEG["va"][1]
ATT_SCALE = A_HEAD_DIM ** -0.5

SHARD_IN = D_IN // N_DEV
SHARD_PAD = 640
SHARD_OUT = D_MODEL // N_DEV
SHARD_ALPHA = B_KEY_WIDTH // N_DEV

SMALL_G_FINAL, SMALL_G_GLA, SMALL_LOSS, SMALL_SINKS, SMALL_G_IN, SMALL_B_ALPHA, SMALL_W_ALPHA = 0, 8, 12, 16, 24, 32, 40
SMALL_ROWS = 72


def _dot(a, b):
    return jnp.dot(a, b, preferred_element_type=F32)


def _dot_nt(a, b):
    return lax.dot_general(a, b, (((1,), (1,)), ((), ())), preferred_element_type=F32)


def _dot_tn(a, b):
    return lax.dot_general(a, b, (((0,), (0,)), ((), ())), preferred_element_type=F32)


def _sigmoid(z):
    return 1.0 / (1.0 + jnp.exp(-z))


def _sigmoid_tanh(z):
    return 0.5 * jnp.tanh(0.5 * z) + 0.5


def _params(*sem):
    return pltpu.CompilerParams(dimension_semantics=sem, vmem_limit_bytes=V7X_VMEM_LIMIT)


def _const_spec(shape):
    nd = len(shape)
    return pl.BlockSpec(shape, lambda *_: (0,) * nd, pipeline_mode=pl.Buffered(1))


def _lane_iota(shape):
    return lax.broadcasted_iota(jnp.int32, shape, 1)


def _row_iota(shape):
    return lax.broadcasted_iota(jnp.int32, shape, 0)


def _split3(v):
    hi = v.astype(BF16)
    r1 = v - hi.astype(F32)
    mid = r1.astype(BF16)
    lo = (r1 - mid.astype(F32)).astype(BF16)
    return hi, mid, lo


def _put_rows(ref, row0, vec):
    for r in range(vec.shape[1] // LANES):
        ref[row0 + r:row0 + r + 1, :] = vec[:, r * LANES:(r + 1) * LANES]


def _take_rows(slab, row0, n):
    return jnp.concatenate([slab[row0 + r:row0 + r + 1, :] for r in range(n)], axis=1)


def _rope_lane_constants():
    half = ROPE_DIM // 2
    inv_freq = np.exp(-math.log(ROPE_THETA) * np.arange(half, dtype=np.float32) * np.float32(2.0 / ROPE_DIM)).astype(np.float32)
    lane = np.arange(LANES)
    j = lane % A_HEAD_DIM
    invf = np.where(j < ROPE_DIM, inv_freq[j % half], 0.0).astype(np.float32)
    sign = np.where(j < half, -1.0, np.where(j < ROPE_DIM, 1.0, 0.0)).astype(np.float32)
    return jnp.asarray(invf)[None, :], jnp.asarray(sign)[None, :]


def _rope_slab(t, cos, sin_signed):
    first = (_lane_iota(t.shape) % A_HEAD_DIM) < (ROPE_DIM // 2)
    partner = jnp.where(first, pltpu.roll(t, LANES - ROPE_DIM // 2, 1), pltpu.roll(t, ROPE_DIM // 2, 1))
    return t * cos + partner * sin_signed


def _shard_pad_cols(j):
    cut = ALR_SRC + B_GATE_RANK
    shift = RANK_PAD - B_GATE_RANK
    a, b = j * SHARD_IN, (j + 1) * SHARD_IN
    if b <= cut:
        return [(a, b)]
    if a >= cut:
        return [(a + shift, b + shift)]
    return [(a, cut), (cut + shift, b + shift)]


def _in_proj(x2, cosf, sinf, g_in, wt_sh, wa_pad, b_alpha, later_shards):
    T = x2.shape[0]
    tm = math.gcd(T, 512)
    last = T // tm - 1
    nl = len(later_shards)

    def body(x_ref, cos_ref, sin_ref, g_ref, wsh_ref, wa_ref, ba_ref, *rest):
        sh_refs, rest = rest[:nl], rest[nl:]
        (h_ref, qkv_ref, za_ref, q_ref, k_ref, vb_ref, zb_ref, alr_ref, u_ref, cum_ref, ga_ref, gb_ref, wt_out) = rest[:13]
        all_refs, (wt_ref, send_sems, recv_sems, local_sems, wt_sem) = rest[13:13 + nl], rest[13 + nl:]
        wt_copy = pltpu.make_async_copy(wt_ref, wt_out, wt_sem)
        px, py, pc = _my_place()
        my_dev = 4 * px + 2 * py + pc

        def wcopy(a, r, slot):
            dx, dy, dc = FLIPS[r]
            return pltpu.make_async_remote_copy(
                src_ref=sh_refs[a], dst_ref=all_refs[a].at[slot], send_sem=send_sems.at[a, r],
                recv_sem=recv_sems.at[a, r], device_id=(px ^ dx, py ^ dy, pc ^ dc), device_id_type=MESH)

        keep = [pltpu.make_async_copy(sh_refs[a], all_refs[a].at[my_dev], local_sems.at[a]) for a in range(nl)]

        @pl.when(pl.program_id(0) == 0)
        def _():
            for a in range(nl):
                keep[a].start()
                for r in range(len(FLIPS)):
                    wcopy(a, r, my_dev).start()

        @pl.when(pl.program_id(0) == 0)
        def _():
            for j in range(N_DEV):
                src = j * SHARD_PAD
                for a, b in _shard_pad_cols(j):
                    wt_ref[a:b, :] = wsh_ref[src:src + b - a, :]
                    src += b - a
            a, b = SEG["alr"]
            wt_ref[a + B_GATE_RANK:b, :] = jnp.zeros((RANK_PAD - B_GATE_RANK, D_MODEL), BF16)
            wt_copy.start()

        x = x_ref[...]
        r = lax.rsqrt(jnp.mean(x * x, axis=-1, keepdims=True) + NORM_EPS)
        h = (x * r * g_ref[...]).astype(BF16)
        h_ref[...] = h

        def seg(name):
            a, b = SEG[name]
            return _dot_nt(h, wt_ref[a:b, :])

        cos, sin = cos_ref[...], sin_ref[...]
        qa = seg("qa") * ATT_SCALE
        for s in range(A_WIDTH // LANES):
            qkv_ref[:, s * LANES:(s + 1) * LANES] = _rope_slab(qa[:, s * LANES:(s + 1) * LANES], cos, sin).astype(BF16)
        qkv_ref[:, QKV_K:QKV_V] = _rope_slab(seg("ka"), cos, sin).astype(BF16)
        qkv_ref[:, QKV_V:QKV_W] = seg("va").astype(BF16)
        za_ref[...] = seg("za").astype(BF16)
        q_ref[...] = seg("qb")
        k_ref[...] = seg("kb")
        vb_ref[...] = seg("vb").astype(BF16)
        zb_ref[...] = seg("zb").astype(BF16)
        ga_ref[...] = seg("ga").astype(BF16)
        gb_ref[...] = seg("gb").astype(BF16)
        alr = seg("alr").astype(BF16)
        alr_ref[...] = alr
        u = _dot(alr, wa_ref[...]) + ba_ref[...]
        u_ref[...] = u
        log_a = (jnp.minimum(u, 0.0) - jnp.log(1.0 + jnp.exp(-jnp.abs(u)))) * (1.0 / B_GATE_TEMP)
        row, col = _row_iota((tm, tm)), _lane_iota((tm, tm))
        tri = ((row // B_CHUNK == col // B_CHUNK) & (col <= row)).astype(BF16)
        hi, mid, lo = _split3(log_a)
        cum_ref[...] = _dot(tri, hi) + _dot(tri, mid) + _dot(tri, lo)

        @pl.when(pl.program_id(0) == last)
        def _():
            for a in range(nl):
                for r, (dx, dy, dc) in enumerate(FLIPS):
                    wcopy(a, r, 4 * (px ^ dx) + 2 * (py ^ dy) + (pc ^ dc)).wait_recv()
                    wcopy(a, r, my_dev).wait_send()
                keep[a].wait()
            wt_copy.wait()

    def rows(w):
        return pl.BlockSpec((tm, w), lambda i: (i, 0))

    outs = [("h", D_MODEL, BF16), ("qkv", QKV_W, BF16), ("za", A_WIDTH, BF16), ("q", B_KEY_WIDTH, F32),
            ("k", B_KEY_WIDTH, F32), ("vb", B_WIDTH, BF16), ("zb", B_WIDTH, BF16), ("alr", RANK_PAD, BF16),
            ("u", B_KEY_WIDTH, F32), ("cum", B_KEY_WIDTH, F32), ("ga", D_MODEL, BF16), ("gb", D_MODEL, BF16)]
    res = pl.pallas_call(
        body, name="in_proj", grid=(T // tm,),
        in_specs=[rows(D_MODEL), rows(LANES), rows(LANES), _const_spec((1, D_MODEL)),
                  _const_spec((N_DEV * SHARD_PAD, D_MODEL)), _const_spec((RANK_PAD, B_KEY_WIDTH)),
                  _const_spec((1, B_KEY_WIDTH))] + _any_specs(nl),
        out_specs=[rows(w) for _, w, _ in outs] + _any_specs(1 + nl),
        out_shape=[jax.ShapeDtypeStruct((T, w), dt) for _, w, dt in outs]
                  + [jax.ShapeDtypeStruct((D_IN_PAD, D_MODEL), BF16)]
                  + [jax.ShapeDtypeStruct((N_DEV, *sh.shape), sh.dtype) for sh in later_shards],
        scratch_shapes=[pltpu.VMEM((D_IN_PAD, D_MODEL), BF16),
                        pltpu.SemaphoreType.DMA((nl, len(FLIPS))), pltpu.SemaphoreType.DMA((nl, len(FLIPS))),
                        pltpu.SemaphoreType.DMA((nl,)), pltpu.SemaphoreType.DMA],
        compiler_params=_params("arbitrary"),
    )(x2, cosf, sinf, g_in, wt_sh, wa_pad, b_alpha, *later_shards)
    n_out = len(outs) + 1
    return dict(zip([n for n, _, _ in outs] + ["wt_pad"], res[:n_out])), res[n_out:]


def _dup_kv_head(t, g):
    tf = t.astype(F32)
    keep = (_lane_iota(tf.shape) < A_HEAD_DIM) == (g == 0)
    return jnp.where(keep, tf, pltpu.roll(tf, A_HEAD_DIM, 1)).astype(BF16)


def _stack_heads(t):
    lo = _lane_iota(t.shape) < A_HEAD_DIM
    zero = jnp.zeros_like(t)
    return jnp.concatenate([jnp.where(lo, t, zero), jnp.where(lo, zero, t)], axis=0)


ATT_ROWS = A_HEADS * WINDOW
GROUP_ROWS = ATT_ROWS // A_KV_HEADS
HEADS_PER_GROUP = A_HEADS // A_KV_HEADS


def _band_mask_t(n):
    kj = _row_iota((2 * WINDOW, GROUP_ROWS)) - WINDOW
    qi = _lane_iota((2 * WINDOW, GROUP_ROWS)) % WINDOW
    return (kj <= qi) & (qi - kj < WINDOW) & ((n > 0) | (kj >= 0))


def _stacked_queries(ref, g):
    pairs = range(g * HEADS_PER_GROUP // 2, (g + 1) * HEADS_PER_GROUP // 2)
    return jnp.concatenate([_stack_heads(ref[:, p * LANES:(p + 1) * LANES]) for p in pairs], axis=0)


def _unstack_heads(t, g, ref, dtype):
    lo = _lane_iota((WINDOW, LANES)) < A_HEAD_DIM
    for hh in range(HEADS_PER_GROUP // 2):
        p = g * HEADS_PER_GROUP // 2 + hh
        ref[:, p * LANES:(p + 1) * LANES] = jnp.where(lo, t[2 * hh * WINDOW:(2 * hh + 1) * WINDOW],
                                                       t[(2 * hh + 1) * WINDOW:(2 * hh + 2) * WINDOW]).astype(dtype)


FWD_BLOCKS = 8


def _attn_fwd(qkv, sink_row, B, S):
    T = B * S
    nb = S // WINDOW
    blocks = math.gcd(nb, FWD_BLOCKS)
    steps = nb // blocks

    def one_block(has_prev, sink_ref, q, k, v, o_ref, lse_ref):
        valid = _band_mask_t(has_prev)
        lse_rows = []
        for g in range(A_KV_HEADS):
            kd, vd = _dup_kv_head(k, g), _dup_kv_head(v, g)
            s = jnp.where(valid, _dot_nt(kd, _stacked_queries(q, g)), NEG_BIG)
            sink = sink_ref[:, g * GROUP_ROWS:(g + 1) * GROUP_ROWS]
            m = jnp.maximum(jnp.max(s, axis=0, keepdims=True), sink)
            e = jnp.exp(s - m)
            den = jnp.sum(e, axis=0, keepdims=True) + jnp.exp(sink - m)
            o = _dot_tn((e * (1.0 / den)).astype(BF16), vd)
            _unstack_heads(o, g, o_ref, BF16)
            lse = m + jnp.log(den)
            lse_rows += [lse[:, j * WINDOW:(j + 1) * WINDOW] for j in range(HEADS_PER_GROUP)]
        by_head = jnp.concatenate(lse_rows + [jnp.zeros((WINDOW - A_HEADS, WINDOW), F32)], axis=0)
        lse_ref[...] = by_head.T

    def body(sink_ref, q_ref, kc_ref, vc_ref, kp_ref, vp_ref, o_ref, lse_ref):
        k_all = jnp.concatenate([kp_ref[...], kc_ref[...]], axis=0)
        v_all = jnp.concatenate([vp_ref[...], vc_ref[...]], axis=0)
        for j in range(blocks):
            rows = pl.ds(j * WINDOW, WINDOW)
            keys = slice(j * WINDOW, (j + 2) * WINDOW)
            has_prev = pl.program_id(1) if j == 0 else 1
            one_block(has_prev, sink_ref, q_ref[rows, :], k_all[keys], v_all[keys], o_ref.at[rows], lse_ref.at[rows])

    def cur(col, w):
        return pl.BlockSpec((blocks * WINDOW, w), lambda b, n: (b * steps + n, col))

    def prev(col):
        return pl.BlockSpec((WINDOW, LANES), lambda b, n: (b * nb + jnp.maximum(blocks * n - 1, 0), col))

    kcol, vcol = QKV_K // LANES, QKV_V // LANES
    return pl.pallas_call(
        body, name="attn_fwd", grid=(B, steps),
        in_specs=[_const_spec((1, ATT_ROWS)), cur(0, A_WIDTH), cur(kcol, LANES), cur(vcol, LANES), prev(kcol), prev(vcol)],
        out_specs=[cur(0, A_WIDTH), cur(0, LANES)],
        out_shape=[jax.ShapeDtypeStruct((T, A_WIDTH), BF16), jax.ShapeDtypeStruct((T, LANES), F32)],
        compiler_params=_params("parallel", "parallel"),
    )(sink_row, qkv, qkv, qkv, qkv, qkv)


ATT_CHUNK = 64


def _chunk_masks(n):
    masks = []
    for half in range(WINDOW // ATT_CHUNK):
        qi = _row_iota((ATT_CHUNK, 2 * WINDOW)) + half * ATT_CHUNK
        kj = _lane_iota((ATT_CHUNK, 2 * WINDOW)) - WINDOW
        masks.append((kj <= qi) & (qi - kj < WINDOW) & ((n > 0) | (kj >= 0)))
    return masks


def _all_stacked_queries(ref):
    return jnp.concatenate([_stacked_queries(ref, g) for g in range(A_KV_HEADS)], axis=0)


def _by_group(fn, lhs, rhs_per_group):
    return jnp.concatenate([fn(lhs[g * GROUP_ROWS:(g + 1) * GROUP_ROWS], rhs_per_group[g])
                            for g in range(A_KV_HEADS)], axis=0)


def _attn_bwd(qkv, do, out, lse, sink_col, B, S):
    T = B * S
    nb = S // WINDOW

    def body(sink_ref, q_ref, kc_ref, vc_ref, kp_ref, vp_ref, do_ref, out_ref, lse_ref,
             dq_ref, dkv_ref, dsink_ref, carry_ref, s_ref, dp_ref, ds_ref, p_ref):
        b, n = pl.program_id(0), pl.program_id(1)
        active = n < nb
        masks = _chunk_masks(jnp.minimum(n, nb - 1))

        @pl.when((b == 0) & (n == 0))
        def _():
            dsink_ref[...] = jnp.zeros_like(dsink_ref)

        k = jnp.concatenate([kp_ref[...], kc_ref[...]], axis=0)
        v = jnp.concatenate([vp_ref[...], vc_ref[...]], axis=0)
        kd = [_dup_kv_head(k, g) for g in range(A_KV_HEADS)]
        vd = [_dup_kv_head(v, g) for g in range(A_KV_HEADS)]
        qs = _all_stacked_queries(q_ref)
        dos = _all_stacked_queries(do_ref)
        s_ref[...] = _by_group(_dot_nt, qs, kd)
        dp_ref[...] = _by_group(_dot_nt, dos, vd)
        lane = _lane_iota((ATT_CHUNK, LANES))
        lo = lane < A_HEAD_DIM
        lane1 = _lane_iota((1, LANES))
        dsink_row = jnp.zeros((1, LANES), F32)
        for c in range(ATT_ROWS // ATT_CHUNK):
            rows = slice(c * ATT_CHUNK, (c + 1) * ATT_CHUNK)
            head, half = divmod(c, WINDOW // ATT_CHUNK)
            qrows = slice(half * ATT_CHUNK, (half + 1) * ATT_CHUNK)
            slab = slice((head // 2) * LANES, (head // 2 + 1) * LANES)
            lse_col = jnp.sum(jnp.where(lane == head, lse_ref[qrows, :], 0.0), axis=-1, keepdims=True)
            prod = do_ref[qrows, slab].astype(F32) * out_ref[qrows, slab].astype(F32)
            mine = lo if head % 2 == 0 else jnp.logical_not(lo)
            delta = jnp.sum(jnp.where(mine, prod, 0.0), axis=-1, keepdims=True)
            s = jnp.where(masks[half], s_ref[rows, :], NEG_BIG)
            prob = jnp.exp(s - lse_col)
            p_ref[rows, :] = prob.astype(BF16)
            ds_ref[rows, :] = (prob * (dp_ref[rows, :] - delta)).astype(BF16)
            w = -jnp.exp(sink_ref[rows, :] - lse_col) * delta
            dsink_row += jnp.where(lane1 == head, jnp.sum(w, axis=0, keepdims=True), 0.0)
        dq = _by_group(_dot, ds_ref[...], kd) * ATT_SCALE
        lo_q = _lane_iota((WINDOW, LANES)) < A_HEAD_DIM
        for p in range(A_HEADS // 2):
            dq_ref[:, p * LANES:(p + 1) * LANES] = jnp.where(
                lo_q, dq[2 * p * WINDOW:(2 * p + 1) * WINDOW], dq[(2 * p + 1) * WINDOW:(2 * p + 2) * WINDOW]).astype(BF16)
        lane2 = _lane_iota((2 * WINDOW, LANES))
        dk_tot = jnp.zeros((2 * WINDOW, LANES), F32)
        dv_tot = jnp.zeros((2 * WINDOW, LANES), F32)
        for g in range(A_KV_HEADS):
            grows = slice(g * GROUP_ROWS, (g + 1) * GROUP_ROWS)
            dk_acc = _dot_tn(ds_ref[grows, :], qs[grows])
            dv_acc = _dot_tn(p_ref[grows, :], dos[grows])
            mine = (lane2 < A_HEAD_DIM) == (g == 0)
            dk_tot = jnp.where(mine, dk_acc + pltpu.roll(dk_acc, A_HEAD_DIM, 1), dk_tot)
            dv_tot = jnp.where(mine, dv_acc + pltpu.roll(dv_acc, A_HEAD_DIM, 1), dv_tot)
        gate = jnp.where(active, 1.0, 0.0)
        dsink_ref[0:1, :] += dsink_row * gate
        dkv_ref[:, 0:LANES] = (carry_ref[:, 0:LANES] + dk_tot[:WINDOW] * gate).astype(BF16)
        dkv_ref[:, LANES:] = (carry_ref[:, LANES:] + dv_tot[:WINDOW] * gate).astype(BF16)
        carry_ref[:, 0:LANES] = dk_tot[WINDOW:]
        carry_ref[:, LANES:] = dv_tot[WINDOW:]

    def cur(col, w):
        return pl.BlockSpec((WINDOW, w), lambda b, n: (b * nb + jnp.minimum(n, nb - 1), col))

    def prev(col):
        return pl.BlockSpec((WINDOW, LANES), lambda b, n: (b * nb + jnp.maximum(jnp.minimum(n, nb - 1) - 1, 0), col))

    lag = pl.BlockSpec((WINDOW, 2 * LANES), lambda b, n: (b * nb + jnp.maximum(n - 1, 0), 0))
    kcol, vcol = QKV_K // LANES, QKV_V // LANES
    scores = (ATT_ROWS, 2 * WINDOW)
    return pl.pallas_call(
        body, name="attn_bwd", grid=(B, nb + 1),
        in_specs=[_const_spec((ATT_ROWS, 1)), cur(0, A_WIDTH), cur(kcol, LANES), cur(vcol, LANES),
                  prev(kcol), prev(vcol), cur(0, A_WIDTH), cur(0, A_WIDTH), cur(0, LANES)],
        out_specs=[cur(0, A_WIDTH), lag, pl.BlockSpec((8, LANES), lambda b, n: (0, 0))],
        out_shape=[jax.ShapeDtypeStruct((T, A_WIDTH), BF16), jax.ShapeDtypeStruct((T, 2 * LANES), BF16),
                   jax.ShapeDtypeStruct((8, LANES), F32)],
        scratch_shapes=[pltpu.VMEM((WINDOW, 2 * LANES), F32), pltpu.VMEM(scores, F32), pltpu.VMEM(scores, F32),
                        pltpu.VMEM(scores, BF16), pltpu.VMEM(scores, BF16)],
        compiler_params=_params("arbitrary", "arbitrary"),
    )(sink_col, qkv, qkv, qkv, qkv, qkv, do, out, lse)


GLA_TILE = 256
CHUNKS_PER_TILE = GLA_TILE // B_CHUNK


def _gla_factors(q_ref, k_ref, cum_ref):
    scale = B_KEY_DIM ** -0.5
    cum = cum_ref[...]
    shape = (B_CHUNK, B_KEY_WIDTH)
    last = jnp.concatenate([jnp.broadcast_to(cum_ref[pl.ds(c * B_CHUNK + B_CHUNK - 1, 1), :], shape)
                            for c in range(CHUNKS_PER_TILE)], axis=0)
    mid = jnp.concatenate([jnp.broadcast_to(cum_ref[pl.ds(c * B_CHUNK + B_CHUNK // 2 - 1, 1), :], shape)
                           for c in range(CHUNKS_PER_TILE)], axis=0)
    e_qm, e_km, e_qe, e_kd = jnp.exp(cum - mid), jnp.exp(mid - cum), jnp.exp(cum), jnp.exp(last - cum)
    qs = q_ref[...] * scale
    k = k_ref[...]
    return qs, k, (e_qm, e_km, e_qe, e_kd)


def _head_mask(shape, h):
    return (_lane_iota(shape) // B_KEY_DIM) == h


def _stack_masked(t):
    return jnp.concatenate([jnp.where(_head_mask(t.shape, h), t, 0.0) for h in range(B_HEADS)], axis=0).astype(BF16)


def _select_heads(t):
    shape = (B_CHUNK, B_KEY_WIDTH)
    out = jnp.zeros(shape, F32)
    for h in range(B_HEADS):
        out = jnp.where(_head_mask(shape, h), t[h * B_CHUNK:(h + 1) * B_CHUNK], out)
    return out


def _select_state(t):
    shape = (B_VAL_DIM, B_KEY_WIDTH)
    out = jnp.zeros(shape, F32)
    for h in range(B_HEADS):
        out = jnp.where(_head_mask(shape, h), t[h * B_VAL_DIM:(h + 1) * B_VAL_DIM], out)
    return out


def _rows_by_head(t):
    return jnp.concatenate([t[:, h * B_VAL_DIM:(h + 1) * B_VAL_DIM] for h in range(B_HEADS)], axis=0)


def _intra_mask():
    i, j = _row_iota((GLA_TILE, GLA_TILE)), _lane_iota((GLA_TILE, GLA_TILE))
    return (i // B_CHUNK == j // B_CHUNK) & (j <= i)


def _pair_stack(t, p):
    slab = t[:, p * LANES:(p + 1) * LANES]
    lo = _lane_iota(slab.shape) < B_KEY_DIM
    return jnp.concatenate([jnp.where(lo, slab, 0.0), jnp.where(lo, 0.0, slab)], axis=0).astype(BF16)


def _gla_fwd(q, k, cum, vb, B, S):
    T = B * S
    nt = S // GLA_TILE

    def one_sequence(q_ref, k_ref, cum_ref, v_ref, o_ref, st_all_ref, st_ref):
        qs, kk, (e_qm, e_km, e_qe, e_kd) = _gla_factors(q_ref, k_ref, cum_ref)
        qm, km, qe, kd = qs * e_qm, kk * e_km, qs * e_qe, (kk * e_kd).astype(BF16)
        mask = _intra_mask()
        intra = []
        for p in range(B_HEADS // 2):
            a = _dot_nt(_pair_stack(qm, p), km[:, p * LANES:(p + 1) * LANES].astype(BF16))
            for hh in range(2):
                h = 2 * p + hh
                att = jnp.where(mask, a[hh * GLA_TILE:(hh + 1) * GLA_TILE], 0.0).astype(BF16)
                intra.append(_dot(att, v_ref[:, h * B_VAL_DIM:(h + 1) * B_VAL_DIM]))
        inter = []
        for c in range(CHUNKS_PER_TILE):
            rows = slice(c * B_CHUNK, (c + 1) * B_CHUNK)
            st = st_ref[...]
            st_all_ref[c] = st
            inter.append(_dot_nt(_stack_masked(qe[rows]), st.astype(BF16)))
            inc = _select_state(_dot_tn(v_ref[rows, :], kd[rows]))
            decay = jnp.exp(cum_ref[pl.ds(c * B_CHUNK + B_CHUNK - 1, 1), :])
            st_ref[...] = st * decay + inc
        for h in range(B_HEADS):
            oi = jnp.concatenate([inter[c][h * B_CHUNK:(h + 1) * B_CHUNK] for c in range(CHUNKS_PER_TILE)], axis=0)
            o_ref[:, h * B_VAL_DIM:(h + 1) * B_VAL_DIM] = (intra[h] + oi).astype(BF16)

    def body(q_ref, k_ref, cum_ref, v_ref, o_ref, st_all_ref, st_ref):
        @pl.when(pl.program_id(0) == 0)
        def _():
            st_ref[...] = jnp.zeros_like(st_ref)

        for b in range(B):
            one_sequence(*[r.at[b] for r in (q_ref, k_ref, cum_ref, v_ref, o_ref, st_all_ref, st_ref)])

    def rows(w):
        return pl.BlockSpec((B, GLA_TILE, w), lambda t: (0, t, 0))

    seq = lambda a: a.reshape(B, S, a.shape[-1])
    o, st_all = pl.pallas_call(
        body, name="gla_fwd", grid=(nt,),
        in_specs=[rows(B_KEY_WIDTH), rows(B_KEY_WIDTH), rows(B_KEY_WIDTH), rows(B_WIDTH)],
        out_specs=[rows(B_WIDTH),
                   pl.BlockSpec((B, CHUNKS_PER_TILE, B_VAL_DIM, B_KEY_WIDTH), lambda t: (0, t, 0, 0))],
        out_shape=[jax.ShapeDtypeStruct((B, S, B_WIDTH), BF16),
                   jax.ShapeDtypeStruct((B, S // B_CHUNK, B_VAL_DIM, B_KEY_WIDTH), F32)],
        scratch_shapes=[pltpu.VMEM((B, B_VAL_DIM, B_KEY_WIDTH), F32)],
        compiler_params=_params("arbitrary"),
    )(seq(q), seq(k), seq(cum), seq(vb))
    return o.reshape(T, B_WIDTH), st_all.reshape(T // B_CHUNK, B_VAL_DIM, B_KEY_WIDTH)


def _gla_bwd(q, k, cum, vb, do, st_all, B, S, wgrads):
    T = B * S
    nt = S // GLA_TILE
    scale = B_KEY_DIM ** -0.5
    nw = len(wgrads)

    def one_sequence(q_ref, k_ref, cum_ref, v_ref, do_ref, st_all_ref, dq_ref, dk_ref, dv_ref, dla_ref, dst_ref):
        qs, kk, (e_qm, e_km, e_qe, e_kd) = _gla_factors(q_ref, k_ref, cum_ref)
        qm, km, qe, kd = qs * e_qm, kk * e_km, qs * e_qe, kk * e_kd
        mask = _intra_mask()
        dqm_slabs, dkm_slabs, dv_intra = [], [], []
        for p in range(B_HEADS // 2):
            qm_st = _pair_stack(qm, p)
            km_p = km[:, p * LANES:(p + 1) * LANES].astype(BF16)
            a = _dot_nt(qm_st, km_p)
            da_blocks, dqm_h = [], []
            for hh in range(2):
                h = 2 * p + hh
                vs = slice(h * B_VAL_DIM, (h + 1) * B_VAL_DIM)
                att = jnp.where(mask, a[hh * GLA_TILE:(hh + 1) * GLA_TILE], 0.0).astype(BF16)
                dv_intra.append(_dot_tn(att, do_ref[:, vs]))
                da = jnp.where(mask, _dot_nt(do_ref[:, vs], v_ref[:, vs]), 0.0).astype(BF16)
                da_blocks.append(da)
                dqm_h.append(_dot(da, km_p))
            lo = _lane_iota((GLA_TILE, LANES)) < B_KEY_DIM
            dqm_slabs.append(jnp.where(lo, dqm_h[0], dqm_h[1]))
            dkm_slabs.append(_dot_tn(jnp.concatenate(da_blocks, axis=0), qm_st))
        dqm = jnp.concatenate(dqm_slabs, axis=1)
        dkm = jnp.concatenate(dkm_slabs, axis=1)

        dqe_c, dkd_c, dv_inter, tail_c = ([None] * CHUNKS_PER_TILE for _ in range(4))
        for c in reversed(range(CHUNKS_PER_TILE)):
            rows = slice(c * B_CHUNK, (c + 1) * B_CHUNK)
            dst = dst_ref[...]
            dst_b = dst.astype(BF16)
            dv_inter[c] = _dot_nt(_stack_masked(kd[rows]), dst_b)
            dkd_c[c] = _select_heads(_dot(_rows_by_head(v_ref[rows, :]), dst_b))
            do_c = do_ref[rows, :]
            dqe_c[c] = _select_heads(_dot(_rows_by_head(do_c), st_all_ref[c].astype(BF16)))
            contrib = _select_state(_dot_tn(do_c, qe[rows].astype(BF16)))
            decay = jnp.exp(cum_ref[pl.ds(c * B_CHUNK + B_CHUNK - 1, 1), :])
            tail = (jnp.sum(kk[rows] * dkd_c[c] * e_kd[rows], axis=0, keepdims=True)
                    + decay * jnp.sum(st_all_ref[c] * dst, axis=0, keepdims=True))
            tail_c[c] = jnp.broadcast_to(tail, (B_CHUNK, B_KEY_WIDTH))
            dst_ref[...] = dst * decay + contrib
        dqe = jnp.concatenate(dqe_c, axis=0)
        dkd = jnp.concatenate(dkd_c, axis=0)
        dqs = dqm * e_qm + dqe * e_qe
        dk = dkm * e_km + dkd * e_kd
        dq_ref[...] = (dqs * scale).astype(BF16)
        dk_ref[...] = dk.astype(BF16)
        for h in range(B_HEADS):
            dvi = jnp.concatenate([dv_inter[c][h * B_CHUNK:(h + 1) * B_CHUNK] for c in range(CHUNKS_PER_TILE)], axis=0)
            dv_ref[:, h * B_VAL_DIM:(h + 1) * B_VAL_DIM] = (dv_intra[h] + dvi).astype(BF16)
        dd = qs * dqs - kk * dk
        i, j = _row_iota((GLA_TILE, GLA_TILE)), _lane_iota((GLA_TILE, GLA_TILE))
        upper = ((i // B_CHUNK == j // B_CHUNK) & (j >= i)).astype(BF16)
        hi, mid, lo3 = _split3(dd)
        dla_ref[...] = _dot(upper, hi) + _dot(upper, mid) + _dot(upper, lo3) + jnp.concatenate(tail_c, axis=0)

    def body(q_ref, k_ref, cum_ref, v_ref, do_ref, st_all_ref, *rest):
        g_refs, (dq_ref, dk_ref, dv_ref, dla_ref) = rest[:nw], rest[nw:nw + 4]
        rv_refs, (dst_ref, send_sems, recv_sems) = rest[nw + 4:2 * nw + 4], rest[2 * nw + 4:]
        x, y, c = _my_place()

        def wcopy(a, r):
            dx, dy, dc = FLIPS[r]
            return pltpu.make_async_remote_copy(
                src_ref=g_refs[a].at[4 * (x ^ dx) + 2 * (y ^ dy) + (c ^ dc)], dst_ref=rv_refs[a].at[r],
                send_sem=send_sems.at[a, r], recv_sem=recv_sems.at[a, r],
                device_id=(x ^ dx, y ^ dy, c ^ dc), device_id_type=MESH)

        @pl.when(pl.program_id(0) == 0)
        def _():
            dst_ref[...] = jnp.zeros_like(dst_ref)
            for a in range(nw):
                for r in range(len(FLIPS)):
                    wcopy(a, r).start()

        for b in range(B):
            one_sequence(*[r.at[b] for r in (q_ref, k_ref, cum_ref, v_ref, do_ref, st_all_ref,
                                             dq_ref, dk_ref, dv_ref, dla_ref, dst_ref)])

        @pl.when(pl.program_id(0) == nt - 1)
        def _():
            for a in range(nw):
                for r in range(len(FLIPS)):
                    wcopy(a, r).wait()

    def rows(w):
        return pl.BlockSpec((B, GLA_TILE, w), lambda t: (0, nt - 1 - t, 0))

    seq = lambda a: a.reshape(B, S, a.shape[-1])
    res = pl.pallas_call(
        body, name="gla_bwd", grid=(nt,),
        in_specs=[rows(B_KEY_WIDTH), rows(B_KEY_WIDTH), rows(B_KEY_WIDTH), rows(B_WIDTH), rows(B_WIDTH),
                  pl.BlockSpec((B, CHUNKS_PER_TILE, B_VAL_DIM, B_KEY_WIDTH), lambda t: (0, nt - 1 - t, 0, 0))]
                 + _any_specs(nw),
        out_specs=[rows(B_KEY_WIDTH), rows(B_KEY_WIDTH), rows(B_WIDTH), rows(B_KEY_WIDTH)] + _any_specs(nw),
        out_shape=[jax.ShapeDtypeStruct((B, S, B_KEY_WIDTH), BF16), jax.ShapeDtypeStruct((B, S, B_KEY_WIDTH), BF16),
                   jax.ShapeDtypeStruct((B, S, B_WIDTH), BF16), jax.ShapeDtypeStruct((B, S, B_KEY_WIDTH), F32)]
                  + [jax.ShapeDtypeStruct((len(FLIPS), *g.shape[1:]), g.dtype) for g in wgrads],
        scratch_shapes=[pltpu.VMEM((B, B_VAL_DIM, B_KEY_WIDTH), F32),
                        pltpu.SemaphoreType.DMA((nw, len(FLIPS))), pltpu.SemaphoreType.DMA((nw, len(FLIPS)))],
        compiler_params=_params("arbitrary"),
    )(seq(q), seq(k), seq(cum), seq(vb), seq(do), st_all.reshape(B, S // B_CHUNK, B_VAL_DIM, B_KEY_WIDTH), *wgrads)
    return [a.reshape(T, a.shape[-1]) for a in res[:4]], res[4:]


def _merge(x2, tgt2, attn, za, o_gla, zb, ga, gb, w_oa_sh, w_ob_sh, w_o, g_gla, g_final):
    T = x2.shape[0]
    tm = math.gcd(T, 512)
    sub = math.gcd(tm, 256)
    last = T // tm - 1

    def body(x_ref, tgt_ref, attn_ref, za_ref, og_ref, zb_ref, ga_ref, gb_ref,
             woa_sh_ref, wob_sh_ref, wo_ref, gg_ref, gf_ref,
             dxres_ref, dattn_ref, dog_ref, dza_ref, dzb_ref, dga_ref, dgb_ref,
             dwo_out, dwoa_out, dwob_out, small_ref,
             awo_ref, awoa_ref, awob_ref, agf_ref, agg_ref, loss_ref, woa_ref, wob_ref,
             dwo_ref, dwoa_ref, dwob_ref, w_sems, dw_sems):
        w_copies = [pltpu.make_async_copy(sh.at[j], dst.at[:, j * SHARD_OUT:(j + 1) * SHARD_OUT], w_sems.at[a, j])
                    for a, (sh, dst) in enumerate(((woa_sh_ref, woa_ref), (wob_sh_ref, wob_ref))) for j in range(N_DEV)]
        dw_copies = [pltpu.make_async_copy(src, dst, dw_sems.at[a])
                     for a, (src, dst) in enumerate(((dwo_ref, dwo_out), (dwoa_ref, dwoa_out), (dwob_ref, dwob_out)))]

        @pl.when(pl.program_id(0) == 0)
        def _():
            for cp in w_copies:
                cp.start()
            for r in (awo_ref, awoa_ref, awob_ref, agf_ref, agg_ref, loss_ref):
                r[...] = jnp.zeros_like(r)
            for cp in w_copies:
                cp.wait()

        def one_tile(rows):
            za_v = za_ref[rows, :].astype(F32)
            sig_za = _sigmoid_tanh(za_v)
            silu_a = za_v * sig_za
            attn_v = attn_ref[rows, :].astype(F32)
            oa = (attn_v * silu_a).astype(BF16)
            ya = _dot(oa, woa_ref[...])
            og = og_ref[rows, :].astype(F32)
            zb_v = zb_ref[rows, :].astype(F32)
            sig_zb = _sigmoid_tanh(zb_v)
            silu_b = zb_v * sig_zb
            gg = gg_ref[...]
            on_parts, rinv_parts = [], []
            for h in range(B_HEADS):
                seg = og[:, h * B_VAL_DIM:(h + 1) * B_VAL_DIM]
                rinv = lax.rsqrt(jnp.mean(seg * seg, axis=-1, keepdims=True) + NORM_EPS)
                rinv_parts.append(rinv)
                on_parts.append(seg * rinv)
            on = jnp.concatenate(on_parts, axis=1)
            obn = on * gg
            ob = (obn * silu_b).astype(BF16)
            yb = _dot(ob, wob_ref[...])
            sig_a = _sigmoid_tanh(ga_ref[rows, :].astype(F32))
            sig_b = _sigmoid_tanh(gb_ref[rows, :].astype(F32))
            merged = (sig_a * ya + sig_b * yb).astype(BF16)
            out = x_ref[rows, :] + _dot(merged, wo_ref[...])
            rf = lax.rsqrt(jnp.mean(out * out, axis=-1, keepdims=True) + NORM_EPS)
            nrm = out * rf
            gf = gf_ref[...]
            err = nrm * gf - tgt_ref[rows, :]
            loss = jnp.sum(err * err) * (0.5 / D_MODEL)

            dy = err * (1.0 / D_MODEL)
            dgf = jnp.sum(dy * nrm, axis=0, keepdims=True)
            dn = dy * gf
            dout = rf * (dn - nrm * jnp.mean(dn * nrm, axis=-1, keepdims=True))
            dxres_ref[rows, :] = dout
            dout_b = dout.astype(BF16)
            dmerged = _dot_nt(dout_b, wo_ref[...])
            dya = dmerged * sig_a
            dyb = dmerged * sig_b
            dga_ref[rows, :] = (dmerged * ya * sig_a * (1.0 - sig_a)).astype(BF16)
            dgb_ref[rows, :] = (dmerged * yb * sig_b * (1.0 - sig_b)).astype(BF16)
            dya_b, dyb_b = dya.astype(BF16), dyb.astype(BF16)
            doa = _dot_nt(dya_b, woa_ref[...])
            dattn_ref[rows, :] = (doa * silu_a).astype(BF16)
            dza_ref[rows, :] = (doa * attn_v * (sig_za * (1.0 + za_v * (1.0 - sig_za)))).astype(BF16)
            dob = _dot_nt(dyb_b, wob_ref[...])
            dzb_ref[rows, :] = (dob * obn * (sig_zb * (1.0 + zb_v * (1.0 - sig_zb)))).astype(BF16)
            dobn = dob * silu_b
            dgg = jnp.sum(dobn * on, axis=0, keepdims=True)
            don = dobn * gg
            for h in range(B_HEADS):
                sl = slice(h * B_VAL_DIM, (h + 1) * B_VAL_DIM)
                don_h, on_h = don[:, sl], on[:, sl]
                dog_ref[rows, sl] = (rinv_parts[h] * (don_h - on_h * jnp.mean(don_h * on_h, axis=-1, keepdims=True))
                                     ).astype(BF16)
            return (merged, dout_b, oa, dya_b, ob, dyb_b), (loss, dgf, dgg)

        tiles = [one_tile(pl.ds(j * sub, sub)) for j in range(tm // sub)]
        merged, dout_b, oa, dya_b, ob, dyb_b = (jnp.concatenate(parts, axis=0) for parts in zip(*[t[0] for t in tiles]))
        awo_ref[...] += _dot_tn(merged, dout_b)
        awoa_ref[...] += _dot_tn(oa, dya_b)
        awob_ref[...] += _dot_tn(ob, dyb_b)
        for _, (loss, dgf, dgg) in tiles:
            loss_ref[...] += loss
            agf_ref[...] += dgf
            agg_ref[...] += dgg

        @pl.when(pl.program_id(0) == last)
        def _():
            for j in range(N_DEV):
                dwo_ref[j] = awo_ref[j * SHARD_OUT:(j + 1) * SHARD_OUT, :].astype(BF16)
                dwoa_ref[j] = awoa_ref[:, j * SHARD_OUT:(j + 1) * SHARD_OUT].astype(BF16)
                dwob_ref[j] = awob_ref[:, j * SHARD_OUT:(j + 1) * SHARD_OUT].astype(BF16)
            small_ref[...] = jnp.zeros_like(small_ref)
            _put_rows(small_ref, SMALL_G_FINAL, agf_ref[...])
            _put_rows(small_ref, SMALL_G_GLA, agg_ref[...])
            small_ref[SMALL_LOSS:SMALL_LOSS + 1, :] = loss_ref[...]
            for cp in dw_copies:
                cp.start()
            for cp in dw_copies:
                cp.wait()

    def rows(w):
        return pl.BlockSpec((tm, w), lambda i: (i, 0))

    def whole(shape):
        nd = len(shape)
        return pl.BlockSpec(shape, lambda i: (0,) * nd)

    outs = [((T, D_MODEL), F32, rows(D_MODEL)), ((T, A_WIDTH), BF16, rows(A_WIDTH)), ((T, B_WIDTH), BF16, rows(B_WIDTH)),
            ((T, A_WIDTH), BF16, rows(A_WIDTH)), ((T, B_WIDTH), BF16, rows(B_WIDTH)),
            ((T, D_MODEL), BF16, rows(D_MODEL)), ((T, D_MODEL), BF16, rows(D_MODEL)),
            ((N_DEV, SHARD_OUT, D_MODEL), BF16, pl.BlockSpec(memory_space=pl.ANY)),
            ((N_DEV, A_WIDTH, SHARD_OUT), BF16, pl.BlockSpec(memory_space=pl.ANY)),
            ((N_DEV, B_WIDTH, SHARD_OUT), BF16, pl.BlockSpec(memory_space=pl.ANY)),
            ((SMALL_SINKS, LANES), F32, whole((SMALL_SINKS, LANES)))]
    return pl.pallas_call(
        body, name="merge", grid=(T // tm,),
        in_specs=[rows(D_MODEL), rows(D_MODEL), rows(A_WIDTH), rows(A_WIDTH), rows(B_WIDTH), rows(B_WIDTH),
                  rows(D_MODEL), rows(D_MODEL),
                  pl.BlockSpec(memory_space=pl.ANY), pl.BlockSpec(memory_space=pl.ANY),
                  _const_spec((D_MODEL, D_MODEL)), _const_spec((1, B_WIDTH)), _const_spec((1, D_MODEL))],
        out_specs=[o[2] for o in outs],
        out_shape=[jax.ShapeDtypeStruct(o[0], o[1]) for o in outs],
        scratch_shapes=[pltpu.VMEM((D_MODEL, D_MODEL), F32), pltpu.VMEM((A_WIDTH, D_MODEL), F32),
                        pltpu.VMEM((B_WIDTH, D_MODEL), F32), pltpu.VMEM((1, D_MODEL), F32), pltpu.VMEM((1, B_WIDTH), F32),
                        pltpu.VMEM((1, LANES), F32), pltpu.VMEM((A_WIDTH, D_MODEL), BF16),
                        pltpu.VMEM((B_WIDTH, D_MODEL), BF16),
                        pltpu.VMEM((N_DEV, SHARD_OUT, D_MODEL), BF16), pltpu.VMEM((N_DEV, A_WIDTH, SHARD_OUT), BF16),
                        pltpu.VMEM((N_DEV, B_WIDTH, SHARD_OUT), BF16),
                        pltpu.SemaphoreType.DMA((2, N_DEV)), pltpu.SemaphoreType.DMA((3,))],
        compiler_params=pltpu.CompilerParams(dimension_semantics=("arbitrary",), vmem_limit_bytes=V7X_VMEM_LIMIT_MAX),
    )(x2, tgt2, attn, za, o_gla, zb, ga, gb, w_oa_sh, w_ob_sh, w_o, g_gla, g_final)


def _in_proj_bwd(x2, dxres, cosf, sinf, g_in, wt_pad, wa_pad, parts):
    T = x2.shape[0]
    tm = 256
    last = T // tm - 1
    base = SMALL_G_IN

    def body(x_ref, dxres_ref, cos_ref, sin_ref, g_ref, wt_ref, wa_ref,
             dq_ref, dkv_ref, dza_ref, dqb_ref, dkb_ref, dvb_ref, dzb_ref, dla_ref, u_ref, alr_ref, dga_ref, dgb_ref,
             dx_ref, dsh_ref, small_ref, dproj_ref, agin_ref, aba_ref, awa_ref):
        @pl.when(pl.program_id(0) == 0)
        def _():
            for r in (agin_ref, aba_ref, awa_ref):
                r[...] = jnp.zeros_like(r)

        cos, nsin = cos_ref[...], -sin_ref[...]
        for s in range(A_WIDTH // LANES):
            sl = slice(s * LANES, (s + 1) * LANES)
            dproj_ref[:, sl] = _rope_slab(dq_ref[:, sl].astype(F32), cos, nsin).astype(BF16)
        dproj_ref[:, QKV_K:QKV_V] = _rope_slab(dkv_ref[:, 0:LANES].astype(F32), cos, nsin).astype(BF16)
        dproj_ref[:, QKV_V:QKV_W] = dkv_ref[:, LANES:]

        def put(name, val):
            a, b = SEG[name]
            dproj_ref[:, a:b] = val

        put("za", dza_ref[...])
        put("qb", dqb_ref[...])
        put("kb", dkb_ref[...])
        put("vb", dvb_ref[...])
        put("zb", dzb_ref[...])
        put("ga", dga_ref[...])
        put("gb", dgb_ref[...])
        du = dla_ref[...] * (1.0 / B_GATE_TEMP) * _sigmoid(-u_ref[...])
        aba_ref[...] += jnp.sum(du, axis=0, keepdims=True)
        du_b = du.astype(BF16)
        awa_ref[...] += _dot_tn(alr_ref[...], du_b)
        put("alr", _dot_nt(du_b, wa_ref[...]).astype(BF16))

        for j in range(N_DEV):
            col = (j % 2) * SHARD_PAD
            for a, b in _shard_pad_cols(j):
                dsh_ref[j // 2, :, col:col + b - a] = dproj_ref[:, a:b]
                col += b - a
            dsh_ref[j // 2, :, col:(j % 2 + 1) * SHARD_PAD] = jnp.zeros((tm, SHARD_PAD - SHARD_IN), BF16)

        dh = _dot(dproj_ref[...], wt_ref[...])
        x = x_ref[...]
        r = lax.rsqrt(jnp.mean(x * x, axis=-1, keepdims=True) + NORM_EPS)
        nrm = x * r
        agin_ref[...] += jnp.sum(dh * nrm, axis=0, keepdims=True)
        dn = dh * g_ref[...]
        dx_ref[...] = dxres_ref[...] + r * (dn - nrm * jnp.mean(dn * nrm, axis=-1, keepdims=True))

        @pl.when(pl.program_id(0) == last)
        def _():
            small_ref[...] = jnp.zeros_like(small_ref)
            _put_rows(small_ref, SMALL_G_IN - base, agin_ref[...])
            _put_rows(small_ref, SMALL_B_ALPHA - base, aba_ref[...])
            for half in range(B_KEY_WIDTH // LANES):
                r0 = SMALL_W_ALPHA - base + half * B_GATE_RANK
                small_ref[r0:r0 + B_GATE_RANK, :] = awa_ref[0:B_GATE_RANK, half * LANES:(half + 1) * LANES]

    def rows(w):
        return pl.BlockSpec((tm, w), lambda i: (i, 0))

    names = ["dq", "dkv", "dza", "dqb", "dkb", "dvb", "dzb", "dla", "u", "alr", "dga", "dgb"]
    return pl.pallas_call(
        body, name="in_proj_bwd", grid=(T // tm,),
        in_specs=[rows(D_MODEL), rows(D_MODEL), rows(LANES), rows(LANES), _const_spec((1, D_MODEL)),
                  _const_spec((D_IN_PAD, D_MODEL)), _const_spec((RANK_PAD, B_KEY_WIDTH))]
                 + [rows(parts[n].shape[1]) for n in names],
        out_specs=[rows(D_MODEL), pl.BlockSpec((N_CHIPS, tm, 2 * SHARD_PAD), lambda i: (0, i, 0)),
                   pl.BlockSpec((SMALL_ROWS - base, LANES), lambda i: (0, 0))],
        out_shape=[jax.ShapeDtypeStruct((T, D_MODEL), F32), jax.ShapeDtypeStruct((N_CHIPS, T, 2 * SHARD_PAD), BF16),
                   jax.ShapeDtypeStruct((SMALL_ROWS - base, LANES), F32)],
        scratch_shapes=[pltpu.VMEM((tm, D_IN_PAD), BF16), pltpu.VMEM((1, D_MODEL), F32), pltpu.VMEM((1, B_KEY_WIDTH), F32),
                        pltpu.VMEM((RANK_PAD, B_KEY_WIDTH), F32)],
        compiler_params=_params("arbitrary"),
    )(x2, dxres, cosf, sinf, g_in, wt_pad, wa_pad, *[parts[n] for n in names])


FLIPS = [(dx, dy, dc) for dx in (0, 1) for dy in (0, 1) for dc in (0, 1)][1:]


def _my_place():
    return lax.axis_index("x"), lax.axis_index("y"), lax.axis_index("c")


def _any_specs(n):
    return [pl.BlockSpec(memory_space=pl.ANY)] * n


def _gather_first(shards, pos_col):
    n = len(shards)
    T = pos_col.shape[0]
    rows_per_pass = math.gcd(T, 512)
    invf, sign = _rope_lane_constants()

    def body(*refs):
        ins, (pos_ref, invf_ref, sign_ref) = refs[:n], refs[n:n + 3]
        outs, (cos_ref, sin_ref) = refs[n + 3:2 * n + 3], refs[2 * n + 3:2 * n + 5]
        send_sems, recv_sems, local_sems = refs[2 * n + 5:]
        x, y, c = _my_place()
        me, sibling = (x, y, c), (x, y, 1 - c)
        chips = [(1 - x, y), (x, 1 - y), (1 - x, 1 - y)]

        def block(a, px, py, pc):
            return outs[a].at[4 * px + 2 * py + pc]

        def copy(a, k, blk, to, src=None):
            return pltpu.make_async_remote_copy(
                src_ref=block(a, *blk) if src is None else src, dst_ref=block(a, *blk),
                send_sem=send_sems.at[a, k], recv_sem=recv_sems.at[a, k], device_id=to, device_id_type=MESH)

        mine = [pltpu.make_async_copy(ins[a], block(a, *me), local_sems.at[a]) for a in range(n)]
        for cp in mine:
            cp.start()
        first = []
        for a in range(n):
            first.append(copy(a, 0, me, sibling, src=ins[a]))
            first += [copy(a, 1 + j, me, (*chip, c), src=ins[a]) for j, chip in enumerate(chips)]
        for cp in first:
            cp.start()

        def tables(i, carry):
            rows = pl.ds(pl.multiple_of(i * rows_per_pass, rows_per_pass), rows_per_pass)
            ang = pos_ref[rows, :].astype(F32) * invf_ref[...]
            cos_ref[rows, :] = jnp.cos(ang)
            sin_ref[rows, :] = jnp.sin(ang) * sign_ref[...]
            return carry

        lax.fori_loop(0, T // rows_per_pass, tables, 0)

        passed = []
        for j, chip in enumerate(chips):
            for a in range(n):
                copy(a, 1 + j, (*chip, c), me).wait_recv()
                fwd = copy(a, 4 + j, (*chip, c), sibling)
                fwd.start()
                passed.append(fwd)
        for a in range(n):
            copy(a, 0, sibling, me).wait_recv()
            for j, chip in enumerate(chips):
                copy(a, 4 + j, (*chip, 1 - c), me).wait_recv()
        for cp in first + passed:
            cp.wait_send()
        for cp in mine:
            cp.wait()

    vmem = pl.BlockSpec(memory_space=pltpu.VMEM)
    res = pl.pallas_call(
        body, name="gather_weights",
        in_specs=_any_specs(n) + [vmem] * 3, out_specs=_any_specs(n) + [vmem] * 2,
        out_shape=[jax.ShapeDtypeStruct((N_DEV, *s.shape), s.dtype) for s in shards]
                  + [jax.ShapeDtypeStruct((T, LANES), F32)] * 2,
        scratch_shapes=[pltpu.SemaphoreType.DMA((n, 7)), pltpu.SemaphoreType.DMA((n, 7)), pltpu.SemaphoreType.DMA((n,))],
        compiler_params=pltpu.CompilerParams(vmem_limit_bytes=V7X_VMEM_LIMIT),
    )(*shards, pos_col, invf, sign)
    return res[:n], res[n], res[n + 1]


def _w_in_grad_rs(h, dsh, chip_order, small):
    T = h.shape[0]
    tk = math.gcd(T, 2048)
    nk = T // tk
    chip_flips = [(1, 1), (1, 0), (0, 1)]
    n_steps = len(chip_flips) + 1
    SIB = len(chip_flips)
    k_finish = min(1, nk - 1)

    def body(order_ref, h_ref, d_ref, s_ref, own_ref, recv_ref, sall_ref,
             acc_ref, keep_ref, pre_ref, to_sib_ref, to_chip_ref,
             sib_send, sib_recv, chip_send, chip_recv, ssend_sems, srecv_sems, local_sem):
        i, kk = pl.program_id(0), pl.program_id(1)
        x, y, c = _my_place()
        my_dev = 4 * x + 2 * y + c

        def small_copy(r, slot):
            dx, dy, dc = FLIPS[r]
            return pltpu.make_async_remote_copy(
                src_ref=s_ref, dst_ref=sall_ref.at[slot], send_sem=ssend_sems.at[r], recv_sem=srecv_sems.at[r],
                device_id=(x ^ dx, y ^ dy, c ^ dc), device_id_type=MESH)

        keep_small = pltpu.make_async_copy(s_ref, sall_ref.at[my_dev], local_sem)

        def sib_copy(t):
            dst = recv_ref.at[SIB] if t == SIB else pre_ref.at[t]
            return pltpu.make_async_remote_copy(
                src_ref=to_sib_ref.at[t], dst_ref=dst, send_sem=sib_send.at[t], recv_sem=sib_recv.at[t],
                device_id=(x, y, 1 - c), device_id_type=MESH)

        def chip_copy(t):
            dx, dy = chip_flips[t]
            return pltpu.make_async_remote_copy(
                src_ref=to_chip_ref.at[t], dst_ref=recv_ref.at[t], send_sem=chip_send.at[t], recv_sem=chip_recv.at[t],
                device_id=(x ^ dx, y ^ dy, c), device_id_type=MESH)

        def halves():
            first, second = acc_ref[0:SHARD_PAD, :], acc_ref[SHARD_PAD:2 * SHARD_PAD, :]
            return jnp.where(c == 0, first, second), jnp.where(c == 0, second, first)

        @pl.when((i == 0) & (kk == 0))
        def _():
            keep_small.start()
            for r in range(len(FLIPS)):
                small_copy(r, my_dev).start()

        @pl.when(kk == 0)
        def _():
            acc_ref[...] = jnp.zeros_like(acc_ref)

        acc_ref[...] += _dot_tn(d_ref[...], h_ref[...])

        for t in range(len(chip_flips)):
            @pl.when((i == t + 1) & (kk == k_finish))
            def _(t=t):
                sib_copy(t).wait_recv()
                to_chip_ref[t] = (keep_ref[...] + pre_ref[t].astype(F32)).astype(BF16)
                chip_copy(t).start()

        for t in range(len(chip_flips)):
            @pl.when((i == t) & (kk == nk - 1))
            def _(t=t):
                mine, theirs = halves()
                to_sib_ref[t] = theirs.astype(BF16)
                sib_copy(t).start()
                keep_ref[...] = mine

        @pl.when((i == n_steps - 1) & (kk == nk - 1))
        def _():
            mine, theirs = halves()
            own_ref[...] = mine
            to_sib_ref[SIB] = theirs.astype(BF16)
            sib_copy(SIB).start()
            for t in range(len(chip_flips)):
                sib_copy(t).wait_send()
                chip_copy(t).wait_send()
                chip_copy(t).wait_recv()
            sib_copy(SIB).wait_send()
            sib_copy(SIB).wait_recv()
            for r, (dx, dy, dc) in enumerate(FLIPS):
                small_copy(r, 4 * (x ^ dx) + 2 * (y ^ dy) + (c ^ dc)).wait_recv()
                small_copy(r, my_dev).wait_send()
            keep_small.wait()

    shard = (SHARD_PAD, D_MODEL)
    return pl.pallas_call(
        body, name="w_in_grad_rs",
        grid_spec=pltpu.PrefetchScalarGridSpec(
            num_scalar_prefetch=1, grid=(n_steps, nk),
            in_specs=[pl.BlockSpec((tk, D_MODEL), lambda i, kk, order: (kk, 0)),
                      pl.BlockSpec((None, tk, 2 * SHARD_PAD), lambda i, kk, order: (order[i], kk, 0)),
                      pl.BlockSpec(memory_space=pl.ANY)],
            out_specs=[pl.BlockSpec(shard, lambda i, kk, order: (0, 0)),
                       pl.BlockSpec(memory_space=pl.ANY), pl.BlockSpec(memory_space=pl.ANY)],
            scratch_shapes=[pltpu.VMEM((2 * SHARD_PAD, D_MODEL), F32), pltpu.VMEM(shard, F32),
                            pltpu.VMEM((SIB, *shard), BF16), pltpu.VMEM((SIB + 1, *shard), BF16),
                            pltpu.VMEM((SIB, *shard), BF16),
                            pltpu.SemaphoreType.DMA((SIB + 1,)), pltpu.SemaphoreType.DMA((SIB + 1,)),
                            pltpu.SemaphoreType.DMA((SIB,)), pltpu.SemaphoreType.DMA((SIB,)),
                            pltpu.SemaphoreType.DMA((7,)), pltpu.SemaphoreType.DMA((7,)), pltpu.SemaphoreType.DMA]),
        out_shape=[jax.ShapeDtypeStruct(shard, F32),
                   jax.ShapeDtypeStruct((SIB + 1, *shard), BF16),
                   jax.ShapeDtypeStruct((N_DEV, *small.shape), F32)],
        compiler_params=_params("arbitrary", "arbitrary"),
    )(chip_order, h, dsh, small)


def _adam_math(w, g, m, v):
    m_new = ADAM_B1 * m + (1.0 - ADAM_B1) * g
    v_new = ADAM_B2 * v + (1.0 - ADAM_B2) * (g * g)
    m_hat = m_new / (1.0 - ADAM_B1 ** ADAM_STEP)
    v_hat = v_new / (1.0 - ADAM_B2 ** ADAM_STEP)
    delta = -ADAM_LR * (m_hat / (jnp.sqrt(v_hat) + ADAM_EPS) + ADAM_WD * w)
    return delta, m_new, v_new


def _adam_big(name, own, own_idx, recv, w, m, v):
    rw, cw = w.shape
    rp = own.shape[1]
    steps = 8
    by_cols = rp != rw
    blk_w = (rw, cw // steps) if by_cols else (rw // steps, cw)
    blk_g = (rp, cw // steps) if by_cols else (rw // steps, cw)
    at = (lambda i: (0, i)) if by_cols else (lambda i: (i, 0))

    def body(idx_ref, o_ref, r_ref, w_ref, m_ref, v_ref, g_ref, d_ref, mo_ref, vo_ref):
        g = o_ref[...].astype(F32)
        for r in range(recv.shape[0]):
            g = g + r_ref[r].astype(F32)
        g = g[0:blk_w[0], :]
        g_ref[...] = g
        d_ref[...], mo_ref[...], vo_ref[...] = _adam_math(w_ref[...], g, m_ref[...], v_ref[...])

    spec = pl.BlockSpec(blk_w, lambda i, idx_ref: at(i))
    return pl.pallas_call(
        body, name=name,
        grid_spec=pltpu.PrefetchScalarGridSpec(
            num_scalar_prefetch=1, grid=(steps,),
            in_specs=[pl.BlockSpec((None, *blk_g), lambda i, idx_ref: (idx_ref[0], *at(i))),
                      pl.BlockSpec((recv.shape[0], *blk_g), lambda i, idx_ref: (0, *at(i))), spec, spec, spec],
            out_specs=[spec] * 4),
        out_shape=[jax.ShapeDtypeStruct((rw, cw), F32)] * 4,
        compiler_params=_params("parallel"),
    )(own_idx, own, recv, w, m, v)


def _adam_small(small_all, params):
    flat = [a for triple in params for a in triple]
    n_par = len(params)

    def body(s_ref, *refs):
        ins, outs, loss_ref = refs[:3 * n_par], refs[3 * n_par:-1], refs[-1]
        g_slab = s_ref[0]
        for dev in range(1, N_DEV):
            g_slab = g_slab + s_ref[dev]
        loss_ref[...] = g_slab[SMALL_LOSS:SMALL_LOSS + 1, :]
        dev = 4 * lax.axis_index("x") + 2 * lax.axis_index("y") + lax.axis_index("c")
        alpha_full = jnp.concatenate([g_slab[SMALL_W_ALPHA + half * B_GATE_RANK:SMALL_W_ALPHA + (half + 1) * B_GATE_RANK]
                                      for half in range(B_KEY_WIDTH // LANES)], axis=1)
        alpha_mine = pltpu.roll(alpha_full, (B_KEY_WIDTH - dev * SHARD_ALPHA) % B_KEY_WIDTH, 1)[:, 0:SHARD_ALPHA]
        grads = [_take_rows(g_slab, SMALL_G_IN, D_MODEL // LANES), _take_rows(g_slab, SMALL_G_FINAL, D_MODEL // LANES),
                 _take_rows(g_slab, SMALL_G_GLA, B_WIDTH // LANES), _take_rows(g_slab, SMALL_B_ALPHA, B_KEY_WIDTH // LANES),
                 g_slab[SMALL_SINKS:SMALL_SINKS + 1, 0:A_HEADS], alpha_mine]
        for i, g in enumerate(grads):
            w_ref, m_ref, v_ref = ins[3 * i:3 * i + 3]
            delta, m_new, v_new = _adam_math(w_ref[...], g, m_ref[...], v_ref[...])
            outs[4 * i][...] = g
            outs[4 * i + 1][...] = delta
            outs[4 * i + 2][...] = m_new
            outs[4 * i + 3][...] = v_new

    res = pl.pallas_call(
        body, name="adam_small",
        out_shape=[jax.ShapeDtypeStruct(t[0].shape, F32) for t in params for _ in range(4)]
                  + [jax.ShapeDtypeStruct((1, LANES), F32)],
    )(small_all, *flat)
    return [res[4 * i:4 * i + 4] for i in range(n_par)], res[-1]


def _local_step(x, cosf, sinf, loss_target, g_in, wt_sh, wa_pad, b_alpha, sinks, g_gla, out_shards, g_final, chip_order):
    B, S, _ = x.shape
    T = B * S
    x2 = x.reshape(T, D_MODEL)
    tgt2 = loss_target.reshape(T, D_MODEL)
    f, (g_woa, g_wob, g_wo) = _in_proj(x2, cosf, sinf, g_in, wt_sh, wa_pad, b_alpha, out_shards)
    w_o = g_wo.reshape(D_MODEL, D_MODEL)
    sink_row = jnp.repeat(sinks, WINDOW).reshape(1, ATT_ROWS)
    sink_col = sink_row.reshape(ATT_ROWS, 1)
    attn, lse = _attn_fwd(f["qkv"], sink_row, B, S)
    o_gla, st_all = _gla_fwd(f["q"], f["k"], f["cum"], f["vb"], B, S)
    (dxres, dattn, dog, dza, dzb, dga, dgb, dw_o, dw_oa, dw_ob, small_a) = _merge(
        x2, tgt2, attn, f["za"], o_gla, f["zb"], f["ga"], f["gb"], g_woa, g_wob, w_o, g_gla, g_final)
    dq, dkv, dsink = _attn_bwd(f["qkv"], dattn, attn, lse, sink_col, B, S)
    (dqb, dkb, dvb, dla), (rv_o, rv_oa, rv_ob) = _gla_bwd(f["q"], f["k"], f["cum"], f["vb"], dog, st_all, B, S,
                                                        [dw_o, dw_oa, dw_ob])
    parts = dict(dq=dq, dkv=dkv, dza=dza, dqb=dqb, dkb=dkb, dvb=dvb, dzb=dzb, dla=dla, u=f["u"], alr=f["alr"],
                 dga=dga, dgb=dgb)
    dx, dsh, small_c = _in_proj_bwd(x2, dxres, cosf, sinf, g_in, f["wt_pad"], wa_pad, parts)
    small = jnp.concatenate([small_a, dsink, small_c], axis=0)
    own_in, rv_in, small_all = _w_in_grad_rs(f["h"], dsh, chip_order, small)
    return dict(grad_x=dx.reshape(B, S, D_MODEL), own_in=own_in, rv_in=rv_in,
                own_o=dw_o, rv_o=rv_o, own_oa=dw_oa, rv_oa=rv_oa, own_ob=dw_ob, rv_ob=rv_ob, small_all=small_all)


def kernel(x, positions, g_in, w_in, w_alpha_up, b_alpha, attn_sinks, g_gla_norm, w_out_a, w_out_b, w_o, g_final, loss_target, m_g_in, m_w_in, m_w_alpha_up, m_b_alpha, m_attn_sinks, m_g_gla_norm, m_w_out_a, m_w_out_b, m_w_o, m_g_final, v_g_in, v_w_in, v_w_alpha_up, v_b_alpha, v_attn_sinks, v_g_gla_norm, v_w_out_a, v_w_out_b, v_w_o, v_g_final):
    xi, yi, ci = _my_place()
    dev_idx = (4 * xi + 2 * yi + ci).reshape(1).astype(jnp.int32)
    chip = 2 * xi + yi
    chip_order = jnp.stack([chip ^ 3, chip ^ 2, chip ^ 1, chip]).astype(jnp.int32)

    (g_win, g_wa), cosf, sinf = _gather_first(
        [jnp.pad(w_in[0].T.astype(BF16), ((0, SHARD_PAD - SHARD_IN), (0, 0))), w_alpha_up[0].astype(BF16)],
        positions.reshape(-1, 1))
    wt_sh = g_win.reshape(N_DEV * SHARD_PAD, D_MODEL)
    wa_pad = jnp.pad(jnp.concatenate([g_wa[j] for j in range(N_DEV)], axis=1), ((0, RANK_PAD - B_GATE_RANK), (0, 0)))

    r = _local_step(x, cosf, sinf, loss_target, g_in, wt_sh, wa_pad, b_alpha, attn_sinks[0], g_gla_norm,
                    [w_out_a[0].astype(BF16), w_out_b[0].astype(BF16), w_o[0].astype(BF16)],
                    g_final.reshape(1, D_MODEL), chip_order)

    first = jnp.zeros((1,), jnp.int32)
    big = [[a.T for a in _adam_big("adam_w_in", r["own_in"][None], first, r["rv_in"], w_in[0].T, m_w_in[0].T, v_w_in[0].T)],
           _adam_big("adam_w_out_a", r["own_oa"], dev_idx, r["rv_oa"], w_out_a[0], m_w_out_a[0], v_w_out_a[0]),
           _adam_big("adam_w_out_b", r["own_ob"], dev_idx, r["rv_ob"], w_out_b[0], m_w_out_b[0], v_w_out_b[0]),
           _adam_big("adam_w_o", r["own_o"], dev_idx, r["rv_o"], w_o[0], m_w_o[0], v_w_o[0])]
    row = lambda a: a.reshape(1, D_MODEL)
    (s_in, s_final, s_gla, s_ba, s_sinks, s_wa), loss_row = _adam_small(r["small_all"], [
        (g_in, m_g_in, v_g_in), (row(g_final), row(m_g_final), row(v_g_final)),
        (g_gla_norm, m_g_gla_norm, v_g_gla_norm), (b_alpha, m_b_alpha, v_b_alpha),
        (attn_sinks, m_attn_sinks, v_attn_sinks), (w_alpha_up[0], m_w_alpha_up[0], v_w_alpha_up[0])])

    def group(i):
        return (s_in[i], big[0][i][None], s_wa[i][None], s_ba[i], s_sinks[i], s_gla[i], big[1][i][None], big[2][i][None],
                big[3][i][None], s_final[i].reshape(D_MODEL))

    return (loss_row[0, 0], r["grad_x"], *group(0), *group(1), *group(2), *group(3))
```

```python
import functools
import math

import numpy as np
import jax
import jax.numpy as jnp
from jax import lax
from jax.experimental import pallas as pl
from jax.experimental.pallas import tpu as pltpu

F32 = jnp.float32
BF16 = jnp.bfloat16
MESH = pl.DeviceIdType.MESH

D_MODEL = 1024
A_HEADS, A_KV_HEADS, A_HEAD_DIM = 8, 2, 64
A_WIDTH, A_KV_WIDTH = 512, 128
WINDOW = 128
ROPE_THETA = 500000.0
ROPE_DIM = 16
B_HEADS, B_KEY_DIM, B_VAL_DIM = 4, 64, 128
B_KEY_WIDTH, B_WIDTH = 256, 512
B_GATE_RANK = 16
B_GATE_TEMP = 16.0
B_CHUNK = 64
NORM_EPS = 1e-6
NEG_BIG = -1e30
D_IN = 4880
N_DEV = 8
N_CHIPS = 4
ADAM_LR, ADAM_B1, ADAM_B2, ADAM_EPS, ADAM_WD, ADAM_STEP = 0.001, 0.9, 0.999, 1e-08, 0.01, 10

LANES = 128
V7X_VMEM_LIMIT = 56 * 1024 * 1024
V7X_VMEM_LIMIT_MAX = 62 * 1024 * 1024

RANK_PAD = LANES
SEG = {}
_off = 0
for _name, _w in (("qa", 512), ("ka", 128), ("va", 128), ("za", 512), ("qb", 256), ("kb", 256),
                  ("vb", 512), ("zb", 512), ("alr", RANK_PAD), ("ga", 1024), ("gb", 1024)):
    SEG[_name] = (_off, _off + _w)
    _off += _w
D_IN_PAD = _off
ALR_SRC = SEG["alr"][0]
QKV_K, QKV_V, QKV_W = SEG["ka"][0], SEG["va"][0], SEG["va"][1]
ATT_SCALE = A_HEAD_DIM ** -0.5

SHARD_IN = D_IN // N_DEV
SHARD_PAD = 640
SHARD_OUT = D_MODEL // N_DEV
SHARD_ALPHA = B_KEY_WIDTH // N_DEV

SMALL_G_FINAL, SMALL_G_GLA, SMALL_LOSS, SMALL_SINKS, SMALL_G_IN, SMALL_B_ALPHA, SMALL_W_ALPHA = 0, 8, 12, 16, 24, 32, 40
SMALL_ROWS = 72


def _dot(a, b):
    return jnp.dot(a, b, preferred_element_type=F32)


def _dot_nt(a, b):
    return lax.dot_general(a, b, (((1,), (1,)), ((), ())), preferred_element_type=F32)


def _dot_tn(a, b):
    return lax.dot_general(a, b, (((0,), (0,)), ((), ())), preferred_element_type=F32)


def _sigmoid(z):
    return 1.0 / (1.0 + jnp.exp(-z))


def _sigmoid_tanh(z):
    return 0.5 * jnp.tanh(0.5 * z) + 0.5


def _params(*sem):
    return pltpu.CompilerParams(dimension_semantics=sem, vmem_limit_bytes=V7X_VMEM_LIMIT)


def _const_spec(shape):
    nd = len(shape)
    return pl.BlockSpec(shape, lambda *_: (0,) * nd, pipeline_mode=pl.Buffered(1))


def _lane_iota(shape):
    return lax.broadcasted_iota(jnp.int32, shape, 1)


def _row_iota(shape):
    return lax.broadcasted_iota(jnp.int32, shape, 0)


def _split3(v):
    hi = v.astype(BF16)
    r1 = v - hi.astype(F32)
    mid = r1.astype(BF16)
    lo = (r1 - mid.astype(F32)).astype(BF16)
    return hi, mid, lo


def _put_rows(ref, row0, vec):
    for r in range(vec.shape[1] // LANES):
        ref[row0 + r:row0 + r + 1, :] = vec[:, r * LANES:(r + 1) * LANES]


def _take_rows(slab, row0, n):
    return jnp.concatenate([slab[row0 + r:row0 + r + 1, :] for r in range(n)], axis=1)


def _rope_lane_constants():
    half = ROPE_DIM // 2
    inv_freq = np.exp(-math.log(ROPE_THETA) * np.arange(half, dtype=np.float32) * np.float32(2.0 / ROPE_DIM)).astype(np.float32)
    lane = np.arange(LANES)
    j = lane % A_HEAD_DIM
    invf = np.where(j < ROPE_DIM, inv_freq[j % half], 0.0).astype(np.float32)
    sign = np.where(j < half, -1.0, np.where(j < ROPE_DIM, 1.0, 0.0)).astype(np.float32)
    return jnp.asarray(invf)[None, :], jnp.asarray(sign)[None, :]


def _rope_slab(t, cos, sin_signed):
    first = (_lane_iota(t.shape) % A_HEAD_DIM) < (ROPE_DIM // 2)
    partner = jnp.where(first, pltpu.roll(t, LANES - ROPE_DIM // 2, 1), pltpu.roll(t, ROPE_DIM // 2, 1))
    return t * cos + partner * sin_signed


def _shard_pad_cols(j):
    cut = ALR_SRC + B_GATE_RANK
    shift = RANK_PAD - B_GATE_RANK
    a, b = j * SHARD_IN, (j + 1) * SHARD_IN
    if b <= cut:
        return [(a, b)]
    if a >= cut:
        return [(a + shift, b + shift)]
    return [(a, cut), (cut + shift, b + shift)]


def _in_proj(x2, cosf, sinf, g_in, wt_sh, wa_pad, b_alpha, later_shards):
    T = x2.shape[0]
    tm = math.gcd(T, 512)
    last = T // tm - 1
    nl = len(later_shards)

    def body(x_ref, cos_ref, sin_ref, g_ref, wsh_ref, wa_ref, ba_ref, *rest):
        sh_refs, rest = rest[:nl], rest[nl:]
        (h_ref, qkv_ref, za_ref, q_ref, k_ref, vb_ref, zb_ref, alr_ref, u_ref, cum_ref, ga_ref, gb_ref, wt_out) = rest[:13]
        all_refs, (wt_ref, send_sems, recv_sems, local_sems, wt_sem) = rest[13:13 + nl], rest[13 + nl:]
        wt_copy = pltpu.make_async_copy(wt_ref, wt_out, wt_sem)
        px, py, pc = _my_place()
        my_dev = 4 * px + 2 * py + pc

        def wcopy(a, r, slot):
            dx, dy, dc = FLIPS[r]
            return pltpu.make_async_remote_copy(
                src_ref=sh_refs[a], dst_ref=all_refs[a].at[slot], send_sem=send_sems.at[a, r],
                recv_sem=recv_sems.at[a, r], device_id=(px ^ dx, py ^ dy, pc ^ dc), device_id_type=MESH)

        keep = [pltpu.make_async_copy(sh_refs[a], all_refs[a].at[my_dev], local_sems.at[a]) for a in range(nl)]

        @pl.when(pl.program_id(0) == 0)
        def _():
            for a in range(nl):
                keep[a].start()
                for r in range(len(FLIPS)):
                    wcopy(a, r, my_dev).start()

        @pl.when(pl.program_id(0) == 0)
        def _():
            for j in range(N_DEV):
                src = j * SHARD_PAD
                for a, b in _shard_pad_cols(j):
                    wt_ref[a:b, :] = wsh_ref[src:src + b - a, :]
                    src += b - a
            a, b = SEG["alr"]
            wt_ref[a + B_GATE_RANK:b, :] = jnp.zeros((RANK_PAD - B_GATE_RANK, D_MODEL), BF16)
            wt_copy.start()

        x = x_ref[...]
        r = lax.rsqrt(jnp.mean(x * x, axis=-1, keepdims=True) + NORM_EPS)
        h = (x * r * g_ref[...]).astype(BF16)
        h_ref[...] = h

        def seg(name):
            a, b = SEG[name]
            return _dot_nt(h, wt_ref[a:b, :])

        cos, sin = cos_ref[...], sin_ref[...]
        qa = seg("qa") * ATT_SCALE
        for s in range(A_WIDTH // LANES):
            qkv_ref[:, s * LANES:(s + 1) * LANES] = _rope_slab(qa[:, s * LANES:(s + 1) * LANES], cos, sin).astype(BF16)
        qkv_ref[:, QKV_K:QKV_V] = _rope_slab(seg("ka"), cos, sin).astype(BF16)
        qkv_ref[:, QKV_V:QKV_W] = seg("va").astype(BF16)
        za_ref[...] = seg("za").astype(BF16)
        q_ref[...] = seg("qb")
        k_ref[...] = seg("kb")
        vb_ref[...] = seg("vb").astype(BF16)
        zb_ref[...] = seg("zb").astype(BF16)
        ga_ref[...] = seg("ga").astype(BF16)
        gb_ref[...] = seg("gb").astype(BF16)
        alr = seg("alr").astype(BF16)
        alr_ref[...] = alr
        u = _dot(alr, wa_ref[...]) + ba_ref[...]
        u_ref[...] = u
        log_a = (jnp.minimum(u, 0.0) - jnp.log(1.0 + jnp.exp(-jnp.abs(u)))) * (1.0 / B_GATE_TEMP)
        row, col = _row_iota((tm, tm)), _lane_iota((tm, tm))
        tri = ((row // B_CHUNK == col // B_CHUNK) & (col <= row)).astype(BF16)
        hi, mid, lo = _split3(log_a)
        cum_ref[...] = _dot(tri, hi) + _dot(tri, mid) + _dot(tri, lo)

        @pl.when(pl.program_id(0) == last)
        def _():
            for a in range(nl):
                for r, (dx, dy, dc) in enumerate(FLIPS):
                    wcopy(a, r, 4 * (px ^ dx) + 2 * (py ^ dy) + (pc ^ dc)).wait_recv()
                    wcopy(a, r, my_dev).wait_send()
                keep[a].wait()
            wt_copy.wait()

    def rows(w):
        return pl.BlockSpec((tm, w), lambda i: (i, 0))

    outs = [("h", D_MODEL, BF16), ("qkv", QKV_W, BF16), ("za", A_WIDTH, BF16), ("q", B_KEY_WIDTH, F32),
            ("k", B_KEY_WIDTH, F32), ("vb", B_WIDTH, BF16), ("zb", B_WIDTH, BF16), ("alr", RANK_PAD, BF16),
            ("u", B_KEY_WIDTH, F32), ("cum", B_KEY_WIDTH, F32), ("ga", D_MODEL, BF16), ("gb", D_MODEL, BF16)]
    res = pl.pallas_call(
        body, name="in_proj", grid=(T // tm,),
        in_specs=[rows(D_MODEL), rows(LANES), rows(LANES), _const_spec((1, D_MODEL)),
                  _const_spec((N_DEV * SHARD_PAD, D_MODEL)), _const_spec((RANK_PAD, B_KEY_WIDTH)),
                  _const_spec((1, B_KEY_WIDTH))] + _any_specs(nl),
        out_specs=[rows(w) for _, w, _ in outs] + _any_specs(1 + nl),
        out_shape=[jax.ShapeDtypeStruct((T, w), dt) for _, w, dt in outs]
                  + [jax.ShapeDtypeStruct((D_IN_PAD, D_MODEL), BF16)]
                  + [jax.ShapeDtypeStruct((N_DEV, *sh.shape), sh.dtype) for sh in later_shards],
        scratch_shapes=[pltpu.VMEM((D_IN_PAD, D_MODEL), BF16),
                        pltpu.SemaphoreType.DMA((nl, len(FLIPS))), pltpu.SemaphoreType.DMA((nl, len(FLIPS))),
                        pltpu.SemaphoreType.DMA((nl,)), pltpu.SemaphoreType.DMA],
        compiler_params=_params("arbitrary"),
    )(x2, cosf, sinf, g_in, wt_sh, wa_pad, b_alpha, *later_shards)
    n_out = len(outs) + 1
    return dict(zip([n for n, _, _ in outs] + ["wt_pad"], res[:n_out])), res[n_out:]


def _dup_kv_head(t, g):
    tf = t.astype(F32)
    keep = (_lane_iota(tf.shape) < A_HEAD_DIM) == (g == 0)
    return jnp.where(keep, tf, pltpu.roll(tf, A_HEAD_DIM, 1)).astype(BF16)


def _stack_heads(t):
    lo = _lane_iota(t.shape) < A_HEAD_DIM
    zero = jnp.zeros_like(t)
    return jnp.concatenate([jnp.where(lo, t, zero), jnp.where(lo, zero, t)], axis=0)


ATT_ROWS = A_HEADS * WINDOW
GROUP_ROWS = ATT_ROWS // A_KV_HEADS
HEADS_PER_GROUP = A_HEADS // A_KV_HEADS


def _band_mask_t(n):
    kj = _row_iota((2 * WINDOW, GROUP_ROWS)) - WINDOW
    qi = _lane_iota((2 * WINDOW, GROUP_ROWS)) % WINDOW
    return (kj <= qi) & (qi - kj < WINDOW) & ((n > 0) | (kj >= 0))


def _stacked_queries(ref, g):
    pairs = range(g * HEADS_PER_GROUP // 2, (g + 1) * HEADS_PER_GROUP // 2)
    return jnp.concatenate([_stack_heads(ref[:, p * LANES:(p + 1) * LANES]) for p in pairs], axis=0)


def _unstack_heads(t, g, ref, dtype):
    lo = _lane_iota((WINDOW, LANES)) < A_HEAD_DIM
    for hh in range(HEADS_PER_GROUP // 2):
        p = g * HEADS_PER_GROUP // 2 + hh
        ref[:, p * LANES:(p + 1) * LANES] = jnp.where(lo, t[2 * hh * WINDOW:(2 * hh + 1) * WINDOW],
                                                       t[(2 * hh + 1) * WINDOW:(2 * hh + 2) * WINDOW]).astype(dtype)


FWD_BLOCKS = 8


def _attn_fwd(qkv, sink_row, B, S):
    T = B * S
    nb = S // WINDOW
    blocks = math.gcd(nb, FWD_BLOCKS)
    steps = nb // blocks

    def one_block(has_prev, sink_ref, q, k, v, o_ref, lse_ref):
        valid = _band_mask_t(has_prev)
        lse_rows = []
        for g in range(A_KV_HEADS):
            kd, vd = _dup_kv_head(k, g), _dup_kv_head(v, g)
            s = jnp.where(valid, _dot_nt(kd, _stacked_queries(q, g)), NEG_BIG)
            sink = sink_ref[:, g * GROUP_ROWS:(g + 1) * GROUP_ROWS]
            m = jnp.maximum(jnp.max(s, axis=0, keepdims=True), sink)
            e = jnp.exp(s - m)
            den = jnp.sum(e, axis=0, keepdims=True) + jnp.exp(sink - m)
            o = _dot_tn((e * (1.0 / den)).astype(BF16), vd)
            _unstack_heads(o, g, o_ref, F32)
            lse = m + jnp.log(den)
            lse_rows += [lse[:, j * WINDOW:(j + 1) * WINDOW] for j in range(HEADS_PER_GROUP)]
        by_head = jnp.concatenate(lse_rows + [jnp.zeros((WINDOW - A_HEADS, WINDOW), F32)], axis=0)
        lse_ref[...] = by_head.T

    def body(sink_ref, q_ref, kc_ref, vc_ref, kp_ref, vp_ref, o_ref, lse_ref):
        k_all = jnp.concatenate([kp_ref[...], kc_ref[...]], axis=0)
        v_all = jnp.concatenate([vp_ref[...], vc_ref[...]], axis=0)
        for j in range(blocks):
            rows = pl.ds(j * WINDOW, WINDOW)
            keys = slice(j * WINDOW, (j + 2) * WINDOW)
            has_prev = pl.program_id(1) if j == 0 else 1
            one_block(has_prev, sink_ref, q_ref[rows, :], k_all[keys], v_all[keys], o_ref.at[rows], lse_ref.at[rows])

    def cur(col, w):
        return pl.BlockSpec((blocks * WINDOW, w), lambda b, n: (b * steps + n, col))

    def prev(col):
        return pl.BlockSpec((WINDOW, LANES), lambda b, n: (b * nb + jnp.maximum(blocks * n - 1, 0), col))

    kcol, vcol = QKV_K // LANES, QKV_V // LANES
    return pl.pallas_call(
        body, name="attn_fwd", grid=(B, steps),
        in_specs=[_const_spec((1, ATT_ROWS)), cur(0, A_WIDTH), cur(kcol, LANES), cur(vcol, LANES), prev(kcol), prev(vcol)],
        out_specs=[cur(0, A_WIDTH), cur(0, LANES)],
        out_shape=[jax.ShapeDtypeStruct((T, A_WIDTH), F32), jax.ShapeDtypeStruct((T, LANES), F32)],
        compiler_params=_params("parallel", "parallel"),
    )(sink_row, qkv, qkv, qkv, qkv, qkv)


ATT_CHUNK = 64


def _chunk_masks(n):
    masks = []
    for half in range(WINDOW // ATT_CHUNK):
        qi = _row_iota((ATT_CHUNK, 2 * WINDOW)) + half * ATT_CHUNK
        kj = _lane_iota((ATT_CHUNK, 2 * WINDOW)) - WINDOW
        masks.append((kj <= qi) & (qi - kj < WINDOW) & ((n > 0) | (kj >= 0)))
    return masks


def _all_stacked_queries(ref):
    return jnp.concatenate([_stacked_queries(ref, g) for g in range(A_KV_HEADS)], axis=0)


def _by_group(fn, lhs, rhs_per_group):
    return jnp.concatenate([fn(lhs[g * GROUP_ROWS:(g + 1) * GROUP_ROWS], rhs_per_group[g])
                            for g in range(A_KV_HEADS)], axis=0)


BWD_BLOCKS = 8


def _attn_bwd(qkv, do, out, lse, sink_col, B, S):
    T = B * S
    nb = S // WINDOW
    M = math.gcd(nb, BWD_BLOCKS)
    steps = nb // M
    n_chunks = ATT_ROWS // ATT_CHUNK
    halves = WINDOW // ATT_CHUNK

    def block(has_prev, sink_ref, q, do_b, out_b, lse_b, k, v, scratch, want_dq):
        s_ref, dp_ref, ds_ref, p_ref = scratch
        width = k.shape[0]
        masks = [mk[:, 0:width] for mk in _chunk_masks(has_prev)]
        kd = [_dup_kv_head(k, g) for g in range(A_KV_HEADS)]
        vd = [_dup_kv_head(v, g) for g in range(A_KV_HEADS)]
        qs, dos = _all_stacked_queries(q), _all_stacked_queries(do_b)
        s_ref[...] = _by_group(_dot_nt, qs, kd)
        dp_ref[...] = _by_group(_dot_nt, dos, vd)
        lane = _lane_iota((ATT_CHUNK, LANES))
        lo = lane < A_HEAD_DIM
        lane1 = _lane_iota((1, LANES))
        dsink_row = jnp.zeros((1, LANES), F32)
        for c in range(n_chunks):
            rows = slice(c * ATT_CHUNK, (c + 1) * ATT_CHUNK)
            head, half = divmod(c, halves)
            qrows = slice(half * ATT_CHUNK, (half + 1) * ATT_CHUNK)
            slab = slice((head // 2) * LANES, (head // 2 + 1) * LANES)
            lse_col = jnp.sum(jnp.where(lane == head, lse_b[qrows, :], 0.0), axis=-1, keepdims=True)
            prod = do_b[qrows, slab].astype(F32) * out_b[qrows, slab].astype(F32)
            mine = lo if head % 2 == 0 else jnp.logical_not(lo)
            delta = jnp.sum(jnp.where(mine, prod, 0.0), axis=-1, keepdims=True)
            prob = jnp.exp(jnp.where(masks[half], s_ref[rows, :], NEG_BIG) - lse_col)
            p_ref[rows, :] = prob.astype(BF16)
            ds_ref[rows, :] = (prob * (dp_ref[rows, :] - delta)).astype(BF16)
            w = -jnp.exp(sink_ref[rows, :] - lse_col) * delta
            dsink_row += jnp.where(lane1 == head, jnp.sum(w, axis=0, keepdims=True), 0.0)
        dq = _by_group(_dot, ds_ref[...], kd) * ATT_SCALE if want_dq else None
        groups = [slice(g * GROUP_ROWS, (g + 1) * GROUP_ROWS) for g in range(A_KV_HEADS)]
        dk = [_dot_tn(ds_ref[rows, :], qs[rows]) for rows in groups]
        dv = [_dot_tn(p_ref[rows, :], dos[rows]) for rows in groups]
        return dq, dk, dv, dsink_row

    def fold(per_group):
        lane = _lane_iota((WINDOW, LANES))
        out = jnp.zeros((WINDOW, LANES), F32)
        for g, acc in enumerate(per_group):
            out = jnp.where((lane < A_HEAD_DIM) == (g == 0), acc + pltpu.roll(acc, A_HEAD_DIM, 1), out)
        return out

    def body(sink_ref, q_ref, qn_ref, do_ref, don_ref, out_ref, outn_ref, lse_ref, lsen_ref, kc_ref, kp_ref, vc_ref, vp_ref,
             dq_ref, dkv_ref, dsink_ref, s_scr, dp_scr, ds_scr, p_scr, s_x, dp_x, ds_x, p_x):
        b, m = pl.program_id(0), pl.program_id(1)

        @pl.when((b == 0) & (m == 0))
        def _():
            dsink_ref[...] = jnp.zeros_like(dsink_ref)

        k_all = jnp.concatenate([kp_ref[...], kc_ref[...]], axis=0)
        v_all = jnp.concatenate([vp_ref[...], vc_ref[...]], axis=0)
        results = []
        for j in range(M):
            rows = slice(j * WINDOW, (j + 1) * WINDOW)
            keys = slice(j * WINDOW, (j + 2) * WINDOW)
            has_prev = m if j == 0 else 1
            results.append(block(has_prev, sink_ref, q_ref[rows, :], do_ref[rows, :], out_ref[rows, :], lse_ref[rows, :],
                                 k_all[keys], v_all[keys], (s_scr.at[j], dp_scr.at[j], ds_scr.at[j], p_scr.at[j]), True))
        last_keys = slice(M * WINDOW, (M + 1) * WINDOW)
        _, dk_x, dv_x, _ = block(1, sink_ref, qn_ref[...], don_ref[...], outn_ref[...], lsen_ref[...],
                                 k_all[last_keys], v_all[last_keys], (s_x, dp_x, ds_x, p_x), False)
        has_next = m < steps - 1
        lo_q = _lane_iota((WINDOW, LANES)) < A_HEAD_DIM
        dsink_row = jnp.zeros((1, LANES), F32)
        for j, (dq, dk, dv, ds_row) in enumerate(results):
            rows = slice(j * WINDOW, (j + 1) * WINDOW)
            for p in range(A_HEADS // 2):
                dq_ref[rows, p * LANES:(p + 1) * LANES] = jnp.where(
                    lo_q, dq[2 * p * WINDOW:(2 * p + 1) * WINDOW], dq[(2 * p + 1) * WINDOW:(2 * p + 2) * WINDOW]).astype(BF16)
            if j + 1 < M:
                dk_next = [t[0:WINDOW] for t in results[j + 1][1]]
                dv_next = [t[0:WINDOW] for t in results[j + 1][2]]
            else:
                dk_next = [jnp.where(has_next, t, 0.0) for t in dk_x]
                dv_next = [jnp.where(has_next, t, 0.0) for t in dv_x]
            dkv_ref[rows, 0:LANES] = fold([own[WINDOW:] + nxt for own, nxt in zip(dk, dk_next)]).astype(BF16)
            dkv_ref[rows, LANES:] = fold([own[WINDOW:] + nxt for own, nxt in zip(dv, dv_next)]).astype(BF16)
            dsink_row += ds_row
        dsink_ref[0:1, :] += dsink_row

    def cur(col, w):
        return pl.BlockSpec((M * WINDOW, w), lambda b, m: (b * steps + m, col))

    def nxt(col, w):
        return pl.BlockSpec((WINDOW, w), lambda b, m: (b * nb + jnp.minimum(M * (m + 1), nb - 1), col))

    def prev(col):
        return pl.BlockSpec((WINDOW, LANES), lambda b, m: (b * nb + jnp.maximum(M * m - 1, 0), col))

    kcol, vcol = QKV_K // LANES, QKV_V // LANES
    scores = (M, ATT_ROWS, 2 * WINDOW)
    extra = (ATT_ROWS, WINDOW)
    return pl.pallas_call(
        body, name="attn_bwd", grid=(B, steps),
        in_specs=[_const_spec((ATT_ROWS, 1)), cur(0, A_WIDTH), nxt(0, A_WIDTH), cur(0, A_WIDTH), nxt(0, A_WIDTH),
                  cur(0, A_WIDTH), nxt(0, A_WIDTH), cur(0, LANES), nxt(0, LANES),
                  cur(kcol, LANES), prev(kcol), cur(vcol, LANES), prev(vcol)],
        out_specs=[cur(0, A_WIDTH), cur(0, 2 * LANES), pl.BlockSpec((8, LANES), lambda b, m: (0, 0))],
        out_shape=[jax.ShapeDtypeStruct((T, A_WIDTH), BF16), jax.ShapeDtypeStruct((T, 2 * LANES), BF16),
                   jax.ShapeDtypeStruct((8, LANES), F32)],
        scratch_shapes=[pltpu.VMEM(scores, F32), pltpu.VMEM(scores, F32), pltpu.VMEM(scores, BF16), pltpu.VMEM(scores, BF16),
                        pltpu.VMEM(extra, F32), pltpu.VMEM(extra, F32), pltpu.VMEM(extra, BF16), pltpu.VMEM(extra, BF16)],
        compiler_params=_params("arbitrary", "arbitrary"),
    )(sink_col, qkv, qkv, do, do, out, out, lse, lse, qkv, qkv, qkv, qkv)


GLA_TILE = 256
CHUNKS_PER_TILE = GLA_TILE // B_CHUNK


def _gla_factors(q_ref, k_ref, cum_ref):
    scale = B_KEY_DIM ** -0.5
    cum = cum_ref[...]
    shape = (B_CHUNK, B_KEY_WIDTH)
    last = jnp.concatenate([jnp.broadcast_to(cum_ref[pl.ds(c * B_CHUNK + B_CHUNK - 1, 1), :], shape)
                            for c in range(CHUNKS_PER_TILE)], axis=0)
    mid = jnp.concatenate([jnp.broadcast_to(cum_ref[pl.ds(c * B_CHUNK + B_CHUNK // 2 - 1, 1), :], shape)
                           for c in range(CHUNKS_PER_TILE)], axis=0)
    e_qm, e_km, e_qe, e_kd = jnp.exp(cum - mid), jnp.exp(mid - cum), jnp.exp(cum), jnp.exp(last - cum)
    qs = q_ref[...] * scale
    k = k_ref[...]
    return qs, k, (e_qm, e_km, e_qe, e_kd)


def _head_mask(shape, h):
    return (_lane_iota(shape) // B_KEY_DIM) == h


def _stack_masked(t):
    return jnp.concatenate([jnp.where(_head_mask(t.shape, h), t, 0.0) for h in range(B_HEADS)], axis=0).astype(BF16)


def _select_heads(t):
    shape = (B_CHUNK, B_KEY_WIDTH)
    out = jnp.zeros(shape, F32)
    for h in range(B_HEADS):
        out = jnp.where(_head_mask(shape, h), t[h * B_CHUNK:(h + 1) * B_CHUNK], out)
    return out


def _select_state(t):
    shape = (B_VAL_DIM, B_KEY_WIDTH)
    out = jnp.zeros(shape, F32)
    for h in range(B_HEADS):
        out = jnp.where(_head_mask(shape, h), t[h * B_VAL_DIM:(h + 1) * B_VAL_DIM], out)
    return out


def _rows_by_head(t):
    return jnp.concatenate([t[:, h * B_VAL_DIM:(h + 1) * B_VAL_DIM] for h in range(B_HEADS)], axis=0)


def _intra_mask():
    i, j = _row_iota((GLA_TILE, GLA_TILE)), _lane_iota((GLA_TILE, GLA_TILE))
    return (i // B_CHUNK == j // B_CHUNK) & (j <= i)


def _pair_stack(t, p):
    slab = t[:, p * LANES:(p + 1) * LANES]
    lo = _lane_iota(slab.shape) < B_KEY_DIM
    return jnp.concatenate([jnp.where(lo, slab, 0.0), jnp.where(lo, 0.0, slab)], axis=0).astype(BF16)


def _gla_fwd(q, k, cum, vb, B, S):
    T = B * S
    nt = S // GLA_TILE

    def one_sequence(q_ref, k_ref, cum_ref, v_ref, o_ref, st_all_ref, st_ref):
        qs, kk, (e_qm, e_km, e_qe, e_kd) = _gla_factors(q_ref, k_ref, cum_ref)
        qm, km, qe, kd = qs * e_qm, kk * e_km, qs * e_qe, (kk * e_kd).astype(BF16)
        mask = _intra_mask()
        intra = []
        for p in range(B_HEADS // 2):
            a = _dot_nt(_pair_stack(qm, p), km[:, p * LANES:(p + 1) * LANES].astype(BF16))
            for hh in range(2):
                h = 2 * p + hh
                att = jnp.where(mask, a[hh * GLA_TILE:(hh + 1) * GLA_TILE], 0.0).astype(BF16)
                intra.append(_dot(att, v_ref[:, h * B_VAL_DIM:(h + 1) * B_VAL_DIM]))
        inter = []
        for c in range(CHUNKS_PER_TILE):
            rows = slice(c * B_CHUNK, (c + 1) * B_CHUNK)
            st = st_ref[...]
            st_all_ref[c] = st
            inter.append(_dot_nt(_stack_masked(qe[rows]), st.astype(BF16)))
            inc = _select_state(_dot_tn(v_ref[rows, :], kd[rows]))
            decay = jnp.exp(cum_ref[pl.ds(c * B_CHUNK + B_CHUNK - 1, 1), :])
            st_ref[...] = st * decay + inc
        for h in range(B_HEADS):
            oi = jnp.concatenate([inter[c][h * B_CHUNK:(h + 1) * B_CHUNK] for c in range(CHUNKS_PER_TILE)], axis=0)
            o_ref[:, h * B_VAL_DIM:(h + 1) * B_VAL_DIM] = (intra[h] + oi).astype(BF16)

    def body(q_ref, k_ref, cum_ref, v_ref, o_ref, st_all_ref, st_ref):
        @pl.when(pl.program_id(0) == 0)
        def _():
            st_ref[...] = jnp.zeros_like(st_ref)

        for b in range(B):
            one_sequence(*[r.at[b] for r in (q_ref, k_ref, cum_ref, v_ref, o_ref, st_all_ref, st_ref)])

    def rows(w):
        return pl.BlockSpec((B, GLA_TILE, w), lambda t: (0, t, 0))

    seq = lambda a: a.reshape(B, S, a.shape[-1])
    o, st_all = pl.pallas_call(
        body, name="gla_fwd", grid=(nt,),
        in_specs=[rows(B_KEY_WIDTH), rows(B_KEY_WIDTH), rows(B_KEY_WIDTH), rows(B_WIDTH)],
        out_specs=[rows(B_WIDTH),
                   pl.BlockSpec((B, CHUNKS_PER_TILE, B_VAL_DIM, B_KEY_WIDTH), lambda t: (0, t, 0, 0))],
        out_shape=[jax.ShapeDtypeStruct((B, S, B_WIDTH), BF16),
                   jax.ShapeDtypeStruct((B, S // B_CHUNK, B_VAL_DIM, B_KEY_WIDTH), F32)],
        scratch_shapes=[pltpu.VMEM((B, B_VAL_DIM, B_KEY_WIDTH), F32)],
        compiler_params=_params("arbitrary"),
    )(seq(q), seq(k), seq(cum), seq(vb))
    return o.reshape(T, B_WIDTH), st_all.reshape(T // B_CHUNK, B_VAL_DIM, B_KEY_WIDTH)


def _gla_bwd(q, k, cum, vb, do, st_all, B, S, wgrads):
    T = B * S
    nt = S // GLA_TILE
    scale = B_KEY_DIM ** -0.5
    nw = len(wgrads)

    def one_sequence(q_ref, k_ref, cum_ref, v_ref, do_ref, st_all_ref, dq_ref, dk_ref, dv_ref, dla_ref, dst_ref):
        qs, kk, (e_qm, e_km, e_qe, e_kd) = _gla_factors(q_ref, k_ref, cum_ref)
        qm, km, qe, kd = qs * e_qm, kk * e_km, qs * e_qe, kk * e_kd
        mask = _intra_mask()
        dqm_slabs, dkm_slabs, dv_intra = [], [], []
        for p in range(B_HEADS // 2):
            qm_st = _pair_stack(qm, p)
            km_p = km[:, p * LANES:(p + 1) * LANES].astype(BF16)
            a = _dot_nt(qm_st, km_p)
            da_blocks, dqm_h = [], []
            for hh in range(2):
                h = 2 * p + hh
                vs = slice(h * B_VAL_DIM, (h + 1) * B_VAL_DIM)
                att = jnp.where(mask, a[hh * GLA_TILE:(hh + 1) * GLA_TILE], 0.0).astype(BF16)
                dv_intra.append(_dot_tn(att, do_ref[:, vs]))
                da = jnp.where(mask, _dot_nt(do_ref[:, vs], v_ref[:, vs]), 0.0).astype(BF16)
                da_blocks.append(da)
                dqm_h.append(_dot(da, km_p))
            lo = _lane_iota((GLA_TILE, LANES)) < B_KEY_DIM
            dqm_slabs.append(jnp.where(lo, dqm_h[0], dqm_h[1]))
            dkm_slabs.append(_dot_tn(jnp.concatenate(da_blocks, axis=0), qm_st))
        dqm = jnp.concatenate(dqm_slabs, axis=1)
        dkm = jnp.concatenate(dkm_slabs, axis=1)

        dqe_c, dkd_c, dv_inter, tail_c = ([None] * CHUNKS_PER_TILE for _ in range(4))
        for c in reversed(range(CHUNKS_PER_TILE)):
            rows = slice(c * B_CHUNK, (c + 1) * B_CHUNK)
            dst = dst_ref[...]
            dst_b = dst.astype(BF16)
            dv_inter[c] = _dot_nt(_stack_masked(kd[rows]), dst_b)
            dkd_c[c] = _select_heads(_dot(_rows_by_head(v_ref[rows, :]), dst_b))
            do_c = do_ref[rows, :]
            dqe_c[c] = _select_heads(_dot(_rows_by_head(do_c), st_all_ref[c].astype(BF16)))
            contrib = _select_state(_dot_tn(do_c, qe[rows].astype(BF16)))
            decay = jnp.exp(cum_ref[pl.ds(c * B_CHUNK + B_CHUNK - 1, 1), :])
            tail = (jnp.sum(kk[rows] * dkd_c[c] * e_kd[rows], axis=0, keepdims=True)
                    + decay * jnp.sum(st_all_ref[c] * dst, axis=0, keepdims=True))
            tail_c[c] = jnp.broadcast_to(tail, (B_CHUNK, B_KEY_WIDTH))
            dst_ref[...] = dst * decay + contrib
        dqe = jnp.concatenate(dqe_c, axis=0)
        dkd = jnp.concatenate(dkd_c, axis=0)
        dqs = dqm * e_qm + dqe * e_qe
        dk = dkm * e_km + dkd * e_kd
        dq_ref[...] = (dqs * scale).astype(BF16)
        dk_ref[...] = dk.astype(BF16)
        for h in range(B_HEADS):
            dvi = jnp.concatenate([dv_inter[c][h * B_CHUNK:(h + 1) * B_CHUNK] for c in range(CHUNKS_PER_TILE)], axis=0)
            dv_ref[:, h * B_VAL_DIM:(h + 1) * B_VAL_DIM] = (dv_intra[h] + dvi).astype(BF16)
        dd = qs * dqs - kk * dk
        i, j = _row_iota((GLA_TILE, GLA_TILE)), _lane_iota((GLA_TILE, GLA_TILE))
        upper = ((i // B_CHUNK == j // B_CHUNK) & (j >= i)).astype(BF16)
        hi, mid, lo3 = _split3(dd)
        dla_ref[...] = _dot(upper, hi) + _dot(upper, mid) + _dot(upper, lo3) + jnp.concatenate(tail_c, axis=0)

    def body(q_ref, k_ref, cum_ref, v_ref, do_ref, st_all_ref, *rest):
        g_refs, (dq_ref, dk_ref, dv_ref, dla_ref) = rest[:nw], rest[nw:nw + 4]
        rv_refs, (dst_ref, send_sems, recv_sems) = rest[nw + 4:2 * nw + 4], rest[2 * nw + 4:]
        x, y, c = _my_place()

        def wcopy(a, r):
            dx, dy, dc = FLIPS[r]
            return pltpu.make_async_remote_copy(
                src_ref=g_refs[a].at[4 * (x ^ dx) + 2 * (y ^ dy) + (c ^ dc)], dst_ref=rv_refs[a].at[r],
                send_sem=send_sems.at[a, r], recv_sem=recv_sems.at[a, r],
                device_id=(x ^ dx, y ^ dy, c ^ dc), device_id_type=MESH)

        @pl.when(pl.program_id(0) == 0)
        def _():
            dst_ref[...] = jnp.zeros_like(dst_ref)
            for a in range(nw):
                for r in range(len(FLIPS)):
                    wcopy(a, r).start()

        for b in range(B):
            one_sequence(*[r.at[b] for r in (q_ref, k_ref, cum_ref, v_ref, do_ref, st_all_ref,
                                             dq_ref, dk_ref, dv_ref, dla_ref, dst_ref)])

        @pl.when(pl.program_id(0) == nt - 1)
        def _():
            for a in range(nw):
                for r in range(len(FLIPS)):
                    wcopy(a, r).wait()

    def rows(w):
        return pl.BlockSpec((B, GLA_TILE, w), lambda t: (0, nt - 1 - t, 0))

    seq = lambda a: a.reshape(B, S, a.shape[-1])
    res = pl.pallas_call(
        body, name="gla_bwd", grid=(nt,),
        in_specs=[rows(B_KEY_WIDTH), rows(B_KEY_WIDTH), rows(B_KEY_WIDTH), rows(B_WIDTH), rows(B_WIDTH),
                  pl.BlockSpec((B, CHUNKS_PER_TILE, B_VAL_DIM, B_KEY_WIDTH), lambda t: (0, nt - 1 - t, 0, 0))]
                 + _any_specs(nw),
        out_specs=[rows(B_KEY_WIDTH), rows(B_KEY_WIDTH), rows(B_WIDTH), rows(B_KEY_WIDTH)] + _any_specs(nw),
        out_shape=[jax.ShapeDtypeStruct((B, S, B_KEY_WIDTH), BF16), jax.ShapeDtypeStruct((B, S, B_KEY_WIDTH), BF16),
                   jax.ShapeDtypeStruct((B, S, B_WIDTH), BF16), jax.ShapeDtypeStruct((B, S, B_KEY_WIDTH), F32)]
                  + [jax.ShapeDtypeStruct((len(FLIPS), *g.shape[1:]), g.dtype) for g in wgrads],
        scratch_shapes=[pltpu.VMEM((B, B_VAL_DIM, B_KEY_WIDTH), F32),
                        pltpu.SemaphoreType.DMA((nw, len(FLIPS))), pltpu.SemaphoreType.DMA((nw, len(FLIPS)))],
        compiler_params=_params("arbitrary"),
    )(seq(q), seq(k), seq(cum), seq(vb), seq(do), st_all.reshape(B, S // B_CHUNK, B_VAL_DIM, B_KEY_WIDTH), *wgrads)
    return [a.reshape(T, a.shape[-1]) for a in res[:4]], res[4:]


def _merge(x2, tgt2, attn, za, o_gla, zb, ga, gb, w_oa_sh, w_ob_sh, w_o, g_gla, g_final):
    T = x2.shape[0]
    tm = math.gcd(T, 512)
    sub = math.gcd(tm, 256)
    last = T // tm - 1

    def body(x_ref, tgt_ref, attn_ref, za_ref, og_ref, zb_ref, ga_ref, gb_ref,
             woa_sh_ref, wob_sh_ref, wo_ref, gg_ref, gf_ref,
             dxres_ref, dattn_ref, dog_ref, dza_ref, dzb_ref, dga_ref, dgb_ref,
             dwo_out, dwoa_out, dwob_out, small_ref,
             awo_ref, awoa_ref, awob_ref, agf_ref, agg_ref, loss_ref, woa_ref, wob_ref,
             dwo_ref, dwoa_ref, dwob_ref, w_sems, dw_sems):
        w_copies = [pltpu.make_async_copy(sh.at[j], dst.at[:, j * SHARD_OUT:(j + 1) * SHARD_OUT], w_sems.at[a, j])
                    for a, (sh, dst) in enumerate(((woa_sh_ref, woa_ref), (wob_sh_ref, wob_ref))) for j in range(N_DEV)]
        dw_copies = [pltpu.make_async_copy(src, dst, dw_sems.at[a])
                     for a, (src, dst) in enumerate(((dwo_ref, dwo_out), (dwoa_ref, dwoa_out), (dwob_ref, dwob_out)))]

        @pl.when(pl.program_id(0) == 0)
        def _():
            for cp in w_copies:
                cp.start()
            for r in (awo_ref, awoa_ref, awob_ref, agf_ref, agg_ref, loss_ref):
                r[...] = jnp.zeros_like(r)
            for cp in w_copies:
                cp.wait()

        def one_tile(rows):
            za_v = za_ref[rows, :].astype(F32)
            sig_za = _sigmoid_tanh(za_v)
            silu_a = za_v * sig_za
            attn_v = attn_ref[rows, :].astype(F32)
            oa = (attn_v * silu_a).astype(BF16)
            ya = _dot(oa, woa_ref[...])
            og = og_ref[rows, :].astype(F32)
            zb_v = zb_ref[rows, :].astype(F32)
            sig_zb = _sigmoid_tanh(zb_v)
            silu_b = zb_v * sig_zb
            gg = gg_ref[...]
            on_parts, rinv_parts = [], []
            for h in range(B_HEADS):
                seg = og[:, h * B_VAL_DIM:(h + 1) * B_VAL_DIM]
                rinv = lax.rsqrt(jnp.mean(seg * seg, axis=-1, keepdims=True) + NORM_EPS)
                rinv_parts.append(rinv)
                on_parts.append(seg * rinv)
            on = jnp.concatenate(on_parts, axis=1)
            obn = on * gg
            ob = (obn * silu_b).astype(BF16)
            yb = _dot(ob, wob_ref[...])
            sig_a = _sigmoid_tanh(ga_ref[rows, :].astype(F32))
            sig_b = _sigmoid_tanh(gb_ref[rows, :].astype(F32))
            merged = (sig_a * ya + sig_b * yb).astype(BF16)
            out = x_ref[rows, :] + _dot(merged, wo_ref[...])
            rf = lax.rsqrt(jnp.mean(out * out, axis=-1, keepdims=True) + NORM_EPS)
            nrm = out * rf
            gf = gf_ref[...]
            err = nrm * gf - tgt_ref[rows, :]
            loss = jnp.sum(err * err) * (0.5 / D_MODEL)

            dy = err * (1.0 / D_MODEL)
            dgf = jnp.sum(dy * nrm, axis=0, keepdims=True)
            dn = dy * gf
            dout = rf * (dn - nrm * jnp.mean(dn * nrm, axis=-1, keepdims=True))
            dxres_ref[rows, :] = dout
            dout_b = dout.astype(BF16)
            dmerged = _dot_nt(dout_b, wo_ref[...])
            dya = dmerged * sig_a
            dyb = dmerged * sig_b
            dga_ref[rows, :] = (dmerged * ya * sig_a * (1.0 - sig_a)).astype(BF16)
            dgb_ref[rows, :] = (dmerged * yb * sig_b * (1.0 - sig_b)).astype(BF16)
            dya_b, dyb_b = dya.astype(BF16), dyb.astype(BF16)
            doa = _dot_nt(dya_b, woa_ref[...])
            dattn_ref[rows, :] = (doa * silu_a).astype(BF16)
            dza_ref[rows, :] = (doa * attn_v * (sig_za * (1.0 + za_v * (1.0 - sig_za)))).astype(BF16)
            dob = _dot_nt(dyb_b, wob_ref[...])
            dzb_ref[rows, :] = (dob * obn * (sig_zb * (1.0 + zb_v * (1.0 - sig_zb)))).astype(BF16)
            dobn = dob * silu_b
            dgg = jnp.sum(dobn * on, axis=0, keepdims=True)
            don = dobn * gg
            for h in range(B_HEADS):
                sl = slice(h * B_VAL_DIM, (h + 1) * B_VAL_DIM)
                don_h, on_h = don[:, sl], on[:, sl]
                dog_ref[rows, sl] = (rinv_parts[h] * (don_h - on_h * jnp.mean(don_h * on_h, axis=-1, keepdims=True))
                                     ).astype(BF16)
            return (merged, dout_b, oa, dya_b, ob, dyb_b), (loss, dgf, dgg)

        tiles = [one_tile(pl.ds(j * sub, sub)) for j in range(tm // sub)]
        merged, dout_b, oa, dya_b, ob, dyb_b = (jnp.concatenate(parts, axis=0) for parts in zip(*[t[0] for t in tiles]))
        awo_ref[...] += _dot_tn(merged, dout_b)
        awoa_ref[...] += _dot_tn(oa, dya_b)
        awob_ref[...] += _dot_tn(ob, dyb_b)
        for _, (loss, dgf, dgg) in tiles:
            loss_ref[...] += loss
            agf_ref[...] += dgf
            agg_ref[...] += dgg

        @pl.when(pl.program_id(0) == last)
        def _():
            for j in range(N_DEV):
                dwo_ref[j] = awo_ref[j * SHARD_OUT:(j + 1) * SHARD_OUT, :].astype(BF16)
                dwoa_ref[j] = awoa_ref[:, j * SHARD_OUT:(j + 1) * SHARD_OUT].astype(BF16)
                dwob_ref[j] = awob_ref[:, j * SHARD_OUT:(j + 1) * SHARD_OUT].astype(BF16)
            small_ref[...] = jnp.zeros_like(small_ref)
            _put_rows(small_ref, SMALL_G_FINAL, agf_ref[...])
            _put_rows(small_ref, SMALL_G_GLA, agg_ref[...])
            small_ref[SMALL_LOSS:SMALL_LOSS + 1, :] = loss_ref[...]
            for cp in dw_copies:
                cp.start()
            for cp in dw_copies:
                cp.wait()

    def rows(w):
        return pl.BlockSpec((tm, w), lambda i: (i, 0))

    def whole(shape):
        nd = len(shape)
        return pl.BlockSpec(shape, lambda i: (0,) * nd)

    outs = [((T, D_MODEL), F32, rows(D_MODEL)), ((T, A_WIDTH), BF16, rows(A_WIDTH)), ((T, B_WIDTH), BF16, rows(B_WIDTH)),
            ((T, A_WIDTH), BF16, rows(A_WIDTH)), ((T, B_WIDTH), BF16, rows(B_WIDTH)),
            ((T, D_MODEL), BF16, rows(D_MODEL)), ((T, D_MODEL), BF16, rows(D_MODEL)),
            ((N_DEV, SHARD_OUT, D_MODEL), BF16, pl.BlockSpec(memory_space=pl.ANY)),
            ((N_DEV, A_WIDTH, SHARD_OUT), BF16, pl.BlockSpec(memory_space=pl.ANY)),
            ((N_DEV, B_WIDTH, SHARD_OUT), BF16, pl.BlockSpec(memory_space=pl.ANY)),
            ((SMALL_SINKS, LANES), F32, whole((SMALL_SINKS, LANES)))]
    return pl.pallas_call(
        body, name="merge", grid=(T // tm,),
        in_specs=[rows(D_MODEL), rows(D_MODEL), rows(A_WIDTH), rows(A_WIDTH), rows(B_WIDTH), rows(B_WIDTH),
                  rows(D_MODEL), rows(D_MODEL),
                  pl.BlockSpec(memory_space=pl.ANY), pl.BlockSpec(memory_space=pl.ANY),
                  _const_spec((D_MODEL, D_MODEL)), _const_spec((1, B_WIDTH)), _const_spec((1, D_MODEL))],
        out_specs=[o[2] for o in outs],
        out_shape=[jax.ShapeDtypeStruct(o[0], o[1]) for o in outs],
        scratch_shapes=[pltpu.VMEM((D_MODEL, D_MODEL), F32), pltpu.VMEM((A_WIDTH, D_MODEL), F32),
                        pltpu.VMEM((B_WIDTH, D_MODEL), F32), pltpu.VMEM((1, D_MODEL), F32), pltpu.VMEM((1, B_WIDTH), F32),
                        pltpu.VMEM((1, LANES), F32), pltpu.VMEM((A_WIDTH, D_MODEL), BF16),
                        pltpu.VMEM((B_WIDTH, D_MODEL), BF16),
                        pltpu.VMEM((N_DEV, SHARD_OUT, D_MODEL), BF16), pltpu.VMEM((N_DEV, A_WIDTH, SHARD_OUT), BF16),
                        pltpu.VMEM((N_DEV, B_WIDTH, SHARD_OUT), BF16),
                        pltpu.SemaphoreType.DMA((2, N_DEV)), pltpu.SemaphoreType.DMA((3,))],
        compiler_params=pltpu.CompilerParams(dimension_semantics=("arbitrary",), vmem_limit_bytes=V7X_VMEM_LIMIT_MAX),
    )(x2, tgt2, attn, za, o_gla, zb, ga, gb, w_oa_sh, w_ob_sh, w_o, g_gla, g_final)


def _in_proj_bwd(x2, dxres, cosf, sinf, g_in, wt_pad, wa_pad, parts):
    T = x2.shape[0]
    tm = 256
    last = T // tm - 1
    base = SMALL_G_IN

    def body(x_ref, dxres_ref, cos_ref, sin_ref, g_ref, wt_ref, wa_ref,
             dq_ref, dkv_ref, dza_ref, dqb_ref, dkb_ref, dvb_ref, dzb_ref, dla_ref, u_ref, alr_ref, dga_ref, dgb_ref,
             dx_ref, dsh_ref, small_ref, dproj_ref, agin_ref, aba_ref, awa_ref):
        @pl.when(pl.program_id(0) == 0)
        def _():
            for r in (agin_ref, aba_ref, awa_ref):
                r[...] = jnp.zeros_like(r)

        cos, nsin = cos_ref[...], -sin_ref[...]
        for s in range(A_WIDTH // LANES):
            sl = slice(s * LANES, (s + 1) * LANES)
            dproj_ref[:, sl] = _rope_slab(dq_ref[:, sl].astype(F32), cos, nsin).astype(BF16)
        dproj_ref[:, QKV_K:QKV_V] = _rope_slab(dkv_ref[:, 0:LANES].astype(F32), cos, nsin).astype(BF16)
        dproj_ref[:, QKV_V:QKV_W] = dkv_ref[:, LANES:]

        def put(name, val):
            a, b = SEG[name]
            dproj_ref[:, a:b] = val

        put("za", dza_ref[...])
        put("qb", dqb_ref[...])
        put("kb", dkb_ref[...])
        put("vb", dvb_ref[...])
        put("zb", dzb_ref[...])
        put("ga", dga_ref[...])
        put("gb", dgb_ref[...])
        du = dla_ref[...] * (1.0 / B_GATE_TEMP) * _sigmoid(-u_ref[...])
        aba_ref[...] += jnp.sum(du, axis=0, keepdims=True)
        du_b = du.astype(BF16)
        awa_ref[...] += _dot_tn(alr_ref[...], du_b)
        put("alr", _dot_nt(du_b, wa_ref[...]).astype(BF16))

        for j in range(N_DEV):
            col = (j % 2) * SHARD_PAD
            for a, b in _shard_pad_cols(j):
                dsh_ref[j // 2, :, col:col + b - a] = dproj_ref[:, a:b]
                col += b - a
            dsh_ref[j // 2, :, col:(j % 2 + 1) * SHARD_PAD] = jnp.zeros((tm, SHARD_PAD - SHARD_IN), BF16)

        dh = _dot(dproj_ref[...], wt_ref[...])
        x = x_ref[...]
        r = lax.rsqrt(jnp.mean(x * x, axis=-1, keepdims=True) + NORM_EPS)
        nrm = x * r
        agin_ref[...] += jnp.sum(dh * nrm, axis=0, keepdims=True)
        dn = dh * g_ref[...]
        dx_ref[...] = dxres_ref[...] + r * (dn - nrm * jnp.mean(dn * nrm, axis=-1, keepdims=True))

        @pl.when(pl.program_id(0) == last)
        def _():
            small_ref[...] = jnp.zeros_like(small_ref)
            _put_rows(small_ref, SMALL_G_IN - base, agin_ref[...])
            _put_rows(small_ref, SMALL_B_ALPHA - base, aba_ref[...])
            for half in range(B_KEY_WIDTH // LANES):
                r0 = SMALL_W_ALPHA - base + half * B_GATE_RANK
                small_ref[r0:r0 + B_GATE_RANK, :] = awa_ref[0:B_GATE_RANK, half * LANES:(half + 1) * LANES]

    def rows(w):
        return pl.BlockSpec((tm, w), lambda i: (i, 0))

    names = ["dq", "dkv", "dza", "dqb", "dkb", "dvb", "dzb", "dla", "u", "alr", "dga", "dgb"]
    return pl.pallas_call(
        body, name="in_proj_bwd", grid=(T // tm,),
        in_specs=[rows(D_MODEL), rows(D_MODEL), rows(LANES), rows(LANES), _const_spec((1, D_MODEL)),
                  _const_spec((D_IN_PAD, D_MODEL)), _const_spec((RANK_PAD, B_KEY_WIDTH))]
                 + [rows(parts[n].shape[1]) for n in names],
        out_specs=[rows(D_MODEL), pl.BlockSpec((N_CHIPS, tm, 2 * SHARD_PAD), lambda i: (0, i, 0)),
                   pl.BlockSpec((SMALL_ROWS - base, LANES), lambda i: (0, 0))],
        out_shape=[jax.ShapeDtypeStruct((T, D_MODEL), F32), jax.ShapeDtypeStruct((N_CHIPS, T, 2 * SHARD_PAD), BF16),
                   jax.ShapeDtypeStruct((SMALL_ROWS - base, LANES), F32)],
        scratch_shapes=[pltpu.VMEM((tm, D_IN_PAD), BF16), pltpu.VMEM((1, D_MODEL), F32), pltpu.VMEM((1, B_KEY_WIDTH), F32),
                        pltpu.VMEM((RANK_PAD, B_KEY_WIDTH), F32)],
        compiler_params=_params("arbitrary"),
    )(x2, dxres, cosf, sinf, g_in, wt_pad, wa_pad, *[parts[n] for n in names])


FLIPS = [(dx, dy, dc) for dx in (0, 1) for dy in (0, 1) for dc in (0, 1)][1:]


def _my_place():
    return lax.axis_index("x"), lax.axis_index("y"), lax.axis_index("c")


def _any_specs(n):
    return [pl.BlockSpec(memory_space=pl.ANY)] * n


def _gather_first(shards, pos_col):
    n = len(shards)
    T = pos_col.shape[0]
    rows_per_pass = math.gcd(T, 512)
    invf, sign = _rope_lane_constants()

    def body(*refs):
        ins, (pos_ref, invf_ref, sign_ref) = refs[:n], refs[n:n + 3]
        outs, (cos_ref, sin_ref) = refs[n + 3:2 * n + 3], refs[2 * n + 3:2 * n + 5]
        send_sems, recv_sems, local_sems = refs[2 * n + 5:]
        x, y, c = _my_place()
        me, sibling = (x, y, c), (x, y, 1 - c)
        chips = [(1 - x, y), (x, 1 - y), (1 - x, 1 - y)]

        def block(a, px, py, pc):
            return outs[a].at[4 * px + 2 * py + pc]

        def copy(a, k, blk, to, src=None):
            return pltpu.make_async_remote_copy(
                src_ref=block(a, *blk) if src is None else src, dst_ref=block(a, *blk),
                send_sem=send_sems.at[a, k], recv_sem=recv_sems.at[a, k], device_id=to, device_id_type=MESH)

        mine = [pltpu.make_async_copy(ins[a], block(a, *me), local_sems.at[a]) for a in range(n)]
        for cp in mine:
            cp.start()
        first = []
        for a in range(n):
            first.append(copy(a, 0, me, sibling, src=ins[a]))
            first += [copy(a, 1 + j, me, (*chip, c), src=ins[a]) for j, chip in enumerate(chips)]
        for cp in first:
            cp.start()

        def tables(i, carry):
            rows = pl.ds(pl.multiple_of(i * rows_per_pass, rows_per_pass), rows_per_pass)
            ang = pos_ref[rows, :].astype(F32) * invf_ref[...]
            cos_ref[rows, :] = jnp.cos(ang)
            sin_ref[rows, :] = jnp.sin(ang) * sign_ref[...]
            return carry

        lax.fori_loop(0, T // rows_per_pass, tables, 0)

        passed = []
        for j, chip in enumerate(chips):
            for a in range(n):
                copy(a, 1 + j, (*chip, c), me).wait_recv()
                fwd = copy(a, 4 + j, (*chip, c), sibling)
                fwd.start()
                passed.append(fwd)
        for a in range(n):
            copy(a, 0, sibling, me).wait_recv()
            for j, chip in enumerate(chips):
                copy(a, 4 + j, (*chip, 1 - c), me).wait_recv()
        for cp in first + passed:
            cp.wait_send()
        for cp in mine:
            cp.wait()

    vmem = pl.BlockSpec(memory_space=pltpu.VMEM)
    res = pl.pallas_call(
        body, name="gather_weights",
        in_specs=_any_specs(n) + [vmem] * 3, out_specs=_any_specs(n) + [vmem] * 2,
        out_shape=[jax.ShapeDtypeStruct((N_DEV, *s.shape), s.dtype) for s in shards]
                  + [jax.ShapeDtypeStruct((T, LANES), F32)] * 2,
        scratch_shapes=[pltpu.SemaphoreType.DMA((n, 7)), pltpu.SemaphoreType.DMA((n, 7)), pltpu.SemaphoreType.DMA((n,))],
        compiler_params=pltpu.CompilerParams(vmem_limit_bytes=V7X_VMEM_LIMIT),
    )(*shards, pos_col, invf, sign)
    return res[:n], res[n], res[n + 1]


def _w_in_grad_rs(h, dsh, chip_order, small):
    T = h.shape[0]
    tk = math.gcd(T, 2048)
    nk = T // tk
    chip_flips = [(1, 1), (1, 0), (0, 1)]
    n_steps = len(chip_flips) + 1
    SIB = len(chip_flips)
    k_finish = min(1, nk - 1)

    def body(order_ref, h_ref, d_ref, s_ref, own_ref, recv_ref, sall_ref,
             acc_ref, keep_ref, pre_ref, to_sib_ref, to_chip_ref,
             sib_send, sib_recv, chip_send, chip_recv, ssend_sems, srecv_sems, local_sem):
        i, kk = pl.program_id(0), pl.program_id(1)
        x, y, c = _my_place()
        my_dev = 4 * x + 2 * y + c

        def small_copy(r, slot):
            dx, dy, dc = FLIPS[r]
            return pltpu.make_async_remote_copy(
                src_ref=s_ref, dst_ref=sall_ref.at[slot], send_sem=ssend_sems.at[r], recv_sem=srecv_sems.at[r],
                device_id=(x ^ dx, y ^ dy, c ^ dc), device_id_type=MESH)

        keep_small = pltpu.make_async_copy(s_ref, sall_ref.at[my_dev], local_sem)

        def sib_copy(t):
            dst = recv_ref.at[SIB] if t == SIB else pre_ref.at[t]
            return pltpu.make_async_remote_copy(
                src_ref=to_sib_ref.at[t], dst_ref=dst, send_sem=sib_send.at[t], recv_sem=sib_recv.at[t],
                device_id=(x, y, 1 - c), device_id_type=MESH)

        def chip_copy(t):
            dx, dy = chip_flips[t]
            return pltpu.make_async_remote_copy(
                src_ref=to_chip_ref.at[t], dst_ref=recv_ref.at[t], send_sem=chip_send.at[t], recv_sem=chip_recv.at[t],
                device_id=(x ^ dx, y ^ dy, c), device_id_type=MESH)

        def halves():
            first, second = acc_ref[0:SHARD_PAD, :], acc_ref[SHARD_PAD:2 * SHARD_PAD, :]
            return jnp.where(c == 0, first, second), jnp.where(c == 0, second, first)

        @pl.when((i == 0) & (kk == 0))
        def _():
            keep_small.start()
            for r in range(len(FLIPS)):
                small_copy(r, my_dev).start()

        @pl.when(kk == 0)
        def _():
            acc_ref[...] = jnp.zeros_like(acc_ref)

        acc_ref[...] += _dot_tn(d_ref[...], h_ref[...])

        for t in range(len(chip_flips)):
            @pl.when((i == t + 1) & (kk == k_finish))
            def _(t=t):
                sib_copy(t).wait_recv()
                to_chip_ref[t] = (keep_ref[...] + pre_ref[t].astype(F32)).astype(BF16)
                chip_copy(t).start()

        for t in range(len(chip_flips)):
            @pl.when((i == t) & (kk == nk - 1))
            def _(t=t):
                mine, theirs = halves()
                to_sib_ref[t] = theirs.astype(BF16)
                sib_copy(t).start()
                keep_ref[...] = mine

        @pl.when((i == n_steps - 1) & (kk == nk - 1))
        def _():
            mine, theirs = halves()
            own_ref[...] = mine
            to_sib_ref[SIB] = theirs.astype(BF16)
            sib_copy(SIB).start()
            for t in range(len(chip_flips)):
                sib_copy(t).wait_send()
                chip_copy(t).wait_send()
                chip_copy(t).wait_recv()
            sib_copy(SIB).wait_send()
            sib_copy(SIB).wait_recv()
            for r, (dx, dy, dc) in enumerate(FLIPS):
                small_copy(r, 4 * (x ^ dx) + 2 * (y ^ dy) + (c ^ dc)).wait_recv()
                small_copy(r, my_dev).wait_send()
            keep_small.wait()

    shard = (SHARD_PAD, D_MODEL)
    return pl.pallas_call(
        body, name="w_in_grad_rs",
        grid_spec=pltpu.PrefetchScalarGridSpec(
            num_scalar_prefetch=1, grid=(n_steps, nk),
            in_specs=[pl.BlockSpec((tk, D_MODEL), lambda i, kk, order: (kk, 0)),
                      pl.BlockSpec((None, tk, 2 * SHARD_PAD), lambda i, kk, order: (order[i], kk, 0)),
                      pl.BlockSpec(memory_space=pl.ANY)],
            out_specs=[pl.BlockSpec(shard, lambda i, kk, order: (0, 0)),
                       pl.BlockSpec(memory_space=pl.ANY), pl.BlockSpec(memory_space=pl.ANY)],
            scratch_shapes=[pltpu.VMEM((2 * SHARD_PAD, D_MODEL), F32), pltpu.VMEM(shard, F32),
                            pltpu.VMEM((SIB, *shard), BF16), pltpu.VMEM((SIB + 1, *shard), BF16),
                            pltpu.VMEM((SIB, *shard), BF16),
                            pltpu.SemaphoreType.DMA((SIB + 1,)), pltpu.SemaphoreType.DMA((SIB + 1,)),
                            pltpu.SemaphoreType.DMA((SIB,)), pltpu.SemaphoreType.DMA((SIB,)),
                            pltpu.SemaphoreType.DMA((7,)), pltpu.SemaphoreType.DMA((7,)), pltpu.SemaphoreType.DMA]),
        out_shape=[jax.ShapeDtypeStruct(shard, F32),
                   jax.ShapeDtypeStruct((SIB + 1, *shard), BF16),
                   jax.ShapeDtypeStruct((N_DEV, *small.shape), F32)],
        compiler_params=_params("arbitrary", "arbitrary"),
    )(chip_order, h, dsh, small)


def _adam_math(w, g, m, v):
    m_new = ADAM_B1 * m + (1.0 - ADAM_B1) * g
    v_new = ADAM_B2 * v + (1.0 - ADAM_B2) * (g * g)
    m_hat = m_new / (1.0 - ADAM_B1 ** ADAM_STEP)
    v_hat = v_new / (1.0 - ADAM_B2 ** ADAM_STEP)
    delta = -ADAM_LR * (m_hat / (jnp.sqrt(v_hat) + ADAM_EPS) + ADAM_WD * w)
    return delta, m_new, v_new


def _adam_big(name, own, own_idx, recv, w, m, v):
    rw, cw = w.shape
    rp = own.shape[1]
    steps = 8
    by_cols = rp != rw
    blk_w = (rw, cw // steps) if by_cols else (rw // steps, cw)
    blk_g = (rp, cw // steps) if by_cols else (rw // steps, cw)
    at = (lambda i: (0, i)) if by_cols else (lambda i: (i, 0))

    def body(idx_ref, o_ref, r_ref, w_ref, m_ref, v_ref, g_ref, d_ref, mo_ref, vo_ref):
        g = o_ref[...].astype(F32)
        for r in range(recv.shape[0]):
            g = g + r_ref[r].astype(F32)
        g = g[0:blk_w[0], :]
        g_ref[...] = g
        d_ref[...], mo_ref[...], vo_ref[...] = _adam_math(w_ref[...], g, m_ref[...], v_ref[...])

    spec = pl.BlockSpec(blk_w, lambda i, idx_ref: at(i))
    return pl.pallas_call(
        body, name=name,
        grid_spec=pltpu.PrefetchScalarGridSpec(
            num_scalar_prefetch=1, grid=(steps,),
            in_specs=[pl.BlockSpec((None, *blk_g), lambda i, idx_ref: (idx_ref[0], *at(i))),
                      pl.BlockSpec((recv.shape[0], *blk_g), lambda i, idx_ref: (0, *at(i))), spec, spec, spec],
            out_specs=[spec] * 4),
        out_shape=[jax.ShapeDtypeStruct((rw, cw), F32)] * 4,
        compiler_params=_params("parallel"),
    )(own_idx, own, recv, w, m, v)


def _adam_small(small_all, params):
    flat = [a for triple in params for a in triple]
    n_par = len(params)

    def body(s_ref, *refs):
        ins, outs, loss_ref = refs[:3 * n_par], refs[3 * n_par:-1], refs[-1]
        g_slab = s_ref[0]
        for dev in range(1, N_DEV):
            g_slab = g_slab + s_ref[dev]
        loss_ref[...] = g_slab[SMALL_LOSS:SMALL_LOSS + 1, :]
        dev = 4 * lax.axis_index("x") + 2 * lax.axis_index("y") + lax.axis_index("c")
        alpha_full = jnp.concatenate([g_slab[SMALL_W_ALPHA + half * B_GATE_RANK:SMALL_W_ALPHA + (half + 1) * B_GATE_RANK]
                                      for half in range(B_KEY_WIDTH // LANES)], axis=1)
        alpha_mine = pltpu.roll(alpha_full, (B_KEY_WIDTH - dev * SHARD_ALPHA) % B_KEY_WIDTH, 1)[:, 0:SHARD_ALPHA]
        grads = [_take_rows(g_slab, SMALL_G_IN, D_MODEL // LANES), _take_rows(g_slab, SMALL_G_FINAL, D_MODEL // LANES),
                 _take_rows(g_slab, SMALL_G_GLA, B_WIDTH // LANES), _take_rows(g_slab, SMALL_B_ALPHA, B_KEY_WIDTH // LANES),
                 g_slab[SMALL_SINKS:SMALL_SINKS + 1, 0:A_HEADS], alpha_mine]
        for i, g in enumerate(grads):
            w_ref, m_ref, v_ref = ins[3 * i:3 * i + 3]
            delta, m_new, v_new = _adam_math(w_ref[...], g, m_ref[...], v_ref[...])
            outs[4 * i][...] = g
            outs[4 * i + 1][...] = delta
            outs[4 * i + 2][...] = m_new
            outs[4 * i + 3][...] = v_new

    res = pl.pallas_call(
        body, name="adam_small",
        out_shape=[jax.ShapeDtypeStruct(t[0].shape, F32) for t in params for _ in range(4)]
                  + [jax.ShapeDtypeStruct((1, LANES), F32)],
    )(small_all, *flat)
    return [res[4 * i:4 * i + 4] for i in range(n_par)], res[-1]


def _local_step(x, cosf, sinf, loss_target, g_in, wt_sh, wa_pad, b_alpha, sinks, g_gla, out_shards, g_final, chip_order):
    B, S, _ = x.shape
    T = B * S
    x2 = x.reshape(T, D_MODEL)
    tgt2 = loss_target.reshape(T, D_MODEL)
    f, (g_woa, g_wob, g_wo) = _in_proj(x2, cosf, sinf, g_in, wt_sh, wa_pad, b_alpha, out_shards)
    w_o = g_wo.reshape(D_MODEL, D_MODEL)
    sink_row = jnp.repeat(sinks, WINDOW).reshape(1, ATT_ROWS)
    sink_col = sink_row.reshape(ATT_ROWS, 1)
    attn, lse = _attn_fwd(f["qkv"], sink_row, B, S)
    o_gla, st_all = _gla_fwd(f["q"], f["k"], f["cum"], f["vb"], B, S)
    (dxres, dattn, dog, dza, dzb, dga, dgb, dw_o, dw_oa, dw_ob, small_a) = _merge(
        x2, tgt2, attn, f["za"], o_gla, f["zb"], f["ga"], f["gb"], g_woa, g_wob, w_o, g_gla, g_final)
    dq, dkv, dsink = _attn_bwd(f["qkv"], dattn, attn, lse, sink_col, B, S)
    (dqb, dkb, dvb, dla), (rv_o, rv_oa, rv_ob) = _gla_bwd(f["q"], f["k"], f["cum"], f["vb"], dog, st_all, B, S,
                                                        [dw_o, dw_oa, dw_ob])
    parts = dict(dq=dq, dkv=dkv, dza=dza, dqb=dqb, dkb=dkb, dvb=dvb, dzb=dzb, dla=dla, u=f["u"], alr=f["alr"],
                 dga=dga, dgb=dgb)
    dx, dsh, small_c = _in_proj_bwd(x2, dxres, cosf, sinf, g_in, f["wt_pad"], wa_pad, parts)
    small = jnp.concatenate([small_a, dsink, small_c], axis=0)
    own_in, rv_in, small_all = _w_in_grad_rs(f["h"], dsh, chip_order, small)
    return dict(grad_x=dx.reshape(B, S, D_MODEL), own_in=own_in, rv_in=rv_in,
                own_o=dw_o, rv_o=rv_o, own_oa=dw_oa, rv_oa=rv_oa, own_ob=dw_ob, rv_ob=rv_ob, small_all=small_all)


def kernel(x, positions, g_in, w_in, w_alpha_up, b_alpha, attn_sinks, g_gla_norm, w_out_a, w_out_b, w_o, g_final, loss_target, m_g_in, m_w_in, m_w_alpha_up, m_b_alpha, m_attn_sinks, m_g_gla_norm, m_w_out_a, m_w_out_b, m_w_o, m_g_final, v_g_in, v_w_in, v_w_alpha_up, v_b_alpha, v_attn_sinks, v_g_gla_norm, v_w_out_a, v_w_out_b, v_w_o, v_g_final):
    xi, yi, ci = _my_place()
    dev_idx = (4 * xi + 2 * yi + ci).reshape(1).astype(jnp.int32)
    chip = 2 * xi + yi
    chip_order = jnp.stack([chip ^ 3, chip ^ 2, chip ^ 1, chip]).astype(jnp.int32)

    (g_win, g_wa), cosf, sinf = _gather_first(
        [jnp.pad(w_in[0].T.astype(BF16), ((0, SHARD_PAD - SHARD_IN), (0, 0))), w_alpha_up[0].astype(BF16)],
        positions.reshape(-1, 1))
    wt_sh = g_win.reshape(N_DEV * SHARD_PAD, D_MODEL)
    wa_pad = jnp.pad(jnp.concatenate([g_wa[j] for j in range(N_DEV)], axis=1), ((0, RANK_PAD - B_GATE_RANK), (0, 0)))

    r = _local_step(x, cosf, sinf, loss_target, g_in, wt_sh, wa_pad, b_alpha, attn_sinks[0], g_gla_norm,
                    [w_out_a[0].astype(BF16), w_out_b[0].astype(BF16), w_o[0].astype(BF16)],
                    g_final.reshape(1, D_MODEL), chip_order)

    first = jnp.zeros((1,), jnp.int32)
    big = [[a.T for a in _adam_big("adam_w_in", r["own_in"][None], first, r["rv_in"], w_in[0].T, m_w_in[0].T, v_w_in[0].T)],
           _adam_big("adam_w_out_a", r["own_oa"], dev_idx, r["rv_oa"], w_out_a[0], m_w_out_a[0], v_w_out_a[0]),
           _adam_big("adam_w_out_b", r["own_ob"], dev_idx, r["rv_ob"], w_out_b[0], m_w_out_b[0], v_w_out_b[0]),
           _adam_big("adam_w_o", r["own_o"], dev_idx, r["rv_o"], w_o[0], m_w_o[0], v_w_o[0])]
    row = lambda a: a.reshape(1, D_MODEL)
    (s_in, s_final, s_gla, s_ba, s_sinks, s_wa), loss_row = _adam_small(r["small_all"], [
        (g_in, m_g_in, v_g_in), (row(g_final), row(m_g_final), row(v_g_final)),
        (g_gla_norm, m_g_gla_norm, v_g_gla_norm), (b_alpha, m_b_alpha, v_b_alpha),
        (attn_sinks, m_attn_sinks, v_attn_sinks), (w_alpha_up[0], m_w_alpha_up[0], v_w_alpha_up[0])])

    def group(i):
        return (s_in[i], big[0][i][None], s_wa[i][None], s_ba[i], s_sinks[i], s_gla[i], big[1][i][None], big[2][i][None],
                big[3][i][None], s_final[i].reshape(D_MODEL))

    return (loss_row[0, 0], r["grad_x"], *group(0), *group(1), *group(2), *group(3))
```

```python
import functools
import math

import numpy as np
import jax
import jax.numpy as jnp
from jax import lax
from jax.experimental import pallas as pl
from jax.experimental.pallas import tpu as pltpu

F32 = jnp.float32
BF16 = jnp.bfloat16
MESH = pl.DeviceIdType.MESH

D_MODEL = 1024
A_HEADS, A_KV_HEADS, A_HEAD_DIM = 8, 2, 64
A_WIDTH, A_KV_WIDTH = 512, 128
WINDOW = 128
ROPE_THETA = 500000.0
ROPE_DIM = 16
B_HEADS, B_KEY_DIM, B_VAL_DIM = 4, 64, 128
B_KEY_WIDTH, B_WIDTH = 256, 512
B_GATE_RANK = 16
B_GATE_TEMP = 16.0
B_CHUNK = 64
NORM_EPS = 1e-6
NEG_BIG = -1e30
D_IN = 4880
N_DEV = 8
N_CHIPS = 4
ADAM_LR, ADAM_B1, ADAM_B2, ADAM_EPS, ADAM_WD, ADAM_STEP = 0.001, 0.9, 0.999, 1e-08, 0.01, 10

LANES = 128
V7X_VMEM_LIMIT = 56 * 1024 * 1024
V7X_VMEM_LIMIT_MAX = 62 * 1024 * 1024

RANK_PAD = LANES
SEG = {}
_off = 0
for _name, _w in (("qa", 512), ("ka", 128), ("va", 128), ("za", 512), ("qb", 256), ("kb", 256),
                  ("vb", 512), ("zb", 512), ("alr", RANK_PAD), ("ga", 1024), ("gb", 1024)):
    SEG[_name] = (_off, _off + _w)
    _off += _w
D_IN_PAD = _off
ALR_SRC = SEG["alr"][0]
QKV_K, QKV_V, QKV_W = SEG["ka"][0], SEG["va"][0], SEG["va"][1]
ATT_SCALE = A_HEAD_DIM ** -0.5

SHARD_IN = D_IN // N_DEV
SHARD_PAD = 640
SHARD_OUT = D_MODEL // N_DEV
SHARD_ALPHA = B_KEY_WIDTH // N_DEV

SMALL_G_FINAL, SMALL_G_GLA, SMALL_LOSS, SMALL_SINKS, SMALL_G_IN, SMALL_B_ALPHA, SMALL_W_ALPHA = 0, 8, 12, 16, 24, 32, 40
SMALL_ROWS = 72


def _dot(a, b):
    return jnp.dot(a, b, preferred_element_type=F32)


def _dot_nt(a, b):
    return lax.dot_general(a, b, (((1,), (1,)), ((), ())), preferred_element_type=F32)


def _dot_tn(a, b):
    return lax.dot_general(a, b, (((0,), (0,)), ((), ())), preferred_element_type=F32)


def _sigmoid(z):
    return 1.0 / (1.0 + jnp.exp(-z))


def _sigmoid_tanh(z):
    return 0.5 * jnp.tanh(0.5 * z) + 0.5


def _params(*sem):
    return pltpu.CompilerParams(dimension_semantics=sem, vmem_limit_bytes=V7X_VMEM_LIMIT)


def _const_spec(shape):
    nd = len(shape)
    return pl.BlockSpec(shape, lambda *_: (0,) * nd, pipeline_mode=pl.Buffered(1))


def _lane_iota(shape):
    return lax.broadcasted_iota(jnp.int32, shape, 1)


def _row_iota(shape):
    return lax.broadcasted_iota(jnp.int32, shape, 0)


def _split3(v):
    hi = v.astype(BF16)
    r1 = v - hi.astype(F32)
    mid = r1.astype(BF16)
    lo = (r1 - mid.astype(F32)).astype(BF16)
    return hi, mid, lo


def _put_rows(ref, row0, vec):
    for r in range(vec.shape[1] // LANES):
        ref[row0 + r:row0 + r + 1, :] = vec[:, r * LANES:(r + 1) * LANES]


def _take_rows(slab, row0, n):
    return jnp.concatenate([slab[row0 + r:row0 + r + 1, :] for r in range(n)], axis=1)


def _rope_lane_constants():
    half = ROPE_DIM // 2
    inv_freq = np.exp(-math.log(ROPE_THETA) * np.arange(half, dtype=np.float32) * np.float32(2.0 / ROPE_DIM)).astype(np.float32)
    lane = np.arange(LANES)
    j = lane % A_HEAD_DIM
    invf = np.where(j < ROPE_DIM, inv_freq[j % half], 0.0).astype(np.float32)
    sign = np.where(j < half, -1.0, np.where(j < ROPE_DIM, 1.0, 0.0)).astype(np.float32)
    return jnp.asarray(invf)[None, :], jnp.asarray(sign)[None, :]


def _rope_slab(t, cos, sin_signed):
    first = (_lane_iota(t.shape) % A_HEAD_DIM) < (ROPE_DIM // 2)
    partner = jnp.where(first, pltpu.roll(t, LANES - ROPE_DIM // 2, 1), pltpu.roll(t, ROPE_DIM // 2, 1))
    return t * cos + partner * sin_signed


def _shard_pad_cols(j):
    cut = ALR_SRC + B_GATE_RANK
    shift = RANK_PAD - B_GATE_RANK
    a, b = j * SHARD_IN, (j + 1) * SHARD_IN
    if b <= cut:
        return [(a, b)]
    if a >= cut:
        return [(a + shift, b + shift)]
    return [(a, cut), (cut + shift, b + shift)]


def _in_proj(x2, cosf, sinf, g_in, wt_sh, wa_pad, b_alpha, later_shards):
    T = x2.shape[0]
    tm = math.gcd(T, 512)
    sub = math.gcd(tm, 256)
    last = T // tm - 1
    nl = len(later_shards)

    def body(x_ref, cos_ref, sin_ref, g_ref, wsh_ref, wa_ref, ba_ref, *rest):
        sh_refs, rest = rest[:nl], rest[nl:]
        (h_ref, qkv_ref, za_ref, q_ref, k_ref, vb_ref, zb_ref, alr_ref, u_ref, cum_ref, ga_ref, gb_ref, wt_out) = rest[:13]
        all_refs, (wt_ref, send_sems, recv_sems, local_sems, wt_sem) = rest[13:13 + nl], rest[13 + nl:]
        wt_copy = pltpu.make_async_copy(wt_ref, wt_out, wt_sem)
        px, py, pc = _my_place()
        my_dev = 4 * px + 2 * py + pc

        def wcopy(a, r, slot):
            dx, dy, dc = FLIPS[r]
            return pltpu.make_async_remote_copy(
                src_ref=sh_refs[a], dst_ref=all_refs[a].at[slot], send_sem=send_sems.at[a, r],
                recv_sem=recv_sems.at[a, r], device_id=(px ^ dx, py ^ dy, pc ^ dc), device_id_type=MESH)

        keep = [pltpu.make_async_copy(sh_refs[a], all_refs[a].at[my_dev], local_sems.at[a]) for a in range(nl)]

        @pl.when(pl.program_id(0) == 0)
        def _():
            for a in range(nl):
                keep[a].start()
                for r in range(len(FLIPS)):
                    wcopy(a, r, my_dev).start()

        @pl.when(pl.program_id(0) == 0)
        def _():
            for j in range(N_DEV):
                src = j * SHARD_PAD
                for a, b in _shard_pad_cols(j):
                    wt_ref[a:b, :] = wsh_ref[src:src + b - a, :]
                    src += b - a
            a, b = SEG["alr"]
            wt_ref[a + B_GATE_RANK:b, :] = jnp.zeros((RANK_PAD - B_GATE_RANK, D_MODEL), BF16)
            wt_copy.start()

        def one_tile(rows):
            x = x_ref[rows, :]
            r = lax.rsqrt(jnp.mean(x * x, axis=-1, keepdims=True) + NORM_EPS)
            h = (x * r * g_ref[...]).astype(BF16)
            h_ref[rows, :] = h

            def seg(name):
                a, b = SEG[name]
                return _dot_nt(h, wt_ref[a:b, :])

            alr = seg("alr").astype(BF16)
            alr_ref[rows, :] = alr
            u = _dot(alr, wa_ref[...]) + ba_ref[...]
            u_ref[rows, :] = u
            log_a = (jnp.minimum(u, 0.0) - jnp.log(1.0 + jnp.exp(-jnp.abs(u)))) * (1.0 / B_GATE_TEMP)
            row, col = _row_iota((sub, sub)), _lane_iota((sub, sub))
            tri = ((row // B_CHUNK == col // B_CHUNK) & (col <= row)).astype(BF16)
            hi, mid, lo = _split3(log_a)
            cum_ref[rows, :] = _dot(tri, hi) + _dot(tri, mid) + _dot(tri, lo)

            cos, sin = cos_ref[rows, :], sin_ref[rows, :]
            qa = seg("qa") * ATT_SCALE
            for s in range(A_WIDTH // LANES):
                qkv_ref[rows, s * LANES:(s + 1) * LANES] = _rope_slab(qa[:, s * LANES:(s + 1) * LANES], cos, sin).astype(BF16)
            qkv_ref[rows, QKV_K:QKV_V] = _rope_slab(seg("ka"), cos, sin).astype(BF16)
            qkv_ref[rows, QKV_V:QKV_W] = seg("va").astype(BF16)
            za_ref[rows, :] = seg("za").astype(BF16)
            q_ref[rows, :] = seg("qb")
            k_ref[rows, :] = seg("kb")
            vb_ref[rows, :] = seg("vb").astype(BF16)
            zb_ref[rows, :] = seg("zb").astype(BF16)
            ga_ref[rows, :] = seg("ga").astype(BF16)
            gb_ref[rows, :] = seg("gb").astype(BF16)

        for j in range(tm // sub):
            one_tile(pl.ds(j * sub, sub))

        @pl.when(pl.program_id(0) == last)
        def _():
            for a in range(nl):
                for r, (dx, dy, dc) in enumerate(FLIPS):
                    wcopy(a, r, 4 * (px ^ dx) + 2 * (py ^ dy) + (pc ^ dc)).wait_recv()
                    wcopy(a, r, my_dev).wait_send()
                keep[a].wait()
            wt_copy.wait()

    def rows(w):
        return pl.BlockSpec((tm, w), lambda i: (i, 0))

    outs = [("h", D_MODEL, BF16), ("qkv", QKV_W, BF16), ("za", A_WIDTH, BF16), ("q", B_KEY_WIDTH, F32),
            ("k", B_KEY_WIDTH, F32), ("vb", B_WIDTH, BF16), ("zb", B_WIDTH, BF16), ("alr", RANK_PAD, BF16),
            ("u", B_KEY_WIDTH, F32), ("cum", B_KEY_WIDTH, F32), ("ga", D_MODEL, BF16), ("gb", D_MODEL, BF16)]
    res = pl.pallas_call(
        body, name="in_proj", grid=(T // tm,),
        in_specs=[rows(D_MODEL), rows(LANES), rows(LANES), _const_spec((1, D_MODEL)),
                  _const_spec((N_DEV * SHARD_PAD, D_MODEL)), _const_spec((RANK_PAD, B_KEY_WIDTH)),
                  _const_spec((1, B_KEY_WIDTH))] + _any_specs(nl),
        out_specs=[rows(w) for _, w, _ in outs] + _any_specs(1 + nl),
        out_shape=[jax.ShapeDtypeStruct((T, w), dt) for _, w, dt in outs]
                  + [jax.ShapeDtypeStruct((D_IN_PAD, D_MODEL), BF16)]
                  + [jax.ShapeDtypeStruct((N_DEV, *sh.shape), sh.dtype) for sh in later_shards],
        scratch_shapes=[pltpu.VMEM((D_IN_PAD, D_MODEL), BF16),
                        pltpu.SemaphoreType.DMA((nl, len(FLIPS))), pltpu.SemaphoreType.DMA((nl, len(FLIPS))),
                        pltpu.SemaphoreType.DMA((nl,)), pltpu.SemaphoreType.DMA],
        compiler_params=_params("arbitrary"),
    )(x2, cosf, sinf, g_in, wt_sh, wa_pad, b_alpha, *later_shards)
    n_out = len(outs) + 1
    return dict(zip([n for n, _, _ in outs] + ["wt_pad"], res[:n_out])), res[n_out:]


def _dup_kv_head(t, g):
    tf = t.astype(F32)
    keep = (_lane_iota(tf.shape) < A_HEAD_DIM) == (g == 0)
    return jnp.where(keep, tf, pltpu.roll(tf, A_HEAD_DIM, 1)).astype(BF16)


def _stack_heads(t):
    lo = _lane_iota(t.shape) < A_HEAD_DIM
    zero = jnp.zeros_like(t)
    return jnp.concatenate([jnp.where(lo, t, zero), jnp.where(lo, zero, t)], axis=0)


ATT_ROWS = A_HEADS * WINDOW
GROUP_ROWS = ATT_ROWS // A_KV_HEADS
HEADS_PER_GROUP = A_HEADS // A_KV_HEADS


def _band_mask_t(n):
    kj = _row_iota((2 * WINDOW, GROUP_ROWS)) - WINDOW
    qi = _lane_iota((2 * WINDOW, GROUP_ROWS)) % WINDOW
    return (kj <= qi) & (qi - kj < WINDOW) & ((n > 0) | (kj >= 0))


def _stacked_queries(ref, g):
    pairs = range(g * HEADS_PER_GROUP // 2, (g + 1) * HEADS_PER_GROUP // 2)
    return jnp.concatenate([_stack_heads(ref[:, p * LANES:(p + 1) * LANES]) for p in pairs], axis=0)


def _unstack_heads(t, g, ref, dtype):
    lo = _lane_iota((WINDOW, LANES)) < A_HEAD_DIM
    for hh in range(HEADS_PER_GROUP // 2):
        p = g * HEADS_PER_GROUP // 2 + hh
        ref[:, p * LANES:(p + 1) * LANES] = jnp.where(lo, t[2 * hh * WINDOW:(2 * hh + 1) * WINDOW],
                                                       t[(2 * hh + 1) * WINDOW:(2 * hh + 2) * WINDOW]).astype(dtype)


FWD_BLOCKS = 8


def _attn_fwd(qkv, sink_row, B, S):
    T = B * S
    nb = S // WINDOW
    blocks = math.gcd(nb, FWD_BLOCKS)
    steps = nb // blocks

    def one_block(has_prev, sink_ref, q, k, v, o_ref, lse_ref):
        valid = _band_mask_t(has_prev)
        lse_rows = []
        for g in range(A_KV_HEADS):
            kd, vd = _dup_kv_head(k, g), _dup_kv_head(v, g)
            s = jnp.where(valid, _dot_nt(kd, _stacked_queries(q, g)), NEG_BIG)
            sink = sink_ref[:, g * GROUP_ROWS:(g + 1) * GROUP_ROWS]
            m = jnp.maximum(jnp.max(s, axis=0, keepdims=True), sink)
            e = jnp.exp(s - m)
            den = jnp.sum(e, axis=0, keepdims=True) + jnp.exp(sink - m)
            o = _dot_tn((e * (1.0 / den)).astype(BF16), vd)
            _unstack_heads(o, g, o_ref, F32)
            lse = m + jnp.log(den)
            lse_rows += [lse[:, j * WINDOW:(j + 1) * WINDOW] for j in range(HEADS_PER_GROUP)]
        by_head = jnp.concatenate(lse_rows + [jnp.zeros((WINDOW - A_HEADS, WINDOW), F32)], axis=0)
        lse_ref[...] = by_head.T

    def body(sink_ref, q_ref, kc_ref, vc_ref, kp_ref, vp_ref, o_ref, lse_ref):
        k_all = jnp.concatenate([kp_ref[...], kc_ref[...]], axis=0)
        v_all = jnp.concatenate([vp_ref[...], vc_ref[...]], axis=0)
        for j in range(blocks):
            rows = pl.ds(j * WINDOW, WINDOW)
            keys = slice(j * WINDOW, (j + 2) * WINDOW)
            has_prev = pl.program_id(1) if j == 0 else 1
            one_block(has_prev, sink_ref, q_ref[rows, :], k_all[keys], v_all[keys], o_ref.at[rows], lse_ref.at[rows])

    def cur(col, w):
        return pl.BlockSpec((blocks * WINDOW, w), lambda b, n: (b * steps + n, col))

    def prev(col):
        return pl.BlockSpec((WINDOW, LANES), lambda b, n: (b * nb + jnp.maximum(blocks * n - 1, 0), col))

    kcol, vcol = QKV_K // LANES, QKV_V // LANES
    return pl.pallas_call(
        body, name="attn_fwd", grid=(B, steps),
        in_specs=[_const_spec((1, ATT_ROWS)), cur(0, A_WIDTH), cur(kcol, LANES), cur(vcol, LANES), prev(kcol), prev(vcol)],
        out_specs=[cur(0, A_WIDTH), cur(0, LANES)],
        out_shape=[jax.ShapeDtypeStruct((T, A_WIDTH), F32), jax.ShapeDtypeStruct((T, LANES), F32)],
        compiler_params=_params("parallel", "parallel"),
    )(sink_row, qkv, qkv, qkv, qkv, qkv)


ATT_CHUNK = 64


def _chunk_masks(n):
    masks = []
    for half in range(WINDOW // ATT_CHUNK):
        qi = _row_iota((ATT_CHUNK, 2 * WINDOW)) + half * ATT_CHUNK
        kj = _lane_iota((ATT_CHUNK, 2 * WINDOW)) - WINDOW
        masks.append((kj <= qi) & (qi - kj < WINDOW) & ((n > 0) | (kj >= 0)))
    return masks


def _all_stacked_queries(ref):
    return jnp.concatenate([_stacked_queries(ref, g) for g in range(A_KV_HEADS)], axis=0)


def _by_group(fn, lhs, rhs_per_group):
    return jnp.concatenate([fn(lhs[g * GROUP_ROWS:(g + 1) * GROUP_ROWS], rhs_per_group[g])
                            for g in range(A_KV_HEADS)], axis=0)


BWD_BLOCKS = 8


def _attn_bwd(qkv, do, out, lse, sink_col, B, S):
    T = B * S
    nb = S // WINDOW
    M = math.gcd(nb, BWD_BLOCKS)
    steps = nb // M
    n_chunks = ATT_ROWS // ATT_CHUNK
    halves = WINDOW // ATT_CHUNK

    def block(has_prev, sink_ref, q, do_b, out_b, lse_b, k, v, scratch, want_dq):
        s_ref, dp_ref, ds_ref, p_ref = scratch
        width = k.shape[0]
        masks = [mk[:, 0:width] for mk in _chunk_masks(has_prev)]
        kd = [_dup_kv_head(k, g) for g in range(A_KV_HEADS)]
        vd = [_dup_kv_head(v, g) for g in range(A_KV_HEADS)]
        qs, dos = _all_stacked_queries(q), _all_stacked_queries(do_b)
        s_ref[...] = _by_group(_dot_nt, qs, kd)
        dp_ref[...] = _by_group(_dot_nt, dos, vd)
        lane = _lane_iota((ATT_CHUNK, LANES))
        lo = lane < A_HEAD_DIM
        lane1 = _lane_iota((1, LANES))
        dsink_row = jnp.zeros((1, LANES), F32)
        for c in range(n_chunks):
            rows = slice(c * ATT_CHUNK, (c + 1) * ATT_CHUNK)
            head, half = divmod(c, halves)
            qrows = slice(half * ATT_CHUNK, (half + 1) * ATT_CHUNK)
            slab = slice((head // 2) * LANES, (head // 2 + 1) * LANES)
            lse_col = jnp.sum(jnp.where(lane == head, lse_b[qrows, :], 0.0), axis=-1, keepdims=True)
            prod = do_b[qrows, slab].astype(F32) * out_b[qrows, slab].astype(F32)
            mine = lo if head % 2 == 0 else jnp.logical_not(lo)
            delta = jnp.sum(jnp.where(mine, prod, 0.0), axis=-1, keepdims=True)
            prob = jnp.exp(jnp.where(masks[half], s_ref[rows, :], NEG_BIG) - lse_col)
            p_ref[rows, :] = prob.astype(BF16)
            ds_ref[rows, :] = (prob * (dp_ref[rows, :] - delta)).astype(BF16)
            w = -jnp.exp(sink_ref[rows, :] - lse_col) * delta
            dsink_row += jnp.where(lane1 == head, jnp.sum(w, axis=0, keepdims=True), 0.0)
        dq = _by_group(_dot, ds_ref[...], kd) * ATT_SCALE if want_dq else None
        groups = [slice(g * GROUP_ROWS, (g + 1) * GROUP_ROWS) for g in range(A_KV_HEADS)]
        dk = [_dot_tn(ds_ref[rows, :], qs[rows]) for rows in groups]
        dv = [_dot_tn(p_ref[rows, :], dos[rows]) for rows in groups]
        return dq, dk, dv, dsink_row

    def fold(per_group):
        lane = _lane_iota((WINDOW, LANES))
        out = jnp.zeros((WINDOW, LANES), F32)
        for g, acc in enumerate(per_group):
            out = jnp.where((lane < A_HEAD_DIM) == (g == 0), acc + pltpu.roll(acc, A_HEAD_DIM, 1), out)
        return out

    def body(sink_ref, q_ref, qn_ref, do_ref, don_ref, out_ref, outn_ref, lse_ref, lsen_ref, kc_ref, kp_ref, vc_ref, vp_ref,
             dq_ref, dkv_ref, dsink_ref, s_scr, dp_scr, ds_scr, p_scr, s_x, dp_x, ds_x, p_x):
        b, m = pl.program_id(0), pl.program_id(1)

        @pl.when((b == 0) & (m == 0))
        def _():
            dsink_ref[...] = jnp.zeros_like(dsink_ref)

        k_all = jnp.concatenate([kp_ref[...], kc_ref[...]], axis=0)
        v_all = jnp.concatenate([vp_ref[...], vc_ref[...]], axis=0)
        results = []
        for j in range(M):
            rows = slice(j * WINDOW, (j + 1) * WINDOW)
            keys = slice(j * WINDOW, (j + 2) * WINDOW)
            has_prev = m if j == 0 else 1
            results.append(block(has_prev, sink_ref, q_ref[rows, :], do_ref[rows, :], out_ref[rows, :], lse_ref[rows, :],
                                 k_all[keys], v_all[keys], (s_scr.at[j], dp_scr.at[j], ds_scr.at[j], p_scr.at[j]), True))
        last_keys = slice(M * WINDOW, (M + 1) * WINDOW)
        _, dk_x, dv_x, _ = block(1, sink_ref, qn_ref[...], don_ref[...], outn_ref[...], lsen_ref[...],
                                 k_all[last_keys], v_all[last_keys], (s_x, dp_x, ds_x, p_x), False)
        has_next = m < steps - 1
        lo_q = _lane_iota((WINDOW, LANES)) < A_HEAD_DIM
        dsink_row = jnp.zeros((1, LANES), F32)
        for j, (dq, dk, dv, ds_row) in enumerate(results):
            rows = slice(j * WINDOW, (j + 1) * WINDOW)
            for p in range(A_HEADS // 2):
                dq_ref[rows, p * LANES:(p + 1) * LANES] = jnp.where(
                    lo_q, dq[2 * p * WINDOW:(2 * p + 1) * WINDOW], dq[(2 * p + 1) * WINDOW:(2 * p + 2) * WINDOW]).astype(BF16)
            if j + 1 < M:
                dk_next = [t[0:WINDOW] for t in results[j + 1][1]]
                dv_next = [t[0:WINDOW] for t in results[j + 1][2]]
            else:
                dk_next = [jnp.where(has_next, t, 0.0) for t in dk_x]
                dv_next = [jnp.where(has_next, t, 0.0) for t in dv_x]
            dkv_ref[rows, 0:LANES] = fold([own[WINDOW:] + nxt for own, nxt in zip(dk, dk_next)]).astype(BF16)
            dkv_ref[rows, LANES:] = fold([own[WINDOW:] + nxt for own, nxt in zip(dv, dv_next)]).astype(BF16)
            dsink_row += ds_row
        dsink_ref[0:1, :] += dsink_row

    def cur(col, w):
        return pl.BlockSpec((M * WINDOW, w), lambda b, m: (b * steps + m, col))

    def nxt(col, w):
        return pl.BlockSpec((WINDOW, w), lambda b, m: (b * nb + jnp.minimum(M * (m + 1), nb - 1), col))

    def prev(col):
        return pl.BlockSpec((WINDOW, LANES), lambda b, m: (b * nb + jnp.maximum(M * m - 1, 0), col))

    kcol, vcol = QKV_K // LANES, QKV_V // LANES
    scores = (M, ATT_ROWS, 2 * WINDOW)
    extra = (ATT_ROWS, WINDOW)
    return pl.pallas_call(
        body, name="attn_bwd", grid=(B, steps),
        in_specs=[_const_spec((ATT_ROWS, 1)), cur(0, A_WIDTH), nxt(0, A_WIDTH), cur(0, A_WIDTH), nxt(0, A_WIDTH),
                  cur(0, A_WIDTH), nxt(0, A_WIDTH), cur(0, LANES), nxt(0, LANES),
                  cur(kcol, LANES), prev(kcol), cur(vcol, LANES), prev(vcol)],
        out_specs=[cur(0, A_WIDTH), cur(0, 2 * LANES), pl.BlockSpec((8, LANES), lambda b, m: (0, 0))],
        out_shape=[jax.ShapeDtypeStruct((T, A_WIDTH), BF16), jax.ShapeDtypeStruct((T, 2 * LANES), BF16),
                   jax.ShapeDtypeStruct((8, LANES), F32)],
        scratch_shapes=[pltpu.VMEM(scores, F32), pltpu.VMEM(scores, F32), pltpu.VMEM(scores, BF16), pltpu.VMEM(scores, BF16),
                        pltpu.VMEM(extra, F32), pltpu.VMEM(extra, F32), pltpu.VMEM(extra, BF16), pltpu.VMEM(extra, BF16)],
        compiler_params=_params("arbitrary", "arbitrary"),
    )(sink_col, qkv, qkv, do, do, out, out, lse, lse, qkv, qkv, qkv, qkv)


GLA_TILE = 256
CHUNKS_PER_TILE = GLA_TILE // B_CHUNK
GLA_TILES_PER_STEP = 4


def _gla_factors(q_ref, k_ref, cum_ref):
    scale = B_KEY_DIM ** -0.5
    cum = cum_ref[...]
    shape = (B_CHUNK, B_KEY_WIDTH)
    last = jnp.concatenate([jnp.broadcast_to(cum_ref[pl.ds(c * B_CHUNK + B_CHUNK - 1, 1), :], shape)
                            for c in range(CHUNKS_PER_TILE)], axis=0)
    mid = jnp.concatenate([jnp.broadcast_to(cum_ref[pl.ds(c * B_CHUNK + B_CHUNK // 2 - 1, 1), :], shape)
                           for c in range(CHUNKS_PER_TILE)], axis=0)
    e_qm, e_km, e_qe, e_kd = jnp.exp(cum - mid), jnp.exp(mid - cum), jnp.exp(cum), jnp.exp(last - cum)
    qs = q_ref[...] * scale
    k = k_ref[...]
    return qs, k, (e_qm, e_km, e_qe, e_kd)


def _head_mask(shape, h):
    return (_lane_iota(shape) // B_KEY_DIM) == h


def _stack_masked(t):
    return jnp.concatenate([jnp.where(_head_mask(t.shape, h), t, 0.0) for h in range(B_HEADS)], axis=0).astype(BF16)


def _select_heads(t):
    shape = (B_CHUNK, B_KEY_WIDTH)
    out = jnp.zeros(shape, F32)
    for h in range(B_HEADS):
        out = jnp.where(_head_mask(shape, h), t[h * B_CHUNK:(h + 1) * B_CHUNK], out)
    return out


def _select_state(t):
    shape = (B_VAL_DIM, B_KEY_WIDTH)
    out = jnp.zeros(shape, F32)
    for h in range(B_HEADS):
        out = jnp.where(_head_mask(shape, h), t[h * B_VAL_DIM:(h + 1) * B_VAL_DIM], out)
    return out


def _rows_by_head(t):
    return jnp.concatenate([t[:, h * B_VAL_DIM:(h + 1) * B_VAL_DIM] for h in range(B_HEADS)], axis=0)


def _intra_mask():
    i, j = _row_iota((GLA_TILE, GLA_TILE)), _lane_iota((GLA_TILE, GLA_TILE))
    return (i // B_CHUNK == j // B_CHUNK) & (j <= i)


def _pair_stack(t, p):
    slab = t[:, p * LANES:(p + 1) * LANES]
    lo = _lane_iota(slab.shape) < B_KEY_DIM
    return jnp.concatenate([jnp.where(lo, slab, 0.0), jnp.where(lo, 0.0, slab)], axis=0).astype(BF16)


def _gla_fwd(q, k, cum, vb, B, S):
    T = B * S
    nt = S // GLA_TILE
    tps = math.gcd(nt, GLA_TILES_PER_STEP)

    def one_sequence(q_ref, k_ref, cum_ref, v_ref, o_ref, st_all_ref, st_ref):
        qs, kk, (e_qm, e_km, e_qe, e_kd) = _gla_factors(q_ref, k_ref, cum_ref)
        qm, km, qe, kd = qs * e_qm, kk * e_km, qs * e_qe, (kk * e_kd).astype(BF16)
        mask = _intra_mask()
        intra = []
        for p in range(B_HEADS // 2):
            a = _dot_nt(_pair_stack(qm, p), km[:, p * LANES:(p + 1) * LANES].astype(BF16))
            for hh in range(2):
                h = 2 * p + hh
                att = jnp.where(mask, a[hh * GLA_TILE:(hh + 1) * GLA_TILE], 0.0).astype(BF16)
                intra.append(_dot(att, v_ref[:, h * B_VAL_DIM:(h + 1) * B_VAL_DIM]))
        inter = []
        for c in range(CHUNKS_PER_TILE):
            rows = slice(c * B_CHUNK, (c + 1) * B_CHUNK)
            st = st_ref[...]
            st_all_ref[c] = st
            inter.append(_dot_nt(_stack_masked(qe[rows]), st.astype(BF16)))
            inc = _select_state(_dot_tn(v_ref[rows, :], kd[rows]))
            decay = jnp.exp(cum_ref[pl.ds(c * B_CHUNK + B_CHUNK - 1, 1), :])
            st_ref[...] = st * decay + inc
        for h in range(B_HEADS):
            oi = jnp.concatenate([inter[c][h * B_CHUNK:(h + 1) * B_CHUNK] for c in range(CHUNKS_PER_TILE)], axis=0)
            o_ref[:, h * B_VAL_DIM:(h + 1) * B_VAL_DIM] = (intra[h] + oi).astype(BF16)

    def body(q_ref, k_ref, cum_ref, v_ref, o_ref, st_all_ref, st_ref):
        @pl.when(pl.program_id(0) == 0)
        def _():
            st_ref[...] = jnp.zeros_like(st_ref)

        for b in range(B):
            for tile in range(tps):
                tok = pl.ds(tile * GLA_TILE, GLA_TILE)
                chunks = pl.ds(tile * CHUNKS_PER_TILE, CHUNKS_PER_TILE)
                one_sequence(*[r.at[b, tok] for r in (q_ref, k_ref, cum_ref, v_ref, o_ref)],
                             st_all_ref.at[b, chunks], st_ref.at[b])

    def rows(w):
        return pl.BlockSpec((B, tps * GLA_TILE, w), lambda t: (0, t, 0))

    seq = lambda a: a.reshape(B, S, a.shape[-1])
    o, st_all = pl.pallas_call(
        body, name="gla_fwd", grid=(nt // tps,),
        in_specs=[rows(B_KEY_WIDTH), rows(B_KEY_WIDTH), rows(B_KEY_WIDTH), rows(B_WIDTH)],
        out_specs=[rows(B_WIDTH),
                   pl.BlockSpec((B, tps * CHUNKS_PER_TILE, B_VAL_DIM, B_KEY_WIDTH), lambda t: (0, t, 0, 0))],
        out_shape=[jax.ShapeDtypeStruct((B, S, B_WIDTH), BF16),
                   jax.ShapeDtypeStruct((B, S // B_CHUNK, B_VAL_DIM, B_KEY_WIDTH), F32)],
        scratch_shapes=[pltpu.VMEM((B, B_VAL_DIM, B_KEY_WIDTH), F32)],
        compiler_params=_params("arbitrary"),
    )(seq(q), seq(k), seq(cum), seq(vb))
    return o.reshape(T, B_WIDTH), st_all.reshape(T // B_CHUNK, B_VAL_DIM, B_KEY_WIDTH)


def _gla_bwd(q, k, cum, vb, do, st_all, B, S, wgrads):
    T = B * S
    nt = S // GLA_TILE
    tps = math.gcd(nt, GLA_TILES_PER_STEP)
    steps = nt // tps
    scale = B_KEY_DIM ** -0.5
    nw = len(wgrads)

    def one_sequence(q_ref, k_ref, cum_ref, v_ref, do_ref, st_all_ref, dq_ref, dk_ref, dv_ref, dla_ref, dst_ref):
        qs, kk, (e_qm, e_km, e_qe, e_kd) = _gla_factors(q_ref, k_ref, cum_ref)
        qm, km, qe, kd = qs * e_qm, kk * e_km, qs * e_qe, kk * e_kd
        mask = _intra_mask()
        dqm_slabs, dkm_slabs, dv_intra = [], [], []
        for p in range(B_HEADS // 2):
            qm_st = _pair_stack(qm, p)
            km_p = km[:, p * LANES:(p + 1) * LANES].astype(BF16)
            a = _dot_nt(qm_st, km_p)
            da_blocks, dqm_h = [], []
            for hh in range(2):
                h = 2 * p + hh
                vs = slice(h * B_VAL_DIM, (h + 1) * B_VAL_DIM)
                att = jnp.where(mask, a[hh * GLA_TILE:(hh + 1) * GLA_TILE], 0.0).astype(BF16)
                dv_intra.append(_dot_tn(att, do_ref[:, vs]))
                da = jnp.where(mask, _dot_nt(do_ref[:, vs], v_ref[:, vs]), 0.0).astype(BF16)
                da_blocks.append(da)
                dqm_h.append(_dot(da, km_p))
            lo = _lane_iota((GLA_TILE, LANES)) < B_KEY_DIM
            dqm_slabs.append(jnp.where(lo, dqm_h[0], dqm_h[1]))
            dkm_slabs.append(_dot_tn(jnp.concatenate(da_blocks, axis=0), qm_st))
        dqm = jnp.concatenate(dqm_slabs, axis=1)
        dkm = jnp.concatenate(dkm_slabs, axis=1)

        dqe_c, dkd_c, dv_inter, tail_c = ([None] * CHUNKS_PER_TILE for _ in range(4))
        for c in reversed(range(CHUNKS_PER_TILE)):
            rows = slice(c * B_CHUNK, (c + 1) * B_CHUNK)
            dst = dst_ref[...]
            dst_b = dst.astype(BF16)
            dv_inter[c] = _dot_nt(_stack_masked(kd[rows]), dst_b)
            dkd_c[c] = _select_heads(_dot(_rows_by_head(v_ref[rows, :]), dst_b))
            do_c = do_ref[rows, :]
            dqe_c[c] = _select_heads(_dot(_rows_by_head(do_c), st_all_ref[c].astype(BF16)))
            contrib = _select_state(_dot_tn(do_c, qe[rows].astype(BF16)))
            decay = jnp.exp(cum_ref[pl.ds(c * B_CHUNK + B_CHUNK - 1, 1), :])
            tail = (jnp.sum(kk[rows] * dkd_c[c] * e_kd[rows], axis=0, keepdims=True)
                    + decay * jnp.sum(st_all_ref[c] * dst, axis=0, keepdims=True))
            tail_c[c] = jnp.broadcast_to(tail, (B_CHUNK, B_KEY_WIDTH))
            dst_ref[...] = dst * decay + contrib
        dqe = jnp.concatenate(dqe_c, axis=0)
        dkd = jnp.concatenate(dkd_c, axis=0)
        dqs = dqm * e_qm + dqe * e_qe
        dk = dkm * e_km + dkd * e_kd
        dq_ref[...] = (dqs * scale).astype(BF16)
        dk_ref[...] = dk.astype(BF16)
        for h in range(B_HEADS):
            dvi = jnp.concatenate([dv_inter[c][h * B_CHUNK:(h + 1) * B_CHUNK] for c in range(CHUNKS_PER_TILE)], axis=0)
            dv_ref[:, h * B_VAL_DIM:(h + 1) * B_VAL_DIM] = (dv_intra[h] + dvi).astype(BF16)
        dd = qs * dqs - kk * dk
        i, j = _row_iota((GLA_TILE, GLA_TILE)), _lane_iota((GLA_TILE, GLA_TILE))
        upper = ((i // B_CHUNK == j // B_CHUNK) & (j >= i)).astype(BF16)
        hi, mid, lo3 = _split3(dd)
        dla_ref[...] = _dot(upper, hi) + _dot(upper, mid) + _dot(upper, lo3) + jnp.concatenate(tail_c, axis=0)

    def body(q_ref, k_ref, cum_ref, v_ref, do_ref, st_all_ref, *rest):
        g_refs, (dq_ref, dk_ref, dv_ref, dla_ref) = rest[:nw], rest[nw:nw + 4]
        rv_refs, (dst_ref, send_sems, recv_sems) = rest[nw + 4:2 * nw + 4], rest[2 * nw + 4:]
        x, y, c = _my_place()

        def wcopy(a, r):
            dx, dy, dc = FLIPS[r]
            return pltpu.make_async_remote_copy(
                src_ref=g_refs[a].at[4 * (x ^ dx) + 2 * (y ^ dy) + (c ^ dc)], dst_ref=rv_refs[a].at[r],
                send_sem=send_sems.at[a, r], recv_sem=recv_sems.at[a, r],
                device_id=(x ^ dx, y ^ dy, c ^ dc), device_id_type=MESH)

        @pl.when(pl.program_id(0) == 0)
        def _():
            dst_ref[...] = jnp.zeros_like(dst_ref)
            for a in range(nw):
                for r in range(len(FLIPS)):
                    wcopy(a, r).start()

        for b in range(B):
            for tile in reversed(range(tps)):
                tok = pl.ds(tile * GLA_TILE, GLA_TILE)
                chunks = pl.ds(tile * CHUNKS_PER_TILE, CHUNKS_PER_TILE)
                one_sequence(*[r.at[b, tok] for r in (q_ref, k_ref, cum_ref, v_ref, do_ref)], st_all_ref.at[b, chunks],
                             *[r.at[b, tok] for r in (dq_ref, dk_ref, dv_ref, dla_ref)], dst_ref.at[b])

        @pl.when(pl.program_id(0) == steps - 1)
        def _():
            for a in range(nw):
                for r in range(len(FLIPS)):
                    wcopy(a, r).wait()

    def rows(w):
        return pl.BlockSpec((B, tps * GLA_TILE, w), lambda t: (0, steps - 1 - t, 0))

    seq = lambda a: a.reshape(B, S, a.shape[-1])
    res = pl.pallas_call(
        body, name="gla_bwd", grid=(steps,),
        in_specs=[rows(B_KEY_WIDTH), rows(B_KEY_WIDTH), rows(B_KEY_WIDTH), rows(B_WIDTH), rows(B_WIDTH),
                  pl.BlockSpec((B, tps * CHUNKS_PER_TILE, B_VAL_DIM, B_KEY_WIDTH), lambda t: (0, steps - 1 - t, 0, 0))]
                 + _any_specs(nw),
        out_specs=[rows(B_KEY_WIDTH), rows(B_KEY_WIDTH), rows(B_WIDTH), rows(B_KEY_WIDTH)] + _any_specs(nw),
        out_shape=[jax.ShapeDtypeStruct((B, S, B_KEY_WIDTH), BF16), jax.ShapeDtypeStruct((B, S, B_KEY_WIDTH), BF16),
                   jax.ShapeDtypeStruct((B, S, B_WIDTH), BF16), jax.ShapeDtypeStruct((B, S, B_KEY_WIDTH), F32)]
                  + [jax.ShapeDtypeStruct((len(FLIPS), *g.shape[1:]), g.dtype) for g in wgrads],
        scratch_shapes=[pltpu.VMEM((B, B_VAL_DIM, B_KEY_WIDTH), F32),
                        pltpu.SemaphoreType.DMA((nw, len(FLIPS))), pltpu.SemaphoreType.DMA((nw, len(FLIPS)))],
        compiler_params=_params("arbitrary"),
    )(seq(q), seq(k), seq(cum), seq(vb), seq(do), st_all.reshape(B, S // B_CHUNK, B_VAL_DIM, B_KEY_WIDTH), *wgrads)
    return [a.reshape(T, a.shape[-1]) for a in res[:4]], res[4:]


def _merge(x2, tgt2, attn, za, o_gla, zb, ga, gb, w_oa_sh, w_ob_sh, w_o, g_gla, g_final):
    T = x2.shape[0]
    tm = math.gcd(T, 512)
    sub = math.gcd(tm, 256)
    last = T // tm - 1

    def body(x_ref, tgt_ref, attn_ref, za_ref, og_ref, zb_ref, ga_ref, gb_ref,
             woa_sh_ref, wob_sh_ref, wo_ref, gg_ref, gf_ref,
             dxres_ref, dattn_ref, dog_ref, dza_ref, dzb_ref, dga_ref, dgb_ref,
             dwo_out, dwoa_out, dwob_out, small_ref,
             awo_ref, awoa_ref, awob_ref, agf_ref, agg_ref, loss_ref, woa_ref, wob_ref,
             dwo_ref, dwoa_ref, dwob_ref, w_sems, dw_sems):
        w_copies = [pltpu.make_async_copy(sh.at[j], dst.at[:, j * SHARD_OUT:(j + 1) * SHARD_OUT], w_sems.at[a, j])
                    for a, (sh, dst) in enumerate(((woa_sh_ref, woa_ref), (wob_sh_ref, wob_ref))) for j in range(N_DEV)]
        dw_copies = [pltpu.make_async_copy(src, dst, dw_sems.at[a])
                     for a, (src, dst) in enumerate(((dwo_ref, dwo_out), (dwoa_ref, dwoa_out), (dwob_ref, dwob_out)))]

        @pl.when(pl.program_id(0) == 0)
        def _():
            for cp in w_copies:
                cp.start()
            for r in (awo_ref, awoa_ref, awob_ref, agf_ref, agg_ref, loss_ref):
                r[...] = jnp.zeros_like(r)
            for cp in w_copies:
                cp.wait()

        def one_tile(rows):
            za_v = za_ref[rows, :].astype(F32)
            sig_za = _sigmoid_tanh(za_v)
            silu_a = za_v * sig_za
            attn_v = attn_ref[rows, :].astype(F32)
            oa = (attn_v * silu_a).astype(BF16)
            ya = _dot(oa, woa_ref[...])
            og = og_ref[rows, :].astype(F32)
            zb_v = zb_ref[rows, :].astype(F32)
            sig_zb = _sigmoid_tanh(zb_v)
            silu_b = zb_v * sig_zb
            gg = gg_ref[...]
            on_parts, rinv_parts = [], []
            for h in range(B_HEADS):
                seg = og[:, h * B_VAL_DIM:(h + 1) * B_VAL_DIM]
                rinv = lax.rsqrt(jnp.mean(seg * seg, axis=-1, keepdims=True) + NORM_EPS)
                rinv_parts.append(rinv)
                on_parts.append(seg * rinv)
            on = jnp.concatenate(on_parts, axis=1)
            obn = on * gg
            ob = (obn * silu_b).astype(BF16)
            yb = _dot(ob, wob_ref[...])
            sig_a = _sigmoid_tanh(ga_ref[rows, :].astype(F32))
            sig_b = _sigmoid_tanh(gb_ref[rows, :].astype(F32))
            merged = (sig_a * ya + sig_b * yb).astype(BF16)
            out = x_ref[rows, :] + _dot(merged, wo_ref[...])
            rf = lax.rsqrt(jnp.mean(out * out, axis=-1, keepdims=True) + NORM_EPS)
            nrm = out * rf
            gf = gf_ref[...]
            err = nrm * gf - tgt_ref[rows, :]
            loss = jnp.sum(err * err) * (0.5 / D_MODEL)

            dy = err * (1.0 / D_MODEL)
            dgf = jnp.sum(dy * nrm, axis=0, keepdims=True)
            dn = dy * gf
            dout = rf * (dn - nrm * jnp.mean(dn * nrm, axis=-1, keepdims=True))
            dxres_ref[rows, :] = dout
            dout_b = dout.astype(BF16)
            dmerged = _dot_nt(dout_b, wo_ref[...])
            dya = dmerged * sig_a
            dyb = dmerged * sig_b
            dga_ref[rows, :] = (dmerged * ya * sig_a * (1.0 - sig_a)).astype(BF16)
            dgb_ref[rows, :] = (dmerged * yb * sig_b * (1.0 - sig_b)).astype(BF16)
            dya_b, dyb_b = dya.astype(BF16), dyb.astype(BF16)
            doa = _dot_nt(dya_b, woa_ref[...])
            dattn_ref[rows, :] = (doa * silu_a).astype(BF16)
            dza_ref[rows, :] = (doa * attn_v * (sig_za * (1.0 + za_v * (1.0 - sig_za)))).astype(BF16)
            dob = _dot_nt(dyb_b, wob_ref[...])
            dzb_ref[rows, :] = (dob * obn * (sig_zb * (1.0 + zb_v * (1.0 - sig_zb)))).astype(BF16)
            dobn = dob * silu_b
            dgg = jnp.sum(dobn * on, axis=0, keepdims=True)
            don = dobn * gg
            for h in range(B_HEADS):
                sl = slice(h * B_VAL_DIM, (h + 1) * B_VAL_DIM)
                don_h, on_h = don[:, sl], on[:, sl]
                dog_ref[rows, sl] = (rinv_parts[h] * (don_h - on_h * jnp.mean(don_h * on_h, axis=-1, keepdims=True))
                                     ).astype(BF16)
            return (merged, dout_b, oa, dya_b, ob, dyb_b), (loss, dgf, dgg)

        tiles = [one_tile(pl.ds(j * sub, sub)) for j in range(tm // sub)]
        merged, dout_b, oa, dya_b, ob, dyb_b = (jnp.concatenate(parts, axis=0) for parts in zip(*[t[0] for t in tiles]))
        awo_ref[...] += _dot_tn(merged, dout_b)
        awoa_ref[...] += _dot_tn(oa, dya_b)
        awob_ref[...] += _dot_tn(ob, dyb_b)
        for _, (loss, dgf, dgg) in tiles:
            loss_ref[...] += loss
            agf_ref[...] += dgf
            agg_ref[...] += dgg

        @pl.when(pl.program_id(0) == last)
        def _():
            for j in range(N_DEV):
                dwo_ref[j] = awo_ref[j * SHARD_OUT:(j + 1) * SHARD_OUT, :].astype(BF16)
                dwoa_ref[j] = awoa_ref[:, j * SHARD_OUT:(j + 1) * SHARD_OUT].astype(BF16)
                dwob_ref[j] = awob_ref[:, j * SHARD_OUT:(j + 1) * SHARD_OUT].astype(BF16)
            small_ref[...] = jnp.zeros_like(small_ref)
            _put_rows(small_ref, SMALL_G_FINAL, agf_ref[...])
            _put_rows(small_ref, SMALL_G_GLA, agg_ref[...])
            small_ref[SMALL_LOSS:SMALL_LOSS + 1, :] = loss_ref[...]
            for cp in dw_copies:
                cp.start()
            for cp in dw_copies:
                cp.wait()

    def rows(w):
        return pl.BlockSpec((tm, w), lambda i: (i, 0))

    def whole(shape):
        nd = len(shape)
        return pl.BlockSpec(shape, lambda i: (0,) * nd)

    outs = [((T, D_MODEL), F32, rows(D_MODEL)), ((T, A_WIDTH), BF16, rows(A_WIDTH)), ((T, B_WIDTH), BF16, rows(B_WIDTH)),
            ((T, A_WIDTH), BF16, rows(A_WIDTH)), ((T, B_WIDTH), BF16, rows(B_WIDTH)),
            ((T, D_MODEL), BF16, rows(D_MODEL)), ((T, D_MODEL), BF16, rows(D_MODEL)),
            ((N_DEV, SHARD_OUT, D_MODEL), BF16, pl.BlockSpec(memory_space=pl.ANY)),
            ((N_DEV, A_WIDTH, SHARD_OUT), BF16, pl.BlockSpec(memory_space=pl.ANY)),
            ((N_DEV, B_WIDTH, SHARD_OUT), BF16, pl.BlockSpec(memory_space=pl.ANY)),
            ((SMALL_SINKS, LANES), F32, whole((SMALL_SINKS, LANES)))]
    return pl.pallas_call(
        body, name="merge", grid=(T // tm,),
        in_specs=[rows(D_MODEL), rows(D_MODEL), rows(A_WIDTH), rows(A_WIDTH), rows(B_WIDTH), rows(B_WIDTH),
                  rows(D_MODEL), rows(D_MODEL),
                  pl.BlockSpec(memory_space=pl.ANY), pl.BlockSpec(memory_space=pl.ANY),
                  _const_spec((D_MODEL, D_MODEL)), _const_spec((1, B_WIDTH)), _const_spec((1, D_MODEL))],
        out_specs=[o[2] for o in outs],
        out_shape=[jax.ShapeDtypeStruct(o[0], o[1]) for o in outs],
        scratch_shapes=[pltpu.VMEM((D_MODEL, D_MODEL), F32), pltpu.VMEM((A_WIDTH, D_MODEL), F32),
                        pltpu.VMEM((B_WIDTH, D_MODEL), F32), pltpu.VMEM((1, D_MODEL), F32), pltpu.VMEM((1, B_WIDTH), F32),
                        pltpu.VMEM((1, LANES), F32), pltpu.VMEM((A_WIDTH, D_MODEL), BF16),
                        pltpu.VMEM((B_WIDTH, D_MODEL), BF16),
                        pltpu.VMEM((N_DEV, SHARD_OUT, D_MODEL), BF16), pltpu.VMEM((N_DEV, A_WIDTH, SHARD_OUT), BF16),
                        pltpu.VMEM((N_DEV, B_WIDTH, SHARD_OUT), BF16),
                        pltpu.SemaphoreType.DMA((2, N_DEV)), pltpu.SemaphoreType.DMA((3,))],
        compiler_params=pltpu.CompilerParams(dimension_semantics=("arbitrary",), vmem_limit_bytes=V7X_VMEM_LIMIT_MAX),
    )(x2, tgt2, attn, za, o_gla, zb, ga, gb, w_oa_sh, w_ob_sh, w_o, g_gla, g_final)


def _in_proj_bwd(x2, dxres, cosf, sinf, g_in, wt_pad, wa_pad, parts):
    T = x2.shape[0]
    tm = 256
    last = T // tm - 1
    base = SMALL_G_IN

    def body(x_ref, dxres_ref, cos_ref, sin_ref, g_ref, wt_ref, wa_ref,
             dq_ref, dkv_ref, dza_ref, dqb_ref, dkb_ref, dvb_ref, dzb_ref, dla_ref, u_ref, alr_ref, dga_ref, dgb_ref,
             dx_ref, dsh_ref, small_ref, dproj_ref, agin_ref, aba_ref, awa_ref):
        @pl.when(pl.program_id(0) == 0)
        def _():
            for r in (agin_ref, aba_ref, awa_ref):
                r[...] = jnp.zeros_like(r)

        cos, nsin = cos_ref[...], -sin_ref[...]
        for s in range(A_WIDTH // LANES):
            sl = slice(s * LANES, (s + 1) * LANES)
            dproj_ref[:, sl] = _rope_slab(dq_ref[:, sl].astype(F32), cos, nsin).astype(BF16)
        dproj_ref[:, QKV_K:QKV_V] = _rope_slab(dkv_ref[:, 0:LANES].astype(F32), cos, nsin).astype(BF16)
        dproj_ref[:, QKV_V:QKV_W] = dkv_ref[:, LANES:]

        def put(name, val):
            a, b = SEG[name]
            dproj_ref[:, a:b] = val

        put("za", dza_ref[...])
        put("qb", dqb_ref[...])
        put("kb", dkb_ref[...])
        put("vb", dvb_ref[...])
        put("zb", dzb_ref[...])
        put("ga", dga_ref[...])
        put("gb", dgb_ref[...])
        du = dla_ref[...] * (1.0 / B_GATE_TEMP) * _sigmoid(-u_ref[...])
        aba_ref[...] += jnp.sum(du, axis=0, keepdims=True)
        du_b = du.astype(BF16)
        awa_ref[...] += _dot_tn(alr_ref[...], du_b)
        put("alr", _dot_nt(du_b, wa_ref[...]).astype(BF16))

        for j in range(N_DEV):
            col = (j % 2) * SHARD_PAD
            for a, b in _shard_pad_cols(j):
                dsh_ref[j // 2, :, col:col + b - a] = dproj_ref[:, a:b]
                col += b - a
            dsh_ref[j // 2, :, col:(j % 2 + 1) * SHARD_PAD] = jnp.zeros((tm, SHARD_PAD - SHARD_IN), BF16)

        dh = _dot(dproj_ref[...], wt_ref[...])
        x = x_ref[...]
        r = lax.rsqrt(jnp.mean(x * x, axis=-1, keepdims=True) + NORM_EPS)
        nrm = x * r
        agin_ref[...] += jnp.sum(dh * nrm, axis=0, keepdims=True)
        dn = dh * g_ref[...]
        dx_ref[...] = dxres_ref[...] + r * (dn - nrm * jnp.mean(dn * nrm, axis=-1, keepdims=True))

        @pl.when(pl.program_id(0) == last)
        def _():
            small_ref[...] = jnp.zeros_like(small_ref)
            _put_rows(small_ref, SMALL_G_IN - base, agin_ref[...])
            _put_rows(small_ref, SMALL_B_ALPHA - base, aba_ref[...])
            for half in range(B_KEY_WIDTH // LANES):
                r0 = SMALL_W_ALPHA - base + half * B_GATE_RANK
                small_ref[r0:r0 + B_GATE_RANK, :] = awa_ref[0:B_GATE_RANK, half * LANES:(half + 1) * LANES]

    def rows(w):
        return pl.BlockSpec((tm, w), lambda i: (i, 0))

    names = ["dq", "dkv", "dza", "dqb", "dkb", "dvb", "dzb", "dla", "u", "alr", "dga", "dgb"]
    return pl.pallas_call(
        body, name="in_proj_bwd", grid=(T // tm,),
        in_specs=[rows(D_MODEL), rows(D_MODEL), rows(LANES), rows(LANES), _const_spec((1, D_MODEL)),
                  _const_spec((D_IN_PAD, D_MODEL)), _const_spec((RANK_PAD, B_KEY_WIDTH))]
                 + [rows(parts[n].shape[1]) for n in names],
        out_specs=[rows(D_MODEL), pl.BlockSpec((N_CHIPS, tm, 2 * SHARD_PAD), lambda i: (0, i, 0)),
                   pl.BlockSpec((SMALL_ROWS - base, LANES), lambda i: (0, 0))],
        out_shape=[jax.ShapeDtypeStruct((T, D_MODEL), F32), jax.ShapeDtypeStruct((N_CHIPS, T, 2 * SHARD_PAD), BF16),
                   jax.ShapeDtypeStruct((SMALL_ROWS - base, LANES), F32)],
        scratch_shapes=[pltpu.VMEM((tm, D_IN_PAD), BF16), pltpu.VMEM((1, D_MODEL), F32), pltpu.VMEM((1, B_KEY_WIDTH), F32),
                        pltpu.VMEM((RANK_PAD, B_KEY_WIDTH), F32)],
        compiler_params=_params("arbitrary"),
    )(x2, dxres, cosf, sinf, g_in, wt_pad, wa_pad, *[parts[n] for n in names])


FLIPS = [(dx, dy, dc) for dx in (0, 1) for dy in (0, 1) for dc in (0, 1)][1:]


def _my_place():
    return lax.axis_index("x"), lax.axis_index("y"), lax.axis_index("c")


def _any_specs(n):
    return [pl.BlockSpec(memory_space=pl.ANY)] * n


def _gather_first(shards, pos_col):
    n = len(shards)
    T = pos_col.shape[0]
    rows_per_pass = math.gcd(T, 512)
    invf, sign = _rope_lane_constants()

    def body(*refs):
        ins, (pos_ref, invf_ref, sign_ref) = refs[:n], refs[n:n + 3]
        outs, (cos_ref, sin_ref) = refs[n + 3:2 * n + 3], refs[2 * n + 3:2 * n + 5]
        send_sems, recv_sems, local_sems = refs[2 * n + 5:]
        x, y, c = _my_place()
        me, sibling = (x, y, c), (x, y, 1 - c)
        chips = [(1 - x, y), (x, 1 - y), (1 - x, 1 - y)]

        def block(a, px, py, pc):
            return outs[a].at[4 * px + 2 * py + pc]

        def copy(a, k, blk, to, src=None):
            return pltpu.make_async_remote_copy(
                src_ref=block(a, *blk) if src is None else src, dst_ref=block(a, *blk),
                send_sem=send_sems.at[a, k], recv_sem=recv_sems.at[a, k], device_id=to, device_id_type=MESH)

        mine = [pltpu.make_async_copy(ins[a], block(a, *me), local_sems.at[a]) for a in range(n)]
        for cp in mine:
            cp.start()
        first = []
        for a in range(n):
            first.append(copy(a, 0, me, sibling, src=ins[a]))
            first += [copy(a, 1 + j, me, (*chip, c), src=ins[a]) for j, chip in enumerate(chips)]
        for cp in first:
            cp.start()

        def tables(i, carry):
            rows = pl.ds(pl.multiple_of(i * rows_per_pass, rows_per_pass), rows_per_pass)
            ang = pos_ref[rows, :].astype(F32) * invf_ref[...]
            cos_ref[rows, :] = jnp.cos(ang)
            sin_ref[rows, :] = jnp.sin(ang) * sign_ref[...]
            return carry

        lax.fori_loop(0, T // rows_per_pass, tables, 0)

        passed = []
        for j, chip in enumerate(chips):
            for a in range(n):
                copy(a, 1 + j, (*chip, c), me).wait_recv()
                fwd = copy(a, 4 + j, (*chip, c), sibling)
                fwd.start()
                passed.append(fwd)
        for a in range(n):
            copy(a, 0, sibling, me).wait_recv()
            for j, chip in enumerate(chips):
                copy(a, 4 + j, (*chip, 1 - c), me).wait_recv()
        for cp in first + passed:
            cp.wait_send()
        for cp in mine:
            cp.wait()

    vmem = pl.BlockSpec(memory_space=pltpu.VMEM)
    res = pl.pallas_call(
        body, name="gather_weights",
        in_specs=_any_specs(n) + [vmem] * 3, out_specs=_any_specs(n) + [vmem] * 2,
        out_shape=[jax.ShapeDtypeStruct((N_DEV, *s.shape), s.dtype) for s in shards]
                  + [jax.ShapeDtypeStruct((T, LANES), F32)] * 2,
        scratch_shapes=[pltpu.SemaphoreType.DMA((n, 7)), pltpu.SemaphoreType.DMA((n, 7)), pltpu.SemaphoreType.DMA((n,))],
        compiler_params=pltpu.CompilerParams(vmem_limit_bytes=V7X_VMEM_LIMIT),
    )(*shards, pos_col, invf, sign)
    return res[:n], res[n], res[n + 1]


def _w_in_grad_rs(h, dsh, chip_order, small):
    T = h.shape[0]
    tk = math.gcd(T, 2048)
    nk = T // tk
    chip_flips = [(1, 1), (1, 0), (0, 1)]
    n_steps = len(chip_flips) + 1
    SIB = len(chip_flips)
    k_finish = min(1, nk - 1)

    def body(order_ref, h_ref, d_ref, s_ref, own_ref, recv_ref, sall_ref,
             acc_ref, keep_ref, pre_ref, to_sib_ref, to_chip_ref,
             sib_send, sib_recv, chip_send, chip_recv, ssend_sems, srecv_sems, local_sem):
        i, kk = pl.program_id(0), pl.program_id(1)
        x, y, c = _my_place()
        my_dev = 4 * x + 2 * y + c

        def small_copy(r, slot):
            dx, dy, dc = FLIPS[r]
            return pltpu.make_async_remote_copy(
                src_ref=s_ref, dst_ref=sall_ref.at[slot], send_sem=ssend_sems.at[r], recv_sem=srecv_sems.at[r],
                device_id=(x ^ dx, y ^ dy, c ^ dc), device_id_type=MESH)

        keep_small = pltpu.make_async_copy(s_ref, sall_ref.at[my_dev], local_sem)

        def sib_copy(t):
            dst = recv_ref.at[SIB] if t == SIB else pre_ref.at[t]
            return pltpu.make_async_remote_copy(
                src_ref=to_sib_ref.at[t], dst_ref=dst, send_sem=sib_send.at[t], recv_sem=sib_recv.at[t],
                device_id=(x, y, 1 - c), device_id_type=MESH)

        def chip_copy(t):
            dx, dy = chip_flips[t]
            return pltpu.make_async_remote_copy(
                src_ref=to_chip_ref.at[t], dst_ref=recv_ref.at[t], send_sem=chip_send.at[t], recv_sem=chip_recv.at[t],
                device_id=(x ^ dx, y ^ dy, c), device_id_type=MESH)

        def halves():
            first, second = acc_ref[0:SHARD_PAD, :], acc_ref[SHARD_PAD:2 * SHARD_PAD, :]
            return jnp.where(c == 0, first, second), jnp.where(c == 0, second, first)

        @pl.when((i == 0) & (kk == 0))
        def _():
            keep_small.start()
            for r in range(len(FLIPS)):
                small_copy(r, my_dev).start()

        @pl.when(kk == 0)
        def _():
            acc_ref[...] = jnp.zeros_like(acc_ref)

        acc_ref[...] += _dot_tn(d_ref[...], h_ref[...])

        for t in range(len(chip_flips)):
            @pl.when((i == t + 1) & (kk == k_finish))
            def _(t=t):
                sib_copy(t).wait_recv()
                to_chip_ref[t] = (keep_ref[...] + pre_ref[t].astype(F32)).astype(BF16)
                chip_copy(t).start()

        for t in range(len(chip_flips)):
            @pl.when((i == t) & (kk == nk - 1))
            def _(t=t):
                mine, theirs = halves()
                to_sib_ref[t] = theirs.astype(BF16)
                sib_copy(t).start()
                keep_ref[...] = mine

        @pl.when((i == n_steps - 1) & (kk == nk - 1))
        def _():
            mine, theirs = halves()
            own_ref[...] = mine
            to_sib_ref[SIB] = theirs.astype(BF16)
            sib_copy(SIB).start()
            for t in range(len(chip_flips)):
                sib_copy(t).wait_send()
                chip_copy(t).wait_send()
                chip_copy(t).wait_recv()
            sib_copy(SIB).wait_send()
            sib_copy(SIB).wait_recv()
            for r, (dx, dy, dc) in enumerate(FLIPS):
                small_copy(r, 4 * (x ^ dx) + 2 * (y ^ dy) + (c ^ dc)).wait_recv()
                small_copy(r, my_dev).wait_send()
            keep_small.wait()

    shard = (SHARD_PAD, D_MODEL)
    return pl.pallas_call(
        body, name="w_in_grad_rs",
        grid_spec=pltpu.PrefetchScalarGridSpec(
            num_scalar_prefetch=1, grid=(n_steps, nk),
            in_specs=[pl.BlockSpec((tk, D_MODEL), lambda i, kk, order: (kk, 0)),
                      pl.BlockSpec((None, tk, 2 * SHARD_PAD), lambda i, kk, order: (order[i], kk, 0)),
                      pl.BlockSpec(memory_space=pl.ANY)],
            out_specs=[pl.BlockSpec(shard, lambda i, kk, order: (0, 0)),
                       pl.BlockSpec(memory_space=pl.ANY), pl.BlockSpec(memory_space=pl.ANY)],
            scratch_shapes=[pltpu.VMEM((2 * SHARD_PAD, D_MODEL), F32), pltpu.VMEM(shard, F32),
                            pltpu.VMEM((SIB, *shard), BF16), pltpu.VMEM((SIB + 1, *shard), BF16),
                            pltpu.VMEM((SIB, *shard), BF16),
                            pltpu.SemaphoreType.DMA((SIB + 1,)), pltpu.SemaphoreType.DMA((SIB + 1,)),
                            pltpu.SemaphoreType.DMA((SIB,)), pltpu.SemaphoreType.DMA((SIB,)),
                            pltpu.SemaphoreType.DMA((7,)), pltpu.SemaphoreType.DMA((7,)), pltpu.SemaphoreType.DMA]),
        out_shape=[jax.ShapeDtypeStruct(shard, F32),
                   jax.ShapeDtypeStruct((SIB + 1, *shard), BF16),
                   jax.ShapeDtypeStruct((N_DEV, *small.shape), F32)],
        compiler_params=_params("arbitrary", "arbitrary"),
    )(chip_order, h, dsh, small)


def _adam_math(w, g, m, v):
    m_new = ADAM_B1 * m + (1.0 - ADAM_B1) * g
    v_new = ADAM_B2 * v + (1.0 - ADAM_B2) * (g * g)
    m_hat = m_new / (1.0 - ADAM_B1 ** ADAM_STEP)
    v_hat = v_new / (1.0 - ADAM_B2 ** ADAM_STEP)
    delta = -ADAM_LR * (m_hat / (jnp.sqrt(v_hat) + ADAM_EPS) + ADAM_WD * w)
    return delta, m_new, v_new


def _adam_big(name, own, own_idx, recv, w, m, v):
    rw, cw = w.shape
    rp = own.shape[1]
    steps = 8
    by_cols = rp != rw
    blk_w = (rw, cw // steps) if by_cols else (rw // steps, cw)
    blk_g = (rp, cw // steps) if by_cols else (rw // steps, cw)
    at = (lambda i: (0, i)) if by_cols else (lambda i: (i, 0))

    def body(idx_ref, o_ref, r_ref, w_ref, m_ref, v_ref, g_ref, d_ref, mo_ref, vo_ref):
        g = o_ref[...].astype(F32)
        for r in range(recv.shape[0]):
            g = g + r_ref[r].astype(F32)
        g = g[0:blk_w[0], :]
        g_ref[...] = g
        d_ref[...], mo_ref[...], vo_ref[...] = _adam_math(w_ref[...], g, m_ref[...], v_ref[...])

    spec = pl.BlockSpec(blk_w, lambda i, idx_ref: at(i))
    return pl.pallas_call(
        body, name=name,
        grid_spec=pltpu.PrefetchScalarGridSpec(
            num_scalar_prefetch=1, grid=(steps,),
            in_specs=[pl.BlockSpec((None, *blk_g), lambda i, idx_ref: (idx_ref[0], *at(i))),
                      pl.BlockSpec((recv.shape[0], *blk_g), lambda i, idx_ref: (0, *at(i))), spec, spec, spec],
            out_specs=[spec] * 4),
        out_shape=[jax.ShapeDtypeStruct((rw, cw), F32)] * 4,
        compiler_params=_params("parallel"),
    )(own_idx, own, recv, w, m, v)


def _adam_small(small_all, params):
    flat = [a for triple in params for a in triple]
    n_par = len(params)

    def body(s_ref, *refs):
        ins, outs, loss_ref = refs[:3 * n_par], refs[3 * n_par:-1], refs[-1]
        g_slab = s_ref[0]
        for dev in range(1, N_DEV):
            g_slab = g_slab + s_ref[dev]
        loss_ref[...] = g_slab[SMALL_LOSS:SMALL_LOSS + 1, :]
        dev = 4 * lax.axis_index("x") + 2 * lax.axis_index("y") + lax.axis_index("c")
        alpha_full = jnp.concatenate([g_slab[SMALL_W_ALPHA + half * B_GATE_RANK:SMALL_W_ALPHA + (half + 1) * B_GATE_RANK]
                                      for half in range(B_KEY_WIDTH // LANES)], axis=1)
        alpha_mine = pltpu.roll(alpha_full, (B_KEY_WIDTH - dev * SHARD_ALPHA) % B_KEY_WIDTH, 1)[:, 0:SHARD_ALPHA]
        grads = [_take_rows(g_slab, SMALL_G_IN, D_MODEL // LANES), _take_rows(g_slab, SMALL_G_FINAL, D_MODEL // LANES),
                 _take_rows(g_slab, SMALL_G_GLA, B_WIDTH // LANES), _take_rows(g_slab, SMALL_B_ALPHA, B_KEY_WIDTH // LANES),
                 g_slab[SMALL_SINKS:SMALL_SINKS + 1, 0:A_HEADS], alpha_mine]
        for i, g in enumerate(grads):
            w_ref, m_ref, v_ref = ins[3 * i:3 * i + 3]
            delta, m_new, v_new = _adam_math(w_ref[...], g, m_ref[...], v_ref[...])
            outs[4 * i][...] = g
            outs[4 * i + 1][...] = delta
            outs[4 * i + 2][...] = m_new
            outs[4 * i + 3][...] = v_new

    res = pl.pallas_call(
        body, name="adam_small",
        out_shape=[jax.ShapeDtypeStruct(t[0].shape, F32) for t in params for _ in range(4)]
                  + [jax.ShapeDtypeStruct((1, LANES), F32)],
    )(small_all, *flat)
    return [res[4 * i:4 * i + 4] for i in range(n_par)], res[-1]


def _local_step(x, cosf, sinf, loss_target, g_in, wt_sh, wa_pad, b_alpha, sinks, g_gla, out_shards, g_final, chip_order):
    B, S, _ = x.shape
    T = B * S
    x2 = x.reshape(T, D_MODEL)
    tgt2 = loss_target.reshape(T, D_MODEL)
    f, (g_woa, g_wob, g_wo) = _in_proj(x2, cosf, sinf, g_in, wt_sh, wa_pad, b_alpha, out_shards)
    w_o = g_wo.reshape(D_MODEL, D_MODEL)
    sink_row = jnp.repeat(sinks, WINDOW).reshape(1, ATT_ROWS)
    sink_col = sink_row.reshape(ATT_ROWS, 1)
    attn, lse = _attn_fwd(f["qkv"], sink_row, B, S)
    o_gla, st_all = _gla_fwd(f["q"], f["k"], f["cum"], f["vb"], B, S)
    (dxres, dattn, dog, dza, dzb, dga, dgb, dw_o, dw_oa, dw_ob, small_a) = _merge(
        x2, tgt2, attn, f["za"], o_gla, f["zb"], f["ga"], f["gb"], g_woa, g_wob, w_o, g_gla, g_final)
    dq, dkv, dsink = _attn_bwd(f["qkv"], dattn, attn, lse, sink_col, B, S)
    (dqb, dkb, dvb, dla), (rv_o, rv_oa, rv_ob) = _gla_bwd(f["q"], f["k"], f["cum"], f["vb"], dog, st_all, B, S,
                                                        [dw_o, dw_oa, dw_ob])
    parts = dict(dq=dq, dkv=dkv, dza=dza, dqb=dqb, dkb=dkb, dvb=dvb, dzb=dzb, dla=dla, u=f["u"], alr=f["alr"],
                 dga=dga, dgb=dgb)
    dx, dsh, small_c = _in_proj_bwd(x2, dxres, cosf, sinf, g_in, f["wt_pad"], wa_pad, parts)
    small = jnp.concatenate([small_a, dsink, small_c], axis=0)
    own_in, rv_in, small_all = _w_in_grad_rs(f["h"], dsh, chip_order, small)
    return dict(grad_x=dx.reshape(B, S, D_MODEL), own_in=own_in, rv_in=rv_in,
                own_o=dw_o, rv_o=rv_o, own_oa=dw_oa, rv_oa=rv_oa, own_ob=dw_ob, rv_ob=rv_ob, small_all=small_all)


def kernel(x, positions, g_in, w_in, w_alpha_up, b_alpha, attn_sinks, g_gla_norm, w_out_a, w_out_b, w_o, g_final, loss_target, m_g_in, m_w_in, m_w_alpha_up, m_b_alpha, m_attn_sinks, m_g_gla_norm, m_w_out_a, m_w_out_b, m_w_o, m_g_final, v_g_in, v_w_in, v_w_alpha_up, v_b_alpha, v_attn_sinks, v_g_gla_norm, v_w_out_a, v_w_out_b, v_w_o, v_g_final):
    xi, yi, ci = _my_place()
    dev_idx = (4 * xi + 2 * yi + ci).reshape(1).astype(jnp.int32)
    chip = 2 * xi + yi
    chip_order = jnp.stack([chip ^ 3, chip ^ 2, chip ^ 1, chip]).astype(jnp.int32)

    (g_win, g_wa), cosf, sinf = _gather_first(
        [jnp.pad(w_in[0].T.astype(BF16), ((0, SHARD_PAD - SHARD_IN), (0, 0))), w_alpha_up[0].astype(BF16)],
        positions.reshape(-1, 1))
    wt_sh = g_win.reshape(N_DEV * SHARD_PAD, D_MODEL)
    wa_pad = jnp.pad(jnp.concatenate([g_wa[j] for j in range(N_DEV)], axis=1), ((0, RANK_PAD - B_GATE_RANK), (0, 0)))

    r = _local_step(x, cosf, sinf, loss_target, g_in, wt_sh, wa_pad, b_alpha, attn_sinks[0], g_gla_norm,
                    [w_out_a[0].astype(BF16), w_out_b[0].astype(BF16), w_o[0].astype(BF16)],
                    g_final.reshape(1, D_MODEL), chip_order)

    first = jnp.zeros((1,), jnp.int32)
    big = [[a.T for a in _adam_big("adam_w_in", r["own_in"][None], first, r["rv_in"], w_in[0].T, m_w_in[0].T, v_w_in[0].T)],
           _adam_big("adam_w_out_a", r["own_oa"], dev_idx, r["rv_oa"], w_out_a[0], m_w_out_a[0], v_w_out_a[0]),
           _adam_big("adam_w_out_b", r["own_ob"], dev_idx, r["rv_ob"], w_out_b[0], m_w_out_b[0], v_w_out_b[0]),
           _adam_big("adam_w_o", r["own_o"], dev_idx, r["rv_o"], w_o[0], m_w_o[0], v_w_o[0])]
    row = lambda a: a.reshape(1, D_MODEL)
    (s_in, s_final, s_gla, s_ba, s_sinks, s_wa), loss_row = _adam_small(r["small_all"], [
        (g_in, m_g_in, v_g_in), (row(g_final), row(m_g_final), row(v_g_final)),
        (g_gla_norm, m_g_gla_norm, v_g_gla_norm), (b_alpha, m_b_alpha, v_b_alpha),
        (attn_sinks, m_attn_sinks, v_attn_sinks), (w_alpha_up[0], m_w_alpha_up[0], v_w_alpha_up[0])])

    def group(i):
        return (s_in[i], big[0][i][None], s_wa[i][None], s_ba[i], s_sinks[i], s_gla[i], big[1][i][None], big[2][i][None],
                big[3][i][None], s_final[i].reshape(D_MODEL))

    return (loss_row[0, 0], r["grad_x"], *group(0), *group(1), *group(2), *group(3))
```

```python
import functools
import math

import numpy as np
import jax
import jax.numpy as jnp
from jax import lax
from jax.experimental import pallas as pl
from jax.experimental.pallas import tpu as pltpu

F32 = jnp.float32
BF16 = jnp.bfloat16
MESH = pl.DeviceIdType.MESH

D_MODEL = 1024
A_HEADS, A_KV_HEADS, A_HEAD_DIM = 8, 2, 64
A_WIDTH, A_KV_WIDTH = 512, 128
WINDOW = 128
ROPE_THETA = 500000.0
ROPE_DIM = 16
B_HEADS, B_KEY_DIM, B_VAL_DIM = 4, 64, 128
B_KEY_WIDTH, B_WIDTH = 256, 512
B_GATE_RANK = 16
B_GATE_TEMP = 16.0
B_CHUNK = 64
NORM_EPS = 1e-6
NEG_BIG = -1e30
D_IN = 4880
N_DEV = 8
N_CHIPS = 4
ADAM_LR, ADAM_B1, ADAM_B2, ADAM_EPS, ADAM_WD, ADAM_STEP = 0.001, 0.9, 0.999, 1e-08, 0.01, 10

LANES = 128
V7X_VMEM_LIMIT = 56 * 1024 * 1024
V7X_VMEM_LIMIT_MAX = 62 * 1024 * 1024

RANK_PAD = LANES
SEG = {}
_off = 0
for _name, _w in (("qa", 512), ("ka", 128), ("va", 128), ("za", 512), ("qb", 256), ("kb", 256),
                  ("vb", 512), ("zb", 512), ("alr", RANK_PAD), ("ga", 1024), ("gb", 1024)):
    SEG[_name] = (_off, _off + _w)
    _off += _w
D_IN_PAD = _off
ALR_SRC = SEG["alr"][0]
QKV_K, QKV_V, QKV_W = SEG["ka"][0], SEG["va"][0], SEG["va"][1]
ATT_SCALE = A_HEAD_DIM ** -0.5

SHARD_IN = D_IN // N_DEV
SHARD_PAD = 640
SHARD_OUT = D_MODEL // N_DEV
SHARD_ALPHA = B_KEY_WIDTH // N_DEV

SMALL_G_FINAL, SMALL_G_GLA, SMALL_LOSS, SMALL_SINKS, SMALL_G_IN, SMALL_B_ALPHA, SMALL_W_ALPHA = 0, 8, 12, 16, 24, 32, 40
SMALL_ROWS = 72


def _dot(a, b):
    return jnp.dot(a, b, preferred_element_type=F32)


def _dot_nt(a, b):
    return lax.dot_general(a, b, (((1,), (1,)), ((), ())), preferred_element_type=F32)


def _dot_tn(a, b):
    return lax.dot_general(a, b, (((0,), (0,)), ((), ())), preferred_element_type=F32)


def _sigmoid(z):
    return 1.0 / (1.0 + jnp.exp(-z))


def _sigmoid_tanh(z):
    return 0.5 * jnp.tanh(0.5 * z) + 0.5


def _params(*sem):
    return pltpu.CompilerParams(dimension_semantics=sem, vmem_limit_bytes=V7X_VMEM_LIMIT)


def _const_spec(shape):
    nd = len(shape)
    return pl.BlockSpec(shape, lambda *_: (0,) * nd, pipeline_mode=pl.Buffered(1))


def _lane_iota(shape):
    return lax.broadcasted_iota(jnp.int32, shape, 1)


def _row_iota(shape):
    return lax.broadcasted_iota(jnp.int32, shape, 0)


def _split3(v):
    hi = v.astype(BF16)
    r1 = v - hi.astype(F32)
    mid = r1.astype(BF16)
    lo = (r1 - mid.astype(F32)).astype(BF16)
    return hi, mid, lo


def _put_rows(ref, row0, vec):
    for r in range(vec.shape[1] // LANES):
        ref[row0 + r:row0 + r + 1, :] = vec[:, r * LANES:(r + 1) * LANES]


def _take_rows(slab, row0, n):
    return jnp.concatenate([slab[row0 + r:row0 + r + 1, :] for r in range(n)], axis=1)


def _rope_lane_constants():
    half = ROPE_DIM // 2
    inv_freq = np.exp(-math.log(ROPE_THETA) * np.arange(half, dtype=np.float32) * np.float32(2.0 / ROPE_DIM)).astype(np.float32)
    lane = np.arange(LANES)
    j = lane % A_HEAD_DIM
    invf = np.where(j < ROPE_DIM, inv_freq[j % half], 0.0).astype(np.float32)
    sign = np.where(j < half, -1.0, np.where(j < ROPE_DIM, 1.0, 0.0)).astype(np.float32)
    return jnp.asarray(invf)[None, :], jnp.asarray(sign)[None, :]


def _rope_slab(t, cos, sin_signed):
    first = (_lane_iota(t.shape) % A_HEAD_DIM) < (ROPE_DIM // 2)
    partner = jnp.where(first, pltpu.roll(t, LANES - ROPE_DIM // 2, 1), pltpu.roll(t, ROPE_DIM // 2, 1))
    return t * cos + partner * sin_signed


def _shard_pad_cols(j):
    cut = ALR_SRC + B_GATE_RANK
    shift = RANK_PAD - B_GATE_RANK
    a, b = j * SHARD_IN, (j + 1) * SHARD_IN
    if b <= cut:
        return [(a, b)]
    if a >= cut:
        return [(a + shift, b + shift)]
    return [(a, cut), (cut + shift, b + shift)]


def _in_proj(x2, cosf, sinf, g_in, wt_sh, wa_pad, b_alpha, later_shards):
    T = x2.shape[0]
    tm = math.gcd(T, 512)
    sub = math.gcd(tm, 256)
    last = T // tm - 1
    nl = len(later_shards)

    def body(x_ref, cos_ref, sin_ref, g_ref, wsh_ref, wa_ref, ba_ref, *rest):
        sh_refs, rest = rest[:nl], rest[nl:]
        (h_ref, qkv_ref, za_ref, q_ref, k_ref, vb_ref, zb_ref, alr_ref, u_ref, cum_ref, ga_ref, gb_ref, wt_out) = rest[:13]
        all_refs, (wt_ref, send_sems, recv_sems, local_sems, wt_sem) = rest[13:13 + nl], rest[13 + nl:]
        wt_copy = pltpu.make_async_copy(wt_ref, wt_out, wt_sem)
        px, py, pc = _my_place()
        my_dev = 4 * px + 2 * py + pc

        def wcopy(a, r, slot):
            dx, dy, dc = FLIPS[r]
            return pltpu.make_async_remote_copy(
                src_ref=sh_refs[a], dst_ref=all_refs[a].at[slot], send_sem=send_sems.at[a, r],
                recv_sem=recv_sems.at[a, r], device_id=(px ^ dx, py ^ dy, pc ^ dc), device_id_type=MESH)

        keep = [pltpu.make_async_copy(sh_refs[a], all_refs[a].at[my_dev], local_sems.at[a]) for a in range(nl)]

        @pl.when(pl.program_id(0) == 0)
        def _():
            for a in range(nl):
                keep[a].start()
                for r in range(len(FLIPS)):
                    wcopy(a, r, my_dev).start()

        @pl.when(pl.program_id(0) == 0)
        def _():
            for j in range(N_DEV):
                src = j * SHARD_PAD
                for a, b in _shard_pad_cols(j):
                    wt_ref[a:b, :] = wsh_ref[src:src + b - a, :]
                    src += b - a
            a, b = SEG["alr"]
            wt_ref[a + B_GATE_RANK:b, :] = jnp.zeros((RANK_PAD - B_GATE_RANK, D_MODEL), BF16)
            wt_copy.start()

        def one_tile(rows):
            x = x_ref[rows, :]
            r = lax.rsqrt(jnp.mean(x * x, axis=-1, keepdims=True) + NORM_EPS)
            h = (x * r * g_ref[...]).astype(BF16)
            h_ref[rows, :] = h

            def seg(name):
                a, b = SEG[name]
                return _dot_nt(h, wt_ref[a:b, :])

            alr = seg("alr").astype(BF16)
            alr_ref[rows, :] = alr
            u = _dot(alr, wa_ref[...]) + ba_ref[...]
            u_ref[rows, :] = u
            log_a = (jnp.minimum(u, 0.0) - jnp.log(1.0 + jnp.exp(-jnp.abs(u)))) * (1.0 / B_GATE_TEMP)
            row, col = _row_iota((sub, sub)), _lane_iota((sub, sub))
            tri = ((row // B_CHUNK == col // B_CHUNK) & (col <= row)).astype(BF16)
            hi, mid, lo = _split3(log_a)
            cum_ref[rows, :] = _dot(tri, hi) + _dot(tri, mid) + _dot(tri, lo)

            cos, sin = cos_ref[rows, :], sin_ref[rows, :]
            qa = seg("qa") * ATT_SCALE
            for s in range(A_WIDTH // LANES):
                qkv_ref[rows, s * LANES:(s + 1) * LANES] = _rope_slab(qa[:, s * LANES:(s + 1) * LANES], cos, sin).astype(BF16)
            qkv_ref[rows, QKV_K:QKV_V] = _rope_slab(seg("ka"), cos, sin).astype(BF16)
            qkv_ref[rows, QKV_V:QKV_W] = seg("va").astype(BF16)
            za_ref[rows, :] = seg("za").astype(BF16)
            q_ref[rows, :] = seg("qb")
            k_ref[rows, :] = seg("kb")
            vb_ref[rows, :] = seg("vb").astype(BF16)
            zb_ref[rows, :] = seg("zb").astype(BF16)
            ga_ref[rows, :] = seg("ga").astype(BF16)
            gb_ref[rows, :] = seg("gb").astype(BF16)

        for j in range(tm // sub):
            one_tile(pl.ds(j * sub, sub))

        @pl.when(pl.program_id(0) == last)
        def _():
            for a in range(nl):
                for r, (dx, dy, dc) in enumerate(FLIPS):
                    wcopy(a, r, 4 * (px ^ dx) + 2 * (py ^ dy) + (pc ^ dc)).wait_recv()
                    wcopy(a, r, my_dev).wait_send()
                keep[a].wait()
            wt_copy.wait()

    def rows(w):
        return pl.BlockSpec((tm, w), lambda i: (i, 0))

    outs = [("h", D_MODEL, BF16), ("qkv", QKV_W, BF16), ("za", A_WIDTH, BF16), ("q", B_KEY_WIDTH, F32),
            ("k", B_KEY_WIDTH, F32), ("vb", B_WIDTH, BF16), ("zb", B_WIDTH, BF16), ("alr", RANK_PAD, BF16),
            ("u", B_KEY_WIDTH, F32), ("cum", B_KEY_WIDTH, F32), ("ga", D_MODEL, BF16), ("gb", D_MODEL, BF16)]
    res = pl.pallas_call(
        body, name="in_proj", grid=(T // tm,),
        in_specs=[rows(D_MODEL), rows(LANES), rows(LANES), _const_spec((1, D_MODEL)),
                  _const_spec((N_DEV * SHARD_PAD, D_MODEL)), _const_spec((RANK_PAD, B_KEY_WIDTH)),
                  _const_spec((1, B_KEY_WIDTH))] + _any_specs(nl),
        out_specs=[rows(w) for _, w, _ in outs] + _any_specs(1 + nl),
        out_shape=[jax.ShapeDtypeStruct((T, w), dt) for _, w, dt in outs]
                  + [jax.ShapeDtypeStruct((D_IN_PAD, D_MODEL), BF16)]
                  + [jax.ShapeDtypeStruct((N_DEV, *sh.shape), sh.dtype) for sh in later_shards],
        scratch_shapes=[pltpu.VMEM((D_IN_PAD, D_MODEL), BF16),
                        pltpu.SemaphoreType.DMA((nl, len(FLIPS))), pltpu.SemaphoreType.DMA((nl, len(FLIPS))),
                        pltpu.SemaphoreType.DMA((nl,)), pltpu.SemaphoreType.DMA],
        compiler_params=_params("arbitrary"),
    )(x2, cosf, sinf, g_in, wt_sh, wa_pad, b_alpha, *later_shards)
    n_out = len(outs) + 1
    return dict(zip([n for n, _, _ in outs] + ["wt_pad"], res[:n_out])), res[n_out:]


def _dup_kv_head(t, g):
    tf = t.astype(F32)
    keep = (_lane_iota(tf.shape) < A_HEAD_DIM) == (g == 0)
    return jnp.where(keep, tf, pltpu.roll(tf, A_HEAD_DIM, 1)).astype(BF16)


def _stack_heads(t):
    lo = _lane_iota(t.shape) < A_HEAD_DIM
    zero = jnp.zeros_like(t)
    return jnp.concatenate([jnp.where(lo, t, zero), jnp.where(lo, zero, t)], axis=0)


ATT_ROWS = A_HEADS * WINDOW
GROUP_ROWS = ATT_ROWS // A_KV_HEADS
HEADS_PER_GROUP = A_HEADS // A_KV_HEADS


def _band_mask_t(n):
    kj = _row_iota((2 * WINDOW, GROUP_ROWS)) - WINDOW
    qi = _lane_iota((2 * WINDOW, GROUP_ROWS)) % WINDOW
    return (kj <= qi) & (qi - kj < WINDOW) & ((n > 0) | (kj >= 0))


def _stacked_queries(ref, g):
    pairs = range(g * HEADS_PER_GROUP // 2, (g + 1) * HEADS_PER_GROUP // 2)
    return jnp.concatenate([_stack_heads(ref[:, p * LANES:(p + 1) * LANES]) for p in pairs], axis=0)


def _unstack_heads(t, g, ref, dtype):
    lo = _lane_iota((WINDOW, LANES)) < A_HEAD_DIM
    for hh in range(HEADS_PER_GROUP // 2):
        p = g * HEADS_PER_GROUP // 2 + hh
        ref[:, p * LANES:(p + 1) * LANES] = jnp.where(lo, t[2 * hh * WINDOW:(2 * hh + 1) * WINDOW],
                                                       t[(2 * hh + 1) * WINDOW:(2 * hh + 2) * WINDOW]).astype(dtype)


FWD_BLOCKS = 8


def _attn_fwd(qkv, sink_row, B, S):
    T = B * S
    nb = S // WINDOW
    blocks = math.gcd(nb, FWD_BLOCKS)
    steps = nb // blocks

    def one_block(has_prev, sink_ref, q, k, v, o_ref, lse_ref):
        valid = _band_mask_t(has_prev)
        lse_rows = []
        for g in range(A_KV_HEADS):
            kd, vd = _dup_kv_head(k, g), _dup_kv_head(v, g)
            s = jnp.where(valid, _dot_nt(kd, _stacked_queries(q, g)), NEG_BIG)
            sink = sink_ref[:, g * GROUP_ROWS:(g + 1) * GROUP_ROWS]
            m = jnp.maximum(jnp.max(s, axis=0, keepdims=True), sink)
            e = jnp.exp(s - m)
            den = jnp.sum(e, axis=0, keepdims=True) + jnp.exp(sink - m)
            o = _dot_tn((e * (1.0 / den)).astype(BF16), vd)
            _unstack_heads(o, g, o_ref, F32)
            lse = m + jnp.log(den)
            lse_rows += [lse[:, j * WINDOW:(j + 1) * WINDOW] for j in range(HEADS_PER_GROUP)]
        by_head = jnp.concatenate(lse_rows + [jnp.zeros((WINDOW - A_HEADS, WINDOW), F32)], axis=0)
        lse_ref[...] = by_head.T

    def body(sink_ref, q_ref, kc_ref, vc_ref, kp_ref, vp_ref, o_ref, lse_ref):
        k_all = jnp.concatenate([kp_ref[...], kc_ref[...]], axis=0)
        v_all = jnp.concatenate([vp_ref[...], vc_ref[...]], axis=0)
        for j in range(blocks):
            rows = pl.ds(j * WINDOW, WINDOW)
            keys = slice(j * WINDOW, (j + 2) * WINDOW)
            has_prev = pl.program_id(1) if j == 0 else 1
            one_block(has_prev, sink_ref, q_ref[rows, :], k_all[keys], v_all[keys], o_ref.at[rows], lse_ref.at[rows])

    def cur(col, w):
        return pl.BlockSpec((blocks * WINDOW, w), lambda b, n: (b * steps + n, col))

    def prev(col):
        return pl.BlockSpec((WINDOW, LANES), lambda b, n: (b * nb + jnp.maximum(blocks * n - 1, 0), col))

    kcol, vcol = QKV_K // LANES, QKV_V // LANES
    return pl.pallas_call(
        body, name="attn_fwd", grid=(B, steps),
        in_specs=[_const_spec((1, ATT_ROWS)), cur(0, A_WIDTH), cur(kcol, LANES), cur(vcol, LANES), prev(kcol), prev(vcol)],
        out_specs=[cur(0, A_WIDTH), cur(0, LANES)],
        out_shape=[jax.ShapeDtypeStruct((T, A_WIDTH), F32), jax.ShapeDtypeStruct((T, LANES), F32)],
        compiler_params=_params("parallel", "parallel"),
    )(sink_row, qkv, qkv, qkv, qkv, qkv)


ATT_CHUNK = 64


def _chunk_masks(n):
    masks = []
    for half in range(WINDOW // ATT_CHUNK):
        qi = _row_iota((ATT_CHUNK, 2 * WINDOW)) + half * ATT_CHUNK
        kj = _lane_iota((ATT_CHUNK, 2 * WINDOW)) - WINDOW
        masks.append((kj <= qi) & (qi - kj < WINDOW) & ((n > 0) | (kj >= 0)))
    return masks


def _all_stacked_queries(ref):
    return jnp.concatenate([_stacked_queries(ref, g) for g in range(A_KV_HEADS)], axis=0)


def _by_group(fn, lhs, rhs_per_group):
    return jnp.concatenate([fn(lhs[g * GROUP_ROWS:(g + 1) * GROUP_ROWS], rhs_per_group[g])
                            for g in range(A_KV_HEADS)], axis=0)


BWD_BLOCKS = 8


def _attn_bwd(qkv, do, out, lse, sink_col, B, S):
    T = B * S
    nb = S // WINDOW
    M = math.gcd(nb, BWD_BLOCKS)
    steps = nb // M
    n_chunks = ATT_ROWS // ATT_CHUNK
    halves = WINDOW // ATT_CHUNK

    def block(has_prev, sink_ref, q, do_b, out_b, lse_b, k, v, scratch, want_dq):
        s_ref, dp_ref, ds_ref, p_ref = scratch
        width = k.shape[0]
        masks = [mk[:, 0:width] for mk in _chunk_masks(has_prev)]
        kd = [_dup_kv_head(k, g) for g in range(A_KV_HEADS)]
        vd = [_dup_kv_head(v, g) for g in range(A_KV_HEADS)]
        qs, dos = _all_stacked_queries(q), _all_stacked_queries(do_b)
        s_ref[...] = _by_group(_dot_nt, qs, kd)
        dp_ref[...] = _by_group(_dot_nt, dos, vd)
        lane = _lane_iota((ATT_CHUNK, LANES))
        lo = lane < A_HEAD_DIM
        lane1 = _lane_iota((1, LANES))
        dsink_row = jnp.zeros((1, LANES), F32)
        for c in range(n_chunks):
            rows = slice(c * ATT_CHUNK, (c + 1) * ATT_CHUNK)
            head, half = divmod(c, halves)
            qrows = slice(half * ATT_CHUNK, (half + 1) * ATT_CHUNK)
            slab = slice((head // 2) * LANES, (head // 2 + 1) * LANES)
            lse_col = jnp.sum(jnp.where(lane == head, lse_b[qrows, :], 0.0), axis=-1, keepdims=True)
            prod = do_b[qrows, slab].astype(F32) * out_b[qrows, slab].astype(F32)
            mine = lo if head % 2 == 0 else jnp.logical_not(lo)
            delta = jnp.sum(jnp.where(mine, prod, 0.0), axis=-1, keepdims=True)
            prob = jnp.exp(jnp.where(masks[half], s_ref[rows, :], NEG_BIG) - lse_col)
            p_ref[rows, :] = prob.astype(BF16)
            ds_ref[rows, :] = (prob * (dp_ref[rows, :] - delta)).astype(BF16)
            w = -jnp.exp(sink_ref[rows, :] - lse_col) * delta
            dsink_row += jnp.where(lane1 == head, jnp.sum(w, axis=0, keepdims=True), 0.0)
        dq = _by_group(_dot, ds_ref[...], kd) * ATT_SCALE if want_dq else None
        groups = [slice(g * GROUP_ROWS, (g + 1) * GROUP_ROWS) for g in range(A_KV_HEADS)]
        dk = [_dot_tn(ds_ref[rows, :], qs[rows]) for rows in groups]
        dv = [_dot_tn(p_ref[rows, :], dos[rows]) for rows in groups]
        return dq, dk, dv, dsink_row

    def fold(per_group):
        lane = _lane_iota((WINDOW, LANES))
        out = jnp.zeros((WINDOW, LANES), F32)
        for g, acc in enumerate(per_group):
            out = jnp.where((lane < A_HEAD_DIM) == (g == 0), acc + pltpu.roll(acc, A_HEAD_DIM, 1), out)
        return out

    def body(sink_ref, q_ref, qn_ref, do_ref, don_ref, out_ref, outn_ref, lse_ref, lsen_ref, kc_ref, kp_ref, vc_ref, vp_ref,
             dq_ref, dkv_ref, dsink_ref, s_scr, dp_scr, ds_scr, p_scr, s_x, dp_x, ds_x, p_x):
        b, m = pl.program_id(0), pl.program_id(1)

        @pl.when((b == 0) & (m == 0))
        def _():
            dsink_ref[...] = jnp.zeros_like(dsink_ref)

        k_all = jnp.concatenate([kp_ref[...], kc_ref[...]], axis=0)
        v_all = jnp.concatenate([vp_ref[...], vc_ref[...]], axis=0)
        results = []
        for j in range(M):
            rows = slice(j * WINDOW, (j + 1) * WINDOW)
            keys = slice(j * WINDOW, (j + 2) * WINDOW)
            has_prev = m if j == 0 else 1
            results.append(block(has_prev, sink_ref, q_ref[rows, :], do_ref[rows, :], out_ref[rows, :], lse_ref[rows, :],
                                 k_all[keys], v_all[keys], (s_scr.at[j], dp_scr.at[j], ds_scr.at[j], p_scr.at[j]), True))
        last_keys = slice(M * WINDOW, (M + 1) * WINDOW)
        _, dk_x, dv_x, _ = block(1, sink_ref, qn_ref[...], don_ref[...], outn_ref[...], lsen_ref[...],
                                 k_all[last_keys], v_all[last_keys], (s_x, dp_x, ds_x, p_x), False)
        has_next = m < steps - 1
        lo_q = _lane_iota((WINDOW, LANES)) < A_HEAD_DIM
        dsink_row = jnp.zeros((1, LANES), F32)
        for j, (dq, dk, dv, ds_row) in enumerate(results):
            rows = slice(j * WINDOW, (j + 1) * WINDOW)
            for p in range(A_HEADS // 2):
                dq_ref[rows, p * LANES:(p + 1) * LANES] = jnp.where(
                    lo_q, dq[2 * p * WINDOW:(2 * p + 1) * WINDOW], dq[(2 * p + 1) * WINDOW:(2 * p + 2) * WINDOW]).astype(BF16)
            if j + 1 < M:
                dk_next = [t[0:WINDOW] for t in results[j + 1][1]]
                dv_next = [t[0:WINDOW] for t in results[j + 1][2]]
            else:
                dk_next = [jnp.where(has_next, t, 0.0) for t in dk_x]
                dv_next = [jnp.where(has_next, t, 0.0) for t in dv_x]
            dkv_ref[rows, 0:LANES] = fold([own[WINDOW:] + nxt for own, nxt in zip(dk, dk_next)]).astype(BF16)
            dkv_ref[rows, LANES:] = fold([own[WINDOW:] + nxt for own, nxt in zip(dv, dv_next)]).astype(BF16)
            dsink_row += ds_row
        dsink_ref[0:1, :] += dsink_row

    def cur(col, w):
        return pl.BlockSpec((M * WINDOW, w), lambda b, m: (b * steps + m, col))

    def nxt(col, w):
        return pl.BlockSpec((WINDOW, w), lambda b, m: (b * nb + jnp.minimum(M * (m + 1), nb - 1), col))

    def prev(col):
        return pl.BlockSpec((WINDOW, LANES), lambda b, m: (b * nb + jnp.maximum(M * m - 1, 0), col))

    kcol, vcol = QKV_K // LANES, QKV_V // LANES
    scores = (M, ATT_ROWS, 2 * WINDOW)
    extra = (ATT_ROWS, WINDOW)
    return pl.pallas_call(
        body, name="attn_bwd", grid=(B, steps),
        in_specs=[_const_spec((ATT_ROWS, 1)), cur(0, A_WIDTH), nxt(0, A_WIDTH), cur(0, A_WIDTH), nxt(0, A_WIDTH),
                  cur(0, A_WIDTH), nxt(0, A_WIDTH), cur(0, LANES), nxt(0, LANES),
                  cur(kcol, LANES), prev(kcol), cur(vcol, LANES), prev(vcol)],
        out_specs=[cur(0, A_WIDTH), cur(0, 2 * LANES), pl.BlockSpec((8, LANES), lambda b, m: (0, 0))],
        out_shape=[jax.ShapeDtypeStruct((T, A_WIDTH), BF16), jax.ShapeDtypeStruct((T, 2 * LANES), BF16),
                   jax.ShapeDtypeStruct((8, LANES), F32)],
        scratch_shapes=[pltpu.VMEM(scores, F32), pltpu.VMEM(scores, F32), pltpu.VMEM(scores, BF16), pltpu.VMEM(scores, BF16),
                        pltpu.VMEM(extra, F32), pltpu.VMEM(extra, F32), pltpu.VMEM(extra, BF16), pltpu.VMEM(extra, BF16)],
        compiler_params=_params("arbitrary", "arbitrary"),
    )(sink_col, qkv, qkv, do, do, out, out, lse, lse, qkv, qkv, qkv, qkv)


GLA_TILE = 256
CHUNKS_PER_TILE = GLA_TILE // B_CHUNK
GLA_TILES_PER_STEP = 4


def _gla_factors(q_ref, k_ref, cum_ref):
    scale = B_KEY_DIM ** -0.5
    cum = cum_ref[...]
    shape = (B_CHUNK, B_KEY_WIDTH)
    last = jnp.concatenate([jnp.broadcast_to(cum_ref[pl.ds(c * B_CHUNK + B_CHUNK - 1, 1), :], shape)
                            for c in range(CHUNKS_PER_TILE)], axis=0)
    mid = jnp.concatenate([jnp.broadcast_to(cum_ref[pl.ds(c * B_CHUNK + B_CHUNK // 2 - 1, 1), :], shape)
                           for c in range(CHUNKS_PER_TILE)], axis=0)
    e_qm, e_km, e_qe, e_kd = jnp.exp(cum - mid), jnp.exp(mid - cum), jnp.exp(cum), jnp.exp(last - cum)
    qs = q_ref[...] * scale
    k = k_ref[...]
    return qs, k, (e_qm, e_km, e_qe, e_kd)


def _head_mask(shape, h):
    return (_lane_iota(shape) // B_KEY_DIM) == h


def _stack_masked(t):
    return jnp.concatenate([jnp.where(_head_mask(t.shape, h), t, 0.0) for h in range(B_HEADS)], axis=0).astype(BF16)


def _select_heads(t):
    shape = (B_CHUNK, B_KEY_WIDTH)
    out = jnp.zeros(shape, F32)
    for h in range(B_HEADS):
        out = jnp.where(_head_mask(shape, h), t[h * B_CHUNK:(h + 1) * B_CHUNK], out)
    return out


def _select_state(t):
    shape = (B_VAL_DIM, B_KEY_WIDTH)
    out = jnp.zeros(shape, F32)
    for h in range(B_HEADS):
        out = jnp.where(_head_mask(shape, h), t[h * B_VAL_DIM:(h + 1) * B_VAL_DIM], out)
    return out


def _rows_by_head(t):
    return jnp.concatenate([t[:, h * B_VAL_DIM:(h + 1) * B_VAL_DIM] for h in range(B_HEADS)], axis=0)


def _intra_mask():
    i, j = _row_iota((GLA_TILE, GLA_TILE)), _lane_iota((GLA_TILE, GLA_TILE))
    return (i // B_CHUNK == j // B_CHUNK) & (j <= i)


def _pair_stack(t, p):
    slab = t[:, p * LANES:(p + 1) * LANES]
    lo = _lane_iota(slab.shape) < B_KEY_DIM
    return jnp.concatenate([jnp.where(lo, slab, 0.0), jnp.where(lo, 0.0, slab)], axis=0).astype(BF16)


def _gla_fwd(q, k, cum, vb, B, S):
    T = B * S
    nt = S // GLA_TILE
    tps = math.gcd(nt, GLA_TILES_PER_STEP)

    def one_sequence(q_ref, k_ref, cum_ref, v_ref, o_ref, st_all_ref, st_ref):
        qs, kk, (e_qm, e_km, e_qe, e_kd) = _gla_factors(q_ref, k_ref, cum_ref)
        qm, km, qe, kd = qs * e_qm, kk * e_km, qs * e_qe, (kk * e_kd).astype(BF16)
        mask = _intra_mask()
        intra = []
        for p in range(B_HEADS // 2):
            a = _dot_nt(_pair_stack(qm, p), km[:, p * LANES:(p + 1) * LANES].astype(BF16))
            for hh in range(2):
                h = 2 * p + hh
                att = jnp.where(mask, a[hh * GLA_TILE:(hh + 1) * GLA_TILE], 0.0).astype(BF16)
                intra.append(_dot(att, v_ref[:, h * B_VAL_DIM:(h + 1) * B_VAL_DIM]))
        inter = []
        for c in range(CHUNKS_PER_TILE):
            rows = slice(c * B_CHUNK, (c + 1) * B_CHUNK)
            st = st_ref[...]
            st_all_ref[c] = st
            inter.append(_dot_nt(_stack_masked(qe[rows]), st.astype(BF16)))
            inc = _select_state(_dot_tn(v_ref[rows, :], kd[rows]))
            decay = jnp.exp(cum_ref[pl.ds(c * B_CHUNK + B_CHUNK - 1, 1), :])
            st_ref[...] = st * decay + inc
        for h in range(B_HEADS):
            oi = jnp.concatenate([inter[c][h * B_CHUNK:(h + 1) * B_CHUNK] for c in range(CHUNKS_PER_TILE)], axis=0)
            o_ref[:, h * B_VAL_DIM:(h + 1) * B_VAL_DIM] = (intra[h] + oi).astype(BF16)

    def body(q_ref, k_ref, cum_ref, v_ref, o_ref, st_all_ref, st_ref):
        @pl.when(pl.program_id(0) == 0)
        def _():
            st_ref[...] = jnp.zeros_like(st_ref)

        for b in range(B):
            for tile in range(tps):
                tok = pl.ds(tile * GLA_TILE, GLA_TILE)
                chunks = pl.ds(tile * CHUNKS_PER_TILE, CHUNKS_PER_TILE)
                one_sequence(*[r.at[b, tok] for r in (q_ref, k_ref, cum_ref, v_ref, o_ref)],
                             st_all_ref.at[b, chunks], st_ref.at[b])

    def rows(w):
        return pl.BlockSpec((B, tps * GLA_TILE, w), lambda t: (0, t, 0))

    seq = lambda a: a.reshape(B, S, a.shape[-1])
    o, st_all = pl.pallas_call(
        body, name="gla_fwd", grid=(nt // tps,),
        in_specs=[rows(B_KEY_WIDTH), rows(B_KEY_WIDTH), rows(B_KEY_WIDTH), rows(B_WIDTH)],
        out_specs=[rows(B_WIDTH),
                   pl.BlockSpec((B, tps * CHUNKS_PER_TILE, B_VAL_DIM, B_KEY_WIDTH), lambda t: (0, t, 0, 0))],
        out_shape=[jax.ShapeDtypeStruct((B, S, B_WIDTH), BF16),
                   jax.ShapeDtypeStruct((B, S // B_CHUNK, B_VAL_DIM, B_KEY_WIDTH), F32)],
        scratch_shapes=[pltpu.VMEM((B, B_VAL_DIM, B_KEY_WIDTH), F32)],
        compiler_params=_params("arbitrary"),
    )(seq(q), seq(k), seq(cum), seq(vb))
    return o.reshape(T, B_WIDTH), st_all.reshape(T // B_CHUNK, B_VAL_DIM, B_KEY_WIDTH)


def _gla_bwd(q, k, cum, vb, do, st_all, B, S, wgrads):
    T = B * S
    nt = S // GLA_TILE
    tps = math.gcd(nt, GLA_TILES_PER_STEP)
    steps = nt // tps
    scale = B_KEY_DIM ** -0.5
    nw = len(wgrads)

    def one_sequence(q_ref, k_ref, cum_ref, v_ref, do_ref, st_all_ref, dq_ref, dk_ref, dv_ref, dla_ref, dst_ref):
        qs, kk, (e_qm, e_km, e_qe, e_kd) = _gla_factors(q_ref, k_ref, cum_ref)
        qm, km, qe, kd = qs * e_qm, kk * e_km, qs * e_qe, kk * e_kd
        mask = _intra_mask()
        dqm_slabs, dkm_slabs, dv_intra = [], [], []
        for p in range(B_HEADS // 2):
            qm_st = _pair_stack(qm, p)
            km_p = km[:, p * LANES:(p + 1) * LANES].astype(BF16)
            a = _dot_nt(qm_st, km_p)
            da_blocks, dqm_h = [], []
            for hh in range(2):
                h = 2 * p + hh
                vs = slice(h * B_VAL_DIM, (h + 1) * B_VAL_DIM)
                att = jnp.where(mask, a[hh * GLA_TILE:(hh + 1) * GLA_TILE], 0.0).astype(BF16)
                dv_intra.append(_dot_tn(att, do_ref[:, vs]))
                da = jnp.where(mask, _dot_nt(do_ref[:, vs], v_ref[:, vs]), 0.0).astype(BF16)
                da_blocks.append(da)
                dqm_h.append(_dot(da, km_p))
            lo = _lane_iota((GLA_TILE, LANES)) < B_KEY_DIM
            dqm_slabs.append(jnp.where(lo, dqm_h[0], dqm_h[1]))
            dkm_slabs.append(_dot_tn(jnp.concatenate(da_blocks, axis=0), qm_st))
        dqm = jnp.concatenate(dqm_slabs, axis=1)
        dkm = jnp.concatenate(dkm_slabs, axis=1)

        dqe_c, dkd_c, dv_inter, tail_c = ([None] * CHUNKS_PER_TILE for _ in range(4))
        for c in reversed(range(CHUNKS_PER_TILE)):
            rows = slice(c * B_CHUNK, (c + 1) * B_CHUNK)
            dst = dst_ref[...]
            dst_b = dst.astype(BF16)
            dv_inter[c] = _dot_nt(_stack_masked(kd[rows]), dst_b)
            dkd_c[c] = _select_heads(_dot(_rows_by_head(v_ref[rows, :]), dst_b))
            do_c = do_ref[rows, :]
            dqe_c[c] = _select_heads(_dot(_rows_by_head(do_c), st_all_ref[c].astype(BF16)))
            contrib = _select_state(_dot_tn(do_c, qe[rows].astype(BF16)))
            decay = jnp.exp(cum_ref[pl.ds(c * B_CHUNK + B_CHUNK - 1, 1), :])
            tail = (jnp.sum(kk[rows] * dkd_c[c] * e_kd[rows], axis=0, keepdims=True)
                    + decay * jnp.sum(st_all_ref[c] * dst, axis=0, keepdims=True))
            tail_c[c] = jnp.broadcast_to(tail, (B_CHUNK, B_KEY_WIDTH))
            dst_ref[...] = dst * decay + contrib
        dqe = jnp.concatenate(dqe_c, axis=0)
        dkd = jnp.concatenate(dkd_c, axis=0)
        dqs = dqm * e_qm + dqe * e_qe
        dk = dkm * e_km + dkd * e_kd
        dq_ref[...] = (dqs * scale).astype(BF16)
        dk_ref[...] = dk.astype(BF16)
        for h in range(B_HEADS):
            dvi = jnp.concatenate([dv_inter[c][h * B_CHUNK:(h + 1) * B_CHUNK] for c in range(CHUNKS_PER_TILE)], axis=0)
            dv_ref[:, h * B_VAL_DIM:(h + 1) * B_VAL_DIM] = (dv_intra[h] + dvi).astype(BF16)
        dd = qs * dqs - kk * dk
        i, j = _row_iota((GLA_TILE, GLA_TILE)), _lane_iota((GLA_TILE, GLA_TILE))
        upper = ((i // B_CHUNK == j // B_CHUNK) & (j >= i)).astype(BF16)
        hi, mid, lo3 = _split3(dd)
        dla_ref[...] = _dot(upper, hi) + _dot(upper, mid) + _dot(upper, lo3) + jnp.concatenate(tail_c, axis=0)

    def body(q_ref, k_ref, cum_ref, v_ref, do_ref, st_all_ref, *rest):
        g_refs, (dq_ref, dk_ref, dv_ref, dla_ref) = rest[:nw], rest[nw:nw + 4]
        rv_refs, (dst_ref, send_sems, recv_sems) = rest[nw + 4:2 * nw + 4], rest[2 * nw + 4:]
        x, y, c = _my_place()

        def wcopy(a, r):
            dx, dy, dc = FLIPS[r]
            return pltpu.make_async_remote_copy(
                src_ref=g_refs[a].at[4 * (x ^ dx) + 2 * (y ^ dy) + (c ^ dc)], dst_ref=rv_refs[a].at[r],
                send_sem=send_sems.at[a, r], recv_sem=recv_sems.at[a, r],
                device_id=(x ^ dx, y ^ dy, c ^ dc), device_id_type=MESH)

        @pl.when(pl.program_id(0) == 0)
        def _():
            dst_ref[...] = jnp.zeros_like(dst_ref)
            for a in range(nw):
                for r in range(len(FLIPS)):
                    wcopy(a, r).start()

        for b in range(B):
            for tile in reversed(range(tps)):
                tok = pl.ds(tile * GLA_TILE, GLA_TILE)
                chunks = pl.ds(tile * CHUNKS_PER_TILE, CHUNKS_PER_TILE)
                one_sequence(*[r.at[b, tok] for r in (q_ref, k_ref, cum_ref, v_ref, do_ref)], st_all_ref.at[b, chunks],
                             *[r.at[b, tok] for r in (dq_ref, dk_ref, dv_ref, dla_ref)], dst_ref.at[b])

        @pl.when(pl.program_id(0) == steps - 1)
        def _():
            for a in range(nw):
                for r in range(len(FLIPS)):
                    wcopy(a, r).wait()

    def rows(w):
        return pl.BlockSpec((B, tps * GLA_TILE, w), lambda t: (0, steps - 1 - t, 0))

    seq = lambda a: a.reshape(B, S, a.shape[-1])
    res = pl.pallas_call(
        body, name="gla_bwd", grid=(steps,),
        in_specs=[rows(B_KEY_WIDTH), rows(B_KEY_WIDTH), rows(B_KEY_WIDTH), rows(B_WIDTH), rows(B_WIDTH),
                  pl.BlockSpec((B, tps * CHUNKS_PER_TILE, B_VAL_DIM, B_KEY_WIDTH), lambda t: (0, steps - 1 - t, 0, 0))]
                 + _any_specs(nw),
        out_specs=[rows(B_KEY_WIDTH), rows(B_KEY_WIDTH), rows(B_WIDTH), rows(B_KEY_WIDTH)] + _any_specs(nw),
        out_shape=[jax.ShapeDtypeStruct((B, S, B_KEY_WIDTH), BF16), jax.ShapeDtypeStruct((B, S, B_KEY_WIDTH), BF16),
                   jax.ShapeDtypeStruct((B, S, B_WIDTH), BF16), jax.ShapeDtypeStruct((B, S, B_KEY_WIDTH), F32)]
                  + [jax.ShapeDtypeStruct((len(FLIPS), *g.shape[1:]), g.dtype) for g in wgrads],
        scratch_shapes=[pltpu.VMEM((B, B_VAL_DIM, B_KEY_WIDTH), F32),
                        pltpu.SemaphoreType.DMA((nw, len(FLIPS))), pltpu.SemaphoreType.DMA((nw, len(FLIPS)))],
        compiler_params=_params("arbitrary"),
    )(seq(q), seq(k), seq(cum), seq(vb), seq(do), st_all.reshape(B, S // B_CHUNK, B_VAL_DIM, B_KEY_WIDTH), *wgrads)
    return [a.reshape(T, a.shape[-1]) for a in res[:4]], res[4:]


def _merge(x2, tgt2, attn, za, o_gla, zb, ga, gb, w_oa_sh, w_ob_sh, w_o, g_gla, g_final):
    T = x2.shape[0]
    tm = math.gcd(T, 512)
    sub = math.gcd(tm, 256)
    last = T // tm - 1

    def body(x_ref, tgt_ref, attn_ref, za_ref, og_ref, zb_ref, ga_ref, gb_ref,
             woa_sh_ref, wob_sh_ref, wo_ref, gg_ref, gf_ref,
             dxres_ref, dattn_ref, dog_ref, dza_ref, dzb_ref, dga_ref, dgb_ref,
             dwo_out, dwoa_out, dwob_out, small_ref,
             awo_ref, awoa_ref, awob_ref, agf_ref, agg_ref, loss_ref, woa_ref, wob_ref,
             dwo_ref, dwoa_ref, dwob_ref, w_sems, dw_sems):
        w_copies = [pltpu.make_async_copy(sh.at[j], dst.at[:, j * SHARD_OUT:(j + 1) * SHARD_OUT], w_sems.at[a, j])
                    for a, (sh, dst) in enumerate(((woa_sh_ref, woa_ref), (wob_sh_ref, wob_ref))) for j in range(N_DEV)]
        dw_copies = [pltpu.make_async_copy(src, dst, dw_sems.at[a])
                     for a, (src, dst) in enumerate(((dwo_ref, dwo_out), (dwoa_ref, dwoa_out), (dwob_ref, dwob_out)))]

        @pl.when(pl.program_id(0) == 0)
        def _():
            for cp in w_copies:
                cp.start()
            for r in (awo_ref, awoa_ref, awob_ref, agf_ref, agg_ref, loss_ref):
                r[...] = jnp.zeros_like(r)
            for cp in w_copies:
                cp.wait()

        def one_tile(rows):
            za_v = za_ref[rows, :].astype(F32)
            sig_za = _sigmoid_tanh(za_v)
            silu_a = za_v * sig_za
            attn_v = attn_ref[rows, :].astype(F32)
            oa = (attn_v * silu_a).astype(BF16)
            ya = _dot(oa, woa_ref[...])
            og = og_ref[rows, :].astype(F32)
            zb_v = zb_ref[rows, :].astype(F32)
            sig_zb = _sigmoid_tanh(zb_v)
            silu_b = zb_v * sig_zb
            gg = gg_ref[...]
            on_parts, rinv_parts = [], []
            for h in range(B_HEADS):
                seg = og[:, h * B_VAL_DIM:(h + 1) * B_VAL_DIM]
                rinv = lax.rsqrt(jnp.mean(seg * seg, axis=-1, keepdims=True) + NORM_EPS)
                rinv_parts.append(rinv)
                on_parts.append(seg * rinv)
            on = jnp.concatenate(on_parts, axis=1)
            obn = on * gg
            ob = (obn * silu_b).astype(BF16)
            yb = _dot(ob, wob_ref[...])
            sig_a = _sigmoid_tanh(ga_ref[rows, :].astype(F32))
            sig_b = _sigmoid_tanh(gb_ref[rows, :].astype(F32))
            merged = (sig_a * ya + sig_b * yb).astype(BF16)
            out = x_ref[rows, :] + _dot(merged, wo_ref[...])
            rf = lax.rsqrt(jnp.mean(out * out, axis=-1, keepdims=True) + NORM_EPS)
            nrm = out * rf
            gf = gf_ref[...]
            err = nrm * gf - tgt_ref[rows, :]
            loss = jnp.sum(err * err) * (0.5 / D_MODEL)

            dy = err * (1.0 / D_MODEL)
            dgf = jnp.sum(dy * nrm, axis=0, keepdims=True)
            dn = dy * gf
            dout = rf * (dn - nrm * jnp.mean(dn * nrm, axis=-1, keepdims=True))
            dxres_ref[rows, :] = dout
            dout_b = dout.astype(BF16)
            dmerged = _dot_nt(dout_b, wo_ref[...])
            dya = dmerged * sig_a
            dyb = dmerged * sig_b
            dga_ref[rows, :] = (dmerged * ya * sig_a * (1.0 - sig_a)).astype(BF16)
            dgb_ref[rows, :] = (dmerged * yb * sig_b * (1.0 - sig_b)).astype(BF16)
            dya_b, dyb_b = dya.astype(BF16), dyb.astype(BF16)
            doa = _dot_nt(dya_b, woa_ref[...])
            dattn_ref[rows, :] = (doa * silu_a).astype(BF16)
            dza_ref[rows, :] = (doa * attn_v * (sig_za * (1.0 + za_v * (1.0 - sig_za)))).astype(BF16)
            dob = _dot_nt(dyb_b, wob_ref[...])
            dzb_ref[rows, :] = (dob * obn * (sig_zb * (1.0 + zb_v * (1.0 - sig_zb)))).astype(BF16)
            dobn = dob * silu_b
            dgg = jnp.sum(dobn * on, axis=0, keepdims=True)
            don = dobn * gg
            for h in range(B_HEADS):
                sl = slice(h * B_VAL_DIM, (h + 1) * B_VAL_DIM)
                don_h, on_h = don[:, sl], on[:, sl]
                dog_ref[rows, sl] = (rinv_parts[h] * (don_h - on_h * jnp.mean(don_h * on_h, axis=-1, keepdims=True))
                                     ).astype(BF16)
            return (merged, dout_b, oa, dya_b, ob, dyb_b), (loss, dgf, dgg)

        tiles = [one_tile(pl.ds(j * sub, sub)) for j in range(tm // sub)]
        merged, dout_b, oa, dya_b, ob, dyb_b = (jnp.concatenate(parts, axis=0) for parts in zip(*[t[0] for t in tiles]))
        awo_ref[...] += _dot_tn(merged, dout_b)
        awoa_ref[...] += _dot_tn(oa, dya_b)
        awob_ref[...] += _dot_tn(ob, dyb_b)
        for _, (loss, dgf, dgg) in tiles:
            loss_ref[...] += loss
            agf_ref[...] += dgf
            agg_ref[...] += dgg

        @pl.when(pl.program_id(0) == last)
        def _():
            for j in range(N_DEV):
                dwo_ref[j] = awo_ref[j * SHARD_OUT:(j + 1) * SHARD_OUT, :].astype(BF16)
                dwoa_ref[j] = awoa_ref[:, j * SHARD_OUT:(j + 1) * SHARD_OUT].astype(BF16)
                dwob_ref[j] = awob_ref[:, j * SHARD_OUT:(j + 1) * SHARD_OUT].astype(BF16)
            small_ref[...] = jnp.zeros_like(small_ref)
            _put_rows(small_ref, SMALL_G_FINAL, agf_ref[...])
            _put_rows(small_ref, SMALL_G_GLA, agg_ref[...])
            small_ref[SMALL_LOSS:SMALL_LOSS + 1, :] = loss_ref[...]
            for cp in dw_copies:
                cp.start()
            for cp in dw_copies:
                cp.wait()

    def rows(w):
        return pl.BlockSpec((tm, w), lambda i: (i, 0))

    def whole(shape):
        nd = len(shape)
        return pl.BlockSpec(shape, lambda i: (0,) * nd)

    outs = [((T, D_MODEL), F32, rows(D_MODEL)), ((T, A_WIDTH), BF16, rows(A_WIDTH)), ((T, B_WIDTH), BF16, rows(B_WIDTH)),
            ((T, A_WIDTH), BF16, rows(A_WIDTH)), ((T, B_WIDTH), BF16, rows(B_WIDTH)),
            ((T, D_MODEL), BF16, rows(D_MODEL)), ((T, D_MODEL), BF16, rows(D_MODEL)),
            ((N_DEV, SHARD_OUT, D_MODEL), BF16, pl.BlockSpec(memory_space=pl.ANY)),
            ((N_DEV, A_WIDTH, SHARD_OUT), BF16, pl.BlockSpec(memory_space=pl.ANY)),
            ((N_DEV, B_WIDTH, SHARD_OUT), BF16, pl.BlockSpec(memory_space=pl.ANY)),
            ((SMALL_SINKS, LANES), F32, whole((SMALL_SINKS, LANES)))]
    return pl.pallas_call(
        body, name="merge", grid=(T // tm,),
        in_specs=[rows(D_MODEL), rows(D_MODEL), rows(A_WIDTH), rows(A_WIDTH), rows(B_WIDTH), rows(B_WIDTH),
                  rows(D_MODEL), rows(D_MODEL),
                  pl.BlockSpec(memory_space=pl.ANY), pl.BlockSpec(memory_space=pl.ANY),
                  _const_spec((D_MODEL, D_MODEL)), _const_spec((1, B_WIDTH)), _const_spec((1, D_MODEL))],
        out_specs=[o[2] for o in outs],
        out_shape=[jax.ShapeDtypeStruct(o[0], o[1]) for o in outs],
        scratch_shapes=[pltpu.VMEM((D_MODEL, D_MODEL), F32), pltpu.VMEM((A_WIDTH, D_MODEL), F32),
                        pltpu.VMEM((B_WIDTH, D_MODEL), F32), pltpu.VMEM((1, D_MODEL), F32), pltpu.VMEM((1, B_WIDTH), F32),
                        pltpu.VMEM((1, LANES), F32), pltpu.VMEM((A_WIDTH, D_MODEL), BF16),
                        pltpu.VMEM((B_WIDTH, D_MODEL), BF16),
                        pltpu.VMEM((N_DEV, SHARD_OUT, D_MODEL), BF16), pltpu.VMEM((N_DEV, A_WIDTH, SHARD_OUT), BF16),
                        pltpu.VMEM((N_DEV, B_WIDTH, SHARD_OUT), BF16),
                        pltpu.SemaphoreType.DMA((2, N_DEV)), pltpu.SemaphoreType.DMA((3,))],
        compiler_params=pltpu.CompilerParams(dimension_semantics=("arbitrary",), vmem_limit_bytes=V7X_VMEM_LIMIT_MAX),
    )(x2, tgt2, attn, za, o_gla, zb, ga, gb, w_oa_sh, w_ob_sh, w_o, g_gla, g_final)


def _in_proj_bwd(x2, dxres, cosf, sinf, g_in, wt_pad, wa_pad, parts):
    T = x2.shape[0]
    tm = math.gcd(T, 512)
    sub = math.gcd(tm, 256)
    last = T // tm - 1
    base = SMALL_G_IN

    def body(x_ref, dxres_ref, cos_ref, sin_ref, g_ref, wt_ref, wa_ref,
             dq_ref, dkv_ref, dza_ref, dqb_ref, dkb_ref, dvb_ref, dzb_ref, dla_ref, u_ref, alr_ref, dga_ref, dgb_ref,
             dx_ref, dsh_ref, small_ref, dproj_ref, agin_ref, aba_ref, awa_ref):
        @pl.when(pl.program_id(0) == 0)
        def _():
            for r in (agin_ref, aba_ref, awa_ref):
                r[...] = jnp.zeros_like(r)

        def one_tile(rows):
            cos, nsin = cos_ref[rows, :], -sin_ref[rows, :]
            for s in range(A_WIDTH // LANES):
                sl = slice(s * LANES, (s + 1) * LANES)
                dproj_ref[rows, sl] = _rope_slab(dq_ref[rows, sl].astype(F32), cos, nsin).astype(BF16)
            dproj_ref[rows, QKV_K:QKV_V] = _rope_slab(dkv_ref[rows, 0:LANES].astype(F32), cos, nsin).astype(BF16)
            dproj_ref[rows, QKV_V:QKV_W] = dkv_ref[rows, LANES:]

            def put(name, val):
                a, b = SEG[name]
                dproj_ref[rows, a:b] = val

            put("za", dza_ref[rows, :])
            put("qb", dqb_ref[rows, :])
            put("kb", dkb_ref[rows, :])
            put("vb", dvb_ref[rows, :])
            put("zb", dzb_ref[rows, :])
            put("ga", dga_ref[rows, :])
            put("gb", dgb_ref[rows, :])
            du = dla_ref[rows, :] * (1.0 / B_GATE_TEMP) * _sigmoid(-u_ref[rows, :])
            du_b = du.astype(BF16)
            put("alr", _dot_nt(du_b, wa_ref[...]).astype(BF16))

            for j in range(N_DEV):
                col = (j % 2) * SHARD_PAD
                for a, b in _shard_pad_cols(j):
                    dsh_ref[j // 2, rows, col:col + b - a] = dproj_ref[rows, a:b]
                    col += b - a
                dsh_ref[j // 2, rows, col:(j % 2 + 1) * SHARD_PAD] = jnp.zeros((sub, SHARD_PAD - SHARD_IN), BF16)

            dh = _dot(dproj_ref[rows, :], wt_ref[...])
            x = x_ref[rows, :]
            r = lax.rsqrt(jnp.mean(x * x, axis=-1, keepdims=True) + NORM_EPS)
            nrm = x * r
            dn = dh * g_ref[...]
            dx_ref[rows, :] = dxres_ref[rows, :] + r * (dn - nrm * jnp.mean(dn * nrm, axis=-1, keepdims=True))
            return jnp.sum(dh * nrm, axis=0, keepdims=True), jnp.sum(du, axis=0, keepdims=True), alr_ref[rows, :], du_b

        for j in range(tm // sub):
            dgin, dba, alr, du_b = one_tile(pl.ds(j * sub, sub))
            agin_ref[...] += dgin
            aba_ref[...] += dba
            awa_ref[...] += _dot_tn(alr, du_b)

        @pl.when(pl.program_id(0) == last)
        def _():
            small_ref[...] = jnp.zeros_like(small_ref)
            _put_rows(small_ref, SMALL_G_IN - base, agin_ref[...])
            _put_rows(small_ref, SMALL_B_ALPHA - base, aba_ref[...])
            for half in range(B_KEY_WIDTH // LANES):
                r0 = SMALL_W_ALPHA - base + half * B_GATE_RANK
                small_ref[r0:r0 + B_GATE_RANK, :] = awa_ref[0:B_GATE_RANK, half * LANES:(half + 1) * LANES]

    def rows(w):
        return pl.BlockSpec((tm, w), lambda i: (i, 0))

    names = ["dq", "dkv", "dza", "dqb", "dkb", "dvb", "dzb", "dla", "u", "alr", "dga", "dgb"]
    return pl.pallas_call(
        body, name="in_proj_bwd", grid=(T // tm,),
        in_specs=[rows(D_MODEL), rows(D_MODEL), rows(LANES), rows(LANES), _const_spec((1, D_MODEL)),
                  _const_spec((D_IN_PAD, D_MODEL)), _const_spec((RANK_PAD, B_KEY_WIDTH))]
                 + [rows(parts[n].shape[1]) for n in names],
        out_specs=[rows(D_MODEL), pl.BlockSpec((N_CHIPS, tm, 2 * SHARD_PAD), lambda i: (0, i, 0)),
                   pl.BlockSpec((SMALL_ROWS - base, LANES), lambda i: (0, 0))],
        out_shape=[jax.ShapeDtypeStruct((T, D_MODEL), F32), jax.ShapeDtypeStruct((N_CHIPS, T, 2 * SHARD_PAD), BF16),
                   jax.ShapeDtypeStruct((SMALL_ROWS - base, LANES), F32)],
        scratch_shapes=[pltpu.VMEM((tm, D_IN_PAD), BF16), pltpu.VMEM((1, D_MODEL), F32), pltpu.VMEM((1, B_KEY_WIDTH), F32),
                        pltpu.VMEM((RANK_PAD, B_KEY_WIDTH), F32)],
        compiler_params=pltpu.CompilerParams(dimension_semantics=("arbitrary",), vmem_limit_bytes=V7X_VMEM_LIMIT_MAX),
    )(x2, dxres, cosf, sinf, g_in, wt_pad, wa_pad, *[parts[n] for n in names])


FLIPS = [(dx, dy, dc) for dx in (0, 1) for dy in (0, 1) for dc in (0, 1)][1:]


def _my_place():
    return lax.axis_index("x"), lax.axis_index("y"), lax.axis_index("c")


def _any_specs(n):
    return [pl.BlockSpec(memory_space=pl.ANY)] * n


def _gather_first(shards, pos_col):
    n = len(shards)
    T = pos_col.shape[0]
    rows_per_pass = math.gcd(T, 512)
    invf, sign = _rope_lane_constants()

    def body(*refs):
        ins, (pos_ref, invf_ref, sign_ref) = refs[:n], refs[n:n + 3]
        outs, (cos_ref, sin_ref) = refs[n + 3:2 * n + 3], refs[2 * n + 3:2 * n + 5]
        send_sems, recv_sems, local_sems = refs[2 * n + 5:]
        x, y, c = _my_place()
        me, sibling = (x, y, c), (x, y, 1 - c)
        chips = [(1 - x, y), (x, 1 - y), (1 - x, 1 - y)]

        def block(a, px, py, pc):
            return outs[a].at[4 * px + 2 * py + pc]

        def copy(a, k, blk, to, src=None):
            return pltpu.make_async_remote_copy(
                src_ref=block(a, *blk) if src is None else src, dst_ref=block(a, *blk),
                send_sem=send_sems.at[a, k], recv_sem=recv_sems.at[a, k], device_id=to, device_id_type=MESH)

        mine = [pltpu.make_async_copy(ins[a], block(a, *me), local_sems.at[a]) for a in range(n)]
        for cp in mine:
            cp.start()
        first = []
        for a in range(n):
            first.append(copy(a, 0, me, sibling, src=ins[a]))
            first += [copy(a, 1 + j, me, (*chip, c), src=ins[a]) for j, chip in enumerate(chips)]
        for cp in first:
            cp.start()

        def tables(i, carry):
            rows = pl.ds(pl.multiple_of(i * rows_per_pass, rows_per_pass), rows_per_pass)
            ang = pos_ref[rows, :].astype(F32) * invf_ref[...]
            cos_ref[rows, :] = jnp.cos(ang)
            sin_ref[rows, :] = jnp.sin(ang) * sign_ref[...]
            return carry

        lax.fori_loop(0, T // rows_per_pass, tables, 0)

        passed = []
        for j, chip in enumerate(chips):
            for a in range(n):
                copy(a, 1 + j, (*chip, c), me).wait_recv()
                fwd = copy(a, 4 + j, (*chip, c), sibling)
                fwd.start()
                passed.append(fwd)
        for a in range(n):
            copy(a, 0, sibling, me).wait_recv()
            for j, chip in enumerate(chips):
                copy(a, 4 + j, (*chip, 1 - c), me).wait_recv()
        for cp in first + passed:
            cp.wait_send()
        for cp in mine:
            cp.wait()

    vmem = pl.BlockSpec(memory_space=pltpu.VMEM)
    res = pl.pallas_call(
        body, name="gather_weights",
        in_specs=_any_specs(n) + [vmem] * 3, out_specs=_any_specs(n) + [vmem] * 2,
        out_shape=[jax.ShapeDtypeStruct((N_DEV, *s.shape), s.dtype) for s in shards]
                  + [jax.ShapeDtypeStruct((T, LANES), F32)] * 2,
        scratch_shapes=[pltpu.SemaphoreType.DMA((n, 7)), pltpu.SemaphoreType.DMA((n, 7)), pltpu.SemaphoreType.DMA((n,))],
        compiler_params=pltpu.CompilerParams(vmem_limit_bytes=V7X_VMEM_LIMIT),
    )(*shards, pos_col, invf, sign)
    return res[:n], res[n], res[n + 1]


def _w_in_grad_rs(h, dsh, chip_order, small):
    T = h.shape[0]
    tk = math.gcd(T, 2048)
    nk = T // tk
    chip_flips = [(1, 1), (1, 0), (0, 1)]
    n_steps = len(chip_flips) + 1
    SIB = len(chip_flips)
    k_finish = min(1, nk - 1)

    def body(order_ref, h_ref, d_ref, s_ref, own_ref, recv_ref, sall_ref,
             acc_ref, keep_ref, pre_ref, to_sib_ref, to_chip_ref,
             sib_send, sib_recv, chip_send, chip_recv, ssend_sems, srecv_sems, local_sem):
        i, kk = pl.program_id(0), pl.program_id(1)
        x, y, c = _my_place()
        my_dev = 4 * x + 2 * y + c

        def small_copy(r, slot):
            dx, dy, dc = FLIPS[r]
            return pltpu.make_async_remote_copy(
                src_ref=s_ref, dst_ref=sall_ref.at[slot], send_sem=ssend_sems.at[r], recv_sem=srecv_sems.at[r],
                device_id=(x ^ dx, y ^ dy, c ^ dc), device_id_type=MESH)

        keep_small = pltpu.make_async_copy(s_ref, sall_ref.at[my_dev], local_sem)

        def sib_copy(t):
            dst = recv_ref.at[SIB] if t == SIB else pre_ref.at[t]
            return pltpu.make_async_remote_copy(
                src_ref=to_sib_ref.at[t], dst_ref=dst, send_sem=sib_send.at[t], recv_sem=sib_recv.at[t],
                device_id=(x, y, 1 - c), device_id_type=MESH)

        def chip_copy(t):
            dx, dy = chip_flips[t]
            return pltpu.make_async_remote_copy(
                src_ref=to_chip_ref.at[t], dst_ref=recv_ref.at[t], send_sem=chip_send.at[t], recv_sem=chip_recv.at[t],
                device_id=(x ^ dx, y ^ dy, c), device_id_type=MESH)

        def halves():
            first, second = acc_ref[0:SHARD_PAD, :], acc_ref[SHARD_PAD:2 * SHARD_PAD, :]
            return jnp.where(c == 0, first, second), jnp.where(c == 0, second, first)

        @pl.when((i == 0) & (kk == 0))
        def _():
            keep_small.start()
            for r in range(len(FLIPS)):
                small_copy(r, my_dev).start()

        @pl.when(kk == 0)
        def _():
            acc_ref[...] = jnp.zeros_like(acc_ref)

        acc_ref[...] += _dot_tn(d_ref[...], h_ref[...])

        for t in range(len(chip_flips)):
            @pl.when((i == t + 1) & (kk == k_finish))
            def _(t=t):
                sib_copy(t).wait_recv()
                to_chip_ref[t] = (keep_ref[...] + pre_ref[t].astype(F32)).astype(BF16)
                chip_copy(t).start()

        for t in range(len(chip_flips)):
            @pl.when((i == t) & (kk == nk - 1))
            def _(t=t):
                mine, theirs = halves()
                to_sib_ref[t] = theirs.astype(BF16)
                sib_copy(t).start()
                keep_ref[...] = mine

        @pl.when((i == n_steps - 1) & (kk == nk - 1))
        def _():
            mine, theirs = halves()
            own_ref[...] = mine
            to_sib_ref[SIB] = theirs.astype(BF16)
            sib_copy(SIB).start()
            for t in range(len(chip_flips)):
                sib_copy(t).wait_send()
                chip_copy(t).wait_send()
                chip_copy(t).wait_recv()
            sib_copy(SIB).wait_send()
            sib_copy(SIB).wait_recv()
            for r, (dx, dy, dc) in enumerate(FLIPS):
                small_copy(r, 4 * (x ^ dx) + 2 * (y ^ dy) + (c ^ dc)).wait_recv()
                small_copy(r, my_dev).wait_send()
            keep_small.wait()

    shard = (SHARD_PAD, D_MODEL)
    return pl.pallas_call(
        body, name="w_in_grad_rs",
        grid_spec=pltpu.PrefetchScalarGridSpec(
            num_scalar_prefetch=1, grid=(n_steps, nk),
            in_specs=[pl.BlockSpec((tk, D_MODEL), lambda i, kk, order: (kk, 0)),
                      pl.BlockSpec((None, tk, 2 * SHARD_PAD), lambda i, kk, order: (order[i], kk, 0)),
                      pl.BlockSpec(memory_space=pl.ANY)],
            out_specs=[pl.BlockSpec(shard, lambda i, kk, order: (0, 0)),
                       pl.BlockSpec(memory_space=pl.ANY), pl.BlockSpec(memory_space=pl.ANY)],
            scratch_shapes=[pltpu.VMEM((2 * SHARD_PAD, D_MODEL), F32), pltpu.VMEM(shard, F32),
                            pltpu.VMEM((SIB, *shard), BF16), pltpu.VMEM((SIB + 1, *shard), BF16),
                            pltpu.VMEM((SIB, *shard), BF16),
                            pltpu.SemaphoreType.DMA((SIB + 1,)), pltpu.SemaphoreType.DMA((SIB + 1,)),
                            pltpu.SemaphoreType.DMA((SIB,)), pltpu.SemaphoreType.DMA((SIB,)),
                            pltpu.SemaphoreType.DMA((7,)), pltpu.SemaphoreType.DMA((7,)), pltpu.SemaphoreType.DMA]),
        out_shape=[jax.ShapeDtypeStruct(shard, F32),
                   jax.ShapeDtypeStruct((SIB + 1, *shard), BF16),
                   jax.ShapeDtypeStruct((N_DEV, *small.shape), F32)],
        compiler_params=_params("arbitrary", "arbitrary"),
    )(chip_order, h, dsh, small)


def _adam_math(w, g, m, v):
    m_new = ADAM_B1 * m + (1.0 - ADAM_B1) * g
    v_new = ADAM_B2 * v + (1.0 - ADAM_B2) * (g * g)
    m_hat = m_new / (1.0 - ADAM_B1 ** ADAM_STEP)
    v_hat = v_new / (1.0 - ADAM_B2 ** ADAM_STEP)
    delta = -ADAM_LR * (m_hat / (jnp.sqrt(v_hat) + ADAM_EPS) + ADAM_WD * w)
    return delta, m_new, v_new


def _adam_big(jobs):
    steps = 8
    n = len(jobs)
    idx = jnp.stack([job[1] for job in jobs]).astype(jnp.int32)
    blocks = []
    for own, _, recv, w, m, v in jobs:
        (rw, cw), rp = w.shape, own.shape[1]
        by_cols = rp != rw
        blk_w = (rw, cw // steps) if by_cols else (rw // steps, cw)
        blk_g = (rp, cw // steps) if by_cols else (rw // steps, cw)
        blocks.append((blk_w, blk_g, by_cols))

    def body(idx_ref, *refs):
        ins, outs = refs[:5 * n], refs[5 * n:]
        for j, (blk_w, _, _) in enumerate(blocks):
            o_ref, r_ref, w_ref, m_ref, v_ref = ins[5 * j:5 * j + 5]
            g_ref, d_ref, mo_ref, vo_ref = outs[4 * j:4 * j + 4]
            g = o_ref[...].astype(F32)
            for r in range(r_ref.shape[0]):
                g = g + r_ref[r].astype(F32)
            g = g[0:blk_w[0], :]
            g_ref[...] = g
            d_ref[...], mo_ref[...], vo_ref[...] = _adam_math(w_ref[...], g, m_ref[...], v_ref[...])

    in_specs, out_specs, out_shape, args = [], [], [], []
    for j, ((own, _, recv, w, m, v), (blk_w, blk_g, by_cols)) in enumerate(zip(jobs, blocks)):
        at = (lambda i: (0, i)) if by_cols else (lambda i: (i, 0))
        spec = pl.BlockSpec(blk_w, lambda i, idx_ref, at=at: at(i))
        in_specs += [pl.BlockSpec((None, *blk_g), lambda i, idx_ref, at=at, j=j: (idx_ref[j], *at(i))),
                     pl.BlockSpec((recv.shape[0], *blk_g), lambda i, idx_ref, at=at: (0, *at(i))), spec, spec, spec]
        out_specs += [spec] * 4
        out_shape += [jax.ShapeDtypeStruct(w.shape, F32)] * 4
        args += [own, recv, w, m, v]
    res = pl.pallas_call(
        body, name="adam_big",
        grid_spec=pltpu.PrefetchScalarGridSpec(num_scalar_prefetch=1, grid=(steps,), in_specs=in_specs, out_specs=out_specs),
        out_shape=out_shape,
        compiler_params=_params("parallel"),
    )(idx, *args)
    return [res[4 * j:4 * j + 4] for j in range(n)]


def _adam_small(small_all, params):
    flat = [a for triple in params for a in triple]
    n_par = len(params)

    def body(s_ref, *refs):
        ins, outs, loss_ref = refs[:3 * n_par], refs[3 * n_par:-1], refs[-1]
        g_slab = s_ref[0]
        for dev in range(1, N_DEV):
            g_slab = g_slab + s_ref[dev]
        loss_ref[...] = g_slab[SMALL_LOSS:SMALL_LOSS + 1, :]
        dev = 4 * lax.axis_index("x") + 2 * lax.axis_index("y") + lax.axis_index("c")
        alpha_full = jnp.concatenate([g_slab[SMALL_W_ALPHA + half * B_GATE_RANK:SMALL_W_ALPHA + (half + 1) * B_GATE_RANK]
                                      for half in range(B_KEY_WIDTH // LANES)], axis=1)
        alpha_mine = pltpu.roll(alpha_full, (B_KEY_WIDTH - dev * SHARD_ALPHA) % B_KEY_WIDTH, 1)[:, 0:SHARD_ALPHA]
        grads = [_take_rows(g_slab, SMALL_G_IN, D_MODEL // LANES), _take_rows(g_slab, SMALL_G_FINAL, D_MODEL // LANES),
                 _take_rows(g_slab, SMALL_G_GLA, B_WIDTH // LANES), _take_rows(g_slab, SMALL_B_ALPHA, B_KEY_WIDTH // LANES),
                 g_slab[SMALL_SINKS:SMALL_SINKS + 1, 0:A_HEADS], alpha_mine]
        for i, g in enumerate(grads):
            w_ref, m_ref, v_ref = ins[3 * i:3 * i + 3]
            delta, m_new, v_new = _adam_math(w_ref[...], g, m_ref[...], v_ref[...])
            outs[4 * i][...] = g
            outs[4 * i + 1][...] = delta
            outs[4 * i + 2][...] = m_new
            outs[4 * i + 3][...] = v_new

    res = pl.pallas_call(
        body, name="adam_small",
        out_shape=[jax.ShapeDtypeStruct(t[0].shape, F32) for t in params for _ in range(4)]
                  + [jax.ShapeDtypeStruct((1, LANES), F32)],
    )(small_all, *flat)
    return [res[4 * i:4 * i + 4] for i in range(n_par)], res[-1]


def _local_step(x, cosf, sinf, loss_target, g_in, wt_sh, wa_pad, b_alpha, sinks, g_gla, out_shards, g_final, chip_order):
    B, S, _ = x.shape
    T = B * S
    x2 = x.reshape(T, D_MODEL)
    tgt2 = loss_target.reshape(T, D_MODEL)
    f, (g_woa, g_wob, g_wo) = _in_proj(x2, cosf, sinf, g_in, wt_sh, wa_pad, b_alpha, out_shards)
    w_o = g_wo.reshape(D_MODEL, D_MODEL)
    sink_row = jnp.repeat(sinks, WINDOW).reshape(1, ATT_ROWS)
    sink_col = sink_row.reshape(ATT_ROWS, 1)
    attn, lse = _attn_fwd(f["qkv"], sink_row, B, S)
    o_gla, st_all = _gla_fwd(f["q"], f["k"], f["cum"], f["vb"], B, S)
    (dxres, dattn, dog, dza, dzb, dga, dgb, dw_o, dw_oa, dw_ob, small_a) = _merge(
        x2, tgt2, attn, f["za"], o_gla, f["zb"], f["ga"], f["gb"], g_woa, g_wob, w_o, g_gla, g_final)
    dq, dkv, dsink = _attn_bwd(f["qkv"], dattn, attn, lse, sink_col, B, S)
    (dqb, dkb, dvb, dla), (rv_o, rv_oa, rv_ob) = _gla_bwd(f["q"], f["k"], f["cum"], f["vb"], dog, st_all, B, S,
                                                        [dw_o, dw_oa, dw_ob])
    parts = dict(dq=dq, dkv=dkv, dza=dza, dqb=dqb, dkb=dkb, dvb=dvb, dzb=dzb, dla=dla, u=f["u"], alr=f["alr"],
                 dga=dga, dgb=dgb)
    dx, dsh, small_c = _in_proj_bwd(x2, dxres, cosf, sinf, g_in, f["wt_pad"], wa_pad, parts)
    small = jnp.concatenate([small_a, dsink, small_c], axis=0)
    own_in, rv_in, small_all = _w_in_grad_rs(f["h"], dsh, chip_order, small)
    return dict(grad_x=dx.reshape(B, S, D_MODEL), own_in=own_in, rv_in=rv_in,
                own_o=dw_o, rv_o=rv_o, own_oa=dw_oa, rv_oa=rv_oa, own_ob=dw_ob, rv_ob=rv_ob, small_all=small_all)


def kernel(x, positions, g_in, w_in, w_alpha_up, b_alpha, attn_sinks, g_gla_norm, w_out_a, w_out_b, w_o, g_final, loss_target, m_g_in, m_w_in, m_w_alpha_up, m_b_alpha, m_attn_sinks, m_g_gla_norm, m_w_out_a, m_w_out_b, m_w_o, m_g_final, v_g_in, v_w_in, v_w_alpha_up, v_b_alpha, v_attn_sinks, v_g_gla_norm, v_w_out_a, v_w_out_b, v_w_o, v_g_final):
    xi, yi, ci = _my_place()
    chip = 2 * xi + yi
    chip_order = jnp.stack([chip ^ 3, chip ^ 2, chip ^ 1, chip]).astype(jnp.int32)

    (g_win, g_wa), cosf, sinf = _gather_first(
        [jnp.pad(w_in[0].T.astype(BF16), ((0, SHARD_PAD - SHARD_IN), (0, 0))), w_alpha_up[0].astype(BF16)],
        positions.reshape(-1, 1))
    wt_sh = g_win.reshape(N_DEV * SHARD_PAD, D_MODEL)
    wa_pad = jnp.pad(jnp.concatenate([g_wa[j] for j in range(N_DEV)], axis=1), ((0, RANK_PAD - B_GATE_RANK), (0, 0)))

    r = _local_step(x, cosf, sinf, loss_target, g_in, wt_sh, wa_pad, b_alpha, attn_sinks[0], g_gla_norm,
                    [w_out_a[0].astype(BF16), w_out_b[0].astype(BF16), w_o[0].astype(BF16)],
                    g_final.reshape(1, D_MODEL), chip_order)

    dev = 4 * xi + 2 * yi + ci
    big = _adam_big([(r["own_in"][None], jnp.int32(0), r["rv_in"], w_in[0].T, m_w_in[0].T, v_w_in[0].T),
                     (r["own_oa"], dev, r["rv_oa"], w_out_a[0], m_w_out_a[0], v_w_out_a[0]),
                     (r["own_ob"], dev, r["rv_ob"], w_out_b[0], m_w_out_b[0], v_w_out_b[0]),
                     (r["own_o"], dev, r["rv_o"], w_o[0], m_w_o[0], v_w_o[0])])
    big[0] = [a.T for a in big[0]]
    row = lambda a: a.reshape(1, D_MODEL)
    (s_in, s_final, s_gla, s_ba, s_sinks, s_wa), loss_row = _adam_small(r["small_all"], [
        (g_in, m_g_in, v_g_in), (row(g_final), row(m_g_final), row(v_g_final)),
        (g_gla_norm, m_g_gla_norm, v_g_gla_norm), (b_alpha, m_b_alpha, v_b_alpha),
        (attn_sinks, m_attn_sinks, v_attn_sinks), (w_alpha_up[0], m_w_alpha_up[0], v_w_alpha_up[0])])

    def group(i):
        return (s_in[i], big[0][i][None], s_wa[i][None], s_ba[i], s_sinks[i], s_gla[i], big[1][i][None], big[2][i][None],
                big[3][i][None], s_final[i].reshape(D_MODEL))

    return (loss_row[0, 0], r["grad_x"], *group(0), *group(1), *group(2), *group(3))
```

```python
import functools
import math

import numpy as np
import jax
import jax.numpy as jnp
from jax import lax
from jax.experimental import pallas as pl
from jax.experimental.pallas import tpu as pltpu

F32 = jnp.float32
BF16 = jnp.bfloat16
MESH = pl.DeviceIdType.MESH

D_MODEL = 1024
A_HEADS, A_KV_HEADS, A_HEAD_DIM = 8, 2, 64
A_WIDTH, A_KV_WIDTH = 512, 128
WINDOW = 128
ROPE_THETA = 500000.0
ROPE_DIM = 16
B_HEADS, B_KEY_DIM, B_VAL_DIM = 4, 64, 128
B_KEY_WIDTH, B_WIDTH = 256, 512
B_GATE_RANK = 16
B_GATE_TEMP = 16.0
B_CHUNK = 64
NORM_EPS = 1e-6
NEG_BIG = -1e30
D_IN = 4880
N_DEV = 8
N_CHIPS = 4
ADAM_LR, ADAM_B1, ADAM_B2, ADAM_EPS, ADAM_WD, ADAM_STEP = 0.001, 0.9, 0.999, 1e-08, 0.01, 10

LANES = 128
V7X_VMEM_LIMIT = 56 * 1024 * 1024
V7X_VMEM_LIMIT_MAX = 62 * 1024 * 1024

RANK_PAD = LANES
SEG = {}
_off = 0
for _name, _w in (("qa", 512), ("ka", 128), ("va", 128), ("za", 512), ("qb", 256), ("kb", 256),
                  ("vb", 512), ("zb", 512), ("alr", RANK_PAD), ("ga", 1024), ("gb", 1024)):
    SEG[_name] = (_off, _off + _w)
    _off += _w
D_IN_PAD = _off
ALR_SRC = SEG["alr"][0]
QKV_K, QKV_V, QKV_W = SEG["ka"][0], SEG["va"][0], SEG["va"][1]
ATT_SCALE = A_HEAD_DIM ** -0.5

SHARD_IN = D_IN // N_DEV
SHARD_PAD = 640
SHARD_OUT = D_MODEL // N_DEV
SHARD_ALPHA = B_KEY_WIDTH // N_DEV

SMALL_G_FINAL, SMALL_G_GLA, SMALL_LOSS, SMALL_SINKS, SMALL_G_IN, SMALL_B_ALPHA, SMALL_W_ALPHA = 0, 8, 12, 16, 24, 32, 40
SMALL_ROWS = 72


def _dot(a, b):
    return jnp.dot(a, b, preferred_element_type=F32)


def _dot_nt(a, b):
    return lax.dot_general(a, b, (((1,), (1,)), ((), ())), preferred_element_type=F32)


def _dot_tn(a, b):
    return lax.dot_general(a, b, (((0,), (0,)), ((), ())), preferred_element_type=F32)


def _sigmoid(z):
    return 1.0 / (1.0 + jnp.exp(-z))


def _sigmoid_tanh(z):
    return 0.5 * jnp.tanh(0.5 * z) + 0.5


def _params(*sem):
    return pltpu.CompilerParams(dimension_semantics=sem, vmem_limit_bytes=V7X_VMEM_LIMIT)


def _const_spec(shape):
    nd = len(shape)
    return pl.BlockSpec(shape, lambda *_: (0,) * nd, pipeline_mode=pl.Buffered(1))


def _lane_iota(shape):
    return lax.broadcasted_iota(jnp.int32, shape, 1)


def _row_iota(shape):
    return lax.broadcasted_iota(jnp.int32, shape, 0)


def _split3(v):
    hi = v.astype(BF16)
    r1 = v - hi.astype(F32)
    mid = r1.astype(BF16)
    lo = (r1 - mid.astype(F32)).astype(BF16)
    return hi, mid, lo


def _put_rows(ref, row0, vec):
    for r in range(vec.shape[1] // LANES):
        ref[row0 + r:row0 + r + 1, :] = vec[:, r * LANES:(r + 1) * LANES]


def _take_rows(slab, row0, n):
    return jnp.concatenate([slab[row0 + r:row0 + r + 1, :] for r in range(n)], axis=1)


def _rope_lane_constants():
    half = ROPE_DIM // 2
    inv_freq = np.exp(-math.log(ROPE_THETA) * np.arange(half, dtype=np.float32) * np.float32(2.0 / ROPE_DIM)).astype(np.float32)
    lane = np.arange(LANES)
    j = lane % A_HEAD_DIM
    invf = np.where(j < ROPE_DIM, inv_freq[j % half], 0.0).astype(np.float32)
    sign = np.where(j < half, -1.0, np.where(j < ROPE_DIM, 1.0, 0.0)).astype(np.float32)
    return jnp.asarray(invf)[None, :], jnp.asarray(sign)[None, :]


def _rope_slab(t, cos, sin_signed):
    first = (_lane_iota(t.shape) % A_HEAD_DIM) < (ROPE_DIM // 2)
    partner = jnp.where(first, pltpu.roll(t, LANES - ROPE_DIM // 2, 1), pltpu.roll(t, ROPE_DIM // 2, 1))
    return t * cos + partner * sin_signed


def _shard_pad_cols(j):
    cut = ALR_SRC + B_GATE_RANK
    shift = RANK_PAD - B_GATE_RANK
    a, b = j * SHARD_IN, (j + 1) * SHARD_IN
    if b <= cut:
        return [(a, b)]
    if a >= cut:
        return [(a + shift, b + shift)]
    return [(a, cut), (cut + shift, b + shift)]


def _in_proj(x2, cosf, sinf, g_in, wt_sh, wa_pad, b_alpha, later_shards):
    T = x2.shape[0]
    tm = math.gcd(T, 512)
    sub = math.gcd(tm, 256)
    last = T // tm - 1
    nl = len(later_shards)

    def body(x_ref, cos_ref, sin_ref, g_ref, wsh_ref, wa_ref, ba_ref, *rest):
        sh_refs, rest = rest[:nl], rest[nl:]
        (h_ref, qkv_ref, za_ref, q_ref, k_ref, vb_ref, zb_ref, alr_ref, u_ref, cum_ref, ga_ref, gb_ref, wt_out) = rest[:13]
        all_refs, (wt_ref, send_sems, recv_sems, local_sems, wt_sem) = rest[13:13 + nl], rest[13 + nl:]
        wt_copy = pltpu.make_async_copy(wt_ref, wt_out, wt_sem)
        px, py, pc = _my_place()
        my_dev = 4 * px + 2 * py + pc

        def wcopy(a, r, slot):
            dx, dy, dc = FLIPS[r]
            return pltpu.make_async_remote_copy(
                src_ref=sh_refs[a], dst_ref=all_refs[a].at[slot], send_sem=send_sems.at[a, r],
                recv_sem=recv_sems.at[a, r], device_id=(px ^ dx, py ^ dy, pc ^ dc), device_id_type=MESH)

        keep = [pltpu.make_async_copy(sh_refs[a], all_refs[a].at[my_dev], local_sems.at[a]) for a in range(nl)]

        @pl.when(pl.program_id(0) == 0)
        def _():
            for a in range(nl):
                keep[a].start()
                for r in range(len(FLIPS)):
                    wcopy(a, r, my_dev).start()

        @pl.when(pl.program_id(0) == 0)
        def _():
            for j in range(N_DEV):
                src = j * SHARD_PAD
                for a, b in _shard_pad_cols(j):
                    wt_ref[a:b, :] = wsh_ref[src:src + b - a, :]
                    src += b - a
            a, b = SEG["alr"]
            wt_ref[a + B_GATE_RANK:b, :] = jnp.zeros((RANK_PAD - B_GATE_RANK, D_MODEL), BF16)
            wt_copy.start()

        def one_tile(rows):
            x = x_ref[rows, :]
            r = lax.rsqrt(jnp.mean(x * x, axis=-1, keepdims=True) + NORM_EPS)
            h = (x * r * g_ref[...]).astype(BF16)
            h_ref[rows, :] = h

            def seg(name):
                a, b = SEG[name]
                return _dot_nt(h, wt_ref[a:b, :])

            alr = seg("alr").astype(BF16)
            alr_ref[rows, :] = alr
            u = _dot(alr, wa_ref[...]) + ba_ref[...]
            u_ref[rows, :] = u
            log_a = (jnp.minimum(u, 0.0) - jnp.log(1.0 + jnp.exp(-jnp.abs(u)))) * (1.0 / B_GATE_TEMP)
            row, col = _row_iota((sub, sub)), _lane_iota((sub, sub))
            tri = ((row // B_CHUNK == col // B_CHUNK) & (col <= row)).astype(BF16)
            hi, mid, lo = _split3(log_a)
            cum_ref[rows, :] = _dot(tri, hi) + _dot(tri, mid) + _dot(tri, lo)

            cos, sin = cos_ref[rows, :], sin_ref[rows, :]
            qa = seg("qa") * ATT_SCALE
            for s in range(A_WIDTH // LANES):
                qkv_ref[rows, s * LANES:(s + 1) * LANES] = _rope_slab(qa[:, s * LANES:(s + 1) * LANES], cos, sin).astype(BF16)
            qkv_ref[rows, QKV_K:QKV_V] = _rope_slab(seg("ka"), cos, sin).astype(BF16)
            qkv_ref[rows, QKV_V:QKV_W] = seg("va").astype(BF16)
            za_ref[rows, :] = seg("za").astype(BF16)
            q_ref[rows, :] = seg("qb")
            k_ref[rows, :] = seg("kb")
            vb_ref[rows, :] = seg("vb").astype(BF16)
            zb_ref[rows, :] = seg("zb").astype(BF16)
            ga_ref[rows, :] = seg("ga").astype(BF16)
            gb_ref[rows, :] = seg("gb").astype(BF16)

        for j in range(tm // sub):
            one_tile(pl.ds(j * sub, sub))

        @pl.when(pl.program_id(0) == last)
        def _():
            for a in range(nl):
                for r, (dx, dy, dc) in enumerate(FLIPS):
                    wcopy(a, r, 4 * (px ^ dx) + 2 * (py ^ dy) + (pc ^ dc)).wait_recv()
                    wcopy(a, r, my_dev).wait_send()
                keep[a].wait()
            wt_copy.wait()

    def rows(w):
        return pl.BlockSpec((tm, w), lambda i: (i, 0))

    outs = [("h", D_MODEL, BF16), ("qkv", QKV_W, BF16), ("za", A_WIDTH, BF16), ("q", B_KEY_WIDTH, F32),
            ("k", B_KEY_WIDTH, F32), ("vb", B_WIDTH, BF16), ("zb", B_WIDTH, BF16), ("alr", RANK_PAD, BF16),
            ("u", B_KEY_WIDTH, F32), ("cum", B_KEY_WIDTH, F32), ("ga", D_MODEL, BF16), ("gb", D_MODEL, BF16)]
    res = pl.pallas_call(
        body, name="in_proj", grid=(T // tm,),
        in_specs=[rows(D_MODEL), rows(LANES), rows(LANES), _const_spec((1, D_MODEL)),
                  _const_spec((N_DEV * SHARD_PAD, D_MODEL)), _const_spec((RANK_PAD, B_KEY_WIDTH)),
                  _const_spec((1, B_KEY_WIDTH))] + _any_specs(nl),
        out_specs=[rows(w) for _, w, _ in outs] + _any_specs(1 + nl),
        out_shape=[jax.ShapeDtypeStruct((T, w), dt) for _, w, dt in outs]
                  + [jax.ShapeDtypeStruct((D_IN_PAD, D_MODEL), BF16)]
                  + [jax.ShapeDtypeStruct((N_DEV, *sh.shape), sh.dtype) for sh in later_shards],
        scratch_shapes=[pltpu.VMEM((D_IN_PAD, D_MODEL), BF16),
                        pltpu.SemaphoreType.DMA((nl, len(FLIPS))), pltpu.SemaphoreType.DMA((nl, len(FLIPS))),
                        pltpu.SemaphoreType.DMA((nl,)), pltpu.SemaphoreType.DMA],
        compiler_params=_params("arbitrary"),
    )(x2, cosf, sinf, g_in, wt_sh, wa_pad, b_alpha, *later_shards)
    n_out = len(outs) + 1
    return dict(zip([n for n, _, _ in outs] + ["wt_pad"], res[:n_out])), res[n_out:]


def _dup_kv_head(t, g):
    tf = t.astype(F32)
    keep = (_lane_iota(tf.shape) < A_HEAD_DIM) == (g == 0)
    return jnp.where(keep, tf, pltpu.roll(tf, A_HEAD_DIM, 1)).astype(BF16)


def _stack_heads(t):
    lo = _lane_iota(t.shape) < A_HEAD_DIM
    zero = jnp.zeros_like(t)
    return jnp.concatenate([jnp.where(lo, t, zero), jnp.where(lo, zero, t)], axis=0)


ATT_ROWS = A_HEADS * WINDOW
GROUP_ROWS = ATT_ROWS // A_KV_HEADS
HEADS_PER_GROUP = A_HEADS // A_KV_HEADS


def _band_mask_t(n):
    kj = _row_iota((2 * WINDOW, GROUP_ROWS)) - WINDOW
    qi = _lane_iota((2 * WINDOW, GROUP_ROWS)) % WINDOW
    return (kj <= qi) & (qi - kj < WINDOW) & ((n > 0) | (kj >= 0))


def _stacked_queries(ref, g):
    pairs = range(g * HEADS_PER_GROUP // 2, (g + 1) * HEADS_PER_GROUP // 2)
    return jnp.concatenate([_stack_heads(ref[:, p * LANES:(p + 1) * LANES]) for p in pairs], axis=0)


def _unstack_heads(t, g, ref, dtype):
    lo = _lane_iota((WINDOW, LANES)) < A_HEAD_DIM
    for hh in range(HEADS_PER_GROUP // 2):
        p = g * HEADS_PER_GROUP // 2 + hh
        ref[:, p * LANES:(p + 1) * LANES] = jnp.where(lo, t[2 * hh * WINDOW:(2 * hh + 1) * WINDOW],
                                                       t[(2 * hh + 1) * WINDOW:(2 * hh + 2) * WINDOW]).astype(dtype)


FWD_BLOCKS = 16


def _attn_fwd(qkv, sink_row, B, S):
    T = B * S
    nb = S // WINDOW
    blocks = math.gcd(nb, FWD_BLOCKS)
    steps = nb // blocks

    def one_block(has_prev, sink_ref, q, k, v, o_ref, lse_ref):
        valid = _band_mask_t(has_prev)
        lse_rows = []
        for g in range(A_KV_HEADS):
            kd, vd = _dup_kv_head(k, g), _dup_kv_head(v, g)
            s = jnp.where(valid, _dot_nt(kd, _stacked_queries(q, g)), NEG_BIG)
            sink = sink_ref[:, g * GROUP_ROWS:(g + 1) * GROUP_ROWS]
            m = jnp.maximum(jnp.max(s, axis=0, keepdims=True), sink)
            e = jnp.exp(s - m)
            den = jnp.sum(e, axis=0, keepdims=True) + jnp.exp(sink - m)
            o = _dot_tn((e * (1.0 / den)).astype(BF16), vd)
            _unstack_heads(o, g, o_ref, F32)
            lse = m + jnp.log(den)
            lse_rows += [lse[:, j * WINDOW:(j + 1) * WINDOW] for j in range(HEADS_PER_GROUP)]
        by_head = jnp.concatenate(lse_rows + [jnp.zeros((WINDOW - A_HEADS, WINDOW), F32)], axis=0)
        lse_ref[...] = by_head.T

    def body(sink_ref, q_ref, kc_ref, vc_ref, kp_ref, vp_ref, o_ref, lse_ref):
        k_all = jnp.concatenate([kp_ref[...], kc_ref[...]], axis=0)
        v_all = jnp.concatenate([vp_ref[...], vc_ref[...]], axis=0)
        for j in range(blocks):
            rows = pl.ds(j * WINDOW, WINDOW)
            keys = slice(j * WINDOW, (j + 2) * WINDOW)
            has_prev = pl.program_id(1) if j == 0 else 1
            one_block(has_prev, sink_ref, q_ref[rows, :], k_all[keys], v_all[keys], o_ref.at[rows], lse_ref.at[rows])

    def cur(col, w):
        return pl.BlockSpec((blocks * WINDOW, w), lambda b, n: (b * steps + n, col))

    def prev(col):
        return pl.BlockSpec((WINDOW, LANES), lambda b, n: (b * nb + jnp.maximum(blocks * n - 1, 0), col))

    kcol, vcol = QKV_K // LANES, QKV_V // LANES
    return pl.pallas_call(
        body, name="attn_fwd", grid=(B, steps),
        in_specs=[_const_spec((1, ATT_ROWS)), cur(0, A_WIDTH), cur(kcol, LANES), cur(vcol, LANES), prev(kcol), prev(vcol)],
        out_specs=[cur(0, A_WIDTH), cur(0, LANES)],
        out_shape=[jax.ShapeDtypeStruct((T, A_WIDTH), F32), jax.ShapeDtypeStruct((T, LANES), F32)],
        compiler_params=_params("parallel", "parallel"),
    )(sink_row, qkv, qkv, qkv, qkv, qkv)


ATT_CHUNK = 128


def _chunk_masks(n):
    masks = []
    for half in range(WINDOW // ATT_CHUNK):
        qi = _row_iota((ATT_CHUNK, 2 * WINDOW)) + half * ATT_CHUNK
        kj = _lane_iota((ATT_CHUNK, 2 * WINDOW)) - WINDOW
        masks.append((kj <= qi) & (qi - kj < WINDOW) & ((n > 0) | (kj >= 0)))
    return masks


def _all_stacked_queries(ref):
    return jnp.concatenate([_stacked_queries(ref, g) for g in range(A_KV_HEADS)], axis=0)


def _by_group(fn, lhs, rhs_per_group):
    return jnp.concatenate([fn(lhs[g * GROUP_ROWS:(g + 1) * GROUP_ROWS], rhs_per_group[g])
                            for g in range(A_KV_HEADS)], axis=0)


BWD_BLOCKS = 8


def _attn_bwd(qkv, do, out, lse, sink_col, B, S):
    T = B * S
    nb = S // WINDOW
    M = math.gcd(nb, BWD_BLOCKS)
    steps = nb // M
    n_chunks = ATT_ROWS // ATT_CHUNK
    halves = WINDOW // ATT_CHUNK

    def block(has_prev, sink_ref, q, do_b, out_b, lse_b, k, v, scratch, want_dq):
        s_ref, dp_ref, ds_ref, p_ref = scratch
        width = k.shape[0]
        masks = [mk[:, 0:width] for mk in _chunk_masks(has_prev)]
        kd = [_dup_kv_head(k, g) for g in range(A_KV_HEADS)]
        vd = [_dup_kv_head(v, g) for g in range(A_KV_HEADS)]
        qs, dos = _all_stacked_queries(q), _all_stacked_queries(do_b)
        s_ref[...] = _by_group(_dot_nt, qs, kd)
        dp_ref[...] = _by_group(_dot_nt, dos, vd)
        lane = _lane_iota((ATT_CHUNK, LANES))
        lo = lane < A_HEAD_DIM
        lane1 = _lane_iota((1, LANES))
        dsink_row = jnp.zeros((1, LANES), F32)
        for c in range(n_chunks):
            rows = slice(c * ATT_CHUNK, (c + 1) * ATT_CHUNK)
            head, half = divmod(c, halves)
            qrows = slice(half * ATT_CHUNK, (half + 1) * ATT_CHUNK)
            slab = slice((head // 2) * LANES, (head // 2 + 1) * LANES)
            lse_col = jnp.sum(jnp.where(lane == head, lse_b[qrows, :], 0.0), axis=-1, keepdims=True)
            prod = do_b[qrows, slab].astype(F32) * out_b[qrows, slab].astype(F32)
            mine = lo if head % 2 == 0 else jnp.logical_not(lo)
            delta = jnp.sum(jnp.where(mine, prod, 0.0), axis=-1, keepdims=True)
            prob = jnp.exp(jnp.where(masks[half], s_ref[rows, :], NEG_BIG) - lse_col)
            p_ref[rows, :] = prob.astype(BF16)
            ds_ref[rows, :] = (prob * (dp_ref[rows, :] - delta)).astype(BF16)
            w = -jnp.exp(sink_ref[rows, :] - lse_col) * delta
            dsink_row += jnp.where(lane1 == head, jnp.sum(w, axis=0, keepdims=True), 0.0)
        dq = _by_group(_dot, ds_ref[...], kd) * ATT_SCALE if want_dq else None
        groups = [slice(g * GROUP_ROWS, (g + 1) * GROUP_ROWS) for g in range(A_KV_HEADS)]
        dk = [_dot_tn(ds_ref[rows, :], qs[rows]) for rows in groups]
        dv = [_dot_tn(p_ref[rows, :], dos[rows]) for rows in groups]
        return dq, dk, dv, dsink_row

    def fold(per_group):
        lane = _lane_iota((WINDOW, LANES))
        out = jnp.zeros((WINDOW, LANES), F32)
        for g, acc in enumerate(per_group):
            out = jnp.where((lane < A_HEAD_DIM) == (g == 0), acc + pltpu.roll(acc, A_HEAD_DIM, 1), out)
        return out

    def body(sink_ref, q_ref, qn_ref, do_ref, don_ref, out_ref, outn_ref, lse_ref, lsen_ref, kc_ref, kp_ref, vc_ref, vp_ref,
             dq_ref, dkv_ref, dsink_ref, s_scr, dp_scr, ds_scr, p_scr, s_x, dp_x, ds_x, p_x):
        b, m = pl.program_id(0), pl.program_id(1)

        @pl.when((b == 0) & (m == 0))
        def _():
            dsink_ref[...] = jnp.zeros_like(dsink_ref)

        k_all = jnp.concatenate([kp_ref[...], kc_ref[...]], axis=0)
        v_all = jnp.concatenate([vp_ref[...], vc_ref[...]], axis=0)
        results = []
        for j in range(M):
            rows = slice(j * WINDOW, (j + 1) * WINDOW)
            keys = slice(j * WINDOW, (j + 2) * WINDOW)
            has_prev = m if j == 0 else 1
            results.append(block(has_prev, sink_ref, q_ref[rows, :], do_ref[rows, :], out_ref[rows, :], lse_ref[rows, :],
                                 k_all[keys], v_all[keys], (s_scr.at[j], dp_scr.at[j], ds_scr.at[j], p_scr.at[j]), True))
        last_keys = slice(M * WINDOW, (M + 1) * WINDOW)
        _, dk_x, dv_x, _ = block(1, sink_ref, qn_ref[...], don_ref[...], outn_ref[...], lsen_ref[...],
                                 k_all[last_keys], v_all[last_keys], (s_x, dp_x, ds_x, p_x), False)
        has_next = m < steps - 1
        lo_q = _lane_iota((WINDOW, LANES)) < A_HEAD_DIM
        dsink_row = jnp.zeros((1, LANES), F32)
        for j, (dq, dk, dv, ds_row) in enumerate(results):
            rows = slice(j * WINDOW, (j + 1) * WINDOW)
            for p in range(A_HEADS // 2):
                dq_ref[rows, p * LANES:(p + 1) * LANES] = jnp.where(
                    lo_q, dq[2 * p * WINDOW:(2 * p + 1) * WINDOW], dq[(2 * p + 1) * WINDOW:(2 * p + 2) * WINDOW]).astype(BF16)
            if j + 1 < M:
                dk_next = [t[0:WINDOW] for t in results[j + 1][1]]
                dv_next = [t[0:WINDOW] for t in results[j + 1][2]]
            else:
                dk_next = [jnp.where(has_next, t, 0.0) for t in dk_x]
                dv_next = [jnp.where(has_next, t, 0.0) for t in dv_x]
            dkv_ref[rows, 0:LANES] = fold([own[WINDOW:] + nxt for own, nxt in zip(dk, dk_next)]).astype(BF16)
            dkv_ref[rows, LANES:] = fold([own[WINDOW:] + nxt for own, nxt in zip(dv, dv_next)]).astype(BF16)
            dsink_row += ds_row
        dsink_ref[0:1, :] += dsink_row

    def cur(col, w):
        return pl.BlockSpec((M * WINDOW, w), lambda b, m: (b * steps + m, col))

    def nxt(col, w):
        return pl.BlockSpec((WINDOW, w), lambda b, m: (b * nb + jnp.minimum(M * (m + 1), nb - 1), col))

    def prev(col):
        return pl.BlockSpec((WINDOW, LANES), lambda b, m: (b * nb + jnp.maximum(M * m - 1, 0), col))

    kcol, vcol = QKV_K // LANES, QKV_V // LANES
    scores = (M, ATT_ROWS, 2 * WINDOW)
    extra = (ATT_ROWS, WINDOW)
    return pl.pallas_call(
        body, name="attn_bwd", grid=(B, steps),
        in_specs=[_const_spec((ATT_ROWS, 1)), cur(0, A_WIDTH), nxt(0, A_WIDTH), cur(0, A_WIDTH), nxt(0, A_WIDTH),
                  cur(0, A_WIDTH), nxt(0, A_WIDTH), cur(0, LANES), nxt(0, LANES),
                  cur(kcol, LANES), prev(kcol), cur(vcol, LANES), prev(vcol)],
        out_specs=[cur(0, A_WIDTH), cur(0, 2 * LANES), pl.BlockSpec((8, LANES), lambda b, m: (0, 0))],
        out_shape=[jax.ShapeDtypeStruct((T, A_WIDTH), BF16), jax.ShapeDtypeStruct((T, 2 * LANES), BF16),
                   jax.ShapeDtypeStruct((8, LANES), F32)],
        scratch_shapes=[pltpu.VMEM(scores, F32), pltpu.VMEM(scores, F32), pltpu.VMEM(scores, BF16), pltpu.VMEM(scores, BF16),
                        pltpu.VMEM(extra, F32), pltpu.VMEM(extra, F32), pltpu.VMEM(extra, BF16), pltpu.VMEM(extra, BF16)],
        compiler_params=_params("arbitrary", "arbitrary"),
    )(sink_col, qkv, qkv, do, do, out, out, lse, lse, qkv, qkv, qkv, qkv)


GLA_TILE = 256
CHUNKS_PER_TILE = GLA_TILE // B_CHUNK
GLA_TILES_PER_STEP = 4


def _gla_factors(q_ref, k_ref, cum_ref):
    scale = B_KEY_DIM ** -0.5
    cum = cum_ref[...]
    shape = (B_CHUNK, B_KEY_WIDTH)
    last = jnp.concatenate([jnp.broadcast_to(cum_ref[pl.ds(c * B_CHUNK + B_CHUNK - 1, 1), :], shape)
                            for c in range(CHUNKS_PER_TILE)], axis=0)
    mid = jnp.concatenate([jnp.broadcast_to(cum_ref[pl.ds(c * B_CHUNK + B_CHUNK // 2 - 1, 1), :], shape)
                           for c in range(CHUNKS_PER_TILE)], axis=0)
    e_qm, e_km, e_qe, e_kd = jnp.exp(cum - mid), jnp.exp(mid - cum), jnp.exp(cum), jnp.exp(last - cum)
    qs = q_ref[...] * scale
    k = k_ref[...]
    return qs, k, (e_qm, e_km, e_qe, e_kd)


def _head_mask(shape, h):
    return (_lane_iota(shape) // B_KEY_DIM) == h


def _stack_masked(t):
    return jnp.concatenate([jnp.where(_head_mask(t.shape, h), t, 0.0) for h in range(B_HEADS)], axis=0).astype(BF16)


def _select_heads(t):
    shape = (B_CHUNK, B_KEY_WIDTH)
    out = jnp.zeros(shape, F32)
    for h in range(B_HEADS):
        out = jnp.where(_head_mask(shape, h), t[h * B_CHUNK:(h + 1) * B_CHUNK], out)
    return out


def _select_state(t):
    shape = (B_VAL_DIM, B_KEY_WIDTH)
    out = jnp.zeros(shape, F32)
    for h in range(B_HEADS):
        out = jnp.where(_head_mask(shape, h), t[h * B_VAL_DIM:(h + 1) * B_VAL_DIM], out)
    return out


def _rows_by_head(t):
    return jnp.concatenate([t[:, h * B_VAL_DIM:(h + 1) * B_VAL_DIM] for h in range(B_HEADS)], axis=0)


def _intra_mask():
    i, j = _row_iota((GLA_TILE, GLA_TILE)), _lane_iota((GLA_TILE, GLA_TILE))
    return (i // B_CHUNK == j // B_CHUNK) & (j <= i)


def _pair_stack(t, p):
    slab = t[:, p * LANES:(p + 1) * LANES]
    lo = _lane_iota(slab.shape) < B_KEY_DIM
    return jnp.concatenate([jnp.where(lo, slab, 0.0), jnp.where(lo, 0.0, slab)], axis=0).astype(BF16)


def _gla_fwd(q, k, cum, vb, B, S):
    T = B * S
    nt = S // GLA_TILE
    tps = math.gcd(nt, GLA_TILES_PER_STEP)

    def one_sequence(q_ref, k_ref, cum_ref, v_ref, o_ref, st_all_ref, st_ref):
        qs, kk, (e_qm, e_km, e_qe, e_kd) = _gla_factors(q_ref, k_ref, cum_ref)
        qm, km, qe, kd = qs * e_qm, kk * e_km, qs * e_qe, (kk * e_kd).astype(BF16)
        mask = _intra_mask()
        intra = []
        for p in range(B_HEADS // 2):
            a = _dot_nt(_pair_stack(qm, p), km[:, p * LANES:(p + 1) * LANES].astype(BF16))
            for hh in range(2):
                h = 2 * p + hh
                att = jnp.where(mask, a[hh * GLA_TILE:(hh + 1) * GLA_TILE], 0.0).astype(BF16)
                intra.append(_dot(att, v_ref[:, h * B_VAL_DIM:(h + 1) * B_VAL_DIM]))
        inter = []
        for c in range(CHUNKS_PER_TILE):
            rows = slice(c * B_CHUNK, (c + 1) * B_CHUNK)
            st = st_ref[...]
            st_all_ref[c] = st
            inter.append(_dot_nt(_stack_masked(qe[rows]), st.astype(BF16)))
            inc = _select_state(_dot_tn(v_ref[rows, :], kd[rows]))
            decay = jnp.exp(cum_ref[pl.ds(c * B_CHUNK + B_CHUNK - 1, 1), :])
            st_ref[...] = st * decay + inc
        for h in range(B_HEADS):
            oi = jnp.concatenate([inter[c][h * B_CHUNK:(h + 1) * B_CHUNK] for c in range(CHUNKS_PER_TILE)], axis=0)
            o_ref[:, h * B_VAL_DIM:(h + 1) * B_VAL_DIM] = (intra[h] + oi).astype(BF16)

    def body(q_ref, k_ref, cum_ref, v_ref, o_ref, st_all_ref, st_ref):
        @pl.when(pl.program_id(0) == 0)
        def _():
            st_ref[...] = jnp.zeros_like(st_ref)

        for b in range(B):
            for tile in range(tps):
                tok = pl.ds(tile * GLA_TILE, GLA_TILE)
                chunks = pl.ds(tile * CHUNKS_PER_TILE, CHUNKS_PER_TILE)
                one_sequence(*[r.at[b, tok] for r in (q_ref, k_ref, cum_ref, v_ref, o_ref)],
                             st_all_ref.at[b, chunks], st_ref.at[b])

    def rows(w):
        return pl.BlockSpec((B, tps * GLA_TILE, w), lambda t: (0, t, 0))

    seq = lambda a: a.reshape(B, S, a.shape[-1])
    o, st_all = pl.pallas_call(
        body, name="gla_fwd", grid=(nt // tps,),
        in_specs=[rows(B_KEY_WIDTH), rows(B_KEY_WIDTH), rows(B_KEY_WIDTH), rows(B_WIDTH)],
        out_specs=[rows(B_WIDTH),
                   pl.BlockSpec((B, tps * CHUNKS_PER_TILE, B_VAL_DIM, B_KEY_WIDTH), lambda t: (0, t, 0, 0))],
        out_shape=[jax.ShapeDtypeStruct((B, S, B_WIDTH), BF16),
                   jax.ShapeDtypeStruct((B, S // B_CHUNK, B_VAL_DIM, B_KEY_WIDTH), F32)],
        scratch_shapes=[pltpu.VMEM((B, B_VAL_DIM, B_KEY_WIDTH), F32)],
        compiler_params=_params("arbitrary"),
    )(seq(q), seq(k), seq(cum), seq(vb))
    return o.reshape(T, B_WIDTH), st_all.reshape(T // B_CHUNK, B_VAL_DIM, B_KEY_WIDTH)


def _gla_bwd(q, k, cum, vb, do, st_all, B, S, wgrads):
    T = B * S
    nt = S // GLA_TILE
    tps = math.gcd(nt, GLA_TILES_PER_STEP)
    steps = nt // tps
    scale = B_KEY_DIM ** -0.5
    nw = len(wgrads)

    def one_sequence(q_ref, k_ref, cum_ref, v_ref, do_ref, st_all_ref, dq_ref, dk_ref, dv_ref, dla_ref, dst_ref):
        qs, kk, (e_qm, e_km, e_qe, e_kd) = _gla_factors(q_ref, k_ref, cum_ref)
        qm, km, qe, kd = qs * e_qm, kk * e_km, qs * e_qe, kk * e_kd
        mask = _intra_mask()
        dqm_slabs, dkm_slabs, dv_intra = [], [], []
        for p in range(B_HEADS // 2):
            qm_st = _pair_stack(qm, p)
            km_p = km[:, p * LANES:(p + 1) * LANES].astype(BF16)
            a = _dot_nt(qm_st, km_p)
            da_blocks, dqm_h = [], []
            for hh in range(2):
                h = 2 * p + hh
                vs = slice(h * B_VAL_DIM, (h + 1) * B_VAL_DIM)
                att = jnp.where(mask, a[hh * GLA_TILE:(hh + 1) * GLA_TILE], 0.0).astype(BF16)
                dv_intra.append(_dot_tn(att, do_ref[:, vs]))
                da = jnp.where(mask, _dot_nt(do_ref[:, vs], v_ref[:, vs]), 0.0).astype(BF16)
                da_blocks.append(da)
                dqm_h.append(_dot(da, km_p))
            lo = _lane_iota((GLA_TILE, LANES)) < B_KEY_DIM
            dqm_slabs.append(jnp.where(lo, dqm_h[0], dqm_h[1]))
            dkm_slabs.append(_dot_tn(jnp.concatenate(da_blocks, axis=0), qm_st))
        dqm = jnp.concatenate(dqm_slabs, axis=1)
        dkm = jnp.concatenate(dkm_slabs, axis=1)

        dqe_c, dkd_c, dv_inter, tail_c = ([None] * CHUNKS_PER_TILE for _ in range(4))
        for c in reversed(range(CHUNKS_PER_TILE)):
            rows = slice(c * B_CHUNK, (c + 1) * B_CHUNK)
            dst = dst_ref[...]
            dst_b = dst.astype(BF16)
            dv_inter[c] = _dot_nt(_stack_masked(kd[rows]), dst_b)
            dkd_c[c] = _select_heads(_dot(_rows_by_head(v_ref[rows, :]), dst_b))
            do_c = do_ref[rows, :]
            dqe_c[c] = _select_heads(_dot(_rows_by_head(do_c), st_all_ref[c].astype(BF16)))
            contrib = _select_state(_dot_tn(do_c, qe[rows].astype(BF16)))
            decay = jnp.exp(cum_ref[pl.ds(c * B_CHUNK + B_CHUNK - 1, 1), :])
            tail = (jnp.sum(kk[rows] * dkd_c[c] * e_kd[rows], axis=0, keepdims=True)
                    + decay * jnp.sum(st_all_ref[c] * dst, axis=0, keepdims=True))
            tail_c[c] = jnp.broadcast_to(tail, (B_CHUNK, B_KEY_WIDTH))
            dst_ref[...] = dst * decay + contrib
        dqe = jnp.concatenate(dqe_c, axis=0)
        dkd = jnp.concatenate(dkd_c, axis=0)
        dqs = dqm * e_qm + dqe * e_qe
        dk = dkm * e_km + dkd * e_kd
        dq_ref[...] = (dqs * scale).astype(BF16)
        dk_ref[...] = dk.astype(BF16)
        for h in range(B_HEADS):
            dvi = jnp.concatenate([dv_inter[c][h * B_CHUNK:(h + 1) * B_CHUNK] for c in range(CHUNKS_PER_TILE)], axis=0)
            dv_ref[:, h * B_VAL_DIM:(h + 1) * B_VAL_DIM] = (dv_intra[h] + dvi).astype(BF16)
        dd = qs * dqs - kk * dk
        i, j = _row_iota((GLA_TILE, GLA_TILE)), _lane_iota((GLA_TILE, GLA_TILE))
        upper = ((i // B_CHUNK == j // B_CHUNK) & (j >= i)).astype(BF16)
        hi, mid, lo3 = _split3(dd)
        dla_ref[...] = _dot(upper, hi) + _dot(upper, mid) + _dot(upper, lo3) + jnp.concatenate(tail_c, axis=0)

    def body(q_ref, k_ref, cum_ref, v_ref, do_ref, st_all_ref, *rest):
        g_refs, (dq_ref, dk_ref, dv_ref, dla_ref) = rest[:nw], rest[nw:nw + 4]
        rv_refs, (dst_ref, send_sems, recv_sems) = rest[nw + 4:2 * nw + 4], rest[2 * nw + 4:]
        x, y, c = _my_place()

        def wcopy(a, r):
            dx, dy, dc = FLIPS[r]
            return pltpu.make_async_remote_copy(
                src_ref=g_refs[a].at[4 * (x ^ dx) + 2 * (y ^ dy) + (c ^ dc)], dst_ref=rv_refs[a].at[r],
                send_sem=send_sems.at[a, r], recv_sem=recv_sems.at[a, r],
                device_id=(x ^ dx, y ^ dy, c ^ dc), device_id_type=MESH)

        @pl.when(pl.program_id(0) == 0)
        def _():
            dst_ref[...] = jnp.zeros_like(dst_ref)
            for a in range(nw):
                for r in range(len(FLIPS)):
                    wcopy(a, r).start()

        for b in range(B):
            for tile in reversed(range(tps)):
                tok = pl.ds(tile * GLA_TILE, GLA_TILE)
                chunks = pl.ds(tile * CHUNKS_PER_TILE, CHUNKS_PER_TILE)
                one_sequence(*[r.at[b, tok] for r in (q_ref, k_ref, cum_ref, v_ref, do_ref)], st_all_ref.at[b, chunks],
                             *[r.at[b, tok] for r in (dq_ref, dk_ref, dv_ref, dla_ref)], dst_ref.at[b])

        @pl.when(pl.program_id(0) == steps - 1)
        def _():
            for a in range(nw):
                for r in range(len(FLIPS)):
                    wcopy(a, r).wait()

    def rows(w):
        return pl.BlockSpec((B, tps * GLA_TILE, w), lambda t: (0, steps - 1 - t, 0))

    seq = lambda a: a.reshape(B, S, a.shape[-1])
    res = pl.pallas_call(
        body, name="gla_bwd", grid=(steps,),
        in_specs=[rows(B_KEY_WIDTH), rows(B_KEY_WIDTH), rows(B_KEY_WIDTH), rows(B_WIDTH), rows(B_WIDTH),
                  pl.BlockSpec((B, tps * CHUNKS_PER_TILE, B_VAL_DIM, B_KEY_WIDTH), lambda t: (0, steps - 1 - t, 0, 0))]
                 + _any_specs(nw),
        out_specs=[rows(B_KEY_WIDTH), rows(B_KEY_WIDTH), rows(B_WIDTH), rows(B_KEY_WIDTH)] + _any_specs(nw),
        out_shape=[jax.ShapeDtypeStruct((B, S, B_KEY_WIDTH), BF16), jax.ShapeDtypeStruct((B, S, B_KEY_WIDTH), BF16),
                   jax.ShapeDtypeStruct((B, S, B_WIDTH), BF16), jax.ShapeDtypeStruct((B, S, B_KEY_WIDTH), F32)]
                  + [jax.ShapeDtypeStruct((len(FLIPS), *g.shape[1:]), g.dtype) for g in wgrads],
        scratch_shapes=[pltpu.VMEM((B, B_VAL_DIM, B_KEY_WIDTH), F32),
                        pltpu.SemaphoreType.DMA((nw, len(FLIPS))), pltpu.SemaphoreType.DMA((nw, len(FLIPS)))],
        compiler_params=_params("arbitrary"),
    )(seq(q), seq(k), seq(cum), seq(vb), seq(do), st_all.reshape(B, S // B_CHUNK, B_VAL_DIM, B_KEY_WIDTH), *wgrads)
    return [a.reshape(T, a.shape[-1]) for a in res[:4]], res[4:]


def _merge(x2, tgt2, attn, za, o_gla, zb, ga, gb, w_oa_sh, w_ob_sh, w_o, g_gla, g_final):
    T = x2.shape[0]
    tm = math.gcd(T, 512)
    sub = math.gcd(tm, 256)
    last = T // tm - 1

    def body(x_ref, tgt_ref, attn_ref, za_ref, og_ref, zb_ref, ga_ref, gb_ref,
             woa_sh_ref, wob_sh_ref, wo_ref, gg_ref, gf_ref,
             dxres_ref, dattn_ref, dog_ref, dza_ref, dzb_ref, dga_ref, dgb_ref,
             dwo_out, dwoa_out, dwob_out, small_ref,
             awo_ref, awoa_ref, awob_ref, agf_ref, agg_ref, loss_ref, woa_ref, wob_ref,
             dwo_ref, dwoa_ref, dwob_ref, w_sems, dw_sems):
        w_copies = [pltpu.make_async_copy(sh.at[j], dst.at[:, j * SHARD_OUT:(j + 1) * SHARD_OUT], w_sems.at[a, j])
                    for a, (sh, dst) in enumerate(((woa_sh_ref, woa_ref), (wob_sh_ref, wob_ref))) for j in range(N_DEV)]
        dw_copies = [pltpu.make_async_copy(src, dst, dw_sems.at[a])
                     for a, (src, dst) in enumerate(((dwo_ref, dwo_out), (dwoa_ref, dwoa_out), (dwob_ref, dwob_out)))]

        @pl.when(pl.program_id(0) == 0)
        def _():
            for cp in w_copies:
                cp.start()
            for r in (awo_ref, awoa_ref, awob_ref, agf_ref, agg_ref, loss_ref):
                r[...] = jnp.zeros_like(r)
            for cp in w_copies:
                cp.wait()

        def one_tile(rows):
            za_v = za_ref[rows, :].astype(F32)
            sig_za = _sigmoid_tanh(za_v)
            silu_a = za_v * sig_za
            attn_v = attn_ref[rows, :].astype(F32)
            oa = (attn_v * silu_a).astype(BF16)
            ya = _dot(oa, woa_ref[...])
            og = og_ref[rows, :].astype(F32)
            zb_v = zb_ref[rows, :].astype(F32)
            sig_zb = _sigmoid_tanh(zb_v)
            silu_b = zb_v * sig_zb
            gg = gg_ref[...]
            on_parts, rinv_parts = [], []
            for h in range(B_HEADS):
                seg = og[:, h * B_VAL_DIM:(h + 1) * B_VAL_DIM]
                rinv = lax.rsqrt(jnp.mean(seg * seg, axis=-1, keepdims=True) + NORM_EPS)
                rinv_parts.append(rinv)
                on_parts.append(seg * rinv)
            on = jnp.concatenate(on_parts, axis=1)
            obn = on * gg
            ob = (obn * silu_b).astype(BF16)
            yb = _dot(ob, wob_ref[...])
            sig_a = _sigmoid_tanh(ga_ref[rows, :].astype(F32))
            sig_b = _sigmoid_tanh(gb_ref[rows, :].astype(F32))
            merged = (sig_a * ya + sig_b * yb).astype(BF16)
            out = x_ref[rows, :] + _dot(merged, wo_ref[...])
            rf = lax.rsqrt(jnp.mean(out * out, axis=-1, keepdims=True) + NORM_EPS)
            nrm = out * rf
            gf = gf_ref[...]
            err = nrm * gf - tgt_ref[rows, :]
            loss = jnp.sum(err * err) * (0.5 / D_MODEL)

            dy = err * (1.0 / D_MODEL)
            dgf = jnp.sum(dy * nrm, axis=0, keepdims=True)
            dn = dy * gf
            dout = rf * (dn - nrm * jnp.mean(dn * nrm, axis=-1, keepdims=True))
            dxres_ref[rows, :] = dout
            dout_b = dout.astype(BF16)
            dmerged = _dot_nt(dout_b, wo_ref[...])
            dya = dmerged * sig_a
            dyb = dmerged * sig_b
            dga_ref[rows, :] = (dmerged * ya * sig_a * (1.0 - sig_a)).astype(BF16)
            dgb_ref[rows, :] = (dmerged * yb * sig_b * (1.0 - sig_b)).astype(BF16)
            dya_b, dyb_b = dya.astype(BF16), dyb.astype(BF16)
            doa = _dot_nt(dya_b, woa_ref[...])
            dattn_ref[rows, :] = (doa * silu_a).astype(BF16)
            dza_ref[rows, :] = (doa * attn_v * (sig_za * (1.0 + za_v * (1.0 - sig_za)))).astype(BF16)
            dob = _dot_nt(dyb_b, wob_ref[...])
            dzb_ref[rows, :] = (dob * obn * (sig_zb * (1.0 + zb_v * (1.0 - sig_zb)))).astype(BF16)
            dobn = dob * silu_b
            dgg = jnp.sum(dobn * on, axis=0, keepdims=True)
            don = dobn * gg
            for h in range(B_HEADS):
                sl = slice(h * B_VAL_DIM, (h + 1) * B_VAL_DIM)
                don_h, on_h = don[:, sl], on[:, sl]
                dog_ref[rows, sl] = (rinv_parts[h] * (don_h - on_h * jnp.mean(don_h * on_h, axis=-1, keepdims=True))
                                     ).astype(BF16)
            return (merged, dout_b, oa, dya_b, ob, dyb_b), (loss, dgf, dgg)

        tiles = [one_tile(pl.ds(j * sub, sub)) for j in range(tm // sub)]
        merged, dout_b, oa, dya_b, ob, dyb_b = (jnp.concatenate(parts, axis=0) for parts in zip(*[t[0] for t in tiles]))
        awo_ref[...] += _dot_tn(merged, dout_b)
        awoa_ref[...] += _dot_tn(oa, dya_b)
        awob_ref[...] += _dot_tn(ob, dyb_b)
        for _, (loss, dgf, dgg) in tiles:
            loss_ref[...] += loss
            agf_ref[...] += dgf
            agg_ref[...] += dgg

        @pl.when(pl.program_id(0) == last)
        def _():
            for j in range(N_DEV):
                dwo_ref[j] = awo_ref[j * SHARD_OUT:(j + 1) * SHARD_OUT, :].astype(BF16)
                dwoa_ref[j] = awoa_ref[:, j * SHARD_OUT:(j + 1) * SHARD_OUT].astype(BF16)
                dwob_ref[j] = awob_ref[:, j * SHARD_OUT:(j + 1) * SHARD_OUT].astype(BF16)
            small_ref[...] = jnp.zeros_like(small_ref)
            _put_rows(small_ref, SMALL_G_FINAL, agf_ref[...])
            _put_rows(small_ref, SMALL_G_GLA, agg_ref[...])
            small_ref[SMALL_LOSS:SMALL_LOSS + 1, :] = loss_ref[...]
            for cp in dw_copies:
                cp.start()
            for cp in dw_copies:
                cp.wait()

    def rows(w):
        return pl.BlockSpec((tm, w), lambda i: (i, 0))

    def whole(shape):
        nd = len(shape)
        return pl.BlockSpec(shape, lambda i: (0,) * nd)

    outs = [((T, D_MODEL), F32, rows(D_MODEL)), ((T, A_WIDTH), BF16, rows(A_WIDTH)), ((T, B_WIDTH), BF16, rows(B_WIDTH)),
            ((T, A_WIDTH), BF16, rows(A_WIDTH)), ((T, B_WIDTH), BF16, rows(B_WIDTH)),
            ((T, D_MODEL), BF16, rows(D_MODEL)), ((T, D_MODEL), BF16, rows(D_MODEL)),
            ((N_DEV, SHARD_OUT, D_MODEL), BF16, pl.BlockSpec(memory_space=pl.ANY)),
            ((N_DEV, A_WIDTH, SHARD_OUT), BF16, pl.BlockSpec(memory_space=pl.ANY)),
            ((N_DEV, B_WIDTH, SHARD_OUT), BF16, pl.BlockSpec(memory_space=pl.ANY)),
            ((SMALL_SINKS, LANES), F32, whole((SMALL_SINKS, LANES)))]
    return pl.pallas_call(
        body, name="merge", grid=(T // tm,),
        in_specs=[rows(D_MODEL), rows(D_MODEL), rows(A_WIDTH), rows(A_WIDTH), rows(B_WIDTH), rows(B_WIDTH),
                  rows(D_MODEL), rows(D_MODEL),
                  pl.BlockSpec(memory_space=pl.ANY), pl.BlockSpec(memory_space=pl.ANY),
                  _const_spec((D_MODEL, D_MODEL)), _const_spec((1, B_WIDTH)), _const_spec((1, D_MODEL))],
        out_specs=[o[2] for o in outs],
        out_shape=[jax.ShapeDtypeStruct(o[0], o[1]) for o in outs],
        scratch_shapes=[pltpu.VMEM((D_MODEL, D_MODEL), F32), pltpu.VMEM((A_WIDTH, D_MODEL), F32),
                        pltpu.VMEM((B_WIDTH, D_MODEL), F32), pltpu.VMEM((1, D_MODEL), F32), pltpu.VMEM((1, B_WIDTH), F32),
                        pltpu.VMEM((1, LANES), F32), pltpu.VMEM((A_WIDTH, D_MODEL), BF16),
                        pltpu.VMEM((B_WIDTH, D_MODEL), BF16),
                        pltpu.VMEM((N_DEV, SHARD_OUT, D_MODEL), BF16), pltpu.VMEM((N_DEV, A_WIDTH, SHARD_OUT), BF16),
                        pltpu.VMEM((N_DEV, B_WIDTH, SHARD_OUT), BF16),
                        pltpu.SemaphoreType.DMA((2, N_DEV)), pltpu.SemaphoreType.DMA((3,))],
        compiler_params=pltpu.CompilerParams(dimension_semantics=("arbitrary",), vmem_limit_bytes=V7X_VMEM_LIMIT_MAX),
    )(x2, tgt2, attn, za, o_gla, zb, ga, gb, w_oa_sh, w_ob_sh, w_o, g_gla, g_final)


def _in_proj_bwd(x2, dxres, cosf, sinf, g_in, wt_pad, wa_pad, parts):
    T = x2.shape[0]
    tm = math.gcd(T, 512)
    sub = math.gcd(tm, 256)
    last = T // tm - 1
    base = SMALL_G_IN

    def body(x_ref, dxres_ref, cos_ref, sin_ref, g_ref, wt_ref, wa_ref,
             dq_ref, dkv_ref, dza_ref, dqb_ref, dkb_ref, dvb_ref, dzb_ref, dla_ref, u_ref, alr_ref, dga_ref, dgb_ref,
             dx_ref, dsh_ref, small_ref, dproj_ref, agin_ref, aba_ref, awa_ref):
        @pl.when(pl.program_id(0) == 0)
        def _():
            for r in (agin_ref, aba_ref, awa_ref):
                r[...] = jnp.zeros_like(r)

        def one_tile(rows):
            cos, nsin = cos_ref[rows, :], -sin_ref[rows, :]
            for s in range(A_WIDTH // LANES):
                sl = slice(s * LANES, (s + 1) * LANES)
                dproj_ref[rows, sl] = _rope_slab(dq_ref[rows, sl].astype(F32), cos, nsin).astype(BF16)
            dproj_ref[rows, QKV_K:QKV_V] = _rope_slab(dkv_ref[rows, 0:LANES].astype(F32), cos, nsin).astype(BF16)
            dproj_ref[rows, QKV_V:QKV_W] = dkv_ref[rows, LANES:]

            def put(name, val):
                a, b = SEG[name]
                dproj_ref[rows, a:b] = val

            put("za", dza_ref[rows, :])
            put("qb", dqb_ref[rows, :])
            put("kb", dkb_ref[rows, :])
            put("vb", dvb_ref[rows, :])
            put("zb", dzb_ref[rows, :])
            put("ga", dga_ref[rows, :])
            put("gb", dgb_ref[rows, :])
            du = dla_ref[rows, :] * (1.0 / B_GATE_TEMP) * _sigmoid(-u_ref[rows, :])
            du_b = du.astype(BF16)
            put("alr", _dot_nt(du_b, wa_ref[...]).astype(BF16))

            for j in range(N_DEV):
                col = (j % 2) * SHARD_PAD
                for a, b in _shard_pad_cols(j):
                    dsh_ref[j // 2, rows, col:col + b - a] = dproj_ref[rows, a:b]
                    col += b - a
                dsh_ref[j // 2, rows, col:(j % 2 + 1) * SHARD_PAD] = jnp.zeros((sub, SHARD_PAD - SHARD_IN), BF16)

            dh = _dot(dproj_ref[rows, :], wt_ref[...])
            x = x_ref[rows, :]
            r = lax.rsqrt(jnp.mean(x * x, axis=-1, keepdims=True) + NORM_EPS)
            nrm = x * r
            dn = dh * g_ref[...]
            dx_ref[rows, :] = dxres_ref[rows, :] + r * (dn - nrm * jnp.mean(dn * nrm, axis=-1, keepdims=True))
            return jnp.sum(dh * nrm, axis=0, keepdims=True), jnp.sum(du, axis=0, keepdims=True), alr_ref[rows, :], du_b

        for j in range(tm // sub):
            dgin, dba, alr, du_b = one_tile(pl.ds(j * sub, sub))
            agin_ref[...] += dgin
            aba_ref[...] += dba
            awa_ref[...] += _dot_tn(alr, du_b)

        @pl.when(pl.program_id(0) == last)
        def _():
            small_ref[...] = jnp.zeros_like(small_ref)
            _put_rows(small_ref, SMALL_G_IN - base, agin_ref[...])
            _put_rows(small_ref, SMALL_B_ALPHA - base, aba_ref[...])
            for half in range(B_KEY_WIDTH // LANES):
                r0 = SMALL_W_ALPHA - base + half * B_GATE_RANK
                small_ref[r0:r0 + B_GATE_RANK, :] = awa_ref[0:B_GATE_RANK, half * LANES:(half + 1) * LANES]

    def rows(w):
        return pl.BlockSpec((tm, w), lambda i: (i, 0))

    names = ["dq", "dkv", "dza", "dqb", "dkb", "dvb", "dzb", "dla", "u", "alr", "dga", "dgb"]
    return pl.pallas_call(
        body, name="in_proj_bwd", grid=(T // tm,),
        in_specs=[rows(D_MODEL), rows(D_MODEL), rows(LANES), rows(LANES), _const_spec((1, D_MODEL)),
                  _const_spec((D_IN_PAD, D_MODEL)), _const_spec((RANK_PAD, B_KEY_WIDTH))]
                 + [rows(parts[n].shape[1]) for n in names],
        out_specs=[rows(D_MODEL), pl.BlockSpec((N_CHIPS, tm, 2 * SHARD_PAD), lambda i: (0, i, 0)),
                   pl.BlockSpec((SMALL_ROWS - base, LANES), lambda i: (0, 0))],
        out_shape=[jax.ShapeDtypeStruct((T, D_MODEL), F32), jax.ShapeDtypeStruct((N_CHIPS, T, 2 * SHARD_PAD), BF16),
                   jax.ShapeDtypeStruct((SMALL_ROWS - base, LANES), F32)],
        scratch_shapes=[pltpu.VMEM((tm, D_IN_PAD), BF16), pltpu.VMEM((1, D_MODEL), F32), pltpu.VMEM((1, B_KEY_WIDTH), F32),
                        pltpu.VMEM((RANK_PAD, B_KEY_WIDTH), F32)],
        compiler_params=pltpu.CompilerParams(dimension_semantics=("arbitrary",), vmem_limit_bytes=V7X_VMEM_LIMIT_MAX),
    )(x2, dxres, cosf, sinf, g_in, wt_pad, wa_pad, *[parts[n] for n in names])


FLIPS = [(dx, dy, dc) for dx in (0, 1) for dy in (0, 1) for dc in (0, 1)][1:]


def _my_place():
    return lax.axis_index("x"), lax.axis_index("y"), lax.axis_index("c")


def _any_specs(n):
    return [pl.BlockSpec(memory_space=pl.ANY)] * n


def _gather_first(shards, pos_col):
    n = len(shards)
    T = pos_col.shape[0]
    rows_per_pass = math.gcd(T, 512)
    invf, sign = _rope_lane_constants()

    def body(*refs):
        ins, (pos_ref, invf_ref, sign_ref) = refs[:n], refs[n:n + 3]
        outs, (cos_ref, sin_ref) = refs[n + 3:2 * n + 3], refs[2 * n + 3:2 * n + 5]
        send_sems, recv_sems, local_sems = refs[2 * n + 5:]
        x, y, c = _my_place()
        me, sibling = (x, y, c), (x, y, 1 - c)
        chips = [(1 - x, y), (x, 1 - y), (1 - x, 1 - y)]

        def block(a, px, py, pc):
            return outs[a].at[4 * px + 2 * py + pc]

        def copy(a, k, blk, to, src=None):
            return pltpu.make_async_remote_copy(
                src_ref=block(a, *blk) if src is None else src, dst_ref=block(a, *blk),
                send_sem=send_sems.at[a, k], recv_sem=recv_sems.at[a, k], device_id=to, device_id_type=MESH)

        mine = [pltpu.make_async_copy(ins[a], block(a, *me), local_sems.at[a]) for a in range(n)]
        for cp in mine:
            cp.start()
        first = []
        for a in range(n):
            first.append(copy(a, 0, me, sibling, src=ins[a]))
            first += [copy(a, 1 + j, me, (*chip, c), src=ins[a]) for j, chip in enumerate(chips)]
        for cp in first:
            cp.start()

        def tables(i, carry):
            rows = pl.ds(pl.multiple_of(i * rows_per_pass, rows_per_pass), rows_per_pass)
            ang = pos_ref[rows, :].astype(F32) * invf_ref[...]
            cos_ref[rows, :] = jnp.cos(ang)
            sin_ref[rows, :] = jnp.sin(ang) * sign_ref[...]
            return carry

        lax.fori_loop(0, T // rows_per_pass, tables, 0)

        passed = []
        for j, chip in enumerate(chips):
            for a in range(n):
                copy(a, 1 + j, (*chip, c), me).wait_recv()
                fwd = copy(a, 4 + j, (*chip, c), sibling)
                fwd.start()
                passed.append(fwd)
        for a in range(n):
            copy(a, 0, sibling, me).wait_recv()
            for j, chip in enumerate(chips):
                copy(a, 4 + j, (*chip, 1 - c), me).wait_recv()
        for cp in first + passed:
            cp.wait_send()
        for cp in mine:
            cp.wait()

    vmem = pl.BlockSpec(memory_space=pltpu.VMEM)
    res = pl.pallas_call(
        body, name="gather_weights",
        in_specs=_any_specs(n) + [vmem] * 3, out_specs=_any_specs(n) + [vmem] * 2,
        out_shape=[jax.ShapeDtypeStruct((N_DEV, *s.shape), s.dtype) for s in shards]
                  + [jax.ShapeDtypeStruct((T, LANES), F32)] * 2,
        scratch_shapes=[pltpu.SemaphoreType.DMA((n, 7)), pltpu.SemaphoreType.DMA((n, 7)), pltpu.SemaphoreType.DMA((n,))],
        compiler_params=pltpu.CompilerParams(vmem_limit_bytes=V7X_VMEM_LIMIT),
    )(*shards, pos_col, invf, sign)
    return res[:n], res[n], res[n + 1]


def _w_in_grad_rs(h, dsh, chip_order, small):
    T = h.shape[0]
    tk = math.gcd(T, 2048)
    nk = T // tk
    chip_flips = [(1, 1), (1, 0), (0, 1)]
    n_steps = len(chip_flips) + 1
    SIB = len(chip_flips)
    k_finish = min(1, nk - 1)

    def body(order_ref, h_ref, d_ref, s_ref, own_ref, recv_ref, sall_ref,
             acc_ref, keep_ref, pre_ref, to_sib_ref, to_chip_ref,
             sib_send, sib_recv, chip_send, chip_recv, ssend_sems, srecv_sems, local_sem):
        i, kk = pl.program_id(0), pl.program_id(1)
        x, y, c = _my_place()
        my_dev = 4 * x + 2 * y + c

        def small_copy(r, slot):
            dx, dy, dc = FLIPS[r]
            return pltpu.make_async_remote_copy(
                src_ref=s_ref, dst_ref=sall_ref.at[slot], send_sem=ssend_sems.at[r], recv_sem=srecv_sems.at[r],
                device_id=(x ^ dx, y ^ dy, c ^ dc), device_id_type=MESH)

        keep_small = pltpu.make_async_copy(s_ref, sall_ref.at[my_dev], local_sem)

        def sib_copy(t):
            dst = recv_ref.at[SIB] if t == SIB else pre_ref.at[t]
            return pltpu.make_async_remote_copy(
                src_ref=to_sib_ref.at[t], dst_ref=dst, send_sem=sib_send.at[t], recv_sem=sib_recv.at[t],
                device_id=(x, y, 1 - c), device_id_type=MESH)

        def chip_copy(t):
            dx, dy = chip_flips[t]
            return pltpu.make_async_remote_copy(
                src_ref=to_chip_ref.at[t], dst_ref=recv_ref.at[t], send_sem=chip_send.at[t], recv_sem=chip_recv.at[t],
                device_id=(x ^ dx, y ^ dy, c), device_id_type=MESH)

        def halves():
            first, second = acc_ref[0:SHARD_PAD, :], acc_ref[SHARD_PAD:2 * SHARD_PAD, :]
            return jnp.where(c == 0, first, second), jnp.where(c == 0, second, first)

        @pl.when((i == 0) & (kk == 0))
        def _():
            keep_small.start()
            for r in range(len(FLIPS)):
                small_copy(r, my_dev).start()

        @pl.when(kk == 0)
        def _():
            acc_ref[...] = jnp.zeros_like(acc_ref)

        acc_ref[...] += _dot_tn(d_ref[...], h_ref[...])

        for t in range(len(chip_flips)):
            @pl.when((i == t + 1) & (kk == k_finish))
            def _(t=t):
                sib_copy(t).wait_recv()
                to_chip_ref[t] = (keep_ref[...] + pre_ref[t].astype(F32)).astype(BF16)
                chip_copy(t).start()

        for t in range(len(chip_flips)):
            @pl.when((i == t) & (kk == nk - 1))
            def _(t=t):
                mine, theirs = halves()
                to_sib_ref[t] = theirs.astype(BF16)
                sib_copy(t).start()
                keep_ref[...] = mine

        @pl.when((i == n_steps - 1) & (kk == nk - 1))
        def _():
            mine, theirs = halves()
            own_ref[...] = mine
            to_sib_ref[SIB] = theirs.astype(BF16)
            sib_copy(SIB).start()
            for t in range(len(chip_flips)):
                sib_copy(t).wait_send()
                chip_copy(t).wait_send()
                chip_copy(t).wait_recv()
            sib_copy(SIB).wait_send()
            sib_copy(SIB).wait_recv()
            for r, (dx, dy, dc) in enumerate(FLIPS):
                small_copy(r, 4 * (x ^ dx) + 2 * (y ^ dy) + (c ^ dc)).wait_recv()
                small_copy(r, my_dev).wait_send()
            keep_small.wait()

    shard = (SHARD_PAD, D_MODEL)
    return pl.pallas_call(
        body, name="w_in_grad_rs",
        grid_spec=pltpu.PrefetchScalarGridSpec(
            num_scalar_prefetch=1, grid=(n_steps, nk),
            in_specs=[pl.BlockSpec((tk, D_MODEL), lambda i, kk, order: (kk, 0)),
                      pl.BlockSpec((None, tk, 2 * SHARD_PAD), lambda i, kk, order: (order[i], kk, 0)),
                      pl.BlockSpec(memory_space=pl.ANY)],
            out_specs=[pl.BlockSpec(shard, lambda i, kk, order: (0, 0)),
                       pl.BlockSpec(memory_space=pl.ANY), pl.BlockSpec(memory_space=pl.ANY)],
            scratch_shapes=[pltpu.VMEM((2 * SHARD_PAD, D_MODEL), F32), pltpu.VMEM(shard, F32),
                            pltpu.VMEM((SIB, *shard), BF16), pltpu.VMEM((SIB + 1, *shard), BF16),
                            pltpu.VMEM((SIB, *shard), BF16),
                            pltpu.SemaphoreType.DMA((SIB + 1,)), pltpu.SemaphoreType.DMA((SIB + 1,)),
                            pltpu.SemaphoreType.DMA((SIB,)), pltpu.SemaphoreType.DMA((SIB,)),
                            pltpu.SemaphoreType.DMA((7,)), pltpu.SemaphoreType.DMA((7,)), pltpu.SemaphoreType.DMA]),
        out_shape=[jax.ShapeDtypeStruct(shard, F32),
                   jax.ShapeDtypeStruct((SIB + 1, *shard), BF16),
                   jax.ShapeDtypeStruct((N_DEV, *small.shape), F32)],
        compiler_params=_params("arbitrary", "arbitrary"),
    )(chip_order, h, dsh, small)


def _adam_math(w, g, m, v):
    m_new = ADAM_B1 * m + (1.0 - ADAM_B1) * g
    v_new = ADAM_B2 * v + (1.0 - ADAM_B2) * (g * g)
    m_hat = m_new / (1.0 - ADAM_B1 ** ADAM_STEP)
    v_hat = v_new / (1.0 - ADAM_B2 ** ADAM_STEP)
    delta = -ADAM_LR * (m_hat / (jnp.sqrt(v_hat) + ADAM_EPS) + ADAM_WD * w)
    return delta, m_new, v_new


def _adam_big(jobs):
    steps = 8
    n = len(jobs)
    idx = jnp.stack([job[1] for job in jobs]).astype(jnp.int32)
    blocks = []
    for own, _, recv, w, m, v in jobs:
        (rw, cw), rp = w.shape, own.shape[1]
        by_cols = rp != rw
        blk_w = (rw, cw // steps) if by_cols else (rw // steps, cw)
        blk_g = (rp, cw // steps) if by_cols else (rw // steps, cw)
        blocks.append((blk_w, blk_g, by_cols))

    def body(idx_ref, *refs):
        ins, outs = refs[:5 * n], refs[5 * n:]
        for j, (blk_w, _, _) in enumerate(blocks):
            o_ref, r_ref, w_ref, m_ref, v_ref = ins[5 * j:5 * j + 5]
            g_ref, d_ref, mo_ref, vo_ref = outs[4 * j:4 * j + 4]
            g = o_ref[...].astype(F32)
            for r in range(r_ref.shape[0]):
                g = g + r_ref[r].astype(F32)
            g = g[0:blk_w[0], :]
            g_ref[...] = g
            d_ref[...], mo_ref[...], vo_ref[...] = _adam_math(w_ref[...], g, m_ref[...], v_ref[...])

    in_specs, out_specs, out_shape, args = [], [], [], []
    for j, ((own, _, recv, w, m, v), (blk_w, blk_g, by_cols)) in enumerate(zip(jobs, blocks)):
        at = (lambda i: (0, i)) if by_cols else (lambda i: (i, 0))
        spec = pl.BlockSpec(blk_w, lambda i, idx_ref, at=at: at(i))
        in_specs += [pl.BlockSpec((None, *blk_g), lambda i, idx_ref, at=at, j=j: (idx_ref[j], *at(i))),
                     pl.BlockSpec((recv.shape[0], *blk_g), lambda i, idx_ref, at=at: (0, *at(i))), spec, spec, spec]
        out_specs += [spec] * 4
        out_shape += [jax.ShapeDtypeStruct(w.shape, F32)] * 4
        args += [own, recv, w, m, v]
    res = pl.pallas_call(
        body, name="adam_big",
        grid_spec=pltpu.PrefetchScalarGridSpec(num_scalar_prefetch=1, grid=(steps,), in_specs=in_specs, out_specs=out_specs),
        out_shape=out_shape,
        compiler_params=_params("parallel"),
    )(idx, *args)
    return [res[4 * j:4 * j + 4] for j in range(n)]


def _adam_small(small_all, params):
    flat = [a for triple in params for a in triple]
    n_par = len(params)

    def body(s_ref, *refs):
        ins, outs, loss_ref = refs[:3 * n_par], refs[3 * n_par:-1], refs[-1]
        g_slab = s_ref[0]
        for dev in range(1, N_DEV):
            g_slab = g_slab + s_ref[dev]
        loss_ref[...] = g_slab[SMALL_LOSS:SMALL_LOSS + 1, :]
        dev = 4 * lax.axis_index("x") + 2 * lax.axis_index("y") + lax.axis_index("c")
        alpha_full = jnp.concatenate([g_slab[SMALL_W_ALPHA + half * B_GATE_RANK:SMALL_W_ALPHA + (half + 1) * B_GATE_RANK]
                                      for half in range(B_KEY_WIDTH // LANES)], axis=1)
        alpha_mine = pltpu.roll(alpha_full, (B_KEY_WIDTH - dev * SHARD_ALPHA) % B_KEY_WIDTH, 1)[:, 0:SHARD_ALPHA]
        grads = [_take_rows(g_slab, SMALL_G_IN, D_MODEL // LANES), _take_rows(g_slab, SMALL_G_FINAL, D_MODEL // LANES),
                 _take_rows(g_slab, SMALL_G_GLA, B_WIDTH // LANES), _take_rows(g_slab, SMALL_B_ALPHA, B_KEY_WIDTH // LANES),
                 g_slab[SMALL_SINKS:SMALL_SINKS + 1, 0:A_HEADS], alpha_mine]
        for i, g in enumerate(grads):
            w_ref, m_ref, v_ref = ins[3 * i:3 * i + 3]
            delta, m_new, v_new = _adam_math(w_ref[...], g, m_ref[...], v_ref[...])
            outs[4 * i][...] = g
            outs[4 * i + 1][...] = delta
            outs[4 * i + 2][...] = m_new
            outs[4 * i + 3][...] = v_new

    res = pl.pallas_call(
        body, name="adam_small",
        out_shape=[jax.ShapeDtypeStruct(t[0].shape, F32) for t in params for _ in range(4)]
                  + [jax.ShapeDtypeStruct((1, LANES), F32)],
    )(small_all, *flat)
    return [res[4 * i:4 * i + 4] for i in range(n_par)], res[-1]


def _local_step(x, cosf, sinf, loss_target, g_in, wt_sh, wa_pad, b_alpha, sinks, g_gla, out_shards, g_final, chip_order):
    B, S, _ = x.shape
    T = B * S
    x2 = x.reshape(T, D_MODEL)
    tgt2 = loss_target.reshape(T, D_MODEL)
    f, (g_woa, g_wob, g_wo) = _in_proj(x2, cosf, sinf, g_in, wt_sh, wa_pad, b_alpha, out_shards)
    w_o = g_wo.reshape(D_MODEL, D_MODEL)
    sink_row = jnp.repeat(sinks, WINDOW).reshape(1, ATT_ROWS)
    sink_col = sink_row.reshape(ATT_ROWS, 1)
    attn, lse = _attn_fwd(f["qkv"], sink_row, B, S)
    o_gla, st_all = _gla_fwd(f["q"], f["k"], f["cum"], f["vb"], B, S)
    (dxres, dattn, dog, dza, dzb, dga, dgb, dw_o, dw_oa, dw_ob, small_a) = _merge(
        x2, tgt2, attn, f["za"], o_gla, f["zb"], f["ga"], f["gb"], g_woa, g_wob, w_o, g_gla, g_final)
    dq, dkv, dsink = _attn_bwd(f["qkv"], dattn, attn, lse, sink_col, B, S)
    (dqb, dkb, dvb, dla), (rv_o, rv_oa, rv_ob) = _gla_bwd(f["q"], f["k"], f["cum"], f["vb"], dog, st_all, B, S,
                                                        [dw_o, dw_oa, dw_ob])
    parts = dict(dq=dq, dkv=dkv, dza=dza, dqb=dqb, dkb=dkb, dvb=dvb, dzb=dzb, dla=dla, u=f["u"], alr=f["alr"],
                 dga=dga, dgb=dgb)
    dx, dsh, small_c = _in_proj_bwd(x2, dxres, cosf, sinf, g_in, f["wt_pad"], wa_pad, parts)
    small = jnp.concatenate([small_a, dsink, small_c], axis=0)
    own_in, rv_in, small_all = _w_in_grad_rs(f["h"], dsh, chip_order, small)
    return dict(grad_x=dx.reshape(B, S, D_MODEL), own_in=own_in, rv_in=rv_in,
                own_o=dw_o, rv_o=rv_o, own_oa=dw_oa, rv_oa=rv_oa, own_ob=dw_ob, rv_ob=rv_ob, small_all=small_all)


def kernel(x, positions, g_in, w_in, w_alpha_up, b_alpha, attn_sinks, g_gla_norm, w_out_a, w_out_b, w_o, g_final, loss_target, m_g_in, m_w_in, m_w_alpha_up, m_b_alpha, m_attn_sinks, m_g_gla_norm, m_w_out_a, m_w_out_b, m_w_o, m_g_final, v_g_in, v_w_in, v_w_alpha_up, v_b_alpha, v_attn_sinks, v_g_gla_norm, v_w_out_a, v_w_out_b, v_w_o, v_g_final):
    xi, yi, ci = _my_place()
    chip = 2 * xi + yi
    chip_order = jnp.stack([chip ^ 3, chip ^ 2, chip ^ 1, chip]).astype(jnp.int32)

    (g_win, g_wa), cosf, sinf = _gather_first(
        [jnp.pad(w_in[0].T.astype(BF16), ((0, SHARD_PAD - SHARD_IN), (0, 0))), w_alpha_up[0].astype(BF16)],
        positions.reshape(-1, 1))
    wt_sh = g_win.reshape(N_DEV * SHARD_PAD, D_MODEL)
    wa_pad = jnp.pad(jnp.concatenate([g_wa[j] for j in range(N_DEV)], axis=1), ((0, RANK_PAD - B_GATE_RANK), (0, 0)))

    r = _local_step(x, cosf, sinf, loss_target, g_in, wt_sh, wa_pad, b_alpha, attn_sinks[0], g_gla_norm,
                    [w_out_a[0].astype(BF16), w_out_b[0].astype(BF16), w_o[0].astype(BF16)],
                    g_final.reshape(1, D_MODEL), chip_order)

    dev = 4 * xi + 2 * yi + ci
    big = _adam_big([(r["own_in"][None], jnp.int32(0), r["rv_in"], w_in[0].T, m_w_in[0].T, v_w_in[0].T),
                     (r["own_oa"], dev, r["rv_oa"], w_out_a[0], m_w_out_a[0], v_w_out_a[0]),
                     (r["own_ob"], dev, r["rv_ob"], w_out_b[0], m_w_out_b[0], v_w_out_b[0]),
                     (r["own_o"], dev, r["rv_o"], w_o[0], m_w_o[0], v_w_o[0])])
    big[0] = [a.T for a in big[0]]
    row = lambda a: a.reshape(1, D_MODEL)
    (s_in, s_final, s_gla, s_ba, s_sinks, s_wa), loss_row = _adam_small(r["small_all"], [
        (g_in, m_g_in, v_g_in), (row(g_final), row(m_g_final), row(v_g_final)),
        (g_gla_norm, m_g_gla_norm, v_g_gla_norm), (b_alpha, m_b_alpha, v_b_alpha),
        (attn_sinks, m_attn_sinks, v_attn_sinks), (w_alpha_up[0], m_w_alpha_up[0], v_w_alpha_up[0])])

    def group(i):
        return (s_in[i], big[0][i][None], s_wa[i][None], s_ba[i], s_sinks[i], s_gla[i], big[1][i][None], big[2][i][None],
                big[3][i][None], s_final[i].reshape(D_MODEL))

    return (loss_row[0, 0], r["grad_x"], *group(0), *group(1), *group(2), *group(3))
```

```python
import functools
import math

import numpy as np
import jax
import jax.numpy as jnp
from jax import lax
from jax.experimental import pallas as pl
from jax.experimental.pallas import tpu as pltpu

F32 = jnp.float32
BF16 = jnp.bfloat16
MESH = pl.DeviceIdType.MESH

D_MODEL = 1024
A_HEADS, A_KV_HEADS, A_HEAD_DIM = 8, 2, 64
A_WIDTH, A_KV_WIDTH = 512, 128
WINDOW = 128
ROPE_THETA = 500000.0
ROPE_DIM = 16
B_HEADS, B_KEY_DIM, B_VAL_DIM = 4, 64, 128
B_KEY_WIDTH, B_WIDTH = 256, 512
B_GATE_RANK = 16
B_GATE_TEMP = 16.0
B_CHUNK = 64
NORM_EPS = 1e-6
NEG_BIG = -1e30
D_IN = 4880
N_DEV = 8
N_CHIPS = 4
ADAM_LR, ADAM_B1, ADAM_B2, ADAM_EPS, ADAM_WD, ADAM_STEP = 0.001, 0.9, 0.999, 1e-08, 0.01, 10

LANES = 128
V7X_VMEM_LIMIT = 56 * 1024 * 1024
V7X_VMEM_LIMIT_MAX = 62 * 1024 * 1024

RANK_PAD = LANES
SEG = {}
_off = 0
for _name, _w in (("qa", 512), ("ka", 128), ("va", 128), ("za", 512), ("qb", 256), ("kb", 256),
                  ("vb", 512), ("zb", 512), ("alr", RANK_PAD), ("ga", 1024), ("gb", 1024)):
    SEG[_name] = (_off, _off + _w)
    _off += _w
D_IN_PAD = _off
ALR_SRC = SEG["alr"][0]
QKV_K, QKV_V, QKV_W = SEG["ka"][0], SEG["va"][0], SEG["va"][1]
ATT_SCALE = A_HEAD_DIM ** -0.5

SHARD_IN = D_IN // N_DEV
SHARD_PAD = 640
SHARD_OUT = D_MODEL // N_DEV
SHARD_ALPHA = B_KEY_WIDTH // N_DEV

SMALL_G_FINAL, SMALL_G_GLA, SMALL_LOSS, SMALL_SINKS, SMALL_G_IN, SMALL_B_ALPHA, SMALL_W_ALPHA = 0, 8, 12, 16, 24, 32, 40
SMALL_ROWS = 72


def _dot(a, b):
    return jnp.dot(a, b, preferred_element_type=F32)


def _dot_nt(a, b):
    return lax.dot_general(a, b, (((1,), (1,)), ((), ())), preferred_element_type=F32)


def _dot_tn(a, b):
    return lax.dot_general(a, b, (((0,), (0,)), ((), ())), preferred_element_type=F32)


def _sigmoid(z):
    return 1.0 / (1.0 + jnp.exp(-z))


def _sigmoid_tanh(z):
    return 0.5 * jnp.tanh(0.5 * z) + 0.5


def _params(*sem):
    return pltpu.CompilerParams(dimension_semantics=sem, vmem_limit_bytes=V7X_VMEM_LIMIT)


def _const_spec(shape):
    nd = len(shape)
    return pl.BlockSpec(shape, lambda *_: (0,) * nd, pipeline_mode=pl.Buffered(1))


def _lane_iota(shape):
    return lax.broadcasted_iota(jnp.int32, shape, 1)


def _row_iota(shape):
    return lax.broadcasted_iota(jnp.int32, shape, 0)


def _split3(v):
    hi = v.astype(BF16)
    r1 = v - hi.astype(F32)
    mid = r1.astype(BF16)
    lo = (r1 - mid.astype(F32)).astype(BF16)
    return hi, mid, lo


def _put_rows(ref, row0, vec):
    for r in range(vec.shape[1] // LANES):
        ref[row0 + r:row0 + r + 1, :] = vec[:, r * LANES:(r + 1) * LANES]


def _take_rows(slab, row0, n):
    return jnp.concatenate([slab[row0 + r:row0 + r + 1, :] for r in range(n)], axis=1)


def _rope_lane_constants():
    half = ROPE_DIM // 2
    inv_freq = np.exp(-math.log(ROPE_THETA) * np.arange(half, dtype=np.float32) * np.float32(2.0 / ROPE_DIM)).astype(np.float32)
    lane = np.arange(LANES)
    j = lane % A_HEAD_DIM
    invf = np.where(j < ROPE_DIM, inv_freq[j % half], 0.0).astype(np.float32)
    sign = np.where(j < half, -1.0, np.where(j < ROPE_DIM, 1.0, 0.0)).astype(np.float32)
    return jnp.asarray(invf)[None, :], jnp.asarray(sign)[None, :]


def _rope_slab(t, cos, sin_signed):
    first = (_lane_iota(t.shape) % A_HEAD_DIM) < (ROPE_DIM // 2)
    partner = jnp.where(first, pltpu.roll(t, LANES - ROPE_DIM // 2, 1), pltpu.roll(t, ROPE_DIM // 2, 1))
    return t * cos + partner * sin_signed


def _shard_pad_cols(j):
    cut = ALR_SRC + B_GATE_RANK
    shift = RANK_PAD - B_GATE_RANK
    a, b = j * SHARD_IN, (j + 1) * SHARD_IN
    if b <= cut:
        return [(a, b)]
    if a >= cut:
        return [(a + shift, b + shift)]
    return [(a, cut), (cut + shift, b + shift)]


def _in_proj(x2, cosf, sinf, g_in, wt_sh, wa_pad, b_alpha, later_shards):
    T = x2.shape[0]
    tm = math.gcd(T, 512)
    sub = math.gcd(tm, 256)
    last = T // tm - 1
    nl = len(later_shards)

    def body(x_ref, cos_ref, sin_ref, g_ref, wsh_ref, wa_ref, ba_ref, *rest):
        sh_refs, rest = rest[:nl], rest[nl:]
        (h_ref, qkv_ref, za_ref, q_ref, k_ref, vb_ref, zb_ref, alr_ref, u_ref, cum_ref, ga_ref, gb_ref, wt_out) = rest[:13]
        all_refs, (wt_ref, send_sems, recv_sems, local_sems, wt_sem) = rest[13:13 + nl], rest[13 + nl:]
        wt_copy = pltpu.make_async_copy(wt_ref, wt_out, wt_sem)
        px, py, pc = _my_place()
        my_dev = 4 * px + 2 * py + pc

        def wcopy(a, r, slot):
            dx, dy, dc = FLIPS[r]
            return pltpu.make_async_remote_copy(
                src_ref=sh_refs[a], dst_ref=all_refs[a].at[slot], send_sem=send_sems.at[a, r],
                recv_sem=recv_sems.at[a, r], device_id=(px ^ dx, py ^ dy, pc ^ dc), device_id_type=MESH)

        keep = [pltpu.make_async_copy(sh_refs[a], all_refs[a].at[my_dev], local_sems.at[a]) for a in range(nl)]

        @pl.when(pl.program_id(0) == 0)
        def _():
            for a in range(nl):
                keep[a].start()
                for r in range(len(FLIPS)):
                    wcopy(a, r, my_dev).start()

        @pl.when(pl.program_id(0) == 0)
        def _():
            for j in range(N_DEV):
                src = j * SHARD_PAD
                for a, b in _shard_pad_cols(j):
                    wt_ref[a:b, :] = wsh_ref[src:src + b - a, :]
                    src += b - a
            a, b = SEG["alr"]
            wt_ref[a + B_GATE_RANK:b, :] = jnp.zeros((RANK_PAD - B_GATE_RANK, D_MODEL), BF16)
            wt_copy.start()

        def one_tile(rows):
            x = x_ref[rows, :]
            r = lax.rsqrt(jnp.mean(x * x, axis=-1, keepdims=True) + NORM_EPS)
            h = (x * r * g_ref[...]).astype(BF16)
            h_ref[rows, :] = h

            def seg(name):
                a, b = SEG[name]
                return _dot_nt(h, wt_ref[a:b, :])

            alr = seg("alr").astype(BF16)
            alr_ref[rows, :] = alr
            u = _dot(alr, wa_ref[...]) + ba_ref[...]
            u_ref[rows, :] = u
            log_a = (jnp.minimum(u, 0.0) - jnp.log(1.0 + jnp.exp(-jnp.abs(u)))) * (1.0 / B_GATE_TEMP)
            row, col = _row_iota((sub, sub)), _lane_iota((sub, sub))
            tri = ((row // B_CHUNK == col // B_CHUNK) & (col <= row)).astype(BF16)
            hi, mid, lo = _split3(log_a)
            cum_ref[rows, :] = _dot(tri, hi) + _dot(tri, mid) + _dot(tri, lo)

            cos, sin = cos_ref[rows, :], sin_ref[rows, :]
            qa = seg("qa") * ATT_SCALE
            for s in range(A_WIDTH // LANES):
                qkv_ref[rows, s * LANES:(s + 1) * LANES] = _rope_slab(qa[:, s * LANES:(s + 1) * LANES], cos, sin).astype(BF16)
            qkv_ref[rows, QKV_K:QKV_V] = _rope_slab(seg("ka"), cos, sin).astype(BF16)
            qkv_ref[rows, QKV_V:QKV_W] = seg("va").astype(BF16)
            za_ref[rows, :] = seg("za").astype(BF16)
            q_ref[rows, :] = seg("qb")
            k_ref[rows, :] = seg("kb")
            vb_ref[rows, :] = seg("vb").astype(BF16)
            zb_ref[rows, :] = seg("zb").astype(BF16)
            ga_ref[rows, :] = seg("ga").astype(BF16)
            gb_ref[rows, :] = seg("gb").astype(BF16)

        for j in range(tm // sub):
            one_tile(pl.ds(j * sub, sub))

        @pl.when(pl.program_id(0) == last)
        def _():
            for a in range(nl):
                for r, (dx, dy, dc) in enumerate(FLIPS):
                    wcopy(a, r, 4 * (px ^ dx) + 2 * (py ^ dy) + (pc ^ dc)).wait_recv()
                    wcopy(a, r, my_dev).wait_send()
                keep[a].wait()
            wt_copy.wait()

    def rows(w):
        return pl.BlockSpec((tm, w), lambda i: (i, 0))

    outs = [("h", D_MODEL, BF16), ("qkv", QKV_W, BF16), ("za", A_WIDTH, BF16), ("q", B_KEY_WIDTH, F32),
            ("k", B_KEY_WIDTH, F32), ("vb", B_WIDTH, BF16), ("zb", B_WIDTH, BF16), ("alr", RANK_PAD, BF16),
            ("u", B_KEY_WIDTH, F32), ("cum", B_KEY_WIDTH, F32), ("ga", D_MODEL, BF16), ("gb", D_MODEL, BF16)]
    res = pl.pallas_call(
        body, name="in_proj", grid=(T // tm,),
        in_specs=[rows(D_MODEL), rows(LANES), rows(LANES), _const_spec((1, D_MODEL)),
                  _const_spec((N_DEV * SHARD_PAD, D_MODEL)), _const_spec((RANK_PAD, B_KEY_WIDTH)),
                  _const_spec((1, B_KEY_WIDTH))] + _any_specs(nl),
        out_specs=[rows(w) for _, w, _ in outs] + _any_specs(1 + nl),
        out_shape=[jax.ShapeDtypeStruct((T, w), dt) for _, w, dt in outs]
                  + [jax.ShapeDtypeStruct((D_IN_PAD, D_MODEL), BF16)]
                  + [jax.ShapeDtypeStruct((N_DEV, *sh.shape), sh.dtype) for sh in later_shards],
        scratch_shapes=[pltpu.VMEM((D_IN_PAD, D_MODEL), BF16),
                        pltpu.SemaphoreType.DMA((nl, len(FLIPS))), pltpu.SemaphoreType.DMA((nl, len(FLIPS))),
                        pltpu.SemaphoreType.DMA((nl,)), pltpu.SemaphoreType.DMA],
        compiler_params=_params("arbitrary"),
    )(x2, cosf, sinf, g_in, wt_sh, wa_pad, b_alpha, *later_shards)
    n_out = len(outs) + 1
    return dict(zip([n for n, _, _ in outs] + ["wt_pad"], res[:n_out])), res[n_out:]


def _dup_kv_head(t, g):
    tf = t.astype(F32)
    keep = (_lane_iota(tf.shape) < A_HEAD_DIM) == (g == 0)
    return jnp.where(keep, tf, pltpu.roll(tf, A_HEAD_DIM, 1)).astype(BF16)


def _stack_heads(t):
    lo = _lane_iota(t.shape) < A_HEAD_DIM
    zero = jnp.zeros_like(t)
    return jnp.concatenate([jnp.where(lo, t, zero), jnp.where(lo, zero, t)], axis=0)


ATT_ROWS = A_HEADS * WINDOW
GROUP_ROWS = ATT_ROWS // A_KV_HEADS
HEADS_PER_GROUP = A_HEADS // A_KV_HEADS


def _band_mask_t(n):
    kj = _row_iota((2 * WINDOW, GROUP_ROWS)) - WINDOW
    qi = _lane_iota((2 * WINDOW, GROUP_ROWS)) % WINDOW
    return (kj <= qi) & (qi - kj < WINDOW) & ((n > 0) | (kj >= 0))


def _stacked_queries(ref, g):
    pairs = range(g * HEADS_PER_GROUP // 2, (g + 1) * HEADS_PER_GROUP // 2)
    return jnp.concatenate([_stack_heads(ref[:, p * LANES:(p + 1) * LANES]) for p in pairs], axis=0)


def _unstack_heads(t, g, ref, dtype):
    lo = _lane_iota((WINDOW, LANES)) < A_HEAD_DIM
    for hh in range(HEADS_PER_GROUP // 2):
        p = g * HEADS_PER_GROUP // 2 + hh
        ref[:, p * LANES:(p + 1) * LANES] = jnp.where(lo, t[2 * hh * WINDOW:(2 * hh + 1) * WINDOW],
                                                       t[(2 * hh + 1) * WINDOW:(2 * hh + 2) * WINDOW]).astype(dtype)


FWD_BLOCKS = 16


def _attn_fwd(qkv, sink_row, B, S):
    T = B * S
    nb = S // WINDOW
    blocks = math.gcd(nb, FWD_BLOCKS)
    steps = nb // blocks

    def one_block(has_prev, sink_ref, q, k, v, o_ref, lse_ref):
        valid = _band_mask_t(has_prev)
        lse_rows = []
        for g in range(A_KV_HEADS):
            kd, vd = _dup_kv_head(k, g), _dup_kv_head(v, g)
            s = jnp.where(valid, _dot_nt(kd, _stacked_queries(q, g)), NEG_BIG)
            sink = sink_ref[:, g * GROUP_ROWS:(g + 1) * GROUP_ROWS]
            m = jnp.maximum(jnp.max(s, axis=0, keepdims=True), sink)
            e = jnp.exp(s - m)
            den = jnp.sum(e, axis=0, keepdims=True) + jnp.exp(sink - m)
            o = _dot_tn((e * (1.0 / den)).astype(BF16), vd)
            _unstack_heads(o, g, o_ref, F32)
            lse = m + jnp.log(den)
            lse_rows += [lse[:, j * WINDOW:(j + 1) * WINDOW] for j in range(HEADS_PER_GROUP)]
        by_head = jnp.concatenate(lse_rows + [jnp.zeros((WINDOW - A_HEADS, WINDOW), F32)], axis=0)
        lse_ref[...] = by_head.T

    def body(sink_ref, q_ref, kc_ref, vc_ref, kp_ref, vp_ref, o_ref, lse_ref):
        k_all = jnp.concatenate([kp_ref[...], kc_ref[...]], axis=0)
        v_all = jnp.concatenate([vp_ref[...], vc_ref[...]], axis=0)
        for j in range(blocks):
            rows = pl.ds(j * WINDOW, WINDOW)
            keys = slice(j * WINDOW, (j + 2) * WINDOW)
            has_prev = pl.program_id(1) if j == 0 else 1
            one_block(has_prev, sink_ref, q_ref[rows, :], k_all[keys], v_all[keys], o_ref.at[rows], lse_ref.at[rows])

    def cur(col, w):
        return pl.BlockSpec((blocks * WINDOW, w), lambda b, n: (b * steps + n, col))

    def prev(col):
        return pl.BlockSpec((WINDOW, LANES), lambda b, n: (b * nb + jnp.maximum(blocks * n - 1, 0), col))

    kcol, vcol = QKV_K // LANES, QKV_V // LANES
    return pl.pallas_call(
        body, name="attn_fwd", grid=(B, steps),
        in_specs=[_const_spec((1, ATT_ROWS)), cur(0, A_WIDTH), cur(kcol, LANES), cur(vcol, LANES), prev(kcol), prev(vcol)],
        out_specs=[cur(0, A_WIDTH), cur(0, LANES)],
        out_shape=[jax.ShapeDtypeStruct((T, A_WIDTH), F32), jax.ShapeDtypeStruct((T, LANES), F32)],
        compiler_params=_params("parallel", "parallel"),
    )(sink_row, qkv, qkv, qkv, qkv, qkv)


ATT_CHUNK = 128


def _chunk_masks(n):
    masks = []
    for half in range(WINDOW // ATT_CHUNK):
        qi = _row_iota((ATT_CHUNK, 2 * WINDOW)) + half * ATT_CHUNK
        kj = _lane_iota((ATT_CHUNK, 2 * WINDOW)) - WINDOW
        masks.append((kj <= qi) & (qi - kj < WINDOW) & ((n > 0) | (kj >= 0)))
    return masks


def _all_stacked_queries(ref):
    return jnp.concatenate([_stacked_queries(ref, g) for g in range(A_KV_HEADS)], axis=0)


def _by_group(fn, lhs, rhs_per_group):
    return jnp.concatenate([fn(lhs[g * GROUP_ROWS:(g + 1) * GROUP_ROWS], rhs_per_group[g])
                            for g in range(A_KV_HEADS)], axis=0)


BWD_BLOCKS = 8


def _attn_bwd(qkv, do, out, lse, sink_col, B, S):
    T = B * S
    nb = S // WINDOW
    M = math.gcd(nb, BWD_BLOCKS)
    steps = nb // M
    n_chunks = ATT_ROWS // ATT_CHUNK
    halves = WINDOW // ATT_CHUNK

    def block(has_prev, sink_ref, q, do_b, out_b, lse_b, k, v, scratch, want_dq):
        s_ref, dp_ref, ds_ref, p_ref = scratch
        width = k.shape[0]
        masks = [mk[:, 0:width] for mk in _chunk_masks(has_prev)]
        kd = [_dup_kv_head(k, g) for g in range(A_KV_HEADS)]
        vd = [_dup_kv_head(v, g) for g in range(A_KV_HEADS)]
        qs, dos = _all_stacked_queries(q), _all_stacked_queries(do_b)
        s_ref[...] = _by_group(_dot_nt, qs, kd)
        dp_ref[...] = _by_group(_dot_nt, dos, vd)
        lane = _lane_iota((ATT_CHUNK, LANES))
        lo = lane < A_HEAD_DIM
        lane1 = _lane_iota((1, LANES))
        dsink_row = jnp.zeros((1, LANES), F32)
        for c in range(n_chunks):
            rows = slice(c * ATT_CHUNK, (c + 1) * ATT_CHUNK)
            head, half = divmod(c, halves)
            qrows = slice(half * ATT_CHUNK, (half + 1) * ATT_CHUNK)
            slab = slice((head // 2) * LANES, (head // 2 + 1) * LANES)
            lse_col = jnp.sum(jnp.where(lane == head, lse_b[qrows, :], 0.0), axis=-1, keepdims=True)
            prod = do_b[qrows, slab].astype(F32) * out_b[qrows, slab].astype(F32)
            mine = lo if head % 2 == 0 else jnp.logical_not(lo)
            delta = jnp.sum(jnp.where(mine, prod, 0.0), axis=-1, keepdims=True)
            prob = jnp.exp(jnp.where(masks[half], s_ref[rows, :], NEG_BIG) - lse_col)
            p_ref[rows, :] = prob.astype(BF16)
            ds_ref[rows, :] = (prob * (dp_ref[rows, :] - delta)).astype(BF16)
            w = -jnp.exp(sink_ref[rows, :] - lse_col) * delta
            dsink_row += jnp.where(lane1 == head, jnp.sum(w, axis=0, keepdims=True), 0.0)
        dq = _by_group(_dot, ds_ref[...], kd) * ATT_SCALE if want_dq else None
        groups = [slice(g * GROUP_ROWS, (g + 1) * GROUP_ROWS) for g in range(A_KV_HEADS)]
        dk = [_dot_tn(ds_ref[rows, :], qs[rows]) for rows in groups]
        dv = [_dot_tn(p_ref[rows, :], dos[rows]) for rows in groups]
        return dq, dk, dv, dsink_row

    def fold(per_group):
        lane = _lane_iota((WINDOW, LANES))
        out = jnp.zeros((WINDOW, LANES), F32)
        for g, acc in enumerate(per_group):
            out = jnp.where((lane < A_HEAD_DIM) == (g == 0), acc + pltpu.roll(acc, A_HEAD_DIM, 1), out)
        return out

    def body(sink_ref, q_ref, qn_ref, do_ref, don_ref, out_ref, outn_ref, lse_ref, lsen_ref, kc_ref, kp_ref, vc_ref, vp_ref,
             dq_ref, dkv_ref, dsink_ref, s_scr, dp_scr, ds_scr, p_scr, s_x, dp_x, ds_x, p_x):
        b, m = pl.program_id(0), pl.program_id(1)

        @pl.when((b == 0) & (m == 0))
        def _():
            dsink_ref[...] = jnp.zeros_like(dsink_ref)

        k_all = jnp.concatenate([kp_ref[...], kc_ref[...]], axis=0)
        v_all = jnp.concatenate([vp_ref[...], vc_ref[...]], axis=0)
        results = []
        for j in range(M):
            rows = slice(j * WINDOW, (j + 1) * WINDOW)
            keys = slice(j * WINDOW, (j + 2) * WINDOW)
            has_prev = m if j == 0 else 1
            results.append(block(has_prev, sink_ref, q_ref[rows, :], do_ref[rows, :], out_ref[rows, :], lse_ref[rows, :],
                                 k_all[keys], v_all[keys], (s_scr.at[j], dp_scr.at[j], ds_scr.at[j], p_scr.at[j]), True))
        last_keys = slice(M * WINDOW, (M + 1) * WINDOW)
        _, dk_x, dv_x, _ = block(1, sink_ref, qn_ref[...], don_ref[...], outn_ref[...], lsen_ref[...],
                                 k_all[last_keys], v_all[last_keys], (s_x, dp_x, ds_x, p_x), False)
        has_next = m < steps - 1
        lo_q = _lane_iota((WINDOW, LANES)) < A_HEAD_DIM
        dsink_row = jnp.zeros((1, LANES), F32)
        for j, (dq, dk, dv, ds_row) in enumerate(results):
            rows = slice(j * WINDOW, (j + 1) * WINDOW)
            for p in range(A_HEADS // 2):
                dq_ref[rows, p * LANES:(p + 1) * LANES] = jnp.where(
                    lo_q, dq[2 * p * WINDOW:(2 * p + 1) * WINDOW], dq[(2 * p + 1) * WINDOW:(2 * p + 2) * WINDOW]).astype(BF16)
            if j + 1 < M:
                dk_next = [t[0:WINDOW] for t in results[j + 1][1]]
                dv_next = [t[0:WINDOW] for t in results[j + 1][2]]
            else:
                dk_next = [jnp.where(has_next, t, 0.0) for t in dk_x]
                dv_next = [jnp.where(has_next, t, 0.0) for t in dv_x]
            dkv_ref[rows, 0:LANES] = fold([own[WINDOW:] + nxt for own, nxt in zip(dk, dk_next)]).astype(BF16)
            dkv_ref[rows, LANES:] = fold([own[WINDOW:] + nxt for own, nxt in zip(dv, dv_next)]).astype(BF16)
            dsink_row += ds_row
        dsink_ref[0:1, :] += dsink_row

    def cur(col, w):
        return pl.BlockSpec((M * WINDOW, w), lambda b, m: (b * steps + m, col))

    def nxt(col, w):
        return pl.BlockSpec((WINDOW, w), lambda b, m: (b * nb + jnp.minimum(M * (m + 1), nb - 1), col))

    def prev(col):
        return pl.BlockSpec((WINDOW, LANES), lambda b, m: (b * nb + jnp.maximum(M * m - 1, 0), col))

    kcol, vcol = QKV_K // LANES, QKV_V // LANES
    scores = (M, ATT_ROWS, 2 * WINDOW)
    extra = (ATT_ROWS, WINDOW)
    return pl.pallas_call(
        body, name="attn_bwd", grid=(B, steps),
        in_specs=[_const_spec((ATT_ROWS, 1)), cur(0, A_WIDTH), nxt(0, A_WIDTH), cur(0, A_WIDTH), nxt(0, A_WIDTH),
                  cur(0, A_WIDTH), nxt(0, A_WIDTH), cur(0, LANES), nxt(0, LANES),
                  cur(kcol, LANES), prev(kcol), cur(vcol, LANES), prev(vcol)],
        out_specs=[cur(0, A_WIDTH), cur(0, 2 * LANES), pl.BlockSpec((8, LANES), lambda b, m: (0, 0))],
        out_shape=[jax.ShapeDtypeStruct((T, A_WIDTH), BF16), jax.ShapeDtypeStruct((T, 2 * LANES), BF16),
                   jax.ShapeDtypeStruct((8, LANES), F32)],
        scratch_shapes=[pltpu.VMEM(scores, F32), pltpu.VMEM(scores, F32), pltpu.VMEM(scores, BF16), pltpu.VMEM(scores, BF16),
                        pltpu.VMEM(extra, F32), pltpu.VMEM(extra, F32), pltpu.VMEM(extra, BF16), pltpu.VMEM(extra, BF16)],
        compiler_params=_params("arbitrary", "arbitrary"),
    )(sink_col, qkv, qkv, do, do, out, out, lse, lse, qkv, qkv, qkv, qkv)


GLA_FWD_TILING = (64, 16)
GLA_BWD_TILING = (128, 8)


def _gla_factors(q_ref, k_ref, cum_ref):
    cpt = q_ref.shape[0] // B_CHUNK
    scale = B_KEY_DIM ** -0.5
    cum = cum_ref[...]
    shape = (B_CHUNK, B_KEY_WIDTH)
    last = jnp.concatenate([jnp.broadcast_to(cum_ref[pl.ds(c * B_CHUNK + B_CHUNK - 1, 1), :], shape)
                            for c in range(cpt)], axis=0)
    mid = jnp.concatenate([jnp.broadcast_to(cum_ref[pl.ds(c * B_CHUNK + B_CHUNK // 2 - 1, 1), :], shape)
                           for c in range(cpt)], axis=0)
    e_qm, e_km, e_qe, e_kd = jnp.exp(cum - mid), jnp.exp(mid - cum), jnp.exp(cum), jnp.exp(last - cum)
    qs = q_ref[...] * scale
    k = k_ref[...]
    return qs, k, (e_qm, e_km, e_qe, e_kd)


def _head_mask(shape, h):
    return (_lane_iota(shape) // B_KEY_DIM) == h


def _stack_masked(t):
    return jnp.concatenate([jnp.where(_head_mask(t.shape, h), t, 0.0) for h in range(B_HEADS)], axis=0).astype(BF16)


def _select_heads(t):
    shape = (B_CHUNK, B_KEY_WIDTH)
    out = jnp.zeros(shape, F32)
    for h in range(B_HEADS):
        out = jnp.where(_head_mask(shape, h), t[h * B_CHUNK:(h + 1) * B_CHUNK], out)
    return out


def _select_state(t):
    shape = (B_VAL_DIM, B_KEY_WIDTH)
    out = jnp.zeros(shape, F32)
    for h in range(B_HEADS):
        out = jnp.where(_head_mask(shape, h), t[h * B_VAL_DIM:(h + 1) * B_VAL_DIM], out)
    return out


def _rows_by_head(t):
    return jnp.concatenate([t[:, h * B_VAL_DIM:(h + 1) * B_VAL_DIM] for h in range(B_HEADS)], axis=0)


def _intra_mask(tile_rows):
    i, j = _row_iota((tile_rows, tile_rows)), _lane_iota((tile_rows, tile_rows))
    return (i // B_CHUNK == j // B_CHUNK) & (j <= i)


def _pair_stack(t, p):
    slab = t[:, p * LANES:(p + 1) * LANES]
    lo = _lane_iota(slab.shape) < B_KEY_DIM
    return jnp.concatenate([jnp.where(lo, slab, 0.0), jnp.where(lo, 0.0, slab)], axis=0).astype(BF16)


def _gla_fwd(q, k, cum, vb, B, S):
    T = B * S
    tile_rows = math.gcd(S, GLA_FWD_TILING[0])
    cpt = tile_rows // B_CHUNK
    nt = S // tile_rows
    tps = math.gcd(nt, GLA_FWD_TILING[1])

    def one_sequence(q_ref, k_ref, cum_ref, v_ref, o_ref, st_all_ref, st_ref):
        qs, kk, (e_qm, e_km, e_qe, e_kd) = _gla_factors(q_ref, k_ref, cum_ref)
        qm, km, qe, kd = qs * e_qm, kk * e_km, qs * e_qe, (kk * e_kd).astype(BF16)
        mask = _intra_mask(tile_rows)
        intra = []
        for p in range(B_HEADS // 2):
            a = _dot_nt(_pair_stack(qm, p), km[:, p * LANES:(p + 1) * LANES].astype(BF16))
            for hh in range(2):
                h = 2 * p + hh
                att = jnp.where(mask, a[hh * tile_rows:(hh + 1) * tile_rows], 0.0).astype(BF16)
                intra.append(_dot(att, v_ref[:, h * B_VAL_DIM:(h + 1) * B_VAL_DIM]))
        inter = []
        for c in range(cpt):
            rows = slice(c * B_CHUNK, (c + 1) * B_CHUNK)
            st = st_ref[...]
            st_all_ref[c] = st
            inter.append(_dot_nt(_stack_masked(qe[rows]), st.astype(BF16)))
            inc = _select_state(_dot_tn(v_ref[rows, :], kd[rows]))
            decay = jnp.exp(cum_ref[pl.ds(c * B_CHUNK + B_CHUNK - 1, 1), :])
            st_ref[...] = st * decay + inc
        for h in range(B_HEADS):
            oi = jnp.concatenate([inter[c][h * B_CHUNK:(h + 1) * B_CHUNK] for c in range(cpt)], axis=0)
            o_ref[:, h * B_VAL_DIM:(h + 1) * B_VAL_DIM] = (intra[h] + oi).astype(BF16)

    def body(q_ref, k_ref, cum_ref, v_ref, o_ref, st_all_ref, st_ref):
        @pl.when(pl.program_id(0) == 0)
        def _():
            st_ref[...] = jnp.zeros_like(st_ref)

        for b in range(B):
            for tile in range(tps):
                tok = pl.ds(tile * tile_rows, tile_rows)
                chunks = pl.ds(tile * cpt, cpt)
                one_sequence(*[r.at[b, tok] for r in (q_ref, k_ref, cum_ref, v_ref, o_ref)],
                             st_all_ref.at[b, chunks], st_ref.at[b])

    def rows(w):
        return pl.BlockSpec((B, tps * tile_rows, w), lambda t: (0, t, 0))

    seq = lambda a: a.reshape(B, S, a.shape[-1])
    o, st_all = pl.pallas_call(
        body, name="gla_fwd", grid=(nt // tps,),
        in_specs=[rows(B_KEY_WIDTH), rows(B_KEY_WIDTH), rows(B_KEY_WIDTH), rows(B_WIDTH)],
        out_specs=[rows(B_WIDTH),
                   pl.BlockSpec((B, tps * cpt, B_VAL_DIM, B_KEY_WIDTH), lambda t: (0, t, 0, 0))],
        out_shape=[jax.ShapeDtypeStruct((B, S, B_WIDTH), BF16),
                   jax.ShapeDtypeStruct((B, S // B_CHUNK, B_VAL_DIM, B_KEY_WIDTH), F32)],
        scratch_shapes=[pltpu.VMEM((B, B_VAL_DIM, B_KEY_WIDTH), F32)],
        compiler_params=_params("arbitrary"),
    )(seq(q), seq(k), seq(cum), seq(vb))
    return o.reshape(T, B_WIDTH), st_all.reshape(T // B_CHUNK, B_VAL_DIM, B_KEY_WIDTH)


def _gla_bwd(q, k, cum, vb, do, st_all, B, S, wgrads):
    T = B * S
    tile_rows = math.gcd(S, GLA_BWD_TILING[0])
    cpt = tile_rows // B_CHUNK
    nt = S // tile_rows
    tps = math.gcd(nt, GLA_BWD_TILING[1])
    steps = nt // tps
    scale = B_KEY_DIM ** -0.5
    nw = len(wgrads)

    def one_sequence(q_ref, k_ref, cum_ref, v_ref, do_ref, st_all_ref, dq_ref, dk_ref, dv_ref, dla_ref, dst_ref):
        qs, kk, (e_qm, e_km, e_qe, e_kd) = _gla_factors(q_ref, k_ref, cum_ref)
        qm, km, qe, kd = qs * e_qm, kk * e_km, qs * e_qe, kk * e_kd
        mask = _intra_mask(tile_rows)
        dqm_slabs, dkm_slabs, dv_intra = [], [], []
        for p in range(B_HEADS // 2):
            qm_st = _pair_stack(qm, p)
            km_p = km[:, p * LANES:(p + 1) * LANES].astype(BF16)
            a = _dot_nt(qm_st, km_p)
            da_blocks, dqm_h = [], []
            for hh in range(2):
                h = 2 * p + hh
                vs = slice(h * B_VAL_DIM, (h + 1) * B_VAL_DIM)
                att = jnp.where(mask, a[hh * tile_rows:(hh + 1) * tile_rows], 0.0).astype(BF16)
                dv_intra.append(_dot_tn(att, do_ref[:, vs]))
                da = jnp.where(mask, _dot_nt(do_ref[:, vs], v_ref[:, vs]), 0.0).astype(BF16)
                da_blocks.append(da)
                dqm_h.append(_dot(da, km_p))
            lo = _lane_iota((tile_rows, LANES)) < B_KEY_DIM
            dqm_slabs.append(jnp.where(lo, dqm_h[0], dqm_h[1]))
            dkm_slabs.append(_dot_tn(jnp.concatenate(da_blocks, axis=0), qm_st))
        dqm = jnp.concatenate(dqm_slabs, axis=1)
        dkm = jnp.concatenate(dkm_slabs, axis=1)

        dqe_c, dkd_c, dv_inter, tail_c = ([None] * cpt for _ in range(4))
        for c in reversed(range(cpt)):
            rows = slice(c * B_CHUNK, (c + 1) * B_CHUNK)
            dst = dst_ref[...]
            dst_b = dst.astype(BF16)
            dv_inter[c] = _dot_nt(_stack_masked(kd[rows]), dst_b)
            dkd_c[c] = _select_heads(_dot(_rows_by_head(v_ref[rows, :]), dst_b))
            do_c = do_ref[rows, :]
            dqe_c[c] = _select_heads(_dot(_rows_by_head(do_c), st_all_ref[c].astype(BF16)))
            contrib = _select_state(_dot_tn(do_c, qe[rows].astype(BF16)))
            decay = jnp.exp(cum_ref[pl.ds(c * B_CHUNK + B_CHUNK - 1, 1), :])
            tail = (jnp.sum(kk[rows] * dkd_c[c] * e_kd[rows], axis=0, keepdims=True)
                    + decay * jnp.sum(st_all_ref[c] * dst, axis=0, keepdims=True))
            tail_c[c] = jnp.broadcast_to(tail, (B_CHUNK, B_KEY_WIDTH))
            dst_ref[...] = dst * decay + contrib
        dqe = jnp.concatenate(dqe_c, axis=0)
        dkd = jnp.concatenate(dkd_c, axis=0)
        dqs = dqm * e_qm + dqe * e_qe
        dk = dkm * e_km + dkd * e_kd
        dq_ref[...] = (dqs * scale).astype(BF16)
        dk_ref[...] = dk.astype(BF16)
        for h in range(B_HEADS):
            dvi = jnp.concatenate([dv_inter[c][h * B_CHUNK:(h + 1) * B_CHUNK] for c in range(cpt)], axis=0)
            dv_ref[:, h * B_VAL_DIM:(h + 1) * B_VAL_DIM] = (dv_intra[h] + dvi).astype(BF16)
        dd = qs * dqs - kk * dk
        i, j = _row_iota((tile_rows, tile_rows)), _lane_iota((tile_rows, tile_rows))
        upper = ((i // B_CHUNK == j // B_CHUNK) & (j >= i)).astype(BF16)
        hi, mid, lo3 = _split3(dd)
        dla_ref[...] = _dot(upper, hi) + _dot(upper, mid) + _dot(upper, lo3) + jnp.concatenate(tail_c, axis=0)

    def body(q_ref, k_ref, cum_ref, v_ref, do_ref, st_all_ref, *rest):
        g_refs, (dq_ref, dk_ref, dv_ref, dla_ref) = rest[:nw], rest[nw:nw + 4]
        rv_refs, (dst_ref, send_sems, recv_sems) = rest[nw + 4:2 * nw + 4], rest[2 * nw + 4:]
        x, y, c = _my_place()

        def wcopy(a, r):
            dx, dy, dc = FLIPS[r]
            return pltpu.make_async_remote_copy(
                src_ref=g_refs[a].at[4 * (x ^ dx) + 2 * (y ^ dy) + (c ^ dc)], dst_ref=rv_refs[a].at[r],
                send_sem=send_sems.at[a, r], recv_sem=recv_sems.at[a, r],
                device_id=(x ^ dx, y ^ dy, c ^ dc), device_id_type=MESH)

        @pl.when(pl.program_id(0) == 0)
        def _():
            dst_ref[...] = jnp.zeros_like(dst_ref)
            for a in range(nw):
                for r in range(len(FLIPS)):
                    wcopy(a, r).start()

        for b in range(B):
            for tile in reversed(range(tps)):
                tok = pl.ds(tile * tile_rows, tile_rows)
                chunks = pl.ds(tile * cpt, cpt)
                one_sequence(*[r.at[b, tok] for r in (q_ref, k_ref, cum_ref, v_ref, do_ref)], st_all_ref.at[b, chunks],
                             *[r.at[b, tok] for r in (dq_ref, dk_ref, dv_ref, dla_ref)], dst_ref.at[b])

        @pl.when(pl.program_id(0) == steps - 1)
        def _():
            for a in range(nw):
                for r in range(len(FLIPS)):
                    wcopy(a, r).wait()

    def rows(w):
        return pl.BlockSpec((B, tps * tile_rows, w), lambda t: (0, steps - 1 - t, 0))

    seq = lambda a: a.reshape(B, S, a.shape[-1])
    res = pl.pallas_call(
        body, name="gla_bwd", grid=(steps,),
        in_specs=[rows(B_KEY_WIDTH), rows(B_KEY_WIDTH), rows(B_KEY_WIDTH), rows(B_WIDTH), rows(B_WIDTH),
                  pl.BlockSpec((B, tps * cpt, B_VAL_DIM, B_KEY_WIDTH), lambda t: (0, steps - 1 - t, 0, 0))]
                 + _any_specs(nw),
        out_specs=[rows(B_KEY_WIDTH), rows(B_KEY_WIDTH), rows(B_WIDTH), rows(B_KEY_WIDTH)] + _any_specs(nw),
        out_shape=[jax.ShapeDtypeStruct((B, S, B_KEY_WIDTH), BF16), jax.ShapeDtypeStruct((B, S, B_KEY_WIDTH), BF16),
                   jax.ShapeDtypeStruct((B, S, B_WIDTH), BF16), jax.ShapeDtypeStruct((B, S, B_KEY_WIDTH), F32)]
                  + [jax.ShapeDtypeStruct((len(FLIPS), *g.shape[1:]), g.dtype) for g in wgrads],
        scratch_shapes=[pltpu.VMEM((B, B_VAL_DIM, B_KEY_WIDTH), F32),
                        pltpu.SemaphoreType.DMA((nw, len(FLIPS))), pltpu.SemaphoreType.DMA((nw, len(FLIPS)))],
        compiler_params=_params("arbitrary"),
    )(seq(q), seq(k), seq(cum), seq(vb), seq(do), st_all.reshape(B, S // B_CHUNK, B_VAL_DIM, B_KEY_WIDTH), *wgrads)
    return [a.reshape(T, a.shape[-1]) for a in res[:4]], res[4:]


def _merge(x2, tgt2, attn, za, o_gla, zb, ga, gb, w_oa_sh, w_ob_sh, w_o, g_gla, g_final):
    T = x2.shape[0]
    tm = math.gcd(T, 512)
    sub = math.gcd(tm, 256)
    last = T // tm - 1

    def body(x_ref, tgt_ref, attn_ref, za_ref, og_ref, zb_ref, ga_ref, gb_ref,
             woa_sh_ref, wob_sh_ref, wo_ref, gg_ref, gf_ref,
             dxres_ref, dattn_ref, dog_ref, dza_ref, dzb_ref, dga_ref, dgb_ref,
             dwo_out, dwoa_out, dwob_out, small_ref,
             awo_ref, awoa_ref, awob_ref, agf_ref, agg_ref, loss_ref, woa_ref, wob_ref,
             dwo_ref, dwoa_ref, dwob_ref, w_sems, dw_sems):
        w_copies = [pltpu.make_async_copy(sh.at[j], dst.at[:, j * SHARD_OUT:(j + 1) * SHARD_OUT], w_sems.at[a, j])
                    for a, (sh, dst) in enumerate(((woa_sh_ref, woa_ref), (wob_sh_ref, wob_ref))) for j in range(N_DEV)]
        dw_copies = [pltpu.make_async_copy(src, dst, dw_sems.at[a])
                     for a, (src, dst) in enumerate(((dwo_ref, dwo_out), (dwoa_ref, dwoa_out), (dwob_ref, dwob_out)))]

        @pl.when(pl.program_id(0) == 0)
        def _():
            for cp in w_copies:
                cp.start()
            for r in (awo_ref, awoa_ref, awob_ref, agf_ref, agg_ref, loss_ref):
                r[...] = jnp.zeros_like(r)
            for cp in w_copies:
                cp.wait()

        def one_tile(rows):
            za_v = za_ref[rows, :].astype(F32)
            sig_za = _sigmoid_tanh(za_v)
            silu_a = za_v * sig_za
            attn_v = attn_ref[rows, :].astype(F32)
            oa = (attn_v * silu_a).astype(BF16)
            ya = _dot(oa, woa_ref[...])
            og = og_ref[rows, :].astype(F32)
            zb_v = zb_ref[rows, :].astype(F32)
            sig_zb = _sigmoid_tanh(zb_v)
            silu_b = zb_v * sig_zb
            gg = gg_ref[...]
            on_parts, rinv_parts = [], []
            for h in range(B_HEADS):
                seg = og[:, h * B_VAL_DIM:(h + 1) * B_VAL_DIM]
                rinv = lax.rsqrt(jnp.mean(seg * seg, axis=-1, keepdims=True) + NORM_EPS)
                rinv_parts.append(rinv)
                on_parts.append(seg * rinv)
            on = jnp.concatenate(on_parts, axis=1)
            obn = on * gg
            ob = (obn * silu_b).astype(BF16)
            yb = _dot(ob, wob_ref[...])
            sig_a = _sigmoid_tanh(ga_ref[rows, :].astype(F32))
            sig_b = _sigmoid_tanh(gb_ref[rows, :].astype(F32))
            merged = (sig_a * ya + sig_b * yb).astype(BF16)
            out = x_ref[rows, :] + _dot(merged, wo_ref[...])
            rf = lax.rsqrt(jnp.mean(out * out, axis=-1, keepdims=True) + NORM_EPS)
            nrm = out * rf
            gf = gf_ref[...]
            err = nrm * gf - tgt_ref[rows, :]
            loss = jnp.sum(err * err) * (0.5 / D_MODEL)

            dy = err * (1.0 / D_MODEL)
            dgf = jnp.sum(dy * nrm, axis=0, keepdims=True)
            dn = dy * gf
            dout = rf * (dn - nrm * jnp.mean(dn * nrm, axis=-1, keepdims=True))
            dxres_ref[rows, :] = dout
            dout_b = dout.astype(BF16)
            dmerged = _dot_nt(dout_b, wo_ref[...])
            dya = dmerged * sig_a
            dyb = dmerged * sig_b
            dga_ref[rows, :] = (dmerged * ya * sig_a * (1.0 - sig_a)).astype(BF16)
            dgb_ref[rows, :] = (dmerged * yb * sig_b * (1.0 - sig_b)).astype(BF16)
            dya_b, dyb_b = dya.astype(BF16), dyb.astype(BF16)
            doa = _dot_nt(dya_b, woa_ref[...])
            dattn_ref[rows, :] = (doa * silu_a).astype(BF16)
            dza_ref[rows, :] = (doa * attn_v * (sig_za * (1.0 + za_v * (1.0 - sig_za)))).astype(BF16)
            dob = _dot_nt(dyb_b, wob_ref[...])
            dzb_ref[rows, :] = (dob * obn * (sig_zb * (1.0 + zb_v * (1.0 - sig_zb)))).astype(BF16)
            dobn = dob * silu_b
            dgg = jnp.sum(dobn * on, axis=0, keepdims=True)
            don = dobn * gg
            for h in range(B_HEADS):
                sl = slice(h * B_VAL_DIM, (h + 1) * B_VAL_DIM)
                don_h, on_h = don[:, sl], on[:, sl]
                dog_ref[rows, sl] = (rinv_parts[h] * (don_h - on_h * jnp.mean(don_h * on_h, axis=-1, keepdims=True))
                                     ).astype(BF16)
            return (merged, dout_b, oa, dya_b, ob, dyb_b), (loss, dgf, dgg)

        tiles = [one_tile(pl.ds(j * sub, sub)) for j in range(tm // sub)]
        merged, dout_b, oa, dya_b, ob, dyb_b = (jnp.concatenate(parts, axis=0) for parts in zip(*[t[0] for t in tiles]))
        awo_ref[...] += _dot_tn(merged, dout_b)
        awoa_ref[...] += _dot_tn(oa, dya_b)
        awob_ref[...] += _dot_tn(ob, dyb_b)
        for _, (loss, dgf, dgg) in tiles:
            loss_ref[...] += loss
            agf_ref[...] += dgf
            agg_ref[...] += dgg

        @pl.when(pl.program_id(0) == last)
        def _():
            for j in range(N_DEV):
                dwo_ref[j] = awo_ref[j * SHARD_OUT:(j + 1) * SHARD_OUT, :].astype(BF16)
                dwoa_ref[j] = awoa_ref[:, j * SHARD_OUT:(j + 1) * SHARD_OUT].astype(BF16)
                dwob_ref[j] = awob_ref[:, j * SHARD_OUT:(j + 1) * SHARD_OUT].astype(BF16)
            small_ref[...] = jnp.zeros_like(small_ref)
            _put_rows(small_ref, SMALL_G_FINAL, agf_ref[...])
            _put_rows(small_ref, SMALL_G_GLA, agg_ref[...])
            small_ref[SMALL_LOSS:SMALL_LOSS + 1, :] = loss_ref[...]
            for cp in dw_copies:
                cp.start()
            for cp in dw_copies:
                cp.wait()

    def rows(w):
        return pl.BlockSpec((tm, w), lambda i: (i, 0))

    def whole(shape):
        nd = len(shape)
        return pl.BlockSpec(shape, lambda i: (0,) * nd)

    outs = [((T, D_MODEL), F32, rows(D_MODEL)), ((T, A_WIDTH), BF16, rows(A_WIDTH)), ((T, B_WIDTH), BF16, rows(B_WIDTH)),
            ((T, A_WIDTH), BF16, rows(A_WIDTH)), ((T, B_WIDTH), BF16, rows(B_WIDTH)),
            ((T, D_MODEL), BF16, rows(D_MODEL)), ((T, D_MODEL), BF16, rows(D_MODEL)),
            ((N_DEV, SHARD_OUT, D_MODEL), BF16, pl.BlockSpec(memory_space=pl.ANY)),
            ((N_DEV, A_WIDTH, SHARD_OUT), BF16, pl.BlockSpec(memory_space=pl.ANY)),
            ((N_DEV, B_WIDTH, SHARD_OUT), BF16, pl.BlockSpec(memory_space=pl.ANY)),
            ((SMALL_SINKS, LANES), F32, whole((SMALL_SINKS, LANES)))]
    return pl.pallas_call(
        body, name="merge", grid=(T // tm,),
        in_specs=[rows(D_MODEL), rows(D_MODEL), rows(A_WIDTH), rows(A_WIDTH), rows(B_WIDTH), rows(B_WIDTH),
                  rows(D_MODEL), rows(D_MODEL),
                  pl.BlockSpec(memory_space=pl.ANY), pl.BlockSpec(memory_space=pl.ANY),
                  _const_spec((D_MODEL, D_MODEL)), _const_spec((1, B_WIDTH)), _const_spec((1, D_MODEL))],
        out_specs=[o[2] for o in outs],
        out_shape=[jax.ShapeDtypeStruct(o[0], o[1]) for o in outs],
        scratch_shapes=[pltpu.VMEM((D_MODEL, D_MODEL), F32), pltpu.VMEM((A_WIDTH, D_MODEL), F32),
                        pltpu.VMEM((B_WIDTH, D_MODEL), F32), pltpu.VMEM((1, D_MODEL), F32), pltpu.VMEM((1, B_WIDTH), F32),
                        pltpu.VMEM((1, LANES), F32), pltpu.VMEM((A_WIDTH, D_MODEL), BF16),
                        pltpu.VMEM((B_WIDTH, D_MODEL), BF16),
                        pltpu.VMEM((N_DEV, SHARD_OUT, D_MODEL), BF16), pltpu.VMEM((N_DEV, A_WIDTH, SHARD_OUT), BF16),
                        pltpu.VMEM((N_DEV, B_WIDTH, SHARD_OUT), BF16),
                        pltpu.SemaphoreType.DMA((2, N_DEV)), pltpu.SemaphoreType.DMA((3,))],
        compiler_params=pltpu.CompilerParams(dimension_semantics=("arbitrary",), vmem_limit_bytes=V7X_VMEM_LIMIT_MAX),
    )(x2, tgt2, attn, za, o_gla, zb, ga, gb, w_oa_sh, w_ob_sh, w_o, g_gla, g_final)


def _in_proj_bwd(x2, dxres, cosf, sinf, g_in, wt_pad, wa_pad, parts):
    T = x2.shape[0]
    tm = math.gcd(T, 512)
    sub = math.gcd(tm, 256)
    last = T // tm - 1
    base = SMALL_G_IN

    def body(x_ref, dxres_ref, cos_ref, sin_ref, g_ref, wt_ref, wa_ref,
             dq_ref, dkv_ref, dza_ref, dqb_ref, dkb_ref, dvb_ref, dzb_ref, dla_ref, u_ref, alr_ref, dga_ref, dgb_ref,
             dx_ref, dsh_ref, small_ref, dproj_ref, agin_ref, aba_ref, awa_ref):
        @pl.when(pl.program_id(0) == 0)
        def _():
            for r in (agin_ref, aba_ref, awa_ref):
                r[...] = jnp.zeros_like(r)

        def one_tile(rows):
            cos, nsin = cos_ref[rows, :], -sin_ref[rows, :]
            for s in range(A_WIDTH // LANES):
                sl = slice(s * LANES, (s + 1) * LANES)
                dproj_ref[rows, sl] = _rope_slab(dq_ref[rows, sl].astype(F32), cos, nsin).astype(BF16)
            dproj_ref[rows, QKV_K:QKV_V] = _rope_slab(dkv_ref[rows, 0:LANES].astype(F32), cos, nsin).astype(BF16)
            dproj_ref[rows, QKV_V:QKV_W] = dkv_ref[rows, LANES:]

            def put(name, val):
                a, b = SEG[name]
                dproj_ref[rows, a:b] = val

            put("za", dza_ref[rows, :])
            put("qb", dqb_ref[rows, :])
            put("kb", dkb_ref[rows, :])
            put("vb", dvb_ref[rows, :])
            put("zb", dzb_ref[rows, :])
            put("ga", dga_ref[rows, :])
            put("gb", dgb_ref[rows, :])
            du = dla_ref[rows, :] * (1.0 / B_GATE_TEMP) * _sigmoid(-u_ref[rows, :])
            du_b = du.astype(BF16)
            put("alr", _dot_nt(du_b, wa_ref[...]).astype(BF16))

            for j in range(N_DEV):
                col = (j % 2) * SHARD_PAD
                for a, b in _shard_pad_cols(j):
                    dsh_ref[j // 2, rows, col:col + b - a] = dproj_ref[rows, a:b]
                    col += b - a
                dsh_ref[j // 2, rows, col:(j % 2 + 1) * SHARD_PAD] = jnp.zeros((sub, SHARD_PAD - SHARD_IN), BF16)

            dh = _dot(dproj_ref[rows, :], wt_ref[...])
            x = x_ref[rows, :]
            r = lax.rsqrt(jnp.mean(x * x, axis=-1, keepdims=True) + NORM_EPS)
            nrm = x * r
            dn = dh * g_ref[...]
            dx_ref[rows, :] = dxres_ref[rows, :] + r * (dn - nrm * jnp.mean(dn * nrm, axis=-1, keepdims=True))
            return jnp.sum(dh * nrm, axis=0, keepdims=True), jnp.sum(du, axis=0, keepdims=True), alr_ref[rows, :], du_b

        for j in range(tm // sub):
            dgin, dba, alr, du_b = one_tile(pl.ds(j * sub, sub))
            agin_ref[...] += dgin
            aba_ref[...] += dba
            awa_ref[...] += _dot_tn(alr, du_b)

        @pl.when(pl.program_id(0) == last)
        def _():
            small_ref[...] = jnp.zeros_like(small_ref)
            _put_rows(small_ref, SMALL_G_IN - base, agin_ref[...])
            _put_rows(small_ref, SMALL_B_ALPHA - base, aba_ref[...])
            for half in range(B_KEY_WIDTH // LANES):
                r0 = SMALL_W_ALPHA - base + half * B_GATE_RANK
                small_ref[r0:r0 + B_GATE_RANK, :] = awa_ref[0:B_GATE_RANK, half * LANES:(half + 1) * LANES]

    def rows(w):
        return pl.BlockSpec((tm, w), lambda i: (i, 0))

    names = ["dq", "dkv", "dza", "dqb", "dkb", "dvb", "dzb", "dla", "u", "alr", "dga", "dgb"]
    return pl.pallas_call(
        body, name="in_proj_bwd", grid=(T // tm,),
        in_specs=[rows(D_MODEL), rows(D_MODEL), rows(LANES), rows(LANES), _const_spec((1, D_MODEL)),
                  _const_spec((D_IN_PAD, D_MODEL)), _const_spec((RANK_PAD, B_KEY_WIDTH))]
                 + [rows(parts[n].shape[1]) for n in names],
        out_specs=[rows(D_MODEL), pl.BlockSpec((N_CHIPS, tm, 2 * SHARD_PAD), lambda i: (0, i, 0)),
                   pl.BlockSpec((SMALL_ROWS - base, LANES), lambda i: (0, 0))],
        out_shape=[jax.ShapeDtypeStruct((T, D_MODEL), F32), jax.ShapeDtypeStruct((N_CHIPS, T, 2 * SHARD_PAD), BF16),
                   jax.ShapeDtypeStruct((SMALL_ROWS - base, LANES), F32)],
        scratch_shapes=[pltpu.VMEM((tm, D_IN_PAD), BF16), pltpu.VMEM((1, D_MODEL), F32), pltpu.VMEM((1, B_KEY_WIDTH), F32),
                        pltpu.VMEM((RANK_PAD, B_KEY_WIDTH), F32)],
        compiler_params=pltpu.CompilerParams(dimension_semantics=("arbitrary",), vmem_limit_bytes=V7X_VMEM_LIMIT_MAX),
    )(x2, dxres, cosf, sinf, g_in, wt_pad, wa_pad, *[parts[n] for n in names])


FLIPS = [(dx, dy, dc) for dx in (0, 1) for dy in (0, 1) for dc in (0, 1)][1:]


def _my_place():
    return lax.axis_index("x"), lax.axis_index("y"), lax.axis_index("c")


def _any_specs(n):
    return [pl.BlockSpec(memory_space=pl.ANY)] * n


def _gather_first(shards, pos_col):
    n = len(shards)
    T = pos_col.shape[0]
    rows_per_pass = math.gcd(T, 512)
    invf, sign = _rope_lane_constants()

    def body(*refs):
        ins, (pos_ref, invf_ref, sign_ref) = refs[:n], refs[n:n + 3]
        outs, (cos_ref, sin_ref) = refs[n + 3:2 * n + 3], refs[2 * n + 3:2 * n + 5]
        send_sems, recv_sems, local_sems = refs[2 * n + 5:]
        x, y, c = _my_place()
        me, sibling = (x, y, c), (x, y, 1 - c)
        chips = [(1 - x, y), (x, 1 - y), (1 - x, 1 - y)]

        def block(a, px, py, pc):
            return outs[a].at[4 * px + 2 * py + pc]

        def copy(a, k, blk, to, src=None):
            return pltpu.make_async_remote_copy(
                src_ref=block(a, *blk) if src is None else src, dst_ref=block(a, *blk),
                send_sem=send_sems.at[a, k], recv_sem=recv_sems.at[a, k], device_id=to, device_id_type=MESH)

        mine = [pltpu.make_async_copy(ins[a], block(a, *me), local_sems.at[a]) for a in range(n)]
        for cp in mine:
            cp.start()
        first = []
        for a in range(n):
            first.append(copy(a, 0, me, sibling, src=ins[a]))
            first += [copy(a, 1 + j, me, (*chip, c), src=ins[a]) for j, chip in enumerate(chips)]
        for cp in first:
            cp.start()

        def tables(i, carry):
            rows = pl.ds(pl.multiple_of(i * rows_per_pass, rows_per_pass), rows_per_pass)
            ang = pos_ref[rows, :].astype(F32) * invf_ref[...]
            cos_ref[rows, :] = jnp.cos(ang)
            sin_ref[rows, :] = jnp.sin(ang) * sign_ref[...]
            return carry

        lax.fori_loop(0, T // rows_per_pass, tables, 0)

        passed = []
        for j, chip in enumerate(chips):
            for a in range(n):
                copy(a, 1 + j, (*chip, c), me).wait_recv()
                fwd = copy(a, 4 + j, (*chip, c), sibling)
                fwd.start()
                passed.append(fwd)
        for a in range(n):
            copy(a, 0, sibling, me).wait_recv()
            for j, chip in enumerate(chips):
                copy(a, 4 + j, (*chip, 1 - c), me).wait_recv()
        for cp in first + passed:
            cp.wait_send()
        for cp in mine:
            cp.wait()

    vmem = pl.BlockSpec(memory_space=pltpu.VMEM)
    res = pl.pallas_call(
        body, name="gather_weights",
        in_specs=_any_specs(n) + [vmem] * 3, out_specs=_any_specs(n) + [vmem] * 2,
        out_shape=[jax.ShapeDtypeStruct((N_DEV, *s.shape), s.dtype) for s in shards]
                  + [jax.ShapeDtypeStruct((T, LANES), F32)] * 2,
        scratch_shapes=[pltpu.SemaphoreType.DMA((n, 7)), pltpu.SemaphoreType.DMA((n, 7)), pltpu.SemaphoreType.DMA((n,))],
        compiler_params=pltpu.CompilerParams(vmem_limit_bytes=V7X_VMEM_LIMIT),
    )(*shards, pos_col, invf, sign)
    return res[:n], res[n], res[n + 1]


def _w_in_grad_rs(h, dsh, chip_order, small):
    T = h.shape[0]
    tk = math.gcd(T, 2048)
    nk = T // tk
    chip_flips = [(1, 1), (1, 0), (0, 1)]
    n_steps = len(chip_flips) + 1
    SIB = len(chip_flips)
    k_finish = min(1, nk - 1)

    def body(order_ref, h_ref, d_ref, s_ref, own_ref, recv_ref, sall_ref,
             acc_ref, keep_ref, pre_ref, to_sib_ref, to_chip_ref,
             sib_send, sib_recv, chip_send, chip_recv, ssend_sems, srecv_sems, local_sem):
        i, kk = pl.program_id(0), pl.program_id(1)
        x, y, c = _my_place()
        my_dev = 4 * x + 2 * y + c

        def small_copy(r, slot):
            dx, dy, dc = FLIPS[r]
            return pltpu.make_async_remote_copy(
                src_ref=s_ref, dst_ref=sall_ref.at[slot], send_sem=ssend_sems.at[r], recv_sem=srecv_sems.at[r],
                device_id=(x ^ dx, y ^ dy, c ^ dc), device_id_type=MESH)

        keep_small = pltpu.make_async_copy(s_ref, sall_ref.at[my_dev], local_sem)

        def sib_copy(t):
            dst = recv_ref.at[SIB] if t == SIB else pre_ref.at[t]
            return pltpu.make_async_remote_copy(
                src_ref=to_sib_ref.at[t], dst_ref=dst, send_sem=sib_send.at[t], recv_sem=sib_recv.at[t],
                device_id=(x, y, 1 - c), device_id_type=MESH)

        def chip_copy(t):
            dx, dy = chip_flips[t]
            return pltpu.make_async_remote_copy(
                src_ref=to_chip_ref.at[t], dst_ref=recv_ref.at[t], send_sem=chip_send.at[t], recv_sem=chip_recv.at[t],
                device_id=(x ^ dx, y ^ dy, c), device_id_type=MESH)

        def halves():
            first, second = acc_ref[0:SHARD_PAD, :], acc_ref[SHARD_PAD:2 * SHARD_PAD, :]
            return jnp.where(c == 0, first, second), jnp.where(c == 0, second, first)

        @pl.when((i == 0) & (kk == 0))
        def _():
            keep_small.start()
            for r in range(len(FLIPS)):
                small_copy(r, my_dev).start()

        @pl.when(kk == 0)
        def _():
            acc_ref[...] = jnp.zeros_like(acc_ref)

        acc_ref[...] += _dot_tn(d_ref[...], h_ref[...])

        for t in range(len(chip_flips)):
            @pl.when((i == t + 1) & (kk == k_finish))
            def _(t=t):
                sib_copy(t).wait_recv()
                to_chip_ref[t] = (keep_ref[...] + pre_ref[t].astype(F32)).astype(BF16)
                chip_copy(t).start()

        for t in range(len(chip_flips)):
            @pl.when((i == t) & (kk == nk - 1))
            def _(t=t):
                mine, theirs = halves()
                to_sib_ref[t] = theirs.astype(BF16)
                sib_copy(t).start()
                keep_ref[...] = mine

        @pl.when((i == n_steps - 1) & (kk == nk - 1))
        def _():
            mine, theirs = halves()
            own_ref[...] = mine
            to_sib_ref[SIB] = theirs.astype(BF16)
            sib_copy(SIB).start()
            for t in range(len(chip_flips)):
                sib_copy(t).wait_send()
                chip_copy(t).wait_send()
                chip_copy(t).wait_recv()
            sib_copy(SIB).wait_send()
            sib_copy(SIB).wait_recv()
            for r, (dx, dy, dc) in enumerate(FLIPS):
                small_copy(r, 4 * (x ^ dx) + 2 * (y ^ dy) + (c ^ dc)).wait_recv()
                small_copy(r, my_dev).wait_send()
            keep_small.wait()

    shard = (SHARD_PAD, D_MODEL)
    return pl.pallas_call(
        body, name="w_in_grad_rs",
        grid_spec=pltpu.PrefetchScalarGridSpec(
            num_scalar_prefetch=1, grid=(n_steps, nk),
            in_specs=[pl.BlockSpec((tk, D_MODEL), lambda i, kk, order: (kk, 0)),
                      pl.BlockSpec((None, tk, 2 * SHARD_PAD), lambda i, kk, order: (order[i], kk, 0)),
                      pl.BlockSpec(memory_space=pl.ANY)],
            out_specs=[pl.BlockSpec(shard, lambda i, kk, order: (0, 0)),
                       pl.BlockSpec(memory_space=pl.ANY), pl.BlockSpec(memory_space=pl.ANY)],
            scratch_shapes=[pltpu.VMEM((2 * SHARD_PAD, D_MODEL), F32), pltpu.VMEM(shard, F32),
                            pltpu.VMEM((SIB, *shard), BF16), pltpu.VMEM((SIB + 1, *shard), BF16),
                            pltpu.VMEM((SIB, *shard), BF16),
                            pltpu.SemaphoreType.DMA((SIB + 1,)), pltpu.SemaphoreType.DMA((SIB + 1,)),
                            pltpu.SemaphoreType.DMA((SIB,)), pltpu.SemaphoreType.DMA((SIB,)),
                            pltpu.SemaphoreType.DMA((7,)), pltpu.SemaphoreType.DMA((7,)), pltpu.SemaphoreType.DMA]),
        out_shape=[jax.ShapeDtypeStruct(shard, F32),
                   jax.ShapeDtypeStruct((SIB + 1, *shard), BF16),
                   jax.ShapeDtypeStruct((N_DEV, *small.shape), F32)],
        compiler_params=_params("arbitrary", "arbitrary"),
    )(chip_order, h, dsh, small)


def _adam_math(w, g, m, v):
    m_new = ADAM_B1 * m + (1.0 - ADAM_B1) * g
    v_new = ADAM_B2 * v + (1.0 - ADAM_B2) * (g * g)
    m_hat = m_new / (1.0 - ADAM_B1 ** ADAM_STEP)
    v_hat = v_new / (1.0 - ADAM_B2 ** ADAM_STEP)
    delta = -ADAM_LR * (m_hat / (jnp.sqrt(v_hat) + ADAM_EPS) + ADAM_WD * w)
    return delta, m_new, v_new


def _adam_big(jobs):
    steps = 8
    n = len(jobs)
    idx = jnp.stack([job[1] for job in jobs]).astype(jnp.int32)
    blocks = []
    for own, _, recv, w, m, v in jobs:
        (rw, cw), rp = w.shape, own.shape[1]
        by_cols = rp != rw
        blk_w = (rw, cw // steps) if by_cols else (rw // steps, cw)
        blk_g = (rp, cw // steps) if by_cols else (rw // steps, cw)
        blocks.append((blk_w, blk_g, by_cols))

    def body(idx_ref, *refs):
        ins, outs = refs[:5 * n], refs[5 * n:]
        for j, (blk_w, _, _) in enumerate(blocks):
            o_ref, r_ref, w_ref, m_ref, v_ref = ins[5 * j:5 * j + 5]
            g_ref, d_ref, mo_ref, vo_ref = outs[4 * j:4 * j + 4]
            g = o_ref[...].astype(F32)
            for r in range(r_ref.shape[0]):
                g = g + r_ref[r].astype(F32)
            g = g[0:blk_w[0], :]
            g_ref[...] = g
            d_ref[...], mo_ref[...], vo_ref[...] = _adam_math(w_ref[...], g, m_ref[...], v_ref[...])

    in_specs, out_specs, out_shape, args = [], [], [], []
    for j, ((own, _, recv, w, m, v), (blk_w, blk_g, by_cols)) in enumerate(zip(jobs, blocks)):
        at = (lambda i: (0, i)) if by_cols else (lambda i: (i, 0))
        spec = pl.BlockSpec(blk_w, lambda i, idx_ref, at=at: at(i))
        in_specs += [pl.BlockSpec((None, *blk_g), lambda i, idx_ref, at=at, j=j: (idx_ref[j], *at(i))),
                     pl.BlockSpec((recv.shape[0], *blk_g), lambda i, idx_ref, at=at: (0, *at(i))), spec, spec, spec]
        out_specs += [spec] * 4
        out_shape += [jax.ShapeDtypeStruct(w.shape, F32)] * 4
        args += [own, recv, w, m, v]
    res = pl.pallas_call(
        body, name="adam_big",
        grid_spec=pltpu.PrefetchScalarGridSpec(num_scalar_prefetch=1, grid=(steps,), in_specs=in_specs, out_specs=out_specs),
        out_shape=out_shape,
        compiler_params=_params("parallel"),
    )(idx, *args)
    return [res[4 * j:4 * j + 4] for j in range(n)]


def _adam_small(small_all, params):
    flat = [a for triple in params for a in triple]
    n_par = len(params)

    def body(s_ref, *refs):
        ins, outs, loss_ref = refs[:3 * n_par], refs[3 * n_par:-1], refs[-1]
        g_slab = s_ref[0]
        for dev in range(1, N_DEV):
            g_slab = g_slab + s_ref[dev]
        loss_ref[...] = g_slab[SMALL_LOSS:SMALL_LOSS + 1, :]
        dev = 4 * lax.axis_index("x") + 2 * lax.axis_index("y") + lax.axis_index("c")
        alpha_full = jnp.concatenate([g_slab[SMALL_W_ALPHA + half * B_GATE_RANK:SMALL_W_ALPHA + (half + 1) * B_GATE_RANK]
                                      for half in range(B_KEY_WIDTH // LANES)], axis=1)
        alpha_mine = pltpu.roll(alpha_full, (B_KEY_WIDTH - dev * SHARD_ALPHA) % B_KEY_WIDTH, 1)[:, 0:SHARD_ALPHA]
        grads = [_take_rows(g_slab, SMALL_G_IN, D_MODEL // LANES), _take_rows(g_slab, SMALL_G_FINAL, D_MODEL // LANES),
                 _take_rows(g_slab, SMALL_G_GLA, B_WIDTH // LANES), _take_rows(g_slab, SMALL_B_ALPHA, B_KEY_WIDTH // LANES),
                 g_slab[SMALL_SINKS:SMALL_SINKS + 1, 0:A_HEADS], alpha_mine]
        for i, g in enumerate(grads):
            w_ref, m_ref, v_ref = ins[3 * i:3 * i + 3]
            delta, m_new, v_new = _adam_math(w_ref[...], g, m_ref[...], v_ref[...])
            outs[4 * i][...] = g
            outs[4 * i + 1][...] = delta
            outs[4 * i + 2][...] = m_new
            outs[4 * i + 3][...] = v_new

    res = pl.pallas_call(
        body, name="adam_small",
        out_shape=[jax.ShapeDtypeStruct(t[0].shape, F32) for t in params for _ in range(4)]
                  + [jax.ShapeDtypeStruct((1, LANES), F32)],
    )(small_all, *flat)
    return [res[4 * i:4 * i + 4] for i in range(n_par)], res[-1]


def _local_step(x, cosf, sinf, loss_target, g_in, wt_sh, wa_pad, b_alpha, sinks, g_gla, out_shards, g_final, chip_order):
    B, S, _ = x.shape
    T = B * S
    x2 = x.reshape(T, D_MODEL)
    tgt2 = loss_target.reshape(T, D_MODEL)
    f, (g_woa, g_wob, g_wo) = _in_proj(x2, cosf, sinf, g_in, wt_sh, wa_pad, b_alpha, out_shards)
    w_o = g_wo.reshape(D_MODEL, D_MODEL)
    sink_row = jnp.repeat(sinks, WINDOW).reshape(1, ATT_ROWS)
    sink_col = sink_row.reshape(ATT_ROWS, 1)
    attn, lse = _attn_fwd(f["qkv"], sink_row, B, S)
    o_gla, st_all = _gla_fwd(f["q"], f["k"], f["cum"], f["vb"], B, S)
    (dxres, dattn, dog, dza, dzb, dga, dgb, dw_o, dw_oa, dw_ob, small_a) = _merge(
        x2, tgt2, attn, f["za"], o_gla, f["zb"], f["ga"], f["gb"], g_woa, g_wob, w_o, g_gla, g_final)
    dq, dkv, dsink = _attn_bwd(f["qkv"], dattn, attn, lse, sink_col, B, S)
    (dqb, dkb, dvb, dla), (rv_o, rv_oa, rv_ob) = _gla_bwd(f["q"], f["k"], f["cum"], f["vb"], dog, st_all, B, S,
                                                        [dw_o, dw_oa, dw_ob])
    parts = dict(dq=dq, dkv=dkv, dza=dza, dqb=dqb, dkb=dkb, dvb=dvb, dzb=dzb, dla=dla, u=f["u"], alr=f["alr"],
                 dga=dga, dgb=dgb)
    dx, dsh, small_c = _in_proj_bwd(x2, dxres, cosf, sinf, g_in, f["wt_pad"], wa_pad, parts)
    small = jnp.concatenate([small_a, dsink, small_c], axis=0)
    own_in, rv_in, small_all = _w_in_grad_rs(f["h"], dsh, chip_order, small)
    return dict(grad_x=dx.reshape(B, S, D_MODEL), own_in=own_in, rv_in=rv_in,
                own_o=dw_o, rv_o=rv_o, own_oa=dw_oa, rv_oa=rv_oa, own_ob=dw_ob, rv_ob=rv_ob, small_all=small_all)


def kernel(x, positions, g_in, w_in, w_alpha_up, b_alpha, attn_sinks, g_gla_norm, w_out_a, w_out_b, w_o, g_final, loss_target, m_g_in, m_w_in, m_w_alpha_up, m_b_alpha, m_attn_sinks, m_g_gla_norm, m_w_out_a, m_w_out_b, m_w_o, m_g_final, v_g_in, v_w_in, v_w_alpha_up, v_b_alpha, v_attn_sinks, v_g_gla_norm, v_w_out_a, v_w_out_b, v_w_o, v_g_final):
    xi, yi, ci = _my_place()
    chip = 2 * xi + yi
    chip_order = jnp.stack([chip ^ 3, chip ^ 2, chip ^ 1, chip]).astype(jnp.int32)

    (g_win, g_wa), cosf, sinf = _gather_first(
        [jnp.pad(w_in[0].T.astype(BF16), ((0, SHARD_PAD - SHARD_IN), (0, 0))), w_alpha_up[0].astype(BF16)],
        positions.reshape(-1, 1))
    wt_sh = g_win.reshape(N_DEV * SHARD_PAD, D_MODEL)
    wa_pad = jnp.pad(jnp.concatenate([g_wa[j] for j in range(N_DEV)], axis=1), ((0, RANK_PAD - B_GATE_RANK), (0, 0)))

    r = _local_step(x, cosf, sinf, loss_target, g_in, wt_sh, wa_pad, b_alpha, attn_sinks[0], g_gla_norm,
                    [w_out_a[0].astype(BF16), w_out_b[0].astype(BF16), w_o[0].astype(BF16)],
                    g_final.reshape(1, D_MODEL), chip_order)

    dev = 4 * xi + 2 * yi + ci
    big = _adam_big([(r["own_in"][None], jnp.int32(0), r["rv_in"], w_in[0].T, m_w_in[0].T, v_w_in[0].T),
                     (r["own_oa"], dev, r["rv_oa"], w_out_a[0], m_w_out_a[0], v_w_out_a[0]),
                     (r["own_ob"], dev, r["rv_ob"], w_out_b[0], m_w_out_b[0], v_w_out_b[0]),
                     (r["own_o"], dev, r["rv_o"], w_o[0], m_w_o[0], v_w_o[0])])
    big[0] = [a.T for a in big[0]]
    row = lambda a: a.reshape(1, D_MODEL)
    (s_in, s_final, s_gla, s_ba, s_sinks, s_wa), loss_row = _adam_small(r["small_all"], [
        (g_in, m_g_in, v_g_in), (row(g_final), row(m_g_final), row(v_g_final)),
        (g_gla_norm, m_g_gla_norm, v_g_gla_norm), (b_alpha, m_b_alpha, v_b_alpha),
        (attn_sinks, m_attn_sinks, v_attn_sinks), (w_alpha_up[0], m_w_alpha_up[0], v_w_alpha_up[0])])

    def group(i):
        return (s_in[i], big[0][i][None], s_wa[i][None], s_ba[i], s_sinks[i], s_gla[i], big[1][i][None], big[2][i][None],
                big[3][i][None], s_final[i].reshape(D_MODEL))

    return (loss_row[0, 0], r["grad_x"], *group(0), *group(1), *group(2), *group(3))
```

```python
import functools
import math

import numpy as np
import jax
import jax.numpy as jnp
from jax import lax
from jax.experimental import pallas as pl
from jax.experimental.pallas import tpu as pltpu

F32 = jnp.float32
BF16 = jnp.bfloat16
MESH = pl.DeviceIdType.MESH

D_MODEL = 1024
A_HEADS, A_KV_HEADS, A_HEAD_DIM = 8, 2, 64
A_WIDTH, A_KV_WIDTH = 512, 128
WINDOW = 128
ROPE_THETA = 500000.0
ROPE_DIM = 16
B_HEADS, B_KEY_DIM, B_VAL_DIM = 4, 64, 128
B_KEY_WIDTH, B_WIDTH = 256, 512
B_GATE_RANK = 16
B_GATE_TEMP = 16.0
B_CHUNK = 64
NORM_EPS = 1e-6
NEG_BIG = -1e30
D_IN = 4880
N_DEV = 8
N_CHIPS = 4
ADAM_LR, ADAM_B1, ADAM_B2, ADAM_EPS, ADAM_WD, ADAM_STEP = 0.001, 0.9, 0.999, 1e-08, 0.01, 10

LANES = 128
V7X_VMEM_LIMIT = 56 * 1024 * 1024
V7X_VMEM_LIMIT_MAX = 62 * 1024 * 1024

RANK_PAD = LANES
SEG = {}
_off = 0
for _name, _w in (("qa", 512), ("ka", 128), ("va", 128), ("za", 512), ("qb", 256), ("kb", 256),
                  ("vb", 512), ("zb", 512), ("alr", RANK_PAD), ("ga", 1024), ("gb", 1024)):
    SEG[_name] = (_off, _off + _w)
    _off += _w
D_IN_PAD = _off
ALR_SRC = SEG["alr"][0]
QKV_K, QKV_V, QKV_W = SEG["ka"][0], SEG["va"][0], SEG["va"][1]
ATT_SCALE = A_HEAD_DIM ** -0.5

SHARD_IN = D_IN // N_DEV
SHARD_PAD = 640
SHARD_OUT = D_MODEL // N_DEV
SHARD_ALPHA = B_KEY_WIDTH // N_DEV

SMALL_G_FINAL, SMALL_G_GLA, SMALL_LOSS, SMALL_SINKS, SMALL_G_IN, SMALL_B_ALPHA, SMALL_W_ALPHA = 0, 8, 12, 16, 24, 32, 40
SMALL_ROWS = 72


def _dot(a, b):
    return jnp.dot(a, b, preferred_element_type=F32)


def _dot_nt(a, b):
    return lax.dot_general(a, b, (((1,), (1,)), ((), ())), preferred_element_type=F32)


def _dot_tn(a, b):
    return lax.dot_general(a, b, (((0,), (0,)), ((), ())), preferred_element_type=F32)


def _sigmoid(z):
    return 1.0 / (1.0 + jnp.exp(-z))


def _sigmoid_tanh(z):
    return 0.5 * jnp.tanh(0.5 * z) + 0.5


def _params(*sem):
    return pltpu.CompilerParams(dimension_semantics=sem, vmem_limit_bytes=V7X_VMEM_LIMIT)


def _const_spec(shape):
    nd = len(shape)
    return pl.BlockSpec(shape, lambda *_: (0,) * nd, pipeline_mode=pl.Buffered(1))


def _lane_iota(shape):
    return lax.broadcasted_iota(jnp.int32, shape, 1)


def _row_iota(shape):
    return lax.broadcasted_iota(jnp.int32, shape, 0)


def _split3(v):
    hi = v.astype(BF16)
    r1 = v - hi.astype(F32)
    mid = r1.astype(BF16)
    lo = (r1 - mid.astype(F32)).astype(BF16)
    return hi, mid, lo


def _put_rows(ref, row0, vec):
    for r in range(vec.shape[1] // LANES):
        ref[row0 + r:row0 + r + 1, :] = vec[:, r * LANES:(r + 1) * LANES]


def _take_rows(slab, row0, n):
    return jnp.concatenate([slab[row0 + r:row0 + r + 1, :] for r in range(n)], axis=1)


def _rope_lane_constants():
    half = ROPE_DIM // 2
    inv_freq = np.exp(-math.log(ROPE_THETA) * np.arange(half, dtype=np.float32) * np.float32(2.0 / ROPE_DIM)).astype(np.float32)
    lane = np.arange(LANES)
    j = lane % A_HEAD_DIM
    invf = np.where(j < ROPE_DIM, inv_freq[j % half], 0.0).astype(np.float32)
    sign = np.where(j < half, -1.0, np.where(j < ROPE_DIM, 1.0, 0.0)).astype(np.float32)
    return jnp.asarray(invf)[None, :], jnp.asarray(sign)[None, :]


def _rope_slab(t, cos, sin_signed):
    first = (_lane_iota(t.shape) % A_HEAD_DIM) < (ROPE_DIM // 2)
    partner = jnp.where(first, pltpu.roll(t, LANES - ROPE_DIM // 2, 1), pltpu.roll(t, ROPE_DIM // 2, 1))
    return t * cos + partner * sin_signed


def _shard_pad_cols(j):
    cut = ALR_SRC + B_GATE_RANK
    shift = RANK_PAD - B_GATE_RANK
    a, b = j * SHARD_IN, (j + 1) * SHARD_IN
    if b <= cut:
        return [(a, b)]
    if a >= cut:
        return [(a + shift, b + shift)]
    return [(a, cut), (cut + shift, b + shift)]


def _in_proj(x2, cosf, sinf, g_in, wt_sh, wa_pad, b_alpha, later_shards):
    T = x2.shape[0]
    tm = math.gcd(T, 512)
    sub = math.gcd(tm, 256)
    last = T // tm - 1
    nl = len(later_shards)

    def body(x_ref, cos_ref, sin_ref, g_ref, wsh_ref, wa_ref, ba_ref, *rest):
        sh_refs, rest = rest[:nl], rest[nl:]
        (h_ref, qkv_ref, za_ref, q_ref, k_ref, vb_ref, zb_ref, alr_ref, u_ref, cum_ref, ga_ref, gb_ref, wt_out) = rest[:13]
        all_refs, (wt_ref, send_sems, recv_sems, local_sems, wt_sem) = rest[13:13 + nl], rest[13 + nl:]
        wt_copy = pltpu.make_async_copy(wt_ref, wt_out, wt_sem)
        px, py, pc = _my_place()
        my_dev = 4 * px + 2 * py + pc

        def wcopy(a, r, slot):
            dx, dy, dc = FLIPS[r]
            return pltpu.make_async_remote_copy(
                src_ref=sh_refs[a], dst_ref=all_refs[a].at[slot], send_sem=send_sems.at[a, r],
                recv_sem=recv_sems.at[a, r], device_id=(px ^ dx, py ^ dy, pc ^ dc), device_id_type=MESH)

        keep = [pltpu.make_async_copy(sh_refs[a], all_refs[a].at[my_dev], local_sems.at[a]) for a in range(nl)]

        @pl.when(pl.program_id(0) == 0)
        def _():
            for a in range(nl):
                keep[a].start()
                for r in range(len(FLIPS)):
                    wcopy(a, r, my_dev).start()

        @pl.when(pl.program_id(0) == 0)
        def _():
            for j in range(N_DEV):
                src = j * SHARD_PAD
                for a, b in _shard_pad_cols(j):
                    wt_ref[a:b, :] = wsh_ref[src:src + b - a, :]
                    src += b - a
            a, b = SEG["alr"]
            wt_ref[a + B_GATE_RANK:b, :] = jnp.zeros((RANK_PAD - B_GATE_RANK, D_MODEL), BF16)
            wt_copy.start()

        def one_tile(rows):
            x = x_ref[rows, :]
            r = lax.rsqrt(jnp.mean(x * x, axis=-1, keepdims=True) + NORM_EPS)
            h = (x * r * g_ref[...]).astype(BF16)
            h_ref[rows, :] = h

            def seg(name):
                a, b = SEG[name]
                return _dot_nt(h, wt_ref[a:b, :])

            alr = seg("alr").astype(BF16)
            alr_ref[rows, :] = alr
            u = _dot(alr, wa_ref[...]) + ba_ref[...]
            u_ref[rows, :] = u
            log_a = (jnp.minimum(u, 0.0) - jnp.log(1.0 + jnp.exp(-jnp.abs(u)))) * (1.0 / B_GATE_TEMP)
            row, col = _row_iota((sub, sub)), _lane_iota((sub, sub))
            tri = ((row // B_CHUNK == col // B_CHUNK) & (col <= row)).astype(BF16)
            hi, mid, lo = _split3(log_a)
            cum_ref[rows, :] = _dot(tri, hi) + _dot(tri, mid) + _dot(tri, lo)

            cos, sin = cos_ref[rows, :], sin_ref[rows, :]
            qa = seg("qa") * ATT_SCALE
            for s in range(A_WIDTH // LANES):
                qkv_ref[rows, s * LANES:(s + 1) * LANES] = _rope_slab(qa[:, s * LANES:(s + 1) * LANES], cos, sin).astype(BF16)
            qkv_ref[rows, QKV_K:QKV_V] = _rope_slab(seg("ka"), cos, sin).astype(BF16)
            qkv_ref[rows, QKV_V:QKV_W] = seg("va").astype(BF16)
            za_ref[rows, :] = seg("za").astype(BF16)
            q_ref[rows, :] = seg("qb")
            k_ref[rows, :] = seg("kb")
            vb_ref[rows, :] = seg("vb").astype(BF16)
            zb_ref[rows, :] = seg("zb").astype(BF16)
            ga_ref[rows, :] = seg("ga").astype(BF16)
            gb_ref[rows, :] = seg("gb").astype(BF16)

        for j in range(tm // sub):
            one_tile(pl.ds(j * sub, sub))

        @pl.when(pl.program_id(0) == last)
        def _():
            for a in range(nl):
                for r, (dx, dy, dc) in enumerate(FLIPS):
                    wcopy(a, r, 4 * (px ^ dx) + 2 * (py ^ dy) + (pc ^ dc)).wait_recv()
                    wcopy(a, r, my_dev).wait_send()
                keep[a].wait()
            wt_copy.wait()

    def rows(w):
        return pl.BlockSpec((tm, w), lambda i: (i, 0))

    outs = [("h", D_MODEL, BF16), ("qkv", QKV_W, BF16), ("za", A_WIDTH, BF16), ("q", B_KEY_WIDTH, F32),
            ("k", B_KEY_WIDTH, F32), ("vb", B_WIDTH, BF16), ("zb", B_WIDTH, BF16), ("alr", RANK_PAD, BF16),
            ("u", B_KEY_WIDTH, F32), ("cum", B_KEY_WIDTH, F32), ("ga", D_MODEL, BF16), ("gb", D_MODEL, BF16)]
    res = pl.pallas_call(
        body, name="in_proj", grid=(T // tm,),
        in_specs=[rows(D_MODEL), rows(LANES), rows(LANES), _const_spec((1, D_MODEL)),
                  _const_spec((N_DEV * SHARD_PAD, D_MODEL)), _const_spec((RANK_PAD, B_KEY_WIDTH)),
                  _const_spec((1, B_KEY_WIDTH))] + _any_specs(nl),
        out_specs=[rows(w) for _, w, _ in outs] + _any_specs(1 + nl),
        out_shape=[jax.ShapeDtypeStruct((T, w), dt) for _, w, dt in outs]
                  + [jax.ShapeDtypeStruct((D_IN_PAD, D_MODEL), BF16)]
                  + [jax.ShapeDtypeStruct((N_DEV, *sh.shape), sh.dtype) for sh in later_shards],
        scratch_shapes=[pltpu.VMEM((D_IN_PAD, D_MODEL), BF16),
                        pltpu.SemaphoreType.DMA((nl, len(FLIPS))), pltpu.SemaphoreType.DMA((nl, len(FLIPS))),
                        pltpu.SemaphoreType.DMA((nl,)), pltpu.SemaphoreType.DMA],
        compiler_params=_params("arbitrary"),
    )(x2, cosf, sinf, g_in, wt_sh, wa_pad, b_alpha, *later_shards)
    n_out = len(outs) + 1
    return dict(zip([n for n, _, _ in outs] + ["wt_pad"], res[:n_out])), res[n_out:]


def _dup_kv_head(t, g):
    tf = t.astype(F32)
    keep = (_lane_iota(tf.shape) < A_HEAD_DIM) == (g == 0)
    return jnp.where(keep, tf, pltpu.roll(tf, A_HEAD_DIM, 1)).astype(BF16)


def _stack_heads(t):
    lo = _lane_iota(t.shape) < A_HEAD_DIM
    zero = jnp.zeros_like(t)
    return jnp.concatenate([jnp.where(lo, t, zero), jnp.where(lo, zero, t)], axis=0)


ATT_ROWS = A_HEADS * WINDOW
GROUP_ROWS = ATT_ROWS // A_KV_HEADS
HEADS_PER_GROUP = A_HEADS // A_KV_HEADS


def _band_mask_t(n):
    kj = _row_iota((2 * WINDOW, GROUP_ROWS)) - WINDOW
    qi = _lane_iota((2 * WINDOW, GROUP_ROWS)) % WINDOW
    return (kj <= qi) & (qi - kj < WINDOW) & ((n > 0) | (kj >= 0))


def _stacked_queries(ref, g):
    pairs = range(g * HEADS_PER_GROUP // 2, (g + 1) * HEADS_PER_GROUP // 2)
    return jnp.concatenate([_stack_heads(ref[:, p * LANES:(p + 1) * LANES]) for p in pairs], axis=0)


def _unstack_heads(t, g, ref, dtype):
    lo = _lane_iota((WINDOW, LANES)) < A_HEAD_DIM
    for hh in range(HEADS_PER_GROUP // 2):
        p = g * HEADS_PER_GROUP // 2 + hh
        ref[:, p * LANES:(p + 1) * LANES] = jnp.where(lo, t[2 * hh * WINDOW:(2 * hh + 1) * WINDOW],
                                                       t[(2 * hh + 1) * WINDOW:(2 * hh + 2) * WINDOW]).astype(dtype)


FWD_BLOCKS = 16


def _attn_fwd(qkv, sink_row, B, S):
    T = B * S
    nb = S // WINDOW
    blocks = math.gcd(nb, FWD_BLOCKS)
    steps = nb // blocks

    def one_block(has_prev, sink_ref, q, k, v, o_ref, lse_ref):
        valid = _band_mask_t(has_prev)
        lse_rows = []
        for g in range(A_KV_HEADS):
            kd, vd = _dup_kv_head(k, g), _dup_kv_head(v, g)
            s = jnp.where(valid, _dot_nt(kd, _stacked_queries(q, g)), NEG_BIG)
            sink = sink_ref[:, g * GROUP_ROWS:(g + 1) * GROUP_ROWS]
            m = jnp.maximum(jnp.max(s, axis=0, keepdims=True), sink)
            e = jnp.exp(s - m)
            den = jnp.sum(e, axis=0, keepdims=True) + jnp.exp(sink - m)
            o = _dot_tn((e * (1.0 / den)).astype(BF16), vd)
            _unstack_heads(o, g, o_ref, F32)
            lse = m + jnp.log(den)
            lse_rows += [lse[:, j * WINDOW:(j + 1) * WINDOW] for j in range(HEADS_PER_GROUP)]
        by_head = jnp.concatenate(lse_rows + [jnp.zeros((WINDOW - A_HEADS, WINDOW), F32)], axis=0)
        lse_ref[...] = by_head.T

    def body(sink_ref, q_ref, kc_ref, vc_ref, kp_ref, vp_ref, o_ref, lse_ref):
        k_all = jnp.concatenate([kp_ref[...], kc_ref[...]], axis=0)
        v_all = jnp.concatenate([vp_ref[...], vc_ref[...]], axis=0)
        for j in range(blocks):
            rows = pl.ds(j * WINDOW, WINDOW)
            keys = slice(j * WINDOW, (j + 2) * WINDOW)
            has_prev = pl.program_id(1) if j == 0 else 1
            one_block(has_prev, sink_ref, q_ref[rows, :], k_all[keys], v_all[keys], o_ref.at[rows], lse_ref.at[rows])

    def cur(col, w):
        return pl.BlockSpec((blocks * WINDOW, w), lambda b, n: (b * steps + n, col))

    def prev(col):
        return pl.BlockSpec((WINDOW, LANES), lambda b, n: (b * nb + jnp.maximum(blocks * n - 1, 0), col))

    kcol, vcol = QKV_K // LANES, QKV_V // LANES
    return pl.pallas_call(
        body, name="attn_fwd", grid=(B, steps),
        in_specs=[_const_spec((1, ATT_ROWS)), cur(0, A_WIDTH), cur(kcol, LANES), cur(vcol, LANES), prev(kcol), prev(vcol)],
        out_specs=[cur(0, A_WIDTH), cur(0, LANES)],
        out_shape=[jax.ShapeDtypeStruct((T, A_WIDTH), F32), jax.ShapeDtypeStruct((T, LANES), F32)],
        compiler_params=_params("parallel", "parallel"),
    )(sink_row, qkv, qkv, qkv, qkv, qkv)


ATT_CHUNK = 128


def _chunk_masks(n):
    masks = []
    for half in range(WINDOW // ATT_CHUNK):
        qi = _row_iota((ATT_CHUNK, 2 * WINDOW)) + half * ATT_CHUNK
        kj = _lane_iota((ATT_CHUNK, 2 * WINDOW)) - WINDOW
        masks.append((kj <= qi) & (qi - kj < WINDOW) & ((n > 0) | (kj >= 0)))
    return masks


def _all_stacked_queries(ref):
    return jnp.concatenate([_stacked_queries(ref, g) for g in range(A_KV_HEADS)], axis=0)


def _by_group(fn, lhs, rhs_per_group):
    return jnp.concatenate([fn(lhs[g * GROUP_ROWS:(g + 1) * GROUP_ROWS], rhs_per_group[g])
                            for g in range(A_KV_HEADS)], axis=0)


BWD_BLOCKS = 8


def _attn_bwd(qkv, do, out, lse, sink_col, B, S):
    T = B * S
    nb = S // WINDOW
    M = math.gcd(nb, BWD_BLOCKS)
    steps = nb // M
    n_chunks = ATT_ROWS // ATT_CHUNK
    halves = WINDOW // ATT_CHUNK

    def block(has_prev, sink_ref, q, do_b, out_b, lse_b, k, v, scratch, want_dq):
        s_ref, dp_ref, ds_ref, p_ref = scratch
        width = k.shape[0]
        masks = [mk[:, 0:width] for mk in _chunk_masks(has_prev)]
        kd = [_dup_kv_head(k, g) for g in range(A_KV_HEADS)]
        vd = [_dup_kv_head(v, g) for g in range(A_KV_HEADS)]
        qs, dos = _all_stacked_queries(q), _all_stacked_queries(do_b)
        s_ref[...] = _by_group(_dot_nt, qs, kd)
        dp_ref[...] = _by_group(_dot_nt, dos, vd)
        lane = _lane_iota((ATT_CHUNK, LANES))
        lo = lane < A_HEAD_DIM
        lane1 = _lane_iota((1, LANES))
        dsink_row = jnp.zeros((1, LANES), F32)
        for c in range(n_chunks):
            rows = slice(c * ATT_CHUNK, (c + 1) * ATT_CHUNK)
            head, half = divmod(c, halves)
            qrows = slice(half * ATT_CHUNK, (half + 1) * ATT_CHUNK)
            slab = slice((head // 2) * LANES, (head // 2 + 1) * LANES)
            lse_col = jnp.sum(jnp.where(lane == head, lse_b[qrows, :], 0.0), axis=-1, keepdims=True)
            prod = do_b[qrows, slab].astype(F32) * out_b[qrows, slab].astype(F32)
            mine = lo if head % 2 == 0 else jnp.logical_not(lo)
            delta = jnp.sum(jnp.where(mine, prod, 0.0), axis=-1, keepdims=True)
            prob = jnp.exp(jnp.where(masks[half], s_ref[rows, :], NEG_BIG) - lse_col)
            p_ref[rows, :] = prob.astype(BF16)
            ds_ref[rows, :] = (prob * (dp_ref[rows, :] - delta)).astype(BF16)
            w = -jnp.exp(sink_ref[rows, :] - lse_col) * delta
            dsink_row += jnp.where(lane1 == head, jnp.sum(w, axis=0, keepdims=True), 0.0)
        dq = _by_group(_dot, ds_ref[...], kd) * ATT_SCALE if want_dq else None
        groups = [slice(g * GROUP_ROWS, (g + 1) * GROUP_ROWS) for g in range(A_KV_HEADS)]
        dk = [_dot_tn(ds_ref[rows, :], qs[rows]) for rows in groups]
        dv = [_dot_tn(p_ref[rows, :], dos[rows]) for rows in groups]
        return dq, dk, dv, dsink_row

    def fold(per_group):
        lane = _lane_iota((WINDOW, LANES))
        out = jnp.zeros((WINDOW, LANES), F32)
        for g, acc in enumerate(per_group):
            out = jnp.where((lane < A_HEAD_DIM) == (g == 0), acc + pltpu.roll(acc, A_HEAD_DIM, 1), out)
        return out

    def body(sink_ref, q_ref, qn_ref, do_ref, don_ref, out_ref, outn_ref, lse_ref, lsen_ref, kc_ref, kp_ref, vc_ref, vp_ref,
             dq_ref, dkv_ref, dsink_ref, s_scr, dp_scr, ds_scr, p_scr, s_x, dp_x, ds_x, p_x):
        b, m = pl.program_id(0), pl.program_id(1)

        @pl.when((b == 0) & (m == 0))
        def _():
            dsink_ref[...] = jnp.zeros_like(dsink_ref)

        k_all = jnp.concatenate([kp_ref[...], kc_ref[...]], axis=0)
        v_all = jnp.concatenate([vp_ref[...], vc_ref[...]], axis=0)
        results = []
        for j in range(M):
            rows = slice(j * WINDOW, (j + 1) * WINDOW)
            keys = slice(j * WINDOW, (j + 2) * WINDOW)
            has_prev = m if j == 0 else 1
            results.append(block(has_prev, sink_ref, q_ref[rows, :], do_ref[rows, :], out_ref[rows, :], lse_ref[rows, :],
                                 k_all[keys], v_all[keys], (s_scr.at[j], dp_scr.at[j], ds_scr.at[j], p_scr.at[j]), True))
        last_keys = slice(M * WINDOW, (M + 1) * WINDOW)
        _, dk_x, dv_x, _ = block(1, sink_ref, qn_ref[...], don_ref[...], outn_ref[...], lsen_ref[...],
                                 k_all[last_keys], v_all[last_keys], (s_x, dp_x, ds_x, p_x), False)
        has_next = m < steps - 1
        lo_q = _lane_iota((WINDOW, LANES)) < A_HEAD_DIM
        dsink_row = jnp.zeros((1, LANES), F32)
        for j, (dq, dk, dv, ds_row) in enumerate(results):
            rows = slice(j * WINDOW, (j + 1) * WINDOW)
            for p in range(A_HEADS // 2):
                dq_ref[rows, p * LANES:(p + 1) * LANES] = jnp.where(
                    lo_q, dq[2 * p * WINDOW:(2 * p + 1) * WINDOW], dq[(2 * p + 1) * WINDOW:(2 * p + 2) * WINDOW]).astype(BF16)
            if j + 1 < M:
                dk_next = [t[0:WINDOW] for t in results[j + 1][1]]
                dv_next = [t[0:WINDOW] for t in results[j + 1][2]]
            else:
                dk_next = [jnp.where(has_next, t, 0.0) for t in dk_x]
                dv_next = [jnp.where(has_next, t, 0.0) for t in dv_x]
            dkv_ref[rows, 0:LANES] = fold([own[WINDOW:] + nxt for own, nxt in zip(dk, dk_next)]).astype(BF16)
            dkv_ref[rows, LANES:] = fold([own[WINDOW:] + nxt for own, nxt in zip(dv, dv_next)]).astype(BF16)
            dsink_row += ds_row
        dsink_ref[0:1, :] += dsink_row

    def cur(col, w):
        return pl.BlockSpec((M * WINDOW, w), lambda b, m: (b * steps + m, col))

    def nxt(col, w):
        return pl.BlockSpec((WINDOW, w), lambda b, m: (b * nb + jnp.minimum(M * (m + 1), nb - 1), col))

    def prev(col):
        return pl.BlockSpec((WINDOW, LANES), lambda b, m: (b * nb + jnp.maximum(M * m - 1, 0), col))

    kcol, vcol = QKV_K // LANES, QKV_V // LANES
    scores = (M, ATT_ROWS, 2 * WINDOW)
    extra = (ATT_ROWS, WINDOW)
    return pl.pallas_call(
        body, name="attn_bwd", grid=(B, steps),
        in_specs=[_const_spec((ATT_ROWS, 1)), cur(0, A_WIDTH), nxt(0, A_WIDTH), cur(0, A_WIDTH), nxt(0, A_WIDTH),
                  cur(0, A_WIDTH), nxt(0, A_WIDTH), cur(0, LANES), nxt(0, LANES),
                  cur(kcol, LANES), prev(kcol), cur(vcol, LANES), prev(vcol)],
        out_specs=[cur(0, A_WIDTH), cur(0, 2 * LANES), pl.BlockSpec((8, LANES), lambda b, m: (0, 0))],
        out_shape=[jax.ShapeDtypeStruct((T, A_WIDTH), BF16), jax.ShapeDtypeStruct((T, 2 * LANES), BF16),
                   jax.ShapeDtypeStruct((8, LANES), F32)],
        scratch_shapes=[pltpu.VMEM(scores, F32), pltpu.VMEM(scores, F32), pltpu.VMEM(scores, BF16), pltpu.VMEM(scores, BF16),
                        pltpu.VMEM(extra, F32), pltpu.VMEM(extra, F32), pltpu.VMEM(extra, BF16), pltpu.VMEM(extra, BF16)],
        compiler_params=_params("arbitrary", "arbitrary"),
    )(sink_col, qkv, qkv, do, do, out, out, lse, lse, qkv, qkv, qkv, qkv)


GLA_FWD_TILING = (64, 16)
GLA_BWD_TILING = (256, 4)


def _gla_factors(q_ref, k_ref, cum_ref):
    cpt = q_ref.shape[0] // B_CHUNK
    scale = B_KEY_DIM ** -0.5
    cum = cum_ref[...]
    shape = (B_CHUNK, B_KEY_WIDTH)
    last = jnp.concatenate([jnp.broadcast_to(cum_ref[pl.ds(c * B_CHUNK + B_CHUNK - 1, 1), :], shape)
                            for c in range(cpt)], axis=0)
    mid = jnp.concatenate([jnp.broadcast_to(cum_ref[pl.ds(c * B_CHUNK + B_CHUNK // 2 - 1, 1), :], shape)
                           for c in range(cpt)], axis=0)
    e_qm, e_km, e_qe, e_kd = jnp.exp(cum - mid), jnp.exp(mid - cum), jnp.exp(cum), jnp.exp(last - cum)
    qs = q_ref[...] * scale
    k = k_ref[...]
    return qs, k, (e_qm, e_km, e_qe, e_kd)


def _head_mask(shape, h):
    return (_lane_iota(shape) // B_KEY_DIM) == h


def _stack_masked(t):
    return jnp.concatenate([jnp.where(_head_mask(t.shape, h), t, 0.0) for h in range(B_HEADS)], axis=0).astype(BF16)


def _select_heads(t):
    shape = (B_CHUNK, B_KEY_WIDTH)
    out = jnp.zeros(shape, F32)
    for h in range(B_HEADS):
        out = jnp.where(_head_mask(shape, h), t[h * B_CHUNK:(h + 1) * B_CHUNK], out)
    return out


def _select_state(t):
    shape = (B_VAL_DIM, B_KEY_WIDTH)
    out = jnp.zeros(shape, F32)
    for h in range(B_HEADS):
        out = jnp.where(_head_mask(shape, h), t[h * B_VAL_DIM:(h + 1) * B_VAL_DIM], out)
    return out


def _rows_by_head(t):
    return jnp.concatenate([t[:, h * B_VAL_DIM:(h + 1) * B_VAL_DIM] for h in range(B_HEADS)], axis=0)


def _intra_mask(tile_rows):
    i, j = _row_iota((tile_rows, tile_rows)), _lane_iota((tile_rows, tile_rows))
    return (i // B_CHUNK == j // B_CHUNK) & (j <= i)


def _pair_stack(t, p):
    slab = t[:, p * LANES:(p + 1) * LANES]
    lo = _lane_iota(slab.shape) < B_KEY_DIM
    return jnp.concatenate([jnp.where(lo, slab, 0.0), jnp.where(lo, 0.0, slab)], axis=0).astype(BF16)


def _gla_fwd(q, k, cum, vb, B, S):
    T = B * S
    tile_rows = math.gcd(S, GLA_FWD_TILING[0])
    cpt = tile_rows // B_CHUNK
    nt = S // tile_rows
    tps = math.gcd(nt, GLA_FWD_TILING[1])

    def one_sequence(q_ref, k_ref, cum_ref, v_ref, o_ref, st_all_ref, st_ref):
        qs, kk, (e_qm, e_km, e_qe, e_kd) = _gla_factors(q_ref, k_ref, cum_ref)
        qm, km, qe, kd = qs * e_qm, kk * e_km, qs * e_qe, (kk * e_kd).astype(BF16)
        mask = _intra_mask(tile_rows)
        intra = []
        for p in range(B_HEADS // 2):
            a = _dot_nt(_pair_stack(qm, p), km[:, p * LANES:(p + 1) * LANES].astype(BF16))
            for hh in range(2):
                h = 2 * p + hh
                att = jnp.where(mask, a[hh * tile_rows:(hh + 1) * tile_rows], 0.0).astype(BF16)
                intra.append(_dot(att, v_ref[:, h * B_VAL_DIM:(h + 1) * B_VAL_DIM]))
        inter = []
        for c in range(cpt):
            rows = slice(c * B_CHUNK, (c + 1) * B_CHUNK)
            st = st_ref[...]
            st_all_ref[c] = st
            inter.append(_dot_nt(_stack_masked(qe[rows]), st.astype(BF16)))
            inc = _select_state(_dot_tn(v_ref[rows, :], kd[rows]))
            decay = jnp.exp(cum_ref[pl.ds(c * B_CHUNK + B_CHUNK - 1, 1), :])
            st_ref[...] = st * decay + inc
        for h in range(B_HEADS):
            oi = jnp.concatenate([inter[c][h * B_CHUNK:(h + 1) * B_CHUNK] for c in range(cpt)], axis=0)
            o_ref[:, h * B_VAL_DIM:(h + 1) * B_VAL_DIM] = (intra[h] + oi).astype(BF16)

    def body(q_ref, k_ref, cum_ref, v_ref, o_ref, st_all_ref, st_ref):
        @pl.when(pl.program_id(0) == 0)
        def _():
            st_ref[...] = jnp.zeros_like(st_ref)

        for b in range(B):
            for tile in range(tps):
                tok = pl.ds(tile * tile_rows, tile_rows)
                chunks = pl.ds(tile * cpt, cpt)
                one_sequence(*[r.at[b, tok] for r in (q_ref, k_ref, cum_ref, v_ref, o_ref)],
                             st_all_ref.at[b, chunks], st_ref.at[b])

    def rows(w):
        return pl.BlockSpec((B, tps * tile_rows, w), lambda t: (0, t, 0))

    seq = lambda a: a.reshape(B, S, a.shape[-1])
    o, st_all = pl.pallas_call(
        body, name="gla_fwd", grid=(nt // tps,),
        in_specs=[rows(B_KEY_WIDTH), rows(B_KEY_WIDTH), rows(B_KEY_WIDTH), rows(B_WIDTH)],
        out_specs=[rows(B_WIDTH),
                   pl.BlockSpec((B, tps * cpt, B_VAL_DIM, B_KEY_WIDTH), lambda t: (0, t, 0, 0))],
        out_shape=[jax.ShapeDtypeStruct((B, S, B_WIDTH), BF16),
                   jax.ShapeDtypeStruct((B, S // B_CHUNK, B_VAL_DIM, B_KEY_WIDTH), F32)],
        scratch_shapes=[pltpu.VMEM((B, B_VAL_DIM, B_KEY_WIDTH), F32)],
        compiler_params=_params("arbitrary"),
    )(seq(q), seq(k), seq(cum), seq(vb))
    return o.reshape(T, B_WIDTH), st_all.reshape(T // B_CHUNK, B_VAL_DIM, B_KEY_WIDTH)


def _gla_bwd(q, k, cum, vb, do, st_all, B, S, wgrads):
    T = B * S
    tile_rows = math.gcd(S, GLA_BWD_TILING[0])
    cpt = tile_rows // B_CHUNK
    nt = S // tile_rows
    tps = math.gcd(nt, GLA_BWD_TILING[1])
    steps = nt // tps
    scale = B_KEY_DIM ** -0.5
    nw = len(wgrads)

    def one_sequence(q_ref, k_ref, cum_ref, v_ref, do_ref, st_all_ref, dq_ref, dk_ref, dv_ref, dla_ref, dst_ref):
        qs, kk, (e_qm, e_km, e_qe, e_kd) = _gla_factors(q_ref, k_ref, cum_ref)
        qm, km, qe, kd = qs * e_qm, kk * e_km, qs * e_qe, kk * e_kd
        mask = _intra_mask(tile_rows)
        dqm_slabs, dkm_slabs, dv_intra = [], [], []
        for p in range(B_HEADS // 2):
            qm_st = _pair_stack(qm, p)
            km_p = km[:, p * LANES:(p + 1) * LANES].astype(BF16)
            a = _dot_nt(qm_st, km_p)
            da_blocks, dqm_h = [], []
            for hh in range(2):
                h = 2 * p + hh
                vs = slice(h * B_VAL_DIM, (h + 1) * B_VAL_DIM)
                att = jnp.where(mask, a[hh * tile_rows:(hh + 1) * tile_rows], 0.0).astype(BF16)
                dv_intra.append(_dot_tn(att, do_ref[:, vs]))
                da = jnp.where(mask, _dot_nt(do_ref[:, vs], v_ref[:, vs]), 0.0).astype(BF16)
                da_blocks.append(da)
                dqm_h.append(_dot(da, km_p))
            lo = _lane_iota((tile_rows, LANES)) < B_KEY_DIM
            dqm_slabs.append(jnp.where(lo, dqm_h[0], dqm_h[1]))
            dkm_slabs.append(_dot_tn(jnp.concatenate(da_blocks, axis=0), qm_st))
        dqm = jnp.concatenate(dqm_slabs, axis=1)
        dkm = jnp.concatenate(dkm_slabs, axis=1)

        dqe_c, dkd_c, dv_inter, tail_c = ([None] * cpt for _ in range(4))
        for c in reversed(range(cpt)):
            rows = slice(c * B_CHUNK, (c + 1) * B_CHUNK)
            dst = dst_ref[...]
            dst_b = dst.astype(BF16)
            dv_inter[c] = _dot_nt(_stack_masked(kd[rows]), dst_b)
            dkd_c[c] = _select_heads(_dot(_rows_by_head(v_ref[rows, :]), dst_b))
            do_c = do_ref[rows, :]
            dqe_c[c] = _select_heads(_dot(_rows_by_head(do_c), st_all_ref[c].astype(BF16)))
            contrib = _select_state(_dot_tn(do_c, qe[rows].astype(BF16)))
            decay = jnp.exp(cum_ref[pl.ds(c * B_CHUNK + B_CHUNK - 1, 1), :])
            tail = (jnp.sum(kk[rows] * dkd_c[c] * e_kd[rows], axis=0, keepdims=True)
                    + decay * jnp.sum(st_all_ref[c] * dst, axis=0, keepdims=True))
            tail_c[c] = jnp.broadcast_to(tail, (B_CHUNK, B_KEY_WIDTH))
            dst_ref[...] = dst * decay + contrib
        dqe = jnp.concatenate(dqe_c, axis=0)
        dkd = jnp.concatenate(dkd_c, axis=0)
        dqs = dqm * e_qm + dqe * e_qe
        dk = dkm * e_km + dkd * e_kd
        dq_ref[...] = (dqs * scale).astype(BF16)
        dk_ref[...] = dk.astype(BF16)
        for h in range(B_HEADS):
            dvi = jnp.concatenate([dv_inter[c][h * B_CHUNK:(h + 1) * B_CHUNK] for c in range(cpt)], axis=0)
            dv_ref[:, h * B_VAL_DIM:(h + 1) * B_VAL_DIM] = (dv_intra[h] + dvi).astype(BF16)
        dd = qs * dqs - kk * dk
        i, j = _row_iota((tile_rows, tile_rows)), _lane_iota((tile_rows, tile_rows))
        upper = ((i // B_CHUNK == j // B_CHUNK) & (j >= i)).astype(BF16)
        hi, mid, lo3 = _split3(dd)
        dla_ref[...] = _dot(upper, hi) + _dot(upper, mid) + _dot(upper, lo3) + jnp.concatenate(tail_c, axis=0)

    def body(q_ref, k_ref, cum_ref, v_ref, do_ref, st_all_ref, *rest):
        g_refs, (dq_ref, dk_ref, dv_ref, dla_ref) = rest[:nw], rest[nw:nw + 4]
        rv_refs, (dst_ref, send_sems, recv_sems) = rest[nw + 4:2 * nw + 4], rest[2 * nw + 4:]
        x, y, c = _my_place()

        def wcopy(a, r):
            dx, dy, dc = FLIPS[r]
            return pltpu.make_async_remote_copy(
                src_ref=g_refs[a].at[4 * (x ^ dx) + 2 * (y ^ dy) + (c ^ dc)], dst_ref=rv_refs[a].at[r],
                send_sem=send_sems.at[a, r], recv_sem=recv_sems.at[a, r],
                device_id=(x ^ dx, y ^ dy, c ^ dc), device_id_type=MESH)

        @pl.when(pl.program_id(0) == 0)
        def _():
            dst_ref[...] = jnp.zeros_like(dst_ref)
            for a in range(nw):
                for r in range(len(FLIPS)):
                    wcopy(a, r).start()

        for b in range(B):
            for tile in reversed(range(tps)):
                tok = pl.ds(tile * tile_rows, tile_rows)
                chunks = pl.ds(tile * cpt, cpt)
                one_sequence(*[r.at[b, tok] for r in (q_ref, k_ref, cum_ref, v_ref, do_ref)], st_all_ref.at[b, chunks],
                             *[r.at[b, tok] for r in (dq_ref, dk_ref, dv_ref, dla_ref)], dst_ref.at[b])

        @pl.when(pl.program_id(0) == steps - 1)
        def _():
            for a in range(nw):
                for r in range(len(FLIPS)):
                    wcopy(a, r).wait()

    def rows(w):
        return pl.BlockSpec((B, tps * tile_rows, w), lambda t: (0, steps - 1 - t, 0))

    seq = lambda a: a.reshape(B, S, a.shape[-1])
    res = pl.pallas_call(
        body, name="gla_bwd", grid=(steps,),
        in_specs=[rows(B_KEY_WIDTH), rows(B_KEY_WIDTH), rows(B_KEY_WIDTH), rows(B_WIDTH), rows(B_WIDTH),
                  pl.BlockSpec((B, tps * cpt, B_VAL_DIM, B_KEY_WIDTH), lambda t: (0, steps - 1 - t, 0, 0))]
                 + _any_specs(nw),
        out_specs=[rows(B_KEY_WIDTH), rows(B_KEY_WIDTH), rows(B_WIDTH), rows(B_KEY_WIDTH)] + _any_specs(nw),
        out_shape=[jax.ShapeDtypeStruct((B, S, B_KEY_WIDTH), BF16), jax.ShapeDtypeStruct((B, S, B_KEY_WIDTH), BF16),
                   jax.ShapeDtypeStruct((B, S, B_WIDTH), BF16), jax.ShapeDtypeStruct((B, S, B_KEY_WIDTH), F32)]
                  + [jax.ShapeDtypeStruct((len(FLIPS), *g.shape[1:]), g.dtype) for g in wgrads],
        scratch_shapes=[pltpu.VMEM((B, B_VAL_DIM, B_KEY_WIDTH), F32),
                        pltpu.SemaphoreType.DMA((nw, len(FLIPS))), pltpu.SemaphoreType.DMA((nw, len(FLIPS)))],
        compiler_params=_params("arbitrary"),
    )(seq(q), seq(k), seq(cum), seq(vb), seq(do), st_all.reshape(B, S // B_CHUNK, B_VAL_DIM, B_KEY_WIDTH), *wgrads)
    return [a.reshape(T, a.shape[-1]) for a in res[:4]], res[4:]


def _merge(x2, tgt2, attn, za, o_gla, zb, ga, gb, w_oa_sh, w_ob_sh, w_o, g_gla, g_final):
    T = x2.shape[0]
    tm = math.gcd(T, 512)
    sub = math.gcd(tm, 256)
    last = T // tm - 1

    def body(x_ref, tgt_ref, attn_ref, za_ref, og_ref, zb_ref, ga_ref, gb_ref,
             woa_sh_ref, wob_sh_ref, wo_ref, gg_ref, gf_ref,
             dxres_ref, dattn_ref, dog_ref, dza_ref, dzb_ref, dga_ref, dgb_ref,
             dwo_out, dwoa_out, dwob_out, small_ref,
             awo_ref, awoa_ref, awob_ref, agf_ref, agg_ref, loss_ref, woa_ref, wob_ref,
             dwo_ref, dwoa_ref, dwob_ref, w_sems, dw_sems):
        w_copies = [pltpu.make_async_copy(sh.at[j], dst.at[:, j * SHARD_OUT:(j + 1) * SHARD_OUT], w_sems.at[a, j])
                    for a, (sh, dst) in enumerate(((woa_sh_ref, woa_ref), (wob_sh_ref, wob_ref))) for j in range(N_DEV)]
        dw_copies = [pltpu.make_async_copy(src, dst, dw_sems.at[a])
                     for a, (src, dst) in enumerate(((dwo_ref, dwo_out), (dwoa_ref, dwoa_out), (dwob_ref, dwob_out)))]

        @pl.when(pl.program_id(0) == 0)
        def _():
            for cp in w_copies:
                cp.start()
            for r in (awo_ref, awoa_ref, awob_ref, agf_ref, agg_ref, loss_ref):
                r[...] = jnp.zeros_like(r)
            for cp in w_copies:
                cp.wait()

        def one_tile(rows):
            za_v = za_ref[rows, :].astype(F32)
            sig_za = _sigmoid_tanh(za_v)
            silu_a = za_v * sig_za
            attn_v = attn_ref[rows, :].astype(F32)
            oa = (attn_v * silu_a).astype(BF16)
            ya = _dot(oa, woa_ref[...])
            og = og_ref[rows, :].astype(F32)
            zb_v = zb_ref[rows, :].astype(F32)
            sig_zb = _sigmoid_tanh(zb_v)
            silu_b = zb_v * sig_zb
            gg = gg_ref[...]
            on_parts, rinv_parts = [], []
            for h in range(B_HEADS):
                seg = og[:, h * B_VAL_DIM:(h + 1) * B_VAL_DIM]
                rinv = lax.rsqrt(jnp.mean(seg * seg, axis=-1, keepdims=True) + NORM_EPS)
                rinv_parts.append(rinv)
                on_parts.append(seg * rinv)
            on = jnp.concatenate(on_parts, axis=1)
            obn = on * gg
            ob = (obn * silu_b).astype(BF16)
            yb = _dot(ob, wob_ref[...])
            sig_a = _sigmoid_tanh(ga_ref[rows, :].astype(F32))
            sig_b = _sigmoid_tanh(gb_ref[rows, :].astype(F32))
            merged = (sig_a * ya + sig_b * yb).astype(BF16)
            out = x_ref[rows, :] + _dot(merged, wo_ref[...])
            rf = lax.rsqrt(jnp.mean(out * out, axis=-1, keepdims=True) + NORM_EPS)
            nrm = out * rf
            gf = gf_ref[...]
            err = nrm * gf - tgt_ref[rows, :]
            loss = jnp.sum(err * err) * (0.5 / D_MODEL)

            dy = err * (1.0 / D_MODEL)
            dgf = jnp.sum(dy * nrm, axis=0, keepdims=True)
            dn = dy * gf
            dout = rf * (dn - nrm * jnp.mean(dn * nrm, axis=-1, keepdims=True))
            dxres_ref[rows, :] = dout
            dout_b = dout.astype(BF16)
            dmerged = _dot_nt(dout_b, wo_ref[...])
            dya = dmerged * sig_a
            dyb = dmerged * sig_b
            dga_ref[rows, :] = (dmerged * ya * sig_a * (1.0 - sig_a)).astype(BF16)
            dgb_ref[rows, :] = (dmerged * yb * sig_b * (1.0 - sig_b)).astype(BF16)
            dya_b, dyb_b = dya.astype(BF16), dyb.astype(BF16)
            doa = _dot_nt(dya_b, woa_ref[...])
            dattn_ref[rows, :] = (doa * silu_a).astype(BF16)
            dza_ref[rows, :] = (doa * attn_v * (sig_za * (1.0 + za_v * (1.0 - sig_za)))).astype(BF16)
            dob = _dot_nt(dyb_b, wob_ref[...])
            dzb_ref[rows, :] = (dob * obn * (sig_zb * (1.0 + zb_v * (1.0 - sig_zb)))).astype(BF16)
            dobn = dob * silu_b
            dgg = jnp.sum(dobn * on, axis=0, keepdims=True)
            don = dobn * gg
            for h in range(B_HEADS):
                sl = slice(h * B_VAL_DIM, (h + 1) * B_VAL_DIM)
                don_h, on_h = don[:, sl], on[:, sl]
                dog_ref[rows, sl] = (rinv_parts[h] * (don_h - on_h * jnp.mean(don_h * on_h, axis=-1, keepdims=True))
                                     ).astype(BF16)
            return (merged, dout_b, oa, dya_b, ob, dyb_b), (loss, dgf, dgg)

        tiles = [one_tile(pl.ds(j * sub, sub)) for j in range(tm // sub)]
        merged, dout_b, oa, dya_b, ob, dyb_b = (jnp.concatenate(parts, axis=0) for parts in zip(*[t[0] for t in tiles]))
        awo_ref[...] += _dot_tn(merged, dout_b)
        awoa_ref[...] += _dot_tn(oa, dya_b)
        awob_ref[...] += _dot_tn(ob, dyb_b)
        for _, (loss, dgf, dgg) in tiles:
            loss_ref[...] += loss
            agf_ref[...] += dgf
            agg_ref[...] += dgg

        @pl.when(pl.program_id(0) == last)
        def _():
            for j in range(N_DEV):
                dwo_ref[j] = awo_ref[j * SHARD_OUT:(j + 1) * SHARD_OUT, :].astype(BF16)
                dwoa_ref[j] = awoa_ref[:, j * SHARD_OUT:(j + 1) * SHARD_OUT].astype(BF16)
                dwob_ref[j] = awob_ref[:, j * SHARD_OUT:(j + 1) * SHARD_OUT].astype(BF16)
            small_ref[...] = jnp.zeros_like(small_ref)
            _put_rows(small_ref, SMALL_G_FINAL, agf_ref[...])
            _put_rows(small_ref, SMALL_G_GLA, agg_ref[...])
            small_ref[SMALL_LOSS:SMALL_LOSS + 1, :] = loss_ref[...]
            for cp in dw_copies:
                cp.start()
            for cp in dw_copies:
                cp.wait()

    def rows(w):
        return pl.BlockSpec((tm, w), lambda i: (i, 0))

    def whole(shape):
        nd = len(shape)
        return pl.BlockSpec(shape, lambda i: (0,) * nd)

    outs = [((T, D_MODEL), F32, rows(D_MODEL)), ((T, A_WIDTH), BF16, rows(A_WIDTH)), ((T, B_WIDTH), BF16, rows(B_WIDTH)),
            ((T, A_WIDTH), BF16, rows(A_WIDTH)), ((T, B_WIDTH), BF16, rows(B_WIDTH)),
            ((T, D_MODEL), BF16, rows(D_MODEL)), ((T, D_MODEL), BF16, rows(D_MODEL)),
            ((N_DEV, SHARD_OUT, D_MODEL), BF16, pl.BlockSpec(memory_space=pl.ANY)),
            ((N_DEV, A_WIDTH, SHARD_OUT), BF16, pl.BlockSpec(memory_space=pl.ANY)),
            ((N_DEV, B_WIDTH, SHARD_OUT), BF16, pl.BlockSpec(memory_space=pl.ANY)),
            ((SMALL_SINKS, LANES), F32, whole((SMALL_SINKS, LANES)))]
    return pl.pallas_call(
        body, name="merge", grid=(T // tm,),
        in_specs=[rows(D_MODEL), rows(D_MODEL), rows(A_WIDTH), rows(A_WIDTH), rows(B_WIDTH), rows(B_WIDTH),
                  rows(D_MODEL), rows(D_MODEL),
                  pl.BlockSpec(memory_space=pl.ANY), pl.BlockSpec(memory_space=pl.ANY),
                  _const_spec((D_MODEL, D_MODEL)), _const_spec((1, B_WIDTH)), _const_spec((1, D_MODEL))],
        out_specs=[o[2] for o in outs],
        out_shape=[jax.ShapeDtypeStruct(o[0], o[1]) for o in outs],
        scratch_shapes=[pltpu.VMEM((D_MODEL, D_MODEL), F32), pltpu.VMEM((A_WIDTH, D_MODEL), F32),
                        pltpu.VMEM((B_WIDTH, D_MODEL), F32), pltpu.VMEM((1, D_MODEL), F32), pltpu.VMEM((1, B_WIDTH), F32),
                        pltpu.VMEM((1, LANES), F32), pltpu.VMEM((A_WIDTH, D_MODEL), BF16),
                        pltpu.VMEM((B_WIDTH, D_MODEL), BF16),
                        pltpu.VMEM((N_DEV, SHARD_OUT, D_MODEL), BF16), pltpu.VMEM((N_DEV, A_WIDTH, SHARD_OUT), BF16),
                        pltpu.VMEM((N_DEV, B_WIDTH, SHARD_OUT), BF16),
                        pltpu.SemaphoreType.DMA((2, N_DEV)), pltpu.SemaphoreType.DMA((3,))],
        compiler_params=pltpu.CompilerParams(dimension_semantics=("arbitrary",), vmem_limit_bytes=V7X_VMEM_LIMIT_MAX),
    )(x2, tgt2, attn, za, o_gla, zb, ga, gb, w_oa_sh, w_ob_sh, w_o, g_gla, g_final)


def _in_proj_bwd(x2, dxres, cosf, sinf, g_in, wt_pad, wa_pad, parts):
    T = x2.shape[0]
    tm = math.gcd(T, 512)
    sub = math.gcd(tm, 256)
    last = T // tm - 1
    base = SMALL_G_IN

    def body(x_ref, dxres_ref, cos_ref, sin_ref, g_ref, wt_ref, wa_ref,
             dq_ref, dkv_ref, dza_ref, dqb_ref, dkb_ref, dvb_ref, dzb_ref, dla_ref, u_ref, alr_ref, dga_ref, dgb_ref,
             dx_ref, dsh_ref, small_ref, dproj_ref, agin_ref, aba_ref, awa_ref):
        @pl.when(pl.program_id(0) == 0)
        def _():
            for r in (agin_ref, aba_ref, awa_ref):
                r[...] = jnp.zeros_like(r)

        def one_tile(rows):
            cos, nsin = cos_ref[rows, :], -sin_ref[rows, :]
            for s in range(A_WIDTH // LANES):
                sl = slice(s * LANES, (s + 1) * LANES)
                dproj_ref[rows, sl] = _rope_slab(dq_ref[rows, sl].astype(F32), cos, nsin).astype(BF16)
            dproj_ref[rows, QKV_K:QKV_V] = _rope_slab(dkv_ref[rows, 0:LANES].astype(F32), cos, nsin).astype(BF16)
            dproj_ref[rows, QKV_V:QKV_W] = dkv_ref[rows, LANES:]

            def put(name, val):
                a, b = SEG[name]
                dproj_ref[rows, a:b] = val

            put("za", dza_ref[rows, :])
            put("qb", dqb_ref[rows, :])
            put("kb", dkb_ref[rows, :])
            put("vb", dvb_ref[rows, :])
            put("zb", dzb_ref[rows, :])
            put("ga", dga_ref[rows, :])
            put("gb", dgb_ref[rows, :])
            du = dla_ref[rows, :] * (1.0 / B_GATE_TEMP) * _sigmoid(-u_ref[rows, :])
            du_b = du.astype(BF16)
            put("alr", _dot_nt(du_b, wa_ref[...]).astype(BF16))

            for j in range(N_DEV):
                col = (j % 2) * SHARD_PAD
                for a, b in _shard_pad_cols(j):
                    dsh_ref[j // 2, rows, col:col + b - a] = dproj_ref[rows, a:b]
                    col += b - a
                dsh_ref[j // 2, rows, col:(j % 2 + 1) * SHARD_PAD] = jnp.zeros((sub, SHARD_PAD - SHARD_IN), BF16)

            dh = _dot(dproj_ref[rows, :], wt_ref[...])
            x = x_ref[rows, :]
            r = lax.rsqrt(jnp.mean(x * x, axis=-1, keepdims=True) + NORM_EPS)
            nrm = x * r
            dn = dh * g_ref[...]
            dx_ref[rows, :] = dxres_ref[rows, :] + r * (dn - nrm * jnp.mean(dn * nrm, axis=-1, keepdims=True))
            return jnp.sum(dh * nrm, axis=0, keepdims=True), jnp.sum(du, axis=0, keepdims=True), alr_ref[rows, :], du_b

        for j in range(tm // sub):
            dgin, dba, alr, du_b = one_tile(pl.ds(j * sub, sub))
            agin_ref[...] += dgin
            aba_ref[...] += dba
            awa_ref[...] += _dot_tn(alr, du_b)

        @pl.when(pl.program_id(0) == last)
        def _():
            small_ref[...] = jnp.zeros_like(small_ref)
            _put_rows(small_ref, SMALL_G_IN - base, agin_ref[...])
            _put_rows(small_ref, SMALL_B_ALPHA - base, aba_ref[...])
            for half in range(B_KEY_WIDTH // LANES):
                r0 = SMALL_W_ALPHA - base + half * B_GATE_RANK
                small_ref[r0:r0 + B_GATE_RANK, :] = awa_ref[0:B_GATE_RANK, half * LANES:(half + 1) * LANES]

    def rows(w):
        return pl.BlockSpec((tm, w), lambda i: (i, 0))

    names = ["dq", "dkv", "dza", "dqb", "dkb", "dvb", "dzb", "dla", "u", "alr", "dga", "dgb"]
    return pl.pallas_call(
        body, name="in_proj_bwd", grid=(T // tm,),
        in_specs=[rows(D_MODEL), rows(D_MODEL), rows(LANES), rows(LANES), _const_spec((1, D_MODEL)),
                  _const_spec((D_IN_PAD, D_MODEL)), _const_spec((RANK_PAD, B_KEY_WIDTH))]
                 + [rows(parts[n].shape[1]) for n in names],
        out_specs=[rows(D_MODEL), pl.BlockSpec((N_CHIPS, tm, 2 * SHARD_PAD), lambda i: (0, i, 0)),
                   pl.BlockSpec((SMALL_ROWS - base, LANES), lambda i: (0, 0))],
        out_shape=[jax.ShapeDtypeStruct((T, D_MODEL), F32), jax.ShapeDtypeStruct((N_CHIPS, T, 2 * SHARD_PAD), BF16),
                   jax.ShapeDtypeStruct((SMALL_ROWS - base, LANES), F32)],
        scratch_shapes=[pltpu.VMEM((tm, D_IN_PAD), BF16), pltpu.VMEM((1, D_MODEL), F32), pltpu.VMEM((1, B_KEY_WIDTH), F32),
                        pltpu.VMEM((RANK_PAD, B_KEY_WIDTH), F32)],
        compiler_params=pltpu.CompilerParams(dimension_semantics=("arbitrary",), vmem_limit_bytes=V7X_VMEM_LIMIT_MAX),
    )(x2, dxres, cosf, sinf, g_in, wt_pad, wa_pad, *[parts[n] for n in names])


FLIPS = [(dx, dy, dc) for dx in (0, 1) for dy in (0, 1) for dc in (0, 1)][1:]


def _my_place():
    return lax.axis_index("x"), lax.axis_index("y"), lax.axis_index("c")


def _any_specs(n):
    return [pl.BlockSpec(memory_space=pl.ANY)] * n


def _gather_first(shards, pos_col):
    n = len(shards)
    T = pos_col.shape[0]
    rows_per_pass = math.gcd(T, 512)
    invf, sign = _rope_lane_constants()

    def body(*refs):
        ins, (pos_ref, invf_ref, sign_ref) = refs[:n], refs[n:n + 3]
        outs, (cos_ref, sin_ref) = refs[n + 3:2 * n + 3], refs[2 * n + 3:2 * n + 5]
        send_sems, recv_sems, local_sems = refs[2 * n + 5:]
        x, y, c = _my_place()
        me, sibling = (x, y, c), (x, y, 1 - c)
        chips = [(1 - x, y), (x, 1 - y), (1 - x, 1 - y)]

        def block(a, px, py, pc):
            return outs[a].at[4 * px + 2 * py + pc]

        def copy(a, k, blk, to, src=None):
            return pltpu.make_async_remote_copy(
                src_ref=block(a, *blk) if src is None else src, dst_ref=block(a, *blk),
                send_sem=send_sems.at[a, k], recv_sem=recv_sems.at[a, k], device_id=to, device_id_type=MESH)

        mine = [pltpu.make_async_copy(ins[a], block(a, *me), local_sems.at[a]) for a in range(n)]
        for cp in mine:
            cp.start()
        first = []
        for a in range(n):
            first.append(copy(a, 0, me, sibling, src=ins[a]))
            first += [copy(a, 1 + j, me, (*chip, c), src=ins[a]) for j, chip in enumerate(chips)]
        for cp in first:
            cp.start()

        def tables(i, carry):
            rows = pl.ds(pl.multiple_of(i * rows_per_pass, rows_per_pass), rows_per_pass)
            ang = pos_ref[rows, :].astype(F32) * invf_ref[...]
            cos_ref[rows, :] = jnp.cos(ang)
            sin_ref[rows, :] = jnp.sin(ang) * sign_ref[...]
            return carry

        lax.fori_loop(0, T // rows_per_pass, tables, 0)

        passed = []
        for j, chip in enumerate(chips):
            for a in range(n):
                copy(a, 1 + j, (*chip, c), me).wait_recv()
                fwd = copy(a, 4 + j, (*chip, c), sibling)
                fwd.start()
                passed.append(fwd)
        for a in range(n):
            copy(a, 0, sibling, me).wait_recv()
            for j, chip in enumerate(chips):
                copy(a, 4 + j, (*chip, 1 - c), me).wait_recv()
        for cp in first + passed:
            cp.wait_send()
        for cp in mine:
            cp.wait()

    vmem = pl.BlockSpec(memory_space=pltpu.VMEM)
    res = pl.pallas_call(
        body, name="gather_weights",
        in_specs=_any_specs(n) + [vmem] * 3, out_specs=_any_specs(n) + [vmem] * 2,
        out_shape=[jax.ShapeDtypeStruct((N_DEV, *s.shape), s.dtype) for s in shards]
                  + [jax.ShapeDtypeStruct((T, LANES), F32)] * 2,
        scratch_shapes=[pltpu.SemaphoreType.DMA((n, 7)), pltpu.SemaphoreType.DMA((n, 7)), pltpu.SemaphoreType.DMA((n,))],
        compiler_params=pltpu.CompilerParams(vmem_limit_bytes=V7X_VMEM_LIMIT),
    )(*shards, pos_col, invf, sign)
    return res[:n], res[n], res[n + 1]


def _w_in_grad_rs(h, dsh, chip_order, small):
    T = h.shape[0]
    tk = math.gcd(T, 2048)
    nk = T // tk
    chip_flips = [(1, 1), (1, 0), (0, 1)]
    n_steps = len(chip_flips) + 1
    SIB = len(chip_flips)
    k_finish = min(1, nk - 1)

    def body(order_ref, h_ref, d_ref, s_ref, own_ref, recv_ref, sall_ref,
             acc_ref, keep_ref, pre_ref, to_sib_ref, to_chip_ref,
             sib_send, sib_recv, chip_send, chip_recv, ssend_sems, srecv_sems, local_sem):
        i, kk = pl.program_id(0), pl.program_id(1)
        x, y, c = _my_place()
        my_dev = 4 * x + 2 * y + c

        def small_copy(r, slot):
            dx, dy, dc = FLIPS[r]
            return pltpu.make_async_remote_copy(
                src_ref=s_ref, dst_ref=sall_ref.at[slot], send_sem=ssend_sems.at[r], recv_sem=srecv_sems.at[r],
                device_id=(x ^ dx, y ^ dy, c ^ dc), device_id_type=MESH)

        keep_small = pltpu.make_async_copy(s_ref, sall_ref.at[my_dev], local_sem)

        def sib_copy(t):
            dst = recv_ref.at[SIB] if t == SIB else pre_ref.at[t]
            return pltpu.make_async_remote_copy(
                src_ref=to_sib_ref.at[t], dst_ref=dst, send_sem=sib_send.at[t], recv_sem=sib_recv.at[t],
                device_id=(x, y, 1 - c), device_id_type=MESH)

        def chip_copy(t):
            dx, dy = chip_flips[t]
            return pltpu.make_async_remote_copy(
                src_ref=to_chip_ref.at[t], dst_ref=recv_ref.at[t], send_sem=chip_send.at[t], recv_sem=chip_recv.at[t],
                device_id=(x ^ dx, y ^ dy, c), device_id_type=MESH)

        def halves():
            first, second = acc_ref[0:SHARD_PAD, :], acc_ref[SHARD_PAD:2 * SHARD_PAD, :]
            return jnp.where(c == 0, first, second), jnp.where(c == 0, second, first)

        @pl.when((i == 0) & (kk == 0))
        def _():
            keep_small.start()
            for r in range(len(FLIPS)):
                small_copy(r, my_dev).start()

        @pl.when(kk == 0)
        def _():
            acc_ref[...] = jnp.zeros_like(acc_ref)

        acc_ref[...] += _dot_tn(d_ref[...], h_ref[...])

        for t in range(len(chip_flips)):
            @pl.when((i == t + 1) & (kk == k_finish))
            def _(t=t):
                sib_copy(t).wait_recv()
                to_chip_ref[t] = (keep_ref[...] + pre_ref[t].astype(F32)).astype(BF16)
                chip_copy(t).start()

        for t in range(len(chip_flips)):
            @pl.when((i == t) & (kk == nk - 1))
            def _(t=t):
                mine, theirs = halves()
                to_sib_ref[t] = theirs.astype(BF16)
                sib_copy(t).start()
                keep_ref[...] = mine

        @pl.when((i == n_steps - 1) & (kk == nk - 1))
        def _():
            mine, theirs = halves()
            own_ref[...] = mine
            to_sib_ref[SIB] = theirs.astype(BF16)
            sib_copy(SIB).start()
            for t in range(len(chip_flips)):
                sib_copy(t).wait_send()
                chip_copy(t).wait_send()
                chip_copy(t).wait_recv()
            sib_copy(SIB).wait_send()
            sib_copy(SIB).wait_recv()
            for r, (dx, dy, dc) in enumerate(FLIPS):
                small_copy(r, 4 * (x ^ dx) + 2 * (y ^ dy) + (c ^ dc)).wait_recv()
                small_copy(r, my_dev).wait_send()
            keep_small.wait()

    shard = (SHARD_PAD, D_MODEL)
    return pl.pallas_call(
        body, name="w_in_grad_rs",
        grid_spec=pltpu.PrefetchScalarGridSpec(
            num_scalar_prefetch=1, grid=(n_steps, nk),
            in_specs=[pl.BlockSpec((tk, D_MODEL), lambda i, kk, order: (kk, 0)),
                      pl.BlockSpec((None, tk, 2 * SHARD_PAD), lambda i, kk, order: (order[i], kk, 0)),
                      pl.BlockSpec(memory_space=pl.ANY)],
            out_specs=[pl.BlockSpec(shard, lambda i, kk, order: (0, 0)),
                       pl.BlockSpec(memory_space=pl.ANY), pl.BlockSpec(memory_space=pl.ANY)],
            scratch_shapes=[pltpu.VMEM((2 * SHARD_PAD, D_MODEL), F32), pltpu.VMEM(shard, F32),
                            pltpu.VMEM((SIB, *shard), BF16), pltpu.VMEM((SIB + 1, *shard), BF16),
                            pltpu.VMEM((SIB, *shard), BF16),
                            pltpu.SemaphoreType.DMA((SIB + 1,)), pltpu.SemaphoreType.DMA((SIB + 1,)),
                            pltpu.SemaphoreType.DMA((SIB,)), pltpu.SemaphoreType.DMA((SIB,)),
                            pltpu.SemaphoreType.DMA((7,)), pltpu.SemaphoreType.DMA((7,)), pltpu.SemaphoreType.DMA]),
        out_shape=[jax.ShapeDtypeStruct(shard, F32),
                   jax.ShapeDtypeStruct((SIB + 1, *shard), BF16),
                   jax.ShapeDtypeStruct((N_DEV, *small.shape), F32)],
        compiler_params=_params("arbitrary", "arbitrary"),
    )(chip_order, h, dsh, small)


def _adam_math(w, g, m, v):
    m_new = ADAM_B1 * m + (1.0 - ADAM_B1) * g
    v_new = ADAM_B2 * v + (1.0 - ADAM_B2) * (g * g)
    m_hat = m_new / (1.0 - ADAM_B1 ** ADAM_STEP)
    v_hat = v_new / (1.0 - ADAM_B2 ** ADAM_STEP)
    delta = -ADAM_LR * (m_hat / (jnp.sqrt(v_hat) + ADAM_EPS) + ADAM_WD * w)
    return delta, m_new, v_new


def _adam_big(jobs):
    steps = 8
    n = len(jobs)
    idx = jnp.stack([job[1] for job in jobs]).astype(jnp.int32)
    blocks = []
    for own, _, recv, w, m, v in jobs:
        (rw, cw), rp = w.shape, own.shape[1]
        by_cols = rp != rw
        blk_w = (rw, cw // steps) if by_cols else (rw // steps, cw)
        blk_g = (rp, cw // steps) if by_cols else (rw // steps, cw)
        blocks.append((blk_w, blk_g, by_cols))

    def body(idx_ref, *refs):
        ins, outs = refs[:5 * n], refs[5 * n:]
        for j, (blk_w, _, _) in enumerate(blocks):
            o_ref, r_ref, w_ref, m_ref, v_ref = ins[5 * j:5 * j + 5]
            g_ref, d_ref, mo_ref, vo_ref = outs[4 * j:4 * j + 4]
            g = o_ref[...].astype(F32)
            for r in range(r_ref.shape[0]):
                g = g + r_ref[r].astype(F32)
            g = g[0:blk_w[0], :]
            g_ref[...] = g
            d_ref[...], mo_ref[...], vo_ref[...] = _adam_math(w_ref[...], g, m_ref[...], v_ref[...])

    in_specs, out_specs, out_shape, args = [], [], [], []
    for j, ((own, _, recv, w, m, v), (blk_w, blk_g, by_cols)) in enumerate(zip(jobs, blocks)):
        at = (lambda i: (0, i)) if by_cols else (lambda i: (i, 0))
        spec = pl.BlockSpec(blk_w, lambda i, idx_ref, at=at: at(i))
        in_specs += [pl.BlockSpec((None, *blk_g), lambda i, idx_ref, at=at, j=j: (idx_ref[j], *at(i))),
                     pl.BlockSpec((recv.shape[0], *blk_g), lambda i, idx_ref, at=at: (0, *at(i))), spec, spec, spec]
        out_specs += [spec] * 4
        out_shape += [jax.ShapeDtypeStruct(w.shape, F32)] * 4
        args += [own, recv, w, m, v]
    res = pl.pallas_call(
        body, name="adam_big",
        grid_spec=pltpu.PrefetchScalarGridSpec(num_scalar_prefetch=1, grid=(steps,), in_specs=in_specs, out_specs=out_specs),
        out_shape=out_shape,
        compiler_params=_params("parallel"),
    )(idx, *args)
    return [res[4 * j:4 * j + 4] for j in range(n)]


def _adam_small(small_all, params):
    flat = [a for triple in params for a in triple]
    n_par = len(params)

    def body(s_ref, *refs):
        ins, outs, loss_ref = refs[:3 * n_par], refs[3 * n_par:-1], refs[-1]
        g_slab = s_ref[0]
        for dev in range(1, N_DEV):
            g_slab = g_slab + s_ref[dev]
        loss_ref[...] = g_slab[SMALL_LOSS:SMALL_LOSS + 1, :]
        dev = 4 * lax.axis_index("x") + 2 * lax.axis_index("y") + lax.axis_index("c")
        alpha_full = jnp.concatenate([g_slab[SMALL_W_ALPHA + half * B_GATE_RANK:SMALL_W_ALPHA + (half + 1) * B_GATE_RANK]
                                      for half in range(B_KEY_WIDTH // LANES)], axis=1)
        alpha_mine = pltpu.roll(alpha_full, (B_KEY_WIDTH - dev * SHARD_ALPHA) % B_KEY_WIDTH, 1)[:, 0:SHARD_ALPHA]
        grads = [_take_rows(g_slab, SMALL_G_IN, D_MODEL // LANES), _take_rows(g_slab, SMALL_G_FINAL, D_MODEL // LANES),
                 _take_rows(g_slab, SMALL_G_GLA, B_WIDTH // LANES), _take_rows(g_slab, SMALL_B_ALPHA, B_KEY_WIDTH // LANES),
                 g_slab[SMALL_SINKS:SMALL_SINKS + 1, 0:A_HEADS], alpha_mine]
        for i, g in enumerate(grads):
            w_ref, m_ref, v_ref = ins[3 * i:3 * i + 3]
            delta, m_new, v_new = _adam_math(w_ref[...], g, m_ref[...], v_ref[...])
            outs[4 * i][...] = g
            outs[4 * i + 1][...] = delta
            outs[4 * i + 2][...] = m_new
            outs[4 * i + 3][...] = v_new

    res = pl.pallas_call(
        body, name="adam_small",
        out_shape=[jax.ShapeDtypeStruct(t[0].shape, F32) for t in params for _ in range(4)]
                  + [jax.ShapeDtypeStruct((1, LANES), F32)],
    )(small_all, *flat)
    return [res[4 * i:4 * i + 4] for i in range(n_par)], res[-1]


def _local_step(x, cosf, sinf, loss_target, g_in, wt_sh, wa_pad, b_alpha, sinks, g_gla, out_shards, g_final, chip_order):
    B, S, _ = x.shape
    T = B * S
    x2 = x.reshape(T, D_MODEL)
    tgt2 = loss_target.reshape(T, D_MODEL)
    f, (g_woa, g_wob, g_wo) = _in_proj(x2, cosf, sinf, g_in, wt_sh, wa_pad, b_alpha, out_shards)
    w_o = g_wo.reshape(D_MODEL, D_MODEL)
    sink_row = jnp.repeat(sinks, WINDOW).reshape(1, ATT_ROWS)
    sink_col = sink_row.reshape(ATT_ROWS, 1)
    attn, lse = _attn_fwd(f["qkv"], sink_row, B, S)
    o_gla, st_all = _gla_fwd(f["q"], f["k"], f["cum"], f["vb"], B, S)
    (dxres, dattn, dog, dza, dzb, dga, dgb, dw_o, dw_oa, dw_ob, small_a) = _merge(
        x2, tgt2, attn, f["za"], o_gla, f["zb"], f["ga"], f["gb"], g_woa, g_wob, w_o, g_gla, g_final)
    dq, dkv, dsink = _attn_bwd(f["qkv"], dattn, attn, lse, sink_col, B, S)
    (dqb, dkb, dvb, dla), (rv_o, rv_oa, rv_ob) = _gla_bwd(f["q"], f["k"], f["cum"], f["vb"], dog, st_all, B, S,
                                                        [dw_o, dw_oa, dw_ob])
    parts = dict(dq=dq, dkv=dkv, dza=dza, dqb=dqb, dkb=dkb, dvb=dvb, dzb=dzb, dla=dla, u=f["u"], alr=f["alr"],
                 dga=dga, dgb=dgb)
    dx, dsh, small_c = _in_proj_bwd(x2, dxres, cosf, sinf, g_in, f["wt_pad"], wa_pad, parts)
    small = jnp.concatenate([small_a, dsink, small_c], axis=0)
    own_in, rv_in, small_all = _w_in_grad_rs(f["h"], dsh, chip_order, small)
    return dict(grad_x=dx.reshape(B, S, D_MODEL), own_in=own_in, rv_in=rv_in,
                own_o=dw_o, rv_o=rv_o, own_oa=dw_oa, rv_oa=rv_oa, own_ob=dw_ob, rv_ob=rv_ob, small_all=small_all)


def kernel(x, positions, g_in, w_in, w_alpha_up, b_alpha, attn_sinks, g_gla_norm, w_out_a, w_out_b, w_o, g_final, loss_target, m_g_in, m_w_in, m_w_alpha_up, m_b_alpha, m_attn_sinks, m_g_gla_norm, m_w_out_a, m_w_out_b, m_w_o, m_g_final, v_g_in, v_w_in, v_w_alpha_up, v_b_alpha, v_attn_sinks, v_g_gla_norm, v_w_out_a, v_w_out_b, v_w_o, v_g_final):
    xi, yi, ci = _my_place()
    chip = 2 * xi + yi
    chip_order = jnp.stack([chip ^ 3, chip ^ 2, chip ^ 1, chip]).astype(jnp.int32)

    (g_win, g_wa), cosf, sinf = _gather_first(
        [jnp.pad(w_in[0].T.astype(BF16), ((0, SHARD_PAD - SHARD_IN), (0, 0))), w_alpha_up[0].astype(BF16)],
        positions.reshape(-1, 1))
    wt_sh = g_win.reshape(N_DEV * SHARD_PAD, D_MODEL)
    wa_pad = jnp.pad(jnp.concatenate([g_wa[j] for j in range(N_DEV)], axis=1), ((0, RANK_PAD - B_GATE_RANK), (0, 0)))

    r = _local_step(x, cosf, sinf, loss_target, g_in, wt_sh, wa_pad, b_alpha, attn_sinks[0], g_gla_norm,
                    [w_out_a[0].astype(BF16), w_out_b[0].astype(BF16), w_o[0].astype(BF16)],
                    g_final.reshape(1, D_MODEL), chip_order)

    dev = 4 * xi + 2 * yi + ci
    big = _adam_big([(r["own_in"][None], jnp.int32(0), r["rv_in"], w_in[0].T, m_w_in[0].T, v_w_in[0].T),
                     (r["own_oa"], dev, r["rv_oa"], w_out_a[0], m_w_out_a[0], v_w_out_a[0]),
                     (r["own_ob"], dev, r["rv_ob"], w_out_b[0], m_w_out_b[0], v_w_out_b[0]),
                     (r["own_o"], dev, r["rv_o"], w_o[0], m_w_o[0], v_w_o[0])])
    big[0] = [a.T for a in big[0]]
    row = lambda a: a.reshape(1, D_MODEL)
    (s_in, s_final, s_gla, s_ba, s_sinks, s_wa), loss_row = _adam_small(r["small_all"], [
        (g_in, m_g_in, v_g_in), (row(g_final), row(m_g_final), row(v_g_final)),
        (g_gla_norm, m_g_gla_norm, v_g_gla_norm), (b_alpha, m_b_alpha, v_b_alpha),
        (attn_sinks, m_attn_sinks, v_attn_sinks), (w_alpha_up[0], m_w_alpha_up[0], v_w_alpha_up[0])])

    def group(i):
        return (s_in[i], big[0][i][None], s_wa[i][None], s_ba[i], s_sinks[i], s_gla[i], big[1][i][None], big[2][i][None],
                big[3][i][None], s_final[i].reshape(D_MODEL))

    return (loss_row[0, 0], r["grad_x"], *group(0), *group(1), *group(2), *group(3))
```

```python
import functools
import math

import numpy as np
import jax
import jax.numpy as jnp
from jax import lax
from jax.experimental import pallas as pl
from jax.experimental.pallas import tpu as pltpu

F32 = jnp.float32
BF16 = jnp.bfloat16
MESH = pl.DeviceIdType.MESH

D_MODEL = 1024
A_HEADS, A_KV_HEADS, A_HEAD_DIM = 8, 2, 64
A_WIDTH, A_KV_WIDTH = 512, 128
WINDOW = 128
ROPE_THETA = 500000.0
ROPE_DIM = 16
B_HEADS, B_KEY_DIM, B_VAL_DIM = 4, 64, 128
B_KEY_WIDTH, B_WIDTH = 256, 512
B_GATE_RANK = 16
B_GATE_TEMP = 16.0
B_CHUNK = 64
NORM_EPS = 1e-6
NEG_BIG = -1e30
D_IN = 4880
N_DEV = 8
N_CHIPS = 4
ADAM_LR, ADAM_B1, ADAM_B2, ADAM_EPS, ADAM_WD, ADAM_STEP = 0.001, 0.9, 0.999, 1e-08, 0.01, 10

LANES = 128
V7X_VMEM_LIMIT = 56 * 1024 * 1024
V7X_VMEM_LIMIT_MAX = 62 * 1024 * 1024

RANK_PAD = LANES
SEG = {}
_off = 0
for _name, _w in (("qa", 512), ("ka", 128), ("va", 128), ("za", 512), ("qb", 256), ("kb", 256),
                  ("vb", 512), ("zb", 512), ("alr", RANK_PAD), ("ga", 1024), ("gb", 1024)):
    SEG[_name] = (_off, _off + _w)
    _off += _w
D_IN_PAD = _off
ALR_SRC = SEG["alr"][0]
QKV_K, QKV_V, QKV_W = SEG["ka"][0], SEG["va"][0], SEG["va"][1]
ATT_SCALE = A_HEAD_DIM ** -0.5

SHARD_IN = D_IN // N_DEV
SHARD_PAD = 640
SHARD_OUT = D_MODEL // N_DEV
SHARD_ALPHA = B_KEY_WIDTH // N_DEV

SMALL_G_FINAL, SMALL_G_GLA, SMALL_LOSS, SMALL_SINKS, SMALL_G_IN, SMALL_B_ALPHA, SMALL_W_ALPHA = 0, 8, 12, 16, 24, 32, 40
SMALL_ROWS = 72


def _dot(a, b):
    return jnp.dot(a, b, preferred_element_type=F32)


def _dot_nt(a, b):
    return lax.dot_general(a, b, (((1,), (1,)), ((), ())), preferred_element_type=F32)


def _dot_tn(a, b):
    return lax.dot_general(a, b, (((0,), (0,)), ((), ())), preferred_element_type=F32)


def _sigmoid(z):
    return 1.0 / (1.0 + jnp.exp(-z))


def _sigmoid_tanh(z):
    return 0.5 * jnp.tanh(0.5 * z) + 0.5


def _params(*sem):
    return pltpu.CompilerParams(dimension_semantics=sem, vmem_limit_bytes=V7X_VMEM_LIMIT)


def _const_spec(shape):
    nd = len(shape)
    return pl.BlockSpec(shape, lambda *_: (0,) * nd, pipeline_mode=pl.Buffered(1))


def _lane_iota(shape):
    return lax.broadcasted_iota(jnp.int32, shape, 1)


def _row_iota(shape):
    return lax.broadcasted_iota(jnp.int32, shape, 0)


def _split3(v):
    hi = v.astype(BF16)
    r1 = v - hi.astype(F32)
    mid = r1.astype(BF16)
    lo = (r1 - mid.astype(F32)).astype(BF16)
    return hi, mid, lo


def _put_rows(ref, row0, vec):
    for r in range(vec.shape[1] // LANES):
        ref[row0 + r:row0 + r + 1, :] = vec[:, r * LANES:(r + 1) * LANES]


def _take_rows(slab, row0, n):
    return jnp.concatenate([slab[row0 + r:row0 + r + 1, :] for r in range(n)], axis=1)


def _rope_lane_constants():
    half = ROPE_DIM // 2
    inv_freq = np.exp(-math.log(ROPE_THETA) * np.arange(half, dtype=np.float32) * np.float32(2.0 / ROPE_DIM)).astype(np.float32)
    lane = np.arange(LANES)
    j = lane % A_HEAD_DIM
    invf = np.where(j < ROPE_DIM, inv_freq[j % half], 0.0).astype(np.float32)
    sign = np.where(j < half, -1.0, np.where(j < ROPE_DIM, 1.0, 0.0)).astype(np.float32)
    return jnp.asarray(invf)[None, :], jnp.asarray(sign)[None, :]


def _rope_slab(t, cos, sin_signed):
    first = (_lane_iota(t.shape) % A_HEAD_DIM) < (ROPE_DIM // 2)
    partner = jnp.where(first, pltpu.roll(t, LANES - ROPE_DIM // 2, 1), pltpu.roll(t, ROPE_DIM // 2, 1))
    return t * cos + partner * sin_signed


def _shard_pad_cols(j):
    cut = ALR_SRC + B_GATE_RANK
    shift = RANK_PAD - B_GATE_RANK
    a, b = j * SHARD_IN, (j + 1) * SHARD_IN
    if b <= cut:
        return [(a, b)]
    if a >= cut:
        return [(a + shift, b + shift)]
    return [(a, cut), (cut + shift, b + shift)]


def _in_proj(x2, cosf, sinf, g_in, wt_sh, wa_pad, b_alpha, later_shards):
    T = x2.shape[0]
    tm = math.gcd(T, 512)
    sub = math.gcd(tm, 256)
    last = T // tm - 1
    nl = len(later_shards)

    def body(x_ref, cos_ref, sin_ref, g_ref, wsh_ref, wa_ref, ba_ref, *rest):
        sh_refs, rest = rest[:nl], rest[nl:]
        (h_ref, qkv_ref, za_ref, q_ref, k_ref, vb_ref, zb_ref, alr_ref, u_ref, cum_ref, ga_ref, gb_ref, wt_out) = rest[:13]
        all_refs, (wt_ref, send_sems, recv_sems, local_sems, wt_sem) = rest[13:13 + nl], rest[13 + nl:]
        wt_copy = pltpu.make_async_copy(wt_ref, wt_out, wt_sem)
        px, py, pc = _my_place()
        my_dev = 4 * px + 2 * py + pc

        def wcopy(a, r, slot):
            dx, dy, dc = FLIPS[r]
            return pltpu.make_async_remote_copy(
                src_ref=sh_refs[a], dst_ref=all_refs[a].at[slot], send_sem=send_sems.at[a, r],
                recv_sem=recv_sems.at[a, r], device_id=(px ^ dx, py ^ dy, pc ^ dc), device_id_type=MESH)

        keep = [pltpu.make_async_copy(sh_refs[a], all_refs[a].at[my_dev], local_sems.at[a]) for a in range(nl)]

        @pl.when(pl.program_id(0) == 0)
        def _():
            for a in range(nl):
                keep[a].start()
                for r in range(len(FLIPS)):
                    wcopy(a, r, my_dev).start()

        @pl.when(pl.program_id(0) == 0)
        def _():
            for j in range(N_DEV):
                src = j * SHARD_PAD
                for a, b in _shard_pad_cols(j):
                    wt_ref[a:b, :] = wsh_ref[src:src + b - a, :]
                    src += b - a
            a, b = SEG["alr"]
            wt_ref[a + B_GATE_RANK:b, :] = jnp.zeros((RANK_PAD - B_GATE_RANK, D_MODEL), BF16)
            wt_copy.start()

        def one_tile(rows):
            x = x_ref[rows, :]
            r = lax.rsqrt(jnp.mean(x * x, axis=-1, keepdims=True) + NORM_EPS)
            h = (x * r * g_ref[...]).astype(BF16)
            h_ref[rows, :] = h

            def seg(name):
                a, b = SEG[name]
                return _dot_nt(h, wt_ref[a:b, :])

            alr = seg("alr").astype(BF16)
            alr_ref[rows, :] = alr
            u = _dot(alr, wa_ref[...]) + ba_ref[...]
            u_ref[rows, :] = u
            log_a = (jnp.minimum(u, 0.0) - jnp.log(1.0 + jnp.exp(-jnp.abs(u)))) * (1.0 / B_GATE_TEMP)
            row, col = _row_iota((sub, sub)), _lane_iota((sub, sub))
            tri = ((row // B_CHUNK == col // B_CHUNK) & (col <= row)).astype(BF16)
            hi, mid, lo = _split3(log_a)
            cum_ref[rows, :] = _dot(tri, hi) + _dot(tri, mid) + _dot(tri, lo)

            cos, sin = cos_ref[rows, :], sin_ref[rows, :]
            qa = seg("qa") * ATT_SCALE
            for s in range(A_WIDTH // LANES):
                qkv_ref[rows, s * LANES:(s + 1) * LANES] = _rope_slab(qa[:, s * LANES:(s + 1) * LANES], cos, sin).astype(BF16)
            qkv_ref[rows, QKV_K:QKV_V] = _rope_slab(seg("ka"), cos, sin).astype(BF16)
            qkv_ref[rows, QKV_V:QKV_W] = seg("va").astype(BF16)
            za_ref[rows, :] = seg("za").astype(BF16)
            q_ref[rows, :] = seg("qb")
            k_ref[rows, :] = seg("kb")
            vb_ref[rows, :] = seg("vb").astype(BF16)
            zb_ref[rows, :] = seg("zb").astype(BF16)
            ga_ref[rows, :] = seg("ga").astype(BF16)
            gb_ref[rows, :] = seg("gb").astype(BF16)

        for j in range(tm // sub):
            one_tile(pl.ds(j * sub, sub))

        @pl.when(pl.program_id(0) == last)
        def _():
            for a in range(nl):
                for r, (dx, dy, dc) in enumerate(FLIPS):
                    wcopy(a, r, 4 * (px ^ dx) + 2 * (py ^ dy) + (pc ^ dc)).wait_recv()
                    wcopy(a, r, my_dev).wait_send()
                keep[a].wait()
            wt_copy.wait()

    def rows(w):
        return pl.BlockSpec((tm, w), lambda i: (i, 0))

    outs = [("h", D_MODEL, BF16), ("qkv", QKV_W, BF16), ("za", A_WIDTH, BF16), ("q", B_KEY_WIDTH, F32),
            ("k", B_KEY_WIDTH, F32), ("vb", B_WIDTH, BF16), ("zb", B_WIDTH, BF16), ("alr", RANK_PAD, BF16),
            ("u", B_KEY_WIDTH, F32), ("cum", B_KEY_WIDTH, F32), ("ga", D_MODEL, BF16), ("gb", D_MODEL, BF16)]
    res = pl.pallas_call(
        body, name="in_proj", grid=(T // tm,),
        in_specs=[rows(D_MODEL), rows(LANES), rows(LANES), _const_spec((1, D_MODEL)),
                  _const_spec((N_DEV * SHARD_PAD, D_MODEL)), _const_spec((RANK_PAD, B_KEY_WIDTH)),
                  _const_spec((1, B_KEY_WIDTH))] + _any_specs(nl),
        out_specs=[rows(w) for _, w, _ in outs] + _any_specs(1 + nl),
        out_shape=[jax.ShapeDtypeStruct((T, w), dt) for _, w, dt in outs]
                  + [jax.ShapeDtypeStruct((D_IN_PAD, D_MODEL), BF16)]
                  + [jax.ShapeDtypeStruct((N_DEV, *sh.shape), sh.dtype) for sh in later_shards],
        scratch_shapes=[pltpu.VMEM((D_IN_PAD, D_MODEL), BF16),
                        pltpu.SemaphoreType.DMA((nl, len(FLIPS))), pltpu.SemaphoreType.DMA((nl, len(FLIPS))),
                        pltpu.SemaphoreType.DMA((nl,)), pltpu.SemaphoreType.DMA],
        compiler_params=_params("arbitrary"),
    )(x2, cosf, sinf, g_in, wt_sh, wa_pad, b_alpha, *later_shards)
    n_out = len(outs) + 1
    return dict(zip([n for n, _, _ in outs] + ["wt_pad"], res[:n_out])), res[n_out:]


def _dup_kv_head(t, g):
    tf = t.astype(F32)
    keep = (_lane_iota(tf.shape) < A_HEAD_DIM) == (g == 0)
    return jnp.where(keep, tf, pltpu.roll(tf, A_HEAD_DIM, 1)).astype(BF16)


def _stack_heads(t):
    lo = _lane_iota(t.shape) < A_HEAD_DIM
    zero = jnp.zeros_like(t)
    return jnp.concatenate([jnp.where(lo, t, zero), jnp.where(lo, zero, t)], axis=0)


ATT_ROWS = A_HEADS * WINDOW
GROUP_ROWS = ATT_ROWS // A_KV_HEADS
HEADS_PER_GROUP = A_HEADS // A_KV_HEADS


def _band_mask_t(n):
    kj = _row_iota((2 * WINDOW, GROUP_ROWS)) - WINDOW
    qi = _lane_iota((2 * WINDOW, GROUP_ROWS)) % WINDOW
    return (kj <= qi) & (qi - kj < WINDOW) & ((n > 0) | (kj >= 0))


def _stacked_queries(ref, g):
    pairs = range(g * HEADS_PER_GROUP // 2, (g + 1) * HEADS_PER_GROUP // 2)
    return jnp.concatenate([_stack_heads(ref[:, p * LANES:(p + 1) * LANES]) for p in pairs], axis=0)


def _unstack_heads(t, g, ref, dtype):
    lo = _lane_iota((WINDOW, LANES)) < A_HEAD_DIM
    for hh in range(HEADS_PER_GROUP // 2):
        p = g * HEADS_PER_GROUP // 2 + hh
        ref[:, p * LANES:(p + 1) * LANES] = jnp.where(lo, t[2 * hh * WINDOW:(2 * hh + 1) * WINDOW],
                                                       t[(2 * hh + 1) * WINDOW:(2 * hh + 2) * WINDOW]).astype(dtype)


FWD_BLOCKS = 16


def _attn_fwd(qkv, sink_row, B, S):
    T = B * S
    nb = S // WINDOW
    blocks = math.gcd(nb, FWD_BLOCKS)
    steps = nb // blocks

    def one_block(has_prev, sink_ref, q, k, v, o_ref, lse_ref):
        valid = _band_mask_t(has_prev)
        lse_rows = []
        for g in range(A_KV_HEADS):
            kd, vd = _dup_kv_head(k, g), _dup_kv_head(v, g)
            s = jnp.where(valid, _dot_nt(kd, _stacked_queries(q, g)), NEG_BIG)
            sink = sink_ref[:, g * GROUP_ROWS:(g + 1) * GROUP_ROWS]
            m = jnp.maximum(jnp.max(s, axis=0, keepdims=True), sink)
            e = jnp.exp(s - m)
            den = jnp.sum(e, axis=0, keepdims=True) + jnp.exp(sink - m)
            o = _dot_tn((e * (1.0 / den)).astype(BF16), vd)
            _unstack_heads(o, g, o_ref, F32)
            lse = m + jnp.log(den)
            lse_rows += [lse[:, j * WINDOW:(j + 1) * WINDOW] for j in range(HEADS_PER_GROUP)]
        by_head = jnp.concatenate(lse_rows + [jnp.zeros((WINDOW - A_HEADS, WINDOW), F32)], axis=0)
        lse_ref[...] = by_head.T

    def body(sink_ref, q_ref, kc_ref, vc_ref, kp_ref, vp_ref, o_ref, lse_ref):
        k_all = jnp.concatenate([kp_ref[...], kc_ref[...]], axis=0)
        v_all = jnp.concatenate([vp_ref[...], vc_ref[...]], axis=0)
        for j in range(blocks):
            rows = pl.ds(j * WINDOW, WINDOW)
            keys = slice(j * WINDOW, (j + 2) * WINDOW)
            has_prev = pl.program_id(1) if j == 0 else 1
            one_block(has_prev, sink_ref, q_ref[rows, :], k_all[keys], v_all[keys], o_ref.at[rows], lse_ref.at[rows])

    def cur(col, w):
        return pl.BlockSpec((blocks * WINDOW, w), lambda b, n: (b * steps + n, col))

    def prev(col):
        return pl.BlockSpec((WINDOW, LANES), lambda b, n: (b * nb + jnp.maximum(blocks * n - 1, 0), col))

    kcol, vcol = QKV_K // LANES, QKV_V // LANES
    return pl.pallas_call(
        body, name="attn_fwd", grid=(B, steps),
        in_specs=[_const_spec((1, ATT_ROWS)), cur(0, A_WIDTH), cur(kcol, LANES), cur(vcol, LANES), prev(kcol), prev(vcol)],
        out_specs=[cur(0, A_WIDTH), cur(0, LANES)],
        out_shape=[jax.ShapeDtypeStruct((T, A_WIDTH), F32), jax.ShapeDtypeStruct((T, LANES), F32)],
        compiler_params=_params("parallel", "parallel"),
    )(sink_row, qkv, qkv, qkv, qkv, qkv)


ATT_CHUNK = 128


def _chunk_masks(n):
    masks = []
    for half in range(WINDOW // ATT_CHUNK):
        qi = _row_iota((ATT_CHUNK, 2 * WINDOW)) + half * ATT_CHUNK
        kj = _lane_iota((ATT_CHUNK, 2 * WINDOW)) - WINDOW
        masks.append((kj <= qi) & (qi - kj < WINDOW) & ((n > 0) | (kj >= 0)))
    return masks


def _all_stacked_queries(ref):
    return jnp.concatenate([_stacked_queries(ref, g) for g in range(A_KV_HEADS)], axis=0)


def _by_group(fn, lhs, rhs_per_group):
    return jnp.concatenate([fn(lhs[g * GROUP_ROWS:(g + 1) * GROUP_ROWS], rhs_per_group[g])
                            for g in range(A_KV_HEADS)], axis=0)


BWD_BLOCKS = 8


def _attn_bwd(qkv, do, out, lse, sink_col, B, S):
    T = B * S
    nb = S // WINDOW
    M = math.gcd(nb, BWD_BLOCKS)
    steps = nb // M
    n_chunks = ATT_ROWS // ATT_CHUNK
    halves = WINDOW // ATT_CHUNK

    def block(has_prev, sink_ref, q, do_b, out_b, lse_b, k, v, scratch, want_dq):
        s_ref, dp_ref, ds_ref, p_ref = scratch
        width = k.shape[0]
        masks = [mk[:, 0:width] for mk in _chunk_masks(has_prev)]
        kd = [_dup_kv_head(k, g) for g in range(A_KV_HEADS)]
        vd = [_dup_kv_head(v, g) for g in range(A_KV_HEADS)]
        qs, dos = _all_stacked_queries(q), _all_stacked_queries(do_b)
        s_ref[...] = _by_group(_dot_nt, qs, kd)
        dp_ref[...] = _by_group(_dot_nt, dos, vd)
        lane = _lane_iota((ATT_CHUNK, LANES))
        lo = lane < A_HEAD_DIM
        lane1 = _lane_iota((1, LANES))
        dsink_row = jnp.zeros((1, LANES), F32)
        for c in range(n_chunks):
            rows = slice(c * ATT_CHUNK, (c + 1) * ATT_CHUNK)
            head, half = divmod(c, halves)
            qrows = slice(half * ATT_CHUNK, (half + 1) * ATT_CHUNK)
            slab = slice((head // 2) * LANES, (head // 2 + 1) * LANES)
            lse_col = jnp.sum(jnp.where(lane == head, lse_b[qrows, :], 0.0), axis=-1, keepdims=True)
            prod = do_b[qrows, slab].astype(F32) * out_b[qrows, slab].astype(F32)
            mine = lo if head % 2 == 0 else jnp.logical_not(lo)
            delta = jnp.sum(jnp.where(mine, prod, 0.0), axis=-1, keepdims=True)
            prob = jnp.exp(jnp.where(masks[half], s_ref[rows, :], NEG_BIG) - lse_col)
            p_ref[rows, :] = prob.astype(BF16)
            ds_ref[rows, :] = (prob * (dp_ref[rows, :] - delta)).astype(BF16)
            w = -jnp.exp(sink_ref[rows, :] - lse_col) * delta
            dsink_row += jnp.where(lane1 == head, jnp.sum(w, axis=0, keepdims=True), 0.0)
        dq = _by_group(_dot, ds_ref[...], kd) * ATT_SCALE if want_dq else None
        groups = [slice(g * GROUP_ROWS, (g + 1) * GROUP_ROWS) for g in range(A_KV_HEADS)]
        dk = [_dot_tn(ds_ref[rows, :], qs[rows]) for rows in groups]
        dv = [_dot_tn(p_ref[rows, :], dos[rows]) for rows in groups]
        return dq, dk, dv, dsink_row

    def fold(per_group):
        lane = _lane_iota((WINDOW, LANES))
        out = jnp.zeros((WINDOW, LANES), F32)
        for g, acc in enumerate(per_group):
            out = jnp.where((lane < A_HEAD_DIM) == (g == 0), acc + pltpu.roll(acc, A_HEAD_DIM, 1), out)
        return out

    def body(sink_ref, q_ref, qn_ref, do_ref, don_ref, out_ref, outn_ref, lse_ref, lsen_ref, kc_ref, kp_ref, vc_ref, vp_ref,
             dq_ref, dkv_ref, dsink_ref, s_scr, dp_scr, ds_scr, p_scr, s_x, dp_x, ds_x, p_x):
        b, m = pl.program_id(0), pl.program_id(1)

        @pl.when((b == 0) & (m == 0))
        def _():
            dsink_ref[...] = jnp.zeros_like(dsink_ref)

        k_all = jnp.concatenate([kp_ref[...], kc_ref[...]], axis=0)
        v_all = jnp.concatenate([vp_ref[...], vc_ref[...]], axis=0)
        results = []
        for j in range(M):
            rows = slice(j * WINDOW, (j + 1) * WINDOW)
            keys = slice(j * WINDOW, (j + 2) * WINDOW)
            has_prev = m if j == 0 else 1
            results.append(block(has_prev, sink_ref, q_ref[rows, :], do_ref[rows, :], out_ref[rows, :], lse_ref[rows, :],
                                 k_all[keys], v_all[keys], (s_scr.at[j], dp_scr.at[j], ds_scr.at[j], p_scr.at[j]), True))
        last_keys = slice(M * WINDOW, (M + 1) * WINDOW)
        _, dk_x, dv_x, _ = block(1, sink_ref, qn_ref[...], don_ref[...], outn_ref[...], lsen_ref[...],
                                 k_all[last_keys], v_all[last_keys], (s_x, dp_x, ds_x, p_x), False)
        has_next = m < steps - 1
        lo_q = _lane_iota((WINDOW, LANES)) < A_HEAD_DIM
        dsink_row = jnp.zeros((1, LANES), F32)
        for j, (dq, dk, dv, ds_row) in enumerate(results):
            rows = slice(j * WINDOW, (j + 1) * WINDOW)
            for p in range(A_HEADS // 2):
                dq_ref[rows, p * LANES:(p + 1) * LANES] = jnp.where(
                    lo_q, dq[2 * p * WINDOW:(2 * p + 1) * WINDOW], dq[(2 * p + 1) * WINDOW:(2 * p + 2) * WINDOW]).astype(BF16)
            if j + 1 < M:
                dk_next = [t[0:WINDOW] for t in results[j + 1][1]]
                dv_next = [t[0:WINDOW] for t in results[j + 1][2]]
            else:
                dk_next = [jnp.where(has_next, t, 0.0) for t in dk_x]
                dv_next = [jnp.where(has_next, t, 0.0) for t in dv_x]
            dkv_ref[rows, 0:LANES] = fold([own[WINDOW:] + nxt for own, nxt in zip(dk, dk_next)]).astype(BF16)
            dkv_ref[rows, LANES:] = fold([own[WINDOW:] + nxt for own, nxt in zip(dv, dv_next)]).astype(BF16)
            dsink_row += ds_row
        dsink_ref[0:1, :] += dsink_row

    def cur(col, w):
        return pl.BlockSpec((M * WINDOW, w), lambda b, m: (b * steps + m, col))

    def nxt(col, w):
        return pl.BlockSpec((WINDOW, w), lambda b, m: (b * nb + jnp.minimum(M * (m + 1), nb - 1), col))

    def prev(col):
        return pl.BlockSpec((WINDOW, LANES), lambda b, m: (b * nb + jnp.maximum(M * m - 1, 0), col))

    kcol, vcol = QKV_K // LANES, QKV_V // LANES
    scores = (M, ATT_ROWS, 2 * WINDOW)
    extra = (ATT_ROWS, WINDOW)
    return pl.pallas_call(
        body, name="attn_bwd", grid=(B, steps),
        in_specs=[_const_spec((ATT_ROWS, 1)), cur(0, A_WIDTH), nxt(0, A_WIDTH), cur(0, A_WIDTH), nxt(0, A_WIDTH),
                  cur(0, A_WIDTH), nxt(0, A_WIDTH), cur(0, LANES), nxt(0, LANES),
                  cur(kcol, LANES), prev(kcol), cur(vcol, LANES), prev(vcol)],
        out_specs=[cur(0, A_WIDTH), cur(0, 2 * LANES), pl.BlockSpec((8, LANES), lambda b, m: (0, 0))],
        out_shape=[jax.ShapeDtypeStruct((T, A_WIDTH), BF16), jax.ShapeDtypeStruct((T, 2 * LANES), BF16),
                   jax.ShapeDtypeStruct((8, LANES), F32)],
        scratch_shapes=[pltpu.VMEM(scores, F32), pltpu.VMEM(scores, F32), pltpu.VMEM(scores, BF16), pltpu.VMEM(scores, BF16),
                        pltpu.VMEM(extra, F32), pltpu.VMEM(extra, F32), pltpu.VMEM(extra, BF16), pltpu.VMEM(extra, BF16)],
        compiler_params=_params("arbitrary", "arbitrary"),
    )(sink_col, qkv, qkv, do, do, out, out, lse, lse, qkv, qkv, qkv, qkv)


GLA_FWD_TILING = (64, 16)
GLA_BWD_TILING = (256, 4)


def _gla_factors(q_ref, k_ref, cum_ref):
    cpt = q_ref.shape[0] // B_CHUNK
    scale = B_KEY_DIM ** -0.5
    cum = cum_ref[...]
    shape = (B_CHUNK, B_KEY_WIDTH)
    last = jnp.concatenate([jnp.broadcast_to(cum_ref[pl.ds(c * B_CHUNK + B_CHUNK - 1, 1), :], shape)
                            for c in range(cpt)], axis=0)
    mid = jnp.concatenate([jnp.broadcast_to(cum_ref[pl.ds(c * B_CHUNK + B_CHUNK // 2 - 1, 1), :], shape)
                           for c in range(cpt)], axis=0)
    e_qm, e_km, e_qe, e_kd = jnp.exp(cum - mid), jnp.exp(mid - cum), jnp.exp(cum), jnp.exp(last - cum)
    qs = q_ref[...] * scale
    k = k_ref[...]
    return qs, k, (e_qm, e_km, e_qe, e_kd)


def _head_mask(shape, h):
    return (_lane_iota(shape) // B_KEY_DIM) == h


def _stack_masked(t):
    return jnp.concatenate([jnp.where(_head_mask(t.shape, h), t, 0.0) for h in range(B_HEADS)], axis=0).astype(BF16)


def _select_heads(t):
    shape = (B_CHUNK, B_KEY_WIDTH)
    out = jnp.zeros(shape, F32)
    for h in range(B_HEADS):
        out = jnp.where(_head_mask(shape, h), t[h * B_CHUNK:(h + 1) * B_CHUNK], out)
    return out


def _select_state(t):
    shape = (B_VAL_DIM, B_KEY_WIDTH)
    out = jnp.zeros(shape, F32)
    for h in range(B_HEADS):
        out = jnp.where(_head_mask(shape, h), t[h * B_VAL_DIM:(h + 1) * B_VAL_DIM], out)
    return out


def _rows_by_head(t):
    return jnp.concatenate([t[:, h * B_VAL_DIM:(h + 1) * B_VAL_DIM] for h in range(B_HEADS)], axis=0)


def _intra_mask(tile_rows):
    i, j = _row_iota((tile_rows, tile_rows)), _lane_iota((tile_rows, tile_rows))
    return (i // B_CHUNK == j // B_CHUNK) & (j <= i)


def _pair_stack(t, p):
    slab = t[:, p * LANES:(p + 1) * LANES]
    lo = _lane_iota(slab.shape) < B_KEY_DIM
    return jnp.concatenate([jnp.where(lo, slab, 0.0), jnp.where(lo, 0.0, slab)], axis=0).astype(BF16)


def _gla_fwd(q, k, cum, vb, B, S):
    T = B * S
    tile_rows = math.gcd(S, GLA_FWD_TILING[0])
    cpt = tile_rows // B_CHUNK
    nt = S // tile_rows
    tps = math.gcd(nt, GLA_FWD_TILING[1])

    def one_sequence(q_ref, k_ref, cum_ref, v_ref, o_ref, st_all_ref, st_ref):
        qs, kk, (e_qm, e_km, e_qe, e_kd) = _gla_factors(q_ref, k_ref, cum_ref)
        qm, km, qe, kd = qs * e_qm, kk * e_km, qs * e_qe, (kk * e_kd).astype(BF16)
        mask = _intra_mask(tile_rows)
        intra = []
        for p in range(B_HEADS // 2):
            a = _dot_nt(_pair_stack(qm, p), km[:, p * LANES:(p + 1) * LANES].astype(BF16))
            for hh in range(2):
                h = 2 * p + hh
                att = jnp.where(mask, a[hh * tile_rows:(hh + 1) * tile_rows], 0.0).astype(BF16)
                intra.append(_dot(att, v_ref[:, h * B_VAL_DIM:(h + 1) * B_VAL_DIM]))
        inter = []
        for c in range(cpt):
            rows = slice(c * B_CHUNK, (c + 1) * B_CHUNK)
            st = st_ref[...]
            st_all_ref[c] = st
            inter.append(_dot_nt(_stack_masked(qe[rows]), st.astype(BF16)))
            inc = _select_state(_dot_tn(v_ref[rows, :], kd[rows]))
            decay = jnp.exp(cum_ref[pl.ds(c * B_CHUNK + B_CHUNK - 1, 1), :])
            st_ref[...] = st * decay + inc
        for h in range(B_HEADS):
            oi = jnp.concatenate([inter[c][h * B_CHUNK:(h + 1) * B_CHUNK] for c in range(cpt)], axis=0)
            o_ref[:, h * B_VAL_DIM:(h + 1) * B_VAL_DIM] = (intra[h] + oi).astype(BF16)

    def body(q_ref, k_ref, cum_ref, v_ref, o_ref, st_all_ref, st_ref):
        @pl.when(pl.program_id(0) == 0)
        def _():
            st_ref[...] = jnp.zeros_like(st_ref)

        for b in range(B):
            for tile in range(tps):
                tok = pl.ds(tile * tile_rows, tile_rows)
                chunks = pl.ds(tile * cpt, cpt)
                one_sequence(*[r.at[b, tok] for r in (q_ref, k_ref, cum_ref, v_ref, o_ref)],
                             st_all_ref.at[b, chunks], st_ref.at[b])

    def rows(w):
        return pl.BlockSpec((B, tps * tile_rows, w), lambda t: (0, t, 0))

    seq = lambda a: a.reshape(B, S, a.shape[-1])
    o, st_all = pl.pallas_call(
        body, name="gla_fwd", grid=(nt // tps,),
        in_specs=[rows(B_KEY_WIDTH), rows(B_KEY_WIDTH), rows(B_KEY_WIDTH), rows(B_WIDTH)],
        out_specs=[rows(B_WIDTH),
                   pl.BlockSpec((B, tps * cpt, B_VAL_DIM, B_KEY_WIDTH), lambda t: (0, t, 0, 0))],
        out_shape=[jax.ShapeDtypeStruct((B, S, B_WIDTH), BF16),
                   jax.ShapeDtypeStruct((B, S // B_CHUNK, B_VAL_DIM, B_KEY_WIDTH), F32)],
        scratch_shapes=[pltpu.VMEM((B, B_VAL_DIM, B_KEY_WIDTH), F32)],
        compiler_params=_params("arbitrary"),
    )(seq(q), seq(k), seq(cum), seq(vb))
    return o.reshape(T, B_WIDTH), st_all.reshape(T // B_CHUNK, B_VAL_DIM, B_KEY_WIDTH)


def _gla_bwd(q, k, cum, vb, do, st_all, B, S):
    T = B * S
    tile_rows = math.gcd(S, GLA_BWD_TILING[0])
    cpt = tile_rows // B_CHUNK
    nt = S // tile_rows
    tps = math.gcd(nt, GLA_BWD_TILING[1])
    steps = nt // tps
    scale = B_KEY_DIM ** -0.5

    def one_sequence(q_ref, k_ref, cum_ref, v_ref, do_ref, st_all_ref, dq_ref, dk_ref, dv_ref, dla_ref, dst_ref):
        qs, kk, (e_qm, e_km, e_qe, e_kd) = _gla_factors(q_ref, k_ref, cum_ref)
        qm, km, qe, kd = qs * e_qm, kk * e_km, qs * e_qe, kk * e_kd
        mask = _intra_mask(tile_rows)
        dqm_slabs, dkm_slabs, dv_intra = [], [], []
        for p in range(B_HEADS // 2):
            qm_st = _pair_stack(qm, p)
            km_p = km[:, p * LANES:(p + 1) * LANES].astype(BF16)
            a = _dot_nt(qm_st, km_p)
            da_blocks, dqm_h = [], []
            for hh in range(2):
                h = 2 * p + hh
                vs = slice(h * B_VAL_DIM, (h + 1) * B_VAL_DIM)
                att = jnp.where(mask, a[hh * tile_rows:(hh + 1) * tile_rows], 0.0).astype(BF16)
                dv_intra.append(_dot_tn(att, do_ref[:, vs]))
                da = jnp.where(mask, _dot_nt(do_ref[:, vs], v_ref[:, vs]), 0.0).astype(BF16)
                da_blocks.append(da)
                dqm_h.append(_dot(da, km_p))
            lo = _lane_iota((tile_rows, LANES)) < B_KEY_DIM
            dqm_slabs.append(jnp.where(lo, dqm_h[0], dqm_h[1]))
            dkm_slabs.append(_dot_tn(jnp.concatenate(da_blocks, axis=0), qm_st))
        dqm = jnp.concatenate(dqm_slabs, axis=1)
        dkm = jnp.concatenate(dkm_slabs, axis=1)

        dqe_c, dkd_c, dv_inter, tail_c = ([None] * cpt for _ in range(4))
        for c in reversed(range(cpt)):
            rows = slice(c * B_CHUNK, (c + 1) * B_CHUNK)
            dst = dst_ref[...]
            dst_b = dst.astype(BF16)
            dv_inter[c] = _dot_nt(_stack_masked(kd[rows]), dst_b)
            dkd_c[c] = _select_heads(_dot(_rows_by_head(v_ref[rows, :]), dst_b))
            do_c = do_ref[rows, :]
            dqe_c[c] = _select_heads(_dot(_rows_by_head(do_c), st_all_ref[c].astype(BF16)))
            contrib = _select_state(_dot_tn(do_c, qe[rows].astype(BF16)))
            decay = jnp.exp(cum_ref[pl.ds(c * B_CHUNK + B_CHUNK - 1, 1), :])
            tail = (jnp.sum(kk[rows] * dkd_c[c] * e_kd[rows], axis=0, keepdims=True)
                    + decay * jnp.sum(st_all_ref[c] * dst, axis=0, keepdims=True))
            tail_c[c] = jnp.broadcast_to(tail, (B_CHUNK, B_KEY_WIDTH))
            dst_ref[...] = dst * decay + contrib
        dqe = jnp.concatenate(dqe_c, axis=0)
        dkd = jnp.concatenate(dkd_c, axis=0)
        dqs = dqm * e_qm + dqe * e_qe
        dk = dkm * e_km + dkd * e_kd
        dq_ref[...] = (dqs * scale).astype(BF16)
        dk_ref[...] = dk.astype(BF16)
        for h in range(B_HEADS):
            dvi = jnp.concatenate([dv_inter[c][h * B_CHUNK:(h + 1) * B_CHUNK] for c in range(cpt)], axis=0)
            dv_ref[:, h * B_VAL_DIM:(h + 1) * B_VAL_DIM] = (dv_intra[h] + dvi).astype(BF16)
        dd = qs * dqs - kk * dk
        i, j = _row_iota((tile_rows, tile_rows)), _lane_iota((tile_rows, tile_rows))
        upper = ((i // B_CHUNK == j // B_CHUNK) & (j >= i)).astype(BF16)
        hi, mid, lo3 = _split3(dd)
        dla_ref[...] = _dot(upper, hi) + _dot(upper, mid) + _dot(upper, lo3) + jnp.concatenate(tail_c, axis=0)

    def body(q_ref, k_ref, cum_ref, v_ref, do_ref, st_all_ref, dq_ref, dk_ref, dv_ref, dla_ref, dst_ref):
        @pl.when(pl.program_id(0) == 0)
        def _():
            dst_ref[...] = jnp.zeros_like(dst_ref)

        for b in range(B):
            for tile in reversed(range(tps)):
                tok = pl.ds(tile * tile_rows, tile_rows)
                chunks = pl.ds(tile * cpt, cpt)
                one_sequence(*[r.at[b, tok] for r in (q_ref, k_ref, cum_ref, v_ref, do_ref)], st_all_ref.at[b, chunks],
                             *[r.at[b, tok] for r in (dq_ref, dk_ref, dv_ref, dla_ref)], dst_ref.at[b])

    def rows(w):
        return pl.BlockSpec((B, tps * tile_rows, w), lambda t: (0, steps - 1 - t, 0))

    seq = lambda a: a.reshape(B, S, a.shape[-1])
    res = pl.pallas_call(
        body, name="gla_bwd", grid=(steps,),
        in_specs=[rows(B_KEY_WIDTH), rows(B_KEY_WIDTH), rows(B_KEY_WIDTH), rows(B_WIDTH), rows(B_WIDTH),
                  pl.BlockSpec((B, tps * cpt, B_VAL_DIM, B_KEY_WIDTH), lambda t: (0, steps - 1 - t, 0, 0))],
        out_specs=[rows(B_KEY_WIDTH), rows(B_KEY_WIDTH), rows(B_WIDTH), rows(B_KEY_WIDTH)],
        out_shape=[jax.ShapeDtypeStruct((B, S, B_KEY_WIDTH), BF16), jax.ShapeDtypeStruct((B, S, B_KEY_WIDTH), BF16),
                   jax.ShapeDtypeStruct((B, S, B_WIDTH), BF16), jax.ShapeDtypeStruct((B, S, B_KEY_WIDTH), F32)],
        scratch_shapes=[pltpu.VMEM((B, B_VAL_DIM, B_KEY_WIDTH), F32)],
        compiler_params=_params("arbitrary"),
    )(seq(q), seq(k), seq(cum), seq(vb), seq(do), st_all.reshape(B, S // B_CHUNK, B_VAL_DIM, B_KEY_WIDTH))
    return [a.reshape(T, a.shape[-1]) for a in res]


def _merge(x2, tgt2, attn, za, o_gla, zb, ga, gb, w_oa_sh, w_ob_sh, w_o, g_gla, g_final):
    T = x2.shape[0]
    tm = math.gcd(T, 512)
    sub = math.gcd(tm, 256)
    last = T // tm - 1

    def body(x_ref, tgt_ref, attn_ref, za_ref, og_ref, zb_ref, ga_ref, gb_ref,
             woa_sh_ref, wob_sh_ref, wo_ref, gg_ref, gf_ref,
             dxres_ref, dattn_ref, dog_ref, dza_ref, dzb_ref, dga_ref, dgb_ref,
             dwo_out, dwoa_out, dwob_out, small_ref,
             awo_ref, awoa_ref, awob_ref, agf_ref, agg_ref, loss_ref, woa_ref, wob_ref,
             dwo_ref, dwoa_ref, dwob_ref, w_sems, dw_sems):
        w_copies = [pltpu.make_async_copy(sh.at[j], dst.at[:, j * SHARD_OUT:(j + 1) * SHARD_OUT], w_sems.at[a, j])
                    for a, (sh, dst) in enumerate(((woa_sh_ref, woa_ref), (wob_sh_ref, wob_ref))) for j in range(N_DEV)]
        dw_copies = [pltpu.make_async_copy(src, dst, dw_sems.at[a])
                     for a, (src, dst) in enumerate(((dwo_ref, dwo_out), (dwoa_ref, dwoa_out), (dwob_ref, dwob_out)))]

        @pl.when(pl.program_id(0) == 0)
        def _():
            for cp in w_copies:
                cp.start()
            for r in (awo_ref, awoa_ref, awob_ref, agf_ref, agg_ref, loss_ref):
                r[...] = jnp.zeros_like(r)
            for cp in w_copies:
                cp.wait()

        def one_tile(rows):
            za_v = za_ref[rows, :].astype(F32)
            sig_za = _sigmoid_tanh(za_v)
            silu_a = za_v * sig_za
            attn_v = attn_ref[rows, :].astype(F32)
            oa = (attn_v * silu_a).astype(BF16)
            ya = _dot(oa, woa_ref[...])
            og = og_ref[rows, :].astype(F32)
            zb_v = zb_ref[rows, :].astype(F32)
            sig_zb = _sigmoid_tanh(zb_v)
            silu_b = zb_v * sig_zb
            gg = gg_ref[...]
            on_parts, rinv_parts = [], []
            for h in range(B_HEADS):
                seg = og[:, h * B_VAL_DIM:(h + 1) * B_VAL_DIM]
                rinv = lax.rsqrt(jnp.mean(seg * seg, axis=-1, keepdims=True) + NORM_EPS)
                rinv_parts.append(rinv)
                on_parts.append(seg * rinv)
            on = jnp.concatenate(on_parts, axis=1)
            obn = on * gg
            ob = (obn * silu_b).astype(BF16)
            yb = _dot(ob, wob_ref[...])
            sig_a = _sigmoid_tanh(ga_ref[rows, :].astype(F32))
            sig_b = _sigmoid_tanh(gb_ref[rows, :].astype(F32))
            merged = (sig_a * ya + sig_b * yb).astype(BF16)
            out = x_ref[rows, :] + _dot(merged, wo_ref[...])
            rf = lax.rsqrt(jnp.mean(out * out, axis=-1, keepdims=True) + NORM_EPS)
            nrm = out * rf
            gf = gf_ref[...]
            err = nrm * gf - tgt_ref[rows, :]
            loss = jnp.sum(err * err) * (0.5 / D_MODEL)

            dy = err * (1.0 / D_MODEL)
            dgf = jnp.sum(dy * nrm, axis=0, keepdims=True)
            dn = dy * gf
            dout = rf * (dn - nrm * jnp.mean(dn * nrm, axis=-1, keepdims=True))
            dxres_ref[rows, :] = dout
            dout_b = dout.astype(BF16)
            dmerged = _dot_nt(dout_b, wo_ref[...])
            dya = dmerged * sig_a
            dyb = dmerged * sig_b
            dga_ref[rows, :] = (dmerged * ya * sig_a * (1.0 - sig_a)).astype(BF16)
            dgb_ref[rows, :] = (dmerged * yb * sig_b * (1.0 - sig_b)).astype(BF16)
            dya_b, dyb_b = dya.astype(BF16), dyb.astype(BF16)
            doa = _dot_nt(dya_b, woa_ref[...])
            dattn_ref[rows, :] = (doa * silu_a).astype(BF16)
            dza_ref[rows, :] = (doa * attn_v * (sig_za * (1.0 + za_v * (1.0 - sig_za)))).astype(BF16)
            dob = _dot_nt(dyb_b, wob_ref[...])
            dzb_ref[rows, :] = (dob * obn * (sig_zb * (1.0 + zb_v * (1.0 - sig_zb)))).astype(BF16)
            dobn = dob * silu_b
            dgg = jnp.sum(dobn * on, axis=0, keepdims=True)
            don = dobn * gg
            for h in range(B_HEADS):
                sl = slice(h * B_VAL_DIM, (h + 1) * B_VAL_DIM)
                don_h, on_h = don[:, sl], on[:, sl]
                dog_ref[rows, sl] = (rinv_parts[h] * (don_h - on_h * jnp.mean(don_h * on_h, axis=-1, keepdims=True))
                                     ).astype(BF16)
            return (merged, dout_b, oa, dya_b, ob, dyb_b), (loss, dgf, dgg)

        tiles = [one_tile(pl.ds(j * sub, sub)) for j in range(tm // sub)]
        merged, dout_b, oa, dya_b, ob, dyb_b = (jnp.concatenate(parts, axis=0) for parts in zip(*[t[0] for t in tiles]))
        awo_ref[...] += _dot_tn(merged, dout_b)
        awoa_ref[...] += _dot_tn(oa, dya_b)
        awob_ref[...] += _dot_tn(ob, dyb_b)
        for _, (loss, dgf, dgg) in tiles:
            loss_ref[...] += loss
            agf_ref[...] += dgf
            agg_ref[...] += dgg

        @pl.when(pl.program_id(0) == last)
        def _():
            for j in range(N_DEV):
                dwo_ref[j] = awo_ref[j * SHARD_OUT:(j + 1) * SHARD_OUT, :].astype(BF16)
                dwoa_ref[j] = awoa_ref[:, j * SHARD_OUT:(j + 1) * SHARD_OUT].astype(BF16)
                dwob_ref[j] = awob_ref[:, j * SHARD_OUT:(j + 1) * SHARD_OUT].astype(BF16)
            small_ref[...] = jnp.zeros_like(small_ref)
            _put_rows(small_ref, SMALL_G_FINAL, agf_ref[...])
            _put_rows(small_ref, SMALL_G_GLA, agg_ref[...])
            small_ref[SMALL_LOSS:SMALL_LOSS + 1, :] = loss_ref[...]
            for cp in dw_copies:
                cp.start()
            for cp in dw_copies:
                cp.wait()

    def rows(w):
        return pl.BlockSpec((tm, w), lambda i: (i, 0))

    def whole(shape):
        nd = len(shape)
        return pl.BlockSpec(shape, lambda i: (0,) * nd)

    outs = [((T, D_MODEL), F32, rows(D_MODEL)), ((T, A_WIDTH), BF16, rows(A_WIDTH)), ((T, B_WIDTH), BF16, rows(B_WIDTH)),
            ((T, A_WIDTH), BF16, rows(A_WIDTH)), ((T, B_WIDTH), BF16, rows(B_WIDTH)),
            ((T, D_MODEL), BF16, rows(D_MODEL)), ((T, D_MODEL), BF16, rows(D_MODEL)),
            ((N_DEV, SHARD_OUT, D_MODEL), BF16, pl.BlockSpec(memory_space=pl.ANY)),
            ((N_DEV, A_WIDTH, SHARD_OUT), BF16, pl.BlockSpec(memory_space=pl.ANY)),
            ((N_DEV, B_WIDTH, SHARD_OUT), BF16, pl.BlockSpec(memory_space=pl.ANY)),
            ((SMALL_SINKS, LANES), F32, whole((SMALL_SINKS, LANES)))]
    return pl.pallas_call(
        body, name="merge", grid=(T // tm,),
        in_specs=[rows(D_MODEL), rows(D_MODEL), rows(A_WIDTH), rows(A_WIDTH), rows(B_WIDTH), rows(B_WIDTH),
                  rows(D_MODEL), rows(D_MODEL),
                  pl.BlockSpec(memory_space=pl.ANY), pl.BlockSpec(memory_space=pl.ANY),
                  _const_spec((D_MODEL, D_MODEL)), _const_spec((1, B_WIDTH)), _const_spec((1, D_MODEL))],
        out_specs=[o[2] for o in outs],
        out_shape=[jax.ShapeDtypeStruct(o[0], o[1]) for o in outs],
        scratch_shapes=[pltpu.VMEM((D_MODEL, D_MODEL), F32), pltpu.VMEM((A_WIDTH, D_MODEL), F32),
                        pltpu.VMEM((B_WIDTH, D_MODEL), F32), pltpu.VMEM((1, D_MODEL), F32), pltpu.VMEM((1, B_WIDTH), F32),
                        pltpu.VMEM((1, LANES), F32), pltpu.VMEM((A_WIDTH, D_MODEL), BF16),
                        pltpu.VMEM((B_WIDTH, D_MODEL), BF16),
                        pltpu.VMEM((N_DEV, SHARD_OUT, D_MODEL), BF16), pltpu.VMEM((N_DEV, A_WIDTH, SHARD_OUT), BF16),
                        pltpu.VMEM((N_DEV, B_WIDTH, SHARD_OUT), BF16),
                        pltpu.SemaphoreType.DMA((2, N_DEV)), pltpu.SemaphoreType.DMA((3,))],
        compiler_params=pltpu.CompilerParams(dimension_semantics=("arbitrary",), vmem_limit_bytes=V7X_VMEM_LIMIT_MAX),
    )(x2, tgt2, attn, za, o_gla, zb, ga, gb, w_oa_sh, w_ob_sh, w_o, g_gla, g_final)


def _in_proj_bwd(x2, dxres, cosf, sinf, g_in, wt_pad, wa_pad, parts):
    T = x2.shape[0]
    tm = math.gcd(T, 512)
    sub = math.gcd(tm, 256)
    last = T // tm - 1
    base = SMALL_G_IN

    def body(x_ref, dxres_ref, cos_ref, sin_ref, g_ref, wt_ref, wa_ref,
             dq_ref, dkv_ref, dza_ref, dqb_ref, dkb_ref, dvb_ref, dzb_ref, dla_ref, u_ref, alr_ref, dga_ref, dgb_ref,
             dx_ref, dsh_ref, small_ref, dproj_ref, agin_ref, aba_ref, awa_ref):
        @pl.when(pl.program_id(0) == 0)
        def _():
            for r in (agin_ref, aba_ref, awa_ref):
                r[...] = jnp.zeros_like(r)

        def one_tile(rows):
            cos, nsin = cos_ref[rows, :], -sin_ref[rows, :]
            for s in range(A_WIDTH // LANES):
                sl = slice(s * LANES, (s + 1) * LANES)
                dproj_ref[rows, sl] = _rope_slab(dq_ref[rows, sl].astype(F32), cos, nsin).astype(BF16)
            dproj_ref[rows, QKV_K:QKV_V] = _rope_slab(dkv_ref[rows, 0:LANES].astype(F32), cos, nsin).astype(BF16)
            dproj_ref[rows, QKV_V:QKV_W] = dkv_ref[rows, LANES:]

            def put(name, val):
                a, b = SEG[name]
                dproj_ref[rows, a:b] = val

            put("za", dza_ref[rows, :])
            put("qb", dqb_ref[rows, :])
            put("kb", dkb_ref[rows, :])
            put("vb", dvb_ref[rows, :])
            put("zb", dzb_ref[rows, :])
            put("ga", dga_ref[rows, :])
            put("gb", dgb_ref[rows, :])
            du = dla_ref[rows, :] * (1.0 / B_GATE_TEMP) * _sigmoid(-u_ref[rows, :])
            du_b = du.astype(BF16)
            put("alr", _dot_nt(du_b, wa_ref[...]).astype(BF16))

            for j in range(N_DEV):
                col = (j % 2) * SHARD_PAD
                for a, b in _shard_pad_cols(j):
                    dsh_ref[j // 2, rows, col:col + b - a] = dproj_ref[rows, a:b]
                    col += b - a
                dsh_ref[j // 2, rows, col:(j % 2 + 1) * SHARD_PAD] = jnp.zeros((sub, SHARD_PAD - SHARD_IN), BF16)

            dh = _dot(dproj_ref[rows, :], wt_ref[...])
            x = x_ref[rows, :]
            r = lax.rsqrt(jnp.mean(x * x, axis=-1, keepdims=True) + NORM_EPS)
            nrm = x * r
            dn = dh * g_ref[...]
            dx_ref[rows, :] = dxres_ref[rows, :] + r * (dn - nrm * jnp.mean(dn * nrm, axis=-1, keepdims=True))
            return jnp.sum(dh * nrm, axis=0, keepdims=True), jnp.sum(du, axis=0, keepdims=True), alr_ref[rows, :], du_b

        for j in range(tm // sub):
            dgin, dba, alr, du_b = one_tile(pl.ds(j * sub, sub))
            agin_ref[...] += dgin
            aba_ref[...] += dba
            awa_ref[...] += _dot_tn(alr, du_b)

        @pl.when(pl.program_id(0) == last)
        def _():
            small_ref[...] = jnp.zeros_like(small_ref)
            _put_rows(small_ref, SMALL_G_IN - base, agin_ref[...])
            _put_rows(small_ref, SMALL_B_ALPHA - base, aba_ref[...])
            for half in range(B_KEY_WIDTH // LANES):
                r0 = SMALL_W_ALPHA - base + half * B_GATE_RANK
                small_ref[r0:r0 + B_GATE_RANK, :] = awa_ref[0:B_GATE_RANK, half * LANES:(half + 1) * LANES]

    def rows(w):
        return pl.BlockSpec((tm, w), lambda i: (i, 0))

    names = ["dq", "dkv", "dza", "dqb", "dkb", "dvb", "dzb", "dla", "u", "alr", "dga", "dgb"]
    return pl.pallas_call(
        body, name="in_proj_bwd", grid=(T // tm,),
        in_specs=[rows(D_MODEL), rows(D_MODEL), rows(LANES), rows(LANES), _const_spec((1, D_MODEL)),
                  _const_spec((D_IN_PAD, D_MODEL)), _const_spec((RANK_PAD, B_KEY_WIDTH))]
                 + [rows(parts[n].shape[1]) for n in names],
        out_specs=[rows(D_MODEL), pl.BlockSpec((N_CHIPS, tm, 2 * SHARD_PAD), lambda i: (0, i, 0)),
                   pl.BlockSpec((SMALL_ROWS - base, LANES), lambda i: (0, 0))],
        out_shape=[jax.ShapeDtypeStruct((T, D_MODEL), F32), jax.ShapeDtypeStruct((N_CHIPS, T, 2 * SHARD_PAD), BF16),
                   jax.ShapeDtypeStruct((SMALL_ROWS - base, LANES), F32)],
        scratch_shapes=[pltpu.VMEM((tm, D_IN_PAD), BF16), pltpu.VMEM((1, D_MODEL), F32), pltpu.VMEM((1, B_KEY_WIDTH), F32),
                        pltpu.VMEM((RANK_PAD, B_KEY_WIDTH), F32)],
        compiler_params=pltpu.CompilerParams(dimension_semantics=("arbitrary",), vmem_limit_bytes=V7X_VMEM_LIMIT_MAX),
    )(x2, dxres, cosf, sinf, g_in, wt_pad, wa_pad, *[parts[n] for n in names])


FLIPS = [(dx, dy, dc) for dx in (0, 1) for dy in (0, 1) for dc in (0, 1)][1:]


def _my_place():
    return lax.axis_index("x"), lax.axis_index("y"), lax.axis_index("c")


def _any_specs(n):
    return [pl.BlockSpec(memory_space=pl.ANY)] * n


def _gather_first(shards, pos_col):
    n = len(shards)
    T = pos_col.shape[0]
    rows_per_pass = math.gcd(T, 512)
    invf, sign = _rope_lane_constants()

    def body(*refs):
        ins, (pos_ref, invf_ref, sign_ref) = refs[:n], refs[n:n + 3]
        outs, (cos_ref, sin_ref) = refs[n + 3:2 * n + 3], refs[2 * n + 3:2 * n + 5]
        send_sems, recv_sems, local_sems = refs[2 * n + 5:]
        x, y, c = _my_place()
        me, sibling = (x, y, c), (x, y, 1 - c)
        chips = [(1 - x, y), (x, 1 - y), (1 - x, 1 - y)]

        def block(a, px, py, pc):
            return outs[a].at[4 * px + 2 * py + pc]

        def copy(a, k, blk, to, src=None):
            return pltpu.make_async_remote_copy(
                src_ref=block(a, *blk) if src is None else src, dst_ref=block(a, *blk),
                send_sem=send_sems.at[a, k], recv_sem=recv_sems.at[a, k], device_id=to, device_id_type=MESH)

        mine = [pltpu.make_async_copy(ins[a], block(a, *me), local_sems.at[a]) for a in range(n)]
        for cp in mine:
            cp.start()
        first = []
        for a in range(n):
            first.append(copy(a, 0, me, sibling, src=ins[a]))
            first += [copy(a, 1 + j, me, (*chip, c), src=ins[a]) for j, chip in enumerate(chips)]
        for cp in first:
            cp.start()

        def tables(i, carry):
            rows = pl.ds(pl.multiple_of(i * rows_per_pass, rows_per_pass), rows_per_pass)
            ang = pos_ref[rows, :].astype(F32) * invf_ref[...]
            cos_ref[rows, :] = jnp.cos(ang)
            sin_ref[rows, :] = jnp.sin(ang) * sign_ref[...]
            return carry

        lax.fori_loop(0, T // rows_per_pass, tables, 0)

        passed = []
        for j, chip in enumerate(chips):
            for a in range(n):
                copy(a, 1 + j, (*chip, c), me).wait_recv()
                fwd = copy(a, 4 + j, (*chip, c), sibling)
                fwd.start()
                passed.append(fwd)
        for a in range(n):
            copy(a, 0, sibling, me).wait_recv()
            for j, chip in enumerate(chips):
                copy(a, 4 + j, (*chip, 1 - c), me).wait_recv()
        for cp in first + passed:
            cp.wait_send()
        for cp in mine:
            cp.wait()

    vmem = pl.BlockSpec(memory_space=pltpu.VMEM)
    res = pl.pallas_call(
        body, name="gather_weights",
        in_specs=_any_specs(n) + [vmem] * 3, out_specs=_any_specs(n) + [vmem] * 2,
        out_shape=[jax.ShapeDtypeStruct((N_DEV, *s.shape), s.dtype) for s in shards]
                  + [jax.ShapeDtypeStruct((T, LANES), F32)] * 2,
        scratch_shapes=[pltpu.SemaphoreType.DMA((n, 7)), pltpu.SemaphoreType.DMA((n, 7)), pltpu.SemaphoreType.DMA((n,))],
        compiler_params=pltpu.CompilerParams(vmem_limit_bytes=V7X_VMEM_LIMIT),
    )(*shards, pos_col, invf, sign)
    return res[:n], res[n], res[n + 1]


def _w_in_grad_rs(h, dsh, chip_order, small, wgrads):
    T = h.shape[0]
    tk = math.gcd(T, 2048)
    nk = T // tk
    chip_flips = [(1, 1), (1, 0), (0, 1)]
    n_steps = len(chip_flips) + 1
    SIB = len(chip_flips)
    nw = len(wgrads)
    k_finish = min(1, nk - 1)

    def body(order_ref, h_ref, d_ref, s_ref, *rest):
        g_refs, (own_ref, recv_ref, sall_ref), rv_refs = rest[:nw], rest[nw:nw + 3], rest[nw + 3:2 * nw + 3]
        (acc_ref, keep_ref, pre_ref, to_sib_ref, to_chip_ref, sib_send, sib_recv, chip_send, chip_recv,
         ssend_sems, srecv_sems, local_sem, wsend_sems, wrecv_sems) = rest[2 * nw + 3:]
        i, kk = pl.program_id(0), pl.program_id(1)
        x, y, c = _my_place()
        my_dev = 4 * x + 2 * y + c

        def wcopy(a, r):
            dx, dy, dc = FLIPS[r]
            return pltpu.make_async_remote_copy(
                src_ref=g_refs[a].at[4 * (x ^ dx) + 2 * (y ^ dy) + (c ^ dc)], dst_ref=rv_refs[a].at[r],
                send_sem=wsend_sems.at[a, r], recv_sem=wrecv_sems.at[a, r],
                device_id=(x ^ dx, y ^ dy, c ^ dc), device_id_type=MESH)

        def small_copy(r, slot):
            dx, dy, dc = FLIPS[r]
            return pltpu.make_async_remote_copy(
                src_ref=s_ref, dst_ref=sall_ref.at[slot], send_sem=ssend_sems.at[r], recv_sem=srecv_sems.at[r],
                device_id=(x ^ dx, y ^ dy, c ^ dc), device_id_type=MESH)

        keep_small = pltpu.make_async_copy(s_ref, sall_ref.at[my_dev], local_sem)

        def sib_copy(t):
            dst = recv_ref.at[SIB] if t == SIB else pre_ref.at[t]
            return pltpu.make_async_remote_copy(
                src_ref=to_sib_ref.at[t], dst_ref=dst, send_sem=sib_send.at[t], recv_sem=sib_recv.at[t],
                device_id=(x, y, 1 - c), device_id_type=MESH)

        def chip_copy(t):
            dx, dy = chip_flips[t]
            return pltpu.make_async_remote_copy(
                src_ref=to_chip_ref.at[t], dst_ref=recv_ref.at[t], send_sem=chip_send.at[t], recv_sem=chip_recv.at[t],
                device_id=(x ^ dx, y ^ dy, c), device_id_type=MESH)

        def halves():
            first, second = acc_ref[0:SHARD_PAD, :], acc_ref[SHARD_PAD:2 * SHARD_PAD, :]
            return jnp.where(c == 0, first, second), jnp.where(c == 0, second, first)

        @pl.when((i == 0) & (kk == 0))
        def _():
            keep_small.start()
            for r in range(len(FLIPS)):
                small_copy(r, my_dev).start()
                for a in range(nw):
                    wcopy(a, r).start()

        @pl.when(kk == 0)
        def _():
            acc_ref[...] = jnp.zeros_like(acc_ref)

        acc_ref[...] += _dot_tn(d_ref[...], h_ref[...])

        for t in range(len(chip_flips)):
            @pl.when((i == t + 1) & (kk == k_finish))
            def _(t=t):
                sib_copy(t).wait_recv()
                to_chip_ref[t] = (keep_ref[...] + pre_ref[t].astype(F32)).astype(BF16)
                chip_copy(t).start()

        for t in range(len(chip_flips)):
            @pl.when((i == t) & (kk == nk - 1))
            def _(t=t):
                mine, theirs = halves()
                to_sib_ref[t] = theirs.astype(BF16)
                sib_copy(t).start()
                keep_ref[...] = mine

        @pl.when((i == n_steps - 1) & (kk == nk - 1))
        def _():
            mine, theirs = halves()
            own_ref[...] = mine
            to_sib_ref[SIB] = theirs.astype(BF16)
            sib_copy(SIB).start()
            for t in range(len(chip_flips)):
                sib_copy(t).wait_send()
                chip_copy(t).wait_send()
                chip_copy(t).wait_recv()
            sib_copy(SIB).wait_send()
            sib_copy(SIB).wait_recv()
            for r, (dx, dy, dc) in enumerate(FLIPS):
                small_copy(r, 4 * (x ^ dx) + 2 * (y ^ dy) + (c ^ dc)).wait_recv()
                small_copy(r, my_dev).wait_send()
                for a in range(nw):
                    wcopy(a, r).wait()
            keep_small.wait()

    shard = (SHARD_PAD, D_MODEL)
    res = pl.pallas_call(
        body, name="w_in_grad_rs",
        grid_spec=pltpu.PrefetchScalarGridSpec(
            num_scalar_prefetch=1, grid=(n_steps, nk),
            in_specs=[pl.BlockSpec((tk, D_MODEL), lambda i, kk, order: (kk, 0)),
                      pl.BlockSpec((None, tk, 2 * SHARD_PAD), lambda i, kk, order: (order[i], kk, 0)),
                      pl.BlockSpec(memory_space=pl.ANY)] + _any_specs(nw),
            out_specs=[pl.BlockSpec(shard, lambda i, kk, order: (0, 0)),
                       pl.BlockSpec(memory_space=pl.ANY), pl.BlockSpec(memory_space=pl.ANY)] + _any_specs(nw),
            scratch_shapes=[pltpu.VMEM((2 * SHARD_PAD, D_MODEL), F32), pltpu.VMEM(shard, F32),
                            pltpu.VMEM((SIB, *shard), BF16), pltpu.VMEM((SIB + 1, *shard), BF16),
                            pltpu.VMEM((SIB, *shard), BF16),
                            pltpu.SemaphoreType.DMA((SIB + 1,)), pltpu.SemaphoreType.DMA((SIB + 1,)),
                            pltpu.SemaphoreType.DMA((SIB,)), pltpu.SemaphoreType.DMA((SIB,)),
                            pltpu.SemaphoreType.DMA((7,)), pltpu.SemaphoreType.DMA((7,)), pltpu.SemaphoreType.DMA,
                            pltpu.SemaphoreType.DMA((nw, len(FLIPS))), pltpu.SemaphoreType.DMA((nw, len(FLIPS)))]),
        out_shape=[jax.ShapeDtypeStruct(shard, F32),
                   jax.ShapeDtypeStruct((SIB + 1, *shard), BF16),
                   jax.ShapeDtypeStruct((N_DEV, *small.shape), F32)]
                  + [jax.ShapeDtypeStruct((len(FLIPS), *g.shape[1:]), g.dtype) for g in wgrads],
        compiler_params=_params("arbitrary", "arbitrary"),
    )(chip_order, h, dsh, small, *wgrads)
    return res[0], res[1], res[2], res[3:]


def _adam_math(w, g, m, v):
    m_new = ADAM_B1 * m + (1.0 - ADAM_B1) * g
    v_new = ADAM_B2 * v + (1.0 - ADAM_B2) * (g * g)
    m_hat = m_new / (1.0 - ADAM_B1 ** ADAM_STEP)
    v_hat = v_new / (1.0 - ADAM_B2 ** ADAM_STEP)
    delta = -ADAM_LR * (m_hat / (jnp.sqrt(v_hat) + ADAM_EPS) + ADAM_WD * w)
    return delta, m_new, v_new


def _adam_big(jobs):
    steps = 8
    n = len(jobs)
    idx = jnp.stack([job[1] for job in jobs]).astype(jnp.int32)
    blocks = []
    for own, _, recv, w, m, v in jobs:
        (rw, cw), rp = w.shape, own.shape[1]
        by_cols = rp != rw
        blk_w = (rw, cw // steps) if by_cols else (rw // steps, cw)
        blk_g = (rp, cw // steps) if by_cols else (rw // steps, cw)
        blocks.append((blk_w, blk_g, by_cols))

    def body(idx_ref, *refs):
        ins, outs = refs[:5 * n], refs[5 * n:]
        for j, (blk_w, _, _) in enumerate(blocks):
            o_ref, r_ref, w_ref, m_ref, v_ref = ins[5 * j:5 * j + 5]
            g_ref, d_ref, mo_ref, vo_ref = outs[4 * j:4 * j + 4]
            g = o_ref[...].astype(F32)
            for r in range(r_ref.shape[0]):
                g = g + r_ref[r].astype(F32)
            g = g[0:blk_w[0], :]
            g_ref[...] = g
            d_ref[...], mo_ref[...], vo_ref[...] = _adam_math(w_ref[...], g, m_ref[...], v_ref[...])

    in_specs, out_specs, out_shape, args = [], [], [], []
    for j, ((own, _, recv, w, m, v), (blk_w, blk_g, by_cols)) in enumerate(zip(jobs, blocks)):
        at = (lambda i: (0, i)) if by_cols else (lambda i: (i, 0))
        spec = pl.BlockSpec(blk_w, lambda i, idx_ref, at=at: at(i))
        in_specs += [pl.BlockSpec((None, *blk_g), lambda i, idx_ref, at=at, j=j: (idx_ref[j], *at(i))),
                     pl.BlockSpec((recv.shape[0], *blk_g), lambda i, idx_ref, at=at: (0, *at(i))), spec, spec, spec]
        out_specs += [spec] * 4
        out_shape += [jax.ShapeDtypeStruct(w.shape, F32)] * 4
        args += [own, recv, w, m, v]
    res = pl.pallas_call(
        body, name="adam_big",
        grid_spec=pltpu.PrefetchScalarGridSpec(num_scalar_prefetch=1, grid=(steps,), in_specs=in_specs, out_specs=out_specs),
        out_shape=out_shape,
        compiler_params=_params("parallel"),
    )(idx, *args)
    return [res[4 * j:4 * j + 4] for j in range(n)]


def _adam_small(small_all, params):
    flat = [a for triple in params for a in triple]
    n_par = len(params)

    def body(s_ref, *refs):
        ins, outs, loss_ref = refs[:3 * n_par], refs[3 * n_par:-1], refs[-1]
        g_slab = s_ref[0]
        for dev in range(1, N_DEV):
            g_slab = g_slab + s_ref[dev]
        loss_ref[...] = g_slab[SMALL_LOSS:SMALL_LOSS + 1, :]
        dev = 4 * lax.axis_index("x") + 2 * lax.axis_index("y") + lax.axis_index("c")
        alpha_full = jnp.concatenate([g_slab[SMALL_W_ALPHA + half * B_GATE_RANK:SMALL_W_ALPHA + (half + 1) * B_GATE_RANK]
                                      for half in range(B_KEY_WIDTH // LANES)], axis=1)
        alpha_mine = pltpu.roll(alpha_full, (B_KEY_WIDTH - dev * SHARD_ALPHA) % B_KEY_WIDTH, 1)[:, 0:SHARD_ALPHA]
        grads = [_take_rows(g_slab, SMALL_G_IN, D_MODEL // LANES), _take_rows(g_slab, SMALL_G_FINAL, D_MODEL // LANES),
                 _take_rows(g_slab, SMALL_G_GLA, B_WIDTH // LANES), _take_rows(g_slab, SMALL_B_ALPHA, B_KEY_WIDTH // LANES),
                 g_slab[SMALL_SINKS:SMALL_SINKS + 1, 0:A_HEADS], alpha_mine]
        for i, g in enumerate(grads):
            w_ref, m_ref, v_ref = ins[3 * i:3 * i + 3]
            delta, m_new, v_new = _adam_math(w_ref[...], g, m_ref[...], v_ref[...])
            outs[4 * i][...] = g
            outs[4 * i + 1][...] = delta
            outs[4 * i + 2][...] = m_new
            outs[4 * i + 3][...] = v_new

    res = pl.pallas_call(
        body, name="adam_small",
        out_shape=[jax.ShapeDtypeStruct(t[0].shape, F32) for t in params for _ in range(4)]
                  + [jax.ShapeDtypeStruct((1, LANES), F32)],
    )(small_all, *flat)
    return [res[4 * i:4 * i + 4] for i in range(n_par)], res[-1]


def _local_step(x, cosf, sinf, loss_target, g_in, wt_sh, wa_pad, b_alpha, sinks, g_gla, out_shards, g_final, chip_order):
    B, S, _ = x.shape
    T = B * S
    x2 = x.reshape(T, D_MODEL)
    tgt2 = loss_target.reshape(T, D_MODEL)
    f, (g_woa, g_wob, g_wo) = _in_proj(x2, cosf, sinf, g_in, wt_sh, wa_pad, b_alpha, out_shards)
    w_o = g_wo.reshape(D_MODEL, D_MODEL)
    sink_row = jnp.repeat(sinks, WINDOW).reshape(1, ATT_ROWS)
    sink_col = sink_row.reshape(ATT_ROWS, 1)
    attn, lse = _attn_fwd(f["qkv"], sink_row, B, S)
    o_gla, st_all = _gla_fwd(f["q"], f["k"], f["cum"], f["vb"], B, S)
    (dxres, dattn, dog, dza, dzb, dga, dgb, dw_o, dw_oa, dw_ob, small_a) = _merge(
        x2, tgt2, attn, f["za"], o_gla, f["zb"], f["ga"], f["gb"], g_woa, g_wob, w_o, g_gla, g_final)
    dq, dkv, dsink = _attn_bwd(f["qkv"], dattn, attn, lse, sink_col, B, S)
    dqb, dkb, dvb, dla = _gla_bwd(f["q"], f["k"], f["cum"], f["vb"], dog, st_all, B, S)
    parts = dict(dq=dq, dkv=dkv, dza=dza, dqb=dqb, dkb=dkb, dvb=dvb, dzb=dzb, dla=dla, u=f["u"], alr=f["alr"],
                 dga=dga, dgb=dgb)
    dx, dsh, small_c = _in_proj_bwd(x2, dxres, cosf, sinf, g_in, f["wt_pad"], wa_pad, parts)
    small = jnp.concatenate([small_a, dsink, small_c], axis=0)
    own_in, rv_in, small_all, (rv_o, rv_oa, rv_ob) = _w_in_grad_rs(f["h"], dsh, chip_order, small, [dw_o, dw_oa, dw_ob])
    return dict(grad_x=dx.reshape(B, S, D_MODEL), own_in=own_in, rv_in=rv_in,
                own_o=dw_o, rv_o=rv_o, own_oa=dw_oa, rv_oa=rv_oa, own_ob=dw_ob, rv_ob=rv_ob, small_all=small_all)


def kernel(x, positions, g_in, w_in, w_alpha_up, b_alpha, attn_sinks, g_gla_norm, w_out_a, w_out_b, w_o, g_final, loss_target, m_g_in, m_w_in, m_w_alpha_up, m_b_alpha, m_attn_sinks, m_g_gla_norm, m_w_out_a, m_w_out_b, m_w_o, m_g_final, v_g_in, v_w_in, v_w_alpha_up, v_b_alpha, v_attn_sinks, v_g_gla_norm, v_w_out_a, v_w_out_b, v_w_o, v_g_final):
    xi, yi, ci = _my_place()
    chip = 2 * xi + yi
    chip_order = jnp.stack([chip ^ 3, chip ^ 2, chip ^ 1, chip]).astype(jnp.int32)

    (g_win, g_wa), cosf, sinf = _gather_first(
        [jnp.pad(w_in[0].T.astype(BF16), ((0, SHARD_PAD - SHARD_IN), (0, 0))), w_alpha_up[0].astype(BF16)],
        positions.reshape(-1, 1))
    wt_sh = g_win.reshape(N_DEV * SHARD_PAD, D_MODEL)
    wa_pad = jnp.pad(jnp.concatenate([g_wa[j] for j in range(N_DEV)], axis=1), ((0, RANK_PAD - B_GATE_RANK), (0, 0)))

    r = _local_step(x, cosf, sinf, loss_target, g_in, wt_sh, wa_pad, b_alpha, attn_sinks[0], g_gla_norm,
                    [w_out_a[0].astype(BF16), w_out_b[0].astype(BF16), w_o[0].astype(BF16)],
                    g_final.reshape(1, D_MODEL), chip_order)

    dev = 4 * xi + 2 * yi + ci
    big = _adam_big([(r["own_in"][None], jnp.int32(0), r["rv_in"], w_in[0].T, m_w_in[0].T, v_w_in[0].T),
                     (r["own_oa"], dev, r["rv_oa"], w_out_a[0], m_w_out_a[0], v_w_out_a[0]),
                     (r["own_ob"], dev, r["rv_ob"], w_out_b[0], m_w_out_b[0], v_w_out_b[0]),
                     (r["own_o"], dev, r["rv_o"], w_o[0], m_w_o[0], v_w_o[0])])
    big[0] = [a.T for a in big[0]]
    row = lambda a: a.reshape(1, D_MODEL)
    (s_in, s_final, s_gla, s_ba, s_sinks, s_wa), loss_row = _adam_small(r["small_all"], [
        (g_in, m_g_in, v_g_in), (row(g_final), row(m_g_final), row(v_g_final)),
        (g_gla_norm, m_g_gla_norm, v_g_gla_norm), (b_alpha, m_b_alpha, v_b_alpha),
        (attn_sinks, m_attn_sinks, v_attn_sinks), (w_alpha_up[0], m_w_alpha_up[0], v_w_alpha_up[0])])

    def group(i):
        return (s_in[i], big[0][i][None], s_wa[i][None], s_ba[i], s_sinks[i], s_gla[i], big[1][i][None], big[2][i][None],
                big[3][i][None], s_final[i].reshape(D_MODEL))

    return (loss_row[0, 0], r["grad_x"], *group(0), *group(1), *group(2), *group(3))
```

```python
import functools
import math

import numpy as np
import jax
import jax.numpy as jnp
from jax import lax
from jax.experimental import pallas as pl
from jax.experimental.pallas import tpu as pltpu

F32 = jnp.float32
BF16 = jnp.bfloat16
MESH = pl.DeviceIdType.MESH

D_MODEL = 1024
A_HEADS, A_KV_HEADS, A_HEAD_DIM = 8, 2, 64
A_WIDTH, A_KV_WIDTH = 512, 128
WINDOW = 128
ROPE_THETA = 500000.0
ROPE_DIM = 16
B_HEADS, B_KEY_DIM, B_VAL_DIM = 4, 64, 128
B_KEY_WIDTH, B_WIDTH = 256, 512
B_GATE_RANK = 16
B_GATE_TEMP = 16.0
B_CHUNK = 64
NORM_EPS = 1e-6
NEG_BIG = -1e30
D_IN = 4880
N_DEV = 8
N_CHIPS = 4
ADAM_LR, ADAM_B1, ADAM_B2, ADAM_EPS, ADAM_WD, ADAM_STEP = 0.001, 0.9, 0.999, 1e-08, 0.01, 10

LANES = 128
V7X_VMEM_LIMIT = 56 * 1024 * 1024
V7X_VMEM_LIMIT_MAX = 62 * 1024 * 1024

RANK_PAD = LANES
SEG = {}
_off = 0
for _name, _w in (("qa", 512), ("ka", 128), ("va", 128), ("za", 512), ("qb", 256), ("kb", 256),
                  ("vb", 512), ("zb", 512), ("alr", RANK_PAD), ("ga", 1024), ("gb", 1024)):
    SEG[_name] = (_off, _off + _w)
    _off += _w
D_IN_PAD = _off
ALR_SRC = SEG["alr"][0]
QKV_K, QKV_V, QKV_W = SEG["ka"][0], SEG["va"][0], SEG["va"][1]
ATT_SCALE = A_HEAD_DIM ** -0.5

SHARD_IN = D_IN // N_DEV
SHARD_PAD = 640
SHARD_OUT = D_MODEL // N_DEV
SHARD_ALPHA = B_KEY_WIDTH // N_DEV

SMALL_G_FINAL, SMALL_G_GLA, SMALL_LOSS, SMALL_SINKS, SMALL_G_IN, SMALL_B_ALPHA, SMALL_W_ALPHA = 0, 8, 12, 16, 24, 32, 40
SMALL_ROWS = 72


def _dot(a, b):
    return jnp.dot(a, b, preferred_element_type=F32)


def _dot_nt(a, b):
    return lax.dot_general(a, b, (((1,), (1,)), ((), ())), preferred_element_type=F32)


def _dot_tn(a, b):
    return lax.dot_general(a, b, (((0,), (0,)), ((), ())), preferred_element_type=F32)


def _sigmoid(z):
    return 1.0 / (1.0 + jnp.exp(-z))


def _sigmoid_tanh(z):
    return 0.5 * jnp.tanh(0.5 * z) + 0.5


def _params(*sem):
    return pltpu.CompilerParams(dimension_semantics=sem, vmem_limit_bytes=V7X_VMEM_LIMIT)


def _const_spec(shape):
    nd = len(shape)
    return pl.BlockSpec(shape, lambda *_: (0,) * nd, pipeline_mode=pl.Buffered(1))


def _lane_iota(shape):
    return lax.broadcasted_iota(jnp.int32, shape, 1)


def _row_iota(shape):
    return lax.broadcasted_iota(jnp.int32, shape, 0)


def _split3(v):
    hi = v.astype(BF16)
    r1 = v - hi.astype(F32)
    mid = r1.astype(BF16)
    lo = (r1 - mid.astype(F32)).astype(BF16)
    return hi, mid, lo


def _put_rows(ref, row0, vec):
    for r in range(vec.shape[1] // LANES):
        ref[row0 + r:row0 + r + 1, :] = vec[:, r * LANES:(r + 1) * LANES]


def _take_rows(slab, row0, n):
    return jnp.concatenate([slab[row0 + r:row0 + r + 1, :] for r in range(n)], axis=1)


def _rope_lane_constants():
    half = ROPE_DIM // 2
    inv_freq = np.exp(-math.log(ROPE_THETA) * np.arange(half, dtype=np.float32) * np.float32(2.0 / ROPE_DIM)).astype(np.float32)
    lane = np.arange(LANES)
    j = lane % A_HEAD_DIM
    invf = np.where(j < ROPE_DIM, inv_freq[j % half], 0.0).astype(np.float32)
    sign = np.where(j < half, -1.0, np.where(j < ROPE_DIM, 1.0, 0.0)).astype(np.float32)
    return jnp.asarray(invf)[None, :], jnp.asarray(sign)[None, :]


def _rope_slab(t, cos, sin_signed):
    first = (_lane_iota(t.shape) % A_HEAD_DIM) < (ROPE_DIM // 2)
    partner = jnp.where(first, pltpu.roll(t, LANES - ROPE_DIM // 2, 1), pltpu.roll(t, ROPE_DIM // 2, 1))
    return t * cos + partner * sin_signed


def _shard_pad_cols(j):
    cut = ALR_SRC + B_GATE_RANK
    shift = RANK_PAD - B_GATE_RANK
    a, b = j * SHARD_IN, (j + 1) * SHARD_IN
    if b <= cut:
        return [(a, b)]
    if a >= cut:
        return [(a + shift, b + shift)]
    return [(a, cut), (cut + shift, b + shift)]


def _in_proj(x2, cosf, sinf, g_in, wt_sh, wa_pad, b_alpha, later_shards):
    T = x2.shape[0]
    tm = math.gcd(T, 512)
    sub = math.gcd(tm, 256)
    last = T // tm - 1
    nl = len(later_shards)

    def body(x_ref, cos_ref, sin_ref, g_ref, wsh_ref, wa_ref, ba_ref, *rest):
        sh_refs, rest = rest[:nl], rest[nl:]
        (h_ref, qkv_ref, za_ref, q_ref, k_ref, vb_ref, zb_ref, alr_ref, u_ref, cum_ref, ga_ref, gb_ref, wt_out) = rest[:13]
        all_refs, (wt_ref, send_sems, recv_sems, local_sems, wt_sem) = rest[13:13 + nl], rest[13 + nl:]
        wt_copy = pltpu.make_async_copy(wt_ref, wt_out, wt_sem)
        px, py, pc = _my_place()
        my_dev = 4 * px + 2 * py + pc

        def wcopy(a, r, slot):
            dx, dy, dc = FLIPS[r]
            return pltpu.make_async_remote_copy(
                src_ref=sh_refs[a], dst_ref=all_refs[a].at[slot], send_sem=send_sems.at[a, r],
                recv_sem=recv_sems.at[a, r], device_id=(px ^ dx, py ^ dy, pc ^ dc), device_id_type=MESH)

        keep = [pltpu.make_async_copy(sh_refs[a], all_refs[a].at[my_dev], local_sems.at[a]) for a in range(nl)]

        @pl.when(pl.program_id(0) == 0)
        def _():
            for a in range(nl):
                keep[a].start()
                for r in range(len(FLIPS)):
                    wcopy(a, r, my_dev).start()

        @pl.when(pl.program_id(0) == 0)
        def _():
            for j in range(N_DEV):
                src = j * SHARD_PAD
                for a, b in _shard_pad_cols(j):
                    wt_ref[a:b, :] = wsh_ref[src:src + b - a, :]
                    src += b - a
            a, b = SEG["alr"]
            wt_ref[a + B_GATE_RANK:b, :] = jnp.zeros((RANK_PAD - B_GATE_RANK, D_MODEL), BF16)
            wt_copy.start()

        def one_tile(rows):
            x = x_ref[rows, :]
            r = lax.rsqrt(jnp.mean(x * x, axis=-1, keepdims=True) + NORM_EPS)
            h = (x * r * g_ref[...]).astype(BF16)
            h_ref[rows, :] = h

            def seg(name):
                a, b = SEG[name]
                return _dot_nt(h, wt_ref[a:b, :])

            alr = seg("alr").astype(BF16)
            alr_ref[rows, :] = alr
            u = _dot(alr, wa_ref[...]) + ba_ref[...]
            u_ref[rows, :] = u
            log_a = (jnp.minimum(u, 0.0) - jnp.log(1.0 + jnp.exp(-jnp.abs(u)))) * (1.0 / B_GATE_TEMP)
            row, col = _row_iota((sub, sub)), _lane_iota((sub, sub))
            tri = ((row // B_CHUNK == col // B_CHUNK) & (col <= row)).astype(BF16)
            hi, mid, lo = _split3(log_a)
            cum_ref[rows, :] = _dot(tri, hi) + _dot(tri, mid) + _dot(tri, lo)

            cos, sin = cos_ref[rows, :], sin_ref[rows, :]
            qa = seg("qa") * ATT_SCALE
            for s in range(A_WIDTH // LANES):
                qkv_ref[rows, s * LANES:(s + 1) * LANES] = _rope_slab(qa[:, s * LANES:(s + 1) * LANES], cos, sin).astype(BF16)
            qkv_ref[rows, QKV_K:QKV_V] = _rope_slab(seg("ka"), cos, sin).astype(BF16)
            qkv_ref[rows, QKV_V:QKV_W] = seg("va").astype(BF16)
            za_ref[rows, :] = seg("za").astype(BF16)
            q_ref[rows, :] = seg("qb")
            k_ref[rows, :] = seg("kb")
            vb_ref[rows, :] = seg("vb").astype(BF16)
            zb_ref[rows, :] = seg("zb").astype(BF16)
            ga_ref[rows, :] = seg("ga").astype(BF16)
            gb_ref[rows, :] = seg("gb").astype(BF16)

        for j in range(tm // sub):
            one_tile(pl.ds(j * sub, sub))

        @pl.when(pl.program_id(0) == last)
        def _():
            for a in range(nl):
                for r, (dx, dy, dc) in enumerate(FLIPS):
                    wcopy(a, r, 4 * (px ^ dx) + 2 * (py ^ dy) + (pc ^ dc)).wait_recv()
                    wcopy(a, r, my_dev).wait_send()
                keep[a].wait()
            wt_copy.wait()

    def rows(w):
        return pl.BlockSpec((tm, w), lambda i: (i, 0))

    outs = [("h", D_MODEL, BF16), ("qkv", QKV_W, BF16), ("za", A_WIDTH, BF16), ("q", B_KEY_WIDTH, F32),
            ("k", B_KEY_WIDTH, F32), ("vb", B_WIDTH, BF16), ("zb", B_WIDTH, BF16), ("alr", RANK_PAD, BF16),
            ("u", B_KEY_WIDTH, F32), ("cum", B_KEY_WIDTH, F32), ("ga", D_MODEL, BF16), ("gb", D_MODEL, BF16)]
    res = pl.pallas_call(
        body, name="in_proj", grid=(T // tm,),
        in_specs=[rows(D_MODEL), rows(LANES), rows(LANES), _const_spec((1, D_MODEL)),
                  _const_spec((N_DEV * SHARD_PAD, D_MODEL)), _const_spec((RANK_PAD, B_KEY_WIDTH)),
                  _const_spec((1, B_KEY_WIDTH))] + _any_specs(nl),
        out_specs=[rows(w) for _, w, _ in outs] + _any_specs(1 + nl),
        out_shape=[jax.ShapeDtypeStruct((T, w), dt) for _, w, dt in outs]
                  + [jax.ShapeDtypeStruct((D_IN_PAD, D_MODEL), BF16)]
                  + [jax.ShapeDtypeStruct((N_DEV, *sh.shape), sh.dtype) for sh in later_shards],
        scratch_shapes=[pltpu.VMEM((D_IN_PAD, D_MODEL), BF16),
                        pltpu.SemaphoreType.DMA((nl, len(FLIPS))), pltpu.SemaphoreType.DMA((nl, len(FLIPS))),
                        pltpu.SemaphoreType.DMA((nl,)), pltpu.SemaphoreType.DMA],
        compiler_params=_params("arbitrary"),
    )(x2, cosf, sinf, g_in, wt_sh, wa_pad, b_alpha, *later_shards)
    n_out = len(outs) + 1
    return dict(zip([n for n, _, _ in outs] + ["wt_pad"], res[:n_out])), res[n_out:]


def _dup_kv_head(t, g):
    tf = t.astype(F32)
    keep = (_lane_iota(tf.shape) < A_HEAD_DIM) == (g == 0)
    return jnp.where(keep, tf, pltpu.roll(tf, A_HEAD_DIM, 1)).astype(BF16)


def _stack_heads(t):
    lo = _lane_iota(t.shape) < A_HEAD_DIM
    zero = jnp.zeros_like(t)
    return jnp.concatenate([jnp.where(lo, t, zero), jnp.where(lo, zero, t)], axis=0)


ATT_ROWS = A_HEADS * WINDOW
GROUP_ROWS = ATT_ROWS // A_KV_HEADS
HEADS_PER_GROUP = A_HEADS // A_KV_HEADS


def _band_mask_t(n):
    kj = _row_iota((2 * WINDOW, GROUP_ROWS)) - WINDOW
    qi = _lane_iota((2 * WINDOW, GROUP_ROWS)) % WINDOW
    return (kj <= qi) & (qi - kj < WINDOW) & ((n > 0) | (kj >= 0))


def _stacked_queries(ref, g):
    pairs = range(g * HEADS_PER_GROUP // 2, (g + 1) * HEADS_PER_GROUP // 2)
    return jnp.concatenate([_stack_heads(ref[:, p * LANES:(p + 1) * LANES]) for p in pairs], axis=0)


def _unstack_heads(t, g, ref, dtype):
    lo = _lane_iota((WINDOW, LANES)) < A_HEAD_DIM
    for hh in range(HEADS_PER_GROUP // 2):
        p = g * HEADS_PER_GROUP // 2 + hh
        ref[:, p * LANES:(p + 1) * LANES] = jnp.where(lo, t[2 * hh * WINDOW:(2 * hh + 1) * WINDOW],
                                                       t[(2 * hh + 1) * WINDOW:(2 * hh + 2) * WINDOW]).astype(dtype)


FWD_BLOCKS = 16


def _attn_fwd(qkv, sink_row, B, S):
    T = B * S
    nb = S // WINDOW
    blocks = math.gcd(nb, FWD_BLOCKS)
    steps = nb // blocks

    def one_block(has_prev, sink_ref, q, k, v, o_ref, lse_ref):
        valid = _band_mask_t(has_prev)
        lse_rows = []
        for g in range(A_KV_HEADS):
            kd, vd = _dup_kv_head(k, g), _dup_kv_head(v, g)
            s = jnp.where(valid, _dot_nt(kd, _stacked_queries(q, g)), NEG_BIG)
            sink = sink_ref[:, g * GROUP_ROWS:(g + 1) * GROUP_ROWS]
            m = jnp.maximum(jnp.max(s, axis=0, keepdims=True), sink)
            e = jnp.exp(s - m)
            den = jnp.sum(e, axis=0, keepdims=True) + jnp.exp(sink - m)
            o = _dot_tn((e * (1.0 / den)).astype(BF16), vd)
            _unstack_heads(o, g, o_ref, F32)
            lse = m + jnp.log(den)
            lse_rows += [lse[:, j * WINDOW:(j + 1) * WINDOW] for j in range(HEADS_PER_GROUP)]
        by_head = jnp.concatenate(lse_rows + [jnp.zeros((WINDOW - A_HEADS, WINDOW), F32)], axis=0)
        lse_ref[...] = by_head.T

    def body(sink_ref, q_ref, kc_ref, vc_ref, kp_ref, vp_ref, o_ref, lse_ref):
        k_all = jnp.concatenate([kp_ref[...], kc_ref[...]], axis=0)
        v_all = jnp.concatenate([vp_ref[...], vc_ref[...]], axis=0)
        for j in range(blocks):
            rows = pl.ds(j * WINDOW, WINDOW)
            keys = slice(j * WINDOW, (j + 2) * WINDOW)
            has_prev = pl.program_id(1) if j == 0 else 1
            one_block(has_prev, sink_ref, q_ref[rows, :], k_all[keys], v_all[keys], o_ref.at[rows], lse_ref.at[rows])

    def cur(col, w):
        return pl.BlockSpec((blocks * WINDOW, w), lambda b, n: (b * steps + n, col))

    def prev(col):
        return pl.BlockSpec((WINDOW, LANES), lambda b, n: (b * nb + jnp.maximum(blocks * n - 1, 0), col))

    kcol, vcol = QKV_K // LANES, QKV_V // LANES
    return pl.pallas_call(
        body, name="attn_fwd", grid=(B, steps),
        in_specs=[_const_spec((1, ATT_ROWS)), cur(0, A_WIDTH), cur(kcol, LANES), cur(vcol, LANES), prev(kcol), prev(vcol)],
        out_specs=[cur(0, A_WIDTH), cur(0, LANES)],
        out_shape=[jax.ShapeDtypeStruct((T, A_WIDTH), F32), jax.ShapeDtypeStruct((T, LANES), F32)],
        compiler_params=_params("parallel", "parallel"),
    )(sink_row, qkv, qkv, qkv, qkv, qkv)


ATT_CHUNK = 128


def _chunk_masks(n):
    masks = []
    for half in range(WINDOW // ATT_CHUNK):
        qi = _row_iota((ATT_CHUNK, 2 * WINDOW)) + half * ATT_CHUNK
        kj = _lane_iota((ATT_CHUNK, 2 * WINDOW)) - WINDOW
        masks.append((kj <= qi) & (qi - kj < WINDOW) & ((n > 0) | (kj >= 0)))
    return masks


def _all_stacked_queries(ref):
    return jnp.concatenate([_stacked_queries(ref, g) for g in range(A_KV_HEADS)], axis=0)


def _by_group(fn, lhs, rhs_per_group):
    return jnp.concatenate([fn(lhs[g * GROUP_ROWS:(g + 1) * GROUP_ROWS], rhs_per_group[g])
                            for g in range(A_KV_HEADS)], axis=0)


BWD_BLOCKS = 8


def _attn_bwd(qkv, do, out, lse, sink_col, B, S):
    T = B * S
    nb = S // WINDOW
    M = math.gcd(nb, BWD_BLOCKS)
    steps = nb // M
    n_chunks = ATT_ROWS // ATT_CHUNK
    halves = WINDOW // ATT_CHUNK

    def block(has_prev, sink_ref, q, do_b, out_b, lse_b, k, v, scratch, want_dq):
        s_ref, dp_ref, ds_ref, p_ref = scratch
        width = k.shape[0]
        masks = [mk[:, 0:width] for mk in _chunk_masks(has_prev)]
        kd = [_dup_kv_head(k, g) for g in range(A_KV_HEADS)]
        vd = [_dup_kv_head(v, g) for g in range(A_KV_HEADS)]
        qs, dos = _all_stacked_queries(q), _all_stacked_queries(do_b)
        s_ref[...] = _by_group(_dot_nt, qs, kd)
        dp_ref[...] = _by_group(_dot_nt, dos, vd)
        lane = _lane_iota((ATT_CHUNK, LANES))
        lo = lane < A_HEAD_DIM
        lane1 = _lane_iota((1, LANES))
        dsink_row = jnp.zeros((1, LANES), F32)
        for c in range(n_chunks):
            rows = slice(c * ATT_CHUNK, (c + 1) * ATT_CHUNK)
            head, half = divmod(c, halves)
            qrows = slice(half * ATT_CHUNK, (half + 1) * ATT_CHUNK)
            slab = slice((head // 2) * LANES, (head // 2 + 1) * LANES)
            lse_col = jnp.sum(jnp.where(lane == head, lse_b[qrows, :], 0.0), axis=-1, keepdims=True)
            prod = do_b[qrows, slab].astype(F32) * out_b[qrows, slab].astype(F32)
            mine = lo if head % 2 == 0 else jnp.logical_not(lo)
            delta = jnp.sum(jnp.where(mine, prod, 0.0), axis=-1, keepdims=True)
            prob = jnp.exp(jnp.where(masks[half], s_ref[rows, :], NEG_BIG) - lse_col)
            p_ref[rows, :] = prob.astype(BF16)
            ds_ref[rows, :] = (prob * (dp_ref[rows, :] - delta)).astype(BF16)
            w = -jnp.exp(sink_ref[rows, :] - lse_col) * delta
            dsink_row += jnp.where(lane1 == head, jnp.sum(w, axis=0, keepdims=True), 0.0)
        dq = _by_group(_dot, ds_ref[...], kd) * ATT_SCALE if want_dq else None
        groups = [slice(g * GROUP_ROWS, (g + 1) * GROUP_ROWS) for g in range(A_KV_HEADS)]
        dk = [_dot_tn(ds_ref[rows, :], qs[rows]) for rows in groups]
        dv = [_dot_tn(p_ref[rows, :], dos[rows]) for rows in groups]
        return dq, dk, dv, dsink_row

    def fold(per_group):
        lane = _lane_iota((WINDOW, LANES))
        out = jnp.zeros((WINDOW, LANES), F32)
        for g, acc in enumerate(per_group):
            out = jnp.where((lane < A_HEAD_DIM) == (g == 0), acc + pltpu.roll(acc, A_HEAD_DIM, 1), out)
        return out

    def body(sink_ref, q_ref, qn_ref, do_ref, don_ref, out_ref, outn_ref, lse_ref, lsen_ref, kc_ref, kp_ref, vc_ref, vp_ref,
             dq_ref, dkv_ref, dsink_ref, s_scr, dp_scr, ds_scr, p_scr, s_x, dp_x, ds_x, p_x):
        b, m = pl.program_id(0), pl.program_id(1)

        @pl.when((b == 0) & (m == 0))
        def _():
            dsink_ref[...] = jnp.zeros_like(dsink_ref)

        k_all = jnp.concatenate([kp_ref[...], kc_ref[...]], axis=0)
        v_all = jnp.concatenate([vp_ref[...], vc_ref[...]], axis=0)
        results = []
        for j in range(M):
            rows = slice(j * WINDOW, (j + 1) * WINDOW)
            keys = slice(j * WINDOW, (j + 2) * WINDOW)
            has_prev = m if j == 0 else 1
            results.append(block(has_prev, sink_ref, q_ref[rows, :], do_ref[rows, :], out_ref[rows, :], lse_ref[rows, :],
                                 k_all[keys], v_all[keys], (s_scr.at[j], dp_scr.at[j], ds_scr.at[j], p_scr.at[j]), True))
        last_keys = slice(M * WINDOW, (M + 1) * WINDOW)
        _, dk_x, dv_x, _ = block(1, sink_ref, qn_ref[...], don_ref[...], outn_ref[...], lsen_ref[...],
                                 k_all[last_keys], v_all[last_keys], (s_x, dp_x, ds_x, p_x), False)
        has_next = m < steps - 1
        lo_q = _lane_iota((WINDOW, LANES)) < A_HEAD_DIM
        dsink_row = jnp.zeros((1, LANES), F32)
        for j, (dq, dk, dv, ds_row) in enumerate(results):
            rows = slice(j * WINDOW, (j + 1) * WINDOW)
            for p in range(A_HEADS // 2):
                dq_ref[rows, p * LANES:(p + 1) * LANES] = jnp.where(
                    lo_q, dq[2 * p * WINDOW:(2 * p + 1) * WINDOW], dq[(2 * p + 1) * WINDOW:(2 * p + 2) * WINDOW]).astype(BF16)
            if j + 1 < M:
                dk_next = [t[0:WINDOW] for t in results[j + 1][1]]
                dv_next = [t[0:WINDOW] for t in results[j + 1][2]]
            else:
                dk_next = [jnp.where(has_next, t, 0.0) for t in dk_x]
                dv_next = [jnp.where(has_next, t, 0.0) for t in dv_x]
            dkv_ref[rows, 0:LANES] = fold([own[WINDOW:] + nxt for own, nxt in zip(dk, dk_next)]).astype(BF16)
            dkv_ref[rows, LANES:] = fold([own[WINDOW:] + nxt for own, nxt in zip(dv, dv_next)]).astype(BF16)
            dsink_row += ds_row
        dsink_ref[0:1, :] += dsink_row

    def cur(col, w):
        return pl.BlockSpec((M * WINDOW, w), lambda b, m: (b * steps + m, col))

    def nxt(col, w):
        return pl.BlockSpec((WINDOW, w), lambda b, m: (b * nb + jnp.minimum(M * (m + 1), nb - 1), col))

    def prev(col):
        return pl.BlockSpec((WINDOW, LANES), lambda b, m: (b * nb + jnp.maximum(M * m - 1, 0), col))

    kcol, vcol = QKV_K // LANES, QKV_V // LANES
    scores = (M, ATT_ROWS, 2 * WINDOW)
    extra = (ATT_ROWS, WINDOW)
    return pl.pallas_call(
        body, name="attn_bwd", grid=(B, steps),
        in_specs=[_const_spec((ATT_ROWS, 1)), cur(0, A_WIDTH), nxt(0, A_WIDTH), cur(0, A_WIDTH), nxt(0, A_WIDTH),
                  cur(0, A_WIDTH), nxt(0, A_WIDTH), cur(0, LANES), nxt(0, LANES),
                  cur(kcol, LANES), prev(kcol), cur(vcol, LANES), prev(vcol)],
        out_specs=[cur(0, A_WIDTH), cur(0, 2 * LANES), pl.BlockSpec((8, LANES), lambda b, m: (0, 0))],
        out_shape=[jax.ShapeDtypeStruct((T, A_WIDTH), BF16), jax.ShapeDtypeStruct((T, 2 * LANES), BF16),
                   jax.ShapeDtypeStruct((8, LANES), F32)],
        scratch_shapes=[pltpu.VMEM(scores, F32), pltpu.VMEM(scores, F32), pltpu.VMEM(scores, BF16), pltpu.VMEM(scores, BF16),
                        pltpu.VMEM(extra, F32), pltpu.VMEM(extra, F32), pltpu.VMEM(extra, BF16), pltpu.VMEM(extra, BF16)],
        compiler_params=_params("arbitrary", "arbitrary"),
    )(sink_col, qkv, qkv, do, do, out, out, lse, lse, qkv, qkv, qkv, qkv)


GLA_FWD_TILING = (64, 16)
GLA_BWD_TILING = (256, 4)


def _gla_factors(q_ref, k_ref, cum_ref):
    cpt = q_ref.shape[0] // B_CHUNK
    scale = B_KEY_DIM ** -0.5
    cum = cum_ref[...]
    shape = (B_CHUNK, B_KEY_WIDTH)
    last = jnp.concatenate([jnp.broadcast_to(cum_ref[pl.ds(c * B_CHUNK + B_CHUNK - 1, 1), :], shape)
                            for c in range(cpt)], axis=0)
    mid = jnp.concatenate([jnp.broadcast_to(cum_ref[pl.ds(c * B_CHUNK + B_CHUNK // 2 - 1, 1), :], shape)
                           for c in range(cpt)], axis=0)
    e_qm, e_km, e_qe, e_kd = jnp.exp(cum - mid), jnp.exp(mid - cum), jnp.exp(cum), jnp.exp(last - cum)
    qs = q_ref[...] * scale
    k = k_ref[...]
    return qs, k, (e_qm, e_km, e_qe, e_kd)


def _head_mask(shape, h):
    return (_lane_iota(shape) // B_KEY_DIM) == h


def _stack_masked(t):
    return jnp.concatenate([jnp.where(_head_mask(t.shape, h), t, 0.0) for h in range(B_HEADS)], axis=0).astype(BF16)


def _select_heads(t):
    shape = (B_CHUNK, B_KEY_WIDTH)
    out = jnp.zeros(shape, F32)
    for h in range(B_HEADS):
        out = jnp.where(_head_mask(shape, h), t[h * B_CHUNK:(h + 1) * B_CHUNK], out)
    return out


def _select_state(t):
    shape = (B_VAL_DIM, B_KEY_WIDTH)
    out = jnp.zeros(shape, F32)
    for h in range(B_HEADS):
        out = jnp.where(_head_mask(shape, h), t[h * B_VAL_DIM:(h + 1) * B_VAL_DIM], out)
    return out


def _rows_by_head(t):
    return jnp.concatenate([t[:, h * B_VAL_DIM:(h + 1) * B_VAL_DIM] for h in range(B_HEADS)], axis=0)


def _intra_mask(tile_rows):
    i, j = _row_iota((tile_rows, tile_rows)), _lane_iota((tile_rows, tile_rows))
    return (i // B_CHUNK == j // B_CHUNK) & (j <= i)


def _pair_stack(t, p):
    slab = t[:, p * LANES:(p + 1) * LANES]
    lo = _lane_iota(slab.shape) < B_KEY_DIM
    return jnp.concatenate([jnp.where(lo, slab, 0.0), jnp.where(lo, 0.0, slab)], axis=0).astype(BF16)


def _gla_fwd(q, k, cum, vb, B, S):
    T = B * S
    tile_rows = math.gcd(S, GLA_FWD_TILING[0])
    cpt = tile_rows // B_CHUNK
    nt = S // tile_rows
    tps = math.gcd(nt, GLA_FWD_TILING[1])

    def one_sequence(q_ref, k_ref, cum_ref, v_ref, o_ref, st_all_ref, st_ref):
        qs, kk, (e_qm, e_km, e_qe, e_kd) = _gla_factors(q_ref, k_ref, cum_ref)
        qm, km, qe, kd = qs * e_qm, kk * e_km, qs * e_qe, (kk * e_kd).astype(BF16)
        mask = _intra_mask(tile_rows)
        intra = []
        for p in range(B_HEADS // 2):
            a = _dot_nt(_pair_stack(qm, p), km[:, p * LANES:(p + 1) * LANES].astype(BF16))
            for hh in range(2):
                h = 2 * p + hh
                att = jnp.where(mask, a[hh * tile_rows:(hh + 1) * tile_rows], 0.0).astype(BF16)
                intra.append(_dot(att, v_ref[:, h * B_VAL_DIM:(h + 1) * B_VAL_DIM]))
        inter = []
        for c in range(cpt):
            rows = slice(c * B_CHUNK, (c + 1) * B_CHUNK)
            st = st_ref[...]
            st_all_ref[c] = st
            inter.append(_dot_nt(_stack_masked(qe[rows]), st.astype(BF16)))
            inc = _select_state(_dot_tn(v_ref[rows, :], kd[rows]))
            decay = jnp.exp(cum_ref[pl.ds(c * B_CHUNK + B_CHUNK - 1, 1), :])
            st_ref[...] = st * decay + inc
        for h in range(B_HEADS):
            oi = jnp.concatenate([inter[c][h * B_CHUNK:(h + 1) * B_CHUNK] for c in range(cpt)], axis=0)
            o_ref[:, h * B_VAL_DIM:(h + 1) * B_VAL_DIM] = (intra[h] + oi).astype(BF16)

    def body(q_ref, k_ref, cum_ref, v_ref, o_ref, st_all_ref, st_ref):
        @pl.when(pl.program_id(0) == 0)
        def _():
            st_ref[...] = jnp.zeros_like(st_ref)

        for b in range(B):
            for tile in range(tps):
                tok = pl.ds(tile * tile_rows, tile_rows)
                chunks = pl.ds(tile * cpt, cpt)
                one_sequence(*[r.at[b, tok] for r in (q_ref, k_ref, cum_ref, v_ref, o_ref)],
                             st_all_ref.at[b, chunks], st_ref.at[b])

    def rows(w):
        return pl.BlockSpec((B, tps * tile_rows, w), lambda t: (0, t, 0))

    seq = lambda a: a.reshape(B, S, a.shape[-1])
    o, st_all = pl.pallas_call(
        body, name="gla_fwd", grid=(nt // tps,),
        in_specs=[rows(B_KEY_WIDTH), rows(B_KEY_WIDTH), rows(B_KEY_WIDTH), rows(B_WIDTH)],
        out_specs=[rows(B_WIDTH),
                   pl.BlockSpec((B, tps * cpt, B_VAL_DIM, B_KEY_WIDTH), lambda t: (0, t, 0, 0))],
        out_shape=[jax.ShapeDtypeStruct((B, S, B_WIDTH), BF16),
                   jax.ShapeDtypeStruct((B, S // B_CHUNK, B_VAL_DIM, B_KEY_WIDTH), F32)],
        scratch_shapes=[pltpu.VMEM((B, B_VAL_DIM, B_KEY_WIDTH), F32)],
        compiler_params=_params("arbitrary"),
    )(seq(q), seq(k), seq(cum), seq(vb))
    return o.reshape(T, B_WIDTH), st_all.reshape(T // B_CHUNK, B_VAL_DIM, B_KEY_WIDTH)


def _gla_bwd(q, k, cum, vb, do, st_all, B, S):
    T = B * S
    tile_rows = math.gcd(S, GLA_BWD_TILING[0])
    cpt = tile_rows // B_CHUNK
    nt = S // tile_rows
    tps = math.gcd(nt, GLA_BWD_TILING[1])
    steps = nt // tps
    scale = B_KEY_DIM ** -0.5

    def one_sequence(q_ref, k_ref, cum_ref, v_ref, do_ref, st_all_ref, dq_ref, dk_ref, dv_ref, dla_ref, dst_ref):
        qs, kk, (e_qm, e_km, e_qe, e_kd) = _gla_factors(q_ref, k_ref, cum_ref)
        qm, km, qe, kd = qs * e_qm, kk * e_km, qs * e_qe, kk * e_kd
        mask = _intra_mask(tile_rows)
        dqm_slabs, dkm_slabs, dv_intra = [], [], []
        for p in range(B_HEADS // 2):
            qm_st = _pair_stack(qm, p)
            km_p = km[:, p * LANES:(p + 1) * LANES].astype(BF16)
            a = _dot_nt(qm_st, km_p)
            da_blocks, dqm_h = [], []
            for hh in range(2):
                h = 2 * p + hh
                vs = slice(h * B_VAL_DIM, (h + 1) * B_VAL_DIM)
                att = jnp.where(mask, a[hh * tile_rows:(hh + 1) * tile_rows], 0.0).astype(BF16)
                dv_intra.append(_dot_tn(att, do_ref[:, vs]))
                da = jnp.where(mask, _dot_nt(do_ref[:, vs], v_ref[:, vs]), 0.0).astype(BF16)
                da_blocks.append(da)
                dqm_h.append(_dot(da, km_p))
            lo = _lane_iota((tile_rows, LANES)) < B_KEY_DIM
            dqm_slabs.append(jnp.where(lo, dqm_h[0], dqm_h[1]))
            dkm_slabs.append(_dot_tn(jnp.concatenate(da_blocks, axis=0), qm_st))
        dqm = jnp.concatenate(dqm_slabs, axis=1)
        dkm = jnp.concatenate(dkm_slabs, axis=1)

        dqe_c, dkd_c, dv_inter, tail_c = ([None] * cpt for _ in range(4))
        for c in reversed(range(cpt)):
            rows = slice(c * B_CHUNK, (c + 1) * B_CHUNK)
            dst = dst_ref[...]
            dst_b = dst.astype(BF16)
            dv_inter[c] = _dot_nt(_stack_masked(kd[rows]), dst_b)
            dkd_c[c] = _select_heads(_dot(_rows_by_head(v_ref[rows, :]), dst_b))
            do_c = do_ref[rows, :]
            dqe_c[c] = _select_heads(_dot(_rows_by_head(do_c), st_all_ref[c].astype(BF16)))
            contrib = _select_state(_dot_tn(do_c, qe[rows].astype(BF16)))
            decay = jnp.exp(cum_ref[pl.ds(c * B_CHUNK + B_CHUNK - 1, 1), :])
            tail = (jnp.sum(kk[rows] * dkd_c[c] * e_kd[rows], axis=0, keepdims=True)
                    + decay * jnp.sum(st_all_ref[c] * dst, axis=0, keepdims=True))
            tail_c[c] = jnp.broadcast_to(tail, (B_CHUNK, B_KEY_WIDTH))
            dst_ref[...] = dst * decay + contrib
        dqe = jnp.concatenate(dqe_c, axis=0)
        dkd = jnp.concatenate(dkd_c, axis=0)
        dqs = dqm * e_qm + dqe * e_qe
        dk = dkm * e_km + dkd * e_kd
        dq_ref[...] = (dqs * scale).astype(BF16)
        dk_ref[...] = dk.astype(BF16)
        for h in range(B_HEADS):
            dvi = jnp.concatenate([dv_inter[c][h * B_CHUNK:(h + 1) * B_CHUNK] for c in range(cpt)], axis=0)
            dv_ref[:, h * B_VAL_DIM:(h + 1) * B_VAL_DIM] = (dv_intra[h] + dvi).astype(BF16)
        dd = qs * dqs - kk * dk
        i, j = _row_iota((tile_rows, tile_rows)), _lane_iota((tile_rows, tile_rows))
        upper = ((i // B_CHUNK == j // B_CHUNK) & (j >= i)).astype(BF16)
        hi, mid, lo3 = _split3(dd)
        dla_ref[...] = _dot(upper, hi) + _dot(upper, mid) + _dot(upper, lo3) + jnp.concatenate(tail_c, axis=0)

    def body(q_ref, k_ref, cum_ref, v_ref, do_ref, st_all_ref, dq_ref, dk_ref, dv_ref, dla_ref, dst_ref):
        @pl.when(pl.program_id(0) == 0)
        def _():
            dst_ref[...] = jnp.zeros_like(dst_ref)

        for b in range(B):
            for tile in reversed(range(tps)):
                tok = pl.ds(tile * tile_rows, tile_rows)
                chunks = pl.ds(tile * cpt, cpt)
                one_sequence(*[r.at[b, tok] for r in (q_ref, k_ref, cum_ref, v_ref, do_ref)], st_all_ref.at[b, chunks],
                             *[r.at[b, tok] for r in (dq_ref, dk_ref, dv_ref, dla_ref)], dst_ref.at[b])

    def rows(w):
        return pl.BlockSpec((B, tps * tile_rows, w), lambda t: (0, steps - 1 - t, 0))

    seq = lambda a: a.reshape(B, S, a.shape[-1])
    res = pl.pallas_call(
        body, name="gla_bwd", grid=(steps,),
        in_specs=[rows(B_KEY_WIDTH), rows(B_KEY_WIDTH), rows(B_KEY_WIDTH), rows(B_WIDTH), rows(B_WIDTH),
                  pl.BlockSpec((B, tps * cpt, B_VAL_DIM, B_KEY_WIDTH), lambda t: (0, steps - 1 - t, 0, 0))],
        out_specs=[rows(B_KEY_WIDTH), rows(B_KEY_WIDTH), rows(B_WIDTH), rows(B_KEY_WIDTH)],
        out_shape=[jax.ShapeDtypeStruct((B, S, B_KEY_WIDTH), BF16), jax.ShapeDtypeStruct((B, S, B_KEY_WIDTH), BF16),
                   jax.ShapeDtypeStruct((B, S, B_WIDTH), BF16), jax.ShapeDtypeStruct((B, S, B_KEY_WIDTH), F32)],
        scratch_shapes=[pltpu.VMEM((B, B_VAL_DIM, B_KEY_WIDTH), F32)],
        compiler_params=_params("arbitrary"),
    )(seq(q), seq(k), seq(cum), seq(vb), seq(do), st_all.reshape(B, S // B_CHUNK, B_VAL_DIM, B_KEY_WIDTH))
    return [a.reshape(T, a.shape[-1]) for a in res]


def _merge(x2, tgt2, attn, za, o_gla, zb, ga, gb, w_oa_sh, w_ob_sh, w_o, g_gla, g_final):
    T = x2.shape[0]
    tm = math.gcd(T, 512)
    sub = math.gcd(tm, 256)
    last = T // tm - 1

    def body(x_ref, tgt_ref, attn_ref, za_ref, og_ref, zb_ref, ga_ref, gb_ref,
             woa_sh_ref, wob_sh_ref, wo_ref, gg_ref, gf_ref,
             dxres_ref, dattn_ref, dog_ref, dza_ref, dzb_ref, dga_ref, dgb_ref,
             dwo_out, dwoa_out, dwob_out, small_ref,
             awo_ref, awoa_ref, awob_ref, agf_ref, agg_ref, loss_ref, woa_ref, wob_ref,
             dwo_ref, dwoa_ref, dwob_ref, w_sems, dw_sems):
        w_copies = [pltpu.make_async_copy(sh.at[j], dst.at[:, j * SHARD_OUT:(j + 1) * SHARD_OUT], w_sems.at[a, j])
                    for a, (sh, dst) in enumerate(((woa_sh_ref, woa_ref), (wob_sh_ref, wob_ref))) for j in range(N_DEV)]
        dw_copies = [pltpu.make_async_copy(src, dst, dw_sems.at[a])
                     for a, (src, dst) in enumerate(((dwo_ref, dwo_out), (dwoa_ref, dwoa_out), (dwob_ref, dwob_out)))]

        @pl.when(pl.program_id(0) == 0)
        def _():
            for cp in w_copies:
                cp.start()
            for r in (awo_ref, awoa_ref, awob_ref, agf_ref, agg_ref, loss_ref):
                r[...] = jnp.zeros_like(r)
            for cp in w_copies:
                cp.wait()

        def one_tile(rows):
            za_v = za_ref[rows, :].astype(F32)
            sig_za = _sigmoid_tanh(za_v)
            silu_a = za_v * sig_za
            attn_v = attn_ref[rows, :].astype(F32)
            oa = (attn_v * silu_a).astype(BF16)
            ya = _dot(oa, woa_ref[...])
            og = og_ref[rows, :].astype(F32)
            zb_v = zb_ref[rows, :].astype(F32)
            sig_zb = _sigmoid_tanh(zb_v)
            silu_b = zb_v * sig_zb
            gg = gg_ref[...]
            on_parts, rinv_parts = [], []
            for h in range(B_HEADS):
                seg = og[:, h * B_VAL_DIM:(h + 1) * B_VAL_DIM]
                rinv = lax.rsqrt(jnp.mean(seg * seg, axis=-1, keepdims=True) + NORM_EPS)
                rinv_parts.append(rinv)
                on_parts.append(seg * rinv)
            on = jnp.concatenate(on_parts, axis=1)
            obn = on * gg
            ob = (obn * silu_b).astype(BF16)
            yb = _dot(ob, wob_ref[...])
            sig_a = _sigmoid_tanh(ga_ref[rows, :].astype(F32))
            sig_b = _sigmoid_tanh(gb_ref[rows, :].astype(F32))
            merged = (sig_a * ya + sig_b * yb).astype(BF16)
            out = x_ref[rows, :] + _dot(merged, wo_ref[...])
            rf = lax.rsqrt(jnp.mean(out * out, axis=-1, keepdims=True) + NORM_EPS)
            nrm = out * rf
            gf = gf_ref[...]
            err = nrm * gf - tgt_ref[rows, :]
            loss = jnp.sum(err * err) * (0.5 / D_MODEL)

            dy = err * (1.0 / D_MODEL)
            dgf = jnp.sum(dy * nrm, axis=0, keepdims=True)
            dn = dy * gf
            dout = rf * (dn - nrm * jnp.mean(dn * nrm, axis=-1, keepdims=True))
            dxres_ref[rows, :] = dout
            dout_b = dout.astype(BF16)
            dmerged = _dot_nt(dout_b, wo_ref[...])
            dya = dmerged * sig_a
            dyb = dmerged * sig_b
            dga_ref[rows, :] = (dmerged * ya * sig_a * (1.0 - sig_a)).astype(BF16)
            dgb_ref[rows, :] = (dmerged * yb * sig_b * (1.0 - sig_b)).astype(BF16)
            dya_b, dyb_b = dya.astype(BF16), dyb.astype(BF16)
            doa = _dot_nt(dya_b, woa_ref[...])
            dattn_ref[rows, :] = (doa * silu_a).astype(BF16)
            dza_ref[rows, :] = (doa * attn_v * (sig_za * (1.0 + za_v * (1.0 - sig_za)))).astype(BF16)
            dob = _dot_nt(dyb_b, wob_ref[...])
            dzb_ref[rows, :] = (dob * obn * (sig_zb * (1.0 + zb_v * (1.0 - sig_zb)))).astype(BF16)
            dobn = dob * silu_b
            dgg = jnp.sum(dobn * on, axis=0, keepdims=True)
            don = dobn * gg
            for h in range(B_HEADS):
                sl = slice(h * B_VAL_DIM, (h + 1) * B_VAL_DIM)
                don_h, on_h = don[:, sl], on[:, sl]
                dog_ref[rows, sl] = (rinv_parts[h] * (don_h - on_h * jnp.mean(don_h * on_h, axis=-1, keepdims=True))
                                     ).astype(BF16)
            return (merged, dout_b, oa, dya_b, ob, dyb_b), (loss, dgf, dgg)

        tiles = [one_tile(pl.ds(j * sub, sub)) for j in range(tm // sub)]
        merged, dout_b, oa, dya_b, ob, dyb_b = (jnp.concatenate(parts, axis=0) for parts in zip(*[t[0] for t in tiles]))
        awo_ref[...] += _dot_tn(merged, dout_b)
        awoa_ref[...] += _dot_tn(oa, dya_b)
        awob_ref[...] += _dot_tn(ob, dyb_b)
        for _, (loss, dgf, dgg) in tiles:
            loss_ref[...] += loss
            agf_ref[...] += dgf
            agg_ref[...] += dgg

        @pl.when(pl.program_id(0) == last)
        def _():
            for j in range(N_DEV):
                dwo_ref[j] = awo_ref[j * SHARD_OUT:(j + 1) * SHARD_OUT, :].astype(BF16)
                dwoa_ref[j] = awoa_ref[:, j * SHARD_OUT:(j + 1) * SHARD_OUT].astype(BF16)
                dwob_ref[j] = awob_ref[:, j * SHARD_OUT:(j + 1) * SHARD_OUT].astype(BF16)
            small_ref[...] = jnp.zeros_like(small_ref)
            _put_rows(small_ref, SMALL_G_FINAL, agf_ref[...])
            _put_rows(small_ref, SMALL_G_GLA, agg_ref[...])
            small_ref[SMALL_LOSS:SMALL_LOSS + 1, :] = loss_ref[...]
            for cp in dw_copies:
                cp.start()
            for cp in dw_copies:
                cp.wait()

    def rows(w):
        return pl.BlockSpec((tm, w), lambda i: (i, 0))

    def whole(shape):
        nd = len(shape)
        return pl.BlockSpec(shape, lambda i: (0,) * nd)

    outs = [((T, D_MODEL), F32, rows(D_MODEL)), ((T, A_WIDTH), BF16, rows(A_WIDTH)), ((T, B_WIDTH), BF16, rows(B_WIDTH)),
            ((T, A_WIDTH), BF16, rows(A_WIDTH)), ((T, B_WIDTH), BF16, rows(B_WIDTH)),
            ((T, D_MODEL), BF16, rows(D_MODEL)), ((T, D_MODEL), BF16, rows(D_MODEL)),
            ((N_DEV, SHARD_OUT, D_MODEL), BF16, pl.BlockSpec(memory_space=pl.ANY)),
            ((N_DEV, A_WIDTH, SHARD_OUT), BF16, pl.BlockSpec(memory_space=pl.ANY)),
            ((N_DEV, B_WIDTH, SHARD_OUT), BF16, pl.BlockSpec(memory_space=pl.ANY)),
            ((SMALL_SINKS, LANES), F32, whole((SMALL_SINKS, LANES)))]
    return pl.pallas_call(
        body, name="merge", grid=(T // tm,),
        in_specs=[rows(D_MODEL), rows(D_MODEL), rows(A_WIDTH), rows(A_WIDTH), rows(B_WIDTH), rows(B_WIDTH),
                  rows(D_MODEL), rows(D_MODEL),
                  pl.BlockSpec(memory_space=pl.ANY), pl.BlockSpec(memory_space=pl.ANY),
                  _const_spec((D_MODEL, D_MODEL)), _const_spec((1, B_WIDTH)), _const_spec((1, D_MODEL))],
        out_specs=[o[2] for o in outs],
        out_shape=[jax.ShapeDtypeStruct(o[0], o[1]) for o in outs],
        scratch_shapes=[pltpu.VMEM((D_MODEL, D_MODEL), F32), pltpu.VMEM((A_WIDTH, D_MODEL), F32),
                        pltpu.VMEM((B_WIDTH, D_MODEL), F32), pltpu.VMEM((1, D_MODEL), F32), pltpu.VMEM((1, B_WIDTH), F32),
                        pltpu.VMEM((1, LANES), F32), pltpu.VMEM((A_WIDTH, D_MODEL), BF16),
                        pltpu.VMEM((B_WIDTH, D_MODEL), BF16),
                        pltpu.VMEM((N_DEV, SHARD_OUT, D_MODEL), BF16), pltpu.VMEM((N_DEV, A_WIDTH, SHARD_OUT), BF16),
                        pltpu.VMEM((N_DEV, B_WIDTH, SHARD_OUT), BF16),
                        pltpu.SemaphoreType.DMA((2, N_DEV)), pltpu.SemaphoreType.DMA((3,))],
        compiler_params=pltpu.CompilerParams(dimension_semantics=("arbitrary",), vmem_limit_bytes=V7X_VMEM_LIMIT_MAX),
    )(x2, tgt2, attn, za, o_gla, zb, ga, gb, w_oa_sh, w_ob_sh, w_o, g_gla, g_final)


def _in_proj_bwd(x2, dxres, cosf, sinf, g_in, wt_pad, wa_pad, parts):
    T = x2.shape[0]
    tm = math.gcd(T, 512)
    sub = math.gcd(tm, 256)
    last = T // tm - 1
    base = SMALL_G_IN

    def body(x_ref, dxres_ref, cos_ref, sin_ref, g_ref, wt_ref, wa_ref,
             dq_ref, dkv_ref, dza_ref, dqb_ref, dkb_ref, dvb_ref, dzb_ref, dla_ref, u_ref, alr_ref, dga_ref, dgb_ref,
             dx_ref, dsh_ref, small_ref, dproj_ref, agin_ref, aba_ref, awa_ref):
        @pl.when(pl.program_id(0) == 0)
        def _():
            for r in (agin_ref, aba_ref, awa_ref):
                r[...] = jnp.zeros_like(r)

        def one_tile(rows):
            cos, nsin = cos_ref[rows, :], -sin_ref[rows, :]
            for s in range(A_WIDTH // LANES):
                sl = slice(s * LANES, (s + 1) * LANES)
                dproj_ref[rows, sl] = _rope_slab(dq_ref[rows, sl].astype(F32), cos, nsin).astype(BF16)
            dproj_ref[rows, QKV_K:QKV_V] = _rope_slab(dkv_ref[rows, 0:LANES].astype(F32), cos, nsin).astype(BF16)
            dproj_ref[rows, QKV_V:QKV_W] = dkv_ref[rows, LANES:]

            def put(name, val):
                a, b = SEG[name]
                dproj_ref[rows, a:b] = val

            put("za", dza_ref[rows, :])
            put("qb", dqb_ref[rows, :])
            put("kb", dkb_ref[rows, :])
            put("vb", dvb_ref[rows, :])
            put("zb", dzb_ref[rows, :])
            put("ga", dga_ref[rows, :])
            put("gb", dgb_ref[rows, :])
            du = dla_ref[rows, :] * (1.0 / B_GATE_TEMP) * _sigmoid(-u_ref[rows, :])
            du_b = du.astype(BF16)
            put("alr", _dot_nt(du_b, wa_ref[...]).astype(BF16))

            for j in range(N_DEV):
                col = (j % 2) * SHARD_PAD
                for a, b in _shard_pad_cols(j):
                    dsh_ref[j // 2, rows, col:col + b - a] = dproj_ref[rows, a:b]
                    col += b - a
                dsh_ref[j // 2, rows, col:(j % 2 + 1) * SHARD_PAD] = jnp.zeros((sub, SHARD_PAD - SHARD_IN), BF16)

            dh = _dot(dproj_ref[rows, :], wt_ref[...])
            x = x_ref[rows, :]
            r = lax.rsqrt(jnp.mean(x * x, axis=-1, keepdims=True) + NORM_EPS)
            nrm = x * r
            dn = dh * g_ref[...]
            dx_ref[rows, :] = dxres_ref[rows, :] + r * (dn - nrm * jnp.mean(dn * nrm, axis=-1, keepdims=True))
            return jnp.sum(dh * nrm, axis=0, keepdims=True), jnp.sum(du, axis=0, keepdims=True), alr_ref[rows, :], du_b

        for j in range(tm // sub):
            dgin, dba, alr, du_b = one_tile(pl.ds(j * sub, sub))
            agin_ref[...] += dgin
            aba_ref[...] += dba
            awa_ref[...] += _dot_tn(alr, du_b)

        @pl.when(pl.program_id(0) == last)
        def _():
            small_ref[...] = jnp.zeros_like(small_ref)
            _put_rows(small_ref, SMALL_G_IN - base, agin_ref[...])
            _put_rows(small_ref, SMALL_B_ALPHA - base, aba_ref[...])
            for half in range(B_KEY_WIDTH // LANES):
                r0 = SMALL_W_ALPHA - base + half * B_GATE_RANK
                small_ref[r0:r0 + B_GATE_RANK, :] = awa_ref[0:B_GATE_RANK, half * LANES:(half + 1) * LANES]

    def rows(w):
        return pl.BlockSpec((tm, w), lambda i: (i, 0))

    names = ["dq", "dkv", "dza", "dqb", "dkb", "dvb", "dzb", "dla", "u", "alr", "dga", "dgb"]
    return pl.pallas_call(
        body, name="in_proj_bwd", grid=(T // tm,),
        in_specs=[rows(D_MODEL), rows(D_MODEL), rows(LANES), rows(LANES), _const_spec((1, D_MODEL)),
                  _const_spec((D_IN_PAD, D_MODEL)), _const_spec((RANK_PAD, B_KEY_WIDTH))]
                 + [rows(parts[n].shape[1]) for n in names],
        out_specs=[rows(D_MODEL), pl.BlockSpec((N_CHIPS, tm, 2 * SHARD_PAD), lambda i: (0, i, 0)),
                   pl.BlockSpec((SMALL_ROWS - base, LANES), lambda i: (0, 0))],
        out_shape=[jax.ShapeDtypeStruct((T, D_MODEL), F32), jax.ShapeDtypeStruct((N_CHIPS, T, 2 * SHARD_PAD), BF16),
                   jax.ShapeDtypeStruct((SMALL_ROWS - base, LANES), F32)],
        scratch_shapes=[pltpu.VMEM((tm, D_IN_PAD), BF16), pltpu.VMEM((1, D_MODEL), F32), pltpu.VMEM((1, B_KEY_WIDTH), F32),
                        pltpu.VMEM((RANK_PAD, B_KEY_WIDTH), F32)],
        compiler_params=pltpu.CompilerParams(dimension_semantics=("arbitrary",), vmem_limit_bytes=V7X_VMEM_LIMIT_MAX),
    )(x2, dxres, cosf, sinf, g_in, wt_pad, wa_pad, *[parts[n] for n in names])


FLIPS = [(dx, dy, dc) for dx in (0, 1) for dy in (0, 1) for dc in (0, 1)][1:]


def _my_place():
    return lax.axis_index("x"), lax.axis_index("y"), lax.axis_index("c")


def _any_specs(n):
    return [pl.BlockSpec(memory_space=pl.ANY)] * n


def _gather_first(shards, pos_col):
    n = len(shards)
    T = pos_col.shape[0]
    rows_per_pass = math.gcd(T, 512)
    invf, sign = _rope_lane_constants()

    def body(*refs):
        ins, (pos_ref, invf_ref, sign_ref) = refs[:n], refs[n:n + 3]
        outs, (cos_ref, sin_ref) = refs[n + 3:2 * n + 3], refs[2 * n + 3:2 * n + 5]
        send_sems, recv_sems, local_sems = refs[2 * n + 5:]
        x, y, c = _my_place()
        me, sibling = (x, y, c), (x, y, 1 - c)
        chips = [(1 - x, y), (x, 1 - y), (1 - x, 1 - y)]

        def block(a, px, py, pc):
            return outs[a].at[4 * px + 2 * py + pc]

        def copy(a, k, blk, to, src=None):
            return pltpu.make_async_remote_copy(
                src_ref=block(a, *blk) if src is None else src, dst_ref=block(a, *blk),
                send_sem=send_sems.at[a, k], recv_sem=recv_sems.at[a, k], device_id=to, device_id_type=MESH)

        mine = [pltpu.make_async_copy(ins[a], block(a, *me), local_sems.at[a]) for a in range(n)]
        for cp in mine:
            cp.start()
        first = []
        for a in range(n):
            first.append(copy(a, 0, me, sibling, src=ins[a]))
            first += [copy(a, 1 + j, me, (*chip, c), src=ins[a]) for j, chip in enumerate(chips)]
        for cp in first:
            cp.start()

        def tables(i, carry):
            rows = pl.ds(pl.multiple_of(i * rows_per_pass, rows_per_pass), rows_per_pass)
            ang = pos_ref[rows, :].astype(F32) * invf_ref[...]
            cos_ref[rows, :] = jnp.cos(ang)
            sin_ref[rows, :] = jnp.sin(ang) * sign_ref[...]
            return carry

        lax.fori_loop(0, T // rows_per_pass, tables, 0)

        passed = []
        for j, chip in enumerate(chips):
            for a in range(n):
                copy(a, 1 + j, (*chip, c), me).wait_recv()
                fwd = copy(a, 4 + j, (*chip, c), sibling)
                fwd.start()
                passed.append(fwd)
        for a in range(n):
            copy(a, 0, sibling, me).wait_recv()
            for j, chip in enumerate(chips):
                copy(a, 4 + j, (*chip, 1 - c), me).wait_recv()
        for cp in first + passed:
            cp.wait_send()
        for cp in mine:
            cp.wait()

    vmem = pl.BlockSpec(memory_space=pltpu.VMEM)
    res = pl.pallas_call(
        body, name="gather_weights",
        in_specs=_any_specs(n) + [vmem] * 3, out_specs=_any_specs(n) + [vmem] * 2,
        out_shape=[jax.ShapeDtypeStruct((N_DEV, *s.shape), s.dtype) for s in shards]
                  + [jax.ShapeDtypeStruct((T, LANES), F32)] * 2,
        scratch_shapes=[pltpu.SemaphoreType.DMA((n, 7)), pltpu.SemaphoreType.DMA((n, 7)), pltpu.SemaphoreType.DMA((n,))],
        compiler_params=pltpu.CompilerParams(vmem_limit_bytes=V7X_VMEM_LIMIT),
    )(*shards, pos_col, invf, sign)
    return res[:n], res[n], res[n + 1]


def _w_in_grad_rs(h, dsh, chip_order, small, wgrads):
    T = h.shape[0]
    tk = math.gcd(T, 2048)
    nk = T // tk
    chip_flips = [(1, 1), (1, 0), (0, 1)]
    n_steps = len(chip_flips) + 1
    SIB = len(chip_flips)
    nw = len(wgrads)
    halves = (slice(0, SHARD_PAD), slice(SHARD_PAD, 2 * SHARD_PAD))

    def body(order_ref, h_ref, d_ref, s_ref, *rest):
        g_refs, (own_ref, recv_ref, sall_ref), rv_refs = rest[:nw], rest[nw:nw + 3], rest[nw + 3:2 * nw + 3]
        (acc_ref, pre_ref, to_sib_ref, to_chip_ref, sib_send, sib_recv, chip_send, chip_recv,
         ssend_sems, srecv_sems, local_sem, wsend_sems, wrecv_sems) = rest[2 * nw + 3:]
        i, kk = pl.program_id(0), pl.program_id(1)
        x, y, c = _my_place()
        my_dev = 4 * x + 2 * y + c

        def wcopy(a, r):
            dx, dy, dc = FLIPS[r]
            return pltpu.make_async_remote_copy(
                src_ref=g_refs[a].at[4 * (x ^ dx) + 2 * (y ^ dy) + (c ^ dc)], dst_ref=rv_refs[a].at[r],
                send_sem=wsend_sems.at[a, r], recv_sem=wrecv_sems.at[a, r],
                device_id=(x ^ dx, y ^ dy, c ^ dc), device_id_type=MESH)

        def small_copy(r, slot):
            dx, dy, dc = FLIPS[r]
            return pltpu.make_async_remote_copy(
                src_ref=s_ref, dst_ref=sall_ref.at[slot], send_sem=ssend_sems.at[r], recv_sem=srecv_sems.at[r],
                device_id=(x ^ dx, y ^ dy, c ^ dc), device_id_type=MESH)

        keep_small = pltpu.make_async_copy(s_ref, sall_ref.at[my_dev], local_sem)

        def sib_copy(t):
            dst = recv_ref.at[SIB] if t == SIB else pre_ref.at[t]
            return pltpu.make_async_remote_copy(
                src_ref=to_sib_ref.at[t], dst_ref=dst, send_sem=sib_send.at[t], recv_sem=sib_recv.at[t],
                device_id=(x, y, 1 - c), device_id_type=MESH)

        def chip_copy(t):
            dx, dy = chip_flips[t]
            return pltpu.make_async_remote_copy(
                src_ref=to_chip_ref.at[t], dst_ref=recv_ref.at[t], send_sem=chip_send.at[t], recv_sem=chip_recv.at[t],
                device_id=(x ^ dx, y ^ dy, c), device_id_type=MESH)

        def accumulate(rows):
            acc_ref[rows, :] += _dot_tn(d_ref[:, rows], h_ref[...])

        @pl.when((i == 0) & (kk == 0))
        def _():
            keep_small.start()
            for r in range(len(FLIPS)):
                small_copy(r, my_dev).start()
                for a in range(nw):
                    wcopy(a, r).start()

        @pl.when(kk == 0)
        def _():
            acc_ref[...] = jnp.zeros_like(acc_ref)

        @pl.when(kk < nk - 1)
        def _():
            accumulate(slice(0, 2 * SHARD_PAD))

        for core in range(2):
            @pl.when((kk == nk - 1) & (c == core))
            def _(mine=halves[core], theirs=halves[1 - core]):
                accumulate(theirs)
                for t in range(n_steps):
                    @pl.when(i == t)
                    def _(t=t):
                        to_sib_ref[t] = acc_ref[theirs, :].astype(BF16)
                        sib_copy(t).start()
                accumulate(mine)
                for t in range(len(chip_flips)):
                    @pl.when(i == t)
                    def _(t=t):
                        sib_copy(t).wait_recv()
                        to_chip_ref[t] = (acc_ref[mine, :] + pre_ref[t].astype(F32)).astype(BF16)
                        chip_copy(t).start()

                @pl.when(i == n_steps - 1)
                def _():
                    own_ref[...] = acc_ref[mine, :]

        @pl.when((i == n_steps - 1) & (kk == nk - 1))
        def _():
            for t in range(len(chip_flips)):
                sib_copy(t).wait_send()
                chip_copy(t).wait_send()
                chip_copy(t).wait_recv()
            sib_copy(SIB).wait_send()
            sib_copy(SIB).wait_recv()
            for r, (dx, dy, dc) in enumerate(FLIPS):
                small_copy(r, 4 * (x ^ dx) + 2 * (y ^ dy) + (c ^ dc)).wait_recv()
                small_copy(r, my_dev).wait_send()
                for a in range(nw):
                    wcopy(a, r).wait()
            keep_small.wait()

    shard = (SHARD_PAD, D_MODEL)
    res = pl.pallas_call(
        body, name="w_in_grad_rs",
        grid_spec=pltpu.PrefetchScalarGridSpec(
            num_scalar_prefetch=1, grid=(n_steps, nk),
            in_specs=[pl.BlockSpec((tk, D_MODEL), lambda i, kk, order: (kk, 0)),
                      pl.BlockSpec((None, tk, 2 * SHARD_PAD), lambda i, kk, order: (order[i], kk, 0)),
                      pl.BlockSpec(memory_space=pl.ANY)] + _any_specs(nw),
            out_specs=[pl.BlockSpec(shard, lambda i, kk, order: (0, 0)),
                       pl.BlockSpec(memory_space=pl.ANY), pl.BlockSpec(memory_space=pl.ANY)] + _any_specs(nw),
            scratch_shapes=[pltpu.VMEM((2 * SHARD_PAD, D_MODEL), F32),
                            pltpu.VMEM((SIB, *shard), BF16), pltpu.VMEM((SIB + 1, *shard), BF16),
                            pltpu.VMEM((SIB, *shard), BF16),
                            pltpu.SemaphoreType.DMA((SIB + 1,)), pltpu.SemaphoreType.DMA((SIB + 1,)),
                            pltpu.SemaphoreType.DMA((SIB,)), pltpu.SemaphoreType.DMA((SIB,)),
                            pltpu.SemaphoreType.DMA((7,)), pltpu.SemaphoreType.DMA((7,)), pltpu.SemaphoreType.DMA,
                            pltpu.SemaphoreType.DMA((nw, len(FLIPS))), pltpu.SemaphoreType.DMA((nw, len(FLIPS)))]),
        out_shape=[jax.ShapeDtypeStruct(shard, F32),
                   jax.ShapeDtypeStruct((SIB + 1, *shard), BF16),
                   jax.ShapeDtypeStruct((N_DEV, *small.shape), F32)]
                  + [jax.ShapeDtypeStruct((len(FLIPS), *g.shape[1:]), g.dtype) for g in wgrads],
        compiler_params=_params("arbitrary", "arbitrary"),
    )(chip_order, h, dsh, small, *wgrads)
    return res[0], res[1], res[2], res[3:]


def _adam_math(w, g, m, v):
    m_new = ADAM_B1 * m + (1.0 - ADAM_B1) * g
    v_new = ADAM_B2 * v + (1.0 - ADAM_B2) * (g * g)
    m_hat = m_new / (1.0 - ADAM_B1 ** ADAM_STEP)
    v_hat = v_new / (1.0 - ADAM_B2 ** ADAM_STEP)
    delta = -ADAM_LR * (m_hat / (jnp.sqrt(v_hat) + ADAM_EPS) + ADAM_WD * w)
    return delta, m_new, v_new


def _adam_big(jobs):
    steps = 8
    n = len(jobs)
    idx = jnp.stack([job[1] for job in jobs]).astype(jnp.int32)
    blocks = []
    for own, _, recv, w, m, v in jobs:
        (rw, cw), rp = w.shape, own.shape[1]
        by_cols = rp != rw
        blk_w = (rw, cw // steps) if by_cols else (rw // steps, cw)
        blk_g = (rp, cw // steps) if by_cols else (rw // steps, cw)
        blocks.append((blk_w, blk_g, by_cols))

    def body(idx_ref, *refs):
        ins, outs = refs[:5 * n], refs[5 * n:]
        for j, (blk_w, _, _) in enumerate(blocks):
            o_ref, r_ref, w_ref, m_ref, v_ref = ins[5 * j:5 * j + 5]
            g_ref, d_ref, mo_ref, vo_ref = outs[4 * j:4 * j + 4]
            g = o_ref[...].astype(F32)
            for r in range(r_ref.shape[0]):
                g = g + r_ref[r].astype(F32)
            g = g[0:blk_w[0], :]
            g_ref[...] = g
            d_ref[...], mo_ref[...], vo_ref[...] = _adam_math(w_ref[...], g, m_ref[...], v_ref[...])

    in_specs, out_specs, out_shape, args = [], [], [], []
    for j, ((own, _, recv, w, m, v), (blk_w, blk_g, by_cols)) in enumerate(zip(jobs, blocks)):
        at = (lambda i: (0, i)) if by_cols else (lambda i: (i, 0))
        spec = pl.BlockSpec(blk_w, lambda i, idx_ref, at=at: at(i))
        in_specs += [pl.BlockSpec((None, *blk_g), lambda i, idx_ref, at=at, j=j: (idx_ref[j], *at(i))),
                     pl.BlockSpec((recv.shape[0], *blk_g), lambda i, idx_ref, at=at: (0, *at(i))), spec, spec, spec]
        out_specs += [spec] * 4
        out_shape += [jax.ShapeDtypeStruct(w.shape, F32)] * 4
        args += [own, recv, w, m, v]
    res = pl.pallas_call(
        body, name="adam_big",
        grid_spec=pltpu.PrefetchScalarGridSpec(num_scalar_prefetch=1, grid=(steps,), in_specs=in_specs, out_specs=out_specs),
        out_shape=out_shape,
        compiler_params=_params("parallel"),
    )(idx, *args)
    return [res[4 * j:4 * j + 4] for j in range(n)]


def _adam_small(small_all, params):
    flat = [a for triple in params for a in triple]
    n_par = len(params)

    def body(s_ref, *refs):
        ins, outs, loss_ref = refs[:3 * n_par], refs[3 * n_par:-1], refs[-1]
        g_slab = s_ref[0]
        for dev in range(1, N_DEV):
            g_slab = g_slab + s_ref[dev]
        loss_ref[...] = g_slab[SMALL_LOSS:SMALL_LOSS + 1, :]
        dev = 4 * lax.axis_index("x") + 2 * lax.axis_index("y") + lax.axis_index("c")
        alpha_full = jnp.concatenate([g_slab[SMALL_W_ALPHA + half * B_GATE_RANK:SMALL_W_ALPHA + (half + 1) * B_GATE_RANK]
                                      for half in range(B_KEY_WIDTH // LANES)], axis=1)
        alpha_mine = pltpu.roll(alpha_full, (B_KEY_WIDTH - dev * SHARD_ALPHA) % B_KEY_WIDTH, 1)[:, 0:SHARD_ALPHA]
        grads = [_take_rows(g_slab, SMALL_G_IN, D_MODEL // LANES), _take_rows(g_slab, SMALL_G_FINAL, D_MODEL // LANES),
                 _take_rows(g_slab, SMALL_G_GLA, B_WIDTH // LANES), _take_rows(g_slab, SMALL_B_ALPHA, B_KEY_WIDTH // LANES),
                 g_slab[SMALL_SINKS:SMALL_SINKS + 1, 0:A_HEADS], alpha_mine]
        for i, g in enumerate(grads):
            w_ref, m_ref, v_ref = ins[3 * i:3 * i + 3]
            delta, m_new, v_new = _adam_math(w_ref[...], g, m_ref[...], v_ref[...])
            outs[4 * i][...] = g
            outs[4 * i + 1][...] = delta
            outs[4 * i + 2][...] = m_new
            outs[4 * i + 3][...] = v_new

    res = pl.pallas_call(
        body, name="adam_small",
        out_shape=[jax.ShapeDtypeStruct(t[0].shape, F32) for t in params for _ in range(4)]
                  + [jax.ShapeDtypeStruct((1, LANES), F32)],
    )(small_all, *flat)
    return [res[4 * i:4 * i + 4] for i in range(n_par)], res[-1]


def _local_step(x, cosf, sinf, loss_target, g_in, wt_sh, wa_pad, b_alpha, sinks, g_gla, out_shards, g_final, chip_order):
    B, S, _ = x.shape
    T = B * S
    x2 = x.reshape(T, D_MODEL)
    tgt2 = loss_target.reshape(T, D_MODEL)
    f, (g_woa, g_wob, g_wo) = _in_proj(x2, cosf, sinf, g_in, wt_sh, wa_pad, b_alpha, out_shards)
    w_o = g_wo.reshape(D_MODEL, D_MODEL)
    sink_row = jnp.repeat(sinks, WINDOW).reshape(1, ATT_ROWS)
    sink_col = sink_row.reshape(ATT_ROWS, 1)
    attn, lse = _attn_fwd(f["qkv"], sink_row, B, S)
    o_gla, st_all = _gla_fwd(f["q"], f["k"], f["cum"], f["vb"], B, S)
    (dxres, dattn, dog, dza, dzb, dga, dgb, dw_o, dw_oa, dw_ob, small_a) = _merge(
        x2, tgt2, attn, f["za"], o_gla, f["zb"], f["ga"], f["gb"], g_woa, g_wob, w_o, g_gla, g_final)
    dq, dkv, dsink = _attn_bwd(f["qkv"], dattn, attn, lse, sink_col, B, S)
    dqb, dkb, dvb, dla = _gla_bwd(f["q"], f["k"], f["cum"], f["vb"], dog, st_all, B, S)
    parts = dict(dq=dq, dkv=dkv, dza=dza, dqb=dqb, dkb=dkb, dvb=dvb, dzb=dzb, dla=dla, u=f["u"], alr=f["alr"],
                 dga=dga, dgb=dgb)
    dx, dsh, small_c = _in_proj_bwd(x2, dxres, cosf, sinf, g_in, f["wt_pad"], wa_pad, parts)
    small = jnp.concatenate([small_a, dsink, small_c], axis=0)
    own_in, rv_in, small_all, (rv_o, rv_oa, rv_ob) = _w_in_grad_rs(f["h"], dsh, chip_order, small, [dw_o, dw_oa, dw_ob])
    return dict(grad_x=dx.reshape(B, S, D_MODEL), own_in=own_in, rv_in=rv_in,
                own_o=dw_o, rv_o=rv_o, own_oa=dw_oa, rv_oa=rv_oa, own_ob=dw_ob, rv_ob=rv_ob, small_all=small_all)


def kernel(x, positions, g_in, w_in, w_alpha_up, b_alpha, attn_sinks, g_gla_norm, w_out_a, w_out_b, w_o, g_final, loss_target, m_g_in, m_w_in, m_w_alpha_up, m_b_alpha, m_attn_sinks, m_g_gla_norm, m_w_out_a, m_w_out_b, m_w_o, m_g_final, v_g_in, v_w_in, v_w_alpha_up, v_b_alpha, v_attn_sinks, v_g_gla_norm, v_w_out_a, v_w_out_b, v_w_o, v_g_final):
    xi, yi, ci = _my_place()
    chip = 2 * xi + yi
    chip_order = jnp.stack([chip ^ 3, chip ^ 2, chip ^ 1, chip]).astype(jnp.int32)

    (g_win, g_wa), cosf, sinf = _gather_first(
        [jnp.pad(w_in[0].T.astype(BF16), ((0, SHARD_PAD - SHARD_IN), (0, 0))), w_alpha_up[0].astype(BF16)],
        positions.reshape(-1, 1))
    wt_sh = g_win.reshape(N_DEV * SHARD_PAD, D_MODEL)
    wa_pad = jnp.pad(jnp.concatenate([g_wa[j] for j in range(N_DEV)], axis=1), ((0, RANK_PAD - B_GATE_RANK), (0, 0)))

    r = _local_step(x, cosf, sinf, loss_target, g_in, wt_sh, wa_pad, b_alpha, attn_sinks[0], g_gla_norm,
                    [w_out_a[0].astype(BF16), w_out_b[0].astype(BF16), w_o[0].astype(BF16)],
                    g_final.reshape(1, D_MODEL), chip_order)

    dev = 4 * xi + 2 * yi + ci
    big = _adam_big([(r["own_in"][None], jnp.int32(0), r["rv_in"], w_in[0].T, m_w_in[0].T, v_w_in[0].T),
                     (r["own_oa"], dev, r["rv_oa"], w_out_a[0], m_w_out_a[0], v_w_out_a[0]),
                     (r["own_ob"], dev, r["rv_ob"], w_out_b[0], m_w_out_b[0], v_w_out_b[0]),
                     (r["own_o"], dev, r["rv_o"], w_o[0], m_w_o[0], v_w_o[0])])
    big[0] = [a.T for a in big[0]]
    row = lambda a: a.reshape(1, D_MODEL)
    (s_in, s_final, s_gla, s_ba, s_sinks, s_wa), loss_row = _adam_small(r["small_all"], [
        (g_in, m_g_in, v_g_in), (row(g_final), row(m_g_final), row(v_g_final)),
        (g_gla_norm, m_g_gla_norm, v_g_gla_norm), (b_alpha, m_b_alpha, v_b_alpha),
        (attn_sinks, m_attn_sinks, v_attn_sinks), (w_alpha_up[0], m_w_alpha_up[0], v_w_alpha_up[0])])

    def group(i):
        return (s_in[i], big[0][i][None], s_wa[i][None], s_ba[i], s_sinks[i], s_gla[i], big[1][i][None], big[2][i][None],
                big[3][i][None], s_final[i].reshape(D_MODEL))

    return (loss_row[0, 0], r["grad_x"], *group(0), *group(1), *group(2), *group(3))
```

```python
import functools
import math

import numpy as np
import jax
import jax.numpy as jnp
from jax import lax
from jax.experimental import pallas as pl
from jax.experimental.pallas import tpu as pltpu

F32 = jnp.float32
BF16 = jnp.bfloat16
MESH = pl.DeviceIdType.MESH

D_MODEL = 1024
A_HEADS, A_KV_HEADS, A_HEAD_DIM = 8, 2, 64
A_WIDTH, A_KV_WIDTH = 512, 128
WINDOW = 128
ROPE_THETA = 500000.0
ROPE_DIM = 16
B_HEADS, B_KEY_DIM, B_VAL_DIM = 4, 64, 128
B_KEY_WIDTH, B_WIDTH = 256, 512
B_GATE_RANK = 16
B_GATE_TEMP = 16.0
B_CHUNK = 64
NORM_EPS = 1e-6
NEG_BIG = -1e30
D_IN = 4880
N_DEV = 8
N_CHIPS = 4
ADAM_LR, ADAM_B1, ADAM_B2, ADAM_EPS, ADAM_WD, ADAM_STEP = 0.001, 0.9, 0.999, 1e-08, 0.01, 10

LANES = 128
V7X_VMEM_LIMIT = 56 * 1024 * 1024
V7X_VMEM_LIMIT_MAX = 62 * 1024 * 1024

RANK_PAD = LANES
SEG = {}
_off = 0
for _name, _w in (("qa", 512), ("ka", 128), ("va", 128), ("za", 512), ("qb", 256), ("kb", 256),
                  ("vb", 512), ("zb", 512), ("alr", RANK_PAD), ("ga", 1024), ("gb", 1024)):
    SEG[_name] = (_off, _off + _w)
    _off += _w
D_IN_PAD = _off
ALR_SRC = SEG["alr"][0]
QKV_K, QKV_V, QKV_W = SEG["ka"][0], SEG["va"][0], SEG["va"][1]
ATT_SCALE = A_HEAD_DIM ** -0.5

SHARD_IN = D_IN // N_DEV
SHARD_PAD = 640
SHARD_OUT = D_MODEL // N_DEV
SHARD_ALPHA = B_KEY_WIDTH // N_DEV

SMALL_G_FINAL, SMALL_G_GLA, SMALL_LOSS, SMALL_SINKS, SMALL_G_IN, SMALL_B_ALPHA, SMALL_W_ALPHA = 0, 8, 12, 16, 24, 32, 40
SMALL_ROWS = 72


def _dot(a, b):
    return jnp.dot(a, b, preferred_element_type=F32)


def _dot_nt(a, b):
    return lax.dot_general(a, b, (((1,), (1,)), ((), ())), preferred_element_type=F32)


def _dot_tn(a, b):
    return lax.dot_general(a, b, (((0,), (0,)), ((), ())), preferred_element_type=F32)


def _sigmoid(z):
    return 1.0 / (1.0 + jnp.exp(-z))


def _sigmoid_tanh(z):
    return 0.5 * jnp.tanh(0.5 * z) + 0.5


def _params(*sem):
    return pltpu.CompilerParams(dimension_semantics=sem, vmem_limit_bytes=V7X_VMEM_LIMIT)


def _const_spec(shape):
    nd = len(shape)
    return pl.BlockSpec(shape, lambda *_: (0,) * nd, pipeline_mode=pl.Buffered(1))


def _lane_iota(shape):
    return lax.broadcasted_iota(jnp.int32, shape, 1)


def _row_iota(shape):
    return lax.broadcasted_iota(jnp.int32, shape, 0)


def _split3(v):
    hi = v.astype(BF16)
    r1 = v - hi.astype(F32)
    mid = r1.astype(BF16)
    lo = (r1 - mid.astype(F32)).astype(BF16)
    return hi, mid, lo


def _put_rows(ref, row0, vec):
    for r in range(vec.shape[1] // LANES):
        ref[row0 + r:row0 + r + 1, :] = vec[:, r * LANES:(r + 1) * LANES]


def _take_rows(slab, row0, n):
    return jnp.concatenate([slab[row0 + r:row0 + r + 1, :] for r in range(n)], axis=1)


def _rope_lane_constants():
    half = ROPE_DIM // 2
    inv_freq = np.exp(-math.log(ROPE_THETA) * np.arange(half, dtype=np.float32) * np.float32(2.0 / ROPE_DIM)).astype(np.float32)
    lane = np.arange(LANES)
    j = lane % A_HEAD_DIM
    invf = np.where(j < ROPE_DIM, inv_freq[j % half], 0.0).astype(np.float32)
    sign = np.where(j < half, -1.0, np.where(j < ROPE_DIM, 1.0, 0.0)).astype(np.float32)
    return jnp.asarray(invf)[None, :], jnp.asarray(sign)[None, :]


def _rope_slab(t, cos, sin_signed):
    first = (_lane_iota(t.shape) % A_HEAD_DIM) < (ROPE_DIM // 2)
    partner = jnp.where(first, pltpu.roll(t, LANES - ROPE_DIM // 2, 1), pltpu.roll(t, ROPE_DIM // 2, 1))
    return t * cos + partner * sin_signed


def _shard_pad_cols(j):
    cut = ALR_SRC + B_GATE_RANK
    shift = RANK_PAD - B_GATE_RANK
    a, b = j * SHARD_IN, (j + 1) * SHARD_IN
    if b <= cut:
        return [(a, b)]
    if a >= cut:
        return [(a + shift, b + shift)]
    return [(a, cut), (cut + shift, b + shift)]


def _in_proj(x2, cosf, sinf, g_in, wt_sh, wa_pad, b_alpha, later_shards):
    T = x2.shape[0]
    tm = math.gcd(T, 512)
    sub = math.gcd(tm, 256)
    last = T // tm - 1
    nl = len(later_shards)

    def body(x_ref, cos_ref, sin_ref, g_ref, wsh_ref, wa_ref, ba_ref, *rest):
        sh_refs, rest = rest[:nl], rest[nl:]
        (h_ref, qkv_ref, za_ref, q_ref, k_ref, vb_ref, zb_ref, alr_ref, u_ref, cum_ref, ga_ref, gb_ref, wt_out) = rest[:13]
        all_refs, (wt_ref, send_sems, recv_sems, local_sems, wt_sem) = rest[13:13 + nl], rest[13 + nl:]
        wt_copy = pltpu.make_async_copy(wt_ref, wt_out, wt_sem)
        px, py, pc = _my_place()
        my_dev = 4 * px + 2 * py + pc

        def wcopy(a, r, slot):
            dx, dy, dc = FLIPS[r]
            return pltpu.make_async_remote_copy(
                src_ref=sh_refs[a], dst_ref=all_refs[a].at[slot], send_sem=send_sems.at[a, r],
                recv_sem=recv_sems.at[a, r], device_id=(px ^ dx, py ^ dy, pc ^ dc), device_id_type=MESH)

        keep = [pltpu.make_async_copy(sh_refs[a], all_refs[a].at[my_dev], local_sems.at[a]) for a in range(nl)]

        @pl.when(pl.program_id(0) == 0)
        def _():
            for a in range(nl):
                keep[a].start()
                for r in range(len(FLIPS)):
                    wcopy(a, r, my_dev).start()

        @pl.when(pl.program_id(0) == 0)
        def _():
            for j in range(N_DEV):
                src = j * SHARD_PAD
                for a, b in _shard_pad_cols(j):
                    wt_ref[a:b, :] = wsh_ref[src:src + b - a, :]
                    src += b - a
            a, b = SEG["alr"]
            wt_ref[a + B_GATE_RANK:b, :] = jnp.zeros((RANK_PAD - B_GATE_RANK, D_MODEL), BF16)
            wt_copy.start()

        def one_tile(rows):
            x = x_ref[rows, :]
            r = lax.rsqrt(jnp.mean(x * x, axis=-1, keepdims=True) + NORM_EPS)
            h = (x * r * g_ref[...]).astype(BF16)
            h_ref[rows, :] = h

            def seg(name):
                a, b = SEG[name]
                return _dot_nt(h, wt_ref[a:b, :])

            alr = seg("alr").astype(BF16)
            alr_ref[rows, :] = alr
            u = _dot(alr, wa_ref[...]) + ba_ref[...]
            u_ref[rows, :] = u
            log_a = (jnp.minimum(u, 0.0) - jnp.log(1.0 + jnp.exp(-jnp.abs(u)))) * (1.0 / B_GATE_TEMP)
            row, col = _row_iota((sub, sub)), _lane_iota((sub, sub))
            tri = ((row // B_CHUNK == col // B_CHUNK) & (col <= row)).astype(BF16)
            hi, mid, lo = _split3(log_a)
            cum_ref[rows, :] = _dot(tri, hi) + _dot(tri, mid) + _dot(tri, lo)

            cos, sin = cos_ref[rows, :], sin_ref[rows, :]
            qa = seg("qa") * ATT_SCALE
            for s in range(A_WIDTH // LANES):
                qkv_ref[rows, s * LANES:(s + 1) * LANES] = _rope_slab(qa[:, s * LANES:(s + 1) * LANES], cos, sin).astype(BF16)
            qkv_ref[rows, QKV_K:QKV_V] = _rope_slab(seg("ka"), cos, sin).astype(BF16)
            qkv_ref[rows, QKV_V:QKV_W] = seg("va").astype(BF16)
            za_ref[rows, :] = seg("za").astype(BF16)
            q_ref[rows, :] = seg("qb")
            k_ref[rows, :] = seg("kb")
            vb_ref[rows, :] = seg("vb").astype(BF16)
            zb_ref[rows, :] = seg("zb").astype(BF16)
            ga_ref[rows, :] = seg("ga").astype(BF16)
            gb_ref[rows, :] = seg("gb").astype(BF16)

        for j in range(tm // sub):
            one_tile(pl.ds(j * sub, sub))

        @pl.when(pl.program_id(0) == last)
        def _():
            for a in range(nl):
                for r, (dx, dy, dc) in enumerate(FLIPS):
                    wcopy(a, r, 4 * (px ^ dx) + 2 * (py ^ dy) + (pc ^ dc)).wait_recv()
                    wcopy(a, r, my_dev).wait_send()
                keep[a].wait()
            wt_copy.wait()

    def rows(w):
        return pl.BlockSpec((tm, w), lambda i: (i, 0))

    outs = [("h", D_MODEL, BF16), ("qkv", QKV_W, BF16), ("za", A_WIDTH, BF16), ("q", B_KEY_WIDTH, F32),
            ("k", B_KEY_WIDTH, F32), ("vb", B_WIDTH, BF16), ("zb", B_WIDTH, BF16), ("alr", RANK_PAD, BF16),
            ("u", B_KEY_WIDTH, F32), ("cum", B_KEY_WIDTH, F32), ("ga", D_MODEL, BF16), ("gb", D_MODEL, BF16)]
    res = pl.pallas_call(
        body, name="in_proj", grid=(T // tm,),
        in_specs=[rows(D_MODEL), rows(LANES), rows(LANES), _const_spec((1, D_MODEL)),
                  _const_spec((N_DEV * SHARD_PAD, D_MODEL)), _const_spec((RANK_PAD, B_KEY_WIDTH)),
                  _const_spec((1, B_KEY_WIDTH))] + _any_specs(nl),
        out_specs=[rows(w) for _, w, _ in outs] + _any_specs(1 + nl),
        out_shape=[jax.ShapeDtypeStruct((T, w), dt) for _, w, dt in outs]
                  + [jax.ShapeDtypeStruct((D_IN_PAD, D_MODEL), BF16)]
                  + [jax.ShapeDtypeStruct((N_DEV, *sh.shape), sh.dtype) for sh in later_shards],
        scratch_shapes=[pltpu.VMEM((D_IN_PAD, D_MODEL), BF16),
                        pltpu.SemaphoreType.DMA((nl, len(FLIPS))), pltpu.SemaphoreType.DMA((nl, len(FLIPS))),
                        pltpu.SemaphoreType.DMA((nl,)), pltpu.SemaphoreType.DMA],
        compiler_params=_params("arbitrary"),
    )(x2, cosf, sinf, g_in, wt_sh, wa_pad, b_alpha, *later_shards)
    n_out = len(outs) + 1
    return dict(zip([n for n, _, _ in outs] + ["wt_pad"], res[:n_out])), res[n_out:]


def _dup_kv_head(t, g):
    tf = t.astype(F32)
    keep = (_lane_iota(tf.shape) < A_HEAD_DIM) == (g == 0)
    return jnp.where(keep, tf, pltpu.roll(tf, A_HEAD_DIM, 1)).astype(BF16)


def _stack_heads(t):
    lo = _lane_iota(t.shape) < A_HEAD_DIM
    zero = jnp.zeros_like(t)
    return jnp.concatenate([jnp.where(lo, t, zero), jnp.where(lo, zero, t)], axis=0)


ATT_ROWS = A_HEADS * WINDOW
GROUP_ROWS = ATT_ROWS // A_KV_HEADS
HEADS_PER_GROUP = A_HEADS // A_KV_HEADS


def _band_mask_t(n):
    kj = _row_iota((2 * WINDOW, GROUP_ROWS)) - WINDOW
    qi = _lane_iota((2 * WINDOW, GROUP_ROWS)) % WINDOW
    return (kj <= qi) & (qi - kj < WINDOW) & ((n > 0) | (kj >= 0))


def _stacked_queries(ref, g):
    pairs = range(g * HEADS_PER_GROUP // 2, (g + 1) * HEADS_PER_GROUP // 2)
    return jnp.concatenate([_stack_heads(ref[:, p * LANES:(p + 1) * LANES]) for p in pairs], axis=0)


def _unstack_heads(t, g, ref, dtype):
    lo = _lane_iota((WINDOW, LANES)) < A_HEAD_DIM
    for hh in range(HEADS_PER_GROUP // 2):
        p = g * HEADS_PER_GROUP // 2 + hh
        ref[:, p * LANES:(p + 1) * LANES] = jnp.where(lo, t[2 * hh * WINDOW:(2 * hh + 1) * WINDOW],
                                                       t[(2 * hh + 1) * WINDOW:(2 * hh + 2) * WINDOW]).astype(dtype)


FWD_BLOCKS = 16


def _attn_fwd(qkv, sink_row, B, S):
    T = B * S
    nb = S // WINDOW
    blocks = math.gcd(nb, FWD_BLOCKS)
    steps = nb // blocks

    def one_block(has_prev, sink_ref, q, k, v, o_ref, lse_ref):
        valid = _band_mask_t(has_prev)
        lse_rows = []
        for g in range(A_KV_HEADS):
            kd, vd = _dup_kv_head(k, g), _dup_kv_head(v, g)
            s = jnp.where(valid, _dot_nt(kd, _stacked_queries(q, g)), NEG_BIG)
            sink = sink_ref[:, g * GROUP_ROWS:(g + 1) * GROUP_ROWS]
            m = jnp.maximum(jnp.max(s, axis=0, keepdims=True), sink)
            e = jnp.exp(s - m)
            den = jnp.sum(e, axis=0, keepdims=True) + jnp.exp(sink - m)
            o = _dot_tn((e * (1.0 / den)).astype(BF16), vd)
            _unstack_heads(o, g, o_ref, F32)
            lse = m + jnp.log(den)
            lse_rows += [lse[:, j * WINDOW:(j + 1) * WINDOW] for j in range(HEADS_PER_GROUP)]
        by_head = jnp.concatenate(lse_rows + [jnp.zeros((WINDOW - A_HEADS, WINDOW), F32)], axis=0)
        lse_ref[...] = by_head.T

    def body(sink_ref, q_ref, kc_ref, vc_ref, kp_ref, vp_ref, o_ref, lse_ref):
        k_all = jnp.concatenate([kp_ref[...], kc_ref[...]], axis=0)
        v_all = jnp.concatenate([vp_ref[...], vc_ref[...]], axis=0)
        for j in range(blocks):
            rows = pl.ds(j * WINDOW, WINDOW)
            keys = slice(j * WINDOW, (j + 2) * WINDOW)
            has_prev = pl.program_id(1) if j == 0 else 1
            one_block(has_prev, sink_ref, q_ref[rows, :], k_all[keys], v_all[keys], o_ref.at[rows], lse_ref.at[rows])

    def cur(col, w):
        return pl.BlockSpec((blocks * WINDOW, w), lambda b, n: (b * steps + n, col))

    def prev(col):
        return pl.BlockSpec((WINDOW, LANES), lambda b, n: (b * nb + jnp.maximum(blocks * n - 1, 0), col))

    kcol, vcol = QKV_K // LANES, QKV_V // LANES
    return pl.pallas_call(
        body, name="attn_fwd", grid=(B, steps),
        in_specs=[_const_spec((1, ATT_ROWS)), cur(0, A_WIDTH), cur(kcol, LANES), cur(vcol, LANES), prev(kcol), prev(vcol)],
        out_specs=[cur(0, A_WIDTH), cur(0, LANES)],
        out_shape=[jax.ShapeDtypeStruct((T, A_WIDTH), F32), jax.ShapeDtypeStruct((T, LANES), F32)],
        compiler_params=_params("parallel", "parallel"),
    )(sink_row, qkv, qkv, qkv, qkv, qkv)


ATT_CHUNK = 128


def _chunk_masks(n):
    masks = []
    for half in range(WINDOW // ATT_CHUNK):
        qi = _row_iota((ATT_CHUNK, 2 * WINDOW)) + half * ATT_CHUNK
        kj = _lane_iota((ATT_CHUNK, 2 * WINDOW)) - WINDOW
        masks.append((kj <= qi) & (qi - kj < WINDOW) & ((n > 0) | (kj >= 0)))
    return masks


def _all_stacked_queries(ref):
    return jnp.concatenate([_stacked_queries(ref, g) for g in range(A_KV_HEADS)], axis=0)


def _by_group(fn, lhs, rhs_per_group):
    return jnp.concatenate([fn(lhs[g * GROUP_ROWS:(g + 1) * GROUP_ROWS], rhs_per_group[g])
                            for g in range(A_KV_HEADS)], axis=0)


BWD_BLOCKS = 8


def _attn_bwd(qkv, do, out, lse, sink_col, B, S):
    T = B * S
    nb = S // WINDOW
    M = math.gcd(nb, BWD_BLOCKS)
    steps = nb // M
    n_chunks = ATT_ROWS // ATT_CHUNK
    halves = WINDOW // ATT_CHUNK

    def block(has_prev, sink_ref, q, do_b, out_b, lse_b, k, v, scratch, want_dq):
        s_ref, dp_ref, ds_ref, p_ref = scratch
        width = k.shape[0]
        masks = [mk[:, 0:width] for mk in _chunk_masks(has_prev)]
        kd = [_dup_kv_head(k, g) for g in range(A_KV_HEADS)]
        vd = [_dup_kv_head(v, g) for g in range(A_KV_HEADS)]
        qs, dos = _all_stacked_queries(q), _all_stacked_queries(do_b)
        s_ref[...] = _by_group(_dot_nt, qs, kd)
        dp_ref[...] = _by_group(_dot_nt, dos, vd)
        lane = _lane_iota((ATT_CHUNK, LANES))
        lo = lane < A_HEAD_DIM
        lane1 = _lane_iota((1, LANES))
        dsink_row = jnp.zeros((1, LANES), F32)
        for c in range(n_chunks):
            rows = slice(c * ATT_CHUNK, (c + 1) * ATT_CHUNK)
            head, half = divmod(c, halves)
            qrows = slice(half * ATT_CHUNK, (half + 1) * ATT_CHUNK)
            slab = slice((head // 2) * LANES, (head // 2 + 1) * LANES)
            lse_col = jnp.sum(jnp.where(lane == head, lse_b[qrows, :], 0.0), axis=-1, keepdims=True)
            prod = do_b[qrows, slab].astype(F32) * out_b[qrows, slab].astype(F32)
            mine = lo if head % 2 == 0 else jnp.logical_not(lo)
            delta = jnp.sum(jnp.where(mine, prod, 0.0), axis=-1, keepdims=True)
            prob = jnp.exp(jnp.where(masks[half], s_ref[rows, :], NEG_BIG) - lse_col)
            p_ref[rows, :] = prob.astype(BF16)
            ds_ref[rows, :] = (prob * (dp_ref[rows, :] - delta)).astype(BF16)
            w = -jnp.exp(sink_ref[rows, :] - lse_col) * delta
            dsink_row += jnp.where(lane1 == head, jnp.sum(w, axis=0, keepdims=True), 0.0)
        dq = _by_group(_dot, ds_ref[...], kd) * ATT_SCALE if want_dq else None
        groups = [slice(g * GROUP_ROWS, (g + 1) * GROUP_ROWS) for g in range(A_KV_HEADS)]
        dk = [_dot_tn(ds_ref[rows, :], qs[rows]) for rows in groups]
        dv = [_dot_tn(p_ref[rows, :], dos[rows]) for rows in groups]
        return dq, dk, dv, dsink_row

    def fold(per_group):
        lane = _lane_iota((WINDOW, LANES))
        out = jnp.zeros((WINDOW, LANES), F32)
        for g, acc in enumerate(per_group):
            out = jnp.where((lane < A_HEAD_DIM) == (g == 0), acc + pltpu.roll(acc, A_HEAD_DIM, 1), out)
        return out

    def body(sink_ref, q_ref, qn_ref, do_ref, don_ref, out_ref, outn_ref, lse_ref, lsen_ref, kc_ref, kp_ref, vc_ref, vp_ref,
             dq_ref, dkv_ref, dsink_ref, s_scr, dp_scr, ds_scr, p_scr, s_x, dp_x, ds_x, p_x):
        b, m = pl.program_id(0), pl.program_id(1)

        @pl.when((b == 0) & (m == 0))
        def _():
            dsink_ref[...] = jnp.zeros_like(dsink_ref)

        k_all = jnp.concatenate([kp_ref[...], kc_ref[...]], axis=0)
        v_all = jnp.concatenate([vp_ref[...], vc_ref[...]], axis=0)
        results = []
        for j in range(M):
            rows = slice(j * WINDOW, (j + 1) * WINDOW)
            keys = slice(j * WINDOW, (j + 2) * WINDOW)
            has_prev = m if j == 0 else 1
            results.append(block(has_prev, sink_ref, q_ref[rows, :], do_ref[rows, :], out_ref[rows, :], lse_ref[rows, :],
                                 k_all[keys], v_all[keys], (s_scr.at[j], dp_scr.at[j], ds_scr.at[j], p_scr.at[j]), True))
        last_keys = slice(M * WINDOW, (M + 1) * WINDOW)
        _, dk_x, dv_x, _ = block(1, sink_ref, qn_ref[...], don_ref[...], outn_ref[...], lsen_ref[...],
                                 k_all[last_keys], v_all[last_keys], (s_x, dp_x, ds_x, p_x), False)
        has_next = m < steps - 1
        lo_q = _lane_iota((WINDOW, LANES)) < A_HEAD_DIM
        dsink_row = jnp.zeros((1, LANES), F32)
        for j, (dq, dk, dv, ds_row) in enumerate(results):
            rows = slice(j * WINDOW, (j + 1) * WINDOW)
            for p in range(A_HEADS // 2):
                dq_ref[rows, p * LANES:(p + 1) * LANES] = jnp.where(
                    lo_q, dq[2 * p * WINDOW:(2 * p + 1) * WINDOW], dq[(2 * p + 1) * WINDOW:(2 * p + 2) * WINDOW]).astype(BF16)
            if j + 1 < M:
                dk_next = [t[0:WINDOW] for t in results[j + 1][1]]
                dv_next = [t[0:WINDOW] for t in results[j + 1][2]]
            else:
                dk_next = [jnp.where(has_next, t, 0.0) for t in dk_x]
                dv_next = [jnp.where(has_next, t, 0.0) for t in dv_x]
            dkv_ref[rows, 0:LANES] = fold([own[WINDOW:] + nxt for own, nxt in zip(dk, dk_next)]).astype(BF16)
            dkv_ref[rows, LANES:] = fold([own[WINDOW:] + nxt for own, nxt in zip(dv, dv_next)]).astype(BF16)
            dsink_row += ds_row
        dsink_ref[0:1, :] += dsink_row

    def cur(col, w):
        return pl.BlockSpec((M * WINDOW, w), lambda b, m: (b * steps + m, col))

    def nxt(col, w):
        return pl.BlockSpec((WINDOW, w), lambda b, m: (b * nb + jnp.minimum(M * (m + 1), nb - 1), col))

    def prev(col):
        return pl.BlockSpec((WINDOW, LANES), lambda b, m: (b * nb + jnp.maximum(M * m - 1, 0), col))

    kcol, vcol = QKV_K // LANES, QKV_V // LANES
    scores = (M, ATT_ROWS, 2 * WINDOW)
    extra = (ATT_ROWS, WINDOW)
    return pl.pallas_call(
        body, name="attn_bwd", grid=(B, steps),
        in_specs=[_const_spec((ATT_ROWS, 1)), cur(0, A_WIDTH), nxt(0, A_WIDTH), cur(0, A_WIDTH), nxt(0, A_WIDTH),
                  cur(0, A_WIDTH), nxt(0, A_WIDTH), cur(0, LANES), nxt(0, LANES),
                  cur(kcol, LANES), prev(kcol), cur(vcol, LANES), prev(vcol)],
        out_specs=[cur(0, A_WIDTH), cur(0, 2 * LANES), pl.BlockSpec((8, LANES), lambda b, m: (0, 0))],
        out_shape=[jax.ShapeDtypeStruct((T, A_WIDTH), BF16), jax.ShapeDtypeStruct((T, 2 * LANES), BF16),
                   jax.ShapeDtypeStruct((8, LANES), F32)],
        scratch_shapes=[pltpu.VMEM(scores, F32), pltpu.VMEM(scores, F32), pltpu.VMEM(scores, BF16), pltpu.VMEM(scores, BF16),
                        pltpu.VMEM(extra, F32), pltpu.VMEM(extra, F32), pltpu.VMEM(extra, BF16), pltpu.VMEM(extra, BF16)],
        compiler_params=_params("arbitrary", "arbitrary"),
    )(sink_col, qkv, qkv, do, do, out, out, lse, lse, qkv, qkv, qkv, qkv)


GLA_FWD_TILING = (64, 16)
GLA_BWD_TILING = (256, 4)


def _gla_factors(q_ref, k_ref, cum_ref):
    cpt = q_ref.shape[0] // B_CHUNK
    scale = B_KEY_DIM ** -0.5
    cum = cum_ref[...]
    shape = (B_CHUNK, B_KEY_WIDTH)
    last = jnp.concatenate([jnp.broadcast_to(cum_ref[pl.ds(c * B_CHUNK + B_CHUNK - 1, 1), :], shape)
                            for c in range(cpt)], axis=0)
    mid = jnp.concatenate([jnp.broadcast_to(cum_ref[pl.ds(c * B_CHUNK + B_CHUNK // 2 - 1, 1), :], shape)
                           for c in range(cpt)], axis=0)
    e_qm, e_km, e_qe, e_kd = jnp.exp(cum - mid), jnp.exp(mid - cum), jnp.exp(cum), jnp.exp(last - cum)
    qs = q_ref[...] * scale
    k = k_ref[...]
    return qs, k, (e_qm, e_km, e_qe, e_kd)


def _head_mask(shape, h):
    return (_lane_iota(shape) // B_KEY_DIM) == h


def _stack_masked(t):
    return jnp.concatenate([jnp.where(_head_mask(t.shape, h), t, 0.0) for h in range(B_HEADS)], axis=0).astype(BF16)


def _select_heads(t):
    shape = (B_CHUNK, B_KEY_WIDTH)
    out = jnp.zeros(shape, F32)
    for h in range(B_HEADS):
        out = jnp.where(_head_mask(shape, h), t[h * B_CHUNK:(h + 1) * B_CHUNK], out)
    return out


def _select_state(t):
    shape = (B_VAL_DIM, B_KEY_WIDTH)
    out = jnp.zeros(shape, F32)
    for h in range(B_HEADS):
        out = jnp.where(_head_mask(shape, h), t[h * B_VAL_DIM:(h + 1) * B_VAL_DIM], out)
    return out


def _rows_by_head(t):
    return jnp.concatenate([t[:, h * B_VAL_DIM:(h + 1) * B_VAL_DIM] for h in range(B_HEADS)], axis=0)


def _intra_mask(tile_rows):
    i, j = _row_iota((tile_rows, tile_rows)), _lane_iota((tile_rows, tile_rows))
    return (i // B_CHUNK == j // B_CHUNK) & (j <= i)


def _pair_stack(t, p):
    slab = t[:, p * LANES:(p + 1) * LANES]
    lo = _lane_iota(slab.shape) < B_KEY_DIM
    return jnp.concatenate([jnp.where(lo, slab, 0.0), jnp.where(lo, 0.0, slab)], axis=0).astype(BF16)


def _gla_fwd(q, k, cum, vb, B, S):
    T = B * S
    tile_rows = math.gcd(S, GLA_FWD_TILING[0])
    cpt = tile_rows // B_CHUNK
    nt = S // tile_rows
    tps = math.gcd(nt, GLA_FWD_TILING[1])

    def one_sequence(q_ref, k_ref, cum_ref, v_ref, o_ref, st_all_ref, st_ref):
        qs, kk, (e_qm, e_km, e_qe, e_kd) = _gla_factors(q_ref, k_ref, cum_ref)
        qm, km, qe, kd = qs * e_qm, kk * e_km, qs * e_qe, (kk * e_kd).astype(BF16)
        mask = _intra_mask(tile_rows)
        intra = []
        for p in range(B_HEADS // 2):
            a = _dot_nt(_pair_stack(qm, p), km[:, p * LANES:(p + 1) * LANES].astype(BF16))
            for hh in range(2):
                h = 2 * p + hh
                att = jnp.where(mask, a[hh * tile_rows:(hh + 1) * tile_rows], 0.0).astype(BF16)
                intra.append(_dot(att, v_ref[:, h * B_VAL_DIM:(h + 1) * B_VAL_DIM]))
        inter = []
        for c in range(cpt):
            rows = slice(c * B_CHUNK, (c + 1) * B_CHUNK)
            st = st_ref[...]
            st_all_ref[c] = st
            inter.append(_dot_nt(_stack_masked(qe[rows]), st.astype(BF16)))
            inc = _select_state(_dot_tn(v_ref[rows, :], kd[rows]))
            decay = jnp.exp(cum_ref[pl.ds(c * B_CHUNK + B_CHUNK - 1, 1), :])
            st_ref[...] = st * decay + inc
        for h in range(B_HEADS):
            oi = jnp.concatenate([inter[c][h * B_CHUNK:(h + 1) * B_CHUNK] for c in range(cpt)], axis=0)
            o_ref[:, h * B_VAL_DIM:(h + 1) * B_VAL_DIM] = (intra[h] + oi).astype(BF16)

    def body(q_ref, k_ref, cum_ref, v_ref, o_ref, st_all_ref, st_ref):
        @pl.when(pl.program_id(0) == 0)
        def _():
            st_ref[...] = jnp.zeros_like(st_ref)

        for b in range(B):
            for tile in range(tps):
                tok = pl.ds(tile * tile_rows, tile_rows)
                chunks = pl.ds(tile * cpt, cpt)
                one_sequence(*[r.at[b, tok] for r in (q_ref, k_ref, cum_ref, v_ref, o_ref)],
                             st_all_ref.at[b, chunks], st_ref.at[b])

    def rows(w):
        return pl.BlockSpec((B, tps * tile_rows, w), lambda t: (0, t, 0))

    seq = lambda a: a.reshape(B, S, a.shape[-1])
    o, st_all = pl.pallas_call(
        body, name="gla_fwd", grid=(nt // tps,),
        in_specs=[rows(B_KEY_WIDTH), rows(B_KEY_WIDTH), rows(B_KEY_WIDTH), rows(B_WIDTH)],
        out_specs=[rows(B_WIDTH),
                   pl.BlockSpec((B, tps * cpt, B_VAL_DIM, B_KEY_WIDTH), lambda t: (0, t, 0, 0))],
        out_shape=[jax.ShapeDtypeStruct((B, S, B_WIDTH), BF16),
                   jax.ShapeDtypeStruct((B, S // B_CHUNK, B_VAL_DIM, B_KEY_WIDTH), F32)],
        scratch_shapes=[pltpu.VMEM((B, B_VAL_DIM, B_KEY_WIDTH), F32)],
        compiler_params=_params("arbitrary"),
    )(seq(q), seq(k), seq(cum), seq(vb))
    return o.reshape(T, B_WIDTH), st_all.reshape(T // B_CHUNK, B_VAL_DIM, B_KEY_WIDTH)


def _gla_bwd(q, k, cum, vb, do, st_all, B, S):
    T = B * S
    tile_rows = math.gcd(S, GLA_BWD_TILING[0])
    cpt = tile_rows // B_CHUNK
    nt = S // tile_rows
    tps = math.gcd(nt, GLA_BWD_TILING[1])
    steps = nt // tps
    scale = B_KEY_DIM ** -0.5

    def one_sequence(q_ref, k_ref, cum_ref, v_ref, do_ref, st_all_ref, dq_ref, dk_ref, dv_ref, dla_ref, dst_ref):
        qs, kk, (e_qm, e_km, e_qe, e_kd) = _gla_factors(q_ref, k_ref, cum_ref)
        qm, km, qe, kd = qs * e_qm, kk * e_km, qs * e_qe, kk * e_kd
        mask = _intra_mask(tile_rows)
        dqm_slabs, dkm_slabs, dv_intra = [], [], []
        for p in range(B_HEADS // 2):
            qm_st = _pair_stack(qm, p)
            km_p = km[:, p * LANES:(p + 1) * LANES].astype(BF16)
            a = _dot_nt(qm_st, km_p)
            da_blocks, dqm_h = [], []
            for hh in range(2):
                h = 2 * p + hh
                vs = slice(h * B_VAL_DIM, (h + 1) * B_VAL_DIM)
                att = jnp.where(mask, a[hh * tile_rows:(hh + 1) * tile_rows], 0.0).astype(BF16)
                dv_intra.append(_dot_tn(att, do_ref[:, vs]))
                da = jnp.where(mask, _dot_nt(do_ref[:, vs], v_ref[:, vs]), 0.0).astype(BF16)
                da_blocks.append(da)
                dqm_h.append(_dot(da, km_p))
            lo = _lane_iota((tile_rows, LANES)) < B_KEY_DIM
            dqm_slabs.append(jnp.where(lo, dqm_h[0], dqm_h[1]))
            dkm_slabs.append(_dot_tn(jnp.concatenate(da_blocks, axis=0), qm_st))
        dqm = jnp.concatenate(dqm_slabs, axis=1)
        dkm = jnp.concatenate(dkm_slabs, axis=1)

        dqe_c, dkd_c, dv_inter, tail_c = ([None] * cpt for _ in range(4))
        for c in reversed(range(cpt)):
            rows = slice(c * B_CHUNK, (c + 1) * B_CHUNK)
            dst = dst_ref[...]
            dst_b = dst.astype(BF16)
            dv_inter[c] = _dot_nt(_stack_masked(kd[rows]), dst_b)
            dkd_c[c] = _select_heads(_dot(_rows_by_head(v_ref[rows, :]), dst_b))
            do_c = do_ref[rows, :]
            dqe_c[c] = _select_heads(_dot(_rows_by_head(do_c), st_all_ref[c].astype(BF16)))
            contrib = _select_state(_dot_tn(do_c, qe[rows].astype(BF16)))
            decay = jnp.exp(cum_ref[pl.ds(c * B_CHUNK + B_CHUNK - 1, 1), :])
            tail = (jnp.sum(kk[rows] * dkd_c[c] * e_kd[rows], axis=0, keepdims=True)
                    + decay * jnp.sum(st_all_ref[c] * dst, axis=0, keepdims=True))
            tail_c[c] = jnp.broadcast_to(tail, (B_CHUNK, B_KEY_WIDTH))
            dst_ref[...] = dst * decay + contrib
        dqe = jnp.concatenate(dqe_c, axis=0)
        dkd = jnp.concatenate(dkd_c, axis=0)
        dqs = dqm * e_qm + dqe * e_qe
        dk = dkm * e_km + dkd * e_kd
        dq_ref[...] = (dqs * scale).astype(BF16)
        dk_ref[...] = dk.astype(BF16)
        for h in range(B_HEADS):
            dvi = jnp.concatenate([dv_inter[c][h * B_CHUNK:(h + 1) * B_CHUNK] for c in range(cpt)], axis=0)
            dv_ref[:, h * B_VAL_DIM:(h + 1) * B_VAL_DIM] = (dv_intra[h] + dvi).astype(BF16)
        dd = qs * dqs - kk * dk
        i, j = _row_iota((tile_rows, tile_rows)), _lane_iota((tile_rows, tile_rows))
        upper = ((i // B_CHUNK == j // B_CHUNK) & (j >= i)).astype(BF16)
        hi, mid, lo3 = _split3(dd)
        dla_ref[...] = _dot(upper, hi) + _dot(upper, mid) + _dot(upper, lo3) + jnp.concatenate(tail_c, axis=0)

    def body(q_ref, k_ref, cum_ref, v_ref, do_ref, st_all_ref, dq_ref, dk_ref, dv_ref, dla_ref, dst_ref):
        @pl.when(pl.program_id(0) == 0)
        def _():
            dst_ref[...] = jnp.zeros_like(dst_ref)

        for b in range(B):
            for tile in reversed(range(tps)):
                tok = pl.ds(tile * tile_rows, tile_rows)
                chunks = pl.ds(tile * cpt, cpt)
                one_sequence(*[r.at[b, tok] for r in (q_ref, k_ref, cum_ref, v_ref, do_ref)], st_all_ref.at[b, chunks],
                             *[r.at[b, tok] for r in (dq_ref, dk_ref, dv_ref, dla_ref)], dst_ref.at[b])

    def rows(w):
        return pl.BlockSpec((B, tps * tile_rows, w), lambda t: (0, steps - 1 - t, 0))

    seq = lambda a: a.reshape(B, S, a.shape[-1])
    res = pl.pallas_call(
        body, name="gla_bwd", grid=(steps,),
        in_specs=[rows(B_KEY_WIDTH), rows(B_KEY_WIDTH), rows(B_KEY_WIDTH), rows(B_WIDTH), rows(B_WIDTH),
                  pl.BlockSpec((B, tps * cpt, B_VAL_DIM, B_KEY_WIDTH), lambda t: (0, steps - 1 - t, 0, 0))],
        out_specs=[rows(B_KEY_WIDTH), rows(B_KEY_WIDTH), rows(B_WIDTH), rows(B_KEY_WIDTH)],
        out_shape=[jax.ShapeDtypeStruct((B, S, B_KEY_WIDTH), BF16), jax.ShapeDtypeStruct((B, S, B_KEY_WIDTH), BF16),
                   jax.ShapeDtypeStruct((B, S, B_WIDTH), BF16), jax.ShapeDtypeStruct((B, S, B_KEY_WIDTH), F32)],
        scratch_shapes=[pltpu.VMEM((B, B_VAL_DIM, B_KEY_WIDTH), F32)],
        compiler_params=_params("arbitrary"),
    )(seq(q), seq(k), seq(cum), seq(vb), seq(do), st_all.reshape(B, S // B_CHUNK, B_VAL_DIM, B_KEY_WIDTH))
    return [a.reshape(T, a.shape[-1]) for a in res]


def _merge(x2, tgt2, attn, za, o_gla, zb, ga, gb, w_oa_sh, w_ob_sh, w_o, g_gla, g_final):
    T = x2.shape[0]
    tm = math.gcd(T, 512)
    sub = math.gcd(tm, 256)
    last = T // tm - 1

    def body(x_ref, tgt_ref, attn_ref, za_ref, og_ref, zb_ref, ga_ref, gb_ref,
             woa_sh_ref, wob_sh_ref, wo_ref, gg_ref, gf_ref,
             dxres_ref, dattn_ref, dog_ref, dza_ref, dzb_ref, dga_ref, dgb_ref,
             dwo_out, dwoa_out, dwob_out, small_ref,
             awo_ref, awoa_ref, awob_ref, agf_ref, agg_ref, loss_ref, woa_ref, wob_ref,
             dwo_ref, dwoa_ref, dwob_ref, w_sems, dw_sems):
        w_copies = [pltpu.make_async_copy(sh.at[j], dst.at[:, j * SHARD_OUT:(j + 1) * SHARD_OUT], w_sems.at[a, j])
                    for a, (sh, dst) in enumerate(((woa_sh_ref, woa_ref), (wob_sh_ref, wob_ref))) for j in range(N_DEV)]
        dw_copies = [pltpu.make_async_copy(src, dst, dw_sems.at[a])
                     for a, (src, dst) in enumerate(((dwo_ref, dwo_out), (dwoa_ref, dwoa_out), (dwob_ref, dwob_out)))]

        @pl.when(pl.program_id(0) == 0)
        def _():
            for cp in w_copies:
                cp.start()
            for r in (awo_ref, awoa_ref, awob_ref, agf_ref, agg_ref, loss_ref):
                r[...] = jnp.zeros_like(r)
            for cp in w_copies:
                cp.wait()

        def one_tile(rows):
            za_v = za_ref[rows, :].astype(F32)
            sig_za = _sigmoid_tanh(za_v)
            silu_a = za_v * sig_za
            attn_v = attn_ref[rows, :].astype(F32)
            oa = (attn_v * silu_a).astype(BF16)
            ya = _dot(oa, woa_ref[...])
            og = og_ref[rows, :].astype(F32)
            zb_v = zb_ref[rows, :].astype(F32)
            sig_zb = _sigmoid_tanh(zb_v)
            silu_b = zb_v * sig_zb
            gg = gg_ref[...]
            on_parts, rinv_parts = [], []
            for h in range(B_HEADS):
                seg = og[:, h * B_VAL_DIM:(h + 1) * B_VAL_DIM]
                rinv = lax.rsqrt(jnp.mean(seg * seg, axis=-1, keepdims=True) + NORM_EPS)
                rinv_parts.append(rinv)
                on_parts.append(seg * rinv)
            on = jnp.concatenate(on_parts, axis=1)
            obn = on * gg
            ob = (obn * silu_b).astype(BF16)
            yb = _dot(ob, wob_ref[...])
            sig_a = _sigmoid_tanh(ga_ref[rows, :].astype(F32))
            sig_b = _sigmoid_tanh(gb_ref[rows, :].astype(F32))
            merged = (sig_a * ya + sig_b * yb).astype(BF16)
            out = x_ref[rows, :] + _dot(merged, wo_ref[...])
            rf = lax.rsqrt(jnp.mean(out * out, axis=-1, keepdims=True) + NORM_EPS)
            nrm = out * rf
            gf = gf_ref[...]
            err = nrm * gf - tgt_ref[rows, :]
            loss = jnp.sum(err * err) * (0.5 / D_MODEL)

            dy = err * (1.0 / D_MODEL)
            dgf = jnp.sum(dy * nrm, axis=0, keepdims=True)
            dn = dy * gf
            dout = rf * (dn - nrm * jnp.mean(dn * nrm, axis=-1, keepdims=True))
            dxres_ref[rows, :] = dout
            dout_b = dout.astype(BF16)
            dmerged = _dot_nt(dout_b, wo_ref[...])
            dya = dmerged * sig_a
            dyb = dmerged * sig_b
            dga_ref[rows, :] = (dmerged * ya * sig_a * (1.0 - sig_a)).astype(BF16)
            dgb_ref[rows, :] = (dmerged * yb * sig_b * (1.0 - sig_b)).astype(BF16)
            dya_b, dyb_b = dya.astype(BF16), dyb.astype(BF16)
            doa = _dot_nt(dya_b, woa_ref[...])
            dattn_ref[rows, :] = (doa * silu_a).astype(BF16)
            dza_ref[rows, :] = (doa * attn_v * (sig_za * (1.0 + za_v * (1.0 - sig_za)))).astype(BF16)
            dob = _dot_nt(dyb_b, wob_ref[...])
            dzb_ref[rows, :] = (dob * obn * (sig_zb * (1.0 + zb_v * (1.0 - sig_zb)))).astype(BF16)
            dobn = dob * silu_b
            dgg = jnp.sum(dobn * on, axis=0, keepdims=True)
            don = dobn * gg
            for h in range(B_HEADS):
                sl = slice(h * B_VAL_DIM, (h + 1) * B_VAL_DIM)
                don_h, on_h = don[:, sl], on[:, sl]
                dog_ref[rows, sl] = (rinv_parts[h] * (don_h - on_h * jnp.mean(don_h * on_h, axis=-1, keepdims=True))
                                     ).astype(BF16)
            return (merged, dout_b, oa, dya_b, ob, dyb_b), (loss, dgf, dgg)

        tiles = [one_tile(pl.ds(j * sub, sub)) for j in range(tm // sub)]
        merged, dout_b, oa, dya_b, ob, dyb_b = (jnp.concatenate(parts, axis=0) for parts in zip(*[t[0] for t in tiles]))
        awo_ref[...] += _dot_tn(merged, dout_b)
        awoa_ref[...] += _dot_tn(oa, dya_b)
        awob_ref[...] += _dot_tn(ob, dyb_b)
        for _, (loss, dgf, dgg) in tiles:
            loss_ref[...] += loss
            agf_ref[...] += dgf
            agg_ref[...] += dgg

        @pl.when(pl.program_id(0) == last)
        def _():
            for j in range(N_DEV):
                dwo_ref[j] = awo_ref[j * SHARD_OUT:(j + 1) * SHARD_OUT, :].astype(BF16)
                dwoa_ref[j] = awoa_ref[:, j * SHARD_OUT:(j + 1) * SHARD_OUT].astype(BF16)
                dwob_ref[j] = awob_ref[:, j * SHARD_OUT:(j + 1) * SHARD_OUT].astype(BF16)
            small_ref[...] = jnp.zeros_like(small_ref)
            _put_rows(small_ref, SMALL_G_FINAL, agf_ref[...])
            _put_rows(small_ref, SMALL_G_GLA, agg_ref[...])
            small_ref[SMALL_LOSS:SMALL_LOSS + 1, :] = loss_ref[...]
            for cp in dw_copies:
                cp.start()
            for cp in dw_copies:
                cp.wait()

    def rows(w):
        return pl.BlockSpec((tm, w), lambda i: (i, 0))

    def whole(shape):
        nd = len(shape)
        return pl.BlockSpec(shape, lambda i: (0,) * nd)

    outs = [((T, D_MODEL), F32, rows(D_MODEL)), ((T, A_WIDTH), BF16, rows(A_WIDTH)), ((T, B_WIDTH), BF16, rows(B_WIDTH)),
            ((T, A_WIDTH), BF16, rows(A_WIDTH)), ((T, B_WIDTH), BF16, rows(B_WIDTH)),
            ((T, D_MODEL), BF16, rows(D_MODEL)), ((T, D_MODEL), BF16, rows(D_MODEL)),
            ((N_DEV, SHARD_OUT, D_MODEL), BF16, pl.BlockSpec(memory_space=pl.ANY)),
            ((N_DEV, A_WIDTH, SHARD_OUT), BF16, pl.BlockSpec(memory_space=pl.ANY)),
            ((N_DEV, B_WIDTH, SHARD_OUT), BF16, pl.BlockSpec(memory_space=pl.ANY)),
            ((SMALL_SINKS, LANES), F32, whole((SMALL_SINKS, LANES)))]
    return pl.pallas_call(
        body, name="merge", grid=(T // tm,),
        in_specs=[rows(D_MODEL), rows(D_MODEL), rows(A_WIDTH), rows(A_WIDTH), rows(B_WIDTH), rows(B_WIDTH),
                  rows(D_MODEL), rows(D_MODEL),
                  pl.BlockSpec(memory_space=pl.ANY), pl.BlockSpec(memory_space=pl.ANY),
                  _const_spec((D_MODEL, D_MODEL)), _const_spec((1, B_WIDTH)), _const_spec((1, D_MODEL))],
        out_specs=[o[2] for o in outs],
        out_shape=[jax.ShapeDtypeStruct(o[0], o[1]) for o in outs],
        scratch_shapes=[pltpu.VMEM((D_MODEL, D_MODEL), F32), pltpu.VMEM((A_WIDTH, D_MODEL), F32),
                        pltpu.VMEM((B_WIDTH, D_MODEL), F32), pltpu.VMEM((1, D_MODEL), F32), pltpu.VMEM((1, B_WIDTH), F32),
                        pltpu.VMEM((1, LANES), F32), pltpu.VMEM((A_WIDTH, D_MODEL), BF16),
                        pltpu.VMEM((B_WIDTH, D_MODEL), BF16),
                        pltpu.VMEM((N_DEV, SHARD_OUT, D_MODEL), BF16), pltpu.VMEM((N_DEV, A_WIDTH, SHARD_OUT), BF16),
                        pltpu.VMEM((N_DEV, B_WIDTH, SHARD_OUT), BF16),
                        pltpu.SemaphoreType.DMA((2, N_DEV)), pltpu.SemaphoreType.DMA((3,))],
        compiler_params=pltpu.CompilerParams(dimension_semantics=("arbitrary",), vmem_limit_bytes=V7X_VMEM_LIMIT_MAX),
    )(x2, tgt2, attn, za, o_gla, zb, ga, gb, w_oa_sh, w_ob_sh, w_o, g_gla, g_final)


def _in_proj_bwd(x2, dxres, cosf, sinf, g_in, wt_pad, wa_pad, parts):
    T = x2.shape[0]
    tm = math.gcd(T, 512)
    sub = math.gcd(tm, 256)
    last = T // tm - 1
    base = SMALL_G_IN

    def body(x_ref, dxres_ref, cos_ref, sin_ref, g_ref, wt_ref, wa_ref,
             dq_ref, dkv_ref, dza_ref, dqb_ref, dkb_ref, dvb_ref, dzb_ref, dla_ref, u_ref, alr_ref, dga_ref, dgb_ref,
             dx_ref, dsh_ref, small_ref, dproj_ref, agin_ref, aba_ref, awa_ref):
        @pl.when(pl.program_id(0) == 0)
        def _():
            for r in (agin_ref, aba_ref, awa_ref):
                r[...] = jnp.zeros_like(r)

        def one_tile(rows):
            cos, nsin = cos_ref[rows, :], -sin_ref[rows, :]
            for s in range(A_WIDTH // LANES):
                sl = slice(s * LANES, (s + 1) * LANES)
                dproj_ref[rows, sl] = _rope_slab(dq_ref[rows, sl].astype(F32), cos, nsin).astype(BF16)
            dproj_ref[rows, QKV_K:QKV_V] = _rope_slab(dkv_ref[rows, 0:LANES].astype(F32), cos, nsin).astype(BF16)
            dproj_ref[rows, QKV_V:QKV_W] = dkv_ref[rows, LANES:]

            def put(name, val):
                a, b = SEG[name]
                dproj_ref[rows, a:b] = val

            put("za", dza_ref[rows, :])
            put("qb", dqb_ref[rows, :])
            put("kb", dkb_ref[rows, :])
            put("vb", dvb_ref[rows, :])
            put("zb", dzb_ref[rows, :])
            put("ga", dga_ref[rows, :])
            put("gb", dgb_ref[rows, :])
            du = dla_ref[rows, :] * (1.0 / B_GATE_TEMP) * _sigmoid(-u_ref[rows, :])
            du_b = du.astype(BF16)
            put("alr", _dot_nt(du_b, wa_ref[...]).astype(BF16))

            for j in range(N_DEV):
                col = (j % 2) * SHARD_PAD
                for a, b in _shard_pad_cols(j):
                    dsh_ref[j // 2, rows, col:col + b - a] = dproj_ref[rows, a:b]
                    col += b - a
                dsh_ref[j // 2, rows, col:(j % 2 + 1) * SHARD_PAD] = jnp.zeros((sub, SHARD_PAD - SHARD_IN), BF16)

            dh = _dot(dproj_ref[rows, :], wt_ref[...])
            x = x_ref[rows, :]
            r = lax.rsqrt(jnp.mean(x * x, axis=-1, keepdims=True) + NORM_EPS)
            nrm = x * r
            dn = dh * g_ref[...]
            dx_ref[rows, :] = dxres_ref[rows, :] + r * (dn - nrm * jnp.mean(dn * nrm, axis=-1, keepdims=True))
            return jnp.sum(dh * nrm, axis=0, keepdims=True), jnp.sum(du, axis=0, keepdims=True), alr_ref[rows, :], du_b

        for j in range(tm // sub):
            dgin, dba, alr, du_b = one_tile(pl.ds(j * sub, sub))
            agin_ref[...] += dgin
            aba_ref[...] += dba
            awa_ref[...] += _dot_tn(alr, du_b)

        @pl.when(pl.program_id(0) == last)
        def _():
            small_ref[...] = jnp.zeros_like(small_ref)
            _put_rows(small_ref, SMALL_G_IN - base, agin_ref[...])
            _put_rows(small_ref, SMALL_B_ALPHA - base, aba_ref[...])
            for half in range(B_KEY_WIDTH // LANES):
                r0 = SMALL_W_ALPHA - base + half * B_GATE_RANK
                small_ref[r0:r0 + B_GATE_RANK, :] = awa_ref[0:B_GATE_RANK, half * LANES:(half + 1) * LANES]

    def rows(w):
        return pl.BlockSpec((tm, w), lambda i: (i, 0))

    names = ["dq", "dkv", "dza", "dqb", "dkb", "dvb", "dzb", "dla", "u", "alr", "dga", "dgb"]
    return pl.pallas_call(
        body, name="in_proj_bwd", grid=(T // tm,),
        in_specs=[rows(D_MODEL), rows(D_MODEL), rows(LANES), rows(LANES), _const_spec((1, D_MODEL)),
                  _const_spec((D_IN_PAD, D_MODEL)), _const_spec((RANK_PAD, B_KEY_WIDTH))]
                 + [rows(parts[n].shape[1]) for n in names],
        out_specs=[rows(D_MODEL), pl.BlockSpec((N_CHIPS, tm, 2 * SHARD_PAD), lambda i: (0, i, 0)),
                   pl.BlockSpec((SMALL_ROWS - base, LANES), lambda i: (0, 0))],
        out_shape=[jax.ShapeDtypeStruct((T, D_MODEL), F32), jax.ShapeDtypeStruct((N_CHIPS, T, 2 * SHARD_PAD), BF16),
                   jax.ShapeDtypeStruct((SMALL_ROWS - base, LANES), F32)],
        scratch_shapes=[pltpu.VMEM((tm, D_IN_PAD), BF16), pltpu.VMEM((1, D_MODEL), F32), pltpu.VMEM((1, B_KEY_WIDTH), F32),
                        pltpu.VMEM((RANK_PAD, B_KEY_WIDTH), F32)],
        compiler_params=pltpu.CompilerParams(dimension_semantics=("arbitrary",), vmem_limit_bytes=V7X_VMEM_LIMIT_MAX),
    )(x2, dxres, cosf, sinf, g_in, wt_pad, wa_pad, *[parts[n] for n in names])


FLIPS = [(dx, dy, dc) for dx in (0, 1) for dy in (0, 1) for dc in (0, 1)][1:]


def _my_place():
    return lax.axis_index("x"), lax.axis_index("y"), lax.axis_index("c")


def _any_specs(n):
    return [pl.BlockSpec(memory_space=pl.ANY)] * n


def _gather_first(shards, pos_col):
    n = len(shards)
    T = pos_col.shape[0]
    rows_per_pass = math.gcd(T, 512)
    invf, sign = _rope_lane_constants()

    def body(*refs):
        ins, (pos_ref, invf_ref, sign_ref) = refs[:n], refs[n:n + 3]
        outs, (cos_ref, sin_ref) = refs[n + 3:2 * n + 3], refs[2 * n + 3:2 * n + 5]
        send_sems, recv_sems, local_sems = refs[2 * n + 5:]
        x, y, c = _my_place()
        me, sibling = (x, y, c), (x, y, 1 - c)
        onward, source, diagonal = (x ^ (1 - c), y ^ c), (x ^ c, y ^ (1 - c)), (1 - x, 1 - y)

        def block(a, px, py, pc):
            return outs[a].at[4 * px + 2 * py + pc]

        def copy(a, k, blk, to, src=None):
            return pltpu.make_async_remote_copy(
                src_ref=block(a, *blk) if src is None else src, dst_ref=block(a, *blk),
                send_sem=send_sems.at[a, k], recv_sem=recv_sems.at[a, k], device_id=to, device_id_type=MESH)

        mine = [pltpu.make_async_copy(ins[a], block(a, *me), local_sems.at[a]) for a in range(n)]
        for cp in mine:
            cp.start()
        first = []
        for a in range(n):
            first.append(copy(a, 0, me, sibling, src=ins[a]))
            first += [copy(a, 1 + j, me, (*chip, c), src=ins[a]) for j, chip in enumerate((onward, source))]
        for cp in first:
            cp.start()

        def tables(i, carry):
            rows = pl.ds(pl.multiple_of(i * rows_per_pass, rows_per_pass), rows_per_pass)
            ang = pos_ref[rows, :].astype(F32) * invf_ref[...]
            cos_ref[rows, :] = jnp.cos(ang)
            sin_ref[rows, :] = jnp.sin(ang) * sign_ref[...]
            return carry

        lax.fori_loop(0, T // rows_per_pass, tables, 0)

        passed = []
        for j, chip in ((1, source), (0, onward), (2, diagonal)):
            for a in range(n):
                copy(a, 1 + j, (*chip, c), me).wait_recv()
                todo = [copy(a, 4 + j, (*chip, c), sibling)]
                if j == 1:
                    todo.append(copy(a, 1 + 2, (*chip, c), (*onward, c)))
                for cp in todo:
                    cp.start()
                passed += todo
        for a in range(n):
            copy(a, 0, sibling, me).wait_recv()
            for j, chip in enumerate((source, onward, diagonal)):
                copy(a, 4 + j, (*chip, 1 - c), me).wait_recv()
        for cp in first + passed:
            cp.wait_send()
        for cp in mine:
            cp.wait()

    vmem = pl.BlockSpec(memory_space=pltpu.VMEM)
    res = pl.pallas_call(
        body, name="gather_weights",
        in_specs=_any_specs(n) + [vmem] * 3, out_specs=_any_specs(n) + [vmem] * 2,
        out_shape=[jax.ShapeDtypeStruct((N_DEV, *s.shape), s.dtype) for s in shards]
                  + [jax.ShapeDtypeStruct((T, LANES), F32)] * 2,
        scratch_shapes=[pltpu.SemaphoreType.DMA((n, 7)), pltpu.SemaphoreType.DMA((n, 7)), pltpu.SemaphoreType.DMA((n,))],
        compiler_params=pltpu.CompilerParams(vmem_limit_bytes=V7X_VMEM_LIMIT),
    )(*shards, pos_col, invf, sign)
    return res[:n], res[n], res[n + 1]


def _w_in_grad_rs(h, dsh, chip_order, small, wgrads):
    T = h.shape[0]
    tk = math.gcd(T, 2048)
    nk = T // tk
    chip_flips = [(1, 1), (1, 0), (0, 1)]
    n_steps = len(chip_flips) + 1
    SIB = len(chip_flips)
    nw = len(wgrads)
    halves = (slice(0, SHARD_PAD), slice(SHARD_PAD, 2 * SHARD_PAD))

    def body(order_ref, h_ref, d_ref, s_ref, *rest):
        g_refs, (own_ref, recv_ref, sall_ref), rv_refs = rest[:nw], rest[nw:nw + 3], rest[nw + 3:2 * nw + 3]
        (acc_ref, pre_ref, to_sib_ref, to_chip_ref, sib_send, sib_recv, chip_send, chip_recv,
         ssend_sems, srecv_sems, local_sem, wsend_sems, wrecv_sems) = rest[2 * nw + 3:]
        i, kk = pl.program_id(0), pl.program_id(1)
        x, y, c = _my_place()
        my_dev = 4 * x + 2 * y + c

        def wcopy(a, r):
            dx, dy, dc = FLIPS[r]
            return pltpu.make_async_remote_copy(
                src_ref=g_refs[a].at[4 * (x ^ dx) + 2 * (y ^ dy) + (c ^ dc)], dst_ref=rv_refs[a].at[r],
                send_sem=wsend_sems.at[a, r], recv_sem=wrecv_sems.at[a, r],
                device_id=(x ^ dx, y ^ dy, c ^ dc), device_id_type=MESH)

        def small_copy(r, slot):
            dx, dy, dc = FLIPS[r]
            return pltpu.make_async_remote_copy(
                src_ref=s_ref, dst_ref=sall_ref.at[slot], send_sem=ssend_sems.at[r], recv_sem=srecv_sems.at[r],
                device_id=(x ^ dx, y ^ dy, c ^ dc), device_id_type=MESH)

        keep_small = pltpu.make_async_copy(s_ref, sall_ref.at[my_dev], local_sem)

        def sib_copy(t):
            dst = recv_ref.at[SIB] if t == SIB else pre_ref.at[t]
            return pltpu.make_async_remote_copy(
                src_ref=to_sib_ref.at[t], dst_ref=dst, send_sem=sib_send.at[t], recv_sem=sib_recv.at[t],
                device_id=(x, y, 1 - c), device_id_type=MESH)

        def chip_copy(t):
            dx, dy = chip_flips[t]
            return pltpu.make_async_remote_copy(
                src_ref=to_chip_ref.at[t], dst_ref=recv_ref.at[t], send_sem=chip_send.at[t], recv_sem=chip_recv.at[t],
                device_id=(x ^ dx, y ^ dy, c), device_id_type=MESH)

        def accumulate(rows):
            acc_ref[rows, :] += _dot_tn(d_ref[:, rows], h_ref[...])

        @pl.when((i == 0) & (kk == 0))
        def _():
            keep_small.start()
            for r in range(len(FLIPS)):
                small_copy(r, my_dev).start()
                for a in range(nw):
                    wcopy(a, r).start()

        @pl.when(kk == 0)
        def _():
            acc_ref[...] = jnp.zeros_like(acc_ref)

        @pl.when(kk < nk - 1)
        def _():
            accumulate(slice(0, 2 * SHARD_PAD))

        for core in range(2):
            @pl.when((kk == nk - 1) & (c == core))
            def _(mine=halves[core], theirs=halves[1 - core]):
                accumulate(theirs)
                for t in range(n_steps):
                    @pl.when(i == t)
                    def _(t=t):
                        to_sib_ref[t] = acc_ref[theirs, :].astype(BF16)
                        sib_copy(t).start()
                accumulate(mine)
                for t in range(len(chip_flips)):
                    @pl.when(i == t)
                    def _(t=t):
                        sib_copy(t).wait_recv()
                        to_chip_ref[t] = (acc_ref[mine, :] + pre_ref[t].astype(F32)).astype(BF16)
                        chip_copy(t).start()

                @pl.when(i == n_steps - 1)
                def _():
                    own_ref[...] = acc_ref[mine, :]

        @pl.when((i == n_steps - 1) & (kk == nk - 1))
        def _():
            for t in range(len(chip_flips)):
                sib_copy(t).wait_send()
                chip_copy(t).wait_send()
                chip_copy(t).wait_recv()
            sib_copy(SIB).wait_send()
            sib_copy(SIB).wait_recv()
            for r, (dx, dy, dc) in enumerate(FLIPS):
                small_copy(r, 4 * (x ^ dx) + 2 * (y ^ dy) + (c ^ dc)).wait_recv()
                small_copy(r, my_dev).wait_send()
                for a in range(nw):
                    wcopy(a, r).wait()
            keep_small.wait()

    shard = (SHARD_PAD, D_MODEL)
    res = pl.pallas_call(
        body, name="w_in_grad_rs",
        grid_spec=pltpu.PrefetchScalarGridSpec(
            num_scalar_prefetch=1, grid=(n_steps, nk),
            in_specs=[pl.BlockSpec((tk, D_MODEL), lambda i, kk, order: (kk, 0)),
                      pl.BlockSpec((None, tk, 2 * SHARD_PAD), lambda i, kk, order: (order[i], kk, 0)),
                      pl.BlockSpec(memory_space=pl.ANY)] + _any_specs(nw),
            out_specs=[pl.BlockSpec(shard, lambda i, kk, order: (0, 0)),
                       pl.BlockSpec(memory_space=pl.ANY), pl.BlockSpec(memory_space=pl.ANY)] + _any_specs(nw),
            scratch_shapes=[pltpu.VMEM((2 * SHARD_PAD, D_MODEL), F32),
                            pltpu.VMEM((SIB, *shard), BF16), pltpu.VMEM((SIB + 1, *shard), BF16),
                            pltpu.VMEM((SIB, *shard), BF16),
                            pltpu.SemaphoreType.DMA((SIB + 1,)), pltpu.SemaphoreType.DMA((SIB + 1,)),
                            pltpu.SemaphoreType.DMA((SIB,)), pltpu.SemaphoreType.DMA((SIB,)),
                            pltpu.SemaphoreType.DMA((7,)), pltpu.SemaphoreType.DMA((7,)), pltpu.SemaphoreType.DMA,
                            pltpu.SemaphoreType.DMA((nw, len(FLIPS))), pltpu.SemaphoreType.DMA((nw, len(FLIPS)))]),
        out_shape=[jax.ShapeDtypeStruct(shard, F32),
                   jax.ShapeDtypeStruct((SIB + 1, *shard), BF16),
                   jax.ShapeDtypeStruct((N_DEV, *small.shape), F32)]
                  + [jax.ShapeDtypeStruct((len(FLIPS), *g.shape[1:]), g.dtype) for g in wgrads],
        compiler_params=_params("arbitrary", "arbitrary"),
    )(chip_order, h, dsh, small, *wgrads)
    return res[0], res[1], res[2], res[3:]


def _adam_math(w, g, m, v):
    m_new = ADAM_B1 * m + (1.0 - ADAM_B1) * g
    v_new = ADAM_B2 * v + (1.0 - ADAM_B2) * (g * g)
    m_hat = m_new / (1.0 - ADAM_B1 ** ADAM_STEP)
    v_hat = v_new / (1.0 - ADAM_B2 ** ADAM_STEP)
    delta = -ADAM_LR * (m_hat / (jnp.sqrt(v_hat) + ADAM_EPS) + ADAM_WD * w)
    return delta, m_new, v_new


def _adam_big(jobs):
    steps = 8
    n = len(jobs)
    idx = jnp.stack([job[1] for job in jobs]).astype(jnp.int32)
    blocks = []
    for own, _, recv, w, m, v in jobs:
        (rw, cw), rp = w.shape, own.shape[1]
        by_cols = rp != rw
        blk_w = (rw, cw // steps) if by_cols else (rw // steps, cw)
        blk_g = (rp, cw // steps) if by_cols else (rw // steps, cw)
        blocks.append((blk_w, blk_g, by_cols))

    def body(idx_ref, *refs):
        ins, outs = refs[:5 * n], refs[5 * n:]
        for j, (blk_w, _, _) in enumerate(blocks):
            o_ref, r_ref, w_ref, m_ref, v_ref = ins[5 * j:5 * j + 5]
            g_ref, d_ref, mo_ref, vo_ref = outs[4 * j:4 * j + 4]
            g = o_ref[...].astype(F32)
            for r in range(r_ref.shape[0]):
                g = g + r_ref[r].astype(F32)
            g = g[0:blk_w[0], :]
            g_ref[...] = g
            d_ref[...], mo_ref[...], vo_ref[...] = _adam_math(w_ref[...], g, m_ref[...], v_ref[...])

    in_specs, out_specs, out_shape, args = [], [], [], []
    for j, ((own, _, recv, w, m, v), (blk_w, blk_g, by_cols)) in enumerate(zip(jobs, blocks)):
        at = (lambda i: (0, i)) if by_cols else (lambda i: (i, 0))
        spec = pl.BlockSpec(blk_w, lambda i, idx_ref, at=at: at(i))
        in_specs += [pl.BlockSpec((None, *blk_g), lambda i, idx_ref, at=at, j=j: (idx_ref[j], *at(i))),
                     pl.BlockSpec((recv.shape[0], *blk_g), lambda i, idx_ref, at=at: (0, *at(i))), spec, spec, spec]
        out_specs += [spec] * 4
        out_shape += [jax.ShapeDtypeStruct(w.shape, F32)] * 4
        args += [own, recv, w, m, v]
    res = pl.pallas_call(
        body, name="adam_big",
        grid_spec=pltpu.PrefetchScalarGridSpec(num_scalar_prefetch=1, grid=(steps,), in_specs=in_specs, out_specs=out_specs),
        out_shape=out_shape,
        compiler_params=_params("parallel"),
    )(idx, *args)
    return [res[4 * j:4 * j + 4] for j in range(n)]


def _adam_small(small_all, params):
    flat = [a for triple in params for a in triple]
    n_par = len(params)

    def body(s_ref, *refs):
        ins, outs, loss_ref = refs[:3 * n_par], refs[3 * n_par:-1], refs[-1]
        g_slab = s_ref[0]
        for dev in range(1, N_DEV):
            g_slab = g_slab + s_ref[dev]
        loss_ref[...] = g_slab[SMALL_LOSS:SMALL_LOSS + 1, :]
        dev = 4 * lax.axis_index("x") + 2 * lax.axis_index("y") + lax.axis_index("c")
        alpha_full = jnp.concatenate([g_slab[SMALL_W_ALPHA + half * B_GATE_RANK:SMALL_W_ALPHA + (half + 1) * B_GATE_RANK]
                                      for half in range(B_KEY_WIDTH // LANES)], axis=1)
        alpha_mine = pltpu.roll(alpha_full, (B_KEY_WIDTH - dev * SHARD_ALPHA) % B_KEY_WIDTH, 1)[:, 0:SHARD_ALPHA]
        grads = [_take_rows(g_slab, SMALL_G_IN, D_MODEL // LANES), _take_rows(g_slab, SMALL_G_FINAL, D_MODEL // LANES),
                 _take_rows(g_slab, SMALL_G_GLA, B_WIDTH // LANES), _take_rows(g_slab, SMALL_B_ALPHA, B_KEY_WIDTH // LANES),
                 g_slab[SMALL_SINKS:SMALL_SINKS + 1, 0:A_HEADS], alpha_mine]
        for i, g in enumerate(grads):
            w_ref, m_ref, v_ref = ins[3 * i:3 * i + 3]
            delta, m_new, v_new = _adam_math(w_ref[...], g, m_ref[...], v_ref[...])
            outs[4 * i][...] = g
            outs[4 * i + 1][...] = delta
            outs[4 * i + 2][...] = m_new
            outs[4 * i + 3][...] = v_new

    res = pl.pallas_call(
        body, name="adam_small",
        out_shape=[jax.ShapeDtypeStruct(t[0].shape, F32) for t in params for _ in range(4)]
                  + [jax.ShapeDtypeStruct((1, LANES), F32)],
    )(small_all, *flat)
    return [res[4 * i:4 * i + 4] for i in range(n_par)], res[-1]


def _local_step(x, cosf, sinf, loss_target, g_in, wt_sh, wa_pad, b_alpha, sinks, g_gla, out_shards, g_final, chip_order):
    B, S, _ = x.shape
    T = B * S
    x2 = x.reshape(T, D_MODEL)
    tgt2 = loss_target.reshape(T, D_MODEL)
    f, (g_woa, g_wob, g_wo) = _in_proj(x2, cosf, sinf, g_in, wt_sh, wa_pad, b_alpha, out_shards)
    w_o = g_wo.reshape(D_MODEL, D_MODEL)
    sink_row = jnp.repeat(sinks, WINDOW).reshape(1, ATT_ROWS)
    sink_col = sink_row.reshape(ATT_ROWS, 1)
    attn, lse = _attn_fwd(f["qkv"], sink_row, B, S)
    o_gla, st_all = _gla_fwd(f["q"], f["k"], f["cum"], f["vb"], B, S)
    (dxres, dattn, dog, dza, dzb, dga, dgb, dw_o, dw_oa, dw_ob, small_a) = _merge(
        x2, tgt2, attn, f["za"], o_gla, f["zb"], f["ga"], f["gb"], g_woa, g_wob, w_o, g_gla, g_final)
    dq, dkv, dsink = _attn_bwd(f["qkv"], dattn, attn, lse, sink_col, B, S)
    dqb, dkb, dvb, dla = _gla_bwd(f["q"], f["k"], f["cum"], f["vb"], dog, st_all, B, S)
    parts = dict(dq=dq, dkv=dkv, dza=dza, dqb=dqb, dkb=dkb, dvb=dvb, dzb=dzb, dla=dla, u=f["u"], alr=f["alr"],
                 dga=dga, dgb=dgb)
    dx, dsh, small_c = _in_proj_bwd(x2, dxres, cosf, sinf, g_in, f["wt_pad"], wa_pad, parts)
    small = jnp.concatenate([small_a, dsink, small_c], axis=0)
    own_in, rv_in, small_all, (rv_o, rv_oa, rv_ob) = _w_in_grad_rs(f["h"], dsh, chip_order, small, [dw_o, dw_oa, dw_ob])
    return dict(grad_x=dx.reshape(B, S, D_MODEL), own_in=own_in, rv_in=rv_in,
                own_o=dw_o, rv_o=rv_o, own_oa=dw_oa, rv_oa=rv_oa, own_ob=dw_ob, rv_ob=rv_ob, small_all=small_all)


def kernel(x, positions, g_in, w_in, w_alpha_up, b_alpha, attn_sinks, g_gla_norm, w_out_a, w_out_b, w_o, g_final, loss_target, m_g_in, m_w_in, m_w_alpha_up, m_b_alpha, m_attn_sinks, m_g_gla_norm, m_w_out_a, m_w_out_b, m_w_o, m_g_final, v_g_in, v_w_in, v_w_alpha_up, v_b_alpha, v_attn_sinks, v_g_gla_norm, v_w_out_a, v_w_out_b, v_w_o, v_g_final):
    xi, yi, ci = _my_place()
    chip = 2 * xi + yi
    chip_order = jnp.stack([chip ^ 3, chip ^ 2, chip ^ 1, chip]).astype(jnp.int32)

    (g_win, g_wa), cosf, sinf = _gather_first(
        [jnp.pad(w_in[0].T.astype(BF16), ((0, SHARD_PAD - SHARD_IN), (0, 0))), w_alpha_up[0].astype(BF16)],
        positions.reshape(-1, 1))
    wt_sh = g_win.reshape(N_DEV * SHARD_PAD, D_MODEL)
    wa_pad = jnp.pad(jnp.concatenate([g_wa[j] for j in range(N_DEV)], axis=1), ((0, RANK_PAD - B_GATE_RANK), (0, 0)))

    r = _local_step(x, cosf, sinf, loss_target, g_in, wt_sh, wa_pad, b_alpha, attn_sinks[0], g_gla_norm,
                    [w_out_a[0].astype(BF16), w_out_b[0].astype(BF16), w_o[0].astype(BF16)],
                    g_final.reshape(1, D_MODEL), chip_order)

    dev = 4 * xi + 2 * yi + ci
    big = _adam_big([(r["own_in"][None], jnp.int32(0), r["rv_in"], w_in[0].T, m_w_in[0].T, v_w_in[0].T),
                     (r["own_oa"], dev, r["rv_oa"], w_out_a[0], m_w_out_a[0], v_w_out_a[0]),
                     (r["own_ob"], dev, r["rv_ob"], w_out_b[0], m_w_out_b[0], v_w_out_b[0]),
                     (r["own_o"], dev, r["rv_o"], w_o[0], m_w_o[0], v_w_o[0])])
    big[0] = [a.T for a in big[0]]
    row = lambda a: a.reshape(1, D_MODEL)
    (s_in, s_final, s_gla, s_ba, s_sinks, s_wa), loss_row = _adam_small(r["small_all"], [
        (g_in, m_g_in, v_g_in), (row(g_final), row(m_g_final), row(v_g_final)),
        (g_gla_norm, m_g_gla_norm, v_g_gla_norm), (b_alpha, m_b_alpha, v_b_alpha),
        (attn_sinks, m_attn_sinks, v_attn_sinks), (w_alpha_up[0], m_w_alpha_up[0], v_w_alpha_up[0])])

    def group(i):
        return (s_in[i], big[0][i][None], s_wa[i][None], s_ba[i], s_sinks[i], s_gla[i], big[1][i][None], big[2][i][None],
                big[3][i][None], s_final[i].reshape(D_MODEL))

    return (loss_row[0, 0], r["grad_x"], *group(0), *group(1), *group(2), *group(3))
```

```python
import functools
import math

import numpy as np
import jax
import jax.numpy as jnp
from jax import lax
from jax.experimental import pallas as pl
from jax.experimental.pallas import tpu as pltpu

F32 = jnp.float32
BF16 = jnp.bfloat16
MESH = pl.DeviceIdType.MESH

D_MODEL = 1024
A_HEADS, A_KV_HEADS, A_HEAD_DIM = 8, 2, 64
A_WIDTH, A_KV_WIDTH = 512, 128
WINDOW = 128
ROPE_THETA = 500000.0
ROPE_DIM = 16
B_HEADS, B_KEY_DIM, B_VAL_DIM = 4, 64, 128
B_KEY_WIDTH, B_WIDTH = 256, 512
B_GATE_RANK = 16
B_GATE_TEMP = 16.0
B_CHUNK = 64
NORM_EPS = 1e-6
NEG_BIG = -1e30
D_IN = 4880
N_DEV = 8
N_CHIPS = 4
ADAM_LR, ADAM_B1, ADAM_B2, ADAM_EPS, ADAM_WD, ADAM_STEP = 0.001, 0.9, 0.999, 1e-08, 0.01, 10

LANES = 128
V7X_VMEM_LIMIT = 56 * 1024 * 1024
V7X_VMEM_LIMIT_MAX = 62 * 1024 * 1024

RANK_PAD = LANES
SEG = {}
_off = 0
for _name, _w in (("qa", 512), ("ka", 128), ("va", 128), ("za", 512), ("qb", 256), ("kb", 256),
                  ("vb", 512), ("zb", 512), ("alr", RANK_PAD), ("ga", 1024), ("gb", 1024)):
    SEG[_name] = (_off, _off + _w)
    _off += _w
D_IN_PAD = _off
ALR_SRC = SEG["alr"][0]
QKV_K, QKV_V, QKV_W = SEG["ka"][0], SEG["va"][0], SEG["va"][1]
ATT_SCALE = A_HEAD_DIM ** -0.5

SHARD_IN = D_IN // N_DEV
SHARD_PAD = 640
SHARD_OUT = D_MODEL // N_DEV
SHARD_ALPHA = B_KEY_WIDTH // N_DEV

SMALL_G_FINAL, SMALL_G_GLA, SMALL_LOSS, SMALL_SINKS, SMALL_G_IN, SMALL_B_ALPHA, SMALL_W_ALPHA = 0, 8, 12, 16, 24, 32, 40
SMALL_ROWS = 72


def _dot(a, b):
    return jnp.dot(a, b, preferred_element_type=F32)


def _dot_nt(a, b):
    return lax.dot_general(a, b, (((1,), (1,)), ((), ())), preferred_element_type=F32)


def _dot_tn(a, b):
    return lax.dot_general(a, b, (((0,), (0,)), ((), ())), preferred_element_type=F32)


def _sigmoid(z):
    return 1.0 / (1.0 + jnp.exp(-z))


def _sigmoid_tanh(z):
    return 0.5 * jnp.tanh(0.5 * z) + 0.5


def _params(*sem):
    return pltpu.CompilerParams(dimension_semantics=sem, vmem_limit_bytes=V7X_VMEM_LIMIT)


def _const_spec(shape):
    nd = len(shape)
    return pl.BlockSpec(shape, lambda *_: (0,) * nd, pipeline_mode=pl.Buffered(1))


def _lane_iota(shape):
    return lax.broadcasted_iota(jnp.int32, shape, 1)


def _row_iota(shape):
    return lax.broadcasted_iota(jnp.int32, shape, 0)


def _split3(v):
    hi = v.astype(BF16)
    r1 = v - hi.astype(F32)
    mid = r1.astype(BF16)
    lo = (r1 - mid.astype(F32)).astype(BF16)
    return hi, mid, lo


def _put_rows(ref, row0, vec):
    for r in range(vec.shape[1] // LANES):
        ref[row0 + r:row0 + r + 1, :] = vec[:, r * LANES:(r + 1) * LANES]


def _take_rows(slab, row0, n):
    return jnp.concatenate([slab[row0 + r:row0 + r + 1, :] for r in range(n)], axis=1)


def _rope_lane_constants():
    half = ROPE_DIM // 2
    inv_freq = np.exp(-math.log(ROPE_THETA) * np.arange(half, dtype=np.float32) * np.float32(2.0 / ROPE_DIM)).astype(np.float32)
    lane = np.arange(LANES)
    j = lane % A_HEAD_DIM
    invf = np.where(j < ROPE_DIM, inv_freq[j % half], 0.0).astype(np.float32)
    sign = np.where(j < half, -1.0, np.where(j < ROPE_DIM, 1.0, 0.0)).astype(np.float32)
    return jnp.asarray(invf)[None, :], jnp.asarray(sign)[None, :]


def _rope_slab(t, cos, sin_signed):
    first = (_lane_iota(t.shape) % A_HEAD_DIM) < (ROPE_DIM // 2)
    partner = jnp.where(first, pltpu.roll(t, LANES - ROPE_DIM // 2, 1), pltpu.roll(t, ROPE_DIM // 2, 1))
    return t * cos + partner * sin_signed


def _shard_pad_cols(j):
    cut = ALR_SRC + B_GATE_RANK
    shift = RANK_PAD - B_GATE_RANK
    a, b = j * SHARD_IN, (j + 1) * SHARD_IN
    if b <= cut:
        return [(a, b)]
    if a >= cut:
        return [(a + shift, b + shift)]
    return [(a, cut), (cut + shift, b + shift)]


def _in_proj(x2, cosf, sinf, g_in, wt_sh, wa_pad, b_alpha, later_shards):
    T = x2.shape[0]
    tm = math.gcd(T, 512)
    sub = math.gcd(tm, 256)
    last = T // tm - 1
    nl = len(later_shards)

    def body(x_ref, cos_ref, sin_ref, g_ref, wsh_ref, wa_ref, ba_ref, *rest):
        sh_refs, rest = rest[:nl], rest[nl:]
        (h_ref, qkv_ref, za_ref, q_ref, k_ref, vb_ref, zb_ref, alr_ref, u_ref, cum_ref, ga_ref, gb_ref, wt_out) = rest[:13]
        all_refs, (wt_ref, send_sems, recv_sems, local_sems, wt_sem) = rest[13:13 + nl], rest[13 + nl:]
        wt_copy = pltpu.make_async_copy(wt_ref, wt_out, wt_sem)
        px, py, pc = _my_place()
        my_dev = 4 * px + 2 * py + pc

        def wcopy(a, r, slot):
            dx, dy, dc = FLIPS[r]
            return pltpu.make_async_remote_copy(
                src_ref=sh_refs[a], dst_ref=all_refs[a].at[slot], send_sem=send_sems.at[a, r],
                recv_sem=recv_sems.at[a, r], device_id=(px ^ dx, py ^ dy, pc ^ dc), device_id_type=MESH)

        keep = [pltpu.make_async_copy(sh_refs[a], all_refs[a].at[my_dev], local_sems.at[a]) for a in range(nl)]

        @pl.when(pl.program_id(0) == 0)
        def _():
            for a in range(nl):
                keep[a].start()
                for r in range(len(FLIPS)):
                    wcopy(a, r, my_dev).start()

        @pl.when(pl.program_id(0) == 0)
        def _():
            for j in range(N_DEV):
                src = j * SHARD_PAD
                for a, b in _shard_pad_cols(j):
                    wt_ref[a:b, :] = wsh_ref[src:src + b - a, :]
                    src += b - a
            a, b = SEG["alr"]
            wt_ref[a + B_GATE_RANK:b, :] = jnp.zeros((RANK_PAD - B_GATE_RANK, D_MODEL), BF16)
            wt_copy.start()

        def one_tile(rows):
            x = x_ref[rows, :]
            r = lax.rsqrt(jnp.mean(x * x, axis=-1, keepdims=True) + NORM_EPS)
            h = (x * r * g_ref[...]).astype(BF16)
            h_ref[rows, :] = h

            def seg(name):
                a, b = SEG[name]
                return _dot_nt(h, wt_ref[a:b, :])

            alr = seg("alr").astype(BF16)
            alr_ref[rows, :] = alr
            u = _dot(alr, wa_ref[...]) + ba_ref[...]
            u_ref[rows, :] = u
            log_a = (jnp.minimum(u, 0.0) - jnp.log(1.0 + jnp.exp(-jnp.abs(u)))) * (1.0 / B_GATE_TEMP)
            row, col = _row_iota((sub, sub)), _lane_iota((sub, sub))
            tri = ((row // B_CHUNK == col // B_CHUNK) & (col <= row)).astype(BF16)
            hi, mid, lo = _split3(log_a)
            cum_ref[rows, :] = _dot(tri, hi) + _dot(tri, mid) + _dot(tri, lo)

            cos, sin = cos_ref[rows, :], sin_ref[rows, :]
            qa = seg("qa") * ATT_SCALE
            for s in range(A_WIDTH // LANES):
                qkv_ref[rows, s * LANES:(s + 1) * LANES] = _rope_slab(qa[:, s * LANES:(s + 1) * LANES], cos, sin).astype(BF16)
            qkv_ref[rows, QKV_K:QKV_V] = _rope_slab(seg("ka"), cos, sin).astype(BF16)
            qkv_ref[rows, QKV_V:QKV_W] = seg("va").astype(BF16)
            za_ref[rows, :] = seg("za").astype(BF16)
            q_ref[rows, :] = seg("qb")
            k_ref[rows, :] = seg("kb")
            vb_ref[rows, :] = seg("vb").astype(BF16)
            zb_ref[rows, :] = seg("zb").astype(BF16)
            ga_ref[rows, :] = seg("ga").astype(BF16)
            gb_ref[rows, :] = seg("gb").astype(BF16)

        for j in range(tm // sub):
            one_tile(pl.ds(j * sub, sub))

        @pl.when(pl.program_id(0) == last)
        def _():
            for a in range(nl):
                for r, (dx, dy, dc) in enumerate(FLIPS):
                    wcopy(a, r, 4 * (px ^ dx) + 2 * (py ^ dy) + (pc ^ dc)).wait_recv()
                    wcopy(a, r, my_dev).wait_send()
                keep[a].wait()
            wt_copy.wait()

    def rows(w):
        return pl.BlockSpec((tm, w), lambda i: (i, 0))

    outs = [("h", D_MODEL, BF16), ("qkv", QKV_W, BF16), ("za", A_WIDTH, BF16), ("q", B_KEY_WIDTH, F32),
            ("k", B_KEY_WIDTH, F32), ("vb", B_WIDTH, BF16), ("zb", B_WIDTH, BF16), ("alr", RANK_PAD, BF16),
            ("u", B_KEY_WIDTH, F32), ("cum", B_KEY_WIDTH, F32), ("ga", D_MODEL, BF16), ("gb", D_MODEL, BF16)]
    res = pl.pallas_call(
        body, name="in_proj", grid=(T // tm,),
        in_specs=[rows(D_MODEL), rows(LANES), rows(LANES), _const_spec((1, D_MODEL)),
                  _const_spec((N_DEV * SHARD_PAD, D_MODEL)), _const_spec((RANK_PAD, B_KEY_WIDTH)),
                  _const_spec((1, B_KEY_WIDTH))] + _any_specs(nl),
        out_specs=[rows(w) for _, w, _ in outs] + _any_specs(1 + nl),
        out_shape=[jax.ShapeDtypeStruct((T, w), dt) for _, w, dt in outs]
                  + [jax.ShapeDtypeStruct((D_IN_PAD, D_MODEL), BF16)]
                  + [jax.ShapeDtypeStruct((N_DEV, *sh.shape), sh.dtype) for sh in later_shards],
        scratch_shapes=[pltpu.VMEM((D_IN_PAD, D_MODEL), BF16),
                        pltpu.SemaphoreType.DMA((nl, len(FLIPS))), pltpu.SemaphoreType.DMA((nl, len(FLIPS))),
                        pltpu.SemaphoreType.DMA((nl,)), pltpu.SemaphoreType.DMA],
        compiler_params=_params("arbitrary"),
    )(x2, cosf, sinf, g_in, wt_sh, wa_pad, b_alpha, *later_shards)
    n_out = len(outs) + 1
    return dict(zip([n for n, _, _ in outs] + ["wt_pad"], res[:n_out])), res[n_out:]


def _dup_kv_head(t, g):
    tf = t.astype(F32)
    keep = (_lane_iota(tf.shape) < A_HEAD_DIM) == (g == 0)
    return jnp.where(keep, tf, pltpu.roll(tf, A_HEAD_DIM, 1)).astype(BF16)


def _stack_heads(t):
    lo = _lane_iota(t.shape) < A_HEAD_DIM
    zero = jnp.zeros_like(t)
    return jnp.concatenate([jnp.where(lo, t, zero), jnp.where(lo, zero, t)], axis=0)


ATT_ROWS = A_HEADS * WINDOW
GROUP_ROWS = ATT_ROWS // A_KV_HEADS
HEADS_PER_GROUP = A_HEADS // A_KV_HEADS


def _band_mask_t(n):
    kj = _row_iota((2 * WINDOW, GROUP_ROWS)) - WINDOW
    qi = _lane_iota((2 * WINDOW, GROUP_ROWS)) % WINDOW
    return (kj <= qi) & (qi - kj < WINDOW) & ((n > 0) | (kj >= 0))


def _stacked_queries(ref, g):
    pairs = range(g * HEADS_PER_GROUP // 2, (g + 1) * HEADS_PER_GROUP // 2)
    return jnp.concatenate([_stack_heads(ref[:, p * LANES:(p + 1) * LANES]) for p in pairs], axis=0)


def _unstack_heads(t, g, ref, dtype):
    lo = _lane_iota((WINDOW, LANES)) < A_HEAD_DIM
    for hh in range(HEADS_PER_GROUP // 2):
        p = g * HEADS_PER_GROUP // 2 + hh
        ref[:, p * LANES:(p + 1) * LANES] = jnp.where(lo, t[2 * hh * WINDOW:(2 * hh + 1) * WINDOW],
                                                       t[(2 * hh + 1) * WINDOW:(2 * hh + 2) * WINDOW]).astype(dtype)


FWD_BLOCKS = 16


def _attn_fwd(qkv, sink_row, B, S):
    T = B * S
    nb = S // WINDOW
    blocks = math.gcd(nb, FWD_BLOCKS)
    steps = nb // blocks

    def one_block(has_prev, sink_ref, q, k, v, o_ref, lse_ref):
        valid = _band_mask_t(has_prev)
        lse_rows = []
        for g in range(A_KV_HEADS):
            kd, vd = _dup_kv_head(k, g), _dup_kv_head(v, g)
            s = jnp.where(valid, _dot_nt(kd, _stacked_queries(q, g)), NEG_BIG)
            sink = sink_ref[:, g * GROUP_ROWS:(g + 1) * GROUP_ROWS]
            m = jnp.maximum(jnp.max(s, axis=0, keepdims=True), sink)
            e = jnp.exp(s - m)
            den = jnp.sum(e, axis=0, keepdims=True) + jnp.exp(sink - m)
            o = _dot_tn((e * (1.0 / den)).astype(BF16), vd)
            _unstack_heads(o, g, o_ref, F32)
            lse = m + jnp.log(den)
            lse_rows += [lse[:, j * WINDOW:(j + 1) * WINDOW] for j in range(HEADS_PER_GROUP)]
        by_head = jnp.concatenate(lse_rows + [jnp.zeros((WINDOW - A_HEADS, WINDOW), F32)], axis=0)
        lse_ref[...] = by_head.T

    def body(sink_ref, q_ref, kc_ref, vc_ref, kp_ref, vp_ref, o_ref, lse_ref):
        k_all = jnp.concatenate([kp_ref[...], kc_ref[...]], axis=0)
        v_all = jnp.concatenate([vp_ref[...], vc_ref[...]], axis=0)
        for j in range(blocks):
            rows = pl.ds(j * WINDOW, WINDOW)
            keys = slice(j * WINDOW, (j + 2) * WINDOW)
            has_prev = pl.program_id(1) if j == 0 else 1
            one_block(has_prev, sink_ref, q_ref[rows, :], k_all[keys], v_all[keys], o_ref.at[rows], lse_ref.at[rows])

    def cur(col, w):
        return pl.BlockSpec((blocks * WINDOW, w), lambda b, n: (b * steps + n, col))

    def prev(col):
        return pl.BlockSpec((WINDOW, LANES), lambda b, n: (b * nb + jnp.maximum(blocks * n - 1, 0), col))

    kcol, vcol = QKV_K // LANES, QKV_V // LANES
    return pl.pallas_call(
        body, name="attn_fwd", grid=(B, steps),
        in_specs=[_const_spec((1, ATT_ROWS)), cur(0, A_WIDTH), cur(kcol, LANES), cur(vcol, LANES), prev(kcol), prev(vcol)],
        out_specs=[cur(0, A_WIDTH), cur(0, LANES)],
        out_shape=[jax.ShapeDtypeStruct((T, A_WIDTH), F32), jax.ShapeDtypeStruct((T, LANES), F32)],
        compiler_params=_params("parallel", "parallel"),
    )(sink_row, qkv, qkv, qkv, qkv, qkv)


ATT_CHUNK = 128


def _chunk_masks(n):
    masks = []
    for half in range(WINDOW // ATT_CHUNK):
        qi = _row_iota((ATT_CHUNK, 2 * WINDOW)) + half * ATT_CHUNK
        kj = _lane_iota((ATT_CHUNK, 2 * WINDOW)) - WINDOW
        masks.append((kj <= qi) & (qi - kj < WINDOW) & ((n > 0) | (kj >= 0)))
    return masks


def _all_stacked_queries(ref):
    return jnp.concatenate([_stacked_queries(ref, g) for g in range(A_KV_HEADS)], axis=0)


def _by_group(fn, lhs, rhs_per_group):
    return jnp.concatenate([fn(lhs[g * GROUP_ROWS:(g + 1) * GROUP_ROWS], rhs_per_group[g])
                            for g in range(A_KV_HEADS)], axis=0)


BWD_BLOCKS = 8


def _attn_bwd(qkv, do, out, lse, sink_col, B, S):
    T = B * S
    nb = S // WINDOW
    M = math.gcd(nb, BWD_BLOCKS)
    steps = nb // M
    n_chunks = ATT_ROWS // ATT_CHUNK
    halves = WINDOW // ATT_CHUNK

    def block(has_prev, sink_ref, q, do_b, out_b, lse_b, k, v, scratch, want_dq):
        s_ref, dp_ref, ds_ref, p_ref = scratch
        width = k.shape[0]
        masks = [mk[:, 0:width] for mk in _chunk_masks(has_prev)]
        kd = [_dup_kv_head(k, g) for g in range(A_KV_HEADS)]
        vd = [_dup_kv_head(v, g) for g in range(A_KV_HEADS)]
        qs, dos = _all_stacked_queries(q), _all_stacked_queries(do_b)
        s_ref[...] = _by_group(_dot_nt, qs, kd)
        dp_ref[...] = _by_group(_dot_nt, dos, vd)
        lane = _lane_iota((ATT_CHUNK, LANES))
        lo = lane < A_HEAD_DIM
        lane1 = _lane_iota((1, LANES))
        dsink_row = jnp.zeros((1, LANES), F32)
        for c in range(n_chunks):
            rows = slice(c * ATT_CHUNK, (c + 1) * ATT_CHUNK)
            head, half = divmod(c, halves)
            qrows = slice(half * ATT_CHUNK, (half + 1) * ATT_CHUNK)
            slab = slice((head // 2) * LANES, (head // 2 + 1) * LANES)
            lse_col = jnp.sum(jnp.where(lane == head, lse_b[qrows, :], 0.0), axis=-1, keepdims=True)
            prod = do_b[qrows, slab].astype(F32) * out_b[qrows, slab].astype(F32)
            mine = lo if head % 2 == 0 else jnp.logical_not(lo)
            delta = jnp.sum(jnp.where(mine, prod, 0.0), axis=-1, keepdims=True)
            prob = jnp.exp(jnp.where(masks[half], s_ref[rows, :], NEG_BIG) - lse_col)
            p_ref[rows, :] = prob.astype(BF16)
            ds_ref[rows, :] = (prob * (dp_ref[rows, :] - delta)).astype(BF16)
            w = -jnp.exp(sink_ref[rows, :] - lse_col) * delta
            dsink_row += jnp.where(lane1 == head, jnp.sum(w, axis=0, keepdims=True), 0.0)
        dq = _by_group(_dot, ds_ref[...], kd) * ATT_SCALE if want_dq else None
        groups = [slice(g * GROUP_ROWS, (g + 1) * GROUP_ROWS) for g in range(A_KV_HEADS)]
        dk = [_dot_tn(ds_ref[rows, :], qs[rows]) for rows in groups]
        dv = [_dot_tn(p_ref[rows, :], dos[rows]) for rows in groups]
        return dq, dk, dv, dsink_row

    def fold(per_group):
        lane = _lane_iota((WINDOW, LANES))
        out = jnp.zeros((WINDOW, LANES), F32)
        for g, acc in enumerate(per_group):
            out = jnp.where((lane < A_HEAD_DIM) == (g == 0), acc + pltpu.roll(acc, A_HEAD_DIM, 1), out)
        return out

    def body(sink_ref, q_ref, qn_ref, do_ref, don_ref, out_ref, outn_ref, lse_ref, lsen_ref, kc_ref, kp_ref, vc_ref, vp_ref,
             dq_ref, dkv_ref, dsink_ref, s_scr, dp_scr, ds_scr, p_scr, s_x, dp_x, ds_x, p_x):
        b, m = pl.program_id(0), pl.program_id(1)

        @pl.when((b == 0) & (m == 0))
        def _():
            dsink_ref[...] = jnp.zeros_like(dsink_ref)

        k_all = jnp.concatenate([kp_ref[...], kc_ref[...]], axis=0)
        v_all = jnp.concatenate([vp_ref[...], vc_ref[...]], axis=0)
        results = []
        for j in range(M):
            rows = slice(j * WINDOW, (j + 1) * WINDOW)
            keys = slice(j * WINDOW, (j + 2) * WINDOW)
            has_prev = m if j == 0 else 1
            results.append(block(has_prev, sink_ref, q_ref[rows, :], do_ref[rows, :], out_ref[rows, :], lse_ref[rows, :],
                                 k_all[keys], v_all[keys], (s_scr.at[j], dp_scr.at[j], ds_scr.at[j], p_scr.at[j]), True))
        last_keys = slice(M * WINDOW, (M + 1) * WINDOW)
        _, dk_x, dv_x, _ = block(1, sink_ref, qn_ref[...], don_ref[...], outn_ref[...], lsen_ref[...],
                                 k_all[last_keys], v_all[last_keys], (s_x, dp_x, ds_x, p_x), False)
        has_next = m < steps - 1
        lo_q = _lane_iota((WINDOW, LANES)) < A_HEAD_DIM
        dsink_row = jnp.zeros((1, LANES), F32)
        for j, (dq, dk, dv, ds_row) in enumerate(results):
            rows = slice(j * WINDOW, (j + 1) * WINDOW)
            for p in range(A_HEADS // 2):
                dq_ref[rows, p * LANES:(p + 1) * LANES] = jnp.where(
                    lo_q, dq[2 * p * WINDOW:(2 * p + 1) * WINDOW], dq[(2 * p + 1) * WINDOW:(2 * p + 2) * WINDOW]).astype(BF16)
            if j + 1 < M:
                dk_next = [t[0:WINDOW] for t in results[j + 1][1]]
                dv_next = [t[0:WINDOW] for t in results[j + 1][2]]
            else:
                dk_next = [jnp.where(has_next, t, 0.0) for t in dk_x]
                dv_next = [jnp.where(has_next, t, 0.0) for t in dv_x]
            dkv_ref[rows, 0:LANES] = fold([own[WINDOW:] + nxt for own, nxt in zip(dk, dk_next)]).astype(BF16)
            dkv_ref[rows, LANES:] = fold([own[WINDOW:] + nxt for own, nxt in zip(dv, dv_next)]).astype(BF16)
            dsink_row += ds_row
        dsink_ref[0:1, :] += dsink_row

    def cur(col, w):
        return pl.BlockSpec((M * WINDOW, w), lambda b, m: (b * steps + m, col))

    def nxt(col, w):
        return pl.BlockSpec((WINDOW, w), lambda b, m: (b * nb + jnp.minimum(M * (m + 1), nb - 1), col))

    def prev(col):
        return pl.BlockSpec((WINDOW, LANES), lambda b, m: (b * nb + jnp.maximum(M * m - 1, 0), col))

    kcol, vcol = QKV_K // LANES, QKV_V // LANES
    scores = (M, ATT_ROWS, 2 * WINDOW)
    extra = (ATT_ROWS, WINDOW)
    return pl.pallas_call(
        body, name="attn_bwd", grid=(B, steps),
        in_specs=[_const_spec((ATT_ROWS, 1)), cur(0, A_WIDTH), nxt(0, A_WIDTH), cur(0, A_WIDTH), nxt(0, A_WIDTH),
                  cur(0, A_WIDTH), nxt(0, A_WIDTH), cur(0, LANES), nxt(0, LANES),
                  cur(kcol, LANES), prev(kcol), cur(vcol, LANES), prev(vcol)],
        out_specs=[cur(0, A_WIDTH), cur(0, 2 * LANES), pl.BlockSpec((8, LANES), lambda b, m: (0, 0))],
        out_shape=[jax.ShapeDtypeStruct((T, A_WIDTH), BF16), jax.ShapeDtypeStruct((T, 2 * LANES), BF16),
                   jax.ShapeDtypeStruct((8, LANES), F32)],
        scratch_shapes=[pltpu.VMEM(scores, F32), pltpu.VMEM(scores, F32), pltpu.VMEM(scores, BF16), pltpu.VMEM(scores, BF16),
                        pltpu.VMEM(extra, F32), pltpu.VMEM(extra, F32), pltpu.VMEM(extra, BF16), pltpu.VMEM(extra, BF16)],
        compiler_params=_params("arbitrary", "arbitrary"),
    )(sink_col, qkv, qkv, do, do, out, out, lse, lse, qkv, qkv, qkv, qkv)


GLA_FWD_TILING = (64, 16)
GLA_BWD_TILING = (256, 4)


def _gla_factors(q_ref, k_ref, cum_ref):
    cpt = q_ref.shape[0] // B_CHUNK
    scale = B_KEY_DIM ** -0.5
    cum = cum_ref[...]
    shape = (B_CHUNK, B_KEY_WIDTH)
    last = jnp.concatenate([jnp.broadcast_to(cum_ref[pl.ds(c * B_CHUNK + B_CHUNK - 1, 1), :], shape)
                            for c in range(cpt)], axis=0)
    mid = jnp.concatenate([jnp.broadcast_to(cum_ref[pl.ds(c * B_CHUNK + B_CHUNK // 2 - 1, 1), :], shape)
                           for c in range(cpt)], axis=0)
    e_qm, e_km, e_qe, e_kd = jnp.exp(cum - mid), jnp.exp(mid - cum), jnp.exp(cum), jnp.exp(last - cum)
    qs = q_ref[...] * scale
    k = k_ref[...]
    return qs, k, (e_qm, e_km, e_qe, e_kd)


def _head_mask(shape, h):
    return (_lane_iota(shape) // B_KEY_DIM) == h


def _stack_masked(t):
    return jnp.concatenate([jnp.where(_head_mask(t.shape, h), t, 0.0) for h in range(B_HEADS)], axis=0).astype(BF16)


def _select_heads(t):
    shape = (B_CHUNK, B_KEY_WIDTH)
    out = jnp.zeros(shape, F32)
    for h in range(B_HEADS):
        out = jnp.where(_head_mask(shape, h), t[h * B_CHUNK:(h + 1) * B_CHUNK], out)
    return out


def _select_state(t):
    shape = (B_VAL_DIM, B_KEY_WIDTH)
    out = jnp.zeros(shape, F32)
    for h in range(B_HEADS):
        out = jnp.where(_head_mask(shape, h), t[h * B_VAL_DIM:(h + 1) * B_VAL_DIM], out)
    return out


def _rows_by_head(t):
    return jnp.concatenate([t[:, h * B_VAL_DIM:(h + 1) * B_VAL_DIM] for h in range(B_HEADS)], axis=0)


def _intra_mask(tile_rows):
    i, j = _row_iota((tile_rows, tile_rows)), _lane_iota((tile_rows, tile_rows))
    return (i // B_CHUNK == j // B_CHUNK) & (j <= i)


def _pair_stack(t, p):
    slab = t[:, p * LANES:(p + 1) * LANES]
    lo = _lane_iota(slab.shape) < B_KEY_DIM
    return jnp.concatenate([jnp.where(lo, slab, 0.0), jnp.where(lo, 0.0, slab)], axis=0).astype(BF16)


def _gla_fwd(q, k, cum, vb, B, S):
    T = B * S
    tile_rows = math.gcd(S, GLA_FWD_TILING[0])
    cpt = tile_rows // B_CHUNK
    nt = S // tile_rows
    tps = math.gcd(nt, GLA_FWD_TILING[1])

    def one_sequence(q_ref, k_ref, cum_ref, v_ref, o_ref, st_all_ref, st_ref):
        qs, kk, (e_qm, e_km, e_qe, e_kd) = _gla_factors(q_ref, k_ref, cum_ref)
        qm, km, qe, kd = qs * e_qm, kk * e_km, qs * e_qe, (kk * e_kd).astype(BF16)
        mask = _intra_mask(tile_rows)
        intra = []
        for p in range(B_HEADS // 2):
            a = _dot_nt(_pair_stack(qm, p), km[:, p * LANES:(p + 1) * LANES].astype(BF16))
            for hh in range(2):
                h = 2 * p + hh
                att = jnp.where(mask, a[hh * tile_rows:(hh + 1) * tile_rows], 0.0).astype(BF16)
                intra.append(_dot(att, v_ref[:, h * B_VAL_DIM:(h + 1) * B_VAL_DIM]))
        inter = []
        for c in range(cpt):
            rows = slice(c * B_CHUNK, (c + 1) * B_CHUNK)
            st = st_ref[...]
            st_all_ref[c] = st
            inter.append(_dot_nt(_stack_masked(qe[rows]), st.astype(BF16)))
            inc = _select_state(_dot_tn(v_ref[rows, :], kd[rows]))
            decay = jnp.exp(cum_ref[pl.ds(c * B_CHUNK + B_CHUNK - 1, 1), :])
            st_ref[...] = st * decay + inc
        for h in range(B_HEADS):
            oi = jnp.concatenate([inter[c][h * B_CHUNK:(h + 1) * B_CHUNK] for c in range(cpt)], axis=0)
            o_ref[:, h * B_VAL_DIM:(h + 1) * B_VAL_DIM] = (intra[h] + oi).astype(BF16)

    def body(q_ref, k_ref, cum_ref, v_ref, o_ref, st_all_ref, st_ref):
        @pl.when(pl.program_id(0) == 0)
        def _():
            st_ref[...] = jnp.zeros_like(st_ref)

        for b in range(B):
            for tile in range(tps):
                tok = pl.ds(tile * tile_rows, tile_rows)
                chunks = pl.ds(tile * cpt, cpt)
                one_sequence(*[r.at[b, tok] for r in (q_ref, k_ref, cum_ref, v_ref, o_ref)],
                             st_all_ref.at[b, chunks], st_ref.at[b])

    def rows(w):
        return pl.BlockSpec((B, tps * tile_rows, w), lambda t: (0, t, 0))

    seq = lambda a: a.reshape(B, S, a.shape[-1])
    o, st_all = pl.pallas_call(
        body, name="gla_fwd", grid=(nt // tps,),
        in_specs=[rows(B_KEY_WIDTH), rows(B_KEY_WIDTH), rows(B_KEY_WIDTH), rows(B_WIDTH)],
        out_specs=[rows(B_WIDTH),
                   pl.BlockSpec((B, tps * cpt, B_VAL_DIM, B_KEY_WIDTH), lambda t: (0, t, 0, 0))],
        out_shape=[jax.ShapeDtypeStruct((B, S, B_WIDTH), BF16),
                   jax.ShapeDtypeStruct((B, S // B_CHUNK, B_VAL_DIM, B_KEY_WIDTH), F32)],
        scratch_shapes=[pltpu.VMEM((B, B_VAL_DIM, B_KEY_WIDTH), F32)],
        compiler_params=_params("arbitrary"),
    )(seq(q), seq(k), seq(cum), seq(vb))
    return o.reshape(T, B_WIDTH), st_all.reshape(T // B_CHUNK, B_VAL_DIM, B_KEY_WIDTH)


def _gla_bwd(q, k, cum, vb, do, st_all, B, S, wgrads):
    T = B * S
    tile_rows = math.gcd(S, GLA_BWD_TILING[0])
    cpt = tile_rows // B_CHUNK
    nt = S // tile_rows
    tps = math.gcd(nt, GLA_BWD_TILING[1])
    steps = nt // tps
    scale = B_KEY_DIM ** -0.5
    nw = len(wgrads)

    def one_sequence(q_ref, k_ref, cum_ref, v_ref, do_ref, st_all_ref, dq_ref, dk_ref, dv_ref, dla_ref, dst_ref):
        qs, kk, (e_qm, e_km, e_qe, e_kd) = _gla_factors(q_ref, k_ref, cum_ref)
        qm, km, qe, kd = qs * e_qm, kk * e_km, qs * e_qe, kk * e_kd
        mask = _intra_mask(tile_rows)
        dqm_slabs, dkm_slabs, dv_intra = [], [], []
        for p in range(B_HEADS // 2):
            qm_st = _pair_stack(qm, p)
            km_p = km[:, p * LANES:(p + 1) * LANES].astype(BF16)
            a = _dot_nt(qm_st, km_p)
            da_blocks, dqm_h = [], []
            for hh in range(2):
                h = 2 * p + hh
                vs = slice(h * B_VAL_DIM, (h + 1) * B_VAL_DIM)
                att = jnp.where(mask, a[hh * tile_rows:(hh + 1) * tile_rows], 0.0).astype(BF16)
                dv_intra.append(_dot_tn(att, do_ref[:, vs]))
                da = jnp.where(mask, _dot_nt(do_ref[:, vs], v_ref[:, vs]), 0.0).astype(BF16)
                da_blocks.append(da)
                dqm_h.append(_dot(da, km_p))
            lo = _lane_iota((tile_rows, LANES)) < B_KEY_DIM
            dqm_slabs.append(jnp.where(lo, dqm_h[0], dqm_h[1]))
            dkm_slabs.append(_dot_tn(jnp.concatenate(da_blocks, axis=0), qm_st))
        dqm = jnp.concatenate(dqm_slabs, axis=1)
        dkm = jnp.concatenate(dkm_slabs, axis=1)

        dqe_c, dkd_c, dv_inter, tail_c = ([None] * cpt for _ in range(4))
        for c in reversed(range(cpt)):
            rows = slice(c * B_CHUNK, (c + 1) * B_CHUNK)
            dst = dst_ref[...]
            dst_b = dst.astype(BF16)
            dv_inter[c] = _dot_nt(_stack_masked(kd[rows]), dst_b)
            dkd_c[c] = _select_heads(_dot(_rows_by_head(v_ref[rows, :]), dst_b))
            do_c = do_ref[rows, :]
            dqe_c[c] = _select_heads(_dot(_rows_by_head(do_c), st_all_ref[c].astype(BF16)))
            contrib = _select_state(_dot_tn(do_c, qe[rows].astype(BF16)))
            decay = jnp.exp(cum_ref[pl.ds(c * B_CHUNK + B_CHUNK - 1, 1), :])
            tail = (jnp.sum(kk[rows] * dkd_c[c] * e_kd[rows], axis=0, keepdims=True)
                    + decay * jnp.sum(st_all_ref[c] * dst, axis=0, keepdims=True))
            tail_c[c] = jnp.broadcast_to(tail, (B_CHUNK, B_KEY_WIDTH))
            dst_ref[...] = dst * decay + contrib
        dqe = jnp.concatenate(dqe_c, axis=0)
        dkd = jnp.concatenate(dkd_c, axis=0)
        dqs = dqm * e_qm + dqe * e_qe
        dk = dkm * e_km + dkd * e_kd
        dq_ref[...] = (dqs * scale).astype(BF16)
        dk_ref[...] = dk.astype(BF16)
        for h in range(B_HEADS):
            dvi = jnp.concatenate([dv_inter[c][h * B_CHUNK:(h + 1) * B_CHUNK] for c in range(cpt)], axis=0)
            dv_ref[:, h * B_VAL_DIM:(h + 1) * B_VAL_DIM] = (dv_intra[h] + dvi).astype(BF16)
        dd = qs * dqs - kk * dk
        i, j = _row_iota((tile_rows, tile_rows)), _lane_iota((tile_rows, tile_rows))
        upper = ((i // B_CHUNK == j // B_CHUNK) & (j >= i)).astype(BF16)
        hi, mid, lo3 = _split3(dd)
        dla_ref[...] = _dot(upper, hi) + _dot(upper, mid) + _dot(upper, lo3) + jnp.concatenate(tail_c, axis=0)

    def body(q_ref, k_ref, cum_ref, v_ref, do_ref, st_all_ref, *rest):
        g_refs, (dq_ref, dk_ref, dv_ref, dla_ref) = rest[:nw], rest[nw:nw + 4]
        rv_refs, (dst_ref, send_sems, recv_sems) = rest[nw + 4:2 * nw + 4], rest[2 * nw + 4:]
        x, y, c = _my_place()

        def wcopy(a, r):
            dx, dy, dc = FLIPS[r]
            return pltpu.make_async_remote_copy(
                src_ref=g_refs[a].at[4 * (x ^ dx) + 2 * (y ^ dy) + (c ^ dc)], dst_ref=rv_refs[a].at[r],
                send_sem=send_sems.at[a, r], recv_sem=recv_sems.at[a, r],
                device_id=(x ^ dx, y ^ dy, c ^ dc), device_id_type=MESH)

        @pl.when(pl.program_id(0) == 0)
        def _():
            dst_ref[...] = jnp.zeros_like(dst_ref)
            for a in range(nw):
                for r in range(len(FLIPS)):
                    wcopy(a, r).start()

        for b in range(B):
            for tile in reversed(range(tps)):
                tok = pl.ds(tile * tile_rows, tile_rows)
                chunks = pl.ds(tile * cpt, cpt)
                one_sequence(*[r.at[b, tok] for r in (q_ref, k_ref, cum_ref, v_ref, do_ref)], st_all_ref.at[b, chunks],
                             *[r.at[b, tok] for r in (dq_ref, dk_ref, dv_ref, dla_ref)], dst_ref.at[b])

        @pl.when(pl.program_id(0) == steps - 1)
        def _():
            for a in range(nw):
                for r in range(len(FLIPS)):
                    wcopy(a, r).wait()

    def rows(w):
        return pl.BlockSpec((B, tps * tile_rows, w), lambda t: (0, steps - 1 - t, 0))

    seq = lambda a: a.reshape(B, S, a.shape[-1])
    res = pl.pallas_call(
        body, name="gla_bwd", grid=(steps,),
        in_specs=[rows(B_KEY_WIDTH), rows(B_KEY_WIDTH), rows(B_KEY_WIDTH), rows(B_WIDTH), rows(B_WIDTH),
                  pl.BlockSpec((B, tps * cpt, B_VAL_DIM, B_KEY_WIDTH), lambda t: (0, steps - 1 - t, 0, 0))]
                 + _any_specs(nw),
        out_specs=[rows(B_KEY_WIDTH), rows(B_KEY_WIDTH), rows(B_WIDTH), rows(B_KEY_WIDTH)] + _any_specs(nw),
        out_shape=[jax.ShapeDtypeStruct((B, S, B_KEY_WIDTH), BF16), jax.ShapeDtypeStruct((B, S, B_KEY_WIDTH), BF16),
                   jax.ShapeDtypeStruct((B, S, B_WIDTH), BF16), jax.ShapeDtypeStruct((B, S, B_KEY_WIDTH), F32)]
                  + [jax.ShapeDtypeStruct((len(FLIPS), *g.shape[1:]), g.dtype) for g in wgrads],
        scratch_shapes=[pltpu.VMEM((B, B_VAL_DIM, B_KEY_WIDTH), F32),
                        pltpu.SemaphoreType.DMA((nw, len(FLIPS))), pltpu.SemaphoreType.DMA((nw, len(FLIPS)))],
        compiler_params=_params("arbitrary"),
    )(seq(q), seq(k), seq(cum), seq(vb), seq(do), st_all.reshape(B, S // B_CHUNK, B_VAL_DIM, B_KEY_WIDTH), *wgrads)
    return [a.reshape(T, a.shape[-1]) for a in res[:4]], res[4:]


def _merge(x2, tgt2, attn, za, o_gla, zb, ga, gb, w_oa_sh, w_ob_sh, w_o, g_gla, g_final):
    T = x2.shape[0]
    tm = math.gcd(T, 512)
    sub = math.gcd(tm, 256)
    last = T // tm - 1

    def body(x_ref, tgt_ref, attn_ref, za_ref, og_ref, zb_ref, ga_ref, gb_ref,
             woa_sh_ref, wob_sh_ref, wo_ref, gg_ref, gf_ref,
             dxres_ref, dattn_ref, dog_ref, dza_ref, dzb_ref, dga_ref, dgb_ref,
             dwo_out, dwoa_out, dwob_out, small_ref,
             awo_ref, awoa_ref, awob_ref, agf_ref, agg_ref, loss_ref, woa_ref, wob_ref,
             dwo_ref, dwoa_ref, dwob_ref, w_sems, dw_sems):
        w_copies = [pltpu.make_async_copy(sh.at[j], dst.at[:, j * SHARD_OUT:(j + 1) * SHARD_OUT], w_sems.at[a, j])
                    for a, (sh, dst) in enumerate(((woa_sh_ref, woa_ref), (wob_sh_ref, wob_ref))) for j in range(N_DEV)]
        dw_copies = [pltpu.make_async_copy(src, dst, dw_sems.at[a])
                     for a, (src, dst) in enumerate(((dwo_ref, dwo_out), (dwoa_ref, dwoa_out), (dwob_ref, dwob_out)))]

        @pl.when(pl.program_id(0) == 0)
        def _():
            for cp in w_copies:
                cp.start()
            for r in (awo_ref, awoa_ref, awob_ref, agf_ref, agg_ref, loss_ref):
                r[...] = jnp.zeros_like(r)
            for cp in w_copies:
                cp.wait()

        def one_tile(rows):
            za_v = za_ref[rows, :].astype(F32)
            sig_za = _sigmoid_tanh(za_v)
            silu_a = za_v * sig_za
            attn_v = attn_ref[rows, :].astype(F32)
            oa = (attn_v * silu_a).astype(BF16)
            ya = _dot(oa, woa_ref[...])
            og = og_ref[rows, :].astype(F32)
            zb_v = zb_ref[rows, :].astype(F32)
            sig_zb = _sigmoid_tanh(zb_v)
            silu_b = zb_v * sig_zb
            gg = gg_ref[...]
            on_parts, rinv_parts = [], []
            for h in range(B_HEADS):
                seg = og[:, h * B_VAL_DIM:(h + 1) * B_VAL_DIM]
                rinv = lax.rsqrt(jnp.mean(seg * seg, axis=-1, keepdims=True) + NORM_EPS)
                rinv_parts.append(rinv)
                on_parts.append(seg * rinv)
            on = jnp.concatenate(on_parts, axis=1)
            obn = on * gg
            ob = (obn * silu_b).astype(BF16)
            yb = _dot(ob, wob_ref[...])
            sig_a = _sigmoid_tanh(ga_ref[rows, :].astype(F32))
            sig_b = _sigmoid_tanh(gb_ref[rows, :].astype(F32))
            merged = (sig_a * ya + sig_b * yb).astype(BF16)
            out = x_ref[rows, :] + _dot(merged, wo_ref[...])
            rf = lax.rsqrt(jnp.mean(out * out, axis=-1, keepdims=True) + NORM_EPS)
            nrm = out * rf
            gf = gf_ref[...]
            err = nrm * gf - tgt_ref[rows, :]
            loss = jnp.sum(err * err) * (0.5 / D_MODEL)

            dy = err * (1.0 / D_MODEL)
            dgf = jnp.sum(dy * nrm, axis=0, keepdims=True)
            dn = dy * gf
            dout = rf * (dn - nrm * jnp.mean(dn * nrm, axis=-1, keepdims=True))
            dxres_ref[rows, :] = dout
            dout_b = dout.astype(BF16)
            dmerged = _dot_nt(dout_b, wo_ref[...])
            dya = dmerged * sig_a
            dyb = dmerged * sig_b
            dga_ref[rows, :] = (dmerged * ya * sig_a * (1.0 - sig_a)).astype(BF16)
            dgb_ref[rows, :] = (dmerged * yb * sig_b * (1.0 - sig_b)).astype(BF16)
            dya_b, dyb_b = dya.astype(BF16), dyb.astype(BF16)
            doa = _dot_nt(dya_b, woa_ref[...])
            dattn_ref[rows, :] = (doa * silu_a).astype(BF16)
            dza_ref[rows, :] = (doa * attn_v * (sig_za * (1.0 + za_v * (1.0 - sig_za)))).astype(BF16)
            dob = _dot_nt(dyb_b, wob_ref[...])
            dzb_ref[rows, :] = (dob * obn * (sig_zb * (1.0 + zb_v * (1.0 - sig_zb)))).astype(BF16)
            dobn = dob * silu_b
            dgg = jnp.sum(dobn * on, axis=0, keepdims=True)
            don = dobn * gg
            for h in range(B_HEADS):
                sl = slice(h * B_VAL_DIM, (h + 1) * B_VAL_DIM)
                don_h, on_h = don[:, sl], on[:, sl]
                dog_ref[rows, sl] = (rinv_parts[h] * (don_h - on_h * jnp.mean(don_h * on_h, axis=-1, keepdims=True))
                                     ).astype(BF16)
            return (merged, dout_b, oa, dya_b, ob, dyb_b), (loss, dgf, dgg)

        tiles = [one_tile(pl.ds(j * sub, sub)) for j in range(tm // sub)]
        merged, dout_b, oa, dya_b, ob, dyb_b = (jnp.concatenate(parts, axis=0) for parts in zip(*[t[0] for t in tiles]))
        awo_ref[...] += _dot_tn(merged, dout_b)
        awoa_ref[...] += _dot_tn(oa, dya_b)
        awob_ref[...] += _dot_tn(ob, dyb_b)
        for _, (loss, dgf, dgg) in tiles:
            loss_ref[...] += loss
            agf_ref[...] += dgf
            agg_ref[...] += dgg

        @pl.when(pl.program_id(0) == last)
        def _():
            for j in range(N_DEV):
                dwo_ref[j] = awo_ref[j * SHARD_OUT:(j + 1) * SHARD_OUT, :].astype(BF16)
                dwoa_ref[j] = awoa_ref[:, j * SHARD_OUT:(j + 1) * SHARD_OUT].astype(BF16)
                dwob_ref[j] = awob_ref[:, j * SHARD_OUT:(j + 1) * SHARD_OUT].astype(BF16)
            small_ref[...] = jnp.zeros_like(small_ref)
            _put_rows(small_ref, SMALL_G_FINAL, agf_ref[...])
            _put_rows(small_ref, SMALL_G_GLA, agg_ref[...])
            small_ref[SMALL_LOSS:SMALL_LOSS + 1, :] = loss_ref[...]
            for cp in dw_copies:
                cp.start()
            for cp in dw_copies:
                cp.wait()

    def rows(w):
        return pl.BlockSpec((tm, w), lambda i: (i, 0))

    def whole(shape):
        nd = len(shape)
        return pl.BlockSpec(shape, lambda i: (0,) * nd)

    outs = [((T, D_MODEL), F32, rows(D_MODEL)), ((T, A_WIDTH), BF16, rows(A_WIDTH)), ((T, B_WIDTH), BF16, rows(B_WIDTH)),
            ((T, A_WIDTH), BF16, rows(A_WIDTH)), ((T, B_WIDTH), BF16, rows(B_WIDTH)),
            ((T, D_MODEL), BF16, rows(D_MODEL)), ((T, D_MODEL), BF16, rows(D_MODEL)),
            ((N_DEV, SHARD_OUT, D_MODEL), BF16, pl.BlockSpec(memory_space=pl.ANY)),
            ((N_DEV, A_WIDTH, SHARD_OUT), BF16, pl.BlockSpec(memory_space=pl.ANY)),
            ((N_DEV, B_WIDTH, SHARD_OUT), BF16, pl.BlockSpec(memory_space=pl.ANY)),
            ((SMALL_SINKS, LANES), F32, whole((SMALL_SINKS, LANES)))]
    return pl.pallas_call(
        body, name="merge", grid=(T // tm,),
        in_specs=[rows(D_MODEL), rows(D_MODEL), rows(A_WIDTH), rows(A_WIDTH), rows(B_WIDTH), rows(B_WIDTH),
                  rows(D_MODEL), rows(D_MODEL),
                  pl.BlockSpec(memory_space=pl.ANY), pl.BlockSpec(memory_space=pl.ANY),
                  _const_spec((D_MODEL, D_MODEL)), _const_spec((1, B_WIDTH)), _const_spec((1, D_MODEL))],
        out_specs=[o[2] for o in outs],
        out_shape=[jax.ShapeDtypeStruct(o[0], o[1]) for o in outs],
        scratch_shapes=[pltpu.VMEM((D_MODEL, D_MODEL), F32), pltpu.VMEM((A_WIDTH, D_MODEL), F32),
                        pltpu.VMEM((B_WIDTH, D_MODEL), F32), pltpu.VMEM((1, D_MODEL), F32), pltpu.VMEM((1, B_WIDTH), F32),
                        pltpu.VMEM((1, LANES), F32), pltpu.VMEM((A_WIDTH, D_MODEL), BF16),
                        pltpu.VMEM((B_WIDTH, D_MODEL), BF16),
                        pltpu.VMEM((N_DEV, SHARD_OUT, D_MODEL), BF16), pltpu.VMEM((N_DEV, A_WIDTH, SHARD_OUT), BF16),
                        pltpu.VMEM((N_DEV, B_WIDTH, SHARD_OUT), BF16),
                        pltpu.SemaphoreType.DMA((2, N_DEV)), pltpu.SemaphoreType.DMA((3,))],
        compiler_params=pltpu.CompilerParams(dimension_semantics=("arbitrary",), vmem_limit_bytes=V7X_VMEM_LIMIT_MAX),
    )(x2, tgt2, attn, za, o_gla, zb, ga, gb, w_oa_sh, w_ob_sh, w_o, g_gla, g_final)


def _in_proj_bwd(x2, dxres, cosf, sinf, g_in, wt_pad, wa_pad, parts):
    T = x2.shape[0]
    tm = math.gcd(T, 512)
    sub = math.gcd(tm, 256)
    last = T // tm - 1
    base = SMALL_G_IN

    def body(x_ref, dxres_ref, cos_ref, sin_ref, g_ref, wt_ref, wa_ref,
             dq_ref, dkv_ref, dza_ref, dqb_ref, dkb_ref, dvb_ref, dzb_ref, dla_ref, u_ref, alr_ref, dga_ref, dgb_ref,
             dx_ref, dsh_ref, small_ref, dproj_ref, agin_ref, aba_ref, awa_ref):
        @pl.when(pl.program_id(0) == 0)
        def _():
            for r in (agin_ref, aba_ref, awa_ref):
                r[...] = jnp.zeros_like(r)

        def one_tile(rows):
            cos, nsin = cos_ref[rows, :], -sin_ref[rows, :]
            for s in range(A_WIDTH // LANES):
                sl = slice(s * LANES, (s + 1) * LANES)
                dproj_ref[rows, sl] = _rope_slab(dq_ref[rows, sl].astype(F32), cos, nsin).astype(BF16)
            dproj_ref[rows, QKV_K:QKV_V] = _rope_slab(dkv_ref[rows, 0:LANES].astype(F32), cos, nsin).astype(BF16)
            dproj_ref[rows, QKV_V:QKV_W] = dkv_ref[rows, LANES:]

            def put(name, val):
                a, b = SEG[name]
                dproj_ref[rows, a:b] = val

            put("za", dza_ref[rows, :])
            put("qb", dqb_ref[rows, :])
            put("kb", dkb_ref[rows, :])
            put("vb", dvb_ref[rows, :])
            put("zb", dzb_ref[rows, :])
            put("ga", dga_ref[rows, :])
            put("gb", dgb_ref[rows, :])
            du = dla_ref[rows, :] * (1.0 / B_GATE_TEMP) * _sigmoid(-u_ref[rows, :])
            du_b = du.astype(BF16)
            put("alr", _dot_nt(du_b, wa_ref[...]).astype(BF16))

            for j in range(N_DEV):
                col = (j % 2) * SHARD_PAD
                for a, b in _shard_pad_cols(j):
                    dsh_ref[j // 2, rows, col:col + b - a] = dproj_ref[rows, a:b]
                    col += b - a
                dsh_ref[j // 2, rows, col:(j % 2 + 1) * SHARD_PAD] = jnp.zeros((sub, SHARD_PAD - SHARD_IN), BF16)

            dh = _dot(dproj_ref[rows, :], wt_ref[...])
            x = x_ref[rows, :]
            r = lax.rsqrt(jnp.mean(x * x, axis=-1, keepdims=True) + NORM_EPS)
            nrm = x * r
            dn = dh * g_ref[...]
            dx_ref[rows, :] = dxres_ref[rows, :] + r * (dn - nrm * jnp.mean(dn * nrm, axis=-1, keepdims=True))
            return jnp.sum(dh * nrm, axis=0, keepdims=True), jnp.sum(du, axis=0, keepdims=True), alr_ref[rows, :], du_b

        for j in range(tm // sub):
            dgin, dba, alr, du_b = one_tile(pl.ds(j * sub, sub))
            agin_ref[...] += dgin
            aba_ref[...] += dba
            awa_ref[...] += _dot_tn(alr, du_b)

        @pl.when(pl.program_id(0) == last)
        def _():
            small_ref[...] = jnp.zeros_like(small_ref)
            _put_rows(small_ref, SMALL_G_IN - base, agin_ref[...])
            _put_rows(small_ref, SMALL_B_ALPHA - base, aba_ref[...])
            for half in range(B_KEY_WIDTH // LANES):
                r0 = SMALL_W_ALPHA - base + half * B_GATE_RANK
                small_ref[r0:r0 + B_GATE_RANK, :] = awa_ref[0:B_GATE_RANK, half * LANES:(half + 1) * LANES]

    def rows(w):
        return pl.BlockSpec((tm, w), lambda i: (i, 0))

    names = ["dq", "dkv", "dza", "dqb", "dkb", "dvb", "dzb", "dla", "u", "alr", "dga", "dgb"]
    return pl.pallas_call(
        body, name="in_proj_bwd", grid=(T // tm,),
        in_specs=[rows(D_MODEL), rows(D_MODEL), rows(LANES), rows(LANES), _const_spec((1, D_MODEL)),
                  _const_spec((D_IN_PAD, D_MODEL)), _const_spec((RANK_PAD, B_KEY_WIDTH))]
                 + [rows(parts[n].shape[1]) for n in names],
        out_specs=[rows(D_MODEL), pl.BlockSpec((N_CHIPS, tm, 2 * SHARD_PAD), lambda i: (0, i, 0)),
                   pl.BlockSpec((SMALL_ROWS - base, LANES), lambda i: (0, 0))],
        out_shape=[jax.ShapeDtypeStruct((T, D_MODEL), F32), jax.ShapeDtypeStruct((N_CHIPS, T, 2 * SHARD_PAD), BF16),
                   jax.ShapeDtypeStruct((SMALL_ROWS - base, LANES), F32)],
        scratch_shapes=[pltpu.VMEM((tm, D_IN_PAD), BF16), pltpu.VMEM((1, D_MODEL), F32), pltpu.VMEM((1, B_KEY_WIDTH), F32),
                        pltpu.VMEM((RANK_PAD, B_KEY_WIDTH), F32)],
        compiler_params=pltpu.CompilerParams(dimension_semantics=("arbitrary",), vmem_limit_bytes=V7X_VMEM_LIMIT_MAX),
    )(x2, dxres, cosf, sinf, g_in, wt_pad, wa_pad, *[parts[n] for n in names])


FLIPS = [(dx, dy, dc) for dx in (0, 1) for dy in (0, 1) for dc in (0, 1)][1:]


def _my_place():
    return lax.axis_index("x"), lax.axis_index("y"), lax.axis_index("c")


def _any_specs(n):
    return [pl.BlockSpec(memory_space=pl.ANY)] * n


def _gather_first(shards, pos_col):
    n = len(shards)
    T = pos_col.shape[0]
    rows_per_pass = math.gcd(T, 512)
    invf, sign = _rope_lane_constants()

    def body(*refs):
        ins, (pos_ref, invf_ref, sign_ref) = refs[:n], refs[n:n + 3]
        outs, (cos_ref, sin_ref) = refs[n + 3:2 * n + 3], refs[2 * n + 3:2 * n + 5]
        send_sems, recv_sems, local_sems = refs[2 * n + 5:]
        x, y, c = _my_place()
        me, sibling = (x, y, c), (x, y, 1 - c)
        onward, source, diagonal = (x ^ (1 - c), y ^ c), (x ^ c, y ^ (1 - c)), (1 - x, 1 - y)

        def block(a, px, py, pc):
            return outs[a].at[4 * px + 2 * py + pc]

        def copy(a, k, blk, to, src=None):
            return pltpu.make_async_remote_copy(
                src_ref=block(a, *blk) if src is None else src, dst_ref=block(a, *blk),
                send_sem=send_sems.at[a, k], recv_sem=recv_sems.at[a, k], device_id=to, device_id_type=MESH)

        mine = [pltpu.make_async_copy(ins[a], block(a, *me), local_sems.at[a]) for a in range(n)]
        for cp in mine:
            cp.start()
        first = []
        for a in range(n):
            first.append(copy(a, 0, me, sibling, src=ins[a]))
            first += [copy(a, 1 + j, me, (*chip, c), src=ins[a]) for j, chip in enumerate((onward, source))]
        for cp in first:
            cp.start()

        def tables(i, carry):
            rows = pl.ds(pl.multiple_of(i * rows_per_pass, rows_per_pass), rows_per_pass)
            ang = pos_ref[rows, :].astype(F32) * invf_ref[...]
            cos_ref[rows, :] = jnp.cos(ang)
            sin_ref[rows, :] = jnp.sin(ang) * sign_ref[...]
            return carry

        lax.fori_loop(0, T // rows_per_pass, tables, 0)

        passed = []
        for j, chip in ((1, source), (0, onward), (2, diagonal)):
            for a in range(n):
                copy(a, 1 + j, (*chip, c), me).wait_recv()
                todo = [copy(a, 4 + j, (*chip, c), sibling)]
                if j == 1:
                    todo.append(copy(a, 1 + 2, (*chip, c), (*onward, c)))
                for cp in todo:
                    cp.start()
                passed += todo
        for a in range(n):
            copy(a, 0, sibling, me).wait_recv()
            for j, chip in enumerate((source, onward, diagonal)):
                copy(a, 4 + j, (*chip, 1 - c), me).wait_recv()
        for cp in first + passed:
            cp.wait_send()
        for cp in mine:
            cp.wait()

    vmem = pl.BlockSpec(memory_space=pltpu.VMEM)
    res = pl.pallas_call(
        body, name="gather_weights",
        in_specs=_any_specs(n) + [vmem] * 3, out_specs=_any_specs(n) + [vmem] * 2,
        out_shape=[jax.ShapeDtypeStruct((N_DEV, *s.shape), s.dtype) for s in shards]
                  + [jax.ShapeDtypeStruct((T, LANES), F32)] * 2,
        scratch_shapes=[pltpu.SemaphoreType.DMA((n, 7)), pltpu.SemaphoreType.DMA((n, 7)), pltpu.SemaphoreType.DMA((n,))],
        compiler_params=pltpu.CompilerParams(vmem_limit_bytes=V7X_VMEM_LIMIT),
    )(*shards, pos_col, invf, sign)
    return res[:n], res[n], res[n + 1]


def _w_in_grad_rs(h, dsh, chip_order, small):
    T = h.shape[0]
    tk = math.gcd(T, 2048)
    nk = T // tk
    chip_flips = [(1, 1), (1, 0), (0, 1)]
    n_steps = len(chip_flips) + 1
    SIB = len(chip_flips)
    halves = (slice(0, SHARD_PAD), slice(SHARD_PAD, 2 * SHARD_PAD))

    def body(order_ref, h_ref, d_ref, s_ref, own_ref, recv_ref, sall_ref,
             acc_ref, pre_ref, to_sib_ref, to_chip_ref,
             sib_send, sib_recv, chip_send, chip_recv, ssend_sems, srecv_sems, local_sem):
        i, kk = pl.program_id(0), pl.program_id(1)
        x, y, c = _my_place()
        my_dev = 4 * x + 2 * y + c

        def small_copy(r, slot):
            dx, dy, dc = FLIPS[r]
            return pltpu.make_async_remote_copy(
                src_ref=s_ref, dst_ref=sall_ref.at[slot], send_sem=ssend_sems.at[r], recv_sem=srecv_sems.at[r],
                device_id=(x ^ dx, y ^ dy, c ^ dc), device_id_type=MESH)

        keep_small = pltpu.make_async_copy(s_ref, sall_ref.at[my_dev], local_sem)

        def sib_copy(t):
            dst = recv_ref.at[SIB] if t == SIB else pre_ref.at[t]
            return pltpu.make_async_remote_copy(
                src_ref=to_sib_ref.at[t], dst_ref=dst, send_sem=sib_send.at[t], recv_sem=sib_recv.at[t],
                device_id=(x, y, 1 - c), device_id_type=MESH)

        def chip_copy(t):
            dx, dy = chip_flips[t]
            return pltpu.make_async_remote_copy(
                src_ref=to_chip_ref.at[t], dst_ref=recv_ref.at[t], send_sem=chip_send.at[t], recv_sem=chip_recv.at[t],
                device_id=(x ^ dx, y ^ dy, c), device_id_type=MESH)

        def accumulate(rows):
            acc_ref[rows, :] += _dot_tn(d_ref[:, rows], h_ref[...])

        @pl.when((i == 0) & (kk == 0))
        def _():
            keep_small.start()
            for r in range(len(FLIPS)):
                small_copy(r, my_dev).start()

        @pl.when(kk == 0)
        def _():
            acc_ref[...] = jnp.zeros_like(acc_ref)

        @pl.when(kk < nk - 1)
        def _():
            accumulate(slice(0, 2 * SHARD_PAD))

        for core in range(2):
            @pl.when((kk == nk - 1) & (c == core))
            def _(mine=halves[core], theirs=halves[1 - core]):
                accumulate(theirs)
                for t in range(n_steps):
                    @pl.when(i == t)
                    def _(t=t):
                        to_sib_ref[t] = acc_ref[theirs, :].astype(BF16)
                        sib_copy(t).start()
                accumulate(mine)
                for t in range(len(chip_flips)):
                    @pl.when(i == t)
                    def _(t=t):
                        sib_copy(t).wait_recv()
                        to_chip_ref[t] = (acc_ref[mine, :] + pre_ref[t].astype(F32)).astype(BF16)
                        chip_copy(t).start()

                @pl.when(i == n_steps - 1)
                def _():
                    own_ref[...] = acc_ref[mine, :]

        @pl.when((i == n_steps - 1) & (kk == nk - 1))
        def _():
            for t in range(len(chip_flips)):
                sib_copy(t).wait_send()
                chip_copy(t).wait_send()
                chip_copy(t).wait_recv()
            sib_copy(SIB).wait_send()
            sib_copy(SIB).wait_recv()
            for r, (dx, dy, dc) in enumerate(FLIPS):
                small_copy(r, 4 * (x ^ dx) + 2 * (y ^ dy) + (c ^ dc)).wait_recv()
                small_copy(r, my_dev).wait_send()
            keep_small.wait()

    shard = (SHARD_PAD, D_MODEL)
    return pl.pallas_call(
        body, name="w_in_grad_rs",
        grid_spec=pltpu.PrefetchScalarGridSpec(
            num_scalar_prefetch=1, grid=(n_steps, nk),
            in_specs=[pl.BlockSpec((tk, D_MODEL), lambda i, kk, order: (kk, 0)),
                      pl.BlockSpec((None, tk, 2 * SHARD_PAD), lambda i, kk, order: (order[i], kk, 0)),
                      pl.BlockSpec(memory_space=pl.ANY)],
            out_specs=[pl.BlockSpec(shard, lambda i, kk, order: (0, 0)),
                       pl.BlockSpec(memory_space=pl.ANY), pl.BlockSpec(memory_space=pl.ANY)],
            scratch_shapes=[pltpu.VMEM((2 * SHARD_PAD, D_MODEL), F32),
                            pltpu.VMEM((SIB, *shard), BF16), pltpu.VMEM((SIB + 1, *shard), BF16),
                            pltpu.VMEM((SIB, *shard), BF16),
                            pltpu.SemaphoreType.DMA((SIB + 1,)), pltpu.SemaphoreType.DMA((SIB + 1,)),
                            pltpu.SemaphoreType.DMA((SIB,)), pltpu.SemaphoreType.DMA((SIB,)),
                            pltpu.SemaphoreType.DMA((7,)), pltpu.SemaphoreType.DMA((7,)), pltpu.SemaphoreType.DMA]),
        out_shape=[jax.ShapeDtypeStruct(shard, F32),
                   jax.ShapeDtypeStruct((SIB + 1, *shard), BF16),
                   jax.ShapeDtypeStruct((N_DEV, *small.shape), F32)],
        compiler_params=_params("arbitrary", "arbitrary"),
    )(chip_order, h, dsh, small)


def _adam_math(w, g, m, v):
    m_new = ADAM_B1 * m + (1.0 - ADAM_B1) * g
    v_new = ADAM_B2 * v + (1.0 - ADAM_B2) * (g * g)
    m_hat = m_new / (1.0 - ADAM_B1 ** ADAM_STEP)
    v_hat = v_new / (1.0 - ADAM_B2 ** ADAM_STEP)
    delta = -ADAM_LR * (m_hat / (jnp.sqrt(v_hat) + ADAM_EPS) + ADAM_WD * w)
    return delta, m_new, v_new


def _adam_big(jobs):
    steps = 8
    n = len(jobs)
    idx = jnp.stack([job[1] for job in jobs]).astype(jnp.int32)
    blocks = []
    for own, _, recv, w, m, v in jobs:
        (rw, cw), rp = w.shape, own.shape[1]
        by_cols = rp != rw
        blk_w = (rw, cw // steps) if by_cols else (rw // steps, cw)
        blk_g = (rp, cw // steps) if by_cols else (rw // steps, cw)
        blocks.append((blk_w, blk_g, by_cols))

    def body(idx_ref, *refs):
        ins, outs = refs[:5 * n], refs[5 * n:]
        for j, (blk_w, _, _) in enumerate(blocks):
            o_ref, r_ref, w_ref, m_ref, v_ref = ins[5 * j:5 * j + 5]
            g_ref, d_ref, mo_ref, vo_ref = outs[4 * j:4 * j + 4]
            g = o_ref[...].astype(F32)
            for r in range(r_ref.shape[0]):
                g = g + r_ref[r].astype(F32)
            g = g[0:blk_w[0], :]
            g_ref[...] = g
            d_ref[...], mo_ref[...], vo_ref[...] = _adam_math(w_ref[...], g, m_ref[...], v_ref[...])

    in_specs, out_specs, out_shape, args = [], [], [], []
    for j, ((own, _, recv, w, m, v), (blk_w, blk_g, by_cols)) in enumerate(zip(jobs, blocks)):
        at = (lambda i: (0, i)) if by_cols else (lambda i: (i, 0))
        spec = pl.BlockSpec(blk_w, lambda i, idx_ref, at=at: at(i))
        in_specs += [pl.BlockSpec((None, *blk_g), lambda i, idx_ref, at=at, j=j: (idx_ref[j], *at(i))),
                     pl.BlockSpec((recv.shape[0], *blk_g), lambda i, idx_ref, at=at: (0, *at(i))), spec, spec, spec]
        out_specs += [spec] * 4
        out_shape += [jax.ShapeDtypeStruct(w.shape, F32)] * 4
        args += [own, recv, w, m, v]
    res = pl.pallas_call(
        body, name="adam_big",
        grid_spec=pltpu.PrefetchScalarGridSpec(num_scalar_prefetch=1, grid=(steps,), in_specs=in_specs, out_specs=out_specs),
        out_shape=out_shape,
        compiler_params=_params("parallel"),
    )(idx, *args)
    return [res[4 * j:4 * j + 4] for j in range(n)]


def _adam_small(small_all, params):
    flat = [a for triple in params for a in triple]
    n_par = len(params)

    def body(s_ref, *refs):
        ins, outs, loss_ref = refs[:3 * n_par], refs[3 * n_par:-1], refs[-1]
        g_slab = s_ref[0]
        for dev in range(1, N_DEV):
            g_slab = g_slab + s_ref[dev]
        loss_ref[...] = g_slab[SMALL_LOSS:SMALL_LOSS + 1, :]
        dev = 4 * lax.axis_index("x") + 2 * lax.axis_index("y") + lax.axis_index("c")
        alpha_full = jnp.concatenate([g_slab[SMALL_W_ALPHA + half * B_GATE_RANK:SMALL_W_ALPHA + (half + 1) * B_GATE_RANK]
                                      for half in range(B_KEY_WIDTH // LANES)], axis=1)
        alpha_mine = pltpu.roll(alpha_full, (B_KEY_WIDTH - dev * SHARD_ALPHA) % B_KEY_WIDTH, 1)[:, 0:SHARD_ALPHA]
        grads = [_take_rows(g_slab, SMALL_G_IN, D_MODEL // LANES), _take_rows(g_slab, SMALL_G_FINAL, D_MODEL // LANES),
                 _take_rows(g_slab, SMALL_G_GLA, B_WIDTH // LANES), _take_rows(g_slab, SMALL_B_ALPHA, B_KEY_WIDTH // LANES),
                 g_slab[SMALL_SINKS:SMALL_SINKS + 1, 0:A_HEADS], alpha_mine]
        for i, g in enumerate(grads):
            w_ref, m_ref, v_ref = ins[3 * i:3 * i + 3]
            delta, m_new, v_new = _adam_math(w_ref[...], g, m_ref[...], v_ref[...])
            outs[4 * i][...] = g
            outs[4 * i + 1][...] = delta
            outs[4 * i + 2][...] = m_new
            outs[4 * i + 3][...] = v_new

    res = pl.pallas_call(
        body, name="adam_small",
        out_shape=[jax.ShapeDtypeStruct(t[0].shape, F32) for t in params for _ in range(4)]
                  + [jax.ShapeDtypeStruct((1, LANES), F32)],
    )(small_all, *flat)
    return [res[4 * i:4 * i + 4] for i in range(n_par)], res[-1]


def _local_step(x, cosf, sinf, loss_target, g_in, wt_sh, wa_pad, b_alpha, sinks, g_gla, out_shards, g_final, chip_order):
    B, S, _ = x.shape
    T = B * S
    x2 = x.reshape(T, D_MODEL)
    tgt2 = loss_target.reshape(T, D_MODEL)
    f, (g_woa, g_wob, g_wo) = _in_proj(x2, cosf, sinf, g_in, wt_sh, wa_pad, b_alpha, out_shards)
    w_o = g_wo.reshape(D_MODEL, D_MODEL)
    sink_row = jnp.repeat(sinks, WINDOW).reshape(1, ATT_ROWS)
    sink_col = sink_row.reshape(ATT_ROWS, 1)
    attn, lse = _attn_fwd(f["qkv"], sink_row, B, S)
    o_gla, st_all = _gla_fwd(f["q"], f["k"], f["cum"], f["vb"], B, S)
    (dxres, dattn, dog, dza, dzb, dga, dgb, dw_o, dw_oa, dw_ob, small_a) = _merge(
        x2, tgt2, attn, f["za"], o_gla, f["zb"], f["ga"], f["gb"], g_woa, g_wob, w_o, g_gla, g_final)
    dq, dkv, dsink = _attn_bwd(f["qkv"], dattn, attn, lse, sink_col, B, S)
    (dqb, dkb, dvb, dla), (rv_o, rv_oa, rv_ob) = _gla_bwd(f["q"], f["k"], f["cum"], f["vb"], dog, st_all, B, S,
                                                        [dw_o, dw_oa, dw_ob])
    parts = dict(dq=dq, dkv=dkv, dza=dza, dqb=dqb, dkb=dkb, dvb=dvb, dzb=dzb, dla=dla, u=f["u"], alr=f["alr"],
                 dga=dga, dgb=dgb)
    dx, dsh, small_c = _in_proj_bwd(x2, dxres, cosf, sinf, g_in, f["wt_pad"], wa_pad, parts)
    small = jnp.concatenate([small_a, dsink, small_c], axis=0)
    own_in, rv_in, small_all = _w_in_grad_rs(f["h"], dsh, chip_order, small)
    return dict(grad_x=dx.reshape(B, S, D_MODEL), own_in=own_in, rv_in=rv_in,
                own_o=dw_o, rv_o=rv_o, own_oa=dw_oa, rv_oa=rv_oa, own_ob=dw_ob, rv_ob=rv_ob, small_all=small_all)


def kernel(x, positions, g_in, w_in, w_alpha_up, b_alpha, attn_sinks, g_gla_norm, w_out_a, w_out_b, w_o, g_final, loss_target, m_g_in, m_w_in, m_w_alpha_up, m_b_alpha, m_attn_sinks, m_g_gla_norm, m_w_out_a, m_w_out_b, m_w_o, m_g_final, v_g_in, v_w_in, v_w_alpha_up, v_b_alpha, v_attn_sinks, v_g_gla_norm, v_w_out_a, v_w_out_b, v_w_o, v_g_final):
    xi, yi, ci = _my_place()
    chip = 2 * xi + yi
    chip_order = jnp.stack([chip ^ 3, chip ^ 2, chip ^ 1, chip]).astype(jnp.int32)

    (g_win, g_wa), cosf, sinf = _gather_first(
        [jnp.pad(w_in[0].T.astype(BF16), ((0, SHARD_PAD - SHARD_IN), (0, 0))), w_alpha_up[0].astype(BF16)],
        positions.reshape(-1, 1))
    wt_sh = g_win.reshape(N_DEV * SHARD_PAD, D_MODEL)
    wa_pad = jnp.pad(jnp.concatenate([g_wa[j] for j in range(N_DEV)], axis=1), ((0, RANK_PAD - B_GATE_RANK), (0, 0)))

    r = _local_step(x, cosf, sinf, loss_target, g_in, wt_sh, wa_pad, b_alpha, attn_sinks[0], g_gla_norm,
                    [w_out_a[0].astype(BF16), w_out_b[0].astype(BF16), w_o[0].astype(BF16)],
                    g_final.reshape(1, D_MODEL), chip_order)

    dev = 4 * xi + 2 * yi + ci
    big = _adam_big([(r["own_in"][None], jnp.int32(0), r["rv_in"], w_in[0].T, m_w_in[0].T, v_w_in[0].T),
                     (r["own_oa"], dev, r["rv_oa"], w_out_a[0], m_w_out_a[0], v_w_out_a[0]),
                     (r["own_ob"], dev, r["rv_ob"], w_out_b[0], m_w_out_b[0], v_w_out_b[0]),
                     (r["own_o"], dev, r["rv_o"], w_o[0], m_w_o[0], v_w_o[0])])
    big[0] = [a.T for a in big[0]]
    row = lambda a: a.reshape(1, D_MODEL)
    (s_in, s_final, s_gla, s_ba, s_sinks, s_wa), loss_row = _adam_small(r["small_all"], [
        (g_in, m_g_in, v_g_in), (row(g_final), row(m_g_final), row(v_g_final)),
        (g_gla_norm, m_g_gla_norm, v_g_gla_norm), (b_alpha, m_b_alpha, v_b_alpha),
        (attn_sinks, m_attn_sinks, v_attn_sinks), (w_alpha_up[0], m_w_alpha_up[0], v_w_alpha_up[0])])

    def group(i):
        return (s_in[i], big[0][i][None], s_wa[i][None], s_ba[i], s_sinks[i], s_gla[i], big[1][i][None], big[2][i][None],
                big[3][i][None], s_final[i].reshape(D_MODEL))

    return (loss_row[0, 0], r["grad_x"], *group(0), *group(1), *group(2), *group(3))
```

```python
import functools
import math

import numpy as np
import jax
import jax.numpy as jnp
from jax import lax
from jax.experimental import pallas as pl
from jax.experimental.pallas import tpu as pltpu

F32 = jnp.float32
BF16 = jnp.bfloat16
MESH = pl.DeviceIdType.MESH

D_MODEL = 1024
A_HEADS, A_KV_HEADS, A_HEAD_DIM = 8, 2, 64
A_WIDTH, A_KV_WIDTH = 512, 128
WINDOW = 128
ROPE_THETA = 500000.0
ROPE_DIM = 16
B_HEADS, B_KEY_DIM, B_VAL_DIM = 4, 64, 128
B_KEY_WIDTH, B_WIDTH = 256, 512
B_GATE_RANK = 16
B_GATE_TEMP = 16.0
B_CHUNK = 64
NORM_EPS = 1e-6
NEG_BIG = -1e30
D_IN = 4880
N_DEV = 8
N_CHIPS = 4
ADAM_LR, ADAM_B1, ADAM_B2, ADAM_EPS, ADAM_WD, ADAM_STEP = 0.001, 0.9, 0.999, 1e-08, 0.01, 10

LANES = 128
V7X_VMEM_LIMIT = 56 * 1024 * 1024
V7X_VMEM_LIMIT_MAX = 62 * 1024 * 1024

RANK_PAD = LANES
SEG = {}
_off = 0
for _name, _w in (("qa", 512), ("ka", 128), ("va", 128), ("za", 512), ("qb", 256), ("kb", 256),
                  ("vb", 512), ("zb", 512), ("alr", RANK_PAD), ("ga", 1024), ("gb", 1024)):
    SEG[_name] = (_off, _off + _w)
    _off += _w
D_IN_PAD = _off
ALR_SRC = SEG["alr"][0]
QKV_K, QKV_V, QKV_W = SEG["ka"][0], SEG["va"][0], SEG["va"][1]
ATT_SCALE = A_HEAD_DIM ** -0.5

SHARD_IN = D_IN // N_DEV
SHARD_PAD = 640
SHARD_OUT = D_MODEL // N_DEV
SHARD_ALPHA = B_KEY_WIDTH // N_DEV

SMALL_G_FINAL, SMALL_G_GLA, SMALL_LOSS, SMALL_SINKS, SMALL_G_IN, SMALL_B_ALPHA, SMALL_W_ALPHA = 0, 8, 12, 16, 24, 32, 40
SMALL_ROWS = 72


def _dot(a, b):
    return jnp.dot(a, b, preferred_element_type=F32)


def _dot_nt(a, b):
    return lax.dot_general(a, b, (((1,), (1,)), ((), ())), preferred_element_type=F32)


def _dot_tn(a, b):
    return lax.dot_general(a, b, (((0,), (0,)), ((), ())), preferred_element_type=F32)


def _sigmoid(z):
    return 1.0 / (1.0 + jnp.exp(-z))


def _sigmoid_tanh(z):
    return 0.5 * jnp.tanh(0.5 * z) + 0.5


def _params(*sem):
    return pltpu.CompilerParams(dimension_semantics=sem, vmem_limit_bytes=V7X_VMEM_LIMIT)


def _const_spec(shape):
    nd = len(shape)
    return pl.BlockSpec(shape, lambda *_: (0,) * nd, pipeline_mode=pl.Buffered(1))


def _lane_iota(shape):
    return lax.broadcasted_iota(jnp.int32, shape, 1)


def _row_iota(shape):
    return lax.broadcasted_iota(jnp.int32, shape, 0)


def _split3(v):
    hi = v.astype(BF16)
    r1 = v - hi.astype(F32)
    mid = r1.astype(BF16)
    lo = (r1 - mid.astype(F32)).astype(BF16)
    return hi, mid, lo


def _put_rows(ref, row0, vec):
    for r in range(vec.shape[1] // LANES):
        ref[row0 + r:row0 + r + 1, :] = vec[:, r * LANES:(r + 1) * LANES]


def _take_rows(slab, row0, n):
    return jnp.concatenate([slab[row0 + r:row0 + r + 1, :] for r in range(n)], axis=1)


def _rope_lane_constants():
    half = ROPE_DIM // 2
    inv_freq = np.exp(-math.log(ROPE_THETA) * np.arange(half, dtype=np.float32) * np.float32(2.0 / ROPE_DIM)).astype(np.float32)
    lane = np.arange(LANES)
    j = lane % A_HEAD_DIM
    invf = np.where(j < ROPE_DIM, inv_freq[j % half], 0.0).astype(np.float32)
    sign = np.where(j < half, -1.0, np.where(j < ROPE_DIM, 1.0, 0.0)).astype(np.float32)
    return jnp.asarray(invf)[None, :], jnp.asarray(sign)[None, :]


def _rope_slab(t, cos, sin_signed):
    first = (_lane_iota(t.shape) % A_HEAD_DIM) < (ROPE_DIM // 2)
    partner = jnp.where(first, pltpu.roll(t, LANES - ROPE_DIM // 2, 1), pltpu.roll(t, ROPE_DIM // 2, 1))
    return t * cos + partner * sin_signed


def _shard_pad_cols(j):
    cut = ALR_SRC + B_GATE_RANK
    shift = RANK_PAD - B_GATE_RANK
    a, b = j * SHARD_IN, (j + 1) * SHARD_IN
    if b <= cut:
        return [(a, b)]
    if a >= cut:
        return [(a + shift, b + shift)]
    return [(a, cut), (cut + shift, b + shift)]


def _in_proj(x2, cosf, sinf, g_in, wt_sh, wa_pad, b_alpha, later_shards):
    T = x2.shape[0]
    tm = math.gcd(T, 512)
    sub = math.gcd(tm, 256)
    last = T // tm - 1
    nl = len(later_shards)

    def body(x_ref, cos_ref, sin_ref, g_ref, wsh_ref, wa_ref, ba_ref, *rest):
        sh_refs, rest = rest[:nl], rest[nl:]
        (h_ref, qkv_ref, za_ref, q_ref, k_ref, vb_ref, zb_ref, alr_ref, u_ref, cum_ref, ga_ref, gb_ref, wt_out) = rest[:13]
        all_refs, (wt_ref, send_sems, recv_sems, local_sems, wt_sem) = rest[13:13 + nl], rest[13 + nl:]
        wt_copy = pltpu.make_async_copy(wt_ref, wt_out, wt_sem)
        px, py, pc = _my_place()
        my_dev = 4 * px + 2 * py + pc

        def wcopy(a, r, slot):
            dx, dy, dc = FLIPS[r]
            return pltpu.make_async_remote_copy(
                src_ref=sh_refs[a], dst_ref=all_refs[a].at[slot], send_sem=send_sems.at[a, r],
                recv_sem=recv_sems.at[a, r], device_id=(px ^ dx, py ^ dy, pc ^ dc), device_id_type=MESH)

        keep = [pltpu.make_async_copy(sh_refs[a], all_refs[a].at[my_dev], local_sems.at[a]) for a in range(nl)]

        @pl.when(pl.program_id(0) == 0)
        def _():
            for a in range(nl):
                keep[a].start()
                for r in range(len(FLIPS)):
                    wcopy(a, r, my_dev).start()

        @pl.when(pl.program_id(0) == 0)
        def _():
            for j in range(N_DEV):
                src = j * SHARD_PAD
                for a, b in _shard_pad_cols(j):
                    wt_ref[a:b, :] = wsh_ref[src:src + b - a, :]
                    src += b - a
            a, b = SEG["alr"]
            wt_ref[a + B_GATE_RANK:b, :] = jnp.zeros((RANK_PAD - B_GATE_RANK, D_MODEL), BF16)
            wt_copy.start()

        def one_tile(rows):
            x = x_ref[rows, :]
            r = lax.rsqrt(jnp.mean(x * x, axis=-1, keepdims=True) + NORM_EPS)
            h = (x * r * g_ref[...]).astype(BF16)
            h_ref[rows, :] = h

            def seg(name):
                a, b = SEG[name]
                return _dot_nt(h, wt_ref[a:b, :])

            alr = seg("alr").astype(BF16)
            alr_ref[rows, :] = alr
            u = _dot(alr, wa_ref[...]) + ba_ref[...]
            u_ref[rows, :] = u
            log_a = (jnp.minimum(u, 0.0) - jnp.log(1.0 + jnp.exp(-jnp.abs(u)))) * (1.0 / B_GATE_TEMP)
            row, col = _row_iota((sub, sub)), _lane_iota((sub, sub))
            tri = ((row // B_CHUNK == col // B_CHUNK) & (col <= row)).astype(BF16)
            hi, mid, lo = _split3(log_a)
            cum_ref[rows, :] = _dot(tri, hi) + _dot(tri, mid) + _dot(tri, lo)

            cos, sin = cos_ref[rows, :], sin_ref[rows, :]
            qa = seg("qa") * ATT_SCALE
            for s in range(A_WIDTH // LANES):
                qkv_ref[rows, s * LANES:(s + 1) * LANES] = _rope_slab(qa[:, s * LANES:(s + 1) * LANES], cos, sin).astype(BF16)
            qkv_ref[rows, QKV_K:QKV_V] = _rope_slab(seg("ka"), cos, sin).astype(BF16)
            qkv_ref[rows, QKV_V:QKV_W] = seg("va").astype(BF16)
            za_ref[rows, :] = seg("za").astype(BF16)
            q_ref[rows, :] = seg("qb")
            k_ref[rows, :] = seg("kb")
            vb_ref[rows, :] = seg("vb").astype(BF16)
            zb_ref[rows, :] = seg("zb").astype(BF16)
            ga_ref[rows, :] = seg("ga").astype(BF16)
            gb_ref[rows, :] = seg("gb").astype(BF16)

        for j in range(tm // sub):
            one_tile(pl.ds(j * sub, sub))

        @pl.when(pl.program_id(0) == last)
        def _():
            for a in range(nl):
                for r, (dx, dy, dc) in enumerate(FLIPS):
                    wcopy(a, r, 4 * (px ^ dx) + 2 * (py ^ dy) + (pc ^ dc)).wait_recv()
                    wcopy(a, r, my_dev).wait_send()
                keep[a].wait()
            wt_copy.wait()

    def rows(w):
        return pl.BlockSpec((tm, w), lambda i: (i, 0))

    outs = [("h", D_MODEL, BF16), ("qkv", QKV_W, BF16), ("za", A_WIDTH, BF16), ("q", B_KEY_WIDTH, F32),
            ("k", B_KEY_WIDTH, F32), ("vb", B_WIDTH, BF16), ("zb", B_WIDTH, BF16), ("alr", RANK_PAD, BF16),
            ("u", B_KEY_WIDTH, F32), ("cum", B_KEY_WIDTH, F32), ("ga", D_MODEL, BF16), ("gb", D_MODEL, BF16)]
    res = pl.pallas_call(
        body, name="in_proj", grid=(T // tm,),
        in_specs=[rows(D_MODEL), rows(LANES), rows(LANES), _const_spec((1, D_MODEL)),
                  _const_spec((N_DEV * SHARD_PAD, D_MODEL)), _const_spec((RANK_PAD, B_KEY_WIDTH)),
                  _const_spec((1, B_KEY_WIDTH))] + _any_specs(nl),
        out_specs=[rows(w) for _, w, _ in outs] + _any_specs(1 + nl),
        out_shape=[jax.ShapeDtypeStruct((T, w), dt) for _, w, dt in outs]
                  + [jax.ShapeDtypeStruct((D_IN_PAD, D_MODEL), BF16)]
                  + [jax.ShapeDtypeStruct((N_DEV, *sh.shape), sh.dtype) for sh in later_shards],
        scratch_shapes=[pltpu.VMEM((D_IN_PAD, D_MODEL), BF16),
                        pltpu.SemaphoreType.DMA((nl, len(FLIPS))), pltpu.SemaphoreType.DMA((nl, len(FLIPS))),
                        pltpu.SemaphoreType.DMA((nl,)), pltpu.SemaphoreType.DMA],
        compiler_params=_params("arbitrary"),
    )(x2, cosf, sinf, g_in, wt_sh, wa_pad, b_alpha, *later_shards)
    n_out = len(outs) + 1
    return dict(zip([n for n, _, _ in outs] + ["wt_pad"], res[:n_out])), res[n_out:]


def _dup_kv_head(t, g):
    tf = t.astype(F32)
    keep = (_lane_iota(tf.shape) < A_HEAD_DIM) == (g == 0)
    return jnp.where(keep, tf, pltpu.roll(tf, A_HEAD_DIM, 1)).astype(BF16)


def _stack_heads(t):
    lo = _lane_iota(t.shape) < A_HEAD_DIM
    zero = jnp.zeros_like(t)
    return jnp.concatenate([jnp.where(lo, t, zero), jnp.where(lo, zero, t)], axis=0)


ATT_ROWS = A_HEADS * WINDOW
GROUP_ROWS = ATT_ROWS // A_KV_HEADS
HEADS_PER_GROUP = A_HEADS // A_KV_HEADS


def _band_mask_t(n):
    kj = _row_iota((2 * WINDOW, GROUP_ROWS)) - WINDOW
    qi = _lane_iota((2 * WINDOW, GROUP_ROWS)) % WINDOW
    return (kj <= qi) & (qi - kj < WINDOW) & ((n > 0) | (kj >= 0))


def _stacked_queries(ref, g):
    pairs = range(g * HEADS_PER_GROUP // 2, (g + 1) * HEADS_PER_GROUP // 2)
    return jnp.concatenate([_stack_heads(ref[:, p * LANES:(p + 1) * LANES]) for p in pairs], axis=0)


def _unstack_heads(t, g, ref, dtype):
    lo = _lane_iota((WINDOW, LANES)) < A_HEAD_DIM
    for hh in range(HEADS_PER_GROUP // 2):
        p = g * HEADS_PER_GROUP // 2 + hh
        ref[:, p * LANES:(p + 1) * LANES] = jnp.where(lo, t[2 * hh * WINDOW:(2 * hh + 1) * WINDOW],
                                                       t[(2 * hh + 1) * WINDOW:(2 * hh + 2) * WINDOW]).astype(dtype)


FWD_BLOCKS = 16


def _attn_fwd(qkv, sink_row, B, S):
    T = B * S
    nb = S // WINDOW
    blocks = math.gcd(nb, FWD_BLOCKS)
    steps = nb // blocks

    def one_block(has_prev, sink_ref, q, k, v, o_ref, lse_ref):
        valid = _band_mask_t(has_prev)
        lse_rows = []
        for g in range(A_KV_HEADS):
            kd, vd = _dup_kv_head(k, g), _dup_kv_head(v, g)
            s = jnp.where(valid, _dot_nt(kd, _stacked_queries(q, g)), NEG_BIG)
            sink = sink_ref[:, g * GROUP_ROWS:(g + 1) * GROUP_ROWS]
            m = jnp.maximum(jnp.max(s, axis=0, keepdims=True), sink)
            e = jnp.exp(s - m)
            den = jnp.sum(e, axis=0, keepdims=True) + jnp.exp(sink - m)
            o = _dot_tn((e * (1.0 / den)).astype(BF16), vd)
            _unstack_heads(o, g, o_ref, F32)
            lse = m + jnp.log(den)
            lse_rows += [lse[:, j * WINDOW:(j + 1) * WINDOW] for j in range(HEADS_PER_GROUP)]
        by_head = jnp.concatenate(lse_rows + [jnp.zeros((WINDOW - A_HEADS, WINDOW), F32)], axis=0)
        lse_ref[...] = by_head.T

    def body(sink_ref, q_ref, kc_ref, vc_ref, kp_ref, vp_ref, o_ref, lse_ref):
        k_all = jnp.concatenate([kp_ref[...], kc_ref[...]], axis=0)
        v_all = jnp.concatenate([vp_ref[...], vc_ref[...]], axis=0)
        for j in range(blocks):
            rows = pl.ds(j * WINDOW, WINDOW)
            keys = slice(j * WINDOW, (j + 2) * WINDOW)
            has_prev = pl.program_id(1) if j == 0 else 1
            one_block(has_prev, sink_ref, q_ref[rows, :], k_all[keys], v_all[keys], o_ref.at[rows], lse_ref.at[rows])

    def cur(col, w):
        return pl.BlockSpec((blocks * WINDOW, w), lambda b, n: (b * steps + n, col))

    def prev(col):
        return pl.BlockSpec((WINDOW, LANES), lambda b, n: (b * nb + jnp.maximum(blocks * n - 1, 0), col))

    kcol, vcol = QKV_K // LANES, QKV_V // LANES
    return pl.pallas_call(
        body, name="attn_fwd", grid=(B, steps),
        in_specs=[_const_spec((1, ATT_ROWS)), cur(0, A_WIDTH), cur(kcol, LANES), cur(vcol, LANES), prev(kcol), prev(vcol)],
        out_specs=[cur(0, A_WIDTH), cur(0, LANES)],
        out_shape=[jax.ShapeDtypeStruct((T, A_WIDTH), F32), jax.ShapeDtypeStruct((T, LANES), F32)],
        compiler_params=_params("parallel", "parallel"),
    )(sink_row, qkv, qkv, qkv, qkv, qkv)


ATT_CHUNK = 128


def _chunk_masks(n):
    masks = []
    for half in range(WINDOW // ATT_CHUNK):
        qi = _row_iota((ATT_CHUNK, 2 * WINDOW)) + half * ATT_CHUNK
        kj = _lane_iota((ATT_CHUNK, 2 * WINDOW)) - WINDOW
        masks.append((kj <= qi) & (qi - kj < WINDOW) & ((n > 0) | (kj >= 0)))
    return masks


def _all_stacked_queries(ref):
    return jnp.concatenate([_stacked_queries(ref, g) for g in range(A_KV_HEADS)], axis=0)


def _by_group(fn, lhs, rhs_per_group):
    return jnp.concatenate([fn(lhs[g * GROUP_ROWS:(g + 1) * GROUP_ROWS], rhs_per_group[g])
                            for g in range(A_KV_HEADS)], axis=0)


BWD_BLOCKS = 8


def _attn_bwd(qkv, do, out, lse, sink_col, B, S):
    T = B * S
    nb = S // WINDOW
    M = math.gcd(nb, BWD_BLOCKS)
    steps = nb // M
    n_chunks = ATT_ROWS // ATT_CHUNK
    halves = WINDOW // ATT_CHUNK

    def block(has_prev, sink_ref, q, do_b, out_b, lse_b, k, v, scratch, want_dq):
        s_ref, dp_ref, ds_ref, p_ref = scratch
        width = k.shape[0]
        masks = [mk[:, 0:width] for mk in _chunk_masks(has_prev)]
        kd = [_dup_kv_head(k, g) for g in range(A_KV_HEADS)]
        vd = [_dup_kv_head(v, g) for g in range(A_KV_HEADS)]
        qs, dos = _all_stacked_queries(q), _all_stacked_queries(do_b)
        s_ref[...] = _by_group(_dot_nt, qs, kd)
        dp_ref[...] = _by_group(_dot_nt, dos, vd)
        lane = _lane_iota((ATT_CHUNK, LANES))
        lo = lane < A_HEAD_DIM
        lane1 = _lane_iota((1, LANES))
        dsink_row = jnp.zeros((1, LANES), F32)
        for c in range(n_chunks):
            rows = slice(c * ATT_CHUNK, (c + 1) * ATT_CHUNK)
            head, half = divmod(c, halves)
            qrows = slice(half * ATT_CHUNK, (half + 1) * ATT_CHUNK)
            slab = slice((head // 2) * LANES, (head // 2 + 1) * LANES)
            lse_col = jnp.sum(jnp.where(lane == head, lse_b[qrows, :], 0.0), axis=-1, keepdims=True)
            prod = do_b[qrows, slab].astype(F32) * out_b[qrows, slab].astype(F32)
            mine = lo if head % 2 == 0 else jnp.logical_not(lo)
            delta = jnp.sum(jnp.where(mine, prod, 0.0), axis=-1, keepdims=True)
            prob = jnp.exp(jnp.where(masks[half], s_ref[rows, :], NEG_BIG) - lse_col)
            p_ref[rows, :] = prob.astype(BF16)
            ds_ref[rows, :] = (prob * (dp_ref[rows, :] - delta)).astype(BF16)
            w = -jnp.exp(sink_ref[rows, :] - lse_col) * delta
            dsink_row += jnp.where(lane1 == head, jnp.sum(w, axis=0, keepdims=True), 0.0)
        dq = _by_group(_dot, ds_ref[...], kd) * ATT_SCALE if want_dq else None
        groups = [slice(g * GROUP_ROWS, (g + 1) * GROUP_ROWS) for g in range(A_KV_HEADS)]
        dk = [_dot_tn(ds_ref[rows, :], qs[rows]) for rows in groups]
        dv = [_dot_tn(p_ref[rows, :], dos[rows]) for rows in groups]
        return dq, dk, dv, dsink_row

    def fold(per_group):
        lane = _lane_iota((WINDOW, LANES))
        out = jnp.zeros((WINDOW, LANES), F32)
        for g, acc in enumerate(per_group):
            out = jnp.where((lane < A_HEAD_DIM) == (g == 0), acc + pltpu.roll(acc, A_HEAD_DIM, 1), out)
        return out

    def body(sink_ref, q_ref, qn_ref, do_ref, don_ref, out_ref, outn_ref, lse_ref, lsen_ref, kc_ref, kp_ref, vc_ref, vp_ref,
             dq_ref, dkv_ref, dsink_ref, s_scr, dp_scr, ds_scr, p_scr, s_x, dp_x, ds_x, p_x):
        b, m = pl.program_id(0), pl.program_id(1)

        @pl.when((b == 0) & (m == 0))
        def _():
            dsink_ref[...] = jnp.zeros_like(dsink_ref)

        k_all = jnp.concatenate([kp_ref[...], kc_ref[...]], axis=0)
        v_all = jnp.concatenate([vp_ref[...], vc_ref[...]], axis=0)
        results = []
        for j in range(M):
            rows = slice(j * WINDOW, (j + 1) * WINDOW)
            keys = slice(j * WINDOW, (j + 2) * WINDOW)
            has_prev = m if j == 0 else 1
            results.append(block(has_prev, sink_ref, q_ref[rows, :], do_ref[rows, :], out_ref[rows, :], lse_ref[rows, :],
                                 k_all[keys], v_all[keys], (s_scr.at[j], dp_scr.at[j], ds_scr.at[j], p_scr.at[j]), True))
        last_keys = slice(M * WINDOW, (M + 1) * WINDOW)
        _, dk_x, dv_x, _ = block(1, sink_ref, qn_ref[...], don_ref[...], outn_ref[...], lsen_ref[...],
                                 k_all[last_keys], v_all[last_keys], (s_x, dp_x, ds_x, p_x), False)
        has_next = m < steps - 1
        lo_q = _lane_iota((WINDOW, LANES)) < A_HEAD_DIM
        dsink_row = jnp.zeros((1, LANES), F32)
        for j, (dq, dk, dv, ds_row) in enumerate(results):
            rows = slice(j * WINDOW, (j + 1) * WINDOW)
            for p in range(A_HEADS // 2):
                dq_ref[rows, p * LANES:(p + 1) * LANES] = jnp.where(
                    lo_q, dq[2 * p * WINDOW:(2 * p + 1) * WINDOW], dq[(2 * p + 1) * WINDOW:(2 * p + 2) * WINDOW]).astype(BF16)
            if j + 1 < M:
                dk_next = [t[0:WINDOW] for t in results[j + 1][1]]
                dv_next = [t[0:WINDOW] for t in results[j + 1][2]]
            else:
                dk_next = [jnp.where(has_next, t, 0.0) for t in dk_x]
                dv_next = [jnp.where(has_next, t, 0.0) for t in dv_x]
            dkv_ref[rows, 0:LANES] = fold([own[WINDOW:] + nxt for own, nxt in zip(dk, dk_next)]).astype(BF16)
            dkv_ref[rows, LANES:] = fold([own[WINDOW:] + nxt for own, nxt in zip(dv, dv_next)]).astype(BF16)
            dsink_row += ds_row
        dsink_ref[0:1, :] += dsink_row

    def cur(col, w):
        return pl.BlockSpec((M * WINDOW, w), lambda b, m: (b * steps + m, col))

    def nxt(col, w):
        return pl.BlockSpec((WINDOW, w), lambda b, m: (b * nb + jnp.minimum(M * (m + 1), nb - 1), col))

    def prev(col):
        return pl.BlockSpec((WINDOW, LANES), lambda b, m: (b * nb + jnp.maximum(M * m - 1, 0), col))

    kcol, vcol = QKV_K // LANES, QKV_V // LANES
    scores = (M, ATT_ROWS, 2 * WINDOW)
    extra = (ATT_ROWS, WINDOW)
    return pl.pallas_call(
        body, name="attn_bwd", grid=(B, steps),
        in_specs=[_const_spec((ATT_ROWS, 1)), cur(0, A_WIDTH), nxt(0, A_WIDTH), cur(0, A_WIDTH), nxt(0, A_WIDTH),
                  cur(0, A_WIDTH), nxt(0, A_WIDTH), cur(0, LANES), nxt(0, LANES),
                  cur(kcol, LANES), prev(kcol), cur(vcol, LANES), prev(vcol)],
        out_specs=[cur(0, A_WIDTH), cur(0, 2 * LANES), pl.BlockSpec((8, LANES), lambda b, m: (0, 0))],
        out_shape=[jax.ShapeDtypeStruct((T, A_WIDTH), BF16), jax.ShapeDtypeStruct((T, 2 * LANES), BF16),
                   jax.ShapeDtypeStruct((8, LANES), F32)],
        scratch_shapes=[pltpu.VMEM(scores, F32), pltpu.VMEM(scores, F32), pltpu.VMEM(scores, BF16), pltpu.VMEM(scores, BF16),
                        pltpu.VMEM(extra, F32), pltpu.VMEM(extra, F32), pltpu.VMEM(extra, BF16), pltpu.VMEM(extra, BF16)],
        compiler_params=_params("arbitrary", "arbitrary"),
    )(sink_col, qkv, qkv, do, do, out, out, lse, lse, qkv, qkv, qkv, qkv)


GLA_FWD_TILING = (64, 16)
GLA_BWD_TILING = (256, 4)


def _gla_factors(q_ref, k_ref, cum_ref):
    cpt = q_ref.shape[0] // B_CHUNK
    scale = B_KEY_DIM ** -0.5
    cum = cum_ref[...]
    shape = (B_CHUNK, B_KEY_WIDTH)
    last = jnp.concatenate([jnp.broadcast_to(cum_ref[pl.ds(c * B_CHUNK + B_CHUNK - 1, 1), :], shape)
                            for c in range(cpt)], axis=0)
    mid = jnp.concatenate([jnp.broadcast_to(cum_ref[pl.ds(c * B_CHUNK + B_CHUNK // 2 - 1, 1), :], shape)
                           for c in range(cpt)], axis=0)
    e_qm, e_km, e_qe, e_kd = jnp.exp(cum - mid), jnp.exp(mid - cum), jnp.exp(cum), jnp.exp(last - cum)
    qs = q_ref[...] * scale
    k = k_ref[...]
    return qs, k, (e_qm, e_km, e_qe, e_kd)


def _head_mask(shape, h):
    return (_lane_iota(shape) // B_KEY_DIM) == h


def _stack_masked(t):
    return jnp.concatenate([jnp.where(_head_mask(t.shape, h), t, 0.0) for h in range(B_HEADS)], axis=0).astype(BF16)


def _select_heads(t):
    shape = (B_CHUNK, B_KEY_WIDTH)
    out = jnp.zeros(shape, F32)
    for h in range(B_HEADS):
        out = jnp.where(_head_mask(shape, h), t[h * B_CHUNK:(h + 1) * B_CHUNK], out)
    return out


def _select_state(t):
    shape = (B_VAL_DIM, B_KEY_WIDTH)
    out = jnp.zeros(shape, F32)
    for h in range(B_HEADS):
        out = jnp.where(_head_mask(shape, h), t[h * B_VAL_DIM:(h + 1) * B_VAL_DIM], out)
    return out


def _rows_by_head(t):
    return jnp.concatenate([t[:, h * B_VAL_DIM:(h + 1) * B_VAL_DIM] for h in range(B_HEADS)], axis=0)


def _intra_mask(tile_rows):
    i, j = _row_iota((tile_rows, tile_rows)), _lane_iota((tile_rows, tile_rows))
    return (i // B_CHUNK == j // B_CHUNK) & (j <= i)


def _pair_stack(t, p):
    slab = t[:, p * LANES:(p + 1) * LANES]
    lo = _lane_iota(slab.shape) < B_KEY_DIM
    return jnp.concatenate([jnp.where(lo, slab, 0.0), jnp.where(lo, 0.0, slab)], axis=0).astype(BF16)


def _gla_fwd(q, k, cum, vb, B, S):
    T = B * S
    tile_rows = math.gcd(S, GLA_FWD_TILING[0])
    cpt = tile_rows // B_CHUNK
    nt = S // tile_rows
    tps = math.gcd(nt, GLA_FWD_TILING[1])

    def one_sequence(q_ref, k_ref, cum_ref, v_ref, o_ref, st_all_ref, st_ref):
        qs, kk, (e_qm, e_km, e_qe, e_kd) = _gla_factors(q_ref, k_ref, cum_ref)
        qm, km, qe, kd = qs * e_qm, kk * e_km, qs * e_qe, (kk * e_kd).astype(BF16)
        mask = _intra_mask(tile_rows)
        intra = []
        for p in range(B_HEADS // 2):
            a = _dot_nt(_pair_stack(qm, p), km[:, p * LANES:(p + 1) * LANES].astype(BF16))
            for hh in range(2):
                h = 2 * p + hh
                att = jnp.where(mask, a[hh * tile_rows:(hh + 1) * tile_rows], 0.0).astype(BF16)
                intra.append(_dot(att, v_ref[:, h * B_VAL_DIM:(h + 1) * B_VAL_DIM]))
        inter = []
        for c in range(cpt):
            rows = slice(c * B_CHUNK, (c + 1) * B_CHUNK)
            st = st_ref[...]
            st_all_ref[c] = st
            inter.append(_dot_nt(_stack_masked(qe[rows]), st.astype(BF16)))
            inc = _select_state(_dot_tn(v_ref[rows, :], kd[rows]))
            decay = jnp.exp(cum_ref[pl.ds(c * B_CHUNK + B_CHUNK - 1, 1), :])
            st_ref[...] = st * decay + inc
        for h in range(B_HEADS):
            oi = jnp.concatenate([inter[c][h * B_CHUNK:(h + 1) * B_CHUNK] for c in range(cpt)], axis=0)
            o_ref[:, h * B_VAL_DIM:(h + 1) * B_VAL_DIM] = (intra[h] + oi).astype(BF16)

    def body(q_ref, k_ref, cum_ref, v_ref, o_ref, st_all_ref, st_ref):
        @pl.when(pl.program_id(0) == 0)
        def _():
            st_ref[...] = jnp.zeros_like(st_ref)

        for b in range(B):
            for tile in range(tps):
                tok = pl.ds(tile * tile_rows, tile_rows)
                chunks = pl.ds(tile * cpt, cpt)
                one_sequence(*[r.at[b, tok] for r in (q_ref, k_ref, cum_ref, v_ref, o_ref)],
                             st_all_ref.at[b, chunks], st_ref.at[b])

    def rows(w):
        return pl.BlockSpec((B, tps * tile_rows, w), lambda t: (0, t, 0))

    seq = lambda a: a.reshape(B, S, a.shape[-1])
    o, st_all = pl.pallas_call(
        body, name="gla_fwd", grid=(nt // tps,),
        in_specs=[rows(B_KEY_WIDTH), rows(B_KEY_WIDTH), rows(B_KEY_WIDTH), rows(B_WIDTH)],
        out_specs=[rows(B_WIDTH),
                   pl.BlockSpec((B, tps * cpt, B_VAL_DIM, B_KEY_WIDTH), lambda t: (0, t, 0, 0))],
        out_shape=[jax.ShapeDtypeStruct((B, S, B_WIDTH), BF16),
                   jax.ShapeDtypeStruct((B, S // B_CHUNK, B_VAL_DIM, B_KEY_WIDTH), F32)],
        scratch_shapes=[pltpu.VMEM((B, B_VAL_DIM, B_KEY_WIDTH), F32)],
        compiler_params=_params("arbitrary"),
    )(seq(q), seq(k), seq(cum), seq(vb))
    return o.reshape(T, B_WIDTH), st_all.reshape(T // B_CHUNK, B_VAL_DIM, B_KEY_WIDTH)


def _gla_bwd(q, k, cum, vb, do, st_all, B, S, wgrads):
    T = B * S
    tile_rows = math.gcd(S, GLA_BWD_TILING[0])
    cpt = tile_rows // B_CHUNK
    nt = S // tile_rows
    tps = math.gcd(nt, GLA_BWD_TILING[1])
    steps = nt // tps
    scale = B_KEY_DIM ** -0.5
    nw = len(wgrads)

    def one_sequence(q_ref, k_ref, cum_ref, v_ref, do_ref, st_all_ref, dq_ref, dk_ref, dv_ref, dla_ref, dst_ref):
        qs, kk, (e_qm, e_km, e_qe, e_kd) = _gla_factors(q_ref, k_ref, cum_ref)
        qm, km, qe, kd = qs * e_qm, kk * e_km, qs * e_qe, kk * e_kd
        mask = _intra_mask(tile_rows)
        dqm_slabs, dkm_slabs, dv_intra = [], [], []
        for p in range(B_HEADS // 2):
            qm_st = _pair_stack(qm, p)
            km_p = km[:, p * LANES:(p + 1) * LANES].astype(BF16)
            a = _dot_nt(qm_st, km_p)
            da_blocks, dqm_h = [], []
            for hh in range(2):
                h = 2 * p + hh
                vs = slice(h * B_VAL_DIM, (h + 1) * B_VAL_DIM)
                att = jnp.where(mask, a[hh * tile_rows:(hh + 1) * tile_rows], 0.0).astype(BF16)
                dv_intra.append(_dot_tn(att, do_ref[:, vs]))
                da = jnp.where(mask, _dot_nt(do_ref[:, vs], v_ref[:, vs]), 0.0).astype(BF16)
                da_blocks.append(da)
                dqm_h.append(_dot(da, km_p))
            lo = _lane_iota((tile_rows, LANES)) < B_KEY_DIM
            dqm_slabs.append(jnp.where(lo, dqm_h[0], dqm_h[1]))
            dkm_slabs.append(_dot_tn(jnp.concatenate(da_blocks, axis=0), qm_st))
        dqm = jnp.concatenate(dqm_slabs, axis=1)
        dkm = jnp.concatenate(dkm_slabs, axis=1)

        dqe_c, dkd_c, dv_inter, tail_c = ([None] * cpt for _ in range(4))
        for c in reversed(range(cpt)):
            rows = slice(c * B_CHUNK, (c + 1) * B_CHUNK)
            dst = dst_ref[...]
            dst_b = dst.astype(BF16)
            dv_inter[c] = _dot_nt(_stack_masked(kd[rows]), dst_b)
            dkd_c[c] = _select_heads(_dot(_rows_by_head(v_ref[rows, :]), dst_b))
            do_c = do_ref[rows, :]
            dqe_c[c] = _select_heads(_dot(_rows_by_head(do_c), st_all_ref[c].astype(BF16)))
            contrib = _select_state(_dot_tn(do_c, qe[rows].astype(BF16)))
            decay = jnp.exp(cum_ref[pl.ds(c * B_CHUNK + B_CHUNK - 1, 1), :])
            tail = (jnp.sum(kk[rows] * dkd_c[c] * e_kd[rows], axis=0, keepdims=True)
                    + decay * jnp.sum(st_all_ref[c] * dst, axis=0, keepdims=True))
            tail_c[c] = jnp.broadcast_to(tail, (B_CHUNK, B_KEY_WIDTH))
            dst_ref[...] = dst * decay + contrib
        dqe = jnp.concatenate(dqe_c, axis=0)
        dkd = jnp.concatenate(dkd_c, axis=0)
        dqs = dqm * e_qm + dqe * e_qe
        dk = dkm * e_km + dkd * e_kd
        dq_ref[...] = (dqs * scale).astype(BF16)
        dk_ref[...] = dk.astype(BF16)
        for h in range(B_HEADS):
            dvi = jnp.concatenate([dv_inter[c][h * B_CHUNK:(h + 1) * B_CHUNK] for c in range(cpt)], axis=0)
            dv_ref[:, h * B_VAL_DIM:(h + 1) * B_VAL_DIM] = (dv_intra[h] + dvi).astype(BF16)
        dd = qs * dqs - kk * dk
        i, j = _row_iota((tile_rows, tile_rows)), _lane_iota((tile_rows, tile_rows))
        upper = ((i // B_CHUNK == j // B_CHUNK) & (j >= i)).astype(BF16)
        hi, mid, lo3 = _split3(dd)
        dla_ref[...] = _dot(upper, hi) + _dot(upper, mid) + _dot(upper, lo3) + jnp.concatenate(tail_c, axis=0)

    def body(q_ref, k_ref, cum_ref, v_ref, do_ref, st_all_ref, *rest):
        g_refs, (dq_ref, dk_ref, dv_ref, dla_ref) = rest[:nw], rest[nw:nw + 4]
        rv_refs, (dst_ref, send_sems, recv_sems) = rest[nw + 4:2 * nw + 4], rest[2 * nw + 4:]
        x, y, c = _my_place()

        def wcopy(a, r):
            dx, dy, dc = FLIPS[r]
            return pltpu.make_async_remote_copy(
                src_ref=g_refs[a].at[4 * (x ^ dx) + 2 * (y ^ dy) + (c ^ dc)], dst_ref=rv_refs[a].at[r],
                send_sem=send_sems.at[a, r], recv_sem=recv_sems.at[a, r],
                device_id=(x ^ dx, y ^ dy, c ^ dc), device_id_type=MESH)

        @pl.when(pl.program_id(0) == 0)
        def _():
            dst_ref[...] = jnp.zeros_like(dst_ref)
            for a in range(nw):
                for r in range(len(FLIPS)):
                    wcopy(a, r).start()

        for b in range(B):
            for tile in reversed(range(tps)):
                tok = pl.ds(tile * tile_rows, tile_rows)
                chunks = pl.ds(tile * cpt, cpt)
                one_sequence(*[r.at[b, tok] for r in (q_ref, k_ref, cum_ref, v_ref, do_ref)], st_all_ref.at[b, chunks],
                             *[r.at[b, tok] for r in (dq_ref, dk_ref, dv_ref, dla_ref)], dst_ref.at[b])

        @pl.when(pl.program_id(0) == steps - 1)
        def _():
            for a in range(nw):
                for r in range(len(FLIPS)):
                    wcopy(a, r).wait()

    def rows(w):
        return pl.BlockSpec((B, tps * tile_rows, w), lambda t: (0, steps - 1 - t, 0))

    seq = lambda a: a.reshape(B, S, a.shape[-1])
    res = pl.pallas_call(
        body, name="gla_bwd", grid=(steps,),
        in_specs=[rows(B_KEY_WIDTH), rows(B_KEY_WIDTH), rows(B_KEY_WIDTH), rows(B_WIDTH), rows(B_WIDTH),
                  pl.BlockSpec((B, tps * cpt, B_VAL_DIM, B_KEY_WIDTH), lambda t: (0, steps - 1 - t, 0, 0))]
                 + _any_specs(nw),
        out_specs=[rows(B_KEY_WIDTH), rows(B_KEY_WIDTH), rows(B_WIDTH), rows(B_KEY_WIDTH)] + _any_specs(nw),
        out_shape=[jax.ShapeDtypeStruct((B, S, B_KEY_WIDTH), BF16), jax.ShapeDtypeStruct((B, S, B_KEY_WIDTH), BF16),
                   jax.ShapeDtypeStruct((B, S, B_WIDTH), BF16), jax.ShapeDtypeStruct((B, S, B_KEY_WIDTH), F32)]
                  + [jax.ShapeDtypeStruct((len(FLIPS), *g.shape[1:]), g.dtype) for g in wgrads],
        scratch_shapes=[pltpu.VMEM((B, B_VAL_DIM, B_KEY_WIDTH), F32),
                        pltpu.SemaphoreType.DMA((nw, len(FLIPS))), pltpu.SemaphoreType.DMA((nw, len(FLIPS)))],
        compiler_params=_params("arbitrary"),
    )(seq(q), seq(k), seq(cum), seq(vb), seq(do), st_all.reshape(B, S // B_CHUNK, B_VAL_DIM, B_KEY_WIDTH), *wgrads)
    return [a.reshape(T, a.shape[-1]) for a in res[:4]], res[4:]


def _merge(x2, tgt2, attn, za, o_gla, zb, ga, gb, w_oa_sh, w_ob_sh, w_o, g_gla, g_final):
    T = x2.shape[0]
    tm = math.gcd(T, 512)
    sub = math.gcd(tm, 256)
    last = T // tm - 1

    def body(x_ref, tgt_ref, attn_ref, za_ref, og_ref, zb_ref, ga_ref, gb_ref,
             woa_sh_ref, wob_sh_ref, wo_ref, gg_ref, gf_ref,
             dxres_ref, dattn_ref, dog_ref, dza_ref, dzb_ref, dga_ref, dgb_ref,
             dwo_out, dwoa_out, dwob_out, small_ref,
             awo_ref, awoa_ref, awob_ref, agf_ref, agg_ref, loss_ref, woa_ref, wob_ref,
             dwo_ref, dwoa_ref, dwob_ref, w_sems, dw_sems):
        w_copies = [pltpu.make_async_copy(sh.at[j], dst.at[:, j * SHARD_OUT:(j + 1) * SHARD_OUT], w_sems.at[a, j])
                    for a, (sh, dst) in enumerate(((woa_sh_ref, woa_ref), (wob_sh_ref, wob_ref))) for j in range(N_DEV)]
        dw_copies = [pltpu.make_async_copy(src, dst, dw_sems.at[a])
                     for a, (src, dst) in enumerate(((dwo_ref, dwo_out), (dwoa_ref, dwoa_out), (dwob_ref, dwob_out)))]

        @pl.when(pl.program_id(0) == 0)
        def _():
            for cp in w_copies:
                cp.start()
            for r in (awo_ref, awoa_ref, awob_ref, agf_ref, agg_ref, loss_ref):
                r[...] = jnp.zeros_like(r)
            for cp in w_copies:
                cp.wait()

        def one_tile(rows):
            za_v = za_ref[rows, :].astype(F32)
            sig_za = _sigmoid_tanh(za_v)
            silu_a = za_v * sig_za
            attn_v = attn_ref[rows, :].astype(F32)
            oa = (attn_v * silu_a).astype(BF16)
            ya = _dot(oa, woa_ref[...])
            og = og_ref[rows, :].astype(F32)
            zb_v = zb_ref[rows, :].astype(F32)
            sig_zb = _sigmoid_tanh(zb_v)
            silu_b = zb_v * sig_zb
            gg = gg_ref[...]
            on_parts, rinv_parts = [], []
            for h in range(B_HEADS):
                seg = og[:, h * B_VAL_DIM:(h + 1) * B_VAL_DIM]
                rinv = lax.rsqrt(jnp.mean(seg * seg, axis=-1, keepdims=True) + NORM_EPS)
                rinv_parts.append(rinv)
                on_parts.append(seg * rinv)
            on = jnp.concatenate(on_parts, axis=1)
            obn = on * gg
            ob = (obn * silu_b).astype(BF16)
            yb = _dot(ob, wob_ref[...])
            sig_a = _sigmoid_tanh(ga_ref[rows, :].astype(F32))
            sig_b = _sigmoid_tanh(gb_ref[rows, :].astype(F32))
            merged = (sig_a * ya + sig_b * yb).astype(BF16)
            out = x_ref[rows, :] + _dot(merged, wo_ref[...])
            rf = lax.rsqrt(jnp.mean(out * out, axis=-1, keepdims=True) + NORM_EPS)
            nrm = out * rf
            gf = gf_ref[...]
            err = nrm * gf - tgt_ref[rows, :]
            loss = jnp.sum(err * err) * (0.5 / D_MODEL)

            dy = err * (1.0 / D_MODEL)
            dgf = jnp.sum(dy * nrm, axis=0, keepdims=True)
            dn = dy * gf
            dout = rf * (dn - nrm * jnp.mean(dn * nrm, axis=-1, keepdims=True))
            dxres_ref[rows, :] = dout
            dout_b = dout.astype(BF16)
            dmerged = _dot_nt(dout_b, wo_ref[...])
            dya = dmerged * sig_a
            dyb = dmerged * sig_b
            dga_ref[rows, :] = (dmerged * ya * sig_a * (1.0 - sig_a)).astype(BF16)
            dgb_ref[rows, :] = (dmerged * yb * sig_b * (1.0 - sig_b)).astype(BF16)
            dya_b, dyb_b = dya.astype(BF16), dyb.astype(BF16)
            doa = _dot_nt(dya_b, woa_ref[...])
            dattn_ref[rows, :] = (doa * silu_a).astype(BF16)
            dza_ref[rows, :] = (doa * attn_v * (sig_za * (1.0 + za_v * (1.0 - sig_za)))).astype(BF16)
            dob = _dot_nt(dyb_b, wob_ref[...])
            dzb_ref[rows, :] = (dob * obn * (sig_zb * (1.0 + zb_v * (1.0 - sig_zb)))).astype(BF16)
            dobn = dob * silu_b
            dgg = jnp.sum(dobn * on, axis=0, keepdims=True)
            don = dobn * gg
            for h in range(B_HEADS):
                sl = slice(h * B_VAL_DIM, (h + 1) * B_VAL_DIM)
                don_h, on_h = don[:, sl], on[:, sl]
                dog_ref[rows, sl] = (rinv_parts[h] * (don_h - on_h * jnp.mean(don_h * on_h, axis=-1, keepdims=True))
                                     ).astype(BF16)
            return (merged, dout_b, oa, dya_b, ob, dyb_b), (loss, dgf, dgg)

        tiles = [one_tile(pl.ds(j * sub, sub)) for j in range(tm // sub)]
        merged, dout_b, oa, dya_b, ob, dyb_b = (jnp.concatenate(parts, axis=0) for parts in zip(*[t[0] for t in tiles]))
        awo_ref[...] += _dot_tn(merged, dout_b)
        awoa_ref[...] += _dot_tn(oa, dya_b)
        awob_ref[...] += _dot_tn(ob, dyb_b)
        for _, (loss, dgf, dgg) in tiles:
            loss_ref[...] += loss
            agf_ref[...] += dgf
            agg_ref[...] += dgg

        @pl.when(pl.program_id(0) == last)
        def _():
            for j in range(N_DEV):
                dwo_ref[j] = awo_ref[j * SHARD_OUT:(j + 1) * SHARD_OUT, :].astype(BF16)
                dwoa_ref[j] = awoa_ref[:, j * SHARD_OUT:(j + 1) * SHARD_OUT].astype(BF16)
                dwob_ref[j] = awob_ref[:, j * SHARD_OUT:(j + 1) * SHARD_OUT].astype(BF16)
            small_ref[...] = jnp.zeros_like(small_ref)
            _put_rows(small_ref, SMALL_G_FINAL, agf_ref[...])
            _put_rows(small_ref, SMALL_G_GLA, agg_ref[...])
            small_ref[SMALL_LOSS:SMALL_LOSS + 1, :] = loss_ref[...]
            for cp in dw_copies:
                cp.start()
            for cp in dw_copies:
                cp.wait()

    def rows(w):
        return pl.BlockSpec((tm, w), lambda i: (i, 0))

    def whole(shape):
        nd = len(shape)
        return pl.BlockSpec(shape, lambda i: (0,) * nd)

    outs = [((T, D_MODEL), F32, rows(D_MODEL)), ((T, A_WIDTH), BF16, rows(A_WIDTH)), ((T, B_WIDTH), BF16, rows(B_WIDTH)),
            ((T, A_WIDTH), BF16, rows(A_WIDTH)), ((T, B_WIDTH), BF16, rows(B_WIDTH)),
            ((T, D_MODEL), BF16, rows(D_MODEL)), ((T, D_MODEL), BF16, rows(D_MODEL)),
            ((N_DEV, SHARD_OUT, D_MODEL), BF16, pl.BlockSpec(memory_space=pl.ANY)),
            ((N_DEV, A_WIDTH, SHARD_OUT), BF16, pl.BlockSpec(memory_space=pl.ANY)),
            ((N_DEV, B_WIDTH, SHARD_OUT), BF16, pl.BlockSpec(memory_space=pl.ANY)),
            ((SMALL_SINKS, LANES), F32, whole((SMALL_SINKS, LANES)))]
    return pl.pallas_call(
        body, name="merge", grid=(T // tm,),
        in_specs=[rows(D_MODEL), rows(D_MODEL), rows(A_WIDTH), rows(A_WIDTH), rows(B_WIDTH), rows(B_WIDTH),
                  rows(D_MODEL), rows(D_MODEL),
                  pl.BlockSpec(memory_space=pl.ANY), pl.BlockSpec(memory_space=pl.ANY),
                  _const_spec((D_MODEL, D_MODEL)), _const_spec((1, B_WIDTH)), _const_spec((1, D_MODEL))],
        out_specs=[o[2] for o in outs],
        out_shape=[jax.ShapeDtypeStruct(o[0], o[1]) for o in outs],
        scratch_shapes=[pltpu.VMEM((D_MODEL, D_MODEL), F32), pltpu.VMEM((A_WIDTH, D_MODEL), F32),
                        pltpu.VMEM((B_WIDTH, D_MODEL), F32), pltpu.VMEM((1, D_MODEL), F32), pltpu.VMEM((1, B_WIDTH), F32),
                        pltpu.VMEM((1, LANES), F32), pltpu.VMEM((A_WIDTH, D_MODEL), BF16),
                        pltpu.VMEM((B_WIDTH, D_MODEL), BF16),
                        pltpu.VMEM((N_DEV, SHARD_OUT, D_MODEL), BF16), pltpu.VMEM((N_DEV, A_WIDTH, SHARD_OUT), BF16),
                        pltpu.VMEM((N_DEV, B_WIDTH, SHARD_OUT), BF16),
                        pltpu.SemaphoreType.DMA((2, N_DEV)), pltpu.SemaphoreType.DMA((3,))],
        compiler_params=pltpu.CompilerParams(dimension_semantics=("arbitrary",), vmem_limit_bytes=V7X_VMEM_LIMIT_MAX),
    )(x2, tgt2, attn, za, o_gla, zb, ga, gb, w_oa_sh, w_ob_sh, w_o, g_gla, g_final)


def _in_proj_bwd(x2, dxres, cosf, sinf, g_in, wt_pad, wa_pad, parts):
    T = x2.shape[0]
    tm = math.gcd(T, 512)
    sub = math.gcd(tm, 256)
    last = T // tm - 1
    base = SMALL_G_IN

    def body(x_ref, dxres_ref, cos_ref, sin_ref, g_ref, wt_ref, wa_ref,
             dq_ref, dkv_ref, dza_ref, dqb_ref, dkb_ref, dvb_ref, dzb_ref, dla_ref, u_ref, alr_ref, dga_ref, dgb_ref,
             dx_ref, dsh_ref, small_ref, dproj_ref, agin_ref, aba_ref, awa_ref):
        @pl.when(pl.program_id(0) == 0)
        def _():
            for r in (agin_ref, aba_ref, awa_ref):
                r[...] = jnp.zeros_like(r)

        def one_tile(rows):
            cos, nsin = cos_ref[rows, :], -sin_ref[rows, :]
            for s in range(A_WIDTH // LANES):
                sl = slice(s * LANES, (s + 1) * LANES)
                dproj_ref[rows, sl] = _rope_slab(dq_ref[rows, sl].astype(F32), cos, nsin).astype(BF16)
            dproj_ref[rows, QKV_K:QKV_V] = _rope_slab(dkv_ref[rows, 0:LANES].astype(F32), cos, nsin).astype(BF16)
            dproj_ref[rows, QKV_V:QKV_W] = dkv_ref[rows, LANES:]

            def put(name, val):
                a, b = SEG[name]
                dproj_ref[rows, a:b] = val

            put("za", dza_ref[rows, :])
            put("qb", dqb_ref[rows, :])
            put("kb", dkb_ref[rows, :])
            put("vb", dvb_ref[rows, :])
            put("zb", dzb_ref[rows, :])
            put("ga", dga_ref[rows, :])
            put("gb", dgb_ref[rows, :])
            du = dla_ref[rows, :] * (1.0 / B_GATE_TEMP) * _sigmoid(-u_ref[rows, :])
            du_b = du.astype(BF16)
            put("alr", _dot_nt(du_b, wa_ref[...]).astype(BF16))

            for j in range(N_DEV):
                col = (j % 2) * SHARD_PAD
                for a, b in _shard_pad_cols(j):
                    dsh_ref[j // 2, rows, col:col + b - a] = dproj_ref[rows, a:b]
                    col += b - a
                dsh_ref[j // 2, rows, col:(j % 2 + 1) * SHARD_PAD] = jnp.zeros((sub, SHARD_PAD - SHARD_IN), BF16)

            dh = _dot(dproj_ref[rows, :], wt_ref[...])
            x = x_ref[rows, :]
            r = lax.rsqrt(jnp.mean(x * x, axis=-1, keepdims=True) + NORM_EPS)
            nrm = x * r
            dn = dh * g_ref[...]
            dx_ref[rows, :] = dxres_ref[rows, :] + r * (dn - nrm * jnp.mean(dn * nrm, axis=-1, keepdims=True))
            return jnp.sum(dh * nrm, axis=0, keepdims=True), jnp.sum(du, axis=0, keepdims=True), alr_ref[rows, :], du_b

        for j in range(tm // sub):
            dgin, dba, alr, du_b = one_tile(pl.ds(j * sub, sub))
            agin_ref[...] += dgin
            aba_ref[...] += dba
            awa_ref[...] += _dot_tn(alr, du_b)

        @pl.when(pl.program_id(0) == last)
        def _():
            small_ref[...] = jnp.zeros_like(small_ref)
            _put_rows(small_ref, SMALL_G_IN - base, agin_ref[...])
            _put_rows(small_ref, SMALL_B_ALPHA - base, aba_ref[...])
            for half in range(B_KEY_WIDTH // LANES):
                r0 = SMALL_W_ALPHA - base + half * B_GATE_RANK
                small_ref[r0:r0 + B_GATE_RANK, :] = awa_ref[0:B_GATE_RANK, half * LANES:(half + 1) * LANES]

    def rows(w):
        return pl.BlockSpec((tm, w), lambda i: (i, 0))

    names = ["dq", "dkv", "dza", "dqb", "dkb", "dvb", "dzb", "dla", "u", "alr", "dga", "dgb"]
    return pl.pallas_call(
        body, name="in_proj_bwd", grid=(T // tm,),
        in_specs=[rows(D_MODEL), rows(D_MODEL), rows(LANES), rows(LANES), _const_spec((1, D_MODEL)),
                  _const_spec((D_IN_PAD, D_MODEL)), _const_spec((RANK_PAD, B_KEY_WIDTH))]
                 + [rows(parts[n].shape[1]) for n in names],
        out_specs=[rows(D_MODEL), pl.BlockSpec((N_CHIPS, tm, 2 * SHARD_PAD), lambda i: (0, i, 0)),
                   pl.BlockSpec((SMALL_ROWS - base, LANES), lambda i: (0, 0))],
        out_shape=[jax.ShapeDtypeStruct((T, D_MODEL), F32), jax.ShapeDtypeStruct((N_CHIPS, T, 2 * SHARD_PAD), BF16),
                   jax.ShapeDtypeStruct((SMALL_ROWS - base, LANES), F32)],
        scratch_shapes=[pltpu.VMEM((tm, D_IN_PAD), BF16), pltpu.VMEM((1, D_MODEL), F32), pltpu.VMEM((1, B_KEY_WIDTH), F32),
                        pltpu.VMEM((RANK_PAD, B_KEY_WIDTH), F32)],
        compiler_params=pltpu.CompilerParams(dimension_semantics=("arbitrary",), vmem_limit_bytes=V7X_VMEM_LIMIT_MAX),
    )(x2, dxres, cosf, sinf, g_in, wt_pad, wa_pad, *[parts[n] for n in names])


FLIPS = [(dx, dy, dc) for dx in (0, 1) for dy in (0, 1) for dc in (0, 1)][1:]
GATHER_PIECES = 2
BF16_TILE_ROWS = 16


def _my_place():
    return lax.axis_index("x"), lax.axis_index("y"), lax.axis_index("c")


def _any_specs(n):
    return [pl.BlockSpec(memory_space=pl.ANY)] * n


def _gather_first(shards, pos_col):
    n = len(shards)
    T = pos_col.shape[0]
    rows_per_pass = math.gcd(T, 512)
    pieces = [GATHER_PIECES if sh.shape[0] % (GATHER_PIECES * BF16_TILE_ROWS) == 0 else 1 for sh in shards]
    max_pieces = max(pieces)
    invf, sign = _rope_lane_constants()

    def body(*refs):
        ins, (pos_ref, invf_ref, sign_ref) = refs[:n], refs[n:n + 3]
        outs, (cos_ref, sin_ref) = refs[n + 3:2 * n + 3], refs[2 * n + 3:2 * n + 5]
        send_sems, recv_sems, local_sems = refs[2 * n + 5:]
        x, y, c = _my_place()
        me, sibling = (x, y, c), (x, y, 1 - c)
        onward, source, diagonal = (x ^ (1 - c), y ^ c), (x ^ c, y ^ (1 - c)), (1 - x, 1 - y)

        def block(a, px, py, pc):
            return outs[a].at[4 * px + 2 * py + pc]

        def copy(a, p, k, blk, to, own=False):
            rows = shards[a].shape[0] // pieces[a]
            piece = pl.ds(p * rows, rows)
            return pltpu.make_async_remote_copy(
                src_ref=(ins[a] if own else block(a, *blk)).at[piece], dst_ref=block(a, *blk).at[piece],
                send_sem=send_sems.at[a, p, k], recv_sem=recv_sems.at[a, p, k], device_id=to, device_id_type=MESH)

        every = [(a, p) for a in range(n) for p in range(pieces[a])]
        mine = [pltpu.make_async_copy(ins[a], block(a, *me), local_sems.at[a]) for a in range(n)]
        for cp in mine:
            cp.start()
        first = []
        for a, p in every:
            first.append(copy(a, p, 0, me, sibling, own=True))
            first += [copy(a, p, 1 + j, me, (*chip, c), own=True) for j, chip in enumerate((onward, source))]
        for cp in first:
            cp.start()

        def tables(i, carry):
            rows = pl.ds(pl.multiple_of(i * rows_per_pass, rows_per_pass), rows_per_pass)
            ang = pos_ref[rows, :].astype(F32) * invf_ref[...]
            cos_ref[rows, :] = jnp.cos(ang)
            sin_ref[rows, :] = jnp.sin(ang) * sign_ref[...]
            return carry

        lax.fori_loop(0, T // rows_per_pass, tables, 0)

        passed = []
        for j, chip in ((1, source), (0, onward), (2, diagonal)):
            for a, p in every:
                copy(a, p, 1 + j, (*chip, c), me).wait_recv()
                todo = [copy(a, p, 4 + j, (*chip, c), sibling)]
                if j == 1:
                    todo.append(copy(a, p, 1 + 2, (*chip, c), (*onward, c)))
                for cp in todo:
                    cp.start()
                passed += todo
        for a, p in every:
            copy(a, p, 0, sibling, me).wait_recv()
            for j, chip in enumerate((source, onward, diagonal)):
                copy(a, p, 4 + j, (*chip, 1 - c), me).wait_recv()
        for cp in first + passed:
            cp.wait_send()
        for cp in mine:
            cp.wait()

    vmem = pl.BlockSpec(memory_space=pltpu.VMEM)
    res = pl.pallas_call(
        body, name="gather_weights",
        in_specs=_any_specs(n) + [vmem] * 3, out_specs=_any_specs(n) + [vmem] * 2,
        out_shape=[jax.ShapeDtypeStruct((N_DEV, *s.shape), s.dtype) for s in shards]
                  + [jax.ShapeDtypeStruct((T, LANES), F32)] * 2,
        scratch_shapes=[pltpu.SemaphoreType.DMA((n, max_pieces, 7)), pltpu.SemaphoreType.DMA((n, max_pieces, 7)),
                        pltpu.SemaphoreType.DMA((n,))],
        compiler_params=pltpu.CompilerParams(vmem_limit_bytes=V7X_VMEM_LIMIT),
    )(*shards, pos_col, invf, sign)
    return res[:n], res[n], res[n + 1]


def _w_in_grad_rs(h, dsh, chip_order, small):
    T = h.shape[0]
    tk = math.gcd(T, 2048)
    nk = T // tk
    chip_flips = [(1, 1), (1, 0), (0, 1)]
    n_steps = len(chip_flips) + 1
    SIB = len(chip_flips)
    halves = (slice(0, SHARD_PAD), slice(SHARD_PAD, 2 * SHARD_PAD))

    def body(order_ref, h_ref, d_ref, s_ref, own_ref, recv_ref, sall_ref,
             acc_ref, pre_ref, to_sib_ref, to_chip_ref,
             sib_send, sib_recv, chip_send, chip_recv, ssend_sems, srecv_sems, local_sem):
        i, kk = pl.program_id(0), pl.program_id(1)
        x, y, c = _my_place()
        my_dev = 4 * x + 2 * y + c

        def small_copy(r, slot):
            dx, dy, dc = FLIPS[r]
            return pltpu.make_async_remote_copy(
                src_ref=s_ref, dst_ref=sall_ref.at[slot], send_sem=ssend_sems.at[r], recv_sem=srecv_sems.at[r],
                device_id=(x ^ dx, y ^ dy, c ^ dc), device_id_type=MESH)

        keep_small = pltpu.make_async_copy(s_ref, sall_ref.at[my_dev], local_sem)

        def sib_copy(t):
            dst = recv_ref.at[SIB] if t == SIB else pre_ref.at[t]
            return pltpu.make_async_remote_copy(
                src_ref=to_sib_ref.at[t], dst_ref=dst, send_sem=sib_send.at[t], recv_sem=sib_recv.at[t],
                device_id=(x, y, 1 - c), device_id_type=MESH)

        def chip_copy(t):
            dx, dy = chip_flips[t]
            return pltpu.make_async_remote_copy(
                src_ref=to_chip_ref.at[t], dst_ref=recv_ref.at[t], send_sem=chip_send.at[t], recv_sem=chip_recv.at[t],
                device_id=(x ^ dx, y ^ dy, c), device_id_type=MESH)

        def accumulate(rows):
            acc_ref[rows, :] += _dot_tn(d_ref[:, rows], h_ref[...])

        @pl.when((i == 0) & (kk == 0))
        def _():
            keep_small.start()
            for r in range(len(FLIPS)):
                small_copy(r, my_dev).start()

        @pl.when(kk == 0)
        def _():
            acc_ref[...] = jnp.zeros_like(acc_ref)

        @pl.when(kk < nk - 1)
        def _():
            accumulate(slice(0, 2 * SHARD_PAD))

        for core in range(2):
            @pl.when((kk == nk - 1) & (c == core))
            def _(mine=halves[core], theirs=halves[1 - core]):
                accumulate(theirs)
                for t in range(n_steps):
                    @pl.when(i == t)
                    def _(t=t):
                        to_sib_ref[t] = acc_ref[theirs, :].astype(BF16)
                        sib_copy(t).start()
                accumulate(mine)
                for t in range(len(chip_flips)):
                    @pl.when(i == t)
                    def _(t=t):
                        sib_copy(t).wait_recv()
                        to_chip_ref[t] = (acc_ref[mine, :] + pre_ref[t].astype(F32)).astype(BF16)
                        chip_copy(t).start()

                @pl.when(i == n_steps - 1)
                def _():
                    own_ref[...] = acc_ref[mine, :]

        @pl.when((i == n_steps - 1) & (kk == nk - 1))
        def _():
            for t in range(len(chip_flips)):
                sib_copy(t).wait_send()
                chip_copy(t).wait_send()
                chip_copy(t).wait_recv()
            sib_copy(SIB).wait_send()
            sib_copy(SIB).wait_recv()
            for r, (dx, dy, dc) in enumerate(FLIPS):
                small_copy(r, 4 * (x ^ dx) + 2 * (y ^ dy) + (c ^ dc)).wait_recv()
                small_copy(r, my_dev).wait_send()
            keep_small.wait()

    shard = (SHARD_PAD, D_MODEL)
    return pl.pallas_call(
        body, name="w_in_grad_rs",
        grid_spec=pltpu.PrefetchScalarGridSpec(
            num_scalar_prefetch=1, grid=(n_steps, nk),
            in_specs=[pl.BlockSpec((tk, D_MODEL), lambda i, kk, order: (kk, 0)),
                      pl.BlockSpec((None, tk, 2 * SHARD_PAD), lambda i, kk, order: (order[i], kk, 0)),
                      pl.BlockSpec(memory_space=pl.ANY)],
            out_specs=[pl.BlockSpec(shard, lambda i, kk, order: (0, 0)),
                       pl.BlockSpec(memory_space=pl.ANY), pl.BlockSpec(memory_space=pl.ANY)],
            scratch_shapes=[pltpu.VMEM((2 * SHARD_PAD, D_MODEL), F32),
                            pltpu.VMEM((SIB, *shard), BF16), pltpu.VMEM((SIB + 1, *shard), BF16),
                            pltpu.VMEM((SIB, *shard), BF16),
                            pltpu.SemaphoreType.DMA((SIB + 1,)), pltpu.SemaphoreType.DMA((SIB + 1,)),
                            pltpu.SemaphoreType.DMA((SIB,)), pltpu.SemaphoreType.DMA((SIB,)),
                            pltpu.SemaphoreType.DMA((7,)), pltpu.SemaphoreType.DMA((7,)), pltpu.SemaphoreType.DMA]),
        out_shape=[jax.ShapeDtypeStruct(shard, F32),
                   jax.ShapeDtypeStruct((SIB + 1, *shard), BF16),
                   jax.ShapeDtypeStruct((N_DEV, *small.shape), F32)],
        compiler_params=_params("arbitrary", "arbitrary"),
    )(chip_order, h, dsh, small)


def _adam_math(w, g, m, v):
    m_new = ADAM_B1 * m + (1.0 - ADAM_B1) * g
    v_new = ADAM_B2 * v + (1.0 - ADAM_B2) * (g * g)
    m_hat = m_new / (1.0 - ADAM_B1 ** ADAM_STEP)
    v_hat = v_new / (1.0 - ADAM_B2 ** ADAM_STEP)
    delta = -ADAM_LR * (m_hat / (jnp.sqrt(v_hat) + ADAM_EPS) + ADAM_WD * w)
    return delta, m_new, v_new


def _adam_big(jobs):
    steps = 8
    n = len(jobs)
    idx = jnp.stack([job[1] for job in jobs]).astype(jnp.int32)
    blocks = []
    for own, _, recv, w, m, v in jobs:
        (rw, cw), rp = w.shape, own.shape[1]
        by_cols = rp != rw
        blk_w = (rw, cw // steps) if by_cols else (rw // steps, cw)
        blk_g = (rp, cw // steps) if by_cols else (rw // steps, cw)
        blocks.append((blk_w, blk_g, by_cols))

    def body(idx_ref, *refs):
        ins, outs = refs[:5 * n], refs[5 * n:]
        for j, (blk_w, _, _) in enumerate(blocks):
            o_ref, r_ref, w_ref, m_ref, v_ref = ins[5 * j:5 * j + 5]
            g_ref, d_ref, mo_ref, vo_ref = outs[4 * j:4 * j + 4]
            g = o_ref[...].astype(F32)
            for r in range(r_ref.shape[0]):
                g = g + r_ref[r].astype(F32)
            g = g[0:blk_w[0], :]
            g_ref[...] = g
            d_ref[...], mo_ref[...], vo_ref[...] = _adam_math(w_ref[...], g, m_ref[...], v_ref[...])

    in_specs, out_specs, out_shape, args = [], [], [], []
    for j, ((own, _, recv, w, m, v), (blk_w, blk_g, by_cols)) in enumerate(zip(jobs, blocks)):
        at = (lambda i: (0, i)) if by_cols else (lambda i: (i, 0))
        spec = pl.BlockSpec(blk_w, lambda i, idx_ref, at=at: at(i))
        in_specs += [pl.BlockSpec((None, *blk_g), lambda i, idx_ref, at=at, j=j: (idx_ref[j], *at(i))),
                     pl.BlockSpec((recv.shape[0], *blk_g), lambda i, idx_ref, at=at: (0, *at(i))), spec, spec, spec]
        out_specs += [spec] * 4
        out_shape += [jax.ShapeDtypeStruct(w.shape, F32)] * 4
        args += [own, recv, w, m, v]
    res = pl.pallas_call(
        body, name="adam_big",
        grid_spec=pltpu.PrefetchScalarGridSpec(num_scalar_prefetch=1, grid=(steps,), in_specs=in_specs, out_specs=out_specs),
        out_shape=out_shape,
        compiler_params=_params("parallel"),
    )(idx, *args)
    return [res[4 * j:4 * j + 4] for j in range(n)]


def _adam_small(small_all, params):
    flat = [a for triple in params for a in triple]
    n_par = len(params)

    def body(s_ref, *refs):
        ins, outs, loss_ref = refs[:3 * n_par], refs[3 * n_par:-1], refs[-1]
        g_slab = s_ref[0]
        for dev in range(1, N_DEV):
            g_slab = g_slab + s_ref[dev]
        loss_ref[...] = g_slab[SMALL_LOSS:SMALL_LOSS + 1, :]
        dev = 4 * lax.axis_index("x") + 2 * lax.axis_index("y") + lax.axis_index("c")
        alpha_full = jnp.concatenate([g_slab[SMALL_W_ALPHA + half * B_GATE_RANK:SMALL_W_ALPHA + (half + 1) * B_GATE_RANK]
                                      for half in range(B_KEY_WIDTH // LANES)], axis=1)
        alpha_mine = pltpu.roll(alpha_full, (B_KEY_WIDTH - dev * SHARD_ALPHA) % B_KEY_WIDTH, 1)[:, 0:SHARD_ALPHA]
        grads = [_take_rows(g_slab, SMALL_G_IN, D_MODEL // LANES), _take_rows(g_slab, SMALL_G_FINAL, D_MODEL // LANES),
                 _take_rows(g_slab, SMALL_G_GLA, B_WIDTH // LANES), _take_rows(g_slab, SMALL_B_ALPHA, B_KEY_WIDTH // LANES),
                 g_slab[SMALL_SINKS:SMALL_SINKS + 1, 0:A_HEADS], alpha_mine]
        for i, g in enumerate(grads):
            w_ref, m_ref, v_ref = ins[3 * i:3 * i + 3]
            delta, m_new, v_new = _adam_math(w_ref[...], g, m_ref[...], v_ref[...])
            outs[4 * i][...] = g
            outs[4 * i + 1][...] = delta
            outs[4 * i + 2][...] = m_new
            outs[4 * i + 3][...] = v_new

    res = pl.pallas_call(
        body, name="adam_small",
        out_shape=[jax.ShapeDtypeStruct(t[0].shape, F32) for t in params for _ in range(4)]
                  + [jax.ShapeDtypeStruct((1, LANES), F32)],
    )(small_all, *flat)
    return [res[4 * i:4 * i + 4] for i in range(n_par)], res[-1]


def _local_step(x, cosf, sinf, loss_target, g_in, wt_sh, wa_pad, b_alpha, sinks, g_gla, out_shards, g_final, chip_order):
    B, S, _ = x.shape
    T = B * S
    x2 = x.reshape(T, D_MODEL)
    tgt2 = loss_target.reshape(T, D_MODEL)
    f, (g_woa, g_wob, g_wo) = _in_proj(x2, cosf, sinf, g_in, wt_sh, wa_pad, b_alpha, out_shards)
    w_o = g_wo.reshape(D_MODEL, D_MODEL)
    sink_row = jnp.repeat(sinks, WINDOW).reshape(1, ATT_ROWS)
    sink_col = sink_row.reshape(ATT_ROWS, 1)
    attn, lse = _attn_fwd(f["qkv"], sink_row, B, S)
    o_gla, st_all = _gla_fwd(f["q"], f["k"], f["cum"], f["vb"], B, S)
    (dxres, dattn, dog, dza, dzb, dga, dgb, dw_o, dw_oa, dw_ob, small_a) = _merge(
        x2, tgt2, attn, f["za"], o_gla, f["zb"], f["ga"], f["gb"], g_woa, g_wob, w_o, g_gla, g_final)
    dq, dkv, dsink = _attn_bwd(f["qkv"], dattn, attn, lse, sink_col, B, S)
    (dqb, dkb, dvb, dla), (rv_o, rv_oa, rv_ob) = _gla_bwd(f["q"], f["k"], f["cum"], f["vb"], dog, st_all, B, S,
                                                        [dw_o, dw_oa, dw_ob])
    parts = dict(dq=dq, dkv=dkv, dza=dza, dqb=dqb, dkb=dkb, dvb=dvb, dzb=dzb, dla=dla, u=f["u"], alr=f["alr"],
                 dga=dga, dgb=dgb)
    dx, dsh, small_c = _in_proj_bwd(x2, dxres, cosf, sinf, g_in, f["wt_pad"], wa_pad, parts)
    small = jnp.concatenate([small_a, dsink, small_c], axis=0)
    own_in, rv_in, small_all = _w_in_grad_rs(f["h"], dsh, chip_order, small)
    return dict(grad_x=dx.reshape(B, S, D_MODEL), own_in=own_in, rv_in=rv_in,
                own_o=dw_o, rv_o=rv_o, own_oa=dw_oa, rv_oa=rv_oa, own_ob=dw_ob, rv_ob=rv_ob, small_all=small_all)


def kernel(x, positions, g_in, w_in, w_alpha_up, b_alpha, attn_sinks, g_gla_norm, w_out_a, w_out_b, w_o, g_final, loss_target, m_g_in, m_w_in, m_w_alpha_up, m_b_alpha, m_attn_sinks, m_g_gla_norm, m_w_out_a, m_w_out_b, m_w_o, m_g_final, v_g_in, v_w_in, v_w_alpha_up, v_b_alpha, v_attn_sinks, v_g_gla_norm, v_w_out_a, v_w_out_b, v_w_o, v_g_final):
    xi, yi, ci = _my_place()
    chip = 2 * xi + yi
    chip_order = jnp.stack([chip ^ 3, chip ^ 2, chip ^ 1, chip]).astype(jnp.int32)

    (g_win, g_wa), cosf, sinf = _gather_first(
        [jnp.pad(w_in[0].T.astype(BF16), ((0, SHARD_PAD - SHARD_IN), (0, 0))), w_alpha_up[0].astype(BF16)],
        positions.reshape(-1, 1))
    wt_sh = g_win.reshape(N_DEV * SHARD_PAD, D_MODEL)
    wa_pad = jnp.pad(jnp.concatenate([g_wa[j] for j in range(N_DEV)], axis=1), ((0, RANK_PAD - B_GATE_RANK), (0, 0)))

    r = _local_step(x, cosf, sinf, loss_target, g_in, wt_sh, wa_pad, b_alpha, attn_sinks[0], g_gla_norm,
                    [w_out_a[0].astype(BF16), w_out_b[0].astype(BF16), w_o[0].astype(BF16)],
                    g_final.reshape(1, D_MODEL), chip_order)

    dev = 4 * xi + 2 * yi + ci
    big = _adam_big([(r["own_in"][None], jnp.int32(0), r["rv_in"], w_in[0].T, m_w_in[0].T, v_w_in[0].T),
                     (r["own_oa"], dev, r["rv_oa"], w_out_a[0], m_w_out_a[0], v_w_out_a[0]),
                     (r["own_ob"], dev, r["rv_ob"], w_out_b[0], m_w_out_b[0], v_w_out_b[0]),
                     (r["own_o"], dev, r["rv_o"], w_o[0], m_w_o[0], v_w_o[0])])
    big[0] = [a.T for a in big[0]]
    row = lambda a: a.reshape(1, D_MODEL)
    (s_in, s_final, s_gla, s_ba, s_sinks, s_wa), loss_row = _adam_small(r["small_all"], [
        (g_in, m_g_in, v_g_in), (row(g_final), row(m_g_final), row(v_g_final)),
        (g_gla_norm, m_g_gla_norm, v_g_gla_norm), (b_alpha, m_b_alpha, v_b_alpha),
        (attn_sinks, m_attn_sinks, v_attn_sinks), (w_alpha_up[0], m_w_alpha_up[0], v_w_alpha_up[0])])

    def group(i):
        return (s_in[i], big[0][i][None], s_wa[i][None], s_ba[i], s_sinks[i], s_gla[i], big[1][i][None], big[2][i][None],
                big[3][i][None], s_final[i].reshape(D_MODEL))

    return (loss_row[0, 0], r["grad_x"], *group(0), *group(1), *group(2), *group(3))
```

```python
import functools
import math

import numpy as np
import jax
import jax.numpy as jnp
from jax import lax
from jax.experimental import pallas as pl
from jax.experimental.pallas import tpu as pltpu

F32 = jnp.float32
BF16 = jnp.bfloat16
MESH = pl.DeviceIdType.MESH

D_MODEL = 1024
A_HEADS, A_KV_HEADS, A_HEAD_DIM = 8, 2, 64
A_WIDTH, A_KV_WIDTH = 512, 128
WINDOW = 128
ROPE_THETA = 500000.0
ROPE_DIM = 16
B_HEADS, B_KEY_DIM, B_VAL_DIM = 4, 64, 128
B_KEY_WIDTH, B_WIDTH = 256, 512
B_GATE_RANK = 16
B_GATE_TEMP = 16.0
B_CHUNK = 64
NORM_EPS = 1e-6
NEG_BIG = -1e30
D_IN = 4880
N_DEV = 8
N_CHIPS = 4
ADAM_LR, ADAM_B1, ADAM_B2, ADAM_EPS, ADAM_WD, ADAM_STEP = 0.001, 0.9, 0.999, 1e-08, 0.01, 10

LANES = 128
V7X_VMEM_LIMIT = 56 * 1024 * 1024
V7X_VMEM_LIMIT_MAX = 62 * 1024 * 1024

RANK_PAD = LANES
SEG = {}
_off = 0
for _name, _w in (("qa", 512), ("ka", 128), ("va", 128), ("za", 512), ("qb", 256), ("kb", 256),
                  ("vb", 512), ("zb", 512), ("alr", RANK_PAD), ("ga", 1024), ("gb", 1024)):
    SEG[_name] = (_off, _off + _w)
    _off += _w
D_IN_PAD = _off
ALR_SRC = SEG["alr"][0]
QKV_K, QKV_V, QKV_W = SEG["ka"][0], SEG["va"][0], SEG["va"][1]
ATT_SCALE = A_HEAD_DIM ** -0.5

SHARD_IN = D_IN // N_DEV
SHARD_PAD = 640
SHARD_OUT = D_MODEL // N_DEV
SHARD_ALPHA = B_KEY_WIDTH // N_DEV

SMALL_G_FINAL, SMALL_G_GLA, SMALL_LOSS, SMALL_SINKS, SMALL_G_IN, SMALL_B_ALPHA, SMALL_W_ALPHA = 0, 8, 12, 16, 24, 32, 40
SMALL_ROWS = 72


def _dot(a, b):
    return jnp.dot(a, b, preferred_element_type=F32)


def _dot_nt(a, b):
    return lax.dot_general(a, b, (((1,), (1,)), ((), ())), preferred_element_type=F32)


def _dot_tn(a, b):
    return lax.dot_general(a, b, (((0,), (0,)), ((), ())), preferred_element_type=F32)


def _sigmoid(z):
    return 1.0 / (1.0 + jnp.exp(-z))


def _sigmoid_tanh(z):
    return 0.5 * jnp.tanh(0.5 * z) + 0.5


def _params(*sem):
    return pltpu.CompilerParams(dimension_semantics=sem, vmem_limit_bytes=V7X_VMEM_LIMIT)


def _const_spec(shape):
    nd = len(shape)
    return pl.BlockSpec(shape, lambda *_: (0,) * nd, pipeline_mode=pl.Buffered(1))


def _lane_iota(shape):
    return lax.broadcasted_iota(jnp.int32, shape, 1)


def _row_iota(shape):
    return lax.broadcasted_iota(jnp.int32, shape, 0)


def _split3(v):
    hi = v.astype(BF16)
    r1 = v - hi.astype(F32)
    mid = r1.astype(BF16)
    lo = (r1 - mid.astype(F32)).astype(BF16)
    return hi, mid, lo


def _put_rows(ref, row0, vec):
    for r in range(vec.shape[1] // LANES):
        ref[row0 + r:row0 + r + 1, :] = vec[:, r * LANES:(r + 1) * LANES]


def _take_rows(slab, row0, n):
    return jnp.concatenate([slab[row0 + r:row0 + r + 1, :] for r in range(n)], axis=1)


def _rope_lane_constants():
    half = ROPE_DIM // 2
    inv_freq = np.exp(-math.log(ROPE_THETA) * np.arange(half, dtype=np.float32) * np.float32(2.0 / ROPE_DIM)).astype(np.float32)
    lane = np.arange(LANES)
    j = lane % A_HEAD_DIM
    invf = np.where(j < ROPE_DIM, inv_freq[j % half], 0.0).astype(np.float32)
    sign = np.where(j < half, -1.0, np.where(j < ROPE_DIM, 1.0, 0.0)).astype(np.float32)
    return jnp.asarray(invf)[None, :], jnp.asarray(sign)[None, :]


def _rope_slab(t, cos, sin_signed):
    first = (_lane_iota(t.shape) % A_HEAD_DIM) < (ROPE_DIM // 2)
    partner = jnp.where(first, pltpu.roll(t, LANES - ROPE_DIM // 2, 1), pltpu.roll(t, ROPE_DIM // 2, 1))
    return t * cos + partner * sin_signed


def _shard_pad_cols(j):
    cut = ALR_SRC + B_GATE_RANK
    shift = RANK_PAD - B_GATE_RANK
    a, b = j * SHARD_IN, (j + 1) * SHARD_IN
    if b <= cut:
        return [(a, b)]
    if a >= cut:
        return [(a + shift, b + shift)]
    return [(a, cut), (cut + shift, b + shift)]


def _in_proj(x2, cosf, sinf, g_in, wt_sh, wa_pad, b_alpha, later_shards):
    T = x2.shape[0]
    tm = math.gcd(T, 512)
    sub = math.gcd(tm, 256)
    last = T // tm - 1
    nl = len(later_shards)

    def body(x_ref, cos_ref, sin_ref, g_ref, wsh_ref, wa_ref, ba_ref, *rest):
        sh_refs, rest = rest[:nl], rest[nl:]
        (h_ref, qkv_ref, za_ref, q_ref, k_ref, vb_ref, zb_ref, alr_ref, u_ref, cum_ref, ga_ref, gb_ref, wt_out) = rest[:13]
        all_refs, (wt_ref, send_sems, recv_sems, local_sems, wt_sem) = rest[13:13 + nl], rest[13 + nl:]
        wt_copy = pltpu.make_async_copy(wt_ref, wt_out, wt_sem)
        px, py, pc = _my_place()
        my_dev = 4 * px + 2 * py + pc

        def wcopy(a, r, slot):
            dx, dy, dc = FLIPS[r]
            return pltpu.make_async_remote_copy(
                src_ref=sh_refs[a], dst_ref=all_refs[a].at[slot], send_sem=send_sems.at[a, r],
                recv_sem=recv_sems.at[a, r], device_id=(px ^ dx, py ^ dy, pc ^ dc), device_id_type=MESH)

        keep = [pltpu.make_async_copy(sh_refs[a], all_refs[a].at[my_dev], local_sems.at[a]) for a in range(nl)]

        @pl.when(pl.program_id(0) == 0)
        def _():
            for a in range(nl):
                keep[a].start()
                for r in range(len(FLIPS)):
                    wcopy(a, r, my_dev).start()

        @pl.when(pl.program_id(0) == 0)
        def _():
            for j in range(N_DEV):
                src = j * SHARD_PAD
                for a, b in _shard_pad_cols(j):
                    wt_ref[a:b, :] = wsh_ref[src:src + b - a, :]
                    src += b - a
            a, b = SEG["alr"]
            wt_ref[a + B_GATE_RANK:b, :] = jnp.zeros((RANK_PAD - B_GATE_RANK, D_MODEL), BF16)
            wt_copy.start()

        def one_tile(rows):
            x = x_ref[rows, :]
            r = lax.rsqrt(jnp.mean(x * x, axis=-1, keepdims=True) + NORM_EPS)
            h = (x * r * g_ref[...]).astype(BF16)
            h_ref[rows, :] = h

            def seg(name):
                a, b = SEG[name]
                return _dot_nt(h, wt_ref[a:b, :])

            alr = seg("alr").astype(BF16)
            alr_ref[rows, :] = alr
            u = _dot(alr, wa_ref[...]) + ba_ref[...]
            u_ref[rows, :] = u
            log_a = (jnp.minimum(u, 0.0) - jnp.log(1.0 + jnp.exp(-jnp.abs(u)))) * (1.0 / B_GATE_TEMP)
            row, col = _row_iota((sub, sub)), _lane_iota((sub, sub))
            tri = ((row // B_CHUNK == col // B_CHUNK) & (col <= row)).astype(BF16)
            hi, mid, lo = _split3(log_a)
            cum_ref[rows, :] = _dot(tri, hi) + _dot(tri, mid) + _dot(tri, lo)

            cos, sin = cos_ref[rows, :], sin_ref[rows, :]
            qa = seg("qa") * ATT_SCALE
            for s in range(A_WIDTH // LANES):
                qkv_ref[rows, s * LANES:(s + 1) * LANES] = _rope_slab(qa[:, s * LANES:(s + 1) * LANES], cos, sin).astype(BF16)
            qkv_ref[rows, QKV_K:QKV_V] = _rope_slab(seg("ka"), cos, sin).astype(BF16)
            qkv_ref[rows, QKV_V:QKV_W] = seg("va").astype(BF16)
            za_ref[rows, :] = seg("za").astype(BF16)
            q_ref[rows, :] = seg("qb")
            k_ref[rows, :] = seg("kb")
            vb_ref[rows, :] = seg("vb").astype(BF16)
            zb_ref[rows, :] = seg("zb").astype(BF16)
            ga_ref[rows, :] = seg("ga").astype(BF16)
            gb_ref[rows, :] = seg("gb").astype(BF16)

        for j in range(tm // sub):
            one_tile(pl.ds(j * sub, sub))

        @pl.when(pl.program_id(0) == last)
        def _():
            for a in range(nl):
                for r, (dx, dy, dc) in enumerate(FLIPS):
                    wcopy(a, r, 4 * (px ^ dx) + 2 * (py ^ dy) + (pc ^ dc)).wait_recv()
                    wcopy(a, r, my_dev).wait_send()
                keep[a].wait()
            wt_copy.wait()

    def rows(w):
        return pl.BlockSpec((tm, w), lambda i: (i, 0))

    outs = [("h", D_MODEL, BF16), ("qkv", QKV_W, BF16), ("za", A_WIDTH, BF16), ("q", B_KEY_WIDTH, F32),
            ("k", B_KEY_WIDTH, F32), ("vb", B_WIDTH, BF16), ("zb", B_WIDTH, BF16), ("alr", RANK_PAD, BF16),
            ("u", B_KEY_WIDTH, F32), ("cum", B_KEY_WIDTH, F32), ("ga", D_MODEL, BF16), ("gb", D_MODEL, BF16)]
    res = pl.pallas_call(
        body, name="in_proj", grid=(T // tm,),
        in_specs=[rows(D_MODEL), rows(LANES), rows(LANES), _const_spec((1, D_MODEL)),
                  _const_spec((N_DEV * SHARD_PAD, D_MODEL)), _const_spec((RANK_PAD, B_KEY_WIDTH)),
                  _const_spec((1, B_KEY_WIDTH))] + _any_specs(nl),
        out_specs=[rows(w) for _, w, _ in outs] + _any_specs(1 + nl),
        out_shape=[jax.ShapeDtypeStruct((T, w), dt) for _, w, dt in outs]
                  + [jax.ShapeDtypeStruct((D_IN_PAD, D_MODEL), BF16)]
                  + [jax.ShapeDtypeStruct((N_DEV, *sh.shape), sh.dtype) for sh in later_shards],
        scratch_shapes=[pltpu.VMEM((D_IN_PAD, D_MODEL), BF16),
                        pltpu.SemaphoreType.DMA((nl, len(FLIPS))), pltpu.SemaphoreType.DMA((nl, len(FLIPS))),
                        pltpu.SemaphoreType.DMA((nl,)), pltpu.SemaphoreType.DMA],
        compiler_params=_params("arbitrary"),
    )(x2, cosf, sinf, g_in, wt_sh, wa_pad, b_alpha, *later_shards)
    n_out = len(outs) + 1
    return dict(zip([n for n, _, _ in outs] + ["wt_pad"], res[:n_out])), res[n_out:]


def _dup_kv_head(t, g):
    tf = t.astype(F32)
    keep = (_lane_iota(tf.shape) < A_HEAD_DIM) == (g == 0)
    return jnp.where(keep, tf, pltpu.roll(tf, A_HEAD_DIM, 1)).astype(BF16)


def _stack_heads(t):
    lo = _lane_iota(t.shape) < A_HEAD_DIM
    zero = jnp.zeros_like(t)
    return jnp.concatenate([jnp.where(lo, t, zero), jnp.where(lo, zero, t)], axis=0)


ATT_ROWS = A_HEADS * WINDOW
GROUP_ROWS = ATT_ROWS // A_KV_HEADS
HEADS_PER_GROUP = A_HEADS // A_KV_HEADS


def _band_mask_t(n):
    kj = _row_iota((2 * WINDOW, GROUP_ROWS)) - WINDOW
    qi = _lane_iota((2 * WINDOW, GROUP_ROWS)) % WINDOW
    return (kj <= qi) & (qi - kj < WINDOW) & ((n > 0) | (kj >= 0))


def _stacked_queries(ref, g):
    pairs = range(g * HEADS_PER_GROUP // 2, (g + 1) * HEADS_PER_GROUP // 2)
    return jnp.concatenate([_stack_heads(ref[:, p * LANES:(p + 1) * LANES]) for p in pairs], axis=0)


def _unstack_heads(t, g, ref, dtype):
    lo = _lane_iota((WINDOW, LANES)) < A_HEAD_DIM
    for hh in range(HEADS_PER_GROUP // 2):
        p = g * HEADS_PER_GROUP // 2 + hh
        ref[:, p * LANES:(p + 1) * LANES] = jnp.where(lo, t[2 * hh * WINDOW:(2 * hh + 1) * WINDOW],
                                                       t[(2 * hh + 1) * WINDOW:(2 * hh + 2) * WINDOW]).astype(dtype)


FWD_BLOCKS = 16


def _attn_fwd(qkv, sink_row, B, S):
    T = B * S
    nb = S // WINDOW
    blocks = math.gcd(nb, FWD_BLOCKS)
    steps = nb // blocks

    def one_block(has_prev, sink_ref, q, k, v, o_ref, lse_ref):
        valid = _band_mask_t(has_prev)
        lse_rows = []
        for g in range(A_KV_HEADS):
            kd, vd = _dup_kv_head(k, g), _dup_kv_head(v, g)
            s = jnp.where(valid, _dot_nt(kd, _stacked_queries(q, g)), NEG_BIG)
            sink = sink_ref[:, g * GROUP_ROWS:(g + 1) * GROUP_ROWS]
            m = jnp.maximum(jnp.max(s, axis=0, keepdims=True), sink)
            e = jnp.exp(s - m)
            den = jnp.sum(e, axis=0, keepdims=True) + jnp.exp(sink - m)
            o = _dot_tn((e * (1.0 / den)).astype(BF16), vd)
            _unstack_heads(o, g, o_ref, F32)
            lse = m + jnp.log(den)
            lse_rows += [lse[:, j * WINDOW:(j + 1) * WINDOW] for j in range(HEADS_PER_GROUP)]
        by_head = jnp.concatenate(lse_rows + [jnp.zeros((WINDOW - A_HEADS, WINDOW), F32)], axis=0)
        lse_ref[...] = by_head.T

    def body(sink_ref, q_ref, kc_ref, vc_ref, kp_ref, vp_ref, o_ref, lse_ref):
        k_all = jnp.concatenate([kp_ref[...], kc_ref[...]], axis=0)
        v_all = jnp.concatenate([vp_ref[...], vc_ref[...]], axis=0)
        for j in range(blocks):
            rows = pl.ds(j * WINDOW, WINDOW)
            keys = slice(j * WINDOW, (j + 2) * WINDOW)
            has_prev = pl.program_id(1) if j == 0 else 1
            one_block(has_prev, sink_ref, q_ref[rows, :], k_all[keys], v_all[keys], o_ref.at[rows], lse_ref.at[rows])

    def cur(col, w):
        return pl.BlockSpec((blocks * WINDOW, w), lambda b, n: (b * steps + n, col))

    def prev(col):
        return pl.BlockSpec((WINDOW, LANES), lambda b, n: (b * nb + jnp.maximum(blocks * n - 1, 0), col))

    kcol, vcol = QKV_K // LANES, QKV_V // LANES
    return pl.pallas_call(
        body, name="attn_fwd", grid=(B, steps),
        in_specs=[_const_spec((1, ATT_ROWS)), cur(0, A_WIDTH), cur(kcol, LANES), cur(vcol, LANES), prev(kcol), prev(vcol)],
        out_specs=[cur(0, A_WIDTH), cur(0, LANES)],
        out_shape=[jax.ShapeDtypeStruct((T, A_WIDTH), F32), jax.ShapeDtypeStruct((T, LANES), F32)],
        compiler_params=_params("parallel", "parallel"),
    )(sink_row, qkv, qkv, qkv, qkv, qkv)


ATT_CHUNK = 128


def _chunk_masks(n):
    masks = []
    for half in range(WINDOW // ATT_CHUNK):
        qi = _row_iota((ATT_CHUNK, 2 * WINDOW)) + half * ATT_CHUNK
        kj = _lane_iota((ATT_CHUNK, 2 * WINDOW)) - WINDOW
        masks.append((kj <= qi) & (qi - kj < WINDOW) & ((n > 0) | (kj >= 0)))
    return masks


def _all_stacked_queries(ref):
    return jnp.concatenate([_stacked_queries(ref, g) for g in range(A_KV_HEADS)], axis=0)


def _by_group(fn, lhs, rhs_per_group):
    return jnp.concatenate([fn(lhs[g * GROUP_ROWS:(g + 1) * GROUP_ROWS], rhs_per_group[g])
                            for g in range(A_KV_HEADS)], axis=0)


BWD_BLOCKS = 8


def _attn_bwd(qkv, do, out, lse, sink_col, B, S):
    T = B * S
    nb = S // WINDOW
    M = math.gcd(nb, BWD_BLOCKS)
    steps = nb // M
    n_chunks = ATT_ROWS // ATT_CHUNK
    halves = WINDOW // ATT_CHUNK

    def block(has_prev, sink_ref, q, do_b, out_b, lse_b, k, v, scratch, want_dq):
        s_ref, dp_ref, ds_ref, p_ref = scratch
        width = k.shape[0]
        masks = [mk[:, 0:width] for mk in _chunk_masks(has_prev)]
        kd = [_dup_kv_head(k, g) for g in range(A_KV_HEADS)]
        vd = [_dup_kv_head(v, g) for g in range(A_KV_HEADS)]
        qs, dos = _all_stacked_queries(q), _all_stacked_queries(do_b)
        s_ref[...] = _by_group(_dot_nt, qs, kd)
        dp_ref[...] = _by_group(_dot_nt, dos, vd)
        lane = _lane_iota((ATT_CHUNK, LANES))
        lo = lane < A_HEAD_DIM
        lane1 = _lane_iota((1, LANES))
        dsink_row = jnp.zeros((1, LANES), F32)
        for c in range(n_chunks):
            rows = slice(c * ATT_CHUNK, (c + 1) * ATT_CHUNK)
            head, half = divmod(c, halves)
            qrows = slice(half * ATT_CHUNK, (half + 1) * ATT_CHUNK)
            slab = slice((head // 2) * LANES, (head // 2 + 1) * LANES)
            lse_col = jnp.sum(jnp.where(lane == head, lse_b[qrows, :], 0.0), axis=-1, keepdims=True)
            prod = do_b[qrows, slab].astype(F32) * out_b[qrows, slab].astype(F32)
            mine = lo if head % 2 == 0 else jnp.logical_not(lo)
            delta = jnp.sum(jnp.where(mine, prod, 0.0), axis=-1, keepdims=True)
            prob = jnp.exp(jnp.where(masks[half], s_ref[rows, :], NEG_BIG) - lse_col)
            p_ref[rows, :] = prob.astype(BF16)
            ds_ref[rows, :] = (prob * (dp_ref[rows, :] - delta)).astype(BF16)
            w = -jnp.exp(sink_ref[rows, :] - lse_col) * delta
            dsink_row += jnp.where(lane1 == head, jnp.sum(w, axis=0, keepdims=True), 0.0)
        dq = _by_group(_dot, ds_ref[...], kd) * ATT_SCALE if want_dq else None
        groups = [slice(g * GROUP_ROWS, (g + 1) * GROUP_ROWS) for g in range(A_KV_HEADS)]
        dk = [_dot_tn(ds_ref[rows, :], qs[rows]) for rows in groups]
        dv = [_dot_tn(p_ref[rows, :], dos[rows]) for rows in groups]
        return dq, dk, dv, dsink_row

    def fold(per_group):
        lane = _lane_iota((WINDOW, LANES))
        out = jnp.zeros((WINDOW, LANES), F32)
        for g, acc in enumerate(per_group):
            out = jnp.where((lane < A_HEAD_DIM) == (g == 0), acc + pltpu.roll(acc, A_HEAD_DIM, 1), out)
        return out

    def body(sink_ref, q_ref, qn_ref, do_ref, don_ref, out_ref, outn_ref, lse_ref, lsen_ref, kc_ref, kp_ref, vc_ref, vp_ref,
             dq_ref, dkv_ref, dsink_ref, s_scr, dp_scr, ds_scr, p_scr, s_x, dp_x, ds_x, p_x):
        b, m = pl.program_id(0), pl.program_id(1)

        @pl.when((b == 0) & (m == 0))
        def _():
            dsink_ref[...] = jnp.zeros_like(dsink_ref)

        k_all = jnp.concatenate([kp_ref[...], kc_ref[...]], axis=0)
        v_all = jnp.concatenate([vp_ref[...], vc_ref[...]], axis=0)
        results = []
        for j in range(M):
            rows = slice(j * WINDOW, (j + 1) * WINDOW)
            keys = slice(j * WINDOW, (j + 2) * WINDOW)
            has_prev = m if j == 0 else 1
            results.append(block(has_prev, sink_ref, q_ref[rows, :], do_ref[rows, :], out_ref[rows, :], lse_ref[rows, :],
                                 k_all[keys], v_all[keys], (s_scr.at[j], dp_scr.at[j], ds_scr.at[j], p_scr.at[j]), True))
        last_keys = slice(M * WINDOW, (M + 1) * WINDOW)
        _, dk_x, dv_x, _ = block(1, sink_ref, qn_ref[...], don_ref[...], outn_ref[...], lsen_ref[...],
                                 k_all[last_keys], v_all[last_keys], (s_x, dp_x, ds_x, p_x), False)
        has_next = m < steps - 1
        lo_q = _lane_iota((WINDOW, LANES)) < A_HEAD_DIM
        dsink_row = jnp.zeros((1, LANES), F32)
        for j, (dq, dk, dv, ds_row) in enumerate(results):
            rows = slice(j * WINDOW, (j + 1) * WINDOW)
            for p in range(A_HEADS // 2):
                dq_ref[rows, p * LANES:(p + 1) * LANES] = jnp.where(
                    lo_q, dq[2 * p * WINDOW:(2 * p + 1) * WINDOW], dq[(2 * p + 1) * WINDOW:(2 * p + 2) * WINDOW]).astype(BF16)
            if j + 1 < M:
                dk_next = [t[0:WINDOW] for t in results[j + 1][1]]
                dv_next = [t[0:WINDOW] for t in results[j + 1][2]]
            else:
                dk_next = [jnp.where(has_next, t, 0.0) for t in dk_x]
                dv_next = [jnp.where(has_next, t, 0.0) for t in dv_x]
            dkv_ref[rows, 0:LANES] = fold([own[WINDOW:] + nxt for own, nxt in zip(dk, dk_next)]).astype(BF16)
            dkv_ref[rows, LANES:] = fold([own[WINDOW:] + nxt for own, nxt in zip(dv, dv_next)]).astype(BF16)
            dsink_row += ds_row
        dsink_ref[0:1, :] += dsink_row

    def cur(col, w):
        return pl.BlockSpec((M * WINDOW, w), lambda b, m: (b * steps + m, col))

    def nxt(col, w):
        return pl.BlockSpec((WINDOW, w), lambda b, m: (b * nb + jnp.minimum(M * (m + 1), nb - 1), col))

    def prev(col):
        return pl.BlockSpec((WINDOW, LANES), lambda b, m: (b * nb + jnp.maximum(M * m - 1, 0), col))

    kcol, vcol = QKV_K // LANES, QKV_V // LANES
    scores = (M, ATT_ROWS, 2 * WINDOW)
    extra = (ATT_ROWS, WINDOW)
    return pl.pallas_call(
        body, name="attn_bwd", grid=(B, steps),
        in_specs=[_const_spec((ATT_ROWS, 1)), cur(0, A_WIDTH), nxt(0, A_WIDTH), cur(0, A_WIDTH), nxt(0, A_WIDTH),
                  cur(0, A_WIDTH), nxt(0, A_WIDTH), cur(0, LANES), nxt(0, LANES),
                  cur(kcol, LANES), prev(kcol), cur(vcol, LANES), prev(vcol)],
        out_specs=[cur(0, A_WIDTH), cur(0, 2 * LANES), pl.BlockSpec((8, LANES), lambda b, m: (0, 0))],
        out_shape=[jax.ShapeDtypeStruct((T, A_WIDTH), BF16), jax.ShapeDtypeStruct((T, 2 * LANES), BF16),
                   jax.ShapeDtypeStruct((8, LANES), F32)],
        scratch_shapes=[pltpu.VMEM(scores, F32), pltpu.VMEM(scores, F32), pltpu.VMEM(scores, BF16), pltpu.VMEM(scores, BF16),
                        pltpu.VMEM(extra, F32), pltpu.VMEM(extra, F32), pltpu.VMEM(extra, BF16), pltpu.VMEM(extra, BF16)],
        compiler_params=_params("arbitrary", "arbitrary"),
    )(sink_col, qkv, qkv, do, do, out, out, lse, lse, qkv, qkv, qkv, qkv)


GLA_FWD_TILING = (64, 16)
GLA_BWD_TILING = (256, 4)


def _gla_factors(q_ref, k_ref, cum_ref):
    cpt = q_ref.shape[0] // B_CHUNK
    scale = B_KEY_DIM ** -0.5
    cum = cum_ref[...]
    shape = (B_CHUNK, B_KEY_WIDTH)
    last = jnp.concatenate([jnp.broadcast_to(cum_ref[pl.ds(c * B_CHUNK + B_CHUNK - 1, 1), :], shape)
                            for c in range(cpt)], axis=0)
    mid = jnp.concatenate([jnp.broadcast_to(cum_ref[pl.ds(c * B_CHUNK + B_CHUNK // 2 - 1, 1), :], shape)
                           for c in range(cpt)], axis=0)
    e_qm, e_km, e_qe, e_kd = jnp.exp(cum - mid), jnp.exp(mid - cum), jnp.exp(cum), jnp.exp(last - cum)
    qs = q_ref[...] * scale
    k = k_ref[...]
    return qs, k, (e_qm, e_km, e_qe, e_kd)


def _head_mask(shape, h):
    return (_lane_iota(shape) // B_KEY_DIM) == h


def _stack_masked(t):
    return jnp.concatenate([jnp.where(_head_mask(t.shape, h), t, 0.0) for h in range(B_HEADS)], axis=0).astype(BF16)


def _select_heads(t):
    shape = (B_CHUNK, B_KEY_WIDTH)
    out = jnp.zeros(shape, F32)
    for h in range(B_HEADS):
        out = jnp.where(_head_mask(shape, h), t[h * B_CHUNK:(h + 1) * B_CHUNK], out)
    return out


def _select_state(t):
    shape = (B_VAL_DIM, B_KEY_WIDTH)
    out = jnp.zeros(shape, F32)
    for h in range(B_HEADS):
        out = jnp.where(_head_mask(shape, h), t[h * B_VAL_DIM:(h + 1) * B_VAL_DIM], out)
    return out


def _rows_by_head(t):
    return jnp.concatenate([t[:, h * B_VAL_DIM:(h + 1) * B_VAL_DIM] for h in range(B_HEADS)], axis=0)


def _intra_mask(tile_rows):
    i, j = _row_iota((tile_rows, tile_rows)), _lane_iota((tile_rows, tile_rows))
    return (i // B_CHUNK == j // B_CHUNK) & (j <= i)


def _pair_stack(t, p):
    slab = t[:, p * LANES:(p + 1) * LANES]
    lo = _lane_iota(slab.shape) < B_KEY_DIM
    return jnp.concatenate([jnp.where(lo, slab, 0.0), jnp.where(lo, 0.0, slab)], axis=0).astype(BF16)


def _gla_fwd(q, k, cum, vb, B, S):
    T = B * S
    tile_rows = math.gcd(S, GLA_FWD_TILING[0])
    cpt = tile_rows // B_CHUNK
    nt = S // tile_rows
    tps = math.gcd(nt, GLA_FWD_TILING[1])

    def one_sequence(q_ref, k_ref, cum_ref, v_ref, o_ref, st_all_ref, st_ref):
        qs, kk, (e_qm, e_km, e_qe, e_kd) = _gla_factors(q_ref, k_ref, cum_ref)
        qm, km, qe, kd = qs * e_qm, kk * e_km, qs * e_qe, (kk * e_kd).astype(BF16)
        mask = _intra_mask(tile_rows)
        intra = []
        for p in range(B_HEADS // 2):
            a = _dot_nt(_pair_stack(qm, p), km[:, p * LANES:(p + 1) * LANES].astype(BF16))
            for hh in range(2):
                h = 2 * p + hh
                att = jnp.where(mask, a[hh * tile_rows:(hh + 1) * tile_rows], 0.0).astype(BF16)
                intra.append(_dot(att, v_ref[:, h * B_VAL_DIM:(h + 1) * B_VAL_DIM]))
        inter = []
        for c in range(cpt):
            rows = slice(c * B_CHUNK, (c + 1) * B_CHUNK)
            st = st_ref[...]
            st_all_ref[c] = st
            inter.append(_dot_nt(_stack_masked(qe[rows]), st.astype(BF16)))
            inc = _select_state(_dot_tn(v_ref[rows, :], kd[rows]))
            decay = jnp.exp(cum_ref[pl.ds(c * B_CHUNK + B_CHUNK - 1, 1), :])
            st_ref[...] = st * decay + inc
        for h in range(B_HEADS):
            oi = jnp.concatenate([inter[c][h * B_CHUNK:(h + 1) * B_CHUNK] for c in range(cpt)], axis=0)
            o_ref[:, h * B_VAL_DIM:(h + 1) * B_VAL_DIM] = (intra[h] + oi).astype(BF16)

    def body(q_ref, k_ref, cum_ref, v_ref, o_ref, st_all_ref, st_ref):
        @pl.when(pl.program_id(0) == 0)
        def _():
            st_ref[...] = jnp.zeros_like(st_ref)

        for b in range(B):
            for tile in range(tps):
                tok = pl.ds(tile * tile_rows, tile_rows)
                chunks = pl.ds(tile * cpt, cpt)
                one_sequence(*[r.at[b, tok] for r in (q_ref, k_ref, cum_ref, v_ref, o_ref)],
                             st_all_ref.at[b, chunks], st_ref.at[b])

    def rows(w):
        return pl.BlockSpec((B, tps * tile_rows, w), lambda t: (0, t, 0))

    seq = lambda a: a.reshape(B, S, a.shape[-1])
    o, st_all = pl.pallas_call(
        body, name="gla_fwd", grid=(nt // tps,),
        in_specs=[rows(B_KEY_WIDTH), rows(B_KEY_WIDTH), rows(B_KEY_WIDTH), rows(B_WIDTH)],
        out_specs=[rows(B_WIDTH),
                   pl.BlockSpec((B, tps * cpt, B_VAL_DIM, B_KEY_WIDTH), lambda t: (0, t, 0, 0))],
        out_shape=[jax.ShapeDtypeStruct((B, S, B_WIDTH), BF16),
                   jax.ShapeDtypeStruct((B, S // B_CHUNK, B_VAL_DIM, B_KEY_WIDTH), F32)],
        scratch_shapes=[pltpu.VMEM((B, B_VAL_DIM, B_KEY_WIDTH), F32)],
        compiler_params=_params("arbitrary"),
    )(seq(q), seq(k), seq(cum), seq(vb))
    return o.reshape(T, B_WIDTH), st_all.reshape(T // B_CHUNK, B_VAL_DIM, B_KEY_WIDTH)


def _gla_bwd(q, k, cum, vb, do, st_all, B, S, wgrads):
    T = B * S
    tile_rows = math.gcd(S, GLA_BWD_TILING[0])
    cpt = tile_rows // B_CHUNK
    nt = S // tile_rows
    tps = math.gcd(nt, GLA_BWD_TILING[1])
    steps = nt // tps
    scale = B_KEY_DIM ** -0.5
    nw = len(wgrads)

    def one_sequence(q_ref, k_ref, cum_ref, v_ref, do_ref, st_all_ref, dq_ref, dk_ref, dv_ref, dla_ref, dst_ref):
        qs, kk, (e_qm, e_km, e_qe, e_kd) = _gla_factors(q_ref, k_ref, cum_ref)
        qm, km, qe, kd = qs * e_qm, kk * e_km, qs * e_qe, kk * e_kd
        mask = _intra_mask(tile_rows)
        dqm_slabs, dkm_slabs, dv_intra = [], [], []
        for p in range(B_HEADS // 2):
            qm_st = _pair_stack(qm, p)
            km_p = km[:, p * LANES:(p + 1) * LANES].astype(BF16)
            a = _dot_nt(qm_st, km_p)
            da_blocks, dqm_h = [], []
            for hh in range(2):
                h = 2 * p + hh
                vs = slice(h * B_VAL_DIM, (h + 1) * B_VAL_DIM)
                att = jnp.where(mask, a[hh * tile_rows:(hh + 1) * tile_rows], 0.0).astype(BF16)
                dv_intra.append(_dot_tn(att, do_ref[:, vs]))
                da = jnp.where(mask, _dot_nt(do_ref[:, vs], v_ref[:, vs]), 0.0).astype(BF16)
                da_blocks.append(da)
                dqm_h.append(_dot(da, km_p))
            lo = _lane_iota((tile_rows, LANES)) < B_KEY_DIM
            dqm_slabs.append(jnp.where(lo, dqm_h[0], dqm_h[1]))
            dkm_slabs.append(_dot_tn(jnp.concatenate(da_blocks, axis=0), qm_st))
        dqm = jnp.concatenate(dqm_slabs, axis=1)
        dkm = jnp.concatenate(dkm_slabs, axis=1)

        dqe_c, dkd_c, dv_inter, tail_c = ([None] * cpt for _ in range(4))
        for c in reversed(range(cpt)):
            rows = slice(c * B_CHUNK, (c + 1) * B_CHUNK)
            dst = dst_ref[...]
            dst_b = dst.astype(BF16)
            dv_inter[c] = _dot_nt(_stack_masked(kd[rows]), dst_b)
            dkd_c[c] = _select_heads(_dot(_rows_by_head(v_ref[rows, :]), dst_b))
            do_c = do_ref[rows, :]
            dqe_c[c] = _select_heads(_dot(_rows_by_head(do_c), st_all_ref[c].astype(BF16)))
            contrib = _select_state(_dot_tn(do_c, qe[rows].astype(BF16)))
            decay = jnp.exp(cum_ref[pl.ds(c * B_CHUNK + B_CHUNK - 1, 1), :])
            tail = (jnp.sum(kk[rows] * dkd_c[c] * e_kd[rows], axis=0, keepdims=True)
                    + decay * jnp.sum(st_all_ref[c] * dst, axis=0, keepdims=True))
            tail_c[c] = jnp.broadcast_to(tail, (B_CHUNK, B_KEY_WIDTH))
            dst_ref[...] = dst * decay + contrib
        dqe = jnp.concatenate(dqe_c, axis=0)
        dkd = jnp.concatenate(dkd_c, axis=0)
        dqs = dqm * e_qm + dqe * e_qe
        dk = dkm * e_km + dkd * e_kd
        dq_ref[...] = (dqs * scale).astype(BF16)
        dk_ref[...] = dk.astype(BF16)
        for h in range(B_HEADS):
            dvi = jnp.concatenate([dv_inter[c][h * B_CHUNK:(h + 1) * B_CHUNK] for c in range(cpt)], axis=0)
            dv_ref[:, h * B_VAL_DIM:(h + 1) * B_VAL_DIM] = (dv_intra[h] + dvi).astype(BF16)
        dd = qs * dqs - kk * dk
        i, j = _row_iota((tile_rows, tile_rows)), _lane_iota((tile_rows, tile_rows))
        upper = ((i // B_CHUNK == j // B_CHUNK) & (j >= i)).astype(BF16)
        hi, mid, lo3 = _split3(dd)
        dla_ref[...] = _dot(upper, hi) + _dot(upper, mid) + _dot(upper, lo3) + jnp.concatenate(tail_c, axis=0)

    def body(q_ref, k_ref, cum_ref, v_ref, do_ref, st_all_ref, *rest):
        g_refs, (dq_ref, dk_ref, dv_ref, dla_ref) = rest[:nw], rest[nw:nw + 4]
        rv_refs, (dst_ref, send_sems, recv_sems) = rest[nw + 4:2 * nw + 4], rest[2 * nw + 4:]
        x, y, c = _my_place()

        def wcopy(a, r):
            dx, dy, dc = FLIPS[r]
            return pltpu.make_async_remote_copy(
                src_ref=g_refs[a].at[4 * (x ^ dx) + 2 * (y ^ dy) + (c ^ dc)], dst_ref=rv_refs[a].at[r],
                send_sem=send_sems.at[a, r], recv_sem=recv_sems.at[a, r],
                device_id=(x ^ dx, y ^ dy, c ^ dc), device_id_type=MESH)

        @pl.when(pl.program_id(0) == 0)
        def _():
            dst_ref[...] = jnp.zeros_like(dst_ref)
            for a in range(nw):
                for r in range(len(FLIPS)):
                    wcopy(a, r).start()

        for b in range(B):
            for tile in reversed(range(tps)):
                tok = pl.ds(tile * tile_rows, tile_rows)
                chunks = pl.ds(tile * cpt, cpt)
                one_sequence(*[r.at[b, tok] for r in (q_ref, k_ref, cum_ref, v_ref, do_ref)], st_all_ref.at[b, chunks],
                             *[r.at[b, tok] for r in (dq_ref, dk_ref, dv_ref, dla_ref)], dst_ref.at[b])

        @pl.when(pl.program_id(0) == steps - 1)
        def _():
            for a in range(nw):
                for r in range(len(FLIPS)):
                    wcopy(a, r).wait()

    def rows(w):
        return pl.BlockSpec((B, tps * tile_rows, w), lambda t: (0, steps - 1 - t, 0))

    seq = lambda a: a.reshape(B, S, a.shape[-1])
    res = pl.pallas_call(
        body, name="gla_bwd", grid=(steps,),
        in_specs=[rows(B_KEY_WIDTH), rows(B_KEY_WIDTH), rows(B_KEY_WIDTH), rows(B_WIDTH), rows(B_WIDTH),
                  pl.BlockSpec((B, tps * cpt, B_VAL_DIM, B_KEY_WIDTH), lambda t: (0, steps - 1 - t, 0, 0))]
                 + _any_specs(nw),
        out_specs=[rows(B_KEY_WIDTH), rows(B_KEY_WIDTH), rows(B_WIDTH), rows(B_KEY_WIDTH)] + _any_specs(nw),
        out_shape=[jax.ShapeDtypeStruct((B, S, B_KEY_WIDTH), BF16), jax.ShapeDtypeStruct((B, S, B_KEY_WIDTH), BF16),
                   jax.ShapeDtypeStruct((B, S, B_WIDTH), BF16), jax.ShapeDtypeStruct((B, S, B_KEY_WIDTH), F32)]
                  + [jax.ShapeDtypeStruct((len(FLIPS), *g.shape[1:]), g.dtype) for g in wgrads],
        scratch_shapes=[pltpu.VMEM((B, B_VAL_DIM, B_KEY_WIDTH), F32),
                        pltpu.SemaphoreType.DMA((nw, len(FLIPS))), pltpu.SemaphoreType.DMA((nw, len(FLIPS)))],
        compiler_params=_params("arbitrary"),
    )(seq(q), seq(k), seq(cum), seq(vb), seq(do), st_all.reshape(B, S // B_CHUNK, B_VAL_DIM, B_KEY_WIDTH), *wgrads)
    return [a.reshape(T, a.shape[-1]) for a in res[:4]], res[4:]


def _merge(x2, tgt2, attn, za, o_gla, zb, ga, gb, w_oa_sh, w_ob_sh, w_o, g_gla, g_final):
    T = x2.shape[0]
    tm = math.gcd(T, 512)
    sub = math.gcd(tm, 256)
    last = T // tm - 1

    def body(x_ref, tgt_ref, attn_ref, za_ref, og_ref, zb_ref, ga_ref, gb_ref,
             woa_sh_ref, wob_sh_ref, wo_ref, gg_ref, gf_ref,
             dxres_ref, dattn_ref, dog_ref, dza_ref, dzb_ref, dga_ref, dgb_ref,
             dwo_out, dwoa_out, dwob_out, small_ref,
             awo_ref, awoa_ref, awob_ref, agf_ref, agg_ref, loss_ref, woa_ref, wob_ref,
             dwo_ref, dwoa_ref, dwob_ref, w_sems, dw_sems):
        w_copies = [pltpu.make_async_copy(sh.at[j], dst.at[:, j * SHARD_OUT:(j + 1) * SHARD_OUT], w_sems.at[a, j])
                    for a, (sh, dst) in enumerate(((woa_sh_ref, woa_ref), (wob_sh_ref, wob_ref))) for j in range(N_DEV)]
        dw_copies = [pltpu.make_async_copy(src, dst, dw_sems.at[a])
                     for a, (src, dst) in enumerate(((dwo_ref, dwo_out), (dwoa_ref, dwoa_out), (dwob_ref, dwob_out)))]

        @pl.when(pl.program_id(0) == 0)
        def _():
            for cp in w_copies:
                cp.start()
            for r in (awo_ref, awoa_ref, awob_ref, agf_ref, agg_ref, loss_ref):
                r[...] = jnp.zeros_like(r)
            for cp in w_copies:
                cp.wait()

        def one_tile(rows):
            za_v = za_ref[rows, :].astype(F32)
            sig_za = _sigmoid_tanh(za_v)
            silu_a = za_v * sig_za
            attn_v = attn_ref[rows, :].astype(F32)
            oa = (attn_v * silu_a).astype(BF16)
            ya = _dot(oa, woa_ref[...])
            og = og_ref[rows, :].astype(F32)
            zb_v = zb_ref[rows, :].astype(F32)
            sig_zb = _sigmoid_tanh(zb_v)
            silu_b = zb_v * sig_zb
            gg = gg_ref[...]
            on_parts, rinv_parts = [], []
            for h in range(B_HEADS):
                seg = og[:, h * B_VAL_DIM:(h + 1) * B_VAL_DIM]
                rinv = lax.rsqrt(jnp.mean(seg * seg, axis=-1, keepdims=True) + NORM_EPS)
                rinv_parts.append(rinv)
                on_parts.append(seg * rinv)
            on = jnp.concatenate(on_parts, axis=1)
            obn = on * gg
            ob = (obn * silu_b).astype(BF16)
            yb = _dot(ob, wob_ref[...])
            sig_a = _sigmoid_tanh(ga_ref[rows, :].astype(F32))
            sig_b = _sigmoid_tanh(gb_ref[rows, :].astype(F32))
            merged = (sig_a * ya + sig_b * yb).astype(BF16)
            out = x_ref[rows, :] + _dot(merged, wo_ref[...])
            rf = lax.rsqrt(jnp.mean(out * out, axis=-1, keepdims=True) + NORM_EPS)
            nrm = out * rf
            gf = gf_ref[...]
            err = nrm * gf - tgt_ref[rows, :]
            loss = jnp.sum(err * err) * (0.5 / D_MODEL)

            dy = err * (1.0 / D_MODEL)
            dgf = jnp.sum(dy * nrm, axis=0, keepdims=True)
            dn = dy * gf
            dout = rf * (dn - nrm * jnp.mean(dn * nrm, axis=-1, keepdims=True))
            dxres_ref[rows, :] = dout
            dout_b = dout.astype(BF16)
            dmerged = _dot_nt(dout_b, wo_ref[...])
            dya = dmerged * sig_a
            dyb = dmerged * sig_b
            dga_ref[rows, :] = (dmerged * ya * sig_a * (1.0 - sig_a)).astype(BF16)
            dgb_ref[rows, :] = (dmerged * yb * sig_b * (1.0 - sig_b)).astype(BF16)
            dya_b, dyb_b = dya.astype(BF16), dyb.astype(BF16)
            doa = _dot_nt(dya_b, woa_ref[...])
            dattn_ref[rows, :] = (doa * silu_a).astype(BF16)
            dza_ref[rows, :] = (doa * attn_v * (sig_za * (1.0 + za_v * (1.0 - sig_za)))).astype(BF16)
            dob = _dot_nt(dyb_b, wob_ref[...])
            dzb_ref[rows, :] = (dob * obn * (sig_zb * (1.0 + zb_v * (1.0 - sig_zb)))).astype(BF16)
            dobn = dob * silu_b
            dgg = jnp.sum(dobn * on, axis=0, keepdims=True)
            don = dobn * gg
            for h in range(B_HEADS):
                sl = slice(h * B_VAL_DIM, (h + 1) * B_VAL_DIM)
                don_h, on_h = don[:, sl], on[:, sl]
                dog_ref[rows, sl] = (rinv_parts[h] * (don_h - on_h * jnp.mean(don_h * on_h, axis=-1, keepdims=True))
                                     ).astype(BF16)
            return (merged, dout_b, oa, dya_b, ob, dyb_b), (loss, dgf, dgg)

        tiles = [one_tile(pl.ds(j * sub, sub)) for j in range(tm // sub)]
        merged, dout_b, oa, dya_b, ob, dyb_b = (jnp.concatenate(parts, axis=0) for parts in zip(*[t[0] for t in tiles]))
        awo_ref[...] += _dot_tn(merged, dout_b)
        awoa_ref[...] += _dot_tn(oa, dya_b)
        awob_ref[...] += _dot_tn(ob, dyb_b)
        for _, (loss, dgf, dgg) in tiles:
            loss_ref[...] += loss
            agf_ref[...] += dgf
            agg_ref[...] += dgg

        @pl.when(pl.program_id(0) == last)
        def _():
            for j in range(N_DEV):
                dwo_ref[j] = awo_ref[j * SHARD_OUT:(j + 1) * SHARD_OUT, :].astype(BF16)
                dwoa_ref[j] = awoa_ref[:, j * SHARD_OUT:(j + 1) * SHARD_OUT].astype(BF16)
                dwob_ref[j] = awob_ref[:, j * SHARD_OUT:(j + 1) * SHARD_OUT].astype(BF16)
            small_ref[...] = jnp.zeros_like(small_ref)
            _put_rows(small_ref, SMALL_G_FINAL, agf_ref[...])
            _put_rows(small_ref, SMALL_G_GLA, agg_ref[...])
            small_ref[SMALL_LOSS:SMALL_LOSS + 1, :] = loss_ref[...]
            for cp in dw_copies:
                cp.start()
            for cp in dw_copies:
                cp.wait()

    def rows(w):
        return pl.BlockSpec((tm, w), lambda i: (i, 0))

    def whole(shape):
        nd = len(shape)
        return pl.BlockSpec(shape, lambda i: (0,) * nd)

    outs = [((T, D_MODEL), F32, rows(D_MODEL)), ((T, A_WIDTH), BF16, rows(A_WIDTH)), ((T, B_WIDTH), BF16, rows(B_WIDTH)),
            ((T, A_WIDTH), BF16, rows(A_WIDTH)), ((T, B_WIDTH), BF16, rows(B_WIDTH)),
            ((T, D_MODEL), BF16, rows(D_MODEL)), ((T, D_MODEL), BF16, rows(D_MODEL)),
            ((N_DEV, SHARD_OUT, D_MODEL), BF16, pl.BlockSpec(memory_space=pl.ANY)),
            ((N_DEV, A_WIDTH, SHARD_OUT), BF16, pl.BlockSpec(memory_space=pl.ANY)),
            ((N_DEV, B_WIDTH, SHARD_OUT), BF16, pl.BlockSpec(memory_space=pl.ANY)),
            ((SMALL_SINKS, LANES), F32, whole((SMALL_SINKS, LANES)))]
    return pl.pallas_call(
        body, name="merge", grid=(T // tm,),
        in_specs=[rows(D_MODEL), rows(D_MODEL), rows(A_WIDTH), rows(A_WIDTH), rows(B_WIDTH), rows(B_WIDTH),
                  rows(D_MODEL), rows(D_MODEL),
                  pl.BlockSpec(memory_space=pl.ANY), pl.BlockSpec(memory_space=pl.ANY),
                  _const_spec((D_MODEL, D_MODEL)), _const_spec((1, B_WIDTH)), _const_spec((1, D_MODEL))],
        out_specs=[o[2] for o in outs],
        out_shape=[jax.ShapeDtypeStruct(o[0], o[1]) for o in outs],
        scratch_shapes=[pltpu.VMEM((D_MODEL, D_MODEL), F32), pltpu.VMEM((A_WIDTH, D_MODEL), F32),
                        pltpu.VMEM((B_WIDTH, D_MODEL), F32), pltpu.VMEM((1, D_MODEL), F32), pltpu.VMEM((1, B_WIDTH), F32),
                        pltpu.VMEM((1, LANES), F32), pltpu.VMEM((A_WIDTH, D_MODEL), BF16),
                        pltpu.VMEM((B_WIDTH, D_MODEL), BF16),
                        pltpu.VMEM((N_DEV, SHARD_OUT, D_MODEL), BF16), pltpu.VMEM((N_DEV, A_WIDTH, SHARD_OUT), BF16),
                        pltpu.VMEM((N_DEV, B_WIDTH, SHARD_OUT), BF16),
                        pltpu.SemaphoreType.DMA((2, N_DEV)), pltpu.SemaphoreType.DMA((3,))],
        compiler_params=pltpu.CompilerParams(dimension_semantics=("arbitrary",), vmem_limit_bytes=V7X_VMEM_LIMIT_MAX),
    )(x2, tgt2, attn, za, o_gla, zb, ga, gb, w_oa_sh, w_ob_sh, w_o, g_gla, g_final)


def _in_proj_bwd(x2, dxres, cosf, sinf, g_in, wt_pad, wa_pad, parts):
    T = x2.shape[0]
    tm = math.gcd(T, 512)
    sub = math.gcd(tm, 256)
    last = T // tm - 1
    base = SMALL_G_IN

    def body(x_ref, dxres_ref, cos_ref, sin_ref, g_ref, wt_ref, wa_ref,
             dq_ref, dkv_ref, dza_ref, dqb_ref, dkb_ref, dvb_ref, dzb_ref, dla_ref, u_ref, alr_ref, dga_ref, dgb_ref,
             dx_ref, dsh_ref, small_ref, dproj_ref, agin_ref, aba_ref, awa_ref):
        @pl.when(pl.program_id(0) == 0)
        def _():
            for r in (agin_ref, aba_ref, awa_ref):
                r[...] = jnp.zeros_like(r)

        def one_tile(rows):
            cos, nsin = cos_ref[rows, :], -sin_ref[rows, :]
            for s in range(A_WIDTH // LANES):
                sl = slice(s * LANES, (s + 1) * LANES)
                dproj_ref[rows, sl] = _rope_slab(dq_ref[rows, sl].astype(F32), cos, nsin).astype(BF16)
            dproj_ref[rows, QKV_K:QKV_V] = _rope_slab(dkv_ref[rows, 0:LANES].astype(F32), cos, nsin).astype(BF16)
            dproj_ref[rows, QKV_V:QKV_W] = dkv_ref[rows, LANES:]

            def put(name, val):
                a, b = SEG[name]
                dproj_ref[rows, a:b] = val

            put("za", dza_ref[rows, :])
            put("qb", dqb_ref[rows, :])
            put("kb", dkb_ref[rows, :])
            put("vb", dvb_ref[rows, :])
            put("zb", dzb_ref[rows, :])
            put("ga", dga_ref[rows, :])
            put("gb", dgb_ref[rows, :])
            du = dla_ref[rows, :] * (1.0 / B_GATE_TEMP) * _sigmoid(-u_ref[rows, :])
            du_b = du.astype(BF16)
            put("alr", _dot_nt(du_b, wa_ref[...]).astype(BF16))

            for j in range(N_DEV):
                col = (j % 2) * SHARD_PAD
                for a, b in _shard_pad_cols(j):
                    dsh_ref[j // 2, rows, col:col + b - a] = dproj_ref[rows, a:b]
                    col += b - a
                dsh_ref[j // 2, rows, col:(j % 2 + 1) * SHARD_PAD] = jnp.zeros((sub, SHARD_PAD - SHARD_IN), BF16)

            dh = _dot(dproj_ref[rows, :], wt_ref[...])
            x = x_ref[rows, :]
            r = lax.rsqrt(jnp.mean(x * x, axis=-1, keepdims=True) + NORM_EPS)
            nrm = x * r
            dn = dh * g_ref[...]
            dx_ref[rows, :] = dxres_ref[rows, :] + r * (dn - nrm * jnp.mean(dn * nrm, axis=-1, keepdims=True))
            return jnp.sum(dh * nrm, axis=0, keepdims=True), jnp.sum(du, axis=0, keepdims=True), alr_ref[rows, :], du_b

        for j in range(tm // sub):
            dgin, dba, alr, du_b = one_tile(pl.ds(j * sub, sub))
            agin_ref[...] += dgin
            aba_ref[...] += dba
            awa_ref[...] += _dot_tn(alr, du_b)

        @pl.when(pl.program_id(0) == last)
        def _():
            small_ref[...] = jnp.zeros_like(small_ref)
            _put_rows(small_ref, SMALL_G_IN - base, agin_ref[...])
            _put_rows(small_ref, SMALL_B_ALPHA - base, aba_ref[...])
            for half in range(B_KEY_WIDTH // LANES):
                r0 = SMALL_W_ALPHA - base + half * B_GATE_RANK
                small_ref[r0:r0 + B_GATE_RANK, :] = awa_ref[0:B_GATE_RANK, half * LANES:(half + 1) * LANES]

    def rows(w):
        return pl.BlockSpec((tm, w), lambda i: (i, 0))

    names = ["dq", "dkv", "dza", "dqb", "dkb", "dvb", "dzb", "dla", "u", "alr", "dga", "dgb"]
    return pl.pallas_call(
        body, name="in_proj_bwd", grid=(T // tm,),
        in_specs=[rows(D_MODEL), rows(D_MODEL), rows(LANES), rows(LANES), _const_spec((1, D_MODEL)),
                  _const_spec((D_IN_PAD, D_MODEL)), _const_spec((RANK_PAD, B_KEY_WIDTH))]
                 + [rows(parts[n].shape[1]) for n in names],
        out_specs=[rows(D_MODEL), pl.BlockSpec((N_CHIPS, tm, 2 * SHARD_PAD), lambda i: (0, i, 0)),
                   pl.BlockSpec((SMALL_ROWS - base, LANES), lambda i: (0, 0))],
        out_shape=[jax.ShapeDtypeStruct((T, D_MODEL), F32), jax.ShapeDtypeStruct((N_CHIPS, T, 2 * SHARD_PAD), BF16),
                   jax.ShapeDtypeStruct((SMALL_ROWS - base, LANES), F32)],
        scratch_shapes=[pltpu.VMEM((tm, D_IN_PAD), BF16), pltpu.VMEM((1, D_MODEL), F32), pltpu.VMEM((1, B_KEY_WIDTH), F32),
                        pltpu.VMEM((RANK_PAD, B_KEY_WIDTH), F32)],
        compiler_params=pltpu.CompilerParams(dimension_semantics=("arbitrary",), vmem_limit_bytes=V7X_VMEM_LIMIT_MAX),
    )(x2, dxres, cosf, sinf, g_in, wt_pad, wa_pad, *[parts[n] for n in names])


FLIPS = [(dx, dy, dc) for dx in (0, 1) for dy in (0, 1) for dc in (0, 1)][1:]
GATHER_PIECES = 2
BF16_TILE_ROWS = 16
RS_PIECES = 2


def _my_place():
    return lax.axis_index("x"), lax.axis_index("y"), lax.axis_index("c")


def _any_specs(n):
    return [pl.BlockSpec(memory_space=pl.ANY)] * n


def _gather_first(shards, pos_col):
    n = len(shards)
    T = pos_col.shape[0]
    rows_per_pass = math.gcd(T, 512)
    pieces = [GATHER_PIECES if sh.shape[0] % (GATHER_PIECES * BF16_TILE_ROWS) == 0 else 1 for sh in shards]
    max_pieces = max(pieces)
    invf, sign = _rope_lane_constants()

    def body(*refs):
        ins, (pos_ref, invf_ref, sign_ref) = refs[:n], refs[n:n + 3]
        outs, (cos_ref, sin_ref) = refs[n + 3:2 * n + 3], refs[2 * n + 3:2 * n + 5]
        send_sems, recv_sems, local_sems = refs[2 * n + 5:]
        x, y, c = _my_place()
        me, sibling = (x, y, c), (x, y, 1 - c)
        onward, source, diagonal = (x ^ (1 - c), y ^ c), (x ^ c, y ^ (1 - c)), (1 - x, 1 - y)

        def block(a, px, py, pc):
            return outs[a].at[4 * px + 2 * py + pc]

        def copy(a, p, k, blk, to, own=False):
            rows = shards[a].shape[0] // pieces[a]
            piece = pl.ds(p * rows, rows)
            return pltpu.make_async_remote_copy(
                src_ref=(ins[a] if own else block(a, *blk)).at[piece], dst_ref=block(a, *blk).at[piece],
                send_sem=send_sems.at[a, p, k], recv_sem=recv_sems.at[a, p, k], device_id=to, device_id_type=MESH)

        every = [(a, p) for a in range(n) for p in range(pieces[a])]
        mine = [pltpu.make_async_copy(ins[a], block(a, *me), local_sems.at[a]) for a in range(n)]
        for cp in mine:
            cp.start()
        first = []
        for a, p in every:
            first.append(copy(a, p, 0, me, sibling, own=True))
            first += [copy(a, p, 1 + j, me, (*chip, c), own=True) for j, chip in enumerate((onward, source))]
        for cp in first:
            cp.start()

        def tables(i, carry):
            rows = pl.ds(pl.multiple_of(i * rows_per_pass, rows_per_pass), rows_per_pass)
            ang = pos_ref[rows, :].astype(F32) * invf_ref[...]
            cos_ref[rows, :] = jnp.cos(ang)
            sin_ref[rows, :] = jnp.sin(ang) * sign_ref[...]
            return carry

        lax.fori_loop(0, T // rows_per_pass, tables, 0)

        passed = []
        for j, chip in ((1, source), (0, onward), (2, diagonal)):
            for a, p in every:
                copy(a, p, 1 + j, (*chip, c), me).wait_recv()
                todo = [copy(a, p, 4 + j, (*chip, c), sibling)]
                if j == 1:
                    todo.append(copy(a, p, 1 + 2, (*chip, c), (*onward, c)))
                for cp in todo:
                    cp.start()
                passed += todo
        for a, p in every:
            copy(a, p, 0, sibling, me).wait_recv()
            for j, chip in enumerate((source, onward, diagonal)):
                copy(a, p, 4 + j, (*chip, 1 - c), me).wait_recv()
        for cp in first + passed:
            cp.wait_send()
        for cp in mine:
            cp.wait()

    vmem = pl.BlockSpec(memory_space=pltpu.VMEM)
    res = pl.pallas_call(
        body, name="gather_weights",
        in_specs=_any_specs(n) + [vmem] * 3, out_specs=_any_specs(n) + [vmem] * 2,
        out_shape=[jax.ShapeDtypeStruct((N_DEV, *s.shape), s.dtype) for s in shards]
                  + [jax.ShapeDtypeStruct((T, LANES), F32)] * 2,
        scratch_shapes=[pltpu.SemaphoreType.DMA((n, max_pieces, 7)), pltpu.SemaphoreType.DMA((n, max_pieces, 7)),
                        pltpu.SemaphoreType.DMA((n,))],
        compiler_params=pltpu.CompilerParams(vmem_limit_bytes=V7X_VMEM_LIMIT),
    )(*shards, pos_col, invf, sign)
    return res[:n], res[n], res[n + 1]


def _w_in_grad_rs(h, dsh, chip_order, small):
    T = h.shape[0]
    tk = math.gcd(T, 2048)
    nk = T // tk
    chip_flips = [(1, 1), (1, 0), (0, 1)]
    n_steps = len(chip_flips) + 1
    SIB = len(chip_flips)
    halves = (slice(0, SHARD_PAD), slice(SHARD_PAD, 2 * SHARD_PAD))
    piece_rows = SHARD_PAD // RS_PIECES

    def body(order_ref, h_ref, d_ref, s_ref, own_ref, recv_ref, sall_ref,
             acc_ref, pre_ref, to_sib_ref, to_chip_ref,
             sib_send, sib_recv, chip_send, chip_recv, ssend_sems, srecv_sems, local_sem):
        i, kk = pl.program_id(0), pl.program_id(1)
        x, y, c = _my_place()
        my_dev = 4 * x + 2 * y + c

        def small_copy(r, slot):
            dx, dy, dc = FLIPS[r]
            return pltpu.make_async_remote_copy(
                src_ref=s_ref, dst_ref=sall_ref.at[slot], send_sem=ssend_sems.at[r], recv_sem=srecv_sems.at[r],
                device_id=(x ^ dx, y ^ dy, c ^ dc), device_id_type=MESH)

        keep_small = pltpu.make_async_copy(s_ref, sall_ref.at[my_dev], local_sem)

        def sib_copy(t, q):
            dst = recv_ref.at[SIB] if t == SIB else pre_ref.at[t]
            return pltpu.make_async_remote_copy(
                src_ref=to_sib_ref.at[t, piece(q)], dst_ref=dst.at[piece(q)],
                send_sem=sib_send.at[t, q], recv_sem=sib_recv.at[t, q], device_id=(x, y, 1 - c), device_id_type=MESH)

        def chip_copy(t, q):
            dx, dy = chip_flips[t]
            return pltpu.make_async_remote_copy(
                src_ref=to_chip_ref.at[t, piece(q)], dst_ref=recv_ref.at[t, piece(q)],
                send_sem=chip_send.at[t, q], recv_sem=chip_recv.at[t, q],
                device_id=(x ^ dx, y ^ dy, c), device_id_type=MESH)

        def piece(q):
            return pl.ds(q * piece_rows, piece_rows)

        def accumulate(rows):
            acc_ref[rows, :] += _dot_tn(d_ref[:, rows], h_ref[...])

        @pl.when((i == 0) & (kk == 0))
        def _():
            keep_small.start()
            for r in range(len(FLIPS)):
                small_copy(r, my_dev).start()

        @pl.when(kk == 0)
        def _():
            acc_ref[...] = jnp.zeros_like(acc_ref)

        @pl.when(kk < nk - 1)
        def _():
            accumulate(slice(0, 2 * SHARD_PAD))

        for core in range(2):
            @pl.when((kk == nk - 1) & (c == core))
            def _(mine=halves[core], theirs=halves[1 - core]):
                accumulate(theirs)
                for t in range(n_steps):
                    @pl.when(i == t)
                    def _(t=t):
                        to_sib_ref[t] = acc_ref[theirs, :].astype(BF16)
                        for q in range(RS_PIECES):
                            sib_copy(t, q).start()
                for q in range(RS_PIECES):
                    rows = slice(mine.start + q * piece_rows, mine.start + (q + 1) * piece_rows)
                    accumulate(rows)
                    for t in range(len(chip_flips)):
                        @pl.when(i == t)
                        def _(t=t, q=q, rows=rows):
                            sib_copy(t, q).wait_recv()
                            to_chip_ref[t, piece(q), :] = (acc_ref[rows, :]
                                                           + pre_ref[t, piece(q), :].astype(F32)).astype(BF16)
                            chip_copy(t, q).start()

                @pl.when(i == n_steps - 1)
                def _():
                    own_ref[...] = acc_ref[mine, :]

        @pl.when((i == n_steps - 1) & (kk == nk - 1))
        def _():
            for q in range(RS_PIECES):
                for t in range(len(chip_flips)):
                    sib_copy(t, q).wait_send()
                    chip_copy(t, q).wait_send()
                    chip_copy(t, q).wait_recv()
                sib_copy(SIB, q).wait_send()
                sib_copy(SIB, q).wait_recv()
            for r, (dx, dy, dc) in enumerate(FLIPS):
                small_copy(r, 4 * (x ^ dx) + 2 * (y ^ dy) + (c ^ dc)).wait_recv()
                small_copy(r, my_dev).wait_send()
            keep_small.wait()

    shard = (SHARD_PAD, D_MODEL)
    return pl.pallas_call(
        body, name="w_in_grad_rs",
        grid_spec=pltpu.PrefetchScalarGridSpec(
            num_scalar_prefetch=1, grid=(n_steps, nk),
            in_specs=[pl.BlockSpec((tk, D_MODEL), lambda i, kk, order: (kk, 0)),
                      pl.BlockSpec((None, tk, 2 * SHARD_PAD), lambda i, kk, order: (order[i], kk, 0)),
                      pl.BlockSpec(memory_space=pl.ANY)],
            out_specs=[pl.BlockSpec(shard, lambda i, kk, order: (0, 0)),
                       pl.BlockSpec(memory_space=pl.ANY), pl.BlockSpec(memory_space=pl.ANY)],
            scratch_shapes=[pltpu.VMEM((2 * SHARD_PAD, D_MODEL), F32),
                            pltpu.VMEM((SIB, *shard), BF16), pltpu.VMEM((SIB + 1, *shard), BF16),
                            pltpu.VMEM((SIB, *shard), BF16),
                            pltpu.SemaphoreType.DMA((SIB + 1, RS_PIECES)), pltpu.SemaphoreType.DMA((SIB + 1, RS_PIECES)),
                            pltpu.SemaphoreType.DMA((SIB, RS_PIECES)), pltpu.SemaphoreType.DMA((SIB, RS_PIECES)),
                            pltpu.SemaphoreType.DMA((7,)), pltpu.SemaphoreType.DMA((7,)), pltpu.SemaphoreType.DMA]),
        out_shape=[jax.ShapeDtypeStruct(shard, F32),
                   jax.ShapeDtypeStruct((SIB + 1, *shard), BF16),
                   jax.ShapeDtypeStruct((N_DEV, *small.shape), F32)],
        compiler_params=_params("arbitrary", "arbitrary"),
    )(chip_order, h, dsh, small)


def _adam_math(w, g, m, v):
    m_new = ADAM_B1 * m + (1.0 - ADAM_B1) * g
    v_new = ADAM_B2 * v + (1.0 - ADAM_B2) * (g * g)
    m_hat = m_new / (1.0 - ADAM_B1 ** ADAM_STEP)
    v_hat = v_new / (1.0 - ADAM_B2 ** ADAM_STEP)
    delta = -ADAM_LR * (m_hat / (jnp.sqrt(v_hat) + ADAM_EPS) + ADAM_WD * w)
    return delta, m_new, v_new


def _adam_big(jobs):
    steps = 8
    n = len(jobs)
    idx = jnp.stack([job[1] for job in jobs]).astype(jnp.int32)
    blocks = []
    for own, _, recv, w, m, v in jobs:
        (rw, cw), rp = w.shape, own.shape[1]
        by_cols = rp != rw
        blk_w = (rw, cw // steps) if by_cols else (rw // steps, cw)
        blk_g = (rp, cw // steps) if by_cols else (rw // steps, cw)
        blocks.append((blk_w, blk_g, by_cols))

    def body(idx_ref, *refs):
        ins, outs = refs[:5 * n], refs[5 * n:]
        for j, (blk_w, _, _) in enumerate(blocks):
            o_ref, r_ref, w_ref, m_ref, v_ref = ins[5 * j:5 * j + 5]
            g_ref, d_ref, mo_ref, vo_ref = outs[4 * j:4 * j + 4]
            g = o_ref[...].astype(F32)
            for r in range(r_ref.shape[0]):
                g = g + r_ref[r].astype(F32)
            g = g[0:blk_w[0], :]
            g_ref[...] = g
            d_ref[...], mo_ref[...], vo_ref[...] = _adam_math(w_ref[...], g, m_ref[...], v_ref[...])

    in_specs, out_specs, out_shape, args = [], [], [], []
    for j, ((own, _, recv, w, m, v), (blk_w, blk_g, by_cols)) in enumerate(zip(jobs, blocks)):
        at = (lambda i: (0, i)) if by_cols else (lambda i: (i, 0))
        spec = pl.BlockSpec(blk_w, lambda i, idx_ref, at=at: at(i))
        in_specs += [pl.BlockSpec((None, *blk_g), lambda i, idx_ref, at=at, j=j: (idx_ref[j], *at(i))),
                     pl.BlockSpec((recv.shape[0], *blk_g), lambda i, idx_ref, at=at: (0, *at(i))), spec, spec, spec]
        out_specs += [spec] * 4
        out_shape += [jax.ShapeDtypeStruct(w.shape, F32)] * 4
        args += [own, recv, w, m, v]
    res = pl.pallas_call(
        body, name="adam_big",
        grid_spec=pltpu.PrefetchScalarGridSpec(num_scalar_prefetch=1, grid=(steps,), in_specs=in_specs, out_specs=out_specs),
        out_shape=out_shape,
        compiler_params=_params("parallel"),
    )(idx, *args)
    return [res[4 * j:4 * j + 4] for j in range(n)]


def _adam_small(small_all, params):
    flat = [a for triple in params for a in triple]
    n_par = len(params)

    def body(s_ref, *refs):
        ins, outs, loss_ref = refs[:3 * n_par], refs[3 * n_par:-1], refs[-1]
        g_slab = s_ref[0]
        for dev in range(1, N_DEV):
            g_slab = g_slab + s_ref[dev]
        loss_ref[...] = g_slab[SMALL_LOSS:SMALL_LOSS + 1, :]
        dev = 4 * lax.axis_index("x") + 2 * lax.axis_index("y") + lax.axis_index("c")
        alpha_full = jnp.concatenate([g_slab[SMALL_W_ALPHA + half * B_GATE_RANK:SMALL_W_ALPHA + (half + 1) * B_GATE_RANK]
                                      for half in range(B_KEY_WIDTH // LANES)], axis=1)
        alpha_mine = pltpu.roll(alpha_full, (B_KEY_WIDTH - dev * SHARD_ALPHA) % B_KEY_WIDTH, 1)[:, 0:SHARD_ALPHA]
        grads = [_take_rows(g_slab, SMALL_G_IN, D_MODEL // LANES), _take_rows(g_slab, SMALL_G_FINAL, D_MODEL // LANES),
                 _take_rows(g_slab, SMALL_G_GLA, B_WIDTH // LANES), _take_rows(g_slab, SMALL_B_ALPHA, B_KEY_WIDTH // LANES),
                 g_slab[SMALL_SINKS:SMALL_SINKS + 1, 0:A_HEADS], alpha_mine]
        for i, g in enumerate(grads):
            w_ref, m_ref, v_ref = ins[3 * i:3 * i + 3]
            delta, m_new, v_new = _adam_math(w_ref[...], g, m_ref[...], v_ref[...])
            outs[4 * i][...] = g
            outs[4 * i + 1][...] = delta
            outs[4 * i + 2][...] = m_new
            outs[4 * i + 3][...] = v_new

    res = pl.pallas_call(
        body, name="adam_small",
        out_shape=[jax.ShapeDtypeStruct(t[0].shape, F32) for t in params for _ in range(4)]
                  + [jax.ShapeDtypeStruct((1, LANES), F32)],
    )(small_all, *flat)
    return [res[4 * i:4 * i + 4] for i in range(n_par)], res[-1]


def _local_step(x, cosf, sinf, loss_target, g_in, wt_sh, wa_pad, b_alpha, sinks, g_gla, out_shards, g_final, chip_order):
    B, S, _ = x.shape
    T = B * S
    x2 = x.reshape(T, D_MODEL)
    tgt2 = loss_target.reshape(T, D_MODEL)
    f, (g_woa, g_wob, g_wo) = _in_proj(x2, cosf, sinf, g_in, wt_sh, wa_pad, b_alpha, out_shards)
    w_o = g_wo.reshape(D_MODEL, D_MODEL)
    sink_row = jnp.repeat(sinks, WINDOW).reshape(1, ATT_ROWS)
    sink_col = sink_row.reshape(ATT_ROWS, 1)
    attn, lse = _attn_fwd(f["qkv"], sink_row, B, S)
    o_gla, st_all = _gla_fwd(f["q"], f["k"], f["cum"], f["vb"], B, S)
    (dxres, dattn, dog, dza, dzb, dga, dgb, dw_o, dw_oa, dw_ob, small_a) = _merge(
        x2, tgt2, attn, f["za"], o_gla, f["zb"], f["ga"], f["gb"], g_woa, g_wob, w_o, g_gla, g_final)
    dq, dkv, dsink = _attn_bwd(f["qkv"], dattn, attn, lse, sink_col, B, S)
    (dqb, dkb, dvb, dla), (rv_o, rv_oa, rv_ob) = _gla_bwd(f["q"], f["k"], f["cum"], f["vb"], dog, st_all, B, S,
                                                        [dw_o, dw_oa, dw_ob])
    parts = dict(dq=dq, dkv=dkv, dza=dza, dqb=dqb, dkb=dkb, dvb=dvb, dzb=dzb, dla=dla, u=f["u"], alr=f["alr"],
                 dga=dga, dgb=dgb)
    dx, dsh, small_c = _in_proj_bwd(x2, dxres, cosf, sinf, g_in, f["wt_pad"], wa_pad, parts)
    small = jnp.concatenate([small_a, dsink, small_c], axis=0)
    own_in, rv_in, small_all = _w_in_grad_rs(f["h"], dsh, chip_order, small)
    return dict(grad_x=dx.reshape(B, S, D_MODEL), own_in=own_in, rv_in=rv_in,
                own_o=dw_o, rv_o=rv_o, own_oa=dw_oa, rv_oa=rv_oa, own_ob=dw_ob, rv_ob=rv_ob, small_all=small_all)


def kernel(x, positions, g_in, w_in, w_alpha_up, b_alpha, attn_sinks, g_gla_norm, w_out_a, w_out_b, w_o, g_final, loss_target, m_g_in, m_w_in, m_w_alpha_up, m_b_alpha, m_attn_sinks, m_g_gla_norm, m_w_out_a, m_w_out_b, m_w_o, m_g_final, v_g_in, v_w_in, v_w_alpha_up, v_b_alpha, v_attn_sinks, v_g_gla_norm, v_w_out_a, v_w_out_b, v_w_o, v_g_final):
    xi, yi, ci = _my_place()
    chip = 2 * xi + yi
    chip_order = jnp.stack([chip ^ 3, chip ^ 2, chip ^ 1, chip]).astype(jnp.int32)

    (g_win, g_wa), cosf, sinf = _gather_first(
        [jnp.pad(w_in[0].T.astype(BF16), ((0, SHARD_PAD - SHARD_IN), (0, 0))), w_alpha_up[0].astype(BF16)],
        positions.reshape(-1, 1))
    wt_sh = g_win.reshape(N_DEV * SHARD_PAD, D_MODEL)
    wa_pad = jnp.pad(jnp.concatenate([g_wa[j] for j in range(N_DEV)], axis=1), ((0, RANK_PAD - B_GATE_RANK), (0, 0)))

    r = _local_step(x, cosf, sinf, loss_target, g_in, wt_sh, wa_pad, b_alpha, attn_sinks[0], g_gla_norm,
                    [w_out_a[0].astype(BF16), w_out_b[0].astype(BF16), w_o[0].astype(BF16)],
                    g_final.reshape(1, D_MODEL), chip_order)

    dev = 4 * xi + 2 * yi + ci
    big = _adam_big([(r["own_in"][None], jnp.int32(0), r["rv_in"], w_in[0].T, m_w_in[0].T, v_w_in[0].T),
                     (r["own_oa"], dev, r["rv_oa"], w_out_a[0], m_w_out_a[0], v_w_out_a[0]),
                     (r["own_ob"], dev, r["rv_ob"], w_out_b[0], m_w_out_b[0], v_w_out_b[0]),
                     (r["own_o"], dev, r["rv_o"], w_o[0], m_w_o[0], v_w_o[0])])
    big[0] = [a.T for a in big[0]]
    row = lambda a: a.reshape(1, D_MODEL)
    (s_in, s_final, s_gla, s_ba, s_sinks, s_wa), loss_row = _adam_small(r["small_all"], [
        (g_in, m_g_in, v_g_in), (row(g_final), row(m_g_final), row(v_g_final)),
        (g_gla_norm, m_g_gla_norm, v_g_gla_norm), (b_alpha, m_b_alpha, v_b_alpha),
        (attn_sinks, m_attn_sinks, v_attn_sinks), (w_alpha_up[0], m_w_alpha_up[0], v_w_alpha_up[0])])

    def group(i):
        return (s_in[i], big[0][i][None], s_wa[i][None], s_ba[i], s_sinks[i], s_gla[i], big[1][i][None], big[2][i][None],
                big[3][i][None], s_final[i].reshape(D_MODEL))

    return (loss_row[0, 0], r["grad_x"], *group(0), *group(1), *group(2), *group(3))
```

```python
import functools
import math

import numpy as np
import jax
import jax.numpy as jnp
from jax import lax
from jax.experimental import pallas as pl
from jax.experimental.pallas import tpu as pltpu

F32 = jnp.float32
BF16 = jnp.bfloat16
MESH = pl.DeviceIdType.MESH

D_MODEL = 1024
A_HEADS, A_KV_HEADS, A_HEAD_DIM = 8, 2, 64
A_WIDTH, A_KV_WIDTH = 512, 128
WINDOW = 128
ROPE_THETA = 500000.0
ROPE_DIM = 16
B_HEADS, B_KEY_DIM, B_VAL_DIM = 4, 64, 128
B_KEY_WIDTH, B_WIDTH = 256, 512
B_GATE_RANK = 16
B_GATE_TEMP = 16.0
B_CHUNK = 64
NORM_EPS = 1e-6
NEG_BIG = -1e30
D_IN = 4880
N_DEV = 8
N_CHIPS = 4
ADAM_LR, ADAM_B1, ADAM_B2, ADAM_EPS, ADAM_WD, ADAM_STEP = 0.001, 0.9, 0.999, 1e-08, 0.01, 10

LANES = 128
V7X_VMEM_LIMIT = 56 * 1024 * 1024
V7X_VMEM_LIMIT_MAX = 62 * 1024 * 1024

RANK_PAD = LANES
SEG = {}
_off = 0
for _name, _w in (("qa", 512), ("ka", 128), ("va", 128), ("za", 512), ("qb", 256), ("kb", 256),
                  ("vb", 512), ("zb", 512), ("alr", RANK_PAD), ("ga", 1024), ("gb", 1024)):
    SEG[_name] = (_off, _off + _w)
    _off += _w
D_IN_PAD = _off
ALR_SRC = SEG["alr"][0]
QKV_K, QKV_V, QKV_W = SEG["ka"][0], SEG["va"][0], SEG["va"][1]
ATT_SCALE = A_HEAD_DIM ** -0.5

SHARD_IN = D_IN // N_DEV
SHARD_PAD = 640
SHARD_OUT = D_MODEL // N_DEV
SHARD_ALPHA = B_KEY_WIDTH // N_DEV

SMALL_G_FINAL, SMALL_G_GLA, SMALL_LOSS, SMALL_SINKS, SMALL_G_IN, SMALL_B_ALPHA, SMALL_W_ALPHA = 0, 8, 12, 16, 24, 32, 40
SMALL_ROWS = 72


def _dot(a, b):
    return jnp.dot(a, b, preferred_element_type=F32)


def _dot_nt(a, b):
    return lax.dot_general(a, b, (((1,), (1,)), ((), ())), preferred_element_type=F32)


def _dot_tn(a, b):
    return lax.dot_general(a, b, (((0,), (0,)), ((), ())), preferred_element_type=F32)


def _sigmoid(z):
    return 1.0 / (1.0 + jnp.exp(-z))


def _sigmoid_tanh(z):
    return 0.5 * jnp.tanh(0.5 * z) + 0.5


def _params(*sem):
    return pltpu.CompilerParams(dimension_semantics=sem, vmem_limit_bytes=V7X_VMEM_LIMIT)


def _const_spec(shape):
    nd = len(shape)
    return pl.BlockSpec(shape, lambda *_: (0,) * nd, pipeline_mode=pl.Buffered(1))


def _lane_iota(shape):
    return lax.broadcasted_iota(jnp.int32, shape, 1)


def _row_iota(shape):
    return lax.broadcasted_iota(jnp.int32, shape, 0)


def _split3(v):
    hi = v.astype(BF16)
    r1 = v - hi.astype(F32)
    mid = r1.astype(BF16)
    lo = (r1 - mid.astype(F32)).astype(BF16)
    return hi, mid, lo


def _put_rows(ref, row0, vec):
    for r in range(vec.shape[1] // LANES):
        ref[row0 + r:row0 + r + 1, :] = vec[:, r * LANES:(r + 1) * LANES]


def _take_rows(slab, row0, n):
    return jnp.concatenate([slab[row0 + r:row0 + r + 1, :] for r in range(n)], axis=1)


def _rope_lane_constants():
    half = ROPE_DIM // 2
    inv_freq = np.exp(-math.log(ROPE_THETA) * np.arange(half, dtype=np.float32) * np.float32(2.0 / ROPE_DIM)).astype(np.float32)
    lane = np.arange(LANES)
    j = lane % A_HEAD_DIM
    invf = np.where(j < ROPE_DIM, inv_freq[j % half], 0.0).astype(np.float32)
    sign = np.where(j < half, -1.0, np.where(j < ROPE_DIM, 1.0, 0.0)).astype(np.float32)
    return jnp.asarray(invf)[None, :], jnp.asarray(sign)[None, :]


def _rope_slab(t, cos, sin_signed):
    first = (_lane_iota(t.shape) % A_HEAD_DIM) < (ROPE_DIM // 2)
    partner = jnp.where(first, pltpu.roll(t, LANES - ROPE_DIM // 2, 1), pltpu.roll(t, ROPE_DIM // 2, 1))
    return t * cos + partner * sin_signed


def _shard_pad_cols(j):
    cut = ALR_SRC + B_GATE_RANK
    shift = RANK_PAD - B_GATE_RANK
    a, b = j * SHARD_IN, (j + 1) * SHARD_IN
    if b <= cut:
        return [(a, b)]
    if a >= cut:
        return [(a + shift, b + shift)]
    return [(a, cut), (cut + shift, b + shift)]


def _in_proj(x2, cosf, sinf, g_in, wt_sh, wa_pad, b_alpha, later_shards):
    T = x2.shape[0]
    tm = math.gcd(T, 512)
    sub = math.gcd(tm, 256)
    last = T // tm - 1
    nl = len(later_shards)

    def body(x_ref, cos_ref, sin_ref, g_ref, wsh_ref, wa_ref, ba_ref, *rest):
        sh_refs, rest = rest[:nl], rest[nl:]
        (h_ref, qkv_ref, za_ref, q_ref, k_ref, vb_ref, zb_ref, alr_ref, u_ref, cum_ref, ga_ref, gb_ref, wt_out) = rest[:13]
        all_refs, (wt_ref, send_sems, recv_sems, local_sems, wt_sem) = rest[13:13 + nl], rest[13 + nl:]
        wt_copy = pltpu.make_async_copy(wt_ref, wt_out, wt_sem)
        px, py, pc = _my_place()
        my_dev = 4 * px + 2 * py + pc

        def wcopy(a, r, slot):
            dx, dy, dc = FLIPS[r]
            return pltpu.make_async_remote_copy(
                src_ref=sh_refs[a], dst_ref=all_refs[a].at[slot], send_sem=send_sems.at[a, r],
                recv_sem=recv_sems.at[a, r], device_id=(px ^ dx, py ^ dy, pc ^ dc), device_id_type=MESH)

        keep = [pltpu.make_async_copy(sh_refs[a], all_refs[a].at[my_dev], local_sems.at[a]) for a in range(nl)]

        @pl.when(pl.program_id(0) == 0)
        def _():
            for a in range(nl):
                keep[a].start()
                for r in range(len(FLIPS)):
                    wcopy(a, r, my_dev).start()

        @pl.when(pl.program_id(0) == 0)
        def _():
            for j in range(N_DEV):
                src = j * SHARD_PAD
                for a, b in _shard_pad_cols(j):
                    wt_ref[a:b, :] = wsh_ref[src:src + b - a, :]
                    src += b - a
            a, b = SEG["alr"]
            wt_ref[a + B_GATE_RANK:b, :] = jnp.zeros((RANK_PAD - B_GATE_RANK, D_MODEL), BF16)
            wt_copy.start()

        def one_tile(rows):
            x = x_ref[rows, :]
            r = lax.rsqrt(jnp.mean(x * x, axis=-1, keepdims=True) + NORM_EPS)
            h = (x * r * g_ref[...]).astype(BF16)
            h_ref[rows, :] = h

            def seg(name):
                a, b = SEG[name]
                return _dot_nt(h, wt_ref[a:b, :])

            alr = seg("alr").astype(BF16)
            alr_ref[rows, :] = alr
            u = _dot(alr, wa_ref[...]) + ba_ref[...]
            u_ref[rows, :] = u
            log_a = (jnp.minimum(u, 0.0) - jnp.log(1.0 + jnp.exp(-jnp.abs(u)))) * (1.0 / B_GATE_TEMP)
            row, col = _row_iota((sub, sub)), _lane_iota((sub, sub))
            tri = ((row // B_CHUNK == col // B_CHUNK) & (col <= row)).astype(BF16)
            hi, mid, lo = _split3(log_a)
            cum_ref[rows, :] = _dot(tri, hi) + _dot(tri, mid) + _dot(tri, lo)

            cos, sin = cos_ref[rows, :], sin_ref[rows, :]
            qa = seg("qa") * ATT_SCALE
            for s in range(A_WIDTH // LANES):
                qkv_ref[rows, s * LANES:(s + 1) * LANES] = _rope_slab(qa[:, s * LANES:(s + 1) * LANES], cos, sin).astype(BF16)
            qkv_ref[rows, QKV_K:QKV_V] = _rope_slab(seg("ka"), cos, sin).astype(BF16)
            qkv_ref[rows, QKV_V:QKV_W] = seg("va").astype(BF16)
            za_ref[rows, :] = seg("za").astype(BF16)
            q_ref[rows, :] = seg("qb")
            k_ref[rows, :] = seg("kb")
            vb_ref[rows, :] = seg("vb").astype(BF16)
            zb_ref[rows, :] = seg("zb").astype(BF16)
            ga_ref[rows, :] = seg("ga").astype(BF16)
            gb_ref[rows, :] = seg("gb").astype(BF16)

        for j in range(tm // sub):
            one_tile(pl.ds(j * sub, sub))

        @pl.when(pl.program_id(0) == last)
        def _():
            for a in range(nl):
                for r, (dx, dy, dc) in enumerate(FLIPS):
                    wcopy(a, r, 4 * (px ^ dx) + 2 * (py ^ dy) + (pc ^ dc)).wait_recv()
                    wcopy(a, r, my_dev).wait_send()
                keep[a].wait()
            wt_copy.wait()

    def rows(w):
        return pl.BlockSpec((tm, w), lambda i: (i, 0))

    outs = [("h", D_MODEL, BF16), ("qkv", QKV_W, BF16), ("za", A_WIDTH, BF16), ("q", B_KEY_WIDTH, F32),
            ("k", B_KEY_WIDTH, F32), ("vb", B_WIDTH, BF16), ("zb", B_WIDTH, BF16), ("alr", RANK_PAD, BF16),
            ("u", B_KEY_WIDTH, F32), ("cum", B_KEY_WIDTH, F32), ("ga", D_MODEL, BF16), ("gb", D_MODEL, BF16)]
    res = pl.pallas_call(
        body, name="in_proj", grid=(T // tm,),
        in_specs=[rows(D_MODEL), rows(LANES), rows(LANES), _const_spec((1, D_MODEL)),
                  _const_spec((N_DEV * SHARD_PAD, D_MODEL)), _const_spec((RANK_PAD, B_KEY_WIDTH)),
                  _const_spec((1, B_KEY_WIDTH))] + _any_specs(nl),
        out_specs=[rows(w) for _, w, _ in outs] + _any_specs(1 + nl),
        out_shape=[jax.ShapeDtypeStruct((T, w), dt) for _, w, dt in outs]
                  + [jax.ShapeDtypeStruct((D_IN_PAD, D_MODEL), BF16)]
                  + [jax.ShapeDtypeStruct((N_DEV, *sh.shape), sh.dtype) for sh in later_shards],
        scratch_shapes=[pltpu.VMEM((D_IN_PAD, D_MODEL), BF16),
                        pltpu.SemaphoreType.DMA((nl, len(FLIPS))), pltpu.SemaphoreType.DMA((nl, len(FLIPS))),
                        pltpu.SemaphoreType.DMA((nl,)), pltpu.SemaphoreType.DMA],
        compiler_params=_params("arbitrary"),
    )(x2, cosf, sinf, g_in, wt_sh, wa_pad, b_alpha, *later_shards)
    n_out = len(outs) + 1
    return dict(zip([n for n, _, _ in outs] + ["wt_pad"], res[:n_out])), res[n_out:]


def _dup_kv_head(t, g):
    tf = t.astype(F32)
    keep = (_lane_iota(tf.shape) < A_HEAD_DIM) == (g == 0)
    return jnp.where(keep, tf, pltpu.roll(tf, A_HEAD_DIM, 1)).astype(BF16)


def _stack_heads(t):
    lo = _lane_iota(t.shape) < A_HEAD_DIM
    zero = jnp.zeros_like(t)
    return jnp.concatenate([jnp.where(lo, t, zero), jnp.where(lo, zero, t)], axis=0)


ATT_ROWS = A_HEADS * WINDOW
GROUP_ROWS = ATT_ROWS // A_KV_HEADS
HEADS_PER_GROUP = A_HEADS // A_KV_HEADS


def _band_mask_t(n):
    kj = _row_iota((2 * WINDOW, GROUP_ROWS)) - WINDOW
    qi = _lane_iota((2 * WINDOW, GROUP_ROWS)) % WINDOW
    return (kj <= qi) & (qi - kj < WINDOW) & ((n > 0) | (kj >= 0))


def _stacked_queries(ref, g):
    pairs = range(g * HEADS_PER_GROUP // 2, (g + 1) * HEADS_PER_GROUP // 2)
    return jnp.concatenate([_stack_heads(ref[:, p * LANES:(p + 1) * LANES]) for p in pairs], axis=0)


def _unstack_heads(t, g, ref, dtype):
    lo = _lane_iota((WINDOW, LANES)) < A_HEAD_DIM
    for hh in range(HEADS_PER_GROUP // 2):
        p = g * HEADS_PER_GROUP // 2 + hh
        ref[:, p * LANES:(p + 1) * LANES] = jnp.where(lo, t[2 * hh * WINDOW:(2 * hh + 1) * WINDOW],
                                                       t[(2 * hh + 1) * WINDOW:(2 * hh + 2) * WINDOW]).astype(dtype)


FWD_BLOCKS = 16


def _attn_fwd(qkv, sink_row, B, S):
    T = B * S
    nb = S // WINDOW
    blocks = math.gcd(nb, FWD_BLOCKS)
    steps = nb // blocks

    def one_block(has_prev, sink_ref, q, k, v, o_ref, lse_ref):
        valid = _band_mask_t(has_prev)
        lse_rows = []
        for g in range(A_KV_HEADS):
            kd, vd = _dup_kv_head(k, g), _dup_kv_head(v, g)
            s = jnp.where(valid, _dot_nt(kd, _stacked_queries(q, g)), NEG_BIG)
            sink = sink_ref[:, g * GROUP_ROWS:(g + 1) * GROUP_ROWS]
            m = jnp.maximum(jnp.max(s, axis=0, keepdims=True), sink)
            e = jnp.exp(s - m)
            den = jnp.sum(e, axis=0, keepdims=True) + jnp.exp(sink - m)
            o = _dot_tn((e * (1.0 / den)).astype(BF16), vd)
            _unstack_heads(o, g, o_ref, F32)
            lse = m + jnp.log(den)
            lse_rows += [lse[:, j * WINDOW:(j + 1) * WINDOW] for j in range(HEADS_PER_GROUP)]
        by_head = jnp.concatenate(lse_rows + [jnp.zeros((WINDOW - A_HEADS, WINDOW), F32)], axis=0)
        lse_ref[...] = by_head.T

    def body(sink_ref, q_ref, kc_ref, vc_ref, kp_ref, vp_ref, o_ref, lse_ref):
        k_all = jnp.concatenate([kp_ref[...], kc_ref[...]], axis=0)
        v_all = jnp.concatenate([vp_ref[...], vc_ref[...]], axis=0)
        for j in range(blocks):
            rows = pl.ds(j * WINDOW, WINDOW)
            keys = slice(j * WINDOW, (j + 2) * WINDOW)
            has_prev = pl.program_id(1) if j == 0 else 1
            one_block(has_prev, sink_ref, q_ref[rows, :], k_all[keys], v_all[keys], o_ref.at[rows], lse_ref.at[rows])

    def cur(col, w):
        return pl.BlockSpec((blocks * WINDOW, w), lambda b, n: (b * steps + n, col))

    def prev(col):
        return pl.BlockSpec((WINDOW, LANES), lambda b, n: (b * nb + jnp.maximum(blocks * n - 1, 0), col))

    kcol, vcol = QKV_K // LANES, QKV_V // LANES
    return pl.pallas_call(
        body, name="attn_fwd", grid=(B, steps),
        in_specs=[_const_spec((1, ATT_ROWS)), cur(0, A_WIDTH), cur(kcol, LANES), cur(vcol, LANES), prev(kcol), prev(vcol)],
        out_specs=[cur(0, A_WIDTH), cur(0, LANES)],
        out_shape=[jax.ShapeDtypeStruct((T, A_WIDTH), F32), jax.ShapeDtypeStruct((T, LANES), F32)],
        compiler_params=_params("parallel", "parallel"),
    )(sink_row, qkv, qkv, qkv, qkv, qkv)


ATT_CHUNK = 128


def _chunk_masks(n):
    masks = []
    for half in range(WINDOW // ATT_CHUNK):
        qi = _row_iota((ATT_CHUNK, 2 * WINDOW)) + half * ATT_CHUNK
        kj = _lane_iota((ATT_CHUNK, 2 * WINDOW)) - WINDOW
        masks.append((kj <= qi) & (qi - kj < WINDOW) & ((n > 0) | (kj >= 0)))
    return masks


def _all_stacked_queries(ref):
    return jnp.concatenate([_stacked_queries(ref, g) for g in range(A_KV_HEADS)], axis=0)


def _by_group(fn, lhs, rhs_per_group):
    return jnp.concatenate([fn(lhs[g * GROUP_ROWS:(g + 1) * GROUP_ROWS], rhs_per_group[g])
                            for g in range(A_KV_HEADS)], axis=0)


BWD_BLOCKS = 8


def _attn_bwd(qkv, do, out, lse, sink_col, B, S):
    T = B * S
    nb = S // WINDOW
    M = math.gcd(nb, BWD_BLOCKS)
    steps = nb // M
    n_chunks = ATT_ROWS // ATT_CHUNK
    halves = WINDOW // ATT_CHUNK

    def block(has_prev, sink_ref, q, do_b, out_b, lse_b, k, v, scratch, want_dq):
        s_ref, dp_ref, ds_ref, p_ref = scratch
        width = k.shape[0]
        masks = [mk[:, 0:width] for mk in _chunk_masks(has_prev)]
        kd = [_dup_kv_head(k, g) for g in range(A_KV_HEADS)]
        vd = [_dup_kv_head(v, g) for g in range(A_KV_HEADS)]
        qs, dos = _all_stacked_queries(q), _all_stacked_queries(do_b)
        s_ref[...] = _by_group(_dot_nt, qs, kd)
        dp_ref[...] = _by_group(_dot_nt, dos, vd)
        lane = _lane_iota((ATT_CHUNK, LANES))
        lo = lane < A_HEAD_DIM
        lane1 = _lane_iota((1, LANES))
        dsink_row = jnp.zeros((1, LANES), F32)
        for c in range(n_chunks):
            rows = slice(c * ATT_CHUNK, (c + 1) * ATT_CHUNK)
            head, half = divmod(c, halves)
            qrows = slice(half * ATT_CHUNK, (half + 1) * ATT_CHUNK)
            slab = slice((head // 2) * LANES, (head // 2 + 1) * LANES)
            lse_col = jnp.sum(jnp.where(lane == head, lse_b[qrows, :], 0.0), axis=-1, keepdims=True)
            prod = do_b[qrows, slab].astype(F32) * out_b[qrows, slab].astype(F32)
            mine = lo if head % 2 == 0 else jnp.logical_not(lo)
            delta = jnp.sum(jnp.where(mine, prod, 0.0), axis=-1, keepdims=True)
            prob = jnp.exp(jnp.where(masks[half], s_ref[rows, :], NEG_BIG) - lse_col)
            p_ref[rows, :] = prob.astype(BF16)
            ds_ref[rows, :] = (prob * (dp_ref[rows, :] - delta)).astype(BF16)
            w = -jnp.exp(sink_ref[rows, :] - lse_col) * delta
            dsink_row += jnp.where(lane1 == head, jnp.sum(w, axis=0, keepdims=True), 0.0)
        dq = _by_group(_dot, ds_ref[...], kd) * ATT_SCALE if want_dq else None
        groups = [slice(g * GROUP_ROWS, (g + 1) * GROUP_ROWS) for g in range(A_KV_HEADS)]
        dk = [_dot_tn(ds_ref[rows, :], qs[rows]) for rows in groups]
        dv = [_dot_tn(p_ref[rows, :], dos[rows]) for rows in groups]
        return dq, dk, dv, dsink_row

    def fold(per_group):
        lane = _lane_iota((WINDOW, LANES))
        out = jnp.zeros((WINDOW, LANES), F32)
        for g, acc in enumerate(per_group):
            out = jnp.where((lane < A_HEAD_DIM) == (g == 0), acc + pltpu.roll(acc, A_HEAD_DIM, 1), out)
        return out

    def body(sink_ref, q_ref, qn_ref, do_ref, don_ref, out_ref, outn_ref, lse_ref, lsen_ref, kc_ref, kp_ref, vc_ref, vp_ref,
             dq_ref, dkv_ref, dsink_ref, s_scr, dp_scr, ds_scr, p_scr, s_x, dp_x, ds_x, p_x):
        b, m = pl.program_id(0), pl.program_id(1)

        @pl.when((b == 0) & (m == 0))
        def _():
            dsink_ref[...] = jnp.zeros_like(dsink_ref)

        k_all = jnp.concatenate([kp_ref[...], kc_ref[...]], axis=0)
        v_all = jnp.concatenate([vp_ref[...], vc_ref[...]], axis=0)
        results = []
        for j in range(M):
            rows = slice(j * WINDOW, (j + 1) * WINDOW)
            keys = slice(j * WINDOW, (j + 2) * WINDOW)
            has_prev = m if j == 0 else 1
            results.append(block(has_prev, sink_ref, q_ref[rows, :], do_ref[rows, :], out_ref[rows, :], lse_ref[rows, :],
                                 k_all[keys], v_all[keys], (s_scr.at[j], dp_scr.at[j], ds_scr.at[j], p_scr.at[j]), True))
        last_keys = slice(M * WINDOW, (M + 1) * WINDOW)
        _, dk_x, dv_x, _ = block(1, sink_ref, qn_ref[...], don_ref[...], outn_ref[...], lsen_ref[...],
                                 k_all[last_keys], v_all[last_keys], (s_x, dp_x, ds_x, p_x), False)
        has_next = m < steps - 1
        lo_q = _lane_iota((WINDOW, LANES)) < A_HEAD_DIM
        dsink_row = jnp.zeros((1, LANES), F32)
        for j, (dq, dk, dv, ds_row) in enumerate(results):
            rows = slice(j * WINDOW, (j + 1) * WINDOW)
            for p in range(A_HEADS // 2):
                dq_ref[rows, p * LANES:(p + 1) * LANES] = jnp.where(
                    lo_q, dq[2 * p * WINDOW:(2 * p + 1) * WINDOW], dq[(2 * p + 1) * WINDOW:(2 * p + 2) * WINDOW]).astype(BF16)
            if j + 1 < M:
                dk_next = [t[0:WINDOW] for t in results[j + 1][1]]
                dv_next = [t[0:WINDOW] for t in results[j + 1][2]]
            else:
                dk_next = [jnp.where(has_next, t, 0.0) for t in dk_x]
                dv_next = [jnp.where(has_next, t, 0.0) for t in dv_x]
            dkv_ref[rows, 0:LANES] = fold([own[WINDOW:] + nxt for own, nxt in zip(dk, dk_next)]).astype(BF16)
            dkv_ref[rows, LANES:] = fold([own[WINDOW:] + nxt for own, nxt in zip(dv, dv_next)]).astype(BF16)
            dsink_row += ds_row
        dsink_ref[0:1, :] += dsink_row

    def cur(col, w):
        return pl.BlockSpec((M * WINDOW, w), lambda b, m: (b * steps + m, col))

    def nxt(col, w):
        return pl.BlockSpec((WINDOW, w), lambda b, m: (b * nb + jnp.minimum(M * (m + 1), nb - 1), col))

    def prev(col):
        return pl.BlockSpec((WINDOW, LANES), lambda b, m: (b * nb + jnp.maximum(M * m - 1, 0), col))

    kcol, vcol = QKV_K // LANES, QKV_V // LANES
    scores = (M, ATT_ROWS, 2 * WINDOW)
    extra = (ATT_ROWS, WINDOW)
    return pl.pallas_call(
        body, name="attn_bwd", grid=(B, steps),
        in_specs=[_const_spec((ATT_ROWS, 1)), cur(0, A_WIDTH), nxt(0, A_WIDTH), cur(0, A_WIDTH), nxt(0, A_WIDTH),
                  cur(0, A_WIDTH), nxt(0, A_WIDTH), cur(0, LANES), nxt(0, LANES),
                  cur(kcol, LANES), prev(kcol), cur(vcol, LANES), prev(vcol)],
        out_specs=[cur(0, A_WIDTH), cur(0, 2 * LANES), pl.BlockSpec((8, LANES), lambda b, m: (0, 0))],
        out_shape=[jax.ShapeDtypeStruct((T, A_WIDTH), BF16), jax.ShapeDtypeStruct((T, 2 * LANES), BF16),
                   jax.ShapeDtypeStruct((8, LANES), F32)],
        scratch_shapes=[pltpu.VMEM(scores, F32), pltpu.VMEM(scores, F32), pltpu.VMEM(scores, BF16), pltpu.VMEM(scores, BF16),
                        pltpu.VMEM(extra, F32), pltpu.VMEM(extra, F32), pltpu.VMEM(extra, BF16), pltpu.VMEM(extra, BF16)],
        compiler_params=_params("arbitrary", "arbitrary"),
    )(sink_col, qkv, qkv, do, do, out, out, lse, lse, qkv, qkv, qkv, qkv)


GLA_FWD_TILING = (64, 16)
GLA_BWD_TILING = (256, 4)


def _gla_factors(q_ref, k_ref, cum_ref):
    cpt = q_ref.shape[0] // B_CHUNK
    scale = B_KEY_DIM ** -0.5
    cum = cum_ref[...]
    shape = (B_CHUNK, B_KEY_WIDTH)
    last = jnp.concatenate([jnp.broadcast_to(cum_ref[pl.ds(c * B_CHUNK + B_CHUNK - 1, 1), :], shape)
                            for c in range(cpt)], axis=0)
    mid = jnp.concatenate([jnp.broadcast_to(cum_ref[pl.ds(c * B_CHUNK + B_CHUNK // 2 - 1, 1), :], shape)
                           for c in range(cpt)], axis=0)
    e_qm, e_km, e_qe, e_kd = jnp.exp(cum - mid), jnp.exp(mid - cum), jnp.exp(cum), jnp.exp(last - cum)
    qs = q_ref[...] * scale
    k = k_ref[...]
    return qs, k, (e_qm, e_km, e_qe, e_kd)


def _head_mask(shape, h):
    return (_lane_iota(shape) // B_KEY_DIM) == h


def _stack_masked(t):
    return jnp.concatenate([jnp.where(_head_mask(t.shape, h), t, 0.0) for h in range(B_HEADS)], axis=0).astype(BF16)


def _select_heads(t):
    shape = (B_CHUNK, B_KEY_WIDTH)
    out = jnp.zeros(shape, F32)
    for h in range(B_HEADS):
        out = jnp.where(_head_mask(shape, h), t[h * B_CHUNK:(h + 1) * B_CHUNK], out)
    return out


def _select_state(t):
    shape = (B_VAL_DIM, B_KEY_WIDTH)
    out = jnp.zeros(shape, F32)
    for h in range(B_HEADS):
        out = jnp.where(_head_mask(shape, h), t[h * B_VAL_DIM:(h + 1) * B_VAL_DIM], out)
    return out


def _rows_by_head(t):
    return jnp.concatenate([t[:, h * B_VAL_DIM:(h + 1) * B_VAL_DIM] for h in range(B_HEADS)], axis=0)


def _intra_mask(tile_rows):
    i, j = _row_iota((tile_rows, tile_rows)), _lane_iota((tile_rows, tile_rows))
    return (i // B_CHUNK == j // B_CHUNK) & (j <= i)


def _pair_stack(t, p):
    slab = t[:, p * LANES:(p + 1) * LANES]
    lo = _lane_iota(slab.shape) < B_KEY_DIM
    return jnp.concatenate([jnp.where(lo, slab, 0.0), jnp.where(lo, 0.0, slab)], axis=0).astype(BF16)


def _gla_fwd(q, k, cum, vb, B, S):
    T = B * S
    tile_rows = math.gcd(S, GLA_FWD_TILING[0])
    cpt = tile_rows // B_CHUNK
    nt = S // tile_rows
    tps = math.gcd(nt, GLA_FWD_TILING[1])

    def one_sequence(q_ref, k_ref, cum_ref, v_ref, o_ref, st_all_ref, st_ref):
        qs, kk, (e_qm, e_km, e_qe, e_kd) = _gla_factors(q_ref, k_ref, cum_ref)
        qm, km, qe, kd = qs * e_qm, kk * e_km, qs * e_qe, (kk * e_kd).astype(BF16)
        mask = _intra_mask(tile_rows)
        intra = []
        for p in range(B_HEADS // 2):
            a = _dot_nt(_pair_stack(qm, p), km[:, p * LANES:(p + 1) * LANES].astype(BF16))
            for hh in range(2):
                h = 2 * p + hh
                att = jnp.where(mask, a[hh * tile_rows:(hh + 1) * tile_rows], 0.0).astype(BF16)
                intra.append(_dot(att, v_ref[:, h * B_VAL_DIM:(h + 1) * B_VAL_DIM]))
        inter = []
        for c in range(cpt):
            rows = slice(c * B_CHUNK, (c + 1) * B_CHUNK)
            st = st_ref[...]
            st_all_ref[c] = st
            inter.append(_dot_nt(_stack_masked(qe[rows]), st.astype(BF16)))
            inc = _select_state(_dot_tn(v_ref[rows, :], kd[rows]))
            decay = jnp.exp(cum_ref[pl.ds(c * B_CHUNK + B_CHUNK - 1, 1), :])
            st_ref[...] = st * decay + inc
        for h in range(B_HEADS):
            oi = jnp.concatenate([inter[c][h * B_CHUNK:(h + 1) * B_CHUNK] for c in range(cpt)], axis=0)
            o_ref[:, h * B_VAL_DIM:(h + 1) * B_VAL_DIM] = (intra[h] + oi).astype(BF16)

    def body(q_ref, k_ref, cum_ref, v_ref, o_ref, st_all_ref, st_ref):
        @pl.when(pl.program_id(0) == 0)
        def _():
            st_ref[...] = jnp.zeros_like(st_ref)

        for b in range(B):
            for tile in range(tps):
                tok = pl.ds(tile * tile_rows, tile_rows)
                chunks = pl.ds(tile * cpt, cpt)
                one_sequence(*[r.at[b, tok] for r in (q_ref, k_ref, cum_ref, v_ref, o_ref)],
                             st_all_ref.at[b, chunks], st_ref.at[b])

    def rows(w):
        return pl.BlockSpec((B, tps * tile_rows, w), lambda t: (0, t, 0))

    seq = lambda a: a.reshape(B, S, a.shape[-1])
    o, st_all = pl.pallas_call(
        body, name="gla_fwd", grid=(nt // tps,),
        in_specs=[rows(B_KEY_WIDTH), rows(B_KEY_WIDTH), rows(B_KEY_WIDTH), rows(B_WIDTH)],
        out_specs=[rows(B_WIDTH),
                   pl.BlockSpec((B, tps * cpt, B_VAL_DIM, B_KEY_WIDTH), lambda t: (0, t, 0, 0))],
        out_shape=[jax.ShapeDtypeStruct((B, S, B_WIDTH), BF16),
                   jax.ShapeDtypeStruct((B, S // B_CHUNK, B_VAL_DIM, B_KEY_WIDTH), F32)],
        scratch_shapes=[pltpu.VMEM((B, B_VAL_DIM, B_KEY_WIDTH), F32)],
        compiler_params=_params("arbitrary"),
    )(seq(q), seq(k), seq(cum), seq(vb))
    return o.reshape(T, B_WIDTH), st_all.reshape(T // B_CHUNK, B_VAL_DIM, B_KEY_WIDTH)


def _gla_bwd(q, k, cum, vb, do, st_all, B, S, wgrads):
    T = B * S
    tile_rows = math.gcd(S, GLA_BWD_TILING[0])
    cpt = tile_rows // B_CHUNK
    nt = S // tile_rows
    tps = math.gcd(nt, GLA_BWD_TILING[1])
    steps = nt // tps
    scale = B_KEY_DIM ** -0.5
    nw = len(wgrads)

    def one_sequence(q_ref, k_ref, cum_ref, v_ref, do_ref, st_all_ref, dq_ref, dk_ref, dv_ref, dla_ref, dst_ref):
        qs, kk, (e_qm, e_km, e_qe, e_kd) = _gla_factors(q_ref, k_ref, cum_ref)
        qm, km, qe, kd = qs * e_qm, kk * e_km, qs * e_qe, kk * e_kd
        mask = _intra_mask(tile_rows)
        dqm_slabs, dkm_slabs, dv_intra = [], [], []
        for p in range(B_HEADS // 2):
            qm_st = _pair_stack(qm, p)
            km_p = km[:, p * LANES:(p + 1) * LANES].astype(BF16)
            a = _dot_nt(qm_st, km_p)
            da_blocks, dqm_h = [], []
            for hh in range(2):
                h = 2 * p + hh
                vs = slice(h * B_VAL_DIM, (h + 1) * B_VAL_DIM)
                att = jnp.where(mask, a[hh * tile_rows:(hh + 1) * tile_rows], 0.0).astype(BF16)
                dv_intra.append(_dot_tn(att, do_ref[:, vs]))
                da = jnp.where(mask, _dot_nt(do_ref[:, vs], v_ref[:, vs]), 0.0).astype(BF16)
                da_blocks.append(da)
                dqm_h.append(_dot(da, km_p))
            lo = _lane_iota((tile_rows, LANES)) < B_KEY_DIM
            dqm_slabs.append(jnp.where(lo, dqm_h[0], dqm_h[1]))
            dkm_slabs.append(_dot_tn(jnp.concatenate(da_blocks, axis=0), qm_st))
        dqm = jnp.concatenate(dqm_slabs, axis=1)
        dkm = jnp.concatenate(dkm_slabs, axis=1)

        dqe_c, dkd_c, dv_inter, tail_c = ([None] * cpt for _ in range(4))
        for c in reversed(range(cpt)):
            rows = slice(c * B_CHUNK, (c + 1) * B_CHUNK)
            dst = dst_ref[...]
            dst_b = dst.astype(BF16)
            dv_inter[c] = _dot_nt(_stack_masked(kd[rows]), dst_b)
            dkd_c[c] = _select_heads(_dot(_rows_by_head(v_ref[rows, :]), dst_b))
            do_c = do_ref[rows, :]
            dqe_c[c] = _select_heads(_dot(_rows_by_head(do_c), st_all_ref[c].astype(BF16)))
            contrib = _select_state(_dot_tn(do_c, qe[rows].astype(BF16)))
            decay = jnp.exp(cum_ref[pl.ds(c * B_CHUNK + B_CHUNK - 1, 1), :])
            tail = (jnp.sum(kk[rows] * dkd_c[c] * e_kd[rows], axis=0, keepdims=True)
                    + decay * jnp.sum(st_all_ref[c] * dst, axis=0, keepdims=True))
            tail_c[c] = jnp.broadcast_to(tail, (B_CHUNK, B_KEY_WIDTH))
            dst_ref[...] = dst * decay + contrib
        dqe = jnp.concatenate(dqe_c, axis=0)
        dkd = jnp.concatenate(dkd_c, axis=0)
        dqs = dqm * e_qm + dqe * e_qe
        dk = dkm * e_km + dkd * e_kd
        dq_ref[...] = (dqs * scale).astype(BF16)
        dk_ref[...] = dk.astype(BF16)
        for h in range(B_HEADS):
            dvi = jnp.concatenate([dv_inter[c][h * B_CHUNK:(h + 1) * B_CHUNK] for c in range(cpt)], axis=0)
            dv_ref[:, h * B_VAL_DIM:(h + 1) * B_VAL_DIM] = (dv_intra[h] + dvi).astype(BF16)
        dd = qs * dqs - kk * dk
        i, j = _row_iota((tile_rows, tile_rows)), _lane_iota((tile_rows, tile_rows))
        upper = ((i // B_CHUNK == j // B_CHUNK) & (j >= i)).astype(BF16)
        hi, mid, lo3 = _split3(dd)
        dla_ref[...] = _dot(upper, hi) + _dot(upper, mid) + _dot(upper, lo3) + jnp.concatenate(tail_c, axis=0)

    def body(q_ref, k_ref, cum_ref, v_ref, do_ref, st_all_ref, *rest):
        g_refs, (dq_ref, dk_ref, dv_ref, dla_ref) = rest[:nw], rest[nw:nw + 4]
        rv_refs, (dst_ref, send_sems, recv_sems) = rest[nw + 4:2 * nw + 4], rest[2 * nw + 4:]
        x, y, c = _my_place()

        def wcopy(a, r):
            dx, dy, dc = FLIPS[r]
            return pltpu.make_async_remote_copy(
                src_ref=g_refs[a].at[4 * (x ^ dx) + 2 * (y ^ dy) + (c ^ dc)], dst_ref=rv_refs[a].at[r],
                send_sem=send_sems.at[a, r], recv_sem=recv_sems.at[a, r],
                device_id=(x ^ dx, y ^ dy, c ^ dc), device_id_type=MESH)

        @pl.when(pl.program_id(0) == 0)
        def _():
            dst_ref[...] = jnp.zeros_like(dst_ref)
            for a in range(nw):
                for r in range(len(FLIPS)):
                    wcopy(a, r).start()

        for b in range(B):
            for tile in reversed(range(tps)):
                tok = pl.ds(tile * tile_rows, tile_rows)
                chunks = pl.ds(tile * cpt, cpt)
                one_sequence(*[r.at[b, tok] for r in (q_ref, k_ref, cum_ref, v_ref, do_ref)], st_all_ref.at[b, chunks],
                             *[r.at[b, tok] for r in (dq_ref, dk_ref, dv_ref, dla_ref)], dst_ref.at[b])

        @pl.when(pl.program_id(0) == steps - 1)
        def _():
            for a in range(nw):
                for r in range(len(FLIPS)):
                    wcopy(a, r).wait()

    def rows(w):
        return pl.BlockSpec((B, tps * tile_rows, w), lambda t: (0, steps - 1 - t, 0))

    seq = lambda a: a.reshape(B, S, a.shape[-1])
    res = pl.pallas_call(
        body, name="gla_bwd", grid=(steps,),
        in_specs=[rows(B_KEY_WIDTH), rows(B_KEY_WIDTH), rows(B_KEY_WIDTH), rows(B_WIDTH), rows(B_WIDTH),
                  pl.BlockSpec((B, tps * cpt, B_VAL_DIM, B_KEY_WIDTH), lambda t: (0, steps - 1 - t, 0, 0))]
                 + _any_specs(nw),
        out_specs=[rows(B_KEY_WIDTH), rows(B_KEY_WIDTH), rows(B_WIDTH), rows(B_KEY_WIDTH)] + _any_specs(nw),
        out_shape=[jax.ShapeDtypeStruct((B, S, B_KEY_WIDTH), BF16), jax.ShapeDtypeStruct((B, S, B_KEY_WIDTH), BF16),
                   jax.ShapeDtypeStruct((B, S, B_WIDTH), BF16), jax.ShapeDtypeStruct((B, S, B_KEY_WIDTH), F32)]
                  + [jax.ShapeDtypeStruct((len(FLIPS), *g.shape[1:]), g.dtype) for g in wgrads],
        scratch_shapes=[pltpu.VMEM((B, B_VAL_DIM, B_KEY_WIDTH), F32),
                        pltpu.SemaphoreType.DMA((nw, len(FLIPS))), pltpu.SemaphoreType.DMA((nw, len(FLIPS)))],
        compiler_params=_params("arbitrary"),
    )(seq(q), seq(k), seq(cum), seq(vb), seq(do), st_all.reshape(B, S // B_CHUNK, B_VAL_DIM, B_KEY_WIDTH), *wgrads)
    return [a.reshape(T, a.shape[-1]) for a in res[:4]], res[4:]


def _merge(x2, tgt2, attn, za, o_gla, zb, ga, gb, w_oa_sh, w_ob_sh, w_o, g_gla, g_final):
    T = x2.shape[0]
    tm = math.gcd(T, 512)
    sub = math.gcd(tm, 256)
    last = T // tm - 1

    def body(x_ref, tgt_ref, attn_ref, za_ref, og_ref, zb_ref, ga_ref, gb_ref,
             woa_sh_ref, wob_sh_ref, wo_ref, gg_ref, gf_ref,
             dxres_ref, dattn_ref, dog_ref, dza_ref, dzb_ref, dga_ref, dgb_ref,
             dwo_out, dwoa_out, dwob_out, small_ref,
             awo_ref, awoa_ref, awob_ref, agf_ref, agg_ref, loss_ref, woa_ref, wob_ref,
             dwo_ref, dwoa_ref, dwob_ref, w_sems, dw_sems):
        w_copies = [pltpu.make_async_copy(sh.at[j], dst.at[:, j * SHARD_OUT:(j + 1) * SHARD_OUT], w_sems.at[a, j])
                    for a, (sh, dst) in enumerate(((woa_sh_ref, woa_ref), (wob_sh_ref, wob_ref))) for j in range(N_DEV)]
        dw_copies = [pltpu.make_async_copy(src, dst, dw_sems.at[a])
                     for a, (src, dst) in enumerate(((dwo_ref, dwo_out), (dwoa_ref, dwoa_out), (dwob_ref, dwob_out)))]

        @pl.when(pl.program_id(0) == 0)
        def _():
            for cp in w_copies:
                cp.start()
            for r in (awo_ref, awoa_ref, awob_ref, agf_ref, agg_ref, loss_ref):
                r[...] = jnp.zeros_like(r)
            for cp in w_copies:
                cp.wait()

        def one_tile(rows):
            za_v = za_ref[rows, :].astype(F32)
            sig_za = _sigmoid_tanh(za_v)
            silu_a = za_v * sig_za
            attn_v = attn_ref[rows, :].astype(F32)
            oa = (attn_v * silu_a).astype(BF16)
            ya = _dot(oa, woa_ref[...])
            og = og_ref[rows, :].astype(F32)
            zb_v = zb_ref[rows, :].astype(F32)
            sig_zb = _sigmoid_tanh(zb_v)
            silu_b = zb_v * sig_zb
            gg = gg_ref[...]
            on_parts, rinv_parts = [], []
            for h in range(B_HEADS):
                seg = og[:, h * B_VAL_DIM:(h + 1) * B_VAL_DIM]
                rinv = lax.rsqrt(jnp.mean(seg * seg, axis=-1, keepdims=True) + NORM_EPS)
                rinv_parts.append(rinv)
                on_parts.append(seg * rinv)
            on = jnp.concatenate(on_parts, axis=1)
            obn = on * gg
            ob = (obn * silu_b).astype(BF16)
            yb = _dot(ob, wob_ref[...])
            sig_a = _sigmoid_tanh(ga_ref[rows, :].astype(F32))
            sig_b = _sigmoid_tanh(gb_ref[rows, :].astype(F32))
            merged = (sig_a * ya + sig_b * yb).astype(BF16)
            out = x_ref[rows, :] + _dot(merged, wo_ref[...])
            rf = lax.rsqrt(jnp.mean(out * out, axis=-1, keepdims=True) + NORM_EPS)
            nrm = out * rf
            gf = gf_ref[...]
            err = nrm * gf - tgt_ref[rows, :]
            loss = jnp.sum(err * err) * (0.5 / D_MODEL)

            dy = err * (1.0 / D_MODEL)
            dgf = jnp.sum(dy * nrm, axis=0, keepdims=True)
            dn = dy * gf
            dout = rf * (dn - nrm * jnp.mean(dn * nrm, axis=-1, keepdims=True))
            dxres_ref[rows, :] = dout
            dout_b = dout.astype(BF16)
            dmerged = _dot_nt(dout_b, wo_ref[...])
            dya = dmerged * sig_a
            dyb = dmerged * sig_b
            dga_ref[rows, :] = (dmerged * ya * sig_a * (1.0 - sig_a)).astype(BF16)
            dgb_ref[rows, :] = (dmerged * yb * sig_b * (1.0 - sig_b)).astype(BF16)
            dya_b, dyb_b = dya.astype(BF16), dyb.astype(BF16)
            doa = _dot_nt(dya_b, woa_ref[...])
            dattn_ref[rows, :] = (doa * silu_a).astype(BF16)
            dza_ref[rows, :] = (doa * attn_v * (sig_za * (1.0 + za_v * (1.0 - sig_za)))).astype(BF16)
            dob = _dot_nt(dyb_b, wob_ref[...])
            dzb_ref[rows, :] = (dob * obn * (sig_zb * (1.0 + zb_v * (1.0 - sig_zb)))).astype(BF16)
            dobn = dob * silu_b
            dgg = jnp.sum(dobn * on, axis=0, keepdims=True)
            don = dobn * gg
            for h in range(B_HEADS):
                sl = slice(h * B_VAL_DIM, (h + 1) * B_VAL_DIM)
                don_h, on_h = don[:, sl], on[:, sl]
                dog_ref[rows, sl] = (rinv_parts[h] * (don_h - on_h * jnp.mean(don_h * on_h, axis=-1, keepdims=True))
                                     ).astype(BF16)
            return (merged, dout_b, oa, dya_b, ob, dyb_b), (loss, dgf, dgg)

        tiles = [one_tile(pl.ds(j * sub, sub)) for j in range(tm // sub)]
        merged, dout_b, oa, dya_b, ob, dyb_b = (jnp.concatenate(parts, axis=0) for parts in zip(*[t[0] for t in tiles]))
        awo_ref[...] += _dot_tn(merged, dout_b)
        awoa_ref[...] += _dot_tn(oa, dya_b)
        awob_ref[...] += _dot_tn(ob, dyb_b)
        for _, (loss, dgf, dgg) in tiles:
            loss_ref[...] += loss
            agf_ref[...] += dgf
            agg_ref[...] += dgg

        @pl.when(pl.program_id(0) == last)
        def _():
            for j in range(N_DEV):
                dwo_ref[j] = awo_ref[j * SHARD_OUT:(j + 1) * SHARD_OUT, :].astype(BF16)
                dwoa_ref[j] = awoa_ref[:, j * SHARD_OUT:(j + 1) * SHARD_OUT].astype(BF16)
                dwob_ref[j] = awob_ref[:, j * SHARD_OUT:(j + 1) * SHARD_OUT].astype(BF16)
            small_ref[...] = jnp.zeros_like(small_ref)
            _put_rows(small_ref, SMALL_G_FINAL, agf_ref[...])
            _put_rows(small_ref, SMALL_G_GLA, agg_ref[...])
            small_ref[SMALL_LOSS:SMALL_LOSS + 1, :] = loss_ref[...]
            for cp in dw_copies:
                cp.start()
            for cp in dw_copies:
                cp.wait()

    def rows(w):
        return pl.BlockSpec((tm, w), lambda i: (i, 0))

    def whole(shape):
        nd = len(shape)
        return pl.BlockSpec(shape, lambda i: (0,) * nd)

    outs = [((T, D_MODEL), F32, rows(D_MODEL)), ((T, A_WIDTH), BF16, rows(A_WIDTH)), ((T, B_WIDTH), BF16, rows(B_WIDTH)),
            ((T, A_WIDTH), BF16, rows(A_WIDTH)), ((T, B_WIDTH), BF16, rows(B_WIDTH)),
            ((T, D_MODEL), BF16, rows(D_MODEL)), ((T, D_MODEL), BF16, rows(D_MODEL)),
            ((N_DEV, SHARD_OUT, D_MODEL), BF16, pl.BlockSpec(memory_space=pl.ANY)),
            ((N_DEV, A_WIDTH, SHARD_OUT), BF16, pl.BlockSpec(memory_space=pl.ANY)),
            ((N_DEV, B_WIDTH, SHARD_OUT), BF16, pl.BlockSpec(memory_space=pl.ANY)),
            ((SMALL_SINKS, LANES), F32, whole((SMALL_SINKS, LANES)))]
    return pl.pallas_call(
        body, name="merge", grid=(T // tm,),
        in_specs=[rows(D_MODEL), rows(D_MODEL), rows(A_WIDTH), rows(A_WIDTH), rows(B_WIDTH), rows(B_WIDTH),
                  rows(D_MODEL), rows(D_MODEL),
                  pl.BlockSpec(memory_space=pl.ANY), pl.BlockSpec(memory_space=pl.ANY),
                  _const_spec((D_MODEL, D_MODEL)), _const_spec((1, B_WIDTH)), _const_spec((1, D_MODEL))],
        out_specs=[o[2] for o in outs],
        out_shape=[jax.ShapeDtypeStruct(o[0], o[1]) for o in outs],
        scratch_shapes=[pltpu.VMEM((D_MODEL, D_MODEL), F32), pltpu.VMEM((A_WIDTH, D_MODEL), F32),
                        pltpu.VMEM((B_WIDTH, D_MODEL), F32), pltpu.VMEM((1, D_MODEL), F32), pltpu.VMEM((1, B_WIDTH), F32),
                        pltpu.VMEM((1, LANES), F32), pltpu.VMEM((A_WIDTH, D_MODEL), BF16),
                        pltpu.VMEM((B_WIDTH, D_MODEL), BF16),
                        pltpu.VMEM((N_DEV, SHARD_OUT, D_MODEL), BF16), pltpu.VMEM((N_DEV, A_WIDTH, SHARD_OUT), BF16),
                        pltpu.VMEM((N_DEV, B_WIDTH, SHARD_OUT), BF16),
                        pltpu.SemaphoreType.DMA((2, N_DEV)), pltpu.SemaphoreType.DMA((3,))],
        compiler_params=pltpu.CompilerParams(dimension_semantics=("arbitrary",), vmem_limit_bytes=V7X_VMEM_LIMIT_MAX),
    )(x2, tgt2, attn, za, o_gla, zb, ga, gb, w_oa_sh, w_ob_sh, w_o, g_gla, g_final)


def _in_proj_bwd(x2, dxres, cosf, sinf, g_in, wt_pad, wa_pad, parts):
    T = x2.shape[0]
    tm = math.gcd(T, 512)
    sub = math.gcd(tm, 256)
    last = T // tm - 1
    base = SMALL_G_IN

    def body(x_ref, dxres_ref, cos_ref, sin_ref, g_ref, wt_ref, wa_ref,
             dq_ref, dkv_ref, dza_ref, dqb_ref, dkb_ref, dvb_ref, dzb_ref, dla_ref, u_ref, alr_ref, dga_ref, dgb_ref,
             dx_ref, dsh_ref, small_ref, dproj_ref, agin_ref, aba_ref, awa_ref):
        @pl.when(pl.program_id(0) == 0)
        def _():
            for r in (agin_ref, aba_ref, awa_ref):
                r[...] = jnp.zeros_like(r)

        def one_tile(rows):
            cos, nsin = cos_ref[rows, :], -sin_ref[rows, :]
            for s in range(A_WIDTH // LANES):
                sl = slice(s * LANES, (s + 1) * LANES)
                dproj_ref[rows, sl] = _rope_slab(dq_ref[rows, sl].astype(F32), cos, nsin).astype(BF16)
            dproj_ref[rows, QKV_K:QKV_V] = _rope_slab(dkv_ref[rows, 0:LANES].astype(F32), cos, nsin).astype(BF16)
            dproj_ref[rows, QKV_V:QKV_W] = dkv_ref[rows, LANES:]

            def put(name, val):
                a, b = SEG[name]
                dproj_ref[rows, a:b] = val

            put("za", dza_ref[rows, :])
            put("qb", dqb_ref[rows, :])
            put("kb", dkb_ref[rows, :])
            put("vb", dvb_ref[rows, :])
            put("zb", dzb_ref[rows, :])
            put("ga", dga_ref[rows, :])
            put("gb", dgb_ref[rows, :])
            du = dla_ref[rows, :] * (1.0 / B_GATE_TEMP) * _sigmoid(-u_ref[rows, :])
            du_b = du.astype(BF16)
            put("alr", _dot_nt(du_b, wa_ref[...]).astype(BF16))

            for j in range(N_DEV):
                col = (j % 2) * SHARD_PAD
                for a, b in _shard_pad_cols(j):
                    dsh_ref[j // 2, rows, col:col + b - a] = dproj_ref[rows, a:b]
                    col += b - a
                dsh_ref[j // 2, rows, col:(j % 2 + 1) * SHARD_PAD] = jnp.zeros((sub, SHARD_PAD - SHARD_IN), BF16)

            dh = _dot(dproj_ref[rows, :], wt_ref[...])
            x = x_ref[rows, :]
            r = lax.rsqrt(jnp.mean(x * x, axis=-1, keepdims=True) + NORM_EPS)
            nrm = x * r
            dn = dh * g_ref[...]
            dx_ref[rows, :] = dxres_ref[rows, :] + r * (dn - nrm * jnp.mean(dn * nrm, axis=-1, keepdims=True))
            return jnp.sum(dh * nrm, axis=0, keepdims=True), jnp.sum(du, axis=0, keepdims=True), alr_ref[rows, :], du_b

        for j in range(tm // sub):
            dgin, dba, alr, du_b = one_tile(pl.ds(j * sub, sub))
            agin_ref[...] += dgin
            aba_ref[...] += dba
            awa_ref[...] += _dot_tn(alr, du_b)

        @pl.when(pl.program_id(0) == last)
        def _():
            small_ref[...] = jnp.zeros_like(small_ref)
            _put_rows(small_ref, SMALL_G_IN - base, agin_ref[...])
            _put_rows(small_ref, SMALL_B_ALPHA - base, aba_ref[...])
            for half in range(B_KEY_WIDTH // LANES):
                r0 = SMALL_W_ALPHA - base + half * B_GATE_RANK
                small_ref[r0:r0 + B_GATE_RANK, :] = awa_ref[0:B_GATE_RANK, half * LANES:(half + 1) * LANES]

    def rows(w):
        return pl.BlockSpec((tm, w), lambda i: (i, 0))

    names = ["dq", "dkv", "dza", "dqb", "dkb", "dvb", "dzb", "dla", "u", "alr", "dga", "dgb"]
    return pl.pallas_call(
        body, name="in_proj_bwd", grid=(T // tm,),
        in_specs=[rows(D_MODEL), rows(D_MODEL), rows(LANES), rows(LANES), _const_spec((1, D_MODEL)),
                  _const_spec((D_IN_PAD, D_MODEL)), _const_spec((RANK_PAD, B_KEY_WIDTH))]
                 + [rows(parts[n].shape[1]) for n in names],
        out_specs=[rows(D_MODEL), pl.BlockSpec((N_CHIPS, tm, 2 * SHARD_PAD), lambda i: (0, i, 0)),
                   pl.BlockSpec((SMALL_ROWS - base, LANES), lambda i: (0, 0))],
        out_shape=[jax.ShapeDtypeStruct((T, D_MODEL), F32), jax.ShapeDtypeStruct((N_CHIPS, T, 2 * SHARD_PAD), BF16),
                   jax.ShapeDtypeStruct((SMALL_ROWS - base, LANES), F32)],
        scratch_shapes=[pltpu.VMEM((tm, D_IN_PAD), BF16), pltpu.VMEM((1, D_MODEL), F32), pltpu.VMEM((1, B_KEY_WIDTH), F32),
                        pltpu.VMEM((RANK_PAD, B_KEY_WIDTH), F32)],
        compiler_params=pltpu.CompilerParams(dimension_semantics=("arbitrary",), vmem_limit_bytes=V7X_VMEM_LIMIT_MAX),
    )(x2, dxres, cosf, sinf, g_in, wt_pad, wa_pad, *[parts[n] for n in names])


FLIPS = [(dx, dy, dc) for dx in (0, 1) for dy in (0, 1) for dc in (0, 1)][1:]
GATHER_PIECES = 4
BF16_TILE_ROWS = 16
RS_PIECES = 2


def _my_place():
    return lax.axis_index("x"), lax.axis_index("y"), lax.axis_index("c")


def _any_specs(n):
    return [pl.BlockSpec(memory_space=pl.ANY)] * n


def _gather_first(shards, pos_col):
    n = len(shards)
    T = pos_col.shape[0]
    rows_per_pass = math.gcd(T, 512)
    pieces = [GATHER_PIECES if sh.shape[0] % (GATHER_PIECES * BF16_TILE_ROWS) == 0 else 1 for sh in shards]
    max_pieces = max(pieces)
    invf, sign = _rope_lane_constants()

    def body(*refs):
        ins, (pos_ref, invf_ref, sign_ref) = refs[:n], refs[n:n + 3]
        outs, (cos_ref, sin_ref) = refs[n + 3:2 * n + 3], refs[2 * n + 3:2 * n + 5]
        send_sems, recv_sems, local_sems = refs[2 * n + 5:]
        x, y, c = _my_place()
        me, sibling = (x, y, c), (x, y, 1 - c)
        onward, source, diagonal = (x ^ (1 - c), y ^ c), (x ^ c, y ^ (1 - c)), (1 - x, 1 - y)

        def block(a, px, py, pc):
            return outs[a].at[4 * px + 2 * py + pc]

        def copy(a, p, k, blk, to, own=False):
            rows = shards[a].shape[0] // pieces[a]
            piece = pl.ds(p * rows, rows)
            return pltpu.make_async_remote_copy(
                src_ref=(ins[a] if own else block(a, *blk)).at[piece], dst_ref=block(a, *blk).at[piece],
                send_sem=send_sems.at[a, p, k], recv_sem=recv_sems.at[a, p, k], device_id=to, device_id_type=MESH)

        every = [(a, p) for a in range(n) for p in range(pieces[a])]
        first = []
        for a, p in every:
            first.append(copy(a, p, 0, me, sibling, own=True))
            first += [copy(a, p, 1 + j, me, (*chip, c), own=True) for j, chip in enumerate((onward, source))]
        for cp in first:
            cp.start()
        mine = [pltpu.make_async_copy(ins[a], block(a, *me), local_sems.at[a]) for a in range(n)]
        for cp in mine:
            cp.start()

        def tables(i, carry):
            rows = pl.ds(pl.multiple_of(i * rows_per_pass, rows_per_pass), rows_per_pass)
            ang = pos_ref[rows, :].astype(F32) * invf_ref[...]
            cos_ref[rows, :] = jnp.cos(ang)
            sin_ref[rows, :] = jnp.sin(ang) * sign_ref[...]
            return carry

        lax.fori_loop(0, T // rows_per_pass, tables, 0)

        passed = []
        for j, chip in ((1, source), (0, onward), (2, diagonal)):
            for a, p in every:
                copy(a, p, 1 + j, (*chip, c), me).wait_recv()
                todo = [copy(a, p, 4 + j, (*chip, c), sibling)]
                if j == 1:
                    todo.append(copy(a, p, 1 + 2, (*chip, c), (*onward, c)))
                for cp in todo:
                    cp.start()
                passed += todo
        for a, p in every:
            copy(a, p, 0, sibling, me).wait_recv()
            for j, chip in enumerate((source, onward, diagonal)):
                copy(a, p, 4 + j, (*chip, 1 - c), me).wait_recv()
        for cp in first + passed:
            cp.wait_send()
        for cp in mine:
            cp.wait()

    vmem = pl.BlockSpec(memory_space=pltpu.VMEM)
    res = pl.pallas_call(
        body, name="gather_weights",
        in_specs=_any_specs(n) + [vmem] * 3, out_specs=_any_specs(n) + [vmem] * 2,
        out_shape=[jax.ShapeDtypeStruct((N_DEV, *s.shape), s.dtype) for s in shards]
                  + [jax.ShapeDtypeStruct((T, LANES), F32)] * 2,
        scratch_shapes=[pltpu.SemaphoreType.DMA((n, max_pieces, 7)), pltpu.SemaphoreType.DMA((n, max_pieces, 7)),
                        pltpu.SemaphoreType.DMA((n,))],
        compiler_params=pltpu.CompilerParams(vmem_limit_bytes=V7X_VMEM_LIMIT),
    )(*shards, pos_col, invf, sign)
    return res[:n], res[n], res[n + 1]


def _w_in_grad_rs(h, dsh, chip_order, small):
    T = h.shape[0]
    tk = math.gcd(T, 2048)
    nk = T // tk
    chip_flips = [(1, 1), (1, 0), (0, 1)]
    n_steps = len(chip_flips) + 1
    SIB = len(chip_flips)
    halves = (slice(0, SHARD_PAD), slice(SHARD_PAD, 2 * SHARD_PAD))
    piece_rows = SHARD_PAD // RS_PIECES

    def body(order_ref, h_ref, d_ref, s_ref, own_ref, recv_ref, sall_ref,
             acc_ref, pre_ref, to_sib_ref, to_chip_ref,
             sib_send, sib_recv, chip_send, chip_recv, ssend_sems, srecv_sems, local_sem):
        i, kk = pl.program_id(0), pl.program_id(1)
        x, y, c = _my_place()
        my_dev = 4 * x + 2 * y + c

        def small_copy(r, slot):
            dx, dy, dc = FLIPS[r]
            return pltpu.make_async_remote_copy(
                src_ref=s_ref, dst_ref=sall_ref.at[slot], send_sem=ssend_sems.at[r], recv_sem=srecv_sems.at[r],
                device_id=(x ^ dx, y ^ dy, c ^ dc), device_id_type=MESH)

        keep_small = pltpu.make_async_copy(s_ref, sall_ref.at[my_dev], local_sem)

        def sib_copy(t, q):
            dst = recv_ref.at[SIB] if t == SIB else pre_ref.at[t]
            return pltpu.make_async_remote_copy(
                src_ref=to_sib_ref.at[t, piece(q)], dst_ref=dst.at[piece(q)],
                send_sem=sib_send.at[t, q], recv_sem=sib_recv.at[t, q], device_id=(x, y, 1 - c), device_id_type=MESH)

        def chip_copy(t, q):
            dx, dy = chip_flips[t]
            return pltpu.make_async_remote_copy(
                src_ref=to_chip_ref.at[t, piece(q)], dst_ref=recv_ref.at[t, piece(q)],
                send_sem=chip_send.at[t, q], recv_sem=chip_recv.at[t, q],
                device_id=(x ^ dx, y ^ dy, c), device_id_type=MESH)

        def piece(q):
            return pl.ds(q * piece_rows, piece_rows)

        def accumulate(rows):
            acc_ref[rows, :] += _dot_tn(d_ref[:, rows], h_ref[...])

        @pl.when((i == 0) & (kk == 0))
        def _():
            keep_small.start()
            for r in range(len(FLIPS)):
                small_copy(r, my_dev).start()

        @pl.when(kk == 0)
        def _():
            acc_ref[...] = jnp.zeros_like(acc_ref)

        @pl.when(kk < nk - 1)
        def _():
            accumulate(slice(0, 2 * SHARD_PAD))

        for core in range(2):
            @pl.when((kk == nk - 1) & (c == core))
            def _(mine=halves[core], theirs=halves[1 - core]):
                accumulate(theirs)
                for t in range(n_steps):
                    @pl.when(i == t)
                    def _(t=t):
                        to_sib_ref[t] = acc_ref[theirs, :].astype(BF16)
                        for q in range(RS_PIECES):
                            sib_copy(t, q).start()
                for q in range(RS_PIECES):
                    rows = slice(mine.start + q * piece_rows, mine.start + (q + 1) * piece_rows)
                    accumulate(rows)
                    for t in range(len(chip_flips)):
                        @pl.when(i == t)
                        def _(t=t, q=q, rows=rows):
                            sib_copy(t, q).wait_recv()
                            to_chip_ref[t, piece(q), :] = (acc_ref[rows, :]
                                                           + pre_ref[t, piece(q), :].astype(F32)).astype(BF16)
                            chip_copy(t, q).start()

                @pl.when(i == n_steps - 1)
                def _():
                    own_ref[...] = acc_ref[mine, :]

        @pl.when((i == n_steps - 1) & (kk == nk - 1))
        def _():
            for q in range(RS_PIECES):
                for t in range(len(chip_flips)):
                    sib_copy(t, q).wait_send()
                    chip_copy(t, q).wait_send()
                    chip_copy(t, q).wait_recv()
                sib_copy(SIB, q).wait_send()
                sib_copy(SIB, q).wait_recv()
            for r, (dx, dy, dc) in enumerate(FLIPS):
                small_copy(r, 4 * (x ^ dx) + 2 * (y ^ dy) + (c ^ dc)).wait_recv()
                small_copy(r, my_dev).wait_send()
            keep_small.wait()

    shard = (SHARD_PAD, D_MODEL)
    return pl.pallas_call(
        body, name="w_in_grad_rs",
        grid_spec=pltpu.PrefetchScalarGridSpec(
            num_scalar_prefetch=1, grid=(n_steps, nk),
            in_specs=[pl.BlockSpec((tk, D_MODEL), lambda i, kk, order: (kk, 0)),
                      pl.BlockSpec((None, tk, 2 * SHARD_PAD), lambda i, kk, order: (order[i], kk, 0)),
                      pl.BlockSpec(memory_space=pl.ANY)],
            out_specs=[pl.BlockSpec(shard, lambda i, kk, order: (0, 0)),
                       pl.BlockSpec(memory_space=pl.ANY), pl.BlockSpec(memory_space=pl.ANY)],
            scratch_shapes=[pltpu.VMEM((2 * SHARD_PAD, D_MODEL), F32),
                            pltpu.VMEM((SIB, *shard), BF16), pltpu.VMEM((SIB + 1, *shard), BF16),
                            pltpu.VMEM((SIB, *shard), BF16),
                            pltpu.SemaphoreType.DMA((SIB + 1, RS_PIECES)), pltpu.SemaphoreType.DMA((SIB + 1, RS_PIECES)),
                            pltpu.SemaphoreType.DMA((SIB, RS_PIECES)), pltpu.SemaphoreType.DMA((SIB, RS_PIECES)),
                            pltpu.SemaphoreType.DMA((7,)), pltpu.SemaphoreType.DMA((7,)), pltpu.SemaphoreType.DMA]),
        out_shape=[jax.ShapeDtypeStruct(shard, F32),
                   jax.ShapeDtypeStruct((SIB + 1, *shard), BF16),
                   jax.ShapeDtypeStruct((N_DEV, *small.shape), F32)],
        compiler_params=_params("arbitrary", "arbitrary"),
    )(chip_order, h, dsh, small)


def _adam_math(w, g, m, v):
    m_new = ADAM_B1 * m + (1.0 - ADAM_B1) * g
    v_new = ADAM_B2 * v + (1.0 - ADAM_B2) * (g * g)
    m_hat = m_new / (1.0 - ADAM_B1 ** ADAM_STEP)
    v_hat = v_new / (1.0 - ADAM_B2 ** ADAM_STEP)
    delta = -ADAM_LR * (m_hat / (jnp.sqrt(v_hat) + ADAM_EPS) + ADAM_WD * w)
    return delta, m_new, v_new


def _adam_big(jobs):
    steps = 8
    n = len(jobs)
    idx = jnp.stack([job[1] for job in jobs]).astype(jnp.int32)
    blocks = []
    for own, _, recv, w, m, v in jobs:
        (rw, cw), rp = w.shape, own.shape[1]
        by_cols = rp != rw
        blk_w = (rw, cw // steps) if by_cols else (rw // steps, cw)
        blk_g = (rp, cw // steps) if by_cols else (rw // steps, cw)
        blocks.append((blk_w, blk_g, by_cols))

    def body(idx_ref, *refs):
        ins, outs = refs[:5 * n], refs[5 * n:]
        for j, (blk_w, _, _) in enumerate(blocks):
            o_ref, r_ref, w_ref, m_ref, v_ref = ins[5 * j:5 * j + 5]
            g_ref, d_ref, mo_ref, vo_ref = outs[4 * j:4 * j + 4]
            g = o_ref[...].astype(F32)
            for r in range(r_ref.shape[0]):
                g = g + r_ref[r].astype(F32)
            g = g[0:blk_w[0], :]
            g_ref[...] = g
            d_ref[...], mo_ref[...], vo_ref[...] = _adam_math(w_ref[...], g, m_ref[...], v_ref[...])

    in_specs, out_specs, out_shape, args = [], [], [], []
    for j, ((own, _, recv, w, m, v), (blk_w, blk_g, by_cols)) in enumerate(zip(jobs, blocks)):
        at = (lambda i: (0, i)) if by_cols else (lambda i: (i, 0))
        spec = pl.BlockSpec(blk_w, lambda i, idx_ref, at=at: at(i))
        in_specs += [pl.BlockSpec((None, *blk_g), lambda i, idx_ref, at=at, j=j: (idx_ref[j], *at(i))),
                     pl.BlockSpec((recv.shape[0], *blk_g), lambda i, idx_ref, at=at: (0, *at(i))), spec, spec, spec]
        out_specs += [spec] * 4
        out_shape += [jax.ShapeDtypeStruct(w.shape, F32)] * 4
        args += [own, recv, w, m, v]
    res = pl.pallas_call(
        body, name="adam_big",
        grid_spec=pltpu.PrefetchScalarGridSpec(num_scalar_prefetch=1, grid=(steps,), in_specs=in_specs, out_specs=out_specs),
        out_shape=out_shape,
        compiler_params=_params("parallel"),
    )(idx, *args)
    return [res[4 * j:4 * j + 4] for j in range(n)]


def _adam_small(small_all, params):
    flat = [a for triple in params for a in triple]
    n_par = len(params)

    def body(s_ref, *refs):
        ins, outs, loss_ref = refs[:3 * n_par], refs[3 * n_par:-1], refs[-1]
        g_slab = s_ref[0]
        for dev in range(1, N_DEV):
            g_slab = g_slab + s_ref[dev]
        loss_ref[...] = g_slab[SMALL_LOSS:SMALL_LOSS + 1, :]
        dev = 4 * lax.axis_index("x") + 2 * lax.axis_index("y") + lax.axis_index("c")
        alpha_full = jnp.concatenate([g_slab[SMALL_W_ALPHA + half * B_GATE_RANK:SMALL_W_ALPHA + (half + 1) * B_GATE_RANK]
                                      for half in range(B_KEY_WIDTH // LANES)], axis=1)
        alpha_mine = pltpu.roll(alpha_full, (B_KEY_WIDTH - dev * SHARD_ALPHA) % B_KEY_WIDTH, 1)[:, 0:SHARD_ALPHA]
        grads = [_take_rows(g_slab, SMALL_G_IN, D_MODEL // LANES), _take_rows(g_slab, SMALL_G_FINAL, D_MODEL // LANES),
                 _take_rows(g_slab, SMALL_G_GLA, B_WIDTH // LANES), _take_rows(g_slab, SMALL_B_ALPHA, B_KEY_WIDTH // LANES),
                 g_slab[SMALL_SINKS:SMALL_SINKS + 1, 0:A_HEADS], alpha_mine]
        for i, g in enumerate(grads):
            w_ref, m_ref, v_ref = ins[3 * i:3 * i + 3]
            delta, m_new, v_new = _adam_math(w_ref[...], g, m_ref[...], v_ref[...])
            outs[4 * i][...] = g
            outs[4 * i + 1][...] = delta
            outs[4 * i + 2][...] = m_new
            outs[4 * i + 3][...] = v_new

    res = pl.pallas_call(
        body, name="adam_small",
        out_shape=[jax.ShapeDtypeStruct(t[0].shape, F32) for t in params for _ in range(4)]
                  + [jax.ShapeDtypeStruct((1, LANES), F32)],
    )(small_all, *flat)
    return [res[4 * i:4 * i + 4] for i in range(n_par)], res[-1]


def _local_step(x, cosf, sinf, loss_target, g_in, wt_sh, wa_pad, b_alpha, sinks, g_gla, out_shards, g_final, chip_order):
    B, S, _ = x.shape
    T = B * S
    x2 = x.reshape(T, D_MODEL)
    tgt2 = loss_target.reshape(T, D_MODEL)
    f, (g_woa, g_wob, g_wo) = _in_proj(x2, cosf, sinf, g_in, wt_sh, wa_pad, b_alpha, out_shards)
    w_o = g_wo.reshape(D_MODEL, D_MODEL)
    sink_row = jnp.repeat(sinks, WINDOW).reshape(1, ATT_ROWS)
    sink_col = sink_row.reshape(ATT_ROWS, 1)
    attn, lse = _attn_fwd(f["qkv"], sink_row, B, S)
    o_gla, st_all = _gla_fwd(f["q"], f["k"], f["cum"], f["vb"], B, S)
    (dxres, dattn, dog, dza, dzb, dga, dgb, dw_o, dw_oa, dw_ob, small_a) = _merge(
        x2, tgt2, attn, f["za"], o_gla, f["zb"], f["ga"], f["gb"], g_woa, g_wob, w_o, g_gla, g_final)
    dq, dkv, dsink = _attn_bwd(f["qkv"], dattn, attn, lse, sink_col, B, S)
    (dqb, dkb, dvb, dla), (rv_o, rv_oa, rv_ob) = _gla_bwd(f["q"], f["k"], f["cum"], f["vb"], dog, st_all, B, S,
                                                        [dw_o, dw_oa, dw_ob])
    parts = dict(dq=dq, dkv=dkv, dza=dza, dqb=dqb, dkb=dkb, dvb=dvb, dzb=dzb, dla=dla, u=f["u"], alr=f["alr"],
                 dga=dga, dgb=dgb)
    dx, dsh, small_c = _in_proj_bwd(x2, dxres, cosf, sinf, g_in, f["wt_pad"], wa_pad, parts)
    small = jnp.concatenate([small_a, dsink, small_c], axis=0)
    own_in, rv_in, small_all = _w_in_grad_rs(f["h"], dsh, chip_order, small)
    return dict(grad_x=dx.reshape(B, S, D_MODEL), own_in=own_in, rv_in=rv_in,
                own_o=dw_o, rv_o=rv_o, own_oa=dw_oa, rv_oa=rv_oa, own_ob=dw_ob, rv_ob=rv_ob, small_all=small_all)


def kernel(x, positions, g_in, w_in, w_alpha_up, b_alpha, attn_sinks, g_gla_norm, w_out_a, w_out_b, w_o, g_final, loss_target, m_g_in, m_w_in, m_w_alpha_up, m_b_alpha, m_attn_sinks, m_g_gla_norm, m_w_out_a, m_w_out_b, m_w_o, m_g_final, v_g_in, v_w_in, v_w_alpha_up, v_b_alpha, v_attn_sinks, v_g_gla_norm, v_w_out_a, v_w_out_b, v_w_o, v_g_final):
    xi, yi, ci = _my_place()
    chip = 2 * xi + yi
    chip_order = jnp.stack([chip ^ 3, chip ^ 2, chip ^ 1, chip]).astype(jnp.int32)

    (g_win, g_wa), cosf, sinf = _gather_first(
        [jnp.pad(w_in[0].T.astype(BF16), ((0, SHARD_PAD - SHARD_IN), (0, 0))), w_alpha_up[0].astype(BF16)],
        positions.reshape(-1, 1))
    wt_sh = g_win.reshape(N_DEV * SHARD_PAD, D_MODEL)
    wa_pad = jnp.pad(jnp.concatenate([g_wa[j] for j in range(N_DEV)], axis=1), ((0, RANK_PAD - B_GATE_RANK), (0, 0)))

    r = _local_step(x, cosf, sinf, loss_target, g_in, wt_sh, wa_pad, b_alpha, attn_sinks[0], g_gla_norm,
                    [w_out_a[0].astype(BF16), w_out_b[0].astype(BF16), w_o[0].astype(BF16)],
                    g_final.reshape(1, D_MODEL), chip_order)

    dev = 4 * xi + 2 * yi + ci
    big = _adam_big([(r["own_in"][None], jnp.int32(0), r["rv_in"], w_in[0].T, m_w_in[0].T, v_w_in[0].T),
                     (r["own_oa"], dev, r["rv_oa"], w_out_a[0], m_w_out_a[0], v_w_out_a[0]),
                     (r["own_ob"], dev, r["rv_ob"], w_out_b[0], m_w_out_b[0], v_w_out_b[0]),
                     (r["own_o"], dev, r["rv_o"], w_o[0], m_w_o[0], v_w_o[0])])
    big[0] = [a.T for a in big[0]]
    row = lambda a: a.reshape(1, D_MODEL)
    (s_in, s_final, s_gla, s_ba, s_sinks, s_wa), loss_row = _adam_small(r["small_all"], [
        (g_in, m_g_in, v_g_in), (row(g_final), row(m_g_final), row(v_g_final)),
        (g_gla_norm, m_g_gla_norm, v_g_gla_norm), (b_alpha, m_b_alpha, v_b_alpha),
        (attn_sinks, m_attn_sinks, v_attn_sinks), (w_alpha_up[0], m_w_alpha_up[0], v_w_alpha_up[0])])

    def group(i):
        return (s_in[i], big[0][i][None], s_wa[i][None], s_ba[i], s_sinks[i], s_gla[i], big[1][i][None], big[2][i][None],
                big[3][i][None], s_final[i].reshape(D_MODEL))

    return (loss_row[0, 0], r["grad_x"], *group(0), *group(1), *group(2), *group(3))
```
